```python
import jax, jax.numpy as jnp
from jax import lax
import numpy as np

D_MODEL = 1024
BATCH = 8
SEQ = 4096
DEPTH = 1

CHUNK = 64
LEFT_CHUNKS = 8
BAND = (LEFT_CHUNKS + 1) * CHUNK
N_HEADS = 8
HEAD_DIM = 64
D_ATTN = N_HEADS * HEAD_DIM
D_CONV = 512
CONV_K = 31
MAX_REL = 128
D_FF = 2816
FFN_CONV_K = 3
EPS = 1e-6
NEG_INF = -1e30
IN_WIDTHS = (D_ATTN, D_ATTN, D_ATTN, D_CONV, D_CONV, D_MODEL, D_MODEL)
D_IN = D_ATTN * 3 + D_CONV * 2 + D_MODEL * 2

kernel_name = "hybrid_chunked_attn_conformer_conv_convffn_block"


def rms_norm(x, g):
    xf = x.astype(jnp.float32)
    y = xf * lax.rsqrt(jnp.mean(xf * xf, axis=-1, keepdims=True) + EPS)
    return (y * g.astype(jnp.float32)).astype(x.dtype)


def layer_norm(x, g, b):
    xf = x.astype(jnp.float32)
    mu = jnp.mean(xf, axis=-1, keepdims=True)
    var = jnp.mean(jnp.square(xf - mu), axis=-1, keepdims=True)
    y = (xf - mu) * lax.rsqrt(var + EPS)
    return (y * g.astype(jnp.float32) + b.astype(jnp.float32)).astype(x.dtype)


def causal_dwconv(x, w, b):
    k = w.shape[0]
    y = lax.conv_general_dilated(
        x, w[:, None, :].astype(x.dtype), window_strides=(1,), padding=[(k - 1, 0)],
        dimension_numbers=('NWC', 'WIO', 'NWC'), feature_group_count=x.shape[-1])
    return y + b


def chunk_band(t):
    b, s, h, dh = t.shape
    nc = s // CHUNK
    tc = t.reshape(b, nc, CHUNK, h, dh)
    tp = jnp.pad(tc, ((0, 0), (LEFT_CHUNKS, 0), (0, 0), (0, 0), (0, 0)))
    band = jnp.stack([tp[:, j:j + nc] for j in range(LEFT_CHUNKS + 1)], axis=2)
    return band.reshape(b, nc, BAND, h, dh)


def chunked_rel_attention(q, k, v, rel_bias):
    b, s, _ = q.shape
    nc = s // CHUNK
    qc = q.reshape(b, nc, CHUNK, N_HEADS, HEAD_DIM)
    kb = chunk_band(k.reshape(b, s, N_HEADS, HEAD_DIM))
    vb = chunk_band(v.reshape(b, s, N_HEADS, HEAD_DIM))
    scores = jnp.einsum('bcqhd,bckhd->bhcqk', qc, kb).astype(jnp.float32) * (HEAD_DIM ** -0.5)
    qi = jnp.arange(CHUNK)
    kj = jnp.arange(BAND)
    rel = LEFT_CHUNKS * CHUNK + qi[:, None] - kj[None, :]
    idx = jnp.clip(rel, -MAX_REL, MAX_REL) + MAX_REL
    bias = rel_bias[:, idx].astype(jnp.float32)
    scores = scores + bias[None, :, None, :, :]
    key_chunk = jnp.arange(nc)[:, None] - LEFT_CHUNKS + (kj // CHUNK)[None, :]
    valid = key_chunk >= 0
    scores = jnp.where(valid[None, None, :, None, :], scores, NEG_INF)
    probs = jax.nn.softmax(scores, axis=-1).astype(v.dtype)
    out = jnp.einsum('bhcqk,bckhd->bcqhd', probs, vb)
    return out.reshape(b, s, D_ATTN)


def token_mixer(h, w_in, b_in, rel_bias, w_attn_o, w_dw, b_dw, g_ln, b_ln,
                w_conv_o, b_conv_o, w_mix_o):
    z = h @ w_in + b_in
    splits = list(np.cumsum(IN_WIDTHS)[:-1])
    q, k, v, glu_a, glu_b, gate_a, gate_b = jnp.split(z, splits, axis=-1)
    a = chunked_rel_attention(q, k, v, rel_bias) @ w_attn_o
    u = glu_a * jax.nn.sigmoid(glu_b)
    u = causal_dwconv(u, w_dw, b_dw)
    u = jax.nn.silu(layer_norm(u, g_ln, b_ln))
    cb = u @ w_conv_o + b_conv_o
    y = jax.nn.sigmoid(gate_a) * a + jax.nn.sigmoid(gate_b) * cb
    return y @ w_mix_o


def conv_ffn(h, w_up, w_dw, b_dw, w_down):
    u = causal_dwconv(h @ w_up, w_dw, b_dw)
    val, gt = jnp.split(u, 2, axis=-1)
    return (jax.nn.gelu(gt) * val) @ w_down


def _fwd_setup_inputs(seed: int = 0) -> dict:
    key = jax.random.key(seed)
    ks = jax.random.split(key, 26)
    L = DEPTH

    def nrm(k, shape, scale):
        return jax.random.normal(k, shape, jnp.float32) * scale

    def gain(k, shape):
        return 1.0 + 0.1 * jax.random.normal(k, shape, jnp.float32)

    return {
        "x": nrm(ks[0], (BATCH, SEQ, D_MODEL), 1.0),
        "c": nrm(ks[1], (BATCH, D_MODEL), 1.0),
        "w_ada": nrm(ks[2], (L, D_MODEL, 6 * D_MODEL), 0.5 * D_MODEL ** -0.5),
        "b_ada": nrm(ks[3], (L, 6 * D_MODEL), 0.01),
        "g_pre_mix": gain(ks[4], (L, D_MODEL)),
        "g_post_mix": gain(ks[5], (L, D_MODEL)),
        "w_in": nrm(ks[6], (L, D_MODEL, D_IN), D_MODEL ** -0.5),
        "b_in": nrm(ks[7], (L, D_IN), 0.01),
        "rel_bias": nrm(ks[8], (L, N_HEADS, 2 * MAX_REL + 1), 0.5),
        "w_attn_o": nrm(ks[9], (L, D_ATTN, D_MODEL), D_ATTN ** -0.5),
        "w_dw_conv": nrm(ks[10], (L, CONV_K, D_CONV), CONV_K ** -0.5),
        "b_dw_conv": nrm(ks[11], (L, D_CONV), 0.01),
        "g_conv_ln": gain(ks[12], (L, D_CONV)),
        "b_conv_ln": nrm(ks[13], (L, D_CONV), 0.01),
        "w_conv_o": nrm(ks[14], (L, D_CONV, D_MODEL), D_CONV ** -0.5),
        "b_conv_o": nrm(ks[15], (L, D_MODEL), 0.01),
        "w_mix_o": nrm(ks[16], (L, D_MODEL, D_MODEL), D_MODEL ** -0.5),
        "g_pre_ffn": gain(ks[17], (L, D_MODEL)),
        "g_post_ffn": gain(ks[18], (L, D_MODEL)),
        "w_up": nrm(ks[19], (L, D_MODEL, 2 * D_FF), D_MODEL ** -0.5),
        "w_dw_ffn": nrm(ks[20], (L, FFN_CONV_K, 2 * D_FF), FFN_CONV_K ** -0.5),
        "b_dw_ffn": nrm(ks[21], (L, 2 * D_FF), 0.01),
        "w_down": nrm(ks[22], (L, D_FF, D_MODEL), D_FF ** -0.5),
    }


def _fwd_reference(x, c, w_ada, b_ada, g_pre_mix, g_post_mix, w_in, b_in, rel_bias,
              w_attn_o, w_dw_conv, b_dw_conv, g_conv_ln, b_conv_ln, w_conv_o,
              b_conv_o, w_mix_o, g_pre_ffn, g_post_ffn, w_up, w_dw_ffn, b_dw_ffn,
              w_down):
    c_act = jax.nn.silu(c)
    for l in range(DEPTH):
        mod = c_act @ w_ada[l] + b_ada[l]
        sh_m, sc_m, gt_m, sh_f, sc_f, gt_f = [m[:, None, :] for m in jnp.split(mod, 6, axis=-1)]
        h = rms_norm(x, g_pre_mix[l]) * (1.0 + sc_m) + sh_m
        y = token_mixer(h, w_in[l], b_in[l], rel_bias[l], w_attn_o[l], w_dw_conv[l],
                        b_dw_conv[l], g_conv_ln[l], b_conv_ln[l], w_conv_o[l],
                        b_conv_o[l], w_mix_o[l])
        x = x + gt_m * rms_norm(y, g_post_mix[l])
        h = rms_norm(x, g_pre_ffn[l]) * (1.0 + sc_f) + sh_f
        y = conv_ffn(h, w_up[l], w_dw_ffn[l], b_dw_ffn[l], w_down[l])
        x = x + gt_f * rms_norm(y, g_post_ffn[l])
    return x


import jax as _jax
import jax.numpy as _jnp

TWIN_FORMAT = 'train_step'
FWD_PARAMS = ['x', 'c', 'w_ada', 'b_ada', 'g_pre_mix', 'g_post_mix', 'w_in', 'b_in', 'rel_bias', 'w_attn_o', 'w_dw_conv', 'b_dw_conv', 'g_conv_ln', 'b_conv_ln', 'w_conv_o', 'b_conv_o', 'w_mix_o', 'g_pre_ffn', 'g_post_ffn', 'w_up', 'w_dw_ffn', 'b_dw_ffn', 'w_down']
TWIN_WEIGHTS = ['w_ada', 'b_ada', 'g_pre_mix', 'g_post_mix', 'w_in', 'b_in', 'rel_bias', 'w_attn_o', 'w_dw_conv', 'b_dw_conv', 'g_conv_ln', 'b_conv_ln', 'w_conv_o', 'b_conv_o', 'w_mix_o', 'g_pre_ffn', 'g_post_ffn', 'w_up', 'w_dw_ffn', 'b_dw_ffn', 'w_down']
TWIN_DIFF_INPUT = 'x'
TWIN_INPUTS = ['x', 'c', 'w_ada', 'b_ada', 'g_pre_mix', 'g_post_mix', 'w_in', 'b_in', 'rel_bias', 'w_attn_o', 'w_dw_conv', 'b_dw_conv', 'g_conv_ln', 'b_conv_ln', 'w_conv_o', 'b_conv_o', 'w_mix_o', 'g_pre_ffn', 'g_post_ffn', 'w_up', 'w_dw_ffn', 'b_dw_ffn', 'w_down', 'loss_target', 'm_w_ada', 'm_b_ada', 'm_g_pre_mix', 'm_g_post_mix', 'm_w_in', 'm_b_in', 'm_rel_bias', 'm_w_attn_o', 'm_w_dw_conv', 'm_b_dw_conv', 'm_g_conv_ln', 'm_b_conv_ln', 'm_w_conv_o', 'm_b_conv_o', 'm_w_mix_o', 'm_g_pre_ffn', 'm_g_post_ffn', 'm_w_up', 'm_w_dw_ffn', 'm_b_dw_ffn', 'm_w_down', 'v_w_ada', 'v_b_ada', 'v_g_pre_mix', 'v_g_post_mix', 'v_w_in', 'v_b_in', 'v_rel_bias', 'v_w_attn_o', 'v_w_dw_conv', 'v_b_dw_conv', 'v_g_conv_ln', 'v_b_conv_ln', 'v_w_conv_o', 'v_b_conv_o', 'v_w_mix_o', 'v_g_pre_ffn', 'v_g_post_ffn', 'v_w_up', 'v_w_dw_ffn', 'v_b_dw_ffn', 'v_w_down']
TWIN_OUTPUTS = ['loss', 'grad_x', 'grad_w_ada', 'grad_b_ada', 'grad_g_pre_mix', 'grad_g_post_mix', 'grad_w_in', 'grad_b_in', 'grad_rel_bias', 'grad_w_attn_o', 'grad_w_dw_conv', 'grad_b_dw_conv', 'grad_g_conv_ln', 'grad_b_conv_ln', 'grad_w_conv_o', 'grad_b_conv_o', 'grad_w_mix_o', 'grad_g_pre_ffn', 'grad_g_post_ffn', 'grad_w_up', 'grad_w_dw_ffn', 'grad_b_dw_ffn', 'grad_w_down', 'delta_w_ada', 'delta_b_ada', 'delta_g_pre_mix', 'delta_g_post_mix', 'delta_w_in', 'delta_b_in', 'delta_rel_bias', 'delta_w_attn_o', 'delta_w_dw_conv', 'delta_b_dw_conv', 'delta_g_conv_ln', 'delta_b_conv_ln', 'delta_w_conv_o', 'delta_b_conv_o', 'delta_w_mix_o', 'delta_g_pre_ffn', 'delta_g_post_ffn', 'delta_w_up', 'delta_w_dw_ffn', 'delta_b_dw_ffn', 'delta_w_down', 'new_m_w_ada', 'new_m_b_ada', 'new_m_g_pre_mix', 'new_m_g_post_mix', 'new_m_w_in', 'new_m_b_in', 'new_m_rel_bias', 'new_m_w_attn_o', 'new_m_w_dw_conv', 'new_m_b_dw_conv', 'new_m_g_conv_ln', 'new_m_b_conv_ln', 'new_m_w_conv_o', 'new_m_b_conv_o', 'new_m_w_mix_o', 'new_m_g_pre_ffn', 'new_m_g_post_ffn', 'new_m_w_up', 'new_m_w_dw_ffn', 'new_m_b_dw_ffn', 'new_m_w_down', 'new_v_w_ada', 'new_v_b_ada', 'new_v_g_pre_mix', 'new_v_g_post_mix', 'new_v_w_in', 'new_v_b_in', 'new_v_rel_bias', 'new_v_w_attn_o', 'new_v_w_dw_conv', 'new_v_b_dw_conv', 'new_v_g_conv_ln', 'new_v_b_conv_ln', 'new_v_w_conv_o', 'new_v_b_conv_o', 'new_v_w_mix_o', 'new_v_g_pre_ffn', 'new_v_g_post_ffn', 'new_v_w_up', 'new_v_w_dw_ffn', 'new_v_b_dw_ffn', 'new_v_w_down']
TWIN_LEAF_KINDS = {'loss': 'loss', 'grad_x': 'grad_x', 'grad_w_ada': 'grad_w', 'grad_b_ada': 'grad_w', 'grad_g_pre_mix': 'grad_w', 'grad_g_post_mix': 'grad_w', 'grad_w_in': 'grad_w', 'grad_b_in': 'grad_w', 'grad_rel_bias': 'grad_w', 'grad_w_attn_o': 'grad_w', 'grad_w_dw_conv': 'grad_w', 'grad_b_dw_conv': 'grad_w', 'grad_g_conv_ln': 'grad_w', 'grad_b_conv_ln': 'grad_w', 'grad_w_conv_o': 'grad_w', 'grad_b_conv_o': 'grad_w', 'grad_w_mix_o': 'grad_w', 'grad_g_pre_ffn': 'grad_w', 'grad_g_post_ffn': 'grad_w', 'grad_w_up': 'grad_w', 'grad_w_dw_ffn': 'grad_w', 'grad_b_dw_ffn': 'grad_w', 'grad_w_down': 'grad_w', 'delta_w_ada': 'delta_w', 'delta_b_ada': 'delta_w', 'delta_g_pre_mix': 'delta_w', 'delta_g_post_mix': 'delta_w', 'delta_w_in': 'delta_w', 'delta_b_in': 'delta_w', 'delta_rel_bias': 'delta_w', 'delta_w_attn_o': 'delta_w', 'delta_w_dw_conv': 'delta_w', 'delta_b_dw_conv': 'delta_w', 'delta_g_conv_ln': 'delta_w', 'delta_b_conv_ln': 'delta_w', 'delta_w_conv_o': 'delta_w', 'delta_b_conv_o': 'delta_w', 'delta_w_mix_o': 'delta_w', 'delta_g_pre_ffn': 'delta_w', 'delta_g_post_ffn': 'delta_w', 'delta_w_up': 'delta_w', 'delta_w_dw_ffn': 'delta_w', 'delta_b_dw_ffn': 'delta_w', 'delta_w_down': 'delta_w', 'new_m_w_ada': 'new_m', 'new_m_b_ada': 'new_m', 'new_m_g_pre_mix': 'new_m', 'new_m_g_post_mix': 'new_m', 'new_m_w_in': 'new_m', 'new_m_b_in': 'new_m', 'new_m_rel_bias': 'new_m', 'new_m_w_attn_o': 'new_m', 'new_m_w_dw_conv': 'new_m', 'new_m_b_dw_conv': 'new_m', 'new_m_g_conv_ln': 'new_m', 'new_m_b_conv_ln': 'new_m', 'new_m_w_conv_o': 'new_m', 'new_m_b_conv_o': 'new_m', 'new_m_w_mix_o': 'new_m', 'new_m_g_pre_ffn': 'new_m', 'new_m_g_post_ffn': 'new_m', 'new_m_w_up': 'new_m', 'new_m_w_dw_ffn': 'new_m', 'new_m_b_dw_ffn': 'new_m', 'new_m_w_down': 'new_m', 'new_v_w_ada': 'new_v', 'new_v_b_ada': 'new_v', 'new_v_g_pre_mix': 'new_v', 'new_v_g_post_mix': 'new_v', 'new_v_w_in': 'new_v', 'new_v_b_in': 'new_v', 'new_v_rel_bias': 'new_v', 'new_v_w_attn_o': 'new_v', 'new_v_w_dw_conv': 'new_v', 'new_v_b_dw_conv': 'new_v', 'new_v_g_conv_ln': 'new_v', 'new_v_b_conv_ln': 'new_v', 'new_v_w_conv_o': 'new_v', 'new_v_b_conv_o': 'new_v', 'new_v_w_mix_o': 'new_v', 'new_v_g_pre_ffn': 'new_v', 'new_v_g_post_ffn': 'new_v', 'new_v_w_up': 'new_v', 'new_v_w_dw_ffn': 'new_v', 'new_v_b_dw_ffn': 'new_v', 'new_v_w_down': 'new_v'}


def _forward(args):
    return _fwd_reference(*[args[k] for k in FWD_PARAMS])


def _output_shape():
    out = _jax.eval_shape(lambda: _forward(_fwd_setup_inputs(0)))
    return out.shape, out.dtype

N_MICROBATCH = 1
ADAM_LR = 0.001
ADAM_B1 = 0.9
ADAM_B2 = 0.999
ADAM_EPS = 1e-08
ADAM_WD = 0.01
ADAM_STEP = 10
PER_EXAMPLE_BATCH_AXIS = {'x': 0, 'c': 0, 'loss_target': 0}
SHARED_INPUTS = []
_WEIGHT_DTYPES = {'w_ada': _jnp.float32, 'b_ada': _jnp.float32, 'g_pre_mix': _jnp.float32, 'g_post_mix': _jnp.float32, 'w_in': _jnp.float32, 'b_in': _jnp.float32, 'rel_bias': _jnp.float32, 'w_attn_o': _jnp.float32, 'w_dw_conv': _jnp.float32, 'b_dw_conv': _jnp.float32, 'g_conv_ln': _jnp.float32, 'b_conv_ln': _jnp.float32, 'w_conv_o': _jnp.float32, 'b_conv_o': _jnp.float32, 'w_mix_o': _jnp.float32, 'g_pre_ffn': _jnp.float32, 'g_post_ffn': _jnp.float32, 'w_up': _jnp.float32, 'w_dw_ffn': _jnp.float32, 'b_dw_ffn': _jnp.float32, 'w_down': _jnp.float32}
MOMENT_SCALE = {'w_ada': 1.932099e+00, 'b_ada': 3.654610e+00, 'g_pre_mix': 1.087519e-01, 'g_post_mix': 3.889399e+00, 'w_in': 1.935222e-01, 'b_in': 7.821823e-01, 'rel_bias': 1.547802e-02, 'w_attn_o': 4.339620e-01, 'w_dw_conv': 2.668083e-01, 'b_dw_conv': 1.521294e+00, 'g_conv_ln': 6.863439e-01, 'b_conv_ln': 1.002119e+00, 'w_conv_o': 3.200874e-01, 'b_conv_o': 1.456504e+00, 'w_mix_o': 5.575702e-01, 'g_pre_ffn': 1.404417e-01, 'g_post_ffn': 3.811761e+00, 'w_up': 6.975953e-02, 'w_dw_ffn': 8.038441e-02, 'b_dw_ffn': 1.694338e-01, 'w_down': 1.410757e-01}


def _to_microbatches(a, axis):
    t = _jnp.moveaxis(a, axis, 0)
    t = t.reshape((N_MICROBATCH, t.shape[0] // N_MICROBATCH) + t.shape[1:])
    return _jnp.moveaxis(t, 1, axis + 1)


def setup_inputs(seed: int = 0) -> dict:
    inp = _fwd_setup_inputs(seed)
    key = _jax.random.fold_in(_jax.random.key(seed), 7919)
    shape, _ = _output_shape()
    out = dict(inp)
    out["loss_target"] = _jax.random.normal(_jax.random.fold_in(key, 0), shape, _jnp.float32)
    for i, name in enumerate(TWIN_WEIGHTS):
        w = inp[name].astype(_jnp.float32)
        if MOMENT_SCALE is None:
            s = _jnp.sqrt(_jnp.mean(_jnp.square(w)) + 1e-30)
        else:
            s = MOMENT_SCALE[name]
        km, kv = _jax.random.split(_jax.random.fold_in(key, i + 1))
        out[name] = w
        out["m_" + name] = s * _jax.random.normal(km, w.shape, _jnp.float32)
        out["v_" + name] = (s * s) * _jax.random.uniform(kv, w.shape, _jnp.float32, 0.5, 1.5)
    if N_MICROBATCH > 1:
        for name, axis in PER_EXAMPLE_BATCH_AXIS.items():
            out[name] = _to_microbatches(out[name], axis)
    return {'x': out['x'], 'c': out['c'], 'w_ada': out['w_ada'], 'b_ada': out['b_ada'], 'g_pre_mix': out['g_pre_mix'], 'g_post_mix': out['g_post_mix'], 'w_in': out['w_in'], 'b_in': out['b_in'], 'rel_bias': out['rel_bias'], 'w_attn_o': out['w_attn_o'], 'w_dw_conv': out['w_dw_conv'], 'b_dw_conv': out['b_dw_conv'], 'g_conv_ln': out['g_conv_ln'], 'b_conv_ln': out['b_conv_ln'], 'w_conv_o': out['w_conv_o'], 'b_conv_o': out['b_conv_o'], 'w_mix_o': out['w_mix_o'], 'g_pre_ffn': out['g_pre_ffn'], 'g_post_ffn': out['g_post_ffn'], 'w_up': out['w_up'], 'w_dw_ffn': out['w_dw_ffn'], 'b_dw_ffn': out['b_dw_ffn'], 'w_down': out['w_down'], 'loss_target': out['loss_target'], 'm_w_ada': out['m_w_ada'], 'm_b_ada': out['m_b_ada'], 'm_g_pre_mix': out['m_g_pre_mix'], 'm_g_post_mix': out['m_g_post_mix'], 'm_w_in': out['m_w_in'], 'm_b_in': out['m_b_in'], 'm_rel_bias': out['m_rel_bias'], 'm_w_attn_o': out['m_w_attn_o'], 'm_w_dw_conv': out['m_w_dw_conv'], 'm_b_dw_conv': out['m_b_dw_conv'], 'm_g_conv_ln': out['m_g_conv_ln'], 'm_b_conv_ln': out['m_b_conv_ln'], 'm_w_conv_o': out['m_w_conv_o'], 'm_b_conv_o': out['m_b_conv_o'], 'm_w_mix_o': out['m_w_mix_o'], 'm_g_pre_ffn': out['m_g_pre_ffn'], 'm_g_post_ffn': out['m_g_post_ffn'], 'm_w_up': out['m_w_up'], 'm_w_dw_ffn': out['m_w_dw_ffn'], 'm_b_dw_ffn': out['m_b_dw_ffn'], 'm_w_down': out['m_w_down'], 'v_w_ada': out['v_w_ada'], 'v_b_ada': out['v_b_ada'], 'v_g_pre_mix': out['v_g_pre_mix'], 'v_g_post_mix': out['v_g_post_mix'], 'v_w_in': out['v_w_in'], 'v_b_in': out['v_b_in'], 'v_rel_bias': out['v_rel_bias'], 'v_w_attn_o': out['v_w_attn_o'], 'v_w_dw_conv': out['v_w_dw_conv'], 'v_b_dw_conv': out['v_b_dw_conv'], 'v_g_conv_ln': out['v_g_conv_ln'], 'v_b_conv_ln': out['v_b_conv_ln'], 'v_w_conv_o': out['v_w_conv_o'], 'v_b_conv_o': out['v_b_conv_o'], 'v_w_mix_o': out['v_w_mix_o'], 'v_g_pre_ffn': out['v_g_pre_ffn'], 'v_g_post_ffn': out['v_g_post_ffn'], 'v_w_up': out['v_w_up'], 'v_w_dw_ffn': out['v_w_dw_ffn'], 'v_b_dw_ffn': out['v_b_dw_ffn'], 'v_w_down': out['v_w_down']}


def _loss(weights, diff, rest, loss_target):
    with _jax.named_scope("forward"):
        args = {**rest, TWIN_DIFF_INPUT: diff, **{k: w.astype(_WEIGHT_DTYPES[k]) for k, w in weights.items()}}
        y = _forward(args)
    with _jax.named_scope("loss_head"):
        err = _jnp.square(y.astype(_jnp.float32) - loss_target)
        return 0.5 * _jnp.sum(_jnp.mean(err, axis=-1)) if err.ndim else 0.5 * err


def _adamw(w, g, m, v):
    m = ADAM_B1 * m + (1.0 - ADAM_B1) * g
    v = ADAM_B2 * v + (1.0 - ADAM_B2) * _jnp.square(g)
    m_hat = m / (1.0 - ADAM_B1 ** ADAM_STEP)
    v_hat = v / (1.0 - ADAM_B2 ** ADAM_STEP)
    delta = -ADAM_LR * (m_hat / (_jnp.sqrt(v_hat) + ADAM_EPS) + ADAM_WD * w)
    return delta, m, v


def reference(x, c, w_ada, b_ada, g_pre_mix, g_post_mix, w_in, b_in, rel_bias, w_attn_o, w_dw_conv, b_dw_conv, g_conv_ln, b_conv_ln, w_conv_o, b_conv_o, w_mix_o, g_pre_ffn, g_post_ffn, w_up, w_dw_ffn, b_dw_ffn, w_down, loss_target, m_w_ada, m_b_ada, m_g_pre_mix, m_g_post_mix, m_w_in, m_b_in, m_rel_bias, m_w_attn_o, m_w_dw_conv, m_b_dw_conv, m_g_conv_ln, m_b_conv_ln, m_w_conv_o, m_b_conv_o, m_w_mix_o, m_g_pre_ffn, m_g_post_ffn, m_w_up, m_w_dw_ffn, m_b_dw_ffn, m_w_down, v_w_ada, v_b_ada, v_g_pre_mix, v_g_post_mix, v_w_in, v_b_in, v_rel_bias, v_w_attn_o, v_w_dw_conv, v_b_dw_conv, v_g_conv_ln, v_b_conv_ln, v_w_conv_o, v_b_conv_o, v_w_mix_o, v_g_pre_ffn, v_g_post_ffn, v_w_up, v_w_dw_ffn, v_b_dw_ffn, v_w_down):
    given = dict(x=x, c=c, w_ada=w_ada, b_ada=b_ada, g_pre_mix=g_pre_mix, g_post_mix=g_post_mix, w_in=w_in, b_in=b_in, rel_bias=rel_bias, w_attn_o=w_attn_o, w_dw_conv=w_dw_conv, b_dw_conv=b_dw_conv, g_conv_ln=g_conv_ln, b_conv_ln=b_conv_ln, w_conv_o=w_conv_o, b_conv_o=b_conv_o, w_mix_o=w_mix_o, g_pre_ffn=g_pre_ffn, g_post_ffn=g_post_ffn, w_up=w_up, w_dw_ffn=w_dw_ffn, b_dw_ffn=b_dw_ffn, w_down=w_down, loss_target=loss_target, m_w_ada=m_w_ada, m_b_ada=m_b_ada, m_g_pre_mix=m_g_pre_mix, m_g_post_mix=m_g_post_mix, m_w_in=m_w_in, m_b_in=m_b_in, m_rel_bias=m_rel_bias, m_w_attn_o=m_w_attn_o, m_w_dw_conv=m_w_dw_conv, m_b_dw_conv=m_b_dw_conv, m_g_conv_ln=m_g_conv_ln, m_b_conv_ln=m_b_conv_ln, m_w_conv_o=m_w_conv_o, m_b_conv_o=m_b_conv_o, m_w_mix_o=m_w_mix_o, m_g_pre_ffn=m_g_pre_ffn, m_g_post_ffn=m_g_post_ffn, m_w_up=m_w_up, m_w_dw_ffn=m_w_dw_ffn, m_b_dw_ffn=m_b_dw_ffn, m_w_down=m_w_down, v_w_ada=v_w_ada, v_b_ada=v_b_ada, v_g_pre_mix=v_g_pre_mix, v_g_post_mix=v_g_post_mix, v_w_in=v_w_in, v_b_in=v_b_in, v_rel_bias=v_rel_bias, v_w_attn_o=v_w_attn_o, v_w_dw_conv=v_w_dw_conv, v_b_dw_conv=v_b_dw_conv, v_g_conv_ln=v_g_conv_ln, v_b_conv_ln=v_b_conv_ln, v_w_conv_o=v_w_conv_o, v_b_conv_o=v_b_conv_o, v_w_mix_o=v_w_mix_o, v_g_pre_ffn=v_g_pre_ffn, v_g_post_ffn=v_g_post_ffn, v_w_up=v_w_up, v_w_dw_ffn=v_w_dw_ffn, v_b_dw_ffn=v_b_dw_ffn, v_w_down=v_w_down)
    weights = {n: given[n] for n in TWIN_WEIGHTS}
    shared = {n: given[n] for n in SHARED_INPUTS}
    per_example = {n: given[n] for n in ['x', 'c']}
    grad_fn = _jax.value_and_grad(_loss, argnums=(0, 1))

    def one_microbatch(ex, loss_target):
        ex = dict(ex)
        diff = ex.pop(TWIN_DIFF_INPUT)
        return grad_fn(weights, diff, {**shared, **ex}, loss_target)

    if N_MICROBATCH == 1:
        loss, (grad_w, grad_x) = one_microbatch(per_example, given["loss_target"])
    else:
        def body(carry, xs):
            loss_sum, grad_sum = carry
            l_k, (gw_k, gx_k) = one_microbatch(xs[0], xs[1])
            with _jax.named_scope("update"):
                return (loss_sum + l_k, _jax.tree.map(_jnp.add, grad_sum, gw_k)), gx_k

        init = (_jnp.zeros((), _jnp.float32), _jax.tree.map(_jnp.zeros_like, weights))
        (loss, grad_w), grad_x = _jax.lax.scan(body, init, (per_example, given["loss_target"]))
    with _jax.named_scope("update"):
        delta_w, new_m, new_v = {}, {}, {}
        for n in TWIN_WEIGHTS:
            delta_w[n], new_m[n], new_v[n] = _adamw(weights[n], grad_w[n], given["m_" + n], given["v_" + n])
    return (loss, grad_x, *[grad_w[n] for n in TWIN_WEIGHTS], *[delta_w[n] for n in TWIN_WEIGHTS],
            *[new_m[n] for n in TWIN_WEIGHTS], *[new_v[n] for n in TWIN_WEIGHTS])
```

```python
import functools
import math

import numpy as np
import jax
import jax.numpy as jnp
from jax import lax
from jax.experimental import pallas as pl
from jax.experimental.pallas import tpu as pltpu

F32, BF16 = jnp.float32, jnp.bfloat16
MESH = pl.DeviceIdType.MESH

D = 1024
D_IN = 4608
D_FF = 2816
CONV_K = 31
FFN_K = 3
N_HEADS = 8
CHUNK = 64
LEFT_CHUNKS = 8
MAX_REL = 128
EPS = 1e-6
NEG_INF = -1e30
Q_TILE = 256
WINDOW = Q_TILE + LEFT_CHUNKS * CHUNK
REL_PAD = 384
TOEP = 1024
ROW_TILE = 256
VMEM_LIMIT = 60 * 1024 * 1024

ADAM_LR, ADAM_B1, ADAM_B2, ADAM_EPS, ADAM_WD, ADAM_STEP = 0.001, 0.9, 0.999, 1e-08, 0.01, 10


def _params(sem=None):
    return pltpu.CompilerParams(dimension_semantics=sem, vmem_limit_bytes=VMEM_LIMIT)


def _sds(shape, dtype):
    return jax.ShapeDtypeStruct(tuple(shape), dtype)


def _matmul(a, b, *, form, out_dtype, tm, tn, tk, name, bias=None, add=None, out_sharded=False):
    b3 = b.ndim == 3
    if form == "nn":
        m, k = a.shape
        n = b.shape[0] * b.shape[2] if b3 else b.shape[1]
        dn = (((1,), (0,)), ((), ()))
        a_spec = pl.BlockSpec((tm, tk), lambda i, j, kk: (i, kk))
        b_spec = (pl.BlockSpec((None, tk, tn), lambda i, j, kk: (j, kk, 0)) if b3
                  else pl.BlockSpec((tk, tn), lambda i, j, kk: (kk, j)))
    elif form == "nt":
        m, k = a.shape
        n = b.shape[1] if b3 else b.shape[0]
        dn = (((1,), (1,)), ((), ()))
        a_spec = pl.BlockSpec((tm, tk), lambda i, j, kk: (i, kk))
        b_spec = (pl.BlockSpec((None, tn, tk), lambda i, j, kk: (kk, j, 0)) if b3
                  else pl.BlockSpec((tn, tk), lambda i, j, kk: (j, kk)))
    else:
        k, m = a.shape
        n = b.shape[1]
        dn = (((0,), (0,)), ((), ()))
        a_spec = pl.BlockSpec((tk, tm), lambda i, j, kk: (kk, i))
        b_spec = pl.BlockSpec((tk, tn), lambda i, j, kk: (kk, j))
    assert m % tm == 0 and n % tn == 0 and k % tk == 0, (name, m, n, k, tm, tn, tk)
    nk = k // tk
    in_specs, args = [a_spec, b_spec], [a, b]
    if bias is not None:
        in_specs.append(pl.BlockSpec((1, tn), lambda i, j, kk: (0, j)))
        args.append(bias)
    if add is not None:
        in_specs.append(pl.BlockSpec((tm, tn), lambda i, j, kk: (i, j)))
        args.append(add)
    if out_sharded:
        out_shape = _sds((n // tn, m, tn), out_dtype)
        out_spec = pl.BlockSpec((None, tm, tn), lambda i, j, kk: (j, i, 0))
    else:
        out_shape = _sds((m, n), out_dtype)
        out_spec = pl.BlockSpec((tm, tn), lambda i, j, kk: (i, j))

    def body(*refs):
        a_ref, b_ref = refs[0], refs[1]
        pos = 2
        bias_ref = add_ref = None
        if bias is not None:
            bias_ref, pos = refs[pos], pos + 1
        if add is not None:
            add_ref, pos = refs[pos], pos + 1
        o_ref = refs[pos]
        av, bv = a_ref[...], b_ref[...]
        if av.dtype != BF16:
            av = av.astype(BF16)
        if bv.dtype != BF16:
            bv = bv.astype(BF16)
        p = lax.dot_general(av, bv, dn, preferred_element_type=F32)

        def finish(acc):
            if bias_ref is not None:
                acc = acc + bias_ref[...]
            if add_ref is not None:
                acc = acc + add_ref[...]
            o_ref[...] = acc.astype(o_ref.dtype)

        if nk == 1:
            finish(p)
        else:
            acc_ref = refs[pos + 1]
            kk = pl.program_id(2)

            @pl.when(kk == 0)
            def _():
                acc_ref[...] = p

            @pl.when(kk > 0)
            def _():
                acc_ref[...] += p

            @pl.when(kk == nk - 1)
            def _():
                finish(acc_ref[...])

    return pl.pallas_call(
        body, grid=(m // tm, n // tn, nk), in_specs=in_specs, out_specs=out_spec, out_shape=out_shape,
        scratch_shapes=[pltpu.VMEM((tm, tn), F32)] if nk > 1 else [],
        compiler_params=_params(("parallel", "parallel", "arbitrary")), name=name,
    )(*args)


def _rowcall(fn, rows, consts, row_outs, acc_outs, *, name, tm=ROW_TILE, col_grid=1):
    n_rows = rows[0][0].shape[0]
    assert n_rows % tm == 0
    grid = (col_grid, n_rows // tm)
    in_specs = [pl.BlockSpec((tm, w), functools.partial(lambda c, i, cb: (i, cb + c), cb=cb)) for _, w, cb in rows]
    in_specs += [pl.BlockSpec(k.shape, functools.partial(lambda c, i, nd: (0,) * nd, nd=k.ndim)) for k in consts]
    out_specs = [pl.BlockSpec((tm, w), lambda c, i: (i, c)) for _, _, _, w in row_outs]
    out_specs += [pl.BlockSpec((r, w), lambda c, i: (0, c)) for r, _, w in acc_outs]
    out_shape = [_sds((nr, nc), dt) for nr, nc, dt, _ in row_outs] + [_sds((r, nc), F32) for r, nc, _ in acc_outs]
    n_in, n_ro = len(rows) + len(consts), len(row_outs)

    def body(*refs):
        res = fn(*[r[...] for r in refs[:n_in]])
        if not isinstance(res, (tuple, list)):
            res = (res,)
        outs = refs[n_in:]
        for o_ref, val in zip(outs[:n_ro], res[:n_ro]):
            o_ref[...] = val.astype(o_ref.dtype)
        if acc_outs:
            first = pl.program_id(1) == 0

            @pl.when(first)
            def _():
                for o_ref, val in zip(outs[n_ro:], res[n_ro:]):
                    o_ref[...] = val

            @pl.when(jnp.logical_not(first))
            def _():
                for o_ref, val in zip(outs[n_ro:], res[n_ro:]):
                    o_ref[...] += val

    out = pl.pallas_call(
        body, grid=grid, in_specs=in_specs, out_specs=out_specs, out_shape=out_shape,
        compiler_params=_params(("arbitrary", "arbitrary")), name=name,
    )(*[r[0] for r in rows], *consts)
    return out


def _colsum(v):
    return jnp.sum(v, axis=0, keepdims=True)


def _sigmoid(v):
    return 1.0 / (1.0 + jnp.exp(-v))


_GELU_C = math.sqrt(2.0 / math.pi)


def _gelu(v):
    return 0.5 * v * (1.0 + jnp.tanh(_GELU_C * (v + 0.044715 * (v * v * v))))


def _gelu_and_grad(v):
    th = jnp.tanh(_GELU_C * (v + 0.044715 * (v * v * v)))
    g = 0.5 * v * (1.0 + th)
    dg = 0.5 * (1.0 + th) + 0.5 * v * (1.0 - th * th) * (_GELU_C * (1.0 + 3.0 * 0.044715 * (v * v)))
    return g, dg


def _rms_stats(v):
    r = lax.rsqrt(jnp.mean(v * v, axis=-1, keepdims=True) + EPS)
    return v * r, r


def _rms_bwd(dn, vn, r):
    return r * (dn - vn * jnp.mean(dn * vn, axis=-1, keepdims=True))


def _pre_norm(x, g, sc, sh, name):
    def fn(xv, gv, scv, shv):
        xn, _ = _rms_stats(xv)
        return (xn * gv) * (1.0 + scv) + shv
    return _rowcall(fn, [(x, D, 0)], [g, sc, sh], [(x.shape[0], D, BF16, D)], [], name=name)[0]


def _pre_norm_bwd(dh, x, dx_other, g, sc, name):
    def fn(dhv, xv, dov, gv, scv):
        xn, r = _rms_stats(xv)
        yn = xn * gv
        dyn = dhv * (1.0 + scv)
        dx = _rms_bwd(dyn * gv, xn, r)
        return dov + dx, _colsum(dhv), _colsum(dhv * yn), _colsum(dyn * xn)
    t = x.shape[0]
    return _rowcall(fn, [(dh, D, 0), (x, D, 0), (dx_other, D, 0)], [g, sc], [(t, D, F32, D)],
                    [(1, D, D)] * 3, name=name)


def _post_res(x, ypre, g, gt, name):
    def fn(xv, yv, gv, gtv):
        yn, _ = _rms_stats(yv)
        return xv + gtv * (yn * gv)
    return _rowcall(fn, [(x, D, 0), (ypre, D, 0)], [g, gt], [(x.shape[0], D, F32, D)], [], name=name)[0]


def _post_res_bwd(dxo, ypre, g, gt, name):
    def fn(dv, yv, gv, gtv):
        yn, r = _rms_stats(yv)
        dyn = dv * gtv
        dy = _rms_bwd(dyn * gv, yn, r)
        return dy, _colsum(dyn * yn), _colsum(dv * (yn * gv))
    t = ypre.shape[0]
    return _rowcall(fn, [(dxo, D, 0), (ypre, D, 0)], [g, gt], [(t, D, BF16, D)], [(1, D, D)] * 2, name=name)


def _loss_grad(x2, target, name):
    def fn(xv, tv):
        e = xv - tv
        return e * (1.0 / D), _colsum(e * e) * (0.5 / D)
    return _rowcall(fn, [(x2, D, 0), (target, D, 0)], [], [(x2.shape[0], D, F32, D)], [(1, D, D)], name=name)


def _gate_merge(a, cb, z, name):
    def fn(av, cv, gav, gbv):
        return _sigmoid(gav) * av + _sigmoid(gbv) * cv
    t = a.shape[0]
    return _rowcall(fn, [(a, 512, 0), (cb, 512, 0), (z, 512, 5), (z, 512, 7)], [],
                    [(t, D, BF16, 512)], [], name=name, col_grid=2)[0]


def _gate_merge_bwd(dy, a, cb, z, name):
    def fn(dv, av, cv, gav, gbv):
        sa, sb = _sigmoid(gav), _sigmoid(gbv)
        dcb = dv * sb
        dga = dv * av * (sa * (1.0 - sa))
        dgb = dv * cv * (sb * (1.0 - sb))
        return dv * sa, dcb, dga, dgb, _colsum(dcb), _colsum(dga), _colsum(dgb)
    t = a.shape[0]
    return _rowcall(fn, [(dy, 512, 0), (a, 512, 0), (cb, 512, 0), (z, 512, 5), (z, 512, 7)], [],
                    [(t, D, BF16, 512)] * 4, [(1, D, 512)] * 3, name=name, col_grid=2)


CONV_HALO = 32


def _layer_norm_parts(u):
    mu = jnp.mean(u, axis=-1, keepdims=True)
    d = u - mu
    r = lax.rsqrt(jnp.mean(d * d, axis=-1, keepdims=True) + EPS)
    return d * r, r


def _conv_branch(z, w_dw, b_dw, g_ln, b_ln, name, tm=ROW_TILE):
    t = z.shape[0]
    per = tm // CONV_HALO

    def body(ga_ref, gb_ref, gah_ref, gbh_ref, w_ref, b_ref, g_ref, bl_ref, u1_ref, u3_ref, scr):
        i = pl.program_id(0)
        u0h = gah_ref[...] * _sigmoid(gbh_ref[...])
        scr[0:CONV_HALO, :] = jnp.where(i > 0, u0h, 0.0)
        scr[CONV_HALO:CONV_HALO + tm, :] = ga_ref[...] * _sigmoid(gb_ref[...])
        acc = jnp.zeros((tm, 512), F32) + b_ref[...]
        for j in range(CONV_K):
            acc = acc + w_ref[j:j + 1, :] * scr[pl.ds(CONV_HALO - (CONV_K - 1) + j, tm), :]
        u1_ref[...] = acc
        xh, _ = _layer_norm_parts(acc)
        u2 = xh * g_ref[...] + bl_ref[...]
        u3_ref[...] = (u2 * _sigmoid(u2)).astype(BF16)

    cur = lambda cb: pl.BlockSpec((tm, 512), lambda i: (i, cb))
    halo = lambda cb: pl.BlockSpec((CONV_HALO, 512), lambda i: (jnp.maximum(i * per - 1, 0), cb))
    whole = lambda a: pl.BlockSpec(a.shape, lambda i: (0, 0))
    return pl.pallas_call(
        body, grid=(t // tm,),
        in_specs=[cur(3), cur(4), halo(3), halo(4), whole(w_dw), whole(b_dw), whole(g_ln), whole(b_ln)],
        out_specs=[pl.BlockSpec((tm, 512), lambda i: (i, 0))] * 2,
        out_shape=[_sds((t, 512), F32), _sds((t, 512), BF16)],
        scratch_shapes=[pltpu.VMEM((CONV_HALO + tm, 512), F32)],
        compiler_params=_params(("arbitrary",)), name=name,
    )(z, z, z, z, w_dw, b_dw, g_ln, b_ln)


def _conv_branch_bwd(du3, u1, z, w_dw, g_ln, b_ln, name, tm=ROW_TILE):
    t = z.shape[0]
    per = tm // CONV_HALO
    last = t // tm - 1

    def du1_of(du3v, u1v, g, b):
        xh, r = _layer_norm_parts(u1v)
        u2 = xh * g + b
        s = _sigmoid(u2)
        du2 = du3v * (s * (1.0 + u2 * (1.0 - s)))
        dxh = du2 * g
        du1 = r * (dxh - jnp.mean(dxh, axis=-1, keepdims=True) - xh * jnp.mean(dxh * xh, axis=-1, keepdims=True))
        return du1, du2, xh

    def body(d_ref, u_ref, dn_ref, un_ref, ga_ref, gb_ref, gah_ref, gbh_ref, w_ref, g_ref, bl_ref,
             dglu_ref, dw_ref, dbdw_ref, dg_ref, dbl_ref, dbin_ref, scr, scd):
        i = pl.program_id(0)
        g, b = g_ref[...], bl_ref[...]
        du1, du2, xh = du1_of(d_ref[...], u_ref[...], g, b)
        du1n, _, _ = du1_of(dn_ref[...], un_ref[...], g, b)
        scd[0:tm, :] = du1
        scd[tm:tm + CONV_HALO, :] = jnp.where(i < last, du1n, 0.0)
        sgb = _sigmoid(gb_ref[...])
        ga = ga_ref[...]
        scr[0:CONV_HALO, :] = jnp.where(i > 0, gah_ref[...] * _sigmoid(gbh_ref[...]), 0.0)
        scr[CONV_HALO:CONV_HALO + tm, :] = ga * sgb
        first = i == 0

        @pl.when(first)
        def _():
            dw_ref[...] = jnp.zeros_like(dw_ref)

        du0 = jnp.zeros((tm, 512), F32)
        for j in range(CONV_K):
            du0 = du0 + w_ref[j:j + 1, :] * scd[pl.ds(CONV_K - 1 - j, tm), :]
            dw_ref[j:j + 1, :] += _colsum(du1 * scr[pl.ds(CONV_HALO - (CONV_K - 1) + j, tm), :])
        dga = du0 * sgb
        dgb = du0 * ga * (sgb * (1.0 - sgb))
        dglu_ref[:, 0:512] = dga.astype(BF16)
        dglu_ref[:, 512:1024] = dgb.astype(BF16)
        parts = (_colsum(du1), _colsum(du2 * xh), _colsum(du2))

        @pl.when(first)
        def _():
            dbdw_ref[...], dg_ref[...], dbl_ref[...] = parts
            dbin_ref[:, 0:512] = _colsum(dga)
            dbin_ref[:, 512:1024] = _colsum(dgb)

        @pl.when(jnp.logical_not(first))
        def _():
            dbdw_ref[...] += parts[0]
            dg_ref[...] += parts[1]
            dbl_ref[...] += parts[2]
            dbin_ref[:, 0:512] += _colsum(dga)
            dbin_ref[:, 512:1024] += _colsum(dgb)

    cur = lambda cb: pl.BlockSpec((tm, 512), lambda i: (i, cb))
    prev = lambda cb: pl.BlockSpec((CONV_HALO, 512), lambda i: (jnp.maximum(i * per - 1, 0), cb))
    nxt = pl.BlockSpec((CONV_HALO, 512), lambda i: (jnp.minimum((i + 1) * per, t // CONV_HALO - 1), 0))
    whole = lambda a: pl.BlockSpec(a.shape, lambda i: (0, 0))
    acc = lambda r, w: pl.BlockSpec((r, w), lambda i: (0, 0))
    return pl.pallas_call(
        body, grid=(t // tm,),
        in_specs=[cur(0), cur(0), nxt, nxt, cur(3), cur(4), prev(3), prev(4), whole(w_dw), whole(g_ln), whole(b_ln)],
        out_specs=[pl.BlockSpec((tm, 1024), lambda i: (i, 0)), acc(CONV_K, 512), acc(1, 512), acc(1, 512),
                   acc(1, 512), acc(1, 1024)],
        out_shape=[_sds((t, 1024), BF16), _sds((CONV_K, 512), F32), _sds((1, 512), F32), _sds((1, 512), F32),
                   _sds((1, 512), F32), _sds((1, 1024), F32)],
        scratch_shapes=[pltpu.VMEM((CONV_HALO + tm, 512), F32), pltpu.VMEM((tm + CONV_HALO, 512), F32)],
        compiler_params=_params(("arbitrary",)), name=name,
    )(du3, u1, du3, u1, z, z, z, z, w_dw, g_ln, b_ln)


FF_BLOCK = D_FF // 2
FF_HALO = 8
LANE_CHUNK = 128


def _ffn_act(up, w3, b3, name, tm=ROW_TILE):
    t = up.shape[0]
    per = tm // FF_HALO
    wide = 2 * FF_BLOCK

    def body(u_ref, uh_ref, w_ref, b_ref, o_ref, scr):
        i = pl.program_id(1)
        scr[0:FF_HALO, :] = jnp.where(i > 0, uh_ref[...], 0.0)
        scr[FF_HALO:FF_HALO + tm, :] = u_ref[...]
        for cc in range(FF_BLOCK // LANE_CHUNK):
            vs = slice(cc * LANE_CHUNK, (cc + 1) * LANE_CHUNK)
            gs = slice(FF_BLOCK + cc * LANE_CHUNK, FF_BLOCK + (cc + 1) * LANE_CHUNK)

            def conv(cs):
                acc = b_ref[:, cs] + w_ref[0:1, cs] * scr[pl.ds(FF_HALO - 2, tm), cs]
                acc = acc + w_ref[1:2, cs] * scr[pl.ds(FF_HALO - 1, tm), cs]
                return acc + w_ref[2:3, cs] * scr[pl.ds(FF_HALO, tm), cs]

            o_ref[:, vs] = (_gelu(conv(gs)) * conv(vs)).astype(BF16)

    return pl.pallas_call(
        body, grid=(2, t // tm),
        in_specs=[pl.BlockSpec((tm, wide), lambda c, i: (i, c)),
                  pl.BlockSpec((FF_HALO, wide), lambda c, i: (jnp.maximum(i * per - 1, 0), c)),
                  pl.BlockSpec((FFN_K, wide), lambda c, i: (0, c)),
                  pl.BlockSpec((1, wide), lambda c, i: (0, c))],
        out_specs=pl.BlockSpec((tm, FF_BLOCK), lambda c, i: (i, c)),
        out_shape=_sds((t, D_FF), BF16),
        scratch_shapes=[pltpu.VMEM((FF_HALO + tm, wide), F32)],
        compiler_params=_params(("arbitrary", "arbitrary")), name=name,
    )(up, up, w3, b3)


def _ffn_act_bwd(dact, up, w3, b3, name, tm=ROW_TILE):
    t = up.shape[0]
    per = tm // FF_HALO
    wide = 2 * FF_BLOCK
    last = t // tm - 1
    ext = tm + FF_HALO

    def body(u_ref, up_ref, un_ref, d_ref, dn_ref, w_ref, b_ref, o_ref, dw_ref, db_ref, scr, scd):
        i = pl.program_id(1)
        first = i == 0
        scr[0:FF_HALO, :] = jnp.where(i > 0, up_ref[...], 0.0)
        scr[FF_HALO:FF_HALO + tm, :] = u_ref[...]
        scr[FF_HALO + tm:FF_HALO + ext, :] = un_ref[...]
        dn = jnp.where(i < last, dn_ref[...], 0.0)

        @pl.when(first)
        def _():
            dw_ref[...] = jnp.zeros_like(dw_ref)
            db_ref[...] = jnp.zeros_like(db_ref)

        for cc in range(FF_BLOCK // LANE_CHUNK):
            vs = slice(cc * LANE_CHUNK, (cc + 1) * LANE_CHUNK)
            gs = slice(FF_BLOCK + cc * LANE_CHUNK, FF_BLOCK + (cc + 1) * LANE_CHUNK)

            def conv(cs):
                acc = b_ref[:, cs] + w_ref[0:1, cs] * scr[pl.ds(FF_HALO - 2, ext), cs]
                acc = acc + w_ref[1:2, cs] * scr[pl.ds(FF_HALO - 1, ext), cs]
                return acc + w_ref[2:3, cs] * scr[pl.ds(FF_HALO, ext), cs]

            val = conv(vs)
            gel, dgel = _gelu_and_grad(conv(gs))
            da = jnp.concatenate([d_ref[:, vs], dn[:, vs]], axis=0)
            scd[:, vs] = da * gel
            scd[:, gs] = da * val * dgel
            for cs in (vs, gs):
                o_ref[:, cs] = (w_ref[0:1, cs] * scd[pl.ds(2, tm), cs] + w_ref[1:2, cs] * scd[pl.ds(1, tm), cs]
                                + w_ref[2:3, cs] * scd[pl.ds(0, tm), cs]).astype(BF16)
                dcur = scd[pl.ds(0, tm), cs]
                for j in range(FFN_K):
                    dw_ref[j:j + 1, cs] += _colsum(dcur * scr[pl.ds(FF_HALO - 2 + j, tm), cs])
                db_ref[:, cs] += _colsum(dcur)

    nblk = t // FF_HALO
    return pl.pallas_call(
        body, grid=(2, t // tm),
        in_specs=[pl.BlockSpec((tm, wide), lambda c, i: (i, c)),
                  pl.BlockSpec((FF_HALO, wide), lambda c, i: (jnp.maximum(i * per - 1, 0), c)),
                  pl.BlockSpec((FF_HALO, wide), lambda c, i: (jnp.minimum((i + 1) * per, nblk - 1), c)),
                  pl.BlockSpec((tm, FF_BLOCK), lambda c, i: (i, c)),
                  pl.BlockSpec((FF_HALO, FF_BLOCK), lambda c, i: (jnp.minimum((i + 1) * per, nblk - 1), c)),
                  pl.BlockSpec((FFN_K, wide), lambda c, i: (0, c)),
                  pl.BlockSpec((1, wide), lambda c, i: (0, c))],
        out_specs=[pl.BlockSpec((tm, wide), lambda c, i: (i, c)),
                   pl.BlockSpec((FFN_K, wide), lambda c, i: (0, c)),
                   pl.BlockSpec((1, wide), lambda c, i: (0, c))],
        out_shape=[_sds((t, 2 * D_FF), BF16), _sds((FFN_K, 2 * D_FF), F32), _sds((1, 2 * D_FF), F32)],
        scratch_shapes=[pltpu.VMEM((FF_HALO + ext, wide), F32), pltpu.VMEM((ext, wide), F32)],
        compiler_params=_params(("arbitrary", "arbitrary")), name=name,
    )(up, up, up, dact, dact, w3, b3)


def _toeplitz_map():
    f = np.zeros((TOEP, REL_PAD), np.float32)
    for m in range(TOEP - 1):
        rel = (WINDOW - 1) - m
        f[m, int(np.clip(rel, -MAX_REL, MAX_REL)) + MAX_REL] = 1.0
    return f


def _split3(v):
    hi = v.astype(BF16)
    r1 = v - hi.astype(F32)
    mid = r1.astype(BF16)
    lo = (r1 - mid.astype(F32)).astype(BF16)
    return hi, mid, lo


def _exact_select(v, sel):
    out = None
    for part in _split3(v):
        p = jnp.dot(part, sel, preferred_element_type=F32)
        out = p if out is None else out + p
    return out


def _select_call(v, sel, name):
    def body(v_ref, s_ref, o_ref):
        o_ref[...] = _exact_select(v_ref[...], s_ref[...])
    return pl.pallas_call(body, out_shape=_sds((v.shape[0], sel.shape[1]), F32), name=name)(v, sel)


def _band_bias(gen_row):
    b0 = jnp.broadcast_to(gen_row, (Q_TILE, TOEP))
    bias = pltpu.roll(b0, TOEP - 255, 1, stride=1, stride_axis=0)[:, :WINDOW]
    qq = lax.broadcasted_iota(jnp.int32, (Q_TILE, WINDOW), 0) // CHUNK
    kc = lax.broadcasted_iota(jnp.int32, (Q_TILE, WINDOW), 1) // CHUNK
    return jnp.where((kc >= qq) & (kc <= qq + LEFT_CHUNKS), bias, NEG_INF)


PAD_ROWS = WINDOW - Q_TILE
NT_DIMS = (((1,), (1,)), ((), ()))
TN_DIMS = (((0,), (0,)), ((), ()))


def _head_mask(hh):
    lane = lax.broadcasted_iota(jnp.int32, (1, 128), 1)
    return (lane < 64) if hh == 0 else (lane >= 64)


def _probs(qm, kw, bias, i):
    s = lax.dot_general(qm, kw, NT_DIMS, preferred_element_type=F32) + bias
    col = lax.broadcasted_iota(jnp.int32, (Q_TILE, WINDOW), 1)
    s = jnp.where(col >= PAD_ROWS - Q_TILE * i, s, NEG_INF)
    p = jnp.exp(s - jnp.max(s, axis=-1, keepdims=True))
    return p / jnp.sum(p, axis=-1, keepdims=True)


def _attention(z, gen, name):
    t = z.shape[0]
    n_i = t // Q_TILE

    def body(q_ref, k_ref, v_ref, g_ref, o_ref, kpad, vpad, bias):
        hp, i = pl.program_id(0), pl.program_id(1)

        @pl.when(i == 0)
        def _():
            kpad[0:PAD_ROWS, :] = jnp.zeros((PAD_ROWS, 128), BF16)
            vpad[0:PAD_ROWS, :] = jnp.zeros((PAD_ROWS, 128), BF16)
            kpad[PAD_ROWS:PAD_ROWS + t, :] = k_ref[...].astype(BF16)
            vpad[PAD_ROWS:PAD_ROWS + t, :] = v_ref[...].astype(BF16)
            for hh in range(2):
                bias[hh] = _band_bias(g_ref[pl.ds(2 * hp + hh, 1), :])

        start = pl.multiple_of(i * Q_TILE, Q_TILE)
        kw = kpad[pl.ds(start, WINDOW), :]
        vw = vpad[pl.ds(start, WINDOW), :]
        q = q_ref[...] * (CHUNK ** -0.5)
        out = None
        for hh in range(2):
            mask = _head_mask(hh)
            p = _probs(jnp.where(mask, q, 0.0).astype(BF16), kw, bias[hh], i)
            o = jnp.dot(p.astype(BF16), vw, preferred_element_type=F32)
            out = jnp.where(mask, o, 0.0) if out is None else jnp.where(mask, o, out)
        o_ref[...] = out.astype(BF16)

    return pl.pallas_call(
        body, grid=(4, n_i),
        in_specs=[pl.BlockSpec((Q_TILE, 128), lambda h, i: (i, h)),
                  pl.BlockSpec((t, 128), lambda h, i: (0, 4 + h)),
                  pl.BlockSpec((t, 128), lambda h, i: (0, 8 + h)),
                  pl.BlockSpec((N_HEADS, TOEP), lambda h, i: (0, 0))],
        out_specs=pl.BlockSpec((Q_TILE, 128), lambda h, i: (i, h)),
        out_shape=_sds((t, 512), BF16),
        scratch_shapes=[pltpu.VMEM((PAD_ROWS + t, 128), BF16), pltpu.VMEM((PAD_ROWS + t, 128), BF16),
                        pltpu.VMEM((2, Q_TILE, WINDOW), F32)],
        compiler_params=_params(("arbitrary", "arbitrary")), name=name,
    )(z, z, z, gen)


def _attention_bwd(z, datt, gen, name):
    t = z.shape[0]
    n_i = t // Q_TILE

    def body(q_ref, k_ref, v_ref, d_ref, g_ref, dq_ref, dk_ref, dv_ref, sq_ref, sk_ref, sv_ref, dg_ref,
             kpad, vpad, dkacc, dvacc, bias, dsacc):
        hp, i = pl.program_id(0), pl.program_id(1)

        @pl.when(i == 0)
        def _():
            kpad[0:PAD_ROWS, :] = jnp.zeros((PAD_ROWS, 128), BF16)
            vpad[0:PAD_ROWS, :] = jnp.zeros((PAD_ROWS, 128), BF16)
            kpad[PAD_ROWS:PAD_ROWS + t, :] = k_ref[...].astype(BF16)
            vpad[PAD_ROWS:PAD_ROWS + t, :] = v_ref[...].astype(BF16)
            dkacc[...] = jnp.zeros_like(dkacc)
            dvacc[...] = jnp.zeros_like(dvacc)
            dsacc[...] = jnp.zeros_like(dsacc)
            for hh in range(2):
                bias[hh] = _band_bias(g_ref[pl.ds(2 * hp + hh, 1), :])

        start = pl.multiple_of(i * Q_TILE, Q_TILE)
        win = pl.ds(start, WINDOW)
        kw = kpad[win, :]
        vw = vpad[win, :]
        q = q_ref[...] * (CHUNK ** -0.5)
        do = d_ref[...]
        dq = None
        for hh in range(2):
            mask = _head_mask(hh)
            qm = jnp.where(mask, q, 0.0).astype(BF16)
            dom = jnp.where(mask, do, 0.0).astype(BF16)
            p = _probs(qm, kw, bias[hh], i)
            dp = lax.dot_general(dom, vw, NT_DIMS, preferred_element_type=F32)
            ds = p * (dp - jnp.sum(p * dp, axis=-1, keepdims=True))
            dsacc[hh] += ds
            ds16 = ds.astype(BF16)
            dqh = jnp.dot(ds16, kw, preferred_element_type=F32) * (CHUNK ** -0.5)
            dq = jnp.where(mask, dqh, 0.0) if dq is None else jnp.where(mask, dqh, dq)
            dkacc[win, :] += lax.dot_general(ds16, qm, TN_DIMS, preferred_element_type=F32)
            dvacc[win, :] += lax.dot_general(p.astype(BF16), dom, TN_DIMS, preferred_element_type=F32)
        dq_ref[...] = dq.astype(BF16)

        @pl.when(i == 0)
        def _():
            sq_ref[...] = _colsum(dq)

        @pl.when(i > 0)
        def _():
            sq_ref[...] += _colsum(dq)

        @pl.when(i == n_i - 1)
        def _():
            dk = dkacc[PAD_ROWS:PAD_ROWS + t, :]
            dv = dvacc[PAD_ROWS:PAD_ROWS + t, :]
            dk_ref[...] = dk.astype(BF16)
            dv_ref[...] = dv.astype(BF16)
            sk_ref[...] = _colsum(dk)
            sv_ref[...] = _colsum(dv)
            rr = lax.broadcasted_iota(jnp.int32, (Q_TILE, Q_TILE), 0)
            cc = lax.broadcasted_iota(jnp.int32, (Q_TILE, Q_TILE), 1)
            rev = jnp.where(rr + cc == Q_TILE - 1, 1.0, 0.0).astype(BF16)
            for hh in range(2):
                acc = None
                for part in _split3(dsacc[hh]):
                    pr = jnp.dot(rev, part, preferred_element_type=F32)
                    acc = pr if acc is None else acc + pr
                wide = jnp.concatenate([acc, jnp.zeros((Q_TILE, TOEP - WINDOW), F32)], axis=1)
                dg_ref[pl.ds(2 * hp + hh, 1), :] = _colsum(pltpu.roll(wide, 0, 1, stride=1, stride_axis=0))

    col = lambda off: pl.BlockSpec((t, 128), lambda h, i: (0, off + h))
    tile = lambda: pl.BlockSpec((Q_TILE, 128), lambda h, i: (i, h))
    sums = lambda: pl.BlockSpec((1, 128), lambda h, i: (0, h))
    return pl.pallas_call(
        body, grid=(4, n_i),
        in_specs=[tile(), col(4), col(8), tile(), pl.BlockSpec((N_HEADS, TOEP), lambda h, i: (0, 0))],
        out_specs=[tile(), col(0), col(0), sums(), sums(), sums(), pl.BlockSpec((N_HEADS, TOEP), lambda h, i: (0, 0))],
        out_shape=[_sds((t, 512), BF16)] * 3 + [_sds((1, 512), F32)] * 3 + [_sds((N_HEADS, TOEP), F32)],
        scratch_shapes=[pltpu.VMEM((PAD_ROWS + t, 128), BF16), pltpu.VMEM((PAD_ROWS + t, 128), BF16),
                        pltpu.VMEM((PAD_ROWS + t, 128), F32), pltpu.VMEM((PAD_ROWS + t, 128), F32),
                        pltpu.VMEM((2, Q_TILE, WINDOW), F32), pltpu.VMEM((2, Q_TILE, WINDOW), F32)],
        compiler_params=_params(("arbitrary", "arbitrary")), name=name,
    )(z, z, z, datt, gen)


def _adamw_math(w, g, m, v):
    m = ADAM_B1 * m + (1.0 - ADAM_B1) * g
    v = ADAM_B2 * v + (1.0 - ADAM_B2) * (g * g)
    m_hat = m / (1.0 - ADAM_B1 ** ADAM_STEP)
    v_hat = v / (1.0 - ADAM_B2 ** ADAM_STEP)
    delta = -ADAM_LR * (m_hat / (jnp.sqrt(v_hat) + ADAM_EPS) + ADAM_WD * w)
    return delta, m, v


def _adamw(w, g, m, v, name):
    r, c = w.shape
    tm = next(cand for cand in (256, 176, 128, 64, 32, 16, 8) if r % cand == 0)
    return _rowcall(_adamw_math, [(w, c, 0), (g, c, 0), (m, c, 0), (v, c, 0)], [],
                    [(r, c, F32, c)] * 3, [], name=name, tm=tm)


def _ada_fwd(c_all, w_shard, b_shard, name):
    n = w_shard.shape[1]
    tn = 512

    def body(c_ref, w_ref, b_ref, o_ref, a_ref):
        cv = c_ref[...]
        act = cv * _sigmoid(cv)
        a_ref[...] = act
        o_ref[...] = jnp.dot(act.astype(BF16), w_ref[...].astype(BF16), preferred_element_type=F32) + b_ref[...]

    return pl.pallas_call(
        body, grid=(n // tn,),
        in_specs=[pl.BlockSpec((8, D), lambda j: (0, 0)), pl.BlockSpec((D, tn), lambda j: (0, j)),
                  pl.BlockSpec((1, tn), lambda j: (0, j))],
        out_specs=[pl.BlockSpec((8, tn), lambda j: (0, j)), pl.BlockSpec((8, D), lambda j: (0, 0))],
        out_shape=[_sds((8, n), F32), _sds((8, D), F32)],
        compiler_params=_params(("arbitrary",)), name=name,
    )(c_all, w_shard, b_shard)


def _ada_bwd_adamw(act_t, dmod_shard, w, m, v, name):
    r, c = w.shape
    tm = 256

    def body(a_ref, d_ref, w_ref, m_ref, v_ref, g_ref, dl_ref, nm_ref, nv_ref):
        g = jnp.dot(a_ref[...], d_ref[...], precision=lax.Precision.HIGHEST, preferred_element_type=F32)
        g_ref[...] = g
        dl_ref[...], nm_ref[...], nv_ref[...] = _adamw_math(w_ref[...], g, m_ref[...], v_ref[...])

    blk = pl.BlockSpec((tm, c), lambda i: (i, 0))
    return pl.pallas_call(
        body, grid=(r // tm,),
        in_specs=[pl.BlockSpec((tm, 8), lambda i: (i, 0)), pl.BlockSpec((8, c), lambda i: (0, 0)), blk, blk, blk],
        out_specs=[blk] * 4, out_shape=[_sds((r, c), F32)] * 4,
        compiler_params=_params(("arbitrary",)), name=name,
    )(act_t, dmod_shard, w, m, v)


def _place():
    return lax.axis_index("x"), lax.axis_index("y"), lax.axis_index("c")


def _flip(v, bit):
    return 1 - v if bit else v


ANY = pl.BlockSpec(memory_space=pl.ANY)
VMEM_SPEC = pl.BlockSpec(memory_space=pltpu.VMEM)


def _allgather8(v, name):
    r, c = v.shape

    def body(v_ref, g_ref, tot_ref, send_sems, recv_sems, local_sem):
        x, y, cc = _place()
        me = 4 * x + 2 * y + cc
        mine = pltpu.make_async_copy(v_ref, g_ref.at[me], local_sem)
        mine.start()
        sends = []
        for k in range(1, 8):
            peer = (_flip(x, k & 4), _flip(y, k & 2), _flip(cc, k & 1))
            cp = pltpu.make_async_remote_copy(src_ref=v_ref, dst_ref=g_ref.at[me], send_sem=send_sems.at[k - 1],
                                              recv_sem=recv_sems.at[k - 1], device_id=peer, device_id_type=MESH)
            cp.start()
            sends.append(cp)
        for k in range(1, 8):
            peer = (_flip(x, k & 4), _flip(y, k & 2), _flip(cc, k & 1))
            theirs = g_ref.at[4 * peer[0] + 2 * peer[1] + peer[2]]
            pltpu.make_async_remote_copy(src_ref=v_ref, dst_ref=theirs, send_sem=send_sems.at[k - 1],
                                         recv_sem=recv_sems.at[k - 1], device_id=peer, device_id_type=MESH).wait_recv()
        for cp in sends:
            cp.wait_send()
        mine.wait()
        tot = g_ref[0]
        for d in range(1, 8):
            tot = tot + g_ref[d]
        tot_ref[...] = tot

    return pl.pallas_call(
        body, in_specs=[VMEM_SPEC], out_specs=[VMEM_SPEC, VMEM_SPEC],
        out_shape=[_sds((8, r, c), F32), _sds((r, c), F32)],
        scratch_shapes=[pltpu.SemaphoreType.DMA((7,)), pltpu.SemaphoreType.DMA((7,)), pltpu.SemaphoreType.DMA],
        compiler_params=pltpu.CompilerParams(vmem_limit_bytes=VMEM_LIMIT), name=name,
    )(v)


def _slot(px, py, swapped):
    return 2 * py + px if swapped else 2 * px + py


def _gather_shards(arrs, swapped, name):
    n = len(arrs)

    def body(*refs):
        ins, outs = refs[:n], refs[n:2 * n]
        send1, recv1, send2, recv2, local_sems = refs[2 * n:]
        x, y, c = _place()
        sibling = (x, y, 1 - c)
        chips = [(_flip(x, k & 2), _flip(y, k & 1)) for k in (1, 2, 3)]
        local_copies, sends = [], []
        for a in range(n):
            h = ins[a].shape[0] // 2
            mine = pl.ds(pl.multiple_of(c * h, 8), h)
            own = _slot(x, y, swapped[a])
            lc = pltpu.make_async_copy(ins[a], outs[a].at[own], local_sems.at[a])
            lc.start()
            local_copies.append(lc)
            for j, (px, py) in enumerate(chips):
                cp = pltpu.make_async_remote_copy(
                    src_ref=ins[a].at[mine], dst_ref=outs[a].at[own, mine], send_sem=send1.at[3 * a + j],
                    recv_sem=recv1.at[3 * a + j], device_id=(px, py, c), device_id_type=MESH)
                cp.start()
                sends.append(cp)
        for a in range(n):
            h = ins[a].shape[0] // 2
            mine = pl.ds(pl.multiple_of(c * h, 8), h)
            for j, (px, py) in enumerate(chips):
                piece = outs[a].at[_slot(px, py, swapped[a]), mine]
                pltpu.make_async_remote_copy(
                    src_ref=piece, dst_ref=piece, send_sem=send1.at[3 * a + j], recv_sem=recv1.at[3 * a + j],
                    device_id=(px, py, c), device_id_type=MESH).wait_recv()
                fwd = pltpu.make_async_remote_copy(
                    src_ref=piece, dst_ref=piece, send_sem=send2.at[3 * a + j], recv_sem=recv2.at[3 * a + j],
                    device_id=sibling, device_id_type=MESH)
                fwd.start()
                sends.append(fwd)
        for a in range(n):
            h = ins[a].shape[0] // 2
            other = pl.ds(pl.multiple_of((1 - c) * h, 8), h)
            for j, (px, py) in enumerate(chips):
                piece = outs[a].at[_slot(px, py, swapped[a]), other]
                pltpu.make_async_remote_copy(
                    src_ref=piece, dst_ref=piece, send_sem=send2.at[3 * a + j], recv_sem=recv2.at[3 * a + j],
                    device_id=sibling, device_id_type=MESH).wait_recv()
        for cp in sends:
            cp.wait_send()
        for lc in local_copies:
            lc.wait()

    dma = lambda k: pltpu.SemaphoreType.DMA((k,))
    return pl.pallas_call(
        body, in_specs=[ANY] * n, out_specs=[ANY] * n,
        out_shape=[_sds((4,) + a.shape, a.dtype) for a in arrs],
        scratch_shapes=[dma(3 * n), dma(3 * n), dma(3 * n), dma(3 * n), dma(n)],
        name=name,
    )(*arrs)


def _pair_exchange(grads, name):
    n = len(grads)

    def body(*refs):
        ins, outs = refs[:n], refs[n:2 * n]
        send_sems, recv_sems = refs[2 * n:]
        x, y, c = _place()
        cps = []
        for a in range(n):
            h = ins[a].shape[1] // 2
            theirs = pl.ds(pl.multiple_of((1 - c) * h, 8), h)
            cp = pltpu.make_async_remote_copy(
                src_ref=ins[a].at[:, theirs, :], dst_ref=outs[a], send_sem=send_sems.at[a], recv_sem=recv_sems.at[a],
                device_id=(x, y, 1 - c), device_id_type=MESH)
            cp.start()
            cps.append(cp)
        for cp in cps:
            cp.wait()

    return pl.pallas_call(
        body, in_specs=[ANY] * n, out_specs=[ANY] * n,
        out_shape=[_sds((4, g.shape[1] // 2, g.shape[2]), F32) for g in grads],
        scratch_shapes=[pltpu.SemaphoreType.DMA((n,)), pltpu.SemaphoreType.DMA((n,))], name=name,
    )(*grads)


def _pair_sum(grad, recv, core, name):
    _, r, c = grad.shape
    h = r // 2

    def body(core_ref, g_ref, r_ref, o_ref):
        o_ref[...] = (g_ref[...] + r_ref[...]).astype(BF16)

    return pl.pallas_call(
        body,
        grid_spec=pltpu.PrefetchScalarGridSpec(
            num_scalar_prefetch=1, grid=(4,),
            in_specs=[pl.BlockSpec((None, h, c), lambda s, core_ref: (s, core_ref[0], 0)),
                      pl.BlockSpec((None, h, c), lambda s, core_ref: (s, 0, 0))],
            out_specs=pl.BlockSpec((None, h, c), lambda s, core_ref: (s, 0, 0))),
        out_shape=_sds((4, h, c), BF16), compiler_params=_params(("arbitrary",)), name=name,
    )(core, grad, recv)


def _chip_exchange(parts, swapped, name):
    n = len(parts)

    def body(*refs):
        ins, outs = refs[:n], refs[n:2 * n]
        send_sems, recv_sems = refs[2 * n:]
        x, y, c = _place()
        chips = [(_flip(x, k & 2), _flip(y, k & 1)) for k in (1, 2, 3)]
        cps = []
        for a in range(n):
            for j, (px, py) in enumerate(chips):
                cp = pltpu.make_async_remote_copy(
                    src_ref=ins[a].at[_slot(px, py, swapped[a])], dst_ref=outs[a].at[j],
                    send_sem=send_sems.at[3 * a + j], recv_sem=recv_sems.at[3 * a + j],
                    device_id=(px, py, c), device_id_type=MESH)
                cp.start()
                cps.append(cp)
        for cp in cps:
            cp.wait()

    return pl.pallas_call(
        body, in_specs=[ANY] * n, out_specs=[ANY] * n,
        out_shape=[_sds((3,) + p.shape[1:], BF16) for p in parts],
        scratch_shapes=[pltpu.SemaphoreType.DMA((3 * n,)), pltpu.SemaphoreType.DMA((3 * n,))], name=name,
    )(*parts)


def _chip_sum(part, recv, slot, name):
    _, h, c = part.shape

    def body(slot_ref, p_ref, r_ref, o_ref):
        acc = p_ref[...].astype(F32)
        for j in range(3):
            acc = acc + r_ref[j].astype(F32)
        o_ref[...] = acc

    return pl.pallas_call(
        body,
        grid_spec=pltpu.PrefetchScalarGridSpec(
            num_scalar_prefetch=1, grid=(1,),
            in_specs=[pl.BlockSpec((None, h, c), lambda s, slot_ref: (slot_ref[0], 0, 0)),
                      pl.BlockSpec((3, h, c), lambda s, slot_ref: (0, 0, 0))],
            out_specs=pl.BlockSpec((h, c), lambda s, slot_ref: (0, 0))),
        out_shape=_sds((h, c), F32), compiler_params=_params(("arbitrary",)), name=name,
    )(slot, part, recv)


def _pair_share(halves, name):
    n = len(halves)

    def body(*refs):
        ins, outs = refs[:n], refs[n:2 * n]
        send_sems, recv_sems, local_sems = refs[2 * n:]
        x, y, c = _place()
        cps = []
        for a in range(n):
            h = ins[a].shape[0]
            mine = pl.ds(pl.multiple_of(c * h, 8), h)
            lc = pltpu.make_async_copy(ins[a], outs[a].at[mine], local_sems.at[a])
            lc.start()
            cp = pltpu.make_async_remote_copy(
                src_ref=ins[a], dst_ref=outs[a].at[mine], send_sem=send_sems.at[a], recv_sem=recv_sems.at[a],
                device_id=(x, y, 1 - c), device_id_type=MESH)
            cp.start()
            cps.append((lc, cp))
        for a, (lc, cp) in enumerate(cps):
            h = ins[a].shape[0]
            other = pl.ds(pl.multiple_of((1 - c) * h, 8), h)
            pltpu.make_async_remote_copy(
                src_ref=ins[a], dst_ref=outs[a].at[other], send_sem=send_sems.at[a], recv_sem=recv_sems.at[a],
                device_id=(x, y, 1 - c), device_id_type=MESH).wait_recv()
            cp.wait_send()
            lc.wait()

    return pl.pallas_call(
        body, in_specs=[ANY] * n, out_specs=[ANY] * n,
        out_shape=[_sds((2 * hv.shape[0], hv.shape[1]), F32) for hv in halves],
        scratch_shapes=[pltpu.SemaphoreType.DMA((n,)), pltpu.SemaphoreType.DMA((n,)), pltpu.SemaphoreType.DMA((n,))],
        name=name,
    )(*halves)


def _pack(arrs, rows_multiple=8):
    parts, offs, row = [], [], 0
    for a in arrs:
        flat = a.reshape(-1)
        nrow = -(-flat.shape[0] // D)
        parts.append(jnp.pad(flat, (0, nrow * D - flat.shape[0])))
        offs.append(row)
        row += nrow
    total = -(-row // rows_multiple) * rows_multiple
    if total > row:
        parts.append(jnp.zeros(((total - row) * D,), F32))
    return jnp.concatenate(parts).reshape(total, D), offs


def _unpack(packed, offs, shapes):
    out = []
    for off, shp in zip(offs, shapes):
        size = int(np.prod(shp))
        nrow = -(-size // D)
        out.append(packed[off:off + nrow].reshape(-1)[:size].reshape(shp))
    return out


def _to_bf16(w, name):
    r, c = w.shape
    tm = next(cand for cand in (256, 176, 128, 64, 32, 16) if r % cand == 0)
    return _rowcall(lambda v: v, [(w, c, 0)], [], [(r, c, BF16, c)], [], name=name, tm=tm)[0]


def _unshard_cols(g):
    s, k, n = g.shape
    return jnp.transpose(g, (1, 0, 2)).reshape(k, s * n)


def _ff_swap(v):
    b = FF_BLOCK
    return jnp.concatenate([v[..., 0:b], v[..., 2 * b:3 * b], v[..., b:2 * b], v[..., 3 * b:4 * b]], axis=-1)


def _local_step(x, target, mod, w, small):
    sh_m, sc_m, gt_m, sh_f, sc_f, gt_f = mod
    t = x.shape[0]
    tmm = min(1024, t)

    h1 = _pre_norm(x, small["g_pre_mix"], sc_m, sh_m, "pre_norm_mix")
    z = _matmul(h1, w["in"], form="nn", out_dtype=F32, tm=tmm, tn=1152, tk=D, bias=small["b_in"], name="mm_in")
    att = _attention(z, small["gen"], "attention")
    a = _matmul(att, w["attn_o"], form="nn", out_dtype=F32, tm=tmm, tn=512, tk=512, name="mm_attn_o")
    u1, u3 = _conv_branch(z, small["w_dw_conv"], small["b_dw_conv"], small["g_conv_ln"], small["b_conv_ln"], "conv_branch")
    cb = _matmul(u3, w["conv_o"], form="nn", out_dtype=F32, tm=tmm, tn=512, tk=512, bias=small["b_conv_o"], name="mm_conv_o")
    y = _gate_merge(a, cb, z, "gate_merge")
    ym = _matmul(y, w["mix_o"], form="nn", out_dtype=F32, tm=tmm, tn=512, tk=D, name="mm_mix_o")
    x1 = _post_res(x, ym, small["g_post_mix"], gt_m, "post_res_mix")
    h2 = _pre_norm(x1, small["g_pre_ffn"], sc_f, sh_f, "pre_norm_ffn")
    up = _matmul(h2, w["up"], form="nn", out_dtype=F32, tm=tmm, tn=FF_BLOCK, tk=D, name="mm_up")
    act = _ffn_act(up, small["w_dw_ffn"], small["b_dw_ffn"], "ffn_act")
    yf = _matmul(act, w["down"], form="nn", out_dtype=F32, tm=tmm, tn=512, tk=D_FF, name="mm_down")
    x2 = _post_res(x1, yf, small["g_post_ffn"], gt_f, "post_res_ffn")

    dx2, loss_cols = _loss_grad(x2, target, "loss_grad")
    dyf, d_g_post_ffn, d_gt_f = _post_res_bwd(dx2, yf, small["g_post_ffn"], gt_f, "post_res_ffn_bwd")
    dact = _matmul(dyf, w["down"], form="nt", out_dtype=F32, tm=tmm, tn=FF_BLOCK, tk=D, name="mm_down_dx")
    g_down = _matmul(act, dyf, form="tn", out_dtype=F32, tm=FF_BLOCK, tn=D, tk=tmm, name="mm_down_dw")
    dup, d_w_dw_ffn, d_b_dw_ffn = _ffn_act_bwd(dact, up, small["w_dw_ffn"], small["b_dw_ffn"], "ffn_act_bwd")
    dh2 = _matmul(dup, w["up"], form="nt", out_dtype=F32, tm=tmm, tn=D, tk=FF_BLOCK, name="mm_up_dx")
    g_up = _matmul(h2, dup, form="tn", out_dtype=F32, tm=D, tn=FF_BLOCK, tk=tmm, out_sharded=True, name="mm_up_dw")
    dx1, d_sh_f, d_sc_f, d_g_pre_ffn = _pre_norm_bwd(dh2, x1, dx2, small["g_pre_ffn"], sc_f, "pre_norm_ffn_bwd")
    dym, d_g_post_mix, d_gt_m = _post_res_bwd(dx1, ym, small["g_post_mix"], gt_m, "post_res_mix_bwd")
    dy = _matmul(dym, w["mix_o"], form="nt", out_dtype=F32, tm=tmm, tn=512, tk=D, name="mm_mix_o_dx")
    g_mix_o = _matmul(y, dym, form="tn", out_dtype=F32, tm=D, tn=512, tk=tmm, name="mm_mix_o_dw")
    da, dcb, dgate_a, dgate_b, d_b_conv_o, sga, sgb = _gate_merge_bwd(dy, a, cb, z, "gate_merge_bwd")
    datt = _matmul(da, w["attn_o"], form="nt", out_dtype=F32, tm=tmm, tn=512, tk=D, name="mm_attn_o_dx")
    g_attn_o = _matmul(att, da, form="tn", out_dtype=F32, tm=512, tn=256, tk=tmm, out_sharded=True, name="mm_attn_o_dw")
    du3 = _matmul(dcb, w["conv_o"], form="nt", out_dtype=F32, tm=tmm, tn=512, tk=D, name="mm_conv_o_dx")
    g_conv_o = _matmul(u3, dcb, form="tn", out_dtype=F32, tm=512, tn=256, tk=tmm, out_sharded=True, name="mm_conv_o_dw")
    dglu, d_w_dw_conv, d_b_dw_conv, d_g_conv_ln, d_b_conv_ln, sglu = _conv_branch_bwd(
        du3, u1, z, small["w_dw_conv"], small["g_conv_ln"], small["b_conv_ln"], "conv_branch_bwd")
    dq, dk, dv, sq, sk, sv, dgen = _attention_bwd(z, datt, small["gen"], "attention_bwd")
    dz = jnp.concatenate([dq, dk, dv, dglu, dgate_a, dgate_b], axis=1)
    d_b_in = jnp.concatenate([sq, sk, sv, sglu, sga, sgb], axis=1)
    dh1 = _matmul(dz, w["in"], form="nt", out_dtype=F32, tm=tmm, tn=D, tk=1152, name="mm_in_dx")
    g_in = _matmul(h1, dz, form="tn", out_dtype=F32, tm=D, tn=1152, tk=tmm, out_sharded=True, name="mm_in_dw")
    grad_x, d_sh_m, d_sc_m, d_g_pre_mix = _pre_norm_bwd(dh1, x, dx1, small["g_pre_mix"], sc_m, "pre_norm_mix_bwd")

    dmod = [d_sh_m, d_sc_m, d_gt_m, d_sh_f, d_sc_f, d_gt_f]
    big = {"in": g_in, "attn_o": g_attn_o, "conv_o": g_conv_o, "mix_o": g_mix_o.reshape(4, 256, D),
           "up": g_up, "down": g_down.reshape(4, D_FF // 4, D)}
    sm = {"g_pre_mix": d_g_pre_mix, "g_post_mix": d_g_post_mix, "b_in": d_b_in, "gen": dgen,
          "w_dw_conv": d_w_dw_conv, "b_dw_conv": d_b_dw_conv, "g_conv_ln": d_g_conv_ln, "b_conv_ln": d_b_conv_ln,
          "b_conv_o": d_b_conv_o, "g_pre_ffn": d_g_pre_ffn, "g_post_ffn": d_g_post_ffn,
          "w_dw_ffn": d_w_dw_ffn, "b_dw_ffn": d_b_dw_ffn}
    return loss_cols, grad_x, dmod, big, sm


BIG = ("in", "attn_o", "conv_o", "mix_o", "up", "down")
SWAPPED = {"in": False, "attn_o": False, "conv_o": False, "mix_o": False, "up": True, "down": False}
SMALL_ORDER = ("b_ada", "g_pre_mix", "g_post_mix", "b_in", "rel_bias", "b_dw_conv", "g_conv_ln", "b_conv_ln",
               "b_conv_o", "g_pre_ffn", "g_post_ffn", "b_dw_ffn", "w_dw_conv", "w_dw_ffn")


def kernel(x, c, w_ada, b_ada, g_pre_mix, g_post_mix, w_in, b_in, rel_bias, w_attn_o, w_dw_conv, b_dw_conv, g_conv_ln, b_conv_ln, w_conv_o, b_conv_o, w_mix_o, g_pre_ffn, g_post_ffn, w_up, w_dw_ffn, b_dw_ffn, w_down, loss_target, m_w_ada, m_b_ada, m_g_pre_mix, m_g_post_mix, m_w_in, m_b_in, m_rel_bias, m_w_attn_o, m_w_dw_conv, m_b_dw_conv, m_g_conv_ln, m_b_conv_ln, m_w_conv_o, m_b_conv_o, m_w_mix_o, m_g_pre_ffn, m_g_post_ffn, m_w_up, m_w_dw_ffn, m_b_dw_ffn, m_w_down, v_w_ada, v_b_ada, v_g_pre_mix, v_g_post_mix, v_w_in, v_b_in, v_rel_bias, v_w_attn_o, v_w_dw_conv, v_b_dw_conv, v_g_conv_ln, v_b_conv_ln, v_w_conv_o, v_b_conv_o, v_w_mix_o, v_g_pre_ffn, v_g_post_ffn, v_w_up, v_w_dw_ffn, v_b_dw_ffn, v_w_down):
    given = dict(locals())
    ax, ay, ac = lax.axis_index("x"), lax.axis_index("y"), lax.axis_index("c")
    shard = 2 * ax + ay
    me = 4 * ax + 2 * ay + ac
    xs, target = x[0], loss_target[0]

    c_pad = jnp.pad(c, ((0, 7), (0, 0)))
    c_g, _ = _allgather8(c_pad, "gather_c")
    c_all = c_g[:, 0, :]
    b_ada_shard = lax.dynamic_slice(b_ada, (0, shard * 1536), (1, 1536))
    mod_shard, c_act = _ada_fwd(c_all, w_ada[0], b_ada_shard, "ada_fwd")
    small_in = [jnp.pad(mod_shard, ((0, 8), (0, 0))),
                jnp.pad(w_dw_conv[0], ((0, 1), (0, 0))),
                jnp.pad(w_dw_ffn[0], ((0, 13), (0, 0)))]
    mod_g, wdc_g, wdf_g = _gather_shards(small_in, [False, False, True], "gather_small")
    mod_all = jnp.transpose(mod_g[:, :8, :], (1, 0, 2)).reshape(8, 6 * D)
    mod_row = lax.dynamic_slice(mod_all, (me, 0), (1, 6 * D))
    mod = [mod_row[:, k * D:(k + 1) * D] for k in range(6)]

    shards16 = [_to_bf16(given["w_" + n][0], "cast_" + n) for n in BIG]
    gathered = dict(zip(BIG, _gather_shards(shards16, [SWAPPED[n] for n in BIG], "gather_weights")))
    wts = {"in": gathered["in"], "up": gathered["up"],
           "attn_o": _unshard_cols(gathered["attn_o"]), "conv_o": _unshard_cols(gathered["conv_o"]),
           "mix_o": gathered["mix_o"].reshape(D, D), "down": gathered["down"].reshape(D_FF, D)}

    sel = jnp.asarray(_toeplitz_map())
    rel_pad = jnp.pad(rel_bias[0], ((0, 0), (0, REL_PAD - (2 * MAX_REL + 1))))
    gen = _select_call(rel_pad, sel.T.astype(BF16), "bias_rows")
    small = {"g_pre_mix": g_pre_mix, "g_post_mix": g_post_mix, "b_in": b_in, "gen": gen,
             "w_dw_conv": _unshard_cols(wdc_g[:, :CONV_K, :]), "b_dw_conv": b_dw_conv, "g_conv_ln": g_conv_ln,
             "b_conv_ln": b_conv_ln, "b_conv_o": b_conv_o, "g_pre_ffn": g_pre_ffn, "g_post_ffn": g_post_ffn,
             "w_dw_ffn": _unshard_cols(wdf_g[:, :FFN_K, :]), "b_dw_ffn": _ff_swap(b_dw_ffn)}

    loss_cols, grad_x, dmod, big, sm = _local_step(xs, target, mod, wts, small)
    loss = lax.psum(jnp.sum(loss_cols), ("x", "y", "c"))

    d_rel = _select_call(sm["gen"], sel.astype(BF16), "bias_fold")[:, :2 * MAX_REL + 1]
    small_grads = {"g_pre_mix": sm["g_pre_mix"], "g_post_mix": sm["g_post_mix"], "b_in": sm["b_in"], "rel_bias": d_rel[None],
                   "b_dw_conv": sm["b_dw_conv"], "g_conv_ln": sm["g_conv_ln"], "b_conv_ln": sm["b_conv_ln"],
                   "b_conv_o": sm["b_conv_o"], "g_pre_ffn": sm["g_pre_ffn"], "g_post_ffn": sm["g_post_ffn"],
                   "b_dw_ffn": _ff_swap(sm["b_dw_ffn"]), "w_dw_conv": sm["w_dw_conv"], "w_dw_ffn": _ff_swap(sm["w_dw_ffn"])}
    order = [n for n in SMALL_ORDER if n != "b_ada"]
    packed, offs = _pack([jnp.concatenate(dmod, axis=1)] + [small_grads[n] for n in order])
    every, total = _allgather8(packed, "gather_small_grads")
    dmod_all = every[:, 0:6, :].reshape(8, 6 * D)
    full_shapes = {n: given[n].shape for n in order}
    full_shapes["w_dw_conv"], full_shapes["w_dw_ffn"] = (1, CONV_K, 512), (1, FFN_K, 2 * D_FF)
    sums = dict(zip(order, _unpack(total, offs[1:], [full_shapes[n] for n in order])))
    sums["b_ada"] = total[0:6].reshape(1, 6 * D)
    sums["w_dw_conv"] = lax.dynamic_slice(sums["w_dw_conv"], (0, 0, shard * 128), (1, CONV_K, 128))
    sums["w_dw_ffn"] = lax.dynamic_slice(sums["w_dw_ffn"], (0, 0, shard * FF_BLOCK), (1, FFN_K, FF_BLOCK))

    pw, poffs = _pack([given[n] for n in SMALL_ORDER])
    pg, _ = _pack([sums[n] for n in SMALL_ORDER])
    pm, _ = _pack([given["m_" + n] for n in SMALL_ORDER])
    pv, _ = _pack([given["v_" + n] for n in SMALL_ORDER])
    shapes = [given[n].shape for n in SMALL_ORDER]
    upd = [dict(zip(SMALL_ORDER, _unpack(p, poffs, shapes))) for p in _adamw(pw, pg, pm, pv, "adamw_small")]

    dmod_shard = lax.dynamic_slice(dmod_all, (0, shard * 1536), (8, 1536))
    ada = _ada_bwd_adamw(c_act.T, dmod_shard, w_ada[0], m_w_ada[0], v_w_ada[0], "ada_bwd_adamw")

    core = ac.astype(jnp.int32).reshape(1)
    grads = [big[n] for n in BIG]
    from_sibling = _pair_exchange(grads, "pair_exchange")
    parts = [_pair_sum(g, r, core, "pair_sum_" + n) for n, g, r in zip(BIG, grads, from_sibling)]
    from_chips = _chip_exchange(parts, [SWAPPED[n] for n in BIG], "chip_exchange")
    halves = [_chip_sum(p, r, _slot(ax, ay, SWAPPED[n]).astype(jnp.int32).reshape(1), "chip_sum_" + n)
              for n, p, r in zip(BIG, parts, from_chips)]
    reduced = dict(zip(BIG, _pair_share(halves, "pair_share")))

    out = {"grad_w_ada": ada[0][None], "delta_w_ada": ada[1][None], "new_m_w_ada": ada[2][None], "new_v_w_ada": ada[3][None]}
    for n in BIG:
        g = reduced[n]
        dl, nm, nv = _adamw(given["w_" + n][0], g, given["m_w_" + n][0], given["v_w_" + n][0], "adamw_" + n)
        out["grad_w_" + n], out["delta_w_" + n], out["new_m_w_" + n], out["new_v_w_" + n] = g[None], dl[None], nm[None], nv[None]
    for n in SMALL_ORDER:
        out["grad_" + n], out["delta_" + n], out["new_m_" + n], out["new_v_" + n] = sums[n], upd[0][n], upd[1][n], upd[2][n]

    weights = ["w_ada", "b_ada", "g_pre_mix", "g_post_mix", "w_in", "b_in", "rel_bias", "w_attn_o", "w_dw_conv", "b_dw_conv",
               "g_conv_ln", "b_conv_ln", "w_conv_o", "b_conv_o", "w_mix_o", "g_pre_ffn", "g_post_ffn", "w_up", "w_dw_ffn",
               "b_dw_ffn", "w_down"]
    return (loss, grad_x[None], *[out["grad_" + n] for n in weights], *[out["delta_" + n] for n in weights],
            *[out["new_m_" + n] for n in weights], *[out["new_v_" + n] for n in weights])
```

```python
import functools
import math

import numpy as np
import jax
import jax.numpy as jnp
from jax import lax
from jax.experimental import pallas as pl
from jax.experimental.pallas import tpu as pltpu

F32, BF16 = jnp.float32, jnp.bfloat16
MESH = pl.DeviceIdType.MESH

D = 1024
D_IN = 4608
D_FF = 2816
CONV_K = 31
FFN_K = 3
N_HEADS = 8
CHUNK = 64
LEFT_CHUNKS = 8
MAX_REL = 128
EPS = 1e-6
NEG_INF = -1e30
Q_TILE = 256
WINDOW = Q_TILE + LEFT_CHUNKS * CHUNK
REL_PAD = 384
TOEP = 1024
ROW_TILE = 256
VMEM_LIMIT = 60 * 1024 * 1024

ADAM_LR, ADAM_B1, ADAM_B2, ADAM_EPS, ADAM_WD, ADAM_STEP = 0.001, 0.9, 0.999, 1e-08, 0.01, 10


def _params(sem=None):
    return pltpu.CompilerParams(dimension_semantics=sem, vmem_limit_bytes=VMEM_LIMIT)


def _sds(shape, dtype):
    return jax.ShapeDtypeStruct(tuple(shape), dtype)


def _matmul(a, b, *, form, out_dtype, tm, tn, tk, name, bias=None, add=None, out_sharded=False):
    b3 = b.ndim == 3
    if form == "nn":
        m, k = a.shape
        n = b.shape[0] * b.shape[2] if b3 else b.shape[1]
        dn = (((1,), (0,)), ((), ()))
        a_spec = pl.BlockSpec((tm, tk), lambda i, j, kk: (i, kk))
        b_spec = (pl.BlockSpec((None, tk, tn), lambda i, j, kk: (j, kk, 0)) if b3
                  else pl.BlockSpec((tk, tn), lambda i, j, kk: (kk, j)))
    elif form == "nt":
        m, k = a.shape
        n = b.shape[1] if b3 else b.shape[0]
        dn = (((1,), (1,)), ((), ()))
        a_spec = pl.BlockSpec((tm, tk), lambda i, j, kk: (i, kk))
        b_spec = (pl.BlockSpec((None, tn, tk), lambda i, j, kk: (kk, j, 0)) if b3
                  else pl.BlockSpec((tn, tk), lambda i, j, kk: (j, kk)))
    else:
        k, m = a.shape
        n = b.shape[1]
        dn = (((0,), (0,)), ((), ()))
        a_spec = pl.BlockSpec((tk, tm), lambda i, j, kk: (kk, i))
        b_spec = pl.BlockSpec((tk, tn), lambda i, j, kk: (kk, j))
    assert m % tm == 0 and n % tn == 0 and k % tk == 0, (name, m, n, k, tm, tn, tk)
    nk = k // tk
    in_specs, args = [a_spec, b_spec], [a, b]
    if bias is not None:
        in_specs.append(pl.BlockSpec((1, tn), lambda i, j, kk: (0, j)))
        args.append(bias)
    if add is not None:
        in_specs.append(pl.BlockSpec((tm, tn), lambda i, j, kk: (i, j)))
        args.append(add)
    if out_sharded:
        out_shape = _sds((n // tn, m, tn), out_dtype)
        out_spec = pl.BlockSpec((None, tm, tn), lambda i, j, kk: (j, i, 0))
    else:
        out_shape = _sds((m, n), out_dtype)
        out_spec = pl.BlockSpec((tm, tn), lambda i, j, kk: (i, j))

    def body(*refs):
        a_ref, b_ref = refs[0], refs[1]
        pos = 2
        bias_ref = add_ref = None
        if bias is not None:
            bias_ref, pos = refs[pos], pos + 1
        if add is not None:
            add_ref, pos = refs[pos], pos + 1
        o_ref = refs[pos]
        av, bv = a_ref[...], b_ref[...]
        if av.dtype != BF16:
            av = av.astype(BF16)
        if bv.dtype != BF16:
            bv = bv.astype(BF16)
        p = lax.dot_general(av, bv, dn, preferred_element_type=F32)

        def finish(acc):
            if bias_ref is not None:
                acc = acc + bias_ref[...]
            if add_ref is not None:
                acc = acc + add_ref[...]
            o_ref[...] = acc.astype(o_ref.dtype)

        if nk == 1:
            finish(p)
        else:
            acc_ref = refs[pos + 1]
            kk = pl.program_id(2)

            @pl.when(kk == 0)
            def _():
                acc_ref[...] = p

            @pl.when(kk > 0)
            def _():
                acc_ref[...] += p

            @pl.when(kk == nk - 1)
            def _():
                finish(acc_ref[...])

    return pl.pallas_call(
        body, grid=(m // tm, n // tn, nk), in_specs=in_specs, out_specs=out_spec, out_shape=out_shape,
        scratch_shapes=[pltpu.VMEM((tm, tn), F32)] if nk > 1 else [],
        compiler_params=_params(("parallel", "parallel", "arbitrary")), name=name,
    )(*args)


def _rowcall(fn, rows, consts, row_outs, acc_outs, *, name, tm=ROW_TILE, col_grid=1):
    n_rows = rows[0][0].shape[0]
    assert n_rows % tm == 0
    grid = (col_grid, n_rows // tm)
    in_specs = [pl.BlockSpec((tm, w), functools.partial(lambda c, i, cb: (i, cb + c), cb=cb)) for _, w, cb in rows]
    in_specs += [pl.BlockSpec(k.shape, functools.partial(lambda c, i, nd: (0,) * nd, nd=k.ndim)) for k in consts]
    out_specs = [pl.BlockSpec((tm, w), lambda c, i: (i, c)) for _, _, _, w in row_outs]
    out_specs += [pl.BlockSpec((r, w), lambda c, i: (0, c)) for r, _, w in acc_outs]
    out_shape = [_sds((nr, nc), dt) for nr, nc, dt, _ in row_outs] + [_sds((r, nc), F32) for r, nc, _ in acc_outs]
    n_in, n_ro = len(rows) + len(consts), len(row_outs)

    def body(*refs):
        res = fn(*[r[...] for r in refs[:n_in]])
        if not isinstance(res, (tuple, list)):
            res = (res,)
        outs = refs[n_in:]
        for o_ref, val in zip(outs[:n_ro], res[:n_ro]):
            o_ref[...] = val.astype(o_ref.dtype)
        if acc_outs:
            first = pl.program_id(1) == 0

            @pl.when(first)
            def _():
                for o_ref, val in zip(outs[n_ro:], res[n_ro:]):
                    o_ref[...] = val

            @pl.when(jnp.logical_not(first))
            def _():
                for o_ref, val in zip(outs[n_ro:], res[n_ro:]):
                    o_ref[...] += val

    out = pl.pallas_call(
        body, grid=grid, in_specs=in_specs, out_specs=out_specs, out_shape=out_shape,
        compiler_params=_params(("arbitrary", "arbitrary")), name=name,
    )(*[r[0] for r in rows], *consts)
    return out


def _colsum(v):
    return jnp.sum(v, axis=0, keepdims=True)


def _sigmoid(v):
    return 1.0 / (1.0 + jnp.exp(-v))


_GELU_C = math.sqrt(2.0 / math.pi)


def _gelu(v):
    return 0.5 * v * (1.0 + jnp.tanh(_GELU_C * (v + 0.044715 * (v * v * v))))


def _gelu_and_grad(v):
    th = jnp.tanh(_GELU_C * (v + 0.044715 * (v * v * v)))
    g = 0.5 * v * (1.0 + th)
    dg = 0.5 * (1.0 + th) + 0.5 * v * (1.0 - th * th) * (_GELU_C * (1.0 + 3.0 * 0.044715 * (v * v)))
    return g, dg


def _rms_stats(v):
    r = lax.rsqrt(jnp.mean(v * v, axis=-1, keepdims=True) + EPS)
    return v * r, r


def _rms_bwd(dn, vn, r):
    return r * (dn - vn * jnp.mean(dn * vn, axis=-1, keepdims=True))


def _pre_norm(x, g, sc, sh, name):
    def fn(xv, gv, scv, shv):
        xn, _ = _rms_stats(xv)
        return (xn * gv) * (1.0 + scv) + shv
    return _rowcall(fn, [(x, D, 0)], [g, sc, sh], [(x.shape[0], D, BF16, D)], [], name=name)[0]


def _pre_norm_bwd(dh, x, dx_other, g, sc, name):
    def fn(dhv, xv, dov, gv, scv):
        xn, r = _rms_stats(xv)
        yn = xn * gv
        dyn = dhv * (1.0 + scv)
        dx = _rms_bwd(dyn * gv, xn, r)
        return dov + dx, _colsum(dhv), _colsum(dhv * yn), _colsum(dyn * xn)
    t = x.shape[0]
    return _rowcall(fn, [(dh, D, 0), (x, D, 0), (dx_other, D, 0)], [g, sc], [(t, D, F32, D)],
                    [(1, D, D)] * 3, name=name)


def _post_res(x, ypre, g, gt, name):
    def fn(xv, yv, gv, gtv):
        yn, _ = _rms_stats(yv)
        return xv + gtv * (yn * gv)
    return _rowcall(fn, [(x, D, 0), (ypre, D, 0)], [g, gt], [(x.shape[0], D, F32, D)], [], name=name)[0]


def _post_res_bwd(dxo, ypre, g, gt, name):
    def fn(dv, yv, gv, gtv):
        yn, r = _rms_stats(yv)
        dyn = dv * gtv
        dy = _rms_bwd(dyn * gv, yn, r)
        return dy, _colsum(dyn * yn), _colsum(dv * (yn * gv))
    t = ypre.shape[0]
    return _rowcall(fn, [(dxo, D, 0), (ypre, D, 0)], [g, gt], [(t, D, BF16, D)], [(1, D, D)] * 2, name=name)


def _loss_grad(x2, target, name):
    def fn(xv, tv):
        e = xv - tv
        return e * (1.0 / D), _colsum(e * e) * (0.5 / D)
    return _rowcall(fn, [(x2, D, 0), (target, D, 0)], [], [(x2.shape[0], D, F32, D)], [(1, D, D)], name=name)


def _gate_merge(a, cb, z, name):
    def fn(av, cv, gav, gbv):
        return _sigmoid(gav) * av + _sigmoid(gbv) * cv
    t = a.shape[0]
    return _rowcall(fn, [(a, 512, 0), (cb, 512, 0), (z, 512, 5), (z, 512, 7)], [],
                    [(t, D, BF16, 512)], [], name=name, col_grid=2)[0]


def _gate_merge_bwd(dy, a, cb, z, name):
    def fn(dv, av, cv, gav, gbv):
        sa, sb = _sigmoid(gav), _sigmoid(gbv)
        dcb = dv * sb
        dga = dv * av * (sa * (1.0 - sa))
        dgb = dv * cv * (sb * (1.0 - sb))
        return dv * sa, dcb, dga, dgb, _colsum(dcb), _colsum(dga), _colsum(dgb)
    t = a.shape[0]
    return _rowcall(fn, [(dy, 512, 0), (a, 512, 0), (cb, 512, 0), (z, 512, 5), (z, 512, 7)], [],
                    [(t, D, BF16, 512)] * 4, [(1, D, 512)] * 3, name=name, col_grid=2)


CONV_HALO = 32


def _layer_norm_parts(u):
    mu = jnp.mean(u, axis=-1, keepdims=True)
    d = u - mu
    r = lax.rsqrt(jnp.mean(d * d, axis=-1, keepdims=True) + EPS)
    return d * r, r


def _conv_branch(z, w_dw, b_dw, g_ln, b_ln, name, tm=ROW_TILE):
    t = z.shape[0]
    per = tm // CONV_HALO

    def body(ga_ref, gb_ref, gah_ref, gbh_ref, w_ref, b_ref, g_ref, bl_ref, u1_ref, u3_ref, scr):
        i = pl.program_id(0)
        u0h = gah_ref[...] * _sigmoid(gbh_ref[...])
        scr[0:CONV_HALO, :] = jnp.where(i > 0, u0h, 0.0)
        scr[CONV_HALO:CONV_HALO + tm, :] = ga_ref[...] * _sigmoid(gb_ref[...])
        acc = jnp.zeros((tm, 512), F32) + b_ref[...]
        for j in range(CONV_K):
            acc = acc + w_ref[j:j + 1, :] * scr[pl.ds(CONV_HALO - (CONV_K - 1) + j, tm), :]
        u1_ref[...] = acc
        xh, _ = _layer_norm_parts(acc)
        u2 = xh * g_ref[...] + bl_ref[...]
        u3_ref[...] = (u2 * _sigmoid(u2)).astype(BF16)

    cur = lambda cb: pl.BlockSpec((tm, 512), lambda i: (i, cb))
    halo = lambda cb: pl.BlockSpec((CONV_HALO, 512), lambda i: (jnp.maximum(i * per - 1, 0), cb))
    whole = lambda a: pl.BlockSpec(a.shape, lambda i: (0, 0))
    return pl.pallas_call(
        body, grid=(t // tm,),
        in_specs=[cur(3), cur(4), halo(3), halo(4), whole(w_dw), whole(b_dw), whole(g_ln), whole(b_ln)],
        out_specs=[pl.BlockSpec((tm, 512), lambda i: (i, 0))] * 2,
        out_shape=[_sds((t, 512), F32), _sds((t, 512), BF16)],
        scratch_shapes=[pltpu.VMEM((CONV_HALO + tm, 512), F32)],
        compiler_params=_params(("arbitrary",)), name=name,
    )(z, z, z, z, w_dw, b_dw, g_ln, b_ln)


def _conv_branch_bwd(du3, u1, z, w_dw, g_ln, b_ln, name, tm=ROW_TILE):
    t = z.shape[0]
    per = tm // CONV_HALO
    last = t // tm - 1

    def du1_of(du3v, u1v, g, b):
        xh, r = _layer_norm_parts(u1v)
        u2 = xh * g + b
        s = _sigmoid(u2)
        du2 = du3v * (s * (1.0 + u2 * (1.0 - s)))
        dxh = du2 * g
        du1 = r * (dxh - jnp.mean(dxh, axis=-1, keepdims=True) - xh * jnp.mean(dxh * xh, axis=-1, keepdims=True))
        return du1, du2, xh

    def body(d_ref, u_ref, dn_ref, un_ref, ga_ref, gb_ref, gah_ref, gbh_ref, w_ref, g_ref, bl_ref,
             dglu_ref, dw_ref, dbdw_ref, dg_ref, dbl_ref, dbin_ref, scr, scd):
        i = pl.program_id(0)
        g, b = g_ref[...], bl_ref[...]
        du1, du2, xh = du1_of(d_ref[...], u_ref[...], g, b)
        du1n, _, _ = du1_of(dn_ref[...], un_ref[...], g, b)
        scd[0:tm, :] = du1
        scd[tm:tm + CONV_HALO, :] = jnp.where(i < last, du1n, 0.0)
        sgb = _sigmoid(gb_ref[...])
        ga = ga_ref[...]
        scr[0:CONV_HALO, :] = jnp.where(i > 0, gah_ref[...] * _sigmoid(gbh_ref[...]), 0.0)
        scr[CONV_HALO:CONV_HALO + tm, :] = ga * sgb
        first = i == 0

        @pl.when(first)
        def _():
            dw_ref[...] = jnp.zeros_like(dw_ref)

        du0 = jnp.zeros((tm, 512), F32)
        for j in range(CONV_K):
            du0 = du0 + w_ref[j:j + 1, :] * scd[pl.ds(CONV_K - 1 - j, tm), :]
            dw_ref[j:j + 1, :] += _colsum(du1 * scr[pl.ds(CONV_HALO - (CONV_K - 1) + j, tm), :])
        dga = du0 * sgb
        dgb = du0 * ga * (sgb * (1.0 - sgb))
        dglu_ref[:, 0:512] = dga.astype(BF16)
        dglu_ref[:, 512:1024] = dgb.astype(BF16)
        parts = (_colsum(du1), _colsum(du2 * xh), _colsum(du2))

        @pl.when(first)
        def _():
            dbdw_ref[...], dg_ref[...], dbl_ref[...] = parts
            dbin_ref[:, 0:512] = _colsum(dga)
            dbin_ref[:, 512:1024] = _colsum(dgb)

        @pl.when(jnp.logical_not(first))
        def _():
            dbdw_ref[...] += parts[0]
            dg_ref[...] += parts[1]
            dbl_ref[...] += parts[2]
            dbin_ref[:, 0:512] += _colsum(dga)
            dbin_ref[:, 512:1024] += _colsum(dgb)

    cur = lambda cb: pl.BlockSpec((tm, 512), lambda i: (i, cb))
    prev = lambda cb: pl.BlockSpec((CONV_HALO, 512), lambda i: (jnp.maximum(i * per - 1, 0), cb))
    nxt = pl.BlockSpec((CONV_HALO, 512), lambda i: (jnp.minimum((i + 1) * per, t // CONV_HALO - 1), 0))
    whole = lambda a: pl.BlockSpec(a.shape, lambda i: (0, 0))
    acc = lambda r, w: pl.BlockSpec((r, w), lambda i: (0, 0))
    return pl.pallas_call(
        body, grid=(t // tm,),
        in_specs=[cur(0), cur(0), nxt, nxt, cur(3), cur(4), prev(3), prev(4), whole(w_dw), whole(g_ln), whole(b_ln)],
        out_specs=[pl.BlockSpec((tm, 1024), lambda i: (i, 0)), acc(CONV_K, 512), acc(1, 512), acc(1, 512),
                   acc(1, 512), acc(1, 1024)],
        out_shape=[_sds((t, 1024), BF16), _sds((CONV_K, 512), F32), _sds((1, 512), F32), _sds((1, 512), F32),
                   _sds((1, 512), F32), _sds((1, 1024), F32)],
        scratch_shapes=[pltpu.VMEM((CONV_HALO + tm, 512), F32), pltpu.VMEM((tm + CONV_HALO, 512), F32)],
        compiler_params=_params(("arbitrary",)), name=name,
    )(du3, u1, du3, u1, z, z, z, z, w_dw, g_ln, b_ln)


FF_BLOCK = D_FF // 2
FF_HALO = 8
LANE_CHUNK = 128


def _ffn_act(up, w3, b3, name, tm=ROW_TILE):
    t = up.shape[0]
    per = tm // FF_HALO
    wide = 2 * FF_BLOCK

    def body(u_ref, uh_ref, w_ref, b_ref, o_ref, scr):
        i = pl.program_id(1)
        scr[0:FF_HALO, :] = jnp.where(i > 0, uh_ref[...], 0.0)
        scr[FF_HALO:FF_HALO + tm, :] = u_ref[...]
        for cc in range(FF_BLOCK // LANE_CHUNK):
            vs = slice(cc * LANE_CHUNK, (cc + 1) * LANE_CHUNK)
            gs = slice(FF_BLOCK + cc * LANE_CHUNK, FF_BLOCK + (cc + 1) * LANE_CHUNK)

            def conv(cs):
                acc = b_ref[:, cs] + w_ref[0:1, cs] * scr[pl.ds(FF_HALO - 2, tm), cs]
                acc = acc + w_ref[1:2, cs] * scr[pl.ds(FF_HALO - 1, tm), cs]
                return acc + w_ref[2:3, cs] * scr[pl.ds(FF_HALO, tm), cs]

            o_ref[:, vs] = (_gelu(conv(gs)) * conv(vs)).astype(BF16)

    return pl.pallas_call(
        body, grid=(2, t // tm),
        in_specs=[pl.BlockSpec((tm, wide), lambda c, i: (i, c)),
                  pl.BlockSpec((FF_HALO, wide), lambda c, i: (jnp.maximum(i * per - 1, 0), c)),
                  pl.BlockSpec((FFN_K, wide), lambda c, i: (0, c)),
                  pl.BlockSpec((1, wide), lambda c, i: (0, c))],
        out_specs=pl.BlockSpec((tm, FF_BLOCK), lambda c, i: (i, c)),
        out_shape=_sds((t, D_FF), BF16),
        scratch_shapes=[pltpu.VMEM((FF_HALO + tm, wide), F32)],
        compiler_params=_params(("arbitrary", "arbitrary")), name=name,
    )(up, up, w3, b3)


def _ffn_act_bwd(dact, up, w3, b3, name, tm=ROW_TILE):
    t = up.shape[0]
    per = tm // FF_HALO
    wide = 2 * FF_BLOCK
    last = t // tm - 1
    ext = tm + FF_HALO

    def body(u_ref, up_ref, un_ref, d_ref, dn_ref, w_ref, b_ref, o_ref, dw_ref, db_ref, scr, scd):
        i = pl.program_id(1)
        first = i == 0
        scr[0:FF_HALO, :] = jnp.where(i > 0, up_ref[...], 0.0)
        scr[FF_HALO:FF_HALO + tm, :] = u_ref[...]
        scr[FF_HALO + tm:FF_HALO + ext, :] = un_ref[...]
        dn = jnp.where(i < last, dn_ref[...], 0.0)

        @pl.when(first)
        def _():
            dw_ref[...] = jnp.zeros_like(dw_ref)
            db_ref[...] = jnp.zeros_like(db_ref)

        for cc in range(FF_BLOCK // LANE_CHUNK):
            vs = slice(cc * LANE_CHUNK, (cc + 1) * LANE_CHUNK)
            gs = slice(FF_BLOCK + cc * LANE_CHUNK, FF_BLOCK + (cc + 1) * LANE_CHUNK)

            def conv(cs):
                acc = b_ref[:, cs] + w_ref[0:1, cs] * scr[pl.ds(FF_HALO - 2, ext), cs]
                acc = acc + w_ref[1:2, cs] * scr[pl.ds(FF_HALO - 1, ext), cs]
                return acc + w_ref[2:3, cs] * scr[pl.ds(FF_HALO, ext), cs]

            val = conv(vs)
            gel, dgel = _gelu_and_grad(conv(gs))
            da = jnp.concatenate([d_ref[:, vs], dn[:, vs]], axis=0)
            scd[:, vs] = da * gel
            scd[:, gs] = da * val * dgel
            for cs in (vs, gs):
                o_ref[:, cs] = (w_ref[0:1, cs] * scd[pl.ds(2, tm), cs] + w_ref[1:2, cs] * scd[pl.ds(1, tm), cs]
                                + w_ref[2:3, cs] * scd[pl.ds(0, tm), cs]).astype(BF16)
                dcur = scd[pl.ds(0, tm), cs]
                for j in range(FFN_K):
                    dw_ref[j:j + 1, cs] += _colsum(dcur * scr[pl.ds(FF_HALO - 2 + j, tm), cs])
                db_ref[:, cs] += _colsum(dcur)

    nblk = t // FF_HALO
    return pl.pallas_call(
        body, grid=(2, t // tm),
        in_specs=[pl.BlockSpec((tm, wide), lambda c, i: (i, c)),
                  pl.BlockSpec((FF_HALO, wide), lambda c, i: (jnp.maximum(i * per - 1, 0), c)),
                  pl.BlockSpec((FF_HALO, wide), lambda c, i: (jnp.minimum((i + 1) * per, nblk - 1), c)),
                  pl.BlockSpec((tm, FF_BLOCK), lambda c, i: (i, c)),
                  pl.BlockSpec((FF_HALO, FF_BLOCK), lambda c, i: (jnp.minimum((i + 1) * per, nblk - 1), c)),
                  pl.BlockSpec((FFN_K, wide), lambda c, i: (0, c)),
                  pl.BlockSpec((1, wide), lambda c, i: (0, c))],
        out_specs=[pl.BlockSpec((tm, wide), lambda c, i: (i, c)),
                   pl.BlockSpec((FFN_K, wide), lambda c, i: (0, c)),
                   pl.BlockSpec((1, wide), lambda c, i: (0, c))],
        out_shape=[_sds((t, 2 * D_FF), BF16), _sds((FFN_K, 2 * D_FF), F32), _sds((1, 2 * D_FF), F32)],
        scratch_shapes=[pltpu.VMEM((FF_HALO + ext, wide), F32), pltpu.VMEM((ext, wide), F32)],
        compiler_params=_params(("arbitrary", "arbitrary")), name=name,
    )(up, up, up, dact, dact, w3, b3)


def _toeplitz_map():
    f = np.zeros((TOEP, REL_PAD), np.float32)
    for m in range(TOEP - 1):
        rel = (WINDOW - 1) - m
        f[m, int(np.clip(rel, -MAX_REL, MAX_REL)) + MAX_REL] = 1.0
    return f


def _split3(v):
    hi = v.astype(BF16)
    r1 = v - hi.astype(F32)
    mid = r1.astype(BF16)
    lo = (r1 - mid.astype(F32)).astype(BF16)
    return hi, mid, lo


def _exact_select(v, sel):
    out = None
    for part in _split3(v):
        p = jnp.dot(part, sel, preferred_element_type=F32)
        out = p if out is None else out + p
    return out


def _select_call(v, sel, name):
    def body(v_ref, s_ref, o_ref):
        o_ref[...] = _exact_select(v_ref[...], s_ref[...])
    return pl.pallas_call(body, out_shape=_sds((v.shape[0], sel.shape[1]), F32), name=name)(v, sel)


def _band_bias(gen_row):
    b0 = jnp.broadcast_to(gen_row, (Q_TILE, TOEP))
    bias = pltpu.roll(b0, TOEP - 255, 1, stride=1, stride_axis=0)[:, :WINDOW]
    qq = lax.broadcasted_iota(jnp.int32, (Q_TILE, WINDOW), 0) // CHUNK
    kc = lax.broadcasted_iota(jnp.int32, (Q_TILE, WINDOW), 1) // CHUNK
    return jnp.where((kc >= qq) & (kc <= qq + LEFT_CHUNKS), bias, NEG_INF)


PAD_ROWS = WINDOW - Q_TILE
NT_DIMS = (((1,), (1,)), ((), ()))
TN_DIMS = (((0,), (0,)), ((), ()))


def _head_mask(hh):
    lane = lax.broadcasted_iota(jnp.int32, (1, 128), 1)
    return (lane < 64) if hh == 0 else (lane >= 64)


def _probs(qm, kw, bias, i):
    s = lax.dot_general(qm, kw, NT_DIMS, preferred_element_type=F32) + bias
    col = lax.broadcasted_iota(jnp.int32, (Q_TILE, WINDOW), 1)
    s = jnp.where(col >= PAD_ROWS - Q_TILE * i, s, NEG_INF)
    p = jnp.exp(s - jnp.max(s, axis=-1, keepdims=True))
    return p / jnp.sum(p, axis=-1, keepdims=True)


def _attention(z, gen, name):
    t = z.shape[0]
    n_i = t // Q_TILE

    def body(q_ref, k_ref, v_ref, g_ref, o_ref, kpad, vpad, bias):
        hp, i = pl.program_id(0), pl.program_id(1)

        @pl.when(i == 0)
        def _():
            kpad[0:PAD_ROWS, :] = jnp.zeros((PAD_ROWS, 128), BF16)
            vpad[0:PAD_ROWS, :] = jnp.zeros((PAD_ROWS, 128), BF16)
            kpad[PAD_ROWS:PAD_ROWS + t, :] = k_ref[...].astype(BF16)
            vpad[PAD_ROWS:PAD_ROWS + t, :] = v_ref[...].astype(BF16)
            for hh in range(2):
                bias[hh] = _band_bias(g_ref[pl.ds(2 * hp + hh, 1), :])

        start = pl.multiple_of(i * Q_TILE, Q_TILE)
        kw = kpad[pl.ds(start, WINDOW), :]
        vw = vpad[pl.ds(start, WINDOW), :]
        q = q_ref[...] * (CHUNK ** -0.5)
        out = None
        for hh in range(2):
            mask = _head_mask(hh)
            p = _probs(jnp.where(mask, q, 0.0).astype(BF16), kw, bias[hh], i)
            o = jnp.dot(p.astype(BF16), vw, preferred_element_type=F32)
            out = jnp.where(mask, o, 0.0) if out is None else jnp.where(mask, o, out)
        o_ref[...] = out.astype(BF16)

    return pl.pallas_call(
        body, grid=(4, n_i),
        in_specs=[pl.BlockSpec((Q_TILE, 128), lambda h, i: (i, h)),
                  pl.BlockSpec((t, 128), lambda h, i: (0, 4 + h)),
                  pl.BlockSpec((t, 128), lambda h, i: (0, 8 + h)),
                  pl.BlockSpec((N_HEADS, TOEP), lambda h, i: (0, 0))],
        out_specs=pl.BlockSpec((Q_TILE, 128), lambda h, i: (i, h)),
        out_shape=_sds((t, 512), BF16),
        scratch_shapes=[pltpu.VMEM((PAD_ROWS + t, 128), BF16), pltpu.VMEM((PAD_ROWS + t, 128), BF16),
                        pltpu.VMEM((2, Q_TILE, WINDOW), F32)],
        compiler_params=_params(("arbitrary", "arbitrary")), name=name,
    )(z, z, z, gen)


def _attention_bwd(z, datt, gen, name):
    t = z.shape[0]
    n_i = t // Q_TILE

    def body(q_ref, k_ref, v_ref, d_ref, g_ref, dq_ref, dk_ref, dv_ref, sq_ref, sk_ref, sv_ref, dg_ref,
             kpad, vpad, dkacc, dvacc, bias, dsacc):
        hp, i = pl.program_id(0), pl.program_id(1)

        @pl.when(i == 0)
        def _():
            kpad[0:PAD_ROWS, :] = jnp.zeros((PAD_ROWS, 128), BF16)
            vpad[0:PAD_ROWS, :] = jnp.zeros((PAD_ROWS, 128), BF16)
            kpad[PAD_ROWS:PAD_ROWS + t, :] = k_ref[...].astype(BF16)
            vpad[PAD_ROWS:PAD_ROWS + t, :] = v_ref[...].astype(BF16)
            dkacc[...] = jnp.zeros_like(dkacc)
            dvacc[...] = jnp.zeros_like(dvacc)
            dsacc[...] = jnp.zeros_like(dsacc)
            for hh in range(2):
                bias[hh] = _band_bias(g_ref[pl.ds(2 * hp + hh, 1), :])

        start = pl.multiple_of(i * Q_TILE, Q_TILE)
        win = pl.ds(start, WINDOW)
        kw = kpad[win, :]
        vw = vpad[win, :]
        q = q_ref[...] * (CHUNK ** -0.5)
        do = d_ref[...]
        dq = None
        for hh in range(2):
            mask = _head_mask(hh)
            qm = jnp.where(mask, q, 0.0).astype(BF16)
            dom = jnp.where(mask, do, 0.0).astype(BF16)
            p = _probs(qm, kw, bias[hh], i)
            dp = lax.dot_general(dom, vw, NT_DIMS, preferred_element_type=F32)
            ds = p * (dp - jnp.sum(p * dp, axis=-1, keepdims=True))
            dsacc[hh] += ds
            ds16 = ds.astype(BF16)
            dqh = jnp.dot(ds16, kw, preferred_element_type=F32) * (CHUNK ** -0.5)
            dq = jnp.where(mask, dqh, 0.0) if dq is None else jnp.where(mask, dqh, dq)
            dkacc[win, :] += lax.dot_general(ds16, qm, TN_DIMS, preferred_element_type=F32)
            dvacc[win, :] += lax.dot_general(p.astype(BF16), dom, TN_DIMS, preferred_element_type=F32)
        dq_ref[...] = dq.astype(BF16)

        @pl.when(i == 0)
        def _():
            sq_ref[...] = _colsum(dq)

        @pl.when(i > 0)
        def _():
            sq_ref[...] += _colsum(dq)

        @pl.when(i == n_i - 1)
        def _():
            dk = dkacc[PAD_ROWS:PAD_ROWS + t, :]
            dv = dvacc[PAD_ROWS:PAD_ROWS + t, :]
            dk_ref[...] = dk.astype(BF16)
            dv_ref[...] = dv.astype(BF16)
            sk_ref[...] = _colsum(dk)
            sv_ref[...] = _colsum(dv)
            rr = lax.broadcasted_iota(jnp.int32, (Q_TILE, Q_TILE), 0)
            cc = lax.broadcasted_iota(jnp.int32, (Q_TILE, Q_TILE), 1)
            rev = jnp.where(rr + cc == Q_TILE - 1, 1.0, 0.0).astype(BF16)
            for hh in range(2):
                acc = None
                for part in _split3(dsacc[hh]):
                    pr = jnp.dot(rev, part, preferred_element_type=F32)
                    acc = pr if acc is None else acc + pr
                wide = jnp.concatenate([acc, jnp.zeros((Q_TILE, TOEP - WINDOW), F32)], axis=1)
                dg_ref[pl.ds(2 * hp + hh, 1), :] = _colsum(pltpu.roll(wide, 0, 1, stride=1, stride_axis=0))

    col = lambda off: pl.BlockSpec((t, 128), lambda h, i: (0, off + h))
    tile = lambda: pl.BlockSpec((Q_TILE, 128), lambda h, i: (i, h))
    sums = lambda: pl.BlockSpec((1, 128), lambda h, i: (0, h))
    return pl.pallas_call(
        body, grid=(4, n_i),
        in_specs=[tile(), col(4), col(8), tile(), pl.BlockSpec((N_HEADS, TOEP), lambda h, i: (0, 0))],
        out_specs=[tile(), col(0), col(0), sums(), sums(), sums(), pl.BlockSpec((N_HEADS, TOEP), lambda h, i: (0, 0))],
        out_shape=[_sds((t, 512), BF16)] * 3 + [_sds((1, 512), F32)] * 3 + [_sds((N_HEADS, TOEP), F32)],
        scratch_shapes=[pltpu.VMEM((PAD_ROWS + t, 128), BF16), pltpu.VMEM((PAD_ROWS + t, 128), BF16),
                        pltpu.VMEM((PAD_ROWS + t, 128), F32), pltpu.VMEM((PAD_ROWS + t, 128), F32),
                        pltpu.VMEM((2, Q_TILE, WINDOW), F32), pltpu.VMEM((2, Q_TILE, WINDOW), F32)],
        compiler_params=_params(("arbitrary", "arbitrary")), name=name,
    )(z, z, z, datt, gen)


def _adamw_math(w, g, m, v):
    m = ADAM_B1 * m + (1.0 - ADAM_B1) * g
    v = ADAM_B2 * v + (1.0 - ADAM_B2) * (g * g)
    m_hat = m / (1.0 - ADAM_B1 ** ADAM_STEP)
    v_hat = v / (1.0 - ADAM_B2 ** ADAM_STEP)
    delta = -ADAM_LR * (m_hat / (jnp.sqrt(v_hat) + ADAM_EPS) + ADAM_WD * w)
    return delta, m, v


def _adamw(w, g, m, v, name):
    r, c = w.shape
    tm = next(cand for cand in (256, 176, 128, 64, 32, 16, 8) if r % cand == 0)
    return _rowcall(_adamw_math, [(w, c, 0), (g, c, 0), (m, c, 0), (v, c, 0)], [],
                    [(r, c, F32, c)] * 3, [], name=name, tm=tm)


def _ada_fwd(c_all, w_shard, b_shard, name):
    n = w_shard.shape[1]
    tn = 512

    def body(c_ref, w_ref, b_ref, o_ref, a_ref):
        cv = c_ref[...]
        act = cv * _sigmoid(cv)
        a_ref[...] = act
        o_ref[...] = jnp.dot(act.astype(BF16), w_ref[...].astype(BF16), preferred_element_type=F32) + b_ref[...]

    return pl.pallas_call(
        body, grid=(n // tn,),
        in_specs=[pl.BlockSpec((8, D), lambda j: (0, 0)), pl.BlockSpec((D, tn), lambda j: (0, j)),
                  pl.BlockSpec((1, tn), lambda j: (0, j))],
        out_specs=[pl.BlockSpec((8, tn), lambda j: (0, j)), pl.BlockSpec((8, D), lambda j: (0, 0))],
        out_shape=[_sds((8, n), F32), _sds((8, D), F32)],
        compiler_params=_params(("arbitrary",)), name=name,
    )(c_all, w_shard, b_shard)


def _ada_bwd_adamw(act_t, dmod_shard, w, m, v, name):
    r, c = w.shape
    tm = 256

    def body(a_ref, d_ref, w_ref, m_ref, v_ref, g_ref, dl_ref, nm_ref, nv_ref):
        g = jnp.dot(a_ref[...], d_ref[...], precision=lax.Precision.HIGHEST, preferred_element_type=F32)
        g_ref[...] = g
        dl_ref[...], nm_ref[...], nv_ref[...] = _adamw_math(w_ref[...], g, m_ref[...], v_ref[...])

    blk = pl.BlockSpec((tm, c), lambda i: (i, 0))
    return pl.pallas_call(
        body, grid=(r // tm,),
        in_specs=[pl.BlockSpec((tm, 8), lambda i: (i, 0)), pl.BlockSpec((8, c), lambda i: (0, 0)), blk, blk, blk],
        out_specs=[blk] * 4, out_shape=[_sds((r, c), F32)] * 4,
        compiler_params=_params(("arbitrary",)), name=name,
    )(act_t, dmod_shard, w, m, v)


def _place():
    return lax.axis_index("x"), lax.axis_index("y"), lax.axis_index("c")


def _flip(v, bit):
    return 1 - v if bit else v


ANY = pl.BlockSpec(memory_space=pl.ANY)
VMEM_SPEC = pl.BlockSpec(memory_space=pltpu.VMEM)


def _allgather8(v, name):
    r, c = v.shape

    def body(v_ref, g_ref, tot_ref, send_sems, recv_sems, local_sem):
        x, y, cc = _place()
        me = 4 * x + 2 * y + cc
        mine = pltpu.make_async_copy(v_ref, g_ref.at[me], local_sem)
        mine.start()
        sends = []
        for k in range(1, 8):
            peer = (_flip(x, k & 4), _flip(y, k & 2), _flip(cc, k & 1))
            cp = pltpu.make_async_remote_copy(src_ref=v_ref, dst_ref=g_ref.at[me], send_sem=send_sems.at[k - 1],
                                              recv_sem=recv_sems.at[k - 1], device_id=peer, device_id_type=MESH)
            cp.start()
            sends.append(cp)
        for k in range(1, 8):
            peer = (_flip(x, k & 4), _flip(y, k & 2), _flip(cc, k & 1))
            theirs = g_ref.at[4 * peer[0] + 2 * peer[1] + peer[2]]
            pltpu.make_async_remote_copy(src_ref=v_ref, dst_ref=theirs, send_sem=send_sems.at[k - 1],
                                         recv_sem=recv_sems.at[k - 1], device_id=peer, device_id_type=MESH).wait_recv()
        for cp in sends:
            cp.wait_send()
        mine.wait()
        tot = g_ref[0]
        for d in range(1, 8):
            tot = tot + g_ref[d]
        tot_ref[...] = tot

    return pl.pallas_call(
        body, in_specs=[VMEM_SPEC], out_specs=[VMEM_SPEC, VMEM_SPEC],
        out_shape=[_sds((8, r, c), F32), _sds((r, c), F32)],
        scratch_shapes=[pltpu.SemaphoreType.DMA((7,)), pltpu.SemaphoreType.DMA((7,)), pltpu.SemaphoreType.DMA],
        compiler_params=pltpu.CompilerParams(vmem_limit_bytes=VMEM_LIMIT), name=name,
    )(v)


def _slot(px, py, swapped):
    return 2 * py + px if swapped else 2 * px + py


def _gather_shards(arrs, swapped, name, in_place=False):
    n = len(arrs)

    def body(*refs):
        ins, outs = refs[:n], refs[n:2 * n]
        send1, recv1, send2, recv2, local_sems = refs[2 * n:]
        x, y, c = _place()
        sibling = (x, y, 1 - c)
        chips = [(_flip(x, k & 2), _flip(y, k & 1)) for k in (1, 2, 3)]
        local_copies, sends = [], []
        for a in range(n):
            h = outs[a].shape[1] // 2
            mine = pl.ds(pl.multiple_of(c * h, 8), h)
            own = _slot(x, y, swapped[a])
            if in_place:
                src = outs[a].at[own, mine]
            else:
                src = ins[a].at[mine]
                lc = pltpu.make_async_copy(ins[a], outs[a].at[own], local_sems.at[a])
                lc.start()
                local_copies.append(lc)
            for j, (px, py) in enumerate(chips):
                cp = pltpu.make_async_remote_copy(
                    src_ref=src, dst_ref=outs[a].at[own, mine], send_sem=send1.at[3 * a + j],
                    recv_sem=recv1.at[3 * a + j], device_id=(px, py, c), device_id_type=MESH)
                cp.start()
                sends.append(cp)
        for a in range(n):
            h = outs[a].shape[1] // 2
            mine = pl.ds(pl.multiple_of(c * h, 8), h)
            for j, (px, py) in enumerate(chips):
                piece = outs[a].at[_slot(px, py, swapped[a]), mine]
                pltpu.make_async_remote_copy(
                    src_ref=piece, dst_ref=piece, send_sem=send1.at[3 * a + j], recv_sem=recv1.at[3 * a + j],
                    device_id=(px, py, c), device_id_type=MESH).wait_recv()
                fwd = pltpu.make_async_remote_copy(
                    src_ref=piece, dst_ref=piece, send_sem=send2.at[3 * a + j], recv_sem=recv2.at[3 * a + j],
                    device_id=sibling, device_id_type=MESH)
                fwd.start()
                sends.append(fwd)
        for a in range(n):
            h = outs[a].shape[1] // 2
            other = pl.ds(pl.multiple_of((1 - c) * h, 8), h)
            for j, (px, py) in enumerate(chips):
                piece = outs[a].at[_slot(px, py, swapped[a]), other]
                pltpu.make_async_remote_copy(
                    src_ref=piece, dst_ref=piece, send_sem=send2.at[3 * a + j], recv_sem=recv2.at[3 * a + j],
                    device_id=sibling, device_id_type=MESH).wait_recv()
        for cp in sends:
            cp.wait_send()
        for lc in local_copies:
            lc.wait()

    dma = lambda k: pltpu.SemaphoreType.DMA((k,))
    return pl.pallas_call(
        body, in_specs=[ANY] * n, out_specs=[ANY] * n,
        out_shape=[_sds(a.shape if in_place else (4,) + a.shape, a.dtype) for a in arrs],
        scratch_shapes=[dma(3 * n), dma(3 * n), dma(3 * n), dma(3 * n), dma(n)],
        input_output_aliases={a: a for a in range(n)} if in_place else {},
        name=name,
    )(*arrs)


def _pair_exchange(grads, name):
    n = len(grads)

    def body(*refs):
        ins, outs = refs[:n], refs[n:2 * n]
        send_sems, recv_sems = refs[2 * n:]
        x, y, c = _place()
        cps = []
        for a in range(n):
            h = ins[a].shape[1] // 2
            theirs = pl.ds(pl.multiple_of((1 - c) * h, 8), h)
            cp = pltpu.make_async_remote_copy(
                src_ref=ins[a].at[:, theirs, :], dst_ref=outs[a], send_sem=send_sems.at[a], recv_sem=recv_sems.at[a],
                device_id=(x, y, 1 - c), device_id_type=MESH)
            cp.start()
            cps.append(cp)
        for cp in cps:
            cp.wait()

    return pl.pallas_call(
        body, in_specs=[ANY] * n, out_specs=[ANY] * n,
        out_shape=[_sds((4, g.shape[1] // 2, g.shape[2]), F32) for g in grads],
        scratch_shapes=[pltpu.SemaphoreType.DMA((n,)), pltpu.SemaphoreType.DMA((n,))], name=name,
    )(*grads)


def _pair_sum(grad, recv, core, name):
    _, r, c = grad.shape
    h = r // 2

    def body(core_ref, g_ref, r_ref, o_ref):
        o_ref[...] = (g_ref[...] + r_ref[...]).astype(BF16)

    return pl.pallas_call(
        body,
        grid_spec=pltpu.PrefetchScalarGridSpec(
            num_scalar_prefetch=1, grid=(4,),
            in_specs=[pl.BlockSpec((None, h, c), lambda s, core_ref: (s, core_ref[0], 0)),
                      pl.BlockSpec((None, h, c), lambda s, core_ref: (s, 0, 0))],
            out_specs=pl.BlockSpec((None, h, c), lambda s, core_ref: (s, 0, 0))),
        out_shape=_sds((4, h, c), BF16), compiler_params=_params(("arbitrary",)), name=name,
    )(core, grad, recv)


def _chip_exchange(parts, swapped, name):
    n = len(parts)

    def body(*refs):
        ins, outs = refs[:n], refs[n:2 * n]
        send_sems, recv_sems = refs[2 * n:]
        x, y, c = _place()
        chips = [(_flip(x, k & 2), _flip(y, k & 1)) for k in (1, 2, 3)]
        cps = []
        for a in range(n):
            for j, (px, py) in enumerate(chips):
                cp = pltpu.make_async_remote_copy(
                    src_ref=ins[a].at[_slot(px, py, swapped[a])], dst_ref=outs[a].at[j],
                    send_sem=send_sems.at[3 * a + j], recv_sem=recv_sems.at[3 * a + j],
                    device_id=(px, py, c), device_id_type=MESH)
                cp.start()
                cps.append(cp)
        for cp in cps:
            cp.wait()

    return pl.pallas_call(
        body, in_specs=[ANY] * n, out_specs=[ANY] * n,
        out_shape=[_sds((3,) + p.shape[1:], BF16) for p in parts],
        scratch_shapes=[pltpu.SemaphoreType.DMA((3 * n,)), pltpu.SemaphoreType.DMA((3 * n,))], name=name,
    )(*parts)


def _chip_sum(part, recv, slot_core, name):
    _, h, c = part.shape

    def body(sc_ref, p_ref, r_ref, o_ref):
        acc = p_ref[...].astype(F32)
        for j in range(3):
            acc = acc + r_ref[j].astype(F32)
        o_ref[...] = acc

    return pl.pallas_call(
        body,
        grid_spec=pltpu.PrefetchScalarGridSpec(
            num_scalar_prefetch=1, grid=(1,),
            in_specs=[pl.BlockSpec((None, h, c), lambda s, sc_ref: (sc_ref[0], 0, 0)),
                      pl.BlockSpec((3, h, c), lambda s, sc_ref: (0, 0, 0))],
            out_specs=pl.BlockSpec((h, c), lambda s, sc_ref: (sc_ref[1], 0))),
        out_shape=_sds((2 * h, c), F32), compiler_params=_params(("arbitrary",)), name=name,
    )(slot_core, part, recv)


def _pair_share(shards, name):
    n = len(shards)

    def body(*refs):
        outs = refs[n:2 * n]
        send_sems, recv_sems = refs[2 * n:]
        x, y, c = _place()
        cps = []
        for a in range(n):
            h = outs[a].shape[0] // 2
            mine = outs[a].at[pl.ds(pl.multiple_of(c * h, 8), h)]
            cp = pltpu.make_async_remote_copy(
                src_ref=mine, dst_ref=mine, send_sem=send_sems.at[a], recv_sem=recv_sems.at[a],
                device_id=(x, y, 1 - c), device_id_type=MESH)
            cp.start()
            cps.append(cp)
        for a, cp in enumerate(cps):
            h = outs[a].shape[0] // 2
            other = outs[a].at[pl.ds(pl.multiple_of((1 - c) * h, 8), h)]
            pltpu.make_async_remote_copy(
                src_ref=other, dst_ref=other, send_sem=send_sems.at[a], recv_sem=recv_sems.at[a],
                device_id=(x, y, 1 - c), device_id_type=MESH).wait_recv()
            cp.wait_send()

    return pl.pallas_call(
        body, in_specs=[ANY] * n, out_specs=[ANY] * n,
        out_shape=[_sds(s.shape, F32) for s in shards],
        scratch_shapes=[pltpu.SemaphoreType.DMA((n,)), pltpu.SemaphoreType.DMA((n,))],
        input_output_aliases={a: a for a in range(n)}, name=name,
    )(*shards)


def _pack(arrs, rows_multiple=8):
    parts, offs, row = [], [], 0
    for a in arrs:
        flat = a.reshape(-1)
        nrow = -(-flat.shape[0] // D)
        parts.append(jnp.pad(flat, (0, nrow * D - flat.shape[0])))
        offs.append(row)
        row += nrow
    total = -(-row // rows_multiple) * rows_multiple
    if total > row:
        parts.append(jnp.zeros(((total - row) * D,), F32))
    return jnp.concatenate(parts).reshape(total, D), offs


def _unpack(packed, offs, shapes):
    out = []
    for off, shp in zip(offs, shapes):
        size = int(np.prod(shp))
        nrow = -(-size // D)
        out.append(packed[off:off + nrow].reshape(-1)[:size].reshape(shp))
    return out


def _to_bf16_slot(w, slot, name):
    r, c = w.shape
    tm = next(cand for cand in (256, 176, 128, 64, 32, 16) if r % cand == 0)

    def body(slot_ref, w_ref, o_ref):
        o_ref[...] = w_ref[...].astype(BF16)

    return pl.pallas_call(
        body,
        grid_spec=pltpu.PrefetchScalarGridSpec(
            num_scalar_prefetch=1, grid=(r // tm,),
            in_specs=[pl.BlockSpec((tm, c), lambda i, slot_ref: (i, 0))],
            out_specs=pl.BlockSpec((None, tm, c), lambda i, slot_ref: (slot_ref[0], i, 0))),
        out_shape=_sds((4, r, c), BF16), compiler_params=_params(("arbitrary",)), name=name,
    )(slot, w)


def _unshard_cols(g):
    s, k, n = g.shape
    return jnp.transpose(g, (1, 0, 2)).reshape(k, s * n)


def _ff_swap(v):
    b = FF_BLOCK
    return jnp.concatenate([v[..., 0:b], v[..., 2 * b:3 * b], v[..., b:2 * b], v[..., 3 * b:4 * b]], axis=-1)


def _local_step(x, target, mod, w, small):
    sh_m, sc_m, gt_m, sh_f, sc_f, gt_f = mod
    t = x.shape[0]
    tmm = min(1024, t)

    h1 = _pre_norm(x, small["g_pre_mix"], sc_m, sh_m, "pre_norm_mix")
    z = _matmul(h1, w["in"], form="nn", out_dtype=F32, tm=tmm, tn=1152, tk=D, bias=small["b_in"], name="mm_in")
    att = _attention(z, small["gen"], "attention")
    a = _matmul(att, w["attn_o"], form="nn", out_dtype=F32, tm=tmm, tn=512, tk=512, name="mm_attn_o")
    u1, u3 = _conv_branch(z, small["w_dw_conv"], small["b_dw_conv"], small["g_conv_ln"], small["b_conv_ln"], "conv_branch")
    cb = _matmul(u3, w["conv_o"], form="nn", out_dtype=F32, tm=tmm, tn=512, tk=512, bias=small["b_conv_o"], name="mm_conv_o")
    y = _gate_merge(a, cb, z, "gate_merge")
    ym = _matmul(y, w["mix_o"], form="nn", out_dtype=F32, tm=tmm, tn=512, tk=D, name="mm_mix_o")
    x1 = _post_res(x, ym, small["g_post_mix"], gt_m, "post_res_mix")
    h2 = _pre_norm(x1, small["g_pre_ffn"], sc_f, sh_f, "pre_norm_ffn")
    up = _matmul(h2, w["up"], form="nn", out_dtype=F32, tm=tmm, tn=FF_BLOCK, tk=D, name="mm_up")
    act = _ffn_act(up, small["w_dw_ffn"], small["b_dw_ffn"], "ffn_act")
    yf = _matmul(act, w["down"], form="nn", out_dtype=F32, tm=tmm, tn=512, tk=D_FF, name="mm_down")
    x2 = _post_res(x1, yf, small["g_post_ffn"], gt_f, "post_res_ffn")

    dx2, loss_cols = _loss_grad(x2, target, "loss_grad")
    dyf, d_g_post_ffn, d_gt_f = _post_res_bwd(dx2, yf, small["g_post_ffn"], gt_f, "post_res_ffn_bwd")
    dact = _matmul(dyf, w["down"], form="nt", out_dtype=F32, tm=tmm, tn=FF_BLOCK, tk=D, name="mm_down_dx")
    g_down = _matmul(act, dyf, form="tn", out_dtype=F32, tm=FF_BLOCK, tn=D, tk=tmm, name="mm_down_dw")
    dup, d_w_dw_ffn, d_b_dw_ffn = _ffn_act_bwd(dact, up, small["w_dw_ffn"], small["b_dw_ffn"], "ffn_act_bwd")
    dh2 = _matmul(dup, w["up"], form="nt", out_dtype=F32, tm=tmm, tn=D, tk=FF_BLOCK, name="mm_up_dx")
    g_up = _matmul(h2, dup, form="tn", out_dtype=F32, tm=D, tn=FF_BLOCK, tk=tmm, out_sharded=True, name="mm_up_dw")
    dx1, d_sh_f, d_sc_f, d_g_pre_ffn = _pre_norm_bwd(dh2, x1, dx2, small["g_pre_ffn"], sc_f, "pre_norm_ffn_bwd")
    dym, d_g_post_mix, d_gt_m = _post_res_bwd(dx1, ym, small["g_post_mix"], gt_m, "post_res_mix_bwd")
    dy = _matmul(dym, w["mix_o"], form="nt", out_dtype=F32, tm=tmm, tn=512, tk=D, name="mm_mix_o_dx")
    g_mix_o = _matmul(y, dym, form="tn", out_dtype=F32, tm=D, tn=512, tk=tmm, name="mm_mix_o_dw")
    da, dcb, dgate_a, dgate_b, d_b_conv_o, sga, sgb = _gate_merge_bwd(dy, a, cb, z, "gate_merge_bwd")
    datt = _matmul(da, w["attn_o"], form="nt", out_dtype=F32, tm=tmm, tn=512, tk=D, name="mm_attn_o_dx")
    g_attn_o = _matmul(att, da, form="tn", out_dtype=F32, tm=512, tn=256, tk=tmm, out_sharded=True, name="mm_attn_o_dw")
    du3 = _matmul(dcb, w["conv_o"], form="nt", out_dtype=F32, tm=tmm, tn=512, tk=D, name="mm_conv_o_dx")
    g_conv_o = _matmul(u3, dcb, form="tn", out_dtype=F32, tm=512, tn=256, tk=tmm, out_sharded=True, name="mm_conv_o_dw")
    dglu, d_w_dw_conv, d_b_dw_conv, d_g_conv_ln, d_b_conv_ln, sglu = _conv_branch_bwd(
        du3, u1, z, small["w_dw_conv"], small["g_conv_ln"], small["b_conv_ln"], "conv_branch_bwd")
    dq, dk, dv, sq, sk, sv, dgen = _attention_bwd(z, datt, small["gen"], "attention_bwd")
    dz = jnp.concatenate([dq, dk, dv, dglu, dgate_a, dgate_b], axis=1)
    d_b_in = jnp.concatenate([sq, sk, sv, sglu, sga, sgb], axis=1)
    dh1 = _matmul(dz, w["in"], form="nt", out_dtype=F32, tm=tmm, tn=D, tk=1152, name="mm_in_dx")
    g_in = _matmul(h1, dz, form="tn", out_dtype=F32, tm=D, tn=1152, tk=tmm, out_sharded=True, name="mm_in_dw")
    grad_x, d_sh_m, d_sc_m, d_g_pre_mix = _pre_norm_bwd(dh1, x, dx1, small["g_pre_mix"], sc_m, "pre_norm_mix_bwd")

    dmod = [d_sh_m, d_sc_m, d_gt_m, d_sh_f, d_sc_f, d_gt_f]
    big = {"in": g_in, "attn_o": g_attn_o, "conv_o": g_conv_o, "mix_o": g_mix_o.reshape(4, 256, D),
           "up": g_up, "down": g_down.reshape(4, D_FF // 4, D)}
    sm = {"g_pre_mix": d_g_pre_mix, "g_post_mix": d_g_post_mix, "b_in": d_b_in, "gen": dgen,
          "w_dw_conv": d_w_dw_conv, "b_dw_conv": d_b_dw_conv, "g_conv_ln": d_g_conv_ln, "b_conv_ln": d_b_conv_ln,
          "b_conv_o": d_b_conv_o, "g_pre_ffn": d_g_pre_ffn, "g_post_ffn": d_g_post_ffn,
          "w_dw_ffn": d_w_dw_ffn, "b_dw_ffn": d_b_dw_ffn}
    return loss_cols, grad_x, dmod, big, sm


BIG = ("in", "attn_o", "conv_o", "mix_o", "up", "down")
SWAPPED = {"in": False, "attn_o": False, "conv_o": False, "mix_o": False, "up": True, "down": False}
SMALL_ORDER = ("b_ada", "g_pre_mix", "g_post_mix", "b_in", "rel_bias", "b_dw_conv", "g_conv_ln", "b_conv_ln",
               "b_conv_o", "g_pre_ffn", "g_post_ffn", "b_dw_ffn", "w_dw_conv", "w_dw_ffn")


def kernel(x, c, w_ada, b_ada, g_pre_mix, g_post_mix, w_in, b_in, rel_bias, w_attn_o, w_dw_conv, b_dw_conv, g_conv_ln, b_conv_ln, w_conv_o, b_conv_o, w_mix_o, g_pre_ffn, g_post_ffn, w_up, w_dw_ffn, b_dw_ffn, w_down, loss_target, m_w_ada, m_b_ada, m_g_pre_mix, m_g_post_mix, m_w_in, m_b_in, m_rel_bias, m_w_attn_o, m_w_dw_conv, m_b_dw_conv, m_g_conv_ln, m_b_conv_ln, m_w_conv_o, m_b_conv_o, m_w_mix_o, m_g_pre_ffn, m_g_post_ffn, m_w_up, m_w_dw_ffn, m_b_dw_ffn, m_w_down, v_w_ada, v_b_ada, v_g_pre_mix, v_g_post_mix, v_w_in, v_b_in, v_rel_bias, v_w_attn_o, v_w_dw_conv, v_b_dw_conv, v_g_conv_ln, v_b_conv_ln, v_w_conv_o, v_b_conv_o, v_w_mix_o, v_g_pre_ffn, v_g_post_ffn, v_w_up, v_w_dw_ffn, v_b_dw_ffn, v_w_down):
    given = dict(locals())
    ax, ay, ac = lax.axis_index("x"), lax.axis_index("y"), lax.axis_index("c")
    shard = 2 * ax + ay
    me = 4 * ax + 2 * ay + ac
    xs, target = x[0], loss_target[0]

    c_pad = jnp.pad(c, ((0, 7), (0, 0)))
    c_g, _ = _allgather8(c_pad, "gather_c")
    c_all = c_g[:, 0, :]
    b_ada_shard = lax.dynamic_slice(b_ada, (0, shard * 1536), (1, 1536))
    mod_shard, c_act = _ada_fwd(c_all, w_ada[0], b_ada_shard, "ada_fwd")
    small_in = [jnp.pad(mod_shard, ((0, 8), (0, 0))),
                jnp.pad(w_dw_conv[0], ((0, 1), (0, 0))),
                jnp.pad(w_dw_ffn[0], ((0, 13), (0, 0)))]
    mod_g, wdc_g, wdf_g = _gather_shards(small_in, [False, False, True], "gather_small")
    mod_all = jnp.transpose(mod_g[:, :8, :], (1, 0, 2)).reshape(8, 6 * D)
    mod_row = lax.dynamic_slice(mod_all, (me, 0), (1, 6 * D))
    mod = [mod_row[:, k * D:(k + 1) * D] for k in range(6)]

    slots = {sw: _slot(ax, ay, sw).astype(jnp.int32).reshape(1) for sw in (False, True)}
    shards16 = [_to_bf16_slot(given["w_" + n][0], slots[SWAPPED[n]], "cast_" + n) for n in BIG]
    gathered = dict(zip(BIG, _gather_shards(shards16, [SWAPPED[n] for n in BIG], "gather_weights", in_place=True)))
    wts = {"in": gathered["in"], "up": gathered["up"],
           "attn_o": _unshard_cols(gathered["attn_o"]), "conv_o": _unshard_cols(gathered["conv_o"]),
           "mix_o": gathered["mix_o"].reshape(D, D), "down": gathered["down"].reshape(D_FF, D)}

    sel = jnp.asarray(_toeplitz_map())
    rel_pad = jnp.pad(rel_bias[0], ((0, 0), (0, REL_PAD - (2 * MAX_REL + 1))))
    gen = _select_call(rel_pad, sel.T.astype(BF16), "bias_rows")
    small = {"g_pre_mix": g_pre_mix, "g_post_mix": g_post_mix, "b_in": b_in, "gen": gen,
             "w_dw_conv": _unshard_cols(wdc_g[:, :CONV_K, :]), "b_dw_conv": b_dw_conv, "g_conv_ln": g_conv_ln,
             "b_conv_ln": b_conv_ln, "b_conv_o": b_conv_o, "g_pre_ffn": g_pre_ffn, "g_post_ffn": g_post_ffn,
             "w_dw_ffn": _unshard_cols(wdf_g[:, :FFN_K, :]), "b_dw_ffn": _ff_swap(b_dw_ffn)}

    loss_cols, grad_x, dmod, big, sm = _local_step(xs, target, mod, wts, small)
    loss = lax.psum(jnp.sum(loss_cols), ("x", "y", "c"))

    d_rel = _select_call(sm["gen"], sel.astype(BF16), "bias_fold")[:, :2 * MAX_REL + 1]
    small_grads = {"g_pre_mix": sm["g_pre_mix"], "g_post_mix": sm["g_post_mix"], "b_in": sm["b_in"], "rel_bias": d_rel[None],
                   "b_dw_conv": sm["b_dw_conv"], "g_conv_ln": sm["g_conv_ln"], "b_conv_ln": sm["b_conv_ln"],
                   "b_conv_o": sm["b_conv_o"], "g_pre_ffn": sm["g_pre_ffn"], "g_post_ffn": sm["g_post_ffn"],
                   "b_dw_ffn": _ff_swap(sm["b_dw_ffn"]), "w_dw_conv": sm["w_dw_conv"], "w_dw_ffn": _ff_swap(sm["w_dw_ffn"])}
    order = [n for n in SMALL_ORDER if n != "b_ada"]
    packed, offs = _pack([jnp.concatenate(dmod, axis=1)] + [small_grads[n] for n in order])
    every, total = _allgather8(packed, "gather_small_grads")
    dmod_all = every[:, 0:6, :].reshape(8, 6 * D)
    full_shapes = {n: given[n].shape for n in order}
    full_shapes["w_dw_conv"], full_shapes["w_dw_ffn"] = (1, CONV_K, 512), (1, FFN_K, 2 * D_FF)
    sums = dict(zip(order, _unpack(total, offs[1:], [full_shapes[n] for n in order])))
    sums["b_ada"] = total[0:6].reshape(1, 6 * D)
    sums["w_dw_conv"] = lax.dynamic_slice(sums["w_dw_conv"], (0, 0, shard * 128), (1, CONV_K, 128))
    sums["w_dw_ffn"] = lax.dynamic_slice(sums["w_dw_ffn"], (0, 0, shard * FF_BLOCK), (1, FFN_K, FF_BLOCK))

    pw, poffs = _pack([given[n] for n in SMALL_ORDER])
    pg, _ = _pack([sums[n] for n in SMALL_ORDER])
    pm, _ = _pack([given["m_" + n] for n in SMALL_ORDER])
    pv, _ = _pack([given["v_" + n] for n in SMALL_ORDER])
    shapes = [given[n].shape for n in SMALL_ORDER]
    upd = [dict(zip(SMALL_ORDER, _unpack(p, poffs, shapes))) for p in _adamw(pw, pg, pm, pv, "adamw_small")]

    dmod_shard = lax.dynamic_slice(dmod_all, (0, shard * 1536), (8, 1536))
    ada = _ada_bwd_adamw(c_act.T, dmod_shard, w_ada[0], m_w_ada[0], v_w_ada[0], "ada_bwd_adamw")

    core = ac.astype(jnp.int32).reshape(1)
    grads = [big[n] for n in BIG]
    from_sibling = _pair_exchange(grads, "pair_exchange")
    parts = [_pair_sum(g, r, core, "pair_sum_" + n) for n, g, r in zip(BIG, grads, from_sibling)]
    from_chips = _chip_exchange(parts, [SWAPPED[n] for n in BIG], "chip_exchange")
    halves = [_chip_sum(p, r, jnp.concatenate([slots[SWAPPED[n]], core]), "chip_sum_" + n)
              for n, p, r in zip(BIG, parts, from_chips)]
    reduced = dict(zip(BIG, _pair_share(halves, "pair_share")))

    out = {"grad_w_ada": ada[0][None], "delta_w_ada": ada[1][None], "new_m_w_ada": ada[2][None], "new_v_w_ada": ada[3][None]}
    for n in BIG:
        g = reduced[n]
        dl, nm, nv = _adamw(given["w_" + n][0], g, given["m_w_" + n][0], given["v_w_" + n][0], "adamw_" + n)
        out["grad_w_" + n], out["delta_w_" + n], out["new_m_w_" + n], out["new_v_w_" + n] = g[None], dl[None], nm[None], nv[None]
    for n in SMALL_ORDER:
        out["grad_" + n], out["delta_" + n], out["new_m_" + n], out["new_v_" + n] = sums[n], upd[0][n], upd[1][n], upd[2][n]

    weights = ["w_ada", "b_ada", "g_pre_mix", "g_post_mix", "w_in", "b_in", "rel_bias", "w_attn_o", "w_dw_conv", "b_dw_conv",
               "g_conv_ln", "b_conv_ln", "w_conv_o", "b_conv_o", "w_mix_o", "g_pre_ffn", "g_post_ffn", "w_up", "w_dw_ffn",
               "b_dw_ffn", "w_down"]
    return (loss, grad_x[None], *[out["grad_" + n] for n in weights], *[out["delta_" + n] for n in weights],
            *[out["new_m_" + n] for n in weights], *[out["new_v_" + n] for n in weights])
```

```python
import functools
import math

import numpy as np
import jax
import jax.numpy as jnp
from jax import lax
from jax.experimental import pallas as pl
from jax.experimental.pallas import tpu as pltpu

F32, BF16 = jnp.float32, jnp.bfloat16
MESH = pl.DeviceIdType.MESH

D = 1024
D_IN = 4608
D_FF = 2816
CONV_K = 31
FFN_K = 3
N_HEADS = 8
CHUNK = 64
LEFT_CHUNKS = 8
MAX_REL = 128
EPS = 1e-6
NEG_INF = -1e30
Q_TILE = 256
WINDOW = Q_TILE + LEFT_CHUNKS * CHUNK
REL_PAD = 384
TOEP = 1024
ROW_TILE = 256
VMEM_LIMIT = 60 * 1024 * 1024

ADAM_LR, ADAM_B1, ADAM_B2, ADAM_EPS, ADAM_WD, ADAM_STEP = 0.001, 0.9, 0.999, 1e-08, 0.01, 10


def _params(sem=None):
    return pltpu.CompilerParams(dimension_semantics=sem, vmem_limit_bytes=VMEM_LIMIT)


def _sds(shape, dtype):
    return jax.ShapeDtypeStruct(tuple(shape), dtype)


ANY = pl.BlockSpec(memory_space=pl.ANY)


class _Carried:
    def __init__(self, ins, out_shapes, aliases, n_sems, start, finish):
        self.ins, self.out_shapes, self.aliases = list(ins), list(out_shapes), dict(aliases)
        self.n_sems, self.start, self.finish = n_sems, start, finish


def _call(body, *, grid, in_specs, out_specs, out_shape, scratch_shapes, sem, name, args, carried=None):
    in_specs, out_specs, out_shape = list(in_specs), list(out_specs), list(out_shape)
    scratch_shapes = list(scratch_shapes)
    if carried is None:
        return pl.pallas_call(body, grid=grid, in_specs=in_specs, out_specs=out_specs, out_shape=out_shape,
                              scratch_shapes=scratch_shapes, compiler_params=_params(sem), name=name)(*args)
    n_in, n_out, n_scr = len(in_specs), len(out_specs), len(scratch_shapes)
    c_in, c_out = len(carried.ins), len(carried.out_shapes)

    def full(*refs):
        pos = [0]

        def take(k):
            part = refs[pos[0]:pos[0] + k]
            pos[0] += k
            return part

        ins, cins, outs, couts, scr = take(n_in), take(c_in), take(n_out), take(c_out), take(n_scr)
        send_sems, recv_sems = take(2)
        first = last = None
        for d, size in enumerate(grid):
            pid = pl.program_id(d)
            first = (pid == 0) if first is None else first & (pid == 0)
            last = (pid == size - 1) if last is None else last & (pid == size - 1)

        @pl.when(first)
        def _():
            carried.start(cins, couts, send_sems, recv_sems)

        body(*ins, *outs, *scr)

        @pl.when(last)
        def _():
            carried.finish(cins, couts, send_sems, recv_sems)

    sems = [pltpu.SemaphoreType.DMA((carried.n_sems,)), pltpu.SemaphoreType.DMA((carried.n_sems,))]
    return pl.pallas_call(
        full, grid=grid, in_specs=in_specs + [ANY] * c_in, out_specs=out_specs + [ANY] * c_out,
        out_shape=out_shape + carried.out_shapes, scratch_shapes=scratch_shapes + sems,
        input_output_aliases={n_in + k: n_out + v for k, v in carried.aliases.items()},
        compiler_params=_params(tuple("arbitrary" for _ in grid)), name=name,
    )(*args, *carried.ins)


def _run_carried(carried, name):
    c_in = len(carried.ins)

    def body(*refs):
        cins, couts = refs[:c_in], refs[c_in:c_in + len(carried.out_shapes)]
        send_sems, recv_sems = refs[-2:]
        carried.start(cins, couts, send_sems, recv_sems)
        carried.finish(cins, couts, send_sems, recv_sems)

    return pl.pallas_call(
        body, in_specs=[ANY] * c_in, out_specs=[ANY] * len(carried.out_shapes), out_shape=carried.out_shapes,
        scratch_shapes=[pltpu.SemaphoreType.DMA((carried.n_sems,)), pltpu.SemaphoreType.DMA((carried.n_sems,))],
        input_output_aliases=carried.aliases, name=name,
    )(*carried.ins)


def _matmul(a, b, *, form, out_dtype, tm, tn, tk, name, bias=None, add=None, out_sharded=False, carried=None):
    b3 = b.ndim == 3
    if form == "nn":
        m, k = a.shape
        n = b.shape[0] * b.shape[2] if b3 else b.shape[1]
        dn = (((1,), (0,)), ((), ()))
        a_spec = pl.BlockSpec((tm, tk), lambda i, j, kk: (i, kk))
        b_spec = (pl.BlockSpec((None, tk, tn), lambda i, j, kk: (j, kk, 0)) if b3
                  else pl.BlockSpec((tk, tn), lambda i, j, kk: (kk, j)))
    elif form == "nt":
        m, k = a.shape
        n = b.shape[1] if b3 else b.shape[0]
        dn = (((1,), (1,)), ((), ()))
        a_spec = pl.BlockSpec((tm, tk), lambda i, j, kk: (i, kk))
        b_spec = (pl.BlockSpec((None, tn, tk), lambda i, j, kk: (kk, j, 0)) if b3
                  else pl.BlockSpec((tn, tk), lambda i, j, kk: (j, kk)))
    else:
        k, m = a.shape
        n = b.shape[1]
        dn = (((0,), (0,)), ((), ()))
        a_spec = pl.BlockSpec((tk, tm), lambda i, j, kk: (kk, i))
        b_spec = pl.BlockSpec((tk, tn), lambda i, j, kk: (kk, j))
    assert m % tm == 0 and n % tn == 0 and k % tk == 0, (name, m, n, k, tm, tn, tk)
    nk = k // tk
    in_specs, args = [a_spec, b_spec], [a, b]
    if bias is not None:
        in_specs.append(pl.BlockSpec((1, tn), lambda i, j, kk: (0, j)))
        args.append(bias)
    if add is not None:
        in_specs.append(pl.BlockSpec((tm, tn), lambda i, j, kk: (i, j)))
        args.append(add)
    if out_sharded:
        out_shape = _sds((n // tn, m, tn), out_dtype)
        out_spec = pl.BlockSpec((None, tm, tn), lambda i, j, kk: (j, i, 0))
    else:
        out_shape = _sds((m, n), out_dtype)
        out_spec = pl.BlockSpec((tm, tn), lambda i, j, kk: (i, j))

    def body(*refs):
        a_ref, b_ref = refs[0], refs[1]
        pos = 2
        bias_ref = add_ref = None
        if bias is not None:
            bias_ref, pos = refs[pos], pos + 1
        if add is not None:
            add_ref, pos = refs[pos], pos + 1
        o_ref = refs[pos]
        av, bv = a_ref[...], b_ref[...]
        if av.dtype != BF16:
            av = av.astype(BF16)
        if bv.dtype != BF16:
            bv = bv.astype(BF16)
        p = lax.dot_general(av, bv, dn, preferred_element_type=F32)

        def finish(acc):
            if bias_ref is not None:
                acc = acc + bias_ref[...]
            if add_ref is not None:
                acc = acc + add_ref[...]
            o_ref[...] = acc.astype(o_ref.dtype)

        if nk == 1:
            finish(p)
        else:
            acc_ref = refs[pos + 1]
            kk = pl.program_id(2)

            @pl.when(kk == 0)
            def _():
                acc_ref[...] = p

            @pl.when(kk > 0)
            def _():
                acc_ref[...] += p

            @pl.when(kk == nk - 1)
            def _():
                finish(acc_ref[...])

    res = _call(body, grid=(m // tm, n // tn, nk), in_specs=in_specs, out_specs=[out_spec], out_shape=[out_shape],
                scratch_shapes=[pltpu.VMEM((tm, tn), F32)] if nk > 1 else [],
                sem=("parallel", "parallel", "arbitrary"), name=name, args=args, carried=carried)
    return res[0] if carried is None else (res[0], res[1:])


def _rowcall(fn, rows, consts, row_outs, acc_outs, *, name, tm=ROW_TILE, col_grid=1):
    n_rows = rows[0][0].shape[0]
    assert n_rows % tm == 0
    grid = (col_grid, n_rows // tm)
    in_specs = [pl.BlockSpec((tm, w), functools.partial(lambda c, i, cb: (i, cb + c), cb=cb)) for _, w, cb in rows]
    in_specs += [pl.BlockSpec(k.shape, functools.partial(lambda c, i, nd: (0,) * nd, nd=k.ndim)) for k in consts]
    out_specs = [pl.BlockSpec((tm, w), lambda c, i: (i, c)) for _, _, _, w in row_outs]
    out_specs += [pl.BlockSpec((r, w), lambda c, i: (0, c)) for r, _, w in acc_outs]
    out_shape = [_sds((nr, nc), dt) for nr, nc, dt, _ in row_outs] + [_sds((r, nc), F32) for r, nc, _ in acc_outs]
    n_in, n_ro = len(rows) + len(consts), len(row_outs)

    def body(*refs):
        res = fn(*[r[...] for r in refs[:n_in]])
        if not isinstance(res, (tuple, list)):
            res = (res,)
        outs = refs[n_in:]
        for o_ref, val in zip(outs[:n_ro], res[:n_ro]):
            o_ref[...] = val.astype(o_ref.dtype)
        if acc_outs:
            first = pl.program_id(1) == 0

            @pl.when(first)
            def _():
                for o_ref, val in zip(outs[n_ro:], res[n_ro:]):
                    o_ref[...] = val

            @pl.when(jnp.logical_not(first))
            def _():
                for o_ref, val in zip(outs[n_ro:], res[n_ro:]):
                    o_ref[...] += val

    out = pl.pallas_call(
        body, grid=grid, in_specs=in_specs, out_specs=out_specs, out_shape=out_shape,
        compiler_params=_params(("arbitrary", "arbitrary")), name=name,
    )(*[r[0] for r in rows], *consts)
    return out


def _colsum(v):
    return jnp.sum(v, axis=0, keepdims=True)


def _sigmoid(v):
    return 1.0 / (1.0 + jnp.exp(-v))


_GELU_C = math.sqrt(2.0 / math.pi)


def _gelu(v):
    return 0.5 * v * (1.0 + jnp.tanh(_GELU_C * (v + 0.044715 * (v * v * v))))


def _gelu_and_grad(v):
    th = jnp.tanh(_GELU_C * (v + 0.044715 * (v * v * v)))
    g = 0.5 * v * (1.0 + th)
    dg = 0.5 * (1.0 + th) + 0.5 * v * (1.0 - th * th) * (_GELU_C * (1.0 + 3.0 * 0.044715 * (v * v)))
    return g, dg


def _rms_stats(v):
    r = lax.rsqrt(jnp.mean(v * v, axis=-1, keepdims=True) + EPS)
    return v * r, r


def _rms_bwd(dn, vn, r):
    return r * (dn - vn * jnp.mean(dn * vn, axis=-1, keepdims=True))


def _pre_norm(x, g, sc, sh, name):
    def fn(xv, gv, scv, shv):
        xn, _ = _rms_stats(xv)
        return (xn * gv) * (1.0 + scv) + shv
    return _rowcall(fn, [(x, D, 0)], [g, sc, sh], [(x.shape[0], D, BF16, D)], [], name=name)[0]


def _pre_norm_bwd(dh, x, dx_other, g, sc, name):
    def fn(dhv, xv, dov, gv, scv):
        xn, r = _rms_stats(xv)
        yn = xn * gv
        dyn = dhv * (1.0 + scv)
        dx = _rms_bwd(dyn * gv, xn, r)
        return dov + dx, _colsum(dhv), _colsum(dhv * yn), _colsum(dyn * xn)
    t = x.shape[0]
    return _rowcall(fn, [(dh, D, 0), (x, D, 0), (dx_other, D, 0)], [g, sc], [(t, D, F32, D)],
                    [(1, D, D)] * 3, name=name)


def _post_res(x, ypre, g, gt, name):
    def fn(xv, yv, gv, gtv):
        yn, _ = _rms_stats(yv)
        return xv + gtv * (yn * gv)
    return _rowcall(fn, [(x, D, 0), (ypre, D, 0)], [g, gt], [(x.shape[0], D, F32, D)], [], name=name)[0]


def _post_res_bwd(dxo, ypre, g, gt, name):
    def fn(dv, yv, gv, gtv):
        yn, r = _rms_stats(yv)
        dyn = dv * gtv
        dy = _rms_bwd(dyn * gv, yn, r)
        return dy, _colsum(dyn * yn), _colsum(dv * (yn * gv))
    t = ypre.shape[0]
    return _rowcall(fn, [(dxo, D, 0), (ypre, D, 0)], [g, gt], [(t, D, BF16, D)], [(1, D, D)] * 2, name=name)


def _ffn_tail(x1, yf, target, g, gt, name):
    def fn(xv, yv, tv, gv, gtv):
        yn, r = _rms_stats(yv)
        e = xv + gtv * (yn * gv) - tv
        dx2 = e * (1.0 / D)
        dyn = dx2 * gtv
        dy = _rms_bwd(dyn * gv, yn, r)
        return dx2, dy, _colsum(e * e) * (0.5 / D), _colsum(dyn * yn), _colsum(dx2 * (yn * gv))
    t = x1.shape[0]
    return _rowcall(fn, [(x1, D, 0), (yf, D, 0), (target, D, 0)], [g, gt], [(t, D, F32, D), (t, D, BF16, D)],
                    [(1, D, D)] * 3, name=name)


def _gate_merge(a, cb, z, name):
    def fn(av, cv, gav, gbv):
        return _sigmoid(gav) * av + _sigmoid(gbv) * cv
    t = a.shape[0]
    return _rowcall(fn, [(a, 512, 0), (cb, 512, 0), (z, 512, 5), (z, 512, 7)], [],
                    [(t, D, BF16, 512)], [], name=name, col_grid=2)[0]


def _gate_merge_bwd(dy, a, cb, z, name):
    def fn(dv, av, cv, gav, gbv):
        sa, sb = _sigmoid(gav), _sigmoid(gbv)
        dcb = dv * sb
        dga = dv * av * (sa * (1.0 - sa))
        dgb = dv * cv * (sb * (1.0 - sb))
        return dv * sa, dcb, dga, dgb, _colsum(dcb), _colsum(dga), _colsum(dgb)
    t = a.shape[0]
    return _rowcall(fn, [(dy, 512, 0), (a, 512, 0), (cb, 512, 0), (z, 512, 5), (z, 512, 7)], [],
                    [(t, D, BF16, 512)] * 4, [(1, D, 512)] * 3, name=name, col_grid=2)


CONV_HALO = 32


def _layer_norm_parts(u):
    mu = jnp.mean(u, axis=-1, keepdims=True)
    d = u - mu
    r = lax.rsqrt(jnp.mean(d * d, axis=-1, keepdims=True) + EPS)
    return d * r, r


SUBLANES = 8
ROT_ROWS = 24


def _shifted_copies(src, rot, tm):
    for r in range(1, SUBLANES):
        rot[r - 1] = src[pl.ds(r, tm + ROT_ROWS), :]


def _tap(src, rot, offset, tm):
    q, r = divmod(offset, SUBLANES)
    if r == 0:
        return src[pl.ds(SUBLANES * q, tm), :]
    return rot[r - 1, pl.ds(SUBLANES * q, tm), :]


def _conv_branch(z, w_dw, b_dw, g_ln, b_ln, name, tm=ROW_TILE):
    t = z.shape[0]
    per = tm // CONV_HALO

    def body(ga_ref, gb_ref, gah_ref, gbh_ref, w_ref, b_ref, g_ref, bl_ref, u1_ref, u3_ref, scr, rot):
        i = pl.program_id(0)
        u0h = gah_ref[...] * _sigmoid(gbh_ref[...])
        scr[0:CONV_HALO, :] = jnp.where(i > 0, u0h, 0.0)
        scr[CONV_HALO:CONV_HALO + tm, :] = ga_ref[...] * _sigmoid(gb_ref[...])
        _shifted_copies(scr, rot, tm)
        acc = jnp.zeros((tm, 512), F32) + b_ref[...]
        for j in range(CONV_K):
            acc = acc + w_ref[j:j + 1, :] * _tap(scr, rot, CONV_HALO - (CONV_K - 1) + j, tm)
        u1_ref[...] = acc
        xh, _ = _layer_norm_parts(acc)
        u2 = xh * g_ref[...] + bl_ref[...]
        u3_ref[...] = (u2 * _sigmoid(u2)).astype(BF16)

    cur = lambda cb: pl.BlockSpec((tm, 512), lambda i: (i, cb))
    halo = lambda cb: pl.BlockSpec((CONV_HALO, 512), lambda i: (jnp.maximum(i * per - 1, 0), cb))
    whole = lambda a: pl.BlockSpec(a.shape, lambda i: (0, 0))
    return pl.pallas_call(
        body, grid=(t // tm,),
        in_specs=[cur(3), cur(4), halo(3), halo(4), whole(w_dw), whole(b_dw), whole(g_ln), whole(b_ln)],
        out_specs=[pl.BlockSpec((tm, 512), lambda i: (i, 0))] * 2,
        out_shape=[_sds((t, 512), F32), _sds((t, 512), BF16)],
        scratch_shapes=[pltpu.VMEM((CONV_HALO + tm, 512), F32), pltpu.VMEM((SUBLANES - 1, tm + ROT_ROWS, 512), F32)],
        compiler_params=_params(("arbitrary",)), name=name,
    )(z, z, z, z, w_dw, b_dw, g_ln, b_ln)


def _conv_branch_bwd(du3, u1, z, w_dw, g_ln, b_ln, name, tm=ROW_TILE):
    t = z.shape[0]
    per = tm // CONV_HALO
    last = t // tm - 1

    def du1_of(du3v, u1v, g, b):
        xh, r = _layer_norm_parts(u1v)
        u2 = xh * g + b
        s = _sigmoid(u2)
        du2 = du3v * (s * (1.0 + u2 * (1.0 - s)))
        dxh = du2 * g
        du1 = r * (dxh - jnp.mean(dxh, axis=-1, keepdims=True) - xh * jnp.mean(dxh * xh, axis=-1, keepdims=True))
        return du1, du2, xh

    def body(d_ref, u_ref, dn_ref, un_ref, ga_ref, gb_ref, gah_ref, gbh_ref, w_ref, g_ref, bl_ref,
             dglu_ref, dw_ref, dbdw_ref, dg_ref, dbl_ref, dbin_ref, scr, scd, rot, rotd):
        i = pl.program_id(0)
        g, b = g_ref[...], bl_ref[...]
        du1, du2, xh = du1_of(d_ref[...], u_ref[...], g, b)
        du1n, _, _ = du1_of(dn_ref[...], un_ref[...], g, b)
        scd[0:tm, :] = du1
        scd[tm:tm + CONV_HALO, :] = jnp.where(i < last, du1n, 0.0)
        sgb = _sigmoid(gb_ref[...])
        ga = ga_ref[...]
        scr[0:CONV_HALO, :] = jnp.where(i > 0, gah_ref[...] * _sigmoid(gbh_ref[...]), 0.0)
        scr[CONV_HALO:CONV_HALO + tm, :] = ga * sgb
        _shifted_copies(scr, rot, tm)
        _shifted_copies(scd, rotd, tm)
        first = i == 0

        @pl.when(first)
        def _():
            dw_ref[...] = jnp.zeros_like(dw_ref)

        du0 = jnp.zeros((tm, 512), F32)
        for j in range(CONV_K):
            du0 = du0 + w_ref[j:j + 1, :] * _tap(scd, rotd, CONV_K - 1 - j, tm)
            dw_ref[j:j + 1, :] += _colsum(du1 * _tap(scr, rot, CONV_HALO - (CONV_K - 1) + j, tm))
        dga = du0 * sgb
        dgb = du0 * ga * (sgb * (1.0 - sgb))
        dglu_ref[:, 0:512] = dga.astype(BF16)
        dglu_ref[:, 512:1024] = dgb.astype(BF16)
        parts = (_colsum(du1), _colsum(du2 * xh), _colsum(du2))

        @pl.when(first)
        def _():
            dbdw_ref[...], dg_ref[...], dbl_ref[...] = parts
            dbin_ref[:, 0:512] = _colsum(dga)
            dbin_ref[:, 512:1024] = _colsum(dgb)

        @pl.when(jnp.logical_not(first))
        def _():
            dbdw_ref[...] += parts[0]
            dg_ref[...] += parts[1]
            dbl_ref[...] += parts[2]
            dbin_ref[:, 0:512] += _colsum(dga)
            dbin_ref[:, 512:1024] += _colsum(dgb)

    cur = lambda cb: pl.BlockSpec((tm, 512), lambda i: (i, cb))
    prev = lambda cb: pl.BlockSpec((CONV_HALO, 512), lambda i: (jnp.maximum(i * per - 1, 0), cb))
    nxt = pl.BlockSpec((CONV_HALO, 512), lambda i: (jnp.minimum((i + 1) * per, t // CONV_HALO - 1), 0))
    whole = lambda a: pl.BlockSpec(a.shape, lambda i: (0, 0))
    acc = lambda r, w: pl.BlockSpec((r, w), lambda i: (0, 0))
    return pl.pallas_call(
        body, grid=(t // tm,),
        in_specs=[cur(0), cur(0), nxt, nxt, cur(3), cur(4), prev(3), prev(4), whole(w_dw), whole(g_ln), whole(b_ln)],
        out_specs=[pl.BlockSpec((tm, 1024), lambda i: (i, 0)), acc(CONV_K, 512), acc(1, 512), acc(1, 512),
                   acc(1, 512), acc(1, 1024)],
        out_shape=[_sds((t, 1024), BF16), _sds((CONV_K, 512), F32), _sds((1, 512), F32), _sds((1, 512), F32),
                   _sds((1, 512), F32), _sds((1, 1024), F32)],
        scratch_shapes=[pltpu.VMEM((CONV_HALO + tm, 512), F32), pltpu.VMEM((tm + CONV_HALO, 512), F32),
                        pltpu.VMEM((SUBLANES - 1, tm + ROT_ROWS, 512), F32),
                        pltpu.VMEM((SUBLANES - 1, tm + ROT_ROWS, 512), F32)],
        compiler_params=_params(("arbitrary",)), name=name,
    )(du3, u1, du3, u1, z, z, z, z, w_dw, g_ln, b_ln)


FF_BLOCK = D_FF // 2
FF_HALO = 8
LANE_CHUNK = 128


def _ffn_act(up, w3, b3, name, tm=ROW_TILE):
    t = up.shape[0]
    per = tm // FF_HALO
    wide = 2 * FF_BLOCK

    def body(u_ref, uh_ref, w_ref, b_ref, o_ref, scr):
        i = pl.program_id(1)
        scr[0:FF_HALO, :] = jnp.where(i > 0, uh_ref[...], 0.0)
        scr[FF_HALO:FF_HALO + tm, :] = u_ref[...]
        for cc in range(FF_BLOCK // LANE_CHUNK):
            vs = slice(cc * LANE_CHUNK, (cc + 1) * LANE_CHUNK)
            gs = slice(FF_BLOCK + cc * LANE_CHUNK, FF_BLOCK + (cc + 1) * LANE_CHUNK)

            def conv(cs):
                acc = b_ref[:, cs] + w_ref[0:1, cs] * scr[pl.ds(FF_HALO - 2, tm), cs]
                acc = acc + w_ref[1:2, cs] * scr[pl.ds(FF_HALO - 1, tm), cs]
                return acc + w_ref[2:3, cs] * scr[pl.ds(FF_HALO, tm), cs]

            o_ref[:, vs] = (_gelu(conv(gs)) * conv(vs)).astype(BF16)

    return pl.pallas_call(
        body, grid=(2, t // tm),
        in_specs=[pl.BlockSpec((tm, wide), lambda c, i: (i, c)),
                  pl.BlockSpec((FF_HALO, wide), lambda c, i: (jnp.maximum(i * per - 1, 0), c)),
                  pl.BlockSpec((FFN_K, wide), lambda c, i: (0, c)),
                  pl.BlockSpec((1, wide), lambda c, i: (0, c))],
        out_specs=pl.BlockSpec((tm, FF_BLOCK), lambda c, i: (i, c)),
        out_shape=_sds((t, D_FF), BF16),
        scratch_shapes=[pltpu.VMEM((FF_HALO + tm, wide), F32)],
        compiler_params=_params(("arbitrary", "arbitrary")), name=name,
    )(up, up, w3, b3)


def _ffn_act_bwd(dact, up, w3, b3, name, tm=ROW_TILE):
    t = up.shape[0]
    per = tm // FF_HALO
    wide = 2 * FF_BLOCK
    last = t // tm - 1
    ext = tm + FF_HALO

    def body(u_ref, up_ref, un_ref, d_ref, dn_ref, w_ref, b_ref, o_ref, dw_ref, db_ref, scr, scd):
        i = pl.program_id(1)
        first = i == 0
        scr[0:FF_HALO, :] = jnp.where(i > 0, up_ref[...], 0.0)
        scr[FF_HALO:FF_HALO + tm, :] = u_ref[...]
        scr[FF_HALO + tm:FF_HALO + ext, :] = un_ref[...]
        dn = jnp.where(i < last, dn_ref[...], 0.0)

        @pl.when(first)
        def _():
            dw_ref[...] = jnp.zeros_like(dw_ref)
            db_ref[...] = jnp.zeros_like(db_ref)

        for cc in range(FF_BLOCK // LANE_CHUNK):
            vs = slice(cc * LANE_CHUNK, (cc + 1) * LANE_CHUNK)
            gs = slice(FF_BLOCK + cc * LANE_CHUNK, FF_BLOCK + (cc + 1) * LANE_CHUNK)

            def conv(cs):
                acc = b_ref[:, cs] + w_ref[0:1, cs] * scr[pl.ds(FF_HALO - 2, ext), cs]
                acc = acc + w_ref[1:2, cs] * scr[pl.ds(FF_HALO - 1, ext), cs]
                return acc + w_ref[2:3, cs] * scr[pl.ds(FF_HALO, ext), cs]

            val = conv(vs)
            gel, dgel = _gelu_and_grad(conv(gs))
            da = jnp.concatenate([d_ref[:, vs], dn[:, vs]], axis=0)
            scd[:, vs] = da * gel
            scd[:, gs] = da * val * dgel
            for cs in (vs, gs):
                shifted = [scd[pl.ds(FFN_K - 1 - j, tm), cs] for j in range(FFN_K)]
                ucur = scr[pl.ds(FF_HALO, tm), cs]
                o_ref[:, cs] = (w_ref[0:1, cs] * shifted[0] + w_ref[1:2, cs] * shifted[1]
                                + w_ref[2:3, cs] * shifted[2]).astype(BF16)
                for j in range(FFN_K):
                    dw_ref[j:j + 1, cs] += _colsum(shifted[j] * ucur)
                db_ref[:, cs] += _colsum(shifted[FFN_K - 1])

    nblk = t // FF_HALO
    return pl.pallas_call(
        body, grid=(2, t // tm),
        in_specs=[pl.BlockSpec((tm, wide), lambda c, i: (i, c)),
                  pl.BlockSpec((FF_HALO, wide), lambda c, i: (jnp.maximum(i * per - 1, 0), c)),
                  pl.BlockSpec((FF_HALO, wide), lambda c, i: (jnp.minimum((i + 1) * per, nblk - 1), c)),
                  pl.BlockSpec((tm, FF_BLOCK), lambda c, i: (i, c)),
                  pl.BlockSpec((FF_HALO, FF_BLOCK), lambda c, i: (jnp.minimum((i + 1) * per, nblk - 1), c)),
                  pl.BlockSpec((FFN_K, wide), lambda c, i: (0, c)),
                  pl.BlockSpec((1, wide), lambda c, i: (0, c))],
        out_specs=[pl.BlockSpec((tm, wide), lambda c, i: (i, c)),
                   pl.BlockSpec((FFN_K, wide), lambda c, i: (0, c)),
                   pl.BlockSpec((1, wide), lambda c, i: (0, c))],
        out_shape=[_sds((t, 2 * D_FF), BF16), _sds((FFN_K, 2 * D_FF), F32), _sds((1, 2 * D_FF), F32)],
        scratch_shapes=[pltpu.VMEM((FF_HALO + ext, wide), F32), pltpu.VMEM((ext, wide), F32)],
        compiler_params=_params(("arbitrary", "arbitrary")), name=name,
    )(up, up, up, dact, dact, w3, b3)


def _toeplitz_map():
    f = np.zeros((TOEP, REL_PAD), np.float32)
    for m in range(TOEP - 1):
        rel = (WINDOW - 1) - m
        f[m, int(np.clip(rel, -MAX_REL, MAX_REL)) + MAX_REL] = 1.0
    return f


def _split3(v):
    hi = v.astype(BF16)
    r1 = v - hi.astype(F32)
    mid = r1.astype(BF16)
    lo = (r1 - mid.astype(F32)).astype(BF16)
    return hi, mid, lo


def _exact_select(v, sel):
    out = None
    for part in _split3(v):
        p = jnp.dot(part, sel, preferred_element_type=F32)
        out = p if out is None else out + p
    return out


def _select_call(v, sel, name):
    def body(v_ref, s_ref, o_ref):
        o_ref[...] = _exact_select(v_ref[...], s_ref[...])
    return pl.pallas_call(body, out_shape=_sds((v.shape[0], sel.shape[1]), F32), name=name)(v, sel)


def _band_bias(gen_row):
    b0 = jnp.broadcast_to(gen_row, (Q_TILE, TOEP))
    bias = pltpu.roll(b0, TOEP - 255, 1, stride=1, stride_axis=0)[:, :WINDOW]
    qq = lax.broadcasted_iota(jnp.int32, (Q_TILE, WINDOW), 0) // CHUNK
    kc = lax.broadcasted_iota(jnp.int32, (Q_TILE, WINDOW), 1) // CHUNK
    return jnp.where((kc >= qq) & (kc <= qq + LEFT_CHUNKS), bias, NEG_INF)


PAD_ROWS = WINDOW - Q_TILE
NT_DIMS = (((1,), (1,)), ((), ()))
TN_DIMS = (((0,), (0,)), ((), ()))


def _head_mask(hh):
    lane = lax.broadcasted_iota(jnp.int32, (1, 128), 1)
    return (lane < 64) if hh == 0 else (lane >= 64)


def _probs(qm, kw, bias, i):
    s = lax.dot_general(qm, kw, NT_DIMS, preferred_element_type=F32) + bias
    col = lax.broadcasted_iota(jnp.int32, (Q_TILE, WINDOW), 1)
    s = jnp.where(col >= PAD_ROWS - Q_TILE * i, s, NEG_INF)
    p = jnp.exp(s - jnp.max(s, axis=-1, keepdims=True))
    return p / jnp.sum(p, axis=-1, keepdims=True)


def _attention(z, gen, name, carried=None):
    t = z.shape[0]
    n_i = t // Q_TILE

    def body(q_ref, k_ref, v_ref, g_ref, o_ref, kpad, vpad, bias):
        hp, i = pl.program_id(0), pl.program_id(1)

        @pl.when(i == 0)
        def _():
            kpad[0:PAD_ROWS, :] = jnp.zeros((PAD_ROWS, 128), BF16)
            vpad[0:PAD_ROWS, :] = jnp.zeros((PAD_ROWS, 128), BF16)
            kpad[PAD_ROWS:PAD_ROWS + t, :] = k_ref[...].astype(BF16)
            vpad[PAD_ROWS:PAD_ROWS + t, :] = v_ref[...].astype(BF16)
            for hh in range(2):
                bias[hh] = _band_bias(g_ref[pl.ds(2 * hp + hh, 1), :])

        start = pl.multiple_of(i * Q_TILE, Q_TILE)
        kw = kpad[pl.ds(start, WINDOW), :]
        vw = vpad[pl.ds(start, WINDOW), :]
        q = q_ref[...] * (CHUNK ** -0.5)
        out = None
        for hh in range(2):
            mask = _head_mask(hh)
            p = _probs(jnp.where(mask, q, 0.0).astype(BF16), kw, bias[hh], i)
            o = jnp.dot(p.astype(BF16), vw, preferred_element_type=F32)
            out = jnp.where(mask, o, 0.0) if out is None else jnp.where(mask, o, out)
        o_ref[...] = out.astype(BF16)

    res = _call(
        body, grid=(4, n_i),
        in_specs=[pl.BlockSpec((Q_TILE, 128), lambda h, i: (i, h)),
                  pl.BlockSpec((t, 128), lambda h, i: (0, 4 + h)),
                  pl.BlockSpec((t, 128), lambda h, i: (0, 8 + h)),
                  pl.BlockSpec((N_HEADS, TOEP), lambda h, i: (0, 0))],
        out_specs=[pl.BlockSpec((Q_TILE, 128), lambda h, i: (i, h))],
        out_shape=[_sds((t, 512), BF16)],
        scratch_shapes=[pltpu.VMEM((PAD_ROWS + t, 128), BF16), pltpu.VMEM((PAD_ROWS + t, 128), BF16),
                        pltpu.VMEM((2, Q_TILE, WINDOW), F32)],
        sem=("arbitrary", "arbitrary"), name=name, args=(z, z, z, gen), carried=carried)
    return res[0] if carried is None else (res[0], res[1:])


def _attention_bwd(z, datt, gen, name, carried=None):
    t = z.shape[0]
    n_i = t // Q_TILE

    def body(q_ref, k_ref, v_ref, d_ref, g_ref, dq_ref, dk_ref, dv_ref, sq_ref, sk_ref, sv_ref, dg_ref,
             kpad, vpad, dkacc, dvacc, bias, dsacc):
        hp, i = pl.program_id(0), pl.program_id(1)

        @pl.when(i == 0)
        def _():
            kpad[0:PAD_ROWS, :] = jnp.zeros((PAD_ROWS, 128), BF16)
            vpad[0:PAD_ROWS, :] = jnp.zeros((PAD_ROWS, 128), BF16)
            kpad[PAD_ROWS:PAD_ROWS + t, :] = k_ref[...].astype(BF16)
            vpad[PAD_ROWS:PAD_ROWS + t, :] = v_ref[...].astype(BF16)
            dkacc[...] = jnp.zeros_like(dkacc)
            dvacc[...] = jnp.zeros_like(dvacc)
            dsacc[...] = jnp.zeros_like(dsacc)
            for hh in range(2):
                bias[hh] = _band_bias(g_ref[pl.ds(2 * hp + hh, 1), :])

        start = pl.multiple_of(i * Q_TILE, Q_TILE)
        win = pl.ds(start, WINDOW)
        kw = kpad[win, :]
        vw = vpad[win, :]
        q = q_ref[...] * (CHUNK ** -0.5)
        do = d_ref[...]
        dq = None
        for hh in range(2):
            mask = _head_mask(hh)
            qm = jnp.where(mask, q, 0.0).astype(BF16)
            dom = jnp.where(mask, do, 0.0).astype(BF16)
            p = _probs(qm, kw, bias[hh], i)
            dp = lax.dot_general(dom, vw, NT_DIMS, preferred_element_type=F32)
            ds = p * (dp - jnp.sum(p * dp, axis=-1, keepdims=True))
            dsacc[hh] += ds
            ds16 = ds.astype(BF16)
            dqh = jnp.dot(ds16, kw, preferred_element_type=F32) * (CHUNK ** -0.5)
            dq = jnp.where(mask, dqh, 0.0) if dq is None else jnp.where(mask, dqh, dq)
            dkacc[win, :] += lax.dot_general(ds16, qm, TN_DIMS, preferred_element_type=F32)
            dvacc[win, :] += lax.dot_general(p.astype(BF16), dom, TN_DIMS, preferred_element_type=F32)
        dq_ref[...] = dq.astype(BF16)

        @pl.when(i == 0)
        def _():
            sq_ref[...] = _colsum(dq)

        @pl.when(i > 0)
        def _():
            sq_ref[...] += _colsum(dq)

        @pl.when(i == n_i - 1)
        def _():
            dk = dkacc[PAD_ROWS:PAD_ROWS + t, :]
            dv = dvacc[PAD_ROWS:PAD_ROWS + t, :]
            dk_ref[...] = dk.astype(BF16)
            dv_ref[...] = dv.astype(BF16)
            sk_ref[...] = _colsum(dk)
            sv_ref[...] = _colsum(dv)
            rr = lax.broadcasted_iota(jnp.int32, (Q_TILE, Q_TILE), 0)
            cc = lax.broadcasted_iota(jnp.int32, (Q_TILE, Q_TILE), 1)
            rev = jnp.where(rr + cc == Q_TILE - 1, 1.0, 0.0).astype(BF16)
            for hh in range(2):
                acc = None
                for part in _split3(dsacc[hh]):
                    pr = jnp.dot(rev, part, preferred_element_type=F32)
                    acc = pr if acc is None else acc + pr
                wide = jnp.concatenate([acc, jnp.zeros((Q_TILE, TOEP - WINDOW), F32)], axis=1)
                dg_ref[pl.ds(2 * hp + hh, 1), :] = _colsum(pltpu.roll(wide, 0, 1, stride=1, stride_axis=0))

    col = lambda off: pl.BlockSpec((t, 128), lambda h, i: (0, off + h))
    tile = lambda: pl.BlockSpec((Q_TILE, 128), lambda h, i: (i, h))
    sums = lambda: pl.BlockSpec((1, 128), lambda h, i: (0, h))
    res = _call(
        body, grid=(4, n_i),
        in_specs=[tile(), col(4), col(8), tile(), pl.BlockSpec((N_HEADS, TOEP), lambda h, i: (0, 0))],
        out_specs=[tile(), col(0), col(0), sums(), sums(), sums(), pl.BlockSpec((N_HEADS, TOEP), lambda h, i: (0, 0))],
        out_shape=[_sds((t, 512), BF16)] * 3 + [_sds((1, 512), F32)] * 3 + [_sds((N_HEADS, TOEP), F32)],
        scratch_shapes=[pltpu.VMEM((PAD_ROWS + t, 128), BF16), pltpu.VMEM((PAD_ROWS + t, 128), BF16),
                        pltpu.VMEM((PAD_ROWS + t, 128), F32), pltpu.VMEM((PAD_ROWS + t, 128), F32),
                        pltpu.VMEM((2, Q_TILE, WINDOW), F32), pltpu.VMEM((2, Q_TILE, WINDOW), F32)],
        sem=("arbitrary", "arbitrary"), name=name, args=(z, z, z, datt, gen), carried=carried)
    return res[:7] if carried is None else (res[:7], res[7:])


def _adamw_math(w, g, m, v):
    m = ADAM_B1 * m + (1.0 - ADAM_B1) * g
    v = ADAM_B2 * v + (1.0 - ADAM_B2) * (g * g)
    m_hat = m / (1.0 - ADAM_B1 ** ADAM_STEP)
    v_hat = v / (1.0 - ADAM_B2 ** ADAM_STEP)
    delta = -ADAM_LR * (m_hat / (jnp.sqrt(v_hat) + ADAM_EPS) + ADAM_WD * w)
    return delta, m, v


def _adamw(w, g, m, v, name):
    r, c = w.shape
    tm = next(cand for cand in (256, 176, 128, 64, 32, 16, 8) if r % cand == 0)
    return _rowcall(_adamw_math, [(w, c, 0), (g, c, 0), (m, c, 0), (v, c, 0)], [],
                    [(r, c, F32, c)] * 3, [], name=name, tm=tm)


def _ada_fwd(c_all, w_shard, b_shard, name):
    n = w_shard.shape[1]
    tn = 512

    def body(c_ref, w_ref, b_ref, o_ref, a_ref):
        cv = c_ref[...]
        act = cv * _sigmoid(cv)
        a_ref[...] = act
        o_ref[...] = jnp.dot(act.astype(BF16), w_ref[...].astype(BF16), preferred_element_type=F32) + b_ref[...]

    return pl.pallas_call(
        body, grid=(n // tn,),
        in_specs=[pl.BlockSpec((8, D), lambda j: (0, 0)), pl.BlockSpec((D, tn), lambda j: (0, j)),
                  pl.BlockSpec((1, tn), lambda j: (0, j))],
        out_specs=[pl.BlockSpec((8, tn), lambda j: (0, j)), pl.BlockSpec((8, D), lambda j: (0, 0))],
        out_shape=[_sds((8, n), F32), _sds((8, D), F32)],
        compiler_params=_params(("arbitrary",)), name=name,
    )(c_all, w_shard, b_shard)


def _ada_bwd_adamw(act_t, dmod_shard, w, m, v, name):
    r, c = w.shape
    tm = 256

    def body(a_ref, d_ref, w_ref, m_ref, v_ref, g_ref, dl_ref, nm_ref, nv_ref):
        g = jnp.dot(a_ref[...], d_ref[...], precision=lax.Precision.HIGHEST, preferred_element_type=F32)
        g_ref[...] = g
        dl_ref[...], nm_ref[...], nv_ref[...] = _adamw_math(w_ref[...], g, m_ref[...], v_ref[...])

    blk = pl.BlockSpec((tm, c), lambda i: (i, 0))
    return pl.pallas_call(
        body, grid=(r // tm,),
        in_specs=[pl.BlockSpec((tm, 8), lambda i: (i, 0)), pl.BlockSpec((8, c), lambda i: (0, 0)), blk, blk, blk],
        out_specs=[blk] * 4, out_shape=[_sds((r, c), F32)] * 4,
        compiler_params=_params(("arbitrary",)), name=name,
    )(act_t, dmod_shard, w, m, v)


def _place():
    return lax.axis_index("x"), lax.axis_index("y"), lax.axis_index("c")


def _flip(v, bit):
    return 1 - v if bit else v


VMEM_SPEC = pl.BlockSpec(memory_space=pltpu.VMEM)


def _allgather8(v, name):
    r, c = v.shape

    def body(v_ref, g_ref, tot_ref, send_sems, recv_sems, local_sem):
        x, y, cc = _place()
        me = 4 * x + 2 * y + cc
        mine = pltpu.make_async_copy(v_ref, g_ref.at[me], local_sem)
        mine.start()
        sends = []
        for k in range(1, 8):
            peer = (_flip(x, k & 4), _flip(y, k & 2), _flip(cc, k & 1))
            cp = pltpu.make_async_remote_copy(src_ref=v_ref, dst_ref=g_ref.at[me], send_sem=send_sems.at[k - 1],
                                              recv_sem=recv_sems.at[k - 1], device_id=peer, device_id_type=MESH)
            cp.start()
            sends.append(cp)
        for k in range(1, 8):
            peer = (_flip(x, k & 4), _flip(y, k & 2), _flip(cc, k & 1))
            theirs = g_ref.at[4 * peer[0] + 2 * peer[1] + peer[2]]
            pltpu.make_async_remote_copy(src_ref=v_ref, dst_ref=theirs, send_sem=send_sems.at[k - 1],
                                         recv_sem=recv_sems.at[k - 1], device_id=peer, device_id_type=MESH).wait_recv()
        for cp in sends:
            cp.wait_send()
        mine.wait()
        tot = g_ref[0]
        for d in range(1, 8):
            tot = tot + g_ref[d]
        tot_ref[...] = tot

    return pl.pallas_call(
        body, in_specs=[VMEM_SPEC], out_specs=[VMEM_SPEC, VMEM_SPEC],
        out_shape=[_sds((8, r, c), F32), _sds((r, c), F32)],
        scratch_shapes=[pltpu.SemaphoreType.DMA((7,)), pltpu.SemaphoreType.DMA((7,)), pltpu.SemaphoreType.DMA],
        compiler_params=pltpu.CompilerParams(vmem_limit_bytes=VMEM_LIMIT), name=name,
    )(v)


def _slot(px, py, swapped):
    return 2 * py + px if swapped else 2 * px + py


def _gather_shards(arrs, swapped, name, in_place=False):
    n = len(arrs)

    def body(*refs):
        ins, outs = refs[:n], refs[n:2 * n]
        send1, recv1, send2, recv2, local_sems = refs[2 * n:]
        x, y, c = _place()
        sibling = (x, y, 1 - c)
        chips = [(_flip(x, k & 2), _flip(y, k & 1)) for k in (1, 2, 3)]
        local_copies, sends = [], []
        for a in range(n):
            h = outs[a].shape[1] // 2
            mine = pl.ds(pl.multiple_of(c * h, 8), h)
            own = _slot(x, y, swapped[a])
            if in_place:
                src = outs[a].at[own, mine]
            else:
                src = ins[a].at[mine]
                lc = pltpu.make_async_copy(ins[a], outs[a].at[own], local_sems.at[a])
                lc.start()
                local_copies.append(lc)
            for j, (px, py) in enumerate(chips):
                cp = pltpu.make_async_remote_copy(
                    src_ref=src, dst_ref=outs[a].at[own, mine], send_sem=send1.at[3 * a + j],
                    recv_sem=recv1.at[3 * a + j], device_id=(px, py, c), device_id_type=MESH)
                cp.start()
                sends.append(cp)
        for a in range(n):
            h = outs[a].shape[1] // 2
            mine = pl.ds(pl.multiple_of(c * h, 8), h)
            for j, (px, py) in enumerate(chips):
                piece = outs[a].at[_slot(px, py, swapped[a]), mine]
                pltpu.make_async_remote_copy(
                    src_ref=piece, dst_ref=piece, send_sem=send1.at[3 * a + j], recv_sem=recv1.at[3 * a + j],
                    device_id=(px, py, c), device_id_type=MESH).wait_recv()
                fwd = pltpu.make_async_remote_copy(
                    src_ref=piece, dst_ref=piece, send_sem=send2.at[3 * a + j], recv_sem=recv2.at[3 * a + j],
                    device_id=sibling, device_id_type=MESH)
                fwd.start()
                sends.append(fwd)
        for a in range(n):
            h = outs[a].shape[1] // 2
            other = pl.ds(pl.multiple_of((1 - c) * h, 8), h)
            for j, (px, py) in enumerate(chips):
                piece = outs[a].at[_slot(px, py, swapped[a]), other]
                pltpu.make_async_remote_copy(
                    src_ref=piece, dst_ref=piece, send_sem=send2.at[3 * a + j], recv_sem=recv2.at[3 * a + j],
                    device_id=sibling, device_id_type=MESH).wait_recv()
        for cp in sends:
            cp.wait_send()
        for lc in local_copies:
            lc.wait()

    dma = lambda k: pltpu.SemaphoreType.DMA((k,))
    return pl.pallas_call(
        body, in_specs=[ANY] * n, out_specs=[ANY] * n,
        out_shape=[_sds(a.shape if in_place else (4,) + a.shape, a.dtype) for a in arrs],
        scratch_shapes=[dma(3 * n), dma(3 * n), dma(3 * n), dma(3 * n), dma(n)],
        input_output_aliases={a: a for a in range(n)} if in_place else {},
        name=name,
    )(*arrs)


def _carry_pair_exchange(grads):
    n = len(grads)

    def copies(ins, outs, send_sems, recv_sems):
        x, y, c = _place()
        cps = []
        for a in range(n):
            h = ins[a].shape[1] // 2
            theirs = pl.ds(pl.multiple_of((1 - c) * h, 8), h)
            cps.append(pltpu.make_async_remote_copy(
                src_ref=ins[a].at[:, theirs, :], dst_ref=outs[a], send_sem=send_sems.at[a], recv_sem=recv_sems.at[a],
                device_id=(x, y, 1 - c), device_id_type=MESH))
        return cps

    def start(*refs):
        for cp in copies(*refs):
            cp.start()

    def finish(*refs):
        for cp in copies(*refs):
            cp.wait()

    return _Carried(grads, [_sds((4, g.shape[1] // 2, g.shape[2]), F32) for g in grads], {}, n, start, finish)


def _pair_sum(grad, recv, core, name):
    _, r, c = grad.shape
    h = r // 2

    def body(core_ref, g_ref, r_ref, o_ref):
        o_ref[...] = (g_ref[...] + r_ref[...]).astype(BF16)

    return pl.pallas_call(
        body,
        grid_spec=pltpu.PrefetchScalarGridSpec(
            num_scalar_prefetch=1, grid=(4,),
            in_specs=[pl.BlockSpec((None, h, c), lambda s, core_ref: (s, core_ref[0], 0)),
                      pl.BlockSpec((None, h, c), lambda s, core_ref: (s, 0, 0))],
            out_specs=pl.BlockSpec((None, h, c), lambda s, core_ref: (s, 0, 0))),
        out_shape=_sds((4, h, c), BF16), compiler_params=_params(("arbitrary",)), name=name,
    )(core, grad, recv)


def _carry_chip_exchange(parts, swapped):
    n = len(parts)

    def copies(ins, outs, send_sems, recv_sems):
        x, y, c = _place()
        chips = [(_flip(x, k & 2), _flip(y, k & 1)) for k in (1, 2, 3)]
        cps = []
        for a in range(n):
            for j, (px, py) in enumerate(chips):
                cps.append(pltpu.make_async_remote_copy(
                    src_ref=ins[a].at[_slot(px, py, swapped[a])], dst_ref=outs[a].at[j],
                    send_sem=send_sems.at[3 * a + j], recv_sem=recv_sems.at[3 * a + j],
                    device_id=(px, py, c), device_id_type=MESH))
        return cps

    def start(*refs):
        for cp in copies(*refs):
            cp.start()

    def finish(*refs):
        for cp in copies(*refs):
            cp.wait()

    return _Carried(parts, [_sds((3,) + p.shape[1:], BF16) for p in parts], {}, 3 * n, start, finish)


def _chip_sum(part, recv, slot_core, name):
    _, h, c = part.shape

    def body(sc_ref, p_ref, r_ref, o_ref):
        acc = p_ref[...].astype(F32)
        for j in range(3):
            acc = acc + r_ref[j].astype(F32)
        o_ref[...] = acc

    return pl.pallas_call(
        body,
        grid_spec=pltpu.PrefetchScalarGridSpec(
            num_scalar_prefetch=1, grid=(1,),
            in_specs=[pl.BlockSpec((None, h, c), lambda s, sc_ref: (sc_ref[0], 0, 0)),
                      pl.BlockSpec((3, h, c), lambda s, sc_ref: (0, 0, 0))],
            out_specs=pl.BlockSpec((h, c), lambda s, sc_ref: (sc_ref[1], 0))),
        out_shape=_sds((2 * h, c), F32), compiler_params=_params(("arbitrary",)), name=name,
    )(slot_core, part, recv)


def _carry_pair_share(shards):
    n = len(shards)

    def copies(outs, send_sems, recv_sems, mine):
        x, y, c = _place()
        cps = []
        for a in range(n):
            h = outs[a].shape[0] // 2
            half = outs[a].at[pl.ds(pl.multiple_of((c if mine else 1 - c) * h, 8), h)]
            cps.append(pltpu.make_async_remote_copy(
                src_ref=half, dst_ref=half, send_sem=send_sems.at[a], recv_sem=recv_sems.at[a],
                device_id=(x, y, 1 - c), device_id_type=MESH))
        return cps

    def start(ins, outs, send_sems, recv_sems):
        for cp in copies(outs, send_sems, recv_sems, True):
            cp.start()

    def finish(ins, outs, send_sems, recv_sems):
        for cp in copies(outs, send_sems, recv_sems, False):
            cp.wait_recv()
        for cp in copies(outs, send_sems, recv_sems, True):
            cp.wait_send()

    return _Carried(shards, [_sds(s.shape, F32) for s in shards], {a: a for a in range(n)}, n, start, finish)


def _carry_gather_ici(bufs, swapped):
    n = len(bufs)

    def copies(outs, send_sems, recv_sems, sending):
        x, y, c = _place()
        cps = []
        for a in range(n):
            h = outs[a].shape[1] // 2
            mine = pl.ds(pl.multiple_of(c * h, 8), h)
            for j, k in enumerate((1, 2, 3)):
                px, py = _flip(x, k & 2), _flip(y, k & 1)
                slot = _slot(x, y, swapped[a]) if sending else _slot(px, py, swapped[a])
                piece = outs[a].at[slot, mine]
                cps.append(pltpu.make_async_remote_copy(
                    src_ref=piece, dst_ref=piece, send_sem=send_sems.at[3 * a + j], recv_sem=recv_sems.at[3 * a + j],
                    device_id=(px, py, c), device_id_type=MESH))
        return cps

    def start(ins, outs, send_sems, recv_sems):
        for cp in copies(outs, send_sems, recv_sems, True):
            cp.start()

    def finish(ins, outs, send_sems, recv_sems):
        for cp in copies(outs, send_sems, recv_sems, False):
            cp.wait_recv()
        for cp in copies(outs, send_sems, recv_sems, True):
            cp.wait_send()

    return _Carried(bufs, [_sds(b.shape, b.dtype) for b in bufs], {a: a for a in range(n)}, 3 * n, start, finish)


def _carry_gather_forward(bufs, swapped):
    n = len(bufs)

    def copies(outs, send_sems, recv_sems, sending):
        x, y, c = _place()
        cps = []
        for a in range(n):
            h = outs[a].shape[1] // 2
            rows = pl.ds(pl.multiple_of((c if sending else 1 - c) * h, 8), h)
            for j, k in enumerate((1, 2, 3)):
                piece = outs[a].at[_slot(_flip(x, k & 2), _flip(y, k & 1), swapped[a]), rows]
                cps.append(pltpu.make_async_remote_copy(
                    src_ref=piece, dst_ref=piece, send_sem=send_sems.at[3 * a + j], recv_sem=recv_sems.at[3 * a + j],
                    device_id=(x, y, 1 - c), device_id_type=MESH))
        return cps

    def start(ins, outs, send_sems, recv_sems):
        for cp in copies(outs, send_sems, recv_sems, True):
            cp.start()

    def finish(ins, outs, send_sems, recv_sems):
        for cp in copies(outs, send_sems, recv_sems, False):
            cp.wait_recv()
        for cp in copies(outs, send_sems, recv_sems, True):
            cp.wait_send()

    return _Carried(bufs, [_sds(b.shape, b.dtype) for b in bufs], {a: a for a in range(n)}, 3 * n, start, finish)


def _pack(arrs, rows_multiple=8):
    parts, offs, row = [], [], 0
    for a in arrs:
        flat = a.reshape(-1)
        nrow = -(-flat.shape[0] // D)
        parts.append(jnp.pad(flat, (0, nrow * D - flat.shape[0])))
        offs.append(row)
        row += nrow
    total = -(-row // rows_multiple) * rows_multiple
    if total > row:
        parts.append(jnp.zeros(((total - row) * D,), F32))
    return jnp.concatenate(parts).reshape(total, D), offs


def _unpack(packed, offs, shapes):
    out = []
    for off, shp in zip(offs, shapes):
        size = int(np.prod(shp))
        nrow = -(-size // D)
        out.append(packed[off:off + nrow].reshape(-1)[:size].reshape(shp))
    return out


def _to_bf16_slot(w, slot, name):
    r, c = w.shape
    tm = next(cand for cand in (256, 176, 128, 64, 32, 16) if r % cand == 0)

    def body(slot_ref, w_ref, o_ref):
        o_ref[...] = w_ref[...].astype(BF16)

    return pl.pallas_call(
        body,
        grid_spec=pltpu.PrefetchScalarGridSpec(
            num_scalar_prefetch=1, grid=(r // tm,),
            in_specs=[pl.BlockSpec((tm, c), lambda i, slot_ref: (i, 0))],
            out_specs=pl.BlockSpec((None, tm, c), lambda i, slot_ref: (slot_ref[0], i, 0))),
        out_shape=_sds((4, r, c), BF16), compiler_params=_params(("arbitrary",)), name=name,
    )(slot, w)


def _unshard_cols(g):
    s, k, n = g.shape
    return jnp.transpose(g, (1, 0, 2)).reshape(k, s * n)


def _ff_swap(v):
    b = FF_BLOCK
    return jnp.concatenate([v[..., 0:b], v[..., 2 * b:3 * b], v[..., b:2 * b], v[..., 3 * b:4 * b]], axis=-1)


LATE = ("attn_o", "conv_o", "mix_o", "up", "down")
EARLY_GRADS = ("down", "up", "mix_o", "attn_o", "conv_o")


def _weight_views(bufs):
    return {"up": bufs["up"], "attn_o": _unshard_cols(bufs["attn_o"]), "conv_o": _unshard_cols(bufs["conv_o"]),
            "mix_o": bufs["mix_o"].reshape(D, D), "down": bufs["down"].reshape(D_FF, D)}


def _reduce_start(names, grads, dist):
    tag = "_".join(names)
    recv = _run_carried(_carry_pair_exchange(grads), "pair_exchange_" + tag)
    return [_pair_sum(g, r, dist["core"], "pair_sum_" + n) for n, g, r in zip(names, grads, recv)]


def _reduce_halves(names, parts, from_chips, dist):
    return [_chip_sum(p, r, jnp.concatenate([dist["slots"][SWAPPED[n]], dist["core"]]), "chip_sum_" + n)
            for n, p, r in zip(names, parts, from_chips)]


def _local_step(x, target, mod, w_in, late, small, dist=None):
    sh_m, sc_m, gt_m, sh_f, sc_f, gt_f = mod
    t = x.shape[0]
    tmm = min(1024, t)
    late_swapped = [SWAPPED[n] for n in LATE]

    h1 = _pre_norm(x, small["g_pre_mix"], sc_m, sh_m, "pre_norm_mix")
    z = _matmul(h1, w_in, form="nn", out_dtype=F32, tm=tmm, tn=1152, tk=D, bias=small["b_in"], name="mm_in")
    if dist is None:
        att = _attention(z, small["gen"], "attention")
        bufs = late
    else:
        att, landed = _attention(z, small["gen"], "attention",
                                 carried=_carry_gather_ici([late[n] for n in LATE], late_swapped))
        bufs = dict(zip(LATE, _run_carried(_carry_gather_forward(landed, late_swapped), "gather_forward")))
    w = _weight_views(bufs)
    w["in"] = w_in
    a = _matmul(att, w["attn_o"], form="nn", out_dtype=F32, tm=tmm, tn=512, tk=512, name="mm_attn_o")
    u1, u3 = _conv_branch(z, small["w_dw_conv"], small["b_dw_conv"], small["g_conv_ln"], small["b_conv_ln"], "conv_branch")
    cb = _matmul(u3, w["conv_o"], form="nn", out_dtype=F32, tm=tmm, tn=512, tk=512, bias=small["b_conv_o"], name="mm_conv_o")
    y = _gate_merge(a, cb, z, "gate_merge")
    ym = _matmul(y, w["mix_o"], form="nn", out_dtype=F32, tm=tmm, tn=512, tk=D, name="mm_mix_o")
    x1 = _post_res(x, ym, small["g_post_mix"], gt_m, "post_res_mix")
    h2 = _pre_norm(x1, small["g_pre_ffn"], sc_f, sh_f, "pre_norm_ffn")
    up = _matmul(h2, w["up"], form="nn", out_dtype=F32, tm=tmm, tn=FF_BLOCK, tk=D, name="mm_up")
    act = _ffn_act(up, small["w_dw_ffn"], small["b_dw_ffn"], "ffn_act")
    yf = _matmul(act, w["down"], form="nn", out_dtype=F32, tm=tmm, tn=512, tk=D_FF, name="mm_down")

    dx2, dyf, loss_cols, d_g_post_ffn, d_gt_f = _ffn_tail(x1, yf, target, small["g_post_ffn"], gt_f, "ffn_tail")
    dact = _matmul(dyf, w["down"], form="nt", out_dtype=F32, tm=tmm, tn=FF_BLOCK, tk=D, name="mm_down_dx")
    g_down = _matmul(act, dyf, form="tn", out_dtype=F32, tm=FF_BLOCK, tn=D, tk=tmm, name="mm_down_dw")
    dup, d_w_dw_ffn, d_b_dw_ffn = _ffn_act_bwd(dact, up, small["w_dw_ffn"], small["b_dw_ffn"], "ffn_act_bwd")
    dh2 = _matmul(dup, w["up"], form="nt", out_dtype=F32, tm=tmm, tn=D, tk=FF_BLOCK, name="mm_up_dx")
    g_up = _matmul(h2, dup, form="tn", out_dtype=F32, tm=D, tn=FF_BLOCK, tk=tmm, out_sharded=True, name="mm_up_dw")
    dx1, d_sh_f, d_sc_f, d_g_pre_ffn = _pre_norm_bwd(dh2, x1, dx2, small["g_pre_ffn"], sc_f, "pre_norm_ffn_bwd")
    dym, d_g_post_mix, d_gt_m = _post_res_bwd(dx1, ym, small["g_post_mix"], gt_m, "post_res_mix_bwd")
    dy = _matmul(dym, w["mix_o"], form="nt", out_dtype=F32, tm=tmm, tn=512, tk=D, name="mm_mix_o_dx")
    g_mix_o = _matmul(y, dym, form="tn", out_dtype=F32, tm=D, tn=512, tk=tmm, name="mm_mix_o_dw")
    da, dcb, dgate_a, dgate_b, d_b_conv_o, sga, sgb = _gate_merge_bwd(dy, a, cb, z, "gate_merge_bwd")
    datt = _matmul(da, w["attn_o"], form="nt", out_dtype=F32, tm=tmm, tn=512, tk=D, name="mm_attn_o_dx")
    g_attn_o = _matmul(att, da, form="tn", out_dtype=F32, tm=512, tn=256, tk=tmm, out_sharded=True, name="mm_attn_o_dw")
    du3 = _matmul(dcb, w["conv_o"], form="nt", out_dtype=F32, tm=tmm, tn=512, tk=D, name="mm_conv_o_dx")
    g_conv_o = _matmul(u3, dcb, form="tn", out_dtype=F32, tm=512, tn=256, tk=tmm, out_sharded=True, name="mm_conv_o_dw")
    dglu, d_w_dw_conv, d_b_dw_conv, d_g_conv_ln, d_b_conv_ln, sglu = _conv_branch_bwd(
        du3, u1, z, small["w_dw_conv"], small["g_conv_ln"], small["b_conv_ln"], "conv_branch_bwd")
    big = {"attn_o": g_attn_o, "conv_o": g_conv_o, "mix_o": g_mix_o.reshape(4, 256, D),
           "up": g_up, "down": g_down.reshape(4, D_FF // 4, D)}
    in_dw = dict(form="tn", out_dtype=F32, tm=D, tn=1152, tk=tmm, out_sharded=True, name="mm_in_dw")
    in_dx = dict(form="nt", out_dtype=F32, tm=tmm, tn=D, tk=1152, name="mm_in_dx")
    if dist is None:
        dq, dk, dv, sq, sk, sv, dgen = _attention_bwd(z, datt, small["gen"], "attention_bwd")
        dz = jnp.concatenate([dq, dk, dv, dglu, dgate_a, dgate_b], axis=1)
        big["in"] = _matmul(h1, dz, **in_dw)
        dh1 = _matmul(dz, w_in, **in_dx)
    else:
        early = [big[n] for n in EARLY_GRADS]
        parts = _reduce_start(EARLY_GRADS, early, dist)
        (dq, dk, dv, sq, sk, sv, dgen), from_chips = _attention_bwd(
            z, datt, small["gen"], "attention_bwd",
            carried=_carry_chip_exchange(parts, [SWAPPED[n] for n in EARLY_GRADS]))
        halves = _reduce_halves(EARLY_GRADS, parts, from_chips, dist)
        dz = jnp.concatenate([dq, dk, dv, dglu, dgate_a, dgate_b], axis=1)
        g_in, shards = _matmul(h1, dz, carried=_carry_pair_share(halves), **in_dw)
        big = dict(zip(EARLY_GRADS, shards))
        part_in = _reduce_start(("in",), [g_in], dist)
        dh1, from_chips_in = _matmul(dz, w_in, carried=_carry_chip_exchange(part_in, [False]), **in_dx)
        half_in = _reduce_halves(("in",), part_in, from_chips_in, dist)
        big["in"] = _run_carried(_carry_pair_share(half_in), "pair_share_in")[0]
    d_b_in = jnp.concatenate([sq, sk, sv, sglu, sga, sgb], axis=1)
    grad_x, d_sh_m, d_sc_m, d_g_pre_mix = _pre_norm_bwd(dh1, x, dx1, small["g_pre_mix"], sc_m, "pre_norm_mix_bwd")

    dmod = [d_sh_m, d_sc_m, d_gt_m, d_sh_f, d_sc_f, d_gt_f]
    sm = {"g_pre_mix": d_g_pre_mix, "g_post_mix": d_g_post_mix, "b_in": d_b_in, "gen": dgen,
          "w_dw_conv": d_w_dw_conv, "b_dw_conv": d_b_dw_conv, "g_conv_ln": d_g_conv_ln, "b_conv_ln": d_b_conv_ln,
          "b_conv_o": d_b_conv_o, "g_pre_ffn": d_g_pre_ffn, "g_post_ffn": d_g_post_ffn,
          "w_dw_ffn": d_w_dw_ffn, "b_dw_ffn": d_b_dw_ffn}
    return loss_cols, grad_x, dmod, big, sm


BIG = ("in", "attn_o", "conv_o", "mix_o", "up", "down")
SWAPPED = {"in": False, "attn_o": False, "conv_o": False, "mix_o": False, "up": True, "down": False}
SMALL_ORDER = ("b_ada", "g_pre_mix", "g_post_mix", "b_in", "rel_bias", "b_dw_conv", "g_conv_ln", "b_conv_ln",
               "b_conv_o", "g_pre_ffn", "g_post_ffn", "b_dw_ffn", "w_dw_conv", "w_dw_ffn")


def kernel(x, c, w_ada, b_ada, g_pre_mix, g_post_mix, w_in, b_in, rel_bias, w_attn_o, w_dw_conv, b_dw_conv, g_conv_ln, b_conv_ln, w_conv_o, b_conv_o, w_mix_o, g_pre_ffn, g_post_ffn, w_up, w_dw_ffn, b_dw_ffn, w_down, loss_target, m_w_ada, m_b_ada, m_g_pre_mix, m_g_post_mix, m_w_in, m_b_in, m_rel_bias, m_w_attn_o, m_w_dw_conv, m_b_dw_conv, m_g_conv_ln, m_b_conv_ln, m_w_conv_o, m_b_conv_o, m_w_mix_o, m_g_pre_ffn, m_g_post_ffn, m_w_up, m_w_dw_ffn, m_b_dw_ffn, m_w_down, v_w_ada, v_b_ada, v_g_pre_mix, v_g_post_mix, v_w_in, v_b_in, v_rel_bias, v_w_attn_o, v_w_dw_conv, v_b_dw_conv, v_g_conv_ln, v_b_conv_ln, v_w_conv_o, v_b_conv_o, v_w_mix_o, v_g_pre_ffn, v_g_post_ffn, v_w_up, v_w_dw_ffn, v_b_dw_ffn, v_w_down):
    given = dict(locals())
    ax, ay, ac = lax.axis_index("x"), lax.axis_index("y"), lax.axis_index("c")
    shard = 2 * ax + ay
    me = 4 * ax + 2 * ay + ac
    xs, target = x[0], loss_target[0]

    c_pad = jnp.pad(c, ((0, 7), (0, 0)))
    c_g, _ = _allgather8(c_pad, "gather_c")
    c_all = c_g[:, 0, :]
    b_ada_shard = lax.dynamic_slice(b_ada, (0, shard * 1536), (1, 1536))
    mod_shard, c_act = _ada_fwd(c_all, w_ada[0], b_ada_shard, "ada_fwd")
    small_in = [jnp.pad(mod_shard, ((0, 8), (0, 0))),
                jnp.pad(w_dw_conv[0], ((0, 1), (0, 0))),
                jnp.pad(w_dw_ffn[0], ((0, 13), (0, 0)))]
    mod_g, wdc_g, wdf_g = _gather_shards(small_in, [False, False, True], "gather_small")
    mod_all = jnp.transpose(mod_g[:, :8, :], (1, 0, 2)).reshape(8, 6 * D)
    mod_row = lax.dynamic_slice(mod_all, (me, 0), (1, 6 * D))
    mod = [mod_row[:, k * D:(k + 1) * D] for k in range(6)]

    slots = {sw: _slot(ax, ay, sw).astype(jnp.int32).reshape(1) for sw in (False, True)}
    own = {n: _to_bf16_slot(given["w_" + n][0], slots[SWAPPED[n]], "cast_" + n) for n in BIG}
    w_in_all = _gather_shards([own["in"]], [False], "gather_w_in", in_place=True)[0]
    core = ac.astype(jnp.int32).reshape(1)
    dist = {"core": core, "slots": slots}

    sel = jnp.asarray(_toeplitz_map())
    rel_pad = jnp.pad(rel_bias[0], ((0, 0), (0, REL_PAD - (2 * MAX_REL + 1))))
    gen = _select_call(rel_pad, sel.T.astype(BF16), "bias_rows")
    small = {"g_pre_mix": g_pre_mix, "g_post_mix": g_post_mix, "b_in": b_in, "gen": gen,
             "w_dw_conv": _unshard_cols(wdc_g[:, :CONV_K, :]), "b_dw_conv": b_dw_conv, "g_conv_ln": g_conv_ln,
             "b_conv_ln": b_conv_ln, "b_conv_o": b_conv_o, "g_pre_ffn": g_pre_ffn, "g_post_ffn": g_post_ffn,
             "w_dw_ffn": _unshard_cols(wdf_g[:, :FFN_K, :]), "b_dw_ffn": _ff_swap(b_dw_ffn)}

    loss_cols, grad_x, dmod, reduced, sm = _local_step(xs, target, mod, w_in_all, {n: own[n] for n in LATE}, small, dist)
    loss = lax.psum(jnp.sum(loss_cols), ("x", "y", "c"))

    d_rel = _select_call(sm["gen"], sel.astype(BF16), "bias_fold")[:, :2 * MAX_REL + 1]
    small_grads = {"g_pre_mix": sm["g_pre_mix"], "g_post_mix": sm["g_post_mix"], "b_in": sm["b_in"], "rel_bias": d_rel[None],
                   "b_dw_conv": sm["b_dw_conv"], "g_conv_ln": sm["g_conv_ln"], "b_conv_ln": sm["b_conv_ln"],
                   "b_conv_o": sm["b_conv_o"], "g_pre_ffn": sm["g_pre_ffn"], "g_post_ffn": sm["g_post_ffn"],
                   "b_dw_ffn": _ff_swap(sm["b_dw_ffn"]), "w_dw_conv": sm["w_dw_conv"], "w_dw_ffn": _ff_swap(sm["w_dw_ffn"])}
    order = [n for n in SMALL_ORDER if n != "b_ada"]
    packed, offs = _pack([jnp.concatenate(dmod, axis=1)] + [small_grads[n] for n in order])
    every, total = _allgather8(packed, "gather_small_grads")
    dmod_all = every[:, 0:6, :].reshape(8, 6 * D)
    full_shapes = {n: given[n].shape for n in order}
    full_shapes["w_dw_conv"], full_shapes["w_dw_ffn"] = (1, CONV_K, 512), (1, FFN_K, 2 * D_FF)
    sums = dict(zip(order, _unpack(total, offs[1:], [full_shapes[n] for n in order])))
    sums["b_ada"] = total[0:6].reshape(1, 6 * D)
    sums["w_dw_conv"] = lax.dynamic_slice(sums["w_dw_conv"], (0, 0, shard * 128), (1, CONV_K, 128))
    sums["w_dw_ffn"] = lax.dynamic_slice(sums["w_dw_ffn"], (0, 0, shard * FF_BLOCK), (1, FFN_K, FF_BLOCK))

    pw, poffs = _pack([given[n] for n in SMALL_ORDER])
    pg, _ = _pack([sums[n] for n in SMALL_ORDER])
    pm, _ = _pack([given["m_" + n] for n in SMALL_ORDER])
    pv, _ = _pack([given["v_" + n] for n in SMALL_ORDER])
    shapes = [given[n].shape for n in SMALL_ORDER]
    upd = [dict(zip(SMALL_ORDER, _unpack(p, poffs, shapes))) for p in _adamw(pw, pg, pm, pv, "adamw_small")]

    dmod_shard = lax.dynamic_slice(dmod_all, (0, shard * 1536), (8, 1536))
    ada = _ada_bwd_adamw(c_act.T, dmod_shard, w_ada[0], m_w_ada[0], v_w_ada[0], "ada_bwd_adamw")

    out = {"grad_w_ada": ada[0][None], "delta_w_ada": ada[1][None], "new_m_w_ada": ada[2][None], "new_v_w_ada": ada[3][None]}
    for n in BIG:
        g = reduced[n]
        dl, nm, nv = _adamw(given["w_" + n][0], g, given["m_w_" + n][0], given["v_w_" + n][0], "adamw_" + n)
        out["grad_w_" + n], out["delta_w_" + n], out["new_m_w_" + n], out["new_v_w_" + n] = g[None], dl[None], nm[None], nv[None]
    for n in SMALL_ORDER:
        out["grad_" + n], out["delta_" + n], out["new_m_" + n], out["new_v_" + n] = sums[n], upd[0][n], upd[1][n], upd[2][n]

    weights = ["w_ada", "b_ada", "g_pre_mix", "g_post_mix", "w_in", "b_in", "rel_bias", "w_attn_o", "w_dw_conv", "b_dw_conv",
               "g_conv_ln", "b_conv_ln", "w_conv_o", "b_conv_o", "w_mix_o", "g_pre_ffn", "g_post_ffn", "w_up", "w_dw_ffn",
               "b_dw_ffn", "w_down"]
    return (loss, grad_x[None], *[out["grad_" + n] for n in weights], *[out["delta_" + n] for n in weights],
            *[out["new_m_" + n] for n in weights], *[out["new_v_" + n] for n in weights])
```

```python
import functools
import math

import numpy as np
import jax
import jax.numpy as jnp
from jax import lax
from jax.experimental import pallas as pl
from jax.experimental.pallas import tpu as pltpu

F32, BF16 = jnp.float32, jnp.bfloat16
MESH = pl.DeviceIdType.MESH

D = 1024
D_IN = 4608
D_FF = 2816
CONV_K = 31
FFN_K = 3
N_HEADS = 8
CHUNK = 64
LEFT_CHUNKS = 8
MAX_REL = 128
EPS = 1e-6
NEG_INF = -1e30
Q_TILE = 256
WINDOW = Q_TILE + LEFT_CHUNKS * CHUNK
REL_PAD = 384
TOEP = 1024
ROW_TILE = 256
VMEM_LIMIT = 60 * 1024 * 1024

ADAM_LR, ADAM_B1, ADAM_B2, ADAM_EPS, ADAM_WD, ADAM_STEP = 0.001, 0.9, 0.999, 1e-08, 0.01, 10


def _params(sem=None):
    return pltpu.CompilerParams(dimension_semantics=sem, vmem_limit_bytes=VMEM_LIMIT)


def _sds(shape, dtype):
    return jax.ShapeDtypeStruct(tuple(shape), dtype)


ANY = pl.BlockSpec(memory_space=pl.ANY)


class _Carried:
    def __init__(self, ins, out_shapes, aliases, n_sems, start, finish):
        self.ins, self.out_shapes, self.aliases = list(ins), list(out_shapes), dict(aliases)
        self.n_sems, self.start, self.finish = n_sems, start, finish


def _call(body, *, grid, in_specs, out_specs, out_shape, scratch_shapes, sem, name, args, carried=None):
    in_specs, out_specs, out_shape = list(in_specs), list(out_specs), list(out_shape)
    scratch_shapes = list(scratch_shapes)
    if carried is None:
        return pl.pallas_call(body, grid=grid, in_specs=in_specs, out_specs=out_specs, out_shape=out_shape,
                              scratch_shapes=scratch_shapes, compiler_params=_params(sem), name=name)(*args)
    n_in, n_out, n_scr = len(in_specs), len(out_specs), len(scratch_shapes)
    c_in, c_out = len(carried.ins), len(carried.out_shapes)

    def full(*refs):
        pos = [0]

        def take(k):
            part = refs[pos[0]:pos[0] + k]
            pos[0] += k
            return part

        ins, cins, outs, couts, scr = take(n_in), take(c_in), take(n_out), take(c_out), take(n_scr)
        send_sems, recv_sems = take(2)
        first = last = None
        for d, size in enumerate(grid):
            pid = pl.program_id(d)
            first = (pid == 0) if first is None else first & (pid == 0)
            last = (pid == size - 1) if last is None else last & (pid == size - 1)

        @pl.when(first)
        def _():
            carried.start(cins, couts, send_sems, recv_sems)

        body(*ins, *outs, *scr)

        @pl.when(last)
        def _():
            carried.finish(cins, couts, send_sems, recv_sems)

    sems = [pltpu.SemaphoreType.DMA((carried.n_sems,)), pltpu.SemaphoreType.DMA((carried.n_sems,))]
    return pl.pallas_call(
        full, grid=grid, in_specs=in_specs + [ANY] * c_in, out_specs=out_specs + [ANY] * c_out,
        out_shape=out_shape + carried.out_shapes, scratch_shapes=scratch_shapes + sems,
        input_output_aliases={n_in + k: n_out + v for k, v in carried.aliases.items()},
        compiler_params=_params(tuple("arbitrary" for _ in grid)), name=name,
    )(*args, *carried.ins)


def _run_carried(carried, name):
    c_in = len(carried.ins)

    def body(*refs):
        cins, couts = refs[:c_in], refs[c_in:c_in + len(carried.out_shapes)]
        send_sems, recv_sems = refs[-2:]
        carried.start(cins, couts, send_sems, recv_sems)
        carried.finish(cins, couts, send_sems, recv_sems)

    return pl.pallas_call(
        body, in_specs=[ANY] * c_in, out_specs=[ANY] * len(carried.out_shapes), out_shape=carried.out_shapes,
        scratch_shapes=[pltpu.SemaphoreType.DMA((carried.n_sems,)), pltpu.SemaphoreType.DMA((carried.n_sems,))],
        input_output_aliases=carried.aliases, name=name,
    )(*carried.ins)


def _matmul(a, b, *, form, out_dtype, tm, tn, tk, name, bias=None, add=None, out_sharded=False, carried=None):
    b3 = b.ndim == 3
    if form == "nn":
        m, k = a.shape
        n = b.shape[0] * b.shape[2] if b3 else b.shape[1]
        dn = (((1,), (0,)), ((), ()))
        a_spec = pl.BlockSpec((tm, tk), lambda i, j, kk: (i, kk))
        b_spec = (pl.BlockSpec((None, tk, tn), lambda i, j, kk: (j, kk, 0)) if b3
                  else pl.BlockSpec((tk, tn), lambda i, j, kk: (kk, j)))
    elif form == "nt":
        m, k = a.shape
        n = b.shape[1] if b3 else b.shape[0]
        dn = (((1,), (1,)), ((), ()))
        a_spec = pl.BlockSpec((tm, tk), lambda i, j, kk: (i, kk))
        b_spec = (pl.BlockSpec((None, tn, tk), lambda i, j, kk: (kk, j, 0)) if b3
                  else pl.BlockSpec((tn, tk), lambda i, j, kk: (j, kk)))
    else:
        k, m = a.shape
        n = b.shape[1]
        dn = (((0,), (0,)), ((), ()))
        a_spec = pl.BlockSpec((tk, tm), lambda i, j, kk: (kk, i))
        b_spec = pl.BlockSpec((tk, tn), lambda i, j, kk: (kk, j))
    assert m % tm == 0 and n % tn == 0 and k % tk == 0, (name, m, n, k, tm, tn, tk)
    nk = k // tk
    in_specs, args = [a_spec, b_spec], [a, b]
    if bias is not None:
        in_specs.append(pl.BlockSpec((1, tn), lambda i, j, kk: (0, j)))
        args.append(bias)
    if add is not None:
        in_specs.append(pl.BlockSpec((tm, tn), lambda i, j, kk: (i, j)))
        args.append(add)
    if out_sharded:
        out_shape = _sds((n // tn, m, tn), out_dtype)
        out_spec = pl.BlockSpec((None, tm, tn), lambda i, j, kk: (j, i, 0))
    else:
        out_shape = _sds((m, n), out_dtype)
        out_spec = pl.BlockSpec((tm, tn), lambda i, j, kk: (i, j))

    def body(*refs):
        a_ref, b_ref = refs[0], refs[1]
        pos = 2
        bias_ref = add_ref = None
        if bias is not None:
            bias_ref, pos = refs[pos], pos + 1
        if add is not None:
            add_ref, pos = refs[pos], pos + 1
        o_ref = refs[pos]
        av, bv = a_ref[...], b_ref[...]
        if av.dtype != BF16:
            av = av.astype(BF16)
        if bv.dtype != BF16:
            bv = bv.astype(BF16)
        p = lax.dot_general(av, bv, dn, preferred_element_type=F32)

        def finish(acc):
            if bias_ref is not None:
                acc = acc + bias_ref[...]
            if add_ref is not None:
                acc = acc + add_ref[...]
            o_ref[...] = acc.astype(o_ref.dtype)

        if nk == 1:
            finish(p)
        else:
            acc_ref = refs[pos + 1]
            kk = pl.program_id(2)

            @pl.when(kk == 0)
            def _():
                acc_ref[...] = p

            @pl.when(kk > 0)
            def _():
                acc_ref[...] += p

            @pl.when(kk == nk - 1)
            def _():
                finish(acc_ref[...])

    res = _call(body, grid=(m // tm, n // tn, nk), in_specs=in_specs, out_specs=[out_spec], out_shape=[out_shape],
                scratch_shapes=[pltpu.VMEM((tm, tn), F32)] if nk > 1 else [],
                sem=("parallel", "parallel", "arbitrary"), name=name, args=args, carried=carried)
    return res[0] if carried is None else (res[0], res[1:])


def _rowcall(fn, rows, consts, row_outs, acc_outs, *, name, tm=ROW_TILE, col_grid=1):
    n_rows = rows[0][0].shape[0]
    assert n_rows % tm == 0
    grid = (col_grid, n_rows // tm)
    in_specs = [pl.BlockSpec((tm, w), functools.partial(lambda c, i, cb: (i, cb + c), cb=cb)) for _, w, cb in rows]
    in_specs += [pl.BlockSpec(k.shape, functools.partial(lambda c, i, nd: (0,) * nd, nd=k.ndim)) for k in consts]
    out_specs = [pl.BlockSpec((tm, w), lambda c, i: (i, c)) for _, _, _, w in row_outs]
    out_specs += [pl.BlockSpec((r, w), lambda c, i: (0, c)) for r, _, w in acc_outs]
    out_shape = [_sds((nr, nc), dt) for nr, nc, dt, _ in row_outs] + [_sds((r, nc), F32) for r, nc, _ in acc_outs]
    n_in, n_ro = len(rows) + len(consts), len(row_outs)

    def body(*refs):
        res = fn(*[r[...] for r in refs[:n_in]])
        if not isinstance(res, (tuple, list)):
            res = (res,)
        outs = refs[n_in:]
        for o_ref, val in zip(outs[:n_ro], res[:n_ro]):
            o_ref[...] = val.astype(o_ref.dtype)
        if acc_outs:
            first = pl.program_id(1) == 0

            @pl.when(first)
            def _():
                for o_ref, val in zip(outs[n_ro:], res[n_ro:]):
                    o_ref[...] = val

            @pl.when(jnp.logical_not(first))
            def _():
                for o_ref, val in zip(outs[n_ro:], res[n_ro:]):
                    o_ref[...] += val

    out = pl.pallas_call(
        body, grid=grid, in_specs=in_specs, out_specs=out_specs, out_shape=out_shape,
        compiler_params=_params(("arbitrary", "arbitrary")), name=name,
    )(*[r[0] for r in rows], *consts)
    return out


def _colsum(v):
    return jnp.sum(v, axis=0, keepdims=True)


def _sigmoid(v):
    return 1.0 / (1.0 + jnp.exp(-v))


_GELU_C = math.sqrt(2.0 / math.pi)


def _gelu(v):
    return 0.5 * v * (1.0 + jnp.tanh(_GELU_C * (v + 0.044715 * (v * v * v))))


def _gelu_and_grad(v):
    th = jnp.tanh(_GELU_C * (v + 0.044715 * (v * v * v)))
    g = 0.5 * v * (1.0 + th)
    dg = 0.5 * (1.0 + th) + 0.5 * v * (1.0 - th * th) * (_GELU_C * (1.0 + 3.0 * 0.044715 * (v * v)))
    return g, dg


def _rms_stats(v):
    r = lax.rsqrt(jnp.mean(v * v, axis=-1, keepdims=True) + EPS)
    return v * r, r


def _rms_bwd(dn, vn, r):
    return r * (dn - vn * jnp.mean(dn * vn, axis=-1, keepdims=True))


def _pre_norm(x, g, sc, sh, name):
    def fn(xv, gv, scv, shv):
        xn, _ = _rms_stats(xv)
        return (xn * gv) * (1.0 + scv) + shv
    return _rowcall(fn, [(x, D, 0)], [g, sc, sh], [(x.shape[0], D, BF16, D)], [], name=name)[0]


def _pre_norm_bwd(dh, x, dx_other, g, sc, name):
    def fn(dhv, xv, dov, gv, scv):
        xn, r = _rms_stats(xv)
        yn = xn * gv
        dyn = dhv * (1.0 + scv)
        dx = _rms_bwd(dyn * gv, xn, r)
        return dov + dx, _colsum(dhv), _colsum(dhv * yn), _colsum(dyn * xn)
    t = x.shape[0]
    return _rowcall(fn, [(dh, D, 0), (x, D, 0), (dx_other, D, 0)], [g, sc], [(t, D, F32, D)],
                    [(1, D, D)] * 3, name=name)


def _post_res(x, ypre, g, gt, name):
    def fn(xv, yv, gv, gtv):
        yn, _ = _rms_stats(yv)
        return xv + gtv * (yn * gv)
    return _rowcall(fn, [(x, D, 0), (ypre, D, 0)], [g, gt], [(x.shape[0], D, F32, D)], [], name=name)[0]


def _post_res_bwd(dxo, ypre, g, gt, name):
    def fn(dv, yv, gv, gtv):
        yn, r = _rms_stats(yv)
        dyn = dv * gtv
        dy = _rms_bwd(dyn * gv, yn, r)
        return dy, _colsum(dyn * yn), _colsum(dv * (yn * gv))
    t = ypre.shape[0]
    return _rowcall(fn, [(dxo, D, 0), (ypre, D, 0)], [g, gt], [(t, D, BF16, D)], [(1, D, D)] * 2, name=name)


def _ffn_tail(x1, yf, target, g, gt, name):
    def fn(xv, yv, tv, gv, gtv):
        yn, r = _rms_stats(yv)
        e = xv + gtv * (yn * gv) - tv
        dx2 = e * (1.0 / D)
        dyn = dx2 * gtv
        dy = _rms_bwd(dyn * gv, yn, r)
        return dx2, dy, _colsum(e * e) * (0.5 / D), _colsum(dyn * yn), _colsum(dx2 * (yn * gv))
    t = x1.shape[0]
    return _rowcall(fn, [(x1, D, 0), (yf, D, 0), (target, D, 0)], [g, gt], [(t, D, F32, D), (t, D, BF16, D)],
                    [(1, D, D)] * 3, name=name)


def _gate_merge(a, cb, z, name):
    def fn(av, cv, gav, gbv):
        return _sigmoid(gav) * av + _sigmoid(gbv) * cv
    t = a.shape[0]
    return _rowcall(fn, [(a, 512, 0), (cb, 512, 0), (z, 512, 5), (z, 512, 7)], [],
                    [(t, D, BF16, 512)], [], name=name, col_grid=2)[0]


def _gate_merge_bwd(dy, a, cb, z, name):
    def fn(dv, av, cv, gav, gbv):
        sa, sb = _sigmoid(gav), _sigmoid(gbv)
        dcb = dv * sb
        dga = dv * av * (sa * (1.0 - sa))
        dgb = dv * cv * (sb * (1.0 - sb))
        return dv * sa, dcb, dga, dgb, _colsum(dcb), _colsum(dga), _colsum(dgb)
    t = a.shape[0]
    return _rowcall(fn, [(dy, 512, 0), (a, 512, 0), (cb, 512, 0), (z, 512, 5), (z, 512, 7)], [],
                    [(t, D, BF16, 512)] * 4, [(1, D, 512)] * 3, name=name, col_grid=2)


CONV_HALO = 32


def _layer_norm_parts(u):
    mu = jnp.mean(u, axis=-1, keepdims=True)
    d = u - mu
    r = lax.rsqrt(jnp.mean(d * d, axis=-1, keepdims=True) + EPS)
    return d * r, r


LANES = 128
SUBLANE_ROWS = 8
CONV_ROWS = 64


def _lanes(c):
    return slice(c * LANES, (c + 1) * LANES)


def _conv_branch(z, w_dw, b_dw, g_ln, b_ln, name, tm=ROW_TILE):
    t = z.shape[0]
    per = tm // CONV_HALO
    n_chunks = 512 // LANES

    def body(ga_ref, gb_ref, gah_ref, gbh_ref, w_ref, b_ref, g_ref, bl_ref, u1_ref, u3_ref, scr):
        i = pl.program_id(0)
        u0h = jnp.where(i > 0, gah_ref[...] * _sigmoid(gbh_ref[...]), 0.0)
        u0 = ga_ref[...] * _sigmoid(gb_ref[...])
        for c in range(n_chunks):
            scr[c, 0:CONV_HALO, :] = u0h[:, _lanes(c)]
            scr[c, CONV_HALO:CONV_HALO + tm, :] = u0[:, _lanes(c)]
        for c in range(n_chunks):
            for r0 in range(0, tm, CONV_ROWS):
                acc = jnp.zeros((CONV_ROWS, LANES), F32) + b_ref[:, _lanes(c)]
                for j in range(CONV_K):
                    acc = acc + w_ref[j:j + 1, _lanes(c)] * scr[c, pl.ds(r0 + CONV_HALO - (CONV_K - 1) + j, CONV_ROWS), :]
                u1_ref[r0:r0 + CONV_ROWS, _lanes(c)] = acc
        xh, _ = _layer_norm_parts(u1_ref[...])
        u2 = xh * g_ref[...] + bl_ref[...]
        u3_ref[...] = (u2 * _sigmoid(u2)).astype(BF16)

    cur = lambda cb: pl.BlockSpec((tm, 512), lambda i: (i, cb))
    halo = lambda cb: pl.BlockSpec((CONV_HALO, 512), lambda i: (jnp.maximum(i * per - 1, 0), cb))
    whole = lambda a: pl.BlockSpec(a.shape, lambda i: (0, 0))
    return pl.pallas_call(
        body, grid=(t // tm,),
        in_specs=[cur(3), cur(4), halo(3), halo(4), whole(w_dw), whole(b_dw), whole(g_ln), whole(b_ln)],
        out_specs=[pl.BlockSpec((tm, 512), lambda i: (i, 0))] * 2,
        out_shape=[_sds((t, 512), F32), _sds((t, 512), BF16)],
        scratch_shapes=[pltpu.VMEM((n_chunks, CONV_HALO + tm, LANES), F32)],
        compiler_params=_params(("arbitrary",)), name=name,
    )(z, z, z, z, w_dw, b_dw, g_ln, b_ln)


def _conv_branch_bwd(du3, u1, z, w_dw, g_ln, b_ln, name, tm=ROW_TILE, carried=None):
    t = z.shape[0]
    per = tm // CONV_HALO
    last = t // tm - 1
    n_chunks = 512 // LANES

    def du1_of(du3v, u1v, g, b):
        xh, r = _layer_norm_parts(u1v)
        u2 = xh * g + b
        s = _sigmoid(u2)
        du2 = du3v * (s * (1.0 + u2 * (1.0 - s)))
        dxh = du2 * g
        du1 = r * (dxh - jnp.mean(dxh, axis=-1, keepdims=True) - xh * jnp.mean(dxh * xh, axis=-1, keepdims=True))
        return du1, du2, xh

    def body(d_ref, u_ref, dn_ref, un_ref, ga_ref, gb_ref, gah_ref, gbh_ref, w_ref, g_ref, bl_ref,
             dglu_ref, dw_ref, dbdw_ref, dg_ref, dbl_ref, dbin_ref, scr, scd):
        i = pl.program_id(0)
        g, b = g_ref[...], bl_ref[...]
        du1, du2, xh = du1_of(d_ref[...], u_ref[...], g, b)
        du1n, _, _ = du1_of(dn_ref[...], un_ref[...], g, b)
        du1n = jnp.where(i < last, du1n, 0.0)
        sgb = _sigmoid(gb_ref[...])
        ga = ga_ref[...]
        u0 = ga * sgb
        u0h = jnp.where(i > 0, gah_ref[...] * _sigmoid(gbh_ref[...]), 0.0)
        for c in range(n_chunks):
            scd[c, 0:tm, :] = du1[:, _lanes(c)]
            scd[c, tm:tm + CONV_HALO, :] = du1n[:, _lanes(c)]
            scr[c, 0:CONV_HALO, :] = u0h[:, _lanes(c)]
            scr[c, CONV_HALO:CONV_HALO + tm, :] = u0[:, _lanes(c)]

        @pl.when(i == 0)
        def _():
            for ref in (dw_ref, dbdw_ref, dg_ref, dbl_ref, dbin_ref):
                ref[...] = jnp.zeros_like(ref)

        dsg = ga * (sgb * (1.0 - sgb))
        for c in range(n_chunks):
            gate = slice(512 + c * LANES, 512 + (c + 1) * LANES)
            for r0 in range(0, tm, CONV_ROWS):
                rows = slice(r0, r0 + CONV_ROWS)
                du0 = jnp.zeros((CONV_ROWS, LANES), F32)
                for j in range(CONV_K):
                    du0 = du0 + w_ref[j:j + 1, _lanes(c)] * scd[c, pl.ds(r0 + CONV_K - 1 - j, CONV_ROWS), :]
                dga = du0 * sgb[rows, _lanes(c)]
                dgb = du0 * dsg[rows, _lanes(c)]
                dglu_ref[rows, _lanes(c)] = dga.astype(BF16)
                dglu_ref[rows, gate] = dgb.astype(BF16)
                dbin_ref[:, _lanes(c)] += _colsum(dga)
                dbin_ref[:, gate] += _colsum(dgb)
            for j in range(CONV_K):
                dwj = jnp.zeros((SUBLANE_ROWS, LANES), F32)
                for r0 in range(0, tm, CONV_ROWS):
                    prod = (scd[c, pl.ds(r0, CONV_ROWS), :]
                            * scr[c, pl.ds(r0 + CONV_HALO - (CONV_K - 1) + j, CONV_ROWS), :])
                    dwj = dwj + jnp.sum(prod.reshape(CONV_ROWS // SUBLANE_ROWS, SUBLANE_ROWS, LANES), axis=0)
                dw_ref[j:j + 1, _lanes(c)] += _colsum(dwj)
        dbdw_ref[...] += _colsum(du1)
        dg_ref[...] += _colsum(du2 * xh)
        dbl_ref[...] += _colsum(du2)

    cur = lambda cb: pl.BlockSpec((tm, 512), lambda i: (i, cb))
    prev = lambda cb: pl.BlockSpec((CONV_HALO, 512), lambda i: (jnp.maximum(i * per - 1, 0), cb))
    nxt = pl.BlockSpec((CONV_HALO, 512), lambda i: (jnp.minimum((i + 1) * per, t // CONV_HALO - 1), 0))
    whole = lambda a: pl.BlockSpec(a.shape, lambda i: (0, 0))
    acc = lambda r, w: pl.BlockSpec((r, w), lambda i: (0, 0))
    res = _call(
        body, grid=(t // tm,),
        in_specs=[cur(0), cur(0), nxt, nxt, cur(3), cur(4), prev(3), prev(4), whole(w_dw), whole(g_ln), whole(b_ln)],
        out_specs=[pl.BlockSpec((tm, 1024), lambda i: (i, 0)), acc(CONV_K, 512), acc(1, 512), acc(1, 512),
                   acc(1, 512), acc(1, 1024)],
        out_shape=[_sds((t, 1024), BF16), _sds((CONV_K, 512), F32), _sds((1, 512), F32), _sds((1, 512), F32),
                   _sds((1, 512), F32), _sds((1, 1024), F32)],
        scratch_shapes=[pltpu.VMEM((n_chunks, CONV_HALO + tm, LANES), F32),
                        pltpu.VMEM((n_chunks, tm + CONV_HALO, LANES), F32)],
        sem=("arbitrary",), name=name, args=(du3, u1, du3, u1, z, z, z, z, w_dw, g_ln, b_ln), carried=carried)
    return res[:6] if carried is None else (res[:6], res[6:])


FF_BLOCK = D_FF // 2
FF_HALO = 8
FF_CHUNKS = FF_BLOCK // LANES


def _ffn_conv(w_ref, b_ref, scr, k, rows):
    acc = b_ref[:, _lanes(k)] + w_ref[0:1, _lanes(k)] * scr[k, pl.ds(FF_HALO - 2, rows), :]
    acc = acc + w_ref[1:2, _lanes(k)] * scr[k, pl.ds(FF_HALO - 1, rows), :]
    return acc + w_ref[2:3, _lanes(k)] * scr[k, pl.ds(FF_HALO, rows), :]


def _ffn_act(up, w3, b3, name, tm=ROW_TILE):
    t = up.shape[0]
    per = tm // FF_HALO
    wide = 2 * FF_BLOCK

    def body(u_ref, uh_ref, w_ref, b_ref, o_ref, scr):
        i = pl.program_id(1)
        for k in range(2 * FF_CHUNKS):
            scr[k, 0:FF_HALO, :] = jnp.where(i > 0, uh_ref[:, _lanes(k)], 0.0)
            scr[k, FF_HALO:FF_HALO + tm, :] = u_ref[:, _lanes(k)]
        for cc in range(FF_CHUNKS):
            val = _ffn_conv(w_ref, b_ref, scr, cc, tm)
            gate = _ffn_conv(w_ref, b_ref, scr, FF_CHUNKS + cc, tm)
            o_ref[:, _lanes(cc)] = (_gelu(gate) * val).astype(BF16)

    return pl.pallas_call(
        body, grid=(2, t // tm),
        in_specs=[pl.BlockSpec((tm, wide), lambda c, i: (i, c)),
                  pl.BlockSpec((FF_HALO, wide), lambda c, i: (jnp.maximum(i * per - 1, 0), c)),
                  pl.BlockSpec((FFN_K, wide), lambda c, i: (0, c)),
                  pl.BlockSpec((1, wide), lambda c, i: (0, c))],
        out_specs=pl.BlockSpec((tm, FF_BLOCK), lambda c, i: (i, c)),
        out_shape=_sds((t, D_FF), BF16),
        scratch_shapes=[pltpu.VMEM((2 * FF_CHUNKS, FF_HALO + tm, LANES), F32)],
        compiler_params=_params(("arbitrary", "arbitrary")), name=name,
    )(up, up, w3, b3)


def _ffn_act_bwd(dact, up, w3, b3, name, tm=ROW_TILE):
    t = up.shape[0]
    per = tm // FF_HALO
    wide = 2 * FF_BLOCK
    last = t // tm - 1
    ext = tm + FF_HALO

    def body(u_ref, up_ref, un_ref, d_ref, dn_ref, w_ref, b_ref, o_ref, dw_ref, db_ref, scr, scd):
        i = pl.program_id(1)
        for k in range(2 * FF_CHUNKS):
            scr[k, 0:FF_HALO, :] = jnp.where(i > 0, up_ref[:, _lanes(k)], 0.0)
            scr[k, FF_HALO:FF_HALO + tm, :] = u_ref[:, _lanes(k)]
            scr[k, FF_HALO + tm:FF_HALO + ext, :] = un_ref[:, _lanes(k)]
        dn = jnp.where(i < last, dn_ref[...], 0.0)

        @pl.when(i == 0)
        def _():
            dw_ref[...] = jnp.zeros_like(dw_ref)
            db_ref[...] = jnp.zeros_like(db_ref)

        for cc in range(FF_CHUNKS):
            val = _ffn_conv(w_ref, b_ref, scr, cc, ext)
            gel, dgel = _gelu_and_grad(_ffn_conv(w_ref, b_ref, scr, FF_CHUNKS + cc, ext))
            da = jnp.concatenate([d_ref[:, _lanes(cc)], dn[:, _lanes(cc)]], axis=0)
            scd[cc] = da * gel
            scd[FF_CHUNKS + cc] = da * val * dgel
            for k in (cc, FF_CHUNKS + cc):
                shifted = [scd[k, pl.ds(FFN_K - 1 - j, tm), :] for j in range(FFN_K)]
                ucur = scr[k, pl.ds(FF_HALO, tm), :]
                o_ref[:, _lanes(k)] = (w_ref[0:1, _lanes(k)] * shifted[0] + w_ref[1:2, _lanes(k)] * shifted[1]
                                       + w_ref[2:3, _lanes(k)] * shifted[2]).astype(BF16)
                for j in range(FFN_K):
                    dw_ref[j:j + 1, _lanes(k)] += _colsum(shifted[j] * ucur)
                db_ref[:, _lanes(k)] += _colsum(shifted[FFN_K - 1])

    nblk = t // FF_HALO
    return pl.pallas_call(
        body, grid=(2, t // tm),
        in_specs=[pl.BlockSpec((tm, wide), lambda c, i: (i, c)),
                  pl.BlockSpec((FF_HALO, wide), lambda c, i: (jnp.maximum(i * per - 1, 0), c)),
                  pl.BlockSpec((FF_HALO, wide), lambda c, i: (jnp.minimum((i + 1) * per, nblk - 1), c)),
                  pl.BlockSpec((tm, FF_BLOCK), lambda c, i: (i, c)),
                  pl.BlockSpec((FF_HALO, FF_BLOCK), lambda c, i: (jnp.minimum((i + 1) * per, nblk - 1), c)),
                  pl.BlockSpec((FFN_K, wide), lambda c, i: (0, c)),
                  pl.BlockSpec((1, wide), lambda c, i: (0, c))],
        out_specs=[pl.BlockSpec((tm, wide), lambda c, i: (i, c)),
                   pl.BlockSpec((FFN_K, wide), lambda c, i: (0, c)),
                   pl.BlockSpec((1, wide), lambda c, i: (0, c))],
        out_shape=[_sds((t, 2 * D_FF), BF16), _sds((FFN_K, 2 * D_FF), F32), _sds((1, 2 * D_FF), F32)],
        scratch_shapes=[pltpu.VMEM((2 * FF_CHUNKS, FF_HALO + ext, LANES), F32),
                        pltpu.VMEM((2 * FF_CHUNKS, ext, LANES), F32)],
        compiler_params=_params(("arbitrary", "arbitrary")), name=name,
    )(up, up, up, dact, dact, w3, b3)


def _toeplitz_map():
    f = np.zeros((TOEP, REL_PAD), np.float32)
    for m in range(TOEP - 1):
        rel = (WINDOW - 1) - m
        f[m, int(np.clip(rel, -MAX_REL, MAX_REL)) + MAX_REL] = 1.0
    return f


def _split3(v):
    hi = v.astype(BF16)
    r1 = v - hi.astype(F32)
    mid = r1.astype(BF16)
    lo = (r1 - mid.astype(F32)).astype(BF16)
    return hi, mid, lo


def _exact_select(v, sel):
    out = None
    for part in _split3(v):
        p = jnp.dot(part, sel, preferred_element_type=F32)
        out = p if out is None else out + p
    return out


def _select_call(v, sel, name):
    def body(v_ref, s_ref, o_ref):
        o_ref[...] = _exact_select(v_ref[...], s_ref[...])
    return pl.pallas_call(body, out_shape=_sds((v.shape[0], sel.shape[1]), F32), name=name)(v, sel)


def _band_bias(gen_row):
    b0 = jnp.broadcast_to(gen_row, (Q_TILE, TOEP))
    bias = pltpu.roll(b0, TOEP - 255, 1, stride=1, stride_axis=0)[:, :WINDOW]
    qq = lax.broadcasted_iota(jnp.int32, (Q_TILE, WINDOW), 0) // CHUNK
    kc = lax.broadcasted_iota(jnp.int32, (Q_TILE, WINDOW), 1) // CHUNK
    return jnp.where((kc >= qq) & (kc <= qq + LEFT_CHUNKS), bias, NEG_INF)


PAD_ROWS = WINDOW - Q_TILE
NT_DIMS = (((1,), (1,)), ((), ()))
TN_DIMS = (((0,), (0,)), ((), ()))


def _head_mask(hh):
    lane = lax.broadcasted_iota(jnp.int32, (1, 128), 1)
    return (lane < 64) if hh == 0 else (lane >= 64)


def _probs(qm, kw, bias, i):
    s = lax.dot_general(qm, kw, NT_DIMS, preferred_element_type=F32) + bias
    col = lax.broadcasted_iota(jnp.int32, (Q_TILE, WINDOW), 1)
    s = jnp.where(col >= PAD_ROWS - Q_TILE * i, s, NEG_INF)
    p = jnp.exp(s - jnp.max(s, axis=-1, keepdims=True))
    return p / jnp.sum(p, axis=-1, keepdims=True)


def _attention(z, gen, name, carried=None):
    t = z.shape[0]
    n_i = t // Q_TILE

    def body(q_ref, k_ref, v_ref, g_ref, o_ref, kpad, vpad, bias):
        hp, i = pl.program_id(0), pl.program_id(1)

        @pl.when(i == 0)
        def _():
            kpad[0:PAD_ROWS, :] = jnp.zeros((PAD_ROWS, 128), BF16)
            vpad[0:PAD_ROWS, :] = jnp.zeros((PAD_ROWS, 128), BF16)
            kpad[PAD_ROWS:PAD_ROWS + t, :] = k_ref[...].astype(BF16)
            vpad[PAD_ROWS:PAD_ROWS + t, :] = v_ref[...].astype(BF16)
            for hh in range(2):
                bias[hh] = _band_bias(g_ref[pl.ds(2 * hp + hh, 1), :])

        start = pl.multiple_of(i * Q_TILE, Q_TILE)
        kw = kpad[pl.ds(start, WINDOW), :]
        vw = vpad[pl.ds(start, WINDOW), :]
        q = q_ref[...] * (CHUNK ** -0.5)
        out = None
        for hh in range(2):
            mask = _head_mask(hh)
            p = _probs(jnp.where(mask, q, 0.0).astype(BF16), kw, bias[hh], i)
            o = jnp.dot(p.astype(BF16), vw, preferred_element_type=F32)
            out = jnp.where(mask, o, 0.0) if out is None else jnp.where(mask, o, out)
        o_ref[...] = out.astype(BF16)

    res = _call(
        body, grid=(4, n_i),
        in_specs=[pl.BlockSpec((Q_TILE, 128), lambda h, i: (i, h)),
                  pl.BlockSpec((t, 128), lambda h, i: (0, 4 + h)),
                  pl.BlockSpec((t, 128), lambda h, i: (0, 8 + h)),
                  pl.BlockSpec((N_HEADS, TOEP), lambda h, i: (0, 0))],
        out_specs=[pl.BlockSpec((Q_TILE, 128), lambda h, i: (i, h))],
        out_shape=[_sds((t, 512), BF16)],
        scratch_shapes=[pltpu.VMEM((PAD_ROWS + t, 128), BF16), pltpu.VMEM((PAD_ROWS + t, 128), BF16),
                        pltpu.VMEM((2, Q_TILE, WINDOW), F32)],
        sem=("arbitrary", "arbitrary"), name=name, args=(z, z, z, gen), carried=carried)
    return res[0] if carried is None else (res[0], res[1:])


def _attention_bwd(z, datt, gen, name, carried=None):
    t = z.shape[0]
    n_i = t // Q_TILE

    def body(q_ref, k_ref, v_ref, d_ref, g_ref, dq_ref, dk_ref, dv_ref, sq_ref, sk_ref, sv_ref, dg_ref,
             kpad, vpad, dkacc, dvacc, bias, dsacc):
        hp, i = pl.program_id(0), pl.program_id(1)

        @pl.when(i == 0)
        def _():
            kpad[0:PAD_ROWS, :] = jnp.zeros((PAD_ROWS, 128), BF16)
            vpad[0:PAD_ROWS, :] = jnp.zeros((PAD_ROWS, 128), BF16)
            kpad[PAD_ROWS:PAD_ROWS + t, :] = k_ref[...].astype(BF16)
            vpad[PAD_ROWS:PAD_ROWS + t, :] = v_ref[...].astype(BF16)
            dkacc[...] = jnp.zeros_like(dkacc)
            dvacc[...] = jnp.zeros_like(dvacc)
            dsacc[...] = jnp.zeros_like(dsacc)
            for hh in range(2):
                bias[hh] = _band_bias(g_ref[pl.ds(2 * hp + hh, 1), :])

        start = pl.multiple_of(i * Q_TILE, Q_TILE)
        win = pl.ds(start, WINDOW)
        kw = kpad[win, :]
        vw = vpad[win, :]
        q = q_ref[...] * (CHUNK ** -0.5)
        do = d_ref[...]
        dq = None
        for hh in range(2):
            mask = _head_mask(hh)
            qm = jnp.where(mask, q, 0.0).astype(BF16)
            dom = jnp.where(mask, do, 0.0).astype(BF16)
            p = _probs(qm, kw, bias[hh], i)
            dp = lax.dot_general(dom, vw, NT_DIMS, preferred_element_type=F32)
            ds = p * (dp - jnp.sum(p * dp, axis=-1, keepdims=True))
            dsacc[hh] += ds
            ds16 = ds.astype(BF16)
            dqh = jnp.dot(ds16, kw, preferred_element_type=F32) * (CHUNK ** -0.5)
            dq = jnp.where(mask, dqh, 0.0) if dq is None else jnp.where(mask, dqh, dq)
            dkacc[win, :] += lax.dot_general(ds16, qm, TN_DIMS, preferred_element_type=F32)
            dvacc[win, :] += lax.dot_general(p.astype(BF16), dom, TN_DIMS, preferred_element_type=F32)
        dq_ref[...] = dq.astype(BF16)

        @pl.when(i == 0)
        def _():
            sq_ref[...] = _colsum(dq)

        @pl.when(i > 0)
        def _():
            sq_ref[...] += _colsum(dq)

        @pl.when(i == n_i - 1)
        def _():
            dk = dkacc[PAD_ROWS:PAD_ROWS + t, :]
            dv = dvacc[PAD_ROWS:PAD_ROWS + t, :]
            dk_ref[...] = dk.astype(BF16)
            dv_ref[...] = dv.astype(BF16)
            sk_ref[...] = _colsum(dk)
            sv_ref[...] = _colsum(dv)
            rr = lax.broadcasted_iota(jnp.int32, (Q_TILE, Q_TILE), 0)
            cc = lax.broadcasted_iota(jnp.int32, (Q_TILE, Q_TILE), 1)
            rev = jnp.where(rr + cc == Q_TILE - 1, 1.0, 0.0).astype(BF16)
            for hh in range(2):
                acc = None
                for part in _split3(dsacc[hh]):
                    pr = jnp.dot(rev, part, preferred_element_type=F32)
                    acc = pr if acc is None else acc + pr
                wide = jnp.concatenate([acc, jnp.zeros((Q_TILE, TOEP - WINDOW), F32)], axis=1)
                dg_ref[pl.ds(2 * hp + hh, 1), :] = _colsum(pltpu.roll(wide, 0, 1, stride=1, stride_axis=0))

    col = lambda off: pl.BlockSpec((t, 128), lambda h, i: (0, off + h))
    tile = lambda: pl.BlockSpec((Q_TILE, 128), lambda h, i: (i, h))
    sums = lambda: pl.BlockSpec((1, 128), lambda h, i: (0, h))
    res = _call(
        body, grid=(4, n_i),
        in_specs=[tile(), col(4), col(8), tile(), pl.BlockSpec((N_HEADS, TOEP), lambda h, i: (0, 0))],
        out_specs=[tile(), col(0), col(0), sums(), sums(), sums(), pl.BlockSpec((N_HEADS, TOEP), lambda h, i: (0, 0))],
        out_shape=[_sds((t, 512), BF16)] * 3 + [_sds((1, 512), F32)] * 3 + [_sds((N_HEADS, TOEP), F32)],
        scratch_shapes=[pltpu.VMEM((PAD_ROWS + t, 128), BF16), pltpu.VMEM((PAD_ROWS + t, 128), BF16),
                        pltpu.VMEM((PAD_ROWS + t, 128), F32), pltpu.VMEM((PAD_ROWS + t, 128), F32),
                        pltpu.VMEM((2, Q_TILE, WINDOW), F32), pltpu.VMEM((2, Q_TILE, WINDOW), F32)],
        sem=("arbitrary", "arbitrary"), name=name, args=(z, z, z, datt, gen), carried=carried)
    return res[:7] if carried is None else (res[:7], res[7:])


def _adamw_math(w, g, m, v):
    m = ADAM_B1 * m + (1.0 - ADAM_B1) * g
    v = ADAM_B2 * v + (1.0 - ADAM_B2) * (g * g)
    m_hat = m / (1.0 - ADAM_B1 ** ADAM_STEP)
    v_hat = v / (1.0 - ADAM_B2 ** ADAM_STEP)
    delta = -ADAM_LR * (m_hat / (jnp.sqrt(v_hat) + ADAM_EPS) + ADAM_WD * w)
    return delta, m, v


def _adamw_many(items, name):
    n = len(items)

    def body(*refs):
        ins, outs = refs[:4 * n], refs[4 * n:]
        for k in range(n):
            w, g, m, v = (r[...] for r in ins[4 * k:4 * k + 4])
            outs[3 * k][...], outs[3 * k + 1][...], outs[3 * k + 2][...] = _adamw_math(w, g, m, v)

    flat = [a for item in items for a in item]
    res = pl.pallas_call(body, out_shape=[_sds(item[0].shape, F32) for item in items for _ in range(3)],
                         name=name)(*flat)
    return [tuple(res[3 * k:3 * k + 3]) for k in range(n)]


def _adamw(w, g, m, v, name):
    r, c = w.shape
    tm = next(cand for cand in (256, 176, 128, 64, 32, 16, 8) if r % cand == 0)
    return _rowcall(_adamw_math, [(w, c, 0), (g, c, 0), (m, c, 0), (v, c, 0)], [],
                    [(r, c, F32, c)] * 3, [], name=name, tm=tm)


def _ada_fwd(c_all, w_shard, b_shard, name):
    n = w_shard.shape[1]
    tn = 512

    def body(c_ref, w_ref, b_ref, o_ref, a_ref):
        cv = c_ref[...]
        act = cv * _sigmoid(cv)
        a_ref[...] = act
        o_ref[...] = jnp.dot(act.astype(BF16), w_ref[...].astype(BF16), preferred_element_type=F32) + b_ref[...]

    return pl.pallas_call(
        body, grid=(n // tn,),
        in_specs=[pl.BlockSpec((8, D), lambda j: (0, 0)), pl.BlockSpec((D, tn), lambda j: (0, j)),
                  pl.BlockSpec((1, tn), lambda j: (0, j))],
        out_specs=[pl.BlockSpec((8, tn), lambda j: (0, j)), pl.BlockSpec((8, D), lambda j: (0, 0))],
        out_shape=[_sds((8, n), F32), _sds((8, D), F32)],
        compiler_params=_params(("arbitrary",)), name=name,
    )(c_all, w_shard, b_shard)


def _ada_bwd_adamw(act_t, dmod_shard, w, m, v, name):
    r, c = w.shape
    tm = 256

    def body(a_ref, d_ref, w_ref, m_ref, v_ref, g_ref, dl_ref, nm_ref, nv_ref):
        g = jnp.dot(a_ref[...], d_ref[...], precision=lax.Precision.HIGHEST, preferred_element_type=F32)
        g_ref[...] = g
        dl_ref[...], nm_ref[...], nv_ref[...] = _adamw_math(w_ref[...], g, m_ref[...], v_ref[...])

    blk = pl.BlockSpec((tm, c), lambda i: (i, 0))
    return pl.pallas_call(
        body, grid=(r // tm,),
        in_specs=[pl.BlockSpec((tm, 8), lambda i: (i, 0)), pl.BlockSpec((8, c), lambda i: (0, 0)), blk, blk, blk],
        out_specs=[blk] * 4, out_shape=[_sds((r, c), F32)] * 4,
        compiler_params=_params(("arbitrary",)), name=name,
    )(act_t, dmod_shard, w, m, v)


def _place():
    return lax.axis_index("x"), lax.axis_index("y"), lax.axis_index("c")


def _flip(v, bit):
    return 1 - v if bit else v


VMEM_SPEC = pl.BlockSpec(memory_space=pltpu.VMEM)


def _allgather8(v, name):
    r, c = v.shape

    def body(v_ref, g_ref, tot_ref, send_sems, recv_sems, local_sem):
        x, y, cc = _place()
        me = 4 * x + 2 * y + cc
        mine = pltpu.make_async_copy(v_ref, g_ref.at[me], local_sem)
        mine.start()
        sends = []
        for k in range(1, 8):
            peer = (_flip(x, k & 4), _flip(y, k & 2), _flip(cc, k & 1))
            cp = pltpu.make_async_remote_copy(src_ref=v_ref, dst_ref=g_ref.at[me], send_sem=send_sems.at[k - 1],
                                              recv_sem=recv_sems.at[k - 1], device_id=peer, device_id_type=MESH)
            cp.start()
            sends.append(cp)
        for k in range(1, 8):
            peer = (_flip(x, k & 4), _flip(y, k & 2), _flip(cc, k & 1))
            theirs = g_ref.at[4 * peer[0] + 2 * peer[1] + peer[2]]
            pltpu.make_async_remote_copy(src_ref=v_ref, dst_ref=theirs, send_sem=send_sems.at[k - 1],
                                         recv_sem=recv_sems.at[k - 1], device_id=peer, device_id_type=MESH).wait_recv()
        for cp in sends:
            cp.wait_send()
        mine.wait()
        tot = g_ref[0]
        for d in range(1, 8):
            tot = tot + g_ref[d]
        tot_ref[...] = tot

    return pl.pallas_call(
        body, in_specs=[VMEM_SPEC], out_specs=[VMEM_SPEC, VMEM_SPEC],
        out_shape=[_sds((8, r, c), F32), _sds((r, c), F32)],
        scratch_shapes=[pltpu.SemaphoreType.DMA((7,)), pltpu.SemaphoreType.DMA((7,)), pltpu.SemaphoreType.DMA],
        compiler_params=pltpu.CompilerParams(vmem_limit_bytes=VMEM_LIMIT), name=name,
    )(v)


def _slot(px, py, swapped):
    return 2 * py + px if swapped else 2 * px + py


def _gather_shards(arrs, swapped, name, in_place=False):
    n = len(arrs)

    def body(*refs):
        ins, outs = refs[:n], refs[n:2 * n]
        send1, recv1, send2, recv2, local_sems = refs[2 * n:]
        x, y, c = _place()
        sibling = (x, y, 1 - c)
        chips = [(_flip(x, k & 2), _flip(y, k & 1)) for k in (1, 2, 3)]
        local_copies, sends = [], []
        for a in range(n):
            h = outs[a].shape[1] // 2
            mine = pl.ds(pl.multiple_of(c * h, 8), h)
            own = _slot(x, y, swapped[a])
            if in_place:
                src = outs[a].at[own, mine]
            else:
                src = ins[a].at[mine]
                lc = pltpu.make_async_copy(ins[a], outs[a].at[own], local_sems.at[a])
                lc.start()
                local_copies.append(lc)
            for j, (px, py) in enumerate(chips):
                cp = pltpu.make_async_remote_copy(
                    src_ref=src, dst_ref=outs[a].at[own, mine], send_sem=send1.at[3 * a + j],
                    recv_sem=recv1.at[3 * a + j], device_id=(px, py, c), device_id_type=MESH)
                cp.start()
                sends.append(cp)
        for a in range(n):
            h = outs[a].shape[1] // 2
            mine = pl.ds(pl.multiple_of(c * h, 8), h)
            for j, (px, py) in enumerate(chips):
                piece = outs[a].at[_slot(px, py, swapped[a]), mine]
                pltpu.make_async_remote_copy(
                    src_ref=piece, dst_ref=piece, send_sem=send1.at[3 * a + j], recv_sem=recv1.at[3 * a + j],
                    device_id=(px, py, c), device_id_type=MESH).wait_recv()
                fwd = pltpu.make_async_remote_copy(
                    src_ref=piece, dst_ref=piece, send_sem=send2.at[3 * a + j], recv_sem=recv2.at[3 * a + j],
                    device_id=sibling, device_id_type=MESH)
                fwd.start()
                sends.append(fwd)
        for a in range(n):
            h = outs[a].shape[1] // 2
            other = pl.ds(pl.multiple_of((1 - c) * h, 8), h)
            for j, (px, py) in enumerate(chips):
                piece = outs[a].at[_slot(px, py, swapped[a]), other]
                pltpu.make_async_remote_copy(
                    src_ref=piece, dst_ref=piece, send_sem=send2.at[3 * a + j], recv_sem=recv2.at[3 * a + j],
                    device_id=sibling, device_id_type=MESH).wait_recv()
        for cp in sends:
            cp.wait_send()
        for lc in local_copies:
            lc.wait()

    dma = lambda k: pltpu.SemaphoreType.DMA((k,))
    return pl.pallas_call(
        body, in_specs=[ANY] * n, out_specs=[ANY] * n,
        out_shape=[_sds(a.shape if in_place else (4,) + a.shape, a.dtype) for a in arrs],
        scratch_shapes=[dma(3 * n), dma(3 * n), dma(3 * n), dma(3 * n), dma(n)],
        input_output_aliases={a: a for a in range(n)} if in_place else {},
        name=name,
    )(*arrs)


def _carry_pair_exchange(grads):
    n = len(grads)

    def copies(ins, outs, send_sems, recv_sems):
        x, y, c = _place()
        cps = []
        for a in range(n):
            h = ins[a].shape[1] // 2
            theirs = pl.ds(pl.multiple_of((1 - c) * h, 8), h)
            cps.append(pltpu.make_async_remote_copy(
                src_ref=ins[a].at[:, theirs, :], dst_ref=outs[a], send_sem=send_sems.at[a], recv_sem=recv_sems.at[a],
                device_id=(x, y, 1 - c), device_id_type=MESH))
        return cps

    def start(*refs):
        for cp in copies(*refs):
            cp.start()

    def finish(*refs):
        for cp in copies(*refs):
            cp.wait()

    return _Carried(grads, [_sds((4, g.shape[1] // 2, g.shape[2]), F32) for g in grads], {}, n, start, finish)


def _pair_sum(grad, recv, core, name):
    _, r, c = grad.shape
    h = r // 2

    def body(core_ref, g_ref, r_ref, o_ref):
        o_ref[...] = (g_ref[...] + r_ref[...]).astype(BF16)

    return pl.pallas_call(
        body,
        grid_spec=pltpu.PrefetchScalarGridSpec(
            num_scalar_prefetch=1, grid=(4,),
            in_specs=[pl.BlockSpec((None, h, c), lambda s, core_ref: (s, core_ref[0], 0)),
                      pl.BlockSpec((None, h, c), lambda s, core_ref: (s, 0, 0))],
            out_specs=pl.BlockSpec((None, h, c), lambda s, core_ref: (s, 0, 0))),
        out_shape=_sds((4, h, c), BF16), compiler_params=_params(("arbitrary",)), name=name,
    )(core, grad, recv)


def _carry_chip_exchange(parts, swapped):
    n = len(parts)

    def copies(ins, outs, send_sems, recv_sems):
        x, y, c = _place()
        chips = [(_flip(x, k & 2), _flip(y, k & 1)) for k in (1, 2, 3)]
        cps = []
        for a in range(n):
            for j, (px, py) in enumerate(chips):
                cps.append(pltpu.make_async_remote_copy(
                    src_ref=ins[a].at[_slot(px, py, swapped[a])], dst_ref=outs[a].at[j],
                    send_sem=send_sems.at[3 * a + j], recv_sem=recv_sems.at[3 * a + j],
                    device_id=(px, py, c), device_id_type=MESH))
        return cps

    def start(*refs):
        for cp in copies(*refs):
            cp.start()

    def finish(*refs):
        for cp in copies(*refs):
            cp.wait()

    return _Carried(parts, [_sds((3,) + p.shape[1:], BF16) for p in parts], {}, 3 * n, start, finish)


def _chip_sum(part, recv, slot_core, name):
    _, h, c = part.shape

    def body(sc_ref, p_ref, r_ref, o_ref):
        acc = p_ref[...].astype(F32)
        for j in range(3):
            acc = acc + r_ref[j].astype(F32)
        o_ref[...] = acc

    return pl.pallas_call(
        body,
        grid_spec=pltpu.PrefetchScalarGridSpec(
            num_scalar_prefetch=1, grid=(1,),
            in_specs=[pl.BlockSpec((None, h, c), lambda s, sc_ref: (sc_ref[0], 0, 0)),
                      pl.BlockSpec((3, h, c), lambda s, sc_ref: (0, 0, 0))],
            out_specs=pl.BlockSpec((h, c), lambda s, sc_ref: (sc_ref[1], 0))),
        out_shape=_sds((2 * h, c), F32), compiler_params=_params(("arbitrary",)), name=name,
    )(slot_core, part, recv)


def _carry_pair_share(shards):
    n = len(shards)

    def copies(outs, send_sems, recv_sems, mine):
        x, y, c = _place()
        cps = []
        for a in range(n):
            h = outs[a].shape[0] // 2
            half = outs[a].at[pl.ds(pl.multiple_of((c if mine else 1 - c) * h, 8), h)]
            cps.append(pltpu.make_async_remote_copy(
                src_ref=half, dst_ref=half, send_sem=send_sems.at[a], recv_sem=recv_sems.at[a],
                device_id=(x, y, 1 - c), device_id_type=MESH))
        return cps

    def start(ins, outs, send_sems, recv_sems):
        for cp in copies(outs, send_sems, recv_sems, True):
            cp.start()

    def finish(ins, outs, send_sems, recv_sems):
        for cp in copies(outs, send_sems, recv_sems, False):
            cp.wait_recv()
        for cp in copies(outs, send_sems, recv_sems, True):
            cp.wait_send()

    return _Carried(shards, [_sds(s.shape, F32) for s in shards], {a: a for a in range(n)}, n, start, finish)


def _carry_gather_ici(bufs, swapped):
    n = len(bufs)

    def copies(outs, send_sems, recv_sems, sending):
        x, y, c = _place()
        cps = []
        for a in range(n):
            h = outs[a].shape[1] // 2
            mine = pl.ds(pl.multiple_of(c * h, 8), h)
            for j, k in enumerate((1, 2, 3)):
                px, py = _flip(x, k & 2), _flip(y, k & 1)
                slot = _slot(x, y, swapped[a]) if sending else _slot(px, py, swapped[a])
                piece = outs[a].at[slot, mine]
                cps.append(pltpu.make_async_remote_copy(
                    src_ref=piece, dst_ref=piece, send_sem=send_sems.at[3 * a + j], recv_sem=recv_sems.at[3 * a + j],
                    device_id=(px, py, c), device_id_type=MESH))
        return cps

    def start(ins, outs, send_sems, recv_sems):
        for cp in copies(outs, send_sems, recv_sems, True):
            cp.start()

    def finish(ins, outs, send_sems, recv_sems):
        for cp in copies(outs, send_sems, recv_sems, False):
            cp.wait_recv()
        for cp in copies(outs, send_sems, recv_sems, True):
            cp.wait_send()

    return _Carried(bufs, [_sds(b.shape, b.dtype) for b in bufs], {a: a for a in range(n)}, 3 * n, start, finish)


def _carry_gather_forward(bufs, swapped):
    n = len(bufs)

    def copies(outs, send_sems, recv_sems, sending):
        x, y, c = _place()
        cps = []
        for a in range(n):
            h = outs[a].shape[1] // 2
            rows = pl.ds(pl.multiple_of((c if sending else 1 - c) * h, 8), h)
            for j, k in enumerate((1, 2, 3)):
                piece = outs[a].at[_slot(_flip(x, k & 2), _flip(y, k & 1), swapped[a]), rows]
                cps.append(pltpu.make_async_remote_copy(
                    src_ref=piece, dst_ref=piece, send_sem=send_sems.at[3 * a + j], recv_sem=recv_sems.at[3 * a + j],
                    device_id=(x, y, 1 - c), device_id_type=MESH))
        return cps

    def start(ins, outs, send_sems, recv_sems):
        for cp in copies(outs, send_sems, recv_sems, True):
            cp.start()

    def finish(ins, outs, send_sems, recv_sems):
        for cp in copies(outs, send_sems, recv_sems, False):
            cp.wait_recv()
        for cp in copies(outs, send_sems, recv_sems, True):
            cp.wait_send()

    return _Carried(bufs, [_sds(b.shape, b.dtype) for b in bufs], {a: a for a in range(n)}, 3 * n, start, finish)


def _pack(arrs, rows_multiple=8):
    parts, offs, row = [], [], 0
    for a in arrs:
        flat = a.reshape(-1)
        nrow = -(-flat.shape[0] // D)
        parts.append(jnp.pad(flat, (0, nrow * D - flat.shape[0])))
        offs.append(row)
        row += nrow
    total = -(-row // rows_multiple) * rows_multiple
    if total > row:
        parts.append(jnp.zeros(((total - row) * D,), F32))
    return jnp.concatenate(parts).reshape(total, D), offs


def _unpack(packed, offs, shapes):
    out = []
    for off, shp in zip(offs, shapes):
        size = int(np.prod(shp))
        nrow = -(-size // D)
        out.append(packed[off:off + nrow].reshape(-1)[:size].reshape(shp))
    return out


def _to_bf16_slot(w, slot, name):
    r, c = w.shape
    tm = next(cand for cand in (256, 176, 128, 64, 32, 16) if r % cand == 0)

    def body(slot_ref, w_ref, o_ref):
        o_ref[...] = w_ref[...].astype(BF16)

    return pl.pallas_call(
        body,
        grid_spec=pltpu.PrefetchScalarGridSpec(
            num_scalar_prefetch=1, grid=(r // tm,),
            in_specs=[pl.BlockSpec((tm, c), lambda i, slot_ref: (i, 0))],
            out_specs=pl.BlockSpec((None, tm, c), lambda i, slot_ref: (slot_ref[0], i, 0))),
        out_shape=_sds((4, r, c), BF16), compiler_params=_params(("arbitrary",)), name=name,
    )(slot, w)


def _unshard_cols(g):
    s, k, n = g.shape
    return jnp.transpose(g, (1, 0, 2)).reshape(k, s * n)


def _ff_swap(v):
    b = FF_BLOCK
    return jnp.concatenate([v[..., 0:b], v[..., 2 * b:3 * b], v[..., b:2 * b], v[..., 3 * b:4 * b]], axis=-1)


LATE = ("attn_o", "conv_o", "mix_o", "up", "down")
EARLY_GRADS = ("down", "up", "mix_o", "attn_o", "conv_o")


def _weight_views(bufs):
    return {"up": bufs["up"], "attn_o": _unshard_cols(bufs["attn_o"]), "conv_o": _unshard_cols(bufs["conv_o"]),
            "mix_o": bufs["mix_o"].reshape(D, D), "down": bufs["down"].reshape(D_FF, D)}


def _pair_sums(names, grads, recv, dist):
    return [_pair_sum(g, r, dist["core"], "pair_sum_" + n) for n, g, r in zip(names, grads, recv)]


def _reduce_halves(names, parts, from_chips, dist):
    return [_chip_sum(p, r, jnp.concatenate([dist["slots"][SWAPPED[n]], dist["core"]]), "chip_sum_" + n)
            for n, p, r in zip(names, parts, from_chips)]


def _local_step(x, target, mod, w_in, late, small, dist=None):
    sh_m, sc_m, gt_m, sh_f, sc_f, gt_f = mod
    t = x.shape[0]
    tmm = min(1024, t)
    late_swapped = [SWAPPED[n] for n in LATE]

    h1 = _pre_norm(x, small["g_pre_mix"], sc_m, sh_m, "pre_norm_mix")
    z = _matmul(h1, w_in, form="nn", out_dtype=F32, tm=tmm, tn=1152, tk=D, bias=small["b_in"], name="mm_in")
    if dist is None:
        att = _attention(z, small["gen"], "attention")
        bufs = late
    else:
        att, landed = _attention(z, small["gen"], "attention",
                                 carried=_carry_gather_ici([late[n] for n in LATE], late_swapped))
        bufs = dict(zip(LATE, _run_carried(_carry_gather_forward(landed, late_swapped), "gather_forward")))
    w = _weight_views(bufs)
    w["in"] = w_in
    a = _matmul(att, w["attn_o"], form="nn", out_dtype=F32, tm=tmm, tn=512, tk=512, name="mm_attn_o")
    u1, u3 = _conv_branch(z, small["w_dw_conv"], small["b_dw_conv"], small["g_conv_ln"], small["b_conv_ln"], "conv_branch")
    cb = _matmul(u3, w["conv_o"], form="nn", out_dtype=F32, tm=tmm, tn=512, tk=512, bias=small["b_conv_o"], name="mm_conv_o")
    y = _gate_merge(a, cb, z, "gate_merge")
    ym = _matmul(y, w["mix_o"], form="nn", out_dtype=F32, tm=tmm, tn=512, tk=D, name="mm_mix_o")
    x1 = _post_res(x, ym, small["g_post_mix"], gt_m, "post_res_mix")
    h2 = _pre_norm(x1, small["g_pre_ffn"], sc_f, sh_f, "pre_norm_ffn")
    up = _matmul(h2, w["up"], form="nn", out_dtype=F32, tm=tmm, tn=FF_BLOCK, tk=D, name="mm_up")
    act = _ffn_act(up, small["w_dw_ffn"], small["b_dw_ffn"], "ffn_act")
    yf = _matmul(act, w["down"], form="nn", out_dtype=F32, tm=tmm, tn=512, tk=D_FF, name="mm_down")

    dx2, dyf, loss_cols, d_g_post_ffn, d_gt_f = _ffn_tail(x1, yf, target, small["g_post_ffn"], gt_f, "ffn_tail")
    dact = _matmul(dyf, w["down"], form="nt", out_dtype=F32, tm=tmm, tn=FF_BLOCK, tk=D, name="mm_down_dx")
    g_down = _matmul(act, dyf, form="tn", out_dtype=F32, tm=FF_BLOCK, tn=D, tk=tmm, name="mm_down_dw")
    dup, d_w_dw_ffn, d_b_dw_ffn = _ffn_act_bwd(dact, up, small["w_dw_ffn"], small["b_dw_ffn"], "ffn_act_bwd")
    dh2 = _matmul(dup, w["up"], form="nt", out_dtype=F32, tm=tmm, tn=D, tk=FF_BLOCK, name="mm_up_dx")
    g_up = _matmul(h2, dup, form="tn", out_dtype=F32, tm=D, tn=FF_BLOCK, tk=tmm, out_sharded=True, name="mm_up_dw")
    dx1, d_sh_f, d_sc_f, d_g_pre_ffn = _pre_norm_bwd(dh2, x1, dx2, small["g_pre_ffn"], sc_f, "pre_norm_ffn_bwd")
    dym, d_g_post_mix, d_gt_m = _post_res_bwd(dx1, ym, small["g_post_mix"], gt_m, "post_res_mix_bwd")
    dy = _matmul(dym, w["mix_o"], form="nt", out_dtype=F32, tm=tmm, tn=512, tk=D, name="mm_mix_o_dx")
    g_mix_o = _matmul(y, dym, form="tn", out_dtype=F32, tm=D, tn=512, tk=tmm, name="mm_mix_o_dw")
    da, dcb, dgate_a, dgate_b, d_b_conv_o, sga, sgb = _gate_merge_bwd(dy, a, cb, z, "gate_merge_bwd")
    datt = _matmul(da, w["attn_o"], form="nt", out_dtype=F32, tm=tmm, tn=512, tk=D, name="mm_attn_o_dx")
    g_attn_o = _matmul(att, da, form="tn", out_dtype=F32, tm=512, tn=256, tk=tmm, out_sharded=True, name="mm_attn_o_dw")
    du3 = _matmul(dcb, w["conv_o"], form="nt", out_dtype=F32, tm=tmm, tn=512, tk=D, name="mm_conv_o_dx")
    g_conv_o = _matmul(u3, dcb, form="tn", out_dtype=F32, tm=512, tn=256, tk=tmm, out_sharded=True, name="mm_conv_o_dw")
    big = {"attn_o": g_attn_o, "conv_o": g_conv_o, "mix_o": g_mix_o.reshape(4, 256, D),
           "up": g_up, "down": g_down.reshape(4, D_FF // 4, D)}
    conv_bwd = (du3, u1, z, small["w_dw_conv"], small["g_conv_ln"], small["b_conv_ln"], "conv_branch_bwd")
    in_dw = dict(form="tn", out_dtype=F32, tm=D, tn=1152, tk=tmm, out_sharded=True, name="mm_in_dw")
    in_dx = dict(form="nt", out_dtype=F32, tm=tmm, tn=D, tk=1152, name="mm_in_dx")
    if dist is None:
        dglu, d_w_dw_conv, d_b_dw_conv, d_g_conv_ln, d_b_conv_ln, sglu = _conv_branch_bwd(*conv_bwd)
        dq, dk, dv, sq, sk, sv, dgen = _attention_bwd(z, datt, small["gen"], "attention_bwd")
        dz = jnp.concatenate([dq, dk, dv, dglu, dgate_a, dgate_b], axis=1)
        big["in"] = _matmul(h1, dz, **in_dw)
        dh1 = _matmul(dz, w_in, **in_dx)
    else:
        early = [big[n] for n in EARLY_GRADS]
        (dglu, d_w_dw_conv, d_b_dw_conv, d_g_conv_ln, d_b_conv_ln, sglu), recv = _conv_branch_bwd(
            *conv_bwd, carried=_carry_pair_exchange(early))
        parts = _pair_sums(EARLY_GRADS, early, recv, dist)
        (dq, dk, dv, sq, sk, sv, dgen), from_chips = _attention_bwd(
            z, datt, small["gen"], "attention_bwd",
            carried=_carry_chip_exchange(parts, [SWAPPED[n] for n in EARLY_GRADS]))
        halves = _reduce_halves(EARLY_GRADS, parts, from_chips, dist)
        dz = jnp.concatenate([dq, dk, dv, dglu, dgate_a, dgate_b], axis=1)
        g_in, shards = _matmul(h1, dz, carried=_carry_pair_share(halves), **in_dw)
        big = dict(zip(EARLY_GRADS, shards))
        recv_in = _run_carried(_carry_pair_exchange([g_in]), "pair_exchange_in")
        part_in = _pair_sums(("in",), [g_in], recv_in, dist)
        dh1, from_chips_in = _matmul(dz, w_in, carried=_carry_chip_exchange(part_in, [False]), **in_dx)
        half_in = _reduce_halves(("in",), part_in, from_chips_in, dist)
        big["in"] = _run_carried(_carry_pair_share(half_in), "pair_share_in")[0]
    d_b_in = jnp.concatenate([sq, sk, sv, sglu, sga, sgb], axis=1)
    grad_x, d_sh_m, d_sc_m, d_g_pre_mix = _pre_norm_bwd(dh1, x, dx1, small["g_pre_mix"], sc_m, "pre_norm_mix_bwd")

    dmod = [d_sh_m, d_sc_m, d_gt_m, d_sh_f, d_sc_f, d_gt_f]
    sm = {"g_pre_mix": d_g_pre_mix, "g_post_mix": d_g_post_mix, "b_in": d_b_in, "gen": dgen,
          "w_dw_conv": d_w_dw_conv, "b_dw_conv": d_b_dw_conv, "g_conv_ln": d_g_conv_ln, "b_conv_ln": d_b_conv_ln,
          "b_conv_o": d_b_conv_o, "g_pre_ffn": d_g_pre_ffn, "g_post_ffn": d_g_post_ffn,
          "w_dw_ffn": d_w_dw_ffn, "b_dw_ffn": d_b_dw_ffn}
    return loss_cols, grad_x, dmod, big, sm


BIG = ("in", "attn_o", "conv_o", "mix_o", "up", "down")
SWAPPED = {"in": False, "attn_o": False, "conv_o": False, "mix_o": False, "up": True, "down": False}
SMALL_ORDER = ("b_ada", "g_pre_mix", "g_post_mix", "b_in", "rel_bias", "b_dw_conv", "g_conv_ln", "b_conv_ln",
               "b_conv_o", "g_pre_ffn", "g_post_ffn", "b_dw_ffn", "w_dw_conv", "w_dw_ffn")


def kernel(x, c, w_ada, b_ada, g_pre_mix, g_post_mix, w_in, b_in, rel_bias, w_attn_o, w_dw_conv, b_dw_conv, g_conv_ln, b_conv_ln, w_conv_o, b_conv_o, w_mix_o, g_pre_ffn, g_post_ffn, w_up, w_dw_ffn, b_dw_ffn, w_down, loss_target, m_w_ada, m_b_ada, m_g_pre_mix, m_g_post_mix, m_w_in, m_b_in, m_rel_bias, m_w_attn_o, m_w_dw_conv, m_b_dw_conv, m_g_conv_ln, m_b_conv_ln, m_w_conv_o, m_b_conv_o, m_w_mix_o, m_g_pre_ffn, m_g_post_ffn, m_w_up, m_w_dw_ffn, m_b_dw_ffn, m_w_down, v_w_ada, v_b_ada, v_g_pre_mix, v_g_post_mix, v_w_in, v_b_in, v_rel_bias, v_w_attn_o, v_w_dw_conv, v_b_dw_conv, v_g_conv_ln, v_b_conv_ln, v_w_conv_o, v_b_conv_o, v_w_mix_o, v_g_pre_ffn, v_g_post_ffn, v_w_up, v_w_dw_ffn, v_b_dw_ffn, v_w_down):
    given = dict(locals())
    ax, ay, ac = lax.axis_index("x"), lax.axis_index("y"), lax.axis_index("c")
    shard = 2 * ax + ay
    me = 4 * ax + 2 * ay + ac
    xs, target = x[0], loss_target[0]

    c_pad = jnp.pad(c, ((0, 7), (0, 0)))
    c_g, _ = _allgather8(c_pad, "gather_c")
    c_all = c_g[:, 0, :]
    b_ada_shard = lax.dynamic_slice(b_ada, (0, shard * 1536), (1, 1536))
    mod_shard, c_act = _ada_fwd(c_all, w_ada[0], b_ada_shard, "ada_fwd")
    small_in = [jnp.pad(mod_shard, ((0, 8), (0, 0))),
                jnp.pad(w_dw_conv[0], ((0, 1), (0, 0))),
                jnp.pad(w_dw_ffn[0], ((0, 13), (0, 0)))]
    mod_g, wdc_g, wdf_g = _gather_shards(small_in, [False, False, True], "gather_small")
    mod_all = jnp.transpose(mod_g[:, :8, :], (1, 0, 2)).reshape(8, 6 * D)
    mod_row = lax.dynamic_slice(mod_all, (me, 0), (1, 6 * D))
    mod = [mod_row[:, k * D:(k + 1) * D] for k in range(6)]

    slots = {sw: _slot(ax, ay, sw).astype(jnp.int32).reshape(1) for sw in (False, True)}
    own = {n: _to_bf16_slot(given["w_" + n][0], slots[SWAPPED[n]], "cast_" + n) for n in BIG}
    w_in_all = _gather_shards([own["in"]], [False], "gather_w_in", in_place=True)[0]
    core = ac.astype(jnp.int32).reshape(1)
    dist = {"core": core, "slots": slots}

    sel = jnp.asarray(_toeplitz_map())
    rel_pad = jnp.pad(rel_bias[0], ((0, 0), (0, REL_PAD - (2 * MAX_REL + 1))))
    gen = _select_call(rel_pad, sel.T.astype(BF16), "bias_rows")
    small = {"g_pre_mix": g_pre_mix, "g_post_mix": g_post_mix, "b_in": b_in, "gen": gen,
             "w_dw_conv": _unshard_cols(wdc_g[:, :CONV_K, :]), "b_dw_conv": b_dw_conv, "g_conv_ln": g_conv_ln,
             "b_conv_ln": b_conv_ln, "b_conv_o": b_conv_o, "g_pre_ffn": g_pre_ffn, "g_post_ffn": g_post_ffn,
             "w_dw_ffn": _unshard_cols(wdf_g[:, :FFN_K, :]), "b_dw_ffn": _ff_swap(b_dw_ffn)}

    loss_cols, grad_x, dmod, reduced, sm = _local_step(xs, target, mod, w_in_all, {n: own[n] for n in LATE}, small, dist)
    loss = lax.psum(jnp.sum(loss_cols), ("x", "y", "c"))

    d_rel = _select_call(sm["gen"], sel.astype(BF16), "bias_fold")[:, :2 * MAX_REL + 1]
    small_grads = {"g_pre_mix": sm["g_pre_mix"], "g_post_mix": sm["g_post_mix"], "b_in": sm["b_in"], "rel_bias": d_rel[None],
                   "b_dw_conv": sm["b_dw_conv"], "g_conv_ln": sm["g_conv_ln"], "b_conv_ln": sm["b_conv_ln"],
                   "b_conv_o": sm["b_conv_o"], "g_pre_ffn": sm["g_pre_ffn"], "g_post_ffn": sm["g_post_ffn"],
                   "b_dw_ffn": _ff_swap(sm["b_dw_ffn"]), "w_dw_conv": sm["w_dw_conv"], "w_dw_ffn": _ff_swap(sm["w_dw_ffn"])}
    order = [n for n in SMALL_ORDER if n != "b_ada"]
    packed, offs = _pack([jnp.concatenate(dmod, axis=1)] + [small_grads[n] for n in order])
    every, total = _allgather8(packed, "gather_small_grads")
    dmod_all = every[:, 0:6, :].reshape(8, 6 * D)
    full_shapes = {n: given[n].shape for n in order}
    full_shapes["w_dw_conv"], full_shapes["w_dw_ffn"] = (1, CONV_K, 512), (1, FFN_K, 2 * D_FF)
    sums = dict(zip(order, _unpack(total, offs[1:], [full_shapes[n] for n in order])))
    sums["b_ada"] = total[0:6].reshape(1, 6 * D)
    sums["w_dw_conv"] = lax.dynamic_slice(sums["w_dw_conv"], (0, 0, shard * 128), (1, CONV_K, 128))
    sums["w_dw_ffn"] = lax.dynamic_slice(sums["w_dw_ffn"], (0, 0, shard * FF_BLOCK), (1, FFN_K, FF_BLOCK))

    upd = dict(zip(SMALL_ORDER, _adamw_many(
        [(given[n], sums[n], given["m_" + n], given["v_" + n]) for n in SMALL_ORDER], "adamw_small")))

    dmod_shard = lax.dynamic_slice(dmod_all, (0, shard * 1536), (8, 1536))
    ada = _ada_bwd_adamw(c_act.T, dmod_shard, w_ada[0], m_w_ada[0], v_w_ada[0], "ada_bwd_adamw")

    out = {"grad_w_ada": ada[0][None], "delta_w_ada": ada[1][None], "new_m_w_ada": ada[2][None], "new_v_w_ada": ada[3][None]}
    for n in BIG:
        g = reduced[n]
        dl, nm, nv = _adamw(given["w_" + n][0], g, given["m_w_" + n][0], given["v_w_" + n][0], "adamw_" + n)
        out["grad_w_" + n], out["delta_w_" + n], out["new_m_w_" + n], out["new_v_w_" + n] = g[None], dl[None], nm[None], nv[None]
    for n in SMALL_ORDER:
        out["grad_" + n], out["delta_" + n], out["new_m_" + n], out["new_v_" + n] = sums[n], *upd[n]

    weights = ["w_ada", "b_ada", "g_pre_mix", "g_post_mix", "w_in", "b_in", "rel_bias", "w_attn_o", "w_dw_conv", "b_dw_conv",
               "g_conv_ln", "b_conv_ln", "w_conv_o", "b_conv_o", "w_mix_o", "g_pre_ffn", "g_post_ffn", "w_up", "w_dw_ffn",
               "b_dw_ffn", "w_down"]
    return (loss, grad_x[None], *[out["grad_" + n] for n in weights], *[out["delta_" + n] for n in weights],
            *[out["new_m_" + n] for n in weights], *[out["new_v_" + n] for n in weights])
```

```python
import functools
import math

import numpy as np
import jax
import jax.numpy as jnp
from jax import lax
from jax.experimental import pallas as pl
from jax.experimental.pallas import tpu as pltpu

F32, BF16 = jnp.float32, jnp.bfloat16
MESH = pl.DeviceIdType.MESH

D = 1024
D_IN = 4608
D_FF = 2816
CONV_K = 31
FFN_K = 3
N_HEADS = 8
CHUNK = 64
LEFT_CHUNKS = 8
MAX_REL = 128
EPS = 1e-6
NEG_INF = -1e30
Q_TILE = 256
WINDOW = Q_TILE + LEFT_CHUNKS * CHUNK
REL_PAD = 384
TOEP = 1024
ROW_TILE = 256
VMEM_LIMIT = 60 * 1024 * 1024

ADAM_LR, ADAM_B1, ADAM_B2, ADAM_EPS, ADAM_WD, ADAM_STEP = 0.001, 0.9, 0.999, 1e-08, 0.01, 10


def _params(sem=None):
    return pltpu.CompilerParams(dimension_semantics=sem, vmem_limit_bytes=VMEM_LIMIT)


def _sds(shape, dtype):
    return jax.ShapeDtypeStruct(tuple(shape), dtype)


ANY = pl.BlockSpec(memory_space=pl.ANY)


class _Carried:
    def __init__(self, ins, out_shapes, aliases, n_sems, start, finish):
        self.ins, self.out_shapes, self.aliases = list(ins), list(out_shapes), dict(aliases)
        self.n_sems, self.start, self.finish = n_sems, start, finish


def _call(body, *, grid, in_specs, out_specs, out_shape, scratch_shapes, sem, name, args, carried=None):
    in_specs, out_specs, out_shape = list(in_specs), list(out_specs), list(out_shape)
    scratch_shapes = list(scratch_shapes)
    if carried is None:
        return pl.pallas_call(body, grid=grid, in_specs=in_specs, out_specs=out_specs, out_shape=out_shape,
                              scratch_shapes=scratch_shapes, compiler_params=_params(sem), name=name)(*args)
    n_in, n_out, n_scr = len(in_specs), len(out_specs), len(scratch_shapes)
    c_in, c_out = len(carried.ins), len(carried.out_shapes)

    def full(*refs):
        pos = [0]

        def take(k):
            part = refs[pos[0]:pos[0] + k]
            pos[0] += k
            return part

        ins, cins, outs, couts, scr = take(n_in), take(c_in), take(n_out), take(c_out), take(n_scr)
        send_sems, recv_sems = take(2)
        first = last = None
        for d, size in enumerate(grid):
            pid = pl.program_id(d)
            first = (pid == 0) if first is None else first & (pid == 0)
            last = (pid == size - 1) if last is None else last & (pid == size - 1)

        @pl.when(first)
        def _():
            carried.start(cins, couts, send_sems, recv_sems)

        body(*ins, *outs, *scr)

        @pl.when(last)
        def _():
            carried.finish(cins, couts, send_sems, recv_sems)

    sems = [pltpu.SemaphoreType.DMA((carried.n_sems,)), pltpu.SemaphoreType.DMA((carried.n_sems,))]
    return pl.pallas_call(
        full, grid=grid, in_specs=in_specs + [ANY] * c_in, out_specs=out_specs + [ANY] * c_out,
        out_shape=out_shape + carried.out_shapes, scratch_shapes=scratch_shapes + sems,
        input_output_aliases={n_in + k: n_out + v for k, v in carried.aliases.items()},
        compiler_params=_params(tuple("arbitrary" for _ in grid)), name=name,
    )(*args, *carried.ins)


def _run_carried(carried, name):
    c_in = len(carried.ins)

    def body(*refs):
        cins, couts = refs[:c_in], refs[c_in:c_in + len(carried.out_shapes)]
        send_sems, recv_sems = refs[-2:]
        carried.start(cins, couts, send_sems, recv_sems)
        carried.finish(cins, couts, send_sems, recv_sems)

    return pl.pallas_call(
        body, in_specs=[ANY] * c_in, out_specs=[ANY] * len(carried.out_shapes), out_shape=carried.out_shapes,
        scratch_shapes=[pltpu.SemaphoreType.DMA((carried.n_sems,)), pltpu.SemaphoreType.DMA((carried.n_sems,))],
        input_output_aliases=carried.aliases, name=name,
    )(*carried.ins)


def _matmul(a, b, *, form, out_dtype, tm, tn, tk, name, bias=None, add=None, out_sharded=False, carried=None):
    b3 = b.ndim == 3
    if form == "nn":
        m, k = a.shape
        n = b.shape[0] * b.shape[2] if b3 else b.shape[1]
        dn = (((1,), (0,)), ((), ()))
        a_spec = pl.BlockSpec((tm, tk), lambda i, j, kk: (i, kk))
        b_spec = (pl.BlockSpec((None, tk, tn), lambda i, j, kk: (j, kk, 0)) if b3
                  else pl.BlockSpec((tk, tn), lambda i, j, kk: (kk, j)))
    elif form == "nt":
        m, k = a.shape
        n = b.shape[1] if b3 else b.shape[0]
        dn = (((1,), (1,)), ((), ()))
        a_spec = pl.BlockSpec((tm, tk), lambda i, j, kk: (i, kk))
        b_spec = (pl.BlockSpec((None, tn, tk), lambda i, j, kk: (kk, j, 0)) if b3
                  else pl.BlockSpec((tn, tk), lambda i, j, kk: (j, kk)))
    else:
        k, m = a.shape
        n = b.shape[1]
        dn = (((0,), (0,)), ((), ()))
        a_spec = pl.BlockSpec((tk, tm), lambda i, j, kk: (kk, i))
        b_spec = pl.BlockSpec((tk, tn), lambda i, j, kk: (kk, j))
    assert m % tm == 0 and n % tn == 0 and k % tk == 0, (name, m, n, k, tm, tn, tk)
    nk = k // tk
    in_specs, args = [a_spec, b_spec], [a, b]
    if bias is not None:
        in_specs.append(pl.BlockSpec((1, tn), lambda i, j, kk: (0, j)))
        args.append(bias)
    if add is not None:
        in_specs.append(pl.BlockSpec((tm, tn), lambda i, j, kk: (i, j)))
        args.append(add)
    if out_sharded:
        out_shape = _sds((n // tn, m, tn), out_dtype)
        out_spec = pl.BlockSpec((None, tm, tn), lambda i, j, kk: (j, i, 0))
    else:
        out_shape = _sds((m, n), out_dtype)
        out_spec = pl.BlockSpec((tm, tn), lambda i, j, kk: (i, j))

    def body(*refs):
        a_ref, b_ref = refs[0], refs[1]
        pos = 2
        bias_ref = add_ref = None
        if bias is not None:
            bias_ref, pos = refs[pos], pos + 1
        if add is not None:
            add_ref, pos = refs[pos], pos + 1
        o_ref = refs[pos]
        av, bv = a_ref[...], b_ref[...]
        if av.dtype != BF16:
            av = av.astype(BF16)
        if bv.dtype != BF16:
            bv = bv.astype(BF16)
        p = lax.dot_general(av, bv, dn, preferred_element_type=F32)

        def finish(acc):
            if bias_ref is not None:
                acc = acc + bias_ref[...]
            if add_ref is not None:
                acc = acc + add_ref[...]
            o_ref[...] = acc.astype(o_ref.dtype)

        if nk == 1:
            finish(p)
        else:
            acc_ref = refs[pos + 1]
            kk = pl.program_id(2)

            @pl.when(kk == 0)
            def _():
                acc_ref[...] = p

            @pl.when(kk > 0)
            def _():
                acc_ref[...] += p

            @pl.when(kk == nk - 1)
            def _():
                finish(acc_ref[...])

    res = _call(body, grid=(m // tm, n // tn, nk), in_specs=in_specs, out_specs=[out_spec], out_shape=[out_shape],
                scratch_shapes=[pltpu.VMEM((tm, tn), F32)] if nk > 1 else [],
                sem=("parallel", "parallel", "arbitrary"), name=name, args=args, carried=carried)
    return res[0] if carried is None else (res[0], res[1:])


def _rowcall(fn, rows, consts, row_outs, acc_outs, *, name, tm=ROW_TILE, col_grid=1):
    n_rows = rows[0][0].shape[0]
    assert n_rows % tm == 0
    grid = (col_grid, n_rows // tm)
    in_specs = [pl.BlockSpec((tm, w), functools.partial(lambda c, i, cb: (i, cb + c), cb=cb)) for _, w, cb in rows]
    in_specs += [pl.BlockSpec(k.shape, functools.partial(lambda c, i, nd: (0,) * nd, nd=k.ndim)) for k in consts]
    out_specs = [pl.BlockSpec((tm, w), lambda c, i: (i, c)) for _, _, _, w in row_outs]
    out_specs += [pl.BlockSpec((r, w), lambda c, i: (0, c)) for r, _, w in acc_outs]
    out_shape = [_sds((nr, nc), dt) for nr, nc, dt, _ in row_outs] + [_sds((r, nc), F32) for r, nc, _ in acc_outs]
    n_in, n_ro = len(rows) + len(consts), len(row_outs)

    def body(*refs):
        res = fn(*[r[...] for r in refs[:n_in]])
        if not isinstance(res, (tuple, list)):
            res = (res,)
        outs = refs[n_in:]
        for o_ref, val in zip(outs[:n_ro], res[:n_ro]):
            o_ref[...] = val.astype(o_ref.dtype)
        if acc_outs:
            first = pl.program_id(1) == 0

            @pl.when(first)
            def _():
                for o_ref, val in zip(outs[n_ro:], res[n_ro:]):
                    o_ref[...] = val

            @pl.when(jnp.logical_not(first))
            def _():
                for o_ref, val in zip(outs[n_ro:], res[n_ro:]):
                    o_ref[...] += val

    out = pl.pallas_call(
        body, grid=grid, in_specs=in_specs, out_specs=out_specs, out_shape=out_shape,
        compiler_params=_params(("arbitrary", "arbitrary")), name=name,
    )(*[r[0] for r in rows], *consts)
    return out


def _matmul_rows(b, *, form, tm, tk, fn, rows, consts, row_outs, acc_outs, name, a=None, a_rows=None, a_fn=None,
                 carried=None):
    b3 = b.ndim == 3
    n = b.shape[1] if (b3 or form == "nt") else b.shape[1]
    if form == "nn":
        k, n = b.shape
        b_spec = pl.BlockSpec((tk, n), lambda i, kk: (kk, 0))
        dn = (((1,), (0,)), ((), ()))
    else:
        n = b.shape[1] if b3 else b.shape[0]
        k = b.shape[0] * b.shape[2] if b3 else b.shape[1]
        b_spec = (pl.BlockSpec((None, n, tk), lambda i, kk: (kk, 0, 0)) if b3
                  else pl.BlockSpec((n, tk), lambda i, kk: (0, kk)))
        dn = (((1,), (1,)), ((), ()))
    nk = k // tk
    lhs_in = [(a, tk, 0)] if a is not None else list(a_rows)
    assert a is not None or nk == 1
    m = lhs_in[0][0].shape[0]
    n_lhs = len(lhs_in)
    in_specs = [pl.BlockSpec((tm, tk), lambda i, kk: (i, kk))] if a is not None else [
        pl.BlockSpec((tm, w), functools.partial(lambda i, kk, cb: (i, cb), cb=cb)) for _, w, cb in a_rows]
    in_specs.append(b_spec)
    in_specs += [pl.BlockSpec((tm, w), functools.partial(lambda i, kk, cb: (i, cb), cb=cb)) for _, w, cb in rows]
    in_specs += [pl.BlockSpec(c.shape, functools.partial(lambda i, kk, nd: (0,) * nd, nd=c.ndim)) for c in consts]
    out_specs = [pl.BlockSpec((tm, w), lambda i, kk: (i, 0)) for _, w in row_outs]
    out_specs += [pl.BlockSpec((r, w), lambda i, kk: (0, 0)) for r, w in acc_outs]
    out_shape = [_sds((m, w), dt) for dt, w in row_outs] + [_sds((r, w), F32) for r, w in acc_outs]
    n_rows, n_consts, n_ro, n_acc = len(rows), len(consts), len(row_outs), len(acc_outs)

    def body(*refs):
        pos = n_lhs + 1
        row_refs, const_refs = refs[pos:pos + n_rows], refs[pos + n_rows:pos + n_rows + n_consts]
        pos += n_rows + n_consts
        out_refs, acc_refs = refs[pos:pos + n_ro], refs[pos + n_ro:pos + n_ro + n_acc]
        i, kk = pl.program_id(0), pl.program_id(1)
        lhs = refs[0][...] if a is not None else a_fn(*[r[...] for r in refs[:n_lhs]]).astype(BF16)
        p = lax.dot_general(lhs, refs[n_lhs][...], dn, preferred_element_type=F32)

        def finish(acc):
            extra = [r[...] for r in row_refs] + [c[...] for c in const_refs]
            res = fn(acc, lhs, *extra) if a is None else fn(acc, *extra)
            for o_ref, val in zip(out_refs, res[:n_ro]):
                o_ref[...] = val.astype(o_ref.dtype)
            if n_acc:
                @pl.when(i == 0)
                def _():
                    for o_ref, val in zip(acc_refs, res[n_ro:]):
                        o_ref[...] = val

                @pl.when(i > 0)
                def _():
                    for o_ref, val in zip(acc_refs, res[n_ro:]):
                        o_ref[...] += val

        if nk == 1:
            finish(p)
        else:
            acc_ref = refs[pos + n_ro + n_acc]

            @pl.when(kk == 0)
            def _():
                acc_ref[...] = p

            @pl.when(kk > 0)
            def _():
                acc_ref[...] += p

            @pl.when(kk == nk - 1)
            def _():
                finish(acc_ref[...])

    res = _call(body, grid=(m // tm, nk), in_specs=in_specs, out_specs=out_specs, out_shape=out_shape,
                scratch_shapes=[pltpu.VMEM((tm, n), F32)] if nk > 1 else [], sem=("arbitrary", "arbitrary"),
                name=name, args=[r[0] for r in lhs_in] + [b] + [r[0] for r in rows] + list(consts), carried=carried)
    own = n_ro + n_acc
    return res[:own] if carried is None else (res[:own], res[own:])


def _colsum(v):
    return jnp.sum(v, axis=0, keepdims=True)


def _sigmoid(v):
    return 1.0 / (1.0 + jnp.exp(-v))


_GELU_C = math.sqrt(2.0 / math.pi)


def _gelu(v):
    return 0.5 * v * (1.0 + jnp.tanh(_GELU_C * (v + 0.044715 * (v * v * v))))


def _gelu_and_grad(v):
    th = jnp.tanh(_GELU_C * (v + 0.044715 * (v * v * v)))
    g = 0.5 * v * (1.0 + th)
    dg = 0.5 * (1.0 + th) + 0.5 * v * (1.0 - th * th) * (_GELU_C * (1.0 + 3.0 * 0.044715 * (v * v)))
    return g, dg


def _rms_stats(v):
    r = lax.rsqrt(jnp.mean(v * v, axis=-1, keepdims=True) + EPS)
    return v * r, r


def _rms_bwd(dn, vn, r):
    return r * (dn - vn * jnp.mean(dn * vn, axis=-1, keepdims=True))


def _pre_norm(x, g, sc, sh, name):
    def fn(xv, gv, scv, shv):
        xn, _ = _rms_stats(xv)
        return (xn * gv) * (1.0 + scv) + shv
    return _rowcall(fn, [(x, D, 0)], [g, sc, sh], [(x.shape[0], D, BF16, D)], [], name=name)[0]


def _pre_norm_bwd(dh, x, dx_other, g, sc, name):
    def fn(dhv, xv, dov, gv, scv):
        xn, r = _rms_stats(xv)
        yn = xn * gv
        dyn = dhv * (1.0 + scv)
        dx = _rms_bwd(dyn * gv, xn, r)
        return dov + dx, _colsum(dhv), _colsum(dhv * yn), _colsum(dyn * xn)
    t = x.shape[0]
    return _rowcall(fn, [(dh, D, 0), (x, D, 0), (dx_other, D, 0)], [g, sc], [(t, D, F32, D)],
                    [(1, D, D)] * 3, name=name)


def _post_res(x, ypre, g, gt, name):
    def fn(xv, yv, gv, gtv):
        yn, _ = _rms_stats(yv)
        return xv + gtv * (yn * gv)
    return _rowcall(fn, [(x, D, 0), (ypre, D, 0)], [g, gt], [(x.shape[0], D, F32, D)], [], name=name)[0]


def _post_res_bwd(dxo, ypre, g, gt, name):
    def fn(dv, yv, gv, gtv):
        yn, r = _rms_stats(yv)
        dyn = dv * gtv
        dy = _rms_bwd(dyn * gv, yn, r)
        return dy, _colsum(dyn * yn), _colsum(dv * (yn * gv))
    t = ypre.shape[0]
    return _rowcall(fn, [(dxo, D, 0), (ypre, D, 0)], [g, gt], [(t, D, BF16, D)], [(1, D, D)] * 2, name=name)


def _ffn_tail(x1, yf, target, g, gt, name):
    def fn(xv, yv, tv, gv, gtv):
        yn, r = _rms_stats(yv)
        e = xv + gtv * (yn * gv) - tv
        dx2 = e * (1.0 / D)
        dyn = dx2 * gtv
        dy = _rms_bwd(dyn * gv, yn, r)
        return dx2, dy, _colsum(e * e) * (0.5 / D), _colsum(dyn * yn), _colsum(dx2 * (yn * gv))
    t = x1.shape[0]
    return _rowcall(fn, [(x1, D, 0), (yf, D, 0), (target, D, 0)], [g, gt], [(t, D, F32, D), (t, D, BF16, D)],
                    [(1, D, D)] * 3, name=name)


def _gate_merge(a, cb, z, name):
    def fn(av, cv, gav, gbv):
        return _sigmoid(gav) * av + _sigmoid(gbv) * cv
    t = a.shape[0]
    return _rowcall(fn, [(a, 512, 0), (cb, 512, 0), (z, 512, 5), (z, 512, 7)], [],
                    [(t, D, BF16, 512)], [], name=name, col_grid=2)[0]


def _gate_merge_bwd(dy, a, cb, z, name):
    def fn(dv, av, cv, gav, gbv):
        sa, sb = _sigmoid(gav), _sigmoid(gbv)
        dcb = dv * sb
        dga = dv * av * (sa * (1.0 - sa))
        dgb = dv * cv * (sb * (1.0 - sb))
        return dv * sa, dcb, dga, dgb, _colsum(dcb), _colsum(dga), _colsum(dgb)
    t = a.shape[0]
    return _rowcall(fn, [(dy, 512, 0), (a, 512, 0), (cb, 512, 0), (z, 512, 5), (z, 512, 7)], [],
                    [(t, D, BF16, 512)] * 4, [(1, D, 512)] * 3, name=name, col_grid=2)


CONV_HALO = 32


def _layer_norm_parts(u):
    mu = jnp.mean(u, axis=-1, keepdims=True)
    d = u - mu
    r = lax.rsqrt(jnp.mean(d * d, axis=-1, keepdims=True) + EPS)
    return d * r, r


LANES = 128
SUBLANE_ROWS = 8
CONV_ROWS = 64


def _lanes(c):
    return slice(c * LANES, (c + 1) * LANES)


def _conv_branch(z, w_dw, b_dw, g_ln, b_ln, name, tm=ROW_TILE):
    t = z.shape[0]
    per = tm // CONV_HALO
    n_chunks = 512 // LANES

    def body(ga_ref, gb_ref, gah_ref, gbh_ref, w_ref, b_ref, g_ref, bl_ref, u1_ref, u3_ref, scr):
        i = pl.program_id(0)
        u0h = jnp.where(i > 0, gah_ref[...] * _sigmoid(gbh_ref[...]), 0.0)
        u0 = ga_ref[...] * _sigmoid(gb_ref[...])
        for c in range(n_chunks):
            scr[c, 0:CONV_HALO, :] = u0h[:, _lanes(c)]
            scr[c, CONV_HALO:CONV_HALO + tm, :] = u0[:, _lanes(c)]
        for c in range(n_chunks):
            for r0 in range(0, tm, CONV_ROWS):
                acc = jnp.zeros((CONV_ROWS, LANES), F32) + b_ref[:, _lanes(c)]
                for j in range(CONV_K):
                    acc = acc + w_ref[j:j + 1, _lanes(c)] * scr[c, pl.ds(r0 + CONV_HALO - (CONV_K - 1) + j, CONV_ROWS), :]
                u1_ref[r0:r0 + CONV_ROWS, _lanes(c)] = acc
        xh, _ = _layer_norm_parts(u1_ref[...])
        u2 = xh * g_ref[...] + bl_ref[...]
        u3_ref[...] = (u2 * _sigmoid(u2)).astype(BF16)

    cur = lambda cb: pl.BlockSpec((tm, 512), lambda i: (i, cb))
    halo = lambda cb: pl.BlockSpec((CONV_HALO, 512), lambda i: (jnp.maximum(i * per - 1, 0), cb))
    whole = lambda a: pl.BlockSpec(a.shape, lambda i: (0, 0))
    return pl.pallas_call(
        body, grid=(t // tm,),
        in_specs=[cur(3), cur(4), halo(3), halo(4), whole(w_dw), whole(b_dw), whole(g_ln), whole(b_ln)],
        out_specs=[pl.BlockSpec((tm, 512), lambda i: (i, 0))] * 2,
        out_shape=[_sds((t, 512), F32), _sds((t, 512), BF16)],
        scratch_shapes=[pltpu.VMEM((n_chunks, CONV_HALO + tm, LANES), F32)],
        compiler_params=_params(("arbitrary",)), name=name,
    )(z, z, z, z, w_dw, b_dw, g_ln, b_ln)


def _conv_branch_bwd(du3, u1, z, w_dw, g_ln, b_ln, name, tm=ROW_TILE, carried=None):
    t = z.shape[0]
    per = tm // CONV_HALO
    last = t // tm - 1
    n_chunks = 512 // LANES

    def du1_of(du3v, u1v, g, b):
        xh, r = _layer_norm_parts(u1v)
        u2 = xh * g + b
        s = _sigmoid(u2)
        du2 = du3v * (s * (1.0 + u2 * (1.0 - s)))
        dxh = du2 * g
        du1 = r * (dxh - jnp.mean(dxh, axis=-1, keepdims=True) - xh * jnp.mean(dxh * xh, axis=-1, keepdims=True))
        return du1, du2, xh

    def body(d_ref, u_ref, dn_ref, un_ref, ga_ref, gb_ref, gah_ref, gbh_ref, w_ref, g_ref, bl_ref,
             dglu_ref, dw_ref, dbdw_ref, dg_ref, dbl_ref, dbin_ref, scr, scd):
        i = pl.program_id(0)
        g, b = g_ref[...], bl_ref[...]
        du1, du2, xh = du1_of(d_ref[...], u_ref[...], g, b)
        du1n, _, _ = du1_of(dn_ref[...], un_ref[...], g, b)
        du1n = jnp.where(i < last, du1n, 0.0)
        sgb = _sigmoid(gb_ref[...])
        ga = ga_ref[...]
        u0 = ga * sgb
        u0h = jnp.where(i > 0, gah_ref[...] * _sigmoid(gbh_ref[...]), 0.0)
        for c in range(n_chunks):
            scd[c, 0:tm, :] = du1[:, _lanes(c)]
            scd[c, tm:tm + CONV_HALO, :] = du1n[:, _lanes(c)]
            scr[c, 0:CONV_HALO, :] = u0h[:, _lanes(c)]
            scr[c, CONV_HALO:CONV_HALO + tm, :] = u0[:, _lanes(c)]

        @pl.when(i == 0)
        def _():
            for ref in (dw_ref, dbdw_ref, dg_ref, dbl_ref, dbin_ref):
                ref[...] = jnp.zeros_like(ref)

        dsg = ga * (sgb * (1.0 - sgb))
        for c in range(n_chunks):
            gate = slice(512 + c * LANES, 512 + (c + 1) * LANES)
            for r0 in range(0, tm, CONV_ROWS):
                rows = slice(r0, r0 + CONV_ROWS)
                du0 = jnp.zeros((CONV_ROWS, LANES), F32)
                for j in range(CONV_K):
                    du0 = du0 + w_ref[j:j + 1, _lanes(c)] * scd[c, pl.ds(r0 + CONV_K - 1 - j, CONV_ROWS), :]
                dga = du0 * sgb[rows, _lanes(c)]
                dgb = du0 * dsg[rows, _lanes(c)]
                dglu_ref[rows, _lanes(c)] = dga.astype(BF16)
                dglu_ref[rows, gate] = dgb.astype(BF16)
                dbin_ref[:, _lanes(c)] += _colsum(dga)
                dbin_ref[:, gate] += _colsum(dgb)
            for j in range(CONV_K):
                dwj = jnp.zeros((SUBLANE_ROWS, LANES), F32)
                for r0 in range(0, tm, CONV_ROWS):
                    prod = (scd[c, pl.ds(r0, CONV_ROWS), :]
                            * scr[c, pl.ds(r0 + CONV_HALO - (CONV_K - 1) + j, CONV_ROWS), :])
                    dwj = dwj + jnp.sum(prod.reshape(CONV_ROWS // SUBLANE_ROWS, SUBLANE_ROWS, LANES), axis=0)
                dw_ref[j:j + 1, _lanes(c)] += _colsum(dwj)
        dbdw_ref[...] += _colsum(du1)
        dg_ref[...] += _colsum(du2 * xh)
        dbl_ref[...] += _colsum(du2)

    cur = lambda cb: pl.BlockSpec((tm, 512), lambda i: (i, cb))
    prev = lambda cb: pl.BlockSpec((CONV_HALO, 512), lambda i: (jnp.maximum(i * per - 1, 0), cb))
    nxt = pl.BlockSpec((CONV_HALO, 512), lambda i: (jnp.minimum((i + 1) * per, t // CONV_HALO - 1), 0))
    whole = lambda a: pl.BlockSpec(a.shape, lambda i: (0, 0))
    acc = lambda r, w: pl.BlockSpec((r, w), lambda i: (0, 0))
    res = _call(
        body, grid=(t // tm,),
        in_specs=[cur(0), cur(0), nxt, nxt, cur(3), cur(4), prev(3), prev(4), whole(w_dw), whole(g_ln), whole(b_ln)],
        out_specs=[pl.BlockSpec((tm, 1024), lambda i: (i, 0)), acc(CONV_K, 512), acc(1, 512), acc(1, 512),
                   acc(1, 512), acc(1, 1024)],
        out_shape=[_sds((t, 1024), BF16), _sds((CONV_K, 512), F32), _sds((1, 512), F32), _sds((1, 512), F32),
                   _sds((1, 512), F32), _sds((1, 1024), F32)],
        scratch_shapes=[pltpu.VMEM((n_chunks, CONV_HALO + tm, LANES), F32),
                        pltpu.VMEM((n_chunks, tm + CONV_HALO, LANES), F32)],
        sem=("arbitrary",), name=name, args=(du3, u1, du3, u1, z, z, z, z, w_dw, g_ln, b_ln), carried=carried)
    return res[:6] if carried is None else (res[:6], res[6:])


FF_BLOCK = D_FF // 2
FF_HALO = 8
FF_CHUNKS = FF_BLOCK // LANES


def _ffn_conv(w_ref, b_ref, scr, k, rows):
    acc = b_ref[:, _lanes(k)] + w_ref[0:1, _lanes(k)] * scr[k, pl.ds(FF_HALO - 2, rows), :]
    acc = acc + w_ref[1:2, _lanes(k)] * scr[k, pl.ds(FF_HALO - 1, rows), :]
    return acc + w_ref[2:3, _lanes(k)] * scr[k, pl.ds(FF_HALO, rows), :]


def _ffn_act(up, w3, b3, name, tm=ROW_TILE):
    t = up.shape[0]
    per = tm // FF_HALO
    wide = 2 * FF_BLOCK

    def body(u_ref, uh_ref, w_ref, b_ref, o_ref, scr):
        i = pl.program_id(1)
        for k in range(2 * FF_CHUNKS):
            scr[k, 0:FF_HALO, :] = jnp.where(i > 0, uh_ref[:, _lanes(k)], 0.0)
            scr[k, FF_HALO:FF_HALO + tm, :] = u_ref[:, _lanes(k)]
        for cc in range(FF_CHUNKS):
            val = _ffn_conv(w_ref, b_ref, scr, cc, tm)
            gate = _ffn_conv(w_ref, b_ref, scr, FF_CHUNKS + cc, tm)
            o_ref[:, _lanes(cc)] = (_gelu(gate) * val).astype(BF16)

    return pl.pallas_call(
        body, grid=(2, t // tm),
        in_specs=[pl.BlockSpec((tm, wide), lambda c, i: (i, c)),
                  pl.BlockSpec((FF_HALO, wide), lambda c, i: (jnp.maximum(i * per - 1, 0), c)),
                  pl.BlockSpec((FFN_K, wide), lambda c, i: (0, c)),
                  pl.BlockSpec((1, wide), lambda c, i: (0, c))],
        out_specs=pl.BlockSpec((tm, FF_BLOCK), lambda c, i: (i, c)),
        out_shape=_sds((t, D_FF), BF16),
        scratch_shapes=[pltpu.VMEM((2 * FF_CHUNKS, FF_HALO + tm, LANES), F32)],
        compiler_params=_params(("arbitrary", "arbitrary")), name=name,
    )(up, up, w3, b3)


def _ffn_act_bwd(dact, up, w3, b3, name, tm=ROW_TILE):
    t = up.shape[0]
    per = tm // FF_HALO
    wide = 2 * FF_BLOCK
    last = t // tm - 1
    ext = tm + FF_HALO

    def body(u_ref, up_ref, un_ref, d_ref, dn_ref, w_ref, b_ref, o_ref, dw_ref, db_ref, scr, scd):
        i = pl.program_id(1)
        for k in range(2 * FF_CHUNKS):
            scr[k, 0:FF_HALO, :] = jnp.where(i > 0, up_ref[:, _lanes(k)], 0.0)
            scr[k, FF_HALO:FF_HALO + tm, :] = u_ref[:, _lanes(k)]
            scr[k, FF_HALO + tm:FF_HALO + ext, :] = un_ref[:, _lanes(k)]
        dn = jnp.where(i < last, dn_ref[...], 0.0)

        @pl.when(i == 0)
        def _():
            dw_ref[...] = jnp.zeros_like(dw_ref)
            db_ref[...] = jnp.zeros_like(db_ref)

        for cc in range(FF_CHUNKS):
            val = _ffn_conv(w_ref, b_ref, scr, cc, ext)
            gel, dgel = _gelu_and_grad(_ffn_conv(w_ref, b_ref, scr, FF_CHUNKS + cc, ext))
            da = jnp.concatenate([d_ref[:, _lanes(cc)], dn[:, _lanes(cc)]], axis=0)
            scd[cc] = da * gel
            scd[FF_CHUNKS + cc] = da * val * dgel
            for k in (cc, FF_CHUNKS + cc):
                shifted = [scd[k, pl.ds(FFN_K - 1 - j, tm), :] for j in range(FFN_K)]
                ucur = scr[k, pl.ds(FF_HALO, tm), :]
                o_ref[:, _lanes(k)] = (w_ref[0:1, _lanes(k)] * shifted[0] + w_ref[1:2, _lanes(k)] * shifted[1]
                                       + w_ref[2:3, _lanes(k)] * shifted[2]).astype(BF16)
                for j in range(FFN_K):
                    dw_ref[j:j + 1, _lanes(k)] += _colsum(shifted[j] * ucur)
                db_ref[:, _lanes(k)] += _colsum(shifted[FFN_K - 1])

    nblk = t // FF_HALO
    return pl.pallas_call(
        body, grid=(2, t // tm),
        in_specs=[pl.BlockSpec((tm, wide), lambda c, i: (i, c)),
                  pl.BlockSpec((FF_HALO, wide), lambda c, i: (jnp.maximum(i * per - 1, 0), c)),
                  pl.BlockSpec((FF_HALO, wide), lambda c, i: (jnp.minimum((i + 1) * per, nblk - 1), c)),
                  pl.BlockSpec((tm, FF_BLOCK), lambda c, i: (i, c)),
                  pl.BlockSpec((FF_HALO, FF_BLOCK), lambda c, i: (jnp.minimum((i + 1) * per, nblk - 1), c)),
                  pl.BlockSpec((FFN_K, wide), lambda c, i: (0, c)),
                  pl.BlockSpec((1, wide), lambda c, i: (0, c))],
        out_specs=[pl.BlockSpec((tm, wide), lambda c, i: (i, c)),
                   pl.BlockSpec((FFN_K, wide), lambda c, i: (0, c)),
                   pl.BlockSpec((1, wide), lambda c, i: (0, c))],
        out_shape=[_sds((t, 2 * D_FF), BF16), _sds((FFN_K, 2 * D_FF), F32), _sds((1, 2 * D_FF), F32)],
        scratch_shapes=[pltpu.VMEM((2 * FF_CHUNKS, FF_HALO + ext, LANES), F32),
                        pltpu.VMEM((2 * FF_CHUNKS, ext, LANES), F32)],
        compiler_params=_params(("arbitrary", "arbitrary")), name=name,
    )(up, up, up, dact, dact, w3, b3)


def _toeplitz_map():
    f = np.zeros((TOEP, REL_PAD), np.float32)
    for m in range(TOEP - 1):
        rel = (WINDOW - 1) - m
        f[m, int(np.clip(rel, -MAX_REL, MAX_REL)) + MAX_REL] = 1.0
    return f


def _split3(v):
    hi = v.astype(BF16)
    r1 = v - hi.astype(F32)
    mid = r1.astype(BF16)
    lo = (r1 - mid.astype(F32)).astype(BF16)
    return hi, mid, lo


def _exact_select(v, sel):
    out = None
    for part in _split3(v):
        p = jnp.dot(part, sel, preferred_element_type=F32)
        out = p if out is None else out + p
    return out


def _select_call(v, sel, name):
    def body(v_ref, s_ref, o_ref):
        o_ref[...] = _exact_select(v_ref[...], s_ref[...])
    return pl.pallas_call(body, out_shape=_sds((v.shape[0], sel.shape[1]), F32), name=name)(v, sel)


def _band_bias(gen_row):
    b0 = jnp.broadcast_to(gen_row, (Q_TILE, TOEP))
    bias = pltpu.roll(b0, TOEP - 255, 1, stride=1, stride_axis=0)[:, :WINDOW]
    qq = lax.broadcasted_iota(jnp.int32, (Q_TILE, WINDOW), 0) // CHUNK
    kc = lax.broadcasted_iota(jnp.int32, (Q_TILE, WINDOW), 1) // CHUNK
    return jnp.where((kc >= qq) & (kc <= qq + LEFT_CHUNKS), bias, NEG_INF)


PAD_ROWS = WINDOW - Q_TILE
NT_DIMS = (((1,), (1,)), ((), ()))
TN_DIMS = (((0,), (0,)), ((), ()))


def _head_mask(hh):
    lane = lax.broadcasted_iota(jnp.int32, (1, 128), 1)
    return (lane < 64) if hh == 0 else (lane >= 64)


def _probs(qm, kw, bias, i):
    s = lax.dot_general(qm, kw, NT_DIMS, preferred_element_type=F32) + bias
    col = lax.broadcasted_iota(jnp.int32, (Q_TILE, WINDOW), 1)
    s = jnp.where(col >= PAD_ROWS - Q_TILE * i, s, NEG_INF)
    p = jnp.exp(s - jnp.max(s, axis=-1, keepdims=True))
    return p / jnp.sum(p, axis=-1, keepdims=True)


def _attention(z, gen, name, carried=None):
    t = z.shape[0]
    n_i = t // Q_TILE

    def body(q_ref, k_ref, v_ref, g_ref, o_ref, kpad, vpad, bias):
        hp, i = pl.program_id(0), pl.program_id(1)

        @pl.when(i == 0)
        def _():
            kpad[0:PAD_ROWS, :] = jnp.zeros((PAD_ROWS, 128), BF16)
            vpad[0:PAD_ROWS, :] = jnp.zeros((PAD_ROWS, 128), BF16)
            kpad[PAD_ROWS:PAD_ROWS + t, :] = k_ref[...].astype(BF16)
            vpad[PAD_ROWS:PAD_ROWS + t, :] = v_ref[...].astype(BF16)
            for hh in range(2):
                bias[hh] = _band_bias(g_ref[pl.ds(2 * hp + hh, 1), :])

        start = pl.multiple_of(i * Q_TILE, Q_TILE)
        kw = kpad[pl.ds(start, WINDOW), :]
        vw = vpad[pl.ds(start, WINDOW), :]
        q = q_ref[...] * (CHUNK ** -0.5)
        out = None
        for hh in range(2):
            mask = _head_mask(hh)
            p = _probs(jnp.where(mask, q, 0.0).astype(BF16), kw, bias[hh], i)
            o = jnp.dot(p.astype(BF16), vw, preferred_element_type=F32)
            out = jnp.where(mask, o, 0.0) if out is None else jnp.where(mask, o, out)
        o_ref[...] = out.astype(BF16)

    res = _call(
        body, grid=(4, n_i),
        in_specs=[pl.BlockSpec((Q_TILE, 128), lambda h, i: (i, h)),
                  pl.BlockSpec((t, 128), lambda h, i: (0, 4 + h)),
                  pl.BlockSpec((t, 128), lambda h, i: (0, 8 + h)),
                  pl.BlockSpec((N_HEADS, TOEP), lambda h, i: (0, 0))],
        out_specs=[pl.BlockSpec((Q_TILE, 128), lambda h, i: (i, h))],
        out_shape=[_sds((t, 512), BF16)],
        scratch_shapes=[pltpu.VMEM((PAD_ROWS + t, 128), BF16), pltpu.VMEM((PAD_ROWS + t, 128), BF16),
                        pltpu.VMEM((2, Q_TILE, WINDOW), F32)],
        sem=("arbitrary", "arbitrary"), name=name, args=(z, z, z, gen), carried=carried)
    return res[0] if carried is None else (res[0], res[1:])


def _attention_bwd(z, datt, gen, name, carried=None):
    t = z.shape[0]
    n_i = t // Q_TILE

    def body(q_ref, k_ref, v_ref, d_ref, g_ref, dq_ref, dk_ref, dv_ref, sq_ref, sk_ref, sv_ref, dg_ref,
             kpad, vpad, dkacc, dvacc, bias, dsacc):
        hp, i = pl.program_id(0), pl.program_id(1)

        @pl.when(i == 0)
        def _():
            kpad[0:PAD_ROWS, :] = jnp.zeros((PAD_ROWS, 128), BF16)
            vpad[0:PAD_ROWS, :] = jnp.zeros((PAD_ROWS, 128), BF16)
            kpad[PAD_ROWS:PAD_ROWS + t, :] = k_ref[...].astype(BF16)
            vpad[PAD_ROWS:PAD_ROWS + t, :] = v_ref[...].astype(BF16)
            dkacc[...] = jnp.zeros_like(dkacc)
            dvacc[...] = jnp.zeros_like(dvacc)
            dsacc[...] = jnp.zeros_like(dsacc)
            for hh in range(2):
                bias[hh] = _band_bias(g_ref[pl.ds(2 * hp + hh, 1), :])

        start = pl.multiple_of(i * Q_TILE, Q_TILE)
        win = pl.ds(start, WINDOW)
        kw = kpad[win, :]
        vw = vpad[win, :]
        q = q_ref[...] * (CHUNK ** -0.5)
        do = d_ref[...]
        dq = None
        for hh in range(2):
            mask = _head_mask(hh)
            qm = jnp.where(mask, q, 0.0).astype(BF16)
            dom = jnp.where(mask, do, 0.0).astype(BF16)
            p = _probs(qm, kw, bias[hh], i)
            dp = lax.dot_general(dom, vw, NT_DIMS, preferred_element_type=F32)
            ds = p * (dp - jnp.sum(p * dp, axis=-1, keepdims=True))
            dsacc[hh] += ds
            ds16 = ds.astype(BF16)
            dqh = jnp.dot(ds16, kw, preferred_element_type=F32) * (CHUNK ** -0.5)
            dq = jnp.where(mask, dqh, 0.0) if dq is None else jnp.where(mask, dqh, dq)
            dkacc[win, :] += lax.dot_general(ds16, qm, TN_DIMS, preferred_element_type=F32)
            dvacc[win, :] += lax.dot_general(p.astype(BF16), dom, TN_DIMS, preferred_element_type=F32)
        dq_ref[...] = dq.astype(BF16)

        @pl.when(i == 0)
        def _():
            sq_ref[...] = _colsum(dq)

        @pl.when(i > 0)
        def _():
            sq_ref[...] += _colsum(dq)

        @pl.when(i == n_i - 1)
        def _():
            dk = dkacc[PAD_ROWS:PAD_ROWS + t, :]
            dv = dvacc[PAD_ROWS:PAD_ROWS + t, :]
            dk_ref[...] = dk.astype(BF16)
            dv_ref[...] = dv.astype(BF16)
            sk_ref[...] = _colsum(dk)
            sv_ref[...] = _colsum(dv)
            rr = lax.broadcasted_iota(jnp.int32, (Q_TILE, Q_TILE), 0)
            cc = lax.broadcasted_iota(jnp.int32, (Q_TILE, Q_TILE), 1)
            rev = jnp.where(rr + cc == Q_TILE - 1, 1.0, 0.0).astype(BF16)
            for hh in range(2):
                acc = None
                for part in _split3(dsacc[hh]):
                    pr = jnp.dot(rev, part, preferred_element_type=F32)
                    acc = pr if acc is None else acc + pr
                wide = jnp.concatenate([acc, jnp.zeros((Q_TILE, TOEP - WINDOW), F32)], axis=1)
                dg_ref[pl.ds(2 * hp + hh, 1), :] = _colsum(pltpu.roll(wide, 0, 1, stride=1, stride_axis=0))

    col = lambda off: pl.BlockSpec((t, 128), lambda h, i: (0, off + h))
    tile = lambda: pl.BlockSpec((Q_TILE, 128), lambda h, i: (i, h))
    sums = lambda: pl.BlockSpec((1, 128), lambda h, i: (0, h))
    res = _call(
        body, grid=(4, n_i),
        in_specs=[tile(), col(4), col(8), tile(), pl.BlockSpec((N_HEADS, TOEP), lambda h, i: (0, 0))],
        out_specs=[tile(), col(0), col(0), sums(), sums(), sums(), pl.BlockSpec((N_HEADS, TOEP), lambda h, i: (0, 0))],
        out_shape=[_sds((t, 512), BF16)] * 3 + [_sds((1, 512), F32)] * 3 + [_sds((N_HEADS, TOEP), F32)],
        scratch_shapes=[pltpu.VMEM((PAD_ROWS + t, 128), BF16), pltpu.VMEM((PAD_ROWS + t, 128), BF16),
                        pltpu.VMEM((PAD_ROWS + t, 128), F32), pltpu.VMEM((PAD_ROWS + t, 128), F32),
                        pltpu.VMEM((2, Q_TILE, WINDOW), F32), pltpu.VMEM((2, Q_TILE, WINDOW), F32)],
        sem=("arbitrary", "arbitrary"), name=name, args=(z, z, z, datt, gen), carried=carried)
    return res[:7] if carried is None else (res[:7], res[7:])


def _adamw_math(w, g, m, v):
    m = ADAM_B1 * m + (1.0 - ADAM_B1) * g
    v = ADAM_B2 * v + (1.0 - ADAM_B2) * (g * g)
    m_hat = m / (1.0 - ADAM_B1 ** ADAM_STEP)
    v_hat = v / (1.0 - ADAM_B2 ** ADAM_STEP)
    delta = -ADAM_LR * (m_hat / (jnp.sqrt(v_hat) + ADAM_EPS) + ADAM_WD * w)
    return delta, m, v


def _adamw_many(items, name):
    n = len(items)

    def body(*refs):
        ins, outs = refs[:4 * n], refs[4 * n:]
        for k in range(n):
            w, g, m, v = (r[...] for r in ins[4 * k:4 * k + 4])
            outs[3 * k][...], outs[3 * k + 1][...], outs[3 * k + 2][...] = _adamw_math(w, g, m, v)

    flat = [a for item in items for a in item]
    res = pl.pallas_call(body, out_shape=[_sds(item[0].shape, F32) for item in items for _ in range(3)],
                         name=name)(*flat)
    return [tuple(res[3 * k:3 * k + 3]) for k in range(n)]


def _adamw(w, g, m, v, name):
    r, c = w.shape
    tm = next(cand for cand in (256, 176, 128, 64, 32, 16, 8) if r % cand == 0)
    return _rowcall(_adamw_math, [(w, c, 0), (g, c, 0), (m, c, 0), (v, c, 0)], [],
                    [(r, c, F32, c)] * 3, [], name=name, tm=tm)


def _ada_fwd(c_all, w_shard, b_shard, name):
    n = w_shard.shape[1]
    tn = 512

    def body(c_ref, w_ref, b_ref, o_ref, a_ref):
        cv = c_ref[...]
        act = cv * _sigmoid(cv)
        a_ref[...] = act
        o_ref[...] = jnp.dot(act.astype(BF16), w_ref[...].astype(BF16), preferred_element_type=F32) + b_ref[...]

    return pl.pallas_call(
        body, grid=(n // tn,),
        in_specs=[pl.BlockSpec((8, D), lambda j: (0, 0)), pl.BlockSpec((D, tn), lambda j: (0, j)),
                  pl.BlockSpec((1, tn), lambda j: (0, j))],
        out_specs=[pl.BlockSpec((8, tn), lambda j: (0, j)), pl.BlockSpec((8, D), lambda j: (0, 0))],
        out_shape=[_sds((8, n), F32), _sds((8, D), F32)],
        compiler_params=_params(("arbitrary",)), name=name,
    )(c_all, w_shard, b_shard)


def _ada_bwd_adamw(act_t, dmod_shard, w, m, v, name):
    r, c = w.shape
    tm = 256

    def body(a_ref, d_ref, w_ref, m_ref, v_ref, g_ref, dl_ref, nm_ref, nv_ref):
        g = jnp.dot(a_ref[...], d_ref[...], precision=lax.Precision.HIGHEST, preferred_element_type=F32)
        g_ref[...] = g
        dl_ref[...], nm_ref[...], nv_ref[...] = _adamw_math(w_ref[...], g, m_ref[...], v_ref[...])

    blk = pl.BlockSpec((tm, c), lambda i: (i, 0))
    return pl.pallas_call(
        body, grid=(r // tm,),
        in_specs=[pl.BlockSpec((tm, 8), lambda i: (i, 0)), pl.BlockSpec((8, c), lambda i: (0, 0)), blk, blk, blk],
        out_specs=[blk] * 4, out_shape=[_sds((r, c), F32)] * 4,
        compiler_params=_params(("arbitrary",)), name=name,
    )(act_t, dmod_shard, w, m, v)


def _place():
    return lax.axis_index("x"), lax.axis_index("y"), lax.axis_index("c")


def _flip(v, bit):
    return 1 - v if bit else v


VMEM_SPEC = pl.BlockSpec(memory_space=pltpu.VMEM)


def _allgather8(v, name):
    r, c = v.shape

    def body(v_ref, g_ref, tot_ref, send_sems, recv_sems, local_sem):
        x, y, cc = _place()
        me = 4 * x + 2 * y + cc
        mine = pltpu.make_async_copy(v_ref, g_ref.at[me], local_sem)
        mine.start()
        sends = []
        for k in range(1, 8):
            peer = (_flip(x, k & 4), _flip(y, k & 2), _flip(cc, k & 1))
            cp = pltpu.make_async_remote_copy(src_ref=v_ref, dst_ref=g_ref.at[me], send_sem=send_sems.at[k - 1],
                                              recv_sem=recv_sems.at[k - 1], device_id=peer, device_id_type=MESH)
            cp.start()
            sends.append(cp)
        for k in range(1, 8):
            peer = (_flip(x, k & 4), _flip(y, k & 2), _flip(cc, k & 1))
            theirs = g_ref.at[4 * peer[0] + 2 * peer[1] + peer[2]]
            pltpu.make_async_remote_copy(src_ref=v_ref, dst_ref=theirs, send_sem=send_sems.at[k - 1],
                                         recv_sem=recv_sems.at[k - 1], device_id=peer, device_id_type=MESH).wait_recv()
        for cp in sends:
            cp.wait_send()
        mine.wait()
        tot = g_ref[0]
        for d in range(1, 8):
            tot = tot + g_ref[d]
        tot_ref[...] = tot

    return pl.pallas_call(
        body, in_specs=[VMEM_SPEC], out_specs=[VMEM_SPEC, VMEM_SPEC],
        out_shape=[_sds((8, r, c), F32), _sds((r, c), F32)],
        scratch_shapes=[pltpu.SemaphoreType.DMA((7,)), pltpu.SemaphoreType.DMA((7,)), pltpu.SemaphoreType.DMA],
        compiler_params=pltpu.CompilerParams(vmem_limit_bytes=VMEM_LIMIT), name=name,
    )(v)


def _slot(px, py, swapped):
    return 2 * py + px if swapped else 2 * px + py


def _gather_shards(arrs, swapped, name, in_place=False):
    n = len(arrs)

    def body(*refs):
        ins, outs = refs[:n], refs[n:2 * n]
        send1, recv1, send2, recv2, local_sems = refs[2 * n:]
        x, y, c = _place()
        sibling = (x, y, 1 - c)
        chips = [(_flip(x, k & 2), _flip(y, k & 1)) for k in (1, 2, 3)]
        local_copies, sends = [], []
        for a in range(n):
            h = outs[a].shape[1] // 2
            mine = pl.ds(pl.multiple_of(c * h, 8), h)
            own = _slot(x, y, swapped[a])
            if in_place:
                src = outs[a].at[own, mine]
            else:
                src = ins[a].at[mine]
                lc = pltpu.make_async_copy(ins[a], outs[a].at[own], local_sems.at[a])
                lc.start()
                local_copies.append(lc)
            for j, (px, py) in enumerate(chips):
                cp = pltpu.make_async_remote_copy(
                    src_ref=src, dst_ref=outs[a].at[own, mine], send_sem=send1.at[3 * a + j],
                    recv_sem=recv1.at[3 * a + j], device_id=(px, py, c), device_id_type=MESH)
                cp.start()
                sends.append(cp)
        for a in range(n):
            h = outs[a].shape[1] // 2
            mine = pl.ds(pl.multiple_of(c * h, 8), h)
            for j, (px, py) in enumerate(chips):
                piece = outs[a].at[_slot(px, py, swapped[a]), mine]
                pltpu.make_async_remote_copy(
                    src_ref=piece, dst_ref=piece, send_sem=send1.at[3 * a + j], recv_sem=recv1.at[3 * a + j],
                    device_id=(px, py, c), device_id_type=MESH).wait_recv()
                fwd = pltpu.make_async_remote_copy(
                    src_ref=piece, dst_ref=piece, send_sem=send2.at[3 * a + j], recv_sem=recv2.at[3 * a + j],
                    device_id=sibling, device_id_type=MESH)
                fwd.start()
                sends.append(fwd)
        for a in range(n):
            h = outs[a].shape[1] // 2
            other = pl.ds(pl.multiple_of((1 - c) * h, 8), h)
            for j, (px, py) in enumerate(chips):
                piece = outs[a].at[_slot(px, py, swapped[a]), other]
                pltpu.make_async_remote_copy(
                    src_ref=piece, dst_ref=piece, send_sem=send2.at[3 * a + j], recv_sem=recv2.at[3 * a + j],
                    device_id=sibling, device_id_type=MESH).wait_recv()
        for cp in sends:
            cp.wait_send()
        for lc in local_copies:
            lc.wait()

    dma = lambda k: pltpu.SemaphoreType.DMA((k,))
    return pl.pallas_call(
        body, in_specs=[ANY] * n, out_specs=[ANY] * n,
        out_shape=[_sds(a.shape if in_place else (4,) + a.shape, a.dtype) for a in arrs],
        scratch_shapes=[dma(3 * n), dma(3 * n), dma(3 * n), dma(3 * n), dma(n)],
        input_output_aliases={a: a for a in range(n)} if in_place else {},
        name=name,
    )(*arrs)


def _carry_pair_exchange(grads):
    n = len(grads)

    def copies(ins, outs, send_sems, recv_sems):
        x, y, c = _place()
        cps = []
        for a in range(n):
            h = ins[a].shape[1] // 2
            theirs = pl.ds(pl.multiple_of((1 - c) * h, 8), h)
            cps.append(pltpu.make_async_remote_copy(
                src_ref=ins[a].at[:, theirs, :], dst_ref=outs[a], send_sem=send_sems.at[a], recv_sem=recv_sems.at[a],
                device_id=(x, y, 1 - c), device_id_type=MESH))
        return cps

    def start(*refs):
        for cp in copies(*refs):
            cp.start()

    def finish(*refs):
        for cp in copies(*refs):
            cp.wait()

    return _Carried(grads, [_sds((4, g.shape[1] // 2, g.shape[2]), F32) for g in grads], {}, n, start, finish)


def _pair_sum(grad, recv, core, name):
    _, r, c = grad.shape
    h = r // 2

    def body(core_ref, g_ref, r_ref, o_ref):
        o_ref[...] = (g_ref[...] + r_ref[...]).astype(BF16)

    return pl.pallas_call(
        body,
        grid_spec=pltpu.PrefetchScalarGridSpec(
            num_scalar_prefetch=1, grid=(4,),
            in_specs=[pl.BlockSpec((None, h, c), lambda s, core_ref: (s, core_ref[0], 0)),
                      pl.BlockSpec((None, h, c), lambda s, core_ref: (s, 0, 0))],
            out_specs=pl.BlockSpec((None, h, c), lambda s, core_ref: (s, 0, 0))),
        out_shape=_sds((4, h, c), BF16), compiler_params=_params(("arbitrary",)), name=name,
    )(core, grad, recv)


def _carry_chip_exchange(parts, swapped):
    n = len(parts)

    def copies(ins, outs, send_sems, recv_sems):
        x, y, c = _place()
        chips = [(_flip(x, k & 2), _flip(y, k & 1)) for k in (1, 2, 3)]
        cps = []
        for a in range(n):
            for j, (px, py) in enumerate(chips):
                cps.append(pltpu.make_async_remote_copy(
                    src_ref=ins[a].at[_slot(px, py, swapped[a])], dst_ref=outs[a].at[j],
                    send_sem=send_sems.at[3 * a + j], recv_sem=recv_sems.at[3 * a + j],
                    device_id=(px, py, c), device_id_type=MESH))
        return cps

    def start(*refs):
        for cp in copies(*refs):
            cp.start()

    def finish(*refs):
        for cp in copies(*refs):
            cp.wait()

    return _Carried(parts, [_sds((3,) + p.shape[1:], BF16) for p in parts], {}, 3 * n, start, finish)


def _chip_sum(part, recv, slot_core, name):
    _, h, c = part.shape

    def body(sc_ref, p_ref, r_ref, o_ref):
        acc = p_ref[...].astype(F32)
        for j in range(3):
            acc = acc + r_ref[j].astype(F32)
        o_ref[...] = acc

    return pl.pallas_call(
        body,
        grid_spec=pltpu.PrefetchScalarGridSpec(
            num_scalar_prefetch=1, grid=(1,),
            in_specs=[pl.BlockSpec((None, h, c), lambda s, sc_ref: (sc_ref[0], 0, 0)),
                      pl.BlockSpec((3, h, c), lambda s, sc_ref: (0, 0, 0))],
            out_specs=pl.BlockSpec((h, c), lambda s, sc_ref: (sc_ref[1], 0))),
        out_shape=_sds((2 * h, c), F32), compiler_params=_params(("arbitrary",)), name=name,
    )(slot_core, part, recv)


def _carry_pair_share(shards):
    n = len(shards)

    def copies(outs, send_sems, recv_sems, mine):
        x, y, c = _place()
        cps = []
        for a in range(n):
            h = outs[a].shape[0] // 2
            half = outs[a].at[pl.ds(pl.multiple_of((c if mine else 1 - c) * h, 8), h)]
            cps.append(pltpu.make_async_remote_copy(
                src_ref=half, dst_ref=half, send_sem=send_sems.at[a], recv_sem=recv_sems.at[a],
                device_id=(x, y, 1 - c), device_id_type=MESH))
        return cps

    def start(ins, outs, send_sems, recv_sems):
        for cp in copies(outs, send_sems, recv_sems, True):
            cp.start()

    def finish(ins, outs, send_sems, recv_sems):
        for cp in copies(outs, send_sems, recv_sems, False):
            cp.wait_recv()
        for cp in copies(outs, send_sems, recv_sems, True):
            cp.wait_send()

    return _Carried(shards, [_sds(s.shape, F32) for s in shards], {a: a for a in range(n)}, n, start, finish)


def _carry_gather_ici(bufs, swapped):
    n = len(bufs)

    def copies(outs, send_sems, recv_sems, sending):
        x, y, c = _place()
        cps = []
        for a in range(n):
            h = outs[a].shape[1] // 2
            mine = pl.ds(pl.multiple_of(c * h, 8), h)
            for j, k in enumerate((1, 2, 3)):
                px, py = _flip(x, k & 2), _flip(y, k & 1)
                slot = _slot(x, y, swapped[a]) if sending else _slot(px, py, swapped[a])
                piece = outs[a].at[slot, mine]
                cps.append(pltpu.make_async_remote_copy(
                    src_ref=piece, dst_ref=piece, send_sem=send_sems.at[3 * a + j], recv_sem=recv_sems.at[3 * a + j],
                    device_id=(px, py, c), device_id_type=MESH))
        return cps

    def start(ins, outs, send_sems, recv_sems):
        for cp in copies(outs, send_sems, recv_sems, True):
            cp.start()

    def finish(ins, outs, send_sems, recv_sems):
        for cp in copies(outs, send_sems, recv_sems, False):
            cp.wait_recv()
        for cp in copies(outs, send_sems, recv_sems, True):
            cp.wait_send()

    return _Carried(bufs, [_sds(b.shape, b.dtype) for b in bufs], {a: a for a in range(n)}, 3 * n, start, finish)


def _carry_gather_forward(bufs, swapped):
    n = len(bufs)

    def copies(outs, send_sems, recv_sems, sending):
        x, y, c = _place()
        cps = []
        for a in range(n):
            h = outs[a].shape[1] // 2
            rows = pl.ds(pl.multiple_of((c if sending else 1 - c) * h, 8), h)
            for j, k in enumerate((1, 2, 3)):
                piece = outs[a].at[_slot(_flip(x, k & 2), _flip(y, k & 1), swapped[a]), rows]
                cps.append(pltpu.make_async_remote_copy(
                    src_ref=piece, dst_ref=piece, send_sem=send_sems.at[3 * a + j], recv_sem=recv_sems.at[3 * a + j],
                    device_id=(x, y, 1 - c), device_id_type=MESH))
        return cps

    def start(ins, outs, send_sems, recv_sems):
        for cp in copies(outs, send_sems, recv_sems, True):
            cp.start()

    def finish(ins, outs, send_sems, recv_sems):
        for cp in copies(outs, send_sems, recv_sems, False):
            cp.wait_recv()
        for cp in copies(outs, send_sems, recv_sems, True):
            cp.wait_send()

    return _Carried(bufs, [_sds(b.shape, b.dtype) for b in bufs], {a: a for a in range(n)}, 3 * n, start, finish)


def _pack(arrs, rows_multiple=8):
    parts, offs, row = [], [], 0
    for a in arrs:
        flat = a.reshape(-1)
        nrow = -(-flat.shape[0] // D)
        parts.append(jnp.pad(flat, (0, nrow * D - flat.shape[0])))
        offs.append(row)
        row += nrow
    total = -(-row // rows_multiple) * rows_multiple
    if total > row:
        parts.append(jnp.zeros(((total - row) * D,), F32))
    return jnp.concatenate(parts).reshape(total, D), offs


def _unpack(packed, offs, shapes):
    out = []
    for off, shp in zip(offs, shapes):
        size = int(np.prod(shp))
        nrow = -(-size // D)
        out.append(packed[off:off + nrow].reshape(-1)[:size].reshape(shp))
    return out


def _to_bf16_slot(w, slot, name):
    r, c = w.shape
    tm = next(cand for cand in (256, 176, 128, 64, 32, 16) if r % cand == 0)

    def body(slot_ref, w_ref, o_ref):
        o_ref[...] = w_ref[...].astype(BF16)

    return pl.pallas_call(
        body,
        grid_spec=pltpu.PrefetchScalarGridSpec(
            num_scalar_prefetch=1, grid=(r // tm,),
            in_specs=[pl.BlockSpec((tm, c), lambda i, slot_ref: (i, 0))],
            out_specs=pl.BlockSpec((None, tm, c), lambda i, slot_ref: (slot_ref[0], i, 0))),
        out_shape=_sds((4, r, c), BF16), compiler_params=_params(("arbitrary",)), name=name,
    )(slot, w)


def _unshard_cols(g):
    s, k, n = g.shape
    return jnp.transpose(g, (1, 0, 2)).reshape(k, s * n)


def _ff_swap(v):
    b = FF_BLOCK
    return jnp.concatenate([v[..., 0:b], v[..., 2 * b:3 * b], v[..., b:2 * b], v[..., 3 * b:4 * b]], axis=-1)


LATE = ("attn_o", "conv_o", "mix_o", "up", "down")
EARLY_GRADS = ("down", "up", "mix_o", "attn_o", "conv_o")


def _weight_views(bufs):
    return {"up": bufs["up"], "attn_o": _unshard_cols(bufs["attn_o"]), "conv_o": _unshard_cols(bufs["conv_o"]),
            "mix_o": bufs["mix_o"].reshape(D, D), "down": bufs["down"].reshape(D_FF, D)}


def _pair_sums(names, grads, recv, dist):
    return [_pair_sum(g, r, dist["core"], "pair_sum_" + n) for n, g, r in zip(names, grads, recv)]


def _reduce_halves(names, parts, from_chips, dist):
    return [_chip_sum(p, r, jnp.concatenate([dist["slots"][SWAPPED[n]], dist["core"]]), "chip_sum_" + n)
            for n, p, r in zip(names, parts, from_chips)]


FUSED_TILE = 256


def _gates(z):
    return [(z, 512, 5), (z, 512, 6), (z, 512, 7), (z, 512, 8)]


def _mix_out(a, cb, z, x, w_mix_o, g_post, gt, g_pre2, sc2, sh2, name):
    def lhs(av, cv, ga0, ga1, gb0, gb1):
        ga, gb = jnp.concatenate([ga0, ga1], axis=1), jnp.concatenate([gb0, gb1], axis=1)
        return _sigmoid(ga) * av + _sigmoid(gb) * cv

    def fn(ym, y, xv, gv, gtv, g2v, scv, shv):
        yn, _ = _rms_stats(ym)
        x1 = xv + gtv * (yn * gv)
        xn, _ = _rms_stats(x1)
        return ym, y, x1, (xn * g2v) * (1.0 + scv) + shv

    return _matmul_rows(w_mix_o, form="nn", tm=min(FUSED_TILE, x.shape[0]), tk=D, fn=fn, a_rows=[(a, D, 0), (cb, D, 0)] + _gates(z),
                        a_fn=lhs, rows=[(x, D, 0)], consts=[g_post, gt, g_pre2, sc2, sh2],
                        row_outs=[(F32, D), (BF16, D), (F32, D), (BF16, D)], acc_outs=[], name=name)


def _down_tail(act, w_down, x1, target, g, gt, name):
    def fn(yv, xv, tv, gv, gtv):
        yn, r = _rms_stats(yv)
        e = xv + gtv * (yn * gv) - tv
        dx2 = e * (1.0 / D)
        dyn = dx2 * gtv
        return (dx2, _rms_bwd(dyn * gv, yn, r), _colsum(e * e) * (0.5 / D), _colsum(dyn * yn),
                _colsum(dx2 * (yn * gv)))

    return _matmul_rows(w_down, form="nn", a=act, tm=min(FUSED_TILE, x1.shape[0]), tk=FF_BLOCK, fn=fn,
                        rows=[(x1, D, 0), (target, D, 0)], consts=[g, gt], row_outs=[(F32, D), (BF16, D)],
                        acc_outs=[(1, D)] * 3, name=name)


def _up_dx_tail(dup, w_up, x1, dx2, ym, g_pre2, sc2, g_post, gt, name):
    def fn(dh, xv, dov, ymv, g2v, scv, gv, gtv):
        xn, r = _rms_stats(xv)
        dyn = dh * (1.0 + scv)
        dx1 = dov + _rms_bwd(dyn * g2v, xn, r)
        yn, r2 = _rms_stats(ymv)
        dynm = dx1 * gtv
        return (dx1, _rms_bwd(dynm * gv, yn, r2), _colsum(dh), _colsum(dh * (xn * g2v)), _colsum(dyn * xn),
                _colsum(dynm * yn), _colsum(dx1 * (yn * gv)))

    return _matmul_rows(w_up, form="nt", a=dup, tm=min(FUSED_TILE, x1.shape[0]), tk=FF_BLOCK, fn=fn,
                        rows=[(x1, D, 0), (dx2, D, 0), (ym, D, 0)], consts=[g_pre2, sc2, g_post, gt],
                        row_outs=[(F32, D), (BF16, D)], acc_outs=[(1, D)] * 5, name=name)


def _in_dx_tail(dz, w_in, x, dx1, g, sc, name, carried=None):
    def fn(dh, xv, dov, gv, scv):
        xn, r = _rms_stats(xv)
        dyn = dh * (1.0 + scv)
        return dov + _rms_bwd(dyn * gv, xn, r), _colsum(dh), _colsum(dh * (xn * gv)), _colsum(dyn * xn)

    return _matmul_rows(w_in, form="nt", a=dz, tm=min(FUSED_TILE, x.shape[0]), tk=1152, fn=fn,
                        rows=[(x, D, 0), (dx1, D, 0)], consts=[g, sc], row_outs=[(F32, D)], acc_outs=[(1, D)] * 3,
                        name=name, carried=carried)


def _mix_dx_gates(dym, w_mix_o, a, cb, z, name):
    def fn(dy, av, cv, ga0, ga1, gb0, gb1):
        sa = _sigmoid(jnp.concatenate([ga0, ga1], axis=1))
        sb = _sigmoid(jnp.concatenate([gb0, gb1], axis=1))
        dcb = dy * sb
        dga = dy * av * (sa * (1.0 - sa))
        dgb = dy * cv * (sb * (1.0 - sb))
        return dy * sa, dcb, dga, dgb, _colsum(dcb), _colsum(dga), _colsum(dgb)

    return _matmul_rows(w_mix_o, form="nt", a=dym, tm=min(FUSED_TILE, a.shape[0]), tk=D, fn=fn,
                        rows=[(a, D, 0), (cb, D, 0)] + _gates(z), consts=[], row_outs=[(BF16, D)] * 4,
                        acc_outs=[(1, D)] * 3, name=name)


def _local_step(x, target, mod, w_in, late, small, dist=None):
    sh_m, sc_m, gt_m, sh_f, sc_f, gt_f = mod
    t = x.shape[0]
    tmm = min(1024, t)
    late_swapped = [SWAPPED[n] for n in LATE]

    h1 = _pre_norm(x, small["g_pre_mix"], sc_m, sh_m, "pre_norm_mix")
    z = _matmul(h1, w_in, form="nn", out_dtype=F32, tm=tmm, tn=1152, tk=D, bias=small["b_in"], name="mm_in")
    if dist is None:
        att = _attention(z, small["gen"], "attention")
        bufs = late
    else:
        att, landed = _attention(z, small["gen"], "attention",
                                 carried=_carry_gather_ici([late[n] for n in LATE], late_swapped))
        bufs = dict(zip(LATE, _run_carried(_carry_gather_forward(landed, late_swapped), "gather_forward")))
    w = _weight_views(bufs)
    w["in"] = w_in
    a = _matmul(att, w["attn_o"], form="nn", out_dtype=F32, tm=tmm, tn=512, tk=512, name="mm_attn_o")
    u1, u3 = _conv_branch(z, small["w_dw_conv"], small["b_dw_conv"], small["g_conv_ln"], small["b_conv_ln"], "conv_branch")
    cb = _matmul(u3, w["conv_o"], form="nn", out_dtype=F32, tm=tmm, tn=512, tk=512, bias=small["b_conv_o"], name="mm_conv_o")
    ym, y, x1, h2 = _mix_out(a, cb, z, x, w["mix_o"], small["g_post_mix"], gt_m, small["g_pre_ffn"], sc_f, sh_f, "mix_out")
    up = _matmul(h2, w["up"], form="nn", out_dtype=F32, tm=tmm, tn=FF_BLOCK, tk=D, name="mm_up")
    act = _ffn_act(up, small["w_dw_ffn"], small["b_dw_ffn"], "ffn_act")

    dx2, dyf, loss_cols, d_g_post_ffn, d_gt_f = _down_tail(act, w["down"], x1, target, small["g_post_ffn"], gt_f, "down_tail")
    dact = _matmul(dyf, w["down"], form="nt", out_dtype=F32, tm=tmm, tn=FF_BLOCK, tk=D, name="mm_down_dx")
    g_down = _matmul(act, dyf, form="tn", out_dtype=F32, tm=FF_BLOCK, tn=D, tk=tmm, name="mm_down_dw")
    dup, d_w_dw_ffn, d_b_dw_ffn = _ffn_act_bwd(dact, up, small["w_dw_ffn"], small["b_dw_ffn"], "ffn_act_bwd")
    dx1, dym, d_sh_f, d_sc_f, d_g_pre_ffn, d_g_post_mix, d_gt_m = _up_dx_tail(
        dup, w["up"], x1, dx2, ym, small["g_pre_ffn"], sc_f, small["g_post_mix"], gt_m, "up_dx_tail")
    g_up = _matmul(h2, dup, form="tn", out_dtype=F32, tm=D, tn=FF_BLOCK, tk=tmm, out_sharded=True, name="mm_up_dw")
    da, dcb, dgate_a, dgate_b, d_b_conv_o, sga, sgb = _mix_dx_gates(dym, w["mix_o"], a, cb, z, "mix_dx_gates")
    g_mix_o = _matmul(y, dym, form="tn", out_dtype=F32, tm=D, tn=512, tk=tmm, name="mm_mix_o_dw")
    datt = _matmul(da, w["attn_o"], form="nt", out_dtype=F32, tm=tmm, tn=512, tk=D, name="mm_attn_o_dx")
    g_attn_o = _matmul(att, da, form="tn", out_dtype=F32, tm=512, tn=256, tk=tmm, out_sharded=True, name="mm_attn_o_dw")
    du3 = _matmul(dcb, w["conv_o"], form="nt", out_dtype=F32, tm=tmm, tn=512, tk=D, name="mm_conv_o_dx")
    g_conv_o = _matmul(u3, dcb, form="tn", out_dtype=F32, tm=512, tn=256, tk=tmm, out_sharded=True, name="mm_conv_o_dw")
    big = {"attn_o": g_attn_o, "conv_o": g_conv_o, "mix_o": g_mix_o.reshape(4, 256, D),
           "up": g_up, "down": g_down.reshape(4, D_FF // 4, D)}
    conv_bwd = (du3, u1, z, small["w_dw_conv"], small["g_conv_ln"], small["b_conv_ln"], "conv_branch_bwd")
    in_dw = dict(form="tn", out_dtype=F32, tm=D, tn=1152, tk=tmm, out_sharded=True, name="mm_in_dw")
    in_dx = (w_in, x, dx1, small["g_pre_mix"], sc_m, "in_dx_tail")
    if dist is None:
        dglu, d_w_dw_conv, d_b_dw_conv, d_g_conv_ln, d_b_conv_ln, sglu = _conv_branch_bwd(*conv_bwd)
        dq, dk, dv, sq, sk, sv, dgen = _attention_bwd(z, datt, small["gen"], "attention_bwd")
        dz = jnp.concatenate([dq, dk, dv, dglu, dgate_a, dgate_b], axis=1)
        big["in"] = _matmul(h1, dz, **in_dw)
        grad_x, d_sh_m, d_sc_m, d_g_pre_mix = _in_dx_tail(dz, *in_dx)
    else:
        early = [big[n] for n in EARLY_GRADS]
        (dglu, d_w_dw_conv, d_b_dw_conv, d_g_conv_ln, d_b_conv_ln, sglu), recv = _conv_branch_bwd(
            *conv_bwd, carried=_carry_pair_exchange(early))
        parts = _pair_sums(EARLY_GRADS, early, recv, dist)
        (dq, dk, dv, sq, sk, sv, dgen), from_chips = _attention_bwd(
            z, datt, small["gen"], "attention_bwd",
            carried=_carry_chip_exchange(parts, [SWAPPED[n] for n in EARLY_GRADS]))
        halves = _reduce_halves(EARLY_GRADS, parts, from_chips, dist)
        dz = jnp.concatenate([dq, dk, dv, dglu, dgate_a, dgate_b], axis=1)
        g_in, shards = _matmul(h1, dz, carried=_carry_pair_share(halves), **in_dw)
        big = dict(zip(EARLY_GRADS, shards))
        recv_in = _run_carried(_carry_pair_exchange([g_in]), "pair_exchange_in")
        part_in = _pair_sums(("in",), [g_in], recv_in, dist)
        (grad_x, d_sh_m, d_sc_m, d_g_pre_mix), from_chips_in = _in_dx_tail(
            dz, *in_dx, carried=_carry_chip_exchange(part_in, [False]))
        half_in = _reduce_halves(("in",), part_in, from_chips_in, dist)
        big["in"] = _run_carried(_carry_pair_share(half_in), "pair_share_in")[0]
    d_b_in = jnp.concatenate([sq, sk, sv, sglu, sga, sgb], axis=1)

    dmod = [d_sh_m, d_sc_m, d_gt_m, d_sh_f, d_sc_f, d_gt_f]
    sm = {"g_pre_mix": d_g_pre_mix, "g_post_mix": d_g_post_mix, "b_in": d_b_in, "gen": dgen,
          "w_dw_conv": d_w_dw_conv, "b_dw_conv": d_b_dw_conv, "g_conv_ln": d_g_conv_ln, "b_conv_ln": d_b_conv_ln,
          "b_conv_o": d_b_conv_o, "g_pre_ffn": d_g_pre_ffn, "g_post_ffn": d_g_post_ffn,
          "w_dw_ffn": d_w_dw_ffn, "b_dw_ffn": d_b_dw_ffn}
    return loss_cols, grad_x, dmod, big, sm


BIG = ("in", "attn_o", "conv_o", "mix_o", "up", "down")
SWAPPED = {"in": False, "attn_o": False, "conv_o": False, "mix_o": False, "up": True, "down": False}
SMALL_ORDER = ("b_ada", "g_pre_mix", "g_post_mix", "b_in", "rel_bias", "b_dw_conv", "g_conv_ln", "b_conv_ln",
               "b_conv_o", "g_pre_ffn", "g_post_ffn", "b_dw_ffn", "w_dw_conv", "w_dw_ffn")


def kernel(x, c, w_ada, b_ada, g_pre_mix, g_post_mix, w_in, b_in, rel_bias, w_attn_o, w_dw_conv, b_dw_conv, g_conv_ln, b_conv_ln, w_conv_o, b_conv_o, w_mix_o, g_pre_ffn, g_post_ffn, w_up, w_dw_ffn, b_dw_ffn, w_down, loss_target, m_w_ada, m_b_ada, m_g_pre_mix, m_g_post_mix, m_w_in, m_b_in, m_rel_bias, m_w_attn_o, m_w_dw_conv, m_b_dw_conv, m_g_conv_ln, m_b_conv_ln, m_w_conv_o, m_b_conv_o, m_w_mix_o, m_g_pre_ffn, m_g_post_ffn, m_w_up, m_w_dw_ffn, m_b_dw_ffn, m_w_down, v_w_ada, v_b_ada, v_g_pre_mix, v_g_post_mix, v_w_in, v_b_in, v_rel_bias, v_w_attn_o, v_w_dw_conv, v_b_dw_conv, v_g_conv_ln, v_b_conv_ln, v_w_conv_o, v_b_conv_o, v_w_mix_o, v_g_pre_ffn, v_g_post_ffn, v_w_up, v_w_dw_ffn, v_b_dw_ffn, v_w_down):
    given = dict(locals())
    ax, ay, ac = lax.axis_index("x"), lax.axis_index("y"), lax.axis_index("c")
    shard = 2 * ax + ay
    me = 4 * ax + 2 * ay + ac
    xs, target = x[0], loss_target[0]

    c_pad = jnp.pad(c, ((0, 7), (0, 0)))
    c_g, _ = _allgather8(c_pad, "gather_c")
    c_all = c_g[:, 0, :]
    b_ada_shard = lax.dynamic_slice(b_ada, (0, shard * 1536), (1, 1536))
    mod_shard, c_act = _ada_fwd(c_all, w_ada[0], b_ada_shard, "ada_fwd")
    small_in = [jnp.pad(mod_shard, ((0, 8), (0, 0))),
                jnp.pad(w_dw_conv[0], ((0, 1), (0, 0))),
                jnp.pad(w_dw_ffn[0], ((0, 13), (0, 0)))]
    mod_g, wdc_g, wdf_g = _gather_shards(small_in, [False, False, True], "gather_small")
    mod_all = jnp.transpose(mod_g[:, :8, :], (1, 0, 2)).reshape(8, 6 * D)
    mod_row = lax.dynamic_slice(mod_all, (me, 0), (1, 6 * D))
    mod = [mod_row[:, k * D:(k + 1) * D] for k in range(6)]

    slots = {sw: _slot(ax, ay, sw).astype(jnp.int32).reshape(1) for sw in (False, True)}
    own = {n: _to_bf16_slot(given["w_" + n][0], slots[SWAPPED[n]], "cast_" + n) for n in BIG}
    w_in_all = _gather_shards([own["in"]], [False], "gather_w_in", in_place=True)[0]
    core = ac.astype(jnp.int32).reshape(1)
    dist = {"core": core, "slots": slots}

    sel = jnp.asarray(_toeplitz_map())
    rel_pad = jnp.pad(rel_bias[0], ((0, 0), (0, REL_PAD - (2 * MAX_REL + 1))))
    gen = _select_call(rel_pad, sel.T.astype(BF16), "bias_rows")
    small = {"g_pre_mix": g_pre_mix, "g_post_mix": g_post_mix, "b_in": b_in, "gen": gen,
             "w_dw_conv": _unshard_cols(wdc_g[:, :CONV_K, :]), "b_dw_conv": b_dw_conv, "g_conv_ln": g_conv_ln,
             "b_conv_ln": b_conv_ln, "b_conv_o": b_conv_o, "g_pre_ffn": g_pre_ffn, "g_post_ffn": g_post_ffn,
             "w_dw_ffn": _unshard_cols(wdf_g[:, :FFN_K, :]), "b_dw_ffn": _ff_swap(b_dw_ffn)}

    loss_cols, grad_x, dmod, reduced, sm = _local_step(xs, target, mod, w_in_all, {n: own[n] for n in LATE}, small, dist)
    loss = lax.psum(jnp.sum(loss_cols), ("x", "y", "c"))

    d_rel = _select_call(sm["gen"], sel.astype(BF16), "bias_fold")[:, :2 * MAX_REL + 1]
    small_grads = {"g_pre_mix": sm["g_pre_mix"], "g_post_mix": sm["g_post_mix"], "b_in": sm["b_in"], "rel_bias": d_rel[None],
                   "b_dw_conv": sm["b_dw_conv"], "g_conv_ln": sm["g_conv_ln"], "b_conv_ln": sm["b_conv_ln"],
                   "b_conv_o": sm["b_conv_o"], "g_pre_ffn": sm["g_pre_ffn"], "g_post_ffn": sm["g_post_ffn"],
                   "b_dw_ffn": _ff_swap(sm["b_dw_ffn"]), "w_dw_conv": sm["w_dw_conv"], "w_dw_ffn": _ff_swap(sm["w_dw_ffn"])}
    order = [n for n in SMALL_ORDER if n != "b_ada"]
    packed, offs = _pack([jnp.concatenate(dmod, axis=1)] + [small_grads[n] for n in order])
    every, total = _allgather8(packed, "gather_small_grads")
    dmod_all = every[:, 0:6, :].reshape(8, 6 * D)
    full_shapes = {n: given[n].shape for n in order}
    full_shapes["w_dw_conv"], full_shapes["w_dw_ffn"] = (1, CONV_K, 512), (1, FFN_K, 2 * D_FF)
    sums = dict(zip(order, _unpack(total, offs[1:], [full_shapes[n] for n in order])))
    sums["b_ada"] = total[0:6].reshape(1, 6 * D)
    sums["w_dw_conv"] = lax.dynamic_slice(sums["w_dw_conv"], (0, 0, shard * 128), (1, CONV_K, 128))
    sums["w_dw_ffn"] = lax.dynamic_slice(sums["w_dw_ffn"], (0, 0, shard * FF_BLOCK), (1, FFN_K, FF_BLOCK))

    upd = dict(zip(SMALL_ORDER, _adamw_many(
        [(given[n], sums[n], given["m_" + n], given["v_" + n]) for n in SMALL_ORDER], "adamw_small")))

    dmod_shard = lax.dynamic_slice(dmod_all, (0, shard * 1536), (8, 1536))
    ada = _ada_bwd_adamw(c_act.T, dmod_shard, w_ada[0], m_w_ada[0], v_w_ada[0], "ada_bwd_adamw")

    out = {"grad_w_ada": ada[0][None], "delta_w_ada": ada[1][None], "new_m_w_ada": ada[2][None], "new_v_w_ada": ada[3][None]}
    for n in BIG:
        g = reduced[n]
        dl, nm, nv = _adamw(given["w_" + n][0], g, given["m_w_" + n][0], given["v_w_" + n][0], "adamw_" + n)
        out["grad_w_" + n], out["delta_w_" + n], out["new_m_w_" + n], out["new_v_w_" + n] = g[None], dl[None], nm[None], nv[None]
    for n in SMALL_ORDER:
        out["grad_" + n], out["delta_" + n], out["new_m_" + n], out["new_v_" + n] = sums[n], *upd[n]

    weights = ["w_ada", "b_ada", "g_pre_mix", "g_post_mix", "w_in", "b_in", "rel_bias", "w_attn_o", "w_dw_conv", "b_dw_conv",
               "g_conv_ln", "b_conv_ln", "w_conv_o", "b_conv_o", "w_mix_o", "g_pre_ffn", "g_post_ffn", "w_up", "w_dw_ffn",
               "b_dw_ffn", "w_down"]
    return (loss, grad_x[None], *[out["grad_" + n] for n in weights], *[out["delta_" + n] for n in weights],
            *[out["new_m_" + n] for n in weights], *[out["new_v_" + n] for n in weights])
```

```python
import functools
import math

import numpy as np
import jax
import jax.numpy as jnp
from jax import lax
from jax.experimental import pallas as pl
from jax.experimental.pallas import tpu as pltpu

F32, BF16 = jnp.float32, jnp.bfloat16
MESH = pl.DeviceIdType.MESH

D = 1024
D_IN = 4608
D_FF = 2816
CONV_K = 31
FFN_K = 3
N_HEADS = 8
CHUNK = 64
LEFT_CHUNKS = 8
MAX_REL = 128
EPS = 1e-6
NEG_INF = -1e30
Q_TILE = 256
WINDOW = Q_TILE + LEFT_CHUNKS * CHUNK
REL_PAD = 384
TOEP = 1024
ROW_TILE = 256
VMEM_LIMIT = 60 * 1024 * 1024

ADAM_LR, ADAM_B1, ADAM_B2, ADAM_EPS, ADAM_WD, ADAM_STEP = 0.001, 0.9, 0.999, 1e-08, 0.01, 10


def _params(sem=None):
    return pltpu.CompilerParams(dimension_semantics=sem, vmem_limit_bytes=VMEM_LIMIT)


def _sds(shape, dtype):
    return jax.ShapeDtypeStruct(tuple(shape), dtype)


ANY = pl.BlockSpec(memory_space=pl.ANY)


class _Carried:
    def __init__(self, ins, out_shapes, aliases, n_sems, start, finish):
        self.ins, self.out_shapes, self.aliases = list(ins), list(out_shapes), dict(aliases)
        self.n_sems, self.start, self.finish = n_sems, start, finish


def _call(body, *, grid, in_specs, out_specs, out_shape, scratch_shapes, sem, name, args, carried=None):
    in_specs, out_specs, out_shape = list(in_specs), list(out_specs), list(out_shape)
    scratch_shapes = list(scratch_shapes)
    if carried is None:
        return pl.pallas_call(body, grid=grid, in_specs=in_specs, out_specs=out_specs, out_shape=out_shape,
                              scratch_shapes=scratch_shapes, compiler_params=_params(sem), name=name)(*args)
    n_in, n_out, n_scr = len(in_specs), len(out_specs), len(scratch_shapes)
    c_in, c_out = len(carried.ins), len(carried.out_shapes)

    def full(*refs):
        pos = [0]

        def take(k):
            part = refs[pos[0]:pos[0] + k]
            pos[0] += k
            return part

        ins, cins, outs, couts, scr = take(n_in), take(c_in), take(n_out), take(c_out), take(n_scr)
        send_sems, recv_sems = take(2)
        first = last = None
        for d, size in enumerate(grid):
            pid = pl.program_id(d)
            first = (pid == 0) if first is None else first & (pid == 0)
            last = (pid == size - 1) if last is None else last & (pid == size - 1)

        @pl.when(first)
        def _():
            carried.start(cins, couts, send_sems, recv_sems)

        body(*ins, *outs, *scr)

        @pl.when(last)
        def _():
            carried.finish(cins, couts, send_sems, recv_sems)

    sems = [pltpu.SemaphoreType.DMA((carried.n_sems,)), pltpu.SemaphoreType.DMA((carried.n_sems,))]
    return pl.pallas_call(
        full, grid=grid, in_specs=in_specs + [ANY] * c_in, out_specs=out_specs + [ANY] * c_out,
        out_shape=out_shape + carried.out_shapes, scratch_shapes=scratch_shapes + sems,
        input_output_aliases={n_in + k: n_out + v for k, v in carried.aliases.items()},
        compiler_params=_params(tuple("arbitrary" for _ in grid)), name=name,
    )(*args, *carried.ins)


def _run_carried(carried, name):
    c_in = len(carried.ins)

    def body(*refs):
        cins, couts = refs[:c_in], refs[c_in:c_in + len(carried.out_shapes)]
        send_sems, recv_sems = refs[-2:]
        carried.start(cins, couts, send_sems, recv_sems)
        carried.finish(cins, couts, send_sems, recv_sems)

    return pl.pallas_call(
        body, in_specs=[ANY] * c_in, out_specs=[ANY] * len(carried.out_shapes), out_shape=carried.out_shapes,
        scratch_shapes=[pltpu.SemaphoreType.DMA((carried.n_sems,)), pltpu.SemaphoreType.DMA((carried.n_sems,))],
        input_output_aliases=carried.aliases, name=name,
    )(*carried.ins)


def _matmul(a, b, *, form, out_dtype, tm, tn, tk, name, bias=None, add=None, out_sharded=False, carried=None):
    b3 = b.ndim == 3
    if form == "nn":
        m, k = a.shape
        n = b.shape[0] * b.shape[2] if b3 else b.shape[1]
        dn = (((1,), (0,)), ((), ()))
        a_spec = pl.BlockSpec((tm, tk), lambda i, j, kk: (i, kk))
        b_spec = (pl.BlockSpec((None, tk, tn), lambda i, j, kk: (j, kk, 0)) if b3
                  else pl.BlockSpec((tk, tn), lambda i, j, kk: (kk, j)))
    elif form == "nt":
        m, k = a.shape
        n = b.shape[1] if b3 else b.shape[0]
        dn = (((1,), (1,)), ((), ()))
        a_spec = pl.BlockSpec((tm, tk), lambda i, j, kk: (i, kk))
        b_spec = (pl.BlockSpec((None, tn, tk), lambda i, j, kk: (kk, j, 0)) if b3
                  else pl.BlockSpec((tn, tk), lambda i, j, kk: (j, kk)))
    else:
        k, m = a.shape
        n = b.shape[1]
        dn = (((0,), (0,)), ((), ()))
        a_spec = pl.BlockSpec((tk, tm), lambda i, j, kk: (kk, i))
        b_spec = pl.BlockSpec((tk, tn), lambda i, j, kk: (kk, j))
    assert m % tm == 0 and n % tn == 0 and k % tk == 0, (name, m, n, k, tm, tn, tk)
    nk = k // tk
    in_specs, args = [a_spec, b_spec], [a, b]
    if bias is not None:
        in_specs.append(pl.BlockSpec((1, tn), lambda i, j, kk: (0, j)))
        args.append(bias)
    if add is not None:
        in_specs.append(pl.BlockSpec((tm, tn), lambda i, j, kk: (i, j)))
        args.append(add)
    if out_sharded:
        out_shape = _sds((n // tn, m, tn), out_dtype)
        out_spec = pl.BlockSpec((None, tm, tn), lambda i, j, kk: (j, i, 0))
    else:
        out_shape = _sds((m, n), out_dtype)
        out_spec = pl.BlockSpec((tm, tn), lambda i, j, kk: (i, j))

    def body(*refs):
        a_ref, b_ref = refs[0], refs[1]
        pos = 2
        bias_ref = add_ref = None
        if bias is not None:
            bias_ref, pos = refs[pos], pos + 1
        if add is not None:
            add_ref, pos = refs[pos], pos + 1
        o_ref = refs[pos]
        av, bv = a_ref[...], b_ref[...]
        if av.dtype != BF16:
            av = av.astype(BF16)
        if bv.dtype != BF16:
            bv = bv.astype(BF16)
        p = lax.dot_general(av, bv, dn, preferred_element_type=F32)

        def finish(acc):
            if bias_ref is not None:
                acc = acc + bias_ref[...]
            if add_ref is not None:
                acc = acc + add_ref[...]
            o_ref[...] = acc.astype(o_ref.dtype)

        if nk == 1:
            finish(p)
        else:
            acc_ref = refs[pos + 1]
            kk = pl.program_id(2)

            @pl.when(kk == 0)
            def _():
                acc_ref[...] = p

            @pl.when(kk > 0)
            def _():
                acc_ref[...] += p

            @pl.when(kk == nk - 1)
            def _():
                finish(acc_ref[...])

    res = _call(body, grid=(m // tm, n // tn, nk), in_specs=in_specs, out_specs=[out_spec], out_shape=[out_shape],
                scratch_shapes=[pltpu.VMEM((tm, tn), F32)] if nk > 1 else [],
                sem=("parallel", "parallel", "arbitrary"), name=name, args=args, carried=carried)
    return res[0] if carried is None else (res[0], res[1:])


def _rowcall(fn, rows, consts, row_outs, acc_outs, *, name, tm=ROW_TILE, col_grid=1):
    n_rows = rows[0][0].shape[0]
    assert n_rows % tm == 0
    grid = (col_grid, n_rows // tm)
    in_specs = [pl.BlockSpec((tm, w), functools.partial(lambda c, i, cb: (i, cb + c), cb=cb)) for _, w, cb in rows]
    in_specs += [pl.BlockSpec(k.shape, functools.partial(lambda c, i, nd: (0,) * nd, nd=k.ndim)) for k in consts]
    out_specs = [pl.BlockSpec((tm, w), lambda c, i: (i, c)) for _, _, _, w in row_outs]
    out_specs += [pl.BlockSpec((r, w), lambda c, i: (0, c)) for r, _, w in acc_outs]
    out_shape = [_sds((nr, nc), dt) for nr, nc, dt, _ in row_outs] + [_sds((r, nc), F32) for r, nc, _ in acc_outs]
    n_in, n_ro = len(rows) + len(consts), len(row_outs)

    def body(*refs):
        res = fn(*[r[...] for r in refs[:n_in]])
        if not isinstance(res, (tuple, list)):
            res = (res,)
        outs = refs[n_in:]
        for o_ref, val in zip(outs[:n_ro], res[:n_ro]):
            o_ref[...] = val.astype(o_ref.dtype)
        if acc_outs:
            first = pl.program_id(1) == 0

            @pl.when(first)
            def _():
                for o_ref, val in zip(outs[n_ro:], res[n_ro:]):
                    o_ref[...] = val

            @pl.when(jnp.logical_not(first))
            def _():
                for o_ref, val in zip(outs[n_ro:], res[n_ro:]):
                    o_ref[...] += val

    out = pl.pallas_call(
        body, grid=grid, in_specs=in_specs, out_specs=out_specs, out_shape=out_shape,
        compiler_params=_params(("arbitrary", "arbitrary")), name=name,
    )(*[r[0] for r in rows], *consts)
    return out


def _matmul_rows(b, *, form, tm, tk, fn, rows, consts, row_outs, acc_outs, name, a=None, a_rows=None, a_fn=None,
                 carried=None):
    b3 = b.ndim == 3
    n = b.shape[1] if (b3 or form == "nt") else b.shape[1]
    if form == "nn":
        k, n = b.shape
        b_spec = pl.BlockSpec((tk, n), lambda i, kk: (kk, 0))
        dn = (((1,), (0,)), ((), ()))
    else:
        n = b.shape[1] if b3 else b.shape[0]
        k = b.shape[0] * b.shape[2] if b3 else b.shape[1]
        b_spec = (pl.BlockSpec((None, n, tk), lambda i, kk: (kk, 0, 0)) if b3
                  else pl.BlockSpec((n, tk), lambda i, kk: (0, kk)))
        dn = (((1,), (1,)), ((), ()))
    nk = k // tk
    lhs_in = [(a, tk, 0)] if a is not None else list(a_rows)
    assert a is not None or nk == 1
    m = lhs_in[0][0].shape[0]
    n_lhs = len(lhs_in)
    in_specs = [pl.BlockSpec((tm, tk), lambda i, kk: (i, kk))] if a is not None else [
        pl.BlockSpec((tm, w), functools.partial(lambda i, kk, cb: (i, cb), cb=cb)) for _, w, cb in a_rows]
    in_specs.append(b_spec)
    in_specs += [pl.BlockSpec((tm, w), functools.partial(lambda i, kk, cb: (i, cb), cb=cb)) for _, w, cb in rows]
    in_specs += [pl.BlockSpec(c.shape, functools.partial(lambda i, kk, nd: (0,) * nd, nd=c.ndim)) for c in consts]
    out_specs = [pl.BlockSpec((tm, w), lambda i, kk: (i, 0)) for _, w in row_outs]
    out_specs += [pl.BlockSpec((r, w), lambda i, kk: (0, 0)) for r, w in acc_outs]
    out_shape = [_sds((m, w), dt) for dt, w in row_outs] + [_sds((r, w), F32) for r, w in acc_outs]
    n_rows, n_consts, n_ro, n_acc = len(rows), len(consts), len(row_outs), len(acc_outs)

    def body(*refs):
        pos = n_lhs + 1
        row_refs, const_refs = refs[pos:pos + n_rows], refs[pos + n_rows:pos + n_rows + n_consts]
        pos += n_rows + n_consts
        out_refs, acc_refs = refs[pos:pos + n_ro], refs[pos + n_ro:pos + n_ro + n_acc]
        i, kk = pl.program_id(0), pl.program_id(1)
        lhs = refs[0][...] if a is not None else a_fn(*[r[...] for r in refs[:n_lhs]]).astype(BF16)
        p = lax.dot_general(lhs, refs[n_lhs][...], dn, preferred_element_type=F32)

        def finish(acc):
            extra = [r[...] for r in row_refs] + [c[...] for c in const_refs]
            res = fn(acc, lhs, *extra) if a is None else fn(acc, *extra)
            for o_ref, val in zip(out_refs, res[:n_ro]):
                o_ref[...] = val.astype(o_ref.dtype)
            if n_acc:
                @pl.when(i == 0)
                def _():
                    for o_ref, val in zip(acc_refs, res[n_ro:]):
                        o_ref[...] = val

                @pl.when(i > 0)
                def _():
                    for o_ref, val in zip(acc_refs, res[n_ro:]):
                        o_ref[...] += val

        if nk == 1:
            finish(p)
        else:
            acc_ref = refs[pos + n_ro + n_acc]

            @pl.when(kk == 0)
            def _():
                acc_ref[...] = p

            @pl.when(kk > 0)
            def _():
                acc_ref[...] += p

            @pl.when(kk == nk - 1)
            def _():
                finish(acc_ref[...])

    res = _call(body, grid=(m // tm, nk), in_specs=in_specs, out_specs=out_specs, out_shape=out_shape,
                scratch_shapes=[pltpu.VMEM((tm, n), F32)] if nk > 1 else [], sem=("arbitrary", "arbitrary"),
                name=name, args=[r[0] for r in lhs_in] + [b] + [r[0] for r in rows] + list(consts), carried=carried)
    own = n_ro + n_acc
    return res[:own] if carried is None else (res[:own], res[own:])


def _colsum(v):
    return jnp.sum(v, axis=0, keepdims=True)


def _sigmoid(v):
    return 1.0 / (1.0 + jnp.exp(-v))


_GELU_C = math.sqrt(2.0 / math.pi)


def _gelu(v):
    return 0.5 * v * (1.0 + jnp.tanh(_GELU_C * (v + 0.044715 * (v * v * v))))


def _gelu_and_grad(v):
    th = jnp.tanh(_GELU_C * (v + 0.044715 * (v * v * v)))
    g = 0.5 * v * (1.0 + th)
    dg = 0.5 * (1.0 + th) + 0.5 * v * (1.0 - th * th) * (_GELU_C * (1.0 + 3.0 * 0.044715 * (v * v)))
    return g, dg


def _rms_stats(v):
    r = lax.rsqrt(jnp.mean(v * v, axis=-1, keepdims=True) + EPS)
    return v * r, r


def _rms_bwd(dn, vn, r):
    return r * (dn - vn * jnp.mean(dn * vn, axis=-1, keepdims=True))


def _pre_norm(x, g, sc, sh, name):
    def fn(xv, gv, scv, shv):
        xn, _ = _rms_stats(xv)
        return (xn * gv) * (1.0 + scv) + shv
    return _rowcall(fn, [(x, D, 0)], [g, sc, sh], [(x.shape[0], D, BF16, D)], [], name=name)[0]


def _pre_norm_bwd(dh, x, dx_other, g, sc, name):
    def fn(dhv, xv, dov, gv, scv):
        xn, r = _rms_stats(xv)
        yn = xn * gv
        dyn = dhv * (1.0 + scv)
        dx = _rms_bwd(dyn * gv, xn, r)
        return dov + dx, _colsum(dhv), _colsum(dhv * yn), _colsum(dyn * xn)
    t = x.shape[0]
    return _rowcall(fn, [(dh, D, 0), (x, D, 0), (dx_other, D, 0)], [g, sc], [(t, D, F32, D)],
                    [(1, D, D)] * 3, name=name)


def _post_res(x, ypre, g, gt, name):
    def fn(xv, yv, gv, gtv):
        yn, _ = _rms_stats(yv)
        return xv + gtv * (yn * gv)
    return _rowcall(fn, [(x, D, 0), (ypre, D, 0)], [g, gt], [(x.shape[0], D, F32, D)], [], name=name)[0]


def _post_res_bwd(dxo, ypre, g, gt, name):
    def fn(dv, yv, gv, gtv):
        yn, r = _rms_stats(yv)
        dyn = dv * gtv
        dy = _rms_bwd(dyn * gv, yn, r)
        return dy, _colsum(dyn * yn), _colsum(dv * (yn * gv))
    t = ypre.shape[0]
    return _rowcall(fn, [(dxo, D, 0), (ypre, D, 0)], [g, gt], [(t, D, BF16, D)], [(1, D, D)] * 2, name=name)


def _ffn_tail(x1, yf, target, g, gt, name):
    def fn(xv, yv, tv, gv, gtv):
        yn, r = _rms_stats(yv)
        e = xv + gtv * (yn * gv) - tv
        dx2 = e * (1.0 / D)
        dyn = dx2 * gtv
        dy = _rms_bwd(dyn * gv, yn, r)
        return dx2, dy, _colsum(e * e) * (0.5 / D), _colsum(dyn * yn), _colsum(dx2 * (yn * gv))
    t = x1.shape[0]
    return _rowcall(fn, [(x1, D, 0), (yf, D, 0), (target, D, 0)], [g, gt], [(t, D, F32, D), (t, D, BF16, D)],
                    [(1, D, D)] * 3, name=name)


def _gate_merge(a, cb, z, name):
    def fn(av, cv, gav, gbv):
        return _sigmoid(gav) * av + _sigmoid(gbv) * cv
    t = a.shape[0]
    return _rowcall(fn, [(a, 512, 0), (cb, 512, 0), (z, 512, 5), (z, 512, 7)], [],
                    [(t, D, BF16, 512)], [], name=name, col_grid=2)[0]


def _gate_merge_bwd(dy, a, cb, z, name):
    def fn(dv, av, cv, gav, gbv):
        sa, sb = _sigmoid(gav), _sigmoid(gbv)
        dcb = dv * sb
        dga = dv * av * (sa * (1.0 - sa))
        dgb = dv * cv * (sb * (1.0 - sb))
        return dv * sa, dcb, dga, dgb, _colsum(dcb), _colsum(dga), _colsum(dgb)
    t = a.shape[0]
    return _rowcall(fn, [(dy, 512, 0), (a, 512, 0), (cb, 512, 0), (z, 512, 5), (z, 512, 7)], [],
                    [(t, D, BF16, 512)] * 4, [(1, D, 512)] * 3, name=name, col_grid=2)


CONV_HALO = 32


def _layer_norm_parts(u):
    mu = jnp.mean(u, axis=-1, keepdims=True)
    d = u - mu
    r = lax.rsqrt(jnp.mean(d * d, axis=-1, keepdims=True) + EPS)
    return d * r, r


LANES = 128
SUBLANE_ROWS = 8
CONV_ROWS = 64


def _lanes(c):
    return slice(c * LANES, (c + 1) * LANES)


def _conv_branch(z, w_dw, b_dw, g_ln, b_ln, name, tm=ROW_TILE):
    t = z.shape[0]
    per = tm // CONV_HALO
    n_chunks = 512 // LANES

    def body(ga_ref, gb_ref, gah_ref, gbh_ref, w_ref, b_ref, g_ref, bl_ref, u1_ref, u3_ref, scr):
        i = pl.program_id(0)
        u0h = jnp.where(i > 0, gah_ref[...] * _sigmoid(gbh_ref[...]), 0.0)
        u0 = ga_ref[...] * _sigmoid(gb_ref[...])
        for c in range(n_chunks):
            scr[c, 0:CONV_HALO, :] = u0h[:, _lanes(c)]
            scr[c, CONV_HALO:CONV_HALO + tm, :] = u0[:, _lanes(c)]
        for c in range(n_chunks):
            for r0 in range(0, tm, CONV_ROWS):
                acc = jnp.zeros((CONV_ROWS, LANES), F32) + b_ref[:, _lanes(c)]
                for j in range(CONV_K):
                    acc = acc + w_ref[j:j + 1, _lanes(c)] * scr[c, pl.ds(r0 + CONV_HALO - (CONV_K - 1) + j, CONV_ROWS), :]
                u1_ref[r0:r0 + CONV_ROWS, _lanes(c)] = acc
        xh, _ = _layer_norm_parts(u1_ref[...])
        u2 = xh * g_ref[...] + bl_ref[...]
        u3_ref[...] = (u2 * _sigmoid(u2)).astype(BF16)

    cur = lambda cb: pl.BlockSpec((tm, 512), lambda i: (i, cb))
    halo = lambda cb: pl.BlockSpec((CONV_HALO, 512), lambda i: (jnp.maximum(i * per - 1, 0), cb))
    whole = lambda a: pl.BlockSpec(a.shape, lambda i: (0, 0))
    return pl.pallas_call(
        body, grid=(t // tm,),
        in_specs=[cur(3), cur(4), halo(3), halo(4), whole(w_dw), whole(b_dw), whole(g_ln), whole(b_ln)],
        out_specs=[pl.BlockSpec((tm, 512), lambda i: (i, 0))] * 2,
        out_shape=[_sds((t, 512), F32), _sds((t, 512), BF16)],
        scratch_shapes=[pltpu.VMEM((n_chunks, CONV_HALO + tm, LANES), F32)],
        compiler_params=_params(("arbitrary",)), name=name,
    )(z, z, z, z, w_dw, b_dw, g_ln, b_ln)


def _conv_branch_bwd(du3, u1, z, w_dw, g_ln, b_ln, name, tm=ROW_TILE, carried=None):
    t = z.shape[0]
    per = tm // CONV_HALO
    last = t // tm - 1
    n_chunks = 512 // LANES

    def du1_of(du3v, u1v, g, b):
        xh, r = _layer_norm_parts(u1v)
        u2 = xh * g + b
        s = _sigmoid(u2)
        du2 = du3v * (s * (1.0 + u2 * (1.0 - s)))
        dxh = du2 * g
        du1 = r * (dxh - jnp.mean(dxh, axis=-1, keepdims=True) - xh * jnp.mean(dxh * xh, axis=-1, keepdims=True))
        return du1, du2, xh

    def body(d_ref, u_ref, dn_ref, un_ref, ga_ref, gb_ref, gah_ref, gbh_ref, w_ref, g_ref, bl_ref,
             dglu_ref, dw_ref, dbdw_ref, dg_ref, dbl_ref, dbin_ref, scr, scd):
        i = pl.program_id(0)
        g, b = g_ref[...], bl_ref[...]
        du1, du2, xh = du1_of(d_ref[...], u_ref[...], g, b)
        du1n, _, _ = du1_of(dn_ref[...], un_ref[...], g, b)
        du1n = jnp.where(i < last, du1n, 0.0)
        sgb = _sigmoid(gb_ref[...])
        ga = ga_ref[...]
        u0 = ga * sgb
        u0h = jnp.where(i > 0, gah_ref[...] * _sigmoid(gbh_ref[...]), 0.0)
        for c in range(n_chunks):
            scd[c, 0:tm, :] = du1[:, _lanes(c)]
            scd[c, tm:tm + CONV_HALO, :] = du1n[:, _lanes(c)]
            scr[c, 0:CONV_HALO, :] = u0h[:, _lanes(c)]
            scr[c, CONV_HALO:CONV_HALO + tm, :] = u0[:, _lanes(c)]

        @pl.when(i == 0)
        def _():
            for ref in (dw_ref, dbdw_ref, dg_ref, dbl_ref, dbin_ref):
                ref[...] = jnp.zeros_like(ref)

        dsg = ga * (sgb * (1.0 - sgb))
        for c in range(n_chunks):
            gate = slice(512 + c * LANES, 512 + (c + 1) * LANES)
            for r0 in range(0, tm, CONV_ROWS):
                rows = slice(r0, r0 + CONV_ROWS)
                du0 = jnp.zeros((CONV_ROWS, LANES), F32)
                for j in range(CONV_K):
                    du0 = du0 + w_ref[j:j + 1, _lanes(c)] * scd[c, pl.ds(r0 + CONV_K - 1 - j, CONV_ROWS), :]
                dga = du0 * sgb[rows, _lanes(c)]
                dgb = du0 * dsg[rows, _lanes(c)]
                dglu_ref[rows, _lanes(c)] = dga.astype(BF16)
                dglu_ref[rows, gate] = dgb.astype(BF16)
                dbin_ref[:, _lanes(c)] += _colsum(dga)
                dbin_ref[:, gate] += _colsum(dgb)
            for j in range(CONV_K):
                dwj = jnp.zeros((SUBLANE_ROWS, LANES), F32)
                for r0 in range(0, tm, CONV_ROWS):
                    prod = (scd[c, pl.ds(r0, CONV_ROWS), :]
                            * scr[c, pl.ds(r0 + CONV_HALO - (CONV_K - 1) + j, CONV_ROWS), :])
                    dwj = dwj + jnp.sum(prod.reshape(CONV_ROWS // SUBLANE_ROWS, SUBLANE_ROWS, LANES), axis=0)
                dw_ref[j:j + 1, _lanes(c)] += _colsum(dwj)
        dbdw_ref[...] += _colsum(du1)
        dg_ref[...] += _colsum(du2 * xh)
        dbl_ref[...] += _colsum(du2)

    cur = lambda cb: pl.BlockSpec((tm, 512), lambda i: (i, cb))
    prev = lambda cb: pl.BlockSpec((CONV_HALO, 512), lambda i: (jnp.maximum(i * per - 1, 0), cb))
    nxt = pl.BlockSpec((CONV_HALO, 512), lambda i: (jnp.minimum((i + 1) * per, t // CONV_HALO - 1), 0))
    whole = lambda a: pl.BlockSpec(a.shape, lambda i: (0, 0))
    acc = lambda r, w: pl.BlockSpec((r, w), lambda i: (0, 0))
    res = _call(
        body, grid=(t // tm,),
        in_specs=[cur(0), cur(0), nxt, nxt, cur(3), cur(4), prev(3), prev(4), whole(w_dw), whole(g_ln), whole(b_ln)],
        out_specs=[pl.BlockSpec((tm, 1024), lambda i: (i, 0)), acc(CONV_K, 512), acc(1, 512), acc(1, 512),
                   acc(1, 512), acc(1, 1024)],
        out_shape=[_sds((t, 1024), BF16), _sds((CONV_K, 512), F32), _sds((1, 512), F32), _sds((1, 512), F32),
                   _sds((1, 512), F32), _sds((1, 1024), F32)],
        scratch_shapes=[pltpu.VMEM((n_chunks, CONV_HALO + tm, LANES), F32),
                        pltpu.VMEM((n_chunks, tm + CONV_HALO, LANES), F32)],
        sem=("arbitrary",), name=name, args=(du3, u1, du3, u1, z, z, z, z, w_dw, g_ln, b_ln), carried=carried)
    return res[:6] if carried is None else (res[:6], res[6:])


FF_BLOCK = D_FF // 2
FF_HALO = 8
FF_CHUNKS = FF_BLOCK // LANES


def _ffn_conv(w_ref, b_ref, scr, k, rows):
    acc = b_ref[:, _lanes(k)] + w_ref[0:1, _lanes(k)] * scr[k, pl.ds(FF_HALO - 2, rows), :]
    acc = acc + w_ref[1:2, _lanes(k)] * scr[k, pl.ds(FF_HALO - 1, rows), :]
    return acc + w_ref[2:3, _lanes(k)] * scr[k, pl.ds(FF_HALO, rows), :]


def _ffn_act(up, w3, b3, name, tm=ROW_TILE):
    t = up.shape[0]
    per = tm // FF_HALO
    wide = 2 * FF_BLOCK

    def body(u_ref, uh_ref, w_ref, b_ref, o_ref, scr):
        i = pl.program_id(1)
        for k in range(2 * FF_CHUNKS):
            scr[k, 0:FF_HALO, :] = jnp.where(i > 0, uh_ref[:, _lanes(k)], 0.0)
            scr[k, FF_HALO:FF_HALO + tm, :] = u_ref[:, _lanes(k)]
        for cc in range(FF_CHUNKS):
            val = _ffn_conv(w_ref, b_ref, scr, cc, tm)
            gate = _ffn_conv(w_ref, b_ref, scr, FF_CHUNKS + cc, tm)
            o_ref[:, _lanes(cc)] = (_gelu(gate) * val).astype(BF16)

    return pl.pallas_call(
        body, grid=(2, t // tm),
        in_specs=[pl.BlockSpec((tm, wide), lambda c, i: (i, c)),
                  pl.BlockSpec((FF_HALO, wide), lambda c, i: (jnp.maximum(i * per - 1, 0), c)),
                  pl.BlockSpec((FFN_K, wide), lambda c, i: (0, c)),
                  pl.BlockSpec((1, wide), lambda c, i: (0, c))],
        out_specs=pl.BlockSpec((tm, FF_BLOCK), lambda c, i: (i, c)),
        out_shape=_sds((t, D_FF), BF16),
        scratch_shapes=[pltpu.VMEM((2 * FF_CHUNKS, FF_HALO + tm, LANES), F32)],
        compiler_params=_params(("arbitrary", "arbitrary")), name=name,
    )(up, up, w3, b3)


def _ffn_act_bwd(dact, up, w3, b3, name, tm=ROW_TILE):
    t = up.shape[0]
    per = tm // FF_HALO
    wide = 2 * FF_BLOCK
    last = t // tm - 1
    ext = tm + FF_HALO

    def body(u_ref, up_ref, un_ref, d_ref, dn_ref, w_ref, b_ref, o_ref, dw_ref, db_ref, scr, scd):
        i = pl.program_id(1)
        for k in range(2 * FF_CHUNKS):
            scr[k, 0:FF_HALO, :] = jnp.where(i > 0, up_ref[:, _lanes(k)], 0.0)
            scr[k, FF_HALO:FF_HALO + tm, :] = u_ref[:, _lanes(k)]
            scr[k, FF_HALO + tm:FF_HALO + ext, :] = un_ref[:, _lanes(k)]
        dn = jnp.where(i < last, dn_ref[...], 0.0)

        @pl.when(i == 0)
        def _():
            dw_ref[...] = jnp.zeros_like(dw_ref)
            db_ref[...] = jnp.zeros_like(db_ref)

        for cc in range(FF_CHUNKS):
            val = _ffn_conv(w_ref, b_ref, scr, cc, ext)
            gel, dgel = _gelu_and_grad(_ffn_conv(w_ref, b_ref, scr, FF_CHUNKS + cc, ext))
            da = jnp.concatenate([d_ref[:, _lanes(cc)], dn[:, _lanes(cc)]], axis=0)
            scd[cc] = da * gel
            scd[FF_CHUNKS + cc] = da * val * dgel
            for k in (cc, FF_CHUNKS + cc):
                shifted = [scd[k, pl.ds(FFN_K - 1 - j, tm), :] for j in range(FFN_K)]
                ucur = scr[k, pl.ds(FF_HALO, tm), :]
                o_ref[:, _lanes(k)] = (w_ref[0:1, _lanes(k)] * shifted[0] + w_ref[1:2, _lanes(k)] * shifted[1]
                                       + w_ref[2:3, _lanes(k)] * shifted[2]).astype(BF16)
                for j in range(FFN_K):
                    dw_ref[j:j + 1, _lanes(k)] += _colsum(shifted[j] * ucur)
                db_ref[:, _lanes(k)] += _colsum(shifted[FFN_K - 1])

    nblk = t // FF_HALO
    return pl.pallas_call(
        body, grid=(2, t // tm),
        in_specs=[pl.BlockSpec((tm, wide), lambda c, i: (i, c)),
                  pl.BlockSpec((FF_HALO, wide), lambda c, i: (jnp.maximum(i * per - 1, 0), c)),
                  pl.BlockSpec((FF_HALO, wide), lambda c, i: (jnp.minimum((i + 1) * per, nblk - 1), c)),
                  pl.BlockSpec((tm, FF_BLOCK), lambda c, i: (i, c)),
                  pl.BlockSpec((FF_HALO, FF_BLOCK), lambda c, i: (jnp.minimum((i + 1) * per, nblk - 1), c)),
                  pl.BlockSpec((FFN_K, wide), lambda c, i: (0, c)),
                  pl.BlockSpec((1, wide), lambda c, i: (0, c))],
        out_specs=[pl.BlockSpec((tm, wide), lambda c, i: (i, c)),
                   pl.BlockSpec((FFN_K, wide), lambda c, i: (0, c)),
                   pl.BlockSpec((1, wide), lambda c, i: (0, c))],
        out_shape=[_sds((t, 2 * D_FF), BF16), _sds((FFN_K, 2 * D_FF), F32), _sds((1, 2 * D_FF), F32)],
        scratch_shapes=[pltpu.VMEM((2 * FF_CHUNKS, FF_HALO + ext, LANES), F32),
                        pltpu.VMEM((2 * FF_CHUNKS, ext, LANES), F32)],
        compiler_params=_params(("arbitrary", "arbitrary")), name=name,
    )(up, up, up, dact, dact, w3, b3)


def _toeplitz_map():
    f = np.zeros((TOEP, REL_PAD), np.float32)
    for m in range(TOEP - 1):
        rel = (WINDOW - 1) - m
        f[m, int(np.clip(rel, -MAX_REL, MAX_REL)) + MAX_REL] = 1.0
    return f


def _split3(v):
    hi = v.astype(BF16)
    r1 = v - hi.astype(F32)
    mid = r1.astype(BF16)
    lo = (r1 - mid.astype(F32)).astype(BF16)
    return hi, mid, lo


def _exact_select(v, sel):
    out = None
    for part in _split3(v):
        p = jnp.dot(part, sel, preferred_element_type=F32)
        out = p if out is None else out + p
    return out


def _select_call(v, sel, name):
    def body(v_ref, s_ref, o_ref):
        o_ref[...] = _exact_select(v_ref[...], s_ref[...])
    return pl.pallas_call(body, out_shape=_sds((v.shape[0], sel.shape[1]), F32), name=name)(v, sel)


def _band_bias(gen_row):
    b0 = jnp.broadcast_to(gen_row, (Q_TILE, TOEP))
    bias = pltpu.roll(b0, TOEP - 255, 1, stride=1, stride_axis=0)[:, :WINDOW]
    qq = lax.broadcasted_iota(jnp.int32, (Q_TILE, WINDOW), 0) // CHUNK
    kc = lax.broadcasted_iota(jnp.int32, (Q_TILE, WINDOW), 1) // CHUNK
    return jnp.where((kc >= qq) & (kc <= qq + LEFT_CHUNKS), bias, NEG_INF)


PAD_ROWS = WINDOW - Q_TILE
NT_DIMS = (((1,), (1,)), ((), ()))
TN_DIMS = (((0,), (0,)), ((), ()))


def _head_mask(hh):
    lane = lax.broadcasted_iota(jnp.int32, (1, 128), 1)
    return (lane < 64) if hh == 0 else (lane >= 64)


SOFTMAX_ROWS = 16


def _probs_block(s_scr, bias, hh, rows, i):
    s = s_scr[rows, :] + bias[hh, rows, :]
    col = lax.broadcasted_iota(jnp.int32, (SOFTMAX_ROWS, WINDOW), 1)
    s = jnp.where(col >= PAD_ROWS - Q_TILE * i, s, NEG_INF)
    p = jnp.exp(s - jnp.max(s, axis=-1, keepdims=True))
    return p / jnp.sum(p, axis=-1, keepdims=True)


def _attention(z, gen, name, carried=None):
    t = z.shape[0]
    n_i = t // Q_TILE

    def body(q_ref, k_ref, v_ref, g_ref, o_ref, kpad, vpad, bias, s_scr, p_scr):
        hp, i = pl.program_id(0), pl.program_id(1)

        @pl.when(i == 0)
        def _():
            kpad[0:PAD_ROWS, :] = jnp.zeros((PAD_ROWS, 128), BF16)
            vpad[0:PAD_ROWS, :] = jnp.zeros((PAD_ROWS, 128), BF16)
            kpad[PAD_ROWS:PAD_ROWS + t, :] = k_ref[...].astype(BF16)
            vpad[PAD_ROWS:PAD_ROWS + t, :] = v_ref[...].astype(BF16)
            for hh in range(2):
                bias[hh] = _band_bias(g_ref[pl.ds(2 * hp + hh, 1), :])

        start = pl.multiple_of(i * Q_TILE, Q_TILE)
        kw = kpad[pl.ds(start, WINDOW), :]
        vw = vpad[pl.ds(start, WINDOW), :]
        q = q_ref[...] * (CHUNK ** -0.5)
        out = None
        for hh in range(2):
            mask = _head_mask(hh)
            qm = jnp.where(mask, q, 0.0).astype(BF16)
            s_scr[hh] = lax.dot_general(qm, kw, NT_DIMS, preferred_element_type=F32)
            for r0 in range(0, Q_TILE, SOFTMAX_ROWS):
                rows = slice(r0, r0 + SOFTMAX_ROWS)
                p_scr[hh, rows, :] = _probs_block(s_scr.at[hh], bias, hh, rows, i).astype(BF16)
            o = jnp.dot(p_scr[hh], vw, preferred_element_type=F32)
            out = jnp.where(mask, o, 0.0) if out is None else jnp.where(mask, o, out)
        o_ref[...] = out.astype(BF16)

    res = _call(
        body, grid=(4, n_i),
        in_specs=[pl.BlockSpec((Q_TILE, 128), lambda h, i: (i, h)),
                  pl.BlockSpec((t, 128), lambda h, i: (0, 4 + h)),
                  pl.BlockSpec((t, 128), lambda h, i: (0, 8 + h)),
                  pl.BlockSpec((N_HEADS, TOEP), lambda h, i: (0, 0))],
        out_specs=[pl.BlockSpec((Q_TILE, 128), lambda h, i: (i, h))],
        out_shape=[_sds((t, 512), BF16)],
        scratch_shapes=[pltpu.VMEM((PAD_ROWS + t, 128), BF16), pltpu.VMEM((PAD_ROWS + t, 128), BF16),
                        pltpu.VMEM((2, Q_TILE, WINDOW), F32), pltpu.VMEM((2, Q_TILE, WINDOW), F32),
                        pltpu.VMEM((2, Q_TILE, WINDOW), BF16)],
        sem=("arbitrary", "arbitrary"), name=name, args=(z, z, z, gen), carried=carried)
    return res[0] if carried is None else (res[0], res[1:])


def _attention_bwd(z, datt, gen, name, carried=None):
    t = z.shape[0]
    n_i = t // Q_TILE

    def body(q_ref, k_ref, v_ref, d_ref, g_ref, dq_ref, dk_ref, dv_ref, sq_ref, sk_ref, sv_ref, dg_ref,
             kpad, vpad, dkacc, dvacc, bias, dsacc, s_scr, dp_scr, p_scr, ds_scr):
        hp, i = pl.program_id(0), pl.program_id(1)

        @pl.when(i == 0)
        def _():
            kpad[0:PAD_ROWS, :] = jnp.zeros((PAD_ROWS, 128), BF16)
            vpad[0:PAD_ROWS, :] = jnp.zeros((PAD_ROWS, 128), BF16)
            kpad[PAD_ROWS:PAD_ROWS + t, :] = k_ref[...].astype(BF16)
            vpad[PAD_ROWS:PAD_ROWS + t, :] = v_ref[...].astype(BF16)
            dkacc[...] = jnp.zeros_like(dkacc)
            dvacc[...] = jnp.zeros_like(dvacc)
            dsacc[...] = jnp.zeros_like(dsacc)
            for hh in range(2):
                bias[hh] = _band_bias(g_ref[pl.ds(2 * hp + hh, 1), :])

        start = pl.multiple_of(i * Q_TILE, Q_TILE)
        win = pl.ds(start, WINDOW)
        kw = kpad[win, :]
        vw = vpad[win, :]
        q = q_ref[...] * (CHUNK ** -0.5)
        do = d_ref[...]
        dq = None
        for hh in range(2):
            mask = _head_mask(hh)
            qm = jnp.where(mask, q, 0.0).astype(BF16)
            dom = jnp.where(mask, do, 0.0).astype(BF16)
            s_scr[...] = lax.dot_general(qm, kw, NT_DIMS, preferred_element_type=F32)
            dp_scr[...] = lax.dot_general(dom, vw, NT_DIMS, preferred_element_type=F32)
            for r0 in range(0, Q_TILE, SOFTMAX_ROWS):
                rows = slice(r0, r0 + SOFTMAX_ROWS)
                p = _probs_block(s_scr, bias, hh, rows, i)
                dp = dp_scr[rows, :]
                ds = p * (dp - jnp.sum(p * dp, axis=-1, keepdims=True))
                dsacc[hh, rows, :] += ds
                ds_scr[rows, :] = ds.astype(BF16)
                p_scr[rows, :] = p.astype(BF16)
            ds16 = ds_scr[...]
            dqh = jnp.dot(ds16, kw, preferred_element_type=F32) * (CHUNK ** -0.5)
            dq = jnp.where(mask, dqh, 0.0) if dq is None else jnp.where(mask, dqh, dq)
            dkacc[win, :] += lax.dot_general(ds16, qm, TN_DIMS, preferred_element_type=F32)
            dvacc[win, :] += lax.dot_general(p_scr[...], dom, TN_DIMS, preferred_element_type=F32)
        dq_ref[...] = dq.astype(BF16)

        @pl.when(i == 0)
        def _():
            sq_ref[...] = _colsum(dq)

        @pl.when(i > 0)
        def _():
            sq_ref[...] += _colsum(dq)

        @pl.when(i == n_i - 1)
        def _():
            dk = dkacc[PAD_ROWS:PAD_ROWS + t, :]
            dv = dvacc[PAD_ROWS:PAD_ROWS + t, :]
            dk_ref[...] = dk.astype(BF16)
            dv_ref[...] = dv.astype(BF16)
            sk_ref[...] = _colsum(dk)
            sv_ref[...] = _colsum(dv)
            rr = lax.broadcasted_iota(jnp.int32, (Q_TILE, Q_TILE), 0)
            cc = lax.broadcasted_iota(jnp.int32, (Q_TILE, Q_TILE), 1)
            rev = jnp.where(rr + cc == Q_TILE - 1, 1.0, 0.0).astype(BF16)
            for hh in range(2):
                acc = None
                for part in _split3(dsacc[hh]):
                    pr = jnp.dot(rev, part, preferred_element_type=F32)
                    acc = pr if acc is None else acc + pr
                wide = jnp.concatenate([acc, jnp.zeros((Q_TILE, TOEP - WINDOW), F32)], axis=1)
                dg_ref[pl.ds(2 * hp + hh, 1), :] = _colsum(pltpu.roll(wide, 0, 1, stride=1, stride_axis=0))

    col = lambda off: pl.BlockSpec((t, 128), lambda h, i: (0, off + h))
    tile = lambda: pl.BlockSpec((Q_TILE, 128), lambda h, i: (i, h))
    sums = lambda: pl.BlockSpec((1, 128), lambda h, i: (0, h))
    res = _call(
        body, grid=(4, n_i),
        in_specs=[tile(), col(4), col(8), tile(), pl.BlockSpec((N_HEADS, TOEP), lambda h, i: (0, 0))],
        out_specs=[tile(), col(0), col(0), sums(), sums(), sums(), pl.BlockSpec((N_HEADS, TOEP), lambda h, i: (0, 0))],
        out_shape=[_sds((t, 512), BF16)] * 3 + [_sds((1, 512), F32)] * 3 + [_sds((N_HEADS, TOEP), F32)],
        scratch_shapes=[pltpu.VMEM((PAD_ROWS + t, 128), BF16), pltpu.VMEM((PAD_ROWS + t, 128), BF16),
                        pltpu.VMEM((PAD_ROWS + t, 128), F32), pltpu.VMEM((PAD_ROWS + t, 128), F32),
                        pltpu.VMEM((2, Q_TILE, WINDOW), F32), pltpu.VMEM((2, Q_TILE, WINDOW), F32),
                        pltpu.VMEM((Q_TILE, WINDOW), F32), pltpu.VMEM((Q_TILE, WINDOW), F32),
                        pltpu.VMEM((Q_TILE, WINDOW), BF16), pltpu.VMEM((Q_TILE, WINDOW), BF16)],
        sem=("arbitrary", "arbitrary"), name=name, args=(z, z, z, datt, gen), carried=carried)
    return res[:7] if carried is None else (res[:7], res[7:])


def _adamw_math(w, g, m, v):
    m = ADAM_B1 * m + (1.0 - ADAM_B1) * g
    v = ADAM_B2 * v + (1.0 - ADAM_B2) * (g * g)
    m_hat = m / (1.0 - ADAM_B1 ** ADAM_STEP)
    v_hat = v / (1.0 - ADAM_B2 ** ADAM_STEP)
    delta = -ADAM_LR * (m_hat / (jnp.sqrt(v_hat) + ADAM_EPS) + ADAM_WD * w)
    return delta, m, v


def _adamw_many(items, name):
    n = len(items)

    def body(*refs):
        ins, outs = refs[:4 * n], refs[4 * n:]
        for k in range(n):
            w, g, m, v = (r[...] for r in ins[4 * k:4 * k + 4])
            outs[3 * k][...], outs[3 * k + 1][...], outs[3 * k + 2][...] = _adamw_math(w, g, m, v)

    flat = [a for item in items for a in item]
    res = pl.pallas_call(body, out_shape=[_sds(item[0].shape, F32) for item in items for _ in range(3)],
                         name=name)(*flat)
    return [tuple(res[3 * k:3 * k + 3]) for k in range(n)]


def _adamw(w, g, m, v, name):
    r, c = w.shape
    tm = next(cand for cand in (256, 176, 128, 64, 32, 16, 8) if r % cand == 0)
    return _rowcall(_adamw_math, [(w, c, 0), (g, c, 0), (m, c, 0), (v, c, 0)], [],
                    [(r, c, F32, c)] * 3, [], name=name, tm=tm)


def _ada_fwd(c_all, w_shard, b_shard, name):
    n = w_shard.shape[1]
    tn = 512

    def body(c_ref, w_ref, b_ref, o_ref, a_ref):
        cv = c_ref[...]
        act = cv * _sigmoid(cv)
        a_ref[...] = act
        o_ref[...] = jnp.dot(act.astype(BF16), w_ref[...].astype(BF16), preferred_element_type=F32) + b_ref[...]

    return pl.pallas_call(
        body, grid=(n // tn,),
        in_specs=[pl.BlockSpec((8, D), lambda j: (0, 0)), pl.BlockSpec((D, tn), lambda j: (0, j)),
                  pl.BlockSpec((1, tn), lambda j: (0, j))],
        out_specs=[pl.BlockSpec((8, tn), lambda j: (0, j)), pl.BlockSpec((8, D), lambda j: (0, 0))],
        out_shape=[_sds((8, n), F32), _sds((8, D), F32)],
        compiler_params=_params(("arbitrary",)), name=name,
    )(c_all, w_shard, b_shard)


def _ada_bwd_adamw(act_t, dmod_shard, w, m, v, name):
    r, c = w.shape
    tm = 256

    def body(a_ref, d_ref, w_ref, m_ref, v_ref, g_ref, dl_ref, nm_ref, nv_ref):
        g = jnp.dot(a_ref[...], d_ref[...], precision=lax.Precision.HIGHEST, preferred_element_type=F32)
        g_ref[...] = g
        dl_ref[...], nm_ref[...], nv_ref[...] = _adamw_math(w_ref[...], g, m_ref[...], v_ref[...])

    blk = pl.BlockSpec((tm, c), lambda i: (i, 0))
    return pl.pallas_call(
        body, grid=(r // tm,),
        in_specs=[pl.BlockSpec((tm, 8), lambda i: (i, 0)), pl.BlockSpec((8, c), lambda i: (0, 0)), blk, blk, blk],
        out_specs=[blk] * 4, out_shape=[_sds((r, c), F32)] * 4,
        compiler_params=_params(("arbitrary",)), name=name,
    )(act_t, dmod_shard, w, m, v)


def _place():
    return lax.axis_index("x"), lax.axis_index("y"), lax.axis_index("c")


def _flip(v, bit):
    return 1 - v if bit else v


VMEM_SPEC = pl.BlockSpec(memory_space=pltpu.VMEM)


def _allgather8(v, name):
    r, c = v.shape

    def body(v_ref, g_ref, tot_ref, send_sems, recv_sems, local_sem):
        x, y, cc = _place()
        me = 4 * x + 2 * y + cc
        mine = pltpu.make_async_copy(v_ref, g_ref.at[me], local_sem)
        mine.start()
        sends = []
        for k in range(1, 8):
            peer = (_flip(x, k & 4), _flip(y, k & 2), _flip(cc, k & 1))
            cp = pltpu.make_async_remote_copy(src_ref=v_ref, dst_ref=g_ref.at[me], send_sem=send_sems.at[k - 1],
                                              recv_sem=recv_sems.at[k - 1], device_id=peer, device_id_type=MESH)
            cp.start()
            sends.append(cp)
        for k in range(1, 8):
            peer = (_flip(x, k & 4), _flip(y, k & 2), _flip(cc, k & 1))
            theirs = g_ref.at[4 * peer[0] + 2 * peer[1] + peer[2]]
            pltpu.make_async_remote_copy(src_ref=v_ref, dst_ref=theirs, send_sem=send_sems.at[k - 1],
                                         recv_sem=recv_sems.at[k - 1], device_id=peer, device_id_type=MESH).wait_recv()
        for cp in sends:
            cp.wait_send()
        mine.wait()
        tot = g_ref[0]
        for d in range(1, 8):
            tot = tot + g_ref[d]
        tot_ref[...] = tot

    return pl.pallas_call(
        body, in_specs=[VMEM_SPEC], out_specs=[VMEM_SPEC, VMEM_SPEC],
        out_shape=[_sds((8, r, c), F32), _sds((r, c), F32)],
        scratch_shapes=[pltpu.SemaphoreType.DMA((7,)), pltpu.SemaphoreType.DMA((7,)), pltpu.SemaphoreType.DMA],
        compiler_params=pltpu.CompilerParams(vmem_limit_bytes=VMEM_LIMIT), name=name,
    )(v)


def _slot(px, py, swapped):
    return 2 * py + px if swapped else 2 * px + py


def _gather_shards(arrs, swapped, name, in_place=False):
    n = len(arrs)

    def body(*refs):
        ins, outs = refs[:n], refs[n:2 * n]
        send1, recv1, send2, recv2, local_sems = refs[2 * n:]
        x, y, c = _place()
        sibling = (x, y, 1 - c)
        chips = [(_flip(x, k & 2), _flip(y, k & 1)) for k in (1, 2, 3)]
        local_copies, sends = [], []
        for a in range(n):
            h = outs[a].shape[1] // 2
            mine = pl.ds(pl.multiple_of(c * h, 8), h)
            own = _slot(x, y, swapped[a])
            if in_place:
                src = outs[a].at[own, mine]
            else:
                src = ins[a].at[mine]
                lc = pltpu.make_async_copy(ins[a], outs[a].at[own], local_sems.at[a])
                lc.start()
                local_copies.append(lc)
            for j, (px, py) in enumerate(chips):
                cp = pltpu.make_async_remote_copy(
                    src_ref=src, dst_ref=outs[a].at[own, mine], send_sem=send1.at[3 * a + j],
                    recv_sem=recv1.at[3 * a + j], device_id=(px, py, c), device_id_type=MESH)
                cp.start()
                sends.append(cp)
        for a in range(n):
            h = outs[a].shape[1] // 2
            mine = pl.ds(pl.multiple_of(c * h, 8), h)
            for j, (px, py) in enumerate(chips):
                piece = outs[a].at[_slot(px, py, swapped[a]), mine]
                pltpu.make_async_remote_copy(
                    src_ref=piece, dst_ref=piece, send_sem=send1.at[3 * a + j], recv_sem=recv1.at[3 * a + j],
                    device_id=(px, py, c), device_id_type=MESH).wait_recv()
                fwd = pltpu.make_async_remote_copy(
                    src_ref=piece, dst_ref=piece, send_sem=send2.at[3 * a + j], recv_sem=recv2.at[3 * a + j],
                    device_id=sibling, device_id_type=MESH)
                fwd.start()
                sends.append(fwd)
        for a in range(n):
            h = outs[a].shape[1] // 2
            other = pl.ds(pl.multiple_of((1 - c) * h, 8), h)
            for j, (px, py) in enumerate(chips):
                piece = outs[a].at[_slot(px, py, swapped[a]), other]
                pltpu.make_async_remote_copy(
                    src_ref=piece, dst_ref=piece, send_sem=send2.at[3 * a + j], recv_sem=recv2.at[3 * a + j],
                    device_id=sibling, device_id_type=MESH).wait_recv()
        for cp in sends:
            cp.wait_send()
        for lc in local_copies:
            lc.wait()

    dma = lambda k: pltpu.SemaphoreType.DMA((k,))
    return pl.pallas_call(
        body, in_specs=[ANY] * n, out_specs=[ANY] * n,
        out_shape=[_sds(a.shape if in_place else (4,) + a.shape, a.dtype) for a in arrs],
        scratch_shapes=[dma(3 * n), dma(3 * n), dma(3 * n), dma(3 * n), dma(n)],
        input_output_aliases={a: a for a in range(n)} if in_place else {},
        name=name,
    )(*arrs)


def _carry_pair_exchange(grads):
    n = len(grads)

    def copies(ins, outs, send_sems, recv_sems):
        x, y, c = _place()
        cps = []
        for a in range(n):
            h = ins[a].shape[1] // 2
            theirs = pl.ds(pl.multiple_of((1 - c) * h, 8), h)
            cps.append(pltpu.make_async_remote_copy(
                src_ref=ins[a].at[:, theirs, :], dst_ref=outs[a], send_sem=send_sems.at[a], recv_sem=recv_sems.at[a],
                device_id=(x, y, 1 - c), device_id_type=MESH))
        return cps

    def start(*refs):
        for cp in copies(*refs):
            cp.start()

    def finish(*refs):
        for cp in copies(*refs):
            cp.wait()

    return _Carried(grads, [_sds((4, g.shape[1] // 2, g.shape[2]), F32) for g in grads], {}, n, start, finish)


def _pair_sum(grad, recv, core, name):
    _, r, c = grad.shape
    h = r // 2

    def body(core_ref, g_ref, r_ref, o_ref):
        o_ref[...] = (g_ref[...] + r_ref[...]).astype(BF16)

    return pl.pallas_call(
        body,
        grid_spec=pltpu.PrefetchScalarGridSpec(
            num_scalar_prefetch=1, grid=(4,),
            in_specs=[pl.BlockSpec((None, h, c), lambda s, core_ref: (s, core_ref[0], 0)),
                      pl.BlockSpec((None, h, c), lambda s, core_ref: (s, 0, 0))],
            out_specs=pl.BlockSpec((None, h, c), lambda s, core_ref: (s, 0, 0))),
        out_shape=_sds((4, h, c), BF16), compiler_params=_params(("arbitrary",)), name=name,
    )(core, grad, recv)


def _carry_chip_exchange(parts, swapped):
    n = len(parts)

    def copies(ins, outs, send_sems, recv_sems):
        x, y, c = _place()
        chips = [(_flip(x, k & 2), _flip(y, k & 1)) for k in (1, 2, 3)]
        cps = []
        for a in range(n):
            for j, (px, py) in enumerate(chips):
                cps.append(pltpu.make_async_remote_copy(
                    src_ref=ins[a].at[_slot(px, py, swapped[a])], dst_ref=outs[a].at[j],
                    send_sem=send_sems.at[3 * a + j], recv_sem=recv_sems.at[3 * a + j],
                    device_id=(px, py, c), device_id_type=MESH))
        return cps

    def start(*refs):
        for cp in copies(*refs):
            cp.start()

    def finish(*refs):
        for cp in copies(*refs):
            cp.wait()

    return _Carried(parts, [_sds((3,) + p.shape[1:], BF16) for p in parts], {}, 3 * n, start, finish)


def _chip_sum(part, recv, slot_core, name):
    _, h, c = part.shape

    def body(sc_ref, p_ref, r_ref, o_ref):
        acc = p_ref[...].astype(F32)
        for j in range(3):
            acc = acc + r_ref[j].astype(F32)
        o_ref[...] = acc

    return pl.pallas_call(
        body,
        grid_spec=pltpu.PrefetchScalarGridSpec(
            num_scalar_prefetch=1, grid=(1,),
            in_specs=[pl.BlockSpec((None, h, c), lambda s, sc_ref: (sc_ref[0], 0, 0)),
                      pl.BlockSpec((3, h, c), lambda s, sc_ref: (0, 0, 0))],
            out_specs=pl.BlockSpec((h, c), lambda s, sc_ref: (sc_ref[1], 0))),
        out_shape=_sds((2 * h, c), F32), compiler_params=_params(("arbitrary",)), name=name,
    )(slot_core, part, recv)


def _carry_pair_share(shards):
    n = len(shards)

    def copies(outs, send_sems, recv_sems, mine):
        x, y, c = _place()
        cps = []
        for a in range(n):
            h = outs[a].shape[0] // 2
            half = outs[a].at[pl.ds(pl.multiple_of((c if mine else 1 - c) * h, 8), h)]
            cps.append(pltpu.make_async_remote_copy(
                src_ref=half, dst_ref=half, send_sem=send_sems.at[a], recv_sem=recv_sems.at[a],
                device_id=(x, y, 1 - c), device_id_type=MESH))
        return cps

    def start(ins, outs, send_sems, recv_sems):
        for cp in copies(outs, send_sems, recv_sems, True):
            cp.start()

    def finish(ins, outs, send_sems, recv_sems):
        for cp in copies(outs, send_sems, recv_sems, False):
            cp.wait_recv()
        for cp in copies(outs, send_sems, recv_sems, True):
            cp.wait_send()

    return _Carried(shards, [_sds(s.shape, F32) for s in shards], {a: a for a in range(n)}, n, start, finish)


def _carry_gather_ici(bufs, swapped):
    n = len(bufs)

    def copies(outs, send_sems, recv_sems, sending):
        x, y, c = _place()
        cps = []
        for a in range(n):
            h = outs[a].shape[1] // 2
            mine = pl.ds(pl.multiple_of(c * h, 8), h)
            for j, k in enumerate((1, 2, 3)):
                px, py = _flip(x, k & 2), _flip(y, k & 1)
                slot = _slot(x, y, swapped[a]) if sending else _slot(px, py, swapped[a])
                piece = outs[a].at[slot, mine]
                cps.append(pltpu.make_async_remote_copy(
                    src_ref=piece, dst_ref=piece, send_sem=send_sems.at[3 * a + j], recv_sem=recv_sems.at[3 * a + j],
                    device_id=(px, py, c), device_id_type=MESH))
        return cps

    def start(ins, outs, send_sems, recv_sems):
        for cp in copies(outs, send_sems, recv_sems, True):
            cp.start()

    def finish(ins, outs, send_sems, recv_sems):
        for cp in copies(outs, send_sems, recv_sems, False):
            cp.wait_recv()
        for cp in copies(outs, send_sems, recv_sems, True):
            cp.wait_send()

    return _Carried(bufs, [_sds(b.shape, b.dtype) for b in bufs], {a: a for a in range(n)}, 3 * n, start, finish)


def _carry_gather_forward(bufs, swapped):
    n = len(bufs)

    def copies(outs, send_sems, recv_sems, sending):
        x, y, c = _place()
        cps = []
        for a in range(n):
            h = outs[a].shape[1] // 2
            rows = pl.ds(pl.multiple_of((c if sending else 1 - c) * h, 8), h)
            for j, k in enumerate((1, 2, 3)):
                piece = outs[a].at[_slot(_flip(x, k & 2), _flip(y, k & 1), swapped[a]), rows]
                cps.append(pltpu.make_async_remote_copy(
                    src_ref=piece, dst_ref=piece, send_sem=send_sems.at[3 * a + j], recv_sem=recv_sems.at[3 * a + j],
                    device_id=(x, y, 1 - c), device_id_type=MESH))
        return cps

    def start(ins, outs, send_sems, recv_sems):
        for cp in copies(outs, send_sems, recv_sems, True):
            cp.start()

    def finish(ins, outs, send_sems, recv_sems):
        for cp in copies(outs, send_sems, recv_sems, False):
            cp.wait_recv()
        for cp in copies(outs, send_sems, recv_sems, True):
            cp.wait_send()

    return _Carried(bufs, [_sds(b.shape, b.dtype) for b in bufs], {a: a for a in range(n)}, 3 * n, start, finish)


def _pack(arrs, rows_multiple=8):
    parts, offs, row = [], [], 0
    for a in arrs:
        flat = a.reshape(-1)
        nrow = -(-flat.shape[0] // D)
        parts.append(jnp.pad(flat, (0, nrow * D - flat.shape[0])))
        offs.append(row)
        row += nrow
    total = -(-row // rows_multiple) * rows_multiple
    if total > row:
        parts.append(jnp.zeros(((total - row) * D,), F32))
    return jnp.concatenate(parts).reshape(total, D), offs


def _unpack(packed, offs, shapes):
    out = []
    for off, shp in zip(offs, shapes):
        size = int(np.prod(shp))
        nrow = -(-size // D)
        out.append(packed[off:off + nrow].reshape(-1)[:size].reshape(shp))
    return out


def _to_bf16_slot(w, slot, name):
    r, c = w.shape
    tm = next(cand for cand in (256, 176, 128, 64, 32, 16) if r % cand == 0)

    def body(slot_ref, w_ref, o_ref):
        o_ref[...] = w_ref[...].astype(BF16)

    return pl.pallas_call(
        body,
        grid_spec=pltpu.PrefetchScalarGridSpec(
            num_scalar_prefetch=1, grid=(r // tm,),
            in_specs=[pl.BlockSpec((tm, c), lambda i, slot_ref: (i, 0))],
            out_specs=pl.BlockSpec((None, tm, c), lambda i, slot_ref: (slot_ref[0], i, 0))),
        out_shape=_sds((4, r, c), BF16), compiler_params=_params(("arbitrary",)), name=name,
    )(slot, w)


def _unshard_cols(g):
    s, k, n = g.shape
    return jnp.transpose(g, (1, 0, 2)).reshape(k, s * n)


def _ff_swap(v):
    b = FF_BLOCK
    return jnp.concatenate([v[..., 0:b], v[..., 2 * b:3 * b], v[..., b:2 * b], v[..., 3 * b:4 * b]], axis=-1)


LATE = ("attn_o", "conv_o", "mix_o", "up", "down")
EARLY_GRADS = ("down", "up", "mix_o", "attn_o", "conv_o")


def _weight_views(bufs):
    return {"up": bufs["up"], "attn_o": _unshard_cols(bufs["attn_o"]), "conv_o": _unshard_cols(bufs["conv_o"]),
            "mix_o": bufs["mix_o"].reshape(D, D), "down": bufs["down"].reshape(D_FF, D)}


def _pair_sums(names, grads, recv, dist):
    return [_pair_sum(g, r, dist["core"], "pair_sum_" + n) for n, g, r in zip(names, grads, recv)]


def _reduce_halves(names, parts, from_chips, dist):
    return [_chip_sum(p, r, jnp.concatenate([dist["slots"][SWAPPED[n]], dist["core"]]), "chip_sum_" + n)
            for n, p, r in zip(names, parts, from_chips)]


FUSED_TILE = 256
WIDE_TILE = 512


def _gates(z):
    return [(z, 512, 5), (z, 512, 6), (z, 512, 7), (z, 512, 8)]


def _mix_out(a, cb, z, x, w_mix_o, g_post, gt, g_pre2, sc2, sh2, name):
    def lhs(av, cv, ga0, ga1, gb0, gb1):
        ga, gb = jnp.concatenate([ga0, ga1], axis=1), jnp.concatenate([gb0, gb1], axis=1)
        return _sigmoid(ga) * av + _sigmoid(gb) * cv

    def fn(ym, y, xv, gv, gtv, g2v, scv, shv):
        yn, _ = _rms_stats(ym)
        x1 = xv + gtv * (yn * gv)
        xn, _ = _rms_stats(x1)
        return ym, y, x1, (xn * g2v) * (1.0 + scv) + shv

    return _matmul_rows(w_mix_o, form="nn", tm=min(FUSED_TILE, x.shape[0]), tk=D, fn=fn, a_rows=[(a, D, 0), (cb, D, 0)] + _gates(z),
                        a_fn=lhs, rows=[(x, D, 0)], consts=[g_post, gt, g_pre2, sc2, sh2],
                        row_outs=[(F32, D), (BF16, D), (F32, D), (BF16, D)], acc_outs=[], name=name)


def _down_tail(act, w_down, x1, target, g, gt, name):
    def fn(yv, xv, tv, gv, gtv):
        yn, r = _rms_stats(yv)
        e = xv + gtv * (yn * gv) - tv
        dx2 = e * (1.0 / D)
        dyn = dx2 * gtv
        return (dx2, _rms_bwd(dyn * gv, yn, r), _colsum(e * e) * (0.5 / D), _colsum(dyn * yn),
                _colsum(dx2 * (yn * gv)))

    return _matmul_rows(w_down, form="nn", a=act, tm=min(WIDE_TILE, x1.shape[0]), tk=FF_BLOCK, fn=fn,
                        rows=[(x1, D, 0), (target, D, 0)], consts=[g, gt], row_outs=[(F32, D), (BF16, D)],
                        acc_outs=[(1, D)] * 3, name=name)


def _up_dx_tail(dup, w_up, x1, dx2, ym, g_pre2, sc2, g_post, gt, name):
    def fn(dh, xv, dov, ymv, g2v, scv, gv, gtv):
        xn, r = _rms_stats(xv)
        dyn = dh * (1.0 + scv)
        dx1 = dov + _rms_bwd(dyn * g2v, xn, r)
        yn, r2 = _rms_stats(ymv)
        dynm = dx1 * gtv
        return (dx1, _rms_bwd(dynm * gv, yn, r2), _colsum(dh), _colsum(dh * (xn * g2v)), _colsum(dyn * xn),
                _colsum(dynm * yn), _colsum(dx1 * (yn * gv)))

    return _matmul_rows(w_up, form="nt", a=dup, tm=min(WIDE_TILE, x1.shape[0]), tk=FF_BLOCK, fn=fn,
                        rows=[(x1, D, 0), (dx2, D, 0), (ym, D, 0)], consts=[g_pre2, sc2, g_post, gt],
                        row_outs=[(F32, D), (BF16, D)], acc_outs=[(1, D)] * 5, name=name)


def _in_dx_tail(dz, w_in, x, dx1, g, sc, name, carried=None):
    def fn(dh, xv, dov, gv, scv):
        xn, r = _rms_stats(xv)
        dyn = dh * (1.0 + scv)
        return dov + _rms_bwd(dyn * gv, xn, r), _colsum(dh), _colsum(dh * (xn * gv)), _colsum(dyn * xn)

    return _matmul_rows(w_in, form="nt", a=dz, tm=min(WIDE_TILE, x.shape[0]), tk=1152, fn=fn,
                        rows=[(x, D, 0), (dx1, D, 0)], consts=[g, sc], row_outs=[(F32, D)], acc_outs=[(1, D)] * 3,
                        name=name, carried=carried)


def _mix_dx_gates(dym, w_mix_o, a, cb, z, name):
    def fn(dy, av, cv, ga0, ga1, gb0, gb1):
        sa = _sigmoid(jnp.concatenate([ga0, ga1], axis=1))
        sb = _sigmoid(jnp.concatenate([gb0, gb1], axis=1))
        dcb = dy * sb
        dga = dy * av * (sa * (1.0 - sa))
        dgb = dy * cv * (sb * (1.0 - sb))
        return dy * sa, dcb, dga, dgb, _colsum(dcb), _colsum(dga), _colsum(dgb)

    return _matmul_rows(w_mix_o, form="nt", a=dym, tm=min(FUSED_TILE, a.shape[0]), tk=D, fn=fn,
                        rows=[(a, D, 0), (cb, D, 0)] + _gates(z), consts=[], row_outs=[(BF16, D)] * 4,
                        acc_outs=[(1, D)] * 3, name=name)


def _local_step(x, target, mod, w_in, late, small, dist=None):
    sh_m, sc_m, gt_m, sh_f, sc_f, gt_f = mod
    t = x.shape[0]
    tmm = min(1024, t)
    late_swapped = [SWAPPED[n] for n in LATE]

    h1 = _pre_norm(x, small["g_pre_mix"], sc_m, sh_m, "pre_norm_mix")
    z = _matmul(h1, w_in, form="nn", out_dtype=F32, tm=tmm, tn=1152, tk=D, bias=small["b_in"], name="mm_in")
    if dist is None:
        att = _attention(z, small["gen"], "attention")
        bufs = late
    else:
        att, landed = _attention(z, small["gen"], "attention",
                                 carried=_carry_gather_ici([late[n] for n in LATE], late_swapped))
        bufs = dict(zip(LATE, _run_carried(_carry_gather_forward(landed, late_swapped), "gather_forward")))
    w = _weight_views(bufs)
    w["in"] = w_in
    a = _matmul(att, w["attn_o"], form="nn", out_dtype=F32, tm=tmm, tn=512, tk=512, name="mm_attn_o")
    u1, u3 = _conv_branch(z, small["w_dw_conv"], small["b_dw_conv"], small["g_conv_ln"], small["b_conv_ln"], "conv_branch")
    cb = _matmul(u3, w["conv_o"], form="nn", out_dtype=F32, tm=tmm, tn=512, tk=512, bias=small["b_conv_o"], name="mm_conv_o")
    ym, y, x1, h2 = _mix_out(a, cb, z, x, w["mix_o"], small["g_post_mix"], gt_m, small["g_pre_ffn"], sc_f, sh_f, "mix_out")
    up = _matmul(h2, w["up"], form="nn", out_dtype=F32, tm=tmm, tn=FF_BLOCK, tk=D, name="mm_up")
    act = _ffn_act(up, small["w_dw_ffn"], small["b_dw_ffn"], "ffn_act")

    dx2, dyf, loss_cols, d_g_post_ffn, d_gt_f = _down_tail(act, w["down"], x1, target, small["g_post_ffn"], gt_f, "down_tail")
    dact = _matmul(dyf, w["down"], form="nt", out_dtype=F32, tm=tmm, tn=FF_BLOCK, tk=D, name="mm_down_dx")
    g_down = _matmul(act, dyf, form="tn", out_dtype=F32, tm=FF_BLOCK, tn=512, tk=t, name="mm_down_dw")
    dup, d_w_dw_ffn, d_b_dw_ffn = _ffn_act_bwd(dact, up, small["w_dw_ffn"], small["b_dw_ffn"], "ffn_act_bwd")
    dx1, dym, d_sh_f, d_sc_f, d_g_pre_ffn, d_g_post_mix, d_gt_m = _up_dx_tail(
        dup, w["up"], x1, dx2, ym, small["g_pre_ffn"], sc_f, small["g_post_mix"], gt_m, "up_dx_tail")
    g_up = _matmul(h2, dup, form="tn", out_dtype=F32, tm=512, tn=FF_BLOCK, tk=t, out_sharded=True, name="mm_up_dw")
    da, dcb, dgate_a, dgate_b, d_b_conv_o, sga, sgb = _mix_dx_gates(dym, w["mix_o"], a, cb, z, "mix_dx_gates")
    g_mix_o = _matmul(y, dym, form="tn", out_dtype=F32, tm=D, tn=512, tk=t, name="mm_mix_o_dw")
    datt = _matmul(da, w["attn_o"], form="nt", out_dtype=F32, tm=tmm, tn=512, tk=D, name="mm_attn_o_dx")
    g_attn_o = _matmul(att, da, form="tn", out_dtype=F32, tm=512, tn=256, tk=t, out_sharded=True, name="mm_attn_o_dw")
    du3 = _matmul(dcb, w["conv_o"], form="nt", out_dtype=F32, tm=tmm, tn=512, tk=D, name="mm_conv_o_dx")
    g_conv_o = _matmul(u3, dcb, form="tn", out_dtype=F32, tm=512, tn=256, tk=t, out_sharded=True, name="mm_conv_o_dw")
    big = {"attn_o": g_attn_o, "conv_o": g_conv_o, "mix_o": g_mix_o.reshape(4, 256, D),
           "up": g_up, "down": g_down.reshape(4, D_FF // 4, D)}
    conv_bwd = (du3, u1, z, small["w_dw_conv"], small["g_conv_ln"], small["b_conv_ln"], "conv_branch_bwd")
    in_dw = dict(form="tn", out_dtype=F32, tm=512, tn=1152, tk=t, out_sharded=True, name="mm_in_dw")
    in_dx = (w_in, x, dx1, small["g_pre_mix"], sc_m, "in_dx_tail")
    if dist is None:
        dglu, d_w_dw_conv, d_b_dw_conv, d_g_conv_ln, d_b_conv_ln, sglu = _conv_branch_bwd(*conv_bwd)
        dq, dk, dv, sq, sk, sv, dgen = _attention_bwd(z, datt, small["gen"], "attention_bwd")
        dz = jnp.concatenate([dq, dk, dv, dglu, dgate_a, dgate_b], axis=1)
        big["in"] = _matmul(h1, dz, **in_dw)
        grad_x, d_sh_m, d_sc_m, d_g_pre_mix = _in_dx_tail(dz, *in_dx)
    else:
        early = [big[n] for n in EARLY_GRADS]
        (dglu, d_w_dw_conv, d_b_dw_conv, d_g_conv_ln, d_b_conv_ln, sglu), recv = _conv_branch_bwd(
            *conv_bwd, carried=_carry_pair_exchange(early))
        parts = _pair_sums(EARLY_GRADS, early, recv, dist)
        (dq, dk, dv, sq, sk, sv, dgen), from_chips = _attention_bwd(
            z, datt, small["gen"], "attention_bwd",
            carried=_carry_chip_exchange(parts, [SWAPPED[n] for n in EARLY_GRADS]))
        halves = _reduce_halves(EARLY_GRADS, parts, from_chips, dist)
        dz = jnp.concatenate([dq, dk, dv, dglu, dgate_a, dgate_b], axis=1)
        g_in, shards = _matmul(h1, dz, carried=_carry_pair_share(halves), **in_dw)
        big = dict(zip(EARLY_GRADS, shards))
        recv_in = _run_carried(_carry_pair_exchange([g_in]), "pair_exchange_in")
        part_in = _pair_sums(("in",), [g_in], recv_in, dist)
        (grad_x, d_sh_m, d_sc_m, d_g_pre_mix), from_chips_in = _in_dx_tail(
            dz, *in_dx, carried=_carry_chip_exchange(part_in, [False]))
        half_in = _reduce_halves(("in",), part_in, from_chips_in, dist)
        big["in"] = _run_carried(_carry_pair_share(half_in), "pair_share_in")[0]
    d_b_in = jnp.concatenate([sq, sk, sv, sglu, sga, sgb], axis=1)

    dmod = [d_sh_m, d_sc_m, d_gt_m, d_sh_f, d_sc_f, d_gt_f]
    sm = {"g_pre_mix": d_g_pre_mix, "g_post_mix": d_g_post_mix, "b_in": d_b_in, "gen": dgen,
          "w_dw_conv": d_w_dw_conv, "b_dw_conv": d_b_dw_conv, "g_conv_ln": d_g_conv_ln, "b_conv_ln": d_b_conv_ln,
          "b_conv_o": d_b_conv_o, "g_pre_ffn": d_g_pre_ffn, "g_post_ffn": d_g_post_ffn,
          "w_dw_ffn": d_w_dw_ffn, "b_dw_ffn": d_b_dw_ffn}
    return loss_cols, grad_x, dmod, big, sm


BIG = ("in", "attn_o", "conv_o", "mix_o", "up", "down")
SWAPPED = {"in": False, "attn_o": False, "conv_o": False, "mix_o": False, "up": True, "down": False}
SMALL_ORDER = ("b_ada", "g_pre_mix", "g_post_mix", "b_in", "rel_bias", "b_dw_conv", "g_conv_ln", "b_conv_ln",
               "b_conv_o", "g_pre_ffn", "g_post_ffn", "b_dw_ffn", "w_dw_conv", "w_dw_ffn")


def kernel(x, c, w_ada, b_ada, g_pre_mix, g_post_mix, w_in, b_in, rel_bias, w_attn_o, w_dw_conv, b_dw_conv, g_conv_ln, b_conv_ln, w_conv_o, b_conv_o, w_mix_o, g_pre_ffn, g_post_ffn, w_up, w_dw_ffn, b_dw_ffn, w_down, loss_target, m_w_ada, m_b_ada, m_g_pre_mix, m_g_post_mix, m_w_in, m_b_in, m_rel_bias, m_w_attn_o, m_w_dw_conv, m_b_dw_conv, m_g_conv_ln, m_b_conv_ln, m_w_conv_o, m_b_conv_o, m_w_mix_o, m_g_pre_ffn, m_g_post_ffn, m_w_up, m_w_dw_ffn, m_b_dw_ffn, m_w_down, v_w_ada, v_b_ada, v_g_pre_mix, v_g_post_mix, v_w_in, v_b_in, v_rel_bias, v_w_attn_o, v_w_dw_conv, v_b_dw_conv, v_g_conv_ln, v_b_conv_ln, v_w_conv_o, v_b_conv_o, v_w_mix_o, v_g_pre_ffn, v_g_post_ffn, v_w_up, v_w_dw_ffn, v_b_dw_ffn, v_w_down):
    given = dict(locals())
    ax, ay, ac = lax.axis_index("x"), lax.axis_index("y"), lax.axis_index("c")
    shard = 2 * ax + ay
    me = 4 * ax + 2 * ay + ac
    xs, target = x[0], loss_target[0]

    c_pad = jnp.pad(c, ((0, 7), (0, 0)))
    c_g, _ = _allgather8(c_pad, "gather_c")
    c_all = c_g[:, 0, :]
    b_ada_shard = lax.dynamic_slice(b_ada, (0, shard * 1536), (1, 1536))
    mod_shard, c_act = _ada_fwd(c_all, w_ada[0], b_ada_shard, "ada_fwd")
    small_in = [jnp.pad(mod_shard, ((0, 8), (0, 0))),
                jnp.pad(w_dw_conv[0], ((0, 1), (0, 0))),
                jnp.pad(w_dw_ffn[0], ((0, 13), (0, 0)))]
    mod_g, wdc_g, wdf_g = _gather_shards(small_in, [False, False, True], "gather_small")
    mod_all = jnp.transpose(mod_g[:, :8, :], (1, 0, 2)).reshape(8, 6 * D)
    mod_row = lax.dynamic_slice(mod_all, (me, 0), (1, 6 * D))
    mod = [mod_row[:, k * D:(k + 1) * D] for k in range(6)]

    slots = {sw: _slot(ax, ay, sw).astype(jnp.int32).reshape(1) for sw in (False, True)}
    own = {n: _to_bf16_slot(given["w_" + n][0], slots[SWAPPED[n]], "cast_" + n) for n in BIG}
    w_in_all = _gather_shards([own["in"]], [False], "gather_w_in", in_place=True)[0]
    core = ac.astype(jnp.int32).reshape(1)
    dist = {"core": core, "slots": slots}

    sel = jnp.asarray(_toeplitz_map())
    rel_pad = jnp.pad(rel_bias[0], ((0, 0), (0, REL_PAD - (2 * MAX_REL + 1))))
    gen = _select_call(rel_pad, sel.T.astype(BF16), "bias_rows")
    small = {"g_pre_mix": g_pre_mix, "g_post_mix": g_post_mix, "b_in": b_in, "gen": gen,
             "w_dw_conv": _unshard_cols(wdc_g[:, :CONV_K, :]), "b_dw_conv": b_dw_conv, "g_conv_ln": g_conv_ln,
             "b_conv_ln": b_conv_ln, "b_conv_o": b_conv_o, "g_pre_ffn": g_pre_ffn, "g_post_ffn": g_post_ffn,
             "w_dw_ffn": _unshard_cols(wdf_g[:, :FFN_K, :]), "b_dw_ffn": _ff_swap(b_dw_ffn)}

    loss_cols, grad_x, dmod, reduced, sm = _local_step(xs, target, mod, w_in_all, {n: own[n] for n in LATE}, small, dist)
    loss = lax.psum(jnp.sum(loss_cols), ("x", "y", "c"))

    d_rel = _select_call(sm["gen"], sel.astype(BF16), "bias_fold")[:, :2 * MAX_REL + 1]
    small_grads = {"g_pre_mix": sm["g_pre_mix"], "g_post_mix": sm["g_post_mix"], "b_in": sm["b_in"], "rel_bias": d_rel[None],
                   "b_dw_conv": sm["b_dw_conv"], "g_conv_ln": sm["g_conv_ln"], "b_conv_ln": sm["b_conv_ln"],
                   "b_conv_o": sm["b_conv_o"], "g_pre_ffn": sm["g_pre_ffn"], "g_post_ffn": sm["g_post_ffn"],
                   "b_dw_ffn": _ff_swap(sm["b_dw_ffn"]), "w_dw_conv": sm["w_dw_conv"], "w_dw_ffn": _ff_swap(sm["w_dw_ffn"])}
    order = [n for n in SMALL_ORDER if n != "b_ada"]
    packed, offs = _pack([jnp.concatenate(dmod, axis=1)] + [small_grads[n] for n in order])
    every, total = _allgather8(packed, "gather_small_grads")
    dmod_all = every[:, 0:6, :].reshape(8, 6 * D)
    full_shapes = {n: given[n].shape for n in order}
    full_shapes["w_dw_conv"], full_shapes["w_dw_ffn"] = (1, CONV_K, 512), (1, FFN_K, 2 * D_FF)
    sums = dict(zip(order, _unpack(total, offs[1:], [full_shapes[n] for n in order])))
    sums["b_ada"] = total[0:6].reshape(1, 6 * D)
    sums["w_dw_conv"] = lax.dynamic_slice(sums["w_dw_conv"], (0, 0, shard * 128), (1, CONV_K, 128))
    sums["w_dw_ffn"] = lax.dynamic_slice(sums["w_dw_ffn"], (0, 0, shard * FF_BLOCK), (1, FFN_K, FF_BLOCK))

    upd = dict(zip(SMALL_ORDER, _adamw_many(
        [(given[n], sums[n], given["m_" + n], given["v_" + n]) for n in SMALL_ORDER], "adamw_small")))

    dmod_shard = lax.dynamic_slice(dmod_all, (0, shard * 1536), (8, 1536))
    ada = _ada_bwd_adamw(c_act.T, dmod_shard, w_ada[0], m_w_ada[0], v_w_ada[0], "ada_bwd_adamw")

    out = {"grad_w_ada": ada[0][None], "delta_w_ada": ada[1][None], "new_m_w_ada": ada[2][None], "new_v_w_ada": ada[3][None]}
    for n in BIG:
        g = reduced[n]
        dl, nm, nv = _adamw(given["w_" + n][0], g, given["m_w_" + n][0], given["v_w_" + n][0], "adamw_" + n)
        out["grad_w_" + n], out["delta_w_" + n], out["new_m_w_" + n], out["new_v_w_" + n] = g[None], dl[None], nm[None], nv[None]
    for n in SMALL_ORDER:
        out["grad_" + n], out["delta_" + n], out["new_m_" + n], out["new_v_" + n] = sums[n], *upd[n]

    weights = ["w_ada", "b_ada", "g_pre_mix", "g_post_mix", "w_in", "b_in", "rel_bias", "w_attn_o", "w_dw_conv", "b_dw_conv",
               "g_conv_ln", "b_conv_ln", "w_conv_o", "b_conv_o", "w_mix_o", "g_pre_ffn", "g_post_ffn", "w_up", "w_dw_ffn",
               "b_dw_ffn", "w_down"]
    return (loss, grad_x[None], *[out["grad_" + n] for n in weights], *[out["delta_" + n] for n in weights],
            *[out["new_m_" + n] for n in weights], *[out["new_v_" + n] for n in weights])
```

```python
import functools
import math

import numpy as np
import jax
import jax.numpy as jnp
from jax import lax
from jax.experimental import pallas as pl
from jax.experimental.pallas import tpu as pltpu

F32, BF16 = jnp.float32, jnp.bfloat16
MESH = pl.DeviceIdType.MESH

D = 1024
D_IN = 4608
D_FF = 2816
CONV_K = 31
FFN_K = 3
N_HEADS = 8
CHUNK = 64
LEFT_CHUNKS = 8
MAX_REL = 128
EPS = 1e-6
NEG_INF = -1e30
Q_TILE = 256
WINDOW = Q_TILE + LEFT_CHUNKS * CHUNK
REL_PAD = 384
TOEP = 1024
ROW_TILE = 256
VMEM_LIMIT = 60 * 1024 * 1024

ADAM_LR, ADAM_B1, ADAM_B2, ADAM_EPS, ADAM_WD, ADAM_STEP = 0.001, 0.9, 0.999, 1e-08, 0.01, 10


def _params(sem=None):
    return pltpu.CompilerParams(dimension_semantics=sem, vmem_limit_bytes=VMEM_LIMIT)


def _sds(shape, dtype):
    return jax.ShapeDtypeStruct(tuple(shape), dtype)


ANY = pl.BlockSpec(memory_space=pl.ANY)


class _Carried:
    def __init__(self, ins, out_shapes, aliases, n_sems, start, finish):
        self.ins, self.out_shapes, self.aliases = list(ins), list(out_shapes), dict(aliases)
        self.n_sems, self.start, self.finish = n_sems, start, finish


def _call(body, *, grid, in_specs, out_specs, out_shape, scratch_shapes, sem, name, args, carried=None):
    in_specs, out_specs, out_shape = list(in_specs), list(out_specs), list(out_shape)
    scratch_shapes = list(scratch_shapes)
    if carried is None:
        return pl.pallas_call(body, grid=grid, in_specs=in_specs, out_specs=out_specs, out_shape=out_shape,
                              scratch_shapes=scratch_shapes, compiler_params=_params(sem), name=name)(*args)
    n_in, n_out, n_scr = len(in_specs), len(out_specs), len(scratch_shapes)
    c_in, c_out = len(carried.ins), len(carried.out_shapes)

    def full(*refs):
        pos = [0]

        def take(k):
            part = refs[pos[0]:pos[0] + k]
            pos[0] += k
            return part

        ins, cins, outs, couts, scr = take(n_in), take(c_in), take(n_out), take(c_out), take(n_scr)
        send_sems, recv_sems = take(2)
        first = last = None
        for d, size in enumerate(grid):
            pid = pl.program_id(d)
            first = (pid == 0) if first is None else first & (pid == 0)
            last = (pid == size - 1) if last is None else last & (pid == size - 1)

        @pl.when(first)
        def _():
            carried.start(cins, couts, send_sems, recv_sems)

        body(*ins, *outs, *scr)

        @pl.when(last)
        def _():
            carried.finish(cins, couts, send_sems, recv_sems)

    sems = [pltpu.SemaphoreType.DMA((carried.n_sems,)), pltpu.SemaphoreType.DMA((carried.n_sems,))]
    return pl.pallas_call(
        full, grid=grid, in_specs=in_specs + [ANY] * c_in, out_specs=out_specs + [ANY] * c_out,
        out_shape=out_shape + carried.out_shapes, scratch_shapes=scratch_shapes + sems,
        input_output_aliases={n_in + k: n_out + v for k, v in carried.aliases.items()},
        compiler_params=_params(tuple("arbitrary" for _ in grid)), name=name,
    )(*args, *carried.ins)


def _run_carried(carried, name):
    c_in = len(carried.ins)

    def body(*refs):
        cins, couts = refs[:c_in], refs[c_in:c_in + len(carried.out_shapes)]
        send_sems, recv_sems = refs[-2:]
        carried.start(cins, couts, send_sems, recv_sems)
        carried.finish(cins, couts, send_sems, recv_sems)

    return pl.pallas_call(
        body, in_specs=[ANY] * c_in, out_specs=[ANY] * len(carried.out_shapes), out_shape=carried.out_shapes,
        scratch_shapes=[pltpu.SemaphoreType.DMA((carried.n_sems,)), pltpu.SemaphoreType.DMA((carried.n_sems,))],
        input_output_aliases=carried.aliases, name=name,
    )(*carried.ins)


def _matmul(a, b, *, form, out_dtype, tm, tn, tk, name, bias=None, add=None, out_sharded=False, carried=None):
    b3 = b.ndim == 3
    if form == "nn":
        m, k = a.shape
        n = b.shape[0] * b.shape[2] if b3 else b.shape[1]
        dn = (((1,), (0,)), ((), ()))
        a_spec = pl.BlockSpec((tm, tk), lambda i, j, kk: (i, kk))
        b_spec = (pl.BlockSpec((None, tk, tn), lambda i, j, kk: (j, kk, 0)) if b3
                  else pl.BlockSpec((tk, tn), lambda i, j, kk: (kk, j)))
    elif form == "nt":
        m, k = a.shape
        n = b.shape[1] if b3 else b.shape[0]
        dn = (((1,), (1,)), ((), ()))
        a_spec = pl.BlockSpec((tm, tk), lambda i, j, kk: (i, kk))
        b_spec = (pl.BlockSpec((None, tn, tk), lambda i, j, kk: (kk, j, 0)) if b3
                  else pl.BlockSpec((tn, tk), lambda i, j, kk: (j, kk)))
    else:
        k, m = a.shape
        n = b.shape[1]
        dn = (((0,), (0,)), ((), ()))
        a_spec = pl.BlockSpec((tk, tm), lambda i, j, kk: (kk, i))
        b_spec = pl.BlockSpec((tk, tn), lambda i, j, kk: (kk, j))
    assert m % tm == 0 and n % tn == 0 and k % tk == 0, (name, m, n, k, tm, tn, tk)
    nk = k // tk
    in_specs, args = [a_spec, b_spec], [a, b]
    if bias is not None:
        in_specs.append(pl.BlockSpec((1, tn), lambda i, j, kk: (0, j)))
        args.append(bias)
    if add is not None:
        in_specs.append(pl.BlockSpec((tm, tn), lambda i, j, kk: (i, j)))
        args.append(add)
    if out_sharded:
        out_shape = _sds((n // tn, m, tn), out_dtype)
        out_spec = pl.BlockSpec((None, tm, tn), lambda i, j, kk: (j, i, 0))
    else:
        out_shape = _sds((m, n), out_dtype)
        out_spec = pl.BlockSpec((tm, tn), lambda i, j, kk: (i, j))

    def body(*refs):
        a_ref, b_ref = refs[0], refs[1]
        pos = 2
        bias_ref = add_ref = None
        if bias is not None:
            bias_ref, pos = refs[pos], pos + 1
        if add is not None:
            add_ref, pos = refs[pos], pos + 1
        o_ref = refs[pos]
        av, bv = a_ref[...], b_ref[...]
        if av.dtype != BF16:
            av = av.astype(BF16)
        if bv.dtype != BF16:
            bv = bv.astype(BF16)
        p = lax.dot_general(av, bv, dn, preferred_element_type=F32)

        def finish(acc):
            if bias_ref is not None:
                acc = acc + bias_ref[...]
            if add_ref is not None:
                acc = acc + add_ref[...]
            o_ref[...] = acc.astype(o_ref.dtype)

        if nk == 1:
            finish(p)
        else:
            acc_ref = refs[pos + 1]
            kk = pl.program_id(2)

            @pl.when(kk == 0)
            def _():
                acc_ref[...] = p

            @pl.when(kk > 0)
            def _():
                acc_ref[...] += p

            @pl.when(kk == nk - 1)
            def _():
                finish(acc_ref[...])

    res = _call(body, grid=(m // tm, n // tn, nk), in_specs=in_specs, out_specs=[out_spec], out_shape=[out_shape],
                scratch_shapes=[pltpu.VMEM((tm, tn), F32)] if nk > 1 else [],
                sem=("parallel", "parallel", "arbitrary"), name=name, args=args, carried=carried)
    return res[0] if carried is None else (res[0], res[1:])


def _rowcall(fn, rows, consts, row_outs, acc_outs, *, name, tm=ROW_TILE, col_grid=1):
    n_rows = rows[0][0].shape[0]
    assert n_rows % tm == 0
    grid = (col_grid, n_rows // tm)
    in_specs = [pl.BlockSpec((tm, w), functools.partial(lambda c, i, cb: (i, cb + c), cb=cb)) for _, w, cb in rows]
    in_specs += [pl.BlockSpec(k.shape, functools.partial(lambda c, i, nd: (0,) * nd, nd=k.ndim)) for k in consts]
    out_specs = [pl.BlockSpec((tm, w), lambda c, i: (i, c)) for _, _, _, w in row_outs]
    out_specs += [pl.BlockSpec((r, w), lambda c, i: (0, c)) for r, _, w in acc_outs]
    out_shape = [_sds((nr, nc), dt) for nr, nc, dt, _ in row_outs] + [_sds((r, nc), F32) for r, nc, _ in acc_outs]
    n_in, n_ro = len(rows) + len(consts), len(row_outs)

    def body(*refs):
        res = fn(*[r[...] for r in refs[:n_in]])
        if not isinstance(res, (tuple, list)):
            res = (res,)
        outs = refs[n_in:]
        for o_ref, val in zip(outs[:n_ro], res[:n_ro]):
            o_ref[...] = val.astype(o_ref.dtype)
        if acc_outs:
            first = pl.program_id(1) == 0

            @pl.when(first)
            def _():
                for o_ref, val in zip(outs[n_ro:], res[n_ro:]):
                    o_ref[...] = val

            @pl.when(jnp.logical_not(first))
            def _():
                for o_ref, val in zip(outs[n_ro:], res[n_ro:]):
                    o_ref[...] += val

    out = pl.pallas_call(
        body, grid=grid, in_specs=in_specs, out_specs=out_specs, out_shape=out_shape,
        compiler_params=_params(("arbitrary", "arbitrary")), name=name,
    )(*[r[0] for r in rows], *consts)
    return out


def _matmul_rows(b, *, form, tm, tk, fn, rows, consts, row_outs, acc_outs, name, a=None, a_rows=None, a_fn=None,
                 carried=None):
    b3 = b.ndim == 3
    n = b.shape[1] if (b3 or form == "nt") else b.shape[1]
    if form == "nn":
        k, n = b.shape
        b_spec = pl.BlockSpec((tk, n), lambda i, kk: (kk, 0))
        dn = (((1,), (0,)), ((), ()))
    else:
        n = b.shape[1] if b3 else b.shape[0]
        k = b.shape[0] * b.shape[2] if b3 else b.shape[1]
        b_spec = (pl.BlockSpec((None, n, tk), lambda i, kk: (kk, 0, 0)) if b3
                  else pl.BlockSpec((n, tk), lambda i, kk: (0, kk)))
        dn = (((1,), (1,)), ((), ()))
    nk = k // tk
    lhs_in = [(a, tk, 0)] if a is not None else list(a_rows)
    assert a is not None or nk == 1
    m = lhs_in[0][0].shape[0]
    n_lhs = len(lhs_in)
    in_specs = [pl.BlockSpec((tm, tk), lambda i, kk: (i, kk))] if a is not None else [
        pl.BlockSpec((tm, w), functools.partial(lambda i, kk, cb: (i, cb), cb=cb)) for _, w, cb in a_rows]
    in_specs.append(b_spec)
    in_specs += [pl.BlockSpec((tm, w), functools.partial(lambda i, kk, cb: (i, cb), cb=cb)) for _, w, cb in rows]
    in_specs += [pl.BlockSpec(c.shape, functools.partial(lambda i, kk, nd: (0,) * nd, nd=c.ndim)) for c in consts]
    out_specs = [pl.BlockSpec((tm, w), lambda i, kk: (i, 0)) for _, w in row_outs]
    out_specs += [pl.BlockSpec((r, w), lambda i, kk: (0, 0)) for r, w in acc_outs]
    out_shape = [_sds((m, w), dt) for dt, w in row_outs] + [_sds((r, w), F32) for r, w in acc_outs]
    n_rows, n_consts, n_ro, n_acc = len(rows), len(consts), len(row_outs), len(acc_outs)

    def body(*refs):
        pos = n_lhs + 1
        row_refs, const_refs = refs[pos:pos + n_rows], refs[pos + n_rows:pos + n_rows + n_consts]
        pos += n_rows + n_consts
        out_refs, acc_refs = refs[pos:pos + n_ro], refs[pos + n_ro:pos + n_ro + n_acc]
        i, kk = pl.program_id(0), pl.program_id(1)
        lhs = refs[0][...] if a is not None else a_fn(*[r[...] for r in refs[:n_lhs]]).astype(BF16)
        p = lax.dot_general(lhs, refs[n_lhs][...], dn, preferred_element_type=F32)

        def finish(acc):
            extra = [r[...] for r in row_refs] + [c[...] for c in const_refs]
            res = fn(acc, lhs, *extra) if a is None else fn(acc, *extra)
            for o_ref, val in zip(out_refs, res[:n_ro]):
                o_ref[...] = val.astype(o_ref.dtype)
            if n_acc:
                @pl.when(i == 0)
                def _():
                    for o_ref, val in zip(acc_refs, res[n_ro:]):
                        o_ref[...] = val

                @pl.when(i > 0)
                def _():
                    for o_ref, val in zip(acc_refs, res[n_ro:]):
                        o_ref[...] += val

        if nk == 1:
            finish(p)
        else:
            acc_ref = refs[pos + n_ro + n_acc]

            @pl.when(kk == 0)
            def _():
                acc_ref[...] = p

            @pl.when(kk > 0)
            def _():
                acc_ref[...] += p

            @pl.when(kk == nk - 1)
            def _():
                finish(acc_ref[...])

    res = _call(body, grid=(m // tm, nk), in_specs=in_specs, out_specs=out_specs, out_shape=out_shape,
                scratch_shapes=[pltpu.VMEM((tm, n), F32)] if nk > 1 else [], sem=("arbitrary", "arbitrary"),
                name=name, args=[r[0] for r in lhs_in] + [b] + [r[0] for r in rows] + list(consts), carried=carried)
    own = n_ro + n_acc
    return res[:own] if carried is None else (res[:own], res[own:])


def _colsum(v):
    return jnp.sum(v, axis=0, keepdims=True)


def _sigmoid(v):
    return 1.0 / (1.0 + jnp.exp(-v))


_GELU_C = math.sqrt(2.0 / math.pi)


def _gelu(v):
    return 0.5 * v * (1.0 + jnp.tanh(_GELU_C * (v + 0.044715 * (v * v * v))))


def _gelu_and_grad(v):
    th = jnp.tanh(_GELU_C * (v + 0.044715 * (v * v * v)))
    g = 0.5 * v * (1.0 + th)
    dg = 0.5 * (1.0 + th) + 0.5 * v * (1.0 - th * th) * (_GELU_C * (1.0 + 3.0 * 0.044715 * (v * v)))
    return g, dg


def _rms_stats(v):
    r = lax.rsqrt(jnp.mean(v * v, axis=-1, keepdims=True) + EPS)
    return v * r, r


def _rms_bwd(dn, vn, r):
    return r * (dn - vn * jnp.mean(dn * vn, axis=-1, keepdims=True))


def _pre_norm(x, g, sc, sh, name):
    def fn(xv, gv, scv, shv):
        xn, _ = _rms_stats(xv)
        return (xn * gv) * (1.0 + scv) + shv
    return _rowcall(fn, [(x, D, 0)], [g, sc, sh], [(x.shape[0], D, BF16, D)], [], name=name)[0]


def _pre_norm_bwd(dh, x, dx_other, g, sc, name):
    def fn(dhv, xv, dov, gv, scv):
        xn, r = _rms_stats(xv)
        yn = xn * gv
        dyn = dhv * (1.0 + scv)
        dx = _rms_bwd(dyn * gv, xn, r)
        return dov + dx, _colsum(dhv), _colsum(dhv * yn), _colsum(dyn * xn)
    t = x.shape[0]
    return _rowcall(fn, [(dh, D, 0), (x, D, 0), (dx_other, D, 0)], [g, sc], [(t, D, F32, D)],
                    [(1, D, D)] * 3, name=name)


def _post_res(x, ypre, g, gt, name):
    def fn(xv, yv, gv, gtv):
        yn, _ = _rms_stats(yv)
        return xv + gtv * (yn * gv)
    return _rowcall(fn, [(x, D, 0), (ypre, D, 0)], [g, gt], [(x.shape[0], D, F32, D)], [], name=name)[0]


def _post_res_bwd(dxo, ypre, g, gt, name):
    def fn(dv, yv, gv, gtv):
        yn, r = _rms_stats(yv)
        dyn = dv * gtv
        dy = _rms_bwd(dyn * gv, yn, r)
        return dy, _colsum(dyn * yn), _colsum(dv * (yn * gv))
    t = ypre.shape[0]
    return _rowcall(fn, [(dxo, D, 0), (ypre, D, 0)], [g, gt], [(t, D, BF16, D)], [(1, D, D)] * 2, name=name)


def _ffn_tail(x1, yf, target, g, gt, name):
    def fn(xv, yv, tv, gv, gtv):
        yn, r = _rms_stats(yv)
        e = xv + gtv * (yn * gv) - tv
        dx2 = e * (1.0 / D)
        dyn = dx2 * gtv
        dy = _rms_bwd(dyn * gv, yn, r)
        return dx2, dy, _colsum(e * e) * (0.5 / D), _colsum(dyn * yn), _colsum(dx2 * (yn * gv))
    t = x1.shape[0]
    return _rowcall(fn, [(x1, D, 0), (yf, D, 0), (target, D, 0)], [g, gt], [(t, D, F32, D), (t, D, BF16, D)],
                    [(1, D, D)] * 3, name=name)


def _gate_merge(a, cb, z, name):
    def fn(av, cv, gav, gbv):
        return _sigmoid(gav) * av + _sigmoid(gbv) * cv
    t = a.shape[0]
    return _rowcall(fn, [(a, 512, 0), (cb, 512, 0), (z, 512, 5), (z, 512, 7)], [],
                    [(t, D, BF16, 512)], [], name=name, col_grid=2)[0]


def _gate_merge_bwd(dy, a, cb, z, name):
    def fn(dv, av, cv, gav, gbv):
        sa, sb = _sigmoid(gav), _sigmoid(gbv)
        dcb = dv * sb
        dga = dv * av * (sa * (1.0 - sa))
        dgb = dv * cv * (sb * (1.0 - sb))
        return dv * sa, dcb, dga, dgb, _colsum(dcb), _colsum(dga), _colsum(dgb)
    t = a.shape[0]
    return _rowcall(fn, [(dy, 512, 0), (a, 512, 0), (cb, 512, 0), (z, 512, 5), (z, 512, 7)], [],
                    [(t, D, BF16, 512)] * 4, [(1, D, 512)] * 3, name=name, col_grid=2)


CONV_HALO = 32


def _layer_norm_parts(u):
    mu = jnp.mean(u, axis=-1, keepdims=True)
    d = u - mu
    r = lax.rsqrt(jnp.mean(d * d, axis=-1, keepdims=True) + EPS)
    return d * r, r


LANES = 128
SUBLANE_ROWS = 8
CONV_ROWS = 64


def _lanes(c):
    return slice(c * LANES, (c + 1) * LANES)


def _conv_branch(z, w_dw, b_dw, g_ln, b_ln, name, tm=ROW_TILE):
    t = z.shape[0]
    per = tm // CONV_HALO
    n_chunks = 512 // LANES

    def body(ga_ref, gb_ref, gah_ref, gbh_ref, w_ref, b_ref, g_ref, bl_ref, u1_ref, u3_ref, scr):
        i = pl.program_id(0)
        u0h = jnp.where(i > 0, gah_ref[...] * _sigmoid(gbh_ref[...]), 0.0)
        u0 = ga_ref[...] * _sigmoid(gb_ref[...])
        for c in range(n_chunks):
            scr[c, 0:CONV_HALO, :] = u0h[:, _lanes(c)]
            scr[c, CONV_HALO:CONV_HALO + tm, :] = u0[:, _lanes(c)]
        for c in range(n_chunks):
            for r0 in range(0, tm, CONV_ROWS):
                acc = jnp.zeros((CONV_ROWS, LANES), F32) + b_ref[:, _lanes(c)]
                for j in range(CONV_K):
                    acc = acc + w_ref[j:j + 1, _lanes(c)] * scr[c, pl.ds(r0 + CONV_HALO - (CONV_K - 1) + j, CONV_ROWS), :]
                u1_ref[r0:r0 + CONV_ROWS, _lanes(c)] = acc
        xh, _ = _layer_norm_parts(u1_ref[...])
        u2 = xh * g_ref[...] + bl_ref[...]
        u3_ref[...] = (u2 * _sigmoid(u2)).astype(BF16)

    cur = lambda cb: pl.BlockSpec((tm, 512), lambda i: (i, cb))
    halo = lambda cb: pl.BlockSpec((CONV_HALO, 512), lambda i: (jnp.maximum(i * per - 1, 0), cb))
    whole = lambda a: pl.BlockSpec(a.shape, lambda i: (0, 0))
    return pl.pallas_call(
        body, grid=(t // tm,),
        in_specs=[cur(3), cur(4), halo(3), halo(4), whole(w_dw), whole(b_dw), whole(g_ln), whole(b_ln)],
        out_specs=[pl.BlockSpec((tm, 512), lambda i: (i, 0))] * 2,
        out_shape=[_sds((t, 512), F32), _sds((t, 512), BF16)],
        scratch_shapes=[pltpu.VMEM((n_chunks, CONV_HALO + tm, LANES), F32)],
        compiler_params=_params(("arbitrary",)), name=name,
    )(z, z, z, z, w_dw, b_dw, g_ln, b_ln)


def _conv_branch_bwd(du3, u1, z, w_dw, g_ln, b_ln, name, tm=ROW_TILE, carried=None):
    t = z.shape[0]
    per = tm // CONV_HALO
    last = t // tm - 1
    n_chunks = 512 // LANES

    def du1_of(du3v, u1v, g, b):
        xh, r = _layer_norm_parts(u1v)
        u2 = xh * g + b
        s = _sigmoid(u2)
        du2 = du3v * (s * (1.0 + u2 * (1.0 - s)))
        dxh = du2 * g
        du1 = r * (dxh - jnp.mean(dxh, axis=-1, keepdims=True) - xh * jnp.mean(dxh * xh, axis=-1, keepdims=True))
        return du1, du2, xh

    def body(d_ref, u_ref, dn_ref, un_ref, ga_ref, gb_ref, gah_ref, gbh_ref, w_ref, g_ref, bl_ref,
             dglu_ref, dw_ref, dbdw_ref, dg_ref, dbl_ref, dbin_ref, scr, scd):
        i = pl.program_id(0)
        g, b = g_ref[...], bl_ref[...]
        du1, du2, xh = du1_of(d_ref[...], u_ref[...], g, b)
        du1n, _, _ = du1_of(dn_ref[...], un_ref[...], g, b)
        du1n = jnp.where(i < last, du1n, 0.0)
        sgb = _sigmoid(gb_ref[...])
        ga = ga_ref[...]
        u0 = ga * sgb
        u0h = jnp.where(i > 0, gah_ref[...] * _sigmoid(gbh_ref[...]), 0.0)
        for c in range(n_chunks):
            scd[c, 0:tm, :] = du1[:, _lanes(c)]
            scd[c, tm:tm + CONV_HALO, :] = du1n[:, _lanes(c)]
            scr[c, 0:CONV_HALO, :] = u0h[:, _lanes(c)]
            scr[c, CONV_HALO:CONV_HALO + tm, :] = u0[:, _lanes(c)]

        @pl.when(i == 0)
        def _():
            for ref in (dw_ref, dbdw_ref, dg_ref, dbl_ref, dbin_ref):
                ref[...] = jnp.zeros_like(ref)

        dsg = ga * (sgb * (1.0 - sgb))
        for c in range(n_chunks):
            gate = slice(512 + c * LANES, 512 + (c + 1) * LANES)
            for r0 in range(0, tm, CONV_ROWS):
                rows = slice(r0, r0 + CONV_ROWS)
                du0 = jnp.zeros((CONV_ROWS, LANES), F32)
                for j in range(CONV_K):
                    du0 = du0 + w_ref[j:j + 1, _lanes(c)] * scd[c, pl.ds(r0 + CONV_K - 1 - j, CONV_ROWS), :]
                dga = du0 * sgb[rows, _lanes(c)]
                dgb = du0 * dsg[rows, _lanes(c)]
                dglu_ref[rows, _lanes(c)] = dga.astype(BF16)
                dglu_ref[rows, gate] = dgb.astype(BF16)
                dbin_ref[:, _lanes(c)] += _colsum(dga)
                dbin_ref[:, gate] += _colsum(dgb)
            for j in range(CONV_K):
                dwj = jnp.zeros((SUBLANE_ROWS, LANES), F32)
                for r0 in range(0, tm, CONV_ROWS):
                    prod = (scd[c, pl.ds(r0, CONV_ROWS), :]
                            * scr[c, pl.ds(r0 + CONV_HALO - (CONV_K - 1) + j, CONV_ROWS), :])
                    dwj = dwj + jnp.sum(prod.reshape(CONV_ROWS // SUBLANE_ROWS, SUBLANE_ROWS, LANES), axis=0)
                dw_ref[j:j + 1, _lanes(c)] += _colsum(dwj)
        dbdw_ref[...] += _colsum(du1)
        dg_ref[...] += _colsum(du2 * xh)
        dbl_ref[...] += _colsum(du2)

    cur = lambda cb: pl.BlockSpec((tm, 512), lambda i: (i, cb))
    prev = lambda cb: pl.BlockSpec((CONV_HALO, 512), lambda i: (jnp.maximum(i * per - 1, 0), cb))
    nxt = pl.BlockSpec((CONV_HALO, 512), lambda i: (jnp.minimum((i + 1) * per, t // CONV_HALO - 1), 0))
    whole = lambda a: pl.BlockSpec(a.shape, lambda i: (0, 0))
    acc = lambda r, w: pl.BlockSpec((r, w), lambda i: (0, 0))
    res = _call(
        body, grid=(t // tm,),
        in_specs=[cur(0), cur(0), nxt, nxt, cur(3), cur(4), prev(3), prev(4), whole(w_dw), whole(g_ln), whole(b_ln)],
        out_specs=[pl.BlockSpec((tm, 1024), lambda i: (i, 0)), acc(CONV_K, 512), acc(1, 512), acc(1, 512),
                   acc(1, 512), acc(1, 1024)],
        out_shape=[_sds((t, 1024), BF16), _sds((CONV_K, 512), F32), _sds((1, 512), F32), _sds((1, 512), F32),
                   _sds((1, 512), F32), _sds((1, 1024), F32)],
        scratch_shapes=[pltpu.VMEM((n_chunks, CONV_HALO + tm, LANES), F32),
                        pltpu.VMEM((n_chunks, tm + CONV_HALO, LANES), F32)],
        sem=("arbitrary",), name=name, args=(du3, u1, du3, u1, z, z, z, z, w_dw, g_ln, b_ln), carried=carried)
    return res[:6] if carried is None else (res[:6], res[6:])


FF_BLOCK = D_FF // 2
FF_HALO = 8
FF_CHUNKS = FF_BLOCK // LANES


def _ffn_conv(w_ref, b_ref, scr, k, rows):
    acc = b_ref[:, _lanes(k)] + w_ref[0:1, _lanes(k)] * scr[k, pl.ds(FF_HALO - 2, rows), :]
    acc = acc + w_ref[1:2, _lanes(k)] * scr[k, pl.ds(FF_HALO - 1, rows), :]
    return acc + w_ref[2:3, _lanes(k)] * scr[k, pl.ds(FF_HALO, rows), :]


def _ffn_act(up, w3, b3, name, tm=ROW_TILE):
    t = up.shape[0]
    per = tm // FF_HALO
    wide = 2 * FF_BLOCK

    def body(u_ref, uh_ref, w_ref, b_ref, o_ref, scr):
        i = pl.program_id(1)
        for k in range(2 * FF_CHUNKS):
            scr[k, 0:FF_HALO, :] = jnp.where(i > 0, uh_ref[:, _lanes(k)], 0.0)
            scr[k, FF_HALO:FF_HALO + tm, :] = u_ref[:, _lanes(k)]
        for cc in range(FF_CHUNKS):
            val = _ffn_conv(w_ref, b_ref, scr, cc, tm)
            gate = _ffn_conv(w_ref, b_ref, scr, FF_CHUNKS + cc, tm)
            o_ref[:, _lanes(cc)] = (_gelu(gate) * val).astype(BF16)

    return pl.pallas_call(
        body, grid=(2, t // tm),
        in_specs=[pl.BlockSpec((tm, wide), lambda c, i: (i, c)),
                  pl.BlockSpec((FF_HALO, wide), lambda c, i: (jnp.maximum(i * per - 1, 0), c)),
                  pl.BlockSpec((FFN_K, wide), lambda c, i: (0, c)),
                  pl.BlockSpec((1, wide), lambda c, i: (0, c))],
        out_specs=pl.BlockSpec((tm, FF_BLOCK), lambda c, i: (i, c)),
        out_shape=_sds((t, D_FF), BF16),
        scratch_shapes=[pltpu.VMEM((2 * FF_CHUNKS, FF_HALO + tm, LANES), F32)],
        compiler_params=_params(("arbitrary", "arbitrary")), name=name,
    )(up, up, w3, b3)


def _ffn_act_bwd(dact, up, w3, b3, name, tm=ROW_TILE):
    t = up.shape[0]
    per = tm // FF_HALO
    wide = 2 * FF_BLOCK
    last = t // tm - 1
    ext = tm + FF_HALO

    def body(u_ref, up_ref, un_ref, d_ref, dn_ref, w_ref, b_ref, o_ref, dw_ref, db_ref, scr, scd):
        i = pl.program_id(1)
        for k in range(2 * FF_CHUNKS):
            scr[k, 0:FF_HALO, :] = jnp.where(i > 0, up_ref[:, _lanes(k)], 0.0)
            scr[k, FF_HALO:FF_HALO + tm, :] = u_ref[:, _lanes(k)]
            scr[k, FF_HALO + tm:FF_HALO + ext, :] = un_ref[:, _lanes(k)]
        dn = jnp.where(i < last, dn_ref[...], 0.0)

        @pl.when(i == 0)
        def _():
            dw_ref[...] = jnp.zeros_like(dw_ref)
            db_ref[...] = jnp.zeros_like(db_ref)

        for cc in range(FF_CHUNKS):
            val = _ffn_conv(w_ref, b_ref, scr, cc, ext)
            gel, dgel = _gelu_and_grad(_ffn_conv(w_ref, b_ref, scr, FF_CHUNKS + cc, ext))
            da = jnp.concatenate([d_ref[:, _lanes(cc)], dn[:, _lanes(cc)]], axis=0)
            scd[cc] = da * gel
            scd[FF_CHUNKS + cc] = da * val * dgel
            for k in (cc, FF_CHUNKS + cc):
                shifted = [scd[k, pl.ds(FFN_K - 1 - j, tm), :] for j in range(FFN_K)]
                ucur = scr[k, pl.ds(FF_HALO, tm), :]
                o_ref[:, _lanes(k)] = (w_ref[0:1, _lanes(k)] * shifted[0] + w_ref[1:2, _lanes(k)] * shifted[1]
                                       + w_ref[2:3, _lanes(k)] * shifted[2]).astype(BF16)
                for j in range(FFN_K):
                    dw_ref[j:j + 1, _lanes(k)] += _colsum(shifted[j] * ucur)
                db_ref[:, _lanes(k)] += _colsum(shifted[FFN_K - 1])

    nblk = t // FF_HALO
    return pl.pallas_call(
        body, grid=(2, t // tm),
        in_specs=[pl.BlockSpec((tm, wide), lambda c, i: (i, c)),
                  pl.BlockSpec((FF_HALO, wide), lambda c, i: (jnp.maximum(i * per - 1, 0), c)),
                  pl.BlockSpec((FF_HALO, wide), lambda c, i: (jnp.minimum((i + 1) * per, nblk - 1), c)),
                  pl.BlockSpec((tm, FF_BLOCK), lambda c, i: (i, c)),
                  pl.BlockSpec((FF_HALO, FF_BLOCK), lambda c, i: (jnp.minimum((i + 1) * per, nblk - 1), c)),
                  pl.BlockSpec((FFN_K, wide), lambda c, i: (0, c)),
                  pl.BlockSpec((1, wide), lambda c, i: (0, c))],
        out_specs=[pl.BlockSpec((tm, wide), lambda c, i: (i, c)),
                   pl.BlockSpec((FFN_K, wide), lambda c, i: (0, c)),
                   pl.BlockSpec((1, wide), lambda c, i: (0, c))],
        out_shape=[_sds((t, 2 * D_FF), BF16), _sds((FFN_K, 2 * D_FF), F32), _sds((1, 2 * D_FF), F32)],
        scratch_shapes=[pltpu.VMEM((2 * FF_CHUNKS, FF_HALO + ext, LANES), F32),
                        pltpu.VMEM((2 * FF_CHUNKS, ext, LANES), F32)],
        compiler_params=_params(("arbitrary", "arbitrary")), name=name,
    )(up, up, up, dact, dact, w3, b3)


def _toeplitz_map():
    f = np.zeros((TOEP, REL_PAD), np.float32)
    for m in range(TOEP - 1):
        rel = (WINDOW - 1) - m
        f[m, int(np.clip(rel, -MAX_REL, MAX_REL)) + MAX_REL] = 1.0
    return f


def _split3(v):
    hi = v.astype(BF16)
    r1 = v - hi.astype(F32)
    mid = r1.astype(BF16)
    lo = (r1 - mid.astype(F32)).astype(BF16)
    return hi, mid, lo


def _exact_select(v, sel):
    out = None
    for part in _split3(v):
        p = jnp.dot(part, sel, preferred_element_type=F32)
        out = p if out is None else out + p
    return out


def _select_call(v, sel, name):
    def body(v_ref, s_ref, o_ref):
        o_ref[...] = _exact_select(v_ref[...], s_ref[...])
    return pl.pallas_call(body, out_shape=_sds((v.shape[0], sel.shape[1]), F32), name=name)(v, sel)


def _band_bias(gen_row):
    b0 = jnp.broadcast_to(gen_row, (Q_TILE, TOEP))
    bias = pltpu.roll(b0, TOEP - 255, 1, stride=1, stride_axis=0)[:, :WINDOW]
    qq = lax.broadcasted_iota(jnp.int32, (Q_TILE, WINDOW), 0) // CHUNK
    kc = lax.broadcasted_iota(jnp.int32, (Q_TILE, WINDOW), 1) // CHUNK
    return jnp.where((kc >= qq) & (kc <= qq + LEFT_CHUNKS), bias, NEG_INF)


PAD_ROWS = WINDOW - Q_TILE
NT_DIMS = (((1,), (1,)), ((), ()))
TN_DIMS = (((0,), (0,)), ((), ()))


def _head_mask(hh):
    lane = lax.broadcasted_iota(jnp.int32, (1, 128), 1)
    return (lane < 64) if hh == 0 else (lane >= 64)


SOFTMAX_ROWS = 16


def _probs_block(s_scr, bias, hh, rows, i):
    s = s_scr[rows, :] + bias[hh, rows, :]
    col = lax.broadcasted_iota(jnp.int32, (SOFTMAX_ROWS, WINDOW), 1)
    s = jnp.where(col >= PAD_ROWS - Q_TILE * i, s, NEG_INF)
    p = jnp.exp(s - jnp.max(s, axis=-1, keepdims=True))
    return p / jnp.sum(p, axis=-1, keepdims=True)


def _attention(z, gen, name, carried=None):
    t = z.shape[0]
    n_i = t // Q_TILE

    def body(q_ref, k_ref, v_ref, g_ref, o_ref, kpad, vpad, bias, s_scr, p_scr):
        hp, i = pl.program_id(0), pl.program_id(1)

        @pl.when(i == 0)
        def _():
            kpad[0:PAD_ROWS, :] = jnp.zeros((PAD_ROWS, 128), BF16)
            vpad[0:PAD_ROWS, :] = jnp.zeros((PAD_ROWS, 128), BF16)
            kpad[PAD_ROWS:PAD_ROWS + t, :] = k_ref[...].astype(BF16)
            vpad[PAD_ROWS:PAD_ROWS + t, :] = v_ref[...].astype(BF16)
            for hh in range(2):
                bias[hh] = _band_bias(g_ref[pl.ds(2 * hp + hh, 1), :])

        start = pl.multiple_of(i * Q_TILE, Q_TILE)
        kw = kpad[pl.ds(start, WINDOW), :]
        vw = vpad[pl.ds(start, WINDOW), :]
        q = q_ref[...] * (CHUNK ** -0.5)
        out = None
        for hh in range(2):
            mask = _head_mask(hh)
            qm = jnp.where(mask, q, 0.0).astype(BF16)
            s_scr[hh] = lax.dot_general(qm, kw, NT_DIMS, preferred_element_type=F32)
            for r0 in range(0, Q_TILE, SOFTMAX_ROWS):
                rows = slice(r0, r0 + SOFTMAX_ROWS)
                p_scr[hh, rows, :] = _probs_block(s_scr.at[hh], bias, hh, rows, i).astype(BF16)
            o = jnp.dot(p_scr[hh], vw, preferred_element_type=F32)
            out = jnp.where(mask, o, 0.0) if out is None else jnp.where(mask, o, out)
        o_ref[...] = out.astype(BF16)

    res = _call(
        body, grid=(4, n_i),
        in_specs=[pl.BlockSpec((Q_TILE, 128), lambda h, i: (i, h)),
                  pl.BlockSpec((t, 128), lambda h, i: (0, 4 + h)),
                  pl.BlockSpec((t, 128), lambda h, i: (0, 8 + h)),
                  pl.BlockSpec((N_HEADS, TOEP), lambda h, i: (0, 0))],
        out_specs=[pl.BlockSpec((Q_TILE, 128), lambda h, i: (i, h))],
        out_shape=[_sds((t, 512), BF16)],
        scratch_shapes=[pltpu.VMEM((PAD_ROWS + t, 128), BF16), pltpu.VMEM((PAD_ROWS + t, 128), BF16),
                        pltpu.VMEM((2, Q_TILE, WINDOW), F32), pltpu.VMEM((2, Q_TILE, WINDOW), F32),
                        pltpu.VMEM((2, Q_TILE, WINDOW), BF16)],
        sem=("arbitrary", "arbitrary"), name=name, args=(z, z, z, gen), carried=carried)
    return res[0] if carried is None else (res[0], res[1:])


def _attention_bwd(z, datt, gen, name, carried=None):
    t = z.shape[0]
    n_i = t // Q_TILE

    def body(q_ref, k_ref, v_ref, d_ref, g_ref, dq_ref, dk_ref, dv_ref, sq_ref, sk_ref, sv_ref, dg_ref,
             kpad, vpad, dkacc, dvacc, bias, dsacc, s_scr, dp_scr, p_scr, ds_scr):
        hp, i = pl.program_id(0), pl.program_id(1)

        @pl.when(i == 0)
        def _():
            kpad[0:PAD_ROWS, :] = jnp.zeros((PAD_ROWS, 128), BF16)
            vpad[0:PAD_ROWS, :] = jnp.zeros((PAD_ROWS, 128), BF16)
            kpad[PAD_ROWS:PAD_ROWS + t, :] = k_ref[...].astype(BF16)
            vpad[PAD_ROWS:PAD_ROWS + t, :] = v_ref[...].astype(BF16)
            dkacc[...] = jnp.zeros_like(dkacc)
            dvacc[...] = jnp.zeros_like(dvacc)
            dsacc[...] = jnp.zeros_like(dsacc)
            for hh in range(2):
                bias[hh] = _band_bias(g_ref[pl.ds(2 * hp + hh, 1), :])

        start = pl.multiple_of(i * Q_TILE, Q_TILE)
        win = pl.ds(start, WINDOW)
        kw = kpad[win, :]
        vw = vpad[win, :]
        q = q_ref[...] * (CHUNK ** -0.5)
        do = d_ref[...]
        dq = None
        for hh in range(2):
            mask = _head_mask(hh)
            qm = jnp.where(mask, q, 0.0).astype(BF16)
            dom = jnp.where(mask, do, 0.0).astype(BF16)
            s_scr[...] = lax.dot_general(qm, kw, NT_DIMS, preferred_element_type=F32)
            dp_scr[...] = lax.dot_general(dom, vw, NT_DIMS, preferred_element_type=F32)
            for r0 in range(0, Q_TILE, SOFTMAX_ROWS):
                rows = slice(r0, r0 + SOFTMAX_ROWS)
                p = _probs_block(s_scr, bias, hh, rows, i)
                dp = dp_scr[rows, :]
                ds = p * (dp - jnp.sum(p * dp, axis=-1, keepdims=True))
                dsacc[hh, rows, :] += ds
                ds_scr[rows, :] = ds.astype(BF16)
                p_scr[rows, :] = p.astype(BF16)
            ds16 = ds_scr[...]
            dqh = jnp.dot(ds16, kw, preferred_element_type=F32) * (CHUNK ** -0.5)
            dq = jnp.where(mask, dqh, 0.0) if dq is None else jnp.where(mask, dqh, dq)
            dkacc[win, :] += lax.dot_general(ds16, qm, TN_DIMS, preferred_element_type=F32)
            dvacc[win, :] += lax.dot_general(p_scr[...], dom, TN_DIMS, preferred_element_type=F32)
        dq_ref[...] = dq.astype(BF16)

        @pl.when(i == 0)
        def _():
            sq_ref[...] = _colsum(dq)

        @pl.when(i > 0)
        def _():
            sq_ref[...] += _colsum(dq)

        @pl.when(i == n_i - 1)
        def _():
            dk = dkacc[PAD_ROWS:PAD_ROWS + t, :]
            dv = dvacc[PAD_ROWS:PAD_ROWS + t, :]
            dk_ref[...] = dk.astype(BF16)
            dv_ref[...] = dv.astype(BF16)
            sk_ref[...] = _colsum(dk)
            sv_ref[...] = _colsum(dv)
            rr = lax.broadcasted_iota(jnp.int32, (Q_TILE, Q_TILE), 0)
            cc = lax.broadcasted_iota(jnp.int32, (Q_TILE, Q_TILE), 1)
            rev = jnp.where(rr + cc == Q_TILE - 1, 1.0, 0.0).astype(BF16)
            for hh in range(2):
                acc = None
                for part in _split3(dsacc[hh]):
                    pr = jnp.dot(rev, part, preferred_element_type=F32)
                    acc = pr if acc is None else acc + pr
                wide = jnp.concatenate([acc, jnp.zeros((Q_TILE, TOEP - WINDOW), F32)], axis=1)
                dg_ref[pl.ds(2 * hp + hh, 1), :] = _colsum(pltpu.roll(wide, 0, 1, stride=1, stride_axis=0))

    col = lambda off: pl.BlockSpec((t, 128), lambda h, i: (0, off + h))
    tile = lambda: pl.BlockSpec((Q_TILE, 128), lambda h, i: (i, h))
    sums = lambda: pl.BlockSpec((1, 128), lambda h, i: (0, h))
    res = _call(
        body, grid=(4, n_i),
        in_specs=[tile(), col(4), col(8), tile(), pl.BlockSpec((N_HEADS, TOEP), lambda h, i: (0, 0))],
        out_specs=[tile(), col(0), col(0), sums(), sums(), sums(), pl.BlockSpec((N_HEADS, TOEP), lambda h, i: (0, 0))],
        out_shape=[_sds((t, 512), BF16)] * 3 + [_sds((1, 512), F32)] * 3 + [_sds((N_HEADS, TOEP), F32)],
        scratch_shapes=[pltpu.VMEM((PAD_ROWS + t, 128), BF16), pltpu.VMEM((PAD_ROWS + t, 128), BF16),
                        pltpu.VMEM((PAD_ROWS + t, 128), F32), pltpu.VMEM((PAD_ROWS + t, 128), F32),
                        pltpu.VMEM((2, Q_TILE, WINDOW), F32), pltpu.VMEM((2, Q_TILE, WINDOW), F32),
                        pltpu.VMEM((Q_TILE, WINDOW), F32), pltpu.VMEM((Q_TILE, WINDOW), F32),
                        pltpu.VMEM((Q_TILE, WINDOW), BF16), pltpu.VMEM((Q_TILE, WINDOW), BF16)],
        sem=("arbitrary", "arbitrary"), name=name, args=(z, z, z, datt, gen), carried=carried)
    return res[:7] if carried is None else (res[:7], res[7:])


def _adamw_math(w, g, m, v):
    m = ADAM_B1 * m + (1.0 - ADAM_B1) * g
    v = ADAM_B2 * v + (1.0 - ADAM_B2) * (g * g)
    m_hat = m / (1.0 - ADAM_B1 ** ADAM_STEP)
    v_hat = v / (1.0 - ADAM_B2 ** ADAM_STEP)
    delta = -ADAM_LR * (m_hat / (jnp.sqrt(v_hat) + ADAM_EPS) + ADAM_WD * w)
    return delta, m, v


def _adamw_many(items, name):
    n = len(items)

    def body(*refs):
        ins, outs = refs[:4 * n], refs[4 * n:]
        for k in range(n):
            w, g, m, v = (r[...] for r in ins[4 * k:4 * k + 4])
            outs[3 * k][...], outs[3 * k + 1][...], outs[3 * k + 2][...] = _adamw_math(w, g, m, v)

    flat = [a for item in items for a in item]
    res = pl.pallas_call(body, out_shape=[_sds(item[0].shape, F32) for item in items for _ in range(3)],
                         name=name)(*flat)
    return [tuple(res[3 * k:3 * k + 3]) for k in range(n)]


def _adamw(w, g, m, v, name):
    r, c = w.shape
    tm = next(cand for cand in (256, 176, 128, 64, 32, 16, 8) if r % cand == 0)
    return _rowcall(lambda wv, gv, mv, vv: (gv,) + _adamw_math(wv, gv, mv, vv),
                    [(w, c, 0), (g, c, 0), (m, c, 0), (v, c, 0)], [], [(r, c, F32, c)] * 4, [], name=name, tm=tm)


def _ada_fwd(c_all, w_shard, b_shard, name):
    n = w_shard.shape[1]
    tn = 512

    def body(c_ref, w_ref, b_ref, o_ref, a_ref):
        cv = c_ref[...]
        act = cv * _sigmoid(cv)
        a_ref[...] = act
        o_ref[...] = jnp.dot(act.astype(BF16), w_ref[...].astype(BF16), preferred_element_type=F32) + b_ref[...]

    return pl.pallas_call(
        body, grid=(n // tn,),
        in_specs=[pl.BlockSpec((8, D), lambda j: (0, 0)), pl.BlockSpec((D, tn), lambda j: (0, j)),
                  pl.BlockSpec((1, tn), lambda j: (0, j))],
        out_specs=[pl.BlockSpec((8, tn), lambda j: (0, j)), pl.BlockSpec((8, D), lambda j: (0, 0))],
        out_shape=[_sds((8, n), F32), _sds((8, D), F32)],
        compiler_params=_params(("arbitrary",)), name=name,
    )(c_all, w_shard, b_shard)


def _ada_bwd_adamw(act_t, dmod_shard, w, m, v, name):
    r, c = w.shape
    tm = 256

    def body(a_ref, d_ref, w_ref, m_ref, v_ref, g_ref, dl_ref, nm_ref, nv_ref):
        g = jnp.dot(a_ref[...], d_ref[...], precision=lax.Precision.HIGHEST, preferred_element_type=F32)
        g_ref[...] = g
        dl_ref[...], nm_ref[...], nv_ref[...] = _adamw_math(w_ref[...], g, m_ref[...], v_ref[...])

    blk = pl.BlockSpec((tm, c), lambda i: (i, 0))
    return pl.pallas_call(
        body, grid=(r // tm,),
        in_specs=[pl.BlockSpec((tm, 8), lambda i: (i, 0)), pl.BlockSpec((8, c), lambda i: (0, 0)), blk, blk, blk],
        out_specs=[blk] * 4, out_shape=[_sds((r, c), F32)] * 4,
        compiler_params=_params(("arbitrary",)), name=name,
    )(act_t, dmod_shard, w, m, v)


def _place():
    return lax.axis_index("x"), lax.axis_index("y"), lax.axis_index("c")


def _flip(v, bit):
    return 1 - v if bit else v


VMEM_SPEC = pl.BlockSpec(memory_space=pltpu.VMEM)


def _allgather8(v, name):
    r, c = v.shape

    def body(v_ref, g_ref, tot_ref, send_sems, recv_sems, local_sem):
        x, y, cc = _place()
        me = 4 * x + 2 * y + cc
        mine = pltpu.make_async_copy(v_ref, g_ref.at[me], local_sem)
        mine.start()
        sends = []
        for k in range(1, 8):
            peer = (_flip(x, k & 4), _flip(y, k & 2), _flip(cc, k & 1))
            cp = pltpu.make_async_remote_copy(src_ref=v_ref, dst_ref=g_ref.at[me], send_sem=send_sems.at[k - 1],
                                              recv_sem=recv_sems.at[k - 1], device_id=peer, device_id_type=MESH)
            cp.start()
            sends.append(cp)
        for k in range(1, 8):
            peer = (_flip(x, k & 4), _flip(y, k & 2), _flip(cc, k & 1))
            theirs = g_ref.at[4 * peer[0] + 2 * peer[1] + peer[2]]
            pltpu.make_async_remote_copy(src_ref=v_ref, dst_ref=theirs, send_sem=send_sems.at[k - 1],
                                         recv_sem=recv_sems.at[k - 1], device_id=peer, device_id_type=MESH).wait_recv()
        for cp in sends:
            cp.wait_send()
        mine.wait()
        tot = g_ref[0]
        for d in range(1, 8):
            tot = tot + g_ref[d]
        tot_ref[...] = tot

    return pl.pallas_call(
        body, in_specs=[VMEM_SPEC], out_specs=[VMEM_SPEC, VMEM_SPEC],
        out_shape=[_sds((8, r, c), F32), _sds((r, c), F32)],
        scratch_shapes=[pltpu.SemaphoreType.DMA((7,)), pltpu.SemaphoreType.DMA((7,)), pltpu.SemaphoreType.DMA],
        compiler_params=pltpu.CompilerParams(vmem_limit_bytes=VMEM_LIMIT), name=name,
    )(v)


def _slot(px, py, swapped):
    return 2 * py + px if swapped else 2 * px + py


def _gather_shards(arrs, swapped, name, in_place=False):
    n = len(arrs)

    def body(*refs):
        ins, outs = refs[:n], refs[n:2 * n]
        send1, recv1, send2, recv2, local_sems = refs[2 * n:]
        x, y, c = _place()
        sibling = (x, y, 1 - c)
        chips = [(_flip(x, k & 2), _flip(y, k & 1)) for k in (1, 2, 3)]
        local_copies, sends = [], []
        for a in range(n):
            h = outs[a].shape[1] // 2
            mine = pl.ds(pl.multiple_of(c * h, 8), h)
            own = _slot(x, y, swapped[a])
            if in_place:
                src = outs[a].at[own, mine]
            else:
                src = ins[a].at[mine]
                lc = pltpu.make_async_copy(ins[a], outs[a].at[own], local_sems.at[a])
                lc.start()
                local_copies.append(lc)
            for j, (px, py) in enumerate(chips):
                cp = pltpu.make_async_remote_copy(
                    src_ref=src, dst_ref=outs[a].at[own, mine], send_sem=send1.at[3 * a + j],
                    recv_sem=recv1.at[3 * a + j], device_id=(px, py, c), device_id_type=MESH)
                cp.start()
                sends.append(cp)
        for a in range(n):
            h = outs[a].shape[1] // 2
            mine = pl.ds(pl.multiple_of(c * h, 8), h)
            for j, (px, py) in enumerate(chips):
                piece = outs[a].at[_slot(px, py, swapped[a]), mine]
                pltpu.make_async_remote_copy(
                    src_ref=piece, dst_ref=piece, send_sem=send1.at[3 * a + j], recv_sem=recv1.at[3 * a + j],
                    device_id=(px, py, c), device_id_type=MESH).wait_recv()
                fwd = pltpu.make_async_remote_copy(
                    src_ref=piece, dst_ref=piece, send_sem=send2.at[3 * a + j], recv_sem=recv2.at[3 * a + j],
                    device_id=sibling, device_id_type=MESH)
                fwd.start()
                sends.append(fwd)
        for a in range(n):
            h = outs[a].shape[1] // 2
            other = pl.ds(pl.multiple_of((1 - c) * h, 8), h)
            for j, (px, py) in enumerate(chips):
                piece = outs[a].at[_slot(px, py, swapped[a]), other]
                pltpu.make_async_remote_copy(
                    src_ref=piece, dst_ref=piece, send_sem=send2.at[3 * a + j], recv_sem=recv2.at[3 * a + j],
                    device_id=sibling, device_id_type=MESH).wait_recv()
        for cp in sends:
            cp.wait_send()
        for lc in local_copies:
            lc.wait()

    dma = lambda k: pltpu.SemaphoreType.DMA((k,))
    return pl.pallas_call(
        body, in_specs=[ANY] * n, out_specs=[ANY] * n,
        out_shape=[_sds(a.shape if in_place else (4,) + a.shape, a.dtype) for a in arrs],
        scratch_shapes=[dma(3 * n), dma(3 * n), dma(3 * n), dma(3 * n), dma(n)],
        input_output_aliases={a: a for a in range(n)} if in_place else {},
        name=name,
    )(*arrs)


def _carry_pair_exchange(grads):
    n = len(grads)

    def copies(ins, outs, send_sems, recv_sems):
        x, y, c = _place()
        cps = []
        for a in range(n):
            h = ins[a].shape[1] // 2
            theirs = pl.ds(pl.multiple_of((1 - c) * h, 8), h)
            cps.append(pltpu.make_async_remote_copy(
                src_ref=ins[a].at[:, theirs, :], dst_ref=outs[a], send_sem=send_sems.at[a], recv_sem=recv_sems.at[a],
                device_id=(x, y, 1 - c), device_id_type=MESH))
        return cps

    def start(*refs):
        for cp in copies(*refs):
            cp.start()

    def finish(*refs):
        for cp in copies(*refs):
            cp.wait()

    return _Carried(grads, [_sds((4, g.shape[1] // 2, g.shape[2]), F32) for g in grads], {}, n, start, finish)


def _pair_sum(grad, recv, core, name):
    _, r, c = grad.shape
    h = r // 2

    def body(core_ref, g_ref, r_ref, o_ref):
        o_ref[...] = (g_ref[...] + r_ref[...]).astype(BF16)

    return pl.pallas_call(
        body,
        grid_spec=pltpu.PrefetchScalarGridSpec(
            num_scalar_prefetch=1, grid=(4,),
            in_specs=[pl.BlockSpec((None, h, c), lambda s, core_ref: (s, core_ref[0], 0)),
                      pl.BlockSpec((None, h, c), lambda s, core_ref: (s, 0, 0))],
            out_specs=pl.BlockSpec((None, h, c), lambda s, core_ref: (s, 0, 0))),
        out_shape=_sds((4, h, c), BF16), compiler_params=_params(("arbitrary",)), name=name,
    )(core, grad, recv)


def _carry_chip_exchange(parts, swapped):
    n = len(parts)

    def copies(ins, outs, send_sems, recv_sems):
        x, y, c = _place()
        chips = [(_flip(x, k & 2), _flip(y, k & 1)) for k in (1, 2, 3)]
        cps = []
        for a in range(n):
            for j, (px, py) in enumerate(chips):
                cps.append(pltpu.make_async_remote_copy(
                    src_ref=ins[a].at[_slot(px, py, swapped[a])], dst_ref=outs[a].at[j],
                    send_sem=send_sems.at[3 * a + j], recv_sem=recv_sems.at[3 * a + j],
                    device_id=(px, py, c), device_id_type=MESH))
        return cps

    def start(*refs):
        for cp in copies(*refs):
            cp.start()

    def finish(*refs):
        for cp in copies(*refs):
            cp.wait()

    return _Carried(parts, [_sds((3,) + p.shape[1:], BF16) for p in parts], {}, 3 * n, start, finish)


def _chip_sum(part, recv, slot_core, name):
    _, h, c = part.shape

    def body(sc_ref, p_ref, r_ref, o_ref):
        acc = p_ref[...].astype(F32)
        for j in range(3):
            acc = acc + r_ref[j].astype(F32)
        o_ref[...] = acc

    return pl.pallas_call(
        body,
        grid_spec=pltpu.PrefetchScalarGridSpec(
            num_scalar_prefetch=1, grid=(1,),
            in_specs=[pl.BlockSpec((None, h, c), lambda s, sc_ref: (sc_ref[0], 0, 0)),
                      pl.BlockSpec((3, h, c), lambda s, sc_ref: (0, 0, 0))],
            out_specs=pl.BlockSpec((h, c), lambda s, sc_ref: (sc_ref[1], 0))),
        out_shape=_sds((2 * h, c), F32), compiler_params=_params(("arbitrary",)), name=name,
    )(slot_core, part, recv)


def _carry_pair_share(shards):
    n = len(shards)

    def copies(outs, send_sems, recv_sems, mine):
        x, y, c = _place()
        cps = []
        for a in range(n):
            h = outs[a].shape[0] // 2
            half = outs[a].at[pl.ds(pl.multiple_of((c if mine else 1 - c) * h, 8), h)]
            cps.append(pltpu.make_async_remote_copy(
                src_ref=half, dst_ref=half, send_sem=send_sems.at[a], recv_sem=recv_sems.at[a],
                device_id=(x, y, 1 - c), device_id_type=MESH))
        return cps

    def start(ins, outs, send_sems, recv_sems):
        for cp in copies(outs, send_sems, recv_sems, True):
            cp.start()

    def finish(ins, outs, send_sems, recv_sems):
        for cp in copies(outs, send_sems, recv_sems, False):
            cp.wait_recv()
        for cp in copies(outs, send_sems, recv_sems, True):
            cp.wait_send()

    return _Carried(shards, [_sds(s.shape, F32) for s in shards], {a: a for a in range(n)}, n, start, finish)


def _carry_gather_ici(bufs, swapped):
    n = len(bufs)

    def copies(outs, send_sems, recv_sems, sending):
        x, y, c = _place()
        cps = []
        for a in range(n):
            h = outs[a].shape[1] // 2
            mine = pl.ds(pl.multiple_of(c * h, 8), h)
            for j, k in enumerate((1, 2, 3)):
                px, py = _flip(x, k & 2), _flip(y, k & 1)
                slot = _slot(x, y, swapped[a]) if sending else _slot(px, py, swapped[a])
                piece = outs[a].at[slot, mine]
                cps.append(pltpu.make_async_remote_copy(
                    src_ref=piece, dst_ref=piece, send_sem=send_sems.at[3 * a + j], recv_sem=recv_sems.at[3 * a + j],
                    device_id=(px, py, c), device_id_type=MESH))
        return cps

    def start(ins, outs, send_sems, recv_sems):
        for cp in copies(outs, send_sems, recv_sems, True):
            cp.start()

    def finish(ins, outs, send_sems, recv_sems):
        for cp in copies(outs, send_sems, recv_sems, False):
            cp.wait_recv()
        for cp in copies(outs, send_sems, recv_sems, True):
            cp.wait_send()

    return _Carried(bufs, [_sds(b.shape, b.dtype) for b in bufs], {a: a for a in range(n)}, 3 * n, start, finish)


def _carry_gather_forward(bufs, swapped):
    n = len(bufs)

    def copies(outs, send_sems, recv_sems, sending):
        x, y, c = _place()
        cps = []
        for a in range(n):
            h = outs[a].shape[1] // 2
            rows = pl.ds(pl.multiple_of((c if sending else 1 - c) * h, 8), h)
            for j, k in enumerate((1, 2, 3)):
                piece = outs[a].at[_slot(_flip(x, k & 2), _flip(y, k & 1), swapped[a]), rows]
                cps.append(pltpu.make_async_remote_copy(
                    src_ref=piece, dst_ref=piece, send_sem=send_sems.at[3 * a + j], recv_sem=recv_sems.at[3 * a + j],
                    device_id=(x, y, 1 - c), device_id_type=MESH))
        return cps

    def start(ins, outs, send_sems, recv_sems):
        for cp in copies(outs, send_sems, recv_sems, True):
            cp.start()

    def finish(ins, outs, send_sems, recv_sems):
        for cp in copies(outs, send_sems, recv_sems, False):
            cp.wait_recv()
        for cp in copies(outs, send_sems, recv_sems, True):
            cp.wait_send()

    return _Carried(bufs, [_sds(b.shape, b.dtype) for b in bufs], {a: a for a in range(n)}, 3 * n, start, finish)


def _pack(arrs, rows_multiple=8):
    parts, offs, row = [], [], 0
    for a in arrs:
        flat = a.reshape(-1)
        nrow = -(-flat.shape[0] // D)
        parts.append(jnp.pad(flat, (0, nrow * D - flat.shape[0])))
        offs.append(row)
        row += nrow
    total = -(-row // rows_multiple) * rows_multiple
    if total > row:
        parts.append(jnp.zeros(((total - row) * D,), F32))
    return jnp.concatenate(parts).reshape(total, D), offs


def _unpack(packed, offs, shapes):
    out = []
    for off, shp in zip(offs, shapes):
        size = int(np.prod(shp))
        nrow = -(-size // D)
        out.append(packed[off:off + nrow].reshape(-1)[:size].reshape(shp))
    return out


def _to_bf16_slot(w, slot, name):
    r, c = w.shape
    tm = next(cand for cand in (256, 176, 128, 64, 32, 16) if r % cand == 0)

    def body(slot_ref, w_ref, o_ref):
        o_ref[...] = w_ref[...].astype(BF16)

    return pl.pallas_call(
        body,
        grid_spec=pltpu.PrefetchScalarGridSpec(
            num_scalar_prefetch=1, grid=(r // tm,),
            in_specs=[pl.BlockSpec((tm, c), lambda i, slot_ref: (i, 0))],
            out_specs=pl.BlockSpec((None, tm, c), lambda i, slot_ref: (slot_ref[0], i, 0))),
        out_shape=_sds((4, r, c), BF16), compiler_params=_params(("arbitrary",)), name=name,
    )(slot, w)


def _unshard_cols(g):
    s, k, n = g.shape
    return jnp.transpose(g, (1, 0, 2)).reshape(k, s * n)


def _ff_swap(v):
    b = FF_BLOCK
    return jnp.concatenate([v[..., 0:b], v[..., 2 * b:3 * b], v[..., b:2 * b], v[..., 3 * b:4 * b]], axis=-1)


LATE = ("attn_o", "conv_o", "mix_o", "up", "down")
EARLY_GRADS = ("down", "up", "mix_o", "attn_o", "conv_o")


def _weight_views(bufs):
    return {"up": bufs["up"], "attn_o": _unshard_cols(bufs["attn_o"]), "conv_o": _unshard_cols(bufs["conv_o"]),
            "mix_o": bufs["mix_o"].reshape(D, D), "down": bufs["down"].reshape(D_FF, D)}


def _pair_sums(names, grads, recv, dist):
    return [_pair_sum(g, r, dist["core"], "pair_sum_" + n) for n, g, r in zip(names, grads, recv)]


def _reduce_halves(names, parts, from_chips, dist):
    return [_chip_sum(p, r, jnp.concatenate([dist["slots"][SWAPPED[n]], dist["core"]]), "chip_sum_" + n)
            for n, p, r in zip(names, parts, from_chips)]


FUSED_TILE = 256
WIDE_TILE = 512


def _gates(z):
    return [(z, 512, 5), (z, 512, 6), (z, 512, 7), (z, 512, 8)]


def _mix_out(a, cb, z, x, w_mix_o, g_post, gt, g_pre2, sc2, sh2, name):
    def lhs(av, cv, ga0, ga1, gb0, gb1):
        ga, gb = jnp.concatenate([ga0, ga1], axis=1), jnp.concatenate([gb0, gb1], axis=1)
        return _sigmoid(ga) * av + _sigmoid(gb) * cv

    def fn(ym, y, xv, gv, gtv, g2v, scv, shv):
        yn, _ = _rms_stats(ym)
        x1 = xv + gtv * (yn * gv)
        xn, _ = _rms_stats(x1)
        return ym, y, x1, (xn * g2v) * (1.0 + scv) + shv

    return _matmul_rows(w_mix_o, form="nn", tm=min(FUSED_TILE, x.shape[0]), tk=D, fn=fn, a_rows=[(a, D, 0), (cb, D, 0)] + _gates(z),
                        a_fn=lhs, rows=[(x, D, 0)], consts=[g_post, gt, g_pre2, sc2, sh2],
                        row_outs=[(F32, D), (BF16, D), (F32, D), (BF16, D)], acc_outs=[], name=name)


def _down_tail(act, w_down, x1, target, g, gt, name):
    def fn(yv, xv, tv, gv, gtv):
        yn, r = _rms_stats(yv)
        e = xv + gtv * (yn * gv) - tv
        dx2 = e * (1.0 / D)
        dyn = dx2 * gtv
        return (dx2, _rms_bwd(dyn * gv, yn, r), _colsum(e * e) * (0.5 / D), _colsum(dyn * yn),
                _colsum(dx2 * (yn * gv)))

    return _matmul_rows(w_down, form="nn", a=act, tm=min(WIDE_TILE, x1.shape[0]), tk=FF_BLOCK, fn=fn,
                        rows=[(x1, D, 0), (target, D, 0)], consts=[g, gt], row_outs=[(F32, D), (BF16, D)],
                        acc_outs=[(1, D)] * 3, name=name)


def _up_dx_tail(dup, w_up, x1, dx2, ym, g_pre2, sc2, g_post, gt, name):
    def fn(dh, xv, dov, ymv, g2v, scv, gv, gtv):
        xn, r = _rms_stats(xv)
        dyn = dh * (1.0 + scv)
        dx1 = dov + _rms_bwd(dyn * g2v, xn, r)
        yn, r2 = _rms_stats(ymv)
        dynm = dx1 * gtv
        return (dx1, _rms_bwd(dynm * gv, yn, r2), _colsum(dh), _colsum(dh * (xn * g2v)), _colsum(dyn * xn),
                _colsum(dynm * yn), _colsum(dx1 * (yn * gv)))

    return _matmul_rows(w_up, form="nt", a=dup, tm=min(WIDE_TILE, x1.shape[0]), tk=FF_BLOCK, fn=fn,
                        rows=[(x1, D, 0), (dx2, D, 0), (ym, D, 0)], consts=[g_pre2, sc2, g_post, gt],
                        row_outs=[(F32, D), (BF16, D)], acc_outs=[(1, D)] * 5, name=name)


def _mix_dx_gates(dym, w_mix_o, a, cb, z, name):
    def fn(dy, av, cv, ga0, ga1, gb0, gb1):
        sa = _sigmoid(jnp.concatenate([ga0, ga1], axis=1))
        sb = _sigmoid(jnp.concatenate([gb0, gb1], axis=1))
        dcb = dy * sb
        dga = dy * av * (sa * (1.0 - sa))
        dgb = dy * cv * (sb * (1.0 - sb))
        return dy * sa, dcb, dga, dgb, _colsum(dcb), _colsum(dga), _colsum(dgb)

    return _matmul_rows(w_mix_o, form="nt", a=dym, tm=min(FUSED_TILE, a.shape[0]), tk=D, fn=fn,
                        rows=[(a, D, 0), (cb, D, 0)] + _gates(z), consts=[], row_outs=[(BF16, D)] * 4,
                        acc_outs=[(1, D)] * 3, name=name)


def _local_step(x, target, mod, w_in, late, small, dist=None):
    sh_m, sc_m, gt_m, sh_f, sc_f, gt_f = mod
    t = x.shape[0]
    tmm = min(1024, t)
    late_swapped = [SWAPPED[n] for n in LATE]

    h1 = _pre_norm(x, small["g_pre_mix"], sc_m, sh_m, "pre_norm_mix")
    z = _matmul(h1, w_in, form="nn", out_dtype=F32, tm=tmm, tn=1152, tk=D, bias=small["b_in"], name="mm_in")
    if dist is None:
        att = _attention(z, small["gen"], "attention")
        bufs = dict(late)
    else:
        mid = [n for n in LATE if n != "down"]
        mid_swapped = [SWAPPED[n] for n in mid]
        att, landed = _attention(z, small["gen"], "attention",
                                 carried=_carry_gather_ici([late[n] for n in mid], mid_swapped))
        bufs = dict(zip(mid, _run_carried(_carry_gather_forward(landed, mid_swapped), "gather_forward")))
        bufs["down"] = late["down"]
    w = _weight_views(bufs)
    w["in"] = w_in
    a = _matmul(att, w["attn_o"], form="nn", out_dtype=F32, tm=tmm, tn=512, tk=512, name="mm_attn_o")
    u1, u3 = _conv_branch(z, small["w_dw_conv"], small["b_dw_conv"], small["g_conv_ln"], small["b_conv_ln"], "conv_branch")
    cb = _matmul(u3, w["conv_o"], form="nn", out_dtype=F32, tm=tmm, tn=512, tk=512, bias=small["b_conv_o"], name="mm_conv_o")
    ym, y, x1, h2 = _mix_out(a, cb, z, x, w["mix_o"], small["g_post_mix"], gt_m, small["g_pre_ffn"], sc_f, sh_f, "mix_out")
    mm_up = dict(form="nn", out_dtype=F32, tm=tmm, tn=FF_BLOCK, tk=D, name="mm_up")
    if dist is None:
        up = _matmul(h2, w["up"], **mm_up)
    else:
        up, landed = _matmul(h2, w["up"], carried=_carry_gather_ici([late["down"]], [False]), **mm_up)
        w["down"] = _run_carried(_carry_gather_forward(landed, [False]), "gather_forward_down")[0].reshape(D_FF, D)
    act = _ffn_act(up, small["w_dw_ffn"], small["b_dw_ffn"], "ffn_act")

    dx2, dyf, loss_cols, d_g_post_ffn, d_gt_f = _down_tail(act, w["down"], x1, target, small["g_post_ffn"], gt_f, "down_tail")
    dact = _matmul(dyf, w["down"], form="nt", out_dtype=F32, tm=tmm, tn=FF_BLOCK, tk=D, name="mm_down_dx")
    g_down = _matmul(act, dyf, form="tn", out_dtype=F32, tm=FF_BLOCK, tn=512, tk=t, name="mm_down_dw")
    dup, d_w_dw_ffn, d_b_dw_ffn = _ffn_act_bwd(dact, up, small["w_dw_ffn"], small["b_dw_ffn"], "ffn_act_bwd")
    dx1, dym, d_sh_f, d_sc_f, d_g_pre_ffn, d_g_post_mix, d_gt_m = _up_dx_tail(
        dup, w["up"], x1, dx2, ym, small["g_pre_ffn"], sc_f, small["g_post_mix"], gt_m, "up_dx_tail")
    g_up = _matmul(h2, dup, form="tn", out_dtype=F32, tm=512, tn=FF_BLOCK, tk=t, out_sharded=True, name="mm_up_dw")
    da, dcb, dgate_a, dgate_b, d_b_conv_o, sga, sgb = _mix_dx_gates(dym, w["mix_o"], a, cb, z, "mix_dx_gates")
    g_mix_o = _matmul(y, dym, form="tn", out_dtype=F32, tm=D, tn=512, tk=t, name="mm_mix_o_dw")
    datt = _matmul(da, w["attn_o"], form="nt", out_dtype=F32, tm=tmm, tn=512, tk=D, name="mm_attn_o_dx")
    g_attn_o = _matmul(att, da, form="tn", out_dtype=F32, tm=512, tn=256, tk=t, out_sharded=True, name="mm_attn_o_dw")
    du3 = _matmul(dcb, w["conv_o"], form="nt", out_dtype=F32, tm=tmm, tn=512, tk=D, name="mm_conv_o_dx")
    g_conv_o = _matmul(u3, dcb, form="tn", out_dtype=F32, tm=512, tn=256, tk=t, out_sharded=True, name="mm_conv_o_dw")
    big = {"attn_o": g_attn_o, "conv_o": g_conv_o, "mix_o": g_mix_o.reshape(4, 256, D),
           "up": g_up, "down": g_down.reshape(4, D_FF // 4, D)}
    conv_bwd = (du3, u1, z, small["w_dw_conv"], small["g_conv_ln"], small["b_conv_ln"], "conv_branch_bwd")
    in_dw = dict(form="tn", out_dtype=F32, tm=512, tn=1152, tk=t, out_sharded=True, name="mm_in_dw")
    in_dx = dict(form="nt", out_dtype=F32, tm=tmm, tn=D, tk=1152, name="mm_in_dx")
    if dist is None:
        dglu, d_w_dw_conv, d_b_dw_conv, d_g_conv_ln, d_b_conv_ln, sglu = _conv_branch_bwd(*conv_bwd)
        dq, dk, dv, sq, sk, sv, dgen = _attention_bwd(z, datt, small["gen"], "attention_bwd")
        dz = jnp.concatenate([dq, dk, dv, dglu, dgate_a, dgate_b], axis=1)
        big["in"] = _matmul(h1, dz, **in_dw)
        dh1 = _matmul(dz, w_in, **in_dx)
    else:
        early = [big[n] for n in EARLY_GRADS]
        (dglu, d_w_dw_conv, d_b_dw_conv, d_g_conv_ln, d_b_conv_ln, sglu), recv = _conv_branch_bwd(
            *conv_bwd, carried=_carry_pair_exchange(early))
        parts = _pair_sums(EARLY_GRADS, early, recv, dist)
        (dq, dk, dv, sq, sk, sv, dgen), from_chips = _attention_bwd(
            z, datt, small["gen"], "attention_bwd",
            carried=_carry_chip_exchange(parts, [SWAPPED[n] for n in EARLY_GRADS]))
        halves = _reduce_halves(EARLY_GRADS, parts, from_chips, dist)
        dz = jnp.concatenate([dq, dk, dv, dglu, dgate_a, dgate_b], axis=1)
        g_in, shards = _matmul(h1, dz, carried=_carry_pair_share(halves), **in_dw)
        big = dict(zip(EARLY_GRADS, shards))
        recv_in = _run_carried(_carry_pair_exchange([g_in]), "pair_exchange_in")
        part_in = _pair_sums(("in",), [g_in], recv_in, dist)
        dh1, from_chips_in = _matmul(dz, w_in, carried=_carry_chip_exchange(part_in, [False]), **in_dx)
        half_in = _reduce_halves(("in",), part_in, from_chips_in, dist)
        big["in"] = _run_carried(_carry_pair_share(half_in), "pair_share_in")[0]
    d_b_in = jnp.concatenate([sq, sk, sv, sglu, sga, sgb], axis=1)
    grad_x, d_sh_m, d_sc_m, d_g_pre_mix = _pre_norm_bwd(dh1, x, dx1, small["g_pre_mix"], sc_m, "pre_norm_mix_bwd")

    dmod = [d_sh_m, d_sc_m, d_gt_m, d_sh_f, d_sc_f, d_gt_f]
    sm = {"g_pre_mix": d_g_pre_mix, "g_post_mix": d_g_post_mix, "b_in": d_b_in, "gen": dgen,
          "w_dw_conv": d_w_dw_conv, "b_dw_conv": d_b_dw_conv, "g_conv_ln": d_g_conv_ln, "b_conv_ln": d_b_conv_ln,
          "b_conv_o": d_b_conv_o, "g_pre_ffn": d_g_pre_ffn, "g_post_ffn": d_g_post_ffn,
          "w_dw_ffn": d_w_dw_ffn, "b_dw_ffn": d_b_dw_ffn}
    return loss_cols, grad_x, dmod, big, sm


BIG = ("in", "attn_o", "conv_o", "mix_o", "up", "down")
SWAPPED = {"in": False, "attn_o": False, "conv_o": False, "mix_o": False, "up": True, "down": False}
SMALL_ORDER = ("b_ada", "g_pre_mix", "g_post_mix", "b_in", "rel_bias", "b_dw_conv", "g_conv_ln", "b_conv_ln",
               "b_conv_o", "g_pre_ffn", "g_post_ffn", "b_dw_ffn", "w_dw_conv", "w_dw_ffn")


def kernel(x, c, w_ada, b_ada, g_pre_mix, g_post_mix, w_in, b_in, rel_bias, w_attn_o, w_dw_conv, b_dw_conv, g_conv_ln, b_conv_ln, w_conv_o, b_conv_o, w_mix_o, g_pre_ffn, g_post_ffn, w_up, w_dw_ffn, b_dw_ffn, w_down, loss_target, m_w_ada, m_b_ada, m_g_pre_mix, m_g_post_mix, m_w_in, m_b_in, m_rel_bias, m_w_attn_o, m_w_dw_conv, m_b_dw_conv, m_g_conv_ln, m_b_conv_ln, m_w_conv_o, m_b_conv_o, m_w_mix_o, m_g_pre_ffn, m_g_post_ffn, m_w_up, m_w_dw_ffn, m_b_dw_ffn, m_w_down, v_w_ada, v_b_ada, v_g_pre_mix, v_g_post_mix, v_w_in, v_b_in, v_rel_bias, v_w_attn_o, v_w_dw_conv, v_b_dw_conv, v_g_conv_ln, v_b_conv_ln, v_w_conv_o, v_b_conv_o, v_w_mix_o, v_g_pre_ffn, v_g_post_ffn, v_w_up, v_w_dw_ffn, v_b_dw_ffn, v_w_down):
    given = dict(locals())
    ax, ay, ac = lax.axis_index("x"), lax.axis_index("y"), lax.axis_index("c")
    shard = 2 * ax + ay
    me = 4 * ax + 2 * ay + ac
    xs, target = x[0], loss_target[0]

    c_pad = jnp.pad(c, ((0, 7), (0, 0)))
    c_g, _ = _allgather8(c_pad, "gather_c")
    c_all = c_g[:, 0, :]
    b_ada_shard = lax.dynamic_slice(b_ada, (0, shard * 1536), (1, 1536))
    mod_shard, c_act = _ada_fwd(c_all, w_ada[0], b_ada_shard, "ada_fwd")
    small_in = [jnp.pad(mod_shard, ((0, 8), (0, 0))),
                jnp.pad(w_dw_conv[0], ((0, 1), (0, 0))),
                jnp.pad(w_dw_ffn[0], ((0, 13), (0, 0)))]
    mod_g, wdc_g, wdf_g = _gather_shards(small_in, [False, False, True], "gather_small")
    mod_all = jnp.transpose(mod_g[:, :8, :], (1, 0, 2)).reshape(8, 6 * D)
    mod_row = lax.dynamic_slice(mod_all, (me, 0), (1, 6 * D))
    mod = [mod_row[:, k * D:(k + 1) * D] for k in range(6)]

    slots = {sw: _slot(ax, ay, sw).astype(jnp.int32).reshape(1) for sw in (False, True)}
    own = {n: _to_bf16_slot(given["w_" + n][0], slots[SWAPPED[n]], "cast_" + n) for n in BIG}
    w_in_all = _gather_shards([own["in"]], [False], "gather_w_in", in_place=True)[0]
    core = ac.astype(jnp.int32).reshape(1)
    dist = {"core": core, "slots": slots}

    sel = jnp.asarray(_toeplitz_map())
    rel_pad = jnp.pad(rel_bias[0], ((0, 0), (0, REL_PAD - (2 * MAX_REL + 1))))
    gen = _select_call(rel_pad, sel.T.astype(BF16), "bias_rows")
    small = {"g_pre_mix": g_pre_mix, "g_post_mix": g_post_mix, "b_in": b_in, "gen": gen,
             "w_dw_conv": _unshard_cols(wdc_g[:, :CONV_K, :]), "b_dw_conv": b_dw_conv, "g_conv_ln": g_conv_ln,
             "b_conv_ln": b_conv_ln, "b_conv_o": b_conv_o, "g_pre_ffn": g_pre_ffn, "g_post_ffn": g_post_ffn,
             "w_dw_ffn": _unshard_cols(wdf_g[:, :FFN_K, :]), "b_dw_ffn": _ff_swap(b_dw_ffn)}

    loss_cols, grad_x, dmod, reduced, sm = _local_step(xs, target, mod, w_in_all, {n: own[n] for n in LATE}, small, dist)

    d_rel = _select_call(sm["gen"], sel.astype(BF16), "bias_fold")[:, :2 * MAX_REL + 1]
    small_grads = {"g_pre_mix": sm["g_pre_mix"], "g_post_mix": sm["g_post_mix"], "b_in": sm["b_in"], "rel_bias": d_rel[None],
                   "b_dw_conv": sm["b_dw_conv"], "g_conv_ln": sm["g_conv_ln"], "b_conv_ln": sm["b_conv_ln"],
                   "b_conv_o": sm["b_conv_o"], "g_pre_ffn": sm["g_pre_ffn"], "g_post_ffn": sm["g_post_ffn"],
                   "b_dw_ffn": _ff_swap(sm["b_dw_ffn"]), "w_dw_conv": sm["w_dw_conv"], "w_dw_ffn": _ff_swap(sm["w_dw_ffn"])}
    order = [n for n in SMALL_ORDER if n != "b_ada"]
    packed, offs = _pack([jnp.concatenate(dmod, axis=1)] + [small_grads[n] for n in order] + [loss_cols])
    every, total = _allgather8(packed, "gather_small_grads")
    loss = jnp.sum(total[offs[-1]])
    offs = offs[:-1]
    dmod_all = every[:, 0:6, :].reshape(8, 6 * D)
    full_shapes = {n: given[n].shape for n in order}
    full_shapes["w_dw_conv"], full_shapes["w_dw_ffn"] = (1, CONV_K, 512), (1, FFN_K, 2 * D_FF)
    sums = dict(zip(order, _unpack(total, offs[1:], [full_shapes[n] for n in order])))
    sums["b_ada"] = total[0:6].reshape(1, 6 * D)
    sums["w_dw_conv"] = lax.dynamic_slice(sums["w_dw_conv"], (0, 0, shard * 128), (1, CONV_K, 128))
    sums["w_dw_ffn"] = lax.dynamic_slice(sums["w_dw_ffn"], (0, 0, shard * FF_BLOCK), (1, FFN_K, FF_BLOCK))

    upd = dict(zip(SMALL_ORDER, _adamw_many(
        [(given[n], sums[n], given["m_" + n], given["v_" + n]) for n in SMALL_ORDER], "adamw_small")))

    dmod_shard = lax.dynamic_slice(dmod_all, (0, shard * 1536), (8, 1536))
    ada = _ada_bwd_adamw(c_act.T, dmod_shard, w_ada[0], m_w_ada[0], v_w_ada[0], "ada_bwd_adamw")

    out = {"grad_w_ada": ada[0][None], "delta_w_ada": ada[1][None], "new_m_w_ada": ada[2][None], "new_v_w_ada": ada[3][None]}
    for n in BIG:
        g = reduced[n]
        g, dl, nm, nv = _adamw(given["w_" + n][0], g, given["m_w_" + n][0], given["v_w_" + n][0], "adamw_" + n)
        out["grad_w_" + n], out["delta_w_" + n], out["new_m_w_" + n], out["new_v_w_" + n] = g[None], dl[None], nm[None], nv[None]
    for n in SMALL_ORDER:
        out["grad_" + n], out["delta_" + n], out["new_m_" + n], out["new_v_" + n] = sums[n], *upd[n]

    weights = ["w_ada", "b_ada", "g_pre_mix", "g_post_mix", "w_in", "b_in", "rel_bias", "w_attn_o", "w_dw_conv", "b_dw_conv",
               "g_conv_ln", "b_conv_ln", "w_conv_o", "b_conv_o", "w_mix_o", "g_pre_ffn", "g_post_ffn", "w_up", "w_dw_ffn",
               "b_dw_ffn", "w_down"]
    return (loss, grad_x[None], *[out["grad_" + n] for n in weights], *[out["delta_" + n] for n in weights],
            *[out["new_m_" + n] for n in weights], *[out["new_v_" + n] for n in weights])
```

```python
import functools
import math

import numpy as np
import jax
import jax.numpy as jnp
from jax import lax
from jax.experimental import pallas as pl
from jax.experimental.pallas import tpu as pltpu

F32, BF16 = jnp.float32, jnp.bfloat16
MESH = pl.DeviceIdType.MESH

D = 1024
D_IN = 4608
D_FF = 2816
CONV_K = 31
FFN_K = 3
N_HEADS = 8
CHUNK = 64
LEFT_CHUNKS = 8
MAX_REL = 128
EPS = 1e-6
NEG_INF = -1e30
Q_TILE = 256
WINDOW = Q_TILE + LEFT_CHUNKS * CHUNK
REL_PAD = 384
TOEP = 1024
ROW_TILE = 256
VMEM_LIMIT = 60 * 1024 * 1024

ADAM_LR, ADAM_B1, ADAM_B2, ADAM_EPS, ADAM_WD, ADAM_STEP = 0.001, 0.9, 0.999, 1e-08, 0.01, 10


def _params(sem=None):
    return pltpu.CompilerParams(dimension_semantics=sem, vmem_limit_bytes=VMEM_LIMIT)


def _sds(shape, dtype):
    return jax.ShapeDtypeStruct(tuple(shape), dtype)


ANY = pl.BlockSpec(memory_space=pl.ANY)


class _Carried:
    def __init__(self, ins, out_shapes, aliases, n_sems, start, finish):
        self.ins, self.out_shapes, self.aliases = list(ins), list(out_shapes), dict(aliases)
        self.n_sems, self.start, self.finish = n_sems, start, finish


def _call(body, *, grid, in_specs, out_specs, out_shape, scratch_shapes, sem, name, args, carried=None):
    in_specs, out_specs, out_shape = list(in_specs), list(out_specs), list(out_shape)
    scratch_shapes = list(scratch_shapes)
    if carried is None:
        return pl.pallas_call(body, grid=grid, in_specs=in_specs, out_specs=out_specs, out_shape=out_shape,
                              scratch_shapes=scratch_shapes, compiler_params=_params(sem), name=name)(*args)
    n_in, n_out, n_scr = len(in_specs), len(out_specs), len(scratch_shapes)
    c_in, c_out = len(carried.ins), len(carried.out_shapes)

    def full(*refs):
        pos = [0]

        def take(k):
            part = refs[pos[0]:pos[0] + k]
            pos[0] += k
            return part

        ins, cins, outs, couts, scr = take(n_in), take(c_in), take(n_out), take(c_out), take(n_scr)
        send_sems, recv_sems = take(2)
        first = last = None
        for d, size in enumerate(grid):
            pid = pl.program_id(d)
            first = (pid == 0) if first is None else first & (pid == 0)
            last = (pid == size - 1) if last is None else last & (pid == size - 1)

        @pl.when(first)
        def _():
            carried.start(cins, couts, send_sems, recv_sems)

        body(*ins, *outs, *scr)

        @pl.when(last)
        def _():
            carried.finish(cins, couts, send_sems, recv_sems)

    sems = [pltpu.SemaphoreType.DMA((carried.n_sems,)), pltpu.SemaphoreType.DMA((carried.n_sems,))]
    return pl.pallas_call(
        full, grid=grid, in_specs=in_specs + [ANY] * c_in, out_specs=out_specs + [ANY] * c_out,
        out_shape=out_shape + carried.out_shapes, scratch_shapes=scratch_shapes + sems,
        input_output_aliases={n_in + k: n_out + v for k, v in carried.aliases.items()},
        compiler_params=_params(tuple("arbitrary" for _ in grid)), name=name,
    )(*args, *carried.ins)


def _run_carried(carried, name):
    c_in = len(carried.ins)

    def body(*refs):
        cins, couts = refs[:c_in], refs[c_in:c_in + len(carried.out_shapes)]
        send_sems, recv_sems = refs[-2:]
        carried.start(cins, couts, send_sems, recv_sems)
        carried.finish(cins, couts, send_sems, recv_sems)

    return pl.pallas_call(
        body, in_specs=[ANY] * c_in, out_specs=[ANY] * len(carried.out_shapes), out_shape=carried.out_shapes,
        scratch_shapes=[pltpu.SemaphoreType.DMA((carried.n_sems,)), pltpu.SemaphoreType.DMA((carried.n_sems,))],
        input_output_aliases=carried.aliases, name=name,
    )(*carried.ins)


def _matmul(a, b, *, form, out_dtype, tm, tn, tk, name, bias=None, add=None, out_sharded=False, carried=None):
    b3 = b.ndim == 3
    resident = 0
    if form == "nn":
        m, k = a.shape
        n = b.shape[0] * b.shape[2] if b3 else b.shape[1]
        dn = (((1,), (0,)), ((), ()))
        a_spec = pl.BlockSpec((tm, tk), lambda i, j, kk: (i, kk))
        b_spec = (pl.BlockSpec((None, tk, tn), lambda i, j, kk: (j, kk, 0)) if b3
                  else pl.BlockSpec((tk, tn), lambda i, j, kk: (kk, j)))
    elif form == "nt":
        m, k = a.shape
        n = b.shape[1] if b3 else b.shape[0]
        dn = (((1,), (1,)), ((), ()))
        a_spec = pl.BlockSpec((tm, tk), lambda i, j, kk: (i, kk))
        if b3 and tk == k:
            resident = b.shape[0]
            b_spec = pl.BlockSpec((resident, tn, b.shape[2]), lambda i, j, kk: (0, j, 0))
        else:
            b_spec = (pl.BlockSpec((None, tn, tk), lambda i, j, kk: (kk, j, 0)) if b3
                      else pl.BlockSpec((tn, tk), lambda i, j, kk: (j, kk)))
    else:
        k, m = a.shape
        n = b.shape[1]
        dn = (((0,), (0,)), ((), ()))
        a_spec = pl.BlockSpec((tk, tm), lambda i, j, kk: (kk, i))
        b_spec = pl.BlockSpec((tk, tn), lambda i, j, kk: (kk, j))
    assert m % tm == 0 and n % tn == 0 and k % tk == 0, (name, m, n, k, tm, tn, tk)
    nk = k // tk
    in_specs, args = [a_spec, b_spec], [a, b]
    if bias is not None:
        in_specs.append(pl.BlockSpec((1, tn), lambda i, j, kk: (0, j)))
        args.append(bias)
    if add is not None:
        in_specs.append(pl.BlockSpec((tm, tn), lambda i, j, kk: (i, j)))
        args.append(add)
    if out_sharded:
        out_shape = _sds((n // tn, m, tn), out_dtype)
        out_spec = pl.BlockSpec((None, tm, tn), lambda i, j, kk: (j, i, 0))
    else:
        out_shape = _sds((m, n), out_dtype)
        out_spec = pl.BlockSpec((tm, tn), lambda i, j, kk: (i, j))

    def body(*refs):
        a_ref, b_ref = refs[0], refs[1]
        pos = 2
        bias_ref = add_ref = None
        if bias is not None:
            bias_ref, pos = refs[pos], pos + 1
        if add is not None:
            add_ref, pos = refs[pos], pos + 1
        o_ref = refs[pos]
        if resident:
            ks = b_ref.shape[2]
            p = None
            for s in range(resident):
                ps = lax.dot_general(a_ref[:, s * ks:(s + 1) * ks], b_ref[s], dn, preferred_element_type=F32)
                p = ps if p is None else p + ps
        else:
            av, bv = a_ref[...], b_ref[...]
            if av.dtype != BF16:
                av = av.astype(BF16)
            if bv.dtype != BF16:
                bv = bv.astype(BF16)
            p = lax.dot_general(av, bv, dn, preferred_element_type=F32)

        def finish(acc):
            if bias_ref is not None:
                acc = acc + bias_ref[...]
            if add_ref is not None:
                acc = acc + add_ref[...]
            o_ref[...] = acc.astype(o_ref.dtype)

        if nk == 1:
            finish(p)
        else:
            acc_ref = refs[pos + 1]
            kk = pl.program_id(2)

            @pl.when(kk == 0)
            def _():
                acc_ref[...] = p

            @pl.when(kk > 0)
            def _():
                acc_ref[...] += p

            @pl.when(kk == nk - 1)
            def _():
                finish(acc_ref[...])

    res = _call(body, grid=(m // tm, n // tn, nk), in_specs=in_specs, out_specs=[out_spec], out_shape=[out_shape],
                scratch_shapes=[pltpu.VMEM((tm, tn), F32)] if nk > 1 else [],
                sem=("parallel", "parallel", "arbitrary"), name=name, args=args, carried=carried)
    return res[0] if carried is None else (res[0], res[1:])


def _rowcall(fn, rows, consts, row_outs, acc_outs, *, name, tm=ROW_TILE, col_grid=1):
    n_rows = rows[0][0].shape[0]
    assert n_rows % tm == 0
    grid = (col_grid, n_rows // tm)
    in_specs = [pl.BlockSpec((tm, w), functools.partial(lambda c, i, cb: (i, cb + c), cb=cb)) for _, w, cb in rows]
    in_specs += [pl.BlockSpec(k.shape, functools.partial(lambda c, i, nd: (0,) * nd, nd=k.ndim)) for k in consts]
    out_specs = [pl.BlockSpec((tm, w), lambda c, i: (i, c)) for _, _, _, w in row_outs]
    out_specs += [pl.BlockSpec((r, w), lambda c, i: (0, c)) for r, _, w in acc_outs]
    out_shape = [_sds((nr, nc), dt) for nr, nc, dt, _ in row_outs] + [_sds((r, nc), F32) for r, nc, _ in acc_outs]
    n_in, n_ro = len(rows) + len(consts), len(row_outs)

    def body(*refs):
        res = fn(*[r[...] for r in refs[:n_in]])
        if not isinstance(res, (tuple, list)):
            res = (res,)
        outs = refs[n_in:]
        for o_ref, val in zip(outs[:n_ro], res[:n_ro]):
            o_ref[...] = val.astype(o_ref.dtype)
        if acc_outs:
            first = pl.program_id(1) == 0

            @pl.when(first)
            def _():
                for o_ref, val in zip(outs[n_ro:], res[n_ro:]):
                    o_ref[...] = val

            @pl.when(jnp.logical_not(first))
            def _():
                for o_ref, val in zip(outs[n_ro:], res[n_ro:]):
                    o_ref[...] += val

    out = pl.pallas_call(
        body, grid=grid, in_specs=in_specs, out_specs=out_specs, out_shape=out_shape,
        compiler_params=_params(("arbitrary", "arbitrary")), name=name,
    )(*[r[0] for r in rows], *consts)
    return out


def _matmul_rows(b, *, form, tm, tk, fn, rows, consts, row_outs, acc_outs, name, a=None, a_rows=None, a_fn=None,
                 carried=None):
    b3 = b.ndim == 3
    resident = 0
    if form == "nn":
        k, n = b.shape
        b_spec = pl.BlockSpec((tk, n), lambda i, kk: (kk, 0))
        dn = (((1,), (0,)), ((), ()))
    else:
        n = b.shape[1] if b3 else b.shape[0]
        k = b.shape[0] * b.shape[2] if b3 else b.shape[1]
        if b3 and tk == k:
            resident = b.shape[0]
            b_spec = pl.BlockSpec(b.shape, lambda i, kk: (0, 0, 0))
        else:
            b_spec = (pl.BlockSpec((None, n, tk), lambda i, kk: (kk, 0, 0)) if b3
                      else pl.BlockSpec((n, tk), lambda i, kk: (0, kk)))
        dn = (((1,), (1,)), ((), ()))
    nk = k // tk
    lhs_in = [(a, tk, 0)] if a is not None else list(a_rows)
    assert a is not None or nk == 1
    m = lhs_in[0][0].shape[0]
    n_lhs = len(lhs_in)
    in_specs = [pl.BlockSpec((tm, tk), lambda i, kk: (i, kk))] if a is not None else [
        pl.BlockSpec((tm, w), functools.partial(lambda i, kk, cb: (i, cb), cb=cb)) for _, w, cb in a_rows]
    in_specs.append(b_spec)
    in_specs += [pl.BlockSpec((tm, w), functools.partial(lambda i, kk, cb: (i, cb), cb=cb)) for _, w, cb in rows]
    in_specs += [pl.BlockSpec(c.shape, functools.partial(lambda i, kk, nd: (0,) * nd, nd=c.ndim)) for c in consts]
    out_specs = [pl.BlockSpec((tm, w), lambda i, kk: (i, 0)) for _, w in row_outs]
    out_specs += [pl.BlockSpec((r, w), lambda i, kk: (0, 0)) for r, w in acc_outs]
    out_shape = [_sds((m, w), dt) for dt, w in row_outs] + [_sds((r, w), F32) for r, w in acc_outs]
    n_rows, n_consts, n_ro, n_acc = len(rows), len(consts), len(row_outs), len(acc_outs)

    def body(*refs):
        pos = n_lhs + 1
        row_refs, const_refs = refs[pos:pos + n_rows], refs[pos + n_rows:pos + n_rows + n_consts]
        pos += n_rows + n_consts
        out_refs, acc_refs = refs[pos:pos + n_ro], refs[pos + n_ro:pos + n_ro + n_acc]
        i, kk = pl.program_id(0), pl.program_id(1)
        if resident:
            b_ref, ks, p = refs[n_lhs], b.shape[2], None
            for s in range(resident):
                ps = lax.dot_general(refs[0][:, s * ks:(s + 1) * ks], b_ref[s], dn, preferred_element_type=F32)
                p = ps if p is None else p + ps
        else:
            lhs = refs[0][...] if a is not None else a_fn(*[r[...] for r in refs[:n_lhs]]).astype(BF16)
            p = lax.dot_general(lhs, refs[n_lhs][...], dn, preferred_element_type=F32)

        def finish(acc):
            extra = [r[...] for r in row_refs] + [c[...] for c in const_refs]
            res = fn(acc, lhs, *extra) if a is None else fn(acc, *extra)
            for o_ref, val in zip(out_refs, res[:n_ro]):
                o_ref[...] = val.astype(o_ref.dtype)
            if n_acc:
                @pl.when(i == 0)
                def _():
                    for o_ref, val in zip(acc_refs, res[n_ro:]):
                        o_ref[...] = val

                @pl.when(i > 0)
                def _():
                    for o_ref, val in zip(acc_refs, res[n_ro:]):
                        o_ref[...] += val

        if nk == 1:
            finish(p)
        else:
            acc_ref = refs[pos + n_ro + n_acc]

            @pl.when(kk == 0)
            def _():
                acc_ref[...] = p

            @pl.when(kk > 0)
            def _():
                acc_ref[...] += p

            @pl.when(kk == nk - 1)
            def _():
                finish(acc_ref[...])

    res = _call(body, grid=(m // tm, nk), in_specs=in_specs, out_specs=out_specs, out_shape=out_shape,
                scratch_shapes=[pltpu.VMEM((tm, n), F32)] if nk > 1 else [], sem=("arbitrary", "arbitrary"),
                name=name, args=[r[0] for r in lhs_in] + [b] + [r[0] for r in rows] + list(consts), carried=carried)
    own = n_ro + n_acc
    return res[:own] if carried is None else (res[:own], res[own:])


def _colsum(v):
    return jnp.sum(v, axis=0, keepdims=True)


def _sigmoid(v):
    return 1.0 / (1.0 + jnp.exp(-v))


_GELU_C = math.sqrt(2.0 / math.pi)


def _gelu(v):
    return 0.5 * v * (1.0 + jnp.tanh(_GELU_C * (v + 0.044715 * (v * v * v))))


def _gelu_and_grad(v):
    th = jnp.tanh(_GELU_C * (v + 0.044715 * (v * v * v)))
    g = 0.5 * v * (1.0 + th)
    dg = 0.5 * (1.0 + th) + 0.5 * v * (1.0 - th * th) * (_GELU_C * (1.0 + 3.0 * 0.044715 * (v * v)))
    return g, dg


def _rms_stats(v):
    r = lax.rsqrt(jnp.mean(v * v, axis=-1, keepdims=True) + EPS)
    return v * r, r


def _rms_bwd(dn, vn, r):
    return r * (dn - vn * jnp.mean(dn * vn, axis=-1, keepdims=True))


def _pre_norm(x, g, sc, sh, name):
    def fn(xv, gv, scv, shv):
        xn, _ = _rms_stats(xv)
        return (xn * gv) * (1.0 + scv) + shv
    return _rowcall(fn, [(x, D, 0)], [g, sc, sh], [(x.shape[0], D, BF16, D)], [], name=name)[0]


def _pre_norm_bwd(dh, x, dx_other, g, sc, name):
    def fn(dhv, xv, dov, gv, scv):
        xn, r = _rms_stats(xv)
        yn = xn * gv
        dyn = dhv * (1.0 + scv)
        dx = _rms_bwd(dyn * gv, xn, r)
        return dov + dx, _colsum(dhv), _colsum(dhv * yn), _colsum(dyn * xn)
    t = x.shape[0]
    return _rowcall(fn, [(dh, D, 0), (x, D, 0), (dx_other, D, 0)], [g, sc], [(t, D, F32, D)],
                    [(1, D, D)] * 3, name=name)


def _post_res(x, ypre, g, gt, name):
    def fn(xv, yv, gv, gtv):
        yn, _ = _rms_stats(yv)
        return xv + gtv * (yn * gv)
    return _rowcall(fn, [(x, D, 0), (ypre, D, 0)], [g, gt], [(x.shape[0], D, F32, D)], [], name=name)[0]


def _post_res_bwd(dxo, ypre, g, gt, name):
    def fn(dv, yv, gv, gtv):
        yn, r = _rms_stats(yv)
        dyn = dv * gtv
        dy = _rms_bwd(dyn * gv, yn, r)
        return dy, _colsum(dyn * yn), _colsum(dv * (yn * gv))
    t = ypre.shape[0]
    return _rowcall(fn, [(dxo, D, 0), (ypre, D, 0)], [g, gt], [(t, D, BF16, D)], [(1, D, D)] * 2, name=name)


def _ffn_tail(x1, yf, target, g, gt, name):
    def fn(xv, yv, tv, gv, gtv):
        yn, r = _rms_stats(yv)
        e = xv + gtv * (yn * gv) - tv
        dx2 = e * (1.0 / D)
        dyn = dx2 * gtv
        dy = _rms_bwd(dyn * gv, yn, r)
        return dx2, dy, _colsum(e * e) * (0.5 / D), _colsum(dyn * yn), _colsum(dx2 * (yn * gv))
    t = x1.shape[0]
    return _rowcall(fn, [(x1, D, 0), (yf, D, 0), (target, D, 0)], [g, gt], [(t, D, F32, D), (t, D, BF16, D)],
                    [(1, D, D)] * 3, name=name)


def _gate_merge(a, cb, z, name):
    def fn(av, cv, gav, gbv):
        return _sigmoid(gav) * av + _sigmoid(gbv) * cv
    t = a.shape[0]
    return _rowcall(fn, [(a, 512, 0), (cb, 512, 0), (z, 512, 5), (z, 512, 7)], [],
                    [(t, D, BF16, 512)], [], name=name, col_grid=2)[0]


def _gate_merge_bwd(dy, a, cb, z, name):
    def fn(dv, av, cv, gav, gbv):
        sa, sb = _sigmoid(gav), _sigmoid(gbv)
        dcb = dv * sb
        dga = dv * av * (sa * (1.0 - sa))
        dgb = dv * cv * (sb * (1.0 - sb))
        return dv * sa, dcb, dga, dgb, _colsum(dcb), _colsum(dga), _colsum(dgb)
    t = a.shape[0]
    return _rowcall(fn, [(dy, 512, 0), (a, 512, 0), (cb, 512, 0), (z, 512, 5), (z, 512, 7)], [],
                    [(t, D, BF16, 512)] * 4, [(1, D, 512)] * 3, name=name, col_grid=2)


CONV_HALO = 32


def _layer_norm_parts(u):
    mu = jnp.mean(u, axis=-1, keepdims=True)
    d = u - mu
    r = lax.rsqrt(jnp.mean(d * d, axis=-1, keepdims=True) + EPS)
    return d * r, r


LANES = 128
SUBLANE_ROWS = 8
CONV_ROWS = 64


def _lanes(c):
    return slice(c * LANES, (c + 1) * LANES)


def _conv_branch(z, w_dw, b_dw, g_ln, b_ln, name, tm=ROW_TILE):
    t = z.shape[0]
    per = tm // CONV_HALO
    n_chunks = 512 // LANES

    def body(ga_ref, gb_ref, gah_ref, gbh_ref, w_ref, b_ref, g_ref, bl_ref, u1_ref, u3_ref, scr):
        i = pl.program_id(0)
        u0h = jnp.where(i > 0, gah_ref[...] * _sigmoid(gbh_ref[...]), 0.0)
        u0 = ga_ref[...] * _sigmoid(gb_ref[...])
        for c in range(n_chunks):
            scr[c, 0:CONV_HALO, :] = u0h[:, _lanes(c)]
            scr[c, CONV_HALO:CONV_HALO + tm, :] = u0[:, _lanes(c)]
        for c in range(n_chunks):
            for r0 in range(0, tm, CONV_ROWS):
                acc = jnp.zeros((CONV_ROWS, LANES), F32) + b_ref[:, _lanes(c)]
                for j in range(CONV_K):
                    acc = acc + w_ref[j:j + 1, _lanes(c)] * scr[c, pl.ds(r0 + CONV_HALO - (CONV_K - 1) + j, CONV_ROWS), :]
                u1_ref[r0:r0 + CONV_ROWS, _lanes(c)] = acc
        xh, _ = _layer_norm_parts(u1_ref[...])
        u2 = xh * g_ref[...] + bl_ref[...]
        u3_ref[...] = (u2 * _sigmoid(u2)).astype(BF16)

    cur = lambda cb: pl.BlockSpec((tm, 512), lambda i: (i, cb))
    halo = lambda cb: pl.BlockSpec((CONV_HALO, 512), lambda i: (jnp.maximum(i * per - 1, 0), cb))
    whole = lambda a: pl.BlockSpec(a.shape, lambda i: (0, 0))
    return pl.pallas_call(
        body, grid=(t // tm,),
        in_specs=[cur(3), cur(4), halo(3), halo(4), whole(w_dw), whole(b_dw), whole(g_ln), whole(b_ln)],
        out_specs=[pl.BlockSpec((tm, 512), lambda i: (i, 0))] * 2,
        out_shape=[_sds((t, 512), F32), _sds((t, 512), BF16)],
        scratch_shapes=[pltpu.VMEM((n_chunks, CONV_HALO + tm, LANES), F32)],
        compiler_params=_params(("arbitrary",)), name=name,
    )(z, z, z, z, w_dw, b_dw, g_ln, b_ln)


def _conv_branch_bwd(du3, u1, z, w_dw, g_ln, b_ln, name, tm=ROW_TILE, carried=None):
    t = z.shape[0]
    per = tm // CONV_HALO
    last = t // tm - 1
    n_chunks = 512 // LANES

    def du1_of(du3v, u1v, g, b):
        xh, r = _layer_norm_parts(u1v)
        u2 = xh * g + b
        s = _sigmoid(u2)
        du2 = du3v * (s * (1.0 + u2 * (1.0 - s)))
        dxh = du2 * g
        du1 = r * (dxh - jnp.mean(dxh, axis=-1, keepdims=True) - xh * jnp.mean(dxh * xh, axis=-1, keepdims=True))
        return du1, du2, xh

    def body(d_ref, u_ref, dn_ref, un_ref, ga_ref, gb_ref, gah_ref, gbh_ref, w_ref, g_ref, bl_ref,
             dglu_ref, dw_ref, dbdw_ref, dg_ref, dbl_ref, dbin_ref, scr, scd):
        i = pl.program_id(0)
        g, b = g_ref[...], bl_ref[...]
        du1, du2, xh = du1_of(d_ref[...], u_ref[...], g, b)
        du1n, _, _ = du1_of(dn_ref[...], un_ref[...], g, b)
        du1n = jnp.where(i < last, du1n, 0.0)
        sgb = _sigmoid(gb_ref[...])
        ga = ga_ref[...]
        u0 = ga * sgb
        u0h = jnp.where(i > 0, gah_ref[...] * _sigmoid(gbh_ref[...]), 0.0)
        for c in range(n_chunks):
            scd[c, 0:tm, :] = du1[:, _lanes(c)]
            scd[c, tm:tm + CONV_HALO, :] = du1n[:, _lanes(c)]
            scr[c, 0:CONV_HALO, :] = u0h[:, _lanes(c)]
            scr[c, CONV_HALO:CONV_HALO + tm, :] = u0[:, _lanes(c)]

        @pl.when(i == 0)
        def _():
            for ref in (dw_ref, dbdw_ref, dg_ref, dbl_ref, dbin_ref):
                ref[...] = jnp.zeros_like(ref)

        dsg = ga * (sgb * (1.0 - sgb))
        for c in range(n_chunks):
            gate = slice(512 + c * LANES, 512 + (c + 1) * LANES)
            for r0 in range(0, tm, CONV_ROWS):
                rows = slice(r0, r0 + CONV_ROWS)
                du0 = jnp.zeros((CONV_ROWS, LANES), F32)
                for j in range(CONV_K):
                    du0 = du0 + w_ref[j:j + 1, _lanes(c)] * scd[c, pl.ds(r0 + CONV_K - 1 - j, CONV_ROWS), :]
                dga = du0 * sgb[rows, _lanes(c)]
                dgb = du0 * dsg[rows, _lanes(c)]
                dglu_ref[rows, _lanes(c)] = dga.astype(BF16)
                dglu_ref[rows, gate] = dgb.astype(BF16)
                dbin_ref[:, _lanes(c)] += _colsum(dga)
                dbin_ref[:, gate] += _colsum(dgb)
            for j in range(CONV_K):
                dwj = jnp.zeros((SUBLANE_ROWS, LANES), F32)
                for r0 in range(0, tm, CONV_ROWS):
                    prod = (scd[c, pl.ds(r0, CONV_ROWS), :]
                            * scr[c, pl.ds(r0 + CONV_HALO - (CONV_K - 1) + j, CONV_ROWS), :])
                    dwj = dwj + jnp.sum(prod.reshape(CONV_ROWS // SUBLANE_ROWS, SUBLANE_ROWS, LANES), axis=0)
                dw_ref[j:j + 1, _lanes(c)] += _colsum(dwj)
        dbdw_ref[...] += _colsum(du1)
        dg_ref[...] += _colsum(du2 * xh)
        dbl_ref[...] += _colsum(du2)

    cur = lambda cb: pl.BlockSpec((tm, 512), lambda i: (i, cb))
    prev = lambda cb: pl.BlockSpec((CONV_HALO, 512), lambda i: (jnp.maximum(i * per - 1, 0), cb))
    nxt = pl.BlockSpec((CONV_HALO, 512), lambda i: (jnp.minimum((i + 1) * per, t // CONV_HALO - 1), 0))
    whole = lambda a: pl.BlockSpec(a.shape, lambda i: (0, 0))
    acc = lambda r, w: pl.BlockSpec((r, w), lambda i: (0, 0))
    res = _call(
        body, grid=(t // tm,),
        in_specs=[cur(0), cur(0), nxt, nxt, cur(3), cur(4), prev(3), prev(4), whole(w_dw), whole(g_ln), whole(b_ln)],
        out_specs=[pl.BlockSpec((tm, 1024), lambda i: (i, 0)), acc(CONV_K, 512), acc(1, 512), acc(1, 512),
                   acc(1, 512), acc(1, 1024)],
        out_shape=[_sds((t, 1024), BF16), _sds((CONV_K, 512), F32), _sds((1, 512), F32), _sds((1, 512), F32),
                   _sds((1, 512), F32), _sds((1, 1024), F32)],
        scratch_shapes=[pltpu.VMEM((n_chunks, CONV_HALO + tm, LANES), F32),
                        pltpu.VMEM((n_chunks, tm + CONV_HALO, LANES), F32)],
        sem=("arbitrary",), name=name, args=(du3, u1, du3, u1, z, z, z, z, w_dw, g_ln, b_ln), carried=carried)
    return res[:6] if carried is None else (res[:6], res[6:])


FF_BLOCK = D_FF // 2
FF_HALO = 8
FF_CHUNKS = FF_BLOCK // LANES


def _ffn_conv(w_ref, b_ref, scr, k, rows):
    acc = b_ref[:, _lanes(k)] + w_ref[0:1, _lanes(k)] * scr[k, pl.ds(FF_HALO - 2, rows), :]
    acc = acc + w_ref[1:2, _lanes(k)] * scr[k, pl.ds(FF_HALO - 1, rows), :]
    return acc + w_ref[2:3, _lanes(k)] * scr[k, pl.ds(FF_HALO, rows), :]


def _ffn_act(up, w3, b3, name, tm=ROW_TILE):
    t = up.shape[0]
    per = tm // FF_HALO
    wide = 2 * FF_BLOCK

    def body(u_ref, uh_ref, w_ref, b_ref, o_ref, scr):
        i = pl.program_id(1)
        for k in range(2 * FF_CHUNKS):
            scr[k, 0:FF_HALO, :] = jnp.where(i > 0, uh_ref[:, _lanes(k)], 0.0)
            scr[k, FF_HALO:FF_HALO + tm, :] = u_ref[:, _lanes(k)]
        for cc in range(FF_CHUNKS):
            val = _ffn_conv(w_ref, b_ref, scr, cc, tm)
            gate = _ffn_conv(w_ref, b_ref, scr, FF_CHUNKS + cc, tm)
            o_ref[:, _lanes(cc)] = (_gelu(gate) * val).astype(BF16)

    return pl.pallas_call(
        body, grid=(2, t // tm),
        in_specs=[pl.BlockSpec((tm, wide), lambda c, i: (i, c)),
                  pl.BlockSpec((FF_HALO, wide), lambda c, i: (jnp.maximum(i * per - 1, 0), c)),
                  pl.BlockSpec((FFN_K, wide), lambda c, i: (0, c)),
                  pl.BlockSpec((1, wide), lambda c, i: (0, c))],
        out_specs=pl.BlockSpec((tm, FF_BLOCK), lambda c, i: (i, c)),
        out_shape=_sds((t, D_FF), BF16),
        scratch_shapes=[pltpu.VMEM((2 * FF_CHUNKS, FF_HALO + tm, LANES), F32)],
        compiler_params=_params(("arbitrary", "arbitrary")), name=name,
    )(up, up, w3, b3)


def _ffn_act_bwd(dact, up, w3, b3, name, tm=ROW_TILE):
    t = up.shape[0]
    per = tm // FF_HALO
    wide = 2 * FF_BLOCK
    last = t // tm - 1
    ext = tm + FF_HALO

    def body(u_ref, up_ref, un_ref, d_ref, dn_ref, w_ref, b_ref, o_ref, dw_ref, db_ref, scr, scd):
        i = pl.program_id(1)
        for k in range(2 * FF_CHUNKS):
            scr[k, 0:FF_HALO, :] = jnp.where(i > 0, up_ref[:, _lanes(k)], 0.0)
            scr[k, FF_HALO:FF_HALO + tm, :] = u_ref[:, _lanes(k)]
            scr[k, FF_HALO + tm:FF_HALO + ext, :] = un_ref[:, _lanes(k)]
        dn = jnp.where(i < last, dn_ref[...], 0.0)

        @pl.when(i == 0)
        def _():
            dw_ref[...] = jnp.zeros_like(dw_ref)
            db_ref[...] = jnp.zeros_like(db_ref)

        for cc in range(FF_CHUNKS):
            val = _ffn_conv(w_ref, b_ref, scr, cc, ext)
            gel, dgel = _gelu_and_grad(_ffn_conv(w_ref, b_ref, scr, FF_CHUNKS + cc, ext))
            da = jnp.concatenate([d_ref[:, _lanes(cc)], dn[:, _lanes(cc)]], axis=0)
            scd[cc] = da * gel
            scd[FF_CHUNKS + cc] = da * val * dgel
            for k in (cc, FF_CHUNKS + cc):
                shifted = [scd[k, pl.ds(FFN_K - 1 - j, tm), :] for j in range(FFN_K)]
                ucur = scr[k, pl.ds(FF_HALO, tm), :]
                o_ref[:, _lanes(k)] = (w_ref[0:1, _lanes(k)] * shifted[0] + w_ref[1:2, _lanes(k)] * shifted[1]
                                       + w_ref[2:3, _lanes(k)] * shifted[2]).astype(BF16)
                for j in range(FFN_K):
                    dw_ref[j:j + 1, _lanes(k)] += _colsum(shifted[j] * ucur)
                db_ref[:, _lanes(k)] += _colsum(shifted[FFN_K - 1])

    nblk = t // FF_HALO
    return pl.pallas_call(
        body, grid=(2, t // tm),
        in_specs=[pl.BlockSpec((tm, wide), lambda c, i: (i, c)),
                  pl.BlockSpec((FF_HALO, wide), lambda c, i: (jnp.maximum(i * per - 1, 0), c)),
                  pl.BlockSpec((FF_HALO, wide), lambda c, i: (jnp.minimum((i + 1) * per, nblk - 1), c)),
                  pl.BlockSpec((tm, FF_BLOCK), lambda c, i: (i, c)),
                  pl.BlockSpec((FF_HALO, FF_BLOCK), lambda c, i: (jnp.minimum((i + 1) * per, nblk - 1), c)),
                  pl.BlockSpec((FFN_K, wide), lambda c, i: (0, c)),
                  pl.BlockSpec((1, wide), lambda c, i: (0, c))],
        out_specs=[pl.BlockSpec((tm, wide), lambda c, i: (i, c)),
                   pl.BlockSpec((FFN_K, wide), lambda c, i: (0, c)),
                   pl.BlockSpec((1, wide), lambda c, i: (0, c))],
        out_shape=[_sds((t, 2 * D_FF), BF16), _sds((FFN_K, 2 * D_FF), F32), _sds((1, 2 * D_FF), F32)],
        scratch_shapes=[pltpu.VMEM((2 * FF_CHUNKS, FF_HALO + ext, LANES), F32),
                        pltpu.VMEM((2 * FF_CHUNKS, ext, LANES), F32)],
        compiler_params=_params(("arbitrary", "arbitrary")), name=name,
    )(up, up, up, dact, dact, w3, b3)


def _toeplitz_map():
    f = np.zeros((TOEP, REL_PAD), np.float32)
    for m in range(TOEP - 1):
        rel = (WINDOW - 1) - m
        f[m, int(np.clip(rel, -MAX_REL, MAX_REL)) + MAX_REL] = 1.0
    return f


def _split3(v):
    hi = v.astype(BF16)
    r1 = v - hi.astype(F32)
    mid = r1.astype(BF16)
    lo = (r1 - mid.astype(F32)).astype(BF16)
    return hi, mid, lo


def _exact_select(v, sel):
    out = None
    for part in _split3(v):
        p = jnp.dot(part, sel, preferred_element_type=F32)
        out = p if out is None else out + p
    return out


def _select_call(v, sel, name):
    def body(v_ref, s_ref, o_ref):
        o_ref[...] = _exact_select(v_ref[...], s_ref[...])
    return pl.pallas_call(body, out_shape=_sds((v.shape[0], sel.shape[1]), F32), name=name)(v, sel)


def _band_bias(gen_row):
    b0 = jnp.broadcast_to(gen_row, (Q_TILE, TOEP))
    bias = pltpu.roll(b0, TOEP - 255, 1, stride=1, stride_axis=0)[:, :WINDOW]
    qq = lax.broadcasted_iota(jnp.int32, (Q_TILE, WINDOW), 0) // CHUNK
    kc = lax.broadcasted_iota(jnp.int32, (Q_TILE, WINDOW), 1) // CHUNK
    return jnp.where((kc >= qq) & (kc <= qq + LEFT_CHUNKS), bias, NEG_INF)


PAD_ROWS = WINDOW - Q_TILE
NT_DIMS = (((1,), (1,)), ((), ()))
TN_DIMS = (((0,), (0,)), ((), ()))


def _head_mask(hh):
    lane = lax.broadcasted_iota(jnp.int32, (1, 128), 1)
    return (lane < 64) if hh == 0 else (lane >= 64)


SOFTMAX_ROWS = 16


def _probs_block(s_scr, bias, hh, rows, i):
    s = s_scr[rows, :] + bias[hh, rows, :]
    col = lax.broadcasted_iota(jnp.int32, (SOFTMAX_ROWS, WINDOW), 1)
    s = jnp.where(col >= PAD_ROWS - Q_TILE * i, s, NEG_INF)
    p = jnp.exp(s - jnp.max(s, axis=-1, keepdims=True))
    return p / jnp.sum(p, axis=-1, keepdims=True)


def _attention(z, gen, name, carried=None):
    t = z.shape[0]
    n_i = t // Q_TILE

    def body(q_ref, k_ref, v_ref, g_ref, o_ref, kpad, vpad, bias, s_scr, p_scr):
        hp, i = pl.program_id(0), pl.program_id(1)

        @pl.when(i == 0)
        def _():
            kpad[0:PAD_ROWS, :] = jnp.zeros((PAD_ROWS, 128), BF16)
            vpad[0:PAD_ROWS, :] = jnp.zeros((PAD_ROWS, 128), BF16)
            kpad[PAD_ROWS:PAD_ROWS + t, :] = k_ref[...].astype(BF16)
            vpad[PAD_ROWS:PAD_ROWS + t, :] = v_ref[...].astype(BF16)
            for hh in range(2):
                bias[hh] = _band_bias(g_ref[pl.ds(2 * hp + hh, 1), :])

        start = pl.multiple_of(i * Q_TILE, Q_TILE)
        kw = kpad[pl.ds(start, WINDOW), :]
        vw = vpad[pl.ds(start, WINDOW), :]
        q = q_ref[...] * (CHUNK ** -0.5)
        out = None
        for hh in range(2):
            mask = _head_mask(hh)
            qm = jnp.where(mask, q, 0.0).astype(BF16)
            s_scr[hh] = lax.dot_general(qm, kw, NT_DIMS, preferred_element_type=F32)
            for r0 in range(0, Q_TILE, SOFTMAX_ROWS):
                rows = slice(r0, r0 + SOFTMAX_ROWS)
                p_scr[hh, rows, :] = _probs_block(s_scr.at[hh], bias, hh, rows, i).astype(BF16)
            o = jnp.dot(p_scr[hh], vw, preferred_element_type=F32)
            out = jnp.where(mask, o, 0.0) if out is None else jnp.where(mask, o, out)
        o_ref[...] = out.astype(BF16)

    res = _call(
        body, grid=(4, n_i),
        in_specs=[pl.BlockSpec((Q_TILE, 128), lambda h, i: (i, h)),
                  pl.BlockSpec((t, 128), lambda h, i: (0, 4 + h)),
                  pl.BlockSpec((t, 128), lambda h, i: (0, 8 + h)),
                  pl.BlockSpec((N_HEADS, TOEP), lambda h, i: (0, 0))],
        out_specs=[pl.BlockSpec((Q_TILE, 128), lambda h, i: (i, h))],
        out_shape=[_sds((t, 512), BF16)],
        scratch_shapes=[pltpu.VMEM((PAD_ROWS + t, 128), BF16), pltpu.VMEM((PAD_ROWS + t, 128), BF16),
                        pltpu.VMEM((2, Q_TILE, WINDOW), F32), pltpu.VMEM((2, Q_TILE, WINDOW), F32),
                        pltpu.VMEM((2, Q_TILE, WINDOW), BF16)],
        sem=("arbitrary", "arbitrary"), name=name, args=(z, z, z, gen), carried=carried)
    return res[0] if carried is None else (res[0], res[1:])


def _attention_bwd(z, datt, gen, name, carried=None):
    t = z.shape[0]
    n_i = t // Q_TILE

    def body(q_ref, k_ref, v_ref, d_ref, g_ref, dq_ref, dk_ref, dv_ref, sq_ref, sk_ref, sv_ref, dg_ref,
             kpad, vpad, dkacc, dvacc, bias, dsacc, s_scr, dp_scr, p_scr, ds_scr):
        hp, i = pl.program_id(0), pl.program_id(1)

        @pl.when(i == 0)
        def _():
            kpad[0:PAD_ROWS, :] = jnp.zeros((PAD_ROWS, 128), BF16)
            vpad[0:PAD_ROWS, :] = jnp.zeros((PAD_ROWS, 128), BF16)
            kpad[PAD_ROWS:PAD_ROWS + t, :] = k_ref[...].astype(BF16)
            vpad[PAD_ROWS:PAD_ROWS + t, :] = v_ref[...].astype(BF16)
            dkacc[...] = jnp.zeros_like(dkacc)
            dvacc[...] = jnp.zeros_like(dvacc)
            dsacc[...] = jnp.zeros_like(dsacc)
            for hh in range(2):
                bias[hh] = _band_bias(g_ref[pl.ds(2 * hp + hh, 1), :])

        start = pl.multiple_of(i * Q_TILE, Q_TILE)
        win = pl.ds(start, WINDOW)
        kw = kpad[win, :]
        vw = vpad[win, :]
        q = q_ref[...] * (CHUNK ** -0.5)
        do = d_ref[...]
        dq = None
        for hh in range(2):
            mask = _head_mask(hh)
            qm = jnp.where(mask, q, 0.0).astype(BF16)
            dom = jnp.where(mask, do, 0.0).astype(BF16)
            s_scr[...] = lax.dot_general(qm, kw, NT_DIMS, preferred_element_type=F32)
            dp_scr[...] = lax.dot_general(dom, vw, NT_DIMS, preferred_element_type=F32)
            for r0 in range(0, Q_TILE, SOFTMAX_ROWS):
                rows = slice(r0, r0 + SOFTMAX_ROWS)
                p = _probs_block(s_scr, bias, hh, rows, i)
                dp = dp_scr[rows, :]
                ds = p * (dp - jnp.sum(p * dp, axis=-1, keepdims=True))
                dsacc[hh, rows, :] += ds
                ds_scr[rows, :] = ds.astype(BF16)
                p_scr[rows, :] = p.astype(BF16)
            ds16 = ds_scr[...]
            dqh = jnp.dot(ds16, kw, preferred_element_type=F32) * (CHUNK ** -0.5)
            dq = jnp.where(mask, dqh, 0.0) if dq is None else jnp.where(mask, dqh, dq)
            dkacc[win, :] += lax.dot_general(ds16, qm, TN_DIMS, preferred_element_type=F32)
            dvacc[win, :] += lax.dot_general(p_scr[...], dom, TN_DIMS, preferred_element_type=F32)
        dq_ref[...] = dq.astype(BF16)

        @pl.when(i == 0)
        def _():
            sq_ref[...] = _colsum(dq)

        @pl.when(i > 0)
        def _():
            sq_ref[...] += _colsum(dq)

        @pl.when(i == n_i - 1)
        def _():
            dk = dkacc[PAD_ROWS:PAD_ROWS + t, :]
            dv = dvacc[PAD_ROWS:PAD_ROWS + t, :]
            dk_ref[...] = dk.astype(BF16)
            dv_ref[...] = dv.astype(BF16)
            sk_ref[...] = _colsum(dk)
            sv_ref[...] = _colsum(dv)
            rr = lax.broadcasted_iota(jnp.int32, (Q_TILE, Q_TILE), 0)
            cc = lax.broadcasted_iota(jnp.int32, (Q_TILE, Q_TILE), 1)
            rev = jnp.where(rr + cc == Q_TILE - 1, 1.0, 0.0).astype(BF16)
            for hh in range(2):
                acc = None
                for part in _split3(dsacc[hh]):
                    pr = jnp.dot(rev, part, preferred_element_type=F32)
                    acc = pr if acc is None else acc + pr
                wide = jnp.concatenate([acc, jnp.zeros((Q_TILE, TOEP - WINDOW), F32)], axis=1)
                dg_ref[pl.ds(2 * hp + hh, 1), :] = _colsum(pltpu.roll(wide, 0, 1, stride=1, stride_axis=0))

    col = lambda off: pl.BlockSpec((t, 128), lambda h, i: (0, off + h))
    tile = lambda: pl.BlockSpec((Q_TILE, 128), lambda h, i: (i, h))
    sums = lambda: pl.BlockSpec((1, 128), lambda h, i: (0, h))
    res = _call(
        body, grid=(4, n_i),
        in_specs=[tile(), col(4), col(8), tile(), pl.BlockSpec((N_HEADS, TOEP), lambda h, i: (0, 0))],
        out_specs=[tile(), col(0), col(0), sums(), sums(), sums(), pl.BlockSpec((N_HEADS, TOEP), lambda h, i: (0, 0))],
        out_shape=[_sds((t, 512), BF16)] * 3 + [_sds((1, 512), F32)] * 3 + [_sds((N_HEADS, TOEP), F32)],
        scratch_shapes=[pltpu.VMEM((PAD_ROWS + t, 128), BF16), pltpu.VMEM((PAD_ROWS + t, 128), BF16),
                        pltpu.VMEM((PAD_ROWS + t, 128), F32), pltpu.VMEM((PAD_ROWS + t, 128), F32),
                        pltpu.VMEM((2, Q_TILE, WINDOW), F32), pltpu.VMEM((2, Q_TILE, WINDOW), F32),
                        pltpu.VMEM((Q_TILE, WINDOW), F32), pltpu.VMEM((Q_TILE, WINDOW), F32),
                        pltpu.VMEM((Q_TILE, WINDOW), BF16), pltpu.VMEM((Q_TILE, WINDOW), BF16)],
        sem=("arbitrary", "arbitrary"), name=name, args=(z, z, z, datt, gen), carried=carried)
    return res[:7] if carried is None else (res[:7], res[7:])


def _adamw_math(w, g, m, v):
    m = ADAM_B1 * m + (1.0 - ADAM_B1) * g
    v = ADAM_B2 * v + (1.0 - ADAM_B2) * (g * g)
    m_hat = m / (1.0 - ADAM_B1 ** ADAM_STEP)
    v_hat = v / (1.0 - ADAM_B2 ** ADAM_STEP)
    delta = -ADAM_LR * (m_hat / (jnp.sqrt(v_hat) + ADAM_EPS) + ADAM_WD * w)
    return delta, m, v


def _adamw_many(items, name):
    n = len(items)

    def body(*refs):
        ins, outs = refs[:4 * n], refs[4 * n:]
        for k in range(n):
            w, g, m, v = (r[...] for r in ins[4 * k:4 * k + 4])
            outs[3 * k][...], outs[3 * k + 1][...], outs[3 * k + 2][...] = _adamw_math(w, g, m, v)

    flat = [a for item in items for a in item]
    res = pl.pallas_call(body, out_shape=[_sds(item[0].shape, F32) for item in items for _ in range(3)],
                         name=name)(*flat)
    return [tuple(res[3 * k:3 * k + 3]) for k in range(n)]


def _adamw(w, g, m, v, name):
    r, c = w.shape
    tm = next(cand for cand in (256, 176, 128, 64, 32, 16, 8) if r % cand == 0)
    return _rowcall(lambda wv, gv, mv, vv: (gv,) + _adamw_math(wv, gv, mv, vv),
                    [(w, c, 0), (g, c, 0), (m, c, 0), (v, c, 0)], [], [(r, c, F32, c)] * 4, [], name=name, tm=tm)


def _ada_fwd(c_all, w_shard, b_shard, name):
    n = w_shard.shape[1]
    tn = 512

    def body(c_ref, w_ref, b_ref, o_ref, a_ref):
        cv = c_ref[...]
        act = cv * _sigmoid(cv)
        a_ref[...] = act
        o_ref[...] = jnp.dot(act.astype(BF16), w_ref[...].astype(BF16), preferred_element_type=F32) + b_ref[...]

    return pl.pallas_call(
        body, grid=(n // tn,),
        in_specs=[pl.BlockSpec((8, D), lambda j: (0, 0)), pl.BlockSpec((D, tn), lambda j: (0, j)),
                  pl.BlockSpec((1, tn), lambda j: (0, j))],
        out_specs=[pl.BlockSpec((8, tn), lambda j: (0, j)), pl.BlockSpec((8, D), lambda j: (0, 0))],
        out_shape=[_sds((8, n), F32), _sds((8, D), F32)],
        compiler_params=_params(("arbitrary",)), name=name,
    )(c_all, w_shard, b_shard)


def _ada_bwd_adamw(act_t, dmod_shard, w, m, v, name):
    r, c = w.shape
    tm = 256

    def body(a_ref, d_ref, w_ref, m_ref, v_ref, g_ref, dl_ref, nm_ref, nv_ref):
        g = jnp.dot(a_ref[...], d_ref[...], precision=lax.Precision.HIGHEST, preferred_element_type=F32)
        g_ref[...] = g
        dl_ref[...], nm_ref[...], nv_ref[...] = _adamw_math(w_ref[...], g, m_ref[...], v_ref[...])

    blk = pl.BlockSpec((tm, c), lambda i: (i, 0))
    return pl.pallas_call(
        body, grid=(r // tm,),
        in_specs=[pl.BlockSpec((tm, 8), lambda i: (i, 0)), pl.BlockSpec((8, c), lambda i: (0, 0)), blk, blk, blk],
        out_specs=[blk] * 4, out_shape=[_sds((r, c), F32)] * 4,
        compiler_params=_params(("arbitrary",)), name=name,
    )(act_t, dmod_shard, w, m, v)


def _place():
    return lax.axis_index("x"), lax.axis_index("y"), lax.axis_index("c")


def _flip(v, bit):
    return 1 - v if bit else v


VMEM_SPEC = pl.BlockSpec(memory_space=pltpu.VMEM)


def _allgather8(v, name):
    r, c = v.shape

    def body(v_ref, g_ref, tot_ref, send_sems, recv_sems, local_sem):
        x, y, cc = _place()
        me = 4 * x + 2 * y + cc
        mine = pltpu.make_async_copy(v_ref, g_ref.at[me], local_sem)
        mine.start()
        sends = []
        for k in range(1, 8):
            peer = (_flip(x, k & 4), _flip(y, k & 2), _flip(cc, k & 1))
            cp = pltpu.make_async_remote_copy(src_ref=v_ref, dst_ref=g_ref.at[me], send_sem=send_sems.at[k - 1],
                                              recv_sem=recv_sems.at[k - 1], device_id=peer, device_id_type=MESH)
            cp.start()
            sends.append(cp)
        for k in range(1, 8):
            peer = (_flip(x, k & 4), _flip(y, k & 2), _flip(cc, k & 1))
            theirs = g_ref.at[4 * peer[0] + 2 * peer[1] + peer[2]]
            pltpu.make_async_remote_copy(src_ref=v_ref, dst_ref=theirs, send_sem=send_sems.at[k - 1],
                                         recv_sem=recv_sems.at[k - 1], device_id=peer, device_id_type=MESH).wait_recv()
        for cp in sends:
            cp.wait_send()
        mine.wait()
        tot = g_ref[0]
        for d in range(1, 8):
            tot = tot + g_ref[d]
        tot_ref[...] = tot

    return pl.pallas_call(
        body, in_specs=[VMEM_SPEC], out_specs=[VMEM_SPEC, VMEM_SPEC],
        out_shape=[_sds((8, r, c), F32), _sds((r, c), F32)],
        scratch_shapes=[pltpu.SemaphoreType.DMA((7,)), pltpu.SemaphoreType.DMA((7,)), pltpu.SemaphoreType.DMA],
        compiler_params=pltpu.CompilerParams(vmem_limit_bytes=VMEM_LIMIT), name=name,
    )(v)


def _slot(px, py, swapped):
    return 2 * py + px if swapped else 2 * px + py


def _gather_shards(arrs, swapped, name, in_place=False):
    n = len(arrs)

    def body(*refs):
        ins, outs = refs[:n], refs[n:2 * n]
        send1, recv1, send2, recv2, local_sems = refs[2 * n:]
        x, y, c = _place()
        sibling = (x, y, 1 - c)
        chips = [(_flip(x, k & 2), _flip(y, k & 1)) for k in (1, 2, 3)]
        local_copies, sends = [], []
        for a in range(n):
            h = outs[a].shape[1] // 2
            mine = pl.ds(pl.multiple_of(c * h, 8), h)
            own = _slot(x, y, swapped[a])
            if in_place:
                src = outs[a].at[own, mine]
            else:
                src = ins[a].at[mine]
                lc = pltpu.make_async_copy(ins[a], outs[a].at[own], local_sems.at[a])
                lc.start()
                local_copies.append(lc)
            for j, (px, py) in enumerate(chips):
                cp = pltpu.make_async_remote_copy(
                    src_ref=src, dst_ref=outs[a].at[own, mine], send_sem=send1.at[3 * a + j],
                    recv_sem=recv1.at[3 * a + j], device_id=(px, py, c), device_id_type=MESH)
                cp.start()
                sends.append(cp)
        for a in range(n):
            h = outs[a].shape[1] // 2
            mine = pl.ds(pl.multiple_of(c * h, 8), h)
            for j, (px, py) in enumerate(chips):
                piece = outs[a].at[_slot(px, py, swapped[a]), mine]
                pltpu.make_async_remote_copy(
                    src_ref=piece, dst_ref=piece, send_sem=send1.at[3 * a + j], recv_sem=recv1.at[3 * a + j],
                    device_id=(px, py, c), device_id_type=MESH).wait_recv()
                fwd = pltpu.make_async_remote_copy(
                    src_ref=piece, dst_ref=piece, send_sem=send2.at[3 * a + j], recv_sem=recv2.at[3 * a + j],
                    device_id=sibling, device_id_type=MESH)
                fwd.start()
                sends.append(fwd)
        for a in range(n):
            h = outs[a].shape[1] // 2
            other = pl.ds(pl.multiple_of((1 - c) * h, 8), h)
            for j, (px, py) in enumerate(chips):
                piece = outs[a].at[_slot(px, py, swapped[a]), other]
                pltpu.make_async_remote_copy(
                    src_ref=piece, dst_ref=piece, send_sem=send2.at[3 * a + j], recv_sem=recv2.at[3 * a + j],
                    device_id=sibling, device_id_type=MESH).wait_recv()
        for cp in sends:
            cp.wait_send()
        for lc in local_copies:
            lc.wait()

    dma = lambda k: pltpu.SemaphoreType.DMA((k,))
    return pl.pallas_call(
        body, in_specs=[ANY] * n, out_specs=[ANY] * n,
        out_shape=[_sds(a.shape if in_place else (4,) + a.shape, a.dtype) for a in arrs],
        scratch_shapes=[dma(3 * n), dma(3 * n), dma(3 * n), dma(3 * n), dma(n)],
        input_output_aliases={a: a for a in range(n)} if in_place else {},
        name=name,
    )(*arrs)


def _carry_pair_exchange(grads):
    n = len(grads)

    def copies(ins, outs, send_sems, recv_sems):
        x, y, c = _place()
        cps = []
        for a in range(n):
            h = ins[a].shape[1] // 2
            theirs = pl.ds(pl.multiple_of((1 - c) * h, 8), h)
            cps.append(pltpu.make_async_remote_copy(
                src_ref=ins[a].at[:, theirs, :], dst_ref=outs[a], send_sem=send_sems.at[a], recv_sem=recv_sems.at[a],
                device_id=(x, y, 1 - c), device_id_type=MESH))
        return cps

    def start(*refs):
        for cp in copies(*refs):
            cp.start()

    def finish(*refs):
        for cp in copies(*refs):
            cp.wait()

    return _Carried(grads, [_sds((4, g.shape[1] // 2, g.shape[2]), F32) for g in grads], {}, n, start, finish)


def _row_steps(h):
    return next(q for q in (4, 2, 1) if h % (16 * q) == 0)


def _pair_sum(grad, recv, core, name):
    _, r, c = grad.shape
    h = r // 2
    nr = _row_steps(h)
    th = h // nr

    def body(core_ref, g_ref, r_ref, o_ref):
        o_ref[...] = (g_ref[...] + r_ref[...]).astype(BF16)

    return pl.pallas_call(
        body,
        grid_spec=pltpu.PrefetchScalarGridSpec(
            num_scalar_prefetch=1, grid=(4, nr),
            in_specs=[pl.BlockSpec((None, th, c), lambda s, q, core_ref: (s, core_ref[0] * nr + q, 0)),
                      pl.BlockSpec((None, th, c), lambda s, q, core_ref: (s, q, 0))],
            out_specs=pl.BlockSpec((None, th, c), lambda s, q, core_ref: (s, q, 0))),
        out_shape=_sds((4, h, c), BF16), compiler_params=_params(("arbitrary", "arbitrary")), name=name,
    )(core, grad, recv)


def _carry_chip_exchange(parts, swapped):
    n = len(parts)

    def copies(ins, outs, send_sems, recv_sems):
        x, y, c = _place()
        chips = [(_flip(x, k & 2), _flip(y, k & 1)) for k in (1, 2, 3)]
        cps = []
        for a in range(n):
            for j, (px, py) in enumerate(chips):
                cps.append(pltpu.make_async_remote_copy(
                    src_ref=ins[a].at[_slot(px, py, swapped[a])], dst_ref=outs[a].at[j],
                    send_sem=send_sems.at[3 * a + j], recv_sem=recv_sems.at[3 * a + j],
                    device_id=(px, py, c), device_id_type=MESH))
        return cps

    def start(*refs):
        for cp in copies(*refs):
            cp.start()

    def finish(*refs):
        for cp in copies(*refs):
            cp.wait()

    return _Carried(parts, [_sds((3,) + p.shape[1:], BF16) for p in parts], {}, 3 * n, start, finish)


def _chip_sum(part, recv, slot_core, name):
    _, h, c = part.shape
    nr = _row_steps(h)
    th = h // nr

    def body(sc_ref, p_ref, r_ref, o_ref):
        acc = p_ref[...].astype(F32)
        for j in range(3):
            acc = acc + r_ref[j].astype(F32)
        o_ref[...] = acc

    return pl.pallas_call(
        body,
        grid_spec=pltpu.PrefetchScalarGridSpec(
            num_scalar_prefetch=1, grid=(nr,),
            in_specs=[pl.BlockSpec((None, th, c), lambda q, sc_ref: (sc_ref[0], q, 0)),
                      pl.BlockSpec((3, th, c), lambda q, sc_ref: (0, q, 0))],
            out_specs=pl.BlockSpec((th, c), lambda q, sc_ref: (sc_ref[1] * nr + q, 0))),
        out_shape=_sds((2 * h, c), F32), compiler_params=_params(("arbitrary",)), name=name,
    )(slot_core, part, recv)


def _carry_pair_share(shards):
    n = len(shards)

    def copies(outs, send_sems, recv_sems, mine):
        x, y, c = _place()
        cps = []
        for a in range(n):
            h = outs[a].shape[0] // 2
            half = outs[a].at[pl.ds(pl.multiple_of((c if mine else 1 - c) * h, 8), h)]
            cps.append(pltpu.make_async_remote_copy(
                src_ref=half, dst_ref=half, send_sem=send_sems.at[a], recv_sem=recv_sems.at[a],
                device_id=(x, y, 1 - c), device_id_type=MESH))
        return cps

    def start(ins, outs, send_sems, recv_sems):
        for cp in copies(outs, send_sems, recv_sems, True):
            cp.start()

    def finish(ins, outs, send_sems, recv_sems):
        for cp in copies(outs, send_sems, recv_sems, False):
            cp.wait_recv()
        for cp in copies(outs, send_sems, recv_sems, True):
            cp.wait_send()

    return _Carried(shards, [_sds(s.shape, F32) for s in shards], {a: a for a in range(n)}, n, start, finish)


def _carry_gather_ici(bufs, swapped):
    n = len(bufs)

    def copies(outs, send_sems, recv_sems, sending):
        x, y, c = _place()
        cps = []
        for a in range(n):
            h = outs[a].shape[1] // 2
            mine = pl.ds(pl.multiple_of(c * h, 8), h)
            for j, k in enumerate((1, 2, 3)):
                px, py = _flip(x, k & 2), _flip(y, k & 1)
                slot = _slot(x, y, swapped[a]) if sending else _slot(px, py, swapped[a])
                piece = outs[a].at[slot, mine]
                cps.append(pltpu.make_async_remote_copy(
                    src_ref=piece, dst_ref=piece, send_sem=send_sems.at[3 * a + j], recv_sem=recv_sems.at[3 * a + j],
                    device_id=(px, py, c), device_id_type=MESH))
        return cps

    def start(ins, outs, send_sems, recv_sems):
        for cp in copies(outs, send_sems, recv_sems, True):
            cp.start()

    def finish(ins, outs, send_sems, recv_sems):
        for cp in copies(outs, send_sems, recv_sems, False):
            cp.wait_recv()
        for cp in copies(outs, send_sems, recv_sems, True):
            cp.wait_send()

    return _Carried(bufs, [_sds(b.shape, b.dtype) for b in bufs], {a: a for a in range(n)}, 3 * n, start, finish)


def _carry_gather_forward(bufs, swapped):
    n = len(bufs)

    def copies(outs, send_sems, recv_sems, sending):
        x, y, c = _place()
        cps = []
        for a in range(n):
            h = outs[a].shape[1] // 2
            rows = pl.ds(pl.multiple_of((c if sending else 1 - c) * h, 8), h)
            for j, k in enumerate((1, 2, 3)):
                piece = outs[a].at[_slot(_flip(x, k & 2), _flip(y, k & 1), swapped[a]), rows]
                cps.append(pltpu.make_async_remote_copy(
                    src_ref=piece, dst_ref=piece, send_sem=send_sems.at[3 * a + j], recv_sem=recv_sems.at[3 * a + j],
                    device_id=(x, y, 1 - c), device_id_type=MESH))
        return cps

    def start(ins, outs, send_sems, recv_sems):
        for cp in copies(outs, send_sems, recv_sems, True):
            cp.start()

    def finish(ins, outs, send_sems, recv_sems):
        for cp in copies(outs, send_sems, recv_sems, False):
            cp.wait_recv()
        for cp in copies(outs, send_sems, recv_sems, True):
            cp.wait_send()

    return _Carried(bufs, [_sds(b.shape, b.dtype) for b in bufs], {a: a for a in range(n)}, 3 * n, start, finish)


def _pack(arrs, rows_multiple=8):
    parts, offs, row = [], [], 0
    for a in arrs:
        flat = a.reshape(-1)
        nrow = -(-flat.shape[0] // D)
        parts.append(jnp.pad(flat, (0, nrow * D - flat.shape[0])))
        offs.append(row)
        row += nrow
    total = -(-row // rows_multiple) * rows_multiple
    if total > row:
        parts.append(jnp.zeros(((total - row) * D,), F32))
    return jnp.concatenate(parts).reshape(total, D), offs


def _unpack(packed, offs, shapes):
    out = []
    for off, shp in zip(offs, shapes):
        size = int(np.prod(shp))
        nrow = -(-size // D)
        out.append(packed[off:off + nrow].reshape(-1)[:size].reshape(shp))
    return out


def _to_bf16_slot(w, slot, name):
    r, c = w.shape
    tm = next(cand for cand in (256, 176, 128, 64, 32, 16) if r % cand == 0)

    def body(slot_ref, w_ref, o_ref):
        o_ref[...] = w_ref[...].astype(BF16)

    return pl.pallas_call(
        body,
        grid_spec=pltpu.PrefetchScalarGridSpec(
            num_scalar_prefetch=1, grid=(r // tm,),
            in_specs=[pl.BlockSpec((tm, c), lambda i, slot_ref: (i, 0))],
            out_specs=pl.BlockSpec((None, tm, c), lambda i, slot_ref: (slot_ref[0], i, 0))),
        out_shape=_sds((4, r, c), BF16), compiler_params=_params(("arbitrary",)), name=name,
    )(slot, w)


def _unshard_cols(g):
    s, k, n = g.shape
    return jnp.transpose(g, (1, 0, 2)).reshape(k, s * n)


def _ff_swap(v):
    b = FF_BLOCK
    return jnp.concatenate([v[..., 0:b], v[..., 2 * b:3 * b], v[..., b:2 * b], v[..., 3 * b:4 * b]], axis=-1)


LATE = ("attn_o", "conv_o", "mix_o", "up", "down")
EARLY_GRADS = ("down", "up", "mix_o", "attn_o", "conv_o")


def _weight_views(bufs):
    return {"up": bufs["up"], "attn_o": _unshard_cols(bufs["attn_o"]), "conv_o": _unshard_cols(bufs["conv_o"]),
            "mix_o": bufs["mix_o"].reshape(D, D), "down": bufs["down"].reshape(D_FF, D)}


def _pair_sums(names, grads, recv, dist):
    return [_pair_sum(g, r, dist["core"], "pair_sum_" + n) for n, g, r in zip(names, grads, recv)]


def _reduce_halves(names, parts, from_chips, dist):
    return [_chip_sum(p, r, jnp.concatenate([dist["slots"][SWAPPED[n]], dist["core"]]), "chip_sum_" + n)
            for n, p, r in zip(names, parts, from_chips)]


FUSED_TILE = 256
WIDE_TILE = 512


def _gates(z):
    return [(z, 512, 5), (z, 512, 6), (z, 512, 7), (z, 512, 8)]


def _mix_out(a, cb, z, x, w_mix_o, g_post, gt, g_pre2, sc2, sh2, name):
    def lhs(av, cv, ga0, ga1, gb0, gb1):
        ga, gb = jnp.concatenate([ga0, ga1], axis=1), jnp.concatenate([gb0, gb1], axis=1)
        return _sigmoid(ga) * av + _sigmoid(gb) * cv

    def fn(ym, y, xv, gv, gtv, g2v, scv, shv):
        yn, _ = _rms_stats(ym)
        x1 = xv + gtv * (yn * gv)
        xn, _ = _rms_stats(x1)
        return ym, y, x1, (xn * g2v) * (1.0 + scv) + shv

    return _matmul_rows(w_mix_o, form="nn", tm=min(FUSED_TILE, x.shape[0]), tk=D, fn=fn, a_rows=[(a, D, 0), (cb, D, 0)] + _gates(z),
                        a_fn=lhs, rows=[(x, D, 0)], consts=[g_post, gt, g_pre2, sc2, sh2],
                        row_outs=[(F32, D), (BF16, D), (F32, D), (BF16, D)], acc_outs=[], name=name)


def _down_tail(act, w_down, x1, target, g, gt, name):
    def fn(yv, xv, tv, gv, gtv):
        yn, r = _rms_stats(yv)
        e = xv + gtv * (yn * gv) - tv
        dx2 = e * (1.0 / D)
        dyn = dx2 * gtv
        return (dx2, _rms_bwd(dyn * gv, yn, r), _colsum(e * e) * (0.5 / D), _colsum(dyn * yn),
                _colsum(dx2 * (yn * gv)))

    return _matmul_rows(w_down, form="nn", a=act, tm=min(WIDE_TILE, x1.shape[0]), tk=FF_BLOCK, fn=fn,
                        rows=[(x1, D, 0), (target, D, 0)], consts=[g, gt], row_outs=[(F32, D), (BF16, D)],
                        acc_outs=[(1, D)] * 3, name=name)


def _up_dx_tail(dup, w_up, x1, dx2, ym, g_pre2, sc2, g_post, gt, name):
    def fn(dh, xv, dov, ymv, g2v, scv, gv, gtv):
        xn, r = _rms_stats(xv)
        dyn = dh * (1.0 + scv)
        dx1 = dov + _rms_bwd(dyn * g2v, xn, r)
        yn, r2 = _rms_stats(ymv)
        dynm = dx1 * gtv
        return (dx1, _rms_bwd(dynm * gv, yn, r2), _colsum(dh), _colsum(dh * (xn * g2v)), _colsum(dyn * xn),
                _colsum(dynm * yn), _colsum(dx1 * (yn * gv)))

    return _matmul_rows(w_up, form="nt", a=dup, tm=min(FUSED_TILE, x1.shape[0]), tk=2 * D_FF, fn=fn,
                        rows=[(x1, D, 0), (dx2, D, 0), (ym, D, 0)], consts=[g_pre2, sc2, g_post, gt],
                        row_outs=[(F32, D), (BF16, D)], acc_outs=[(1, D)] * 5, name=name)


def _mix_dx_gates(dym, w_mix_o, a, cb, z, name):
    def fn(dy, av, cv, ga0, ga1, gb0, gb1):
        sa = _sigmoid(jnp.concatenate([ga0, ga1], axis=1))
        sb = _sigmoid(jnp.concatenate([gb0, gb1], axis=1))
        dcb = dy * sb
        dga = dy * av * (sa * (1.0 - sa))
        dgb = dy * cv * (sb * (1.0 - sb))
        return dy * sa, dcb, dga, dgb, _colsum(dcb), _colsum(dga), _colsum(dgb)

    return _matmul_rows(w_mix_o, form="nt", a=dym, tm=min(FUSED_TILE, a.shape[0]), tk=D, fn=fn,
                        rows=[(a, D, 0), (cb, D, 0)] + _gates(z), consts=[], row_outs=[(BF16, D)] * 4,
                        acc_outs=[(1, D)] * 3, name=name)


def _local_step(x, target, mod, w_in, late, small, dist=None):
    sh_m, sc_m, gt_m, sh_f, sc_f, gt_f = mod
    t = x.shape[0]
    tmm = min(1024, t)
    late_swapped = [SWAPPED[n] for n in LATE]

    h1 = _pre_norm(x, small["g_pre_mix"], sc_m, sh_m, "pre_norm_mix")
    z = _matmul(h1, w_in, form="nn", out_dtype=F32, tm=tmm, tn=1152, tk=D, bias=small["b_in"], name="mm_in")
    if dist is None:
        att = _attention(z, small["gen"], "attention")
        bufs = dict(late)
    else:
        mid = [n for n in LATE if n != "down"]
        mid_swapped = [SWAPPED[n] for n in mid]
        att, landed = _attention(z, small["gen"], "attention",
                                 carried=_carry_gather_ici([late[n] for n in mid], mid_swapped))
        bufs = dict(zip(mid, _run_carried(_carry_gather_forward(landed, mid_swapped), "gather_forward")))
        bufs["down"] = late["down"]
    w = _weight_views(bufs)
    w["in"] = w_in
    a = _matmul(att, w["attn_o"], form="nn", out_dtype=F32, tm=tmm, tn=512, tk=512, name="mm_attn_o")
    u1, u3 = _conv_branch(z, small["w_dw_conv"], small["b_dw_conv"], small["g_conv_ln"], small["b_conv_ln"], "conv_branch")
    cb = _matmul(u3, w["conv_o"], form="nn", out_dtype=F32, tm=tmm, tn=512, tk=512, bias=small["b_conv_o"], name="mm_conv_o")
    ym, y, x1, h2 = _mix_out(a, cb, z, x, w["mix_o"], small["g_post_mix"], gt_m, small["g_pre_ffn"], sc_f, sh_f, "mix_out")
    mm_up = dict(form="nn", out_dtype=F32, tm=tmm, tn=FF_BLOCK, tk=D, name="mm_up")
    if dist is None:
        up = _matmul(h2, w["up"], **mm_up)
    else:
        up, landed = _matmul(h2, w["up"], carried=_carry_gather_ici([late["down"]], [False]), **mm_up)
        w["down"] = _run_carried(_carry_gather_forward(landed, [False]), "gather_forward_down")[0].reshape(D_FF, D)
    act = _ffn_act(up, small["w_dw_ffn"], small["b_dw_ffn"], "ffn_act")

    dx2, dyf, loss_cols, d_g_post_ffn, d_gt_f = _down_tail(act, w["down"], x1, target, small["g_post_ffn"], gt_f, "down_tail")
    dact = _matmul(dyf, w["down"], form="nt", out_dtype=F32, tm=tmm, tn=FF_BLOCK, tk=D, name="mm_down_dx")
    g_down = _matmul(act, dyf, form="tn", out_dtype=F32, tm=FF_BLOCK, tn=512, tk=t, name="mm_down_dw")
    dup, d_w_dw_ffn, d_b_dw_ffn = _ffn_act_bwd(dact, up, small["w_dw_ffn"], small["b_dw_ffn"], "ffn_act_bwd")
    dx1, dym, d_sh_f, d_sc_f, d_g_pre_ffn, d_g_post_mix, d_gt_m = _up_dx_tail(
        dup, w["up"], x1, dx2, ym, small["g_pre_ffn"], sc_f, small["g_post_mix"], gt_m, "up_dx_tail")
    g_up = _matmul(h2, dup, form="tn", out_dtype=F32, tm=512, tn=FF_BLOCK, tk=t, out_sharded=True, name="mm_up_dw")
    da, dcb, dgate_a, dgate_b, d_b_conv_o, sga, sgb = _mix_dx_gates(dym, w["mix_o"], a, cb, z, "mix_dx_gates")
    g_mix_o = _matmul(y, dym, form="tn", out_dtype=F32, tm=D, tn=512, tk=t, name="mm_mix_o_dw")
    datt = _matmul(da, w["attn_o"], form="nt", out_dtype=F32, tm=tmm, tn=512, tk=D, name="mm_attn_o_dx")
    g_attn_o = _matmul(att, da, form="tn", out_dtype=F32, tm=512, tn=256, tk=t, out_sharded=True, name="mm_attn_o_dw")
    du3 = _matmul(dcb, w["conv_o"], form="nt", out_dtype=F32, tm=tmm, tn=512, tk=D, name="mm_conv_o_dx")
    g_conv_o = _matmul(u3, dcb, form="tn", out_dtype=F32, tm=512, tn=256, tk=t, out_sharded=True, name="mm_conv_o_dw")
    big = {"attn_o": g_attn_o, "conv_o": g_conv_o, "mix_o": g_mix_o.reshape(4, 256, D),
           "up": g_up, "down": g_down.reshape(4, D_FF // 4, D)}
    conv_bwd = (du3, u1, z, small["w_dw_conv"], small["g_conv_ln"], small["b_conv_ln"], "conv_branch_bwd")
    in_dw = dict(form="tn", out_dtype=F32, tm=512, tn=1152, tk=t, out_sharded=True, name="mm_in_dw")
    in_dx = dict(form="nt", out_dtype=F32, tm=min(WIDE_TILE, t), tn=D, tk=D_IN, name="mm_in_dx")
    if dist is None:
        dglu, d_w_dw_conv, d_b_dw_conv, d_g_conv_ln, d_b_conv_ln, sglu = _conv_branch_bwd(*conv_bwd)
        dq, dk, dv, sq, sk, sv, dgen = _attention_bwd(z, datt, small["gen"], "attention_bwd")
        dz = jnp.concatenate([dq, dk, dv, dglu, dgate_a, dgate_b], axis=1)
        big["in"] = _matmul(h1, dz, **in_dw)
        dh1 = _matmul(dz, w_in, **in_dx)
    else:
        early = [big[n] for n in EARLY_GRADS]
        (dglu, d_w_dw_conv, d_b_dw_conv, d_g_conv_ln, d_b_conv_ln, sglu), recv = _conv_branch_bwd(
            *conv_bwd, carried=_carry_pair_exchange(early))
        parts = _pair_sums(EARLY_GRADS, early, recv, dist)
        (dq, dk, dv, sq, sk, sv, dgen), from_chips = _attention_bwd(
            z, datt, small["gen"], "attention_bwd",
            carried=_carry_chip_exchange(parts, [SWAPPED[n] for n in EARLY_GRADS]))
        halves = _reduce_halves(EARLY_GRADS, parts, from_chips, dist)
        dz = jnp.concatenate([dq, dk, dv, dglu, dgate_a, dgate_b], axis=1)
        g_in, shards = _matmul(h1, dz, carried=_carry_pair_share(halves), **in_dw)
        big = dict(zip(EARLY_GRADS, shards))
        recv_in = _run_carried(_carry_pair_exchange([g_in]), "pair_exchange_in")
        part_in = _pair_sums(("in",), [g_in], recv_in, dist)
        dh1, from_chips_in = _matmul(dz, w_in, carried=_carry_chip_exchange(part_in, [False]), **in_dx)
        half_in = _reduce_halves(("in",), part_in, from_chips_in, dist)
        big["in"] = _run_carried(_carry_pair_share(half_in), "pair_share_in")[0]
    d_b_in = jnp.concatenate([sq, sk, sv, sglu, sga, sgb], axis=1)
    grad_x, d_sh_m, d_sc_m, d_g_pre_mix = _pre_norm_bwd(dh1, x, dx1, small["g_pre_mix"], sc_m, "pre_norm_mix_bwd")

    dmod = [d_sh_m, d_sc_m, d_gt_m, d_sh_f, d_sc_f, d_gt_f]
    sm = {"g_pre_mix": d_g_pre_mix, "g_post_mix": d_g_post_mix, "b_in": d_b_in, "gen": dgen,
          "w_dw_conv": d_w_dw_conv, "b_dw_conv": d_b_dw_conv, "g_conv_ln": d_g_conv_ln, "b_conv_ln": d_b_conv_ln,
          "b_conv_o": d_b_conv_o, "g_pre_ffn": d_g_pre_ffn, "g_post_ffn": d_g_post_ffn,
          "w_dw_ffn": d_w_dw_ffn, "b_dw_ffn": d_b_dw_ffn}
    return loss_cols, grad_x, dmod, big, sm


BIG = ("in", "attn_o", "conv_o", "mix_o", "up", "down")
SWAPPED = {"in": False, "attn_o": False, "conv_o": False, "mix_o": False, "up": True, "down": False}
SMALL_ORDER = ("b_ada", "g_pre_mix", "g_post_mix", "b_in", "rel_bias", "b_dw_conv", "g_conv_ln", "b_conv_ln",
               "b_conv_o", "g_pre_ffn", "g_post_ffn", "b_dw_ffn", "w_dw_conv", "w_dw_ffn")


def kernel(x, c, w_ada, b_ada, g_pre_mix, g_post_mix, w_in, b_in, rel_bias, w_attn_o, w_dw_conv, b_dw_conv, g_conv_ln, b_conv_ln, w_conv_o, b_conv_o, w_mix_o, g_pre_ffn, g_post_ffn, w_up, w_dw_ffn, b_dw_ffn, w_down, loss_target, m_w_ada, m_b_ada, m_g_pre_mix, m_g_post_mix, m_w_in, m_b_in, m_rel_bias, m_w_attn_o, m_w_dw_conv, m_b_dw_conv, m_g_conv_ln, m_b_conv_ln, m_w_conv_o, m_b_conv_o, m_w_mix_o, m_g_pre_ffn, m_g_post_ffn, m_w_up, m_w_dw_ffn, m_b_dw_ffn, m_w_down, v_w_ada, v_b_ada, v_g_pre_mix, v_g_post_mix, v_w_in, v_b_in, v_rel_bias, v_w_attn_o, v_w_dw_conv, v_b_dw_conv, v_g_conv_ln, v_b_conv_ln, v_w_conv_o, v_b_conv_o, v_w_mix_o, v_g_pre_ffn, v_g_post_ffn, v_w_up, v_w_dw_ffn, v_b_dw_ffn, v_w_down):
    given = dict(locals())
    ax, ay, ac = lax.axis_index("x"), lax.axis_index("y"), lax.axis_index("c")
    shard = 2 * ax + ay
    me = 4 * ax + 2 * ay + ac
    xs, target = x[0], loss_target[0]

    c_pad = jnp.pad(c, ((0, 7), (0, 0)))
    c_g, _ = _allgather8(c_pad, "gather_c")
    c_all = c_g[:, 0, :]
    b_ada_shard = lax.dynamic_slice(b_ada, (0, shard * 1536), (1, 1536))
    mod_shard, c_act = _ada_fwd(c_all, w_ada[0], b_ada_shard, "ada_fwd")
    small_in = [jnp.pad(mod_shard, ((0, 8), (0, 0))),
                jnp.pad(w_dw_conv[0], ((0, 1), (0, 0))),
                jnp.pad(w_dw_ffn[0], ((0, 13), (0, 0)))]
    mod_g, wdc_g, wdf_g = _gather_shards(small_in, [False, False, True], "gather_small")
    mod_all = jnp.transpose(mod_g[:, :8, :], (1, 0, 2)).reshape(8, 6 * D)
    mod_row = lax.dynamic_slice(mod_all, (me, 0), (1, 6 * D))
    mod = [mod_row[:, k * D:(k + 1) * D] for k in range(6)]

    slots = {sw: _slot(ax, ay, sw).astype(jnp.int32).reshape(1) for sw in (False, True)}
    own = {n: _to_bf16_slot(given["w_" + n][0], slots[SWAPPED[n]], "cast_" + n) for n in BIG}
    w_in_all = _gather_shards([own["in"]], [False], "gather_w_in", in_place=True)[0]
    core = ac.astype(jnp.int32).reshape(1)
    dist = {"core": core, "slots": slots}

    sel = jnp.asarray(_toeplitz_map())
    rel_pad = jnp.pad(rel_bias[0], ((0, 0), (0, REL_PAD - (2 * MAX_REL + 1))))
    gen = _select_call(rel_pad, sel.T.astype(BF16), "bias_rows")
    small = {"g_pre_mix": g_pre_mix, "g_post_mix": g_post_mix, "b_in": b_in, "gen": gen,
             "w_dw_conv": _unshard_cols(wdc_g[:, :CONV_K, :]), "b_dw_conv": b_dw_conv, "g_conv_ln": g_conv_ln,
             "b_conv_ln": b_conv_ln, "b_conv_o": b_conv_o, "g_pre_ffn": g_pre_ffn, "g_post_ffn": g_post_ffn,
             "w_dw_ffn": _unshard_cols(wdf_g[:, :FFN_K, :]), "b_dw_ffn": _ff_swap(b_dw_ffn)}

    loss_cols, grad_x, dmod, reduced, sm = _local_step(xs, target, mod, w_in_all, {n: own[n] for n in LATE}, small, dist)

    d_rel = _select_call(sm["gen"], sel.astype(BF16), "bias_fold")[:, :2 * MAX_REL + 1]
    small_grads = {"g_pre_mix": sm["g_pre_mix"], "g_post_mix": sm["g_post_mix"], "b_in": sm["b_in"], "rel_bias": d_rel[None],
                   "b_dw_conv": sm["b_dw_conv"], "g_conv_ln": sm["g_conv_ln"], "b_conv_ln": sm["b_conv_ln"],
                   "b_conv_o": sm["b_conv_o"], "g_pre_ffn": sm["g_pre_ffn"], "g_post_ffn": sm["g_post_ffn"],
                   "b_dw_ffn": _ff_swap(sm["b_dw_ffn"]), "w_dw_conv": sm["w_dw_conv"], "w_dw_ffn": _ff_swap(sm["w_dw_ffn"])}
    order = [n for n in SMALL_ORDER if n != "b_ada"]
    packed, offs = _pack([jnp.concatenate(dmod, axis=1)] + [small_grads[n] for n in order] + [loss_cols])
    every, total = _allgather8(packed, "gather_small_grads")
    loss = jnp.sum(total[offs[-1]])
    offs = offs[:-1]
    dmod_all = every[:, 0:6, :].reshape(8, 6 * D)
    full_shapes = {n: given[n].shape for n in order}
    full_shapes["w_dw_conv"], full_shapes["w_dw_ffn"] = (1, CONV_K, 512), (1, FFN_K, 2 * D_FF)
    sums = dict(zip(order, _unpack(total, offs[1:], [full_shapes[n] for n in order])))
    sums["b_ada"] = total[0:6].reshape(1, 6 * D)
    sums["w_dw_conv"] = lax.dynamic_slice(sums["w_dw_conv"], (0, 0, shard * 128), (1, CONV_K, 128))
    sums["w_dw_ffn"] = lax.dynamic_slice(sums["w_dw_ffn"], (0, 0, shard * FF_BLOCK), (1, FFN_K, FF_BLOCK))

    upd = dict(zip(SMALL_ORDER, _adamw_many(
        [(given[n], sums[n], given["m_" + n], given["v_" + n]) for n in SMALL_ORDER], "adamw_small")))

    dmod_shard = lax.dynamic_slice(dmod_all, (0, shard * 1536), (8, 1536))
    ada = _ada_bwd_adamw(c_act.T, dmod_shard, w_ada[0], m_w_ada[0], v_w_ada[0], "ada_bwd_adamw")

    out = {"grad_w_ada": ada[0][None], "delta_w_ada": ada[1][None], "new_m_w_ada": ada[2][None], "new_v_w_ada": ada[3][None]}
    for n in BIG:
        g = reduced[n]
        g, dl, nm, nv = _adamw(given["w_" + n][0], g, given["m_w_" + n][0], given["v_w_" + n][0], "adamw_" + n)
        out["grad_w_" + n], out["delta_w_" + n], out["new_m_w_" + n], out["new_v_w_" + n] = g[None], dl[None], nm[None], nv[None]
    for n in SMALL_ORDER:
        out["grad_" + n], out["delta_" + n], out["new_m_" + n], out["new_v_" + n] = sums[n], *upd[n]

    weights = ["w_ada", "b_ada", "g_pre_mix", "g_post_mix", "w_in", "b_in", "rel_bias", "w_attn_o", "w_dw_conv", "b_dw_conv",
               "g_conv_ln", "b_conv_ln", "w_conv_o", "b_conv_o", "w_mix_o", "g_pre_ffn", "g_post_ffn", "w_up", "w_dw_ffn",
               "b_dw_ffn", "w_down"]
    return (loss, grad_x[None], *[out["grad_" + n] for n in weights], *[out["delta_" + n] for n in weights],
            *[out["new_m_" + n] for n in weights], *[out["new_v_" + n] for n in weights])
```

```python
import functools
import math

import numpy as np
import jax
import jax.numpy as jnp
from jax import lax
from jax.experimental import pallas as pl
from jax.experimental.pallas import tpu as pltpu

F32, BF16 = jnp.float32, jnp.bfloat16
MESH = pl.DeviceIdType.MESH

D = 1024
D_IN = 4608
D_FF = 2816
CONV_K = 31
FFN_K = 3
N_HEADS = 8
CHUNK = 64
LEFT_CHUNKS = 8
MAX_REL = 128
EPS = 1e-6
NEG_INF = -1e30
Q_TILE = 256
WINDOW = Q_TILE + LEFT_CHUNKS * CHUNK
REL_PAD = 384
TOEP = 1024
ROW_TILE = 256
VMEM_LIMIT = 60 * 1024 * 1024

ADAM_LR, ADAM_B1, ADAM_B2, ADAM_EPS, ADAM_WD, ADAM_STEP = 0.001, 0.9, 0.999, 1e-08, 0.01, 10


def _params(sem=None):
    return pltpu.CompilerParams(dimension_semantics=sem, vmem_limit_bytes=VMEM_LIMIT)


def _sds(shape, dtype):
    return jax.ShapeDtypeStruct(tuple(shape), dtype)


ANY = pl.BlockSpec(memory_space=pl.ANY)


class _Carried:
    def __init__(self, ins, out_shapes, aliases, n_sems, start, finish):
        self.ins, self.out_shapes, self.aliases = list(ins), list(out_shapes), dict(aliases)
        self.n_sems, self.start, self.finish = n_sems, start, finish


def _call(body, *, grid, in_specs, out_specs, out_shape, scratch_shapes, sem, name, args, carried=None):
    in_specs, out_specs, out_shape = list(in_specs), list(out_specs), list(out_shape)
    scratch_shapes = list(scratch_shapes)
    if carried is None:
        return pl.pallas_call(body, grid=grid, in_specs=in_specs, out_specs=out_specs, out_shape=out_shape,
                              scratch_shapes=scratch_shapes, compiler_params=_params(sem), name=name)(*args)
    n_in, n_out, n_scr = len(in_specs), len(out_specs), len(scratch_shapes)
    c_in, c_out = len(carried.ins), len(carried.out_shapes)

    def full(*refs):
        pos = [0]

        def take(k):
            part = refs[pos[0]:pos[0] + k]
            pos[0] += k
            return part

        ins, cins, outs, couts, scr = take(n_in), take(c_in), take(n_out), take(c_out), take(n_scr)
        send_sems, recv_sems = take(2)
        first = last = None
        for d, size in enumerate(grid):
            pid = pl.program_id(d)
            first = (pid == 0) if first is None else first & (pid == 0)
            last = (pid == size - 1) if last is None else last & (pid == size - 1)

        @pl.when(first)
        def _():
            carried.start(cins, couts, send_sems, recv_sems)

        body(*ins, *outs, *scr)

        @pl.when(last)
        def _():
            carried.finish(cins, couts, send_sems, recv_sems)

    sems = [pltpu.SemaphoreType.DMA((carried.n_sems,)), pltpu.SemaphoreType.DMA((carried.n_sems,))]
    return pl.pallas_call(
        full, grid=grid, in_specs=in_specs + [ANY] * c_in, out_specs=out_specs + [ANY] * c_out,
        out_shape=out_shape + carried.out_shapes, scratch_shapes=scratch_shapes + sems,
        input_output_aliases={n_in + k: n_out + v for k, v in carried.aliases.items()},
        compiler_params=_params(tuple("arbitrary" for _ in grid)), name=name,
    )(*args, *carried.ins)


def _run_carried(carried, name):
    c_in = len(carried.ins)

    def body(*refs):
        cins, couts = refs[:c_in], refs[c_in:c_in + len(carried.out_shapes)]
        send_sems, recv_sems = refs[-2:]
        carried.start(cins, couts, send_sems, recv_sems)
        carried.finish(cins, couts, send_sems, recv_sems)

    return pl.pallas_call(
        body, in_specs=[ANY] * c_in, out_specs=[ANY] * len(carried.out_shapes), out_shape=carried.out_shapes,
        scratch_shapes=[pltpu.SemaphoreType.DMA((carried.n_sems,)), pltpu.SemaphoreType.DMA((carried.n_sems,))],
        input_output_aliases=carried.aliases, name=name,
    )(*carried.ins)


def _matmul(a, b, *, form, out_dtype, tm, tn, tk, name, bias=None, add=None, out_sharded=False, carried=None):
    b3 = b.ndim == 3
    resident = 0
    if form == "nn":
        m, k = a.shape
        n = b.shape[0] * b.shape[2] if b3 else b.shape[1]
        dn = (((1,), (0,)), ((), ()))
        a_spec = pl.BlockSpec((tm, tk), lambda i, j, kk: (i, kk))
        b_spec = (pl.BlockSpec((None, tk, tn), lambda i, j, kk: (j, kk, 0)) if b3
                  else pl.BlockSpec((tk, tn), lambda i, j, kk: (kk, j)))
    elif form == "nt":
        m, k = a.shape
        n = b.shape[1] if b3 else b.shape[0]
        dn = (((1,), (1,)), ((), ()))
        a_spec = pl.BlockSpec((tm, tk), lambda i, j, kk: (i, kk))
        if b3 and tk == k:
            resident = b.shape[0]
            b_spec = pl.BlockSpec((resident, tn, b.shape[2]), lambda i, j, kk: (0, j, 0))
        else:
            b_spec = (pl.BlockSpec((None, tn, tk), lambda i, j, kk: (kk, j, 0)) if b3
                      else pl.BlockSpec((tn, tk), lambda i, j, kk: (j, kk)))
    else:
        k, m = a.shape
        n = b.shape[1]
        dn = (((0,), (0,)), ((), ()))
        a_spec = pl.BlockSpec((tk, tm), lambda i, j, kk: (kk, i))
        b_spec = pl.BlockSpec((tk, tn), lambda i, j, kk: (kk, j))
    assert m % tm == 0 and n % tn == 0 and k % tk == 0, (name, m, n, k, tm, tn, tk)
    nk = k // tk
    in_specs, args = [a_spec, b_spec], [a, b]
    if bias is not None:
        in_specs.append(pl.BlockSpec((1, tn), lambda i, j, kk: (0, j)))
        args.append(bias)
    if add is not None:
        in_specs.append(pl.BlockSpec((tm, tn), lambda i, j, kk: (i, j)))
        args.append(add)
    if out_sharded:
        out_shape = _sds((n // tn, m, tn), out_dtype)
        out_spec = pl.BlockSpec((None, tm, tn), lambda i, j, kk: (j, i, 0))
    else:
        out_shape = _sds((m, n), out_dtype)
        out_spec = pl.BlockSpec((tm, tn), lambda i, j, kk: (i, j))

    def body(*refs):
        a_ref, b_ref = refs[0], refs[1]
        pos = 2
        bias_ref = add_ref = None
        if bias is not None:
            bias_ref, pos = refs[pos], pos + 1
        if add is not None:
            add_ref, pos = refs[pos], pos + 1
        o_ref = refs[pos]
        if resident:
            ks = b_ref.shape[2]
            p = None
            for s in range(resident):
                ps = lax.dot_general(a_ref[:, s * ks:(s + 1) * ks], b_ref[s], dn, preferred_element_type=F32)
                p = ps if p is None else p + ps
        else:
            av, bv = a_ref[...], b_ref[...]
            if av.dtype != BF16:
                av = av.astype(BF16)
            if bv.dtype != BF16:
                bv = bv.astype(BF16)
            p = lax.dot_general(av, bv, dn, preferred_element_type=F32)

        def finish(acc):
            if bias_ref is not None:
                acc = acc + bias_ref[...]
            if add_ref is not None:
                acc = acc + add_ref[...]
            o_ref[...] = acc.astype(o_ref.dtype)

        if nk == 1:
            finish(p)
        else:
            acc_ref = refs[pos + 1]
            kk = pl.program_id(2)

            @pl.when(kk == 0)
            def _():
                acc_ref[...] = p

            @pl.when(kk > 0)
            def _():
                acc_ref[...] += p

            @pl.when(kk == nk - 1)
            def _():
                finish(acc_ref[...])

    res = _call(body, grid=(m // tm, n // tn, nk), in_specs=in_specs, out_specs=[out_spec], out_shape=[out_shape],
                scratch_shapes=[pltpu.VMEM((tm, tn), F32)] if nk > 1 else [],
                sem=("parallel", "parallel", "arbitrary"), name=name, args=args, carried=carried)
    return res[0] if carried is None else (res[0], res[1:])


def _rowcall(fn, rows, consts, row_outs, acc_outs, *, name, tm=ROW_TILE, col_grid=1):
    n_rows = rows[0][0].shape[0]
    assert n_rows % tm == 0
    grid = (col_grid, n_rows // tm)
    in_specs = [pl.BlockSpec((tm, w), functools.partial(lambda c, i, cb: (i, cb + c), cb=cb)) for _, w, cb in rows]
    in_specs += [pl.BlockSpec(k.shape, functools.partial(lambda c, i, nd: (0,) * nd, nd=k.ndim)) for k in consts]
    out_specs = [pl.BlockSpec((tm, w), lambda c, i: (i, c)) for _, _, _, w in row_outs]
    out_specs += [pl.BlockSpec((r, w), lambda c, i: (0, c)) for r, _, w in acc_outs]
    out_shape = [_sds((nr, nc), dt) for nr, nc, dt, _ in row_outs] + [_sds((r, nc), F32) for r, nc, _ in acc_outs]
    n_in, n_ro = len(rows) + len(consts), len(row_outs)

    def body(*refs):
        res = fn(*[r[...] for r in refs[:n_in]])
        if not isinstance(res, (tuple, list)):
            res = (res,)
        outs = refs[n_in:]
        for o_ref, val in zip(outs[:n_ro], res[:n_ro]):
            o_ref[...] = val.astype(o_ref.dtype)
        if acc_outs:
            first = pl.program_id(1) == 0

            @pl.when(first)
            def _():
                for o_ref, val in zip(outs[n_ro:], res[n_ro:]):
                    o_ref[...] = val

            @pl.when(jnp.logical_not(first))
            def _():
                for o_ref, val in zip(outs[n_ro:], res[n_ro:]):
                    o_ref[...] += val

    out = pl.pallas_call(
        body, grid=grid, in_specs=in_specs, out_specs=out_specs, out_shape=out_shape,
        compiler_params=_params(("arbitrary", "arbitrary")), name=name,
    )(*[r[0] for r in rows], *consts)
    return out


def _matmul_rows(b, *, form, tm, tk, fn, rows, consts, row_outs, acc_outs, name, a=None, a_rows=None, a_fn=None,
                 carried=None):
    b3 = b.ndim == 3
    resident = 0
    if form == "nn":
        k, n = b.shape
        b_spec = pl.BlockSpec((tk, n), lambda i, kk: (kk, 0))
        dn = (((1,), (0,)), ((), ()))
    else:
        n = b.shape[1] if b3 else b.shape[0]
        k = b.shape[0] * b.shape[2] if b3 else b.shape[1]
        if b3 and tk == k:
            resident = b.shape[0]
            b_spec = pl.BlockSpec(b.shape, lambda i, kk: (0, 0, 0))
        else:
            b_spec = (pl.BlockSpec((None, n, tk), lambda i, kk: (kk, 0, 0)) if b3
                      else pl.BlockSpec((n, tk), lambda i, kk: (0, kk)))
        dn = (((1,), (1,)), ((), ()))
    nk = k // tk
    lhs_in = [(a, tk, 0)] if a is not None else list(a_rows)
    assert a is not None or nk == 1
    m = lhs_in[0][0].shape[0]
    n_lhs = len(lhs_in)
    in_specs = [pl.BlockSpec((tm, tk), lambda i, kk: (i, kk))] if a is not None else [
        pl.BlockSpec((tm, w), functools.partial(lambda i, kk, cb: (i, cb), cb=cb)) for _, w, cb in a_rows]
    in_specs.append(b_spec)
    in_specs += [pl.BlockSpec((tm, w), functools.partial(lambda i, kk, cb: (i, cb), cb=cb)) for _, w, cb in rows]
    in_specs += [pl.BlockSpec(c.shape, functools.partial(lambda i, kk, nd: (0,) * nd, nd=c.ndim)) for c in consts]
    out_specs = [pl.BlockSpec((tm, w), lambda i, kk: (i, 0)) for _, w in row_outs]
    out_specs += [pl.BlockSpec((r, w), lambda i, kk: (0, 0)) for r, w in acc_outs]
    out_shape = [_sds((m, w), dt) for dt, w in row_outs] + [_sds((r, w), F32) for r, w in acc_outs]
    n_rows, n_consts, n_ro, n_acc = len(rows), len(consts), len(row_outs), len(acc_outs)

    def body(*refs):
        pos = n_lhs + 1
        row_refs, const_refs = refs[pos:pos + n_rows], refs[pos + n_rows:pos + n_rows + n_consts]
        pos += n_rows + n_consts
        out_refs, acc_refs = refs[pos:pos + n_ro], refs[pos + n_ro:pos + n_ro + n_acc]
        i, kk = pl.program_id(0), pl.program_id(1)
        if resident:
            b_ref, ks, p = refs[n_lhs], b.shape[2], None
            for s in range(resident):
                ps = lax.dot_general(refs[0][:, s * ks:(s + 1) * ks], b_ref[s], dn, preferred_element_type=F32)
                p = ps if p is None else p + ps
        else:
            lhs = refs[0][...] if a is not None else a_fn(*[r[...] for r in refs[:n_lhs]]).astype(BF16)
            p = lax.dot_general(lhs, refs[n_lhs][...], dn, preferred_element_type=F32)

        def finish(acc):
            extra = [r[...] for r in row_refs] + [c[...] for c in const_refs]
            res = fn(acc, lhs, *extra) if a is None else fn(acc, *extra)
            for o_ref, val in zip(out_refs, res[:n_ro]):
                o_ref[...] = val.astype(o_ref.dtype)
            if n_acc:
                @pl.when(i == 0)
                def _():
                    for o_ref, val in zip(acc_refs, res[n_ro:]):
                        o_ref[...] = val

                @pl.when(i > 0)
                def _():
                    for o_ref, val in zip(acc_refs, res[n_ro:]):
                        o_ref[...] += val

        if nk == 1:
            finish(p)
        else:
            acc_ref = refs[pos + n_ro + n_acc]

            @pl.when(kk == 0)
            def _():
                acc_ref[...] = p

            @pl.when(kk > 0)
            def _():
                acc_ref[...] += p

            @pl.when(kk == nk - 1)
            def _():
                finish(acc_ref[...])

    res = _call(body, grid=(m // tm, nk), in_specs=in_specs, out_specs=out_specs, out_shape=out_shape,
                scratch_shapes=[pltpu.VMEM((tm, n), F32)] if nk > 1 else [], sem=("arbitrary", "arbitrary"),
                name=name, args=[r[0] for r in lhs_in] + [b] + [r[0] for r in rows] + list(consts), carried=carried)
    own = n_ro + n_acc
    return res[:own] if carried is None else (res[:own], res[own:])


def _colsum(v):
    return jnp.sum(v, axis=0, keepdims=True)


def _sigmoid(v):
    return 1.0 / (1.0 + jnp.exp(-v))


_GELU_C = math.sqrt(2.0 / math.pi)


def _gelu(v):
    return 0.5 * v * (1.0 + jnp.tanh(_GELU_C * (v + 0.044715 * (v * v * v))))


def _gelu_and_grad(v):
    th = jnp.tanh(_GELU_C * (v + 0.044715 * (v * v * v)))
    g = 0.5 * v * (1.0 + th)
    dg = 0.5 * (1.0 + th) + 0.5 * v * (1.0 - th * th) * (_GELU_C * (1.0 + 3.0 * 0.044715 * (v * v)))
    return g, dg


def _rms_stats(v):
    r = lax.rsqrt(jnp.mean(v * v, axis=-1, keepdims=True) + EPS)
    return v * r, r


def _rms_bwd(dn, vn, r):
    return r * (dn - vn * jnp.mean(dn * vn, axis=-1, keepdims=True))


def _pre_norm(x, g, sc, sh, name):
    def fn(xv, gv, scv, shv):
        xn, _ = _rms_stats(xv)
        return (xn * gv) * (1.0 + scv) + shv
    return _rowcall(fn, [(x, D, 0)], [g, sc, sh], [(x.shape[0], D, BF16, D)], [], name=name)[0]


def _pre_norm_bwd(dh, x, dx_other, g, sc, name):
    def fn(dhv, xv, dov, gv, scv):
        xn, r = _rms_stats(xv)
        yn = xn * gv
        dyn = dhv * (1.0 + scv)
        dx = _rms_bwd(dyn * gv, xn, r)
        return dov + dx, _colsum(dhv), _colsum(dhv * yn), _colsum(dyn * xn)
    t = x.shape[0]
    return _rowcall(fn, [(dh, D, 0), (x, D, 0), (dx_other, D, 0)], [g, sc], [(t, D, F32, D)],
                    [(1, D, D)] * 3, name=name)


def _post_res(x, ypre, g, gt, name):
    def fn(xv, yv, gv, gtv):
        yn, _ = _rms_stats(yv)
        return xv + gtv * (yn * gv)
    return _rowcall(fn, [(x, D, 0), (ypre, D, 0)], [g, gt], [(x.shape[0], D, F32, D)], [], name=name)[0]


def _post_res_bwd(dxo, ypre, g, gt, name):
    def fn(dv, yv, gv, gtv):
        yn, r = _rms_stats(yv)
        dyn = dv * gtv
        dy = _rms_bwd(dyn * gv, yn, r)
        return dy, _colsum(dyn * yn), _colsum(dv * (yn * gv))
    t = ypre.shape[0]
    return _rowcall(fn, [(dxo, D, 0), (ypre, D, 0)], [g, gt], [(t, D, BF16, D)], [(1, D, D)] * 2, name=name)


def _ffn_tail(x1, yf, target, g, gt, name):
    def fn(xv, yv, tv, gv, gtv):
        yn, r = _rms_stats(yv)
        e = xv + gtv * (yn * gv) - tv
        dx2 = e * (1.0 / D)
        dyn = dx2 * gtv
        dy = _rms_bwd(dyn * gv, yn, r)
        return dx2, dy, _colsum(e * e) * (0.5 / D), _colsum(dyn * yn), _colsum(dx2 * (yn * gv))
    t = x1.shape[0]
    return _rowcall(fn, [(x1, D, 0), (yf, D, 0), (target, D, 0)], [g, gt], [(t, D, F32, D), (t, D, BF16, D)],
                    [(1, D, D)] * 3, name=name)


def _gate_merge(a, cb, z, name):
    def fn(av, cv, gav, gbv):
        return _sigmoid(gav) * av + _sigmoid(gbv) * cv
    t = a.shape[0]
    return _rowcall(fn, [(a, 512, 0), (cb, 512, 0), (z, 512, 5), (z, 512, 7)], [],
                    [(t, D, BF16, 512)], [], name=name, col_grid=2)[0]


def _gate_merge_bwd(dy, a, cb, z, name):
    def fn(dv, av, cv, gav, gbv):
        sa, sb = _sigmoid(gav), _sigmoid(gbv)
        dcb = dv * sb
        dga = dv * av * (sa * (1.0 - sa))
        dgb = dv * cv * (sb * (1.0 - sb))
        return dv * sa, dcb, dga, dgb, _colsum(dcb), _colsum(dga), _colsum(dgb)
    t = a.shape[0]
    return _rowcall(fn, [(dy, 512, 0), (a, 512, 0), (cb, 512, 0), (z, 512, 5), (z, 512, 7)], [],
                    [(t, D, BF16, 512)] * 4, [(1, D, 512)] * 3, name=name, col_grid=2)


CONV_HALO = 32


def _layer_norm_parts(u):
    mu = jnp.mean(u, axis=-1, keepdims=True)
    d = u - mu
    r = lax.rsqrt(jnp.mean(d * d, axis=-1, keepdims=True) + EPS)
    return d * r, r


LANES = 128
SUBLANE_ROWS = 8
CONV_ROWS = 64


def _lanes(c):
    return slice(c * LANES, (c + 1) * LANES)


def _conv_branch(z, w_dw, b_dw, g_ln, b_ln, name, tm=ROW_TILE):
    t = z.shape[0]
    per = tm // CONV_HALO
    n_chunks = 512 // LANES

    def body(ga_ref, gb_ref, gah_ref, gbh_ref, w_ref, b_ref, g_ref, bl_ref, u1_ref, u3_ref, scr):
        i = pl.program_id(0)
        u0h = jnp.where(i > 0, gah_ref[...] * _sigmoid(gbh_ref[...]), 0.0)
        u0 = ga_ref[...] * _sigmoid(gb_ref[...])
        for c in range(n_chunks):
            scr[c, 0:CONV_HALO, :] = u0h[:, _lanes(c)]
            scr[c, CONV_HALO:CONV_HALO + tm, :] = u0[:, _lanes(c)]
        for c in range(n_chunks):
            for r0 in range(0, tm, CONV_ROWS):
                acc = jnp.zeros((CONV_ROWS, LANES), F32) + b_ref[:, _lanes(c)]
                for j in range(CONV_K):
                    acc = acc + w_ref[j:j + 1, _lanes(c)] * scr[c, pl.ds(r0 + CONV_HALO - (CONV_K - 1) + j, CONV_ROWS), :]
                u1_ref[r0:r0 + CONV_ROWS, _lanes(c)] = acc
        xh, _ = _layer_norm_parts(u1_ref[...])
        u2 = xh * g_ref[...] + bl_ref[...]
        u3_ref[...] = (u2 * _sigmoid(u2)).astype(BF16)

    cur = lambda cb: pl.BlockSpec((tm, 512), lambda i: (i, cb))
    halo = lambda cb: pl.BlockSpec((CONV_HALO, 512), lambda i: (jnp.maximum(i * per - 1, 0), cb))
    whole = lambda a: pl.BlockSpec(a.shape, lambda i: (0, 0))
    return pl.pallas_call(
        body, grid=(t // tm,),
        in_specs=[cur(3), cur(4), halo(3), halo(4), whole(w_dw), whole(b_dw), whole(g_ln), whole(b_ln)],
        out_specs=[pl.BlockSpec((tm, 512), lambda i: (i, 0))] * 2,
        out_shape=[_sds((t, 512), F32), _sds((t, 512), BF16)],
        scratch_shapes=[pltpu.VMEM((n_chunks, CONV_HALO + tm, LANES), F32)],
        compiler_params=_params(("arbitrary",)), name=name,
    )(z, z, z, z, w_dw, b_dw, g_ln, b_ln)


def _conv_branch_bwd(du3, u1, z, w_dw, g_ln, b_ln, name, tm=ROW_TILE, carried=None):
    t = z.shape[0]
    per = tm // CONV_HALO
    last = t // tm - 1
    n_chunks = 512 // LANES

    def du1_of(du3v, u1v, g, b):
        xh, r = _layer_norm_parts(u1v)
        u2 = xh * g + b
        s = _sigmoid(u2)
        du2 = du3v * (s * (1.0 + u2 * (1.0 - s)))
        dxh = du2 * g
        du1 = r * (dxh - jnp.mean(dxh, axis=-1, keepdims=True) - xh * jnp.mean(dxh * xh, axis=-1, keepdims=True))
        return du1, du2, xh

    def body(d_ref, u_ref, dn_ref, un_ref, ga_ref, gb_ref, gah_ref, gbh_ref, w_ref, g_ref, bl_ref,
             dglu_ref, dw_ref, dbdw_ref, dg_ref, dbl_ref, dbin_ref, scr, scd):
        i = pl.program_id(0)
        g, b = g_ref[...], bl_ref[...]
        du1, du2, xh = du1_of(d_ref[...], u_ref[...], g, b)
        du1n, _, _ = du1_of(dn_ref[...], un_ref[...], g, b)
        du1n = jnp.where(i < last, du1n, 0.0)
        sgb = _sigmoid(gb_ref[...])
        ga = ga_ref[...]
        u0 = ga * sgb
        u0h = jnp.where(i > 0, gah_ref[...] * _sigmoid(gbh_ref[...]), 0.0)
        for c in range(n_chunks):
            scd[c, 0:tm, :] = du1[:, _lanes(c)]
            scd[c, tm:tm + CONV_HALO, :] = du1n[:, _lanes(c)]
            scr[c, 0:CONV_HALO, :] = u0h[:, _lanes(c)]
            scr[c, CONV_HALO:CONV_HALO + tm, :] = u0[:, _lanes(c)]

        @pl.when(i == 0)
        def _():
            for ref in (dw_ref, dbdw_ref, dg_ref, dbl_ref, dbin_ref):
                ref[...] = jnp.zeros_like(ref)

        dsg = ga * (sgb * (1.0 - sgb))
        for c in range(n_chunks):
            gate = slice(512 + c * LANES, 512 + (c + 1) * LANES)
            for r0 in range(0, tm, CONV_ROWS):
                rows = slice(r0, r0 + CONV_ROWS)
                du0 = jnp.zeros((CONV_ROWS, LANES), F32)
                for j in range(CONV_K):
                    du0 = du0 + w_ref[j:j + 1, _lanes(c)] * scd[c, pl.ds(r0 + CONV_K - 1 - j, CONV_ROWS), :]
                dga = du0 * sgb[rows, _lanes(c)]
                dgb = du0 * dsg[rows, _lanes(c)]
                dglu_ref[rows, _lanes(c)] = dga.astype(BF16)
                dglu_ref[rows, gate] = dgb.astype(BF16)
                dbin_ref[:, _lanes(c)] += _colsum(dga)
                dbin_ref[:, gate] += _colsum(dgb)
            for j in range(CONV_K):
                dwj = jnp.zeros((SUBLANE_ROWS, LANES), F32)
                for r0 in range(0, tm, CONV_ROWS):
                    prod = (scd[c, pl.ds(r0, CONV_ROWS), :]
                            * scr[c, pl.ds(r0 + CONV_HALO - (CONV_K - 1) + j, CONV_ROWS), :])
                    dwj = dwj + jnp.sum(prod.reshape(CONV_ROWS // SUBLANE_ROWS, SUBLANE_ROWS, LANES), axis=0)
                dw_ref[j:j + 1, _lanes(c)] += _colsum(dwj)
        dbdw_ref[...] += _colsum(du1)
        dg_ref[...] += _colsum(du2 * xh)
        dbl_ref[...] += _colsum(du2)

    cur = lambda cb: pl.BlockSpec((tm, 512), lambda i: (i, cb))
    prev = lambda cb: pl.BlockSpec((CONV_HALO, 512), lambda i: (jnp.maximum(i * per - 1, 0), cb))
    nxt = pl.BlockSpec((CONV_HALO, 512), lambda i: (jnp.minimum((i + 1) * per, t // CONV_HALO - 1), 0))
    whole = lambda a: pl.BlockSpec(a.shape, lambda i: (0, 0))
    acc = lambda r, w: pl.BlockSpec((r, w), lambda i: (0, 0))
    res = _call(
        body, grid=(t // tm,),
        in_specs=[cur(0), cur(0), nxt, nxt, cur(3), cur(4), prev(3), prev(4), whole(w_dw), whole(g_ln), whole(b_ln)],
        out_specs=[pl.BlockSpec((tm, 1024), lambda i: (i, 0)), acc(CONV_K, 512), acc(1, 512), acc(1, 512),
                   acc(1, 512), acc(1, 1024)],
        out_shape=[_sds((t, 1024), BF16), _sds((CONV_K, 512), F32), _sds((1, 512), F32), _sds((1, 512), F32),
                   _sds((1, 512), F32), _sds((1, 1024), F32)],
        scratch_shapes=[pltpu.VMEM((n_chunks, CONV_HALO + tm, LANES), F32),
                        pltpu.VMEM((n_chunks, tm + CONV_HALO, LANES), F32)],
        sem=("arbitrary",), name=name, args=(du3, u1, du3, u1, z, z, z, z, w_dw, g_ln, b_ln), carried=carried)
    return res[:6] if carried is None else (res[:6], res[6:])


FF_BLOCK = D_FF // 2
FF_HALO = 8
FF_CHUNKS = FF_BLOCK // LANES


FF_ROWS = 64
FF_EXT_ROWS = 88


def _ffn_conv(w_ref, b_ref, scr, k, rows, r0=0):
    acc = b_ref[:, _lanes(k)] + w_ref[0:1, _lanes(k)] * scr[k, pl.ds(r0 + FF_HALO - 2, rows), :]
    acc = acc + w_ref[1:2, _lanes(k)] * scr[k, pl.ds(r0 + FF_HALO - 1, rows), :]
    return acc + w_ref[2:3, _lanes(k)] * scr[k, pl.ds(r0 + FF_HALO, rows), :]


def _ffn_act(up, w3, b3, name, tm=ROW_TILE):
    t = up.shape[0]
    per = tm // FF_HALO
    wide = 2 * FF_BLOCK

    def body(u_ref, uh_ref, w_ref, b_ref, o_ref, scr):
        i = pl.program_id(1)
        for k in range(2 * FF_CHUNKS):
            scr[k, 0:FF_HALO, :] = jnp.where(i > 0, uh_ref[:, _lanes(k)], 0.0)
            scr[k, FF_HALO:FF_HALO + tm, :] = u_ref[:, _lanes(k)]
        for cc in range(FF_CHUNKS):
            for r0 in range(0, tm, FF_ROWS):
                val = _ffn_conv(w_ref, b_ref, scr, cc, FF_ROWS, r0)
                gate = _ffn_conv(w_ref, b_ref, scr, FF_CHUNKS + cc, FF_ROWS, r0)
                o_ref[r0:r0 + FF_ROWS, _lanes(cc)] = (_gelu(gate) * val).astype(BF16)

    return pl.pallas_call(
        body, grid=(2, t // tm),
        in_specs=[pl.BlockSpec((tm, wide), lambda c, i: (i, c)),
                  pl.BlockSpec((FF_HALO, wide), lambda c, i: (jnp.maximum(i * per - 1, 0), c)),
                  pl.BlockSpec((FFN_K, wide), lambda c, i: (0, c)),
                  pl.BlockSpec((1, wide), lambda c, i: (0, c))],
        out_specs=pl.BlockSpec((tm, FF_BLOCK), lambda c, i: (i, c)),
        out_shape=_sds((t, D_FF), BF16),
        scratch_shapes=[pltpu.VMEM((2 * FF_CHUNKS, FF_HALO + tm, LANES), F32)],
        compiler_params=_params(("arbitrary", "arbitrary")), name=name,
    )(up, up, w3, b3)


def _ffn_act_bwd(dact, up, w3, b3, name, tm=ROW_TILE):
    t = up.shape[0]
    per = tm // FF_HALO
    wide = 2 * FF_BLOCK
    last = t // tm - 1
    ext = tm + FF_HALO

    def body(u_ref, up_ref, un_ref, d_ref, dn_ref, w_ref, b_ref, o_ref, dw_ref, db_ref, scr, scd):
        i = pl.program_id(1)
        for k in range(2 * FF_CHUNKS):
            scr[k, 0:FF_HALO, :] = jnp.where(i > 0, up_ref[:, _lanes(k)], 0.0)
            scr[k, FF_HALO:FF_HALO + tm, :] = u_ref[:, _lanes(k)]
            scr[k, FF_HALO + tm:FF_HALO + ext, :] = un_ref[:, _lanes(k)]
        dn = jnp.where(i < last, dn_ref[...], 0.0)

        @pl.when(i == 0)
        def _():
            dw_ref[...] = jnp.zeros_like(dw_ref)
            db_ref[...] = jnp.zeros_like(db_ref)

        for cc in range(FF_CHUNKS):
            gc = FF_CHUNKS + cc
            for r0 in range(0, ext, FF_EXT_ROWS):
                rows = pl.ds(r0, FF_EXT_ROWS)
                val = _ffn_conv(w_ref, b_ref, scr, cc, FF_EXT_ROWS, r0)
                gel, dgel = _gelu_and_grad(_ffn_conv(w_ref, b_ref, scr, gc, FF_EXT_ROWS, r0))
                da = d_ref[r0:r0 + FF_EXT_ROWS, _lanes(cc)] if r0 + FF_EXT_ROWS <= tm else jnp.concatenate(
                    [d_ref[r0:tm, _lanes(cc)], dn[:, _lanes(cc)]], axis=0)
                scd[cc, rows, :] = da * gel
                scd[gc, rows, :] = da * val * dgel
            for k in (cc, gc):
                dwk = [jnp.zeros((SUBLANE_ROWS, LANES), F32) for _ in range(FFN_K)]
                dbk = jnp.zeros((SUBLANE_ROWS, LANES), F32)
                for r0 in range(0, tm, FF_ROWS):
                    shifted = [scd[k, pl.ds(r0 + FFN_K - 1 - j, FF_ROWS), :] for j in range(FFN_K)]
                    ucur = scr[k, pl.ds(r0 + FF_HALO, FF_ROWS), :]
                    o_ref[r0:r0 + FF_ROWS, _lanes(k)] = (
                        w_ref[0:1, _lanes(k)] * shifted[0] + w_ref[1:2, _lanes(k)] * shifted[1]
                        + w_ref[2:3, _lanes(k)] * shifted[2]).astype(BF16)
                    fold = lambda v: jnp.sum(v.reshape(FF_ROWS // SUBLANE_ROWS, SUBLANE_ROWS, LANES), axis=0)
                    for j in range(FFN_K):
                        dwk[j] = dwk[j] + fold(shifted[j] * ucur)
                    dbk = dbk + fold(shifted[FFN_K - 1])
                for j in range(FFN_K):
                    dw_ref[j:j + 1, _lanes(k)] += _colsum(dwk[j])
                db_ref[:, _lanes(k)] += _colsum(dbk)

    nblk = t // FF_HALO
    return pl.pallas_call(
        body, grid=(2, t // tm),
        in_specs=[pl.BlockSpec((tm, wide), lambda c, i: (i, c)),
                  pl.BlockSpec((FF_HALO, wide), lambda c, i: (jnp.maximum(i * per - 1, 0), c)),
                  pl.BlockSpec((FF_HALO, wide), lambda c, i: (jnp.minimum((i + 1) * per, nblk - 1), c)),
                  pl.BlockSpec((tm, FF_BLOCK), lambda c, i: (i, c)),
                  pl.BlockSpec((FF_HALO, FF_BLOCK), lambda c, i: (jnp.minimum((i + 1) * per, nblk - 1), c)),
                  pl.BlockSpec((FFN_K, wide), lambda c, i: (0, c)),
                  pl.BlockSpec((1, wide), lambda c, i: (0, c))],
        out_specs=[pl.BlockSpec((tm, wide), lambda c, i: (i, c)),
                   pl.BlockSpec((FFN_K, wide), lambda c, i: (0, c)),
                   pl.BlockSpec((1, wide), lambda c, i: (0, c))],
        out_shape=[_sds((t, 2 * D_FF), BF16), _sds((FFN_K, 2 * D_FF), F32), _sds((1, 2 * D_FF), F32)],
        scratch_shapes=[pltpu.VMEM((2 * FF_CHUNKS, FF_HALO + ext, LANES), F32),
                        pltpu.VMEM((2 * FF_CHUNKS, ext, LANES), F32)],
        compiler_params=_params(("arbitrary", "arbitrary")), name=name,
    )(up, up, up, dact, dact, w3, b3)


def _toeplitz_map():
    f = np.zeros((TOEP, REL_PAD), np.float32)
    for m in range(TOEP - 1):
        rel = (WINDOW - 1) - m
        f[m, int(np.clip(rel, -MAX_REL, MAX_REL)) + MAX_REL] = 1.0
    return f


def _split3(v):
    hi = v.astype(BF16)
    r1 = v - hi.astype(F32)
    mid = r1.astype(BF16)
    lo = (r1 - mid.astype(F32)).astype(BF16)
    return hi, mid, lo


def _exact_select(v, sel):
    out = None
    for part in _split3(v):
        p = jnp.dot(part, sel, preferred_element_type=F32)
        out = p if out is None else out + p
    return out


def _select_call(v, sel, name):
    def body(v_ref, s_ref, o_ref):
        o_ref[...] = _exact_select(v_ref[...], s_ref[...])
    return pl.pallas_call(body, out_shape=_sds((v.shape[0], sel.shape[1]), F32), name=name)(v, sel)


def _band_bias(gen_row):
    b0 = jnp.broadcast_to(gen_row, (Q_TILE, TOEP))
    bias = pltpu.roll(b0, TOEP - 255, 1, stride=1, stride_axis=0)[:, :WINDOW]
    qq = lax.broadcasted_iota(jnp.int32, (Q_TILE, WINDOW), 0) // CHUNK
    kc = lax.broadcasted_iota(jnp.int32, (Q_TILE, WINDOW), 1) // CHUNK
    return jnp.where((kc >= qq) & (kc <= qq + LEFT_CHUNKS), bias, NEG_INF)


PAD_ROWS = WINDOW - Q_TILE
NT_DIMS = (((1,), (1,)), ((), ()))
TN_DIMS = (((0,), (0,)), ((), ()))


def _head_mask(hh):
    lane = lax.broadcasted_iota(jnp.int32, (1, 128), 1)
    return (lane < 64) if hh == 0 else (lane >= 64)


SOFTMAX_ROWS = 16


def _probs_block(s_scr, bias, hh, rows, i):
    s = s_scr[rows, :] + bias[hh, rows, :]
    col = lax.broadcasted_iota(jnp.int32, (SOFTMAX_ROWS, WINDOW), 1)
    s = jnp.where(col >= PAD_ROWS - Q_TILE * i, s, NEG_INF)
    p = jnp.exp(s - jnp.max(s, axis=-1, keepdims=True))
    return p / jnp.sum(p, axis=-1, keepdims=True)


def _attention(z, gen, name, carried=None):
    t = z.shape[0]
    n_i = t // Q_TILE

    def body(q_ref, k_ref, v_ref, g_ref, o_ref, kpad, vpad, bias, s_scr, p_scr):
        hp, i = pl.program_id(0), pl.program_id(1)

        @pl.when(i == 0)
        def _():
            kpad[0:PAD_ROWS, :] = jnp.zeros((PAD_ROWS, 128), BF16)
            vpad[0:PAD_ROWS, :] = jnp.zeros((PAD_ROWS, 128), BF16)
            kpad[PAD_ROWS:PAD_ROWS + t, :] = k_ref[...].astype(BF16)
            vpad[PAD_ROWS:PAD_ROWS + t, :] = v_ref[...].astype(BF16)
            for hh in range(2):
                bias[hh] = _band_bias(g_ref[pl.ds(2 * hp + hh, 1), :])

        start = pl.multiple_of(i * Q_TILE, Q_TILE)
        kw = kpad[pl.ds(start, WINDOW), :]
        vw = vpad[pl.ds(start, WINDOW), :]
        q = q_ref[...] * (CHUNK ** -0.5)
        out = None
        for hh in range(2):
            mask = _head_mask(hh)
            qm = jnp.where(mask, q, 0.0).astype(BF16)
            s_scr[hh] = lax.dot_general(qm, kw, NT_DIMS, preferred_element_type=F32)
            for r0 in range(0, Q_TILE, SOFTMAX_ROWS):
                rows = slice(r0, r0 + SOFTMAX_ROWS)
                p_scr[hh, rows, :] = _probs_block(s_scr.at[hh], bias, hh, rows, i).astype(BF16)
            o = jnp.dot(p_scr[hh], vw, preferred_element_type=F32)
            out = jnp.where(mask, o, 0.0) if out is None else jnp.where(mask, o, out)
        o_ref[...] = out.astype(BF16)

    res = _call(
        body, grid=(4, n_i),
        in_specs=[pl.BlockSpec((Q_TILE, 128), lambda h, i: (i, h)),
                  pl.BlockSpec((t, 128), lambda h, i: (0, 4 + h)),
                  pl.BlockSpec((t, 128), lambda h, i: (0, 8 + h)),
                  pl.BlockSpec((N_HEADS, TOEP), lambda h, i: (0, 0))],
        out_specs=[pl.BlockSpec((Q_TILE, 128), lambda h, i: (i, h))],
        out_shape=[_sds((t, 512), BF16)],
        scratch_shapes=[pltpu.VMEM((PAD_ROWS + t, 128), BF16), pltpu.VMEM((PAD_ROWS + t, 128), BF16),
                        pltpu.VMEM((2, Q_TILE, WINDOW), F32), pltpu.VMEM((2, Q_TILE, WINDOW), F32),
                        pltpu.VMEM((2, Q_TILE, WINDOW), BF16)],
        sem=("arbitrary", "arbitrary"), name=name, args=(z, z, z, gen), carried=carried)
    return res[0] if carried is None else (res[0], res[1:])


def _attention_bwd(z, datt, gen, name, carried=None):
    t = z.shape[0]
    n_i = t // Q_TILE

    def body(q_ref, k_ref, v_ref, d_ref, g_ref, dq_ref, dk_ref, dv_ref, sq_ref, sk_ref, sv_ref, dg_ref,
             kpad, vpad, dkacc, dvacc, bias, dsacc, s_scr, dp_scr, p_scr, ds_scr):
        hp, i = pl.program_id(0), pl.program_id(1)

        @pl.when(i == 0)
        def _():
            kpad[0:PAD_ROWS, :] = jnp.zeros((PAD_ROWS, 128), BF16)
            vpad[0:PAD_ROWS, :] = jnp.zeros((PAD_ROWS, 128), BF16)
            kpad[PAD_ROWS:PAD_ROWS + t, :] = k_ref[...].astype(BF16)
            vpad[PAD_ROWS:PAD_ROWS + t, :] = v_ref[...].astype(BF16)
            dkacc[...] = jnp.zeros_like(dkacc)
            dvacc[...] = jnp.zeros_like(dvacc)
            dsacc[...] = jnp.zeros_like(dsacc)
            for hh in range(2):
                bias[hh] = _band_bias(g_ref[pl.ds(2 * hp + hh, 1), :])

        start = pl.multiple_of(i * Q_TILE, Q_TILE)
        win = pl.ds(start, WINDOW)
        kw = kpad[win, :]
        vw = vpad[win, :]
        q = q_ref[...] * (CHUNK ** -0.5)
        do = d_ref[...]
        dq = None
        for hh in range(2):
            mask = _head_mask(hh)
            qm = jnp.where(mask, q, 0.0).astype(BF16)
            dom = jnp.where(mask, do, 0.0).astype(BF16)
            s_scr[...] = lax.dot_general(qm, kw, NT_DIMS, preferred_element_type=F32)
            dp_scr[...] = lax.dot_general(dom, vw, NT_DIMS, preferred_element_type=F32)
            for r0 in range(0, Q_TILE, SOFTMAX_ROWS):
                rows = slice(r0, r0 + SOFTMAX_ROWS)
                p = _probs_block(s_scr, bias, hh, rows, i)
                dp = dp_scr[rows, :]
                ds = p * (dp - jnp.sum(p * dp, axis=-1, keepdims=True))
                dsacc[hh, rows, :] += ds
                ds_scr[rows, :] = ds.astype(BF16)
                p_scr[rows, :] = p.astype(BF16)
            ds16 = ds_scr[...]
            dqh = jnp.dot(ds16, kw, preferred_element_type=F32) * (CHUNK ** -0.5)
            dq = jnp.where(mask, dqh, 0.0) if dq is None else jnp.where(mask, dqh, dq)
            dkacc[win, :] += lax.dot_general(ds16, qm, TN_DIMS, preferred_element_type=F32)
            dvacc[win, :] += lax.dot_general(p_scr[...], dom, TN_DIMS, preferred_element_type=F32)
        dq_ref[...] = dq.astype(BF16)

        @pl.when(i == 0)
        def _():
            sq_ref[...] = _colsum(dq)

        @pl.when(i > 0)
        def _():
            sq_ref[...] += _colsum(dq)

        @pl.when(i == n_i - 1)
        def _():
            dk = dkacc[PAD_ROWS:PAD_ROWS + t, :]
            dv = dvacc[PAD_ROWS:PAD_ROWS + t, :]
            dk_ref[...] = dk.astype(BF16)
            dv_ref[...] = dv.astype(BF16)
            sk_ref[...] = _colsum(dk)
            sv_ref[...] = _colsum(dv)
            rr = lax.broadcasted_iota(jnp.int32, (Q_TILE, Q_TILE), 0)
            cc = lax.broadcasted_iota(jnp.int32, (Q_TILE, Q_TILE), 1)
            rev = jnp.where(rr + cc == Q_TILE - 1, 1.0, 0.0).astype(BF16)
            for hh in range(2):
                acc = None
                for part in _split3(dsacc[hh]):
                    pr = jnp.dot(rev, part, preferred_element_type=F32)
                    acc = pr if acc is None else acc + pr
                wide = jnp.concatenate([acc, jnp.zeros((Q_TILE, TOEP - WINDOW), F32)], axis=1)
                dg_ref[pl.ds(2 * hp + hh, 1), :] = _colsum(pltpu.roll(wide, 0, 1, stride=1, stride_axis=0))

    col = lambda off: pl.BlockSpec((t, 128), lambda h, i: (0, off + h))
    tile = lambda: pl.BlockSpec((Q_TILE, 128), lambda h, i: (i, h))
    sums = lambda: pl.BlockSpec((1, 128), lambda h, i: (0, h))
    res = _call(
        body, grid=(4, n_i),
        in_specs=[tile(), col(4), col(8), tile(), pl.BlockSpec((N_HEADS, TOEP), lambda h, i: (0, 0))],
        out_specs=[tile(), col(0), col(0), sums(), sums(), sums(), pl.BlockSpec((N_HEADS, TOEP), lambda h, i: (0, 0))],
        out_shape=[_sds((t, 512), BF16)] * 3 + [_sds((1, 512), F32)] * 3 + [_sds((N_HEADS, TOEP), F32)],
        scratch_shapes=[pltpu.VMEM((PAD_ROWS + t, 128), BF16), pltpu.VMEM((PAD_ROWS + t, 128), BF16),
                        pltpu.VMEM((PAD_ROWS + t, 128), F32), pltpu.VMEM((PAD_ROWS + t, 128), F32),
                        pltpu.VMEM((2, Q_TILE, WINDOW), F32), pltpu.VMEM((2, Q_TILE, WINDOW), F32),
                        pltpu.VMEM((Q_TILE, WINDOW), F32), pltpu.VMEM((Q_TILE, WINDOW), F32),
                        pltpu.VMEM((Q_TILE, WINDOW), BF16), pltpu.VMEM((Q_TILE, WINDOW), BF16)],
        sem=("arbitrary", "arbitrary"), name=name, args=(z, z, z, datt, gen), carried=carried)
    return res[:7] if carried is None else (res[:7], res[7:])


def _adamw_math(w, g, m, v):
    m = ADAM_B1 * m + (1.0 - ADAM_B1) * g
    v = ADAM_B2 * v + (1.0 - ADAM_B2) * (g * g)
    m_hat = m / (1.0 - ADAM_B1 ** ADAM_STEP)
    v_hat = v / (1.0 - ADAM_B2 ** ADAM_STEP)
    delta = -ADAM_LR * (m_hat / (jnp.sqrt(v_hat) + ADAM_EPS) + ADAM_WD * w)
    return delta, m, v


def _adamw_many(items, name):
    n = len(items)

    def body(*refs):
        ins, outs = refs[:4 * n], refs[4 * n:]
        for k in range(n):
            w, g, m, v = (r[...] for r in ins[4 * k:4 * k + 4])
            outs[3 * k][...], outs[3 * k + 1][...], outs[3 * k + 2][...] = _adamw_math(w, g, m, v)

    flat = [a for item in items for a in item]
    res = pl.pallas_call(body, out_shape=[_sds(item[0].shape, F32) for item in items for _ in range(3)],
                         name=name)(*flat)
    return [tuple(res[3 * k:3 * k + 3]) for k in range(n)]


def _adamw(w, g, m, v, name):
    r, c = w.shape
    tm = next(cand for cand in (256, 176, 128, 64, 32, 16, 8) if r % cand == 0)
    return _rowcall(lambda wv, gv, mv, vv: (gv,) + _adamw_math(wv, gv, mv, vv),
                    [(w, c, 0), (g, c, 0), (m, c, 0), (v, c, 0)], [], [(r, c, F32, c)] * 4, [], name=name, tm=tm)


def _ada_fwd(c_all, w_shard, b_shard, name):
    n = w_shard.shape[1]
    tn = 512

    def body(c_ref, w_ref, b_ref, o_ref, a_ref):
        cv = c_ref[...]
        act = cv * _sigmoid(cv)
        a_ref[...] = act
        o_ref[...] = jnp.dot(act.astype(BF16), w_ref[...].astype(BF16), preferred_element_type=F32) + b_ref[...]

    return pl.pallas_call(
        body, grid=(n // tn,),
        in_specs=[pl.BlockSpec((8, D), lambda j: (0, 0)), pl.BlockSpec((D, tn), lambda j: (0, j)),
                  pl.BlockSpec((1, tn), lambda j: (0, j))],
        out_specs=[pl.BlockSpec((8, tn), lambda j: (0, j)), pl.BlockSpec((8, D), lambda j: (0, 0))],
        out_shape=[_sds((8, n), F32), _sds((8, D), F32)],
        compiler_params=_params(("arbitrary",)), name=name,
    )(c_all, w_shard, b_shard)


def _ada_bwd_adamw(act_t, dmod_shard, w, m, v, name):
    r, c = w.shape
    tm = 256

    def body(a_ref, d_ref, w_ref, m_ref, v_ref, g_ref, dl_ref, nm_ref, nv_ref):
        g = jnp.dot(a_ref[...], d_ref[...], precision=lax.Precision.HIGHEST, preferred_element_type=F32)
        g_ref[...] = g
        dl_ref[...], nm_ref[...], nv_ref[...] = _adamw_math(w_ref[...], g, m_ref[...], v_ref[...])

    blk = pl.BlockSpec((tm, c), lambda i: (i, 0))
    return pl.pallas_call(
        body, grid=(r // tm,),
        in_specs=[pl.BlockSpec((tm, 8), lambda i: (i, 0)), pl.BlockSpec((8, c), lambda i: (0, 0)), blk, blk, blk],
        out_specs=[blk] * 4, out_shape=[_sds((r, c), F32)] * 4,
        compiler_params=_params(("arbitrary",)), name=name,
    )(act_t, dmod_shard, w, m, v)


def _place():
    return lax.axis_index("x"), lax.axis_index("y"), lax.axis_index("c")


def _flip(v, bit):
    return 1 - v if bit else v


VMEM_SPEC = pl.BlockSpec(memory_space=pltpu.VMEM)


def _allgather8(v, name):
    r, c = v.shape

    def body(v_ref, g_ref, tot_ref, send_sems, recv_sems, local_sem):
        x, y, cc = _place()
        sibling = (x, y, 1 - cc)
        chips = [(_flip(x, k & 2), _flip(y, k & 1)) for k in (1, 2, 3)]

        def block(px, py, pc):
            return g_ref.at[4 * px + 2 * py + pc]

        def copy(k, place, to, src=None):
            slot = block(*place)
            return pltpu.make_async_remote_copy(src_ref=slot if src is None else src, dst_ref=slot,
                                                send_sem=send_sems.at[k], recv_sem=recv_sems.at[k],
                                                device_id=to, device_id_type=MESH)

        mine = pltpu.make_async_copy(v_ref, block(x, y, cc), local_sem)
        mine.start()
        first = [copy(0, (x, y, cc), sibling, src=v_ref)]
        first += [copy(1 + j, (x, y, cc), (px, py, cc), src=v_ref) for j, (px, py) in enumerate(chips)]
        for cp in first:
            cp.start()
        passed = [copy(4 + j, (px, py, cc), sibling) for j, (px, py) in enumerate(chips)]
        for j, (px, py) in enumerate(chips):
            copy(1 + j, (px, py, cc), (x, y, cc)).wait_recv()
            passed[j].start()
        copy(0, sibling, (x, y, cc)).wait_recv()
        for j, (px, py) in enumerate(chips):
            copy(4 + j, (px, py, 1 - cc), (x, y, cc)).wait_recv()
        for cp in first + passed:
            cp.wait_send()
        mine.wait()
        tot = g_ref[0]
        for d in range(1, 8):
            tot = tot + g_ref[d]
        tot_ref[...] = tot

    return pl.pallas_call(
        body, in_specs=[VMEM_SPEC], out_specs=[VMEM_SPEC, VMEM_SPEC],
        out_shape=[_sds((8, r, c), F32), _sds((r, c), F32)],
        scratch_shapes=[pltpu.SemaphoreType.DMA((7,)), pltpu.SemaphoreType.DMA((7,)), pltpu.SemaphoreType.DMA],
        compiler_params=pltpu.CompilerParams(vmem_limit_bytes=VMEM_LIMIT), name=name,
    )(v)


def _slot(px, py, swapped):
    return 2 * py + px if swapped else 2 * px + py


def _gather_shards(arrs, swapped, name, in_place=False):
    n = len(arrs)

    def body(*refs):
        ins, outs = refs[:n], refs[n:2 * n]
        send1, recv1, send2, recv2, local_sems = refs[2 * n:]
        x, y, c = _place()
        sibling = (x, y, 1 - c)
        chips = [(_flip(x, k & 2), _flip(y, k & 1)) for k in (1, 2, 3)]
        local_copies, sends = [], []
        for a in range(n):
            h = outs[a].shape[1] // 2
            mine = pl.ds(pl.multiple_of(c * h, 8), h)
            own = _slot(x, y, swapped[a])
            if in_place:
                src = outs[a].at[own, mine]
            else:
                src = ins[a].at[mine]
                lc = pltpu.make_async_copy(ins[a], outs[a].at[own], local_sems.at[a])
                lc.start()
                local_copies.append(lc)
            for j, (px, py) in enumerate(chips):
                cp = pltpu.make_async_remote_copy(
                    src_ref=src, dst_ref=outs[a].at[own, mine], send_sem=send1.at[3 * a + j],
                    recv_sem=recv1.at[3 * a + j], device_id=(px, py, c), device_id_type=MESH)
                cp.start()
                sends.append(cp)
        for a in range(n):
            h = outs[a].shape[1] // 2
            mine = pl.ds(pl.multiple_of(c * h, 8), h)
            for j, (px, py) in enumerate(chips):
                piece = outs[a].at[_slot(px, py, swapped[a]), mine]
                pltpu.make_async_remote_copy(
                    src_ref=piece, dst_ref=piece, send_sem=send1.at[3 * a + j], recv_sem=recv1.at[3 * a + j],
                    device_id=(px, py, c), device_id_type=MESH).wait_recv()
                fwd = pltpu.make_async_remote_copy(
                    src_ref=piece, dst_ref=piece, send_sem=send2.at[3 * a + j], recv_sem=recv2.at[3 * a + j],
                    device_id=sibling, device_id_type=MESH)
                fwd.start()
                sends.append(fwd)
        for a in range(n):
            h = outs[a].shape[1] // 2
            other = pl.ds(pl.multiple_of((1 - c) * h, 8), h)
            for j, (px, py) in enumerate(chips):
                piece = outs[a].at[_slot(px, py, swapped[a]), other]
                pltpu.make_async_remote_copy(
                    src_ref=piece, dst_ref=piece, send_sem=send2.at[3 * a + j], recv_sem=recv2.at[3 * a + j],
                    device_id=sibling, device_id_type=MESH).wait_recv()
        for cp in sends:
            cp.wait_send()
        for lc in local_copies:
            lc.wait()

    dma = lambda k: pltpu.SemaphoreType.DMA((k,))
    return pl.pallas_call(
        body, in_specs=[ANY] * n, out_specs=[ANY] * n,
        out_shape=[_sds(a.shape if in_place else (4,) + a.shape, a.dtype) for a in arrs],
        scratch_shapes=[dma(3 * n), dma(3 * n), dma(3 * n), dma(3 * n), dma(n)],
        input_output_aliases={a: a for a in range(n)} if in_place else {},
        name=name,
    )(*arrs)


def _carry_pair_exchange(grads):
    n = len(grads)

    def copies(ins, outs, send_sems, recv_sems):
        x, y, c = _place()
        cps = []
        for a in range(n):
            h = ins[a].shape[1] // 2
            theirs = pl.ds(pl.multiple_of((1 - c) * h, 8), h)
            cps.append(pltpu.make_async_remote_copy(
                src_ref=ins[a].at[:, theirs, :], dst_ref=outs[a], send_sem=send_sems.at[a], recv_sem=recv_sems.at[a],
                device_id=(x, y, 1 - c), device_id_type=MESH))
        return cps

    def start(*refs):
        for cp in copies(*refs):
            cp.start()

    def finish(*refs):
        for cp in copies(*refs):
            cp.wait()

    return _Carried(grads, [_sds((4, g.shape[1] // 2, g.shape[2]), F32) for g in grads], {}, n, start, finish)


def _row_steps(h):
    return 1


def _pair_sum(grad, recv, core, name):
    _, r, c = grad.shape
    h = r // 2
    nr = _row_steps(h)
    th = h // nr

    def body(core_ref, g_ref, r_ref, o_ref):
        o_ref[...] = (g_ref[...] + r_ref[...]).astype(BF16)

    return pl.pallas_call(
        body,
        grid_spec=pltpu.PrefetchScalarGridSpec(
            num_scalar_prefetch=1, grid=(4, nr),
            in_specs=[pl.BlockSpec((None, th, c), lambda s, q, core_ref: (s, core_ref[0] * nr + q, 0)),
                      pl.BlockSpec((None, th, c), lambda s, q, core_ref: (s, q, 0))],
            out_specs=pl.BlockSpec((None, th, c), lambda s, q, core_ref: (s, q, 0))),
        out_shape=_sds((4, h, c), BF16), compiler_params=_params(("arbitrary", "arbitrary")), name=name,
    )(core, grad, recv)


def _carry_chip_exchange(parts, swapped):
    n = len(parts)

    def copies(ins, outs, send_sems, recv_sems):
        x, y, c = _place()
        chips = [(_flip(x, k & 2), _flip(y, k & 1)) for k in (1, 2, 3)]
        cps = []
        for a in range(n):
            for j, (px, py) in enumerate(chips):
                cps.append(pltpu.make_async_remote_copy(
                    src_ref=ins[a].at[_slot(px, py, swapped[a])], dst_ref=outs[a].at[j],
                    send_sem=send_sems.at[3 * a + j], recv_sem=recv_sems.at[3 * a + j],
                    device_id=(px, py, c), device_id_type=MESH))
        return cps

    def start(*refs):
        for cp in copies(*refs):
            cp.start()

    def finish(*refs):
        for cp in copies(*refs):
            cp.wait()

    return _Carried(parts, [_sds((3,) + p.shape[1:], BF16) for p in parts], {}, 3 * n, start, finish)


def _chip_sum(part, recv, slot_core, name):
    _, h, c = part.shape
    nr = _row_steps(h)
    th = h // nr

    def body(sc_ref, p_ref, r_ref, o_ref):
        acc = p_ref[...].astype(F32)
        for j in range(3):
            acc = acc + r_ref[j].astype(F32)
        o_ref[...] = acc

    return pl.pallas_call(
        body,
        grid_spec=pltpu.PrefetchScalarGridSpec(
            num_scalar_prefetch=1, grid=(nr,),
            in_specs=[pl.BlockSpec((None, th, c), lambda q, sc_ref: (sc_ref[0], q, 0)),
                      pl.BlockSpec((3, th, c), lambda q, sc_ref: (0, q, 0))],
            out_specs=pl.BlockSpec((th, c), lambda q, sc_ref: (sc_ref[1] * nr + q, 0))),
        out_shape=_sds((2 * h, c), F32), compiler_params=_params(("arbitrary",)), name=name,
    )(slot_core, part, recv)


def _carry_pair_share(shards):
    n = len(shards)

    def copies(outs, send_sems, recv_sems, mine):
        x, y, c = _place()
        cps = []
        for a in range(n):
            h = outs[a].shape[0] // 2
            half = outs[a].at[pl.ds(pl.multiple_of((c if mine else 1 - c) * h, 8), h)]
            cps.append(pltpu.make_async_remote_copy(
                src_ref=half, dst_ref=half, send_sem=send_sems.at[a], recv_sem=recv_sems.at[a],
                device_id=(x, y, 1 - c), device_id_type=MESH))
        return cps

    def start(ins, outs, send_sems, recv_sems):
        for cp in copies(outs, send_sems, recv_sems, True):
            cp.start()

    def finish(ins, outs, send_sems, recv_sems):
        for cp in copies(outs, send_sems, recv_sems, False):
            cp.wait_recv()
        for cp in copies(outs, send_sems, recv_sems, True):
            cp.wait_send()

    return _Carried(shards, [_sds(s.shape, F32) for s in shards], {a: a for a in range(n)}, n, start, finish)


def _carry_gather_ici(bufs, swapped):
    n = len(bufs)

    def copies(outs, send_sems, recv_sems, sending):
        x, y, c = _place()
        cps = []
        for a in range(n):
            h = outs[a].shape[1] // 2
            mine = pl.ds(pl.multiple_of(c * h, 8), h)
            for j, k in enumerate((1, 2, 3)):
                px, py = _flip(x, k & 2), _flip(y, k & 1)
                slot = _slot(x, y, swapped[a]) if sending else _slot(px, py, swapped[a])
                piece = outs[a].at[slot, mine]
                cps.append(pltpu.make_async_remote_copy(
                    src_ref=piece, dst_ref=piece, send_sem=send_sems.at[3 * a + j], recv_sem=recv_sems.at[3 * a + j],
                    device_id=(px, py, c), device_id_type=MESH))
        return cps

    def start(ins, outs, send_sems, recv_sems):
        for cp in copies(outs, send_sems, recv_sems, True):
            cp.start()

    def finish(ins, outs, send_sems, recv_sems):
        for cp in copies(outs, send_sems, recv_sems, False):
            cp.wait_recv()
        for cp in copies(outs, send_sems, recv_sems, True):
            cp.wait_send()

    return _Carried(bufs, [_sds(b.shape, b.dtype) for b in bufs], {a: a for a in range(n)}, 3 * n, start, finish)


def _carry_gather_forward(bufs, swapped):
    n = len(bufs)

    def copies(outs, send_sems, recv_sems, sending):
        x, y, c = _place()
        cps = []
        for a in range(n):
            h = outs[a].shape[1] // 2
            rows = pl.ds(pl.multiple_of((c if sending else 1 - c) * h, 8), h)
            for j, k in enumerate((1, 2, 3)):
                piece = outs[a].at[_slot(_flip(x, k & 2), _flip(y, k & 1), swapped[a]), rows]
                cps.append(pltpu.make_async_remote_copy(
                    src_ref=piece, dst_ref=piece, send_sem=send_sems.at[3 * a + j], recv_sem=recv_sems.at[3 * a + j],
                    device_id=(x, y, 1 - c), device_id_type=MESH))
        return cps

    def start(ins, outs, send_sems, recv_sems):
        for cp in copies(outs, send_sems, recv_sems, True):
            cp.start()

    def finish(ins, outs, send_sems, recv_sems):
        for cp in copies(outs, send_sems, recv_sems, False):
            cp.wait_recv()
        for cp in copies(outs, send_sems, recv_sems, True):
            cp.wait_send()

    return _Carried(bufs, [_sds(b.shape, b.dtype) for b in bufs], {a: a for a in range(n)}, 3 * n, start, finish)


def _pack(arrs, rows_multiple=8):
    parts, offs, row = [], [], 0
    for a in arrs:
        flat = a.reshape(-1)
        nrow = -(-flat.shape[0] // D)
        parts.append(jnp.pad(flat, (0, nrow * D - flat.shape[0])))
        offs.append(row)
        row += nrow
    total = -(-row // rows_multiple) * rows_multiple
    if total > row:
        parts.append(jnp.zeros(((total - row) * D,), F32))
    return jnp.concatenate(parts).reshape(total, D), offs


def _unpack(packed, offs, shapes):
    out = []
    for off, shp in zip(offs, shapes):
        size = int(np.prod(shp))
        nrow = -(-size // D)
        out.append(packed[off:off + nrow].reshape(-1)[:size].reshape(shp))
    return out


def _to_bf16_slot(w, slot, name):
    r, c = w.shape
    tm = next(cand for cand in (256, 176, 128, 64, 32, 16) if r % cand == 0)

    def body(slot_ref, w_ref, o_ref):
        o_ref[...] = w_ref[...].astype(BF16)

    return pl.pallas_call(
        body,
        grid_spec=pltpu.PrefetchScalarGridSpec(
            num_scalar_prefetch=1, grid=(r // tm,),
            in_specs=[pl.BlockSpec((tm, c), lambda i, slot_ref: (i, 0))],
            out_specs=pl.BlockSpec((None, tm, c), lambda i, slot_ref: (slot_ref[0], i, 0))),
        out_shape=_sds((4, r, c), BF16), compiler_params=_params(("arbitrary",)), name=name,
    )(slot, w)


def _unshard_cols(g):
    s, k, n = g.shape
    return jnp.transpose(g, (1, 0, 2)).reshape(k, s * n)


def _ff_swap(v):
    b = FF_BLOCK
    return jnp.concatenate([v[..., 0:b], v[..., 2 * b:3 * b], v[..., b:2 * b], v[..., 3 * b:4 * b]], axis=-1)


LATE = ("attn_o", "conv_o", "mix_o", "up", "down")
EARLY_GRADS = ("down", "up", "mix_o", "attn_o", "conv_o")


def _weight_views(bufs):
    return {"up": bufs["up"], "attn_o": _unshard_cols(bufs["attn_o"]), "conv_o": _unshard_cols(bufs["conv_o"]),
            "mix_o": bufs["mix_o"].reshape(D, D), "down": bufs["down"].reshape(D_FF, D)}


def _pair_sums(names, grads, recv, dist):
    return [_pair_sum(g, r, dist["core"], "pair_sum_" + n) for n, g, r in zip(names, grads, recv)]


def _reduce_halves(names, parts, from_chips, dist):
    return [_chip_sum(p, r, jnp.concatenate([dist["slots"][SWAPPED[n]], dist["core"]]), "chip_sum_" + n)
            for n, p, r in zip(names, parts, from_chips)]


FUSED_TILE = 256
WIDE_TILE = 512


def _gates(z):
    return [(z, 512, 5), (z, 512, 6), (z, 512, 7), (z, 512, 8)]


def _mix_out(a, cb, z, x, w_mix_o, g_post, gt, g_pre2, sc2, sh2, name):
    def lhs(av, cv, ga0, ga1, gb0, gb1):
        ga, gb = jnp.concatenate([ga0, ga1], axis=1), jnp.concatenate([gb0, gb1], axis=1)
        return _sigmoid(ga) * av + _sigmoid(gb) * cv

    def fn(ym, y, xv, gv, gtv, g2v, scv, shv):
        yn, _ = _rms_stats(ym)
        x1 = xv + gtv * (yn * gv)
        xn, _ = _rms_stats(x1)
        return ym, y, x1, (xn * g2v) * (1.0 + scv) + shv

    return _matmul_rows(w_mix_o, form="nn", tm=min(FUSED_TILE, x.shape[0]), tk=D, fn=fn, a_rows=[(a, D, 0), (cb, D, 0)] + _gates(z),
                        a_fn=lhs, rows=[(x, D, 0)], consts=[g_post, gt, g_pre2, sc2, sh2],
                        row_outs=[(F32, D), (BF16, D), (F32, D), (BF16, D)], acc_outs=[], name=name)


def _down_tail(act, w_down, x1, target, g, gt, name):
    def fn(yv, xv, tv, gv, gtv):
        yn, r = _rms_stats(yv)
        e = xv + gtv * (yn * gv) - tv
        dx2 = e * (1.0 / D)
        dyn = dx2 * gtv
        return (dx2, _rms_bwd(dyn * gv, yn, r), _colsum(e * e) * (0.5 / D), _colsum(dyn * yn),
                _colsum(dx2 * (yn * gv)))

    return _matmul_rows(w_down, form="nn", a=act, tm=min(WIDE_TILE, x1.shape[0]), tk=D_FF, fn=fn,
                        rows=[(x1, D, 0), (target, D, 0)], consts=[g, gt], row_outs=[(F32, D), (BF16, D)],
                        acc_outs=[(1, D)] * 3, name=name)


def _up_dx_tail(dup, w_up, x1, dx2, ym, g_pre2, sc2, g_post, gt, name):
    def fn(dh, xv, dov, ymv, g2v, scv, gv, gtv):
        xn, r = _rms_stats(xv)
        dyn = dh * (1.0 + scv)
        dx1 = dov + _rms_bwd(dyn * g2v, xn, r)
        yn, r2 = _rms_stats(ymv)
        dynm = dx1 * gtv
        return (dx1, _rms_bwd(dynm * gv, yn, r2), _colsum(dh), _colsum(dh * (xn * g2v)), _colsum(dyn * xn),
                _colsum(dynm * yn), _colsum(dx1 * (yn * gv)))

    return _matmul_rows(w_up, form="nt", a=dup, tm=min(FUSED_TILE, x1.shape[0]), tk=2 * D_FF, fn=fn,
                        rows=[(x1, D, 0), (dx2, D, 0), (ym, D, 0)], consts=[g_pre2, sc2, g_post, gt],
                        row_outs=[(F32, D), (BF16, D)], acc_outs=[(1, D)] * 5, name=name)


def _mix_dx_gates(dym, w_mix_o, a, cb, z, name):
    def fn(dy, av, cv, ga0, ga1, gb0, gb1):
        sa = _sigmoid(jnp.concatenate([ga0, ga1], axis=1))
        sb = _sigmoid(jnp.concatenate([gb0, gb1], axis=1))
        dcb = dy * sb
        dga = dy * av * (sa * (1.0 - sa))
        dgb = dy * cv * (sb * (1.0 - sb))
        return dy * sa, dcb, dga, dgb, _colsum(dcb), _colsum(dga), _colsum(dgb)

    return _matmul_rows(w_mix_o, form="nt", a=dym, tm=min(FUSED_TILE, a.shape[0]), tk=D, fn=fn,
                        rows=[(a, D, 0), (cb, D, 0)] + _gates(z), consts=[], row_outs=[(BF16, D)] * 4,
                        acc_outs=[(1, D)] * 3, name=name)


def _local_step(x, target, mod, w_in, late, small, dist=None):
    sh_m, sc_m, gt_m, sh_f, sc_f, gt_f = mod
    t = x.shape[0]
    tmm = min(1024, t)
    late_swapped = [SWAPPED[n] for n in LATE]

    h1 = _pre_norm(x, small["g_pre_mix"], sc_m, sh_m, "pre_norm_mix")
    z = _matmul(h1, w_in, form="nn", out_dtype=F32, tm=tmm, tn=1152, tk=D, bias=small["b_in"], name="mm_in")
    if dist is None:
        att = _attention(z, small["gen"], "attention")
        bufs = dict(late)
    else:
        mid = [n for n in LATE if n != "down"]
        mid_swapped = [SWAPPED[n] for n in mid]
        att, landed = _attention(z, small["gen"], "attention",
                                 carried=_carry_gather_ici([late[n] for n in mid], mid_swapped))
        bufs = dict(zip(mid, _run_carried(_carry_gather_forward(landed, mid_swapped), "gather_forward")))
        bufs["down"] = late["down"]
    w = _weight_views(bufs)
    w["in"] = w_in
    a = _matmul(att, w["attn_o"], form="nn", out_dtype=F32, tm=tmm, tn=512, tk=512, name="mm_attn_o")
    u1, u3 = _conv_branch(z, small["w_dw_conv"], small["b_dw_conv"], small["g_conv_ln"], small["b_conv_ln"], "conv_branch")
    cb = _matmul(u3, w["conv_o"], form="nn", out_dtype=F32, tm=tmm, tn=512, tk=512, bias=small["b_conv_o"], name="mm_conv_o")
    ym, y, x1, h2 = _mix_out(a, cb, z, x, w["mix_o"], small["g_post_mix"], gt_m, small["g_pre_ffn"], sc_f, sh_f, "mix_out")
    mm_up = dict(form="nn", out_dtype=F32, tm=tmm, tn=FF_BLOCK, tk=D, name="mm_up")
    if dist is None:
        up = _matmul(h2, w["up"], **mm_up)
    else:
        up, landed = _matmul(h2, w["up"], carried=_carry_gather_ici([late["down"]], [False]), **mm_up)
        w["down"] = _run_carried(_carry_gather_forward(landed, [False]), "gather_forward_down")[0].reshape(D_FF, D)
    act = _ffn_act(up, small["w_dw_ffn"], small["b_dw_ffn"], "ffn_act")

    dx2, dyf, loss_cols, d_g_post_ffn, d_gt_f = _down_tail(act, w["down"], x1, target, small["g_post_ffn"], gt_f, "down_tail")
    dact = _matmul(dyf, w["down"], form="nt", out_dtype=F32, tm=tmm, tn=FF_BLOCK, tk=D, name="mm_down_dx")
    g_down = _matmul(act, dyf, form="tn", out_dtype=F32, tm=FF_BLOCK, tn=512, tk=t, name="mm_down_dw")
    dup, d_w_dw_ffn, d_b_dw_ffn = _ffn_act_bwd(dact, up, small["w_dw_ffn"], small["b_dw_ffn"], "ffn_act_bwd")
    dx1, dym, d_sh_f, d_sc_f, d_g_pre_ffn, d_g_post_mix, d_gt_m = _up_dx_tail(
        dup, w["up"], x1, dx2, ym, small["g_pre_ffn"], sc_f, small["g_post_mix"], gt_m, "up_dx_tail")
    g_up = _matmul(h2, dup, form="tn", out_dtype=F32, tm=512, tn=FF_BLOCK, tk=t, out_sharded=True, name="mm_up_dw")
    da, dcb, dgate_a, dgate_b, d_b_conv_o, sga, sgb = _mix_dx_gates(dym, w["mix_o"], a, cb, z, "mix_dx_gates")
    g_mix_o = _matmul(y, dym, form="tn", out_dtype=F32, tm=D, tn=512, tk=t, name="mm_mix_o_dw")
    datt = _matmul(da, w["attn_o"], form="nt", out_dtype=F32, tm=tmm, tn=512, tk=D, name="mm_attn_o_dx")
    g_attn_o = _matmul(att, da, form="tn", out_dtype=F32, tm=512, tn=256, tk=t, out_sharded=True, name="mm_attn_o_dw")
    du3 = _matmul(dcb, w["conv_o"], form="nt", out_dtype=F32, tm=tmm, tn=512, tk=D, name="mm_conv_o_dx")
    g_conv_o = _matmul(u3, dcb, form="tn", out_dtype=F32, tm=512, tn=256, tk=t, out_sharded=True, name="mm_conv_o_dw")
    big = {"attn_o": g_attn_o, "conv_o": g_conv_o, "mix_o": g_mix_o.reshape(4, 256, D),
           "up": g_up, "down": g_down.reshape(4, D_FF // 4, D)}
    conv_bwd = (du3, u1, z, small["w_dw_conv"], small["g_conv_ln"], small["b_conv_ln"], "conv_branch_bwd")
    in_dw = dict(form="tn", out_dtype=F32, tm=512, tn=1152, tk=t, out_sharded=True, name="mm_in_dw")
    in_dx = dict(form="nt", out_dtype=F32, tm=min(WIDE_TILE, t), tn=D, tk=D_IN, name="mm_in_dx")
    if dist is None:
        dglu, d_w_dw_conv, d_b_dw_conv, d_g_conv_ln, d_b_conv_ln, sglu = _conv_branch_bwd(*conv_bwd)
        dq, dk, dv, sq, sk, sv, dgen = _attention_bwd(z, datt, small["gen"], "attention_bwd")
        dz = jnp.concatenate([dq, dk, dv, dglu, dgate_a, dgate_b], axis=1)
        big["in"] = _matmul(h1, dz, **in_dw)
        dh1 = _matmul(dz, w_in, **in_dx)
    else:
        early = [big[n] for n in EARLY_GRADS]
        (dglu, d_w_dw_conv, d_b_dw_conv, d_g_conv_ln, d_b_conv_ln, sglu), recv = _conv_branch_bwd(
            *conv_bwd, carried=_carry_pair_exchange(early))
        parts = _pair_sums(EARLY_GRADS, early, recv, dist)
        (dq, dk, dv, sq, sk, sv, dgen), from_chips = _attention_bwd(
            z, datt, small["gen"], "attention_bwd",
            carried=_carry_chip_exchange(parts, [SWAPPED[n] for n in EARLY_GRADS]))
        halves = _reduce_halves(EARLY_GRADS, parts, from_chips, dist)
        dz = jnp.concatenate([dq, dk, dv, dglu, dgate_a, dgate_b], axis=1)
        g_in, shards = _matmul(h1, dz, carried=_carry_pair_share(halves), **in_dw)
        big = dict(zip(EARLY_GRADS, shards))
        recv_in = _run_carried(_carry_pair_exchange([g_in]), "pair_exchange_in")
        part_in = _pair_sums(("in",), [g_in], recv_in, dist)
        dh1, from_chips_in = _matmul(dz, w_in, carried=_carry_chip_exchange(part_in, [False]), **in_dx)
        half_in = _reduce_halves(("in",), part_in, from_chips_in, dist)
        big["in"] = _run_carried(_carry_pair_share(half_in), "pair_share_in")[0]
    d_b_in = jnp.concatenate([sq, sk, sv, sglu, sga, sgb], axis=1)
    grad_x, d_sh_m, d_sc_m, d_g_pre_mix = _pre_norm_bwd(dh1, x, dx1, small["g_pre_mix"], sc_m, "pre_norm_mix_bwd")

    dmod = [d_sh_m, d_sc_m, d_gt_m, d_sh_f, d_sc_f, d_gt_f]
    sm = {"g_pre_mix": d_g_pre_mix, "g_post_mix": d_g_post_mix, "b_in": d_b_in, "gen": dgen,
          "w_dw_conv": d_w_dw_conv, "b_dw_conv": d_b_dw_conv, "g_conv_ln": d_g_conv_ln, "b_conv_ln": d_b_conv_ln,
          "b_conv_o": d_b_conv_o, "g_pre_ffn": d_g_pre_ffn, "g_post_ffn": d_g_post_ffn,
          "w_dw_ffn": d_w_dw_ffn, "b_dw_ffn": d_b_dw_ffn}
    return loss_cols, grad_x, dmod, big, sm


BIG = ("in", "attn_o", "conv_o", "mix_o", "up", "down")
SWAPPED = {"in": False, "attn_o": False, "conv_o": False, "mix_o": False, "up": True, "down": False}
SMALL_ORDER = ("b_ada", "g_pre_mix", "g_post_mix", "b_in", "rel_bias", "b_dw_conv", "g_conv_ln", "b_conv_ln",
               "b_conv_o", "g_pre_ffn", "g_post_ffn", "b_dw_ffn", "w_dw_conv", "w_dw_ffn")


def kernel(x, c, w_ada, b_ada, g_pre_mix, g_post_mix, w_in, b_in, rel_bias, w_attn_o, w_dw_conv, b_dw_conv, g_conv_ln, b_conv_ln, w_conv_o, b_conv_o, w_mix_o, g_pre_ffn, g_post_ffn, w_up, w_dw_ffn, b_dw_ffn, w_down, loss_target, m_w_ada, m_b_ada, m_g_pre_mix, m_g_post_mix, m_w_in, m_b_in, m_rel_bias, m_w_attn_o, m_w_dw_conv, m_b_dw_conv, m_g_conv_ln, m_b_conv_ln, m_w_conv_o, m_b_conv_o, m_w_mix_o, m_g_pre_ffn, m_g_post_ffn, m_w_up, m_w_dw_ffn, m_b_dw_ffn, m_w_down, v_w_ada, v_b_ada, v_g_pre_mix, v_g_post_mix, v_w_in, v_b_in, v_rel_bias, v_w_attn_o, v_w_dw_conv, v_b_dw_conv, v_g_conv_ln, v_b_conv_ln, v_w_conv_o, v_b_conv_o, v_w_mix_o, v_g_pre_ffn, v_g_post_ffn, v_w_up, v_w_dw_ffn, v_b_dw_ffn, v_w_down):
    given = dict(locals())
    ax, ay, ac = lax.axis_index("x"), lax.axis_index("y"), lax.axis_index("c")
    shard = 2 * ax + ay
    me = 4 * ax + 2 * ay + ac
    xs, target = x[0], loss_target[0]

    c_pad = jnp.pad(c, ((0, 7), (0, 0)))
    c_g, _ = _allgather8(c_pad, "gather_c")
    c_all = c_g[:, 0, :]
    b_ada_shard = lax.dynamic_slice(b_ada, (0, shard * 1536), (1, 1536))
    mod_shard, c_act = _ada_fwd(c_all, w_ada[0], b_ada_shard, "ada_fwd")
    small_in = [jnp.pad(mod_shard, ((0, 8), (0, 0))),
                jnp.pad(w_dw_conv[0], ((0, 1), (0, 0))),
                jnp.pad(w_dw_ffn[0], ((0, 13), (0, 0)))]
    mod_g, wdc_g, wdf_g = _gather_shards(small_in, [False, False, True], "gather_small")
    mod_all = jnp.transpose(mod_g[:, :8, :], (1, 0, 2)).reshape(8, 6 * D)
    mod_row = lax.dynamic_slice(mod_all, (me, 0), (1, 6 * D))
    mod = [mod_row[:, k * D:(k + 1) * D] for k in range(6)]

    slots = {sw: _slot(ax, ay, sw).astype(jnp.int32).reshape(1) for sw in (False, True)}
    own = {n: _to_bf16_slot(given["w_" + n][0], slots[SWAPPED[n]], "cast_" + n) for n in BIG}
    w_in_all = _gather_shards([own["in"]], [False], "gather_w_in", in_place=True)[0]
    core = ac.astype(jnp.int32).reshape(1)
    dist = {"core": core, "slots": slots}

    sel = jnp.asarray(_toeplitz_map())
    rel_pad = jnp.pad(rel_bias[0], ((0, 0), (0, REL_PAD - (2 * MAX_REL + 1))))
    gen = _select_call(rel_pad, sel.T.astype(BF16), "bias_rows")
    small = {"g_pre_mix": g_pre_mix, "g_post_mix": g_post_mix, "b_in": b_in, "gen": gen,
             "w_dw_conv": _unshard_cols(wdc_g[:, :CONV_K, :]), "b_dw_conv": b_dw_conv, "g_conv_ln": g_conv_ln,
             "b_conv_ln": b_conv_ln, "b_conv_o": b_conv_o, "g_pre_ffn": g_pre_ffn, "g_post_ffn": g_post_ffn,
             "w_dw_ffn": _unshard_cols(wdf_g[:, :FFN_K, :]), "b_dw_ffn": _ff_swap(b_dw_ffn)}

    loss_cols, grad_x, dmod, reduced, sm = _local_step(xs, target, mod, w_in_all, {n: own[n] for n in LATE}, small, dist)

    d_rel = _select_call(sm["gen"], sel.astype(BF16), "bias_fold")[:, :2 * MAX_REL + 1]
    small_grads = {"g_pre_mix": sm["g_pre_mix"], "g_post_mix": sm["g_post_mix"], "b_in": sm["b_in"], "rel_bias": d_rel[None],
                   "b_dw_conv": sm["b_dw_conv"], "g_conv_ln": sm["g_conv_ln"], "b_conv_ln": sm["b_conv_ln"],
                   "b_conv_o": sm["b_conv_o"], "g_pre_ffn": sm["g_pre_ffn"], "g_post_ffn": sm["g_post_ffn"],
                   "b_dw_ffn": _ff_swap(sm["b_dw_ffn"]), "w_dw_conv": sm["w_dw_conv"], "w_dw_ffn": _ff_swap(sm["w_dw_ffn"])}
    order = [n for n in SMALL_ORDER if n != "b_ada"]
    packed, offs = _pack([jnp.concatenate(dmod, axis=1)] + [small_grads[n] for n in order] + [loss_cols])
    every, total = _allgather8(packed, "gather_small_grads")
    loss = jnp.sum(total[offs[-1]])
    offs = offs[:-1]
    dmod_all = every[:, 0:6, :].reshape(8, 6 * D)
    full_shapes = {n: given[n].shape for n in order}
    full_shapes["w_dw_conv"], full_shapes["w_dw_ffn"] = (1, CONV_K, 512), (1, FFN_K, 2 * D_FF)
    sums = dict(zip(order, _unpack(total, offs[1:], [full_shapes[n] for n in order])))
    sums["b_ada"] = total[0:6].reshape(1, 6 * D)
    sums["w_dw_conv"] = lax.dynamic_slice(sums["w_dw_conv"], (0, 0, shard * 128), (1, CONV_K, 128))
    sums["w_dw_ffn"] = lax.dynamic_slice(sums["w_dw_ffn"], (0, 0, shard * FF_BLOCK), (1, FFN_K, FF_BLOCK))

    upd = dict(zip(SMALL_ORDER, _adamw_many(
        [(given[n], sums[n], given["m_" + n], given["v_" + n]) for n in SMALL_ORDER], "adamw_small")))

    dmod_shard = lax.dynamic_slice(dmod_all, (0, shard * 1536), (8, 1536))
    ada = _ada_bwd_adamw(c_act.T, dmod_shard, w_ada[0], m_w_ada[0], v_w_ada[0], "ada_bwd_adamw")

    out = {"grad_w_ada": ada[0][None], "delta_w_ada": ada[1][None], "new_m_w_ada": ada[2][None], "new_v_w_ada": ada[3][None]}
    for n in BIG:
        g = reduced[n]
        g, dl, nm, nv = _adamw(given["w_" + n][0], g, given["m_w_" + n][0], given["v_w_" + n][0], "adamw_" + n)
        out["grad_w_" + n], out["delta_w_" + n], out["new_m_w_" + n], out["new_v_w_" + n] = g[None], dl[None], nm[None], nv[None]
    for n in SMALL_ORDER:
        out["grad_" + n], out["delta_" + n], out["new_m_" + n], out["new_v_" + n] = sums[n], *upd[n]

    weights = ["w_ada", "b_ada", "g_pre_mix", "g_post_mix", "w_in", "b_in", "rel_bias", "w_attn_o", "w_dw_conv", "b_dw_conv",
               "g_conv_ln", "b_conv_ln", "w_conv_o", "b_conv_o", "w_mix_o", "g_pre_ffn", "g_post_ffn", "w_up", "w_dw_ffn",
               "b_dw_ffn", "w_down"]
    return (loss, grad_x[None], *[out["grad_" + n] for n in weights], *[out["delta_" + n] for n in weights],
            *[out["new_m_" + n] for n in weights], *[out["new_v_" + n] for n in weights])
```

```python
import functools
import math

import numpy as np
import jax
import jax.numpy as jnp
from jax import lax
from jax.experimental import pallas as pl
from jax.experimental.pallas import tpu as pltpu

F32, BF16 = jnp.float32, jnp.bfloat16
MESH = pl.DeviceIdType.MESH

D = 1024
D_IN = 4608
D_FF = 2816
CONV_K = 31
FFN_K = 3
N_HEADS = 8
CHUNK = 64
LEFT_CHUNKS = 8
MAX_REL = 128
EPS = 1e-6
NEG_INF = -1e30
Q_TILE = 256
WINDOW = Q_TILE + LEFT_CHUNKS * CHUNK
REL_PAD = 384
TOEP = 1024
ROW_TILE = 256
VMEM_LIMIT = 60 * 1024 * 1024

ADAM_LR, ADAM_B1, ADAM_B2, ADAM_EPS, ADAM_WD, ADAM_STEP = 0.001, 0.9, 0.999, 1e-08, 0.01, 10


def _params(sem=None):
    return pltpu.CompilerParams(dimension_semantics=sem, vmem_limit_bytes=VMEM_LIMIT)


def _sds(shape, dtype):
    return jax.ShapeDtypeStruct(tuple(shape), dtype)


ANY = pl.BlockSpec(memory_space=pl.ANY)


class _Carried:
    def __init__(self, ins, out_shapes, aliases, n_sems, start, finish):
        self.ins, self.out_shapes, self.aliases = list(ins), list(out_shapes), dict(aliases)
        self.n_sems, self.start, self.finish = n_sems, start, finish


def _call(body, *, grid, in_specs, out_specs, out_shape, scratch_shapes, sem, name, args, carried=None):
    in_specs, out_specs, out_shape = list(in_specs), list(out_specs), list(out_shape)
    scratch_shapes = list(scratch_shapes)
    if carried is None:
        return pl.pallas_call(body, grid=grid, in_specs=in_specs, out_specs=out_specs, out_shape=out_shape,
                              scratch_shapes=scratch_shapes, compiler_params=_params(sem), name=name)(*args)
    n_in, n_out, n_scr = len(in_specs), len(out_specs), len(scratch_shapes)
    c_in, c_out = len(carried.ins), len(carried.out_shapes)

    def full(*refs):
        pos = [0]

        def take(k):
            part = refs[pos[0]:pos[0] + k]
            pos[0] += k
            return part

        ins, cins, outs, couts, scr = take(n_in), take(c_in), take(n_out), take(c_out), take(n_scr)
        send_sems, recv_sems = take(2)
        first = last = None
        for d, size in enumerate(grid):
            pid = pl.program_id(d)
            first = (pid == 0) if first is None else first & (pid == 0)
            last = (pid == size - 1) if last is None else last & (pid == size - 1)

        @pl.when(first)
        def _():
            carried.start(cins, couts, send_sems, recv_sems)

        body(*ins, *outs, *scr)

        @pl.when(last)
        def _():
            carried.finish(cins, couts, send_sems, recv_sems)

    sems = [pltpu.SemaphoreType.DMA((carried.n_sems,)), pltpu.SemaphoreType.DMA((carried.n_sems,))]
    return pl.pallas_call(
        full, grid=grid, in_specs=in_specs + [ANY] * c_in, out_specs=out_specs + [ANY] * c_out,
        out_shape=out_shape + carried.out_shapes, scratch_shapes=scratch_shapes + sems,
        input_output_aliases={n_in + k: n_out + v for k, v in carried.aliases.items()},
        compiler_params=_params(tuple("arbitrary" for _ in grid)), name=name,
    )(*args, *carried.ins)


def _run_carried(carried, name):
    c_in = len(carried.ins)

    def body(*refs):
        cins, couts = refs[:c_in], refs[c_in:c_in + len(carried.out_shapes)]
        send_sems, recv_sems = refs[-2:]
        carried.start(cins, couts, send_sems, recv_sems)
        carried.finish(cins, couts, send_sems, recv_sems)

    return pl.pallas_call(
        body, in_specs=[ANY] * c_in, out_specs=[ANY] * len(carried.out_shapes), out_shape=carried.out_shapes,
        scratch_shapes=[pltpu.SemaphoreType.DMA((carried.n_sems,)), pltpu.SemaphoreType.DMA((carried.n_sems,))],
        input_output_aliases=carried.aliases, name=name,
    )(*carried.ins)


def _matmul(a, b, *, form, out_dtype, tm, tn, tk, name, bias=None, add=None, out_sharded=False, carried=None):
    b3 = b.ndim == 3
    resident = 0
    if form == "nn":
        m, k = a.shape
        n = b.shape[0] * b.shape[2] if b3 else b.shape[1]
        dn = (((1,), (0,)), ((), ()))
        a_spec = pl.BlockSpec((tm, tk), lambda i, j, kk: (i, kk))
        if b3 and tn == n and tk == k:
            resident = b.shape[0]
            b_spec = pl.BlockSpec(b.shape, lambda i, j, kk: (0, 0, 0))
        else:
            b_spec = (pl.BlockSpec((None, tk, tn), lambda i, j, kk: (j, kk, 0)) if b3
                      else pl.BlockSpec((tk, tn), lambda i, j, kk: (kk, j)))
    elif form == "nt":
        m, k = a.shape
        n = b.shape[1] if b3 else b.shape[0]
        dn = (((1,), (1,)), ((), ()))
        a_spec = pl.BlockSpec((tm, tk), lambda i, j, kk: (i, kk))
        if b3 and tk == k:
            resident = b.shape[0]
            b_spec = pl.BlockSpec((resident, tn, b.shape[2]), lambda i, j, kk: (0, j, 0))
        else:
            b_spec = (pl.BlockSpec((None, tn, tk), lambda i, j, kk: (kk, j, 0)) if b3
                      else pl.BlockSpec((tn, tk), lambda i, j, kk: (j, kk)))
    else:
        k, m = a.shape
        n = b.shape[1]
        dn = (((0,), (0,)), ((), ()))
        a_spec = pl.BlockSpec((tk, tm), lambda i, j, kk: (kk, i))
        b_spec = pl.BlockSpec((tk, tn), lambda i, j, kk: (kk, j))
    assert m % tm == 0 and n % tn == 0 and k % tk == 0, (name, m, n, k, tm, tn, tk)
    nk = k // tk
    in_specs, args = [a_spec, b_spec], [a, b]
    if bias is not None:
        in_specs.append(pl.BlockSpec((1, tn), lambda i, j, kk: (0, j)))
        args.append(bias)
    if add is not None:
        in_specs.append(pl.BlockSpec((tm, tn), lambda i, j, kk: (i, j)))
        args.append(add)
    if out_sharded:
        out_shape = _sds((n // tn, m, tn), out_dtype)
        out_spec = pl.BlockSpec((None, tm, tn), lambda i, j, kk: (j, i, 0))
    else:
        out_shape = _sds((m, n), out_dtype)
        out_spec = pl.BlockSpec((tm, tn), lambda i, j, kk: (i, j))

    def body(*refs):
        a_ref, b_ref = refs[0], refs[1]
        pos = 2
        bias_ref = add_ref = None
        if bias is not None:
            bias_ref, pos = refs[pos], pos + 1
        if add is not None:
            add_ref, pos = refs[pos], pos + 1
        o_ref = refs[pos]
        if resident and form == "nn":
            ns = b_ref.shape[2]
            for s in range(resident):
                cols = slice(s * ns, (s + 1) * ns)
                ps = lax.dot_general(a_ref[...], b_ref[s], dn, preferred_element_type=F32)
                if bias_ref is not None:
                    ps = ps + bias_ref[:, cols]
                o_ref[:, cols] = ps.astype(o_ref.dtype)
            return
        if resident:
            ks = b_ref.shape[2]
            p = None
            for s in range(resident):
                ps = lax.dot_general(a_ref[:, s * ks:(s + 1) * ks], b_ref[s], dn, preferred_element_type=F32)
                p = ps if p is None else p + ps
        else:
            av, bv = a_ref[...], b_ref[...]
            if av.dtype != BF16:
                av = av.astype(BF16)
            if bv.dtype != BF16:
                bv = bv.astype(BF16)
            p = lax.dot_general(av, bv, dn, preferred_element_type=F32)

        def finish(acc):
            if bias_ref is not None:
                acc = acc + bias_ref[...]
            if add_ref is not None:
                acc = acc + add_ref[...]
            o_ref[...] = acc.astype(o_ref.dtype)

        if nk == 1:
            finish(p)
        else:
            acc_ref = refs[pos + 1]
            kk = pl.program_id(2)

            @pl.when(kk == 0)
            def _():
                acc_ref[...] = p

            @pl.when(kk > 0)
            def _():
                acc_ref[...] += p

            @pl.when(kk == nk - 1)
            def _():
                finish(acc_ref[...])

    res = _call(body, grid=(m // tm, n // tn, nk), in_specs=in_specs, out_specs=[out_spec], out_shape=[out_shape],
                scratch_shapes=[pltpu.VMEM((tm, tn), F32)] if nk > 1 else [],
                sem=("parallel", "parallel", "arbitrary"), name=name, args=args, carried=carried)
    return res[0] if carried is None else (res[0], res[1:])


def _rowcall(fn, rows, consts, row_outs, acc_outs, *, name, tm=ROW_TILE, col_grid=1):
    n_rows = rows[0][0].shape[0]
    assert n_rows % tm == 0
    grid = (col_grid, n_rows // tm)
    in_specs = [pl.BlockSpec((tm, w), functools.partial(lambda c, i, cb: (i, cb + c), cb=cb)) for _, w, cb in rows]
    in_specs += [pl.BlockSpec(k.shape, functools.partial(lambda c, i, nd: (0,) * nd, nd=k.ndim)) for k in consts]
    out_specs = [pl.BlockSpec((tm, w), lambda c, i: (i, c)) for _, _, _, w in row_outs]
    out_specs += [pl.BlockSpec((r, w), lambda c, i: (0, c)) for r, _, w in acc_outs]
    out_shape = [_sds((nr, nc), dt) for nr, nc, dt, _ in row_outs] + [_sds((r, nc), F32) for r, nc, _ in acc_outs]
    n_in, n_ro = len(rows) + len(consts), len(row_outs)

    def body(*refs):
        res = fn(*[r[...] for r in refs[:n_in]])
        if not isinstance(res, (tuple, list)):
            res = (res,)
        outs = refs[n_in:]
        for o_ref, val in zip(outs[:n_ro], res[:n_ro]):
            o_ref[...] = val.astype(o_ref.dtype)
        if acc_outs:
            first = pl.program_id(1) == 0

            @pl.when(first)
            def _():
                for o_ref, val in zip(outs[n_ro:], res[n_ro:]):
                    o_ref[...] = val

            @pl.when(jnp.logical_not(first))
            def _():
                for o_ref, val in zip(outs[n_ro:], res[n_ro:]):
                    o_ref[...] += val

    out = pl.pallas_call(
        body, grid=grid, in_specs=in_specs, out_specs=out_specs, out_shape=out_shape,
        compiler_params=_params(("arbitrary", "arbitrary")), name=name,
    )(*[r[0] for r in rows], *consts)
    return out


def _matmul_rows(b, *, form, tm, tk, fn, rows, consts, row_outs, acc_outs, name, a=None, a_rows=None, a_fn=None,
                 carried=None):
    b3 = b.ndim == 3
    resident = 0
    if form == "nn":
        k, n = b.shape
        b_spec = pl.BlockSpec((tk, n), lambda i, kk: (kk, 0))
        dn = (((1,), (0,)), ((), ()))
    else:
        n = b.shape[1] if b3 else b.shape[0]
        k = b.shape[0] * b.shape[2] if b3 else b.shape[1]
        if b3 and tk == k:
            resident = b.shape[0]
            b_spec = pl.BlockSpec(b.shape, lambda i, kk: (0, 0, 0))
        else:
            b_spec = (pl.BlockSpec((None, n, tk), lambda i, kk: (kk, 0, 0)) if b3
                      else pl.BlockSpec((n, tk), lambda i, kk: (0, kk)))
        dn = (((1,), (1,)), ((), ()))
    nk = k // tk
    lhs_in = [(a, tk, 0)] if a is not None else list(a_rows)
    assert a is not None or nk == 1
    m = lhs_in[0][0].shape[0]
    n_lhs = len(lhs_in)
    in_specs = [pl.BlockSpec((tm, tk), lambda i, kk: (i, kk))] if a is not None else [
        pl.BlockSpec((tm, w), functools.partial(lambda i, kk, cb: (i, cb), cb=cb)) for _, w, cb in a_rows]
    in_specs.append(b_spec)
    in_specs += [pl.BlockSpec((tm, w), functools.partial(lambda i, kk, cb: (i, cb), cb=cb)) for _, w, cb in rows]
    in_specs += [pl.BlockSpec(c.shape, functools.partial(lambda i, kk, nd: (0,) * nd, nd=c.ndim)) for c in consts]
    out_specs = [pl.BlockSpec((tm, w), lambda i, kk: (i, 0)) for _, w in row_outs]
    out_specs += [pl.BlockSpec((r, w), lambda i, kk: (0, 0)) for r, w in acc_outs]
    out_shape = [_sds((m, w), dt) for dt, w in row_outs] + [_sds((r, w), F32) for r, w in acc_outs]
    n_rows, n_consts, n_ro, n_acc = len(rows), len(consts), len(row_outs), len(acc_outs)

    def body(*refs):
        pos = n_lhs + 1
        row_refs, const_refs = refs[pos:pos + n_rows], refs[pos + n_rows:pos + n_rows + n_consts]
        pos += n_rows + n_consts
        out_refs, acc_refs = refs[pos:pos + n_ro], refs[pos + n_ro:pos + n_ro + n_acc]
        i, kk = pl.program_id(0), pl.program_id(1)
        if resident:
            b_ref, ks, p = refs[n_lhs], b.shape[2], None
            for s in range(resident):
                ps = lax.dot_general(refs[0][:, s * ks:(s + 1) * ks], b_ref[s], dn, preferred_element_type=F32)
                p = ps if p is None else p + ps
        else:
            lhs = refs[0][...] if a is not None else a_fn(*[r[...] for r in refs[:n_lhs]]).astype(BF16)
            p = lax.dot_general(lhs, refs[n_lhs][...], dn, preferred_element_type=F32)

        def finish(acc):
            extra = [r[...] for r in row_refs] + [c[...] for c in const_refs]
            res = fn(acc, lhs, *extra) if a is None else fn(acc, *extra)
            for o_ref, val in zip(out_refs, res[:n_ro]):
                o_ref[...] = val.astype(o_ref.dtype)
            if n_acc:
                @pl.when(i == 0)
                def _():
                    for o_ref, val in zip(acc_refs, res[n_ro:]):
                        o_ref[...] = val

                @pl.when(i > 0)
                def _():
                    for o_ref, val in zip(acc_refs, res[n_ro:]):
                        o_ref[...] += val

        if nk == 1:
            finish(p)
        else:
            acc_ref = refs[pos + n_ro + n_acc]

            @pl.when(kk == 0)
            def _():
                acc_ref[...] = p

            @pl.when(kk > 0)
            def _():
                acc_ref[...] += p

            @pl.when(kk == nk - 1)
            def _():
                finish(acc_ref[...])

    res = _call(body, grid=(m // tm, nk), in_specs=in_specs, out_specs=out_specs, out_shape=out_shape,
                scratch_shapes=[pltpu.VMEM((tm, n), F32)] if nk > 1 else [], sem=("arbitrary", "arbitrary"),
                name=name, args=[r[0] for r in lhs_in] + [b] + [r[0] for r in rows] + list(consts), carried=carried)
    own = n_ro + n_acc
    return res[:own] if carried is None else (res[:own], res[own:])


def _colsum(v):
    return jnp.sum(v, axis=0, keepdims=True)


def _sigmoid(v):
    return 1.0 / (1.0 + jnp.exp(-v))


_GELU_C = math.sqrt(2.0 / math.pi)


def _gelu(v):
    return 0.5 * v * (1.0 + jnp.tanh(_GELU_C * (v + 0.044715 * (v * v * v))))


def _gelu_and_grad(v):
    th = jnp.tanh(_GELU_C * (v + 0.044715 * (v * v * v)))
    g = 0.5 * v * (1.0 + th)
    dg = 0.5 * (1.0 + th) + 0.5 * v * (1.0 - th * th) * (_GELU_C * (1.0 + 3.0 * 0.044715 * (v * v)))
    return g, dg


def _rms_stats(v):
    r = lax.rsqrt(jnp.mean(v * v, axis=-1, keepdims=True) + EPS)
    return v * r, r


def _rms_bwd(dn, vn, r):
    return r * (dn - vn * jnp.mean(dn * vn, axis=-1, keepdims=True))


def _pre_norm(x, g, sc, sh, name):
    def fn(xv, gv, scv, shv):
        xn, _ = _rms_stats(xv)
        return (xn * gv) * (1.0 + scv) + shv
    return _rowcall(fn, [(x, D, 0)], [g, sc, sh], [(x.shape[0], D, BF16, D)], [], name=name)[0]


def _pre_norm_bwd(dh, x, dx_other, g, sc, name):
    def fn(dhv, xv, dov, gv, scv):
        xn, r = _rms_stats(xv)
        yn = xn * gv
        dyn = dhv * (1.0 + scv)
        dx = _rms_bwd(dyn * gv, xn, r)
        return dov + dx, _colsum(dhv), _colsum(dhv * yn), _colsum(dyn * xn)
    t = x.shape[0]
    return _rowcall(fn, [(dh, D, 0), (x, D, 0), (dx_other, D, 0)], [g, sc], [(t, D, F32, D)],
                    [(1, D, D)] * 3, name=name)


def _post_res(x, ypre, g, gt, name):
    def fn(xv, yv, gv, gtv):
        yn, _ = _rms_stats(yv)
        return xv + gtv * (yn * gv)
    return _rowcall(fn, [(x, D, 0), (ypre, D, 0)], [g, gt], [(x.shape[0], D, F32, D)], [], name=name)[0]


def _post_res_bwd(dxo, ypre, g, gt, name):
    def fn(dv, yv, gv, gtv):
        yn, r = _rms_stats(yv)
        dyn = dv * gtv
        dy = _rms_bwd(dyn * gv, yn, r)
        return dy, _colsum(dyn * yn), _colsum(dv * (yn * gv))
    t = ypre.shape[0]
    return _rowcall(fn, [(dxo, D, 0), (ypre, D, 0)], [g, gt], [(t, D, BF16, D)], [(1, D, D)] * 2, name=name)


def _ffn_tail(x1, yf, target, g, gt, name):
    def fn(xv, yv, tv, gv, gtv):
        yn, r = _rms_stats(yv)
        e = xv + gtv * (yn * gv) - tv
        dx2 = e * (1.0 / D)
        dyn = dx2 * gtv
        dy = _rms_bwd(dyn * gv, yn, r)
        return dx2, dy, _colsum(e * e) * (0.5 / D), _colsum(dyn * yn), _colsum(dx2 * (yn * gv))
    t = x1.shape[0]
    return _rowcall(fn, [(x1, D, 0), (yf, D, 0), (target, D, 0)], [g, gt], [(t, D, F32, D), (t, D, BF16, D)],
                    [(1, D, D)] * 3, name=name)


def _gate_merge(a, cb, z, name):
    def fn(av, cv, gav, gbv):
        return _sigmoid(gav) * av + _sigmoid(gbv) * cv
    t = a.shape[0]
    return _rowcall(fn, [(a, 512, 0), (cb, 512, 0), (z, 512, 5), (z, 512, 7)], [],
                    [(t, D, BF16, 512)], [], name=name, col_grid=2)[0]


def _gate_merge_bwd(dy, a, cb, z, name):
    def fn(dv, av, cv, gav, gbv):
        sa, sb = _sigmoid(gav), _sigmoid(gbv)
        dcb = dv * sb
        dga = dv * av * (sa * (1.0 - sa))
        dgb = dv * cv * (sb * (1.0 - sb))
        return dv * sa, dcb, dga, dgb, _colsum(dcb), _colsum(dga), _colsum(dgb)
    t = a.shape[0]
    return _rowcall(fn, [(dy, 512, 0), (a, 512, 0), (cb, 512, 0), (z, 512, 5), (z, 512, 7)], [],
                    [(t, D, BF16, 512)] * 4, [(1, D, 512)] * 3, name=name, col_grid=2)


CONV_HALO = 32


def _layer_norm_parts(u):
    mu = jnp.mean(u, axis=-1, keepdims=True)
    d = u - mu
    r = lax.rsqrt(jnp.mean(d * d, axis=-1, keepdims=True) + EPS)
    return d * r, r


LANES = 128
SUBLANE_ROWS = 8
CONV_ROWS = 64


def _lanes(c):
    return slice(c * LANES, (c + 1) * LANES)


def _conv_branch(z, w_dw, b_dw, g_ln, b_ln, name, tm=ROW_TILE):
    t = z.shape[0]
    per = tm // CONV_HALO
    n_chunks = 512 // LANES

    def body(ga_ref, gb_ref, gah_ref, gbh_ref, w_ref, b_ref, g_ref, bl_ref, u1_ref, u3_ref, scr):
        i = pl.program_id(0)
        u0h = jnp.where(i > 0, gah_ref[...] * _sigmoid(gbh_ref[...]), 0.0)
        u0 = ga_ref[...] * _sigmoid(gb_ref[...])
        for c in range(n_chunks):
            scr[c, 0:CONV_HALO, :] = u0h[:, _lanes(c)]
            scr[c, CONV_HALO:CONV_HALO + tm, :] = u0[:, _lanes(c)]
        for c in range(n_chunks):
            for r0 in range(0, tm, CONV_ROWS):
                acc = jnp.zeros((CONV_ROWS, LANES), F32) + b_ref[:, _lanes(c)]
                for j in range(CONV_K):
                    acc = acc + w_ref[j:j + 1, _lanes(c)] * scr[c, pl.ds(r0 + CONV_HALO - (CONV_K - 1) + j, CONV_ROWS), :]
                u1_ref[r0:r0 + CONV_ROWS, _lanes(c)] = acc
        xh, _ = _layer_norm_parts(u1_ref[...])
        u2 = xh * g_ref[...] + bl_ref[...]
        u3_ref[...] = (u2 * _sigmoid(u2)).astype(BF16)

    cur = lambda cb: pl.BlockSpec((tm, 512), lambda i: (i, cb))
    halo = lambda cb: pl.BlockSpec((CONV_HALO, 512), lambda i: (jnp.maximum(i * per - 1, 0), cb))
    whole = lambda a: pl.BlockSpec(a.shape, lambda i: (0, 0))
    return pl.pallas_call(
        body, grid=(t // tm,),
        in_specs=[cur(3), cur(4), halo(3), halo(4), whole(w_dw), whole(b_dw), whole(g_ln), whole(b_ln)],
        out_specs=[pl.BlockSpec((tm, 512), lambda i: (i, 0))] * 2,
        out_shape=[_sds((t, 512), F32), _sds((t, 512), BF16)],
        scratch_shapes=[pltpu.VMEM((n_chunks, CONV_HALO + tm, LANES), F32)],
        compiler_params=_params(("arbitrary",)), name=name,
    )(z, z, z, z, w_dw, b_dw, g_ln, b_ln)


def _conv_branch_bwd(du3, u1, z, w_dw, g_ln, b_ln, name, tm=ROW_TILE, carried=None):
    t = z.shape[0]
    per = tm // CONV_HALO
    last = t // tm - 1
    n_chunks = 512 // LANES

    def du1_of(du3v, u1v, g, b):
        xh, r = _layer_norm_parts(u1v)
        u2 = xh * g + b
        s = _sigmoid(u2)
        du2 = du3v * (s * (1.0 + u2 * (1.0 - s)))
        dxh = du2 * g
        du1 = r * (dxh - jnp.mean(dxh, axis=-1, keepdims=True) - xh * jnp.mean(dxh * xh, axis=-1, keepdims=True))
        return du1, du2, xh

    def body(d_ref, u_ref, dn_ref, un_ref, ga_ref, gb_ref, gah_ref, gbh_ref, w_ref, g_ref, bl_ref,
             dglu_ref, dw_ref, dbdw_ref, dg_ref, dbl_ref, dbin_ref, scr, scd):
        i = pl.program_id(0)
        g, b = g_ref[...], bl_ref[...]
        du1, du2, xh = du1_of(d_ref[...], u_ref[...], g, b)
        du1n, _, _ = du1_of(dn_ref[...], un_ref[...], g, b)
        du1n = jnp.where(i < last, du1n, 0.0)
        sgb = _sigmoid(gb_ref[...])
        ga = ga_ref[...]
        u0 = ga * sgb
        u0h = jnp.where(i > 0, gah_ref[...] * _sigmoid(gbh_ref[...]), 0.0)
        for c in range(n_chunks):
            scd[c, 0:tm, :] = du1[:, _lanes(c)]
            scd[c, tm:tm + CONV_HALO, :] = du1n[:, _lanes(c)]
            scr[c, 0:CONV_HALO, :] = u0h[:, _lanes(c)]
            scr[c, CONV_HALO:CONV_HALO + tm, :] = u0[:, _lanes(c)]

        @pl.when(i == 0)
        def _():
            for ref in (dw_ref, dbdw_ref, dg_ref, dbl_ref, dbin_ref):
                ref[...] = jnp.zeros_like(ref)

        dsg = ga * (sgb * (1.0 - sgb))
        for c in range(n_chunks):
            gate = slice(512 + c * LANES, 512 + (c + 1) * LANES)
            for r0 in range(0, tm, CONV_ROWS):
                rows = slice(r0, r0 + CONV_ROWS)
                du0 = jnp.zeros((CONV_ROWS, LANES), F32)
                for j in range(CONV_K):
                    du0 = du0 + w_ref[j:j + 1, _lanes(c)] * scd[c, pl.ds(r0 + CONV_K - 1 - j, CONV_ROWS), :]
                dga = du0 * sgb[rows, _lanes(c)]
                dgb = du0 * dsg[rows, _lanes(c)]
                dglu_ref[rows, _lanes(c)] = dga.astype(BF16)
                dglu_ref[rows, gate] = dgb.astype(BF16)
                dbin_ref[:, _lanes(c)] += _colsum(dga)
                dbin_ref[:, gate] += _colsum(dgb)
            for j in range(CONV_K):
                dwj = jnp.zeros((SUBLANE_ROWS, LANES), F32)
                for r0 in range(0, tm, CONV_ROWS):
                    prod = (scd[c, pl.ds(r0, CONV_ROWS), :]
                            * scr[c, pl.ds(r0 + CONV_HALO - (CONV_K - 1) + j, CONV_ROWS), :])
                    dwj = dwj + jnp.sum(prod.reshape(CONV_ROWS // SUBLANE_ROWS, SUBLANE_ROWS, LANES), axis=0)
                dw_ref[j:j + 1, _lanes(c)] += _colsum(dwj)
        dbdw_ref[...] += _colsum(du1)
        dg_ref[...] += _colsum(du2 * xh)
        dbl_ref[...] += _colsum(du2)

    cur = lambda cb: pl.BlockSpec((tm, 512), lambda i: (i, cb))
    prev = lambda cb: pl.BlockSpec((CONV_HALO, 512), lambda i: (jnp.maximum(i * per - 1, 0), cb))
    nxt = pl.BlockSpec((CONV_HALO, 512), lambda i: (jnp.minimum((i + 1) * per, t // CONV_HALO - 1), 0))
    whole = lambda a: pl.BlockSpec(a.shape, lambda i: (0, 0))
    acc = lambda r, w: pl.BlockSpec((r, w), lambda i: (0, 0))
    res = _call(
        body, grid=(t // tm,),
        in_specs=[cur(0), cur(0), nxt, nxt, cur(3), cur(4), prev(3), prev(4), whole(w_dw), whole(g_ln), whole(b_ln)],
        out_specs=[pl.BlockSpec((tm, 1024), lambda i: (i, 0)), acc(CONV_K, 512), acc(1, 512), acc(1, 512),
                   acc(1, 512), acc(1, 1024)],
        out_shape=[_sds((t, 1024), BF16), _sds((CONV_K, 512), F32), _sds((1, 512), F32), _sds((1, 512), F32),
                   _sds((1, 512), F32), _sds((1, 1024), F32)],
        scratch_shapes=[pltpu.VMEM((n_chunks, CONV_HALO + tm, LANES), F32),
                        pltpu.VMEM((n_chunks, tm + CONV_HALO, LANES), F32)],
        sem=("arbitrary",), name=name, args=(du3, u1, du3, u1, z, z, z, z, w_dw, g_ln, b_ln), carried=carried)
    return res[:6] if carried is None else (res[:6], res[6:])


FF_BLOCK = D_FF // 2
FF_HALO = 8
FF_CHUNKS = FF_BLOCK // LANES


FF_ROWS = 64
FF_EXT_ROWS = 88


def _ffn_conv(w_ref, b_ref, scr, k, rows, r0=0):
    acc = b_ref[:, _lanes(k)] + w_ref[0:1, _lanes(k)] * scr[k, pl.ds(r0 + FF_HALO - 2, rows), :]
    acc = acc + w_ref[1:2, _lanes(k)] * scr[k, pl.ds(r0 + FF_HALO - 1, rows), :]
    return acc + w_ref[2:3, _lanes(k)] * scr[k, pl.ds(r0 + FF_HALO, rows), :]


def _ffn_act(up, w3, b3, name, tm=ROW_TILE):
    t = up.shape[0]
    per = tm // FF_HALO
    wide = 2 * FF_BLOCK

    def body(u_ref, uh_ref, w_ref, b_ref, o_ref, scr):
        i = pl.program_id(1)
        for k in range(2 * FF_CHUNKS):
            scr[k, 0:FF_HALO, :] = jnp.where(i > 0, uh_ref[:, _lanes(k)], 0.0)
            scr[k, FF_HALO:FF_HALO + tm, :] = u_ref[:, _lanes(k)]
        for cc in range(FF_CHUNKS):
            for r0 in range(0, tm, FF_ROWS):
                val = _ffn_conv(w_ref, b_ref, scr, cc, FF_ROWS, r0)
                gate = _ffn_conv(w_ref, b_ref, scr, FF_CHUNKS + cc, FF_ROWS, r0)
                o_ref[r0:r0 + FF_ROWS, _lanes(cc)] = (_gelu(gate) * val).astype(BF16)

    return pl.pallas_call(
        body, grid=(2, t // tm),
        in_specs=[pl.BlockSpec((tm, wide), lambda c, i: (i, c)),
                  pl.BlockSpec((FF_HALO, wide), lambda c, i: (jnp.maximum(i * per - 1, 0), c)),
                  pl.BlockSpec((FFN_K, wide), lambda c, i: (0, c)),
                  pl.BlockSpec((1, wide), lambda c, i: (0, c))],
        out_specs=pl.BlockSpec((tm, FF_BLOCK), lambda c, i: (i, c)),
        out_shape=_sds((t, D_FF), BF16),
        scratch_shapes=[pltpu.VMEM((2 * FF_CHUNKS, FF_HALO + tm, LANES), F32)],
        compiler_params=_params(("arbitrary", "arbitrary")), name=name,
    )(up, up, w3, b3)


def _ffn_act_bwd(dact, up, w3, b3, name, tm=ROW_TILE):
    t = up.shape[0]
    per = tm // FF_HALO
    wide = 2 * FF_BLOCK
    last = t // tm - 1
    ext = tm + FF_HALO

    def body(u_ref, up_ref, un_ref, d_ref, dn_ref, w_ref, b_ref, o_ref, dw_ref, db_ref, scr, scd):
        i = pl.program_id(1)
        for k in range(2 * FF_CHUNKS):
            scr[k, 0:FF_HALO, :] = jnp.where(i > 0, up_ref[:, _lanes(k)], 0.0)
            scr[k, FF_HALO:FF_HALO + tm, :] = u_ref[:, _lanes(k)]
            scr[k, FF_HALO + tm:FF_HALO + ext, :] = un_ref[:, _lanes(k)]
        dn = jnp.where(i < last, dn_ref[...], 0.0)

        @pl.when(i == 0)
        def _():
            dw_ref[...] = jnp.zeros_like(dw_ref)
            db_ref[...] = jnp.zeros_like(db_ref)

        for cc in range(FF_CHUNKS):
            gc = FF_CHUNKS + cc
            for r0 in range(0, ext, FF_EXT_ROWS):
                rows = pl.ds(r0, FF_EXT_ROWS)
                val = _ffn_conv(w_ref, b_ref, scr, cc, FF_EXT_ROWS, r0)
                gel, dgel = _gelu_and_grad(_ffn_conv(w_ref, b_ref, scr, gc, FF_EXT_ROWS, r0))
                da = d_ref[r0:r0 + FF_EXT_ROWS, _lanes(cc)] if r0 + FF_EXT_ROWS <= tm else jnp.concatenate(
                    [d_ref[r0:tm, _lanes(cc)], dn[:, _lanes(cc)]], axis=0)
                scd[cc, rows, :] = da * gel
                scd[gc, rows, :] = da * val * dgel
            for k in (cc, gc):
                dwk = [jnp.zeros((SUBLANE_ROWS, LANES), F32) for _ in range(FFN_K)]
                dbk = jnp.zeros((SUBLANE_ROWS, LANES), F32)
                for r0 in range(0, tm, FF_ROWS):
                    shifted = [scd[k, pl.ds(r0 + FFN_K - 1 - j, FF_ROWS), :] for j in range(FFN_K)]
                    ucur = scr[k, pl.ds(r0 + FF_HALO, FF_ROWS), :]
                    o_ref[r0:r0 + FF_ROWS, _lanes(k)] = (
                        w_ref[0:1, _lanes(k)] * shifted[0] + w_ref[1:2, _lanes(k)] * shifted[1]
                        + w_ref[2:3, _lanes(k)] * shifted[2]).astype(BF16)
                    fold = lambda v: jnp.sum(v.reshape(FF_ROWS // SUBLANE_ROWS, SUBLANE_ROWS, LANES), axis=0)
                    for j in range(FFN_K):
                        dwk[j] = dwk[j] + fold(shifted[j] * ucur)
                    dbk = dbk + fold(shifted[FFN_K - 1])
                for j in range(FFN_K):
                    dw_ref[j:j + 1, _lanes(k)] += _colsum(dwk[j])
                db_ref[:, _lanes(k)] += _colsum(dbk)

    nblk = t // FF_HALO
    return pl.pallas_call(
        body, grid=(2, t // tm),
        in_specs=[pl.BlockSpec((tm, wide), lambda c, i: (i, c)),
                  pl.BlockSpec((FF_HALO, wide), lambda c, i: (jnp.maximum(i * per - 1, 0), c)),
                  pl.BlockSpec((FF_HALO, wide), lambda c, i: (jnp.minimum((i + 1) * per, nblk - 1), c)),
                  pl.BlockSpec((tm, FF_BLOCK), lambda c, i: (i, c)),
                  pl.BlockSpec((FF_HALO, FF_BLOCK), lambda c, i: (jnp.minimum((i + 1) * per, nblk - 1), c)),
                  pl.BlockSpec((FFN_K, wide), lambda c, i: (0, c)),
                  pl.BlockSpec((1, wide), lambda c, i: (0, c))],
        out_specs=[pl.BlockSpec((tm, wide), lambda c, i: (i, c)),
                   pl.BlockSpec((FFN_K, wide), lambda c, i: (0, c)),
                   pl.BlockSpec((1, wide), lambda c, i: (0, c))],
        out_shape=[_sds((t, 2 * D_FF), BF16), _sds((FFN_K, 2 * D_FF), F32), _sds((1, 2 * D_FF), F32)],
        scratch_shapes=[pltpu.VMEM((2 * FF_CHUNKS, FF_HALO + ext, LANES), F32),
                        pltpu.VMEM((2 * FF_CHUNKS, ext, LANES), F32)],
        compiler_params=_params(("arbitrary", "arbitrary")), name=name,
    )(up, up, up, dact, dact, w3, b3)


def _toeplitz_map():
    f = np.zeros((TOEP, REL_PAD), np.float32)
    for m in range(TOEP - 1):
        rel = (WINDOW - 1) - m
        f[m, int(np.clip(rel, -MAX_REL, MAX_REL)) + MAX_REL] = 1.0
    return f


def _split3(v):
    hi = v.astype(BF16)
    r1 = v - hi.astype(F32)
    mid = r1.astype(BF16)
    lo = (r1 - mid.astype(F32)).astype(BF16)
    return hi, mid, lo


def _exact_select(v, sel):
    out = None
    for part in _split3(v):
        p = jnp.dot(part, sel, preferred_element_type=F32)
        out = p if out is None else out + p
    return out


def _select_call(v, sel, name):
    def body(v_ref, s_ref, o_ref):
        o_ref[...] = _exact_select(v_ref[...], s_ref[...])
    return pl.pallas_call(body, out_shape=_sds((v.shape[0], sel.shape[1]), F32), name=name)(v, sel)


def _band_bias(gen_row):
    b0 = jnp.broadcast_to(gen_row, (Q_TILE, TOEP))
    bias = pltpu.roll(b0, TOEP - 255, 1, stride=1, stride_axis=0)[:, :WINDOW]
    qq = lax.broadcasted_iota(jnp.int32, (Q_TILE, WINDOW), 0) // CHUNK
    kc = lax.broadcasted_iota(jnp.int32, (Q_TILE, WINDOW), 1) // CHUNK
    return jnp.where((kc >= qq) & (kc <= qq + LEFT_CHUNKS), bias, NEG_INF)


PAD_ROWS = WINDOW - Q_TILE
NT_DIMS = (((1,), (1,)), ((), ()))
TN_DIMS = (((0,), (0,)), ((), ()))


def _head_mask(hh):
    lane = lax.broadcasted_iota(jnp.int32, (1, 128), 1)
    return (lane < 64) if hh == 0 else (lane >= 64)


SOFTMAX_ROWS = 16


def _probs_block(s_scr, bias, hh, rows, i):
    s = s_scr[rows, :] + bias[hh, rows, :]
    col = lax.broadcasted_iota(jnp.int32, (SOFTMAX_ROWS, WINDOW), 1)
    s = jnp.where(col >= PAD_ROWS - Q_TILE * i, s, NEG_INF)
    p = jnp.exp(s - jnp.max(s, axis=-1, keepdims=True))
    return p / jnp.sum(p, axis=-1, keepdims=True)


def _attention(z, gen, name, carried=None):
    t = z.shape[0]
    n_i = t // Q_TILE

    def body(q_ref, k_ref, v_ref, g_ref, o_ref, kpad, vpad, bias, s_scr, p_scr):
        hp, i = pl.program_id(0), pl.program_id(1)

        @pl.when(i == 0)
        def _():
            kpad[0:PAD_ROWS, :] = jnp.zeros((PAD_ROWS, 128), BF16)
            vpad[0:PAD_ROWS, :] = jnp.zeros((PAD_ROWS, 128), BF16)
            kpad[PAD_ROWS:PAD_ROWS + t, :] = k_ref[...].astype(BF16)
            vpad[PAD_ROWS:PAD_ROWS + t, :] = v_ref[...].astype(BF16)
            for hh in range(2):
                bias[hh] = _band_bias(g_ref[pl.ds(2 * hp + hh, 1), :])

        win = pl.ds(pl.multiple_of(i * Q_TILE, Q_TILE), WINDOW)
        out = None
        for hh in range(2):
            mask = _head_mask(hh)
            qm = jnp.where(mask, q_ref[...] * (CHUNK ** -0.5), 0.0).astype(BF16)
            s_scr[hh] = lax.dot_general(qm, kpad[win, :], NT_DIMS, preferred_element_type=F32)
            for r0 in range(0, Q_TILE, SOFTMAX_ROWS):
                rows = slice(r0, r0 + SOFTMAX_ROWS)
                p_scr[hh, rows, :] = _probs_block(s_scr.at[hh], bias, hh, rows, i).astype(BF16)
            o = jnp.dot(p_scr[hh], vpad[win, :], preferred_element_type=F32)
            out = jnp.where(mask, o, 0.0) if out is None else jnp.where(mask, o, out)
        o_ref[...] = out.astype(BF16)

    res = _call(
        body, grid=(4, n_i),
        in_specs=[pl.BlockSpec((Q_TILE, 128), lambda h, i: (i, h)),
                  pl.BlockSpec((t, 128), lambda h, i: (0, 4 + h)),
                  pl.BlockSpec((t, 128), lambda h, i: (0, 8 + h)),
                  pl.BlockSpec((N_HEADS, TOEP), lambda h, i: (0, 0))],
        out_specs=[pl.BlockSpec((Q_TILE, 128), lambda h, i: (i, h))],
        out_shape=[_sds((t, 512), BF16)],
        scratch_shapes=[pltpu.VMEM((PAD_ROWS + t, 128), BF16), pltpu.VMEM((PAD_ROWS + t, 128), BF16),
                        pltpu.VMEM((2, Q_TILE, WINDOW), F32), pltpu.VMEM((2, Q_TILE, WINDOW), F32),
                        pltpu.VMEM((2, Q_TILE, WINDOW), BF16)],
        sem=("arbitrary", "arbitrary"), name=name, args=(z, z, z, gen), carried=carried)
    return res[0] if carried is None else (res[0], res[1:])


def _attention_bwd(z, datt, gen, name, carried=None):
    t = z.shape[0]
    n_i = t // Q_TILE

    def body(q_ref, k_ref, v_ref, d_ref, g_ref, dq_ref, dk_ref, dv_ref, sq_ref, sk_ref, sv_ref, dg_ref,
             kpad, vpad, dkacc, dvacc, bias, dsacc, s_scr, dp_scr, p_scr, ds_scr):
        hp, i = pl.program_id(0), pl.program_id(1)

        @pl.when(i == 0)
        def _():
            kpad[0:PAD_ROWS, :] = jnp.zeros((PAD_ROWS, 128), BF16)
            vpad[0:PAD_ROWS, :] = jnp.zeros((PAD_ROWS, 128), BF16)
            kpad[PAD_ROWS:PAD_ROWS + t, :] = k_ref[...].astype(BF16)
            vpad[PAD_ROWS:PAD_ROWS + t, :] = v_ref[...].astype(BF16)
            dkacc[...] = jnp.zeros_like(dkacc)
            dvacc[...] = jnp.zeros_like(dvacc)
            dsacc[...] = jnp.zeros_like(dsacc)
            for hh in range(2):
                bias[hh] = _band_bias(g_ref[pl.ds(2 * hp + hh, 1), :])

        start = pl.multiple_of(i * Q_TILE, Q_TILE)
        win = pl.ds(start, WINDOW)
        dq = None
        for hh in range(2):
            mask = _head_mask(hh)
            qm = jnp.where(mask, q_ref[...] * (CHUNK ** -0.5), 0.0).astype(BF16)
            dom = jnp.where(mask, d_ref[...], 0.0).astype(BF16)
            s_scr[...] = lax.dot_general(qm, kpad[win, :], NT_DIMS, preferred_element_type=F32)
            dp_scr[...] = lax.dot_general(dom, vpad[win, :], NT_DIMS, preferred_element_type=F32)
            for r0 in range(0, Q_TILE, SOFTMAX_ROWS):
                rows = slice(r0, r0 + SOFTMAX_ROWS)
                p = _probs_block(s_scr, bias, hh, rows, i)
                dp = dp_scr[rows, :]
                ds = p * (dp - jnp.sum(p * dp, axis=-1, keepdims=True))
                dsacc[hh, rows, :] += ds
                ds_scr[rows, :] = ds.astype(BF16)
                p_scr[rows, :] = p.astype(BF16)
            ds16 = ds_scr[...]
            dqh = jnp.dot(ds16, kpad[win, :], preferred_element_type=F32) * (CHUNK ** -0.5)
            dq = jnp.where(mask, dqh, 0.0) if dq is None else jnp.where(mask, dqh, dq)
            dkacc[win, :] += lax.dot_general(ds16, qm, TN_DIMS, preferred_element_type=F32)
            dvacc[win, :] += lax.dot_general(p_scr[...], dom, TN_DIMS, preferred_element_type=F32)
        dq_ref[...] = dq.astype(BF16)

        @pl.when(i == 0)
        def _():
            sq_ref[...] = _colsum(dq)

        @pl.when(i > 0)
        def _():
            sq_ref[...] += _colsum(dq)

        @pl.when(i == n_i - 1)
        def _():
            dk = dkacc[PAD_ROWS:PAD_ROWS + t, :]
            dv = dvacc[PAD_ROWS:PAD_ROWS + t, :]
            dk_ref[...] = dk.astype(BF16)
            dv_ref[...] = dv.astype(BF16)
            sk_ref[...] = _colsum(dk)
            sv_ref[...] = _colsum(dv)
            rr = lax.broadcasted_iota(jnp.int32, (Q_TILE, Q_TILE), 0)
            cc = lax.broadcasted_iota(jnp.int32, (Q_TILE, Q_TILE), 1)
            rev = jnp.where(rr + cc == Q_TILE - 1, 1.0, 0.0).astype(BF16)
            for hh in range(2):
                acc = None
                for part in _split3(dsacc[hh]):
                    pr = jnp.dot(rev, part, preferred_element_type=F32)
                    acc = pr if acc is None else acc + pr
                wide = jnp.concatenate([acc, jnp.zeros((Q_TILE, TOEP - WINDOW), F32)], axis=1)
                dg_ref[pl.ds(2 * hp + hh, 1), :] = _colsum(pltpu.roll(wide, 0, 1, stride=1, stride_axis=0))

    col = lambda off: pl.BlockSpec((t, 128), lambda h, i: (0, off + h))
    tile = lambda: pl.BlockSpec((Q_TILE, 128), lambda h, i: (i, h))
    sums = lambda: pl.BlockSpec((1, 128), lambda h, i: (0, h))
    res = _call(
        body, grid=(4, n_i),
        in_specs=[tile(), col(4), col(8), tile(), pl.BlockSpec((N_HEADS, TOEP), lambda h, i: (0, 0))],
        out_specs=[tile(), col(0), col(0), sums(), sums(), sums(), pl.BlockSpec((N_HEADS, TOEP), lambda h, i: (0, 0))],
        out_shape=[_sds((t, 512), BF16)] * 3 + [_sds((1, 512), F32)] * 3 + [_sds((N_HEADS, TOEP), F32)],
        scratch_shapes=[pltpu.VMEM((PAD_ROWS + t, 128), BF16), pltpu.VMEM((PAD_ROWS + t, 128), BF16),
                        pltpu.VMEM((PAD_ROWS + t, 128), F32), pltpu.VMEM((PAD_ROWS + t, 128), F32),
                        pltpu.VMEM((2, Q_TILE, WINDOW), F32), pltpu.VMEM((2, Q_TILE, WINDOW), F32),
                        pltpu.VMEM((Q_TILE, WINDOW), F32), pltpu.VMEM((Q_TILE, WINDOW), F32),
                        pltpu.VMEM((Q_TILE, WINDOW), BF16), pltpu.VMEM((Q_TILE, WINDOW), BF16)],
        sem=("arbitrary", "arbitrary"), name=name, args=(z, z, z, datt, gen), carried=carried)
    return res[:7] if carried is None else (res[:7], res[7:])


def _adamw_math(w, g, m, v):
    m = ADAM_B1 * m + (1.0 - ADAM_B1) * g
    v = ADAM_B2 * v + (1.0 - ADAM_B2) * (g * g)
    m_hat = m / (1.0 - ADAM_B1 ** ADAM_STEP)
    v_hat = v / (1.0 - ADAM_B2 ** ADAM_STEP)
    delta = -ADAM_LR * (m_hat / (jnp.sqrt(v_hat) + ADAM_EPS) + ADAM_WD * w)
    return delta, m, v


def _adamw_many(items, name):
    n = len(items)

    def body(*refs):
        ins, outs = refs[:4 * n], refs[4 * n:]
        for k in range(n):
            w, g, m, v = (r[...] for r in ins[4 * k:4 * k + 4])
            outs[3 * k][...], outs[3 * k + 1][...], outs[3 * k + 2][...] = _adamw_math(w, g, m, v)

    flat = [a for item in items for a in item]
    res = pl.pallas_call(body, out_shape=[_sds(item[0].shape, F32) for item in items for _ in range(3)],
                         name=name)(*flat)
    return [tuple(res[3 * k:3 * k + 3]) for k in range(n)]


def _adamw(w, g, m, v, name):
    r, c = w.shape
    tm = next(cand for cand in (256, 176, 128, 64, 32, 16, 8) if r % cand == 0)
    return _rowcall(lambda wv, gv, mv, vv: (gv,) + _adamw_math(wv, gv, mv, vv),
                    [(w, c, 0), (g, c, 0), (m, c, 0), (v, c, 0)], [], [(r, c, F32, c)] * 4, [], name=name, tm=tm)


def _ada_fwd(c_all, w_shard, b_shard, name):
    n = w_shard.shape[1]
    tn = 512

    def body(c_ref, w_ref, b_ref, o_ref, a_ref):
        cv = c_ref[...]
        act = cv * _sigmoid(cv)
        a_ref[...] = act
        o_ref[...] = jnp.dot(act.astype(BF16), w_ref[...].astype(BF16), preferred_element_type=F32) + b_ref[...]

    return pl.pallas_call(
        body, grid=(n // tn,),
        in_specs=[pl.BlockSpec((8, D), lambda j: (0, 0)), pl.BlockSpec((D, tn), lambda j: (0, j)),
                  pl.BlockSpec((1, tn), lambda j: (0, j))],
        out_specs=[pl.BlockSpec((8, tn), lambda j: (0, j)), pl.BlockSpec((8, D), lambda j: (0, 0))],
        out_shape=[_sds((8, n), F32), _sds((8, D), F32)],
        compiler_params=_params(("arbitrary",)), name=name,
    )(c_all, w_shard, b_shard)


def _ada_bwd_adamw(act_t, dmod_shard, w, m, v, name):
    r, c = w.shape
    tm = 256

    def body(a_ref, d_ref, w_ref, m_ref, v_ref, g_ref, dl_ref, nm_ref, nv_ref):
        g = jnp.dot(a_ref[...], d_ref[...], precision=lax.Precision.HIGHEST, preferred_element_type=F32)
        g_ref[...] = g
        dl_ref[...], nm_ref[...], nv_ref[...] = _adamw_math(w_ref[...], g, m_ref[...], v_ref[...])

    blk = pl.BlockSpec((tm, c), lambda i: (i, 0))
    return pl.pallas_call(
        body, grid=(r // tm,),
        in_specs=[pl.BlockSpec((tm, 8), lambda i: (i, 0)), pl.BlockSpec((8, c), lambda i: (0, 0)), blk, blk, blk],
        out_specs=[blk] * 4, out_shape=[_sds((r, c), F32)] * 4,
        compiler_params=_params(("arbitrary",)), name=name,
    )(act_t, dmod_shard, w, m, v)


def _place():
    return lax.axis_index("x"), lax.axis_index("y"), lax.axis_index("c")


def _flip(v, bit):
    return 1 - v if bit else v


VMEM_SPEC = pl.BlockSpec(memory_space=pltpu.VMEM)


def _allgather8(v, name):
    r, c = v.shape

    def body(v_ref, g_ref, tot_ref, send_sems, recv_sems, local_sem):
        x, y, cc = _place()
        sibling = (x, y, 1 - cc)
        chips = [(_flip(x, k & 2), _flip(y, k & 1)) for k in (1, 2, 3)]

        def block(px, py, pc):
            return g_ref.at[4 * px + 2 * py + pc]

        def copy(k, place, to, src=None):
            slot = block(*place)
            return pltpu.make_async_remote_copy(src_ref=slot if src is None else src, dst_ref=slot,
                                                send_sem=send_sems.at[k], recv_sem=recv_sems.at[k],
                                                device_id=to, device_id_type=MESH)

        mine = pltpu.make_async_copy(v_ref, block(x, y, cc), local_sem)
        mine.start()
        first = [copy(0, (x, y, cc), sibling, src=v_ref)]
        first += [copy(1 + j, (x, y, cc), (px, py, cc), src=v_ref) for j, (px, py) in enumerate(chips)]
        for cp in first:
            cp.start()
        passed = [copy(4 + j, (px, py, cc), sibling) for j, (px, py) in enumerate(chips)]
        for j, (px, py) in enumerate(chips):
            copy(1 + j, (px, py, cc), (x, y, cc)).wait_recv()
            passed[j].start()
        copy(0, sibling, (x, y, cc)).wait_recv()
        for j, (px, py) in enumerate(chips):
            copy(4 + j, (px, py, 1 - cc), (x, y, cc)).wait_recv()
        for cp in first + passed:
            cp.wait_send()
        mine.wait()
        tot = g_ref[0]
        for d in range(1, 8):
            tot = tot + g_ref[d]
        tot_ref[...] = tot

    return pl.pallas_call(
        body, in_specs=[VMEM_SPEC], out_specs=[VMEM_SPEC, VMEM_SPEC],
        out_shape=[_sds((8, r, c), F32), _sds((r, c), F32)],
        scratch_shapes=[pltpu.SemaphoreType.DMA((7,)), pltpu.SemaphoreType.DMA((7,)), pltpu.SemaphoreType.DMA],
        compiler_params=pltpu.CompilerParams(vmem_limit_bytes=VMEM_LIMIT), name=name,
    )(v)


def _slot(px, py, swapped):
    return 2 * py + px if swapped else 2 * px + py


def _gather_shards(arrs, swapped, name, in_place=False):
    n = len(arrs)

    def body(*refs):
        ins, outs = refs[:n], refs[n:2 * n]
        send1, recv1, send2, recv2, local_sems = refs[2 * n:]
        x, y, c = _place()
        sibling = (x, y, 1 - c)
        chips = [(_flip(x, k & 2), _flip(y, k & 1)) for k in (1, 2, 3)]
        local_copies, sends = [], []
        for a in range(n):
            h = outs[a].shape[1] // 2
            mine = pl.ds(pl.multiple_of(c * h, 8), h)
            own = _slot(x, y, swapped[a])
            if in_place:
                src = outs[a].at[own, mine]
            else:
                src = ins[a].at[mine]
                lc = pltpu.make_async_copy(ins[a], outs[a].at[own], local_sems.at[a])
                lc.start()
                local_copies.append(lc)
            for j, (px, py) in enumerate(chips):
                cp = pltpu.make_async_remote_copy(
                    src_ref=src, dst_ref=outs[a].at[own, mine], send_sem=send1.at[3 * a + j],
                    recv_sem=recv1.at[3 * a + j], device_id=(px, py, c), device_id_type=MESH)
                cp.start()
                sends.append(cp)
        for a in range(n):
            h = outs[a].shape[1] // 2
            mine = pl.ds(pl.multiple_of(c * h, 8), h)
            for j, (px, py) in enumerate(chips):
                piece = outs[a].at[_slot(px, py, swapped[a]), mine]
                pltpu.make_async_remote_copy(
                    src_ref=piece, dst_ref=piece, send_sem=send1.at[3 * a + j], recv_sem=recv1.at[3 * a + j],
                    device_id=(px, py, c), device_id_type=MESH).wait_recv()
                fwd = pltpu.make_async_remote_copy(
                    src_ref=piece, dst_ref=piece, send_sem=send2.at[3 * a + j], recv_sem=recv2.at[3 * a + j],
                    device_id=sibling, device_id_type=MESH)
                fwd.start()
                sends.append(fwd)
        for a in range(n):
            h = outs[a].shape[1] // 2
            other = pl.ds(pl.multiple_of((1 - c) * h, 8), h)
            for j, (px, py) in enumerate(chips):
                piece = outs[a].at[_slot(px, py, swapped[a]), other]
                pltpu.make_async_remote_copy(
                    src_ref=piece, dst_ref=piece, send_sem=send2.at[3 * a + j], recv_sem=recv2.at[3 * a + j],
                    device_id=sibling, device_id_type=MESH).wait_recv()
        for cp in sends:
            cp.wait_send()
        for lc in local_copies:
            lc.wait()

    dma = lambda k: pltpu.SemaphoreType.DMA((k,))
    return pl.pallas_call(
        body, in_specs=[ANY] * n, out_specs=[ANY] * n,
        out_shape=[_sds(a.shape if in_place else (4,) + a.shape, a.dtype) for a in arrs],
        scratch_shapes=[dma(3 * n), dma(3 * n), dma(3 * n), dma(3 * n), dma(n)],
        input_output_aliases={a: a for a in range(n)} if in_place else {},
        name=name,
    )(*arrs)


def _carry_pair_exchange(grads):
    n = len(grads)

    def copies(ins, outs, send_sems, recv_sems):
        x, y, c = _place()
        cps = []
        for a in range(n):
            h = ins[a].shape[1] // 2
            theirs = pl.ds(pl.multiple_of((1 - c) * h, 8), h)
            cps.append(pltpu.make_async_remote_copy(
                src_ref=ins[a].at[:, theirs, :], dst_ref=outs[a], send_sem=send_sems.at[a], recv_sem=recv_sems.at[a],
                device_id=(x, y, 1 - c), device_id_type=MESH))
        return cps

    def start(*refs):
        for cp in copies(*refs):
            cp.start()

    def finish(*refs):
        for cp in copies(*refs):
            cp.wait()

    return _Carried(grads, [_sds((4, g.shape[1] // 2, g.shape[2]), F32) for g in grads], {}, n, start, finish)


def _row_steps(h):
    return 1


def _pair_sum(grad, recv, core, name):
    _, r, c = grad.shape
    h = r // 2
    nr = _row_steps(h)
    th = h // nr

    def body(core_ref, g_ref, r_ref, o_ref):
        o_ref[...] = (g_ref[...] + r_ref[...]).astype(BF16)

    return pl.pallas_call(
        body,
        grid_spec=pltpu.PrefetchScalarGridSpec(
            num_scalar_prefetch=1, grid=(4, nr),
            in_specs=[pl.BlockSpec((None, th, c), lambda s, q, core_ref: (s, core_ref[0] * nr + q, 0)),
                      pl.BlockSpec((None, th, c), lambda s, q, core_ref: (s, q, 0))],
            out_specs=pl.BlockSpec((None, th, c), lambda s, q, core_ref: (s, q, 0))),
        out_shape=_sds((4, h, c), BF16), compiler_params=_params(("arbitrary", "arbitrary")), name=name,
    )(core, grad, recv)


def _carry_chip_exchange(parts, swapped):
    n = len(parts)

    def copies(ins, outs, send_sems, recv_sems):
        x, y, c = _place()
        chips = [(_flip(x, k & 2), _flip(y, k & 1)) for k in (1, 2, 3)]
        cps = []
        for a in range(n):
            for j, (px, py) in enumerate(chips):
                cps.append(pltpu.make_async_remote_copy(
                    src_ref=ins[a].at[_slot(px, py, swapped[a])], dst_ref=outs[a].at[j],
                    send_sem=send_sems.at[3 * a + j], recv_sem=recv_sems.at[3 * a + j],
                    device_id=(px, py, c), device_id_type=MESH))
        return cps

    def start(*refs):
        for cp in copies(*refs):
            cp.start()

    def finish(*refs):
        for cp in copies(*refs):
            cp.wait()

    return _Carried(parts, [_sds((3,) + p.shape[1:], BF16) for p in parts], {}, 3 * n, start, finish)


def _chip_sum(part, recv, slot_core, name):
    _, h, c = part.shape
    nr = _row_steps(h)
    th = h // nr

    def body(sc_ref, p_ref, r_ref, o_ref):
        acc = p_ref[...].astype(F32)
        for j in range(3):
            acc = acc + r_ref[j].astype(F32)
        o_ref[...] = acc

    return pl.pallas_call(
        body,
        grid_spec=pltpu.PrefetchScalarGridSpec(
            num_scalar_prefetch=1, grid=(nr,),
            in_specs=[pl.BlockSpec((None, th, c), lambda q, sc_ref: (sc_ref[0], q, 0)),
                      pl.BlockSpec((3, th, c), lambda q, sc_ref: (0, q, 0))],
            out_specs=pl.BlockSpec((th, c), lambda q, sc_ref: (sc_ref[1] * nr + q, 0))),
        out_shape=_sds((2 * h, c), F32), compiler_params=_params(("arbitrary",)), name=name,
    )(slot_core, part, recv)


def _carry_pair_share(shards):
    n = len(shards)

    def copies(outs, send_sems, recv_sems, mine):
        x, y, c = _place()
        cps = []
        for a in range(n):
            h = outs[a].shape[0] // 2
            half = outs[a].at[pl.ds(pl.multiple_of((c if mine else 1 - c) * h, 8), h)]
            cps.append(pltpu.make_async_remote_copy(
                src_ref=half, dst_ref=half, send_sem=send_sems.at[a], recv_sem=recv_sems.at[a],
                device_id=(x, y, 1 - c), device_id_type=MESH))
        return cps

    def start(ins, outs, send_sems, recv_sems):
        for cp in copies(outs, send_sems, recv_sems, True):
            cp.start()

    def finish(ins, outs, send_sems, recv_sems):
        for cp in copies(outs, send_sems, recv_sems, False):
            cp.wait_recv()
        for cp in copies(outs, send_sems, recv_sems, True):
            cp.wait_send()

    return _Carried(shards, [_sds(s.shape, F32) for s in shards], {a: a for a in range(n)}, n, start, finish)


def _carry_gather_ici(bufs, swapped):
    n = len(bufs)

    def copies(outs, send_sems, recv_sems, sending):
        x, y, c = _place()
        cps = []
        for a in range(n):
            h = outs[a].shape[1] // 2
            mine = pl.ds(pl.multiple_of(c * h, 8), h)
            for j, k in enumerate((1, 2, 3)):
                px, py = _flip(x, k & 2), _flip(y, k & 1)
                slot = _slot(x, y, swapped[a]) if sending else _slot(px, py, swapped[a])
                piece = outs[a].at[slot, mine]
                cps.append(pltpu.make_async_remote_copy(
                    src_ref=piece, dst_ref=piece, send_sem=send_sems.at[3 * a + j], recv_sem=recv_sems.at[3 * a + j],
                    device_id=(px, py, c), device_id_type=MESH))
        return cps

    def start(ins, outs, send_sems, recv_sems):
        for cp in copies(outs, send_sems, recv_sems, True):
            cp.start()

    def finish(ins, outs, send_sems, recv_sems):
        for cp in copies(outs, send_sems, recv_sems, False):
            cp.wait_recv()
        for cp in copies(outs, send_sems, recv_sems, True):
            cp.wait_send()

    return _Carried(bufs, [_sds(b.shape, b.dtype) for b in bufs], {a: a for a in range(n)}, 3 * n, start, finish)


def _carry_gather_forward(bufs, swapped):
    n = len(bufs)

    def copies(outs, send_sems, recv_sems, sending):
        x, y, c = _place()
        cps = []
        for a in range(n):
            h = outs[a].shape[1] // 2
            rows = pl.ds(pl.multiple_of((c if sending else 1 - c) * h, 8), h)
            for j, k in enumerate((1, 2, 3)):
                piece = outs[a].at[_slot(_flip(x, k & 2), _flip(y, k & 1), swapped[a]), rows]
                cps.append(pltpu.make_async_remote_copy(
                    src_ref=piece, dst_ref=piece, send_sem=send_sems.at[3 * a + j], recv_sem=recv_sems.at[3 * a + j],
                    device_id=(x, y, 1 - c), device_id_type=MESH))
        return cps

    def start(ins, outs, send_sems, recv_sems):
        for cp in copies(outs, send_sems, recv_sems, True):
            cp.start()

    def finish(ins, outs, send_sems, recv_sems):
        for cp in copies(outs, send_sems, recv_sems, False):
            cp.wait_recv()
        for cp in copies(outs, send_sems, recv_sems, True):
            cp.wait_send()

    return _Carried(bufs, [_sds(b.shape, b.dtype) for b in bufs], {a: a for a in range(n)}, 3 * n, start, finish)


def _pack(arrs, rows_multiple=8):
    parts, offs, row = [], [], 0
    for a in arrs:
        flat = a.reshape(-1)
        nrow = -(-flat.shape[0] // D)
        parts.append(jnp.pad(flat, (0, nrow * D - flat.shape[0])))
        offs.append(row)
        row += nrow
    total = -(-row // rows_multiple) * rows_multiple
    if total > row:
        parts.append(jnp.zeros(((total - row) * D,), F32))
    return jnp.concatenate(parts).reshape(total, D), offs


def _unpack(packed, offs, shapes):
    out = []
    for off, shp in zip(offs, shapes):
        size = int(np.prod(shp))
        nrow = -(-size // D)
        out.append(packed[off:off + nrow].reshape(-1)[:size].reshape(shp))
    return out


def _to_bf16_slot(w, slot, name):
    r, c = w.shape
    tm = next(cand for cand in (256, 176, 128, 64, 32, 16) if r % cand == 0)

    def body(slot_ref, w_ref, o_ref):
        o_ref[...] = w_ref[...].astype(BF16)

    return pl.pallas_call(
        body,
        grid_spec=pltpu.PrefetchScalarGridSpec(
            num_scalar_prefetch=1, grid=(r // tm,),
            in_specs=[pl.BlockSpec((tm, c), lambda i, slot_ref: (i, 0))],
            out_specs=pl.BlockSpec((None, tm, c), lambda i, slot_ref: (slot_ref[0], i, 0))),
        out_shape=_sds((4, r, c), BF16), compiler_params=_params(("arbitrary",)), name=name,
    )(slot, w)


def _unshard_cols(g):
    s, k, n = g.shape
    return jnp.transpose(g, (1, 0, 2)).reshape(k, s * n)


def _ff_swap(v):
    b = FF_BLOCK
    return jnp.concatenate([v[..., 0:b], v[..., 2 * b:3 * b], v[..., b:2 * b], v[..., 3 * b:4 * b]], axis=-1)


LATE = ("attn_o", "conv_o", "mix_o", "up", "down")
EARLY_GRADS = ("down", "up", "mix_o", "attn_o", "conv_o")


def _weight_views(bufs):
    return {"up": bufs["up"], "attn_o": _unshard_cols(bufs["attn_o"]), "conv_o": _unshard_cols(bufs["conv_o"]),
            "mix_o": bufs["mix_o"].reshape(D, D), "down": bufs["down"].reshape(D_FF, D)}


def _pair_sums(names, grads, recv, dist):
    return [_pair_sum(g, r, dist["core"], "pair_sum_" + n) for n, g, r in zip(names, grads, recv)]


def _reduce_halves(names, parts, from_chips, dist):
    return [_chip_sum(p, r, jnp.concatenate([dist["slots"][SWAPPED[n]], dist["core"]]), "chip_sum_" + n)
            for n, p, r in zip(names, parts, from_chips)]


FUSED_TILE = 256
WIDE_TILE = 512


def _gates(z):
    return [(z, 512, 5), (z, 512, 6), (z, 512, 7), (z, 512, 8)]


def _mix_out(a, cb, z, x, w_mix_o, g_post, gt, g_pre2, sc2, sh2, name):
    def lhs(av, cv, ga0, ga1, gb0, gb1):
        ga, gb = jnp.concatenate([ga0, ga1], axis=1), jnp.concatenate([gb0, gb1], axis=1)
        return _sigmoid(ga) * av + _sigmoid(gb) * cv

    def fn(ym, y, xv, gv, gtv, g2v, scv, shv):
        yn, _ = _rms_stats(ym)
        x1 = xv + gtv * (yn * gv)
        xn, _ = _rms_stats(x1)
        return ym, y, x1, (xn * g2v) * (1.0 + scv) + shv

    return _matmul_rows(w_mix_o, form="nn", tm=min(FUSED_TILE, x.shape[0]), tk=D, fn=fn, a_rows=[(a, D, 0), (cb, D, 0)] + _gates(z),
                        a_fn=lhs, rows=[(x, D, 0)], consts=[g_post, gt, g_pre2, sc2, sh2],
                        row_outs=[(F32, D), (BF16, D), (F32, D), (BF16, D)], acc_outs=[], name=name)


def _down_tail(act, w_down, x1, target, g, gt, name):
    def fn(yv, xv, tv, gv, gtv):
        yn, r = _rms_stats(yv)
        e = xv + gtv * (yn * gv) - tv
        dx2 = e * (1.0 / D)
        dyn = dx2 * gtv
        return (dx2, _rms_bwd(dyn * gv, yn, r), _colsum(e * e) * (0.5 / D), _colsum(dyn * yn),
                _colsum(dx2 * (yn * gv)))

    return _matmul_rows(w_down, form="nn", a=act, tm=min(WIDE_TILE, x1.shape[0]), tk=D_FF, fn=fn,
                        rows=[(x1, D, 0), (target, D, 0)], consts=[g, gt], row_outs=[(F32, D), (BF16, D)],
                        acc_outs=[(1, D)] * 3, name=name)


def _up_dx_tail(dup, w_up, x1, dx2, ym, g_pre2, sc2, g_post, gt, name):
    def fn(dh, xv, dov, ymv, g2v, scv, gv, gtv):
        xn, r = _rms_stats(xv)
        dyn = dh * (1.0 + scv)
        dx1 = dov + _rms_bwd(dyn * g2v, xn, r)
        yn, r2 = _rms_stats(ymv)
        dynm = dx1 * gtv
        return (dx1, _rms_bwd(dynm * gv, yn, r2), _colsum(dh), _colsum(dh * (xn * g2v)), _colsum(dyn * xn),
                _colsum(dynm * yn), _colsum(dx1 * (yn * gv)))

    return _matmul_rows(w_up, form="nt", a=dup, tm=min(FUSED_TILE, x1.shape[0]), tk=2 * D_FF, fn=fn,
                        rows=[(x1, D, 0), (dx2, D, 0), (ym, D, 0)], consts=[g_pre2, sc2, g_post, gt],
                        row_outs=[(F32, D), (BF16, D)], acc_outs=[(1, D)] * 5, name=name)


def _mix_dx_gates(dym, w_mix_o, a, cb, z, name):
    def fn(dy, av, cv, ga0, ga1, gb0, gb1):
        sa = _sigmoid(jnp.concatenate([ga0, ga1], axis=1))
        sb = _sigmoid(jnp.concatenate([gb0, gb1], axis=1))
        dcb = dy * sb
        dga = dy * av * (sa * (1.0 - sa))
        dgb = dy * cv * (sb * (1.0 - sb))
        return dy * sa, dcb, dga, dgb, _colsum(dcb), _colsum(dga), _colsum(dgb)

    return _matmul_rows(w_mix_o, form="nt", a=dym, tm=min(FUSED_TILE, a.shape[0]), tk=D, fn=fn,
                        rows=[(a, D, 0), (cb, D, 0)] + _gates(z), consts=[], row_outs=[(BF16, D)] * 4,
                        acc_outs=[(1, D)] * 3, name=name)


def _local_step(x, target, mod, w_in, late, small, dist=None):
    sh_m, sc_m, gt_m, sh_f, sc_f, gt_f = mod
    t = x.shape[0]
    tmm = min(1024, t)
    late_swapped = [SWAPPED[n] for n in LATE]

    h1 = _pre_norm(x, small["g_pre_mix"], sc_m, sh_m, "pre_norm_mix")
    z = _matmul(h1, w_in, form="nn", out_dtype=F32, tm=min(FUSED_TILE, t), tn=D_IN, tk=D, bias=small["b_in"], name="mm_in")
    if dist is None:
        att = _attention(z, small["gen"], "attention")
        bufs = dict(late)
    else:
        mid = [n for n in LATE if n != "down"]
        mid_swapped = [SWAPPED[n] for n in mid]
        att, landed = _attention(z, small["gen"], "attention",
                                 carried=_carry_gather_ici([late[n] for n in mid], mid_swapped))
        bufs = dict(zip(mid, _run_carried(_carry_gather_forward(landed, mid_swapped), "gather_forward")))
        bufs["down"] = late["down"]
    w = _weight_views(bufs)
    w["in"] = w_in
    a = _matmul(att, w["attn_o"], form="nn", out_dtype=F32, tm=tmm, tn=512, tk=512, name="mm_attn_o")
    u1, u3 = _conv_branch(z, small["w_dw_conv"], small["b_dw_conv"], small["g_conv_ln"], small["b_conv_ln"], "conv_branch")
    cb = _matmul(u3, w["conv_o"], form="nn", out_dtype=F32, tm=tmm, tn=512, tk=512, bias=small["b_conv_o"], name="mm_conv_o")
    ym, y, x1, h2 = _mix_out(a, cb, z, x, w["mix_o"], small["g_post_mix"], gt_m, small["g_pre_ffn"], sc_f, sh_f, "mix_out")
    mm_up = dict(form="nn", out_dtype=F32, tm=min(FUSED_TILE, t), tn=2 * D_FF, tk=D, name="mm_up")
    if dist is None:
        up = _matmul(h2, w["up"], **mm_up)
    else:
        up, landed = _matmul(h2, w["up"], carried=_carry_gather_ici([late["down"]], [False]), **mm_up)
        w["down"] = _run_carried(_carry_gather_forward(landed, [False]), "gather_forward_down")[0].reshape(D_FF, D)
    act = _ffn_act(up, small["w_dw_ffn"], small["b_dw_ffn"], "ffn_act")

    dx2, dyf, loss_cols, d_g_post_ffn, d_gt_f = _down_tail(act, w["down"], x1, target, small["g_post_ffn"], gt_f, "down_tail")
    dact = _matmul(dyf, w["down"], form="nt", out_dtype=F32, tm=tmm, tn=FF_BLOCK, tk=D, name="mm_down_dx")
    g_down = _matmul(act, dyf, form="tn", out_dtype=F32, tm=FF_BLOCK, tn=512, tk=t, name="mm_down_dw")
    dup, d_w_dw_ffn, d_b_dw_ffn = _ffn_act_bwd(dact, up, small["w_dw_ffn"], small["b_dw_ffn"], "ffn_act_bwd")
    dx1, dym, d_sh_f, d_sc_f, d_g_pre_ffn, d_g_post_mix, d_gt_m = _up_dx_tail(
        dup, w["up"], x1, dx2, ym, small["g_pre_ffn"], sc_f, small["g_post_mix"], gt_m, "up_dx_tail")
    g_up = _matmul(h2, dup, form="tn", out_dtype=F32, tm=512, tn=FF_BLOCK, tk=t, out_sharded=True, name="mm_up_dw")
    da, dcb, dgate_a, dgate_b, d_b_conv_o, sga, sgb = _mix_dx_gates(dym, w["mix_o"], a, cb, z, "mix_dx_gates")
    g_mix_o = _matmul(y, dym, form="tn", out_dtype=F32, tm=D, tn=512, tk=t, name="mm_mix_o_dw")
    datt = _matmul(da, w["attn_o"], form="nt", out_dtype=F32, tm=tmm, tn=512, tk=D, name="mm_attn_o_dx")
    g_attn_o = _matmul(att, da, form="tn", out_dtype=F32, tm=512, tn=256, tk=t, out_sharded=True, name="mm_attn_o_dw")
    du3 = _matmul(dcb, w["conv_o"], form="nt", out_dtype=F32, tm=tmm, tn=512, tk=D, name="mm_conv_o_dx")
    g_conv_o = _matmul(u3, dcb, form="tn", out_dtype=F32, tm=512, tn=256, tk=t, out_sharded=True, name="mm_conv_o_dw")
    big = {"attn_o": g_attn_o, "conv_o": g_conv_o, "mix_o": g_mix_o.reshape(4, 256, D),
           "up": g_up, "down": g_down.reshape(4, D_FF // 4, D)}
    conv_bwd = (du3, u1, z, small["w_dw_conv"], small["g_conv_ln"], small["b_conv_ln"], "conv_branch_bwd")
    in_dw = dict(form="tn", out_dtype=F32, tm=512, tn=1152, tk=t, out_sharded=True, name="mm_in_dw")
    in_dx = dict(form="nt", out_dtype=F32, tm=min(WIDE_TILE, t), tn=D, tk=D_IN, name="mm_in_dx")
    if dist is None:
        dglu, d_w_dw_conv, d_b_dw_conv, d_g_conv_ln, d_b_conv_ln, sglu = _conv_branch_bwd(*conv_bwd)
        dq, dk, dv, sq, sk, sv, dgen = _attention_bwd(z, datt, small["gen"], "attention_bwd")
        dz = jnp.concatenate([dq, dk, dv, dglu, dgate_a, dgate_b], axis=1)
        big["in"] = _matmul(h1, dz, **in_dw)
        dh1 = _matmul(dz, w_in, **in_dx)
    else:
        early = [big[n] for n in EARLY_GRADS]
        (dglu, d_w_dw_conv, d_b_dw_conv, d_g_conv_ln, d_b_conv_ln, sglu), recv = _conv_branch_bwd(
            *conv_bwd, carried=_carry_pair_exchange(early))
        parts = _pair_sums(EARLY_GRADS, early, recv, dist)
        (dq, dk, dv, sq, sk, sv, dgen), from_chips = _attention_bwd(
            z, datt, small["gen"], "attention_bwd",
            carried=_carry_chip_exchange(parts, [SWAPPED[n] for n in EARLY_GRADS]))
        halves = _reduce_halves(EARLY_GRADS, parts, from_chips, dist)
        dz = jnp.concatenate([dq, dk, dv, dglu, dgate_a, dgate_b], axis=1)
        g_in, shards = _matmul(h1, dz, carried=_carry_pair_share(halves), **in_dw)
        big = dict(zip(EARLY_GRADS, shards))
        recv_in = _run_carried(_carry_pair_exchange([g_in]), "pair_exchange_in")
        part_in = _pair_sums(("in",), [g_in], recv_in, dist)
        dh1, from_chips_in = _matmul(dz, w_in, carried=_carry_chip_exchange(part_in, [False]), **in_dx)
        half_in = _reduce_halves(("in",), part_in, from_chips_in, dist)
        big["in"] = _run_carried(_carry_pair_share(half_in), "pair_share_in")[0]
    d_b_in = jnp.concatenate([sq, sk, sv, sglu, sga, sgb], axis=1)
    grad_x, d_sh_m, d_sc_m, d_g_pre_mix = _pre_norm_bwd(dh1, x, dx1, small["g_pre_mix"], sc_m, "pre_norm_mix_bwd")

    dmod = [d_sh_m, d_sc_m, d_gt_m, d_sh_f, d_sc_f, d_gt_f]
    sm = {"g_pre_mix": d_g_pre_mix, "g_post_mix": d_g_post_mix, "b_in": d_b_in, "gen": dgen,
          "w_dw_conv": d_w_dw_conv, "b_dw_conv": d_b_dw_conv, "g_conv_ln": d_g_conv_ln, "b_conv_ln": d_b_conv_ln,
          "b_conv_o": d_b_conv_o, "g_pre_ffn": d_g_pre_ffn, "g_post_ffn": d_g_post_ffn,
          "w_dw_ffn": d_w_dw_ffn, "b_dw_ffn": d_b_dw_ffn}
    return loss_cols, grad_x, dmod, big, sm


BIG = ("in", "attn_o", "conv_o", "mix_o", "up", "down")
SWAPPED = {"in": False, "attn_o": False, "conv_o": False, "mix_o": False, "up": True, "down": False}
SMALL_ORDER = ("b_ada", "g_pre_mix", "g_post_mix", "b_in", "rel_bias", "b_dw_conv", "g_conv_ln", "b_conv_ln",
               "b_conv_o", "g_pre_ffn", "g_post_ffn", "b_dw_ffn", "w_dw_conv", "w_dw_ffn")


def kernel(x, c, w_ada, b_ada, g_pre_mix, g_post_mix, w_in, b_in, rel_bias, w_attn_o, w_dw_conv, b_dw_conv, g_conv_ln, b_conv_ln, w_conv_o, b_conv_o, w_mix_o, g_pre_ffn, g_post_ffn, w_up, w_dw_ffn, b_dw_ffn, w_down, loss_target, m_w_ada, m_b_ada, m_g_pre_mix, m_g_post_mix, m_w_in, m_b_in, m_rel_bias, m_w_attn_o, m_w_dw_conv, m_b_dw_conv, m_g_conv_ln, m_b_conv_ln, m_w_conv_o, m_b_conv_o, m_w_mix_o, m_g_pre_ffn, m_g_post_ffn, m_w_up, m_w_dw_ffn, m_b_dw_ffn, m_w_down, v_w_ada, v_b_ada, v_g_pre_mix, v_g_post_mix, v_w_in, v_b_in, v_rel_bias, v_w_attn_o, v_w_dw_conv, v_b_dw_conv, v_g_conv_ln, v_b_conv_ln, v_w_conv_o, v_b_conv_o, v_w_mix_o, v_g_pre_ffn, v_g_post_ffn, v_w_up, v_w_dw_ffn, v_b_dw_ffn, v_w_down):
    given = dict(locals())
    ax, ay, ac = lax.axis_index("x"), lax.axis_index("y"), lax.axis_index("c")
    shard = 2 * ax + ay
    me = 4 * ax + 2 * ay + ac
    xs, target = x[0], loss_target[0]

    c_pad = jnp.pad(c, ((0, 7), (0, 0)))
    c_g, _ = _allgather8(c_pad, "gather_c")
    c_all = c_g[:, 0, :]
    b_ada_shard = lax.dynamic_slice(b_ada, (0, shard * 1536), (1, 1536))
    mod_shard, c_act = _ada_fwd(c_all, w_ada[0], b_ada_shard, "ada_fwd")
    small_in = [jnp.pad(mod_shard, ((0, 8), (0, 0))),
                jnp.pad(w_dw_conv[0], ((0, 1), (0, 0))),
                jnp.pad(w_dw_ffn[0], ((0, 13), (0, 0)))]
    mod_g, wdc_g, wdf_g = _gather_shards(small_in, [False, False, True], "gather_small")
    mod_all = jnp.transpose(mod_g[:, :8, :], (1, 0, 2)).reshape(8, 6 * D)
    mod_row = lax.dynamic_slice(mod_all, (me, 0), (1, 6 * D))
    mod = [mod_row[:, k * D:(k + 1) * D] for k in range(6)]

    slots = {sw: _slot(ax, ay, sw).astype(jnp.int32).reshape(1) for sw in (False, True)}
    own = {n: _to_bf16_slot(given["w_" + n][0], slots[SWAPPED[n]], "cast_" + n) for n in BIG}
    w_in_all = _gather_shards([own["in"]], [False], "gather_w_in", in_place=True)[0]
    core = ac.astype(jnp.int32).reshape(1)
    dist = {"core": core, "slots": slots}

    sel = jnp.asarray(_toeplitz_map())
    rel_pad = jnp.pad(rel_bias[0], ((0, 0), (0, REL_PAD - (2 * MAX_REL + 1))))
    gen = _select_call(rel_pad, sel.T.astype(BF16), "bias_rows")
    small = {"g_pre_mix": g_pre_mix, "g_post_mix": g_post_mix, "b_in": b_in, "gen": gen,
             "w_dw_conv": _unshard_cols(wdc_g[:, :CONV_K, :]), "b_dw_conv": b_dw_conv, "g_conv_ln": g_conv_ln,
             "b_conv_ln": b_conv_ln, "b_conv_o": b_conv_o, "g_pre_ffn": g_pre_ffn, "g_post_ffn": g_post_ffn,
             "w_dw_ffn": _unshard_cols(wdf_g[:, :FFN_K, :]), "b_dw_ffn": _ff_swap(b_dw_ffn)}

    loss_cols, grad_x, dmod, reduced, sm = _local_step(xs, target, mod, w_in_all, {n: own[n] for n in LATE}, small, dist)

    d_rel = _select_call(sm["gen"], sel.astype(BF16), "bias_fold")[:, :2 * MAX_REL + 1]
    small_grads = {"g_pre_mix": sm["g_pre_mix"], "g_post_mix": sm["g_post_mix"], "b_in": sm["b_in"], "rel_bias": d_rel[None],
                   "b_dw_conv": sm["b_dw_conv"], "g_conv_ln": sm["g_conv_ln"], "b_conv_ln": sm["b_conv_ln"],
                   "b_conv_o": sm["b_conv_o"], "g_pre_ffn": sm["g_pre_ffn"], "g_post_ffn": sm["g_post_ffn"],
                   "b_dw_ffn": _ff_swap(sm["b_dw_ffn"]), "w_dw_conv": sm["w_dw_conv"], "w_dw_ffn": _ff_swap(sm["w_dw_ffn"])}
    order = [n for n in SMALL_ORDER if n != "b_ada"]
    packed, offs = _pack([jnp.concatenate(dmod, axis=1)] + [small_grads[n] for n in order] + [loss_cols])
    every, total = _allgather8(packed, "gather_small_grads")
    loss = jnp.sum(total[offs[-1]])
    offs = offs[:-1]
    dmod_all = every[:, 0:6, :].reshape(8, 6 * D)
    full_shapes = {n: given[n].shape for n in order}
    full_shapes["w_dw_conv"], full_shapes["w_dw_ffn"] = (1, CONV_K, 512), (1, FFN_K, 2 * D_FF)
    sums = dict(zip(order, _unpack(total, offs[1:], [full_shapes[n] for n in order])))
    sums["b_ada"] = total[0:6].reshape(1, 6 * D)
    sums["w_dw_conv"] = lax.dynamic_slice(sums["w_dw_conv"], (0, 0, shard * 128), (1, CONV_K, 128))
    sums["w_dw_ffn"] = lax.dynamic_slice(sums["w_dw_ffn"], (0, 0, shard * FF_BLOCK), (1, FFN_K, FF_BLOCK))

    upd = dict(zip(SMALL_ORDER, _adamw_many(
        [(given[n], sums[n], given["m_" + n], given["v_" + n]) for n in SMALL_ORDER], "adamw_small")))

    dmod_shard = lax.dynamic_slice(dmod_all, (0, shard * 1536), (8, 1536))
    ada = _ada_bwd_adamw(c_act.T, dmod_shard, w_ada[0], m_w_ada[0], v_w_ada[0], "ada_bwd_adamw")

    out = {"grad_w_ada": ada[0][None], "delta_w_ada": ada[1][None], "new_m_w_ada": ada[2][None], "new_v_w_ada": ada[3][None]}
    for n in BIG:
        g = reduced[n]
        g, dl, nm, nv = _adamw(given["w_" + n][0], g, given["m_w_" + n][0], given["v_w_" + n][0], "adamw_" + n)
        out["grad_w_" + n], out["delta_w_" + n], out["new_m_w_" + n], out["new_v_w_" + n] = g[None], dl[None], nm[None], nv[None]
    for n in SMALL_ORDER:
        out["grad_" + n], out["delta_" + n], out["new_m_" + n], out["new_v_" + n] = sums[n], *upd[n]

    weights = ["w_ada", "b_ada", "g_pre_mix", "g_post_mix", "w_in", "b_in", "rel_bias", "w_attn_o", "w_dw_conv", "b_dw_conv",
               "g_conv_ln", "b_conv_ln", "w_conv_o", "b_conv_o", "w_mix_o", "g_pre_ffn", "g_post_ffn", "w_up", "w_dw_ffn",
               "b_dw_ffn", "w_down"]
    return (loss, grad_x[None], *[out["grad_" + n] for n in weights], *[out["delta_" + n] for n in weights],
            *[out["new_m_" + n] for n in weights], *[out["new_v_" + n] for n in weights])
```

```python
import functools
import math

import numpy as np
import jax
import jax.numpy as jnp
from jax import lax
from jax.experimental import pallas as pl
from jax.experimental.pallas import tpu as pltpu

F32, BF16 = jnp.float32, jnp.bfloat16
MESH = pl.DeviceIdType.MESH

D = 1024
D_IN = 4608
D_FF = 2816
CONV_K = 31
FFN_K = 3
N_HEADS = 8
CHUNK = 64
LEFT_CHUNKS = 8
MAX_REL = 128
EPS = 1e-6
NEG_INF = -1e30
Q_TILE = 256
WINDOW = Q_TILE + LEFT_CHUNKS * CHUNK
STEP_ROWS = 256
REL_PAD = 384
TOEP = 1024
ROW_TILE = 256
VMEM_LIMIT = 60 * 1024 * 1024

ADAM_LR, ADAM_B1, ADAM_B2, ADAM_EPS, ADAM_WD, ADAM_STEP = 0.001, 0.9, 0.999, 1e-08, 0.01, 10


def _params(sem=None):
    return pltpu.CompilerParams(dimension_semantics=sem, vmem_limit_bytes=VMEM_LIMIT)


def _sds(shape, dtype):
    return jax.ShapeDtypeStruct(tuple(shape), dtype)


ANY = pl.BlockSpec(memory_space=pl.ANY)


class _Carried:
    def __init__(self, ins, out_shapes, aliases, n_sems, start, finish):
        self.ins, self.out_shapes, self.aliases = list(ins), list(out_shapes), dict(aliases)
        self.n_sems, self.start, self.finish = n_sems, start, finish


def _call(body, *, grid, in_specs, out_specs, out_shape, scratch_shapes, sem, name, args, carried=None):
    in_specs, out_specs, out_shape = list(in_specs), list(out_specs), list(out_shape)
    scratch_shapes = list(scratch_shapes)
    if carried is None:
        return pl.pallas_call(body, grid=grid, in_specs=in_specs, out_specs=out_specs, out_shape=out_shape,
                              scratch_shapes=scratch_shapes, compiler_params=_params(sem), name=name)(*args)
    n_in, n_out, n_scr = len(in_specs), len(out_specs), len(scratch_shapes)
    c_in, c_out = len(carried.ins), len(carried.out_shapes)

    def full(*refs):
        pos = [0]

        def take(k):
            part = refs[pos[0]:pos[0] + k]
            pos[0] += k
            return part

        ins, cins, outs, couts, scr = take(n_in), take(c_in), take(n_out), take(c_out), take(n_scr)
        send_sems, recv_sems = take(2)
        first = last = None
        for d, size in enumerate(grid):
            pid = pl.program_id(d)
            first = (pid == 0) if first is None else first & (pid == 0)
            last = (pid == size - 1) if last is None else last & (pid == size - 1)

        @pl.when(first)
        def _():
            carried.start(cins, couts, send_sems, recv_sems)

        body(*ins, *outs, *scr)

        @pl.when(last)
        def _():
            carried.finish(cins, couts, send_sems, recv_sems)

    sems = [pltpu.SemaphoreType.DMA((carried.n_sems,)), pltpu.SemaphoreType.DMA((carried.n_sems,))]
    return pl.pallas_call(
        full, grid=grid, in_specs=in_specs + [ANY] * c_in, out_specs=out_specs + [ANY] * c_out,
        out_shape=out_shape + carried.out_shapes, scratch_shapes=scratch_shapes + sems,
        input_output_aliases={n_in + k: n_out + v for k, v in carried.aliases.items()},
        compiler_params=_params(tuple("arbitrary" for _ in grid)), name=name,
    )(*args, *carried.ins)


def _run_carried(carried, name):
    c_in = len(carried.ins)

    def body(*refs):
        cins, couts = refs[:c_in], refs[c_in:c_in + len(carried.out_shapes)]
        send_sems, recv_sems = refs[-2:]
        carried.start(cins, couts, send_sems, recv_sems)
        carried.finish(cins, couts, send_sems, recv_sems)

    return pl.pallas_call(
        body, in_specs=[ANY] * c_in, out_specs=[ANY] * len(carried.out_shapes), out_shape=carried.out_shapes,
        scratch_shapes=[pltpu.SemaphoreType.DMA((carried.n_sems,)), pltpu.SemaphoreType.DMA((carried.n_sems,))],
        input_output_aliases=carried.aliases, name=name,
    )(*carried.ins)


def _matmul(a, b, *, form, out_dtype, tm, tn, tk, name, bias=None, add=None, out_sharded=False, carried=None):
    b3 = b.ndim == 3
    resident = 0
    if form == "nn":
        m, k = a.shape
        n = b.shape[0] * b.shape[2] if b3 else b.shape[1]
        dn = (((1,), (0,)), ((), ()))
        a_spec = pl.BlockSpec((tm, tk), lambda i, j, kk: (i, kk))
        if b3 and tn == n and tk == k:
            resident = b.shape[0]
            b_spec = pl.BlockSpec(b.shape, lambda i, j, kk: (0, 0, 0))
        else:
            b_spec = (pl.BlockSpec((None, tk, tn), lambda i, j, kk: (j, kk, 0)) if b3
                      else pl.BlockSpec((tk, tn), lambda i, j, kk: (kk, j)))
    elif form == "nt":
        m, k = a.shape
        n = b.shape[1] if b3 else b.shape[0]
        dn = (((1,), (1,)), ((), ()))
        a_spec = pl.BlockSpec((tm, tk), lambda i, j, kk: (i, kk))
        if b3 and tk == k:
            resident = b.shape[0]
            b_spec = pl.BlockSpec((resident, tn, b.shape[2]), lambda i, j, kk: (0, j, 0))
        else:
            b_spec = (pl.BlockSpec((None, tn, tk), lambda i, j, kk: (kk, j, 0)) if b3
                      else pl.BlockSpec((tn, tk), lambda i, j, kk: (j, kk)))
    else:
        k, m = a.shape
        n = b.shape[1]
        dn = (((0,), (0,)), ((), ()))
        a_spec = pl.BlockSpec((tk, tm), lambda i, j, kk: (kk, i))
        b_spec = pl.BlockSpec((tk, tn), lambda i, j, kk: (kk, j))
    assert m % tm == 0 and n % tn == 0 and k % tk == 0, (name, m, n, k, tm, tn, tk)
    nk = k // tk
    in_specs, args = [a_spec, b_spec], [a, b]
    if bias is not None:
        in_specs.append(pl.BlockSpec((1, tn), lambda i, j, kk: (0, j)))
        args.append(bias)
    if add is not None:
        in_specs.append(pl.BlockSpec((tm, tn), lambda i, j, kk: (i, j)))
        args.append(add)
    if out_sharded:
        out_shape = _sds((n // tn, m, tn), out_dtype)
        out_spec = pl.BlockSpec((None, tm, tn), lambda i, j, kk: (j, i, 0))
    else:
        out_shape = _sds((m, n), out_dtype)
        out_spec = pl.BlockSpec((tm, tn), lambda i, j, kk: (i, j))

    def body(*refs):
        a_ref, b_ref = refs[0], refs[1]
        pos = 2
        bias_ref = add_ref = None
        if bias is not None:
            bias_ref, pos = refs[pos], pos + 1
        if add is not None:
            add_ref, pos = refs[pos], pos + 1
        o_ref = refs[pos]
        if resident and form == "nn":
            ns = b_ref.shape[2]
            for s in range(resident):
                cols = slice(s * ns, (s + 1) * ns)
                ps = lax.dot_general(a_ref[...], b_ref[s], dn, preferred_element_type=F32)
                if bias_ref is not None:
                    ps = ps + bias_ref[:, cols]
                o_ref[:, cols] = ps.astype(o_ref.dtype)
            return
        if resident:
            ks = b_ref.shape[2]
            p = None
            for s in range(resident):
                ps = lax.dot_general(a_ref[:, s * ks:(s + 1) * ks], b_ref[s], dn, preferred_element_type=F32)
                p = ps if p is None else p + ps
        else:
            av, bv = a_ref[...], b_ref[...]
            if av.dtype != BF16:
                av = av.astype(BF16)
            if bv.dtype != BF16:
                bv = bv.astype(BF16)
            p = lax.dot_general(av, bv, dn, preferred_element_type=F32)

        def finish(acc):
            if bias_ref is not None:
                acc = acc + bias_ref[...]
            if add_ref is not None:
                acc = acc + add_ref[...]
            o_ref[...] = acc.astype(o_ref.dtype)

        if nk == 1:
            finish(p)
        else:
            acc_ref = refs[pos + 1]
            kk = pl.program_id(2)

            @pl.when(kk == 0)
            def _():
                acc_ref[...] = p

            @pl.when(kk > 0)
            def _():
                acc_ref[...] += p

            @pl.when(kk == nk - 1)
            def _():
                finish(acc_ref[...])

    res = _call(body, grid=(m // tm, n // tn, nk), in_specs=in_specs, out_specs=[out_spec], out_shape=[out_shape],
                scratch_shapes=[pltpu.VMEM((tm, tn), F32)] if nk > 1 else [],
                sem=("parallel", "parallel", "arbitrary"), name=name, args=args, carried=carried)
    return res[0] if carried is None else (res[0], res[1:])


def _rowcall(fn, rows, consts, row_outs, acc_outs, *, name, tm=ROW_TILE, col_grid=1):
    n_rows = rows[0][0].shape[0]
    assert n_rows % tm == 0
    grid = (col_grid, n_rows // tm)
    in_specs = [pl.BlockSpec((tm, w), functools.partial(lambda c, i, cb: (i, cb + c), cb=cb)) for _, w, cb in rows]
    in_specs += [pl.BlockSpec(k.shape, functools.partial(lambda c, i, nd: (0,) * nd, nd=k.ndim)) for k in consts]
    out_specs = [pl.BlockSpec((tm, w), lambda c, i: (i, c)) for _, _, _, w in row_outs]
    out_specs += [pl.BlockSpec((r, w), lambda c, i: (0, c)) for r, _, w in acc_outs]
    out_shape = [_sds((nr, nc), dt) for nr, nc, dt, _ in row_outs] + [_sds((r, nc), F32) for r, nc, _ in acc_outs]
    n_in, n_ro = len(rows) + len(consts), len(row_outs)

    def body(*refs):
        res = fn(*[r[...] for r in refs[:n_in]])
        if not isinstance(res, (tuple, list)):
            res = (res,)
        outs = refs[n_in:]
        for o_ref, val in zip(outs[:n_ro], res[:n_ro]):
            o_ref[...] = val.astype(o_ref.dtype)
        if acc_outs:
            first = pl.program_id(1) == 0

            @pl.when(first)
            def _():
                for o_ref, val in zip(outs[n_ro:], res[n_ro:]):
                    o_ref[...] = val

            @pl.when(jnp.logical_not(first))
            def _():
                for o_ref, val in zip(outs[n_ro:], res[n_ro:]):
                    o_ref[...] += val

    out = pl.pallas_call(
        body, grid=grid, in_specs=in_specs, out_specs=out_specs, out_shape=out_shape,
        compiler_params=_params(("arbitrary", "arbitrary")), name=name,
    )(*[r[0] for r in rows], *consts)
    return out


def _matmul_rows(b, *, form, tm, tk, fn, rows, consts, row_outs, acc_outs, name, a=None, a_rows=None, a_fn=None,
                 carried=None):
    b3 = b.ndim == 3
    resident = 0
    if form == "nn":
        k, n = b.shape
        b_spec = pl.BlockSpec((tk, n), lambda i, kk: (kk, 0))
        dn = (((1,), (0,)), ((), ()))
    else:
        n = b.shape[1] if b3 else b.shape[0]
        k = b.shape[0] * b.shape[2] if b3 else b.shape[1]
        if b3 and tk == k:
            resident = b.shape[0]
            b_spec = pl.BlockSpec(b.shape, lambda i, kk: (0, 0, 0))
        else:
            b_spec = (pl.BlockSpec((None, n, tk), lambda i, kk: (kk, 0, 0)) if b3
                      else pl.BlockSpec((n, tk), lambda i, kk: (0, kk)))
        dn = (((1,), (1,)), ((), ()))
    nk = k // tk
    lhs_in = [(a, tk, 0)] if a is not None else list(a_rows)
    assert a is not None or nk == 1
    m = lhs_in[0][0].shape[0]
    n_lhs = len(lhs_in)
    in_specs = [pl.BlockSpec((tm, tk), lambda i, kk: (i, kk))] if a is not None else [
        pl.BlockSpec((tm, w), functools.partial(lambda i, kk, cb: (i, cb), cb=cb)) for _, w, cb in a_rows]
    in_specs.append(b_spec)
    in_specs += [pl.BlockSpec((tm, w), functools.partial(lambda i, kk, cb: (i, cb), cb=cb)) for _, w, cb in rows]
    in_specs += [pl.BlockSpec(c.shape, functools.partial(lambda i, kk, nd: (0,) * nd, nd=c.ndim)) for c in consts]
    out_specs = [pl.BlockSpec((tm, w), lambda i, kk: (i, 0)) for _, w in row_outs]
    out_specs += [pl.BlockSpec((r, w), lambda i, kk: (0, 0)) for r, w in acc_outs]
    out_shape = [_sds((m, w), dt) for dt, w in row_outs] + [_sds((r, w), F32) for r, w in acc_outs]
    n_rows, n_consts, n_ro, n_acc = len(rows), len(consts), len(row_outs), len(acc_outs)

    def body(*refs):
        pos = n_lhs + 1
        row_refs, const_refs = refs[pos:pos + n_rows], refs[pos + n_rows:pos + n_rows + n_consts]
        pos += n_rows + n_consts
        out_refs, acc_refs = refs[pos:pos + n_ro], refs[pos + n_ro:pos + n_ro + n_acc]
        i, kk = pl.program_id(0), pl.program_id(1)
        if resident:
            b_ref, ks, p = refs[n_lhs], b.shape[2], None
            for s in range(resident):
                ps = lax.dot_general(refs[0][:, s * ks:(s + 1) * ks], b_ref[s], dn, preferred_element_type=F32)
                p = ps if p is None else p + ps
        else:
            lhs = refs[0][...] if a is not None else a_fn(*[r[...] for r in refs[:n_lhs]]).astype(BF16)
            p = lax.dot_general(lhs, refs[n_lhs][...], dn, preferred_element_type=F32)

        def finish(acc):
            extra = [r[...] for r in row_refs] + [c[...] for c in const_refs]
            res = fn(acc, lhs, *extra) if a is None else fn(acc, *extra)
            for o_ref, val in zip(out_refs, res[:n_ro]):
                o_ref[...] = val.astype(o_ref.dtype)
            if n_acc:
                @pl.when(i == 0)
                def _():
                    for o_ref, val in zip(acc_refs, res[n_ro:]):
                        o_ref[...] = val

                @pl.when(i > 0)
                def _():
                    for o_ref, val in zip(acc_refs, res[n_ro:]):
                        o_ref[...] += val

        if nk == 1:
            finish(p)
        else:
            acc_ref = refs[pos + n_ro + n_acc]

            @pl.when(kk == 0)
            def _():
                acc_ref[...] = p

            @pl.when(kk > 0)
            def _():
                acc_ref[...] += p

            @pl.when(kk == nk - 1)
            def _():
                finish(acc_ref[...])

    res = _call(body, grid=(m // tm, nk), in_specs=in_specs, out_specs=out_specs, out_shape=out_shape,
                scratch_shapes=[pltpu.VMEM((tm, n), F32)] if nk > 1 else [], sem=("arbitrary", "arbitrary"),
                name=name, args=[r[0] for r in lhs_in] + [b] + [r[0] for r in rows] + list(consts), carried=carried)
    own = n_ro + n_acc
    return res[:own] if carried is None else (res[:own], res[own:])


def _colsum(v):
    return jnp.sum(v, axis=0, keepdims=True)


def _sigmoid(v):
    return 1.0 / (1.0 + jnp.exp(-v))


_GELU_C = math.sqrt(2.0 / math.pi)


def _gelu(v):
    return 0.5 * v * (1.0 + jnp.tanh(_GELU_C * (v + 0.044715 * (v * v * v))))


def _gelu_and_grad(v):
    th = jnp.tanh(_GELU_C * (v + 0.044715 * (v * v * v)))
    g = 0.5 * v * (1.0 + th)
    dg = 0.5 * (1.0 + th) + 0.5 * v * (1.0 - th * th) * (_GELU_C * (1.0 + 3.0 * 0.044715 * (v * v)))
    return g, dg


def _rms_stats(v):
    r = lax.rsqrt(jnp.mean(v * v, axis=-1, keepdims=True) + EPS)
    return v * r, r


def _rms_bwd(dn, vn, r):
    return r * (dn - vn * jnp.mean(dn * vn, axis=-1, keepdims=True))


def _pre_norm(x, g, sc, sh, name):
    def fn(xv, gv, scv, shv):
        xn, _ = _rms_stats(xv)
        return (xn * gv) * (1.0 + scv) + shv
    return _rowcall(fn, [(x, D, 0)], [g, sc, sh], [(x.shape[0], D, BF16, D)], [], name=name)[0]


def _pre_norm_bwd(dh, x, dx_other, g, sc, name):
    def fn(dhv, xv, dov, gv, scv):
        xn, r = _rms_stats(xv)
        yn = xn * gv
        dyn = dhv * (1.0 + scv)
        dx = _rms_bwd(dyn * gv, xn, r)
        return dov + dx, _colsum(dhv), _colsum(dhv * yn), _colsum(dyn * xn)
    t = x.shape[0]
    return _rowcall(fn, [(dh, D, 0), (x, D, 0), (dx_other, D, 0)], [g, sc], [(t, D, F32, D)],
                    [(1, D, D)] * 3, name=name)


CONV_HALO = 32


def _layer_norm_parts(u):
    mu = jnp.mean(u, axis=-1, keepdims=True)
    d = u - mu
    r = lax.rsqrt(jnp.mean(d * d, axis=-1, keepdims=True) + EPS)
    return d * r, r


LANES = 128
SUBLANE_ROWS = 8
CONV_ROWS = 64


def _lanes(c):
    return slice(c * LANES, (c + 1) * LANES)


def _conv_branch(z, w_dw, b_dw, g_ln, b_ln, name, tm=ROW_TILE):
    t = z.shape[0]
    per = tm // CONV_HALO
    n_chunks = 512 // LANES

    def body(ga_ref, gb_ref, gah_ref, gbh_ref, w_ref, b_ref, g_ref, bl_ref, u1_ref, u3_ref, scr):
        i = pl.program_id(0)
        u0h = jnp.where(i > 0, gah_ref[...] * _sigmoid(gbh_ref[...]), 0.0)
        u0 = ga_ref[...] * _sigmoid(gb_ref[...])
        for c in range(n_chunks):
            scr[c, 0:CONV_HALO, :] = u0h[:, _lanes(c)]
            scr[c, CONV_HALO:CONV_HALO + tm, :] = u0[:, _lanes(c)]
        for c in range(n_chunks):
            for r0 in range(0, tm, CONV_ROWS):
                acc = jnp.zeros((CONV_ROWS, LANES), F32) + b_ref[:, _lanes(c)]
                for j in range(CONV_K):
                    acc = acc + w_ref[j:j + 1, _lanes(c)] * scr[c, pl.ds(r0 + CONV_HALO - (CONV_K - 1) + j, CONV_ROWS), :]
                u1_ref[r0:r0 + CONV_ROWS, _lanes(c)] = acc
        xh, _ = _layer_norm_parts(u1_ref[...])
        u2 = xh * g_ref[...] + bl_ref[...]
        u3_ref[...] = (u2 * _sigmoid(u2)).astype(BF16)

    cur = lambda cb: pl.BlockSpec((tm, 512), lambda i: (i, cb))
    halo = lambda cb: pl.BlockSpec((CONV_HALO, 512), lambda i: (jnp.maximum(i * per - 1, 0), cb))
    whole = lambda a: pl.BlockSpec(a.shape, lambda i: (0, 0))
    return pl.pallas_call(
        body, grid=(t // tm,),
        in_specs=[cur(3), cur(4), halo(3), halo(4), whole(w_dw), whole(b_dw), whole(g_ln), whole(b_ln)],
        out_specs=[pl.BlockSpec((tm, 512), lambda i: (i, 0))] * 2,
        out_shape=[_sds((t, 512), F32), _sds((t, 512), BF16)],
        scratch_shapes=[pltpu.VMEM((n_chunks, CONV_HALO + tm, LANES), F32)],
        compiler_params=_params(("arbitrary",)), name=name,
    )(z, z, z, z, w_dw, b_dw, g_ln, b_ln)


def _conv_branch_bwd(du3, u1, z, w_dw, g_ln, b_ln, name, tm=ROW_TILE, carried=None):
    t = z.shape[0]
    per = tm // CONV_HALO
    last = t // tm - 1
    n_chunks = 512 // LANES

    def du1_of(du3v, u1v, g, b):
        xh, r = _layer_norm_parts(u1v)
        u2 = xh * g + b
        s = _sigmoid(u2)
        du2 = du3v * (s * (1.0 + u2 * (1.0 - s)))
        dxh = du2 * g
        du1 = r * (dxh - jnp.mean(dxh, axis=-1, keepdims=True) - xh * jnp.mean(dxh * xh, axis=-1, keepdims=True))
        return du1, du2, xh

    def body(d_ref, u_ref, dn_ref, un_ref, ga_ref, gb_ref, gah_ref, gbh_ref, w_ref, g_ref, bl_ref,
             dglu_ref, dw_ref, dbdw_ref, dg_ref, dbl_ref, dbin_ref, scr, scd):
        i = pl.program_id(0)
        g, b = g_ref[...], bl_ref[...]
        du1, du2, xh = du1_of(d_ref[...], u_ref[...], g, b)
        du1n, _, _ = du1_of(dn_ref[...], un_ref[...], g, b)
        du1n = jnp.where(i < last, du1n, 0.0)
        sgb = _sigmoid(gb_ref[...])
        ga = ga_ref[...]
        u0 = ga * sgb
        u0h = jnp.where(i > 0, gah_ref[...] * _sigmoid(gbh_ref[...]), 0.0)
        for c in range(n_chunks):
            scd[c, 0:tm, :] = du1[:, _lanes(c)]
            scd[c, tm:tm + CONV_HALO, :] = du1n[:, _lanes(c)]
            scr[c, 0:CONV_HALO, :] = u0h[:, _lanes(c)]
            scr[c, CONV_HALO:CONV_HALO + tm, :] = u0[:, _lanes(c)]

        @pl.when(i == 0)
        def _():
            for ref in (dw_ref, dbdw_ref, dg_ref, dbl_ref, dbin_ref):
                ref[...] = jnp.zeros_like(ref)

        dsg = ga * (sgb * (1.0 - sgb))
        for c in range(n_chunks):
            gate = slice(512 + c * LANES, 512 + (c + 1) * LANES)
            for r0 in range(0, tm, CONV_ROWS):
                rows = slice(r0, r0 + CONV_ROWS)
                du0 = jnp.zeros((CONV_ROWS, LANES), F32)
                for j in range(CONV_K):
                    du0 = du0 + w_ref[j:j + 1, _lanes(c)] * scd[c, pl.ds(r0 + CONV_K - 1 - j, CONV_ROWS), :]
                dga = du0 * sgb[rows, _lanes(c)]
                dgb = du0 * dsg[rows, _lanes(c)]
                dglu_ref[rows, _lanes(c)] = dga.astype(BF16)
                dglu_ref[rows, gate] = dgb.astype(BF16)
                dbin_ref[:, _lanes(c)] += _colsum(dga)
                dbin_ref[:, gate] += _colsum(dgb)
            for j in range(CONV_K):
                dwj = jnp.zeros((SUBLANE_ROWS, LANES), F32)
                for r0 in range(0, tm, CONV_ROWS):
                    prod = (scd[c, pl.ds(r0, CONV_ROWS), :]
                            * scr[c, pl.ds(r0 + CONV_HALO - (CONV_K - 1) + j, CONV_ROWS), :])
                    dwj = dwj + jnp.sum(prod.reshape(CONV_ROWS // SUBLANE_ROWS, SUBLANE_ROWS, LANES), axis=0)
                dw_ref[j:j + 1, _lanes(c)] += _colsum(dwj)
        dbdw_ref[...] += _colsum(du1)
        dg_ref[...] += _colsum(du2 * xh)
        dbl_ref[...] += _colsum(du2)

    cur = lambda cb: pl.BlockSpec((tm, 512), lambda i: (i, cb))
    prev = lambda cb: pl.BlockSpec((CONV_HALO, 512), lambda i: (jnp.maximum(i * per - 1, 0), cb))
    nxt = pl.BlockSpec((CONV_HALO, 512), lambda i: (jnp.minimum((i + 1) * per, t // CONV_HALO - 1), 0))
    whole = lambda a: pl.BlockSpec(a.shape, lambda i: (0, 0))
    acc = lambda r, w: pl.BlockSpec((r, w), lambda i: (0, 0))
    res = _call(
        body, grid=(t // tm,),
        in_specs=[cur(0), cur(0), nxt, nxt, cur(3), cur(4), prev(3), prev(4), whole(w_dw), whole(g_ln), whole(b_ln)],
        out_specs=[pl.BlockSpec((tm, 1024), lambda i: (i, 0)), acc(CONV_K, 512), acc(1, 512), acc(1, 512),
                   acc(1, 512), acc(1, 1024)],
        out_shape=[_sds((t, 1024), BF16), _sds((CONV_K, 512), F32), _sds((1, 512), F32), _sds((1, 512), F32),
                   _sds((1, 512), F32), _sds((1, 1024), F32)],
        scratch_shapes=[pltpu.VMEM((n_chunks, CONV_HALO + tm, LANES), F32),
                        pltpu.VMEM((n_chunks, tm + CONV_HALO, LANES), F32)],
        sem=("arbitrary",), name=name, args=(du3, u1, du3, u1, z, z, z, z, w_dw, g_ln, b_ln), carried=carried)
    return res[:6] if carried is None else (res[:6], res[6:])


FF_BLOCK = D_FF // 2
FF_HALO = 8
FF_CHUNKS = FF_BLOCK // LANES


FF_ROWS = 64
FF_EXT_ROWS = 88


def _ffn_conv(w_ref, b_ref, scr, k, rows, r0=0):
    acc = b_ref[:, _lanes(k)] + w_ref[0:1, _lanes(k)] * scr[k, pl.ds(r0 + FF_HALO - 2, rows), :]
    acc = acc + w_ref[1:2, _lanes(k)] * scr[k, pl.ds(r0 + FF_HALO - 1, rows), :]
    return acc + w_ref[2:3, _lanes(k)] * scr[k, pl.ds(r0 + FF_HALO, rows), :]


def _ffn_act(up, w3, b3, name, tm=ROW_TILE):
    t = up.shape[0]
    per = tm // FF_HALO
    wide = 2 * FF_BLOCK

    def body(u_ref, uh_ref, w_ref, b_ref, o_ref, scr):
        i = pl.program_id(1)
        for k in range(2 * FF_CHUNKS):
            scr[k, 0:FF_HALO, :] = jnp.where(i > 0, uh_ref[:, _lanes(k)], 0.0)
            scr[k, FF_HALO:FF_HALO + tm, :] = u_ref[:, _lanes(k)]
        for cc in range(FF_CHUNKS):
            for r0 in range(0, tm, FF_ROWS):
                val = _ffn_conv(w_ref, b_ref, scr, cc, FF_ROWS, r0)
                gate = _ffn_conv(w_ref, b_ref, scr, FF_CHUNKS + cc, FF_ROWS, r0)
                o_ref[r0:r0 + FF_ROWS, _lanes(cc)] = (_gelu(gate) * val).astype(BF16)

    return pl.pallas_call(
        body, grid=(2, t // tm),
        in_specs=[pl.BlockSpec((tm, wide), lambda c, i: (i, c)),
                  pl.BlockSpec((FF_HALO, wide), lambda c, i: (jnp.maximum(i * per - 1, 0), c)),
                  pl.BlockSpec((FFN_K, wide), lambda c, i: (0, c)),
                  pl.BlockSpec((1, wide), lambda c, i: (0, c))],
        out_specs=pl.BlockSpec((tm, FF_BLOCK), lambda c, i: (i, c)),
        out_shape=_sds((t, D_FF), BF16),
        scratch_shapes=[pltpu.VMEM((2 * FF_CHUNKS, FF_HALO + tm, LANES), F32)],
        compiler_params=_params(("arbitrary", "arbitrary")), name=name,
    )(up, up, w3, b3)


def _ffn_act_bwd(dact, up, w3, b3, name, tm=ROW_TILE):
    t = up.shape[0]
    per = tm // FF_HALO
    wide = 2 * FF_BLOCK
    last = t // tm - 1
    ext = tm + FF_HALO

    def body(u_ref, up_ref, un_ref, d_ref, dn_ref, w_ref, b_ref, o_ref, dw_ref, db_ref, scr, scd):
        i = pl.program_id(1)
        for k in range(2 * FF_CHUNKS):
            scr[k, 0:FF_HALO, :] = jnp.where(i > 0, up_ref[:, _lanes(k)], 0.0)
            scr[k, FF_HALO:FF_HALO + tm, :] = u_ref[:, _lanes(k)]
            scr[k, FF_HALO + tm:FF_HALO + ext, :] = un_ref[:, _lanes(k)]
        dn = jnp.where(i < last, dn_ref[...], 0.0)

        @pl.when(i == 0)
        def _():
            dw_ref[...] = jnp.zeros_like(dw_ref)
            db_ref[...] = jnp.zeros_like(db_ref)

        for cc in range(FF_CHUNKS):
            gc = FF_CHUNKS + cc
            for r0 in range(0, ext, FF_EXT_ROWS):
                rows = pl.ds(r0, FF_EXT_ROWS)
                val = _ffn_conv(w_ref, b_ref, scr, cc, FF_EXT_ROWS, r0)
                gel, dgel = _gelu_and_grad(_ffn_conv(w_ref, b_ref, scr, gc, FF_EXT_ROWS, r0))
                da = d_ref[r0:r0 + FF_EXT_ROWS, _lanes(cc)] if r0 + FF_EXT_ROWS <= tm else jnp.concatenate(
                    [d_ref[r0:tm, _lanes(cc)], dn[:, _lanes(cc)]], axis=0)
                scd[cc, rows, :] = da * gel
                scd[gc, rows, :] = da * val * dgel
            for k in (cc, gc):
                dwk = [jnp.zeros((SUBLANE_ROWS, LANES), F32) for _ in range(FFN_K)]
                dbk = jnp.zeros((SUBLANE_ROWS, LANES), F32)
                for r0 in range(0, tm, FF_ROWS):
                    shifted = [scd[k, pl.ds(r0 + FFN_K - 1 - j, FF_ROWS), :] for j in range(FFN_K)]
                    ucur = scr[k, pl.ds(r0 + FF_HALO, FF_ROWS), :]
                    o_ref[r0:r0 + FF_ROWS, _lanes(k)] = (
                        w_ref[0:1, _lanes(k)] * shifted[0] + w_ref[1:2, _lanes(k)] * shifted[1]
                        + w_ref[2:3, _lanes(k)] * shifted[2]).astype(BF16)
                    fold = lambda v: jnp.sum(v.reshape(FF_ROWS // SUBLANE_ROWS, SUBLANE_ROWS, LANES), axis=0)
                    for j in range(FFN_K):
                        dwk[j] = dwk[j] + fold(shifted[j] * ucur)
                    dbk = dbk + fold(shifted[FFN_K - 1])
                for j in range(FFN_K):
                    dw_ref[j:j + 1, _lanes(k)] += _colsum(dwk[j])
                db_ref[:, _lanes(k)] += _colsum(dbk)

    nblk = t // FF_HALO
    return pl.pallas_call(
        body, grid=(2, t // tm),
        in_specs=[pl.BlockSpec((tm, wide), lambda c, i: (i, c)),
                  pl.BlockSpec((FF_HALO, wide), lambda c, i: (jnp.maximum(i * per - 1, 0), c)),
                  pl.BlockSpec((FF_HALO, wide), lambda c, i: (jnp.minimum((i + 1) * per, nblk - 1), c)),
                  pl.BlockSpec((tm, FF_BLOCK), lambda c, i: (i, c)),
                  pl.BlockSpec((FF_HALO, FF_BLOCK), lambda c, i: (jnp.minimum((i + 1) * per, nblk - 1), c)),
                  pl.BlockSpec((FFN_K, wide), lambda c, i: (0, c)),
                  pl.BlockSpec((1, wide), lambda c, i: (0, c))],
        out_specs=[pl.BlockSpec((tm, wide), lambda c, i: (i, c)),
                   pl.BlockSpec((FFN_K, wide), lambda c, i: (0, c)),
                   pl.BlockSpec((1, wide), lambda c, i: (0, c))],
        out_shape=[_sds((t, 2 * D_FF), BF16), _sds((FFN_K, 2 * D_FF), F32), _sds((1, 2 * D_FF), F32)],
        scratch_shapes=[pltpu.VMEM((2 * FF_CHUNKS, FF_HALO + ext, LANES), F32),
                        pltpu.VMEM((2 * FF_CHUNKS, ext, LANES), F32)],
        compiler_params=_params(("arbitrary", "arbitrary")), name=name,
    )(up, up, up, dact, dact, w3, b3)


def _toeplitz_map():
    f = np.zeros((TOEP, REL_PAD), np.float32)
    for m in range(TOEP - 1):
        rel = (WINDOW - 1) - m
        f[m, int(np.clip(rel, -MAX_REL, MAX_REL)) + MAX_REL] = 1.0
    return f


def _split3(v):
    hi = v.astype(BF16)
    r1 = v - hi.astype(F32)
    mid = r1.astype(BF16)
    lo = (r1 - mid.astype(F32)).astype(BF16)
    return hi, mid, lo


def _exact_select(v, sel):
    out = None
    for part in _split3(v):
        p = jnp.dot(part, sel, preferred_element_type=F32)
        out = p if out is None else out + p
    return out


def _select_call(v, sel, name):
    def body(v_ref, s_ref, o_ref):
        o_ref[...] = _exact_select(v_ref[...], s_ref[...])
    return pl.pallas_call(body, out_shape=_sds((v.shape[0], sel.shape[1]), F32), name=name)(v, sel)


def _band_bias(gen_row):
    b0 = jnp.broadcast_to(gen_row, (Q_TILE, TOEP))
    bias = pltpu.roll(b0, TOEP - (Q_TILE - 1), 1, stride=1, stride_axis=0)[:, :WINDOW]
    qq = lax.broadcasted_iota(jnp.int32, (Q_TILE, WINDOW), 0) // CHUNK
    kc = lax.broadcasted_iota(jnp.int32, (Q_TILE, WINDOW), 1) // CHUNK
    return jnp.where((kc >= qq) & (kc <= qq + LEFT_CHUNKS), bias, NEG_INF)


PAD_ROWS = WINDOW - Q_TILE
NT_DIMS = (((1,), (1,)), ((), ()))
TN_DIMS = (((0,), (0,)), ((), ()))


def _head_mask(hh):
    lane = lax.broadcasted_iota(jnp.int32, (1, 128), 1)
    return (lane < 64) if hh == 0 else (lane >= 64)


SOFTMAX_ROWS = 16


def _probs_block(s_scr, bias, hh, rows, q_start):
    s = s_scr[rows, :] + bias[hh, rows, :]
    col = lax.broadcasted_iota(jnp.int32, (SOFTMAX_ROWS, WINDOW), 1)
    s = jnp.where(col >= PAD_ROWS - q_start, s, NEG_INF)
    p = jnp.exp(s - jnp.max(s, axis=-1, keepdims=True))
    return p / jnp.sum(p, axis=-1, keepdims=True)


def _attention(z, gen, name, carried=None):
    t = z.shape[0]
    n_i = t // STEP_ROWS

    def body(q_ref, k_ref, v_ref, g_ref, o_ref, kpad, vpad, bias, s_scr, p_scr):
        hp, i = pl.program_id(0), pl.program_id(1)

        @pl.when(i == 0)
        def _():
            kpad[0:PAD_ROWS, :] = jnp.zeros((PAD_ROWS, 128), BF16)
            vpad[0:PAD_ROWS, :] = jnp.zeros((PAD_ROWS, 128), BF16)
            kpad[PAD_ROWS:PAD_ROWS + t, :] = k_ref[...].astype(BF16)
            vpad[PAD_ROWS:PAD_ROWS + t, :] = v_ref[...].astype(BF16)
            for hh in range(2):
                bias[hh] = _band_bias(g_ref[pl.ds(2 * hp + hh, 1), :])

        for q0 in range(0, STEP_ROWS, Q_TILE):
            q_start = i * STEP_ROWS + q0
            win = pl.ds(pl.multiple_of(q_start, Q_TILE), WINDOW)
            out = None
            for hh in range(2):
                mask = _head_mask(hh)
                qm = jnp.where(mask, q_ref[q0:q0 + Q_TILE, :] * (CHUNK ** -0.5), 0.0).astype(BF16)
                slot = 2 * (q0 // Q_TILE) + hh
                s_scr[slot] = lax.dot_general(qm, kpad[win, :], NT_DIMS, preferred_element_type=F32)
                for r0 in range(0, Q_TILE, SOFTMAX_ROWS):
                    rows = slice(r0, r0 + SOFTMAX_ROWS)
                    p_scr[slot, rows, :] = _probs_block(s_scr.at[slot], bias, hh, rows, q_start).astype(BF16)
                o = jnp.dot(p_scr[slot], vpad[win, :], preferred_element_type=F32)
                out = jnp.where(mask, o, 0.0) if out is None else jnp.where(mask, o, out)
            o_ref[q0:q0 + Q_TILE, :] = out.astype(BF16)

    res = _call(
        body, grid=(4, n_i),
        in_specs=[pl.BlockSpec((STEP_ROWS, 128), lambda h, i: (i, h)),
                  pl.BlockSpec((t, 128), lambda h, i: (0, 4 + h)),
                  pl.BlockSpec((t, 128), lambda h, i: (0, 8 + h)),
                  pl.BlockSpec((N_HEADS, TOEP), lambda h, i: (0, 0))],
        out_specs=[pl.BlockSpec((STEP_ROWS, 128), lambda h, i: (i, h))],
        out_shape=[_sds((t, 512), BF16)],
        scratch_shapes=[pltpu.VMEM((PAD_ROWS + t, 128), BF16), pltpu.VMEM((PAD_ROWS + t, 128), BF16),
                        pltpu.VMEM((2, Q_TILE, WINDOW), F32), pltpu.VMEM((4, Q_TILE, WINDOW), F32),
                        pltpu.VMEM((4, Q_TILE, WINDOW), BF16)],
        sem=("arbitrary", "arbitrary"), name=name, args=(z, z, z, gen), carried=carried)
    return res[0] if carried is None else (res[0], res[1:])


def _attention_bwd(z, datt, gen, name, carried=None):
    t = z.shape[0]
    n_i = t // STEP_ROWS

    def body(q_ref, k_ref, v_ref, d_ref, g_ref, dq_ref, dk_ref, dv_ref, sq_ref, sk_ref, sv_ref, dg_ref,
             kpad, vpad, dkacc, dvacc, bias, dsacc, s_scr, dp_scr, p_scr, ds_scr):
        hp, i = pl.program_id(0), pl.program_id(1)

        @pl.when(i == 0)
        def _():
            kpad[0:PAD_ROWS, :] = jnp.zeros((PAD_ROWS, 128), BF16)
            vpad[0:PAD_ROWS, :] = jnp.zeros((PAD_ROWS, 128), BF16)
            kpad[PAD_ROWS:PAD_ROWS + t, :] = k_ref[...].astype(BF16)
            vpad[PAD_ROWS:PAD_ROWS + t, :] = v_ref[...].astype(BF16)
            dkacc[...] = jnp.zeros_like(dkacc)
            dvacc[...] = jnp.zeros_like(dvacc)
            dsacc[...] = jnp.zeros_like(dsacc)
            for hh in range(2):
                bias[hh] = _band_bias(g_ref[pl.ds(2 * hp + hh, 1), :])

        dq_sum = None
        for q0 in range(0, STEP_ROWS, Q_TILE):
            q_start = i * STEP_ROWS + q0
            win = pl.ds(pl.multiple_of(q_start, Q_TILE), WINDOW)
            dq = None
            for hh in range(2):
                mask = _head_mask(hh)
                qm = jnp.where(mask, q_ref[q0:q0 + Q_TILE, :] * (CHUNK ** -0.5), 0.0).astype(BF16)
                dom = jnp.where(mask, d_ref[q0:q0 + Q_TILE, :], 0.0).astype(BF16)
                slot = 2 * (q0 // Q_TILE) + hh
                s_scr[slot] = lax.dot_general(qm, kpad[win, :], NT_DIMS, preferred_element_type=F32)
                dp_scr[slot] = lax.dot_general(dom, vpad[win, :], NT_DIMS, preferred_element_type=F32)
                for r0 in range(0, Q_TILE, SOFTMAX_ROWS):
                    rows = slice(r0, r0 + SOFTMAX_ROWS)
                    p = _probs_block(s_scr.at[slot], bias, hh, rows, q_start)
                    dp = dp_scr[slot, rows, :]
                    ds = p * (dp - jnp.sum(p * dp, axis=-1, keepdims=True))
                    dsacc[hh, rows, :] += ds
                    ds_scr[slot, rows, :] = ds.astype(BF16)
                    p_scr[slot, rows, :] = p.astype(BF16)
                ds16 = ds_scr[slot]
                dqh = jnp.dot(ds16, kpad[win, :], preferred_element_type=F32) * (CHUNK ** -0.5)
                dq = jnp.where(mask, dqh, 0.0) if dq is None else jnp.where(mask, dqh, dq)
                dkacc[win, :] += lax.dot_general(ds16, qm, TN_DIMS, preferred_element_type=F32)
                dvacc[win, :] += lax.dot_general(p_scr[slot], dom, TN_DIMS, preferred_element_type=F32)
            dq_ref[q0:q0 + Q_TILE, :] = dq.astype(BF16)
            dq_sum = _colsum(dq) if dq_sum is None else dq_sum + _colsum(dq)

        @pl.when(i == 0)
        def _():
            sq_ref[...] = dq_sum

        @pl.when(i > 0)
        def _():
            sq_ref[...] += dq_sum

        @pl.when(i == n_i - 1)
        def _():
            dk = dkacc[PAD_ROWS:PAD_ROWS + t, :]
            dv = dvacc[PAD_ROWS:PAD_ROWS + t, :]
            dk_ref[...] = dk.astype(BF16)
            dv_ref[...] = dv.astype(BF16)
            sk_ref[...] = _colsum(dk)
            sv_ref[...] = _colsum(dv)
            rr = lax.broadcasted_iota(jnp.int32, (Q_TILE, Q_TILE), 0)
            cc = lax.broadcasted_iota(jnp.int32, (Q_TILE, Q_TILE), 1)
            rev = jnp.where(rr + cc == Q_TILE - 1, 1.0, 0.0).astype(BF16)
            for hh in range(2):
                acc = None
                for part in _split3(dsacc[hh]):
                    pr = jnp.dot(rev, part, preferred_element_type=F32)
                    acc = pr if acc is None else acc + pr
                wide = jnp.concatenate([acc, jnp.zeros((Q_TILE, TOEP - WINDOW), F32)], axis=1)
                dg_ref[pl.ds(2 * hp + hh, 1), :] = _colsum(pltpu.roll(wide, 0, 1, stride=1, stride_axis=0))

    col = lambda off: pl.BlockSpec((t, 128), lambda h, i: (0, off + h))
    tile = lambda: pl.BlockSpec((STEP_ROWS, 128), lambda h, i: (i, h))
    sums = lambda: pl.BlockSpec((1, 128), lambda h, i: (0, h))
    res = _call(
        body, grid=(4, n_i),
        in_specs=[tile(), col(4), col(8), tile(), pl.BlockSpec((N_HEADS, TOEP), lambda h, i: (0, 0))],
        out_specs=[tile(), col(0), col(0), sums(), sums(), sums(), pl.BlockSpec((N_HEADS, TOEP), lambda h, i: (0, 0))],
        out_shape=[_sds((t, 512), BF16)] * 3 + [_sds((1, 512), F32)] * 3 + [_sds((N_HEADS, TOEP), F32)],
        scratch_shapes=[pltpu.VMEM((PAD_ROWS + t, 128), BF16), pltpu.VMEM((PAD_ROWS + t, 128), BF16),
                        pltpu.VMEM((PAD_ROWS + t, 128), F32), pltpu.VMEM((PAD_ROWS + t, 128), F32),
                        pltpu.VMEM((2, Q_TILE, WINDOW), F32), pltpu.VMEM((2, Q_TILE, WINDOW), F32),
                        pltpu.VMEM((4, Q_TILE, WINDOW), F32), pltpu.VMEM((4, Q_TILE, WINDOW), F32),
                        pltpu.VMEM((4, Q_TILE, WINDOW), BF16), pltpu.VMEM((4, Q_TILE, WINDOW), BF16)],
        sem=("arbitrary", "arbitrary"), name=name, args=(z, z, z, datt, gen), carried=carried)
    return res[:7] if carried is None else (res[:7], res[7:])


def _adamw_math(w, g, m, v):
    m = ADAM_B1 * m + (1.0 - ADAM_B1) * g
    v = ADAM_B2 * v + (1.0 - ADAM_B2) * (g * g)
    m_hat = m / (1.0 - ADAM_B1 ** ADAM_STEP)
    v_hat = v / (1.0 - ADAM_B2 ** ADAM_STEP)
    delta = -ADAM_LR * (m_hat / (jnp.sqrt(v_hat) + ADAM_EPS) + ADAM_WD * w)
    return delta, m, v


def _adamw_many(items, name):
    n = len(items)

    def body(*refs):
        ins, outs = refs[:4 * n], refs[4 * n:]
        for k in range(n):
            w, g, m, v = (r[...] for r in ins[4 * k:4 * k + 4])
            outs[3 * k][...], outs[3 * k + 1][...], outs[3 * k + 2][...] = _adamw_math(w, g, m, v)

    flat = [a for item in items for a in item]
    res = pl.pallas_call(body, out_shape=[_sds(item[0].shape, F32) for item in items for _ in range(3)],
                         name=name)(*flat)
    return [tuple(res[3 * k:3 * k + 3]) for k in range(n)]


def _adamw(w, g, m, v, name):
    r, c = w.shape
    tm = next(cand for cand in (256, 176, 128, 64, 32, 16, 8) if r % cand == 0)
    return _rowcall(lambda wv, gv, mv, vv: (gv,) + _adamw_math(wv, gv, mv, vv),
                    [(w, c, 0), (g, c, 0), (m, c, 0), (v, c, 0)], [], [(r, c, F32, c)] * 4, [], name=name, tm=tm)


def _ada_fwd(c_all, w_shard, b_shard, name):
    n = w_shard.shape[1]
    tn = 512

    def body(c_ref, w_ref, b_ref, o_ref, a_ref):
        cv = c_ref[...]
        act = cv * _sigmoid(cv)
        a_ref[...] = act
        o_ref[...] = jnp.dot(act.astype(BF16), w_ref[...].astype(BF16), preferred_element_type=F32) + b_ref[...]

    return pl.pallas_call(
        body, grid=(n // tn,),
        in_specs=[pl.BlockSpec((8, D), lambda j: (0, 0)), pl.BlockSpec((D, tn), lambda j: (0, j)),
                  pl.BlockSpec((1, tn), lambda j: (0, j))],
        out_specs=[pl.BlockSpec((8, tn), lambda j: (0, j)), pl.BlockSpec((8, D), lambda j: (0, 0))],
        out_shape=[_sds((8, n), F32), _sds((8, D), F32)],
        compiler_params=_params(("arbitrary",)), name=name,
    )(c_all, w_shard, b_shard)


def _ada_bwd_adamw(act_t, dmod_shard, w, m, v, name):
    r, c = w.shape
    tm = 256

    def body(a_ref, d_ref, w_ref, m_ref, v_ref, g_ref, dl_ref, nm_ref, nv_ref):
        g = jnp.dot(a_ref[...], d_ref[...], precision=lax.Precision.HIGHEST, preferred_element_type=F32)
        g_ref[...] = g
        dl_ref[...], nm_ref[...], nv_ref[...] = _adamw_math(w_ref[...], g, m_ref[...], v_ref[...])

    blk = pl.BlockSpec((tm, c), lambda i: (i, 0))
    return pl.pallas_call(
        body, grid=(r // tm,),
        in_specs=[pl.BlockSpec((tm, 8), lambda i: (i, 0)), pl.BlockSpec((8, c), lambda i: (0, 0)), blk, blk, blk],
        out_specs=[blk] * 4, out_shape=[_sds((r, c), F32)] * 4,
        compiler_params=_params(("arbitrary",)), name=name,
    )(act_t, dmod_shard, w, m, v)


def _place():
    return lax.axis_index("x"), lax.axis_index("y"), lax.axis_index("c")


def _flip(v, bit):
    return 1 - v if bit else v


VMEM_SPEC = pl.BlockSpec(memory_space=pltpu.VMEM)


def _allgather8(v, name):
    r, c = v.shape

    def body(v_ref, g_ref, tot_ref, send_sems, recv_sems, local_sem):
        x, y, cc = _place()
        sibling = (x, y, 1 - cc)
        chips = [(_flip(x, k & 2), _flip(y, k & 1)) for k in (1, 2, 3)]

        def block(px, py, pc):
            return g_ref.at[4 * px + 2 * py + pc]

        def copy(k, place, to, src=None):
            slot = block(*place)
            return pltpu.make_async_remote_copy(src_ref=slot if src is None else src, dst_ref=slot,
                                                send_sem=send_sems.at[k], recv_sem=recv_sems.at[k],
                                                device_id=to, device_id_type=MESH)

        mine = pltpu.make_async_copy(v_ref, block(x, y, cc), local_sem)
        mine.start()
        first = [copy(0, (x, y, cc), sibling, src=v_ref)]
        first += [copy(1 + j, (x, y, cc), (px, py, cc), src=v_ref) for j, (px, py) in enumerate(chips)]
        for cp in first:
            cp.start()
        passed = [copy(4 + j, (px, py, cc), sibling) for j, (px, py) in enumerate(chips)]
        for j, (px, py) in enumerate(chips):
            copy(1 + j, (px, py, cc), (x, y, cc)).wait_recv()
            passed[j].start()
        copy(0, sibling, (x, y, cc)).wait_recv()
        for j, (px, py) in enumerate(chips):
            copy(4 + j, (px, py, 1 - cc), (x, y, cc)).wait_recv()
        for cp in first + passed:
            cp.wait_send()
        mine.wait()
        tot = g_ref[0]
        for d in range(1, 8):
            tot = tot + g_ref[d]
        tot_ref[...] = tot

    return pl.pallas_call(
        body, in_specs=[VMEM_SPEC], out_specs=[VMEM_SPEC, VMEM_SPEC],
        out_shape=[_sds((8, r, c), F32), _sds((r, c), F32)],
        scratch_shapes=[pltpu.SemaphoreType.DMA((7,)), pltpu.SemaphoreType.DMA((7,)), pltpu.SemaphoreType.DMA],
        compiler_params=pltpu.CompilerParams(vmem_limit_bytes=VMEM_LIMIT), name=name,
    )(v)


def _slot(px, py, swapped):
    return 2 * py + px if swapped else 2 * px + py


def _gather_shards(arrs, swapped, name, in_place=False):
    n = len(arrs)

    def body(*refs):
        ins, outs = refs[:n], refs[n:2 * n]
        send1, recv1, send2, recv2, local_sems = refs[2 * n:]
        x, y, c = _place()
        sibling = (x, y, 1 - c)
        chips = [(_flip(x, k & 2), _flip(y, k & 1)) for k in (1, 2, 3)]
        local_copies, sends = [], []
        for a in range(n):
            h = outs[a].shape[1] // 2
            mine = pl.ds(pl.multiple_of(c * h, 8), h)
            own = _slot(x, y, swapped[a])
            if in_place:
                src = outs[a].at[own, mine]
            else:
                src = ins[a].at[mine]
                lc = pltpu.make_async_copy(ins[a], outs[a].at[own], local_sems.at[a])
                lc.start()
                local_copies.append(lc)
            for j, (px, py) in enumerate(chips):
                cp = pltpu.make_async_remote_copy(
                    src_ref=src, dst_ref=outs[a].at[own, mine], send_sem=send1.at[3 * a + j],
                    recv_sem=recv1.at[3 * a + j], device_id=(px, py, c), device_id_type=MESH)
                cp.start()
                sends.append(cp)
        for a in range(n):
            h = outs[a].shape[1] // 2
            mine = pl.ds(pl.multiple_of(c * h, 8), h)
            for j, (px, py) in enumerate(chips):
                piece = outs[a].at[_slot(px, py, swapped[a]), mine]
                pltpu.make_async_remote_copy(
                    src_ref=piece, dst_ref=piece, send_sem=send1.at[3 * a + j], recv_sem=recv1.at[3 * a + j],
                    device_id=(px, py, c), device_id_type=MESH).wait_recv()
                fwd = pltpu.make_async_remote_copy(
                    src_ref=piece, dst_ref=piece, send_sem=send2.at[3 * a + j], recv_sem=recv2.at[3 * a + j],
                    device_id=sibling, device_id_type=MESH)
                fwd.start()
                sends.append(fwd)
        for a in range(n):
            h = outs[a].shape[1] // 2
            other = pl.ds(pl.multiple_of((1 - c) * h, 8), h)
            for j, (px, py) in enumerate(chips):
                piece = outs[a].at[_slot(px, py, swapped[a]), other]
                pltpu.make_async_remote_copy(
                    src_ref=piece, dst_ref=piece, send_sem=send2.at[3 * a + j], recv_sem=recv2.at[3 * a + j],
                    device_id=sibling, device_id_type=MESH).wait_recv()
        for cp in sends:
            cp.wait_send()
        for lc in local_copies:
            lc.wait()

    dma = lambda k: pltpu.SemaphoreType.DMA((k,))
    return pl.pallas_call(
        body, in_specs=[ANY] * n, out_specs=[ANY] * n,
        out_shape=[_sds(a.shape if in_place else (4,) + a.shape, a.dtype) for a in arrs],
        scratch_shapes=[dma(3 * n), dma(3 * n), dma(3 * n), dma(3 * n), dma(n)],
        input_output_aliases={a: a for a in range(n)} if in_place else {},
        name=name,
    )(*arrs)


def _carry_pair_exchange(grads):
    n = len(grads)

    def copies(ins, outs, send_sems, recv_sems):
        x, y, c = _place()
        cps = []
        for a in range(n):
            h = ins[a].shape[1] // 2
            theirs = pl.ds(pl.multiple_of((1 - c) * h, 8), h)
            cps.append(pltpu.make_async_remote_copy(
                src_ref=ins[a].at[:, theirs, :], dst_ref=outs[a], send_sem=send_sems.at[a], recv_sem=recv_sems.at[a],
                device_id=(x, y, 1 - c), device_id_type=MESH))
        return cps

    def start(*refs):
        for cp in copies(*refs):
            cp.start()

    def finish(*refs):
        for cp in copies(*refs):
            cp.wait()

    return _Carried(grads, [_sds((4, g.shape[1] // 2, g.shape[2]), F32) for g in grads], {}, n, start, finish)


def _row_steps(h):
    return 1


def _pair_sum(grad, recv, core, name):
    _, r, c = grad.shape
    h = r // 2
    nr = _row_steps(h)
    th = h // nr

    def body(core_ref, g_ref, r_ref, o_ref):
        o_ref[...] = (g_ref[...] + r_ref[...]).astype(BF16)

    return pl.pallas_call(
        body,
        grid_spec=pltpu.PrefetchScalarGridSpec(
            num_scalar_prefetch=1, grid=(4, nr),
            in_specs=[pl.BlockSpec((None, th, c), lambda s, q, core_ref: (s, core_ref[0] * nr + q, 0)),
                      pl.BlockSpec((None, th, c), lambda s, q, core_ref: (s, q, 0))],
            out_specs=pl.BlockSpec((None, th, c), lambda s, q, core_ref: (s, q, 0))),
        out_shape=_sds((4, h, c), BF16), compiler_params=_params(("arbitrary", "arbitrary")), name=name,
    )(core, grad, recv)


def _carry_chip_exchange(parts, swapped):
    n = len(parts)

    def copies(ins, outs, send_sems, recv_sems):
        x, y, c = _place()
        chips = [(_flip(x, k & 2), _flip(y, k & 1)) for k in (1, 2, 3)]
        cps = []
        for a in range(n):
            for j, (px, py) in enumerate(chips):
                cps.append(pltpu.make_async_remote_copy(
                    src_ref=ins[a].at[_slot(px, py, swapped[a])], dst_ref=outs[a].at[j],
                    send_sem=send_sems.at[3 * a + j], recv_sem=recv_sems.at[3 * a + j],
                    device_id=(px, py, c), device_id_type=MESH))
        return cps

    def start(*refs):
        for cp in copies(*refs):
            cp.start()

    def finish(*refs):
        for cp in copies(*refs):
            cp.wait()

    return _Carried(parts, [_sds((3,) + p.shape[1:], BF16) for p in parts], {}, 3 * n, start, finish)


def _chip_sum(part, recv, slot_core, name):
    _, h, c = part.shape
    nr = _row_steps(h)
    th = h // nr

    def body(sc_ref, p_ref, r_ref, o_ref):
        acc = p_ref[...].astype(F32)
        for j in range(3):
            acc = acc + r_ref[j].astype(F32)
        o_ref[...] = acc

    return pl.pallas_call(
        body,
        grid_spec=pltpu.PrefetchScalarGridSpec(
            num_scalar_prefetch=1, grid=(nr,),
            in_specs=[pl.BlockSpec((None, th, c), lambda q, sc_ref: (sc_ref[0], q, 0)),
                      pl.BlockSpec((3, th, c), lambda q, sc_ref: (0, q, 0))],
            out_specs=pl.BlockSpec((th, c), lambda q, sc_ref: (sc_ref[1] * nr + q, 0))),
        out_shape=_sds((2 * h, c), F32), compiler_params=_params(("arbitrary",)), name=name,
    )(slot_core, part, recv)


def _carry_pair_share(shards):
    n = len(shards)

    def copies(outs, send_sems, recv_sems, mine):
        x, y, c = _place()
        cps = []
        for a in range(n):
            h = outs[a].shape[0] // 2
            half = outs[a].at[pl.ds(pl.multiple_of((c if mine else 1 - c) * h, 8), h)]
            cps.append(pltpu.make_async_remote_copy(
                src_ref=half, dst_ref=half, send_sem=send_sems.at[a], recv_sem=recv_sems.at[a],
                device_id=(x, y, 1 - c), device_id_type=MESH))
        return cps

    def start(ins, outs, send_sems, recv_sems):
        for cp in copies(outs, send_sems, recv_sems, True):
            cp.start()

    def finish(ins, outs, send_sems, recv_sems):
        for cp in copies(outs, send_sems, recv_sems, False):
            cp.wait_recv()
        for cp in copies(outs, send_sems, recv_sems, True):
            cp.wait_send()

    return _Carried(shards, [_sds(s.shape, F32) for s in shards], {a: a for a in range(n)}, n, start, finish)


def _carry_gather_ici(bufs, swapped):
    n = len(bufs)

    def copies(outs, send_sems, recv_sems, sending):
        x, y, c = _place()
        cps = []
        for a in range(n):
            h = outs[a].shape[1] // 2
            mine = pl.ds(pl.multiple_of(c * h, 8), h)
            for j, k in enumerate((1, 2, 3)):
                px, py = _flip(x, k & 2), _flip(y, k & 1)
                slot = _slot(x, y, swapped[a]) if sending else _slot(px, py, swapped[a])
                piece = outs[a].at[slot, mine]
                cps.append(pltpu.make_async_remote_copy(
                    src_ref=piece, dst_ref=piece, send_sem=send_sems.at[3 * a + j], recv_sem=recv_sems.at[3 * a + j],
                    device_id=(px, py, c), device_id_type=MESH))
        return cps

    def start(ins, outs, send_sems, recv_sems):
        for cp in copies(outs, send_sems, recv_sems, True):
            cp.start()

    def finish(ins, outs, send_sems, recv_sems):
        for cp in copies(outs, send_sems, recv_sems, False):
            cp.wait_recv()
        for cp in copies(outs, send_sems, recv_sems, True):
            cp.wait_send()

    return _Carried(bufs, [_sds(b.shape, b.dtype) for b in bufs], {a: a for a in range(n)}, 3 * n, start, finish)


HBM_SPEC = pl.BlockSpec(memory_space=pltpu.HBM)
SEM_SPEC = pl.BlockSpec(memory_space=pltpu.SEMAPHORE)
SIDE_EFFECT = pltpu.SideEffectType.DATAFLOW_SIDE_EFFECTING


def _ici_pieces(buf, send_sems, recv_sems, swapped, sending):
    x, y, c = _place()
    h = buf.shape[1] // 2
    mine = pl.ds(pl.multiple_of(c * h, 8), h)
    cps = []
    for j, k in enumerate((1, 2, 3)):
        px, py = _flip(x, k & 2), _flip(y, k & 1)
        piece = buf.at[_slot(x, y, swapped) if sending else _slot(px, py, swapped), mine]
        cps.append(pltpu.make_async_remote_copy(src_ref=piece, dst_ref=piece, send_sem=send_sems.at[j],
                                                recv_sem=recv_sems.at[j], device_id=(px, py, c), device_id_type=MESH))
    return cps


def _gather_ici_start(buf, swapped, name):
    def body(buf_ref, send_sems, recv_sems, thru, token):
        for cp in _ici_pieces(thru, send_sems, recv_sems, swapped, True):
            cp.start()
        token[...] = jnp.zeros_like(token)

    return pl.pallas_call(
        body, name=name,
        out_shape=(pltpu.SemaphoreType.DMA((3,)), pltpu.SemaphoreType.DMA((3,)), pltpu.HBM(buf.shape, buf.dtype),
                   jax.ShapeDtypeStruct((8, 128), F32)),
        in_specs=(HBM_SPEC,), out_specs=(SEM_SPEC, SEM_SPEC, HBM_SPEC, VMEM_SPEC), input_output_aliases={0: 2},
        compiler_params=pltpu.CompilerParams(has_side_effects=SIDE_EFFECT),
    )(pltpu.with_memory_space_constraint(buf, pltpu.HBM))


def _gather_ici_wait(send_sems, recv_sems, thru, after, swapped, name):
    def body(thru_ref, send_sems, recv_sems, after_ref, out_ref):
        for cp in _ici_pieces(out_ref, send_sems, recv_sems, swapped, True):
            cp.wait_send()
        for cp in _ici_pieces(out_ref, send_sems, recv_sems, swapped, False):
            cp.wait_recv()

    return pl.pallas_call(
        body, name=name, out_shape=pltpu.HBM(thru.shape, thru.dtype),
        in_specs=(HBM_SPEC, SEM_SPEC, SEM_SPEC, ANY), out_specs=HBM_SPEC, input_output_aliases={0: 0},
        compiler_params=pltpu.CompilerParams(has_side_effects=SIDE_EFFECT),
    )(thru, send_sems, recv_sems, after)


def _carry_gather_forward(bufs, swapped):
    n = len(bufs)

    def copies(outs, send_sems, recv_sems, sending):
        x, y, c = _place()
        cps = []
        for a in range(n):
            h = outs[a].shape[1] // 2
            rows = pl.ds(pl.multiple_of((c if sending else 1 - c) * h, 8), h)
            for j, k in enumerate((1, 2, 3)):
                piece = outs[a].at[_slot(_flip(x, k & 2), _flip(y, k & 1), swapped[a]), rows]
                cps.append(pltpu.make_async_remote_copy(
                    src_ref=piece, dst_ref=piece, send_sem=send_sems.at[3 * a + j], recv_sem=recv_sems.at[3 * a + j],
                    device_id=(x, y, 1 - c), device_id_type=MESH))
        return cps

    def start(ins, outs, send_sems, recv_sems):
        for cp in copies(outs, send_sems, recv_sems, True):
            cp.start()

    def finish(ins, outs, send_sems, recv_sems):
        for cp in copies(outs, send_sems, recv_sems, False):
            cp.wait_recv()
        for cp in copies(outs, send_sems, recv_sems, True):
            cp.wait_send()

    return _Carried(bufs, [_sds(b.shape, b.dtype) for b in bufs], {a: a for a in range(n)}, 3 * n, start, finish)


def _pack(arrs, rows_multiple=8):
    parts, offs, row = [], [], 0
    for a in arrs:
        flat = a.reshape(-1)
        nrow = -(-flat.shape[0] // D)
        parts.append(jnp.pad(flat, (0, nrow * D - flat.shape[0])))
        offs.append(row)
        row += nrow
    total = -(-row // rows_multiple) * rows_multiple
    if total > row:
        parts.append(jnp.zeros(((total - row) * D,), F32))
    return jnp.concatenate(parts).reshape(total, D), offs


def _unpack(packed, offs, shapes):
    out = []
    for off, shp in zip(offs, shapes):
        size = int(np.prod(shp))
        nrow = -(-size // D)
        out.append(packed[off:off + nrow].reshape(-1)[:size].reshape(shp))
    return out


def _to_bf16_slot(w, slot, name):
    r, c = w.shape
    tm = next(cand for cand in (256, 176, 128, 64, 32, 16) if r % cand == 0)

    def body(slot_ref, w_ref, o_ref):
        o_ref[...] = w_ref[...].astype(BF16)

    return pl.pallas_call(
        body,
        grid_spec=pltpu.PrefetchScalarGridSpec(
            num_scalar_prefetch=1, grid=(r // tm,),
            in_specs=[pl.BlockSpec((tm, c), lambda i, slot_ref: (i, 0))],
            out_specs=pl.BlockSpec((None, tm, c), lambda i, slot_ref: (slot_ref[0], i, 0))),
        out_shape=_sds((4, r, c), BF16), compiler_params=_params(("arbitrary",)), name=name,
    )(slot, w)


def _unshard_cols(g):
    s, k, n = g.shape
    return jnp.transpose(g, (1, 0, 2)).reshape(k, s * n)


def _ff_swap(v):
    b = FF_BLOCK
    return jnp.concatenate([v[..., 0:b], v[..., 2 * b:3 * b], v[..., b:2 * b], v[..., 3 * b:4 * b]], axis=-1)


LATE = ("attn_o", "conv_o", "mix_o", "up", "down")
EARLY_GRADS = ("down", "up", "mix_o", "attn_o", "conv_o")


def _weight_views(bufs):
    return {"up": bufs["up"], "attn_o": _unshard_cols(bufs["attn_o"]), "conv_o": _unshard_cols(bufs["conv_o"]),
            "mix_o": bufs["mix_o"].reshape(D, D), "down": bufs["down"].reshape(D_FF, D)}


def _pair_sums(names, grads, recv, dist):
    return [_pair_sum(g, r, dist["core"], "pair_sum_" + n) for n, g, r in zip(names, grads, recv)]


def _reduce_halves(names, parts, from_chips, dist):
    return [_chip_sum(p, r, jnp.concatenate([dist["slots"][SWAPPED[n]], dist["core"]]), "chip_sum_" + n)
            for n, p, r in zip(names, parts, from_chips)]


FUSED_TILE = 256
WIDE_TILE = 512


def _gates(z):
    return [(z, 512, 5), (z, 512, 6), (z, 512, 7), (z, 512, 8)]


def _mix_out(a, cb, z, x, w_mix_o, g_post, gt, g_pre2, sc2, sh2, name):
    def lhs(av, cv, ga0, ga1, gb0, gb1):
        ga, gb = jnp.concatenate([ga0, ga1], axis=1), jnp.concatenate([gb0, gb1], axis=1)
        return _sigmoid(ga) * av + _sigmoid(gb) * cv

    def fn(ym, y, xv, gv, gtv, g2v, scv, shv):
        yn, _ = _rms_stats(ym)
        x1 = xv + gtv * (yn * gv)
        xn, _ = _rms_stats(x1)
        return ym, y, x1, (xn * g2v) * (1.0 + scv) + shv

    return _matmul_rows(w_mix_o, form="nn", tm=min(FUSED_TILE, x.shape[0]), tk=D, fn=fn, a_rows=[(a, D, 0), (cb, D, 0)] + _gates(z),
                        a_fn=lhs, rows=[(x, D, 0)], consts=[g_post, gt, g_pre2, sc2, sh2],
                        row_outs=[(F32, D), (BF16, D), (F32, D), (BF16, D)], acc_outs=[], name=name)


def _down_tail(act, w_down, x1, target, g, gt, name):
    def fn(yv, xv, tv, gv, gtv):
        yn, r = _rms_stats(yv)
        e = xv + gtv * (yn * gv) - tv
        dx2 = e * (1.0 / D)
        dyn = dx2 * gtv
        return (dx2, _rms_bwd(dyn * gv, yn, r), _colsum(e * e) * (0.5 / D), _colsum(dyn * yn),
                _colsum(dx2 * (yn * gv)))

    return _matmul_rows(w_down, form="nn", a=act, tm=min(WIDE_TILE, x1.shape[0]), tk=D_FF, fn=fn,
                        rows=[(x1, D, 0), (target, D, 0)], consts=[g, gt], row_outs=[(F32, D), (BF16, D)],
                        acc_outs=[(1, D)] * 3, name=name)


def _up_dx_tail(dup, w_up, x1, dx2, ym, g_pre2, sc2, g_post, gt, name):
    def fn(dh, xv, dov, ymv, g2v, scv, gv, gtv):
        xn, r = _rms_stats(xv)
        dyn = dh * (1.0 + scv)
        dx1 = dov + _rms_bwd(dyn * g2v, xn, r)
        yn, r2 = _rms_stats(ymv)
        dynm = dx1 * gtv
        return (dx1, _rms_bwd(dynm * gv, yn, r2), _colsum(dh), _colsum(dh * (xn * g2v)), _colsum(dyn * xn),
                _colsum(dynm * yn), _colsum(dx1 * (yn * gv)))

    return _matmul_rows(w_up, form="nt", a=dup, tm=min(FUSED_TILE, x1.shape[0]), tk=2 * D_FF, fn=fn,
                        rows=[(x1, D, 0), (dx2, D, 0), (ym, D, 0)], consts=[g_pre2, sc2, g_post, gt],
                        row_outs=[(F32, D), (BF16, D)], acc_outs=[(1, D)] * 5, name=name)


def _mix_dx_gates(dym, w_mix_o, a, cb, z, name):
    def fn(dy, av, cv, ga0, ga1, gb0, gb1):
        sa = _sigmoid(jnp.concatenate([ga0, ga1], axis=1))
        sb = _sigmoid(jnp.concatenate([gb0, gb1], axis=1))
        dcb = dy * sb
        dga = dy * av * (sa * (1.0 - sa))
        dgb = dy * cv * (sb * (1.0 - sb))
        return dy * sa, dcb, dga, dgb, _colsum(dcb), _colsum(dga), _colsum(dgb)

    return _matmul_rows(w_mix_o, form="nt", a=dym, tm=min(FUSED_TILE, a.shape[0]), tk=D, fn=fn,
                        rows=[(a, D, 0), (cb, D, 0)] + _gates(z), consts=[], row_outs=[(BF16, D)] * 4,
                        acc_outs=[(1, D)] * 3, name=name)


def _local_step(x, target, mod, w_in, late, small, dist=None):
    sh_m, sc_m, gt_m, sh_f, sc_f, gt_f = mod
    t = x.shape[0]
    tmm = min(1024, t)
    late_swapped = [SWAPPED[n] for n in LATE]

    h1 = _pre_norm(x, small["g_pre_mix"], sc_m, sh_m, "pre_norm_mix")
    if callable(w_in):
        w_in = w_in(h1)
    z = _matmul(h1, w_in, form="nn", out_dtype=F32, tm=min(FUSED_TILE, t), tn=D_IN, tk=D, bias=small["b_in"], name="mm_in")
    if dist is None:
        att = _attention(z, small["gen"], "attention")
        bufs = dict(late)
    else:
        mid = [n for n in LATE if n != "down"]
        mid_swapped = [SWAPPED[n] for n in mid]
        att, landed = _attention(z, small["gen"], "attention",
                                 carried=_carry_gather_ici([late[n] for n in mid], mid_swapped))
        bufs = dict(zip(mid, _run_carried(_carry_gather_forward(landed, mid_swapped), "gather_forward")))
        bufs["down"] = late["down"]
    w = _weight_views(bufs)
    w["in"] = w_in
    a = _matmul(att, w["attn_o"], form="nn", out_dtype=F32, tm=tmm, tn=512, tk=512, name="mm_attn_o")
    u1, u3 = _conv_branch(z, small["w_dw_conv"], small["b_dw_conv"], small["g_conv_ln"], small["b_conv_ln"], "conv_branch")
    cb = _matmul(u3, w["conv_o"], form="nn", out_dtype=F32, tm=tmm, tn=512, tk=512, bias=small["b_conv_o"], name="mm_conv_o")
    ym, y, x1, h2 = _mix_out(a, cb, z, x, w["mix_o"], small["g_post_mix"], gt_m, small["g_pre_ffn"], sc_f, sh_f, "mix_out")
    mm_up = dict(form="nn", out_dtype=F32, tm=min(FUSED_TILE, t), tn=2 * D_FF, tk=D, name="mm_up")
    if dist is None:
        up = _matmul(h2, w["up"], **mm_up)
    else:
        up, landed = _matmul(h2, w["up"], carried=_carry_gather_ici([late["down"]], [False]), **mm_up)
        w["down"] = _run_carried(_carry_gather_forward(landed, [False]), "gather_forward_down")[0].reshape(D_FF, D)
    act = _ffn_act(up, small["w_dw_ffn"], small["b_dw_ffn"], "ffn_act")

    dx2, dyf, loss_cols, d_g_post_ffn, d_gt_f = _down_tail(act, w["down"], x1, target, small["g_post_ffn"], gt_f, "down_tail")
    dact = _matmul(dyf, w["down"], form="nt", out_dtype=F32, tm=tmm, tn=FF_BLOCK, tk=D, name="mm_down_dx")
    g_down = _matmul(act, dyf, form="tn", out_dtype=F32, tm=FF_BLOCK, tn=512, tk=t, name="mm_down_dw")
    dup, d_w_dw_ffn, d_b_dw_ffn = _ffn_act_bwd(dact, up, small["w_dw_ffn"], small["b_dw_ffn"], "ffn_act_bwd")
    dx1, dym, d_sh_f, d_sc_f, d_g_pre_ffn, d_g_post_mix, d_gt_m = _up_dx_tail(
        dup, w["up"], x1, dx2, ym, small["g_pre_ffn"], sc_f, small["g_post_mix"], gt_m, "up_dx_tail")
    g_up = _matmul(h2, dup, form="tn", out_dtype=F32, tm=512, tn=FF_BLOCK, tk=t, out_sharded=True, name="mm_up_dw")
    da, dcb, dgate_a, dgate_b, d_b_conv_o, sga, sgb = _mix_dx_gates(dym, w["mix_o"], a, cb, z, "mix_dx_gates")
    g_mix_o = _matmul(y, dym, form="tn", out_dtype=F32, tm=D, tn=512, tk=t, name="mm_mix_o_dw")
    datt = _matmul(da, w["attn_o"], form="nt", out_dtype=F32, tm=tmm, tn=512, tk=D, name="mm_attn_o_dx")
    g_attn_o = _matmul(att, da, form="tn", out_dtype=F32, tm=512, tn=256, tk=t, out_sharded=True, name="mm_attn_o_dw")
    du3 = _matmul(dcb, w["conv_o"], form="nt", out_dtype=F32, tm=tmm, tn=512, tk=D, name="mm_conv_o_dx")
    g_conv_o = _matmul(u3, dcb, form="tn", out_dtype=F32, tm=512, tn=256, tk=t, out_sharded=True, name="mm_conv_o_dw")
    big = {"attn_o": g_attn_o, "conv_o": g_conv_o, "mix_o": g_mix_o.reshape(4, 256, D),
           "up": g_up, "down": g_down.reshape(4, D_FF // 4, D)}
    conv_bwd = (du3, u1, z, small["w_dw_conv"], small["g_conv_ln"], small["b_conv_ln"], "conv_branch_bwd")
    in_dw = dict(form="tn", out_dtype=F32, tm=512, tn=1152, tk=t, out_sharded=True, name="mm_in_dw")
    in_dx = dict(form="nt", out_dtype=F32, tm=min(WIDE_TILE, t), tn=D, tk=D_IN, name="mm_in_dx")
    if dist is None:
        dglu, d_w_dw_conv, d_b_dw_conv, d_g_conv_ln, d_b_conv_ln, sglu = _conv_branch_bwd(*conv_bwd)
        dq, dk, dv, sq, sk, sv, dgen = _attention_bwd(z, datt, small["gen"], "attention_bwd")
        dz = jnp.concatenate([dq, dk, dv, dglu, dgate_a, dgate_b], axis=1)
        big["in"] = _matmul(h1, dz, **in_dw)
        dh1 = _matmul(dz, w_in, **in_dx)
    else:
        early = [big[n] for n in EARLY_GRADS]
        (dglu, d_w_dw_conv, d_b_dw_conv, d_g_conv_ln, d_b_conv_ln, sglu), recv = _conv_branch_bwd(
            *conv_bwd, carried=_carry_pair_exchange(early))
        parts = _pair_sums(EARLY_GRADS, early, recv, dist)
        (dq, dk, dv, sq, sk, sv, dgen), from_chips = _attention_bwd(
            z, datt, small["gen"], "attention_bwd",
            carried=_carry_chip_exchange(parts, [SWAPPED[n] for n in EARLY_GRADS]))
        halves = _reduce_halves(EARLY_GRADS, parts, from_chips, dist)
        dz = jnp.concatenate([dq, dk, dv, dglu, dgate_a, dgate_b], axis=1)
        g_in, shards = _matmul(h1, dz, carried=_carry_pair_share(halves), **in_dw)
        big = dict(zip(EARLY_GRADS, shards))
        recv_in = _run_carried(_carry_pair_exchange([g_in]), "pair_exchange_in")
        part_in = _pair_sums(("in",), [g_in], recv_in, dist)
        dh1, from_chips_in = _matmul(dz, w_in, carried=_carry_chip_exchange(part_in, [False]), **in_dx)
        half_in = _reduce_halves(("in",), part_in, from_chips_in, dist)
        big["in"] = _run_carried(_carry_pair_share(half_in), "pair_share_in")[0]
    d_b_in = jnp.concatenate([sq, sk, sv, sglu, sga, sgb], axis=1)
    grad_x, d_sh_m, d_sc_m, d_g_pre_mix = _pre_norm_bwd(dh1, x, dx1, small["g_pre_mix"], sc_m, "pre_norm_mix_bwd")

    dmod = [d_sh_m, d_sc_m, d_gt_m, d_sh_f, d_sc_f, d_gt_f]
    sm = {"g_pre_mix": d_g_pre_mix, "g_post_mix": d_g_post_mix, "b_in": d_b_in, "gen": dgen,
          "w_dw_conv": d_w_dw_conv, "b_dw_conv": d_b_dw_conv, "g_conv_ln": d_g_conv_ln, "b_conv_ln": d_b_conv_ln,
          "b_conv_o": d_b_conv_o, "g_pre_ffn": d_g_pre_ffn, "g_post_ffn": d_g_post_ffn,
          "w_dw_ffn": d_w_dw_ffn, "b_dw_ffn": d_b_dw_ffn}
    return loss_cols, grad_x, dmod, big, sm


BIG = ("in", "attn_o", "conv_o", "mix_o", "up", "down")
SWAPPED = {"in": False, "attn_o": False, "conv_o": False, "mix_o": False, "up": True, "down": False}
SMALL_ORDER = ("b_ada", "g_pre_mix", "g_post_mix", "b_in", "rel_bias", "b_dw_conv", "g_conv_ln", "b_conv_ln",
               "b_conv_o", "g_pre_ffn", "g_post_ffn", "b_dw_ffn", "w_dw_conv", "w_dw_ffn")


def kernel(x, c, w_ada, b_ada, g_pre_mix, g_post_mix, w_in, b_in, rel_bias, w_attn_o, w_dw_conv, b_dw_conv, g_conv_ln, b_conv_ln, w_conv_o, b_conv_o, w_mix_o, g_pre_ffn, g_post_ffn, w_up, w_dw_ffn, b_dw_ffn, w_down, loss_target, m_w_ada, m_b_ada, m_g_pre_mix, m_g_post_mix, m_w_in, m_b_in, m_rel_bias, m_w_attn_o, m_w_dw_conv, m_b_dw_conv, m_g_conv_ln, m_b_conv_ln, m_w_conv_o, m_b_conv_o, m_w_mix_o, m_g_pre_ffn, m_g_post_ffn, m_w_up, m_w_dw_ffn, m_b_dw_ffn, m_w_down, v_w_ada, v_b_ada, v_g_pre_mix, v_g_post_mix, v_w_in, v_b_in, v_rel_bias, v_w_attn_o, v_w_dw_conv, v_b_dw_conv, v_g_conv_ln, v_b_conv_ln, v_w_conv_o, v_b_conv_o, v_w_mix_o, v_g_pre_ffn, v_g_post_ffn, v_w_up, v_w_dw_ffn, v_b_dw_ffn, v_w_down):
    given = dict(locals())
    ax, ay, ac = lax.axis_index("x"), lax.axis_index("y"), lax.axis_index("c")
    shard = 2 * ax + ay
    me = 4 * ax + 2 * ay + ac
    xs, target = x[0], loss_target[0]

    slots = {sw: _slot(ax, ay, sw).astype(jnp.int32).reshape(1) for sw in (False, True)}
    own = {n: _to_bf16_slot(given["w_" + n][0], slots[SWAPPED[n]], "cast_" + n) for n in BIG}
    in_send, in_recv, in_flight, token = _gather_ici_start(own["in"], False, "gather_w_in_start")

    def w_in_ready(after):
        landed = _gather_ici_wait(in_send, in_recv, in_flight, after, False, "gather_w_in_wait")
        return _run_carried(_carry_gather_forward([landed], [False]), "gather_forward_in")[0]

    c_pad = jnp.pad(c + token[0:1, 0:1], ((0, 7), (0, 0)))
    c_g, _ = _allgather8(c_pad, "gather_c")
    c_all = c_g[:, 0, :]
    b_ada_shard = lax.dynamic_slice(b_ada, (0, shard * 1536), (1, 1536))
    mod_shard, c_act = _ada_fwd(c_all, w_ada[0], b_ada_shard, "ada_fwd")
    small_in = [jnp.pad(mod_shard, ((0, 8), (0, 0))),
                jnp.pad(w_dw_conv[0], ((0, 1), (0, 0))),
                jnp.pad(w_dw_ffn[0], ((0, 13), (0, 0)))]
    mod_g, wdc_g, wdf_g = _gather_shards(small_in, [False, False, True], "gather_small")
    mod_all = jnp.transpose(mod_g[:, :8, :], (1, 0, 2)).reshape(8, 6 * D)
    mod_row = lax.dynamic_slice(mod_all, (me, 0), (1, 6 * D))
    mod = [mod_row[:, k * D:(k + 1) * D] for k in range(6)]

    core = ac.astype(jnp.int32).reshape(1)
    dist = {"core": core, "slots": slots}

    sel = jnp.asarray(_toeplitz_map())
    rel_pad = jnp.pad(rel_bias[0], ((0, 0), (0, REL_PAD - (2 * MAX_REL + 1))))
    gen = _select_call(rel_pad, sel.T.astype(BF16), "bias_rows")
    small = {"g_pre_mix": g_pre_mix, "g_post_mix": g_post_mix, "b_in": b_in, "gen": gen,
             "w_dw_conv": _unshard_cols(wdc_g[:, :CONV_K, :]), "b_dw_conv": b_dw_conv, "g_conv_ln": g_conv_ln,
             "b_conv_ln": b_conv_ln, "b_conv_o": b_conv_o, "g_pre_ffn": g_pre_ffn, "g_post_ffn": g_post_ffn,
             "w_dw_ffn": _unshard_cols(wdf_g[:, :FFN_K, :]), "b_dw_ffn": _ff_swap(b_dw_ffn)}

    loss_cols, grad_x, dmod, reduced, sm = _local_step(xs, target, mod, w_in_ready, {n: own[n] for n in LATE}, small, dist)

    d_rel = _select_call(sm["gen"], sel.astype(BF16), "bias_fold")[:, :2 * MAX_REL + 1]
    small_grads = {"g_pre_mix": sm["g_pre_mix"], "g_post_mix": sm["g_post_mix"], "b_in": sm["b_in"], "rel_bias": d_rel[None],
                   "b_dw_conv": sm["b_dw_conv"], "g_conv_ln": sm["g_conv_ln"], "b_conv_ln": sm["b_conv_ln"],
                   "b_conv_o": sm["b_conv_o"], "g_pre_ffn": sm["g_pre_ffn"], "g_post_ffn": sm["g_post_ffn"],
                   "b_dw_ffn": _ff_swap(sm["b_dw_ffn"]), "w_dw_conv": sm["w_dw_conv"], "w_dw_ffn": _ff_swap(sm["w_dw_ffn"])}
    order = [n for n in SMALL_ORDER if n != "b_ada"]
    packed, offs = _pack([jnp.concatenate(dmod, axis=1)] + [small_grads[n] for n in order] + [loss_cols])
    every, total = _allgather8(packed, "gather_small_grads")
    loss = jnp.sum(total[offs[-1]])
    offs = offs[:-1]
    dmod_all = every[:, 0:6, :].reshape(8, 6 * D)
    full_shapes = {n: given[n].shape for n in order}
    full_shapes["w_dw_conv"], full_shapes["w_dw_ffn"] = (1, CONV_K, 512), (1, FFN_K, 2 * D_FF)
    sums = dict(zip(order, _unpack(total, offs[1:], [full_shapes[n] for n in order])))
    sums["b_ada"] = total[0:6].reshape(1, 6 * D)
    sums["w_dw_conv"] = lax.dynamic_slice(sums["w_dw_conv"], (0, 0, shard * 128), (1, CONV_K, 128))
    sums["w_dw_ffn"] = lax.dynamic_slice(sums["w_dw_ffn"], (0, 0, shard * FF_BLOCK), (1, FFN_K, FF_BLOCK))

    upd = dict(zip(SMALL_ORDER, _adamw_many(
        [(given[n], sums[n], given["m_" + n], given["v_" + n]) for n in SMALL_ORDER], "adamw_small")))

    dmod_shard = lax.dynamic_slice(dmod_all, (0, shard * 1536), (8, 1536))
    ada = _ada_bwd_adamw(c_act.T, dmod_shard, w_ada[0], m_w_ada[0], v_w_ada[0], "ada_bwd_adamw")

    out = {"grad_w_ada": ada[0][None], "delta_w_ada": ada[1][None], "new_m_w_ada": ada[2][None], "new_v_w_ada": ada[3][None]}
    for n in BIG:
        g = reduced[n]
        g, dl, nm, nv = _adamw(given["w_" + n][0], g, given["m_w_" + n][0], given["v_w_" + n][0], "adamw_" + n)
        out["grad_w_" + n], out["delta_w_" + n], out["new_m_w_" + n], out["new_v_w_" + n] = g[None], dl[None], nm[None], nv[None]
    for n in SMALL_ORDER:
        out["grad_" + n], out["delta_" + n], out["new_m_" + n], out["new_v_" + n] = sums[n], *upd[n]

    weights = ["w_ada", "b_ada", "g_pre_mix", "g_post_mix", "w_in", "b_in", "rel_bias", "w_attn_o", "w_dw_conv", "b_dw_conv",
               "g_conv_ln", "b_conv_ln", "w_conv_o", "b_conv_o", "w_mix_o", "g_pre_ffn", "g_post_ffn", "w_up", "w_dw_ffn",
               "b_dw_ffn", "w_down"]
    return (loss, grad_x[None], *[out["grad_" + n] for n in weights], *[out["delta_" + n] for n in weights],
            *[out["new_m_" + n] for n in weights], *[out["new_v_" + n] for n in weights])
```

```python
import functools
import math

import numpy as np
import jax
import jax.numpy as jnp
from jax import lax
from jax.experimental import pallas as pl
from jax.experimental.pallas import tpu as pltpu

F32, BF16 = jnp.float32, jnp.bfloat16
MESH = pl.DeviceIdType.MESH

D = 1024
D_IN = 4608
D_FF = 2816
CONV_K = 31
FFN_K = 3
N_HEADS = 8
CHUNK = 64
LEFT_CHUNKS = 8
MAX_REL = 128
EPS = 1e-6
NEG_INF = -1e30
Q_TILE = 256
WINDOW = Q_TILE + LEFT_CHUNKS * CHUNK
STEP_ROWS = 256
REL_PAD = 384
TOEP = 1024
ROW_TILE = 256
VMEM_LIMIT = 60 * 1024 * 1024

ADAM_LR, ADAM_B1, ADAM_B2, ADAM_EPS, ADAM_WD, ADAM_STEP = 0.001, 0.9, 0.999, 1e-08, 0.01, 10


def _params(sem=None):
    return pltpu.CompilerParams(dimension_semantics=sem, vmem_limit_bytes=VMEM_LIMIT)


def _sds(shape, dtype):
    return jax.ShapeDtypeStruct(tuple(shape), dtype)


ANY = pl.BlockSpec(memory_space=pl.ANY)


class _Carried:
    def __init__(self, ins, out_shapes, aliases, n_sems, start, finish):
        self.ins, self.out_shapes, self.aliases = list(ins), list(out_shapes), dict(aliases)
        self.n_sems, self.start, self.finish = n_sems, start, finish


def _call(body, *, grid, in_specs, out_specs, out_shape, scratch_shapes, sem, name, args, carried=None):
    in_specs, out_specs, out_shape = list(in_specs), list(out_specs), list(out_shape)
    scratch_shapes = list(scratch_shapes)
    if carried is None:
        return pl.pallas_call(body, grid=grid, in_specs=in_specs, out_specs=out_specs, out_shape=out_shape,
                              scratch_shapes=scratch_shapes, compiler_params=_params(sem), name=name)(*args)
    n_in, n_out, n_scr = len(in_specs), len(out_specs), len(scratch_shapes)
    c_in, c_out = len(carried.ins), len(carried.out_shapes)

    def full(*refs):
        pos = [0]

        def take(k):
            part = refs[pos[0]:pos[0] + k]
            pos[0] += k
            return part

        ins, cins, outs, couts, scr = take(n_in), take(c_in), take(n_out), take(c_out), take(n_scr)
        send_sems, recv_sems = take(2)
        first = last = None
        for d, size in enumerate(grid):
            pid = pl.program_id(d)
            first = (pid == 0) if first is None else first & (pid == 0)
            last = (pid == size - 1) if last is None else last & (pid == size - 1)

        @pl.when(first)
        def _():
            carried.start(cins, couts, send_sems, recv_sems)

        body(*ins, *outs, *scr)

        @pl.when(last)
        def _():
            carried.finish(cins, couts, send_sems, recv_sems)

    sems = [pltpu.SemaphoreType.DMA((carried.n_sems,)), pltpu.SemaphoreType.DMA((carried.n_sems,))]
    return pl.pallas_call(
        full, grid=grid, in_specs=in_specs + [ANY] * c_in, out_specs=out_specs + [ANY] * c_out,
        out_shape=out_shape + carried.out_shapes, scratch_shapes=scratch_shapes + sems,
        input_output_aliases={n_in + k: n_out + v for k, v in carried.aliases.items()},
        compiler_params=_params(tuple("arbitrary" for _ in grid)), name=name,
    )(*args, *carried.ins)


def _run_carried(carried, name):
    c_in = len(carried.ins)

    def body(*refs):
        cins, couts = refs[:c_in], refs[c_in:c_in + len(carried.out_shapes)]
        send_sems, recv_sems = refs[-2:]
        carried.start(cins, couts, send_sems, recv_sems)
        carried.finish(cins, couts, send_sems, recv_sems)

    return pl.pallas_call(
        body, in_specs=[ANY] * c_in, out_specs=[ANY] * len(carried.out_shapes), out_shape=carried.out_shapes,
        scratch_shapes=[pltpu.SemaphoreType.DMA((carried.n_sems,)), pltpu.SemaphoreType.DMA((carried.n_sems,))],
        input_output_aliases=carried.aliases, name=name,
    )(*carried.ins)


def _matmul(a, b, *, form, out_dtype, tm, tn, tk, name, bias=None, add=None, out_sharded=False, carried=None):
    b3 = b.ndim == 3
    resident = 0
    if form == "nn":
        m, k = a.shape
        n = b.shape[0] * b.shape[2] if b3 else b.shape[1]
        dn = (((1,), (0,)), ((), ()))
        a_spec = pl.BlockSpec((tm, tk), lambda i, j, kk: (i, kk))
        if b3 and tn == n and tk == k:
            resident = b.shape[0]
            b_spec = pl.BlockSpec(b.shape, lambda i, j, kk: (0, 0, 0))
        else:
            b_spec = (pl.BlockSpec((None, tk, tn), lambda i, j, kk: (j, kk, 0)) if b3
                      else pl.BlockSpec((tk, tn), lambda i, j, kk: (kk, j)))
    elif form == "nt":
        m, k = a.shape
        n = b.shape[1] if b3 else b.shape[0]
        dn = (((1,), (1,)), ((), ()))
        a_spec = pl.BlockSpec((tm, tk), lambda i, j, kk: (i, kk))
        if b3 and tk == k:
            resident = b.shape[0]
            b_spec = pl.BlockSpec((resident, tn, b.shape[2]), lambda i, j, kk: (0, j, 0))
        else:
            b_spec = (pl.BlockSpec((None, tn, tk), lambda i, j, kk: (kk, j, 0)) if b3
                      else pl.BlockSpec((tn, tk), lambda i, j, kk: (j, kk)))
    else:
        k, m = a.shape
        n = b.shape[1]
        dn = (((0,), (0,)), ((), ()))
        a_spec = pl.BlockSpec((tk, tm), lambda i, j, kk: (kk, i))
        b_spec = pl.BlockSpec((tk, tn), lambda i, j, kk: (kk, j))
    assert m % tm == 0 and n % tn == 0 and k % tk == 0, (name, m, n, k, tm, tn, tk)
    nk = k // tk
    in_specs, args = [a_spec, b_spec], [a, b]
    if bias is not None:
        in_specs.append(pl.BlockSpec((1, tn), lambda i, j, kk: (0, j)))
        args.append(bias)
    if add is not None:
        in_specs.append(pl.BlockSpec((tm, tn), lambda i, j, kk: (i, j)))
        args.append(add)
    if out_sharded:
        out_shape = _sds((n // tn, m, tn), out_dtype)
        out_spec = pl.BlockSpec((None, tm, tn), lambda i, j, kk: (j, i, 0))
    else:
        out_shape = _sds((m, n), out_dtype)
        out_spec = pl.BlockSpec((tm, tn), lambda i, j, kk: (i, j))

    def body(*refs):
        a_ref, b_ref = refs[0], refs[1]
        pos = 2
        bias_ref = add_ref = None
        if bias is not None:
            bias_ref, pos = refs[pos], pos + 1
        if add is not None:
            add_ref, pos = refs[pos], pos + 1
        o_ref = refs[pos]
        if resident and form == "nn":
            ns = b_ref.shape[2]
            for s in range(resident):
                cols = slice(s * ns, (s + 1) * ns)
                ps = lax.dot_general(a_ref[...], b_ref[s], dn, preferred_element_type=F32)
                if bias_ref is not None:
                    ps = ps + bias_ref[:, cols]
                o_ref[:, cols] = ps.astype(o_ref.dtype)
            return
        if resident:
            ks = b_ref.shape[2]
            p = None
            for s in range(resident):
                ps = lax.dot_general(a_ref[:, s * ks:(s + 1) * ks], b_ref[s], dn, preferred_element_type=F32)
                p = ps if p is None else p + ps
        else:
            av, bv = a_ref[...], b_ref[...]
            if av.dtype != BF16:
                av = av.astype(BF16)
            if bv.dtype != BF16:
                bv = bv.astype(BF16)
            p = lax.dot_general(av, bv, dn, preferred_element_type=F32)

        def finish(acc):
            if bias_ref is not None:
                acc = acc + bias_ref[...]
            if add_ref is not None:
                acc = acc + add_ref[...]
            o_ref[...] = acc.astype(o_ref.dtype)

        if nk == 1:
            finish(p)
        else:
            acc_ref = refs[pos + 1]
            kk = pl.program_id(2)

            @pl.when(kk == 0)
            def _():
                acc_ref[...] = p

            @pl.when(kk > 0)
            def _():
                acc_ref[...] += p

            @pl.when(kk == nk - 1)
            def _():
                finish(acc_ref[...])

    res = _call(body, grid=(m // tm, n // tn, nk), in_specs=in_specs, out_specs=[out_spec], out_shape=[out_shape],
                scratch_shapes=[pltpu.VMEM((tm, tn), F32)] if nk > 1 else [],
                sem=("parallel", "parallel", "arbitrary"), name=name, args=args, carried=carried)
    return res[0] if carried is None else (res[0], res[1:])


def _rowcall(fn, rows, consts, row_outs, acc_outs, *, name, tm=ROW_TILE, col_grid=1):
    n_rows = rows[0][0].shape[0]
    assert n_rows % tm == 0
    grid = (col_grid, n_rows // tm)
    in_specs = [pl.BlockSpec((tm, w), functools.partial(lambda c, i, cb: (i, cb + c), cb=cb)) for _, w, cb in rows]
    in_specs += [pl.BlockSpec(k.shape, functools.partial(lambda c, i, nd: (0,) * nd, nd=k.ndim)) for k in consts]
    out_specs = [pl.BlockSpec((tm, w), lambda c, i: (i, c)) for _, _, _, w in row_outs]
    out_specs += [pl.BlockSpec((r, w), lambda c, i: (0, c)) for r, _, w in acc_outs]
    out_shape = [_sds((nr, nc), dt) for nr, nc, dt, _ in row_outs] + [_sds((r, nc), F32) for r, nc, _ in acc_outs]
    n_in, n_ro = len(rows) + len(consts), len(row_outs)

    def body(*refs):
        res = fn(*[r[...] for r in refs[:n_in]])
        if not isinstance(res, (tuple, list)):
            res = (res,)
        outs = refs[n_in:]
        for o_ref, val in zip(outs[:n_ro], res[:n_ro]):
            o_ref[...] = val.astype(o_ref.dtype)
        if acc_outs:
            first = pl.program_id(1) == 0

            @pl.when(first)
            def _():
                for o_ref, val in zip(outs[n_ro:], res[n_ro:]):
                    o_ref[...] = val

            @pl.when(jnp.logical_not(first))
            def _():
                for o_ref, val in zip(outs[n_ro:], res[n_ro:]):
                    o_ref[...] += val

    out = pl.pallas_call(
        body, grid=grid, in_specs=in_specs, out_specs=out_specs, out_shape=out_shape,
        compiler_params=_params(("arbitrary", "arbitrary")), name=name,
    )(*[r[0] for r in rows], *consts)
    return out


def _matmul_rows(b, *, form, tm, tk, fn, rows, consts, row_outs, acc_outs, name, a=None, a_rows=None, a_fn=None,
                 carried=None):
    b3 = b.ndim == 3
    resident = 0
    if form == "nn":
        k, n = b.shape
        b_spec = pl.BlockSpec((tk, n), lambda i, kk: (kk, 0))
        dn = (((1,), (0,)), ((), ()))
    else:
        n = b.shape[1] if b3 else b.shape[0]
        k = b.shape[0] * b.shape[2] if b3 else b.shape[1]
        if b3 and tk == k:
            resident = b.shape[0]
            b_spec = pl.BlockSpec(b.shape, lambda i, kk: (0, 0, 0))
        else:
            b_spec = (pl.BlockSpec((None, n, tk), lambda i, kk: (kk, 0, 0)) if b3
                      else pl.BlockSpec((n, tk), lambda i, kk: (0, kk)))
        dn = (((1,), (1,)), ((), ()))
    nk = k // tk
    lhs_in = [(a, tk, 0)] if a is not None else list(a_rows)
    assert a is not None or nk == 1
    m = lhs_in[0][0].shape[0]
    n_lhs = len(lhs_in)
    in_specs = [pl.BlockSpec((tm, tk), lambda i, kk: (i, kk))] if a is not None else [
        pl.BlockSpec((tm, w), functools.partial(lambda i, kk, cb: (i, cb), cb=cb)) for _, w, cb in a_rows]
    in_specs.append(b_spec)
    in_specs += [pl.BlockSpec((tm, w), functools.partial(lambda i, kk, cb: (i, cb), cb=cb)) for _, w, cb in rows]
    in_specs += [pl.BlockSpec(c.shape, functools.partial(lambda i, kk, nd: (0,) * nd, nd=c.ndim)) for c in consts]
    out_specs = [pl.BlockSpec((tm, w), lambda i, kk: (i, 0)) for _, w in row_outs]
    out_specs += [pl.BlockSpec((r, w), lambda i, kk: (0, 0)) for r, w in acc_outs]
    out_shape = [_sds((m, w), dt) for dt, w in row_outs] + [_sds((r, w), F32) for r, w in acc_outs]
    n_rows, n_consts, n_ro, n_acc = len(rows), len(consts), len(row_outs), len(acc_outs)

    def body(*refs):
        pos = n_lhs + 1
        row_refs, const_refs = refs[pos:pos + n_rows], refs[pos + n_rows:pos + n_rows + n_consts]
        pos += n_rows + n_consts
        out_refs, acc_refs = refs[pos:pos + n_ro], refs[pos + n_ro:pos + n_ro + n_acc]
        i, kk = pl.program_id(0), pl.program_id(1)
        if resident:
            b_ref, ks, p = refs[n_lhs], b.shape[2], None
            for s in range(resident):
                ps = lax.dot_general(refs[0][:, s * ks:(s + 1) * ks], b_ref[s], dn, preferred_element_type=F32)
                p = ps if p is None else p + ps
        else:
            lhs = refs[0][...] if a is not None else a_fn(*[r[...] for r in refs[:n_lhs]]).astype(BF16)
            p = lax.dot_general(lhs, refs[n_lhs][...], dn, preferred_element_type=F32)

        def finish(acc):
            extra = [r[...] for r in row_refs] + [c[...] for c in const_refs]
            res = fn(acc, lhs, *extra) if a is None else fn(acc, *extra)
            for o_ref, val in zip(out_refs, res[:n_ro]):
                o_ref[...] = val.astype(o_ref.dtype)
            if n_acc:
                @pl.when(i == 0)
                def _():
                    for o_ref, val in zip(acc_refs, res[n_ro:]):
                        o_ref[...] = val

                @pl.when(i > 0)
                def _():
                    for o_ref, val in zip(acc_refs, res[n_ro:]):
                        o_ref[...] += val

        if nk == 1:
            finish(p)
        else:
            acc_ref = refs[pos + n_ro + n_acc]

            @pl.when(kk == 0)
            def _():
                acc_ref[...] = p

            @pl.when(kk > 0)
            def _():
                acc_ref[...] += p

            @pl.when(kk == nk - 1)
            def _():
                finish(acc_ref[...])

    res = _call(body, grid=(m // tm, nk), in_specs=in_specs, out_specs=out_specs, out_shape=out_shape,
                scratch_shapes=[pltpu.VMEM((tm, n), F32)] if nk > 1 else [], sem=("arbitrary", "arbitrary"),
                name=name, args=[r[0] for r in lhs_in] + [b] + [r[0] for r in rows] + list(consts), carried=carried)
    own = n_ro + n_acc
    return res[:own] if carried is None else (res[:own], res[own:])


def _colsum(v):
    return jnp.sum(v, axis=0, keepdims=True)


def _sigmoid(v):
    return 1.0 / (1.0 + jnp.exp(-v))


_GELU_C = math.sqrt(2.0 / math.pi)


def _gelu(v):
    return 0.5 * v * (1.0 + jnp.tanh(_GELU_C * (v + 0.044715 * (v * v * v))))


def _gelu_and_grad(v):
    th = jnp.tanh(_GELU_C * (v + 0.044715 * (v * v * v)))
    g = 0.5 * v * (1.0 + th)
    dg = 0.5 * (1.0 + th) + 0.5 * v * (1.0 - th * th) * (_GELU_C * (1.0 + 3.0 * 0.044715 * (v * v)))
    return g, dg


def _rms_stats(v):
    r = lax.rsqrt(jnp.mean(v * v, axis=-1, keepdims=True) + EPS)
    return v * r, r


def _rms_bwd(dn, vn, r):
    return r * (dn - vn * jnp.mean(dn * vn, axis=-1, keepdims=True))


def _pre_norm(x, g, sc, sh, name):
    def fn(xv, gv, scv, shv):
        xn, _ = _rms_stats(xv)
        return (xn * gv) * (1.0 + scv) + shv
    return _rowcall(fn, [(x, D, 0)], [g, sc, sh], [(x.shape[0], D, BF16, D)], [], name=name)[0]


def _pre_norm_bwd(dh, x, dx_other, g, sc, name):
    def fn(dhv, xv, dov, gv, scv):
        xn, r = _rms_stats(xv)
        yn = xn * gv
        dyn = dhv * (1.0 + scv)
        dx = _rms_bwd(dyn * gv, xn, r)
        return dov + dx, _colsum(dhv), _colsum(dhv * yn), _colsum(dyn * xn)
    t = x.shape[0]
    return _rowcall(fn, [(dh, D, 0), (x, D, 0), (dx_other, D, 0)], [g, sc], [(t, D, F32, D)],
                    [(1, D, D)] * 3, name=name)


CONV_HALO = 32


def _layer_norm_parts(u):
    mu = jnp.mean(u, axis=-1, keepdims=True)
    d = u - mu
    r = lax.rsqrt(jnp.mean(d * d, axis=-1, keepdims=True) + EPS)
    return d * r, r


LANES = 128
SUBLANE_ROWS = 8
CONV_ROWS = 64


def _lanes(c):
    return slice(c * LANES, (c + 1) * LANES)


def _conv_branch(z, w_dw, b_dw, g_ln, b_ln, name, tm=ROW_TILE):
    t = z.shape[0]
    per = tm // CONV_HALO
    n_chunks = 512 // LANES

    def body(ga_ref, gb_ref, gah_ref, gbh_ref, w_ref, b_ref, g_ref, bl_ref, u1_ref, u3_ref, scr):
        i = pl.program_id(0)
        u0h = jnp.where(i > 0, gah_ref[...] * _sigmoid(gbh_ref[...]), 0.0)
        u0 = ga_ref[...] * _sigmoid(gb_ref[...])
        for c in range(n_chunks):
            scr[c, 0:CONV_HALO, :] = u0h[:, _lanes(c)]
            scr[c, CONV_HALO:CONV_HALO + tm, :] = u0[:, _lanes(c)]
        for c in range(n_chunks):
            for r0 in range(0, tm, CONV_ROWS):
                acc = jnp.zeros((CONV_ROWS, LANES), F32) + b_ref[:, _lanes(c)]
                for j in range(CONV_K):
                    acc = acc + w_ref[j:j + 1, _lanes(c)] * scr[c, pl.ds(r0 + CONV_HALO - (CONV_K - 1) + j, CONV_ROWS), :]
                u1_ref[r0:r0 + CONV_ROWS, _lanes(c)] = acc
        xh, _ = _layer_norm_parts(u1_ref[...])
        u2 = xh * g_ref[...] + bl_ref[...]
        u3_ref[...] = (u2 * _sigmoid(u2)).astype(BF16)

    cur = lambda cb: pl.BlockSpec((tm, 512), lambda i: (i, cb))
    halo = lambda cb: pl.BlockSpec((CONV_HALO, 512), lambda i: (jnp.maximum(i * per - 1, 0), cb))
    whole = lambda a: pl.BlockSpec(a.shape, lambda i: (0, 0))
    return pl.pallas_call(
        body, grid=(t // tm,),
        in_specs=[cur(3), cur(4), halo(3), halo(4), whole(w_dw), whole(b_dw), whole(g_ln), whole(b_ln)],
        out_specs=[pl.BlockSpec((tm, 512), lambda i: (i, 0))] * 2,
        out_shape=[_sds((t, 512), F32), _sds((t, 512), BF16)],
        scratch_shapes=[pltpu.VMEM((n_chunks, CONV_HALO + tm, LANES), F32)],
        compiler_params=_params(("arbitrary",)), name=name,
    )(z, z, z, z, w_dw, b_dw, g_ln, b_ln)


def _conv_branch_bwd(du3, u1, z, w_dw, g_ln, b_ln, name, tm=ROW_TILE, carried=None):
    t = z.shape[0]
    per = tm // CONV_HALO
    last = t // tm - 1
    n_chunks = 512 // LANES

    def du1_of(du3v, u1v, g, b):
        xh, r = _layer_norm_parts(u1v)
        u2 = xh * g + b
        s = _sigmoid(u2)
        du2 = du3v * (s * (1.0 + u2 * (1.0 - s)))
        dxh = du2 * g
        du1 = r * (dxh - jnp.mean(dxh, axis=-1, keepdims=True) - xh * jnp.mean(dxh * xh, axis=-1, keepdims=True))
        return du1, du2, xh

    def body(d_ref, u_ref, dn_ref, un_ref, ga_ref, gb_ref, gah_ref, gbh_ref, w_ref, g_ref, bl_ref,
             dglu_ref, dw_ref, dbdw_ref, dg_ref, dbl_ref, dbin_ref, scr, scd):
        i = pl.program_id(0)
        g, b = g_ref[...], bl_ref[...]
        du1, du2, xh = du1_of(d_ref[...], u_ref[...], g, b)
        du1n, _, _ = du1_of(dn_ref[...], un_ref[...], g, b)
        du1n = jnp.where(i < last, du1n, 0.0)
        sgb = _sigmoid(gb_ref[...])
        ga = ga_ref[...]
        u0 = ga * sgb
        u0h = jnp.where(i > 0, gah_ref[...] * _sigmoid(gbh_ref[...]), 0.0)
        for c in range(n_chunks):
            scd[c, 0:tm, :] = du1[:, _lanes(c)]
            scd[c, tm:tm + CONV_HALO, :] = du1n[:, _lanes(c)]
            scr[c, 0:CONV_HALO, :] = u0h[:, _lanes(c)]
            scr[c, CONV_HALO:CONV_HALO + tm, :] = u0[:, _lanes(c)]

        @pl.when(i == 0)
        def _():
            for ref in (dw_ref, dbdw_ref, dg_ref, dbl_ref, dbin_ref):
                ref[...] = jnp.zeros_like(ref)

        dsg = ga * (sgb * (1.0 - sgb))
        for c in range(n_chunks):
            gate = slice(512 + c * LANES, 512 + (c + 1) * LANES)
            for r0 in range(0, tm, CONV_ROWS):
                rows = slice(r0, r0 + CONV_ROWS)
                du0 = jnp.zeros((CONV_ROWS, LANES), F32)
                for j in range(CONV_K):
                    du0 = du0 + w_ref[j:j + 1, _lanes(c)] * scd[c, pl.ds(r0 + CONV_K - 1 - j, CONV_ROWS), :]
                dga = du0 * sgb[rows, _lanes(c)]
                dgb = du0 * dsg[rows, _lanes(c)]
                dglu_ref[rows, _lanes(c)] = dga.astype(BF16)
                dglu_ref[rows, gate] = dgb.astype(BF16)
                dbin_ref[:, _lanes(c)] += _colsum(dga)
                dbin_ref[:, gate] += _colsum(dgb)
            for j in range(CONV_K):
                dwj = jnp.zeros((SUBLANE_ROWS, LANES), F32)
                for r0 in range(0, tm, CONV_ROWS):
                    prod = (scd[c, pl.ds(r0, CONV_ROWS), :]
                            * scr[c, pl.ds(r0 + CONV_HALO - (CONV_K - 1) + j, CONV_ROWS), :])
                    dwj = dwj + jnp.sum(prod.reshape(CONV_ROWS // SUBLANE_ROWS, SUBLANE_ROWS, LANES), axis=0)
                dw_ref[j:j + 1, _lanes(c)] += _colsum(dwj)
        dbdw_ref[...] += _colsum(du1)
        dg_ref[...] += _colsum(du2 * xh)
        dbl_ref[...] += _colsum(du2)

    cur = lambda cb: pl.BlockSpec((tm, 512), lambda i: (i, cb))
    prev = lambda cb: pl.BlockSpec((CONV_HALO, 512), lambda i: (jnp.maximum(i * per - 1, 0), cb))
    nxt = pl.BlockSpec((CONV_HALO, 512), lambda i: (jnp.minimum((i + 1) * per, t // CONV_HALO - 1), 0))
    whole = lambda a: pl.BlockSpec(a.shape, lambda i: (0, 0))
    acc = lambda r, w: pl.BlockSpec((r, w), lambda i: (0, 0))
    res = _call(
        body, grid=(t // tm,),
        in_specs=[cur(0), cur(0), nxt, nxt, cur(3), cur(4), prev(3), prev(4), whole(w_dw), whole(g_ln), whole(b_ln)],
        out_specs=[pl.BlockSpec((tm, 1024), lambda i: (i, 0)), acc(CONV_K, 512), acc(1, 512), acc(1, 512),
                   acc(1, 512), acc(1, 1024)],
        out_shape=[_sds((t, 1024), BF16), _sds((CONV_K, 512), F32), _sds((1, 512), F32), _sds((1, 512), F32),
                   _sds((1, 512), F32), _sds((1, 1024), F32)],
        scratch_shapes=[pltpu.VMEM((n_chunks, CONV_HALO + tm, LANES), F32),
                        pltpu.VMEM((n_chunks, tm + CONV_HALO, LANES), F32)],
        sem=("arbitrary",), name=name, args=(du3, u1, du3, u1, z, z, z, z, w_dw, g_ln, b_ln), carried=carried)
    return res[:6] if carried is None else (res[:6], res[6:])


FF_BLOCK = D_FF // 2
FF_HALO = 8
FF_CHUNKS = FF_BLOCK // LANES


FF_ROWS = 64
FF_EXT_ROWS = 88


def _ffn_conv(w_ref, b_ref, scr, k, rows, r0=0):
    acc = b_ref[:, _lanes(k)] + w_ref[0:1, _lanes(k)] * scr[k, pl.ds(r0 + FF_HALO - 2, rows), :]
    acc = acc + w_ref[1:2, _lanes(k)] * scr[k, pl.ds(r0 + FF_HALO - 1, rows), :]
    return acc + w_ref[2:3, _lanes(k)] * scr[k, pl.ds(r0 + FF_HALO, rows), :]


def _ffn_act(up, w3, b3, name, tm=ROW_TILE):
    t = up.shape[0]
    per = tm // FF_HALO
    wide = 2 * FF_BLOCK

    def body(u_ref, uh_ref, w_ref, b_ref, o_ref, scr):
        i = pl.program_id(1)
        for k in range(2 * FF_CHUNKS):
            scr[k, 0:FF_HALO, :] = jnp.where(i > 0, uh_ref[:, _lanes(k)], 0.0)
            scr[k, FF_HALO:FF_HALO + tm, :] = u_ref[:, _lanes(k)]
        for cc in range(FF_CHUNKS):
            for r0 in range(0, tm, FF_ROWS):
                val = _ffn_conv(w_ref, b_ref, scr, cc, FF_ROWS, r0)
                gate = _ffn_conv(w_ref, b_ref, scr, FF_CHUNKS + cc, FF_ROWS, r0)
                o_ref[r0:r0 + FF_ROWS, _lanes(cc)] = (_gelu(gate) * val).astype(BF16)

    return pl.pallas_call(
        body, grid=(2, t // tm),
        in_specs=[pl.BlockSpec((tm, wide), lambda c, i: (i, c)),
                  pl.BlockSpec((FF_HALO, wide), lambda c, i: (jnp.maximum(i * per - 1, 0), c)),
                  pl.BlockSpec((FFN_K, wide), lambda c, i: (0, c)),
                  pl.BlockSpec((1, wide), lambda c, i: (0, c))],
        out_specs=pl.BlockSpec((tm, FF_BLOCK), lambda c, i: (i, c)),
        out_shape=_sds((t, D_FF), BF16),
        scratch_shapes=[pltpu.VMEM((2 * FF_CHUNKS, FF_HALO + tm, LANES), F32)],
        compiler_params=_params(("arbitrary", "arbitrary")), name=name,
    )(up, up, w3, b3)


def _ffn_act_bwd(dact, up, w3, b3, name, tm=ROW_TILE):
    t = up.shape[0]
    per = tm // FF_HALO
    wide = 2 * FF_BLOCK
    last = t // tm - 1
    ext = tm + FF_HALO

    def body(u_ref, up_ref, un_ref, d_ref, dn_ref, w_ref, b_ref, o_ref, dw_ref, db_ref, scr, scd):
        i = pl.program_id(1)
        for k in range(2 * FF_CHUNKS):
            scr[k, 0:FF_HALO, :] = jnp.where(i > 0, up_ref[:, _lanes(k)], 0.0)
            scr[k, FF_HALO:FF_HALO + tm, :] = u_ref[:, _lanes(k)]
            scr[k, FF_HALO + tm:FF_HALO + ext, :] = un_ref[:, _lanes(k)]
        dn = jnp.where(i < last, dn_ref[...], 0.0)

        @pl.when(i == 0)
        def _():
            dw_ref[...] = jnp.zeros_like(dw_ref)
            db_ref[...] = jnp.zeros_like(db_ref)

        for cc in range(FF_CHUNKS):
            gc = FF_CHUNKS + cc
            for r0 in range(0, ext, FF_EXT_ROWS):
                rows = pl.ds(r0, FF_EXT_ROWS)
                val = _ffn_conv(w_ref, b_ref, scr, cc, FF_EXT_ROWS, r0)
                gel, dgel = _gelu_and_grad(_ffn_conv(w_ref, b_ref, scr, gc, FF_EXT_ROWS, r0))
                da = d_ref[r0:r0 + FF_EXT_ROWS, _lanes(cc)] if r0 + FF_EXT_ROWS <= tm else jnp.concatenate(
                    [d_ref[r0:tm, _lanes(cc)], dn[:, _lanes(cc)]], axis=0)
                scd[cc, rows, :] = da * gel
                scd[gc, rows, :] = da * val * dgel
            for k in (cc, gc):
                dwk = [jnp.zeros((SUBLANE_ROWS, LANES), F32) for _ in range(FFN_K)]
                dbk = jnp.zeros((SUBLANE_ROWS, LANES), F32)
                for r0 in range(0, tm, FF_ROWS):
                    shifted = [scd[k, pl.ds(r0 + FFN_K - 1 - j, FF_ROWS), :] for j in range(FFN_K)]
                    ucur = scr[k, pl.ds(r0 + FF_HALO, FF_ROWS), :]
                    o_ref[r0:r0 + FF_ROWS, _lanes(k)] = (
                        w_ref[0:1, _lanes(k)] * shifted[0] + w_ref[1:2, _lanes(k)] * shifted[1]
                        + w_ref[2:3, _lanes(k)] * shifted[2]).astype(BF16)
                    fold = lambda v: jnp.sum(v.reshape(FF_ROWS // SUBLANE_ROWS, SUBLANE_ROWS, LANES), axis=0)
                    for j in range(FFN_K):
                        dwk[j] = dwk[j] + fold(shifted[j] * ucur)
                    dbk = dbk + fold(shifted[FFN_K - 1])
                for j in range(FFN_K):
                    dw_ref[j:j + 1, _lanes(k)] += _colsum(dwk[j])
                db_ref[:, _lanes(k)] += _colsum(dbk)

    nblk = t // FF_HALO
    return pl.pallas_call(
        body, grid=(2, t // tm),
        in_specs=[pl.BlockSpec((tm, wide), lambda c, i: (i, c)),
                  pl.BlockSpec((FF_HALO, wide), lambda c, i: (jnp.maximum(i * per - 1, 0), c)),
                  pl.BlockSpec((FF_HALO, wide), lambda c, i: (jnp.minimum((i + 1) * per, nblk - 1), c)),
                  pl.BlockSpec((tm, FF_BLOCK), lambda c, i: (i, c)),
                  pl.BlockSpec((FF_HALO, FF_BLOCK), lambda c, i: (jnp.minimum((i + 1) * per, nblk - 1), c)),
                  pl.BlockSpec((FFN_K, wide), lambda c, i: (0, c)),
                  pl.BlockSpec((1, wide), lambda c, i: (0, c))],
        out_specs=[pl.BlockSpec((tm, wide), lambda c, i: (i, c)),
                   pl.BlockSpec((FFN_K, wide), lambda c, i: (0, c)),
                   pl.BlockSpec((1, wide), lambda c, i: (0, c))],
        out_shape=[_sds((t, 2 * D_FF), BF16), _sds((FFN_K, 2 * D_FF), F32), _sds((1, 2 * D_FF), F32)],
        scratch_shapes=[pltpu.VMEM((2 * FF_CHUNKS, FF_HALO + ext, LANES), F32),
                        pltpu.VMEM((2 * FF_CHUNKS, ext, LANES), F32)],
        compiler_params=_params(("arbitrary", "arbitrary")), name=name,
    )(up, up, up, dact, dact, w3, b3)


def _toeplitz_map():
    f = np.zeros((TOEP, REL_PAD), np.float32)
    for m in range(TOEP - 1):
        rel = (WINDOW - 1) - m
        f[m, int(np.clip(rel, -MAX_REL, MAX_REL)) + MAX_REL] = 1.0
    return f


def _split3(v):
    hi = v.astype(BF16)
    r1 = v - hi.astype(F32)
    mid = r1.astype(BF16)
    lo = (r1 - mid.astype(F32)).astype(BF16)
    return hi, mid, lo


def _exact_select(v, sel):
    out = None
    for part in _split3(v):
        p = jnp.dot(part, sel, preferred_element_type=F32)
        out = p if out is None else out + p
    return out


def _select_call(v, sel, name):
    def body(v_ref, s_ref, o_ref):
        o_ref[...] = _exact_select(v_ref[...], s_ref[...])
    return pl.pallas_call(body, out_shape=_sds((v.shape[0], sel.shape[1]), F32), name=name)(v, sel)


def _band_bias(gen_row):
    b0 = jnp.broadcast_to(gen_row, (Q_TILE, TOEP))
    bias = pltpu.roll(b0, TOEP - (Q_TILE - 1), 1, stride=1, stride_axis=0)[:, :WINDOW]
    qq = lax.broadcasted_iota(jnp.int32, (Q_TILE, WINDOW), 0) // CHUNK
    kc = lax.broadcasted_iota(jnp.int32, (Q_TILE, WINDOW), 1) // CHUNK
    return jnp.where((kc >= qq) & (kc <= qq + LEFT_CHUNKS), bias, NEG_INF)


PAD_ROWS = WINDOW - Q_TILE
NT_DIMS = (((1,), (1,)), ((), ()))
TN_DIMS = (((0,), (0,)), ((), ()))


def _head_mask(hh):
    lane = lax.broadcasted_iota(jnp.int32, (1, 128), 1)
    return (lane < 64) if hh == 0 else (lane >= 64)


SOFTMAX_ROWS = 16


def _probs_block(s_scr, bias, hh, rows, q_start):
    s = s_scr[rows, :] + bias[hh, rows, :]
    col = lax.broadcasted_iota(jnp.int32, (SOFTMAX_ROWS, WINDOW), 1)
    s = jnp.where(col >= PAD_ROWS - q_start, s, NEG_INF)
    p = jnp.exp(s - jnp.max(s, axis=-1, keepdims=True))
    return p / jnp.sum(p, axis=-1, keepdims=True)


def _attention(z, gen, name, carried=None):
    t = z.shape[0]
    n_i = t // STEP_ROWS

    def body(q_ref, k_ref, v_ref, g_ref, o_ref, kpad, vpad, bias, s_scr, p_scr):
        hp, i = pl.program_id(0), pl.program_id(1)

        @pl.when(i == 0)
        def _():
            kpad[0:PAD_ROWS, :] = jnp.zeros((PAD_ROWS, 128), BF16)
            vpad[0:PAD_ROWS, :] = jnp.zeros((PAD_ROWS, 128), BF16)
            kpad[PAD_ROWS:PAD_ROWS + t, :] = k_ref[...].astype(BF16)
            vpad[PAD_ROWS:PAD_ROWS + t, :] = v_ref[...].astype(BF16)
            for hh in range(2):
                bias[hh] = _band_bias(g_ref[pl.ds(2 * hp + hh, 1), :])

        for q0 in range(0, STEP_ROWS, Q_TILE):
            q_start = i * STEP_ROWS + q0
            win = pl.ds(pl.multiple_of(q_start, Q_TILE), WINDOW)
            out = None
            for hh in range(2):
                mask = _head_mask(hh)
                qm = jnp.where(mask, q_ref[q0:q0 + Q_TILE, :] * (CHUNK ** -0.5), 0.0).astype(BF16)
                slot = 2 * (q0 // Q_TILE) + hh
                s_scr[slot] = lax.dot_general(qm, kpad[win, :], NT_DIMS, preferred_element_type=F32)
                for r0 in range(0, Q_TILE, SOFTMAX_ROWS):
                    rows = slice(r0, r0 + SOFTMAX_ROWS)
                    p_scr[slot, rows, :] = _probs_block(s_scr.at[slot], bias, hh, rows, q_start).astype(BF16)
                o = jnp.dot(p_scr[slot], vpad[win, :], preferred_element_type=F32)
                out = jnp.where(mask, o, 0.0) if out is None else jnp.where(mask, o, out)
            o_ref[q0:q0 + Q_TILE, :] = out.astype(BF16)

    res = _call(
        body, grid=(4, n_i),
        in_specs=[pl.BlockSpec((STEP_ROWS, 128), lambda h, i: (i, h)),
                  pl.BlockSpec((t, 128), lambda h, i: (0, 4 + h)),
                  pl.BlockSpec((t, 128), lambda h, i: (0, 8 + h)),
                  pl.BlockSpec((N_HEADS, TOEP), lambda h, i: (0, 0))],
        out_specs=[pl.BlockSpec((STEP_ROWS, 128), lambda h, i: (i, h))],
        out_shape=[_sds((t, 512), BF16)],
        scratch_shapes=[pltpu.VMEM((PAD_ROWS + t, 128), BF16), pltpu.VMEM((PAD_ROWS + t, 128), BF16),
                        pltpu.VMEM((2, Q_TILE, WINDOW), F32), pltpu.VMEM((4, Q_TILE, WINDOW), F32),
                        pltpu.VMEM((4, Q_TILE, WINDOW), BF16)],
        sem=("arbitrary", "arbitrary"), name=name, args=(z, z, z, gen), carried=carried)
    return res[0] if carried is None else (res[0], res[1:])


def _attention_bwd(z, datt, gen, name, carried=None):
    t = z.shape[0]
    n_i = t // STEP_ROWS

    def body(q_ref, k_ref, v_ref, d_ref, g_ref, dq_ref, dk_ref, dv_ref, sq_ref, sk_ref, sv_ref, dg_ref,
             kpad, vpad, dkacc, dvacc, bias, dsacc, s_scr, dp_scr, p_scr, ds_scr):
        hp, i = pl.program_id(0), pl.program_id(1)

        @pl.when(i == 0)
        def _():
            kpad[0:PAD_ROWS, :] = jnp.zeros((PAD_ROWS, 128), BF16)
            vpad[0:PAD_ROWS, :] = jnp.zeros((PAD_ROWS, 128), BF16)
            kpad[PAD_ROWS:PAD_ROWS + t, :] = k_ref[...].astype(BF16)
            vpad[PAD_ROWS:PAD_ROWS + t, :] = v_ref[...].astype(BF16)
            dkacc[...] = jnp.zeros_like(dkacc)
            dvacc[...] = jnp.zeros_like(dvacc)
            dsacc[...] = jnp.zeros_like(dsacc)
            for hh in range(2):
                bias[hh] = _band_bias(g_ref[pl.ds(2 * hp + hh, 1), :])

        dq_sum = None
        for q0 in range(0, STEP_ROWS, Q_TILE):
            q_start = i * STEP_ROWS + q0
            win = pl.ds(pl.multiple_of(q_start, Q_TILE), WINDOW)
            dq = None
            for hh in range(2):
                mask = _head_mask(hh)
                qm = jnp.where(mask, q_ref[q0:q0 + Q_TILE, :] * (CHUNK ** -0.5), 0.0).astype(BF16)
                dom = jnp.where(mask, d_ref[q0:q0 + Q_TILE, :], 0.0).astype(BF16)
                slot = 2 * (q0 // Q_TILE) + hh
                s_scr[slot] = lax.dot_general(qm, kpad[win, :], NT_DIMS, preferred_element_type=F32)
                dp_scr[slot] = lax.dot_general(dom, vpad[win, :], NT_DIMS, preferred_element_type=F32)
                for r0 in range(0, Q_TILE, SOFTMAX_ROWS):
                    rows = slice(r0, r0 + SOFTMAX_ROWS)
                    p = _probs_block(s_scr.at[slot], bias, hh, rows, q_start)
                    dp = dp_scr[slot, rows, :]
                    ds = p * (dp - jnp.sum(p * dp, axis=-1, keepdims=True))
                    dsacc[hh, rows, :] += ds
                    ds_scr[slot, rows, :] = ds.astype(BF16)
                    p_scr[slot, rows, :] = p.astype(BF16)
                ds16 = ds_scr[slot]
                dqh = jnp.dot(ds16, kpad[win, :], preferred_element_type=F32) * (CHUNK ** -0.5)
                dq = jnp.where(mask, dqh, 0.0) if dq is None else jnp.where(mask, dqh, dq)
                dkacc[win, :] += lax.dot_general(ds16, qm, TN_DIMS, preferred_element_type=F32)
                dvacc[win, :] += lax.dot_general(p_scr[slot], dom, TN_DIMS, preferred_element_type=F32)
            dq_ref[q0:q0 + Q_TILE, :] = dq.astype(BF16)
            dq_sum = _colsum(dq) if dq_sum is None else dq_sum + _colsum(dq)

        @pl.when(i == 0)
        def _():
            sq_ref[...] = dq_sum

        @pl.when(i > 0)
        def _():
            sq_ref[...] += dq_sum

        @pl.when(i == n_i - 1)
        def _():
            dk = dkacc[PAD_ROWS:PAD_ROWS + t, :]
            dv = dvacc[PAD_ROWS:PAD_ROWS + t, :]
            dk_ref[...] = dk.astype(BF16)
            dv_ref[...] = dv.astype(BF16)
            sk_ref[...] = _colsum(dk)
            sv_ref[...] = _colsum(dv)
            rr = lax.broadcasted_iota(jnp.int32, (Q_TILE, Q_TILE), 0)
            cc = lax.broadcasted_iota(jnp.int32, (Q_TILE, Q_TILE), 1)
            rev = jnp.where(rr + cc == Q_TILE - 1, 1.0, 0.0).astype(BF16)
            for hh in range(2):
                acc = None
                for part in _split3(dsacc[hh]):
                    pr = jnp.dot(rev, part, preferred_element_type=F32)
                    acc = pr if acc is None else acc + pr
                wide = jnp.concatenate([acc, jnp.zeros((Q_TILE, TOEP - WINDOW), F32)], axis=1)
                dg_ref[pl.ds(2 * hp + hh, 1), :] = _colsum(pltpu.roll(wide, 0, 1, stride=1, stride_axis=0))

    col = lambda off: pl.BlockSpec((t, 128), lambda h, i: (0, off + h))
    tile = lambda: pl.BlockSpec((STEP_ROWS, 128), lambda h, i: (i, h))
    sums = lambda: pl.BlockSpec((1, 128), lambda h, i: (0, h))
    res = _call(
        body, grid=(4, n_i),
        in_specs=[tile(), col(4), col(8), tile(), pl.BlockSpec((N_HEADS, TOEP), lambda h, i: (0, 0))],
        out_specs=[tile(), col(0), col(0), sums(), sums(), sums(), pl.BlockSpec((N_HEADS, TOEP), lambda h, i: (0, 0))],
        out_shape=[_sds((t, 512), BF16)] * 3 + [_sds((1, 512), F32)] * 3 + [_sds((N_HEADS, TOEP), F32)],
        scratch_shapes=[pltpu.VMEM((PAD_ROWS + t, 128), BF16), pltpu.VMEM((PAD_ROWS + t, 128), BF16),
                        pltpu.VMEM((PAD_ROWS + t, 128), F32), pltpu.VMEM((PAD_ROWS + t, 128), F32),
                        pltpu.VMEM((2, Q_TILE, WINDOW), F32), pltpu.VMEM((2, Q_TILE, WINDOW), F32),
                        pltpu.VMEM((4, Q_TILE, WINDOW), F32), pltpu.VMEM((4, Q_TILE, WINDOW), F32),
                        pltpu.VMEM((4, Q_TILE, WINDOW), BF16), pltpu.VMEM((4, Q_TILE, WINDOW), BF16)],
        sem=("arbitrary", "arbitrary"), name=name, args=(z, z, z, datt, gen), carried=carried)
    return res[:7] if carried is None else (res[:7], res[7:])


def _adamw_math(w, g, m, v):
    m = ADAM_B1 * m + (1.0 - ADAM_B1) * g
    v = ADAM_B2 * v + (1.0 - ADAM_B2) * (g * g)
    m_hat = m / (1.0 - ADAM_B1 ** ADAM_STEP)
    v_hat = v / (1.0 - ADAM_B2 ** ADAM_STEP)
    delta = -ADAM_LR * (m_hat / (jnp.sqrt(v_hat) + ADAM_EPS) + ADAM_WD * w)
    return delta, m, v


def _adamw_many(items, name):
    n = len(items)

    def body(*refs):
        ins, outs = refs[:4 * n], refs[4 * n:]
        for k in range(n):
            w, g, m, v = (r[...] for r in ins[4 * k:4 * k + 4])
            outs[3 * k][...], outs[3 * k + 1][...], outs[3 * k + 2][...] = _adamw_math(w, g, m, v)

    flat = [a for item in items for a in item]
    res = pl.pallas_call(body, out_shape=[_sds(item[0].shape, F32) for item in items for _ in range(3)],
                         name=name)(*flat)
    return [tuple(res[3 * k:3 * k + 3]) for k in range(n)]


def _adamw(w, g, m, v, name):
    r, c = w.shape
    tm = next(cand for cand in (256, 176, 128, 64, 32, 16, 8) if r % cand == 0)
    return _rowcall(lambda wv, gv, mv, vv: (gv,) + _adamw_math(wv, gv, mv, vv),
                    [(w, c, 0), (g, c, 0), (m, c, 0), (v, c, 0)], [], [(r, c, F32, c)] * 4, [], name=name, tm=tm)


def _ada_fwd(c_all, w_shard, b_shard, name):
    n = w_shard.shape[1]
    tn = 512

    def body(c_ref, w_ref, b_ref, o_ref, a_ref):
        cv = c_ref[...]
        act = cv * _sigmoid(cv)
        a_ref[...] = act
        o_ref[...] = jnp.dot(act.astype(BF16), w_ref[...].astype(BF16), preferred_element_type=F32) + b_ref[...]

    return pl.pallas_call(
        body, grid=(n // tn,),
        in_specs=[pl.BlockSpec((8, D), lambda j: (0, 0)), pl.BlockSpec((D, tn), lambda j: (0, j)),
                  pl.BlockSpec((1, tn), lambda j: (0, j))],
        out_specs=[pl.BlockSpec((8, tn), lambda j: (0, j)), pl.BlockSpec((8, D), lambda j: (0, 0))],
        out_shape=[_sds((8, n), F32), _sds((8, D), F32)],
        compiler_params=_params(("arbitrary",)), name=name,
    )(c_all, w_shard, b_shard)


def _ada_bwd_adamw(act_t, dmod_shard, w, m, v, name):
    r, c = w.shape
    tm = 256

    def body(a_ref, d_ref, w_ref, m_ref, v_ref, g_ref, dl_ref, nm_ref, nv_ref):
        g = jnp.dot(a_ref[...], d_ref[...], precision=lax.Precision.HIGHEST, preferred_element_type=F32)
        g_ref[...] = g
        dl_ref[...], nm_ref[...], nv_ref[...] = _adamw_math(w_ref[...], g, m_ref[...], v_ref[...])

    blk = pl.BlockSpec((tm, c), lambda i: (i, 0))
    return pl.pallas_call(
        body, grid=(r // tm,),
        in_specs=[pl.BlockSpec((tm, 8), lambda i: (i, 0)), pl.BlockSpec((8, c), lambda i: (0, 0)), blk, blk, blk],
        out_specs=[blk] * 4, out_shape=[_sds((r, c), F32)] * 4,
        compiler_params=_params(("arbitrary",)), name=name,
    )(act_t, dmod_shard, w, m, v)


def _place():
    return lax.axis_index("x"), lax.axis_index("y"), lax.axis_index("c")


def _flip(v, bit):
    return 1 - v if bit else v


VMEM_SPEC = pl.BlockSpec(memory_space=pltpu.VMEM)


def _allgather8(v, name):
    r, c = v.shape

    def body(v_ref, g_ref, tot_ref, send_sems, recv_sems, local_sem):
        x, y, cc = _place()
        sibling = (x, y, 1 - cc)
        chips = [(_flip(x, k & 2), _flip(y, k & 1)) for k in (1, 2, 3)]

        def block(px, py, pc):
            return g_ref.at[4 * px + 2 * py + pc]

        def copy(k, place, to, src=None):
            slot = block(*place)
            return pltpu.make_async_remote_copy(src_ref=slot if src is None else src, dst_ref=slot,
                                                send_sem=send_sems.at[k], recv_sem=recv_sems.at[k],
                                                device_id=to, device_id_type=MESH)

        mine = pltpu.make_async_copy(v_ref, block(x, y, cc), local_sem)
        mine.start()
        first = [copy(0, (x, y, cc), sibling, src=v_ref)]
        first += [copy(1 + j, (x, y, cc), (px, py, cc), src=v_ref) for j, (px, py) in enumerate(chips)]
        for cp in first:
            cp.start()
        passed = [copy(4 + j, (px, py, cc), sibling) for j, (px, py) in enumerate(chips)]
        for j, (px, py) in enumerate(chips):
            copy(1 + j, (px, py, cc), (x, y, cc)).wait_recv()
            passed[j].start()
        copy(0, sibling, (x, y, cc)).wait_recv()
        for j, (px, py) in enumerate(chips):
            copy(4 + j, (px, py, 1 - cc), (x, y, cc)).wait_recv()
        for cp in first + passed:
            cp.wait_send()
        mine.wait()
        tot = g_ref[0]
        for d in range(1, 8):
            tot = tot + g_ref[d]
        tot_ref[...] = tot

    return pl.pallas_call(
        body, in_specs=[VMEM_SPEC], out_specs=[VMEM_SPEC, VMEM_SPEC],
        out_shape=[_sds((8, r, c), F32), _sds((r, c), F32)],
        scratch_shapes=[pltpu.SemaphoreType.DMA((7,)), pltpu.SemaphoreType.DMA((7,)), pltpu.SemaphoreType.DMA],
        compiler_params=pltpu.CompilerParams(vmem_limit_bytes=VMEM_LIMIT), name=name,
    )(v)


def _slot(px, py, swapped):
    return 2 * py + px if swapped else 2 * px + py


def _gather_shards(arrs, swapped, name, in_place=False):
    n = len(arrs)

    def body(*refs):
        ins, outs = refs[:n], refs[n:2 * n]
        send1, recv1, send2, recv2, local_sems = refs[2 * n:]
        x, y, c = _place()
        sibling = (x, y, 1 - c)
        chips = [(_flip(x, k & 2), _flip(y, k & 1)) for k in (1, 2, 3)]
        local_copies, sends = [], []
        for a in range(n):
            h = outs[a].shape[1] // 2
            mine = pl.ds(pl.multiple_of(c * h, 8), h)
            own = _slot(x, y, swapped[a])
            if in_place:
                src = outs[a].at[own, mine]
            else:
                src = ins[a].at[mine]
                lc = pltpu.make_async_copy(ins[a], outs[a].at[own], local_sems.at[a])
                lc.start()
                local_copies.append(lc)
            for j, (px, py) in enumerate(chips):
                cp = pltpu.make_async_remote_copy(
                    src_ref=src, dst_ref=outs[a].at[own, mine], send_sem=send1.at[3 * a + j],
                    recv_sem=recv1.at[3 * a + j], device_id=(px, py, c), device_id_type=MESH)
                cp.start()
                sends.append(cp)
        for a in range(n):
            h = outs[a].shape[1] // 2
            mine = pl.ds(pl.multiple_of(c * h, 8), h)
            for j, (px, py) in enumerate(chips):
                piece = outs[a].at[_slot(px, py, swapped[a]), mine]
                pltpu.make_async_remote_copy(
                    src_ref=piece, dst_ref=piece, send_sem=send1.at[3 * a + j], recv_sem=recv1.at[3 * a + j],
                    device_id=(px, py, c), device_id_type=MESH).wait_recv()
                fwd = pltpu.make_async_remote_copy(
                    src_ref=piece, dst_ref=piece, send_sem=send2.at[3 * a + j], recv_sem=recv2.at[3 * a + j],
                    device_id=sibling, device_id_type=MESH)
                fwd.start()
                sends.append(fwd)
        for a in range(n):
            h = outs[a].shape[1] // 2
            other = pl.ds(pl.multiple_of((1 - c) * h, 8), h)
            for j, (px, py) in enumerate(chips):
                piece = outs[a].at[_slot(px, py, swapped[a]), other]
                pltpu.make_async_remote_copy(
                    src_ref=piece, dst_ref=piece, send_sem=send2.at[3 * a + j], recv_sem=recv2.at[3 * a + j],
                    device_id=sibling, device_id_type=MESH).wait_recv()
        for cp in sends:
            cp.wait_send()
        for lc in local_copies:
            lc.wait()

    dma = lambda k: pltpu.SemaphoreType.DMA((k,))
    return pl.pallas_call(
        body, in_specs=[ANY] * n, out_specs=[ANY] * n,
        out_shape=[_sds(a.shape if in_place else (4,) + a.shape, a.dtype) for a in arrs],
        scratch_shapes=[dma(3 * n), dma(3 * n), dma(3 * n), dma(3 * n), dma(n)],
        input_output_aliases={a: a for a in range(n)} if in_place else {},
        name=name,
    )(*arrs)


def _carry_pair_exchange(grads):
    n = len(grads)

    def copies(ins, outs, send_sems, recv_sems):
        x, y, c = _place()
        cps = []
        for a in range(n):
            h = ins[a].shape[1] // 2
            theirs = pl.ds(pl.multiple_of((1 - c) * h, 8), h)
            cps.append(pltpu.make_async_remote_copy(
                src_ref=ins[a].at[:, theirs, :], dst_ref=outs[a], send_sem=send_sems.at[a], recv_sem=recv_sems.at[a],
                device_id=(x, y, 1 - c), device_id_type=MESH))
        return cps

    def start(*refs):
        for cp in copies(*refs):
            cp.start()

    def finish(*refs):
        for cp in copies(*refs):
            cp.wait()

    return _Carried(grads, [_sds((4, g.shape[1] // 2, g.shape[2]), F32) for g in grads], {}, n, start, finish)


def _row_steps(h):
    return 1


def _pair_sum(grad, recv, core, name):
    _, r, c = grad.shape
    h = r // 2
    nr = _row_steps(h)
    th = h // nr

    def body(core_ref, g_ref, r_ref, o_ref):
        o_ref[...] = (g_ref[...] + r_ref[...]).astype(BF16)

    return pl.pallas_call(
        body,
        grid_spec=pltpu.PrefetchScalarGridSpec(
            num_scalar_prefetch=1, grid=(4, nr),
            in_specs=[pl.BlockSpec((None, th, c), lambda s, q, core_ref: (s, core_ref[0] * nr + q, 0)),
                      pl.BlockSpec((None, th, c), lambda s, q, core_ref: (s, q, 0))],
            out_specs=pl.BlockSpec((None, th, c), lambda s, q, core_ref: (s, q, 0))),
        out_shape=_sds((4, h, c), BF16), compiler_params=_params(("arbitrary", "arbitrary")), name=name,
    )(core, grad, recv)


def _carry_chip_exchange(parts, swapped):
    n = len(parts)

    def copies(ins, outs, send_sems, recv_sems):
        x, y, c = _place()
        chips = [(_flip(x, k & 2), _flip(y, k & 1)) for k in (1, 2, 3)]
        cps = []
        for a in range(n):
            for j, (px, py) in enumerate(chips):
                cps.append(pltpu.make_async_remote_copy(
                    src_ref=ins[a].at[_slot(px, py, swapped[a])], dst_ref=outs[a].at[j],
                    send_sem=send_sems.at[3 * a + j], recv_sem=recv_sems.at[3 * a + j],
                    device_id=(px, py, c), device_id_type=MESH))
        return cps

    def start(*refs):
        for cp in copies(*refs):
            cp.start()

    def finish(*refs):
        for cp in copies(*refs):
            cp.wait()

    return _Carried(parts, [_sds((3,) + p.shape[1:], BF16) for p in parts], {}, 3 * n, start, finish)


def _chip_sum(part, recv, slot_core, name):
    _, h, c = part.shape
    nr = _row_steps(h)
    th = h // nr

    def body(sc_ref, p_ref, r_ref, o_ref):
        acc = p_ref[...].astype(F32)
        for j in range(3):
            acc = acc + r_ref[j].astype(F32)
        o_ref[...] = acc

    return pl.pallas_call(
        body,
        grid_spec=pltpu.PrefetchScalarGridSpec(
            num_scalar_prefetch=1, grid=(nr,),
            in_specs=[pl.BlockSpec((None, th, c), lambda q, sc_ref: (sc_ref[0], q, 0)),
                      pl.BlockSpec((3, th, c), lambda q, sc_ref: (0, q, 0))],
            out_specs=pl.BlockSpec((th, c), lambda q, sc_ref: (sc_ref[1] * nr + q, 0))),
        out_shape=_sds((2 * h, c), F32), compiler_params=_params(("arbitrary",)), name=name,
    )(slot_core, part, recv)


def _carry_pair_share(shards):
    n = len(shards)

    def copies(outs, send_sems, recv_sems, mine):
        x, y, c = _place()
        cps = []
        for a in range(n):
            h = outs[a].shape[0] // 2
            half = outs[a].at[pl.ds(pl.multiple_of((c if mine else 1 - c) * h, 8), h)]
            cps.append(pltpu.make_async_remote_copy(
                src_ref=half, dst_ref=half, send_sem=send_sems.at[a], recv_sem=recv_sems.at[a],
                device_id=(x, y, 1 - c), device_id_type=MESH))
        return cps

    def start(ins, outs, send_sems, recv_sems):
        for cp in copies(outs, send_sems, recv_sems, True):
            cp.start()

    def finish(ins, outs, send_sems, recv_sems):
        for cp in copies(outs, send_sems, recv_sems, False):
            cp.wait_recv()
        for cp in copies(outs, send_sems, recv_sems, True):
            cp.wait_send()

    return _Carried(shards, [_sds(s.shape, F32) for s in shards], {a: a for a in range(n)}, n, start, finish)


def _carry_gather_ici(bufs, swapped):
    n = len(bufs)

    def copies(outs, send_sems, recv_sems, sending):
        x, y, c = _place()
        cps = []
        for a in range(n):
            h = outs[a].shape[1] // 2
            mine = pl.ds(pl.multiple_of(c * h, 8), h)
            for j, k in enumerate((1, 2, 3)):
                px, py = _flip(x, k & 2), _flip(y, k & 1)
                slot = _slot(x, y, swapped[a]) if sending else _slot(px, py, swapped[a])
                piece = outs[a].at[slot, mine]
                cps.append(pltpu.make_async_remote_copy(
                    src_ref=piece, dst_ref=piece, send_sem=send_sems.at[3 * a + j], recv_sem=recv_sems.at[3 * a + j],
                    device_id=(px, py, c), device_id_type=MESH))
        return cps

    def start(ins, outs, send_sems, recv_sems):
        for cp in copies(outs, send_sems, recv_sems, True):
            cp.start()

    def finish(ins, outs, send_sems, recv_sems):
        for cp in copies(outs, send_sems, recv_sems, False):
            cp.wait_recv()
        for cp in copies(outs, send_sems, recv_sems, True):
            cp.wait_send()

    return _Carried(bufs, [_sds(b.shape, b.dtype) for b in bufs], {a: a for a in range(n)}, 3 * n, start, finish)


HBM_SPEC = pl.BlockSpec(memory_space=pltpu.HBM)
SEM_SPEC = pl.BlockSpec(memory_space=pltpu.SEMAPHORE)
SIDE_EFFECT = pltpu.SideEffectType.DATAFLOW_SIDE_EFFECTING


def _ici_pieces(buf, send_sems, recv_sems, swapped, sending):
    x, y, c = _place()
    h = buf.shape[1] // 2
    mine = pl.ds(pl.multiple_of(c * h, 8), h)
    cps = []
    for j, k in enumerate((1, 2, 3)):
        px, py = _flip(x, k & 2), _flip(y, k & 1)
        piece = buf.at[_slot(x, y, swapped) if sending else _slot(px, py, swapped), mine]
        cps.append(pltpu.make_async_remote_copy(src_ref=piece, dst_ref=piece, send_sem=send_sems.at[j],
                                                recv_sem=recv_sems.at[j], device_id=(px, py, c), device_id_type=MESH))
    return cps


def _gather_ici_start(buf, after, swapped, name):
    def body(buf_ref, after_ref, send_sems, recv_sems, thru, token):
        for cp in _ici_pieces(thru, send_sems, recv_sems, swapped, True):
            cp.start()
        token[...] = jnp.zeros_like(token)

    return pl.pallas_call(
        body, name=name,
        out_shape=(pltpu.SemaphoreType.DMA((3,)), pltpu.SemaphoreType.DMA((3,)), pltpu.HBM(buf.shape, buf.dtype),
                   jax.ShapeDtypeStruct((8, 128), F32)),
        in_specs=(HBM_SPEC, ANY), out_specs=(SEM_SPEC, SEM_SPEC, HBM_SPEC, VMEM_SPEC), input_output_aliases={0: 2},
        compiler_params=pltpu.CompilerParams(has_side_effects=SIDE_EFFECT),
    )(pltpu.with_memory_space_constraint(buf, pltpu.HBM), after)


def _gather_ici_wait(send_sems, recv_sems, thru, after, swapped, name):
    def body(thru_ref, send_sems, recv_sems, after_ref, out_ref):
        for cp in _ici_pieces(out_ref, send_sems, recv_sems, swapped, True):
            cp.wait_send()
        for cp in _ici_pieces(out_ref, send_sems, recv_sems, swapped, False):
            cp.wait_recv()

    return pl.pallas_call(
        body, name=name, out_shape=pltpu.HBM(thru.shape, thru.dtype),
        in_specs=(HBM_SPEC, SEM_SPEC, SEM_SPEC, ANY), out_specs=HBM_SPEC, input_output_aliases={0: 0},
        compiler_params=pltpu.CompilerParams(has_side_effects=SIDE_EFFECT),
    )(thru, send_sems, recv_sems, after)


def _carry_gather_forward(bufs, swapped):
    n = len(bufs)

    def copies(outs, send_sems, recv_sems, sending):
        x, y, c = _place()
        cps = []
        for a in range(n):
            h = outs[a].shape[1] // 2
            rows = pl.ds(pl.multiple_of((c if sending else 1 - c) * h, 8), h)
            for j, k in enumerate((1, 2, 3)):
                piece = outs[a].at[_slot(_flip(x, k & 2), _flip(y, k & 1), swapped[a]), rows]
                cps.append(pltpu.make_async_remote_copy(
                    src_ref=piece, dst_ref=piece, send_sem=send_sems.at[3 * a + j], recv_sem=recv_sems.at[3 * a + j],
                    device_id=(x, y, 1 - c), device_id_type=MESH))
        return cps

    def start(ins, outs, send_sems, recv_sems):
        for cp in copies(outs, send_sems, recv_sems, True):
            cp.start()

    def finish(ins, outs, send_sems, recv_sems):
        for cp in copies(outs, send_sems, recv_sems, False):
            cp.wait_recv()
        for cp in copies(outs, send_sems, recv_sems, True):
            cp.wait_send()

    return _Carried(bufs, [_sds(b.shape, b.dtype) for b in bufs], {a: a for a in range(n)}, 3 * n, start, finish)


def _pack(arrs, rows_multiple=8):
    parts, offs, row = [], [], 0
    for a in arrs:
        flat = a.reshape(-1)
        nrow = -(-flat.shape[0] // D)
        parts.append(jnp.pad(flat, (0, nrow * D - flat.shape[0])))
        offs.append(row)
        row += nrow
    total = -(-row // rows_multiple) * rows_multiple
    if total > row:
        parts.append(jnp.zeros(((total - row) * D,), F32))
    return jnp.concatenate(parts).reshape(total, D), offs


def _unpack(packed, offs, shapes):
    out = []
    for off, shp in zip(offs, shapes):
        size = int(np.prod(shp))
        nrow = -(-size // D)
        out.append(packed[off:off + nrow].reshape(-1)[:size].reshape(shp))
    return out


def _to_bf16_slot(w, slot, name):
    r, c = w.shape
    tm = next(cand for cand in (256, 176, 128, 64, 32, 16) if r % cand == 0)

    def body(slot_ref, w_ref, o_ref):
        o_ref[...] = w_ref[...].astype(BF16)

    return pl.pallas_call(
        body,
        grid_spec=pltpu.PrefetchScalarGridSpec(
            num_scalar_prefetch=1, grid=(r // tm,),
            in_specs=[pl.BlockSpec((tm, c), lambda i, slot_ref: (i, 0))],
            out_specs=pl.BlockSpec((None, tm, c), lambda i, slot_ref: (slot_ref[0], i, 0))),
        out_shape=_sds((4, r, c), BF16), compiler_params=_params(("arbitrary",)), name=name,
    )(slot, w)


def _unshard_cols(g):
    s, k, n = g.shape
    return jnp.transpose(g, (1, 0, 2)).reshape(k, s * n)


def _ff_swap(v):
    b = FF_BLOCK
    return jnp.concatenate([v[..., 0:b], v[..., 2 * b:3 * b], v[..., b:2 * b], v[..., 3 * b:4 * b]], axis=-1)


LATE = ("attn_o", "conv_o", "mix_o", "up", "down")
EARLY_GRADS = ("down", "up", "mix_o", "attn_o", "conv_o")


def _weight_views(bufs):
    return {"up": bufs["up"], "attn_o": _unshard_cols(bufs["attn_o"]), "conv_o": _unshard_cols(bufs["conv_o"]),
            "mix_o": bufs["mix_o"].reshape(D, D), "down": bufs["down"].reshape(D_FF, D)}


def _pair_sums(names, grads, recv, dist):
    return [_pair_sum(g, r, dist["core"], "pair_sum_" + n) for n, g, r in zip(names, grads, recv)]


def _reduce_halves(names, parts, from_chips, dist):
    return [_chip_sum(p, r, jnp.concatenate([dist["slots"][SWAPPED[n]], dist["core"]]), "chip_sum_" + n)
            for n, p, r in zip(names, parts, from_chips)]


FUSED_TILE = 256
WIDE_TILE = 512


def _gates(z):
    return [(z, 512, 5), (z, 512, 6), (z, 512, 7), (z, 512, 8)]


def _mix_out(a, cb, z, x, w_mix_o, g_post, gt, g_pre2, sc2, sh2, name):
    def lhs(av, cv, ga0, ga1, gb0, gb1):
        ga, gb = jnp.concatenate([ga0, ga1], axis=1), jnp.concatenate([gb0, gb1], axis=1)
        return _sigmoid(ga) * av + _sigmoid(gb) * cv

    def fn(ym, y, xv, gv, gtv, g2v, scv, shv):
        yn, _ = _rms_stats(ym)
        x1 = xv + gtv * (yn * gv)
        xn, _ = _rms_stats(x1)
        return ym, y, x1, (xn * g2v) * (1.0 + scv) + shv

    return _matmul_rows(w_mix_o, form="nn", tm=min(FUSED_TILE, x.shape[0]), tk=D, fn=fn, a_rows=[(a, D, 0), (cb, D, 0)] + _gates(z),
                        a_fn=lhs, rows=[(x, D, 0)], consts=[g_post, gt, g_pre2, sc2, sh2],
                        row_outs=[(F32, D), (BF16, D), (F32, D), (BF16, D)], acc_outs=[], name=name)


def _down_tail(act, w_down, x1, target, g, gt, name):
    def fn(yv, xv, tv, gv, gtv):
        yn, r = _rms_stats(yv)
        e = xv + gtv * (yn * gv) - tv
        dx2 = e * (1.0 / D)
        dyn = dx2 * gtv
        return (dx2, _rms_bwd(dyn * gv, yn, r), _colsum(e * e) * (0.5 / D), _colsum(dyn * yn),
                _colsum(dx2 * (yn * gv)))

    return _matmul_rows(w_down, form="nn", a=act, tm=min(WIDE_TILE, x1.shape[0]), tk=D_FF, fn=fn,
                        rows=[(x1, D, 0), (target, D, 0)], consts=[g, gt], row_outs=[(F32, D), (BF16, D)],
                        acc_outs=[(1, D)] * 3, name=name)


def _up_dx_tail(dup, w_up, x1, dx2, ym, g_pre2, sc2, g_post, gt, name):
    def fn(dh, xv, dov, ymv, g2v, scv, gv, gtv):
        xn, r = _rms_stats(xv)
        dyn = dh * (1.0 + scv)
        dx1 = dov + _rms_bwd(dyn * g2v, xn, r)
        yn, r2 = _rms_stats(ymv)
        dynm = dx1 * gtv
        return (dx1, _rms_bwd(dynm * gv, yn, r2), _colsum(dh), _colsum(dh * (xn * g2v)), _colsum(dyn * xn),
                _colsum(dynm * yn), _colsum(dx1 * (yn * gv)))

    return _matmul_rows(w_up, form="nt", a=dup, tm=min(FUSED_TILE, x1.shape[0]), tk=2 * D_FF, fn=fn,
                        rows=[(x1, D, 0), (dx2, D, 0), (ym, D, 0)], consts=[g_pre2, sc2, g_post, gt],
                        row_outs=[(F32, D), (BF16, D)], acc_outs=[(1, D)] * 5, name=name)


def _mix_dx_gates(dym, w_mix_o, a, cb, z, name):
    def fn(dy, av, cv, ga0, ga1, gb0, gb1):
        sa = _sigmoid(jnp.concatenate([ga0, ga1], axis=1))
        sb = _sigmoid(jnp.concatenate([gb0, gb1], axis=1))
        dcb = dy * sb
        dga = dy * av * (sa * (1.0 - sa))
        dgb = dy * cv * (sb * (1.0 - sb))
        return dy * sa, dcb, dga, dgb, _colsum(dcb), _colsum(dga), _colsum(dgb)

    return _matmul_rows(w_mix_o, form="nt", a=dym, tm=min(FUSED_TILE, a.shape[0]), tk=D, fn=fn,
                        rows=[(a, D, 0), (cb, D, 0)] + _gates(z), consts=[], row_outs=[(BF16, D)] * 4,
                        acc_outs=[(1, D)] * 3, name=name)


def _local_step(x, target, mod, w_in, late, small, dist=None):
    sh_m, sc_m, gt_m, sh_f, sc_f, gt_f = mod
    t = x.shape[0]
    tmm = min(1024, t)
    late_swapped = [SWAPPED[n] for n in LATE]

    h1 = _pre_norm(x, small["g_pre_mix"], sc_m, sh_m, "pre_norm_mix")
    if callable(w_in):
        w_in = w_in(h1)
    z = _matmul(h1, w_in, form="nn", out_dtype=F32, tm=min(FUSED_TILE, t), tn=D_IN, tk=D, bias=small["b_in"], name="mm_in")
    if dist is None:
        att = _attention(z, small["gen"], "attention")
        bufs = dict(late)
    else:
        mid = [n for n in LATE if n != "down"]
        mid_swapped = [SWAPPED[n] for n in mid]
        att, landed = _attention(z, small["gen"], "attention",
                                 carried=_carry_gather_ici([late[n] for n in mid], mid_swapped))
        bufs = dict(zip(mid, _run_carried(_carry_gather_forward(landed, mid_swapped), "gather_forward")))
        bufs["down"] = late["down"]
    w = _weight_views(bufs)
    w["in"] = w_in
    a = _matmul(att, w["attn_o"], form="nn", out_dtype=F32, tm=tmm, tn=512, tk=512, name="mm_attn_o")
    u1, u3 = _conv_branch(z, small["w_dw_conv"], small["b_dw_conv"], small["g_conv_ln"], small["b_conv_ln"], "conv_branch")
    cb = _matmul(u3, w["conv_o"], form="nn", out_dtype=F32, tm=tmm, tn=512, tk=512, bias=small["b_conv_o"], name="mm_conv_o")
    ym, y, x1, h2 = _mix_out(a, cb, z, x, w["mix_o"], small["g_post_mix"], gt_m, small["g_pre_ffn"], sc_f, sh_f, "mix_out")
    mm_up = dict(form="nn", out_dtype=F32, tm=min(FUSED_TILE, t), tn=2 * D_FF, tk=D, name="mm_up")
    if dist is None:
        up = _matmul(h2, w["up"], **mm_up)
    else:
        up, landed = _matmul(h2, w["up"], carried=_carry_gather_ici([late["down"]], [False]), **mm_up)
        w["down"] = _run_carried(_carry_gather_forward(landed, [False]), "gather_forward_down")[0].reshape(D_FF, D)
    act = _ffn_act(up, small["w_dw_ffn"], small["b_dw_ffn"], "ffn_act")

    dx2, dyf, loss_cols, d_g_post_ffn, d_gt_f = _down_tail(act, w["down"], x1, target, small["g_post_ffn"], gt_f, "down_tail")
    dact = _matmul(dyf, w["down"], form="nt", out_dtype=F32, tm=tmm, tn=FF_BLOCK, tk=D, name="mm_down_dx")
    g_down = _matmul(act, dyf, form="tn", out_dtype=F32, tm=FF_BLOCK, tn=512, tk=t, name="mm_down_dw")
    dup, d_w_dw_ffn, d_b_dw_ffn = _ffn_act_bwd(dact, up, small["w_dw_ffn"], small["b_dw_ffn"], "ffn_act_bwd")
    dx1, dym, d_sh_f, d_sc_f, d_g_pre_ffn, d_g_post_mix, d_gt_m = _up_dx_tail(
        dup, w["up"], x1, dx2, ym, small["g_pre_ffn"], sc_f, small["g_post_mix"], gt_m, "up_dx_tail")
    g_up = _matmul(h2, dup, form="tn", out_dtype=F32, tm=512, tn=FF_BLOCK, tk=t, out_sharded=True, name="mm_up_dw")
    da, dcb, dgate_a, dgate_b, d_b_conv_o, sga, sgb = _mix_dx_gates(dym, w["mix_o"], a, cb, z, "mix_dx_gates")
    g_mix_o = _matmul(y, dym, form="tn", out_dtype=F32, tm=D, tn=512, tk=t, name="mm_mix_o_dw")
    datt = _matmul(da, w["attn_o"], form="nt", out_dtype=F32, tm=tmm, tn=512, tk=D, name="mm_attn_o_dx")
    g_attn_o = _matmul(att, da, form="tn", out_dtype=F32, tm=512, tn=256, tk=t, out_sharded=True, name="mm_attn_o_dw")
    du3 = _matmul(dcb, w["conv_o"], form="nt", out_dtype=F32, tm=tmm, tn=512, tk=D, name="mm_conv_o_dx")
    g_conv_o = _matmul(u3, dcb, form="tn", out_dtype=F32, tm=512, tn=256, tk=t, out_sharded=True, name="mm_conv_o_dw")
    big = {"attn_o": g_attn_o, "conv_o": g_conv_o, "mix_o": g_mix_o.reshape(4, 256, D),
           "up": g_up, "down": g_down.reshape(4, D_FF // 4, D)}
    conv_bwd = (du3, u1, z, small["w_dw_conv"], small["g_conv_ln"], small["b_conv_ln"], "conv_branch_bwd")
    in_dw = dict(form="tn", out_dtype=F32, tm=512, tn=1152, tk=t, out_sharded=True, name="mm_in_dw")
    in_dx = dict(form="nt", out_dtype=F32, tm=min(WIDE_TILE, t), tn=D, tk=D_IN, name="mm_in_dx")
    if dist is None:
        dglu, d_w_dw_conv, d_b_dw_conv, d_g_conv_ln, d_b_conv_ln, sglu = _conv_branch_bwd(*conv_bwd)
        dq, dk, dv, sq, sk, sv, dgen = _attention_bwd(z, datt, small["gen"], "attention_bwd")
        dz = jnp.concatenate([dq, dk, dv, dglu, dgate_a, dgate_b], axis=1)
        big["in"] = _matmul(h1, dz, **in_dw)
        dh1 = _matmul(dz, w_in, **in_dx)
    else:
        early = [big[n] for n in EARLY_GRADS]
        (dglu, d_w_dw_conv, d_b_dw_conv, d_g_conv_ln, d_b_conv_ln, sglu), recv = _conv_branch_bwd(
            *conv_bwd, carried=_carry_pair_exchange(early))
        parts = _pair_sums(EARLY_GRADS, early, recv, dist)
        (dq, dk, dv, sq, sk, sv, dgen), from_chips = _attention_bwd(
            z, datt, small["gen"], "attention_bwd",
            carried=_carry_chip_exchange(parts, [SWAPPED[n] for n in EARLY_GRADS]))
        halves = _reduce_halves(EARLY_GRADS, parts, from_chips, dist)
        dz = jnp.concatenate([dq, dk, dv, dglu, dgate_a, dgate_b], axis=1)
        g_in, shards = _matmul(h1, dz, carried=_carry_pair_share(halves), **in_dw)
        big = dict(zip(EARLY_GRADS, shards))
        recv_in = _run_carried(_carry_pair_exchange([g_in]), "pair_exchange_in")
        part_in = _pair_sums(("in",), [g_in], recv_in, dist)
        dh1, from_chips_in = _matmul(dz, w_in, carried=_carry_chip_exchange(part_in, [False]), **in_dx)
        half_in = _reduce_halves(("in",), part_in, from_chips_in, dist)
        big["in"] = _run_carried(_carry_pair_share(half_in), "pair_share_in")[0]
    d_b_in = jnp.concatenate([sq, sk, sv, sglu, sga, sgb], axis=1)
    grad_x, d_sh_m, d_sc_m, d_g_pre_mix = _pre_norm_bwd(dh1, x, dx1, small["g_pre_mix"], sc_m, "pre_norm_mix_bwd")

    dmod = [d_sh_m, d_sc_m, d_gt_m, d_sh_f, d_sc_f, d_gt_f]
    sm = {"g_pre_mix": d_g_pre_mix, "g_post_mix": d_g_post_mix, "b_in": d_b_in, "gen": dgen,
          "w_dw_conv": d_w_dw_conv, "b_dw_conv": d_b_dw_conv, "g_conv_ln": d_g_conv_ln, "b_conv_ln": d_b_conv_ln,
          "b_conv_o": d_b_conv_o, "g_pre_ffn": d_g_pre_ffn, "g_post_ffn": d_g_post_ffn,
          "w_dw_ffn": d_w_dw_ffn, "b_dw_ffn": d_b_dw_ffn}
    return loss_cols, grad_x, dmod, big, sm


BIG = ("in", "attn_o", "conv_o", "mix_o", "up", "down")
SWAPPED = {"in": False, "attn_o": False, "conv_o": False, "mix_o": False, "up": True, "down": False}
SMALL_ORDER = ("b_ada", "g_pre_mix", "g_post_mix", "b_in", "rel_bias", "b_dw_conv", "g_conv_ln", "b_conv_ln",
               "b_conv_o", "g_pre_ffn", "g_post_ffn", "b_dw_ffn", "w_dw_conv", "w_dw_ffn")


def kernel(x, c, w_ada, b_ada, g_pre_mix, g_post_mix, w_in, b_in, rel_bias, w_attn_o, w_dw_conv, b_dw_conv, g_conv_ln, b_conv_ln, w_conv_o, b_conv_o, w_mix_o, g_pre_ffn, g_post_ffn, w_up, w_dw_ffn, b_dw_ffn, w_down, loss_target, m_w_ada, m_b_ada, m_g_pre_mix, m_g_post_mix, m_w_in, m_b_in, m_rel_bias, m_w_attn_o, m_w_dw_conv, m_b_dw_conv, m_g_conv_ln, m_b_conv_ln, m_w_conv_o, m_b_conv_o, m_w_mix_o, m_g_pre_ffn, m_g_post_ffn, m_w_up, m_w_dw_ffn, m_b_dw_ffn, m_w_down, v_w_ada, v_b_ada, v_g_pre_mix, v_g_post_mix, v_w_in, v_b_in, v_rel_bias, v_w_attn_o, v_w_dw_conv, v_b_dw_conv, v_g_conv_ln, v_b_conv_ln, v_w_conv_o, v_b_conv_o, v_w_mix_o, v_g_pre_ffn, v_g_post_ffn, v_w_up, v_w_dw_ffn, v_b_dw_ffn, v_w_down):
    given = dict(locals())
    ax, ay, ac = lax.axis_index("x"), lax.axis_index("y"), lax.axis_index("c")
    shard = 2 * ax + ay
    me = 4 * ax + 2 * ay + ac
    xs, target = x[0], loss_target[0]

    slots = {sw: _slot(ax, ay, sw).astype(jnp.int32).reshape(1) for sw in (False, True)}
    own = {n: _to_bf16_slot(given["w_" + n][0], slots[SWAPPED[n]], "cast_" + n) for n in BIG}

    c_pad = jnp.pad(c, ((0, 7), (0, 0)))
    c_g, _ = _allgather8(c_pad, "gather_c")
    c_all = c_g[:, 0, :]
    b_ada_shard = lax.dynamic_slice(b_ada, (0, shard * 1536), (1, 1536))
    mod_shard, c_act = _ada_fwd(c_all, w_ada[0], b_ada_shard, "ada_fwd")
    small_in = [jnp.pad(mod_shard, ((0, 8), (0, 0))),
                jnp.pad(w_dw_conv[0], ((0, 1), (0, 0))),
                jnp.pad(w_dw_ffn[0], ((0, 13), (0, 0)))]
    mod_g, wdc_g, wdf_g = _gather_shards(small_in, [False, False, True], "gather_small")
    mod_all = jnp.transpose(mod_g[:, :8, :], (1, 0, 2)).reshape(8, 6 * D)
    in_send, in_recv, in_flight, token = _gather_ici_start(own["in"], mod_g, False, "gather_w_in_start")

    def w_in_ready(after):
        landed = _gather_ici_wait(in_send, in_recv, in_flight, after, False, "gather_w_in_wait")
        return _run_carried(_carry_gather_forward([landed], [False]), "gather_forward_in")[0]

    mod_row = lax.dynamic_slice(mod_all, (me, 0), (1, 6 * D)) + token[0:1, 0:1]
    mod = [mod_row[:, k * D:(k + 1) * D] for k in range(6)]

    core = ac.astype(jnp.int32).reshape(1)
    dist = {"core": core, "slots": slots}

    sel = jnp.asarray(_toeplitz_map())
    rel_pad = jnp.pad(rel_bias[0], ((0, 0), (0, REL_PAD - (2 * MAX_REL + 1))))
    gen = _select_call(rel_pad, sel.T.astype(BF16), "bias_rows")
    small = {"g_pre_mix": g_pre_mix, "g_post_mix": g_post_mix, "b_in": b_in, "gen": gen,
             "w_dw_conv": _unshard_cols(wdc_g[:, :CONV_K, :]), "b_dw_conv": b_dw_conv, "g_conv_ln": g_conv_ln,
             "b_conv_ln": b_conv_ln, "b_conv_o": b_conv_o, "g_pre_ffn": g_pre_ffn, "g_post_ffn": g_post_ffn,
             "w_dw_ffn": _unshard_cols(wdf_g[:, :FFN_K, :]), "b_dw_ffn": _ff_swap(b_dw_ffn)}

    loss_cols, grad_x, dmod, reduced, sm = _local_step(xs, target, mod, w_in_ready, {n: own[n] for n in LATE}, small, dist)

    d_rel = _select_call(sm["gen"], sel.astype(BF16), "bias_fold")[:, :2 * MAX_REL + 1]
    small_grads = {"g_pre_mix": sm["g_pre_mix"], "g_post_mix": sm["g_post_mix"], "b_in": sm["b_in"], "rel_bias": d_rel[None],
                   "b_dw_conv": sm["b_dw_conv"], "g_conv_ln": sm["g_conv_ln"], "b_conv_ln": sm["b_conv_ln"],
                   "b_conv_o": sm["b_conv_o"], "g_pre_ffn": sm["g_pre_ffn"], "g_post_ffn": sm["g_post_ffn"],
                   "b_dw_ffn": _ff_swap(sm["b_dw_ffn"]), "w_dw_conv": sm["w_dw_conv"], "w_dw_ffn": _ff_swap(sm["w_dw_ffn"])}
    order = [n for n in SMALL_ORDER if n != "b_ada"]
    packed, offs = _pack([jnp.concatenate(dmod, axis=1)] + [small_grads[n] for n in order] + [loss_cols])
    every, total = _allgather8(packed, "gather_small_grads")
    loss = jnp.sum(total[offs[-1]])
    offs = offs[:-1]
    dmod_all = every[:, 0:6, :].reshape(8, 6 * D)
    full_shapes = {n: given[n].shape for n in order}
    full_shapes["w_dw_conv"], full_shapes["w_dw_ffn"] = (1, CONV_K, 512), (1, FFN_K, 2 * D_FF)
    sums = dict(zip(order, _unpack(total, offs[1:], [full_shapes[n] for n in order])))
    sums["b_ada"] = total[0:6].reshape(1, 6 * D)
    sums["w_dw_conv"] = lax.dynamic_slice(sums["w_dw_conv"], (0, 0, shard * 128), (1, CONV_K, 128))
    sums["w_dw_ffn"] = lax.dynamic_slice(sums["w_dw_ffn"], (0, 0, shard * FF_BLOCK), (1, FFN_K, FF_BLOCK))

    upd = dict(zip(SMALL_ORDER, _adamw_many(
        [(given[n], sums[n], given["m_" + n], given["v_" + n]) for n in SMALL_ORDER], "adamw_small")))

    dmod_shard = lax.dynamic_slice(dmod_all, (0, shard * 1536), (8, 1536))
    ada = _ada_bwd_adamw(c_act.T, dmod_shard, w_ada[0], m_w_ada[0], v_w_ada[0], "ada_bwd_adamw")

    out = {"grad_w_ada": ada[0][None], "delta_w_ada": ada[1][None], "new_m_w_ada": ada[2][None], "new_v_w_ada": ada[3][None]}
    for n in BIG:
        g = reduced[n]
        g, dl, nm, nv = _adamw(given["w_" + n][0], g, given["m_w_" + n][0], given["v_w_" + n][0], "adamw_" + n)
        out["grad_w_" + n], out["delta_w_" + n], out["new_m_w_" + n], out["new_v_w_" + n] = g[None], dl[None], nm[None], nv[None]
    for n in SMALL_ORDER:
        out["grad_" + n], out["delta_" + n], out["new_m_" + n], out["new_v_" + n] = sums[n], *upd[n]

    weights = ["w_ada", "b_ada", "g_pre_mix", "g_post_mix", "w_in", "b_in", "rel_bias", "w_attn_o", "w_dw_conv", "b_dw_conv",
               "g_conv_ln", "b_conv_ln", "w_conv_o", "b_conv_o", "w_mix_o", "g_pre_ffn", "g_post_ffn", "w_up", "w_dw_ffn",
               "b_dw_ffn", "w_down"]
    return (loss, grad_x[None], *[out["grad_" + n] for n in weights], *[out["delta_" + n] for n in weights],
            *[out["new_m_" + n] for n in weights], *[out["new_v_" + n] for n in weights])
```

```python
import functools
import math

import numpy as np
import jax
import jax.numpy as jnp
from jax import lax
from jax.experimental import pallas as pl
from jax.experimental.pallas import tpu as pltpu

F32, BF16 = jnp.float32, jnp.bfloat16
MESH = pl.DeviceIdType.MESH

D = 1024
D_IN = 4608
D_FF = 2816
CONV_K = 31
FFN_K = 3
N_HEADS = 8
CHUNK = 64
LEFT_CHUNKS = 8
MAX_REL = 128
EPS = 1e-6
NEG_INF = -1e30
Q_TILE = 256
WINDOW = Q_TILE + LEFT_CHUNKS * CHUNK
STEP_ROWS = 256
REL_PAD = 384
TOEP = 1024
ROW_TILE = 256
VMEM_LIMIT = 60 * 1024 * 1024

ADAM_LR, ADAM_B1, ADAM_B2, ADAM_EPS, ADAM_WD, ADAM_STEP = 0.001, 0.9, 0.999, 1e-08, 0.01, 10


def _params(sem=None):
    return pltpu.CompilerParams(dimension_semantics=sem, vmem_limit_bytes=VMEM_LIMIT)


def _sds(shape, dtype):
    return jax.ShapeDtypeStruct(tuple(shape), dtype)


ANY = pl.BlockSpec(memory_space=pl.ANY)


class _Carried:
    def __init__(self, ins, out_shapes, aliases, n_sems, start, finish):
        self.ins, self.out_shapes, self.aliases = list(ins), list(out_shapes), dict(aliases)
        self.n_sems, self.start, self.finish = n_sems, start, finish


def _call(body, *, grid, in_specs, out_specs, out_shape, scratch_shapes, sem, name, args, carried=None):
    in_specs, out_specs, out_shape = list(in_specs), list(out_specs), list(out_shape)
    scratch_shapes = list(scratch_shapes)
    if carried is None:
        return pl.pallas_call(body, grid=grid, in_specs=in_specs, out_specs=out_specs, out_shape=out_shape,
                              scratch_shapes=scratch_shapes, compiler_params=_params(sem), name=name)(*args)
    n_in, n_out, n_scr = len(in_specs), len(out_specs), len(scratch_shapes)
    c_in, c_out = len(carried.ins), len(carried.out_shapes)

    def full(*refs):
        pos = [0]

        def take(k):
            part = refs[pos[0]:pos[0] + k]
            pos[0] += k
            return part

        ins, cins, outs, couts, scr = take(n_in), take(c_in), take(n_out), take(c_out), take(n_scr)
        send_sems, recv_sems = take(2)
        first = last = None
        for d, size in enumerate(grid):
            pid = pl.program_id(d)
            first = (pid == 0) if first is None else first & (pid == 0)
            last = (pid == size - 1) if last is None else last & (pid == size - 1)

        @pl.when(first)
        def _():
            carried.start(cins, couts, send_sems, recv_sems)

        body(*ins, *outs, *scr)

        @pl.when(last)
        def _():
            carried.finish(cins, couts, send_sems, recv_sems)

    sems = [pltpu.SemaphoreType.DMA((carried.n_sems,)), pltpu.SemaphoreType.DMA((carried.n_sems,))]
    return pl.pallas_call(
        full, grid=grid, in_specs=in_specs + [ANY] * c_in, out_specs=out_specs + [ANY] * c_out,
        out_shape=out_shape + carried.out_shapes, scratch_shapes=scratch_shapes + sems,
        input_output_aliases={n_in + k: n_out + v for k, v in carried.aliases.items()},
        compiler_params=_params(tuple("arbitrary" for _ in grid)), name=name,
    )(*args, *carried.ins)


def _run_carried(carried, name):
    c_in = len(carried.ins)

    def body(*refs):
        cins, couts = refs[:c_in], refs[c_in:c_in + len(carried.out_shapes)]
        send_sems, recv_sems = refs[-2:]
        carried.start(cins, couts, send_sems, recv_sems)
        carried.finish(cins, couts, send_sems, recv_sems)

    return pl.pallas_call(
        body, in_specs=[ANY] * c_in, out_specs=[ANY] * len(carried.out_shapes), out_shape=carried.out_shapes,
        scratch_shapes=[pltpu.SemaphoreType.DMA((carried.n_sems,)), pltpu.SemaphoreType.DMA((carried.n_sems,))],
        input_output_aliases=carried.aliases, name=name,
    )(*carried.ins)


def _matmul(a, b, *, form, out_dtype, tm, tn, tk, name, bias=None, add=None, out_sharded=False, carried=None):
    b3 = b.ndim == 3
    resident = 0
    if form == "nn":
        m, k = a.shape
        n = b.shape[0] * b.shape[2] if b3 else b.shape[1]
        dn = (((1,), (0,)), ((), ()))
        a_spec = pl.BlockSpec((tm, tk), lambda i, j, kk: (i, kk))
        if b3 and tn == n and tk == k:
            resident = b.shape[0]
            b_spec = pl.BlockSpec(b.shape, lambda i, j, kk: (0, 0, 0))
        else:
            b_spec = (pl.BlockSpec((None, tk, tn), lambda i, j, kk: (j, kk, 0)) if b3
                      else pl.BlockSpec((tk, tn), lambda i, j, kk: (kk, j)))
    elif form == "nt":
        m, k = a.shape
        n = b.shape[1] if b3 else b.shape[0]
        dn = (((1,), (1,)), ((), ()))
        a_spec = pl.BlockSpec((tm, tk), lambda i, j, kk: (i, kk))
        if b3 and tk == k:
            resident = b.shape[0]
            b_spec = pl.BlockSpec((resident, tn, b.shape[2]), lambda i, j, kk: (0, j, 0))
        else:
            b_spec = (pl.BlockSpec((None, tn, tk), lambda i, j, kk: (kk, j, 0)) if b3
                      else pl.BlockSpec((tn, tk), lambda i, j, kk: (j, kk)))
    else:
        k, m = a.shape
        n = b.shape[1]
        dn = (((0,), (0,)), ((), ()))
        a_spec = pl.BlockSpec((tk, tm), lambda i, j, kk: (kk, i))
        b_spec = pl.BlockSpec((tk, tn), lambda i, j, kk: (kk, j))
    assert m % tm == 0 and n % tn == 0 and k % tk == 0, (name, m, n, k, tm, tn, tk)
    nk = k // tk
    in_specs, args = [a_spec, b_spec], [a, b]
    if bias is not None:
        in_specs.append(pl.BlockSpec((1, tn), lambda i, j, kk: (0, j)))
        args.append(bias)
    if add is not None:
        in_specs.append(pl.BlockSpec((tm, tn), lambda i, j, kk: (i, j)))
        args.append(add)
    if out_sharded:
        out_shape = _sds((n // tn, m, tn), out_dtype)
        out_spec = pl.BlockSpec((None, tm, tn), lambda i, j, kk: (j, i, 0))
    else:
        out_shape = _sds((m, n), out_dtype)
        out_spec = pl.BlockSpec((tm, tn), lambda i, j, kk: (i, j))

    def body(*refs):
        a_ref, b_ref = refs[0], refs[1]
        pos = 2
        bias_ref = add_ref = None
        if bias is not None:
            bias_ref, pos = refs[pos], pos + 1
        if add is not None:
            add_ref, pos = refs[pos], pos + 1
        o_ref = refs[pos]
        if resident and form == "nn":
            ns = b_ref.shape[2]
            for s in range(resident):
                cols = slice(s * ns, (s + 1) * ns)
                ps = lax.dot_general(a_ref[...], b_ref[s], dn, preferred_element_type=F32)
                if bias_ref is not None:
                    ps = ps + bias_ref[:, cols]
                o_ref[:, cols] = ps.astype(o_ref.dtype)
            return
        if resident:
            ks = b_ref.shape[2]
            p = None
            for s in range(resident):
                ps = lax.dot_general(a_ref[:, s * ks:(s + 1) * ks], b_ref[s], dn, preferred_element_type=F32)
                p = ps if p is None else p + ps
        else:
            av, bv = a_ref[...], b_ref[...]
            if av.dtype != BF16:
                av = av.astype(BF16)
            if bv.dtype != BF16:
                bv = bv.astype(BF16)
            p = lax.dot_general(av, bv, dn, preferred_element_type=F32)

        def finish(acc):
            if bias_ref is not None:
                acc = acc + bias_ref[...]
            if add_ref is not None:
                acc = acc + add_ref[...]
            o_ref[...] = acc.astype(o_ref.dtype)

        if nk == 1:
            finish(p)
        else:
            acc_ref = refs[pos + 1]
            kk = pl.program_id(2)

            @pl.when(kk == 0)
            def _():
                acc_ref[...] = p

            @pl.when(kk > 0)
            def _():
                acc_ref[...] += p

            @pl.when(kk == nk - 1)
            def _():
                finish(acc_ref[...])

    res = _call(body, grid=(m // tm, n // tn, nk), in_specs=in_specs, out_specs=[out_spec], out_shape=[out_shape],
                scratch_shapes=[pltpu.VMEM((tm, tn), F32)] if nk > 1 else [],
                sem=("parallel", "parallel", "arbitrary"), name=name, args=args, carried=carried)
    return res[0] if carried is None else (res[0], res[1:])


def _rowcall(fn, rows, consts, row_outs, acc_outs, *, name, tm=ROW_TILE, col_grid=1):
    n_rows = rows[0][0].shape[0]
    assert n_rows % tm == 0
    grid = (col_grid, n_rows // tm)
    in_specs = [pl.BlockSpec((tm, w), functools.partial(lambda c, i, cb: (i, cb + c), cb=cb)) for _, w, cb in rows]
    in_specs += [pl.BlockSpec(k.shape, functools.partial(lambda c, i, nd: (0,) * nd, nd=k.ndim)) for k in consts]
    out_specs = [pl.BlockSpec((tm, w), lambda c, i: (i, c)) for _, _, _, w in row_outs]
    out_specs += [pl.BlockSpec((r, w), lambda c, i: (0, c)) for r, _, w in acc_outs]
    out_shape = [_sds((nr, nc), dt) for nr, nc, dt, _ in row_outs] + [_sds((r, nc), F32) for r, nc, _ in acc_outs]
    n_in, n_ro = len(rows) + len(consts), len(row_outs)

    def body(*refs):
        res = fn(*[r[...] for r in refs[:n_in]])
        if not isinstance(res, (tuple, list)):
            res = (res,)
        outs = refs[n_in:]
        for o_ref, val in zip(outs[:n_ro], res[:n_ro]):
            o_ref[...] = val.astype(o_ref.dtype)
        if acc_outs:
            first = pl.program_id(1) == 0

            @pl.when(first)
            def _():
                for o_ref, val in zip(outs[n_ro:], res[n_ro:]):
                    o_ref[...] = val

            @pl.when(jnp.logical_not(first))
            def _():
                for o_ref, val in zip(outs[n_ro:], res[n_ro:]):
                    o_ref[...] += val

    out = pl.pallas_call(
        body, grid=grid, in_specs=in_specs, out_specs=out_specs, out_shape=out_shape,
        compiler_params=_params(("arbitrary", "arbitrary")), name=name,
    )(*[r[0] for r in rows], *consts)
    return out


def _matmul_rows(b, *, form, tm, tk, fn, rows, consts, row_outs, acc_outs, name, a=None, a_rows=None, a_fn=None,
                 carried=None):
    b3 = b.ndim == 3
    resident = 0
    if form == "nn":
        k, n = b.shape
        b_spec = pl.BlockSpec((tk, n), lambda i, kk: (kk, 0))
        dn = (((1,), (0,)), ((), ()))
    else:
        n = b.shape[1] if b3 else b.shape[0]
        k = b.shape[0] * b.shape[2] if b3 else b.shape[1]
        if b3 and tk == k:
            resident = b.shape[0]
            b_spec = pl.BlockSpec(b.shape, lambda i, kk: (0, 0, 0))
        else:
            b_spec = (pl.BlockSpec((None, n, tk), lambda i, kk: (kk, 0, 0)) if b3
                      else pl.BlockSpec((n, tk), lambda i, kk: (0, kk)))
        dn = (((1,), (1,)), ((), ()))
    nk = k // tk
    lhs_in = [(a, tk, 0)] if a is not None else list(a_rows)
    assert a is not None or nk == 1
    m = lhs_in[0][0].shape[0]
    n_lhs = len(lhs_in)
    in_specs = [pl.BlockSpec((tm, tk), lambda i, kk: (i, kk))] if a is not None else [
        pl.BlockSpec((tm, w), functools.partial(lambda i, kk, cb: (i, cb), cb=cb)) for _, w, cb in a_rows]
    in_specs.append(b_spec)
    in_specs += [pl.BlockSpec((tm, w), functools.partial(lambda i, kk, cb: (i, cb), cb=cb)) for _, w, cb in rows]
    in_specs += [pl.BlockSpec(c.shape, functools.partial(lambda i, kk, nd: (0,) * nd, nd=c.ndim)) for c in consts]
    out_specs = [pl.BlockSpec((tm, w), lambda i, kk: (i, 0)) for _, w in row_outs]
    out_specs += [pl.BlockSpec((r, w), lambda i, kk: (0, 0)) for r, w in acc_outs]
    out_shape = [_sds((m, w), dt) for dt, w in row_outs] + [_sds((r, w), F32) for r, w in acc_outs]
    n_rows, n_consts, n_ro, n_acc = len(rows), len(consts), len(row_outs), len(acc_outs)

    def body(*refs):
        pos = n_lhs + 1
        row_refs, const_refs = refs[pos:pos + n_rows], refs[pos + n_rows:pos + n_rows + n_consts]
        pos += n_rows + n_consts
        out_refs, acc_refs = refs[pos:pos + n_ro], refs[pos + n_ro:pos + n_ro + n_acc]
        i, kk = pl.program_id(0), pl.program_id(1)
        if resident:
            b_ref, ks, p = refs[n_lhs], b.shape[2], None
            for s in range(resident):
                ps = lax.dot_general(refs[0][:, s * ks:(s + 1) * ks], b_ref[s], dn, preferred_element_type=F32)
                p = ps if p is None else p + ps
        else:
            lhs = refs[0][...] if a is not None else a_fn(*[r[...] for r in refs[:n_lhs]]).astype(BF16)
            p = lax.dot_general(lhs, refs[n_lhs][...], dn, preferred_element_type=F32)

        def finish(acc):
            extra = [r[...] for r in row_refs] + [c[...] for c in const_refs]
            res = fn(acc, lhs, *extra) if a is None else fn(acc, *extra)
            for o_ref, val in zip(out_refs, res[:n_ro]):
                o_ref[...] = val.astype(o_ref.dtype)
            if n_acc:
                @pl.when(i == 0)
                def _():
                    for o_ref, val in zip(acc_refs, res[n_ro:]):
                        o_ref[...] = val

                @pl.when(i > 0)
                def _():
                    for o_ref, val in zip(acc_refs, res[n_ro:]):
                        o_ref[...] += val

        if nk == 1:
            finish(p)
        else:
            acc_ref = refs[pos + n_ro + n_acc]

            @pl.when(kk == 0)
            def _():
                acc_ref[...] = p

            @pl.when(kk > 0)
            def _():
                acc_ref[...] += p

            @pl.when(kk == nk - 1)
            def _():
                finish(acc_ref[...])

    res = _call(body, grid=(m // tm, nk), in_specs=in_specs, out_specs=out_specs, out_shape=out_shape,
                scratch_shapes=[pltpu.VMEM((tm, n), F32)] if nk > 1 else [], sem=("arbitrary", "arbitrary"),
                name=name, args=[r[0] for r in lhs_in] + [b] + [r[0] for r in rows] + list(consts), carried=carried)
    own = n_ro + n_acc
    return res[:own] if carried is None else (res[:own], res[own:])


def _colsum(v):
    return jnp.sum(v, axis=0, keepdims=True)


def _sigmoid(v):
    return 1.0 / (1.0 + jnp.exp(-v))


_GELU_C = math.sqrt(2.0 / math.pi)


def _gelu(v):
    return 0.5 * v * (1.0 + jnp.tanh(_GELU_C * (v + 0.044715 * (v * v * v))))


def _gelu_and_grad(v):
    th = jnp.tanh(_GELU_C * (v + 0.044715 * (v * v * v)))
    g = 0.5 * v * (1.0 + th)
    dg = 0.5 * (1.0 + th) + 0.5 * v * (1.0 - th * th) * (_GELU_C * (1.0 + 3.0 * 0.044715 * (v * v)))
    return g, dg


def _rms_stats(v):
    r = lax.rsqrt(jnp.mean(v * v, axis=-1, keepdims=True) + EPS)
    return v * r, r


def _rms_bwd(dn, vn, r):
    return r * (dn - vn * jnp.mean(dn * vn, axis=-1, keepdims=True))


def _pre_norm(x, g, sc, sh, name):
    def fn(xv, gv, scv, shv):
        xn, _ = _rms_stats(xv)
        return (xn * gv) * (1.0 + scv) + shv
    return _rowcall(fn, [(x, D, 0)], [g, sc, sh], [(x.shape[0], D, BF16, D)], [], name=name)[0]


def _pre_norm_bwd(dh, x, dx_other, g, sc, name):
    def fn(dhv, xv, dov, gv, scv):
        xn, r = _rms_stats(xv)
        yn = xn * gv
        dyn = dhv * (1.0 + scv)
        dx = _rms_bwd(dyn * gv, xn, r)
        return dov + dx, _colsum(dhv), _colsum(dhv * yn), _colsum(dyn * xn)
    t = x.shape[0]
    return _rowcall(fn, [(dh, D, 0), (x, D, 0), (dx_other, D, 0)], [g, sc], [(t, D, F32, D)],
                    [(1, D, D)] * 3, name=name)


CONV_HALO = 32


def _layer_norm_parts(u):
    mu = jnp.mean(u, axis=-1, keepdims=True)
    d = u - mu
    r = lax.rsqrt(jnp.mean(d * d, axis=-1, keepdims=True) + EPS)
    return d * r, r


LANES = 128
SUBLANE_ROWS = 8
CONV_ROWS = 64


def _lanes(c):
    return slice(c * LANES, (c + 1) * LANES)


def _conv_branch(z, w_dw, b_dw, g_ln, b_ln, name, tm=ROW_TILE, carried=None):
    t = z.shape[0]
    per = tm // CONV_HALO
    n_chunks = 512 // LANES

    def body(ga_ref, gb_ref, gah_ref, gbh_ref, w_ref, b_ref, g_ref, bl_ref, u1_ref, u3_ref, scr):
        i = pl.program_id(0)
        u0h = jnp.where(i > 0, gah_ref[...] * _sigmoid(gbh_ref[...]), 0.0)
        u0 = ga_ref[...] * _sigmoid(gb_ref[...])
        for c in range(n_chunks):
            scr[c, 0:CONV_HALO, :] = u0h[:, _lanes(c)]
            scr[c, CONV_HALO:CONV_HALO + tm, :] = u0[:, _lanes(c)]
        for c in range(n_chunks):
            for r0 in range(0, tm, CONV_ROWS):
                acc = jnp.zeros((CONV_ROWS, LANES), F32) + b_ref[:, _lanes(c)]
                for j in range(CONV_K):
                    acc = acc + w_ref[j:j + 1, _lanes(c)] * scr[c, pl.ds(r0 + CONV_HALO - (CONV_K - 1) + j, CONV_ROWS), :]
                u1_ref[r0:r0 + CONV_ROWS, _lanes(c)] = acc
        xh, _ = _layer_norm_parts(u1_ref[...])
        u2 = xh * g_ref[...] + bl_ref[...]
        u3_ref[...] = (u2 * _sigmoid(u2)).astype(BF16)

    cur = lambda cb: pl.BlockSpec((tm, 512), lambda i: (i, cb))
    halo = lambda cb: pl.BlockSpec((CONV_HALO, 512), lambda i: (jnp.maximum(i * per - 1, 0), cb))
    whole = lambda a: pl.BlockSpec(a.shape, lambda i: (0, 0))
    res = _call(
        body, grid=(t // tm,),
        in_specs=[cur(3), cur(4), halo(3), halo(4), whole(w_dw), whole(b_dw), whole(g_ln), whole(b_ln)],
        out_specs=[pl.BlockSpec((tm, 512), lambda i: (i, 0))] * 2,
        out_shape=[_sds((t, 512), F32), _sds((t, 512), BF16)],
        scratch_shapes=[pltpu.VMEM((n_chunks, CONV_HALO + tm, LANES), F32)],
        sem=("arbitrary",), name=name, args=(z, z, z, z, w_dw, b_dw, g_ln, b_ln), carried=carried)
    return res[:2] if carried is None else (res[:2], res[2:])


def _conv_branch_bwd(du3, u1, z, w_dw, g_ln, b_ln, name, tm=ROW_TILE, carried=None):
    t = z.shape[0]
    per = tm // CONV_HALO
    last = t // tm - 1
    n_chunks = 512 // LANES

    def du1_of(du3v, u1v, g, b):
        xh, r = _layer_norm_parts(u1v)
        u2 = xh * g + b
        s = _sigmoid(u2)
        du2 = du3v * (s * (1.0 + u2 * (1.0 - s)))
        dxh = du2 * g
        du1 = r * (dxh - jnp.mean(dxh, axis=-1, keepdims=True) - xh * jnp.mean(dxh * xh, axis=-1, keepdims=True))
        return du1, du2, xh

    def body(d_ref, u_ref, dn_ref, un_ref, ga_ref, gb_ref, gah_ref, gbh_ref, w_ref, g_ref, bl_ref,
             dglu_ref, dw_ref, dbdw_ref, dg_ref, dbl_ref, dbin_ref, scr, scd):
        i = pl.program_id(0)
        g, b = g_ref[...], bl_ref[...]
        du1, du2, xh = du1_of(d_ref[...], u_ref[...], g, b)
        du1n, _, _ = du1_of(dn_ref[...], un_ref[...], g, b)
        du1n = jnp.where(i < last, du1n, 0.0)
        sgb = _sigmoid(gb_ref[...])
        ga = ga_ref[...]
        u0 = ga * sgb
        u0h = jnp.where(i > 0, gah_ref[...] * _sigmoid(gbh_ref[...]), 0.0)
        for c in range(n_chunks):
            scd[c, 0:tm, :] = du1[:, _lanes(c)]
            scd[c, tm:tm + CONV_HALO, :] = du1n[:, _lanes(c)]
            scr[c, 0:CONV_HALO, :] = u0h[:, _lanes(c)]
            scr[c, CONV_HALO:CONV_HALO + tm, :] = u0[:, _lanes(c)]

        @pl.when(i == 0)
        def _():
            for ref in (dw_ref, dbdw_ref, dg_ref, dbl_ref, dbin_ref):
                ref[...] = jnp.zeros_like(ref)

        dsg = ga * (sgb * (1.0 - sgb))
        for c in range(n_chunks):
            gate = slice(512 + c * LANES, 512 + (c + 1) * LANES)
            for r0 in range(0, tm, CONV_ROWS):
                rows = slice(r0, r0 + CONV_ROWS)
                du0 = jnp.zeros((CONV_ROWS, LANES), F32)
                for j in range(CONV_K):
                    du0 = du0 + w_ref[j:j + 1, _lanes(c)] * scd[c, pl.ds(r0 + CONV_K - 1 - j, CONV_ROWS), :]
                dga = du0 * sgb[rows, _lanes(c)]
                dgb = du0 * dsg[rows, _lanes(c)]
                dglu_ref[rows, _lanes(c)] = dga.astype(BF16)
                dglu_ref[rows, gate] = dgb.astype(BF16)
                dbin_ref[:, _lanes(c)] += _colsum(dga)
                dbin_ref[:, gate] += _colsum(dgb)
            for j in range(CONV_K):
                dwj = jnp.zeros((SUBLANE_ROWS, LANES), F32)
                for r0 in range(0, tm, CONV_ROWS):
                    prod = (scd[c, pl.ds(r0, CONV_ROWS), :]
                            * scr[c, pl.ds(r0 + CONV_HALO - (CONV_K - 1) + j, CONV_ROWS), :])
                    dwj = dwj + jnp.sum(prod.reshape(CONV_ROWS // SUBLANE_ROWS, SUBLANE_ROWS, LANES), axis=0)
                dw_ref[j:j + 1, _lanes(c)] += _colsum(dwj)
        dbdw_ref[...] += _colsum(du1)
        dg_ref[...] += _colsum(du2 * xh)
        dbl_ref[...] += _colsum(du2)

    cur = lambda cb: pl.BlockSpec((tm, 512), lambda i: (i, cb))
    prev = lambda cb: pl.BlockSpec((CONV_HALO, 512), lambda i: (jnp.maximum(i * per - 1, 0), cb))
    nxt = pl.BlockSpec((CONV_HALO, 512), lambda i: (jnp.minimum((i + 1) * per, t // CONV_HALO - 1), 0))
    whole = lambda a: pl.BlockSpec(a.shape, lambda i: (0, 0))
    acc = lambda r, w: pl.BlockSpec((r, w), lambda i: (0, 0))
    res = _call(
        body, grid=(t // tm,),
        in_specs=[cur(0), cur(0), nxt, nxt, cur(3), cur(4), prev(3), prev(4), whole(w_dw), whole(g_ln), whole(b_ln)],
        out_specs=[pl.BlockSpec((tm, 1024), lambda i: (i, 0)), acc(CONV_K, 512), acc(1, 512), acc(1, 512),
                   acc(1, 512), acc(1, 1024)],
        out_shape=[_sds((t, 1024), BF16), _sds((CONV_K, 512), F32), _sds((1, 512), F32), _sds((1, 512), F32),
                   _sds((1, 512), F32), _sds((1, 1024), F32)],
        scratch_shapes=[pltpu.VMEM((n_chunks, CONV_HALO + tm, LANES), F32),
                        pltpu.VMEM((n_chunks, tm + CONV_HALO, LANES), F32)],
        sem=("arbitrary",), name=name, args=(du3, u1, du3, u1, z, z, z, z, w_dw, g_ln, b_ln), carried=carried)
    return res[:6] if carried is None else (res[:6], res[6:])


FF_BLOCK = D_FF // 2
FF_HALO = 8
FF_CHUNKS = FF_BLOCK // LANES


FF_ROWS = 64
FF_EXT_ROWS = 88


def _ffn_conv(w_ref, b_ref, scr, k, rows, r0=0):
    acc = b_ref[:, _lanes(k)] + w_ref[0:1, _lanes(k)] * scr[k, pl.ds(r0 + FF_HALO - 2, rows), :]
    acc = acc + w_ref[1:2, _lanes(k)] * scr[k, pl.ds(r0 + FF_HALO - 1, rows), :]
    return acc + w_ref[2:3, _lanes(k)] * scr[k, pl.ds(r0 + FF_HALO, rows), :]


def _ffn_act(up, w3, b3, name, tm=ROW_TILE, carried=None):
    t = up.shape[0]
    per = tm // FF_HALO
    wide = 2 * FF_BLOCK

    def body(u_ref, uh_ref, w_ref, b_ref, o_ref, scr):
        i = pl.program_id(1)
        for k in range(2 * FF_CHUNKS):
            scr[k, 0:FF_HALO, :] = jnp.where(i > 0, uh_ref[:, _lanes(k)], 0.0)
            scr[k, FF_HALO:FF_HALO + tm, :] = u_ref[:, _lanes(k)]
        for cc in range(FF_CHUNKS):
            for r0 in range(0, tm, FF_ROWS):
                val = _ffn_conv(w_ref, b_ref, scr, cc, FF_ROWS, r0)
                gate = _ffn_conv(w_ref, b_ref, scr, FF_CHUNKS + cc, FF_ROWS, r0)
                o_ref[r0:r0 + FF_ROWS, _lanes(cc)] = (_gelu(gate) * val).astype(BF16)

    res = _call(
        body, grid=(2, t // tm),
        in_specs=[pl.BlockSpec((tm, wide), lambda c, i: (i, c)),
                  pl.BlockSpec((FF_HALO, wide), lambda c, i: (jnp.maximum(i * per - 1, 0), c)),
                  pl.BlockSpec((FFN_K, wide), lambda c, i: (0, c)),
                  pl.BlockSpec((1, wide), lambda c, i: (0, c))],
        out_specs=[pl.BlockSpec((tm, FF_BLOCK), lambda c, i: (i, c))],
        out_shape=[_sds((t, D_FF), BF16)],
        scratch_shapes=[pltpu.VMEM((2 * FF_CHUNKS, FF_HALO + tm, LANES), F32)],
        sem=("arbitrary", "arbitrary"), name=name, args=(up, up, w3, b3), carried=carried)
    return res[0] if carried is None else (res[0], res[1:])


def _ffn_act_bwd(dact, up, w3, b3, name, tm=ROW_TILE):
    t = up.shape[0]
    per = tm // FF_HALO
    wide = 2 * FF_BLOCK
    last = t // tm - 1
    ext = tm + FF_HALO

    def body(u_ref, up_ref, un_ref, d_ref, dn_ref, w_ref, b_ref, o_ref, dw_ref, db_ref, scr, scd):
        i = pl.program_id(1)
        for k in range(2 * FF_CHUNKS):
            scr[k, 0:FF_HALO, :] = jnp.where(i > 0, up_ref[:, _lanes(k)], 0.0)
            scr[k, FF_HALO:FF_HALO + tm, :] = u_ref[:, _lanes(k)]
            scr[k, FF_HALO + tm:FF_HALO + ext, :] = un_ref[:, _lanes(k)]
        dn = jnp.where(i < last, dn_ref[...], 0.0)

        @pl.when(i == 0)
        def _():
            dw_ref[...] = jnp.zeros_like(dw_ref)
            db_ref[...] = jnp.zeros_like(db_ref)

        for cc in range(FF_CHUNKS):
            gc = FF_CHUNKS + cc
            for r0 in range(0, ext, FF_EXT_ROWS):
                rows = pl.ds(r0, FF_EXT_ROWS)
                val = _ffn_conv(w_ref, b_ref, scr, cc, FF_EXT_ROWS, r0)
                gel, dgel = _gelu_and_grad(_ffn_conv(w_ref, b_ref, scr, gc, FF_EXT_ROWS, r0))
                da = d_ref[r0:r0 + FF_EXT_ROWS, _lanes(cc)] if r0 + FF_EXT_ROWS <= tm else jnp.concatenate(
                    [d_ref[r0:tm, _lanes(cc)], dn[:, _lanes(cc)]], axis=0)
                scd[cc, rows, :] = da * gel
                scd[gc, rows, :] = da * val * dgel
            for k in (cc, gc):
                dwk = [jnp.zeros((SUBLANE_ROWS, LANES), F32) for _ in range(FFN_K)]
                dbk = jnp.zeros((SUBLANE_ROWS, LANES), F32)
                for r0 in range(0, tm, FF_ROWS):
                    shifted = [scd[k, pl.ds(r0 + FFN_K - 1 - j, FF_ROWS), :] for j in range(FFN_K)]
                    ucur = scr[k, pl.ds(r0 + FF_HALO, FF_ROWS), :]
                    o_ref[r0:r0 + FF_ROWS, _lanes(k)] = (
                        w_ref[0:1, _lanes(k)] * shifted[0] + w_ref[1:2, _lanes(k)] * shifted[1]
                        + w_ref[2:3, _lanes(k)] * shifted[2]).astype(BF16)
                    fold = lambda v: jnp.sum(v.reshape(FF_ROWS // SUBLANE_ROWS, SUBLANE_ROWS, LANES), axis=0)
                    for j in range(FFN_K):
                        dwk[j] = dwk[j] + fold(shifted[j] * ucur)
                    dbk = dbk + fold(shifted[FFN_K - 1])
                for j in range(FFN_K):
                    dw_ref[j:j + 1, _lanes(k)] += _colsum(dwk[j])
                db_ref[:, _lanes(k)] += _colsum(dbk)

    nblk = t // FF_HALO
    return pl.pallas_call(
        body, grid=(2, t // tm),
        in_specs=[pl.BlockSpec((tm, wide), lambda c, i: (i, c)),
                  pl.BlockSpec((FF_HALO, wide), lambda c, i: (jnp.maximum(i * per - 1, 0), c)),
                  pl.BlockSpec((FF_HALO, wide), lambda c, i: (jnp.minimum((i + 1) * per, nblk - 1), c)),
                  pl.BlockSpec((tm, FF_BLOCK), lambda c, i: (i, c)),
                  pl.BlockSpec((FF_HALO, FF_BLOCK), lambda c, i: (jnp.minimum((i + 1) * per, nblk - 1), c)),
                  pl.BlockSpec((FFN_K, wide), lambda c, i: (0, c)),
                  pl.BlockSpec((1, wide), lambda c, i: (0, c))],
        out_specs=[pl.BlockSpec((tm, wide), lambda c, i: (i, c)),
                   pl.BlockSpec((FFN_K, wide), lambda c, i: (0, c)),
                   pl.BlockSpec((1, wide), lambda c, i: (0, c))],
        out_shape=[_sds((t, 2 * D_FF), BF16), _sds((FFN_K, 2 * D_FF), F32), _sds((1, 2 * D_FF), F32)],
        scratch_shapes=[pltpu.VMEM((2 * FF_CHUNKS, FF_HALO + ext, LANES), F32),
                        pltpu.VMEM((2 * FF_CHUNKS, ext, LANES), F32)],
        compiler_params=_params(("arbitrary", "arbitrary")), name=name,
    )(up, up, up, dact, dact, w3, b3)


def _toeplitz_map():
    f = np.zeros((TOEP, REL_PAD), np.float32)
    for m in range(TOEP - 1):
        rel = (WINDOW - 1) - m
        f[m, int(np.clip(rel, -MAX_REL, MAX_REL)) + MAX_REL] = 1.0
    return f


def _split3(v):
    hi = v.astype(BF16)
    r1 = v - hi.astype(F32)
    mid = r1.astype(BF16)
    lo = (r1 - mid.astype(F32)).astype(BF16)
    return hi, mid, lo


def _exact_select(v, sel):
    out = None
    for part in _split3(v):
        p = jnp.dot(part, sel, preferred_element_type=F32)
        out = p if out is None else out + p
    return out


def _select_call(v, sel, name):
    def body(v_ref, s_ref, o_ref):
        o_ref[...] = _exact_select(v_ref[...], s_ref[...])
    return pl.pallas_call(body, out_shape=_sds((v.shape[0], sel.shape[1]), F32), name=name)(v, sel)


def _band_bias(gen_row):
    b0 = jnp.broadcast_to(gen_row, (Q_TILE, TOEP))
    bias = pltpu.roll(b0, TOEP - (Q_TILE - 1), 1, stride=1, stride_axis=0)[:, :WINDOW]
    qq = lax.broadcasted_iota(jnp.int32, (Q_TILE, WINDOW), 0) // CHUNK
    kc = lax.broadcasted_iota(jnp.int32, (Q_TILE, WINDOW), 1) // CHUNK
    return jnp.where((kc >= qq) & (kc <= qq + LEFT_CHUNKS), bias, NEG_INF)


PAD_ROWS = WINDOW - Q_TILE
NT_DIMS = (((1,), (1,)), ((), ()))
TN_DIMS = (((0,), (0,)), ((), ()))


def _head_mask(hh):
    lane = lax.broadcasted_iota(jnp.int32, (1, 128), 1)
    return (lane < 64) if hh == 0 else (lane >= 64)


SOFTMAX_ROWS = 16


def _probs_block(s_scr, bias, hh, rows, q_start):
    s = s_scr[rows, :] + bias[hh, rows, :]
    col = lax.broadcasted_iota(jnp.int32, (SOFTMAX_ROWS, WINDOW), 1)
    s = jnp.where(col >= PAD_ROWS - q_start, s, NEG_INF)
    p = jnp.exp(s - jnp.max(s, axis=-1, keepdims=True))
    return p / jnp.sum(p, axis=-1, keepdims=True)


def _attention(z, gen, name, carried=None):
    t = z.shape[0]
    n_i = t // STEP_ROWS

    def body(q_ref, k_ref, v_ref, g_ref, o_ref, kpad, vpad, bias, s_scr, p_scr):
        hp, i = pl.program_id(0), pl.program_id(1)

        @pl.when(i == 0)
        def _():
            kpad[0:PAD_ROWS, :] = jnp.zeros((PAD_ROWS, 128), BF16)
            vpad[0:PAD_ROWS, :] = jnp.zeros((PAD_ROWS, 128), BF16)
            kpad[PAD_ROWS:PAD_ROWS + t, :] = k_ref[...].astype(BF16)
            vpad[PAD_ROWS:PAD_ROWS + t, :] = v_ref[...].astype(BF16)
            for hh in range(2):
                bias[hh] = _band_bias(g_ref[pl.ds(2 * hp + hh, 1), :])

        for q0 in range(0, STEP_ROWS, Q_TILE):
            q_start = i * STEP_ROWS + q0
            win = pl.ds(pl.multiple_of(q_start, Q_TILE), WINDOW)
            out = None
            for hh in range(2):
                mask = _head_mask(hh)
                qm = jnp.where(mask, q_ref[q0:q0 + Q_TILE, :] * (CHUNK ** -0.5), 0.0).astype(BF16)
                slot = 2 * (q0 // Q_TILE) + hh
                s_scr[slot] = lax.dot_general(qm, kpad[win, :], NT_DIMS, preferred_element_type=F32)
                for r0 in range(0, Q_TILE, SOFTMAX_ROWS):
                    rows = slice(r0, r0 + SOFTMAX_ROWS)
                    p_scr[slot, rows, :] = _probs_block(s_scr.at[slot], bias, hh, rows, q_start).astype(BF16)
                o = jnp.dot(p_scr[slot], vpad[win, :], preferred_element_type=F32)
                out = jnp.where(mask, o, 0.0) if out is None else jnp.where(mask, o, out)
            o_ref[q0:q0 + Q_TILE, :] = out.astype(BF16)

    res = _call(
        body, grid=(4, n_i),
        in_specs=[pl.BlockSpec((STEP_ROWS, 128), lambda h, i: (i, h)),
                  pl.BlockSpec((t, 128), lambda h, i: (0, 4 + h)),
                  pl.BlockSpec((t, 128), lambda h, i: (0, 8 + h)),
                  pl.BlockSpec((N_HEADS, TOEP), lambda h, i: (0, 0))],
        out_specs=[pl.BlockSpec((STEP_ROWS, 128), lambda h, i: (i, h))],
        out_shape=[_sds((t, 512), BF16)],
        scratch_shapes=[pltpu.VMEM((PAD_ROWS + t, 128), BF16), pltpu.VMEM((PAD_ROWS + t, 128), BF16),
                        pltpu.VMEM((2, Q_TILE, WINDOW), F32), pltpu.VMEM((4, Q_TILE, WINDOW), F32),
                        pltpu.VMEM((4, Q_TILE, WINDOW), BF16)],
        sem=("arbitrary", "arbitrary"), name=name, args=(z, z, z, gen), carried=carried)
    return res[0] if carried is None else (res[0], res[1:])


def _attention_bwd(z, datt, gen, name, carried=None):
    t = z.shape[0]
    n_i = t // STEP_ROWS

    def body(q_ref, k_ref, v_ref, d_ref, g_ref, dq_ref, dk_ref, dv_ref, sq_ref, sk_ref, sv_ref, dg_ref,
             kpad, vpad, dkacc, dvacc, bias, dsacc, s_scr, dp_scr, p_scr, ds_scr):
        hp, i = pl.program_id(0), pl.program_id(1)

        @pl.when(i == 0)
        def _():
            kpad[0:PAD_ROWS, :] = jnp.zeros((PAD_ROWS, 128), BF16)
            vpad[0:PAD_ROWS, :] = jnp.zeros((PAD_ROWS, 128), BF16)
            kpad[PAD_ROWS:PAD_ROWS + t, :] = k_ref[...].astype(BF16)
            vpad[PAD_ROWS:PAD_ROWS + t, :] = v_ref[...].astype(BF16)
            dkacc[...] = jnp.zeros_like(dkacc)
            dvacc[...] = jnp.zeros_like(dvacc)
            dsacc[...] = jnp.zeros_like(dsacc)
            for hh in range(2):
                bias[hh] = _band_bias(g_ref[pl.ds(2 * hp + hh, 1), :])

        dq_sum = None
        for q0 in range(0, STEP_ROWS, Q_TILE):
            q_start = i * STEP_ROWS + q0
            win = pl.ds(pl.multiple_of(q_start, Q_TILE), WINDOW)
            dq = None
            for hh in range(2):
                mask = _head_mask(hh)
                qm = jnp.where(mask, q_ref[q0:q0 + Q_TILE, :] * (CHUNK ** -0.5), 0.0).astype(BF16)
                dom = jnp.where(mask, d_ref[q0:q0 + Q_TILE, :], 0.0).astype(BF16)
                slot = 2 * (q0 // Q_TILE) + hh
                s_scr[slot] = lax.dot_general(qm, kpad[win, :], NT_DIMS, preferred_element_type=F32)
                dp_scr[slot] = lax.dot_general(dom, vpad[win, :], NT_DIMS, preferred_element_type=F32)
                for r0 in range(0, Q_TILE, SOFTMAX_ROWS):
                    rows = slice(r0, r0 + SOFTMAX_ROWS)
                    p = _probs_block(s_scr.at[slot], bias, hh, rows, q_start)
                    dp = dp_scr[slot, rows, :]
                    ds = p * (dp - jnp.sum(p * dp, axis=-1, keepdims=True))
                    dsacc[hh, rows, :] += ds
                    ds_scr[slot, rows, :] = ds.astype(BF16)
                    p_scr[slot, rows, :] = p.astype(BF16)
                ds16 = ds_scr[slot]
                dqh = jnp.dot(ds16, kpad[win, :], preferred_element_type=F32) * (CHUNK ** -0.5)
                dq = jnp.where(mask, dqh, 0.0) if dq is None else jnp.where(mask, dqh, dq)
                dkacc[win, :] += lax.dot_general(ds16, qm, TN_DIMS, preferred_element_type=F32)
                dvacc[win, :] += lax.dot_general(p_scr[slot], dom, TN_DIMS, preferred_element_type=F32)
            dq_ref[q0:q0 + Q_TILE, :] = dq.astype(BF16)
            dq_sum = _colsum(dq) if dq_sum is None else dq_sum + _colsum(dq)

        @pl.when(i == 0)
        def _():
            sq_ref[...] = dq_sum

        @pl.when(i > 0)
        def _():
            sq_ref[...] += dq_sum

        @pl.when(i == n_i - 1)
        def _():
            dk = dkacc[PAD_ROWS:PAD_ROWS + t, :]
            dv = dvacc[PAD_ROWS:PAD_ROWS + t, :]
            dk_ref[...] = dk.astype(BF16)
            dv_ref[...] = dv.astype(BF16)
            sk_ref[...] = _colsum(dk)
            sv_ref[...] = _colsum(dv)
            rr = lax.broadcasted_iota(jnp.int32, (Q_TILE, Q_TILE), 0)
            cc = lax.broadcasted_iota(jnp.int32, (Q_TILE, Q_TILE), 1)
            rev = jnp.where(rr + cc == Q_TILE - 1, 1.0, 0.0).astype(BF16)
            for hh in range(2):
                acc = None
                for part in _split3(dsacc[hh]):
                    pr = jnp.dot(rev, part, preferred_element_type=F32)
                    acc = pr if acc is None else acc + pr
                wide = jnp.concatenate([acc, jnp.zeros((Q_TILE, TOEP - WINDOW), F32)], axis=1)
                dg_ref[pl.ds(2 * hp + hh, 1), :] = _colsum(pltpu.roll(wide, 0, 1, stride=1, stride_axis=0))

    col = lambda off: pl.BlockSpec((t, 128), lambda h, i: (0, off + h))
    tile = lambda: pl.BlockSpec((STEP_ROWS, 128), lambda h, i: (i, h))
    sums = lambda: pl.BlockSpec((1, 128), lambda h, i: (0, h))
    res = _call(
        body, grid=(4, n_i),
        in_specs=[tile(), col(4), col(8), tile(), pl.BlockSpec((N_HEADS, TOEP), lambda h, i: (0, 0))],
        out_specs=[tile(), col(0), col(0), sums(), sums(), sums(), pl.BlockSpec((N_HEADS, TOEP), lambda h, i: (0, 0))],
        out_shape=[_sds((t, 512), BF16)] * 3 + [_sds((1, 512), F32)] * 3 + [_sds((N_HEADS, TOEP), F32)],
        scratch_shapes=[pltpu.VMEM((PAD_ROWS + t, 128), BF16), pltpu.VMEM((PAD_ROWS + t, 128), BF16),
                        pltpu.VMEM((PAD_ROWS + t, 128), F32), pltpu.VMEM((PAD_ROWS + t, 128), F32),
                        pltpu.VMEM((2, Q_TILE, WINDOW), F32), pltpu.VMEM((2, Q_TILE, WINDOW), F32),
                        pltpu.VMEM((4, Q_TILE, WINDOW), F32), pltpu.VMEM((4, Q_TILE, WINDOW), F32),
                        pltpu.VMEM((4, Q_TILE, WINDOW), BF16), pltpu.VMEM((4, Q_TILE, WINDOW), BF16)],
        sem=("arbitrary", "arbitrary"), name=name, args=(z, z, z, datt, gen), carried=carried)
    return res[:7] if carried is None else (res[:7], res[7:])


def _adamw_math(w, g, m, v):
    m = ADAM_B1 * m + (1.0 - ADAM_B1) * g
    v = ADAM_B2 * v + (1.0 - ADAM_B2) * (g * g)
    m_hat = m / (1.0 - ADAM_B1 ** ADAM_STEP)
    v_hat = v / (1.0 - ADAM_B2 ** ADAM_STEP)
    delta = -ADAM_LR * (m_hat / (jnp.sqrt(v_hat) + ADAM_EPS) + ADAM_WD * w)
    return delta, m, v


def _adamw_many(items, name):
    n = len(items)

    def body(*refs):
        ins, outs = refs[:4 * n], refs[4 * n:]
        for k in range(n):
            w, g, m, v = (r[...] for r in ins[4 * k:4 * k + 4])
            outs[3 * k][...], outs[3 * k + 1][...], outs[3 * k + 2][...] = _adamw_math(w, g, m, v)

    flat = [a for item in items for a in item]
    res = pl.pallas_call(body, out_shape=[_sds(item[0].shape, F32) for item in items for _ in range(3)],
                         name=name)(*flat)
    return [tuple(res[3 * k:3 * k + 3]) for k in range(n)]


def _adamw(w, g, m, v, name, after):
    r, c = w.shape
    tm = next(cand for cand in (256, 176, 128, 64, 32, 16, 8) if r % cand == 0)
    return _rowcall(lambda wv, gv, mv, vv, _: (gv,) + _adamw_math(wv, gv, mv, vv),
                    [(w, c, 0), (g, c, 0), (m, c, 0), (v, c, 0)], [after], [(r, c, F32, c)] * 4, [], name=name, tm=tm)


def _ada_fwd(c_all, w_shard, b_shard, name):
    n = w_shard.shape[1]
    tn = 512

    def body(c_ref, w_ref, b_ref, o_ref, a_ref):
        cv = c_ref[...]
        act = cv * _sigmoid(cv)
        a_ref[...] = act
        o_ref[...] = jnp.dot(act.astype(BF16), w_ref[...].astype(BF16), preferred_element_type=F32) + b_ref[...]

    return pl.pallas_call(
        body, grid=(n // tn,),
        in_specs=[pl.BlockSpec((8, D), lambda j: (0, 0)), pl.BlockSpec((D, tn), lambda j: (0, j)),
                  pl.BlockSpec((1, tn), lambda j: (0, j))],
        out_specs=[pl.BlockSpec((8, tn), lambda j: (0, j)), pl.BlockSpec((8, D), lambda j: (0, 0))],
        out_shape=[_sds((8, n), F32), _sds((8, D), F32)],
        compiler_params=_params(("arbitrary",)), name=name,
    )(c_all, w_shard, b_shard)


def _ada_bwd_adamw(act_t, dmod_shard, w, m, v, name):
    r, c = w.shape
    tm = 256

    def body(a_ref, d_ref, w_ref, m_ref, v_ref, g_ref, dl_ref, nm_ref, nv_ref):
        g = jnp.dot(a_ref[...], d_ref[...], precision=lax.Precision.HIGHEST, preferred_element_type=F32)
        g_ref[...] = g
        dl_ref[...], nm_ref[...], nv_ref[...] = _adamw_math(w_ref[...], g, m_ref[...], v_ref[...])

    blk = pl.BlockSpec((tm, c), lambda i: (i, 0))
    return pl.pallas_call(
        body, grid=(r // tm,),
        in_specs=[pl.BlockSpec((tm, 8), lambda i: (i, 0)), pl.BlockSpec((8, c), lambda i: (0, 0)), blk, blk, blk],
        out_specs=[blk] * 4, out_shape=[_sds((r, c), F32)] * 4,
        compiler_params=_params(("arbitrary",)), name=name,
    )(act_t, dmod_shard, w, m, v)


def _place():
    return lax.axis_index("x"), lax.axis_index("y"), lax.axis_index("c")


def _flip(v, bit):
    return 1 - v if bit else v


VMEM_SPEC = pl.BlockSpec(memory_space=pltpu.VMEM)


def _allgather8(v, name):
    r, c = v.shape

    def body(v_ref, g_ref, tot_ref, send_sems, recv_sems, local_sem):
        x, y, cc = _place()
        sibling = (x, y, 1 - cc)
        chips = [(_flip(x, k & 2), _flip(y, k & 1)) for k in (1, 2, 3)]

        def block(px, py, pc):
            return g_ref.at[4 * px + 2 * py + pc]

        def copy(k, place, to, src=None):
            slot = block(*place)
            return pltpu.make_async_remote_copy(src_ref=slot if src is None else src, dst_ref=slot,
                                                send_sem=send_sems.at[k], recv_sem=recv_sems.at[k],
                                                device_id=to, device_id_type=MESH)

        mine = pltpu.make_async_copy(v_ref, block(x, y, cc), local_sem)
        mine.start()
        first = [copy(0, (x, y, cc), sibling, src=v_ref)]
        first += [copy(1 + j, (x, y, cc), (px, py, cc), src=v_ref) for j, (px, py) in enumerate(chips)]
        for cp in first:
            cp.start()
        passed = [copy(4 + j, (px, py, cc), sibling) for j, (px, py) in enumerate(chips)]
        for j, (px, py) in enumerate(chips):
            copy(1 + j, (px, py, cc), (x, y, cc)).wait_recv()
            passed[j].start()
        copy(0, sibling, (x, y, cc)).wait_recv()
        for j, (px, py) in enumerate(chips):
            copy(4 + j, (px, py, 1 - cc), (x, y, cc)).wait_recv()
        for cp in first + passed:
            cp.wait_send()
        mine.wait()
        tot = g_ref[0]
        for d in range(1, 8):
            tot = tot + g_ref[d]
        tot_ref[...] = tot

    return pl.pallas_call(
        body, in_specs=[VMEM_SPEC], out_specs=[VMEM_SPEC, VMEM_SPEC],
        out_shape=[_sds((8, r, c), F32), _sds((r, c), F32)],
        scratch_shapes=[pltpu.SemaphoreType.DMA((7,)), pltpu.SemaphoreType.DMA((7,)), pltpu.SemaphoreType.DMA],
        compiler_params=pltpu.CompilerParams(vmem_limit_bytes=VMEM_LIMIT), name=name,
    )(v)


def _slot(px, py, swapped):
    return 2 * py + px if swapped else 2 * px + py


def _gather_shards(arrs, swapped, name, in_place=False):
    n = len(arrs)

    def body(*refs):
        ins, outs = refs[:n], refs[n:2 * n]
        send1, recv1, send2, recv2, local_sems = refs[2 * n:]
        x, y, c = _place()
        sibling = (x, y, 1 - c)
        chips = [(_flip(x, k & 2), _flip(y, k & 1)) for k in (1, 2, 3)]
        local_copies, sends = [], []
        for a in range(n):
            h = outs[a].shape[1] // 2
            mine = pl.ds(pl.multiple_of(c * h, 8), h)
            own = _slot(x, y, swapped[a])
            if in_place:
                src = outs[a].at[own, mine]
            else:
                src = ins[a].at[mine]
                lc = pltpu.make_async_copy(ins[a], outs[a].at[own], local_sems.at[a])
                lc.start()
                local_copies.append(lc)
            for j, (px, py) in enumerate(chips):
                cp = pltpu.make_async_remote_copy(
                    src_ref=src, dst_ref=outs[a].at[own, mine], send_sem=send1.at[3 * a + j],
                    recv_sem=recv1.at[3 * a + j], device_id=(px, py, c), device_id_type=MESH)
                cp.start()
                sends.append(cp)
        for a in range(n):
            h = outs[a].shape[1] // 2
            mine = pl.ds(pl.multiple_of(c * h, 8), h)
            for j, (px, py) in enumerate(chips):
                piece = outs[a].at[_slot(px, py, swapped[a]), mine]
                pltpu.make_async_remote_copy(
                    src_ref=piece, dst_ref=piece, send_sem=send1.at[3 * a + j], recv_sem=recv1.at[3 * a + j],
                    device_id=(px, py, c), device_id_type=MESH).wait_recv()
                fwd = pltpu.make_async_remote_copy(
                    src_ref=piece, dst_ref=piece, send_sem=send2.at[3 * a + j], recv_sem=recv2.at[3 * a + j],
                    device_id=sibling, device_id_type=MESH)
                fwd.start()
                sends.append(fwd)
        for a in range(n):
            h = outs[a].shape[1] // 2
            other = pl.ds(pl.multiple_of((1 - c) * h, 8), h)
            for j, (px, py) in enumerate(chips):
                piece = outs[a].at[_slot(px, py, swapped[a]), other]
                pltpu.make_async_remote_copy(
                    src_ref=piece, dst_ref=piece, send_sem=send2.at[3 * a + j], recv_sem=recv2.at[3 * a + j],
                    device_id=sibling, device_id_type=MESH).wait_recv()
        for cp in sends:
            cp.wait_send()
        for lc in local_copies:
            lc.wait()

    dma = lambda k: pltpu.SemaphoreType.DMA((k,))
    return pl.pallas_call(
        body, in_specs=[ANY] * n, out_specs=[ANY] * n,
        out_shape=[_sds(a.shape if in_place else (4,) + a.shape, a.dtype) for a in arrs],
        scratch_shapes=[dma(3 * n), dma(3 * n), dma(3 * n), dma(3 * n), dma(n)],
        input_output_aliases={a: a for a in range(n)} if in_place else {},
        name=name,
    )(*arrs)


def _carry_pair_exchange(grads):
    n = len(grads)

    def copies(ins, outs, send_sems, recv_sems):
        x, y, c = _place()
        cps = []
        for a in range(n):
            h = ins[a].shape[1] // 2
            theirs = pl.ds(pl.multiple_of((1 - c) * h, 8), h)
            cps.append(pltpu.make_async_remote_copy(
                src_ref=ins[a].at[:, theirs, :], dst_ref=outs[a], send_sem=send_sems.at[a], recv_sem=recv_sems.at[a],
                device_id=(x, y, 1 - c), device_id_type=MESH))
        return cps

    def start(*refs):
        for cp in copies(*refs):
            cp.start()

    def finish(*refs):
        for cp in copies(*refs):
            cp.wait()

    return _Carried(grads, [_sds((4, g.shape[1] // 2, g.shape[2]), F32) for g in grads], {}, n, start, finish)


def _row_steps(h):
    return 1


def _pair_sum(grad, recv, core, name):
    _, r, c = grad.shape
    h = r // 2
    nr = _row_steps(h)
    th = h // nr

    def body(core_ref, g_ref, r_ref, o_ref):
        o_ref[...] = (g_ref[...] + r_ref[...]).astype(BF16)

    return pl.pallas_call(
        body,
        grid_spec=pltpu.PrefetchScalarGridSpec(
            num_scalar_prefetch=1, grid=(4, nr),
            in_specs=[pl.BlockSpec((None, th, c), lambda s, q, core_ref: (s, core_ref[0] * nr + q, 0)),
                      pl.BlockSpec((None, th, c), lambda s, q, core_ref: (s, q, 0))],
            out_specs=pl.BlockSpec((None, th, c), lambda s, q, core_ref: (s, q, 0))),
        out_shape=_sds((4, h, c), BF16), compiler_params=_params(("arbitrary", "arbitrary")), name=name,
    )(core, grad, recv)


def _carry_chip_exchange(parts, swapped):
    n = len(parts)

    def copies(ins, outs, send_sems, recv_sems):
        x, y, c = _place()
        chips = [(_flip(x, k & 2), _flip(y, k & 1)) for k in (1, 2, 3)]
        cps = []
        for a in range(n):
            for j, (px, py) in enumerate(chips):
                cps.append(pltpu.make_async_remote_copy(
                    src_ref=ins[a].at[_slot(px, py, swapped[a])], dst_ref=outs[a].at[j],
                    send_sem=send_sems.at[3 * a + j], recv_sem=recv_sems.at[3 * a + j],
                    device_id=(px, py, c), device_id_type=MESH))
        return cps

    def start(*refs):
        for cp in copies(*refs):
            cp.start()

    def finish(*refs):
        for cp in copies(*refs):
            cp.wait()

    return _Carried(parts, [_sds((3,) + p.shape[1:], BF16) for p in parts], {}, 3 * n, start, finish)


def _chip_sum(part, recv, slot_core, name):
    _, h, c = part.shape
    nr = _row_steps(h)
    th = h // nr

    def body(sc_ref, p_ref, r_ref, o_ref):
        acc = p_ref[...].astype(F32)
        for j in range(3):
            acc = acc + r_ref[j].astype(F32)
        o_ref[...] = acc

    return pl.pallas_call(
        body,
        grid_spec=pltpu.PrefetchScalarGridSpec(
            num_scalar_prefetch=1, grid=(nr,),
            in_specs=[pl.BlockSpec((None, th, c), lambda q, sc_ref: (sc_ref[0], q, 0)),
                      pl.BlockSpec((3, th, c), lambda q, sc_ref: (0, q, 0))],
            out_specs=pl.BlockSpec((th, c), lambda q, sc_ref: (sc_ref[1] * nr + q, 0))),
        out_shape=_sds((2 * h, c), F32), compiler_params=_params(("arbitrary",)), name=name,
    )(slot_core, part, recv)


def _carry_pair_share(shards):
    n = len(shards)

    def copies(outs, send_sems, recv_sems, mine):
        x, y, c = _place()
        cps = []
        for a in range(n):
            h = outs[a].shape[0] // 2
            half = outs[a].at[pl.ds(pl.multiple_of((c if mine else 1 - c) * h, 8), h)]
            cps.append(pltpu.make_async_remote_copy(
                src_ref=half, dst_ref=half, send_sem=send_sems.at[a], recv_sem=recv_sems.at[a],
                device_id=(x, y, 1 - c), device_id_type=MESH))
        return cps

    def start(ins, outs, send_sems, recv_sems):
        for cp in copies(outs, send_sems, recv_sems, True):
            cp.start()

    def finish(ins, outs, send_sems, recv_sems):
        for cp in copies(outs, send_sems, recv_sems, False):
            cp.wait_recv()
        for cp in copies(outs, send_sems, recv_sems, True):
            cp.wait_send()

    return _Carried(shards, [_sds(s.shape, F32) for s in shards], {a: a for a in range(n)}, n, start, finish)


def _carry_gather_ici(bufs, swapped):
    n = len(bufs)

    def copies(outs, send_sems, recv_sems, sending):
        x, y, c = _place()
        cps = []
        for a in range(n):
            h = outs[a].shape[1] // 2
            mine = pl.ds(pl.multiple_of(c * h, 8), h)
            for j, k in enumerate((1, 2, 3)):
                px, py = _flip(x, k & 2), _flip(y, k & 1)
                slot = _slot(x, y, swapped[a]) if sending else _slot(px, py, swapped[a])
                piece = outs[a].at[slot, mine]
                cps.append(pltpu.make_async_remote_copy(
                    src_ref=piece, dst_ref=piece, send_sem=send_sems.at[3 * a + j], recv_sem=recv_sems.at[3 * a + j],
                    device_id=(px, py, c), device_id_type=MESH))
        return cps

    def start(ins, outs, send_sems, recv_sems):
        for cp in copies(outs, send_sems, recv_sems, True):
            cp.start()

    def finish(ins, outs, send_sems, recv_sems):
        for cp in copies(outs, send_sems, recv_sems, False):
            cp.wait_recv()
        for cp in copies(outs, send_sems, recv_sems, True):
            cp.wait_send()

    return _Carried(bufs, [_sds(b.shape, b.dtype) for b in bufs], {a: a for a in range(n)}, 3 * n, start, finish)


HBM_SPEC = pl.BlockSpec(memory_space=pltpu.HBM)
SEM_SPEC = pl.BlockSpec(memory_space=pltpu.SEMAPHORE)
SIDE_EFFECT = pltpu.SideEffectType.DATAFLOW_SIDE_EFFECTING


def _ici_pieces(buf, send_sems, recv_sems, swapped, sending):
    x, y, c = _place()
    h = buf.shape[1] // 2
    mine = pl.ds(pl.multiple_of(c * h, 8), h)
    cps = []
    for j, k in enumerate((1, 2, 3)):
        px, py = _flip(x, k & 2), _flip(y, k & 1)
        piece = buf.at[_slot(x, y, swapped) if sending else _slot(px, py, swapped), mine]
        cps.append(pltpu.make_async_remote_copy(src_ref=piece, dst_ref=piece, send_sem=send_sems.at[j],
                                                recv_sem=recv_sems.at[j], device_id=(px, py, c), device_id_type=MESH))
    return cps


def _gather_ici_start(buf, after, swapped, name):
    def body(buf_ref, after_ref, send_sems, recv_sems, thru, token):
        for cp in _ici_pieces(thru, send_sems, recv_sems, swapped, True):
            cp.start()
        token[...] = jnp.zeros_like(token)

    return pl.pallas_call(
        body, name=name,
        out_shape=(pltpu.SemaphoreType.DMA((3,)), pltpu.SemaphoreType.DMA((3,)), pltpu.HBM(buf.shape, buf.dtype),
                   jax.ShapeDtypeStruct((8, 128), F32)),
        in_specs=(HBM_SPEC, ANY), out_specs=(SEM_SPEC, SEM_SPEC, HBM_SPEC, VMEM_SPEC), input_output_aliases={0: 2},
        compiler_params=pltpu.CompilerParams(has_side_effects=SIDE_EFFECT),
    )(pltpu.with_memory_space_constraint(buf, pltpu.HBM), after)


def _gather_ici_wait(send_sems, recv_sems, thru, after, swapped, name):
    def body(thru_ref, send_sems, recv_sems, after_ref, out_ref):
        for cp in _ici_pieces(out_ref, send_sems, recv_sems, swapped, True):
            cp.wait_send()
        for cp in _ici_pieces(out_ref, send_sems, recv_sems, swapped, False):
            cp.wait_recv()

    return pl.pallas_call(
        body, name=name, out_shape=pltpu.HBM(thru.shape, thru.dtype),
        in_specs=(HBM_SPEC, SEM_SPEC, SEM_SPEC, ANY), out_specs=HBM_SPEC, input_output_aliases={0: 0},
        compiler_params=pltpu.CompilerParams(has_side_effects=SIDE_EFFECT),
    )(thru, send_sems, recv_sems, after)


def _all8_copies(buf, send_sems, recv_sems, sending):
    x, y, c = _place()
    cps = []
    for k in range(1, 8):
        px, py, pc = _flip(x, k & 4), _flip(y, k & 2), _flip(c, k & 1)
        slot = buf.at[4 * x + 2 * y + c] if sending else buf.at[4 * px + 2 * py + pc]
        cps.append(pltpu.make_async_remote_copy(src_ref=slot, dst_ref=slot, send_sem=send_sems.at[k - 1],
                                                recv_sem=recv_sems.at[k - 1], device_id=(px, py, pc), device_id_type=MESH))
    return cps


def _all8_start(buf, name):
    def body(buf_ref, send_sems, recv_sems, thru, token):
        for cp in _all8_copies(thru, send_sems, recv_sems, True):
            cp.start()
        token[...] = jnp.zeros_like(token)

    return pl.pallas_call(
        body, name=name,
        out_shape=(pltpu.SemaphoreType.DMA((7,)), pltpu.SemaphoreType.DMA((7,)), pltpu.HBM(buf.shape, buf.dtype),
                   jax.ShapeDtypeStruct((8, 128), F32)),
        in_specs=(HBM_SPEC,), out_specs=(SEM_SPEC, SEM_SPEC, HBM_SPEC, VMEM_SPEC), input_output_aliases={0: 2},
        compiler_params=pltpu.CompilerParams(has_side_effects=SIDE_EFFECT),
    )(pltpu.with_memory_space_constraint(buf, pltpu.HBM))


def _all8_wait(send_sems, recv_sems, thru, after, name):
    def body(thru_ref, send_sems, recv_sems, after_ref, out_ref):
        for cp in _all8_copies(out_ref, send_sems, recv_sems, True):
            cp.wait_send()
        for cp in _all8_copies(out_ref, send_sems, recv_sems, False):
            cp.wait_recv()

    return pl.pallas_call(
        body, name=name, out_shape=pltpu.HBM(thru.shape, thru.dtype),
        in_specs=(HBM_SPEC, SEM_SPEC, SEM_SPEC, ANY), out_specs=HBM_SPEC, input_output_aliases={0: 0},
        compiler_params=pltpu.CompilerParams(has_side_effects=SIDE_EFFECT),
    )(thru, send_sems, recv_sems, after)


def _sum8(g, name):
    def body(g_ref, o_ref):
        tot = g_ref[0]
        for d in range(1, 8):
            tot = tot + g_ref[d]
        o_ref[...] = tot

    return pl.pallas_call(body, out_shape=_sds(g.shape[1:], F32), name=name)(g)


def _carry_gather_forward(bufs, swapped):
    n = len(bufs)

    def copies(outs, send_sems, recv_sems, sending):
        x, y, c = _place()
        cps = []
        for a in range(n):
            h = outs[a].shape[1] // 2
            rows = pl.ds(pl.multiple_of((c if sending else 1 - c) * h, 8), h)
            for j, k in enumerate((1, 2, 3)):
                piece = outs[a].at[_slot(_flip(x, k & 2), _flip(y, k & 1), swapped[a]), rows]
                cps.append(pltpu.make_async_remote_copy(
                    src_ref=piece, dst_ref=piece, send_sem=send_sems.at[3 * a + j], recv_sem=recv_sems.at[3 * a + j],
                    device_id=(x, y, 1 - c), device_id_type=MESH))
        return cps

    def start(ins, outs, send_sems, recv_sems):
        for cp in copies(outs, send_sems, recv_sems, True):
            cp.start()

    def finish(ins, outs, send_sems, recv_sems):
        for cp in copies(outs, send_sems, recv_sems, False):
            cp.wait_recv()
        for cp in copies(outs, send_sems, recv_sems, True):
            cp.wait_send()

    return _Carried(bufs, [_sds(b.shape, b.dtype) for b in bufs], {a: a for a in range(n)}, 3 * n, start, finish)


def _pack(arrs, rows_multiple=8):
    parts, offs, row = [], [], 0
    for a in arrs:
        flat = a.reshape(-1)
        nrow = -(-flat.shape[0] // D)
        parts.append(jnp.pad(flat, (0, nrow * D - flat.shape[0])))
        offs.append(row)
        row += nrow
    total = -(-row // rows_multiple) * rows_multiple
    if total > row:
        parts.append(jnp.zeros(((total - row) * D,), F32))
    return jnp.concatenate(parts).reshape(total, D), offs


def _unpack(packed, offs, shapes):
    out = []
    for off, shp in zip(offs, shapes):
        size = int(np.prod(shp))
        nrow = -(-size // D)
        out.append(packed[off:off + nrow].reshape(-1)[:size].reshape(shp))
    return out


def _to_bf16_slot(w, slot, name, after=None):
    r, c = w.shape
    tm = next(cand for cand in (256, 176, 128, 64, 32, 16) if r % cand == 0)

    def body(slot_ref, w_ref, *rest):
        rest[-1][...] = w_ref[...].astype(BF16)

    in_specs = [pl.BlockSpec((tm, c), lambda i, slot_ref: (i, 0))]
    if after is not None:
        in_specs.append(pl.BlockSpec((8, 128), lambda i, slot_ref: (0, 0)))
    return pl.pallas_call(
        body,
        grid_spec=pltpu.PrefetchScalarGridSpec(
            num_scalar_prefetch=1, grid=(r // tm,), in_specs=in_specs,
            out_specs=pl.BlockSpec((None, tm, c), lambda i, slot_ref: (slot_ref[0], i, 0))),
        out_shape=_sds((4, r, c), BF16), compiler_params=_params(("arbitrary",)), name=name,
    )(slot, w, *([] if after is None else [after]))


def _unshard_cols(g):
    s, k, n = g.shape
    return jnp.transpose(g, (1, 0, 2)).reshape(k, s * n)


def _ff_swap(v):
    b = FF_BLOCK
    return jnp.concatenate([v[..., 0:b], v[..., 2 * b:3 * b], v[..., b:2 * b], v[..., 3 * b:4 * b]], axis=-1)


LATE = ("attn_o", "conv_o", "mix_o", "up", "down")
EARLY_GRADS = ("down", "up", "mix_o", "attn_o", "conv_o")


def _weight_views(bufs):
    return {"up": bufs["up"], "attn_o": _unshard_cols(bufs["attn_o"]), "conv_o": _unshard_cols(bufs["conv_o"]),
            "mix_o": bufs["mix_o"].reshape(D, D), "down": bufs["down"].reshape(D_FF, D)}


def _pair_sums(names, grads, recv, dist):
    return [_pair_sum(g, r, dist["core"], "pair_sum_" + n) for n, g, r in zip(names, grads, recv)]


def _reduce_halves(names, parts, from_chips, dist):
    return [_chip_sum(p, r, jnp.concatenate([dist["slots"][SWAPPED[n]], dist["core"]]), "chip_sum_" + n)
            for n, p, r in zip(names, parts, from_chips)]


FUSED_TILE = 256
WIDE_TILE = 512


def _gates(z):
    return [(z, 512, 5), (z, 512, 6), (z, 512, 7), (z, 512, 8)]


def _mix_out(a, cb, z, x, w_mix_o, g_post, gt, g_pre2, sc2, sh2, name):
    def lhs(av, cv, ga0, ga1, gb0, gb1):
        ga, gb = jnp.concatenate([ga0, ga1], axis=1), jnp.concatenate([gb0, gb1], axis=1)
        return _sigmoid(ga) * av + _sigmoid(gb) * cv

    def fn(ym, y, xv, gv, gtv, g2v, scv, shv):
        yn, _ = _rms_stats(ym)
        x1 = xv + gtv * (yn * gv)
        xn, _ = _rms_stats(x1)
        return ym, y, x1, (xn * g2v) * (1.0 + scv) + shv

    return _matmul_rows(w_mix_o, form="nn", tm=min(FUSED_TILE, x.shape[0]), tk=D, fn=fn, a_rows=[(a, D, 0), (cb, D, 0)] + _gates(z),
                        a_fn=lhs, rows=[(x, D, 0)], consts=[g_post, gt, g_pre2, sc2, sh2],
                        row_outs=[(F32, D), (BF16, D), (F32, D), (BF16, D)], acc_outs=[], name=name)


def _down_tail(act, w_down, x1, target, g, gt, name):
    def fn(yv, xv, tv, gv, gtv):
        yn, r = _rms_stats(yv)
        e = xv + gtv * (yn * gv) - tv
        dx2 = e * (1.0 / D)
        dyn = dx2 * gtv
        return (dx2, _rms_bwd(dyn * gv, yn, r), _colsum(e * e) * (0.5 / D), _colsum(dyn * yn),
                _colsum(dx2 * (yn * gv)))

    return _matmul_rows(w_down, form="nn", a=act, tm=min(WIDE_TILE, x1.shape[0]), tk=D_FF, fn=fn,
                        rows=[(x1, D, 0), (target, D, 0)], consts=[g, gt], row_outs=[(F32, D), (BF16, D)],
                        acc_outs=[(1, D)] * 3, name=name)


def _up_dx_tail(dup, w_up, x1, dx2, ym, g_pre2, sc2, g_post, gt, name):
    def fn(dh, xv, dov, ymv, g2v, scv, gv, gtv):
        xn, r = _rms_stats(xv)
        dyn = dh * (1.0 + scv)
        dx1 = dov + _rms_bwd(dyn * g2v, xn, r)
        yn, r2 = _rms_stats(ymv)
        dynm = dx1 * gtv
        return (dx1, _rms_bwd(dynm * gv, yn, r2), _colsum(dh), _colsum(dh * (xn * g2v)), _colsum(dyn * xn),
                _colsum(dynm * yn), _colsum(dx1 * (yn * gv)))

    return _matmul_rows(w_up, form="nt", a=dup, tm=min(FUSED_TILE, x1.shape[0]), tk=2 * D_FF, fn=fn,
                        rows=[(x1, D, 0), (dx2, D, 0), (ym, D, 0)], consts=[g_pre2, sc2, g_post, gt],
                        row_outs=[(F32, D), (BF16, D)], acc_outs=[(1, D)] * 5, name=name)


def _mix_dx_gates(dym, w_mix_o, a, cb, z, name):
    def fn(dy, av, cv, ga0, ga1, gb0, gb1):
        sa = _sigmoid(jnp.concatenate([ga0, ga1], axis=1))
        sb = _sigmoid(jnp.concatenate([gb0, gb1], axis=1))
        dcb = dy * sb
        dga = dy * av * (sa * (1.0 - sa))
        dgb = dy * cv * (sb * (1.0 - sb))
        return dy * sa, dcb, dga, dgb, _colsum(dcb), _colsum(dga), _colsum(dgb)

    return _matmul_rows(w_mix_o, form="nt", a=dym, tm=min(FUSED_TILE, a.shape[0]), tk=D, fn=fn,
                        rows=[(a, D, 0), (cb, D, 0)] + _gates(z), consts=[], row_outs=[(BF16, D)] * 4,
                        acc_outs=[(1, D)] * 3, name=name)


def _local_step(x, target, mod, w_in, late, small, dist=None):
    sh_m, sc_m, gt_m, sh_f, sc_f, gt_f = mod
    t = x.shape[0]
    tmm = min(1024, t)
    late_swapped = [SWAPPED[n] for n in LATE]

    h1 = _pre_norm(x, small["g_pre_mix"], sc_m, sh_m, "pre_norm_mix")
    if callable(w_in):
        w_in = w_in(h1)
    z = _matmul(h1, w_in, form="nn", out_dtype=F32, tm=min(FUSED_TILE, t), tn=D_IN, tk=D, bias=small["b_in"], name="mm_in")
    conv = (z, small["w_dw_conv"], small["b_dw_conv"], small["g_conv_ln"], small["b_conv_ln"], "conv_branch")
    if dist is None:
        att = _attention(z, small["gen"], "attention")
        u1, u3 = _conv_branch(*conv)
        bufs = dict(late)
    else:
        mid = [n for n in LATE if n != "down"]
        mid_swapped = [SWAPPED[n] for n in mid]
        att, landed = _attention(z, small["gen"], "attention",
                                 carried=_carry_gather_ici([late[n] for n in mid], mid_swapped))
        (u1, u3), gathered = _conv_branch(*conv, carried=_carry_gather_forward(landed, mid_swapped))
        bufs = dict(zip(mid, gathered))
        bufs["down"] = late["down"]
    w = _weight_views(bufs)
    w["in"] = w_in
    a = _matmul(att, w["attn_o"], form="nn", out_dtype=F32, tm=tmm, tn=512, tk=512, name="mm_attn_o")
    cb = _matmul(u3, w["conv_o"], form="nn", out_dtype=F32, tm=tmm, tn=512, tk=512, bias=small["b_conv_o"], name="mm_conv_o")
    ym, y, x1, h2 = _mix_out(a, cb, z, x, w["mix_o"], small["g_post_mix"], gt_m, small["g_pre_ffn"], sc_f, sh_f, "mix_out")
    mm_up = dict(form="nn", out_dtype=F32, tm=min(FUSED_TILE, t), tn=2 * D_FF, tk=D, name="mm_up")
    ffn_act = (small["w_dw_ffn"], small["b_dw_ffn"], "ffn_act")
    if dist is None:
        up = _matmul(h2, w["up"], **mm_up)
        act = _ffn_act(up, *ffn_act)
    else:
        up, landed = _matmul(h2, w["up"], carried=_carry_gather_ici([late["down"]], [False]), **mm_up)
        act, down = _ffn_act(up, *ffn_act, carried=_carry_gather_forward(landed, [False]))
        w["down"] = down[0].reshape(D_FF, D)

    dx2, dyf, loss_cols, d_g_post_ffn, d_gt_f = _down_tail(act, w["down"], x1, target, small["g_post_ffn"], gt_f, "down_tail")
    dact = _matmul(dyf, w["down"], form="nt", out_dtype=F32, tm=tmm, tn=FF_BLOCK, tk=D, name="mm_down_dx")
    g_down = _matmul(act, dyf, form="tn", out_dtype=F32, tm=FF_BLOCK, tn=512, tk=t, name="mm_down_dw")
    dup, d_w_dw_ffn, d_b_dw_ffn = _ffn_act_bwd(dact, up, small["w_dw_ffn"], small["b_dw_ffn"], "ffn_act_bwd")
    dx1, dym, d_sh_f, d_sc_f, d_g_pre_ffn, d_g_post_mix, d_gt_m = _up_dx_tail(
        dup, w["up"], x1, dx2, ym, small["g_pre_ffn"], sc_f, small["g_post_mix"], gt_m, "up_dx_tail")
    g_up = _matmul(h2, dup, form="tn", out_dtype=F32, tm=512, tn=FF_BLOCK, tk=t, out_sharded=True, name="mm_up_dw")
    da, dcb, dgate_a, dgate_b, d_b_conv_o, sga, sgb = _mix_dx_gates(dym, w["mix_o"], a, cb, z, "mix_dx_gates")
    g_mix_o = _matmul(y, dym, form="tn", out_dtype=F32, tm=D, tn=512, tk=t, name="mm_mix_o_dw")
    datt = _matmul(da, w["attn_o"], form="nt", out_dtype=F32, tm=tmm, tn=512, tk=D, name="mm_attn_o_dx")
    g_attn_o = _matmul(att, da, form="tn", out_dtype=F32, tm=512, tn=256, tk=t, out_sharded=True, name="mm_attn_o_dw")
    du3 = _matmul(dcb, w["conv_o"], form="nt", out_dtype=F32, tm=tmm, tn=512, tk=D, name="mm_conv_o_dx")
    g_conv_o = _matmul(u3, dcb, form="tn", out_dtype=F32, tm=512, tn=256, tk=t, out_sharded=True, name="mm_conv_o_dw")
    big = {"attn_o": g_attn_o, "conv_o": g_conv_o, "mix_o": g_mix_o.reshape(4, 256, D),
           "up": g_up, "down": g_down.reshape(4, D_FF // 4, D)}
    conv_bwd = (du3, u1, z, small["w_dw_conv"], small["g_conv_ln"], small["b_conv_ln"], "conv_branch_bwd")
    in_dw = dict(form="tn", out_dtype=F32, tm=512, tn=1152, tk=t, out_sharded=True, name="mm_in_dw")
    in_dx = dict(form="nt", out_dtype=F32, tm=min(WIDE_TILE, t), tn=D, tk=D_IN, name="mm_in_dx")
    if dist is None:
        dglu, d_w_dw_conv, d_b_dw_conv, d_g_conv_ln, d_b_conv_ln, sglu = _conv_branch_bwd(*conv_bwd)
        dq, dk, dv, sq, sk, sv, dgen = _attention_bwd(z, datt, small["gen"], "attention_bwd")
        dz = jnp.concatenate([dq, dk, dv, dglu, dgate_a, dgate_b], axis=1)
        big["in"] = _matmul(h1, dz, **in_dw)
        dh1 = _matmul(dz, w_in, **in_dx)
    else:
        early = [big[n] for n in EARLY_GRADS]
        (dglu, d_w_dw_conv, d_b_dw_conv, d_g_conv_ln, d_b_conv_ln, sglu), recv = _conv_branch_bwd(
            *conv_bwd, carried=_carry_pair_exchange(early))
        parts = _pair_sums(EARLY_GRADS, early, recv, dist)
        (dq, dk, dv, sq, sk, sv, dgen), from_chips = _attention_bwd(
            z, datt, small["gen"], "attention_bwd",
            carried=_carry_chip_exchange(parts, [SWAPPED[n] for n in EARLY_GRADS]))
        halves = _reduce_halves(EARLY_GRADS, parts, from_chips, dist)
        dz = jnp.concatenate([dq, dk, dv, dglu, dgate_a, dgate_b], axis=1)
        g_in, shards = _matmul(h1, dz, carried=_carry_pair_share(halves), **in_dw)
        big = dict(zip(EARLY_GRADS, shards))
        recv_in = _run_carried(_carry_pair_exchange([g_in]), "pair_exchange_in")
        part_in = _pair_sums(("in",), [g_in], recv_in, dist)
        dh1, from_chips_in = _matmul(dz, w_in, carried=_carry_chip_exchange(part_in, [False]), **in_dx)
        half_in = _reduce_halves(("in",), part_in, from_chips_in, dist)
        big["in"] = _run_carried(_carry_pair_share(half_in), "pair_share_in")[0]
    d_b_in = jnp.concatenate([sq, sk, sv, sglu, sga, sgb], axis=1)
    grad_x, d_sh_m, d_sc_m, d_g_pre_mix = _pre_norm_bwd(dh1, x, dx1, small["g_pre_mix"], sc_m, "pre_norm_mix_bwd")

    dmod = [d_sh_m, d_sc_m, d_gt_m, d_sh_f, d_sc_f, d_gt_f]
    sm = {"g_pre_mix": d_g_pre_mix, "g_post_mix": d_g_post_mix, "b_in": d_b_in, "gen": dgen,
          "w_dw_conv": d_w_dw_conv, "b_dw_conv": d_b_dw_conv, "g_conv_ln": d_g_conv_ln, "b_conv_ln": d_b_conv_ln,
          "b_conv_o": d_b_conv_o, "g_pre_ffn": d_g_pre_ffn, "g_post_ffn": d_g_post_ffn,
          "w_dw_ffn": d_w_dw_ffn, "b_dw_ffn": d_b_dw_ffn}
    return loss_cols, grad_x, dmod, big, sm


BIG = ("in", "attn_o", "conv_o", "mix_o", "up", "down")
SWAPPED = {"in": False, "attn_o": False, "conv_o": False, "mix_o": False, "up": True, "down": False}
SMALL_ORDER = ("b_ada", "g_pre_mix", "g_post_mix", "b_in", "rel_bias", "b_dw_conv", "g_conv_ln", "b_conv_ln",
               "b_conv_o", "g_pre_ffn", "g_post_ffn", "b_dw_ffn", "w_dw_conv", "w_dw_ffn")


def kernel(x, c, w_ada, b_ada, g_pre_mix, g_post_mix, w_in, b_in, rel_bias, w_attn_o, w_dw_conv, b_dw_conv, g_conv_ln, b_conv_ln, w_conv_o, b_conv_o, w_mix_o, g_pre_ffn, g_post_ffn, w_up, w_dw_ffn, b_dw_ffn, w_down, loss_target, m_w_ada, m_b_ada, m_g_pre_mix, m_g_post_mix, m_w_in, m_b_in, m_rel_bias, m_w_attn_o, m_w_dw_conv, m_b_dw_conv, m_g_conv_ln, m_b_conv_ln, m_w_conv_o, m_b_conv_o, m_w_mix_o, m_g_pre_ffn, m_g_post_ffn, m_w_up, m_w_dw_ffn, m_b_dw_ffn, m_w_down, v_w_ada, v_b_ada, v_g_pre_mix, v_g_post_mix, v_w_in, v_b_in, v_rel_bias, v_w_attn_o, v_w_dw_conv, v_b_dw_conv, v_g_conv_ln, v_b_conv_ln, v_w_conv_o, v_b_conv_o, v_w_mix_o, v_g_pre_ffn, v_g_post_ffn, v_w_up, v_w_dw_ffn, v_b_dw_ffn, v_w_down):
    given = dict(locals())
    ax, ay, ac = lax.axis_index("x"), lax.axis_index("y"), lax.axis_index("c")
    shard = 2 * ax + ay
    me = 4 * ax + 2 * ay + ac
    xs, target = x[0], loss_target[0]

    slots = {sw: _slot(ax, ay, sw).astype(jnp.int32).reshape(1) for sw in (False, True)}
    own = {"in": _to_bf16_slot(w_in[0], slots[False], "cast_in")}

    c_pad = jnp.pad(c, ((0, 7), (0, 0)))
    c_g, _ = _allgather8(c_pad, "gather_c")
    c_all = c_g[:, 0, :]
    b_ada_shard = lax.dynamic_slice(b_ada, (0, shard * 1536), (1, 1536))
    mod_shard, c_act = _ada_fwd(c_all, w_ada[0], b_ada_shard, "ada_fwd")
    small_in = [jnp.pad(mod_shard, ((0, 8), (0, 0))),
                jnp.pad(w_dw_conv[0], ((0, 1), (0, 0))),
                jnp.pad(w_dw_ffn[0], ((0, 13), (0, 0)))]
    mod_g, wdc_g, wdf_g = _gather_shards(small_in, [False, False, True], "gather_small")
    mod_all = jnp.transpose(mod_g[:, :8, :], (1, 0, 2)).reshape(8, 6 * D)
    in_send, in_recv, in_flight, token = _gather_ici_start(own["in"], mod_g, False, "gather_w_in_start")

    def w_in_ready(after):
        landed = _gather_ici_wait(in_send, in_recv, in_flight, after, False, "gather_w_in_wait")
        return _run_carried(_carry_gather_forward([landed], [False]), "gather_forward_in")[0]

    for n in LATE:
        own[n] = _to_bf16_slot(given["w_" + n][0], slots[SWAPPED[n]], "cast_" + n, after=token)
    mod_row = lax.dynamic_slice(mod_all, (me, 0), (1, 6 * D)) + token[0:1, 0:1]
    mod = [mod_row[:, k * D:(k + 1) * D] for k in range(6)]

    core = ac.astype(jnp.int32).reshape(1)
    dist = {"core": core, "slots": slots}

    sel = jnp.asarray(_toeplitz_map())
    rel_pad = jnp.pad(rel_bias[0], ((0, 0), (0, REL_PAD - (2 * MAX_REL + 1))))
    gen = _select_call(rel_pad, sel.T.astype(BF16), "bias_rows")
    small = {"g_pre_mix": g_pre_mix, "g_post_mix": g_post_mix, "b_in": b_in, "gen": gen,
             "w_dw_conv": _unshard_cols(wdc_g[:, :CONV_K, :]), "b_dw_conv": b_dw_conv, "g_conv_ln": g_conv_ln,
             "b_conv_ln": b_conv_ln, "b_conv_o": b_conv_o, "g_pre_ffn": g_pre_ffn, "g_post_ffn": g_post_ffn,
             "w_dw_ffn": _unshard_cols(wdf_g[:, :FFN_K, :]), "b_dw_ffn": _ff_swap(b_dw_ffn)}

    loss_cols, grad_x, dmod, reduced, sm = _local_step(xs, target, mod, w_in_ready, {n: own[n] for n in LATE}, small, dist)

    d_rel = _select_call(sm["gen"], sel.astype(BF16), "bias_fold")[:, :2 * MAX_REL + 1]
    small_grads = {"g_pre_mix": sm["g_pre_mix"], "g_post_mix": sm["g_post_mix"], "b_in": sm["b_in"], "rel_bias": d_rel[None],
                   "b_dw_conv": sm["b_dw_conv"], "g_conv_ln": sm["g_conv_ln"], "b_conv_ln": sm["b_conv_ln"],
                   "b_conv_o": sm["b_conv_o"], "g_pre_ffn": sm["g_pre_ffn"], "g_post_ffn": sm["g_post_ffn"],
                   "b_dw_ffn": _ff_swap(sm["b_dw_ffn"]), "w_dw_conv": sm["w_dw_conv"], "w_dw_ffn": _ff_swap(sm["w_dw_ffn"])}
    order = [n for n in SMALL_ORDER if n != "b_ada"]
    packed, offs = _pack([jnp.concatenate(dmod, axis=1)] + [small_grads[n] for n in order] + [loss_cols])
    mine = lax.dynamic_update_slice(jnp.zeros((8,) + packed.shape, F32), packed[None], (me, 0, 0))
    sg_send, sg_recv, sg_flight, sg_token = _all8_start(mine, "gather_small_grads_start")

    out = {}
    for n in BIG:
        g, dl, nm, nv = _adamw(given["w_" + n][0], reduced[n], given["m_w_" + n][0], given["v_w_" + n][0],
                               "adamw_" + n, sg_token)
        out["grad_w_" + n], out["delta_w_" + n], out["new_m_w_" + n], out["new_v_w_" + n] = g[None], dl[None], nm[None], nv[None]
    every = _all8_wait(sg_send, sg_recv, sg_flight, out["delta_w_in"], "gather_small_grads_wait")
    total = _sum8(every, "sum_small_grads")
    loss = jnp.sum(total[offs[-1]])
    offs = offs[:-1]
    dmod_all = every[:, 0:6, :].reshape(8, 6 * D)
    full_shapes = {n: given[n].shape for n in order}
    full_shapes["w_dw_conv"], full_shapes["w_dw_ffn"] = (1, CONV_K, 512), (1, FFN_K, 2 * D_FF)
    sums = dict(zip(order, _unpack(total, offs[1:], [full_shapes[n] for n in order])))
    sums["b_ada"] = total[0:6].reshape(1, 6 * D)
    sums["w_dw_conv"] = lax.dynamic_slice(sums["w_dw_conv"], (0, 0, shard * 128), (1, CONV_K, 128))
    sums["w_dw_ffn"] = lax.dynamic_slice(sums["w_dw_ffn"], (0, 0, shard * FF_BLOCK), (1, FFN_K, FF_BLOCK))

    upd = dict(zip(SMALL_ORDER, _adamw_many(
        [(given[n], sums[n], given["m_" + n], given["v_" + n]) for n in SMALL_ORDER], "adamw_small")))

    dmod_shard = lax.dynamic_slice(dmod_all, (0, shard * 1536), (8, 1536))
    ada = _ada_bwd_adamw(c_act.T, dmod_shard, w_ada[0], m_w_ada[0], v_w_ada[0], "ada_bwd_adamw")

    out.update({"grad_w_ada": ada[0][None], "delta_w_ada": ada[1][None], "new_m_w_ada": ada[2][None],
                "new_v_w_ada": ada[3][None]})
    for n in SMALL_ORDER:
        out["grad_" + n], out["delta_" + n], out["new_m_" + n], out["new_v_" + n] = sums[n], *upd[n]

    weights = ["w_ada", "b_ada", "g_pre_mix", "g_post_mix", "w_in", "b_in", "rel_bias", "w_attn_o", "w_dw_conv", "b_dw_conv",
               "g_conv_ln", "b_conv_ln", "w_conv_o", "b_conv_o", "w_mix_o", "g_pre_ffn", "g_post_ffn", "w_up", "w_dw_ffn",
               "b_dw_ffn", "w_down"]
    return (loss, grad_x[None], *[out["grad_" + n] for n in weights], *[out["delta_" + n] for n in weights],
            *[out["new_m_" + n] for n in weights], *[out["new_v_" + n] for n in weights])
```

```python
import functools
import math

import numpy as np
import jax
import jax.numpy as jnp
from jax import lax
from jax.experimental import pallas as pl
from jax.experimental.pallas import tpu as pltpu

F32, BF16 = jnp.float32, jnp.bfloat16
MESH = pl.DeviceIdType.MESH

D = 1024
D_IN = 4608
D_FF = 2816
CONV_K = 31
FFN_K = 3
N_HEADS = 8
CHUNK = 64
LEFT_CHUNKS = 8
MAX_REL = 128
EPS = 1e-6
NEG_INF = -1e30
Q_TILE = 256
WINDOW = Q_TILE + LEFT_CHUNKS * CHUNK
STEP_ROWS = 256
REL_PAD = 384
TOEP = 1024
ROW_TILE = 256
VMEM_LIMIT = 60 * 1024 * 1024

ADAM_LR, ADAM_B1, ADAM_B2, ADAM_EPS, ADAM_WD, ADAM_STEP = 0.001, 0.9, 0.999, 1e-08, 0.01, 10


def _params(sem=None):
    return pltpu.CompilerParams(dimension_semantics=sem, vmem_limit_bytes=VMEM_LIMIT)


def _sds(shape, dtype):
    return jax.ShapeDtypeStruct(tuple(shape), dtype)


ANY = pl.BlockSpec(memory_space=pl.ANY)


class _Carried:
    def __init__(self, ins, out_shapes, aliases, n_sems, start, finish):
        self.ins, self.out_shapes, self.aliases = list(ins), list(out_shapes), dict(aliases)
        self.n_sems, self.start, self.finish = n_sems, start, finish


def _call(body, *, grid, in_specs, out_specs, out_shape, scratch_shapes, sem, name, args, carried=None):
    in_specs, out_specs, out_shape = list(in_specs), list(out_specs), list(out_shape)
    scratch_shapes = list(scratch_shapes)
    if carried is None:
        return pl.pallas_call(body, grid=grid, in_specs=in_specs, out_specs=out_specs, out_shape=out_shape,
                              scratch_shapes=scratch_shapes, compiler_params=_params(sem), name=name)(*args)
    n_in, n_out, n_scr = len(in_specs), len(out_specs), len(scratch_shapes)
    c_in, c_out = len(carried.ins), len(carried.out_shapes)

    def full(*refs):
        pos = [0]

        def take(k):
            part = refs[pos[0]:pos[0] + k]
            pos[0] += k
            return part

        ins, cins, outs, couts, scr = take(n_in), take(c_in), take(n_out), take(c_out), take(n_scr)
        send_sems, recv_sems = take(2)
        first = last = None
        for d, size in enumerate(grid):
            pid = pl.program_id(d)
            first = (pid == 0) if first is None else first & (pid == 0)
            last = (pid == size - 1) if last is None else last & (pid == size - 1)

        @pl.when(first)
        def _():
            carried.start(cins, couts, send_sems, recv_sems)

        body(*ins, *outs, *scr)

        @pl.when(last)
        def _():
            carried.finish(cins, couts, send_sems, recv_sems)

    sems = [pltpu.SemaphoreType.DMA((carried.n_sems,)), pltpu.SemaphoreType.DMA((carried.n_sems,))]
    return pl.pallas_call(
        full, grid=grid, in_specs=in_specs + [ANY] * c_in, out_specs=out_specs + [ANY] * c_out,
        out_shape=out_shape + carried.out_shapes, scratch_shapes=scratch_shapes + sems,
        input_output_aliases={n_in + k: n_out + v for k, v in carried.aliases.items()},
        compiler_params=_params(tuple("arbitrary" for _ in grid)), name=name,
    )(*args, *carried.ins)


def _run_carried(carried, name):
    c_in = len(carried.ins)

    def body(*refs):
        cins, couts = refs[:c_in], refs[c_in:c_in + len(carried.out_shapes)]
        send_sems, recv_sems = refs[-2:]
        carried.start(cins, couts, send_sems, recv_sems)
        carried.finish(cins, couts, send_sems, recv_sems)

    return pl.pallas_call(
        body, in_specs=[ANY] * c_in, out_specs=[ANY] * len(carried.out_shapes), out_shape=carried.out_shapes,
        scratch_shapes=[pltpu.SemaphoreType.DMA((carried.n_sems,)), pltpu.SemaphoreType.DMA((carried.n_sems,))],
        input_output_aliases=carried.aliases, name=name,
    )(*carried.ins)


def _matmul(a, b, *, form, out_dtype, tm, tn, tk, name, bias=None, add=None, out_sharded=False, carried=None):
    b3 = b.ndim == 3
    resident = 0
    if form == "nn":
        m, k = a.shape
        n = b.shape[0] * b.shape[2] if b3 else b.shape[1]
        dn = (((1,), (0,)), ((), ()))
        a_spec = pl.BlockSpec((tm, tk), lambda i, j, kk: (i, kk))
        if b3 and tn == n and tk == k:
            resident = b.shape[0]
            b_spec = pl.BlockSpec(b.shape, lambda i, j, kk: (0, 0, 0))
        else:
            b_spec = (pl.BlockSpec((None, tk, tn), lambda i, j, kk: (j, kk, 0)) if b3
                      else pl.BlockSpec((tk, tn), lambda i, j, kk: (kk, j)))
    elif form == "nt":
        m, k = a.shape
        n = b.shape[1] if b3 else b.shape[0]
        dn = (((1,), (1,)), ((), ()))
        a_spec = pl.BlockSpec((tm, tk), lambda i, j, kk: (i, kk))
        if b3 and tk == k:
            resident = b.shape[0]
            b_spec = pl.BlockSpec((resident, tn, b.shape[2]), lambda i, j, kk: (0, j, 0))
        else:
            b_spec = (pl.BlockSpec((None, tn, tk), lambda i, j, kk: (kk, j, 0)) if b3
                      else pl.BlockSpec((tn, tk), lambda i, j, kk: (j, kk)))
    else:
        k, m = a.shape
        n = b.shape[1]
        dn = (((0,), (0,)), ((), ()))
        a_spec = pl.BlockSpec((tk, tm), lambda i, j, kk: (kk, i))
        b_spec = pl.BlockSpec((tk, tn), lambda i, j, kk: (kk, j))
    assert m % tm == 0 and n % tn == 0 and k % tk == 0, (name, m, n, k, tm, tn, tk)
    nk = k // tk
    in_specs, args = [a_spec, b_spec], [a, b]
    if bias is not None:
        in_specs.append(pl.BlockSpec((1, tn), lambda i, j, kk: (0, j)))
        args.append(bias)
    if add is not None:
        in_specs.append(pl.BlockSpec((tm, tn), lambda i, j, kk: (i, j)))
        args.append(add)
    if out_sharded:
        out_shape = _sds((n // tn, m, tn), out_dtype)
        out_spec = pl.BlockSpec((None, tm, tn), lambda i, j, kk: (j, i, 0))
    else:
        out_shape = _sds((m, n), out_dtype)
        out_spec = pl.BlockSpec((tm, tn), lambda i, j, kk: (i, j))

    def body(*refs):
        a_ref, b_ref = refs[0], refs[1]
        pos = 2
        bias_ref = add_ref = None
        if bias is not None:
            bias_ref, pos = refs[pos], pos + 1
        if add is not None:
            add_ref, pos = refs[pos], pos + 1
        o_ref = refs[pos]
        if resident and form == "nn":
            ns = b_ref.shape[2]
            for s in range(resident):
                cols = slice(s * ns, (s + 1) * ns)
                ps = lax.dot_general(a_ref[...], b_ref[s], dn, preferred_element_type=F32)
                if bias_ref is not None:
                    ps = ps + bias_ref[:, cols]
                o_ref[:, cols] = ps.astype(o_ref.dtype)
            return
        if resident:
            ks = b_ref.shape[2]
            p = None
            for s in range(resident):
                ps = lax.dot_general(a_ref[:, s * ks:(s + 1) * ks], b_ref[s], dn, preferred_element_type=F32)
                p = ps if p is None else p + ps
        else:
            av, bv = a_ref[...], b_ref[...]
            if av.dtype != BF16:
                av = av.astype(BF16)
            if bv.dtype != BF16:
                bv = bv.astype(BF16)
            p = lax.dot_general(av, bv, dn, preferred_element_type=F32)

        def finish(acc):
            if bias_ref is not None:
                acc = acc + bias_ref[...]
            if add_ref is not None:
                acc = acc + add_ref[...]
            o_ref[...] = acc.astype(o_ref.dtype)

        if nk == 1:
            finish(p)
        else:
            acc_ref = refs[pos + 1]
            kk = pl.program_id(2)

            @pl.when(kk == 0)
            def _():
                acc_ref[...] = p

            @pl.when(kk > 0)
            def _():
                acc_ref[...] += p

            @pl.when(kk == nk - 1)
            def _():
                finish(acc_ref[...])

    res = _call(body, grid=(m // tm, n // tn, nk), in_specs=in_specs, out_specs=[out_spec], out_shape=[out_shape],
                scratch_shapes=[pltpu.VMEM((tm, tn), F32)] if nk > 1 else [],
                sem=("parallel", "parallel", "arbitrary"), name=name, args=args, carried=carried)
    return res[0] if carried is None else (res[0], res[1:])


def _rowcall(fn, rows, consts, row_outs, acc_outs, *, name, tm=ROW_TILE, col_grid=1):
    n_rows = rows[0][0].shape[0]
    assert n_rows % tm == 0
    grid = (col_grid, n_rows // tm)
    in_specs = [pl.BlockSpec((tm, w), functools.partial(lambda c, i, cb: (i, cb + c), cb=cb)) for _, w, cb in rows]
    in_specs += [pl.BlockSpec(k.shape, functools.partial(lambda c, i, nd: (0,) * nd, nd=k.ndim)) for k in consts]
    out_specs = [pl.BlockSpec((tm, w), lambda c, i: (i, c)) for _, _, _, w in row_outs]
    out_specs += [pl.BlockSpec((r, w), lambda c, i: (0, c)) for r, _, w in acc_outs]
    out_shape = [_sds((nr, nc), dt) for nr, nc, dt, _ in row_outs] + [_sds((r, nc), F32) for r, nc, _ in acc_outs]
    n_in, n_ro = len(rows) + len(consts), len(row_outs)

    def body(*refs):
        res = fn(*[r[...] for r in refs[:n_in]])
        if not isinstance(res, (tuple, list)):
            res = (res,)
        outs = refs[n_in:]
        for o_ref, val in zip(outs[:n_ro], res[:n_ro]):
            o_ref[...] = val.astype(o_ref.dtype)
        if acc_outs:
            first = pl.program_id(1) == 0

            @pl.when(first)
            def _():
                for o_ref, val in zip(outs[n_ro:], res[n_ro:]):
                    o_ref[...] = val

            @pl.when(jnp.logical_not(first))
            def _():
                for o_ref, val in zip(outs[n_ro:], res[n_ro:]):
                    o_ref[...] += val

    out = pl.pallas_call(
        body, grid=grid, in_specs=in_specs, out_specs=out_specs, out_shape=out_shape,
        compiler_params=_params(("arbitrary", "arbitrary")), name=name,
    )(*[r[0] for r in rows], *consts)
    return out


def _matmul_rows(b, *, form, tm, tk, fn, rows, consts, row_outs, acc_outs, name, a=None, a_rows=None, a_fn=None,
                 carried=None):
    b3 = b.ndim == 3
    resident = 0
    if form == "nn":
        k, n = b.shape
        b_spec = pl.BlockSpec((tk, n), lambda i, kk: (kk, 0))
        dn = (((1,), (0,)), ((), ()))
    else:
        n = b.shape[1] if b3 else b.shape[0]
        k = b.shape[0] * b.shape[2] if b3 else b.shape[1]
        if b3 and tk == k:
            resident = b.shape[0]
            b_spec = pl.BlockSpec(b.shape, lambda i, kk: (0, 0, 0))
        else:
            b_spec = (pl.BlockSpec((None, n, tk), lambda i, kk: (kk, 0, 0)) if b3
                      else pl.BlockSpec((n, tk), lambda i, kk: (0, kk)))
        dn = (((1,), (1,)), ((), ()))
    nk = k // tk
    lhs_in = [(a, tk, 0)] if a is not None else list(a_rows)
    assert a is not None or nk == 1
    m = lhs_in[0][0].shape[0]
    n_lhs = len(lhs_in)
    in_specs = [pl.BlockSpec((tm, tk), lambda i, kk: (i, kk))] if a is not None else [
        pl.BlockSpec((tm, w), functools.partial(lambda i, kk, cb: (i, cb), cb=cb)) for _, w, cb in a_rows]
    in_specs.append(b_spec)
    in_specs += [pl.BlockSpec((tm, w), functools.partial(lambda i, kk, cb: (i, cb), cb=cb)) for _, w, cb in rows]
    in_specs += [pl.BlockSpec(c.shape, functools.partial(lambda i, kk, nd: (0,) * nd, nd=c.ndim)) for c in consts]
    out_specs = [pl.BlockSpec((tm, w), lambda i, kk: (i, 0)) for _, w in row_outs]
    out_specs += [pl.BlockSpec((r, w), lambda i, kk: (0, 0)) for r, w in acc_outs]
    out_shape = [_sds((m, w), dt) for dt, w in row_outs] + [_sds((r, w), F32) for r, w in acc_outs]
    n_rows, n_consts, n_ro, n_acc = len(rows), len(consts), len(row_outs), len(acc_outs)

    def body(*refs):
        pos = n_lhs + 1
        row_refs, const_refs = refs[pos:pos + n_rows], refs[pos + n_rows:pos + n_rows + n_consts]
        pos += n_rows + n_consts
        out_refs, acc_refs = refs[pos:pos + n_ro], refs[pos + n_ro:pos + n_ro + n_acc]
        i, kk = pl.program_id(0), pl.program_id(1)
        if resident:
            b_ref, ks, p = refs[n_lhs], b.shape[2], None
            for s in range(resident):
                ps = lax.dot_general(refs[0][:, s * ks:(s + 1) * ks], b_ref[s], dn, preferred_element_type=F32)
                p = ps if p is None else p + ps
        else:
            lhs = refs[0][...] if a is not None else a_fn(*[r[...] for r in refs[:n_lhs]]).astype(BF16)
            p = lax.dot_general(lhs, refs[n_lhs][...], dn, preferred_element_type=F32)

        def finish(acc):
            extra = [r[...] for r in row_refs] + [c[...] for c in const_refs]
            res = fn(acc, lhs, *extra) if a is None else fn(acc, *extra)
            for o_ref, val in zip(out_refs, res[:n_ro]):
                o_ref[...] = val.astype(o_ref.dtype)
            if n_acc:
                @pl.when(i == 0)
                def _():
                    for o_ref, val in zip(acc_refs, res[n_ro:]):
                        o_ref[...] = val

                @pl.when(i > 0)
                def _():
                    for o_ref, val in zip(acc_refs, res[n_ro:]):
                        o_ref[...] += val

        if nk == 1:
            finish(p)
        else:
            acc_ref = refs[pos + n_ro + n_acc]

            @pl.when(kk == 0)
            def _():
                acc_ref[...] = p

            @pl.when(kk > 0)
            def _():
                acc_ref[...] += p

            @pl.when(kk == nk - 1)
            def _():
                finish(acc_ref[...])

    res = _call(body, grid=(m // tm, nk), in_specs=in_specs, out_specs=out_specs, out_shape=out_shape,
                scratch_shapes=[pltpu.VMEM((tm, n), F32)] if nk > 1 else [], sem=("arbitrary", "arbitrary"),
                name=name, args=[r[0] for r in lhs_in] + [b] + [r[0] for r in rows] + list(consts), carried=carried)
    own = n_ro + n_acc
    return res[:own] if carried is None else (res[:own], res[own:])


def _colsum(v):
    return jnp.sum(v, axis=0, keepdims=True)


def _sigmoid(v):
    return 1.0 / (1.0 + jnp.exp(-v))


_GELU_C = math.sqrt(2.0 / math.pi)


def _gelu(v):
    return 0.5 * v * (1.0 + jnp.tanh(_GELU_C * (v + 0.044715 * (v * v * v))))


def _gelu_and_grad(v):
    th = jnp.tanh(_GELU_C * (v + 0.044715 * (v * v * v)))
    g = 0.5 * v * (1.0 + th)
    dg = 0.5 * (1.0 + th) + 0.5 * v * (1.0 - th * th) * (_GELU_C * (1.0 + 3.0 * 0.044715 * (v * v)))
    return g, dg


def _rms_stats(v):
    r = lax.rsqrt(jnp.mean(v * v, axis=-1, keepdims=True) + EPS)
    return v * r, r


def _rms_bwd(dn, vn, r):
    return r * (dn - vn * jnp.mean(dn * vn, axis=-1, keepdims=True))


def _pre_norm(x, g, sc, sh, name):
    def fn(xv, gv, scv, shv):
        xn, _ = _rms_stats(xv)
        return (xn * gv) * (1.0 + scv) + shv
    return _rowcall(fn, [(x, D, 0)], [g, sc, sh], [(x.shape[0], D, BF16, D)], [], name=name)[0]


def _pre_norm_bwd(dh, x, dx_other, g, sc, name):
    def fn(dhv, xv, dov, gv, scv):
        xn, r = _rms_stats(xv)
        yn = xn * gv
        dyn = dhv * (1.0 + scv)
        dx = _rms_bwd(dyn * gv, xn, r)
        return dov + dx, _colsum(dhv), _colsum(dhv * yn), _colsum(dyn * xn)
    t = x.shape[0]
    return _rowcall(fn, [(dh, D, 0), (x, D, 0), (dx_other, D, 0)], [g, sc], [(t, D, F32, D)],
                    [(1, D, D)] * 3, name=name)


CONV_HALO = 32


def _layer_norm_parts(u):
    mu = jnp.mean(u, axis=-1, keepdims=True)
    d = u - mu
    r = lax.rsqrt(jnp.mean(d * d, axis=-1, keepdims=True) + EPS)
    return d * r, r


LANES = 128
SUBLANE_ROWS = 8
CONV_ROWS = 64


def _lanes(c):
    return slice(c * LANES, (c + 1) * LANES)


def _conv_branch(z, w_dw, b_dw, g_ln, b_ln, name, tm=ROW_TILE, carried=None):
    t = z.shape[0]
    per = tm // CONV_HALO
    n_chunks = 512 // LANES

    def body(ga_ref, gb_ref, gah_ref, gbh_ref, w_ref, b_ref, g_ref, bl_ref, u1_ref, u3_ref, scr):
        i = pl.program_id(0)
        u0h = jnp.where(i > 0, gah_ref[...] * _sigmoid(gbh_ref[...]), 0.0)
        u0 = ga_ref[...] * _sigmoid(gb_ref[...])
        for c in range(n_chunks):
            scr[c, 0:CONV_HALO, :] = u0h[:, _lanes(c)]
            scr[c, CONV_HALO:CONV_HALO + tm, :] = u0[:, _lanes(c)]
        for c in range(n_chunks):
            for r0 in range(0, tm, CONV_ROWS):
                acc = jnp.zeros((CONV_ROWS, LANES), F32) + b_ref[:, _lanes(c)]
                for j in range(CONV_K):
                    acc = acc + w_ref[j:j + 1, _lanes(c)] * scr[c, pl.ds(r0 + CONV_HALO - (CONV_K - 1) + j, CONV_ROWS), :]
                u1_ref[r0:r0 + CONV_ROWS, _lanes(c)] = acc
        xh, _ = _layer_norm_parts(u1_ref[...])
        u2 = xh * g_ref[...] + bl_ref[...]
        u3_ref[...] = (u2 * _sigmoid(u2)).astype(BF16)

    cur = lambda cb: pl.BlockSpec((tm, 512), lambda i: (i, cb))
    halo = lambda cb: pl.BlockSpec((CONV_HALO, 512), lambda i: (jnp.maximum(i * per - 1, 0), cb))
    whole = lambda a: pl.BlockSpec(a.shape, lambda i: (0, 0))
    res = _call(
        body, grid=(t // tm,),
        in_specs=[cur(3), cur(4), halo(3), halo(4), whole(w_dw), whole(b_dw), whole(g_ln), whole(b_ln)],
        out_specs=[pl.BlockSpec((tm, 512), lambda i: (i, 0))] * 2,
        out_shape=[_sds((t, 512), F32), _sds((t, 512), BF16)],
        scratch_shapes=[pltpu.VMEM((n_chunks, CONV_HALO + tm, LANES), F32)],
        sem=("arbitrary",), name=name, args=(z, z, z, z, w_dw, b_dw, g_ln, b_ln), carried=carried)
    return res[:2] if carried is None else (res[:2], res[2:])


def _conv_branch_bwd(du3, u1, z, w_dw, g_ln, b_ln, name, tm=ROW_TILE, carried=None):
    t = z.shape[0]
    per = tm // CONV_HALO
    last = t // tm - 1
    n_chunks = 512 // LANES

    def du1_of(du3v, u1v, g, b):
        xh, r = _layer_norm_parts(u1v)
        u2 = xh * g + b
        s = _sigmoid(u2)
        du2 = du3v * (s * (1.0 + u2 * (1.0 - s)))
        dxh = du2 * g
        du1 = r * (dxh - jnp.mean(dxh, axis=-1, keepdims=True) - xh * jnp.mean(dxh * xh, axis=-1, keepdims=True))
        return du1, du2, xh

    def body(d_ref, u_ref, dn_ref, un_ref, ga_ref, gb_ref, gah_ref, gbh_ref, w_ref, g_ref, bl_ref,
             dglu_ref, dw_ref, dbdw_ref, dg_ref, dbl_ref, dbin_ref, scr, scd):
        i = pl.program_id(0)
        g, b = g_ref[...], bl_ref[...]
        du1, du2, xh = du1_of(d_ref[...], u_ref[...], g, b)
        du1n, _, _ = du1_of(dn_ref[...], un_ref[...], g, b)
        du1n = jnp.where(i < last, du1n, 0.0)
        sgb = _sigmoid(gb_ref[...])
        ga = ga_ref[...]
        u0 = ga * sgb
        u0h = jnp.where(i > 0, gah_ref[...] * _sigmoid(gbh_ref[...]), 0.0)
        for c in range(n_chunks):
            scd[c, 0:tm, :] = du1[:, _lanes(c)]
            scd[c, tm:tm + CONV_HALO, :] = du1n[:, _lanes(c)]
            scr[c, 0:CONV_HALO, :] = u0h[:, _lanes(c)]
            scr[c, CONV_HALO:CONV_HALO + tm, :] = u0[:, _lanes(c)]

        @pl.when(i == 0)
        def _():
            for ref in (dw_ref, dbdw_ref, dg_ref, dbl_ref, dbin_ref):
                ref[...] = jnp.zeros_like(ref)

        dsg = ga * (sgb * (1.0 - sgb))
        for c in range(n_chunks):
            gate = slice(512 + c * LANES, 512 + (c + 1) * LANES)
            for r0 in range(0, tm, CONV_ROWS):
                rows = slice(r0, r0 + CONV_ROWS)
                du0 = jnp.zeros((CONV_ROWS, LANES), F32)
                for j in range(CONV_K):
                    du0 = du0 + w_ref[j:j + 1, _lanes(c)] * scd[c, pl.ds(r0 + CONV_K - 1 - j, CONV_ROWS), :]
                dga = du0 * sgb[rows, _lanes(c)]
                dgb = du0 * dsg[rows, _lanes(c)]
                dglu_ref[rows, _lanes(c)] = dga.astype(BF16)
                dglu_ref[rows, gate] = dgb.astype(BF16)
                dbin_ref[:, _lanes(c)] += _colsum(dga)
                dbin_ref[:, gate] += _colsum(dgb)
            for j in range(CONV_K):
                dwj = jnp.zeros((SUBLANE_ROWS, LANES), F32)
                for r0 in range(0, tm, CONV_ROWS):
                    prod = (scd[c, pl.ds(r0, CONV_ROWS), :]
                            * scr[c, pl.ds(r0 + CONV_HALO - (CONV_K - 1) + j, CONV_ROWS), :])
                    dwj = dwj + jnp.sum(prod.reshape(CONV_ROWS // SUBLANE_ROWS, SUBLANE_ROWS, LANES), axis=0)
                dw_ref[j:j + 1, _lanes(c)] += _colsum(dwj)
        dbdw_ref[...] += _colsum(du1)
        dg_ref[...] += _colsum(du2 * xh)
        dbl_ref[...] += _colsum(du2)

    cur = lambda cb: pl.BlockSpec((tm, 512), lambda i: (i, cb))
    prev = lambda cb: pl.BlockSpec((CONV_HALO, 512), lambda i: (jnp.maximum(i * per - 1, 0), cb))
    nxt = pl.BlockSpec((CONV_HALO, 512), lambda i: (jnp.minimum((i + 1) * per, t // CONV_HALO - 1), 0))
    whole = lambda a: pl.BlockSpec(a.shape, lambda i: (0, 0))
    acc = lambda r, w: pl.BlockSpec((r, w), lambda i: (0, 0))
    res = _call(
        body, grid=(t // tm,),
        in_specs=[cur(0), cur(0), nxt, nxt, cur(3), cur(4), prev(3), prev(4), whole(w_dw), whole(g_ln), whole(b_ln)],
        out_specs=[pl.BlockSpec((tm, 1024), lambda i: (i, 0)), acc(CONV_K, 512), acc(1, 512), acc(1, 512),
                   acc(1, 512), acc(1, 1024)],
        out_shape=[_sds((t, 1024), BF16), _sds((CONV_K, 512), F32), _sds((1, 512), F32), _sds((1, 512), F32),
                   _sds((1, 512), F32), _sds((1, 1024), F32)],
        scratch_shapes=[pltpu.VMEM((n_chunks, CONV_HALO + tm, LANES), F32),
                        pltpu.VMEM((n_chunks, tm + CONV_HALO, LANES), F32)],
        sem=("arbitrary",), name=name, args=(du3, u1, du3, u1, z, z, z, z, w_dw, g_ln, b_ln), carried=carried)
    return res[:6] if carried is None else (res[:6], res[6:])


FF_BLOCK = D_FF // 2
FF_HALO = 8
FF_CHUNKS = FF_BLOCK // LANES


FF_ROWS = 64
FF_EXT_ROWS = 88


def _ffn_conv(w_ref, b_ref, scr, k, rows, r0=0):
    acc = b_ref[:, _lanes(k)] + w_ref[0:1, _lanes(k)] * scr[k, pl.ds(r0 + FF_HALO - 2, rows), :]
    acc = acc + w_ref[1:2, _lanes(k)] * scr[k, pl.ds(r0 + FF_HALO - 1, rows), :]
    return acc + w_ref[2:3, _lanes(k)] * scr[k, pl.ds(r0 + FF_HALO, rows), :]


def _ffn_act(up, w3, b3, name, tm=ROW_TILE, carried=None):
    t = up.shape[0]
    per = tm // FF_HALO
    wide = 2 * FF_BLOCK

    def body(u_ref, uh_ref, w_ref, b_ref, o_ref, scr):
        i = pl.program_id(1)
        for k in range(2 * FF_CHUNKS):
            scr[k, 0:FF_HALO, :] = jnp.where(i > 0, uh_ref[:, _lanes(k)], 0.0)
            scr[k, FF_HALO:FF_HALO + tm, :] = u_ref[:, _lanes(k)]
        for cc in range(FF_CHUNKS):
            for r0 in range(0, tm, FF_ROWS):
                val = _ffn_conv(w_ref, b_ref, scr, cc, FF_ROWS, r0)
                gate = _ffn_conv(w_ref, b_ref, scr, FF_CHUNKS + cc, FF_ROWS, r0)
                o_ref[r0:r0 + FF_ROWS, _lanes(cc)] = (_gelu(gate) * val).astype(BF16)

    res = _call(
        body, grid=(2, t // tm),
        in_specs=[pl.BlockSpec((tm, wide), lambda c, i: (i, c)),
                  pl.BlockSpec((FF_HALO, wide), lambda c, i: (jnp.maximum(i * per - 1, 0), c)),
                  pl.BlockSpec((FFN_K, wide), lambda c, i: (0, c)),
                  pl.BlockSpec((1, wide), lambda c, i: (0, c))],
        out_specs=[pl.BlockSpec((tm, FF_BLOCK), lambda c, i: (i, c))],
        out_shape=[_sds((t, D_FF), BF16)],
        scratch_shapes=[pltpu.VMEM((2 * FF_CHUNKS, FF_HALO + tm, LANES), F32)],
        sem=("arbitrary", "arbitrary"), name=name, args=(up, up, w3, b3), carried=carried)
    return res[0] if carried is None else (res[0], res[1:])


def _ffn_act_bwd(dact, up, w3, b3, name, tm=ROW_TILE):
    t = up.shape[0]
    per = tm // FF_HALO
    wide = 2 * FF_BLOCK
    last = t // tm - 1
    ext = tm + FF_HALO

    def body(u_ref, up_ref, un_ref, d_ref, dn_ref, w_ref, b_ref, o_ref, dw_ref, db_ref, scr, scd):
        i = pl.program_id(1)
        for k in range(2 * FF_CHUNKS):
            scr[k, 0:FF_HALO, :] = jnp.where(i > 0, up_ref[:, _lanes(k)], 0.0)
            scr[k, FF_HALO:FF_HALO + tm, :] = u_ref[:, _lanes(k)]
            scr[k, FF_HALO + tm:FF_HALO + ext, :] = un_ref[:, _lanes(k)]
        dn = jnp.where(i < last, dn_ref[...], 0.0)

        @pl.when(i == 0)
        def _():
            dw_ref[...] = jnp.zeros_like(dw_ref)
            db_ref[...] = jnp.zeros_like(db_ref)

        for cc in range(FF_CHUNKS):
            gc = FF_CHUNKS + cc
            for r0 in range(0, ext, FF_EXT_ROWS):
                rows = pl.ds(r0, FF_EXT_ROWS)
                val = _ffn_conv(w_ref, b_ref, scr, cc, FF_EXT_ROWS, r0)
                gel, dgel = _gelu_and_grad(_ffn_conv(w_ref, b_ref, scr, gc, FF_EXT_ROWS, r0))
                da = d_ref[r0:r0 + FF_EXT_ROWS, _lanes(cc)] if r0 + FF_EXT_ROWS <= tm else jnp.concatenate(
                    [d_ref[r0:tm, _lanes(cc)], dn[:, _lanes(cc)]], axis=0)
                scd[cc, rows, :] = da * gel
                scd[gc, rows, :] = da * val * dgel
            for k in (cc, gc):
                dwk = [jnp.zeros((SUBLANE_ROWS, LANES), F32) for _ in range(FFN_K)]
                dbk = jnp.zeros((SUBLANE_ROWS, LANES), F32)
                for r0 in range(0, tm, FF_ROWS):
                    shifted = [scd[k, pl.ds(r0 + FFN_K - 1 - j, FF_ROWS), :] for j in range(FFN_K)]
                    ucur = scr[k, pl.ds(r0 + FF_HALO, FF_ROWS), :]
                    o_ref[r0:r0 + FF_ROWS, _lanes(k)] = (
                        w_ref[0:1, _lanes(k)] * shifted[0] + w_ref[1:2, _lanes(k)] * shifted[1]
                        + w_ref[2:3, _lanes(k)] * shifted[2]).astype(BF16)
                    fold = lambda v: jnp.sum(v.reshape(FF_ROWS // SUBLANE_ROWS, SUBLANE_ROWS, LANES), axis=0)
                    for j in range(FFN_K):
                        dwk[j] = dwk[j] + fold(shifted[j] * ucur)
                    dbk = dbk + fold(shifted[FFN_K - 1])
                for j in range(FFN_K):
                    dw_ref[j:j + 1, _lanes(k)] += _colsum(dwk[j])
                db_ref[:, _lanes(k)] += _colsum(dbk)

    nblk = t // FF_HALO
    return pl.pallas_call(
        body, grid=(2, t // tm),
        in_specs=[pl.BlockSpec((tm, wide), lambda c, i: (i, c)),
                  pl.BlockSpec((FF_HALO, wide), lambda c, i: (jnp.maximum(i * per - 1, 0), c)),
                  pl.BlockSpec((FF_HALO, wide), lambda c, i: (jnp.minimum((i + 1) * per, nblk - 1), c)),
                  pl.BlockSpec((tm, FF_BLOCK), lambda c, i: (i, c)),
                  pl.BlockSpec((FF_HALO, FF_BLOCK), lambda c, i: (jnp.minimum((i + 1) * per, nblk - 1), c)),
                  pl.BlockSpec((FFN_K, wide), lambda c, i: (0, c)),
                  pl.BlockSpec((1, wide), lambda c, i: (0, c))],
        out_specs=[pl.BlockSpec((tm, wide), lambda c, i: (i, c)),
                   pl.BlockSpec((FFN_K, wide), lambda c, i: (0, c)),
                   pl.BlockSpec((1, wide), lambda c, i: (0, c))],
        out_shape=[_sds((t, 2 * D_FF), BF16), _sds((FFN_K, 2 * D_FF), F32), _sds((1, 2 * D_FF), F32)],
        scratch_shapes=[pltpu.VMEM((2 * FF_CHUNKS, FF_HALO + ext, LANES), F32),
                        pltpu.VMEM((2 * FF_CHUNKS, ext, LANES), F32)],
        compiler_params=_params(("arbitrary", "arbitrary")), name=name,
    )(up, up, up, dact, dact, w3, b3)


def _toeplitz_map():
    f = np.zeros((TOEP, REL_PAD), np.float32)
    for m in range(TOEP - 1):
        rel = (WINDOW - 1) - m
        f[m, int(np.clip(rel, -MAX_REL, MAX_REL)) + MAX_REL] = 1.0
    return f


def _split3(v):
    hi = v.astype(BF16)
    r1 = v - hi.astype(F32)
    mid = r1.astype(BF16)
    lo = (r1 - mid.astype(F32)).astype(BF16)
    return hi, mid, lo


def _exact_select(v, sel):
    out = None
    for part in _split3(v):
        p = jnp.dot(part, sel, preferred_element_type=F32)
        out = p if out is None else out + p
    return out


def _select_call(v, sel, name):
    def body(v_ref, s_ref, o_ref):
        o_ref[...] = _exact_select(v_ref[...], s_ref[...])
    return pl.pallas_call(body, out_shape=_sds((v.shape[0], sel.shape[1]), F32), name=name)(v, sel)


def _band_bias(gen_row):
    b0 = jnp.broadcast_to(gen_row, (Q_TILE, TOEP))
    bias = pltpu.roll(b0, TOEP - (Q_TILE - 1), 1, stride=1, stride_axis=0)[:, :WINDOW]
    qq = lax.broadcasted_iota(jnp.int32, (Q_TILE, WINDOW), 0) // CHUNK
    kc = lax.broadcasted_iota(jnp.int32, (Q_TILE, WINDOW), 1) // CHUNK
    return jnp.where((kc >= qq) & (kc <= qq + LEFT_CHUNKS), bias, NEG_INF)


PAD_ROWS = WINDOW - Q_TILE
NT_DIMS = (((1,), (1,)), ((), ()))
TN_DIMS = (((0,), (0,)), ((), ()))


def _head_mask(hh):
    lane = lax.broadcasted_iota(jnp.int32, (1, 128), 1)
    return (lane < 64) if hh == 0 else (lane >= 64)


SOFTMAX_ROWS = 16


def _probs_block(s_scr, bias, hh, rows, q_start):
    s = s_scr[rows, :] + bias[hh, rows, :]
    col = lax.broadcasted_iota(jnp.int32, (SOFTMAX_ROWS, WINDOW), 1)
    s = jnp.where(col >= PAD_ROWS - q_start, s, NEG_INF)
    p = jnp.exp(s - jnp.max(s, axis=-1, keepdims=True))
    return p / jnp.sum(p, axis=-1, keepdims=True)


def _attention(z, gen, name, carried=None):
    t = z.shape[0]
    n_i = t // STEP_ROWS

    def body(q_ref, k_ref, v_ref, g_ref, o_ref, kpad, vpad, bias, s_scr, p_scr):
        hp, i = pl.program_id(0), pl.program_id(1)

        @pl.when(i == 0)
        def _():
            kpad[0:PAD_ROWS, :] = jnp.zeros((PAD_ROWS, 128), BF16)
            vpad[0:PAD_ROWS, :] = jnp.zeros((PAD_ROWS, 128), BF16)
            kpad[PAD_ROWS:PAD_ROWS + t, :] = k_ref[...].astype(BF16)
            vpad[PAD_ROWS:PAD_ROWS + t, :] = v_ref[...].astype(BF16)
            for hh in range(2):
                bias[hh] = _band_bias(g_ref[pl.ds(2 * hp + hh, 1), :])

        for q0 in range(0, STEP_ROWS, Q_TILE):
            q_start = i * STEP_ROWS + q0
            win = pl.ds(pl.multiple_of(q_start, Q_TILE), WINDOW)
            out = None
            for hh in range(2):
                mask = _head_mask(hh)
                qm = jnp.where(mask, q_ref[q0:q0 + Q_TILE, :] * (CHUNK ** -0.5), 0.0).astype(BF16)
                slot = 2 * (q0 // Q_TILE) + hh
                s_scr[slot] = lax.dot_general(qm, kpad[win, :], NT_DIMS, preferred_element_type=F32)
                for r0 in range(0, Q_TILE, SOFTMAX_ROWS):
                    rows = slice(r0, r0 + SOFTMAX_ROWS)
                    p_scr[slot, rows, :] = _probs_block(s_scr.at[slot], bias, hh, rows, q_start).astype(BF16)
                o = jnp.dot(p_scr[slot], vpad[win, :], preferred_element_type=F32)
                out = jnp.where(mask, o, 0.0) if out is None else jnp.where(mask, o, out)
            o_ref[q0:q0 + Q_TILE, :] = out.astype(BF16)

    res = _call(
        body, grid=(4, n_i),
        in_specs=[pl.BlockSpec((STEP_ROWS, 128), lambda h, i: (i, h)),
                  pl.BlockSpec((t, 128), lambda h, i: (0, 4 + h)),
                  pl.BlockSpec((t, 128), lambda h, i: (0, 8 + h)),
                  pl.BlockSpec((N_HEADS, TOEP), lambda h, i: (0, 0))],
        out_specs=[pl.BlockSpec((STEP_ROWS, 128), lambda h, i: (i, h))],
        out_shape=[_sds((t, 512), BF16)],
        scratch_shapes=[pltpu.VMEM((PAD_ROWS + t, 128), BF16), pltpu.VMEM((PAD_ROWS + t, 128), BF16),
                        pltpu.VMEM((2, Q_TILE, WINDOW), F32), pltpu.VMEM((4, Q_TILE, WINDOW), F32),
                        pltpu.VMEM((4, Q_TILE, WINDOW), BF16)],
        sem=("arbitrary", "arbitrary"), name=name, args=(z, z, z, gen), carried=carried)
    return res[0] if carried is None else (res[0], res[1:])


def _attention_bwd(z, datt, gen, name, carried=None):
    t = z.shape[0]
    n_i = t // STEP_ROWS

    def body(q_ref, k_ref, v_ref, d_ref, g_ref, dq_ref, dk_ref, dv_ref, sq_ref, sk_ref, sv_ref, dg_ref,
             kpad, vpad, dkacc, dvacc, bias, dsacc, s_scr, dp_scr, p_scr, ds_scr):
        hp, i = pl.program_id(0), pl.program_id(1)

        @pl.when(i == 0)
        def _():
            kpad[0:PAD_ROWS, :] = jnp.zeros((PAD_ROWS, 128), BF16)
            vpad[0:PAD_ROWS, :] = jnp.zeros((PAD_ROWS, 128), BF16)
            kpad[PAD_ROWS:PAD_ROWS + t, :] = k_ref[...].astype(BF16)
            vpad[PAD_ROWS:PAD_ROWS + t, :] = v_ref[...].astype(BF16)
            dkacc[...] = jnp.zeros_like(dkacc)
            dvacc[...] = jnp.zeros_like(dvacc)
            dsacc[...] = jnp.zeros_like(dsacc)
            for hh in range(2):
                bias[hh] = _band_bias(g_ref[pl.ds(2 * hp + hh, 1), :])

        dq_sum = None
        for q0 in range(0, STEP_ROWS, Q_TILE):
            q_start = i * STEP_ROWS + q0
            win = pl.ds(pl.multiple_of(q_start, Q_TILE), WINDOW)
            dq = None
            for hh in range(2):
                mask = _head_mask(hh)
                qm = jnp.where(mask, q_ref[q0:q0 + Q_TILE, :] * (CHUNK ** -0.5), 0.0).astype(BF16)
                dom = jnp.where(mask, d_ref[q0:q0 + Q_TILE, :], 0.0).astype(BF16)
                slot = 2 * (q0 // Q_TILE) + hh
                s_scr[slot] = lax.dot_general(qm, kpad[win, :], NT_DIMS, preferred_element_type=F32)
                dp_scr[slot] = lax.dot_general(dom, vpad[win, :], NT_DIMS, preferred_element_type=F32)
                for r0 in range(0, Q_TILE, SOFTMAX_ROWS):
                    rows = slice(r0, r0 + SOFTMAX_ROWS)
                    p = _probs_block(s_scr.at[slot], bias, hh, rows, q_start)
                    dp = dp_scr[slot, rows, :]
                    ds = p * (dp - jnp.sum(p * dp, axis=-1, keepdims=True))
                    dsacc[hh, rows, :] += ds
                    ds_scr[slot, rows, :] = ds.astype(BF16)
                    p_scr[slot, rows, :] = p.astype(BF16)
                ds16 = ds_scr[slot]
                dqh = jnp.dot(ds16, kpad[win, :], preferred_element_type=F32) * (CHUNK ** -0.5)
                dq = jnp.where(mask, dqh, 0.0) if dq is None else jnp.where(mask, dqh, dq)
                dkacc[win, :] += lax.dot_general(ds16, qm, TN_DIMS, preferred_element_type=F32)
                dvacc[win, :] += lax.dot_general(p_scr[slot], dom, TN_DIMS, preferred_element_type=F32)
            dq_ref[q0:q0 + Q_TILE, :] = dq.astype(BF16)
            dq_sum = _colsum(dq) if dq_sum is None else dq_sum + _colsum(dq)

        @pl.when(i == 0)
        def _():
            sq_ref[...] = dq_sum

        @pl.when(i > 0)
        def _():
            sq_ref[...] += dq_sum

        @pl.when(i == n_i - 1)
        def _():
            dk = dkacc[PAD_ROWS:PAD_ROWS + t, :]
            dv = dvacc[PAD_ROWS:PAD_ROWS + t, :]
            dk_ref[...] = dk.astype(BF16)
            dv_ref[...] = dv.astype(BF16)
            sk_ref[...] = _colsum(dk)
            sv_ref[...] = _colsum(dv)
            rr = lax.broadcasted_iota(jnp.int32, (Q_TILE, Q_TILE), 0)
            cc = lax.broadcasted_iota(jnp.int32, (Q_TILE, Q_TILE), 1)
            rev = jnp.where(rr + cc == Q_TILE - 1, 1.0, 0.0).astype(BF16)
            for hh in range(2):
                acc = None
                for part in _split3(dsacc[hh]):
                    pr = jnp.dot(rev, part, preferred_element_type=F32)
                    acc = pr if acc is None else acc + pr
                wide = jnp.concatenate([acc, jnp.zeros((Q_TILE, TOEP - WINDOW), F32)], axis=1)
                dg_ref[pl.ds(2 * hp + hh, 1), :] = _colsum(pltpu.roll(wide, 0, 1, stride=1, stride_axis=0))

    col = lambda off: pl.BlockSpec((t, 128), lambda h, i: (0, off + h))
    tile = lambda: pl.BlockSpec((STEP_ROWS, 128), lambda h, i: (i, h))
    sums = lambda: pl.BlockSpec((1, 128), lambda h, i: (0, h))
    res = _call(
        body, grid=(4, n_i),
        in_specs=[tile(), col(4), col(8), tile(), pl.BlockSpec((N_HEADS, TOEP), lambda h, i: (0, 0))],
        out_specs=[tile(), col(0), col(0), sums(), sums(), sums(), pl.BlockSpec((N_HEADS, TOEP), lambda h, i: (0, 0))],
        out_shape=[_sds((t, 512), BF16)] * 3 + [_sds((1, 512), F32)] * 3 + [_sds((N_HEADS, TOEP), F32)],
        scratch_shapes=[pltpu.VMEM((PAD_ROWS + t, 128), BF16), pltpu.VMEM((PAD_ROWS + t, 128), BF16),
                        pltpu.VMEM((PAD_ROWS + t, 128), F32), pltpu.VMEM((PAD_ROWS + t, 128), F32),
                        pltpu.VMEM((2, Q_TILE, WINDOW), F32), pltpu.VMEM((2, Q_TILE, WINDOW), F32),
                        pltpu.VMEM((4, Q_TILE, WINDOW), F32), pltpu.VMEM((4, Q_TILE, WINDOW), F32),
                        pltpu.VMEM((4, Q_TILE, WINDOW), BF16), pltpu.VMEM((4, Q_TILE, WINDOW), BF16)],
        sem=("arbitrary", "arbitrary"), name=name, args=(z, z, z, datt, gen), carried=carried)
    return res[:7] if carried is None else (res[:7], res[7:])


def _adamw_math(w, g, m, v):
    m = ADAM_B1 * m + (1.0 - ADAM_B1) * g
    v = ADAM_B2 * v + (1.0 - ADAM_B2) * (g * g)
    m_hat = m / (1.0 - ADAM_B1 ** ADAM_STEP)
    v_hat = v / (1.0 - ADAM_B2 ** ADAM_STEP)
    delta = -ADAM_LR * (m_hat / (jnp.sqrt(v_hat) + ADAM_EPS) + ADAM_WD * w)
    return delta, m, v


def _adamw_many(items, name):
    n = len(items)

    def body(*refs):
        ins, outs = refs[:4 * n], refs[4 * n:]
        for k in range(n):
            w, g, m, v = (r[...] for r in ins[4 * k:4 * k + 4])
            outs[3 * k][...], outs[3 * k + 1][...], outs[3 * k + 2][...] = _adamw_math(w, g, m, v)

    flat = [a for item in items for a in item]
    res = pl.pallas_call(body, out_shape=[_sds(item[0].shape, F32) for item in items for _ in range(3)],
                         name=name)(*flat)
    return [tuple(res[3 * k:3 * k + 3]) for k in range(n)]


def _adamw(w, g, m, v, name, after):
    r, c = w.shape
    tm = next(cand for cand in (256, 176, 128, 64, 32, 16, 8) if r % cand == 0)
    return _rowcall(lambda wv, gv, mv, vv, _: (gv,) + _adamw_math(wv, gv, mv, vv),
                    [(w, c, 0), (g, c, 0), (m, c, 0), (v, c, 0)], [after], [(r, c, F32, c)] * 4, [], name=name, tm=tm)


def _ada_fwd(c_all, w_shard, b_shard, name):
    n = w_shard.shape[1]
    tn = 512

    def body(c_ref, w_ref, b_ref, o_ref, a_ref):
        cv = c_ref[...]
        act = cv * _sigmoid(cv)
        a_ref[...] = act
        o_ref[...] = jnp.dot(act.astype(BF16), w_ref[...].astype(BF16), preferred_element_type=F32) + b_ref[...]

    return pl.pallas_call(
        body, grid=(n // tn,),
        in_specs=[pl.BlockSpec((8, D), lambda j: (0, 0)), pl.BlockSpec((D, tn), lambda j: (0, j)),
                  pl.BlockSpec((1, tn), lambda j: (0, j))],
        out_specs=[pl.BlockSpec((8, tn), lambda j: (0, j)), pl.BlockSpec((8, D), lambda j: (0, 0))],
        out_shape=[_sds((8, n), F32), _sds((8, D), F32)],
        compiler_params=_params(("arbitrary",)), name=name,
    )(c_all, w_shard, b_shard)


def _ada_bwd_adamw(act_t, dmod_shard, w, m, v, name):
    r, c = w.shape
    tm = 256

    def body(a_ref, d_ref, w_ref, m_ref, v_ref, g_ref, dl_ref, nm_ref, nv_ref):
        g = jnp.dot(a_ref[...], d_ref[...], precision=lax.Precision.HIGHEST, preferred_element_type=F32)
        g_ref[...] = g
        dl_ref[...], nm_ref[...], nv_ref[...] = _adamw_math(w_ref[...], g, m_ref[...], v_ref[...])

    blk = pl.BlockSpec((tm, c), lambda i: (i, 0))
    return pl.pallas_call(
        body, grid=(r // tm,),
        in_specs=[pl.BlockSpec((tm, 8), lambda i: (i, 0)), pl.BlockSpec((8, c), lambda i: (0, 0)), blk, blk, blk],
        out_specs=[blk] * 4, out_shape=[_sds((r, c), F32)] * 4,
        compiler_params=_params(("arbitrary",)), name=name,
    )(act_t, dmod_shard, w, m, v)


def _place():
    return lax.axis_index("x"), lax.axis_index("y"), lax.axis_index("c")


def _flip(v, bit):
    return 1 - v if bit else v


VMEM_SPEC = pl.BlockSpec(memory_space=pltpu.VMEM)


def _allgather8(v, name):
    r, c = v.shape

    def body(v_ref, g_ref, tot_ref, send_sems, recv_sems, local_sem):
        x, y, cc = _place()
        sibling = (x, y, 1 - cc)
        chips = [(_flip(x, k & 2), _flip(y, k & 1)) for k in (1, 2, 3)]

        def block(px, py, pc):
            return g_ref.at[4 * px + 2 * py + pc]

        def copy(k, place, to, src=None):
            slot = block(*place)
            return pltpu.make_async_remote_copy(src_ref=slot if src is None else src, dst_ref=slot,
                                                send_sem=send_sems.at[k], recv_sem=recv_sems.at[k],
                                                device_id=to, device_id_type=MESH)

        mine = pltpu.make_async_copy(v_ref, block(x, y, cc), local_sem)
        mine.start()
        first = [copy(0, (x, y, cc), sibling, src=v_ref)]
        first += [copy(1 + j, (x, y, cc), (px, py, cc), src=v_ref) for j, (px, py) in enumerate(chips)]
        for cp in first:
            cp.start()
        passed = [copy(4 + j, (px, py, cc), sibling) for j, (px, py) in enumerate(chips)]
        for j, (px, py) in enumerate(chips):
            copy(1 + j, (px, py, cc), (x, y, cc)).wait_recv()
            passed[j].start()
        copy(0, sibling, (x, y, cc)).wait_recv()
        for j, (px, py) in enumerate(chips):
            copy(4 + j, (px, py, 1 - cc), (x, y, cc)).wait_recv()
        for cp in first + passed:
            cp.wait_send()
        mine.wait()
        tot = g_ref[0]
        for d in range(1, 8):
            tot = tot + g_ref[d]
        tot_ref[...] = tot

    return pl.pallas_call(
        body, in_specs=[VMEM_SPEC], out_specs=[VMEM_SPEC, VMEM_SPEC],
        out_shape=[_sds((8, r, c), F32), _sds((r, c), F32)],
        scratch_shapes=[pltpu.SemaphoreType.DMA((7,)), pltpu.SemaphoreType.DMA((7,)), pltpu.SemaphoreType.DMA],
        compiler_params=pltpu.CompilerParams(vmem_limit_bytes=VMEM_LIMIT), name=name,
    )(v)


def _slot(px, py, swapped):
    return 2 * py + px if swapped else 2 * px + py


def _gather_shards(arrs, swapped, name, in_place=False):
    n = len(arrs)

    def body(*refs):
        ins, outs = refs[:n], refs[n:2 * n]
        send1, recv1, send2, recv2, local_sems = refs[2 * n:]
        x, y, c = _place()
        sibling = (x, y, 1 - c)
        chips = [(_flip(x, k & 2), _flip(y, k & 1)) for k in (1, 2, 3)]
        local_copies, sends = [], []
        for a in range(n):
            h = outs[a].shape[1] // 2
            mine = pl.ds(pl.multiple_of(c * h, 8), h)
            own = _slot(x, y, swapped[a])
            if in_place:
                src = outs[a].at[own, mine]
            else:
                src = ins[a].at[mine]
                lc = pltpu.make_async_copy(ins[a], outs[a].at[own], local_sems.at[a])
                lc.start()
                local_copies.append(lc)
            for j, (px, py) in enumerate(chips):
                cp = pltpu.make_async_remote_copy(
                    src_ref=src, dst_ref=outs[a].at[own, mine], send_sem=send1.at[3 * a + j],
                    recv_sem=recv1.at[3 * a + j], device_id=(px, py, c), device_id_type=MESH)
                cp.start()
                sends.append(cp)
        for a in range(n):
            h = outs[a].shape[1] // 2
            mine = pl.ds(pl.multiple_of(c * h, 8), h)
            for j, (px, py) in enumerate(chips):
                piece = outs[a].at[_slot(px, py, swapped[a]), mine]
                pltpu.make_async_remote_copy(
                    src_ref=piece, dst_ref=piece, send_sem=send1.at[3 * a + j], recv_sem=recv1.at[3 * a + j],
                    device_id=(px, py, c), device_id_type=MESH).wait_recv()
                fwd = pltpu.make_async_remote_copy(
                    src_ref=piece, dst_ref=piece, send_sem=send2.at[3 * a + j], recv_sem=recv2.at[3 * a + j],
                    device_id=sibling, device_id_type=MESH)
                fwd.start()
                sends.append(fwd)
        for a in range(n):
            h = outs[a].shape[1] // 2
            other = pl.ds(pl.multiple_of((1 - c) * h, 8), h)
            for j, (px, py) in enumerate(chips):
                piece = outs[a].at[_slot(px, py, swapped[a]), other]
                pltpu.make_async_remote_copy(
                    src_ref=piece, dst_ref=piece, send_sem=send2.at[3 * a + j], recv_sem=recv2.at[3 * a + j],
                    device_id=sibling, device_id_type=MESH).wait_recv()
        for cp in sends:
            cp.wait_send()
        for lc in local_copies:
            lc.wait()

    dma = lambda k: pltpu.SemaphoreType.DMA((k,))
    return pl.pallas_call(
        body, in_specs=[ANY] * n, out_specs=[ANY] * n,
        out_shape=[_sds(a.shape if in_place else (4,) + a.shape, a.dtype) for a in arrs],
        scratch_shapes=[dma(3 * n), dma(3 * n), dma(3 * n), dma(3 * n), dma(n)],
        input_output_aliases={a: a for a in range(n)} if in_place else {},
        name=name,
    )(*arrs)


def _carry_pair_exchange(grads):
    n = len(grads)

    def copies(ins, outs, send_sems, recv_sems):
        x, y, c = _place()
        cps = []
        for a in range(n):
            h = ins[a].shape[1] // 2
            theirs = pl.ds(pl.multiple_of((1 - c) * h, 8), h)
            cps.append(pltpu.make_async_remote_copy(
                src_ref=ins[a].at[:, theirs, :], dst_ref=outs[a], send_sem=send_sems.at[a], recv_sem=recv_sems.at[a],
                device_id=(x, y, 1 - c), device_id_type=MESH))
        return cps

    def start(*refs):
        for cp in copies(*refs):
            cp.start()

    def finish(*refs):
        for cp in copies(*refs):
            cp.wait()

    return _Carried(grads, [_sds((4, g.shape[1] // 2, g.shape[2]), F32) for g in grads], {}, n, start, finish)


def _pair_sum(grad, recv, core, name):
    _, r, c = grad.shape
    h = r // 2

    def body(core_ref, g_ref, r_ref, o_ref):
        o_ref[...] = (g_ref[...] + r_ref[...]).astype(BF16)

    return pl.pallas_call(
        body,
        grid_spec=pltpu.PrefetchScalarGridSpec(
            num_scalar_prefetch=1, grid=(4,),
            in_specs=[pl.BlockSpec((None, h, c), lambda s, core_ref: (s, core_ref[0], 0)),
                      pl.BlockSpec((None, h, c), lambda s, core_ref: (s, 0, 0))],
            out_specs=pl.BlockSpec((None, h, c), lambda s, core_ref: (s, 0, 0))),
        out_shape=_sds((4, h, c), BF16), compiler_params=_params(("arbitrary",)), name=name,
    )(core, grad, recv)


def _carry_chip_exchange(parts, swapped):
    n = len(parts)

    def copies(ins, outs, send_sems, recv_sems):
        x, y, c = _place()
        chips = [(_flip(x, k & 2), _flip(y, k & 1)) for k in (1, 2, 3)]
        cps = []
        for a in range(n):
            for j, (px, py) in enumerate(chips):
                cps.append(pltpu.make_async_remote_copy(
                    src_ref=ins[a].at[_slot(px, py, swapped[a])], dst_ref=outs[a].at[j],
                    send_sem=send_sems.at[3 * a + j], recv_sem=recv_sems.at[3 * a + j],
                    device_id=(px, py, c), device_id_type=MESH))
        return cps

    def start(*refs):
        for cp in copies(*refs):
            cp.start()

    def finish(*refs):
        for cp in copies(*refs):
            cp.wait()

    return _Carried(parts, [_sds((3,) + p.shape[1:], BF16) for p in parts], {}, 3 * n, start, finish)


def _chip_sum(part, recv, slot_core, name):
    _, h, c = part.shape

    def body(sc_ref, p_ref, r_ref, o_ref):
        acc = p_ref[...].astype(F32)
        for j in range(3):
            acc = acc + r_ref[j].astype(F32)
        o_ref[...] = acc

    return pl.pallas_call(
        body,
        grid_spec=pltpu.PrefetchScalarGridSpec(
            num_scalar_prefetch=1, grid=(1,),
            in_specs=[pl.BlockSpec((None, h, c), lambda q, sc_ref: (sc_ref[0], 0, 0)),
                      pl.BlockSpec((3, h, c), lambda q, sc_ref: (0, 0, 0))],
            out_specs=pl.BlockSpec((h, c), lambda q, sc_ref: (sc_ref[1], 0))),
        out_shape=_sds((2 * h, c), F32), compiler_params=_params(("arbitrary",)), name=name,
    )(slot_core, part, recv)


def _carry_pair_share(shards):
    n = len(shards)

    def copies(outs, send_sems, recv_sems, mine):
        x, y, c = _place()
        cps = []
        for a in range(n):
            h = outs[a].shape[0] // 2
            half = outs[a].at[pl.ds(pl.multiple_of((c if mine else 1 - c) * h, 8), h)]
            cps.append(pltpu.make_async_remote_copy(
                src_ref=half, dst_ref=half, send_sem=send_sems.at[a], recv_sem=recv_sems.at[a],
                device_id=(x, y, 1 - c), device_id_type=MESH))
        return cps

    def start(ins, outs, send_sems, recv_sems):
        for cp in copies(outs, send_sems, recv_sems, True):
            cp.start()

    def finish(ins, outs, send_sems, recv_sems):
        for cp in copies(outs, send_sems, recv_sems, False):
            cp.wait_recv()
        for cp in copies(outs, send_sems, recv_sems, True):
            cp.wait_send()

    return _Carried(shards, [_sds(s.shape, F32) for s in shards], {a: a for a in range(n)}, n, start, finish)


def _carry_gather_ici(bufs, swapped):
    n = len(bufs)

    def copies(outs, send_sems, recv_sems, sending):
        x, y, c = _place()
        cps = []
        for a in range(n):
            h = outs[a].shape[1] // 2
            mine = pl.ds(pl.multiple_of(c * h, 8), h)
            for j, k in enumerate((1, 2, 3)):
                px, py = _flip(x, k & 2), _flip(y, k & 1)
                slot = _slot(x, y, swapped[a]) if sending else _slot(px, py, swapped[a])
                piece = outs[a].at[slot, mine]
                cps.append(pltpu.make_async_remote_copy(
                    src_ref=piece, dst_ref=piece, send_sem=send_sems.at[3 * a + j], recv_sem=recv_sems.at[3 * a + j],
                    device_id=(px, py, c), device_id_type=MESH))
        return cps

    def start(ins, outs, send_sems, recv_sems):
        for cp in copies(outs, send_sems, recv_sems, True):
            cp.start()

    def finish(ins, outs, send_sems, recv_sems):
        for cp in copies(outs, send_sems, recv_sems, False):
            cp.wait_recv()
        for cp in copies(outs, send_sems, recv_sems, True):
            cp.wait_send()

    return _Carried(bufs, [_sds(b.shape, b.dtype) for b in bufs], {a: a for a in range(n)}, 3 * n, start, finish)


HBM_SPEC = pl.BlockSpec(memory_space=pltpu.HBM)
SEM_SPEC = pl.BlockSpec(memory_space=pltpu.SEMAPHORE)
SIDE_EFFECT = pltpu.SideEffectType.DATAFLOW_SIDE_EFFECTING


def _ici_pieces(buf, send_sems, recv_sems, swapped, sending):
    x, y, c = _place()
    h = buf.shape[1] // 2
    mine = pl.ds(pl.multiple_of(c * h, 8), h)
    cps = []
    for j, k in enumerate((1, 2, 3)):
        px, py = _flip(x, k & 2), _flip(y, k & 1)
        piece = buf.at[_slot(x, y, swapped) if sending else _slot(px, py, swapped), mine]
        cps.append(pltpu.make_async_remote_copy(src_ref=piece, dst_ref=piece, send_sem=send_sems.at[j],
                                                recv_sem=recv_sems.at[j], device_id=(px, py, c), device_id_type=MESH))
    return cps


def _gather_ici_start(buf, after, swapped, name):
    def body(buf_ref, after_ref, send_sems, recv_sems, thru, token):
        for cp in _ici_pieces(thru, send_sems, recv_sems, swapped, True):
            cp.start()
        token[...] = jnp.zeros_like(token)

    return pl.pallas_call(
        body, name=name,
        out_shape=(pltpu.SemaphoreType.DMA((3,)), pltpu.SemaphoreType.DMA((3,)), pltpu.HBM(buf.shape, buf.dtype),
                   jax.ShapeDtypeStruct((8, 128), F32)),
        in_specs=(HBM_SPEC, ANY), out_specs=(SEM_SPEC, SEM_SPEC, HBM_SPEC, VMEM_SPEC), input_output_aliases={0: 2},
        compiler_params=pltpu.CompilerParams(has_side_effects=SIDE_EFFECT),
    )(pltpu.with_memory_space_constraint(buf, pltpu.HBM), after)


def _gather_ici_wait(send_sems, recv_sems, thru, after, swapped, name):
    def body(thru_ref, send_sems, recv_sems, after_ref, out_ref):
        for cp in _ici_pieces(out_ref, send_sems, recv_sems, swapped, True):
            cp.wait_send()
        for cp in _ici_pieces(out_ref, send_sems, recv_sems, swapped, False):
            cp.wait_recv()

    return pl.pallas_call(
        body, name=name, out_shape=pltpu.HBM(thru.shape, thru.dtype),
        in_specs=(HBM_SPEC, SEM_SPEC, SEM_SPEC, ANY), out_specs=HBM_SPEC, input_output_aliases={0: 0},
        compiler_params=pltpu.CompilerParams(has_side_effects=SIDE_EFFECT),
    )(thru, send_sems, recv_sems, after)


def _all8_copies(buf, send_sems, recv_sems, sending):
    x, y, c = _place()
    cps = []
    for k in range(1, 8):
        px, py, pc = _flip(x, k & 4), _flip(y, k & 2), _flip(c, k & 1)
        slot = buf.at[4 * x + 2 * y + c] if sending else buf.at[4 * px + 2 * py + pc]
        cps.append(pltpu.make_async_remote_copy(src_ref=slot, dst_ref=slot, send_sem=send_sems.at[k - 1],
                                                recv_sem=recv_sems.at[k - 1], device_id=(px, py, pc), device_id_type=MESH))
    return cps


def _all8_start(buf, name):
    def body(buf_ref, send_sems, recv_sems, thru, token):
        for cp in _all8_copies(thru, send_sems, recv_sems, True):
            cp.start()
        token[...] = jnp.zeros_like(token)

    return pl.pallas_call(
        body, name=name,
        out_shape=(pltpu.SemaphoreType.DMA((7,)), pltpu.SemaphoreType.DMA((7,)), pltpu.HBM(buf.shape, buf.dtype),
                   jax.ShapeDtypeStruct((8, 128), F32)),
        in_specs=(HBM_SPEC,), out_specs=(SEM_SPEC, SEM_SPEC, HBM_SPEC, VMEM_SPEC), input_output_aliases={0: 2},
        compiler_params=pltpu.CompilerParams(has_side_effects=SIDE_EFFECT),
    )(pltpu.with_memory_space_constraint(buf, pltpu.HBM))


def _all8_wait(send_sems, recv_sems, thru, after, name):
    def body(thru_ref, send_sems, recv_sems, after_ref, out_ref):
        for cp in _all8_copies(out_ref, send_sems, recv_sems, True):
            cp.wait_send()
        for cp in _all8_copies(out_ref, send_sems, recv_sems, False):
            cp.wait_recv()

    return pl.pallas_call(
        body, name=name, out_shape=pltpu.HBM(thru.shape, thru.dtype),
        in_specs=(HBM_SPEC, SEM_SPEC, SEM_SPEC, ANY), out_specs=HBM_SPEC, input_output_aliases={0: 0},
        compiler_params=pltpu.CompilerParams(has_side_effects=SIDE_EFFECT),
    )(thru, send_sems, recv_sems, after)


def _sum8(g, name):
    def body(g_ref, o_ref):
        tot = g_ref[0]
        for d in range(1, 8):
            tot = tot + g_ref[d]
        o_ref[...] = tot

    return pl.pallas_call(body, out_shape=_sds(g.shape[1:], F32), name=name)(g)


def _carry_gather_forward(bufs, swapped):
    n = len(bufs)

    def copies(outs, send_sems, recv_sems, sending):
        x, y, c = _place()
        cps = []
        for a in range(n):
            h = outs[a].shape[1] // 2
            rows = pl.ds(pl.multiple_of((c if sending else 1 - c) * h, 8), h)
            for j, k in enumerate((1, 2, 3)):
                piece = outs[a].at[_slot(_flip(x, k & 2), _flip(y, k & 1), swapped[a]), rows]
                cps.append(pltpu.make_async_remote_copy(
                    src_ref=piece, dst_ref=piece, send_sem=send_sems.at[3 * a + j], recv_sem=recv_sems.at[3 * a + j],
                    device_id=(x, y, 1 - c), device_id_type=MESH))
        return cps

    def start(ins, outs, send_sems, recv_sems):
        for cp in copies(outs, send_sems, recv_sems, True):
            cp.start()

    def finish(ins, outs, send_sems, recv_sems):
        for cp in copies(outs, send_sems, recv_sems, False):
            cp.wait_recv()
        for cp in copies(outs, send_sems, recv_sems, True):
            cp.wait_send()

    return _Carried(bufs, [_sds(b.shape, b.dtype) for b in bufs], {a: a for a in range(n)}, 3 * n, start, finish)


def _pack(arrs, rows_multiple=8):
    parts, offs, row = [], [], 0
    for a in arrs:
        flat = a.reshape(-1)
        nrow = -(-flat.shape[0] // D)
        parts.append(jnp.pad(flat, (0, nrow * D - flat.shape[0])))
        offs.append(row)
        row += nrow
    total = -(-row // rows_multiple) * rows_multiple
    if total > row:
        parts.append(jnp.zeros(((total - row) * D,), F32))
    return jnp.concatenate(parts).reshape(total, D), offs


def _unpack(packed, offs, shapes):
    out = []
    for off, shp in zip(offs, shapes):
        size = int(np.prod(shp))
        nrow = -(-size // D)
        out.append(packed[off:off + nrow].reshape(-1)[:size].reshape(shp))
    return out


def _to_bf16_slot(w, slot, name, after=None):
    r, c = w.shape
    tm = next(cand for cand in (256, 176, 128, 64, 32, 16) if r % cand == 0)

    def body(slot_ref, w_ref, *rest):
        rest[-1][...] = w_ref[...].astype(BF16)

    in_specs = [pl.BlockSpec((tm, c), lambda i, slot_ref: (i, 0))]
    if after is not None:
        in_specs.append(pl.BlockSpec((8, 128), lambda i, slot_ref: (0, 0)))
    return pl.pallas_call(
        body,
        grid_spec=pltpu.PrefetchScalarGridSpec(
            num_scalar_prefetch=1, grid=(r // tm,), in_specs=in_specs,
            out_specs=pl.BlockSpec((None, tm, c), lambda i, slot_ref: (slot_ref[0], i, 0))),
        out_shape=_sds((4, r, c), BF16), compiler_params=_params(("arbitrary",)), name=name,
    )(slot, w, *([] if after is None else [after]))


def _unshard_cols(g):
    s, k, n = g.shape
    return jnp.transpose(g, (1, 0, 2)).reshape(k, s * n)


def _ff_swap(v):
    b = FF_BLOCK
    return jnp.concatenate([v[..., 0:b], v[..., 2 * b:3 * b], v[..., b:2 * b], v[..., 3 * b:4 * b]], axis=-1)


LATE = ("attn_o", "conv_o", "mix_o", "up", "down")
EARLY_GRADS = ("down", "up", "mix_o", "attn_o", "conv_o")


def _weight_views(bufs):
    return {"up": bufs["up"], "attn_o": _unshard_cols(bufs["attn_o"]), "conv_o": _unshard_cols(bufs["conv_o"]),
            "mix_o": bufs["mix_o"].reshape(D, D), "down": bufs["down"].reshape(D_FF, D)}


def _pair_sums(names, grads, recv, dist):
    return [_pair_sum(g, r, dist["core"], "pair_sum_" + n) for n, g, r in zip(names, grads, recv)]


def _reduce_halves(names, parts, from_chips, dist):
    return [_chip_sum(p, r, jnp.concatenate([dist["slots"][SWAPPED[n]], dist["core"]]), "chip_sum_" + n)
            for n, p, r in zip(names, parts, from_chips)]


FUSED_TILE = 256
WIDE_TILE = 512


def _gates(z):
    return [(z, 512, 5), (z, 512, 6), (z, 512, 7), (z, 512, 8)]


def _mix_out(a, cb, z, x, w_mix_o, g_post, gt, g_pre2, sc2, sh2, name):
    def lhs(av, cv, ga0, ga1, gb0, gb1):
        ga, gb = jnp.concatenate([ga0, ga1], axis=1), jnp.concatenate([gb0, gb1], axis=1)
        return _sigmoid(ga) * av + _sigmoid(gb) * cv

    def fn(ym, y, xv, gv, gtv, g2v, scv, shv):
        yn, _ = _rms_stats(ym)
        x1 = xv + gtv * (yn * gv)
        xn, _ = _rms_stats(x1)
        return ym, y, x1, (xn * g2v) * (1.0 + scv) + shv

    return _matmul_rows(w_mix_o, form="nn", tm=min(WIDE_TILE, x.shape[0]), tk=D, fn=fn, a_rows=[(a, D, 0), (cb, D, 0)] + _gates(z),
                        a_fn=lhs, rows=[(x, D, 0)], consts=[g_post, gt, g_pre2, sc2, sh2],
                        row_outs=[(F32, D), (BF16, D), (F32, D), (BF16, D)], acc_outs=[], name=name)


def _down_tail(act, w_down, x1, target, g, gt, name):
    def fn(yv, xv, tv, gv, gtv):
        yn, r = _rms_stats(yv)
        e = xv + gtv * (yn * gv) - tv
        dx2 = e * (1.0 / D)
        dyn = dx2 * gtv
        return (dx2, _rms_bwd(dyn * gv, yn, r), _colsum(e * e) * (0.5 / D), _colsum(dyn * yn),
                _colsum(dx2 * (yn * gv)))

    return _matmul_rows(w_down, form="nn", a=act, tm=min(WIDE_TILE, x1.shape[0]), tk=D_FF, fn=fn,
                        rows=[(x1, D, 0), (target, D, 0)], consts=[g, gt], row_outs=[(F32, D), (BF16, D)],
                        acc_outs=[(1, D)] * 3, name=name)


def _up_dx_tail(dup, w_up, x1, dx2, ym, g_pre2, sc2, g_post, gt, name):
    def fn(dh, xv, dov, ymv, g2v, scv, gv, gtv):
        xn, r = _rms_stats(xv)
        dyn = dh * (1.0 + scv)
        dx1 = dov + _rms_bwd(dyn * g2v, xn, r)
        yn, r2 = _rms_stats(ymv)
        dynm = dx1 * gtv
        return (dx1, _rms_bwd(dynm * gv, yn, r2), _colsum(dh), _colsum(dh * (xn * g2v)), _colsum(dyn * xn),
                _colsum(dynm * yn), _colsum(dx1 * (yn * gv)))

    return _matmul_rows(w_up, form="nt", a=dup, tm=min(FUSED_TILE, x1.shape[0]), tk=2 * D_FF, fn=fn,
                        rows=[(x1, D, 0), (dx2, D, 0), (ym, D, 0)], consts=[g_pre2, sc2, g_post, gt],
                        row_outs=[(F32, D), (BF16, D)], acc_outs=[(1, D)] * 5, name=name)


def _mix_dx_gates(dym, w_mix_o, a, cb, z, name):
    def fn(dy, av, cv, ga0, ga1, gb0, gb1):
        sa = _sigmoid(jnp.concatenate([ga0, ga1], axis=1))
        sb = _sigmoid(jnp.concatenate([gb0, gb1], axis=1))
        dcb = dy * sb
        dga = dy * av * (sa * (1.0 - sa))
        dgb = dy * cv * (sb * (1.0 - sb))
        return dy * sa, dcb, dga, dgb, _colsum(dcb), _colsum(dga), _colsum(dgb)

    return _matmul_rows(w_mix_o, form="nt", a=dym, tm=min(WIDE_TILE, a.shape[0]), tk=D, fn=fn,
                        rows=[(a, D, 0), (cb, D, 0)] + _gates(z), consts=[], row_outs=[(BF16, D)] * 4,
                        acc_outs=[(1, D)] * 3, name=name)


def _local_step(x, target, mod, w_in, late, small, dist=None):
    sh_m, sc_m, gt_m, sh_f, sc_f, gt_f = mod
    t = x.shape[0]
    tmm = min(1024, t)
    late_swapped = [SWAPPED[n] for n in LATE]

    h1 = _pre_norm(x, small["g_pre_mix"], sc_m, sh_m, "pre_norm_mix")
    if callable(w_in):
        w_in = w_in(h1)
    z = _matmul(h1, w_in, form="nn", out_dtype=F32, tm=min(FUSED_TILE, t), tn=D_IN, tk=D, bias=small["b_in"], name="mm_in")
    conv = (z, small["w_dw_conv"], small["b_dw_conv"], small["g_conv_ln"], small["b_conv_ln"], "conv_branch")
    if dist is None:
        att = _attention(z, small["gen"], "attention")
        u1, u3 = _conv_branch(*conv)
        bufs = dict(late)
    else:
        mid = [n for n in LATE if n != "down"]
        mid_swapped = [SWAPPED[n] for n in mid]
        att, landed = _attention(z, small["gen"], "attention",
                                 carried=_carry_gather_ici([late[n] for n in mid], mid_swapped))
        (u1, u3), gathered = _conv_branch(*conv, carried=_carry_gather_forward(landed, mid_swapped))
        bufs = dict(zip(mid, gathered))
        bufs["down"] = late["down"]
    w = _weight_views(bufs)
    w["in"] = w_in
    a = _matmul(att, w["attn_o"], form="nn", out_dtype=F32, tm=tmm, tn=512, tk=512, name="mm_attn_o")
    cb = _matmul(u3, w["conv_o"], form="nn", out_dtype=F32, tm=tmm, tn=512, tk=512, bias=small["b_conv_o"], name="mm_conv_o")
    ym, y, x1, h2 = _mix_out(a, cb, z, x, w["mix_o"], small["g_post_mix"], gt_m, small["g_pre_ffn"], sc_f, sh_f, "mix_out")
    mm_up = dict(form="nn", out_dtype=F32, tm=min(FUSED_TILE, t), tn=2 * D_FF, tk=D, name="mm_up")
    ffn_act = (small["w_dw_ffn"], small["b_dw_ffn"], "ffn_act")
    if dist is None:
        up = _matmul(h2, w["up"], **mm_up)
        act = _ffn_act(up, *ffn_act)
    else:
        up, landed = _matmul(h2, w["up"], carried=_carry_gather_ici([late["down"]], [False]), **mm_up)
        act, down = _ffn_act(up, *ffn_act, carried=_carry_gather_forward(landed, [False]))
        w["down"] = down[0].reshape(D_FF, D)

    dx2, dyf, loss_cols, d_g_post_ffn, d_gt_f = _down_tail(act, w["down"], x1, target, small["g_post_ffn"], gt_f, "down_tail")
    dact = _matmul(dyf, w["down"], form="nt", out_dtype=F32, tm=tmm, tn=FF_BLOCK, tk=D, name="mm_down_dx")
    g_down = _matmul(act, dyf, form="tn", out_dtype=F32, tm=FF_BLOCK, tn=512, tk=t, name="mm_down_dw")
    dup, d_w_dw_ffn, d_b_dw_ffn = _ffn_act_bwd(dact, up, small["w_dw_ffn"], small["b_dw_ffn"], "ffn_act_bwd")
    dx1, dym, d_sh_f, d_sc_f, d_g_pre_ffn, d_g_post_mix, d_gt_m = _up_dx_tail(
        dup, w["up"], x1, dx2, ym, small["g_pre_ffn"], sc_f, small["g_post_mix"], gt_m, "up_dx_tail")
    g_up = _matmul(h2, dup, form="tn", out_dtype=F32, tm=512, tn=FF_BLOCK, tk=t, out_sharded=True, name="mm_up_dw")
    da, dcb, dgate_a, dgate_b, d_b_conv_o, sga, sgb = _mix_dx_gates(dym, w["mix_o"], a, cb, z, "mix_dx_gates")
    g_mix_o = _matmul(y, dym, form="tn", out_dtype=F32, tm=D, tn=512, tk=t, name="mm_mix_o_dw")
    datt = _matmul(da, w["attn_o"], form="nt", out_dtype=F32, tm=tmm, tn=512, tk=D, name="mm_attn_o_dx")
    g_attn_o = _matmul(att, da, form="tn", out_dtype=F32, tm=512, tn=256, tk=t, out_sharded=True, name="mm_attn_o_dw")
    du3 = _matmul(dcb, w["conv_o"], form="nt", out_dtype=F32, tm=tmm, tn=512, tk=D, name="mm_conv_o_dx")
    g_conv_o = _matmul(u3, dcb, form="tn", out_dtype=F32, tm=512, tn=256, tk=t, out_sharded=True, name="mm_conv_o_dw")
    big = {"attn_o": g_attn_o, "conv_o": g_conv_o, "mix_o": g_mix_o.reshape(4, 256, D),
           "up": g_up, "down": g_down.reshape(4, D_FF // 4, D)}
    conv_bwd = (du3, u1, z, small["w_dw_conv"], small["g_conv_ln"], small["b_conv_ln"], "conv_branch_bwd")
    in_dw = dict(form="tn", out_dtype=F32, tm=512, tn=1152, tk=t, out_sharded=True, name="mm_in_dw")
    in_dx = dict(form="nt", out_dtype=F32, tm=min(WIDE_TILE, t), tn=D, tk=D_IN, name="mm_in_dx")
    if dist is None:
        dglu, d_w_dw_conv, d_b_dw_conv, d_g_conv_ln, d_b_conv_ln, sglu = _conv_branch_bwd(*conv_bwd)
        dq, dk, dv, sq, sk, sv, dgen = _attention_bwd(z, datt, small["gen"], "attention_bwd")
        dz = jnp.concatenate([dq, dk, dv, dglu, dgate_a, dgate_b], axis=1)
        big["in"] = _matmul(h1, dz, **in_dw)
        dh1 = _matmul(dz, w_in, **in_dx)
    else:
        early = [big[n] for n in EARLY_GRADS]
        (dglu, d_w_dw_conv, d_b_dw_conv, d_g_conv_ln, d_b_conv_ln, sglu), recv = _conv_branch_bwd(
            *conv_bwd, carried=_carry_pair_exchange(early))
        parts = _pair_sums(EARLY_GRADS, early, recv, dist)
        (dq, dk, dv, sq, sk, sv, dgen), from_chips = _attention_bwd(
            z, datt, small["gen"], "attention_bwd",
            carried=_carry_chip_exchange(parts, [SWAPPED[n] for n in EARLY_GRADS]))
        halves = _reduce_halves(EARLY_GRADS, parts, from_chips, dist)
        dz = jnp.concatenate([dq, dk, dv, dglu, dgate_a, dgate_b], axis=1)
        g_in, shards = _matmul(h1, dz, carried=_carry_pair_share(halves), **in_dw)
        big = dict(zip(EARLY_GRADS, shards))
        recv_in = _run_carried(_carry_pair_exchange([g_in]), "pair_exchange_in")
        part_in = _pair_sums(("in",), [g_in], recv_in, dist)
        dh1, from_chips_in = _matmul(dz, w_in, carried=_carry_chip_exchange(part_in, [False]), **in_dx)
        half_in = _reduce_halves(("in",), part_in, from_chips_in, dist)
        big["in"] = _run_carried(_carry_pair_share(half_in), "pair_share_in")[0]
    d_b_in = jnp.concatenate([sq, sk, sv, sglu, sga, sgb], axis=1)
    grad_x, d_sh_m, d_sc_m, d_g_pre_mix = _pre_norm_bwd(dh1, x, dx1, small["g_pre_mix"], sc_m, "pre_norm_mix_bwd")

    dmod = [d_sh_m, d_sc_m, d_gt_m, d_sh_f, d_sc_f, d_gt_f]
    sm = {"g_pre_mix": d_g_pre_mix, "g_post_mix": d_g_post_mix, "b_in": d_b_in, "gen": dgen,
          "w_dw_conv": d_w_dw_conv, "b_dw_conv": d_b_dw_conv, "g_conv_ln": d_g_conv_ln, "b_conv_ln": d_b_conv_ln,
          "b_conv_o": d_b_conv_o, "g_pre_ffn": d_g_pre_ffn, "g_post_ffn": d_g_post_ffn,
          "w_dw_ffn": d_w_dw_ffn, "b_dw_ffn": d_b_dw_ffn}
    return loss_cols, grad_x, dmod, big, sm


BIG = ("in", "attn_o", "conv_o", "mix_o", "up", "down")
SWAPPED = {"in": False, "attn_o": False, "conv_o": False, "mix_o": False, "up": True, "down": False}
SMALL_ORDER = ("b_ada", "g_pre_mix", "g_post_mix", "b_in", "rel_bias", "b_dw_conv", "g_conv_ln", "b_conv_ln",
               "b_conv_o", "g_pre_ffn", "g_post_ffn", "b_dw_ffn", "w_dw_conv", "w_dw_ffn")


def kernel(x, c, w_ada, b_ada, g_pre_mix, g_post_mix, w_in, b_in, rel_bias, w_attn_o, w_dw_conv, b_dw_conv, g_conv_ln, b_conv_ln, w_conv_o, b_conv_o, w_mix_o, g_pre_ffn, g_post_ffn, w_up, w_dw_ffn, b_dw_ffn, w_down, loss_target, m_w_ada, m_b_ada, m_g_pre_mix, m_g_post_mix, m_w_in, m_b_in, m_rel_bias, m_w_attn_o, m_w_dw_conv, m_b_dw_conv, m_g_conv_ln, m_b_conv_ln, m_w_conv_o, m_b_conv_o, m_w_mix_o, m_g_pre_ffn, m_g_post_ffn, m_w_up, m_w_dw_ffn, m_b_dw_ffn, m_w_down, v_w_ada, v_b_ada, v_g_pre_mix, v_g_post_mix, v_w_in, v_b_in, v_rel_bias, v_w_attn_o, v_w_dw_conv, v_b_dw_conv, v_g_conv_ln, v_b_conv_ln, v_w_conv_o, v_b_conv_o, v_w_mix_o, v_g_pre_ffn, v_g_post_ffn, v_w_up, v_w_dw_ffn, v_b_dw_ffn, v_w_down):
    given = dict(locals())
    ax, ay, ac = lax.axis_index("x"), lax.axis_index("y"), lax.axis_index("c")
    shard = 2 * ax + ay
    me = 4 * ax + 2 * ay + ac
    xs, target = x[0], loss_target[0]

    slots = {sw: _slot(ax, ay, sw).astype(jnp.int32).reshape(1) for sw in (False, True)}
    own = {"in": _to_bf16_slot(w_in[0], slots[False], "cast_in")}

    c_pad = jnp.pad(c, ((0, 7), (0, 0)))
    c_g, _ = _allgather8(c_pad, "gather_c")
    c_all = c_g[:, 0, :]
    b_ada_shard = lax.dynamic_slice(b_ada, (0, shard * 1536), (1, 1536))
    mod_shard, c_act = _ada_fwd(c_all, w_ada[0], b_ada_shard, "ada_fwd")
    small_in = [jnp.pad(mod_shard, ((0, 8), (0, 0))),
                jnp.pad(w_dw_conv[0], ((0, 1), (0, 0))),
                jnp.pad(w_dw_ffn[0], ((0, 13), (0, 0)))]
    mod_g, wdc_g, wdf_g = _gather_shards(small_in, [False, False, True], "gather_small")
    mod_all = jnp.transpose(mod_g[:, :8, :], (1, 0, 2)).reshape(8, 6 * D)
    in_send, in_recv, in_flight, token = _gather_ici_start(own["in"], mod_g, False, "gather_w_in_start")

    def w_in_ready(after):
        landed = _gather_ici_wait(in_send, in_recv, in_flight, after, False, "gather_w_in_wait")
        return _run_carried(_carry_gather_forward([landed], [False]), "gather_forward_in")[0]

    for n in LATE:
        own[n] = _to_bf16_slot(given["w_" + n][0], slots[SWAPPED[n]], "cast_" + n, after=token)
    mod_row = lax.dynamic_slice(mod_all, (me, 0), (1, 6 * D)) + token[0:1, 0:1]
    mod = [mod_row[:, k * D:(k + 1) * D] for k in range(6)]

    core = ac.astype(jnp.int32).reshape(1)
    dist = {"core": core, "slots": slots}

    sel = jnp.asarray(_toeplitz_map())
    rel_pad = jnp.pad(rel_bias[0], ((0, 0), (0, REL_PAD - (2 * MAX_REL + 1))))
    gen = _select_call(rel_pad, sel.T.astype(BF16), "bias_rows")
    small = {"g_pre_mix": g_pre_mix, "g_post_mix": g_post_mix, "b_in": b_in, "gen": gen,
             "w_dw_conv": _unshard_cols(wdc_g[:, :CONV_K, :]), "b_dw_conv": b_dw_conv, "g_conv_ln": g_conv_ln,
             "b_conv_ln": b_conv_ln, "b_conv_o": b_conv_o, "g_pre_ffn": g_pre_ffn, "g_post_ffn": g_post_ffn,
             "w_dw_ffn": _unshard_cols(wdf_g[:, :FFN_K, :]), "b_dw_ffn": _ff_swap(b_dw_ffn)}

    loss_cols, grad_x, dmod, reduced, sm = _local_step(xs, target, mod, w_in_ready, {n: own[n] for n in LATE}, small, dist)

    d_rel = _select_call(sm["gen"], sel.astype(BF16), "bias_fold")[:, :2 * MAX_REL + 1]
    small_grads = {"g_pre_mix": sm["g_pre_mix"], "g_post_mix": sm["g_post_mix"], "b_in": sm["b_in"], "rel_bias": d_rel[None],
                   "b_dw_conv": sm["b_dw_conv"], "g_conv_ln": sm["g_conv_ln"], "b_conv_ln": sm["b_conv_ln"],
                   "b_conv_o": sm["b_conv_o"], "g_pre_ffn": sm["g_pre_ffn"], "g_post_ffn": sm["g_post_ffn"],
                   "b_dw_ffn": _ff_swap(sm["b_dw_ffn"]), "w_dw_conv": sm["w_dw_conv"], "w_dw_ffn": _ff_swap(sm["w_dw_ffn"])}
    order = [n for n in SMALL_ORDER if n != "b_ada"]
    packed, offs = _pack([jnp.concatenate(dmod, axis=1)] + [small_grads[n] for n in order] + [loss_cols])
    mine = lax.dynamic_update_slice(jnp.zeros((8,) + packed.shape, F32), packed[None], (me, 0, 0))
    sg_send, sg_recv, sg_flight, sg_token = _all8_start(mine, "gather_small_grads_start")

    out = {}
    for n in BIG:
        g, dl, nm, nv = _adamw(given["w_" + n][0], reduced[n], given["m_w_" + n][0], given["v_w_" + n][0],
                               "adamw_" + n, sg_token)
        out["grad_w_" + n], out["delta_w_" + n], out["new_m_w_" + n], out["new_v_w_" + n] = g[None], dl[None], nm[None], nv[None]
    every = _all8_wait(sg_send, sg_recv, sg_flight, out["delta_w_in"], "gather_small_grads_wait")
    total = _sum8(every, "sum_small_grads")
    loss = jnp.sum(total[offs[-1]])
    offs = offs[:-1]
    dmod_all = every[:, 0:6, :].reshape(8, 6 * D)
    full_shapes = {n: given[n].shape for n in order}
    full_shapes["w_dw_conv"], full_shapes["w_dw_ffn"] = (1, CONV_K, 512), (1, FFN_K, 2 * D_FF)
    sums = dict(zip(order, _unpack(total, offs[1:], [full_shapes[n] for n in order])))
    sums["b_ada"] = total[0:6].reshape(1, 6 * D)
    sums["w_dw_conv"] = lax.dynamic_slice(sums["w_dw_conv"], (0, 0, shard * 128), (1, CONV_K, 128))
    sums["w_dw_ffn"] = lax.dynamic_slice(sums["w_dw_ffn"], (0, 0, shard * FF_BLOCK), (1, FFN_K, FF_BLOCK))

    upd = dict(zip(SMALL_ORDER, _adamw_many(
        [(given[n], sums[n], given["m_" + n], given["v_" + n]) for n in SMALL_ORDER], "adamw_small")))

    dmod_shard = lax.dynamic_slice(dmod_all, (0, shard * 1536), (8, 1536))
    ada = _ada_bwd_adamw(c_act.T, dmod_shard, w_ada[0], m_w_ada[0], v_w_ada[0], "ada_bwd_adamw")

    out.update({"grad_w_ada": ada[0][None], "delta_w_ada": ada[1][None], "new_m_w_ada": ada[2][None],
                "new_v_w_ada": ada[3][None]})
    for n in SMALL_ORDER:
        out["grad_" + n], out["delta_" + n], out["new_m_" + n], out["new_v_" + n] = sums[n], *upd[n]

    weights = ["w_ada", "b_ada", "g_pre_mix", "g_post_mix", "w_in", "b_in", "rel_bias", "w_attn_o", "w_dw_conv", "b_dw_conv",
               "g_conv_ln", "b_conv_ln", "w_conv_o", "b_conv_o", "w_mix_o", "g_pre_ffn", "g_post_ffn", "w_up", "w_dw_ffn",
               "b_dw_ffn", "w_down"]
    return (loss, grad_x[None], *[out["grad_" + n] for n in weights], *[out["delta_" + n] for n in weights],
            *[out["new_m_" + n] for n in weights], *[out["new_v_" + n] for n in weights])
```

```python
import functools
import math

import numpy as np
import jax
import jax.numpy as jnp
from jax import lax
from jax.experimental import pallas as pl
from jax.experimental.pallas import tpu as pltpu

F32, BF16 = jnp.float32, jnp.bfloat16
MESH = pl.DeviceIdType.MESH

D = 1024
D_IN = 4608
D_FF = 2816
CONV_K = 31
FFN_K = 3
N_HEADS = 8
CHUNK = 64
LEFT_CHUNKS = 8
MAX_REL = 128
EPS = 1e-6
NEG_INF = -1e30
Q_TILE = 256
WINDOW = Q_TILE + LEFT_CHUNKS * CHUNK
STEP_ROWS = 256
REL_PAD = 384
TOEP = 1024
ROW_TILE = 256
VMEM_LIMIT = 60 * 1024 * 1024

ADAM_LR, ADAM_B1, ADAM_B2, ADAM_EPS, ADAM_WD, ADAM_STEP = 0.001, 0.9, 0.999, 1e-08, 0.01, 10


def _params(sem=None):
    return pltpu.CompilerParams(dimension_semantics=sem, vmem_limit_bytes=VMEM_LIMIT)


def _sds(shape, dtype):
    return jax.ShapeDtypeStruct(tuple(shape), dtype)


ANY = pl.BlockSpec(memory_space=pl.ANY)


class _Carried:
    def __init__(self, ins, out_shapes, aliases, n_sems, start, finish):
        self.ins, self.out_shapes, self.aliases = list(ins), list(out_shapes), dict(aliases)
        self.n_sems, self.start, self.finish = n_sems, start, finish


def _call(body, *, grid, in_specs, out_specs, out_shape, scratch_shapes, sem, name, args, carried=None):
    in_specs, out_specs, out_shape = list(in_specs), list(out_specs), list(out_shape)
    scratch_shapes = list(scratch_shapes)
    if carried is None:
        return pl.pallas_call(body, grid=grid, in_specs=in_specs, out_specs=out_specs, out_shape=out_shape,
                              scratch_shapes=scratch_shapes, compiler_params=_params(sem), name=name)(*args)
    n_in, n_out, n_scr = len(in_specs), len(out_specs), len(scratch_shapes)
    c_in, c_out = len(carried.ins), len(carried.out_shapes)

    def full(*refs):
        pos = [0]

        def take(k):
            part = refs[pos[0]:pos[0] + k]
            pos[0] += k
            return part

        ins, cins, outs, couts, scr = take(n_in), take(c_in), take(n_out), take(c_out), take(n_scr)
        send_sems, recv_sems = take(2)
        first = last = None
        for d, size in enumerate(grid):
            pid = pl.program_id(d)
            first = (pid == 0) if first is None else first & (pid == 0)
            last = (pid == size - 1) if last is None else last & (pid == size - 1)

        @pl.when(first)
        def _():
            carried.start(cins, couts, send_sems, recv_sems)

        body(*ins, *outs, *scr)

        @pl.when(last)
        def _():
            carried.finish(cins, couts, send_sems, recv_sems)

    sems = [pltpu.SemaphoreType.DMA((carried.n_sems,)), pltpu.SemaphoreType.DMA((carried.n_sems,))]
    return pl.pallas_call(
        full, grid=grid, in_specs=in_specs + [ANY] * c_in, out_specs=out_specs + [ANY] * c_out,
        out_shape=out_shape + carried.out_shapes, scratch_shapes=scratch_shapes + sems,
        input_output_aliases={n_in + k: n_out + v for k, v in carried.aliases.items()},
        compiler_params=_params(tuple("arbitrary" for _ in grid)), name=name,
    )(*args, *carried.ins)


def _run_carried(carried, name):
    c_in = len(carried.ins)

    def body(*refs):
        cins, couts = refs[:c_in], refs[c_in:c_in + len(carried.out_shapes)]
        send_sems, recv_sems = refs[-2:]
        carried.start(cins, couts, send_sems, recv_sems)
        carried.finish(cins, couts, send_sems, recv_sems)

    return pl.pallas_call(
        body, in_specs=[ANY] * c_in, out_specs=[ANY] * len(carried.out_shapes), out_shape=carried.out_shapes,
        scratch_shapes=[pltpu.SemaphoreType.DMA((carried.n_sems,)), pltpu.SemaphoreType.DMA((carried.n_sems,))],
        input_output_aliases=carried.aliases, name=name,
    )(*carried.ins)


def _start_carried(carried, name):
    c_in, c_out = len(carried.ins), len(carried.out_shapes)

    def body(*refs):
        cins, (send_sems, recv_sems) = refs[:c_in], refs[c_in:c_in + 2]
        couts, token = refs[c_in + 2:c_in + 2 + c_out], refs[-1]
        carried.start(cins, couts, send_sems, recv_sems)
        token[...] = jnp.zeros_like(token)

    hbm, sem, space = pl.BlockSpec(memory_space=pltpu.HBM), pl.BlockSpec(memory_space=pltpu.SEMAPHORE), pltpu.HBM
    res = pl.pallas_call(
        body, name=name,
        out_shape=(pltpu.SemaphoreType.DMA((carried.n_sems,)), pltpu.SemaphoreType.DMA((carried.n_sems,)),
                   *[space(s.shape, s.dtype) for s in carried.out_shapes], jax.ShapeDtypeStruct((8, 128), F32)),
        in_specs=(hbm,) * c_in, out_specs=(sem, sem) + (hbm,) * c_out + (pl.BlockSpec(memory_space=pltpu.VMEM),),
        input_output_aliases={i: 2 + o for i, o in carried.aliases.items()},
        compiler_params=pltpu.CompilerParams(has_side_effects=pltpu.SideEffectType.DATAFLOW_SIDE_EFFECTING),
    )(*[pltpu.with_memory_space_constraint(a, space) for a in carried.ins])
    return (res[0], res[1], list(res[2:2 + c_out])), res[-1]


def _wait_carried(carried, handles, after, name):
    send_sems, recv_sems, flight = handles
    in_place = bool(carried.aliases)
    srcs = [] if in_place else carried.ins
    n_src, c_out = len(srcs), len(flight)

    def body(*refs):
        couts = refs[-c_out:]
        cins = couts if in_place else refs[:n_src]
        carried.finish(cins, couts, refs[n_src + c_out], refs[n_src + c_out + 1])

    hbm, sem = pl.BlockSpec(memory_space=pltpu.HBM), pl.BlockSpec(memory_space=pltpu.SEMAPHORE)
    return pl.pallas_call(
        body, name=name, out_shape=tuple(pltpu.HBM(f.shape, f.dtype) for f in flight),
        in_specs=(hbm,) * (n_src + c_out) + (sem, sem, ANY), out_specs=(hbm,) * c_out,
        input_output_aliases={n_src + k: k for k in range(c_out)},
        compiler_params=pltpu.CompilerParams(has_side_effects=pltpu.SideEffectType.DATAFLOW_SIDE_EFFECTING),
    )(*srcs, *flight, send_sems, recv_sems, after)


def _matmul(a, b, *, form, out_dtype, tm, tn, tk, name, bias=None, add=None, out_sharded=False, carried=None):
    b3 = b.ndim == 3
    resident = 0
    if form == "nn":
        m, k = a.shape
        n = b.shape[0] * b.shape[2] if b3 else b.shape[1]
        dn = (((1,), (0,)), ((), ()))
        a_spec = pl.BlockSpec((tm, tk), lambda i, j, kk: (i, kk))
        if b3 and tn == n and tk == k:
            resident = b.shape[0]
            b_spec = pl.BlockSpec(b.shape, lambda i, j, kk: (0, 0, 0))
        else:
            b_spec = (pl.BlockSpec((None, tk, tn), lambda i, j, kk: (j, kk, 0)) if b3
                      else pl.BlockSpec((tk, tn), lambda i, j, kk: (kk, j)))
    elif form == "nt":
        m, k = a.shape
        n = b.shape[1] if b3 else b.shape[0]
        dn = (((1,), (1,)), ((), ()))
        a_spec = pl.BlockSpec((tm, tk), lambda i, j, kk: (i, kk))
        if b3 and tk == k:
            resident = b.shape[0]
            b_spec = pl.BlockSpec((resident, tn, b.shape[2]), lambda i, j, kk: (0, j, 0))
        else:
            b_spec = (pl.BlockSpec((None, tn, tk), lambda i, j, kk: (kk, j, 0)) if b3
                      else pl.BlockSpec((tn, tk), lambda i, j, kk: (j, kk)))
    else:
        k, m = a.shape
        n = b.shape[1]
        dn = (((0,), (0,)), ((), ()))
        a_spec = pl.BlockSpec((tk, tm), lambda i, j, kk: (kk, i))
        b_spec = pl.BlockSpec((tk, tn), lambda i, j, kk: (kk, j))
    assert m % tm == 0 and n % tn == 0 and k % tk == 0, (name, m, n, k, tm, tn, tk)
    nk = k // tk
    in_specs, args = [a_spec, b_spec], [a, b]
    if bias is not None:
        in_specs.append(pl.BlockSpec((1, tn), lambda i, j, kk: (0, j)))
        args.append(bias)
    if add is not None:
        in_specs.append(pl.BlockSpec((tm, tn), lambda i, j, kk: (i, j)))
        args.append(add)
    if out_sharded:
        out_shape = _sds((n // tn, m, tn), out_dtype)
        out_spec = pl.BlockSpec((None, tm, tn), lambda i, j, kk: (j, i, 0))
    else:
        out_shape = _sds((m, n), out_dtype)
        out_spec = pl.BlockSpec((tm, tn), lambda i, j, kk: (i, j))

    def body(*refs):
        a_ref, b_ref = refs[0], refs[1]
        pos = 2
        bias_ref = add_ref = None
        if bias is not None:
            bias_ref, pos = refs[pos], pos + 1
        if add is not None:
            add_ref, pos = refs[pos], pos + 1
        o_ref = refs[pos]
        if resident and form == "nn":
            ns = b_ref.shape[2]
            for s in range(resident):
                cols = slice(s * ns, (s + 1) * ns)
                ps = lax.dot_general(a_ref[...], b_ref[s], dn, preferred_element_type=F32)
                if bias_ref is not None:
                    ps = ps + bias_ref[:, cols]
                o_ref[:, cols] = ps.astype(o_ref.dtype)
            return
        if resident:
            ks = b_ref.shape[2]
            p = None
            for s in range(resident):
                ps = lax.dot_general(a_ref[:, s * ks:(s + 1) * ks], b_ref[s], dn, preferred_element_type=F32)
                p = ps if p is None else p + ps
        else:
            av, bv = a_ref[...], b_ref[...]
            if av.dtype != BF16:
                av = av.astype(BF16)
            if bv.dtype != BF16:
                bv = bv.astype(BF16)
            p = lax.dot_general(av, bv, dn, preferred_element_type=F32)

        def finish(acc):
            if bias_ref is not None:
                acc = acc + bias_ref[...]
            if add_ref is not None:
                acc = acc + add_ref[...]
            o_ref[...] = acc.astype(o_ref.dtype)

        if nk == 1:
            finish(p)
        else:
            acc_ref = refs[pos + 1]
            kk = pl.program_id(2)

            @pl.when(kk == 0)
            def _():
                acc_ref[...] = p

            @pl.when(kk > 0)
            def _():
                acc_ref[...] += p

            @pl.when(kk == nk - 1)
            def _():
                finish(acc_ref[...])

    res = _call(body, grid=(m // tm, n // tn, nk), in_specs=in_specs, out_specs=[out_spec], out_shape=[out_shape],
                scratch_shapes=[pltpu.VMEM((tm, tn), F32)] if nk > 1 else [],
                sem=("parallel", "parallel", "arbitrary"), name=name, args=args, carried=carried)
    return res[0] if carried is None else (res[0], res[1:])


def _rowcall(fn, rows, consts, row_outs, acc_outs, *, name, tm=ROW_TILE, col_grid=1):
    n_rows = rows[0][0].shape[0]
    assert n_rows % tm == 0
    grid = (col_grid, n_rows // tm)
    in_specs = [pl.BlockSpec((tm, w), functools.partial(lambda c, i, cb: (i, cb + c), cb=cb)) for _, w, cb in rows]
    in_specs += [pl.BlockSpec(k.shape, functools.partial(lambda c, i, nd: (0,) * nd, nd=k.ndim)) for k in consts]
    out_specs = [pl.BlockSpec((tm, w), lambda c, i: (i, c)) for _, _, _, w in row_outs]
    out_specs += [pl.BlockSpec((r, w), lambda c, i: (0, c)) for r, _, w in acc_outs]
    out_shape = [_sds((nr, nc), dt) for nr, nc, dt, _ in row_outs] + [_sds((r, nc), F32) for r, nc, _ in acc_outs]
    n_in, n_ro = len(rows) + len(consts), len(row_outs)

    def body(*refs):
        res = fn(*[r[...] for r in refs[:n_in]])
        if not isinstance(res, (tuple, list)):
            res = (res,)
        outs = refs[n_in:]
        for o_ref, val in zip(outs[:n_ro], res[:n_ro]):
            o_ref[...] = val.astype(o_ref.dtype)
        if acc_outs:
            first = pl.program_id(1) == 0

            @pl.when(first)
            def _():
                for o_ref, val in zip(outs[n_ro:], res[n_ro:]):
                    o_ref[...] = val

            @pl.when(jnp.logical_not(first))
            def _():
                for o_ref, val in zip(outs[n_ro:], res[n_ro:]):
                    o_ref[...] += val

    out = pl.pallas_call(
        body, grid=grid, in_specs=in_specs, out_specs=out_specs, out_shape=out_shape,
        compiler_params=_params(("arbitrary", "arbitrary")), name=name,
    )(*[r[0] for r in rows], *consts)
    return out


def _matmul_rows(b, *, form, tm, tk, fn, rows, consts, row_outs, acc_outs, name, a=None, a_rows=None, a_fn=None,
                 carried=None):
    b3 = b.ndim == 3
    resident = 0
    if form == "nn":
        k, n = b.shape
        b_spec = pl.BlockSpec((tk, n), lambda i, kk: (kk, 0))
        dn = (((1,), (0,)), ((), ()))
    else:
        n = b.shape[1] if b3 else b.shape[0]
        k = b.shape[0] * b.shape[2] if b3 else b.shape[1]
        if b3 and tk == k:
            resident = b.shape[0]
            b_spec = pl.BlockSpec(b.shape, lambda i, kk: (0, 0, 0))
        else:
            b_spec = (pl.BlockSpec((None, n, tk), lambda i, kk: (kk, 0, 0)) if b3
                      else pl.BlockSpec((n, tk), lambda i, kk: (0, kk)))
        dn = (((1,), (1,)), ((), ()))
    nk = k // tk
    lhs_in = [(a, tk, 0)] if a is not None else list(a_rows)
    assert a is not None or nk == 1
    m = lhs_in[0][0].shape[0]
    n_lhs = len(lhs_in)
    in_specs = [pl.BlockSpec((tm, tk), lambda i, kk: (i, kk))] if a is not None else [
        pl.BlockSpec((tm, w), functools.partial(lambda i, kk, cb: (i, cb), cb=cb)) for _, w, cb in a_rows]
    in_specs.append(b_spec)
    in_specs += [pl.BlockSpec((tm, w), functools.partial(lambda i, kk, cb: (i, cb), cb=cb)) for _, w, cb in rows]
    in_specs += [pl.BlockSpec(c.shape, functools.partial(lambda i, kk, nd: (0,) * nd, nd=c.ndim)) for c in consts]
    out_specs = [pl.BlockSpec((tm, w), lambda i, kk: (i, 0)) for _, w in row_outs]
    out_specs += [pl.BlockSpec((r, w), lambda i, kk: (0, 0)) for r, w in acc_outs]
    out_shape = [_sds((m, w), dt) for dt, w in row_outs] + [_sds((r, w), F32) for r, w in acc_outs]
    n_rows, n_consts, n_ro, n_acc = len(rows), len(consts), len(row_outs), len(acc_outs)

    def body(*refs):
        pos = n_lhs + 1
        row_refs, const_refs = refs[pos:pos + n_rows], refs[pos + n_rows:pos + n_rows + n_consts]
        pos += n_rows + n_consts
        out_refs, acc_refs = refs[pos:pos + n_ro], refs[pos + n_ro:pos + n_ro + n_acc]
        i, kk = pl.program_id(0), pl.program_id(1)
        if resident:
            b_ref, ks, p = refs[n_lhs], b.shape[2], None
            for s in range(resident):
                ps = lax.dot_general(refs[0][:, s * ks:(s + 1) * ks], b_ref[s], dn, preferred_element_type=F32)
                p = ps if p is None else p + ps
        else:
            lhs = refs[0][...] if a is not None else a_fn(*[r[...] for r in refs[:n_lhs]]).astype(BF16)
            p = lax.dot_general(lhs, refs[n_lhs][...], dn, preferred_element_type=F32)

        def finish(acc):
            extra = [r[...] for r in row_refs] + [c[...] for c in const_refs]
            res = fn(acc, lhs, *extra) if a is None else fn(acc, *extra)
            for o_ref, val in zip(out_refs, res[:n_ro]):
                o_ref[...] = val.astype(o_ref.dtype)
            if n_acc:
                @pl.when(i == 0)
                def _():
                    for o_ref, val in zip(acc_refs, res[n_ro:]):
                        o_ref[...] = val

                @pl.when(i > 0)
                def _():
                    for o_ref, val in zip(acc_refs, res[n_ro:]):
                        o_ref[...] += val

        if nk == 1:
            finish(p)
        else:
            acc_ref = refs[pos + n_ro + n_acc]

            @pl.when(kk == 0)
            def _():
                acc_ref[...] = p

            @pl.when(kk > 0)
            def _():
                acc_ref[...] += p

            @pl.when(kk == nk - 1)
            def _():
                finish(acc_ref[...])

    res = _call(body, grid=(m // tm, nk), in_specs=in_specs, out_specs=out_specs, out_shape=out_shape,
                scratch_shapes=[pltpu.VMEM((tm, n), F32)] if nk > 1 else [], sem=("arbitrary", "arbitrary"),
                name=name, args=[r[0] for r in lhs_in] + [b] + [r[0] for r in rows] + list(consts), carried=carried)
    own = n_ro + n_acc
    return res[:own] if carried is None else (res[:own], res[own:])


def _colsum(v):
    return jnp.sum(v, axis=0, keepdims=True)


def _sigmoid(v):
    return 1.0 / (1.0 + jnp.exp(-v))


_GELU_C = math.sqrt(2.0 / math.pi)


def _gelu(v):
    return 0.5 * v * (1.0 + jnp.tanh(_GELU_C * (v + 0.044715 * (v * v * v))))


def _gelu_and_grad(v):
    th = jnp.tanh(_GELU_C * (v + 0.044715 * (v * v * v)))
    g = 0.5 * v * (1.0 + th)
    dg = 0.5 * (1.0 + th) + 0.5 * v * (1.0 - th * th) * (_GELU_C * (1.0 + 3.0 * 0.044715 * (v * v)))
    return g, dg


def _rms_stats(v):
    r = lax.rsqrt(jnp.mean(v * v, axis=-1, keepdims=True) + EPS)
    return v * r, r


def _rms_bwd(dn, vn, r):
    return r * (dn - vn * jnp.mean(dn * vn, axis=-1, keepdims=True))


def _pre_norm(x, g, sc, sh, name):
    def fn(xv, gv, scv, shv):
        xn, _ = _rms_stats(xv)
        return (xn * gv) * (1.0 + scv) + shv
    return _rowcall(fn, [(x, D, 0)], [g, sc, sh], [(x.shape[0], D, BF16, D)], [], name=name)[0]


def _pre_norm_bwd(dh, x, dx_other, g, sc, name):
    def fn(dhv, xv, dov, gv, scv):
        xn, r = _rms_stats(xv)
        yn = xn * gv
        dyn = dhv * (1.0 + scv)
        dx = _rms_bwd(dyn * gv, xn, r)
        return dov + dx, _colsum(dhv), _colsum(dhv * yn), _colsum(dyn * xn)
    t = x.shape[0]
    return _rowcall(fn, [(dh, D, 0), (x, D, 0), (dx_other, D, 0)], [g, sc], [(t, D, F32, D)],
                    [(1, D, D)] * 3, name=name)


CONV_HALO = 32


def _layer_norm_parts(u):
    mu = jnp.mean(u, axis=-1, keepdims=True)
    d = u - mu
    r = lax.rsqrt(jnp.mean(d * d, axis=-1, keepdims=True) + EPS)
    return d * r, r


LANES = 128
SUBLANE_ROWS = 8
CONV_ROWS = 64


def _lanes(c):
    return slice(c * LANES, (c + 1) * LANES)


def _conv_branch(z, w_dw, b_dw, g_ln, b_ln, name, tm=ROW_TILE, carried=None):
    t = z.shape[0]
    per = tm // CONV_HALO
    n_chunks = 512 // LANES

    def body(ga_ref, gb_ref, gah_ref, gbh_ref, w_ref, b_ref, g_ref, bl_ref, u1_ref, u3_ref, scr):
        i = pl.program_id(0)
        u0h = jnp.where(i > 0, gah_ref[...] * _sigmoid(gbh_ref[...]), 0.0)
        u0 = ga_ref[...] * _sigmoid(gb_ref[...])
        for c in range(n_chunks):
            scr[c, 0:CONV_HALO, :] = u0h[:, _lanes(c)]
            scr[c, CONV_HALO:CONV_HALO + tm, :] = u0[:, _lanes(c)]
        for c in range(n_chunks):
            for r0 in range(0, tm, CONV_ROWS):
                acc = jnp.zeros((CONV_ROWS, LANES), F32) + b_ref[:, _lanes(c)]
                for j in range(CONV_K):
                    acc = acc + w_ref[j:j + 1, _lanes(c)] * scr[c, pl.ds(r0 + CONV_HALO - (CONV_K - 1) + j, CONV_ROWS), :]
                u1_ref[r0:r0 + CONV_ROWS, _lanes(c)] = acc
        xh, _ = _layer_norm_parts(u1_ref[...])
        u2 = xh * g_ref[...] + bl_ref[...]
        u3_ref[...] = (u2 * _sigmoid(u2)).astype(BF16)

    cur = lambda cb: pl.BlockSpec((tm, 512), lambda i: (i, cb))
    halo = lambda cb: pl.BlockSpec((CONV_HALO, 512), lambda i: (jnp.maximum(i * per - 1, 0), cb))
    whole = lambda a: pl.BlockSpec(a.shape, lambda i: (0, 0))
    res = _call(
        body, grid=(t // tm,),
        in_specs=[cur(3), cur(4), halo(3), halo(4), whole(w_dw), whole(b_dw), whole(g_ln), whole(b_ln)],
        out_specs=[pl.BlockSpec((tm, 512), lambda i: (i, 0))] * 2,
        out_shape=[_sds((t, 512), F32), _sds((t, 512), BF16)],
        scratch_shapes=[pltpu.VMEM((n_chunks, CONV_HALO + tm, LANES), F32)],
        sem=("arbitrary",), name=name, args=(z, z, z, z, w_dw, b_dw, g_ln, b_ln), carried=carried)
    return res[:2] if carried is None else (res[:2], res[2:])


def _conv_branch_bwd(du3, u1, z, w_dw, g_ln, b_ln, name, tm=ROW_TILE, carried=None):
    t = z.shape[0]
    per = tm // CONV_HALO
    last = t // tm - 1
    n_chunks = 512 // LANES

    def du1_of(du3v, u1v, g, b):
        xh, r = _layer_norm_parts(u1v)
        u2 = xh * g + b
        s = _sigmoid(u2)
        du2 = du3v * (s * (1.0 + u2 * (1.0 - s)))
        dxh = du2 * g
        du1 = r * (dxh - jnp.mean(dxh, axis=-1, keepdims=True) - xh * jnp.mean(dxh * xh, axis=-1, keepdims=True))
        return du1, du2, xh

    def body(d_ref, u_ref, dn_ref, un_ref, ga_ref, gb_ref, gah_ref, gbh_ref, w_ref, g_ref, bl_ref,
             dglu_ref, dw_ref, dbdw_ref, dg_ref, dbl_ref, dbin_ref, scr, scd):
        i = pl.program_id(0)
        g, b = g_ref[...], bl_ref[...]
        du1, du2, xh = du1_of(d_ref[...], u_ref[...], g, b)
        du1n, _, _ = du1_of(dn_ref[...], un_ref[...], g, b)
        du1n = jnp.where(i < last, du1n, 0.0)
        sgb = _sigmoid(gb_ref[...])
        ga = ga_ref[...]
        u0 = ga * sgb
        u0h = jnp.where(i > 0, gah_ref[...] * _sigmoid(gbh_ref[...]), 0.0)
        for c in range(n_chunks):
            scd[c, 0:tm, :] = du1[:, _lanes(c)]
            scd[c, tm:tm + CONV_HALO, :] = du1n[:, _lanes(c)]
            scr[c, 0:CONV_HALO, :] = u0h[:, _lanes(c)]
            scr[c, CONV_HALO:CONV_HALO + tm, :] = u0[:, _lanes(c)]

        @pl.when(i == 0)
        def _():
            for ref in (dw_ref, dbdw_ref, dg_ref, dbl_ref, dbin_ref):
                ref[...] = jnp.zeros_like(ref)

        dsg = ga * (sgb * (1.0 - sgb))
        for c in range(n_chunks):
            gate = slice(512 + c * LANES, 512 + (c + 1) * LANES)
            for r0 in range(0, tm, CONV_ROWS):
                rows = slice(r0, r0 + CONV_ROWS)
                du0 = jnp.zeros((CONV_ROWS, LANES), F32)
                for j in range(CONV_K):
                    du0 = du0 + w_ref[j:j + 1, _lanes(c)] * scd[c, pl.ds(r0 + CONV_K - 1 - j, CONV_ROWS), :]
                dga = du0 * sgb[rows, _lanes(c)]
                dgb = du0 * dsg[rows, _lanes(c)]
                dglu_ref[rows, _lanes(c)] = dga.astype(BF16)
                dglu_ref[rows, gate] = dgb.astype(BF16)
                dbin_ref[:, _lanes(c)] += _colsum(dga)
                dbin_ref[:, gate] += _colsum(dgb)
            for j in range(CONV_K):
                dwj = jnp.zeros((SUBLANE_ROWS, LANES), F32)
                for r0 in range(0, tm, CONV_ROWS):
                    prod = (scd[c, pl.ds(r0, CONV_ROWS), :]
                            * scr[c, pl.ds(r0 + CONV_HALO - (CONV_K - 1) + j, CONV_ROWS), :])
                    dwj = dwj + jnp.sum(prod.reshape(CONV_ROWS // SUBLANE_ROWS, SUBLANE_ROWS, LANES), axis=0)
                dw_ref[j:j + 1, _lanes(c)] += _colsum(dwj)
        dbdw_ref[...] += _colsum(du1)
        dg_ref[...] += _colsum(du2 * xh)
        dbl_ref[...] += _colsum(du2)

    cur = lambda cb: pl.BlockSpec((tm, 512), lambda i: (i, cb))
    prev = lambda cb: pl.BlockSpec((CONV_HALO, 512), lambda i: (jnp.maximum(i * per - 1, 0), cb))
    nxt = pl.BlockSpec((CONV_HALO, 512), lambda i: (jnp.minimum((i + 1) * per, t // CONV_HALO - 1), 0))
    whole = lambda a: pl.BlockSpec(a.shape, lambda i: (0, 0))
    acc = lambda r, w: pl.BlockSpec((r, w), lambda i: (0, 0))
    res = _call(
        body, grid=(t // tm,),
        in_specs=[cur(0), cur(0), nxt, nxt, cur(3), cur(4), prev(3), prev(4), whole(w_dw), whole(g_ln), whole(b_ln)],
        out_specs=[pl.BlockSpec((tm, 1024), lambda i: (i, 0)), acc(CONV_K, 512), acc(1, 512), acc(1, 512),
                   acc(1, 512), acc(1, 1024)],
        out_shape=[_sds((t, 1024), BF16), _sds((CONV_K, 512), F32), _sds((1, 512), F32), _sds((1, 512), F32),
                   _sds((1, 512), F32), _sds((1, 1024), F32)],
        scratch_shapes=[pltpu.VMEM((n_chunks, CONV_HALO + tm, LANES), F32),
                        pltpu.VMEM((n_chunks, tm + CONV_HALO, LANES), F32)],
        sem=("arbitrary",), name=name, args=(du3, u1, du3, u1, z, z, z, z, w_dw, g_ln, b_ln), carried=carried)
    return res[:6] if carried is None else (res[:6], res[6:])


FF_BLOCK = D_FF // 2
FF_HALO = 8
FF_CHUNKS = FF_BLOCK // LANES


FF_ROWS = 64
FF_EXT_ROWS = 88


def _ffn_conv(w_ref, b_ref, scr, k, rows, r0=0):
    acc = b_ref[:, _lanes(k)] + w_ref[0:1, _lanes(k)] * scr[k, pl.ds(r0 + FF_HALO - 2, rows), :]
    acc = acc + w_ref[1:2, _lanes(k)] * scr[k, pl.ds(r0 + FF_HALO - 1, rows), :]
    return acc + w_ref[2:3, _lanes(k)] * scr[k, pl.ds(r0 + FF_HALO, rows), :]


def _ffn_act(up, w3, b3, name, tm=ROW_TILE, carried=None):
    t = up.shape[0]
    per = tm // FF_HALO
    wide = 2 * FF_BLOCK

    def body(u_ref, uh_ref, w_ref, b_ref, o_ref, scr):
        i = pl.program_id(1)
        for k in range(2 * FF_CHUNKS):
            scr[k, 0:FF_HALO, :] = jnp.where(i > 0, uh_ref[:, _lanes(k)], 0.0)
            scr[k, FF_HALO:FF_HALO + tm, :] = u_ref[:, _lanes(k)]
        for cc in range(FF_CHUNKS):
            for r0 in range(0, tm, FF_ROWS):
                val = _ffn_conv(w_ref, b_ref, scr, cc, FF_ROWS, r0)
                gate = _ffn_conv(w_ref, b_ref, scr, FF_CHUNKS + cc, FF_ROWS, r0)
                o_ref[r0:r0 + FF_ROWS, _lanes(cc)] = (_gelu(gate) * val).astype(BF16)

    res = _call(
        body, grid=(2, t // tm),
        in_specs=[pl.BlockSpec((tm, wide), lambda c, i: (i, c)),
                  pl.BlockSpec((FF_HALO, wide), lambda c, i: (jnp.maximum(i * per - 1, 0), c)),
                  pl.BlockSpec((FFN_K, wide), lambda c, i: (0, c)),
                  pl.BlockSpec((1, wide), lambda c, i: (0, c))],
        out_specs=[pl.BlockSpec((tm, FF_BLOCK), lambda c, i: (i, c))],
        out_shape=[_sds((t, D_FF), BF16)],
        scratch_shapes=[pltpu.VMEM((2 * FF_CHUNKS, FF_HALO + tm, LANES), F32)],
        sem=("arbitrary", "arbitrary"), name=name, args=(up, up, w3, b3), carried=carried)
    return res[0] if carried is None else (res[0], res[1:])


def _ffn_act_bwd(dact, up, w3, b3, name, tm=ROW_TILE):
    t = up.shape[0]
    per = tm // FF_HALO
    wide = 2 * FF_BLOCK
    last = t // tm - 1
    ext = tm + FF_HALO

    def body(u_ref, up_ref, un_ref, d_ref, dn_ref, w_ref, b_ref, o_ref, dw_ref, db_ref, scr, scd):
        i = pl.program_id(1)
        for k in range(2 * FF_CHUNKS):
            scr[k, 0:FF_HALO, :] = jnp.where(i > 0, up_ref[:, _lanes(k)], 0.0)
            scr[k, FF_HALO:FF_HALO + tm, :] = u_ref[:, _lanes(k)]
            scr[k, FF_HALO + tm:FF_HALO + ext, :] = un_ref[:, _lanes(k)]
        dn = jnp.where(i < last, dn_ref[...], 0.0)

        @pl.when(i == 0)
        def _():
            dw_ref[...] = jnp.zeros_like(dw_ref)
            db_ref[...] = jnp.zeros_like(db_ref)

        for cc in range(FF_CHUNKS):
            gc = FF_CHUNKS + cc
            for r0 in range(0, ext, FF_EXT_ROWS):
                rows = pl.ds(r0, FF_EXT_ROWS)
                val = _ffn_conv(w_ref, b_ref, scr, cc, FF_EXT_ROWS, r0)
                gel, dgel = _gelu_and_grad(_ffn_conv(w_ref, b_ref, scr, gc, FF_EXT_ROWS, r0))
                da = d_ref[r0:r0 + FF_EXT_ROWS, _lanes(cc)] if r0 + FF_EXT_ROWS <= tm else jnp.concatenate(
                    [d_ref[r0:tm, _lanes(cc)], dn[:, _lanes(cc)]], axis=0)
                scd[cc, rows, :] = da * gel
                scd[gc, rows, :] = da * val * dgel
            for k in (cc, gc):
                dwk = [jnp.zeros((SUBLANE_ROWS, LANES), F32) for _ in range(FFN_K)]
                dbk = jnp.zeros((SUBLANE_ROWS, LANES), F32)
                for r0 in range(0, tm, FF_ROWS):
                    shifted = [scd[k, pl.ds(r0 + FFN_K - 1 - j, FF_ROWS), :] for j in range(FFN_K)]
                    ucur = scr[k, pl.ds(r0 + FF_HALO, FF_ROWS), :]
                    o_ref[r0:r0 + FF_ROWS, _lanes(k)] = (
                        w_ref[0:1, _lanes(k)] * shifted[0] + w_ref[1:2, _lanes(k)] * shifted[1]
                        + w_ref[2:3, _lanes(k)] * shifted[2]).astype(BF16)
                    fold = lambda v: jnp.sum(v.reshape(FF_ROWS // SUBLANE_ROWS, SUBLANE_ROWS, LANES), axis=0)
                    for j in range(FFN_K):
                        dwk[j] = dwk[j] + fold(shifted[j] * ucur)
                    dbk = dbk + fold(shifted[FFN_K - 1])
                for j in range(FFN_K):
                    dw_ref[j:j + 1, _lanes(k)] += _colsum(dwk[j])
                db_ref[:, _lanes(k)] += _colsum(dbk)

    nblk = t // FF_HALO
    return pl.pallas_call(
        body, grid=(2, t // tm),
        in_specs=[pl.BlockSpec((tm, wide), lambda c, i: (i, c)),
                  pl.BlockSpec((FF_HALO, wide), lambda c, i: (jnp.maximum(i * per - 1, 0), c)),
                  pl.BlockSpec((FF_HALO, wide), lambda c, i: (jnp.minimum((i + 1) * per, nblk - 1), c)),
                  pl.BlockSpec((tm, FF_BLOCK), lambda c, i: (i, c)),
                  pl.BlockSpec((FF_HALO, FF_BLOCK), lambda c, i: (jnp.minimum((i + 1) * per, nblk - 1), c)),
                  pl.BlockSpec((FFN_K, wide), lambda c, i: (0, c)),
                  pl.BlockSpec((1, wide), lambda c, i: (0, c))],
        out_specs=[pl.BlockSpec((tm, wide), lambda c, i: (i, c)),
                   pl.BlockSpec((FFN_K, wide), lambda c, i: (0, c)),
                   pl.BlockSpec((1, wide), lambda c, i: (0, c))],
        out_shape=[_sds((t, 2 * D_FF), BF16), _sds((FFN_K, 2 * D_FF), F32), _sds((1, 2 * D_FF), F32)],
        scratch_shapes=[pltpu.VMEM((2 * FF_CHUNKS, FF_HALO + ext, LANES), F32),
                        pltpu.VMEM((2 * FF_CHUNKS, ext, LANES), F32)],
        compiler_params=_params(("arbitrary", "arbitrary")), name=name,
    )(up, up, up, dact, dact, w3, b3)


def _toeplitz_map():
    f = np.zeros((TOEP, REL_PAD), np.float32)
    for m in range(TOEP - 1):
        rel = (WINDOW - 1) - m
        f[m, int(np.clip(rel, -MAX_REL, MAX_REL)) + MAX_REL] = 1.0
    return f


def _split3(v):
    hi = v.astype(BF16)
    r1 = v - hi.astype(F32)
    mid = r1.astype(BF16)
    lo = (r1 - mid.astype(F32)).astype(BF16)
    return hi, mid, lo


def _exact_select(v, sel):
    out = None
    for part in _split3(v):
        p = jnp.dot(part, sel, preferred_element_type=F32)
        out = p if out is None else out + p
    return out


def _select_call(v, sel, name):
    def body(v_ref, s_ref, o_ref):
        o_ref[...] = _exact_select(v_ref[...], s_ref[...])
    return pl.pallas_call(body, out_shape=_sds((v.shape[0], sel.shape[1]), F32), name=name)(v, sel)


def _band_bias(gen_row):
    b0 = jnp.broadcast_to(gen_row, (Q_TILE, TOEP))
    bias = pltpu.roll(b0, TOEP - (Q_TILE - 1), 1, stride=1, stride_axis=0)[:, :WINDOW]
    qq = lax.broadcasted_iota(jnp.int32, (Q_TILE, WINDOW), 0) // CHUNK
    kc = lax.broadcasted_iota(jnp.int32, (Q_TILE, WINDOW), 1) // CHUNK
    return jnp.where((kc >= qq) & (kc <= qq + LEFT_CHUNKS), bias, NEG_INF)


PAD_ROWS = WINDOW - Q_TILE
NT_DIMS = (((1,), (1,)), ((), ()))
TN_DIMS = (((0,), (0,)), ((), ()))


def _head_mask(hh):
    lane = lax.broadcasted_iota(jnp.int32, (1, 128), 1)
    return (lane < 64) if hh == 0 else (lane >= 64)


SOFTMAX_ROWS = 16


def _probs_block(s_scr, bias, hh, rows, q_start):
    s = s_scr[rows, :] + bias[hh, rows, :]
    col = lax.broadcasted_iota(jnp.int32, (SOFTMAX_ROWS, WINDOW), 1)
    s = jnp.where(col >= PAD_ROWS - q_start, s, NEG_INF)
    p = jnp.exp(s - jnp.max(s, axis=-1, keepdims=True))
    return p / jnp.sum(p, axis=-1, keepdims=True)


def _attention(z, gen, name, carried=None):
    t = z.shape[0]
    n_i = t // STEP_ROWS

    def body(q_ref, k_ref, v_ref, g_ref, o_ref, kpad, vpad, bias, s_scr, p_scr):
        hp, i = pl.program_id(0), pl.program_id(1)

        @pl.when(i == 0)
        def _():
            kpad[0:PAD_ROWS, :] = jnp.zeros((PAD_ROWS, 128), BF16)
            vpad[0:PAD_ROWS, :] = jnp.zeros((PAD_ROWS, 128), BF16)
            kpad[PAD_ROWS:PAD_ROWS + t, :] = k_ref[...].astype(BF16)
            vpad[PAD_ROWS:PAD_ROWS + t, :] = v_ref[...].astype(BF16)
            for hh in range(2):
                bias[hh] = _band_bias(g_ref[pl.ds(2 * hp + hh, 1), :])

        for q0 in range(0, STEP_ROWS, Q_TILE):
            q_start = i * STEP_ROWS + q0
            win = pl.ds(pl.multiple_of(q_start, Q_TILE), WINDOW)
            out = None
            for hh in range(2):
                mask = _head_mask(hh)
                qm = jnp.where(mask, q_ref[q0:q0 + Q_TILE, :] * (CHUNK ** -0.5), 0.0).astype(BF16)
                slot = 2 * (q0 // Q_TILE) + hh
                s_scr[slot] = lax.dot_general(qm, kpad[win, :], NT_DIMS, preferred_element_type=F32)
                for r0 in range(0, Q_TILE, SOFTMAX_ROWS):
                    rows = slice(r0, r0 + SOFTMAX_ROWS)
                    p_scr[slot, rows, :] = _probs_block(s_scr.at[slot], bias, hh, rows, q_start).astype(BF16)
                o = jnp.dot(p_scr[slot], vpad[win, :], preferred_element_type=F32)
                out = jnp.where(mask, o, 0.0) if out is None else jnp.where(mask, o, out)
            o_ref[q0:q0 + Q_TILE, :] = out.astype(BF16)

    res = _call(
        body, grid=(4, n_i),
        in_specs=[pl.BlockSpec((STEP_ROWS, 128), lambda h, i: (i, h)),
                  pl.BlockSpec((t, 128), lambda h, i: (0, 4 + h)),
                  pl.BlockSpec((t, 128), lambda h, i: (0, 8 + h)),
                  pl.BlockSpec((N_HEADS, TOEP), lambda h, i: (0, 0))],
        out_specs=[pl.BlockSpec((STEP_ROWS, 128), lambda h, i: (i, h))],
        out_shape=[_sds((t, 512), BF16)],
        scratch_shapes=[pltpu.VMEM((PAD_ROWS + t, 128), BF16), pltpu.VMEM((PAD_ROWS + t, 128), BF16),
                        pltpu.VMEM((2, Q_TILE, WINDOW), F32), pltpu.VMEM((4, Q_TILE, WINDOW), F32),
                        pltpu.VMEM((4, Q_TILE, WINDOW), BF16)],
        sem=("arbitrary", "arbitrary"), name=name, args=(z, z, z, gen), carried=carried)
    return res[0] if carried is None else (res[0], res[1:])


def _attention_bwd(z, datt, gen, name, carried=None):
    t = z.shape[0]
    n_i = t // STEP_ROWS

    def body(q_ref, k_ref, v_ref, d_ref, g_ref, dq_ref, dk_ref, dv_ref, sq_ref, sk_ref, sv_ref, dg_ref,
             kpad, vpad, dkacc, dvacc, bias, dsacc, s_scr, dp_scr, p_scr, ds_scr):
        hp, i = pl.program_id(0), pl.program_id(1)

        @pl.when(i == 0)
        def _():
            kpad[0:PAD_ROWS, :] = jnp.zeros((PAD_ROWS, 128), BF16)
            vpad[0:PAD_ROWS, :] = jnp.zeros((PAD_ROWS, 128), BF16)
            kpad[PAD_ROWS:PAD_ROWS + t, :] = k_ref[...].astype(BF16)
            vpad[PAD_ROWS:PAD_ROWS + t, :] = v_ref[...].astype(BF16)
            dkacc[...] = jnp.zeros_like(dkacc)
            dvacc[...] = jnp.zeros_like(dvacc)
            dsacc[...] = jnp.zeros_like(dsacc)
            for hh in range(2):
                bias[hh] = _band_bias(g_ref[pl.ds(2 * hp + hh, 1), :])

        dq_sum = None
        for q0 in range(0, STEP_ROWS, Q_TILE):
            q_start = i * STEP_ROWS + q0
            win = pl.ds(pl.multiple_of(q_start, Q_TILE), WINDOW)
            dq = None
            for hh in range(2):
                mask = _head_mask(hh)
                qm = jnp.where(mask, q_ref[q0:q0 + Q_TILE, :] * (CHUNK ** -0.5), 0.0).astype(BF16)
                dom = jnp.where(mask, d_ref[q0:q0 + Q_TILE, :], 0.0).astype(BF16)
                slot = 2 * (q0 // Q_TILE) + hh
                s_scr[slot] = lax.dot_general(qm, kpad[win, :], NT_DIMS, preferred_element_type=F32)
                dp_scr[slot] = lax.dot_general(dom, vpad[win, :], NT_DIMS, preferred_element_type=F32)
                for r0 in range(0, Q_TILE, SOFTMAX_ROWS):
                    rows = slice(r0, r0 + SOFTMAX_ROWS)
                    p = _probs_block(s_scr.at[slot], bias, hh, rows, q_start)
                    dp = dp_scr[slot, rows, :]
                    ds = p * (dp - jnp.sum(p * dp, axis=-1, keepdims=True))
                    dsacc[hh, rows, :] += ds
                    ds_scr[slot, rows, :] = ds.astype(BF16)
                    p_scr[slot, rows, :] = p.astype(BF16)
                ds16 = ds_scr[slot]
                dqh = jnp.dot(ds16, kpad[win, :], preferred_element_type=F32) * (CHUNK ** -0.5)
                dq = jnp.where(mask, dqh, 0.0) if dq is None else jnp.where(mask, dqh, dq)
                dkacc[win, :] += lax.dot_general(ds16, qm, TN_DIMS, preferred_element_type=F32)
                dvacc[win, :] += lax.dot_general(p_scr[slot], dom, TN_DIMS, preferred_element_type=F32)
            dq_ref[q0:q0 + Q_TILE, :] = dq.astype(BF16)
            dq_sum = _colsum(dq) if dq_sum is None else dq_sum + _colsum(dq)

        @pl.when(i == 0)
        def _():
            sq_ref[...] = dq_sum

        @pl.when(i > 0)
        def _():
            sq_ref[...] += dq_sum

        @pl.when(i == n_i - 1)
        def _():
            dk = dkacc[PAD_ROWS:PAD_ROWS + t, :]
            dv = dvacc[PAD_ROWS:PAD_ROWS + t, :]
            dk_ref[...] = dk.astype(BF16)
            dv_ref[...] = dv.astype(BF16)
            sk_ref[...] = _colsum(dk)
            sv_ref[...] = _colsum(dv)
            rr = lax.broadcasted_iota(jnp.int32, (Q_TILE, Q_TILE), 0)
            cc = lax.broadcasted_iota(jnp.int32, (Q_TILE, Q_TILE), 1)
            rev = jnp.where(rr + cc == Q_TILE - 1, 1.0, 0.0).astype(BF16)
            for hh in range(2):
                acc = None
                for part in _split3(dsacc[hh]):
                    pr = jnp.dot(rev, part, preferred_element_type=F32)
                    acc = pr if acc is None else acc + pr
                wide = jnp.concatenate([acc, jnp.zeros((Q_TILE, TOEP - WINDOW), F32)], axis=1)
                dg_ref[pl.ds(2 * hp + hh, 1), :] = _colsum(pltpu.roll(wide, 0, 1, stride=1, stride_axis=0))

    col = lambda off: pl.BlockSpec((t, 128), lambda h, i: (0, off + h))
    tile = lambda: pl.BlockSpec((STEP_ROWS, 128), lambda h, i: (i, h))
    sums = lambda: pl.BlockSpec((1, 128), lambda h, i: (0, h))
    res = _call(
        body, grid=(4, n_i),
        in_specs=[tile(), col(4), col(8), tile(), pl.BlockSpec((N_HEADS, TOEP), lambda h, i: (0, 0))],
        out_specs=[tile(), col(0), col(0), sums(), sums(), sums(), pl.BlockSpec((N_HEADS, TOEP), lambda h, i: (0, 0))],
        out_shape=[_sds((t, 512), BF16)] * 3 + [_sds((1, 512), F32)] * 3 + [_sds((N_HEADS, TOEP), F32)],
        scratch_shapes=[pltpu.VMEM((PAD_ROWS + t, 128), BF16), pltpu.VMEM((PAD_ROWS + t, 128), BF16),
                        pltpu.VMEM((PAD_ROWS + t, 128), F32), pltpu.VMEM((PAD_ROWS + t, 128), F32),
                        pltpu.VMEM((2, Q_TILE, WINDOW), F32), pltpu.VMEM((2, Q_TILE, WINDOW), F32),
                        pltpu.VMEM((4, Q_TILE, WINDOW), F32), pltpu.VMEM((4, Q_TILE, WINDOW), F32),
                        pltpu.VMEM((4, Q_TILE, WINDOW), BF16), pltpu.VMEM((4, Q_TILE, WINDOW), BF16)],
        sem=("arbitrary", "arbitrary"), name=name, args=(z, z, z, datt, gen), carried=carried)
    return res[:7] if carried is None else (res[:7], res[7:])


def _adamw_math(w, g, m, v):
    m = ADAM_B1 * m + (1.0 - ADAM_B1) * g
    v = ADAM_B2 * v + (1.0 - ADAM_B2) * (g * g)
    m_hat = m / (1.0 - ADAM_B1 ** ADAM_STEP)
    v_hat = v / (1.0 - ADAM_B2 ** ADAM_STEP)
    delta = -ADAM_LR * (m_hat / (jnp.sqrt(v_hat) + ADAM_EPS) + ADAM_WD * w)
    return delta, m, v


def _adamw_many(items, name):
    n = len(items)

    def body(*refs):
        ins, outs = refs[:4 * n], refs[4 * n:]
        for k in range(n):
            w, g, m, v = (r[...] for r in ins[4 * k:4 * k + 4])
            outs[3 * k][...], outs[3 * k + 1][...], outs[3 * k + 2][...] = _adamw_math(w, g, m, v)

    flat = [a for item in items for a in item]
    res = pl.pallas_call(body, out_shape=[_sds(item[0].shape, F32) for item in items for _ in range(3)],
                         name=name)(*flat)
    return [tuple(res[3 * k:3 * k + 3]) for k in range(n)]


def _adamw(w, g, m, v, name, after):
    r, c = w.shape
    tm = next(cand for cand in (256, 176, 128, 64, 32, 16, 8) if r % cand == 0)
    return _rowcall(lambda wv, gv, mv, vv, _: (gv,) + _adamw_math(wv, gv, mv, vv),
                    [(w, c, 0), (g, c, 0), (m, c, 0), (v, c, 0)], [after], [(r, c, F32, c)] * 4, [], name=name, tm=tm)


def _ada_fwd(c_all, w_shard, b_shard, name):
    n = w_shard.shape[1]
    tn = 512

    def body(c_ref, w_ref, b_ref, o_ref, a_ref):
        cv = c_ref[...]
        act = cv * _sigmoid(cv)
        a_ref[...] = act
        o_ref[...] = jnp.dot(act.astype(BF16), w_ref[...].astype(BF16), preferred_element_type=F32) + b_ref[...]

    return pl.pallas_call(
        body, grid=(n // tn,),
        in_specs=[pl.BlockSpec((8, D), lambda j: (0, 0)), pl.BlockSpec((D, tn), lambda j: (0, j)),
                  pl.BlockSpec((1, tn), lambda j: (0, j))],
        out_specs=[pl.BlockSpec((8, tn), lambda j: (0, j)), pl.BlockSpec((8, D), lambda j: (0, 0))],
        out_shape=[_sds((8, n), F32), _sds((8, D), F32)],
        compiler_params=_params(("arbitrary",)), name=name,
    )(c_all, w_shard, b_shard)


def _ada_bwd_adamw(act_t, dmod_shard, w, m, v, name):
    r, c = w.shape
    tm = 256

    def body(a_ref, d_ref, w_ref, m_ref, v_ref, g_ref, dl_ref, nm_ref, nv_ref):
        g = jnp.dot(a_ref[...], d_ref[...], precision=lax.Precision.HIGHEST, preferred_element_type=F32)
        g_ref[...] = g
        dl_ref[...], nm_ref[...], nv_ref[...] = _adamw_math(w_ref[...], g, m_ref[...], v_ref[...])

    blk = pl.BlockSpec((tm, c), lambda i: (i, 0))
    return pl.pallas_call(
        body, grid=(r // tm,),
        in_specs=[pl.BlockSpec((tm, 8), lambda i: (i, 0)), pl.BlockSpec((8, c), lambda i: (0, 0)), blk, blk, blk],
        out_specs=[blk] * 4, out_shape=[_sds((r, c), F32)] * 4,
        compiler_params=_params(("arbitrary",)), name=name,
    )(act_t, dmod_shard, w, m, v)


def _place():
    return lax.axis_index("x"), lax.axis_index("y"), lax.axis_index("c")


def _flip(v, bit):
    return 1 - v if bit else v


VMEM_SPEC = pl.BlockSpec(memory_space=pltpu.VMEM)


def _allgather8(v, name):
    r, c = v.shape

    def body(v_ref, g_ref, tot_ref, send_sems, recv_sems, local_sem):
        x, y, cc = _place()
        sibling = (x, y, 1 - cc)
        chips = [(_flip(x, k & 2), _flip(y, k & 1)) for k in (1, 2, 3)]

        def block(px, py, pc):
            return g_ref.at[4 * px + 2 * py + pc]

        def copy(k, place, to, src=None):
            slot = block(*place)
            return pltpu.make_async_remote_copy(src_ref=slot if src is None else src, dst_ref=slot,
                                                send_sem=send_sems.at[k], recv_sem=recv_sems.at[k],
                                                device_id=to, device_id_type=MESH)

        mine = pltpu.make_async_copy(v_ref, block(x, y, cc), local_sem)
        mine.start()
        first = [copy(0, (x, y, cc), sibling, src=v_ref)]
        first += [copy(1 + j, (x, y, cc), (px, py, cc), src=v_ref) for j, (px, py) in enumerate(chips)]
        for cp in first:
            cp.start()
        passed = [copy(4 + j, (px, py, cc), sibling) for j, (px, py) in enumerate(chips)]
        for j, (px, py) in enumerate(chips):
            copy(1 + j, (px, py, cc), (x, y, cc)).wait_recv()
            passed[j].start()
        copy(0, sibling, (x, y, cc)).wait_recv()
        for j, (px, py) in enumerate(chips):
            copy(4 + j, (px, py, 1 - cc), (x, y, cc)).wait_recv()
        for cp in first + passed:
            cp.wait_send()
        mine.wait()
        tot = g_ref[0]
        for d in range(1, 8):
            tot = tot + g_ref[d]
        tot_ref[...] = tot

    return pl.pallas_call(
        body, in_specs=[VMEM_SPEC], out_specs=[VMEM_SPEC, VMEM_SPEC],
        out_shape=[_sds((8, r, c), F32), _sds((r, c), F32)],
        scratch_shapes=[pltpu.SemaphoreType.DMA((7,)), pltpu.SemaphoreType.DMA((7,)), pltpu.SemaphoreType.DMA],
        compiler_params=pltpu.CompilerParams(vmem_limit_bytes=VMEM_LIMIT), name=name,
    )(v)


def _slot(px, py, swapped):
    return 2 * py + px if swapped else 2 * px + py


def _gather_shards(arrs, swapped, name, in_place=False):
    n = len(arrs)

    def body(*refs):
        ins, outs = refs[:n], refs[n:2 * n]
        send1, recv1, send2, recv2, local_sems = refs[2 * n:]
        x, y, c = _place()
        sibling = (x, y, 1 - c)
        chips = [(_flip(x, k & 2), _flip(y, k & 1)) for k in (1, 2, 3)]
        local_copies, sends = [], []
        for a in range(n):
            h = outs[a].shape[1] // 2
            mine = pl.ds(pl.multiple_of(c * h, 8), h)
            own = _slot(x, y, swapped[a])
            if in_place:
                src = outs[a].at[own, mine]
            else:
                src = ins[a].at[mine]
                lc = pltpu.make_async_copy(ins[a], outs[a].at[own], local_sems.at[a])
                lc.start()
                local_copies.append(lc)
            for j, (px, py) in enumerate(chips):
                cp = pltpu.make_async_remote_copy(
                    src_ref=src, dst_ref=outs[a].at[own, mine], send_sem=send1.at[3 * a + j],
                    recv_sem=recv1.at[3 * a + j], device_id=(px, py, c), device_id_type=MESH)
                cp.start()
                sends.append(cp)
        for a in range(n):
            h = outs[a].shape[1] // 2
            mine = pl.ds(pl.multiple_of(c * h, 8), h)
            for j, (px, py) in enumerate(chips):
                piece = outs[a].at[_slot(px, py, swapped[a]), mine]
                pltpu.make_async_remote_copy(
                    src_ref=piece, dst_ref=piece, send_sem=send1.at[3 * a + j], recv_sem=recv1.at[3 * a + j],
                    device_id=(px, py, c), device_id_type=MESH).wait_recv()
                fwd = pltpu.make_async_remote_copy(
                    src_ref=piece, dst_ref=piece, send_sem=send2.at[3 * a + j], recv_sem=recv2.at[3 * a + j],
                    device_id=sibling, device_id_type=MESH)
                fwd.start()
                sends.append(fwd)
        for a in range(n):
            h = outs[a].shape[1] // 2
            other = pl.ds(pl.multiple_of((1 - c) * h, 8), h)
            for j, (px, py) in enumerate(chips):
                piece = outs[a].at[_slot(px, py, swapped[a]), other]
                pltpu.make_async_remote_copy(
                    src_ref=piece, dst_ref=piece, send_sem=send2.at[3 * a + j], recv_sem=recv2.at[3 * a + j],
                    device_id=sibling, device_id_type=MESH).wait_recv()
        for cp in sends:
            cp.wait_send()
        for lc in local_copies:
            lc.wait()

    dma = lambda k: pltpu.SemaphoreType.DMA((k,))
    return pl.pallas_call(
        body, in_specs=[ANY] * n, out_specs=[ANY] * n,
        out_shape=[_sds(a.shape if in_place else (4,) + a.shape, a.dtype) for a in arrs],
        scratch_shapes=[dma(3 * n), dma(3 * n), dma(3 * n), dma(3 * n), dma(n)],
        input_output_aliases={a: a for a in range(n)} if in_place else {},
        name=name,
    )(*arrs)


def _carry_pair_exchange(grads):
    n = len(grads)

    def copies(ins, outs, send_sems, recv_sems):
        x, y, c = _place()
        cps = []
        for a in range(n):
            h = ins[a].shape[1] // 2
            theirs = pl.ds(pl.multiple_of((1 - c) * h, 8), h)
            cps.append(pltpu.make_async_remote_copy(
                src_ref=ins[a].at[:, theirs, :], dst_ref=outs[a], send_sem=send_sems.at[a], recv_sem=recv_sems.at[a],
                device_id=(x, y, 1 - c), device_id_type=MESH))
        return cps

    def start(*refs):
        for cp in copies(*refs):
            cp.start()

    def finish(*refs):
        for cp in copies(*refs):
            cp.wait()

    return _Carried(grads, [_sds((4, g.shape[1] // 2, g.shape[2]), F32) for g in grads], {}, n, start, finish)


def _pair_sum(grad, recv, core, name):
    _, r, c = grad.shape
    h = r // 2

    def body(core_ref, g_ref, r_ref, o_ref):
        o_ref[...] = (g_ref[...] + r_ref[...]).astype(BF16)

    return pl.pallas_call(
        body,
        grid_spec=pltpu.PrefetchScalarGridSpec(
            num_scalar_prefetch=1, grid=(4,),
            in_specs=[pl.BlockSpec((None, h, c), lambda s, core_ref: (s, core_ref[0], 0)),
                      pl.BlockSpec((None, h, c), lambda s, core_ref: (s, 0, 0))],
            out_specs=pl.BlockSpec((None, h, c), lambda s, core_ref: (s, 0, 0))),
        out_shape=_sds((4, h, c), BF16), compiler_params=_params(("arbitrary",)), name=name,
    )(core, grad, recv)


def _carry_chip_exchange(parts, swapped):
    n = len(parts)

    def copies(ins, outs, send_sems, recv_sems):
        x, y, c = _place()
        chips = [(_flip(x, k & 2), _flip(y, k & 1)) for k in (1, 2, 3)]
        cps = []
        for a in range(n):
            for j, (px, py) in enumerate(chips):
                cps.append(pltpu.make_async_remote_copy(
                    src_ref=ins[a].at[_slot(px, py, swapped[a])], dst_ref=outs[a].at[j],
                    send_sem=send_sems.at[3 * a + j], recv_sem=recv_sems.at[3 * a + j],
                    device_id=(px, py, c), device_id_type=MESH))
        return cps

    def start(*refs):
        for cp in copies(*refs):
            cp.start()

    def finish(*refs):
        for cp in copies(*refs):
            cp.wait()

    return _Carried(parts, [_sds((3,) + p.shape[1:], BF16) for p in parts], {}, 3 * n, start, finish)


def _chip_sum(part, recv, slot_core, name):
    _, h, c = part.shape

    def body(sc_ref, p_ref, r_ref, o_ref):
        acc = p_ref[...].astype(F32)
        for j in range(3):
            acc = acc + r_ref[j].astype(F32)
        o_ref[...] = acc

    return pl.pallas_call(
        body,
        grid_spec=pltpu.PrefetchScalarGridSpec(
            num_scalar_prefetch=1, grid=(1,),
            in_specs=[pl.BlockSpec((None, h, c), lambda q, sc_ref: (sc_ref[0], 0, 0)),
                      pl.BlockSpec((3, h, c), lambda q, sc_ref: (0, 0, 0))],
            out_specs=pl.BlockSpec((h, c), lambda q, sc_ref: (sc_ref[1], 0))),
        out_shape=_sds((2 * h, c), F32), compiler_params=_params(("arbitrary",)), name=name,
    )(slot_core, part, recv)


def _carry_pair_share(shards):
    n = len(shards)

    def copies(outs, send_sems, recv_sems, mine):
        x, y, c = _place()
        cps = []
        for a in range(n):
            h = outs[a].shape[0] // 2
            half = outs[a].at[pl.ds(pl.multiple_of((c if mine else 1 - c) * h, 8), h)]
            cps.append(pltpu.make_async_remote_copy(
                src_ref=half, dst_ref=half, send_sem=send_sems.at[a], recv_sem=recv_sems.at[a],
                device_id=(x, y, 1 - c), device_id_type=MESH))
        return cps

    def start(ins, outs, send_sems, recv_sems):
        for cp in copies(outs, send_sems, recv_sems, True):
            cp.start()

    def finish(ins, outs, send_sems, recv_sems):
        for cp in copies(outs, send_sems, recv_sems, False):
            cp.wait_recv()
        for cp in copies(outs, send_sems, recv_sems, True):
            cp.wait_send()

    return _Carried(shards, [_sds(s.shape, F32) for s in shards], {a: a for a in range(n)}, n, start, finish)


def _carry_gather_ici(bufs, swapped):
    n = len(bufs)

    def copies(outs, send_sems, recv_sems, sending):
        x, y, c = _place()
        cps = []
        for a in range(n):
            h = outs[a].shape[1] // 2
            mine = pl.ds(pl.multiple_of(c * h, 8), h)
            for j, k in enumerate((1, 2, 3)):
                px, py = _flip(x, k & 2), _flip(y, k & 1)
                slot = _slot(x, y, swapped[a]) if sending else _slot(px, py, swapped[a])
                piece = outs[a].at[slot, mine]
                cps.append(pltpu.make_async_remote_copy(
                    src_ref=piece, dst_ref=piece, send_sem=send_sems.at[3 * a + j], recv_sem=recv_sems.at[3 * a + j],
                    device_id=(px, py, c), device_id_type=MESH))
        return cps

    def start(ins, outs, send_sems, recv_sems):
        for cp in copies(outs, send_sems, recv_sems, True):
            cp.start()

    def finish(ins, outs, send_sems, recv_sems):
        for cp in copies(outs, send_sems, recv_sems, False):
            cp.wait_recv()
        for cp in copies(outs, send_sems, recv_sems, True):
            cp.wait_send()

    return _Carried(bufs, [_sds(b.shape, b.dtype) for b in bufs], {a: a for a in range(n)}, 3 * n, start, finish)


HBM_SPEC = pl.BlockSpec(memory_space=pltpu.HBM)
SEM_SPEC = pl.BlockSpec(memory_space=pltpu.SEMAPHORE)
SIDE_EFFECT = pltpu.SideEffectType.DATAFLOW_SIDE_EFFECTING


def _ici_pieces(buf, send_sems, recv_sems, swapped, sending):
    x, y, c = _place()
    h = buf.shape[1] // 2
    mine = pl.ds(pl.multiple_of(c * h, 8), h)
    cps = []
    for j, k in enumerate((1, 2, 3)):
        px, py = _flip(x, k & 2), _flip(y, k & 1)
        piece = buf.at[_slot(x, y, swapped) if sending else _slot(px, py, swapped), mine]
        cps.append(pltpu.make_async_remote_copy(src_ref=piece, dst_ref=piece, send_sem=send_sems.at[j],
                                                recv_sem=recv_sems.at[j], device_id=(px, py, c), device_id_type=MESH))
    return cps


def _gather_ici_start(buf, after, swapped, name):
    def body(buf_ref, after_ref, send_sems, recv_sems, thru, token):
        for cp in _ici_pieces(thru, send_sems, recv_sems, swapped, True):
            cp.start()
        token[...] = jnp.zeros_like(token)

    return pl.pallas_call(
        body, name=name,
        out_shape=(pltpu.SemaphoreType.DMA((3,)), pltpu.SemaphoreType.DMA((3,)), pltpu.HBM(buf.shape, buf.dtype),
                   jax.ShapeDtypeStruct((8, 128), F32)),
        in_specs=(HBM_SPEC, ANY), out_specs=(SEM_SPEC, SEM_SPEC, HBM_SPEC, VMEM_SPEC), input_output_aliases={0: 2},
        compiler_params=pltpu.CompilerParams(has_side_effects=SIDE_EFFECT),
    )(pltpu.with_memory_space_constraint(buf, pltpu.HBM), after)


def _gather_ici_wait(send_sems, recv_sems, thru, after, swapped, name):
    def body(thru_ref, send_sems, recv_sems, after_ref, out_ref):
        for cp in _ici_pieces(out_ref, send_sems, recv_sems, swapped, True):
            cp.wait_send()
        for cp in _ici_pieces(out_ref, send_sems, recv_sems, swapped, False):
            cp.wait_recv()

    return pl.pallas_call(
        body, name=name, out_shape=pltpu.HBM(thru.shape, thru.dtype),
        in_specs=(HBM_SPEC, SEM_SPEC, SEM_SPEC, ANY), out_specs=HBM_SPEC, input_output_aliases={0: 0},
        compiler_params=pltpu.CompilerParams(has_side_effects=SIDE_EFFECT),
    )(thru, send_sems, recv_sems, after)


def _all8_copies(buf, send_sems, recv_sems, sending):
    x, y, c = _place()
    cps = []
    for k in range(1, 8):
        px, py, pc = _flip(x, k & 4), _flip(y, k & 2), _flip(c, k & 1)
        slot = buf.at[4 * x + 2 * y + c] if sending else buf.at[4 * px + 2 * py + pc]
        cps.append(pltpu.make_async_remote_copy(src_ref=slot, dst_ref=slot, send_sem=send_sems.at[k - 1],
                                                recv_sem=recv_sems.at[k - 1], device_id=(px, py, pc), device_id_type=MESH))
    return cps


def _all8_start(buf, name):
    def body(buf_ref, send_sems, recv_sems, thru, token):
        for cp in _all8_copies(thru, send_sems, recv_sems, True):
            cp.start()
        token[...] = jnp.zeros_like(token)

    return pl.pallas_call(
        body, name=name,
        out_shape=(pltpu.SemaphoreType.DMA((7,)), pltpu.SemaphoreType.DMA((7,)), pltpu.HBM(buf.shape, buf.dtype),
                   jax.ShapeDtypeStruct((8, 128), F32)),
        in_specs=(HBM_SPEC,), out_specs=(SEM_SPEC, SEM_SPEC, HBM_SPEC, VMEM_SPEC), input_output_aliases={0: 2},
        compiler_params=pltpu.CompilerParams(has_side_effects=SIDE_EFFECT),
    )(pltpu.with_memory_space_constraint(buf, pltpu.HBM))


def _all8_wait(send_sems, recv_sems, thru, after, name):
    def body(thru_ref, send_sems, recv_sems, after_ref, out_ref):
        for cp in _all8_copies(out_ref, send_sems, recv_sems, True):
            cp.wait_send()
        for cp in _all8_copies(out_ref, send_sems, recv_sems, False):
            cp.wait_recv()

    return pl.pallas_call(
        body, name=name, out_shape=pltpu.HBM(thru.shape, thru.dtype),
        in_specs=(HBM_SPEC, SEM_SPEC, SEM_SPEC, ANY), out_specs=HBM_SPEC, input_output_aliases={0: 0},
        compiler_params=pltpu.CompilerParams(has_side_effects=SIDE_EFFECT),
    )(thru, send_sems, recv_sems, after)


def _sum8(g, name):
    def body(g_ref, o_ref):
        tot = g_ref[0]
        for d in range(1, 8):
            tot = tot + g_ref[d]
        o_ref[...] = tot

    return pl.pallas_call(body, out_shape=_sds(g.shape[1:], F32), name=name)(g)


def _carry_gather_forward(bufs, swapped):
    n = len(bufs)

    def copies(outs, send_sems, recv_sems, sending):
        x, y, c = _place()
        cps = []
        for a in range(n):
            h = outs[a].shape[1] // 2
            rows = pl.ds(pl.multiple_of((c if sending else 1 - c) * h, 8), h)
            for j, k in enumerate((1, 2, 3)):
                piece = outs[a].at[_slot(_flip(x, k & 2), _flip(y, k & 1), swapped[a]), rows]
                cps.append(pltpu.make_async_remote_copy(
                    src_ref=piece, dst_ref=piece, send_sem=send_sems.at[3 * a + j], recv_sem=recv_sems.at[3 * a + j],
                    device_id=(x, y, 1 - c), device_id_type=MESH))
        return cps

    def start(ins, outs, send_sems, recv_sems):
        for cp in copies(outs, send_sems, recv_sems, True):
            cp.start()

    def finish(ins, outs, send_sems, recv_sems):
        for cp in copies(outs, send_sems, recv_sems, False):
            cp.wait_recv()
        for cp in copies(outs, send_sems, recv_sems, True):
            cp.wait_send()

    return _Carried(bufs, [_sds(b.shape, b.dtype) for b in bufs], {a: a for a in range(n)}, 3 * n, start, finish)


def _pack(arrs, rows_multiple=8):
    parts, offs, row = [], [], 0
    for a in arrs:
        flat = a.reshape(-1)
        nrow = -(-flat.shape[0] // D)
        parts.append(jnp.pad(flat, (0, nrow * D - flat.shape[0])))
        offs.append(row)
        row += nrow
    total = -(-row // rows_multiple) * rows_multiple
    if total > row:
        parts.append(jnp.zeros(((total - row) * D,), F32))
    return jnp.concatenate(parts).reshape(total, D), offs


def _unpack(packed, offs, shapes):
    out = []
    for off, shp in zip(offs, shapes):
        size = int(np.prod(shp))
        nrow = -(-size // D)
        out.append(packed[off:off + nrow].reshape(-1)[:size].reshape(shp))
    return out


def _to_bf16_slot(w, slot, name, after=None):
    r, c = w.shape
    tm = next(cand for cand in (256, 176, 128, 64, 32, 16) if r % cand == 0)

    def body(slot_ref, w_ref, *rest):
        rest[-1][...] = w_ref[...].astype(BF16)

    in_specs = [pl.BlockSpec((tm, c), lambda i, slot_ref: (i, 0))]
    if after is not None:
        in_specs.append(pl.BlockSpec((8, 128), lambda i, slot_ref: (0, 0)))
    return pl.pallas_call(
        body,
        grid_spec=pltpu.PrefetchScalarGridSpec(
            num_scalar_prefetch=1, grid=(r // tm,), in_specs=in_specs,
            out_specs=pl.BlockSpec((None, tm, c), lambda i, slot_ref: (slot_ref[0], i, 0))),
        out_shape=_sds((4, r, c), BF16), compiler_params=_params(("arbitrary",)), name=name,
    )(slot, w, *([] if after is None else [after]))


def _unshard_cols(g):
    s, k, n = g.shape
    return jnp.transpose(g, (1, 0, 2)).reshape(k, s * n)


def _ff_swap(v):
    b = FF_BLOCK
    return jnp.concatenate([v[..., 0:b], v[..., 2 * b:3 * b], v[..., b:2 * b], v[..., 3 * b:4 * b]], axis=-1)


LATE = ("attn_o", "conv_o", "mix_o", "up", "down")
EARLY_GRADS = ("down", "up", "mix_o", "attn_o", "conv_o")


def _weight_views(bufs):
    return {"up": bufs["up"], "attn_o": _unshard_cols(bufs["attn_o"]), "conv_o": _unshard_cols(bufs["conv_o"]),
            "mix_o": bufs["mix_o"].reshape(D, D), "down": bufs["down"].reshape(D_FF, D)}


def _pair_sums(names, grads, recv, dist):
    return [_pair_sum(g, r, dist["core"], "pair_sum_" + n) for n, g, r in zip(names, grads, recv)]


def _reduce_halves(names, parts, from_chips, dist):
    return [_chip_sum(p, r, jnp.concatenate([dist["slots"][SWAPPED[n]], dist["core"]]), "chip_sum_" + n)
            for n, p, r in zip(names, parts, from_chips)]


FUSED_TILE = 256
WIDE_TILE = 512


def _gates(z):
    return [(z, 512, 5), (z, 512, 6), (z, 512, 7), (z, 512, 8)]


def _mix_out(a, cb, z, x, w_mix_o, g_post, gt, g_pre2, sc2, sh2, name):
    def lhs(av, cv, ga0, ga1, gb0, gb1):
        ga, gb = jnp.concatenate([ga0, ga1], axis=1), jnp.concatenate([gb0, gb1], axis=1)
        return _sigmoid(ga) * av + _sigmoid(gb) * cv

    def fn(ym, y, xv, gv, gtv, g2v, scv, shv):
        yn, _ = _rms_stats(ym)
        x1 = xv + gtv * (yn * gv)
        xn, _ = _rms_stats(x1)
        return ym, y, x1, (xn * g2v) * (1.0 + scv) + shv

    return _matmul_rows(w_mix_o, form="nn", tm=min(WIDE_TILE, x.shape[0]), tk=D, fn=fn, a_rows=[(a, D, 0), (cb, D, 0)] + _gates(z),
                        a_fn=lhs, rows=[(x, D, 0)], consts=[g_post, gt, g_pre2, sc2, sh2],
                        row_outs=[(F32, D), (BF16, D), (F32, D), (BF16, D)], acc_outs=[], name=name)


def _down_tail(act, w_down, x1, target, g, gt, name):
    def fn(yv, xv, tv, gv, gtv):
        yn, r = _rms_stats(yv)
        e = xv + gtv * (yn * gv) - tv
        dx2 = e * (1.0 / D)
        dyn = dx2 * gtv
        return (dx2, _rms_bwd(dyn * gv, yn, r), _colsum(e * e) * (0.5 / D), _colsum(dyn * yn),
                _colsum(dx2 * (yn * gv)))

    return _matmul_rows(w_down, form="nn", a=act, tm=min(WIDE_TILE, x1.shape[0]), tk=D_FF, fn=fn,
                        rows=[(x1, D, 0), (target, D, 0)], consts=[g, gt], row_outs=[(F32, D), (BF16, D)],
                        acc_outs=[(1, D)] * 3, name=name)


def _up_dx_tail(dup, w_up, x1, dx2, ym, g_pre2, sc2, g_post, gt, name):
    def fn(dh, xv, dov, ymv, g2v, scv, gv, gtv):
        xn, r = _rms_stats(xv)
        dyn = dh * (1.0 + scv)
        dx1 = dov + _rms_bwd(dyn * g2v, xn, r)
        yn, r2 = _rms_stats(ymv)
        dynm = dx1 * gtv
        return (dx1, _rms_bwd(dynm * gv, yn, r2), _colsum(dh), _colsum(dh * (xn * g2v)), _colsum(dyn * xn),
                _colsum(dynm * yn), _colsum(dx1 * (yn * gv)))

    return _matmul_rows(w_up, form="nt", a=dup, tm=min(FUSED_TILE, x1.shape[0]), tk=2 * D_FF, fn=fn,
                        rows=[(x1, D, 0), (dx2, D, 0), (ym, D, 0)], consts=[g_pre2, sc2, g_post, gt],
                        row_outs=[(F32, D), (BF16, D)], acc_outs=[(1, D)] * 5, name=name)


def _mix_dx_gates(dym, w_mix_o, a, cb, z, name):
    def fn(dy, av, cv, ga0, ga1, gb0, gb1):
        sa = _sigmoid(jnp.concatenate([ga0, ga1], axis=1))
        sb = _sigmoid(jnp.concatenate([gb0, gb1], axis=1))
        dcb = dy * sb
        dga = dy * av * (sa * (1.0 - sa))
        dgb = dy * cv * (sb * (1.0 - sb))
        return dy * sa, dcb, dga, dgb, _colsum(dcb), _colsum(dga), _colsum(dgb)

    return _matmul_rows(w_mix_o, form="nt", a=dym, tm=min(WIDE_TILE, a.shape[0]), tk=D, fn=fn,
                        rows=[(a, D, 0), (cb, D, 0)] + _gates(z), consts=[], row_outs=[(BF16, D)] * 4,
                        acc_outs=[(1, D)] * 3, name=name)


def _local_step(x, target, mod, w_in, late, small, dist=None):
    sh_m, sc_m, gt_m, sh_f, sc_f, gt_f = mod
    t = x.shape[0]
    tmm = min(1024, t)
    late_swapped = [SWAPPED[n] for n in LATE]

    h1 = _pre_norm(x, small["g_pre_mix"], sc_m, sh_m, "pre_norm_mix")
    if callable(w_in):
        w_in = w_in(h1)
    z = _matmul(h1, w_in, form="nn", out_dtype=F32, tm=min(FUSED_TILE, t), tn=D_IN, tk=D, bias=small["b_in"], name="mm_in")
    conv = (z, small["w_dw_conv"], small["b_dw_conv"], small["g_conv_ln"], small["b_conv_ln"], "conv_branch")
    if dist is None:
        att = _attention(z, small["gen"], "attention")
        u1, u3 = _conv_branch(*conv)
        bufs = dict(late)
    else:
        mid = [n for n in LATE if n != "down"]
        mid_swapped = [SWAPPED[n] for n in mid]
        att, landed = _attention(z, small["gen"], "attention",
                                 carried=_carry_gather_ici([late[n] for n in mid], mid_swapped))
        (u1, u3), gathered = _conv_branch(*conv, carried=_carry_gather_forward(landed, mid_swapped))
        bufs = dict(zip(mid, gathered))
        bufs["down"] = late["down"]
    w = _weight_views(bufs)
    w["in"] = w_in
    a = _matmul(att, w["attn_o"], form="nn", out_dtype=F32, tm=tmm, tn=512, tk=512, name="mm_attn_o")
    cb = _matmul(u3, w["conv_o"], form="nn", out_dtype=F32, tm=tmm, tn=512, tk=512, bias=small["b_conv_o"], name="mm_conv_o")
    ym, y, x1, h2 = _mix_out(a, cb, z, x, w["mix_o"], small["g_post_mix"], gt_m, small["g_pre_ffn"], sc_f, sh_f, "mix_out")
    mm_up = dict(form="nn", out_dtype=F32, tm=min(FUSED_TILE, t), tn=2 * D_FF, tk=D, name="mm_up")
    ffn_act = (small["w_dw_ffn"], small["b_dw_ffn"], "ffn_act")
    if dist is None:
        up = _matmul(h2, w["up"], **mm_up)
        act = _ffn_act(up, *ffn_act)
    else:
        up, landed = _matmul(h2, w["up"], carried=_carry_gather_ici([late["down"]], [False]), **mm_up)
        act, down = _ffn_act(up, *ffn_act, carried=_carry_gather_forward(landed, [False]))
        w["down"] = down[0].reshape(D_FF, D)

    dx2, dyf, loss_cols, d_g_post_ffn, d_gt_f = _down_tail(act, w["down"], x1, target, small["g_post_ffn"], gt_f, "down_tail")
    dact = _matmul(dyf, w["down"], form="nt", out_dtype=F32, tm=tmm, tn=FF_BLOCK, tk=D, name="mm_down_dx")
    g_down = _matmul(act, dyf, form="tn", out_dtype=F32, tm=FF_BLOCK, tn=512, tk=t, name="mm_down_dw")
    dup, d_w_dw_ffn, d_b_dw_ffn = _ffn_act_bwd(dact, up, small["w_dw_ffn"], small["b_dw_ffn"], "ffn_act_bwd")
    dx1, dym, d_sh_f, d_sc_f, d_g_pre_ffn, d_g_post_mix, d_gt_m = _up_dx_tail(
        dup, w["up"], x1, dx2, ym, small["g_pre_ffn"], sc_f, small["g_post_mix"], gt_m, "up_dx_tail")
    g_up = _matmul(h2, dup, form="tn", out_dtype=F32, tm=512, tn=FF_BLOCK, tk=t, out_sharded=True, name="mm_up_dw")
    da, dcb, dgate_a, dgate_b, d_b_conv_o, sga, sgb = _mix_dx_gates(dym, w["mix_o"], a, cb, z, "mix_dx_gates")
    g_mix_o = _matmul(y, dym, form="tn", out_dtype=F32, tm=D, tn=512, tk=t, name="mm_mix_o_dw")
    datt = _matmul(da, w["attn_o"], form="nt", out_dtype=F32, tm=tmm, tn=512, tk=D, name="mm_attn_o_dx")
    g_attn_o = _matmul(att, da, form="tn", out_dtype=F32, tm=512, tn=256, tk=t, out_sharded=True, name="mm_attn_o_dw")
    du3 = _matmul(dcb, w["conv_o"], form="nt", out_dtype=F32, tm=tmm, tn=512, tk=D, name="mm_conv_o_dx")
    g_conv_o = _matmul(u3, dcb, form="tn", out_dtype=F32, tm=512, tn=256, tk=t, out_sharded=True, name="mm_conv_o_dw")
    big = {"attn_o": g_attn_o, "conv_o": g_conv_o, "mix_o": g_mix_o.reshape(4, 256, D),
           "up": g_up, "down": g_down.reshape(4, D_FF // 4, D)}
    conv_bwd = (du3, u1, z, small["w_dw_conv"], small["g_conv_ln"], small["b_conv_ln"], "conv_branch_bwd")
    in_dw = dict(form="tn", out_dtype=F32, tm=512, tn=1152, tk=t, out_sharded=True, name="mm_in_dw")
    in_dx = dict(form="nt", out_dtype=F32, tm=min(WIDE_TILE, t), tn=D, tk=D_IN, name="mm_in_dx")
    if dist is None:
        dglu, d_w_dw_conv, d_b_dw_conv, d_g_conv_ln, d_b_conv_ln, sglu = _conv_branch_bwd(*conv_bwd)
        dq, dk, dv, sq, sk, sv, dgen = _attention_bwd(z, datt, small["gen"], "attention_bwd")
        dz = jnp.concatenate([dq, dk, dv, dglu, dgate_a, dgate_b], axis=1)
        big["in"] = _matmul(h1, dz, **in_dw)
        dh1 = _matmul(dz, w_in, **in_dx)
    else:
        early = [big[n] for n in EARLY_GRADS]
        (dglu, d_w_dw_conv, d_b_dw_conv, d_g_conv_ln, d_b_conv_ln, sglu), recv = _conv_branch_bwd(
            *conv_bwd, carried=_carry_pair_exchange(early))
        parts = _pair_sums(EARLY_GRADS, early, recv, dist)
        (dq, dk, dv, sq, sk, sv, dgen), from_chips = _attention_bwd(
            z, datt, small["gen"], "attention_bwd",
            carried=_carry_chip_exchange(parts, [SWAPPED[n] for n in EARLY_GRADS]))
        halves = _reduce_halves(EARLY_GRADS, parts, from_chips, dist)
        dz = jnp.concatenate([dq, dk, dv, dglu, dgate_a, dgate_b], axis=1)
        g_in, shards = _matmul(h1, dz, carried=_carry_pair_share(halves), **in_dw)
        big = dict(zip(EARLY_GRADS, shards))
        exchange = _carry_pair_exchange([g_in])
        handles, token = _start_carried(exchange, "pair_exchange_in_start")
        busy = dist["adamw"]("up", big["up"], token)
        recv_in = _wait_carried(exchange, handles, busy, "pair_exchange_in_wait")
        part_in = _pair_sums(("in",), [g_in], recv_in, dist)
        dh1, from_chips_in = _matmul(dz, w_in, carried=_carry_chip_exchange(part_in, [False]), **in_dx)
        half_in = _reduce_halves(("in",), part_in, from_chips_in, dist)
        share = _carry_pair_share(half_in)
        handles, token = _start_carried(share, "pair_share_in_start")
        busy = dist["adamw"]("down", big["down"], token)
        big["in"] = _wait_carried(share, handles, busy, "pair_share_in_wait")[0]
    d_b_in = jnp.concatenate([sq, sk, sv, sglu, sga, sgb], axis=1)
    grad_x, d_sh_m, d_sc_m, d_g_pre_mix = _pre_norm_bwd(dh1, x, dx1, small["g_pre_mix"], sc_m, "pre_norm_mix_bwd")

    dmod = [d_sh_m, d_sc_m, d_gt_m, d_sh_f, d_sc_f, d_gt_f]
    sm = {"g_pre_mix": d_g_pre_mix, "g_post_mix": d_g_post_mix, "b_in": d_b_in, "gen": dgen,
          "w_dw_conv": d_w_dw_conv, "b_dw_conv": d_b_dw_conv, "g_conv_ln": d_g_conv_ln, "b_conv_ln": d_b_conv_ln,
          "b_conv_o": d_b_conv_o, "g_pre_ffn": d_g_pre_ffn, "g_post_ffn": d_g_post_ffn,
          "w_dw_ffn": d_w_dw_ffn, "b_dw_ffn": d_b_dw_ffn}
    return loss_cols, grad_x, dmod, big, sm


BIG = ("in", "attn_o", "conv_o", "mix_o", "up", "down")
SWAPPED = {"in": False, "attn_o": False, "conv_o": False, "mix_o": False, "up": True, "down": False}
SMALL_ORDER = ("b_ada", "g_pre_mix", "g_post_mix", "b_in", "rel_bias", "b_dw_conv", "g_conv_ln", "b_conv_ln",
               "b_conv_o", "g_pre_ffn", "g_post_ffn", "b_dw_ffn", "w_dw_conv", "w_dw_ffn")


def kernel(x, c, w_ada, b_ada, g_pre_mix, g_post_mix, w_in, b_in, rel_bias, w_attn_o, w_dw_conv, b_dw_conv, g_conv_ln, b_conv_ln, w_conv_o, b_conv_o, w_mix_o, g_pre_ffn, g_post_ffn, w_up, w_dw_ffn, b_dw_ffn, w_down, loss_target, m_w_ada, m_b_ada, m_g_pre_mix, m_g_post_mix, m_w_in, m_b_in, m_rel_bias, m_w_attn_o, m_w_dw_conv, m_b_dw_conv, m_g_conv_ln, m_b_conv_ln, m_w_conv_o, m_b_conv_o, m_w_mix_o, m_g_pre_ffn, m_g_post_ffn, m_w_up, m_w_dw_ffn, m_b_dw_ffn, m_w_down, v_w_ada, v_b_ada, v_g_pre_mix, v_g_post_mix, v_w_in, v_b_in, v_rel_bias, v_w_attn_o, v_w_dw_conv, v_b_dw_conv, v_g_conv_ln, v_b_conv_ln, v_w_conv_o, v_b_conv_o, v_w_mix_o, v_g_pre_ffn, v_g_post_ffn, v_w_up, v_w_dw_ffn, v_b_dw_ffn, v_w_down):
    given = dict(locals())
    ax, ay, ac = lax.axis_index("x"), lax.axis_index("y"), lax.axis_index("c")
    shard = 2 * ax + ay
    me = 4 * ax + 2 * ay + ac
    xs, target = x[0], loss_target[0]

    slots = {sw: _slot(ax, ay, sw).astype(jnp.int32).reshape(1) for sw in (False, True)}
    own = {"in": _to_bf16_slot(w_in[0], slots[False], "cast_in")}

    c_pad = jnp.pad(c, ((0, 7), (0, 0)))
    c_g, _ = _allgather8(c_pad, "gather_c")
    c_all = c_g[:, 0, :]
    b_ada_shard = lax.dynamic_slice(b_ada, (0, shard * 1536), (1, 1536))
    mod_shard, c_act = _ada_fwd(c_all, w_ada[0], b_ada_shard, "ada_fwd")
    small_in = [jnp.pad(mod_shard, ((0, 8), (0, 0))),
                jnp.pad(w_dw_conv[0], ((0, 1), (0, 0))),
                jnp.pad(w_dw_ffn[0], ((0, 13), (0, 0)))]
    mod_g, wdc_g, wdf_g = _gather_shards(small_in, [False, False, True], "gather_small")
    mod_all = jnp.transpose(mod_g[:, :8, :], (1, 0, 2)).reshape(8, 6 * D)
    in_send, in_recv, in_flight, token = _gather_ici_start(own["in"], mod_g, False, "gather_w_in_start")

    def w_in_ready(after):
        landed = _gather_ici_wait(in_send, in_recv, in_flight, after, False, "gather_w_in_wait")
        return _run_carried(_carry_gather_forward([landed], [False]), "gather_forward_in")[0]

    for n in LATE:
        own[n] = _to_bf16_slot(given["w_" + n][0], slots[SWAPPED[n]], "cast_" + n, after=token)
    mod_row = lax.dynamic_slice(mod_all, (me, 0), (1, 6 * D)) + token[0:1, 0:1]
    mod = [mod_row[:, k * D:(k + 1) * D] for k in range(6)]

    core = ac.astype(jnp.int32).reshape(1)
    out = {}

    def adamw_big(n, g, token):
        g, dl, nm, nv = _adamw(given["w_" + n][0], g, given["m_w_" + n][0], given["v_w_" + n][0], "adamw_" + n, token)
        out["grad_w_" + n], out["delta_w_" + n], out["new_m_w_" + n], out["new_v_w_" + n] = g[None], dl[None], nm[None], nv[None]
        return dl

    dist = {"core": core, "slots": slots, "adamw": adamw_big}

    sel = jnp.asarray(_toeplitz_map())
    rel_pad = jnp.pad(rel_bias[0], ((0, 0), (0, REL_PAD - (2 * MAX_REL + 1))))
    gen = _select_call(rel_pad, sel.T.astype(BF16), "bias_rows")
    small = {"g_pre_mix": g_pre_mix, "g_post_mix": g_post_mix, "b_in": b_in, "gen": gen,
             "w_dw_conv": _unshard_cols(wdc_g[:, :CONV_K, :]), "b_dw_conv": b_dw_conv, "g_conv_ln": g_conv_ln,
             "b_conv_ln": b_conv_ln, "b_conv_o": b_conv_o, "g_pre_ffn": g_pre_ffn, "g_post_ffn": g_post_ffn,
             "w_dw_ffn": _unshard_cols(wdf_g[:, :FFN_K, :]), "b_dw_ffn": _ff_swap(b_dw_ffn)}

    loss_cols, grad_x, dmod, reduced, sm = _local_step(xs, target, mod, w_in_ready, {n: own[n] for n in LATE}, small, dist)

    d_rel = _select_call(sm["gen"], sel.astype(BF16), "bias_fold")[:, :2 * MAX_REL + 1]
    small_grads = {"g_pre_mix": sm["g_pre_mix"], "g_post_mix": sm["g_post_mix"], "b_in": sm["b_in"], "rel_bias": d_rel[None],
                   "b_dw_conv": sm["b_dw_conv"], "g_conv_ln": sm["g_conv_ln"], "b_conv_ln": sm["b_conv_ln"],
                   "b_conv_o": sm["b_conv_o"], "g_pre_ffn": sm["g_pre_ffn"], "g_post_ffn": sm["g_post_ffn"],
                   "b_dw_ffn": _ff_swap(sm["b_dw_ffn"]), "w_dw_conv": sm["w_dw_conv"], "w_dw_ffn": _ff_swap(sm["w_dw_ffn"])}
    order = [n for n in SMALL_ORDER if n != "b_ada"]
    packed, offs = _pack([jnp.concatenate(dmod, axis=1)] + [small_grads[n] for n in order] + [loss_cols])
    mine = lax.dynamic_update_slice(jnp.zeros((8,) + packed.shape, F32), packed[None], (me, 0, 0))
    sg_send, sg_recv, sg_flight, sg_token = _all8_start(mine, "gather_small_grads_start")

    for n in BIG:
        if "grad_w_" + n not in out:
            busy = adamw_big(n, reduced[n], sg_token)
    every = _all8_wait(sg_send, sg_recv, sg_flight, busy, "gather_small_grads_wait")
    total = _sum8(every, "sum_small_grads")
    loss = jnp.sum(total[offs[-1]])
    offs = offs[:-1]
    dmod_all = every[:, 0:6, :].reshape(8, 6 * D)
    full_shapes = {n: given[n].shape for n in order}
    full_shapes["w_dw_conv"], full_shapes["w_dw_ffn"] = (1, CONV_K, 512), (1, FFN_K, 2 * D_FF)
    sums = dict(zip(order, _unpack(total, offs[1:], [full_shapes[n] for n in order])))
    sums["b_ada"] = total[0:6].reshape(1, 6 * D)
    sums["w_dw_conv"] = lax.dynamic_slice(sums["w_dw_conv"], (0, 0, shard * 128), (1, CONV_K, 128))
    sums["w_dw_ffn"] = lax.dynamic_slice(sums["w_dw_ffn"], (0, 0, shard * FF_BLOCK), (1, FFN_K, FF_BLOCK))

    upd = dict(zip(SMALL_ORDER, _adamw_many(
        [(given[n], sums[n], given["m_" + n], given["v_" + n]) for n in SMALL_ORDER], "adamw_small")))

    dmod_shard = lax.dynamic_slice(dmod_all, (0, shard * 1536), (8, 1536))
    ada = _ada_bwd_adamw(c_act.T, dmod_shard, w_ada[0], m_w_ada[0], v_w_ada[0], "ada_bwd_adamw")

    out.update({"grad_w_ada": ada[0][None], "delta_w_ada": ada[1][None], "new_m_w_ada": ada[2][None],
                "new_v_w_ada": ada[3][None]})
    for n in SMALL_ORDER:
        out["grad_" + n], out["delta_" + n], out["new_m_" + n], out["new_v_" + n] = sums[n], *upd[n]

    weights = ["w_ada", "b_ada", "g_pre_mix", "g_post_mix", "w_in", "b_in", "rel_bias", "w_attn_o", "w_dw_conv", "b_dw_conv",
               "g_conv_ln", "b_conv_ln", "w_conv_o", "b_conv_o", "w_mix_o", "g_pre_ffn", "g_post_ffn", "w_up", "w_dw_ffn",
               "b_dw_ffn", "w_down"]
    return (loss, grad_x[None], *[out["grad_" + n] for n in weights], *[out["delta_" + n] for n in weights],
            *[out["new_m_" + n] for n in weights], *[out["new_v_" + n] for n in weights])
```

```python
import functools
import math

import numpy as np
import jax
import jax.numpy as jnp
from jax import lax
from jax.experimental import pallas as pl
from jax.experimental.pallas import tpu as pltpu

F32, BF16 = jnp.float32, jnp.bfloat16
MESH = pl.DeviceIdType.MESH

D = 1024
D_IN = 4608
D_FF = 2816
CONV_K = 31
FFN_K = 3
N_HEADS = 8
CHUNK = 64
LEFT_CHUNKS = 8
MAX_REL = 128
EPS = 1e-6
NEG_INF = -1e30
Q_TILE = 256
WINDOW = Q_TILE + LEFT_CHUNKS * CHUNK
STEP_ROWS = 256
REL_PAD = 384
TOEP = 1024
ROW_TILE = 256
VMEM_LIMIT = 60 * 1024 * 1024

ADAM_LR, ADAM_B1, ADAM_B2, ADAM_EPS, ADAM_WD, ADAM_STEP = 0.001, 0.9, 0.999, 1e-08, 0.01, 10


def _params(sem=None):
    return pltpu.CompilerParams(dimension_semantics=sem, vmem_limit_bytes=VMEM_LIMIT)


def _sds(shape, dtype):
    return jax.ShapeDtypeStruct(tuple(shape), dtype)


ANY = pl.BlockSpec(memory_space=pl.ANY)


class _Carried:
    def __init__(self, ins, out_shapes, aliases, n_sems, start, finish):
        self.ins, self.out_shapes, self.aliases = list(ins), list(out_shapes), dict(aliases)
        self.n_sems, self.start, self.finish = n_sems, start, finish


def _call(body, *, grid, in_specs, out_specs, out_shape, scratch_shapes, sem, name, args, carried=None):
    in_specs, out_specs, out_shape = list(in_specs), list(out_specs), list(out_shape)
    scratch_shapes = list(scratch_shapes)
    if carried is None:
        return pl.pallas_call(body, grid=grid, in_specs=in_specs, out_specs=out_specs, out_shape=out_shape,
                              scratch_shapes=scratch_shapes, compiler_params=_params(sem), name=name)(*args)
    n_in, n_out, n_scr = len(in_specs), len(out_specs), len(scratch_shapes)
    c_in, c_out = len(carried.ins), len(carried.out_shapes)

    def full(*refs):
        pos = [0]

        def take(k):
            part = refs[pos[0]:pos[0] + k]
            pos[0] += k
            return part

        ins, cins, outs, couts, scr = take(n_in), take(c_in), take(n_out), take(c_out), take(n_scr)
        send_sems, recv_sems = take(2)
        first = last = None
        for d, size in enumerate(grid):
            pid = pl.program_id(d)
            first = (pid == 0) if first is None else first & (pid == 0)
            last = (pid == size - 1) if last is None else last & (pid == size - 1)

        @pl.when(first)
        def _():
            carried.start(cins, couts, send_sems, recv_sems)

        body(*ins, *outs, *scr)

        @pl.when(last)
        def _():
            carried.finish(cins, couts, send_sems, recv_sems)

    sems = [pltpu.SemaphoreType.DMA((carried.n_sems,)), pltpu.SemaphoreType.DMA((carried.n_sems,))]
    return pl.pallas_call(
        full, grid=grid, in_specs=in_specs + [ANY] * c_in, out_specs=out_specs + [ANY] * c_out,
        out_shape=out_shape + carried.out_shapes, scratch_shapes=scratch_shapes + sems,
        input_output_aliases={n_in + k: n_out + v for k, v in carried.aliases.items()},
        compiler_params=_params(tuple("arbitrary" for _ in grid)), name=name,
    )(*args, *carried.ins)


def _run_carried(carried, name):
    c_in = len(carried.ins)

    def body(*refs):
        cins, couts = refs[:c_in], refs[c_in:c_in + len(carried.out_shapes)]
        send_sems, recv_sems = refs[-2:]
        carried.start(cins, couts, send_sems, recv_sems)
        carried.finish(cins, couts, send_sems, recv_sems)

    return pl.pallas_call(
        body, in_specs=[ANY] * c_in, out_specs=[ANY] * len(carried.out_shapes), out_shape=carried.out_shapes,
        scratch_shapes=[pltpu.SemaphoreType.DMA((carried.n_sems,)), pltpu.SemaphoreType.DMA((carried.n_sems,))],
        input_output_aliases=carried.aliases, name=name,
    )(*carried.ins)


def _start_carried(carried, name):
    c_in, c_out = len(carried.ins), len(carried.out_shapes)

    def body(*refs):
        cins, (send_sems, recv_sems) = refs[:c_in], refs[c_in:c_in + 2]
        couts, token = refs[c_in + 2:c_in + 2 + c_out], refs[-1]
        carried.start(cins, couts, send_sems, recv_sems)
        token[...] = jnp.zeros_like(token)

    hbm, sem, space = pl.BlockSpec(memory_space=pltpu.HBM), pl.BlockSpec(memory_space=pltpu.SEMAPHORE), pltpu.HBM
    res = pl.pallas_call(
        body, name=name,
        out_shape=(pltpu.SemaphoreType.DMA((carried.n_sems,)), pltpu.SemaphoreType.DMA((carried.n_sems,)),
                   *[space(s.shape, s.dtype) for s in carried.out_shapes], jax.ShapeDtypeStruct((8, 128), F32)),
        in_specs=(hbm,) * c_in, out_specs=(sem, sem) + (hbm,) * c_out + (pl.BlockSpec(memory_space=pltpu.VMEM),),
        input_output_aliases={i: 2 + o for i, o in carried.aliases.items()},
        compiler_params=pltpu.CompilerParams(has_side_effects=pltpu.SideEffectType.DATAFLOW_SIDE_EFFECTING),
    )(*[pltpu.with_memory_space_constraint(a, space) for a in carried.ins])
    return (res[0], res[1], list(res[2:2 + c_out])), res[-1]


def _wait_carried(carried, handles, after, name):
    send_sems, recv_sems, flight = handles
    in_place = bool(carried.aliases)
    srcs = [] if in_place else carried.ins
    n_src, c_out = len(srcs), len(flight)

    def body(*refs):
        couts = refs[-c_out:]
        cins = couts if in_place else refs[:n_src]
        carried.finish(cins, couts, refs[n_src + c_out], refs[n_src + c_out + 1])

    hbm, sem = pl.BlockSpec(memory_space=pltpu.HBM), pl.BlockSpec(memory_space=pltpu.SEMAPHORE)
    return pl.pallas_call(
        body, name=name, out_shape=tuple(pltpu.HBM(f.shape, f.dtype) for f in flight),
        in_specs=(hbm,) * (n_src + c_out) + (sem, sem, ANY), out_specs=(hbm,) * c_out,
        input_output_aliases={n_src + k: k for k in range(c_out)},
        compiler_params=pltpu.CompilerParams(has_side_effects=pltpu.SideEffectType.DATAFLOW_SIDE_EFFECTING),
    )(*srcs, *flight, send_sems, recv_sems, after)


def _matmul(a, b, *, form, out_dtype, tm, tn, tk, name, bias=None, add=None, out_sharded=False, carried=None):
    b3 = b.ndim == 3
    resident = 0
    if form == "nn":
        m, k = a.shape
        n = b.shape[0] * b.shape[2] if b3 else b.shape[1]
        dn = (((1,), (0,)), ((), ()))
        a_spec = pl.BlockSpec((tm, tk), lambda i, j, kk: (i, kk))
        if b3 and tn == n and tk == k:
            resident = b.shape[0]
            b_spec = pl.BlockSpec(b.shape, lambda i, j, kk: (0, 0, 0))
        else:
            b_spec = (pl.BlockSpec((None, tk, tn), lambda i, j, kk: (j, kk, 0)) if b3
                      else pl.BlockSpec((tk, tn), lambda i, j, kk: (kk, j)))
    elif form == "nt":
        m, k = a.shape
        n = b.shape[1] if b3 else b.shape[0]
        dn = (((1,), (1,)), ((), ()))
        a_spec = pl.BlockSpec((tm, tk), lambda i, j, kk: (i, kk))
        if b3 and tk == k:
            resident = b.shape[0]
            b_spec = pl.BlockSpec((resident, tn, b.shape[2]), lambda i, j, kk: (0, j, 0))
        else:
            b_spec = (pl.BlockSpec((None, tn, tk), lambda i, j, kk: (kk, j, 0)) if b3
                      else pl.BlockSpec((tn, tk), lambda i, j, kk: (j, kk)))
    else:
        k, m = a.shape
        n = b.shape[1]
        dn = (((0,), (0,)), ((), ()))
        a_spec = pl.BlockSpec((tk, tm), lambda i, j, kk: (kk, i))
        b_spec = pl.BlockSpec((tk, tn), lambda i, j, kk: (kk, j))
    assert m % tm == 0 and n % tn == 0 and k % tk == 0, (name, m, n, k, tm, tn, tk)
    nk = k // tk
    in_specs, args = [a_spec, b_spec], [a, b]
    if bias is not None:
        in_specs.append(pl.BlockSpec((1, tn), lambda i, j, kk: (0, j)))
        args.append(bias)
    if add is not None:
        in_specs.append(pl.BlockSpec((tm, tn), lambda i, j, kk: (i, j)))
        args.append(add)
    if out_sharded:
        out_shape = _sds((n // tn, m, tn), out_dtype)
        out_spec = pl.BlockSpec((None, tm, tn), lambda i, j, kk: (j, i, 0))
    else:
        out_shape = _sds((m, n), out_dtype)
        out_spec = pl.BlockSpec((tm, tn), lambda i, j, kk: (i, j))

    def body(*refs):
        a_ref, b_ref = refs[0], refs[1]
        pos = 2
        bias_ref = add_ref = None
        if bias is not None:
            bias_ref, pos = refs[pos], pos + 1
        if add is not None:
            add_ref, pos = refs[pos], pos + 1
        o_ref = refs[pos]
        if resident and form == "nn":
            ns = b_ref.shape[2]
            for s in range(resident):
                cols = slice(s * ns, (s + 1) * ns)
                ps = lax.dot_general(a_ref[...], b_ref[s], dn, preferred_element_type=F32)
                if bias_ref is not None:
                    ps = ps + bias_ref[:, cols]
                o_ref[:, cols] = ps.astype(o_ref.dtype)
            return
        if resident:
            ks = b_ref.shape[2]
            p = None
            for s in range(resident):
                ps = lax.dot_general(a_ref[:, s * ks:(s + 1) * ks], b_ref[s], dn, preferred_element_type=F32)
                p = ps if p is None else p + ps
        else:
            av, bv = a_ref[...], b_ref[...]
            if av.dtype != BF16:
                av = av.astype(BF16)
            if bv.dtype != BF16:
                bv = bv.astype(BF16)
            p = lax.dot_general(av, bv, dn, preferred_element_type=F32)

        def finish(acc):
            if bias_ref is not None:
                acc = acc + bias_ref[...]
            if add_ref is not None:
                acc = acc + add_ref[...]
            o_ref[...] = acc.astype(o_ref.dtype)

        if nk == 1:
            finish(p)
        else:
            acc_ref = refs[pos + 1]
            kk = pl.program_id(2)

            @pl.when(kk == 0)
            def _():
                acc_ref[...] = p

            @pl.when(kk > 0)
            def _():
                acc_ref[...] += p

            @pl.when(kk == nk - 1)
            def _():
                finish(acc_ref[...])

    res = _call(body, grid=(m // tm, n // tn, nk), in_specs=in_specs, out_specs=[out_spec], out_shape=[out_shape],
                scratch_shapes=[pltpu.VMEM((tm, tn), F32)] if nk > 1 else [],
                sem=("parallel", "parallel", "arbitrary"), name=name, args=args, carried=carried)
    return res[0] if carried is None else (res[0], res[1:])


def _rowcall(fn, rows, consts, row_outs, acc_outs, *, name, tm=ROW_TILE, col_grid=1):
    n_rows = rows[0][0].shape[0]
    assert n_rows % tm == 0
    grid = (col_grid, n_rows // tm)
    in_specs = [pl.BlockSpec((tm, w), functools.partial(lambda c, i, cb: (i, cb + c), cb=cb)) for _, w, cb in rows]
    in_specs += [pl.BlockSpec(k.shape, functools.partial(lambda c, i, nd: (0,) * nd, nd=k.ndim)) for k in consts]
    out_specs = [pl.BlockSpec((tm, w), lambda c, i: (i, c)) for _, _, _, w in row_outs]
    out_specs += [pl.BlockSpec((r, w), lambda c, i: (0, c)) for r, _, w in acc_outs]
    out_shape = [_sds((nr, nc), dt) for nr, nc, dt, _ in row_outs] + [_sds((r, nc), F32) for r, nc, _ in acc_outs]
    n_in, n_ro = len(rows) + len(consts), len(row_outs)

    def body(*refs):
        res = fn(*[r[...] for r in refs[:n_in]])
        if not isinstance(res, (tuple, list)):
            res = (res,)
        outs = refs[n_in:]
        for o_ref, val in zip(outs[:n_ro], res[:n_ro]):
            o_ref[...] = val.astype(o_ref.dtype)
        if acc_outs:
            first = pl.program_id(1) == 0

            @pl.when(first)
            def _():
                for o_ref, val in zip(outs[n_ro:], res[n_ro:]):
                    o_ref[...] = val

            @pl.when(jnp.logical_not(first))
            def _():
                for o_ref, val in zip(outs[n_ro:], res[n_ro:]):
                    o_ref[...] += val

    out = pl.pallas_call(
        body, grid=grid, in_specs=in_specs, out_specs=out_specs, out_shape=out_shape,
        compiler_params=_params(("arbitrary", "arbitrary")), name=name,
    )(*[r[0] for r in rows], *consts)
    return out


def _matmul_rows(b, *, form, tm, tk, fn, rows, consts, row_outs, acc_outs, name, a=None, a_rows=None, a_fn=None,
                 carried=None):
    b3 = b.ndim == 3
    resident = 0
    if form == "nn":
        k, n = b.shape
        b_spec = pl.BlockSpec((tk, n), lambda i, kk: (kk, 0))
        dn = (((1,), (0,)), ((), ()))
    else:
        n = b.shape[1] if b3 else b.shape[0]
        k = b.shape[0] * b.shape[2] if b3 else b.shape[1]
        if b3 and tk == k:
            resident = b.shape[0]
            b_spec = pl.BlockSpec(b.shape, lambda i, kk: (0, 0, 0))
        else:
            b_spec = (pl.BlockSpec((None, n, tk), lambda i, kk: (kk, 0, 0)) if b3
                      else pl.BlockSpec((n, tk), lambda i, kk: (0, kk)))
        dn = (((1,), (1,)), ((), ()))
    nk = k // tk
    lhs_in = [(a, tk, 0)] if a is not None else list(a_rows)
    assert a is not None or nk == 1
    m = lhs_in[0][0].shape[0]
    n_lhs = len(lhs_in)
    in_specs = [pl.BlockSpec((tm, tk), lambda i, kk: (i, kk))] if a is not None else [
        pl.BlockSpec((tm, w), functools.partial(lambda i, kk, cb: (i, cb), cb=cb)) for _, w, cb in a_rows]
    in_specs.append(b_spec)
    in_specs += [pl.BlockSpec((tm, w), functools.partial(lambda i, kk, cb: (i, cb), cb=cb)) for _, w, cb in rows]
    in_specs += [pl.BlockSpec(c.shape, functools.partial(lambda i, kk, nd: (0,) * nd, nd=c.ndim)) for c in consts]
    out_specs = [pl.BlockSpec((tm, w), lambda i, kk: (i, 0)) for _, w in row_outs]
    out_specs += [pl.BlockSpec((r, w), lambda i, kk: (0, 0)) for r, w in acc_outs]
    out_shape = [_sds((m, w), dt) for dt, w in row_outs] + [_sds((r, w), F32) for r, w in acc_outs]
    n_rows, n_consts, n_ro, n_acc = len(rows), len(consts), len(row_outs), len(acc_outs)

    def body(*refs):
        pos = n_lhs + 1
        row_refs, const_refs = refs[pos:pos + n_rows], refs[pos + n_rows:pos + n_rows + n_consts]
        pos += n_rows + n_consts
        out_refs, acc_refs = refs[pos:pos + n_ro], refs[pos + n_ro:pos + n_ro + n_acc]
        i, kk = pl.program_id(0), pl.program_id(1)
        if resident:
            b_ref, ks, p = refs[n_lhs], b.shape[2], None
            for s in range(resident):
                ps = lax.dot_general(refs[0][:, s * ks:(s + 1) * ks], b_ref[s], dn, preferred_element_type=F32)
                p = ps if p is None else p + ps
        else:
            lhs = refs[0][...] if a is not None else a_fn(*[r[...] for r in refs[:n_lhs]]).astype(BF16)
            p = lax.dot_general(lhs, refs[n_lhs][...], dn, preferred_element_type=F32)

        def finish(acc):
            extra = [r[...] for r in row_refs] + [c[...] for c in const_refs]
            res = fn(acc, lhs, *extra) if a is None else fn(acc, *extra)
            for o_ref, val in zip(out_refs, res[:n_ro]):
                o_ref[...] = val.astype(o_ref.dtype)
            if n_acc:
                @pl.when(i == 0)
                def _():
                    for o_ref, val in zip(acc_refs, res[n_ro:]):
                        o_ref[...] = val

                @pl.when(i > 0)
                def _():
                    for o_ref, val in zip(acc_refs, res[n_ro:]):
                        o_ref[...] += val

        if nk == 1:
            finish(p)
        else:
            acc_ref = refs[pos + n_ro + n_acc]

            @pl.when(kk == 0)
            def _():
                acc_ref[...] = p

            @pl.when(kk > 0)
            def _():
                acc_ref[...] += p

            @pl.when(kk == nk - 1)
            def _():
                finish(acc_ref[...])

    res = _call(body, grid=(m // tm, nk), in_specs=in_specs, out_specs=out_specs, out_shape=out_shape,
                scratch_shapes=[pltpu.VMEM((tm, n), F32)] if nk > 1 else [], sem=("arbitrary", "arbitrary"),
                name=name, args=[r[0] for r in lhs_in] + [b] + [r[0] for r in rows] + list(consts), carried=carried)
    own = n_ro + n_acc
    return res[:own] if carried is None else (res[:own], res[own:])


def _colsum(v):
    return jnp.sum(v, axis=0, keepdims=True)


def _sigmoid(v):
    return 1.0 / (1.0 + jnp.exp(-v))


_GELU_C = math.sqrt(2.0 / math.pi)


def _gelu(v):
    return 0.5 * v * (1.0 + jnp.tanh(_GELU_C * (v + 0.044715 * (v * v * v))))


def _gelu_and_grad(v):
    th = jnp.tanh(_GELU_C * (v + 0.044715 * (v * v * v)))
    g = 0.5 * v * (1.0 + th)
    dg = 0.5 * (1.0 + th) + 0.5 * v * (1.0 - th * th) * (_GELU_C * (1.0 + 3.0 * 0.044715 * (v * v)))
    return g, dg


def _rms_stats(v):
    r = lax.rsqrt(jnp.mean(v * v, axis=-1, keepdims=True) + EPS)
    return v * r, r


def _rms_bwd(dn, vn, r):
    return r * (dn - vn * jnp.mean(dn * vn, axis=-1, keepdims=True))


def _pre_norm(x, g, sc, sh, name):
    def fn(xv, gv, scv, shv):
        xn, _ = _rms_stats(xv)
        return (xn * gv) * (1.0 + scv) + shv
    return _rowcall(fn, [(x, D, 0)], [g, sc, sh], [(x.shape[0], D, BF16, D)], [], name=name)[0]


def _pre_norm_bwd(dh, x, dx_other, g, sc, name):
    def fn(dhv, xv, dov, gv, scv):
        xn, r = _rms_stats(xv)
        yn = xn * gv
        dyn = dhv * (1.0 + scv)
        dx = _rms_bwd(dyn * gv, xn, r)
        return dov + dx, _colsum(dhv), _colsum(dhv * yn), _colsum(dyn * xn)
    t = x.shape[0]
    return _rowcall(fn, [(dh, D, 0), (x, D, 0), (dx_other, D, 0)], [g, sc], [(t, D, F32, D)],
                    [(1, D, D)] * 3, name=name)


CONV_HALO = 32


def _layer_norm_parts(u):
    mu = jnp.mean(u, axis=-1, keepdims=True)
    d = u - mu
    r = lax.rsqrt(jnp.mean(d * d, axis=-1, keepdims=True) + EPS)
    return d * r, r


LANES = 128
SUBLANE_ROWS = 8
CONV_ROWS = 64


def _lanes(c):
    return slice(c * LANES, (c + 1) * LANES)


def _conv_branch(z, w_dw, b_dw, g_ln, b_ln, name, tm=ROW_TILE, carried=None):
    t = z.shape[0]
    per = tm // CONV_HALO
    n_chunks = 512 // LANES

    def body(ga_ref, gb_ref, gah_ref, gbh_ref, w_ref, b_ref, g_ref, bl_ref, u1_ref, u3_ref, scr):
        i = pl.program_id(0)
        u0h = jnp.where(i > 0, gah_ref[...] * _sigmoid(gbh_ref[...]), 0.0)
        u0 = ga_ref[...] * _sigmoid(gb_ref[...])
        for c in range(n_chunks):
            scr[c, 0:CONV_HALO, :] = u0h[:, _lanes(c)]
            scr[c, CONV_HALO:CONV_HALO + tm, :] = u0[:, _lanes(c)]
        for c in range(n_chunks):
            for r0 in range(0, tm, CONV_ROWS):
                acc = jnp.zeros((CONV_ROWS, LANES), F32) + b_ref[:, _lanes(c)]
                for j in range(CONV_K):
                    acc = acc + w_ref[j:j + 1, _lanes(c)] * scr[c, pl.ds(r0 + CONV_HALO - (CONV_K - 1) + j, CONV_ROWS), :]
                u1_ref[r0:r0 + CONV_ROWS, _lanes(c)] = acc
        xh, _ = _layer_norm_parts(u1_ref[...])
        u2 = xh * g_ref[...] + bl_ref[...]
        u3_ref[...] = (u2 * _sigmoid(u2)).astype(BF16)

    cur = lambda cb: pl.BlockSpec((tm, 512), lambda i: (i, cb))
    halo = lambda cb: pl.BlockSpec((CONV_HALO, 512), lambda i: (jnp.maximum(i * per - 1, 0), cb))
    whole = lambda a: pl.BlockSpec(a.shape, lambda i: (0, 0))
    res = _call(
        body, grid=(t // tm,),
        in_specs=[cur(3), cur(4), halo(3), halo(4), whole(w_dw), whole(b_dw), whole(g_ln), whole(b_ln)],
        out_specs=[pl.BlockSpec((tm, 512), lambda i: (i, 0))] * 2,
        out_shape=[_sds((t, 512), F32), _sds((t, 512), BF16)],
        scratch_shapes=[pltpu.VMEM((n_chunks, CONV_HALO + tm, LANES), F32)],
        sem=("arbitrary",), name=name, args=(z, z, z, z, w_dw, b_dw, g_ln, b_ln), carried=carried)
    return res[:2] if carried is None else (res[:2], res[2:])


def _conv_branch_bwd(du3, u1, z, w_dw, g_ln, b_ln, name, tm=ROW_TILE, carried=None):
    t = z.shape[0]
    per = tm // CONV_HALO
    last = t // tm - 1
    n_chunks = 512 // LANES

    def du1_of(du3v, u1v, g, b):
        xh, r = _layer_norm_parts(u1v)
        u2 = xh * g + b
        s = _sigmoid(u2)
        du2 = du3v * (s * (1.0 + u2 * (1.0 - s)))
        dxh = du2 * g
        du1 = r * (dxh - jnp.mean(dxh, axis=-1, keepdims=True) - xh * jnp.mean(dxh * xh, axis=-1, keepdims=True))
        return du1, du2, xh

    def body(d_ref, u_ref, dn_ref, un_ref, ga_ref, gb_ref, gah_ref, gbh_ref, w_ref, g_ref, bl_ref,
             dglu_ref, dw_ref, dbdw_ref, dg_ref, dbl_ref, dbin_ref, scr, scd):
        i = pl.program_id(0)
        g, b = g_ref[...], bl_ref[...]
        du1, du2, xh = du1_of(d_ref[...], u_ref[...], g, b)
        du1n, _, _ = du1_of(dn_ref[...], un_ref[...], g, b)
        du1n = jnp.where(i < last, du1n, 0.0)
        sgb = _sigmoid(gb_ref[...])
        ga = ga_ref[...]
        u0 = ga * sgb
        u0h = jnp.where(i > 0, gah_ref[...] * _sigmoid(gbh_ref[...]), 0.0)
        for c in range(n_chunks):
            scd[c, 0:tm, :] = du1[:, _lanes(c)]
            scd[c, tm:tm + CONV_HALO, :] = du1n[:, _lanes(c)]
            scr[c, 0:CONV_HALO, :] = u0h[:, _lanes(c)]
            scr[c, CONV_HALO:CONV_HALO + tm, :] = u0[:, _lanes(c)]

        @pl.when(i == 0)
        def _():
            for ref in (dw_ref, dbdw_ref, dg_ref, dbl_ref, dbin_ref):
                ref[...] = jnp.zeros_like(ref)

        dsg = ga * (sgb * (1.0 - sgb))
        for c in range(n_chunks):
            gate = slice(512 + c * LANES, 512 + (c + 1) * LANES)
            for r0 in range(0, tm, CONV_ROWS):
                rows = slice(r0, r0 + CONV_ROWS)
                du0 = jnp.zeros((CONV_ROWS, LANES), F32)
                for j in range(CONV_K):
                    du0 = du0 + w_ref[j:j + 1, _lanes(c)] * scd[c, pl.ds(r0 + CONV_K - 1 - j, CONV_ROWS), :]
                dga = du0 * sgb[rows, _lanes(c)]
                dgb = du0 * dsg[rows, _lanes(c)]
                dglu_ref[rows, _lanes(c)] = dga.astype(BF16)
                dglu_ref[rows, gate] = dgb.astype(BF16)
                dbin_ref[:, _lanes(c)] += _colsum(dga)
                dbin_ref[:, gate] += _colsum(dgb)
            for j in range(CONV_K):
                dwj = jnp.zeros((SUBLANE_ROWS, LANES), F32)
                for r0 in range(0, tm, CONV_ROWS):
                    prod = (scd[c, pl.ds(r0, CONV_ROWS), :]
                            * scr[c, pl.ds(r0 + CONV_HALO - (CONV_K - 1) + j, CONV_ROWS), :])
                    dwj = dwj + jnp.sum(prod.reshape(CONV_ROWS // SUBLANE_ROWS, SUBLANE_ROWS, LANES), axis=0)
                dw_ref[j:j + 1, _lanes(c)] += _colsum(dwj)
        dbdw_ref[...] += _colsum(du1)
        dg_ref[...] += _colsum(du2 * xh)
        dbl_ref[...] += _colsum(du2)

    cur = lambda cb: pl.BlockSpec((tm, 512), lambda i: (i, cb))
    prev = lambda cb: pl.BlockSpec((CONV_HALO, 512), lambda i: (jnp.maximum(i * per - 1, 0), cb))
    nxt = pl.BlockSpec((CONV_HALO, 512), lambda i: (jnp.minimum((i + 1) * per, t // CONV_HALO - 1), 0))
    whole = lambda a: pl.BlockSpec(a.shape, lambda i: (0, 0))
    acc = lambda r, w: pl.BlockSpec((r, w), lambda i: (0, 0))
    res = _call(
        body, grid=(t // tm,),
        in_specs=[cur(0), cur(0), nxt, nxt, cur(3), cur(4), prev(3), prev(4), whole(w_dw), whole(g_ln), whole(b_ln)],
        out_specs=[pl.BlockSpec((tm, 1024), lambda i: (i, 0)), acc(CONV_K, 512), acc(1, 512), acc(1, 512),
                   acc(1, 512), acc(1, 1024)],
        out_shape=[_sds((t, 1024), BF16), _sds((CONV_K, 512), F32), _sds((1, 512), F32), _sds((1, 512), F32),
                   _sds((1, 512), F32), _sds((1, 1024), F32)],
        scratch_shapes=[pltpu.VMEM((n_chunks, CONV_HALO + tm, LANES), F32),
                        pltpu.VMEM((n_chunks, tm + CONV_HALO, LANES), F32)],
        sem=("arbitrary",), name=name, args=(du3, u1, du3, u1, z, z, z, z, w_dw, g_ln, b_ln), carried=carried)
    return res[:6] if carried is None else (res[:6], res[6:])


FF_BLOCK = D_FF // 2
FF_HALO = 8
FF_CHUNKS = FF_BLOCK // LANES


FF_ROWS = 64
FF_EXT_ROWS = 88


def _ffn_conv(w_ref, b_ref, scr, k, rows, r0=0):
    acc = b_ref[:, _lanes(k)] + w_ref[0:1, _lanes(k)] * scr[k, pl.ds(r0 + FF_HALO - 2, rows), :]
    acc = acc + w_ref[1:2, _lanes(k)] * scr[k, pl.ds(r0 + FF_HALO - 1, rows), :]
    return acc + w_ref[2:3, _lanes(k)] * scr[k, pl.ds(r0 + FF_HALO, rows), :]


def _ffn_act(up, w3, b3, name, tm=ROW_TILE, carried=None):
    t = up.shape[0]
    per = tm // FF_HALO
    wide = 2 * FF_BLOCK

    def body(u_ref, uh_ref, w_ref, b_ref, o_ref, scr):
        i = pl.program_id(1)
        for k in range(2 * FF_CHUNKS):
            scr[k, 0:FF_HALO, :] = jnp.where(i > 0, uh_ref[:, _lanes(k)], 0.0)
            scr[k, FF_HALO:FF_HALO + tm, :] = u_ref[:, _lanes(k)]
        for cc in range(FF_CHUNKS):
            for r0 in range(0, tm, FF_ROWS):
                val = _ffn_conv(w_ref, b_ref, scr, cc, FF_ROWS, r0)
                gate = _ffn_conv(w_ref, b_ref, scr, FF_CHUNKS + cc, FF_ROWS, r0)
                o_ref[r0:r0 + FF_ROWS, _lanes(cc)] = (_gelu(gate) * val).astype(BF16)

    res = _call(
        body, grid=(2, t // tm),
        in_specs=[pl.BlockSpec((tm, wide), lambda c, i: (i, c)),
                  pl.BlockSpec((FF_HALO, wide), lambda c, i: (jnp.maximum(i * per - 1, 0), c)),
                  pl.BlockSpec((FFN_K, wide), lambda c, i: (0, c)),
                  pl.BlockSpec((1, wide), lambda c, i: (0, c))],
        out_specs=[pl.BlockSpec((tm, FF_BLOCK), lambda c, i: (i, c))],
        out_shape=[_sds((t, D_FF), BF16)],
        scratch_shapes=[pltpu.VMEM((2 * FF_CHUNKS, FF_HALO + tm, LANES), F32)],
        sem=("arbitrary", "arbitrary"), name=name, args=(up, up, w3, b3), carried=carried)
    return res[0] if carried is None else (res[0], res[1:])


def _ffn_act_bwd(dact, up, w3, b3, name, tm=ROW_TILE):
    t = up.shape[0]
    per = tm // FF_HALO
    wide = 2 * FF_BLOCK
    last = t // tm - 1
    ext = tm + FF_HALO

    def body(u_ref, up_ref, un_ref, d_ref, dn_ref, w_ref, b_ref, o_ref, dw_ref, db_ref, scr, scd):
        i = pl.program_id(1)
        for k in range(2 * FF_CHUNKS):
            scr[k, 0:FF_HALO, :] = jnp.where(i > 0, up_ref[:, _lanes(k)], 0.0)
            scr[k, FF_HALO:FF_HALO + tm, :] = u_ref[:, _lanes(k)]
            scr[k, FF_HALO + tm:FF_HALO + ext, :] = un_ref[:, _lanes(k)]
        dn = jnp.where(i < last, dn_ref[...], 0.0)

        @pl.when(i == 0)
        def _():
            dw_ref[...] = jnp.zeros_like(dw_ref)
            db_ref[...] = jnp.zeros_like(db_ref)

        for cc in range(FF_CHUNKS):
            gc = FF_CHUNKS + cc
            for r0 in range(0, ext, FF_EXT_ROWS):
                rows = pl.ds(r0, FF_EXT_ROWS)
                val = _ffn_conv(w_ref, b_ref, scr, cc, FF_EXT_ROWS, r0)
                gel, dgel = _gelu_and_grad(_ffn_conv(w_ref, b_ref, scr, gc, FF_EXT_ROWS, r0))
                da = d_ref[r0:r0 + FF_EXT_ROWS, _lanes(cc)] if r0 + FF_EXT_ROWS <= tm else jnp.concatenate(
                    [d_ref[r0:tm, _lanes(cc)], dn[:, _lanes(cc)]], axis=0)
                scd[cc, rows, :] = da * gel
                scd[gc, rows, :] = da * val * dgel
            for k in (cc, gc):
                dwk = [jnp.zeros((SUBLANE_ROWS, LANES), F32) for _ in range(FFN_K)]
                dbk = jnp.zeros((SUBLANE_ROWS, LANES), F32)
                for r0 in range(0, tm, FF_ROWS):
                    shifted = [scd[k, pl.ds(r0 + FFN_K - 1 - j, FF_ROWS), :] for j in range(FFN_K)]
                    ucur = scr[k, pl.ds(r0 + FF_HALO, FF_ROWS), :]
                    o_ref[r0:r0 + FF_ROWS, _lanes(k)] = (
                        w_ref[0:1, _lanes(k)] * shifted[0] + w_ref[1:2, _lanes(k)] * shifted[1]
                        + w_ref[2:3, _lanes(k)] * shifted[2]).astype(BF16)
                    fold = lambda v: jnp.sum(v.reshape(FF_ROWS // SUBLANE_ROWS, SUBLANE_ROWS, LANES), axis=0)
                    for j in range(FFN_K):
                        dwk[j] = dwk[j] + fold(shifted[j] * ucur)
                    dbk = dbk + fold(shifted[FFN_K - 1])
                for j in range(FFN_K):
                    dw_ref[j:j + 1, _lanes(k)] += _colsum(dwk[j])
                db_ref[:, _lanes(k)] += _colsum(dbk)

    nblk = t // FF_HALO
    return pl.pallas_call(
        body, grid=(2, t // tm),
        in_specs=[pl.BlockSpec((tm, wide), lambda c, i: (i, c)),
                  pl.BlockSpec((FF_HALO, wide), lambda c, i: (jnp.maximum(i * per - 1, 0), c)),
                  pl.BlockSpec((FF_HALO, wide), lambda c, i: (jnp.minimum((i + 1) * per, nblk - 1), c)),
                  pl.BlockSpec((tm, FF_BLOCK), lambda c, i: (i, c)),
                  pl.BlockSpec((FF_HALO, FF_BLOCK), lambda c, i: (jnp.minimum((i + 1) * per, nblk - 1), c)),
                  pl.BlockSpec((FFN_K, wide), lambda c, i: (0, c)),
                  pl.BlockSpec((1, wide), lambda c, i: (0, c))],
        out_specs=[pl.BlockSpec((tm, wide), lambda c, i: (i, c)),
                   pl.BlockSpec((FFN_K, wide), lambda c, i: (0, c)),
                   pl.BlockSpec((1, wide), lambda c, i: (0, c))],
        out_shape=[_sds((t, 2 * D_FF), BF16), _sds((FFN_K, 2 * D_FF), F32), _sds((1, 2 * D_FF), F32)],
        scratch_shapes=[pltpu.VMEM((2 * FF_CHUNKS, FF_HALO + ext, LANES), F32),
                        pltpu.VMEM((2 * FF_CHUNKS, ext, LANES), F32)],
        compiler_params=_params(("arbitrary", "arbitrary")), name=name,
    )(up, up, up, dact, dact, w3, b3)


def _toeplitz_map():
    f = np.zeros((TOEP, REL_PAD), np.float32)
    for m in range(TOEP - 1):
        rel = (WINDOW - 1) - m
        f[m, int(np.clip(rel, -MAX_REL, MAX_REL)) + MAX_REL] = 1.0
    return f


def _split3(v):
    hi = v.astype(BF16)
    r1 = v - hi.astype(F32)
    mid = r1.astype(BF16)
    lo = (r1 - mid.astype(F32)).astype(BF16)
    return hi, mid, lo


def _exact_select(v, sel):
    out = None
    for part in _split3(v):
        p = jnp.dot(part, sel, preferred_element_type=F32)
        out = p if out is None else out + p
    return out


def _select_call(v, sel, name):
    def body(v_ref, s_ref, o_ref):
        o_ref[...] = _exact_select(v_ref[...], s_ref[...])
    return pl.pallas_call(body, out_shape=_sds((v.shape[0], sel.shape[1]), F32), name=name)(v, sel)


def _band_bias(gen_row):
    b0 = jnp.broadcast_to(gen_row, (Q_TILE, TOEP))
    bias = pltpu.roll(b0, TOEP - (Q_TILE - 1), 1, stride=1, stride_axis=0)[:, :WINDOW]
    qq = lax.broadcasted_iota(jnp.int32, (Q_TILE, WINDOW), 0) // CHUNK
    kc = lax.broadcasted_iota(jnp.int32, (Q_TILE, WINDOW), 1) // CHUNK
    return jnp.where((kc >= qq) & (kc <= qq + LEFT_CHUNKS), bias, NEG_INF)


PAD_ROWS = WINDOW - Q_TILE
NT_DIMS = (((1,), (1,)), ((), ()))
TN_DIMS = (((0,), (0,)), ((), ()))


def _head_mask(hh):
    lane = lax.broadcasted_iota(jnp.int32, (1, 128), 1)
    return (lane < 64) if hh == 0 else (lane >= 64)


SOFTMAX_ROWS = 16


def _probs_block(s_scr, bias, hh, rows, q_start):
    s = s_scr[rows, :] + bias[hh, rows, :]
    col = lax.broadcasted_iota(jnp.int32, (SOFTMAX_ROWS, WINDOW), 1)
    s = jnp.where(col >= PAD_ROWS - q_start, s, NEG_INF)
    p = jnp.exp(s - jnp.max(s, axis=-1, keepdims=True))
    return p / jnp.sum(p, axis=-1, keepdims=True)


def _attention(z, gen, name, carried=None):
    t = z.shape[0]
    n_i = t // STEP_ROWS

    def body(q_ref, k_ref, v_ref, g_ref, o_ref, kpad, vpad, bias, s_scr, p_scr):
        hp, i = pl.program_id(0), pl.program_id(1)

        @pl.when(i == 0)
        def _():
            kpad[0:PAD_ROWS, :] = jnp.zeros((PAD_ROWS, 128), BF16)
            vpad[0:PAD_ROWS, :] = jnp.zeros((PAD_ROWS, 128), BF16)
            kpad[PAD_ROWS:PAD_ROWS + t, :] = k_ref[...].astype(BF16)
            vpad[PAD_ROWS:PAD_ROWS + t, :] = v_ref[...].astype(BF16)
            for hh in range(2):
                bias[hh] = _band_bias(g_ref[pl.ds(2 * hp + hh, 1), :])

        for q0 in range(0, STEP_ROWS, Q_TILE):
            q_start = i * STEP_ROWS + q0
            win = pl.ds(pl.multiple_of(q_start, Q_TILE), WINDOW)
            out = None
            for hh in range(2):
                mask = _head_mask(hh)
                qm = jnp.where(mask, q_ref[q0:q0 + Q_TILE, :] * (CHUNK ** -0.5), 0.0).astype(BF16)
                slot = 2 * (q0 // Q_TILE) + hh
                s_scr[slot] = lax.dot_general(qm, kpad[win, :], NT_DIMS, preferred_element_type=F32)
                for r0 in range(0, Q_TILE, SOFTMAX_ROWS):
                    rows = slice(r0, r0 + SOFTMAX_ROWS)
                    p_scr[slot, rows, :] = _probs_block(s_scr.at[slot], bias, hh, rows, q_start).astype(BF16)
                o = jnp.dot(p_scr[slot], vpad[win, :], preferred_element_type=F32)
                out = jnp.where(mask, o, 0.0) if out is None else jnp.where(mask, o, out)
            o_ref[q0:q0 + Q_TILE, :] = out.astype(BF16)

    res = _call(
        body, grid=(4, n_i),
        in_specs=[pl.BlockSpec((STEP_ROWS, 128), lambda h, i: (i, h)),
                  pl.BlockSpec((t, 128), lambda h, i: (0, 4 + h)),
                  pl.BlockSpec((t, 128), lambda h, i: (0, 8 + h)),
                  pl.BlockSpec((N_HEADS, TOEP), lambda h, i: (0, 0))],
        out_specs=[pl.BlockSpec((STEP_ROWS, 128), lambda h, i: (i, h))],
        out_shape=[_sds((t, 512), BF16)],
        scratch_shapes=[pltpu.VMEM((PAD_ROWS + t, 128), BF16), pltpu.VMEM((PAD_ROWS + t, 128), BF16),
                        pltpu.VMEM((2, Q_TILE, WINDOW), F32), pltpu.VMEM((4, Q_TILE, WINDOW), F32),
                        pltpu.VMEM((4, Q_TILE, WINDOW), BF16)],
        sem=("arbitrary", "arbitrary"), name=name, args=(z, z, z, gen), carried=carried)
    return res[0] if carried is None else (res[0], res[1:])


def _attention_bwd(z, datt, gen, name, carried=None):
    t = z.shape[0]
    n_i = t // STEP_ROWS

    def body(q_ref, k_ref, v_ref, d_ref, g_ref, dq_ref, dk_ref, dv_ref, sq_ref, sk_ref, sv_ref, dg_ref,
             kpad, vpad, dkacc, dvacc, bias, dsacc, s_scr, dp_scr, p_scr, ds_scr):
        hp, i = pl.program_id(0), pl.program_id(1)

        @pl.when(i == 0)
        def _():
            kpad[0:PAD_ROWS, :] = jnp.zeros((PAD_ROWS, 128), BF16)
            vpad[0:PAD_ROWS, :] = jnp.zeros((PAD_ROWS, 128), BF16)
            kpad[PAD_ROWS:PAD_ROWS + t, :] = k_ref[...].astype(BF16)
            vpad[PAD_ROWS:PAD_ROWS + t, :] = v_ref[...].astype(BF16)
            dkacc[...] = jnp.zeros_like(dkacc)
            dvacc[...] = jnp.zeros_like(dvacc)
            dsacc[...] = jnp.zeros_like(dsacc)
            for hh in range(2):
                bias[hh] = _band_bias(g_ref[pl.ds(2 * hp + hh, 1), :])

        dq_sum = None
        for q0 in range(0, STEP_ROWS, Q_TILE):
            q_start = i * STEP_ROWS + q0
            win = pl.ds(pl.multiple_of(q_start, Q_TILE), WINDOW)
            dq = None
            for hh in range(2):
                mask = _head_mask(hh)
                qm = jnp.where(mask, q_ref[q0:q0 + Q_TILE, :] * (CHUNK ** -0.5), 0.0).astype(BF16)
                dom = jnp.where(mask, d_ref[q0:q0 + Q_TILE, :], 0.0).astype(BF16)
                slot = 2 * (q0 // Q_TILE) + hh
                s_scr[slot] = lax.dot_general(qm, kpad[win, :], NT_DIMS, preferred_element_type=F32)
                dp_scr[slot] = lax.dot_general(dom, vpad[win, :], NT_DIMS, preferred_element_type=F32)
                for r0 in range(0, Q_TILE, SOFTMAX_ROWS):
                    rows = slice(r0, r0 + SOFTMAX_ROWS)
                    p = _probs_block(s_scr.at[slot], bias, hh, rows, q_start)
                    dp = dp_scr[slot, rows, :]
                    ds = p * (dp - jnp.sum(p * dp, axis=-1, keepdims=True))
                    dsacc[hh, rows, :] += ds
                    ds_scr[slot, rows, :] = ds.astype(BF16)
                    p_scr[slot, rows, :] = p.astype(BF16)
                ds16 = ds_scr[slot]
                dqh = jnp.dot(ds16, kpad[win, :], preferred_element_type=F32) * (CHUNK ** -0.5)
                dq = jnp.where(mask, dqh, 0.0) if dq is None else jnp.where(mask, dqh, dq)
                dkacc[win, :] += lax.dot_general(ds16, qm, TN_DIMS, preferred_element_type=F32)
                dvacc[win, :] += lax.dot_general(p_scr[slot], dom, TN_DIMS, preferred_element_type=F32)
            dq_ref[q0:q0 + Q_TILE, :] = dq.astype(BF16)
            dq_sum = _colsum(dq) if dq_sum is None else dq_sum + _colsum(dq)

        @pl.when(i == 0)
        def _():
            sq_ref[...] = dq_sum

        @pl.when(i > 0)
        def _():
            sq_ref[...] += dq_sum

        @pl.when(i == n_i - 1)
        def _():
            dk = dkacc[PAD_ROWS:PAD_ROWS + t, :]
            dv = dvacc[PAD_ROWS:PAD_ROWS + t, :]
            dk_ref[...] = dk.astype(BF16)
            dv_ref[...] = dv.astype(BF16)
            sk_ref[...] = _colsum(dk)
            sv_ref[...] = _colsum(dv)
            rr = lax.broadcasted_iota(jnp.int32, (Q_TILE, Q_TILE), 0)
            cc = lax.broadcasted_iota(jnp.int32, (Q_TILE, Q_TILE), 1)
            rev = jnp.where(rr + cc == Q_TILE - 1, 1.0, 0.0).astype(BF16)
            for hh in range(2):
                acc = None
                for part in _split3(dsacc[hh]):
                    pr = jnp.dot(rev, part, preferred_element_type=F32)
                    acc = pr if acc is None else acc + pr
                wide = jnp.concatenate([acc, jnp.zeros((Q_TILE, TOEP - WINDOW), F32)], axis=1)
                dg_ref[pl.ds(2 * hp + hh, 1), :] = _colsum(pltpu.roll(wide, 0, 1, stride=1, stride_axis=0))

    col = lambda off: pl.BlockSpec((t, 128), lambda h, i: (0, off + h))
    tile = lambda: pl.BlockSpec((STEP_ROWS, 128), lambda h, i: (i, h))
    sums = lambda: pl.BlockSpec((1, 128), lambda h, i: (0, h))
    res = _call(
        body, grid=(4, n_i),
        in_specs=[tile(), col(4), col(8), tile(), pl.BlockSpec((N_HEADS, TOEP), lambda h, i: (0, 0))],
        out_specs=[tile(), col(0), col(0), sums(), sums(), sums(), pl.BlockSpec((N_HEADS, TOEP), lambda h, i: (0, 0))],
        out_shape=[_sds((t, 512), BF16)] * 3 + [_sds((1, 512), F32)] * 3 + [_sds((N_HEADS, TOEP), F32)],
        scratch_shapes=[pltpu.VMEM((PAD_ROWS + t, 128), BF16), pltpu.VMEM((PAD_ROWS + t, 128), BF16),
                        pltpu.VMEM((PAD_ROWS + t, 128), F32), pltpu.VMEM((PAD_ROWS + t, 128), F32),
                        pltpu.VMEM((2, Q_TILE, WINDOW), F32), pltpu.VMEM((2, Q_TILE, WINDOW), F32),
                        pltpu.VMEM((4, Q_TILE, WINDOW), F32), pltpu.VMEM((4, Q_TILE, WINDOW), F32),
                        pltpu.VMEM((4, Q_TILE, WINDOW), BF16), pltpu.VMEM((4, Q_TILE, WINDOW), BF16)],
        sem=("arbitrary", "arbitrary"), name=name, args=(z, z, z, datt, gen), carried=carried)
    return res[:7] if carried is None else (res[:7], res[7:])


def _adamw_math(w, g, m, v):
    m = ADAM_B1 * m + (1.0 - ADAM_B1) * g
    v = ADAM_B2 * v + (1.0 - ADAM_B2) * (g * g)
    m_hat = m / (1.0 - ADAM_B1 ** ADAM_STEP)
    v_hat = v / (1.0 - ADAM_B2 ** ADAM_STEP)
    delta = -ADAM_LR * (m_hat / (jnp.sqrt(v_hat) + ADAM_EPS) + ADAM_WD * w)
    return delta, m, v


def _adamw_many(items, name):
    n = len(items)

    def body(*refs):
        ins, outs = refs[:4 * n], refs[4 * n:]
        for k in range(n):
            w, g, m, v = (r[...] for r in ins[4 * k:4 * k + 4])
            outs[3 * k][...], outs[3 * k + 1][...], outs[3 * k + 2][...] = _adamw_math(w, g, m, v)

    flat = [a for item in items for a in item]
    res = pl.pallas_call(body, out_shape=[_sds(item[0].shape, F32) for item in items for _ in range(3)],
                         name=name)(*flat)
    return [tuple(res[3 * k:3 * k + 3]) for k in range(n)]


def _adamw(w, g, m, v, name, after):
    r, c = w.shape
    tm = next(cand for cand in (256, 176, 128, 64, 32, 16, 8) if r % cand == 0)
    return _rowcall(lambda wv, gv, mv, vv, _: (gv,) + _adamw_math(wv, gv, mv, vv),
                    [(w, c, 0), (g, c, 0), (m, c, 0), (v, c, 0)], [after], [(r, c, F32, c)] * 4, [], name=name, tm=tm)


def _ada_fwd(c_all, w_shard, b_shard, name):
    n = w_shard.shape[1]
    tn = 512

    def body(c_ref, w_ref, b_ref, o_ref, a_ref):
        cv = c_ref[...]
        act = cv * _sigmoid(cv)
        a_ref[...] = act
        o_ref[...] = jnp.dot(act.astype(BF16), w_ref[...].astype(BF16), preferred_element_type=F32) + b_ref[...]

    return pl.pallas_call(
        body, grid=(n // tn,),
        in_specs=[pl.BlockSpec((8, D), lambda j: (0, 0)), pl.BlockSpec((D, tn), lambda j: (0, j)),
                  pl.BlockSpec((1, tn), lambda j: (0, j))],
        out_specs=[pl.BlockSpec((8, tn), lambda j: (0, j)), pl.BlockSpec((8, D), lambda j: (0, 0))],
        out_shape=[_sds((8, n), F32), _sds((8, D), F32)],
        compiler_params=_params(("arbitrary",)), name=name,
    )(c_all, w_shard, b_shard)


def _ada_bwd_adamw(act_t, dmod_shard, w, m, v, name):
    r, c = w.shape
    tm = 256

    def body(a_ref, d_ref, w_ref, m_ref, v_ref, g_ref, dl_ref, nm_ref, nv_ref):
        g = jnp.dot(a_ref[...], d_ref[...], precision=lax.Precision.HIGHEST, preferred_element_type=F32)
        g_ref[...] = g
        dl_ref[...], nm_ref[...], nv_ref[...] = _adamw_math(w_ref[...], g, m_ref[...], v_ref[...])

    blk = pl.BlockSpec((tm, c), lambda i: (i, 0))
    return pl.pallas_call(
        body, grid=(r // tm,),
        in_specs=[pl.BlockSpec((tm, 8), lambda i: (i, 0)), pl.BlockSpec((8, c), lambda i: (0, 0)), blk, blk, blk],
        out_specs=[blk] * 4, out_shape=[_sds((r, c), F32)] * 4,
        compiler_params=_params(("arbitrary",)), name=name,
    )(act_t, dmod_shard, w, m, v)


def _place():
    return lax.axis_index("x"), lax.axis_index("y"), lax.axis_index("c")


def _flip(v, bit):
    return 1 - v if bit else v


VMEM_SPEC = pl.BlockSpec(memory_space=pltpu.VMEM)


def _allgather8(v, name):
    r, c = v.shape

    def body(v_ref, g_ref, tot_ref, send_sems, recv_sems, local_sem):
        x, y, cc = _place()
        sibling = (x, y, 1 - cc)
        chips = [(_flip(x, k & 2), _flip(y, k & 1)) for k in (1, 2, 3)]

        def block(px, py, pc):
            return g_ref.at[4 * px + 2 * py + pc]

        def copy(k, place, to, src=None):
            slot = block(*place)
            return pltpu.make_async_remote_copy(src_ref=slot if src is None else src, dst_ref=slot,
                                                send_sem=send_sems.at[k], recv_sem=recv_sems.at[k],
                                                device_id=to, device_id_type=MESH)

        mine = pltpu.make_async_copy(v_ref, block(x, y, cc), local_sem)
        mine.start()
        first = [copy(0, (x, y, cc), sibling, src=v_ref)]
        first += [copy(1 + j, (x, y, cc), (px, py, cc), src=v_ref) for j, (px, py) in enumerate(chips)]
        for cp in first:
            cp.start()
        passed = [copy(4 + j, (px, py, cc), sibling) for j, (px, py) in enumerate(chips)]
        for j, (px, py) in enumerate(chips):
            copy(1 + j, (px, py, cc), (x, y, cc)).wait_recv()
            passed[j].start()
        copy(0, sibling, (x, y, cc)).wait_recv()
        for j, (px, py) in enumerate(chips):
            copy(4 + j, (px, py, 1 - cc), (x, y, cc)).wait_recv()
        for cp in first + passed:
            cp.wait_send()
        mine.wait()
        tot = g_ref[0]
        for d in range(1, 8):
            tot = tot + g_ref[d]
        tot_ref[...] = tot

    return pl.pallas_call(
        body, in_specs=[VMEM_SPEC], out_specs=[VMEM_SPEC, VMEM_SPEC],
        out_shape=[_sds((8, r, c), F32), _sds((r, c), F32)],
        scratch_shapes=[pltpu.SemaphoreType.DMA((7,)), pltpu.SemaphoreType.DMA((7,)), pltpu.SemaphoreType.DMA],
        compiler_params=pltpu.CompilerParams(vmem_limit_bytes=VMEM_LIMIT), name=name,
    )(v)


def _slot(px, py, swapped):
    return 2 * py + px if swapped else 2 * px + py


def _gather_shards(arrs, swapped, name, in_place=False):
    n = len(arrs)

    def body(*refs):
        ins, outs = refs[:n], refs[n:2 * n]
        send1, recv1, send2, recv2, local_sems = refs[2 * n:]
        x, y, c = _place()
        sibling = (x, y, 1 - c)
        chips = [(_flip(x, k & 2), _flip(y, k & 1)) for k in (1, 2, 3)]
        local_copies, sends = [], []
        for a in range(n):
            h = outs[a].shape[1] // 2
            mine = pl.ds(pl.multiple_of(c * h, 8), h)
            own = _slot(x, y, swapped[a])
            if in_place:
                src = outs[a].at[own, mine]
            else:
                src = ins[a].at[mine]
                lc = pltpu.make_async_copy(ins[a], outs[a].at[own], local_sems.at[a])
                lc.start()
                local_copies.append(lc)
            for j, (px, py) in enumerate(chips):
                cp = pltpu.make_async_remote_copy(
                    src_ref=src, dst_ref=outs[a].at[own, mine], send_sem=send1.at[3 * a + j],
                    recv_sem=recv1.at[3 * a + j], device_id=(px, py, c), device_id_type=MESH)
                cp.start()
                sends.append(cp)
        for a in range(n):
            h = outs[a].shape[1] // 2
            mine = pl.ds(pl.multiple_of(c * h, 8), h)
            for j, (px, py) in enumerate(chips):
                piece = outs[a].at[_slot(px, py, swapped[a]), mine]
                pltpu.make_async_remote_copy(
                    src_ref=piece, dst_ref=piece, send_sem=send1.at[3 * a + j], recv_sem=recv1.at[3 * a + j],
                    device_id=(px, py, c), device_id_type=MESH).wait_recv()
                fwd = pltpu.make_async_remote_copy(
                    src_ref=piece, dst_ref=piece, send_sem=send2.at[3 * a + j], recv_sem=recv2.at[3 * a + j],
                    device_id=sibling, device_id_type=MESH)
                fwd.start()
                sends.append(fwd)
        for a in range(n):
            h = outs[a].shape[1] // 2
            other = pl.ds(pl.multiple_of((1 - c) * h, 8), h)
            for j, (px, py) in enumerate(chips):
                piece = outs[a].at[_slot(px, py, swapped[a]), other]
                pltpu.make_async_remote_copy(
                    src_ref=piece, dst_ref=piece, send_sem=send2.at[3 * a + j], recv_sem=recv2.at[3 * a + j],
                    device_id=sibling, device_id_type=MESH).wait_recv()
        for cp in sends:
            cp.wait_send()
        for lc in local_copies:
            lc.wait()

    dma = lambda k: pltpu.SemaphoreType.DMA((k,))
    return pl.pallas_call(
        body, in_specs=[ANY] * n, out_specs=[ANY] * n,
        out_shape=[_sds(a.shape if in_place else (4,) + a.shape, a.dtype) for a in arrs],
        scratch_shapes=[dma(3 * n), dma(3 * n), dma(3 * n), dma(3 * n), dma(n)],
        input_output_aliases={a: a for a in range(n)} if in_place else {},
        name=name,
    )(*arrs)


def _carry_pair_exchange(grads):
    n = len(grads)

    def copies(ins, outs, send_sems, recv_sems):
        x, y, c = _place()
        cps = []
        for a in range(n):
            h = ins[a].shape[1] // 2
            theirs = pl.ds(pl.multiple_of((1 - c) * h, 8), h)
            cps.append(pltpu.make_async_remote_copy(
                src_ref=ins[a].at[:, theirs, :], dst_ref=outs[a], send_sem=send_sems.at[a], recv_sem=recv_sems.at[a],
                device_id=(x, y, 1 - c), device_id_type=MESH))
        return cps

    def start(*refs):
        for cp in copies(*refs):
            cp.start()

    def finish(*refs):
        for cp in copies(*refs):
            cp.wait()

    return _Carried(grads, [_sds((4, g.shape[1] // 2, g.shape[2]), F32) for g in grads], {}, n, start, finish)


def _pair_sum(grad, recv, core, name):
    _, r, c = grad.shape
    h = r // 2

    def body(core_ref, g_ref, r_ref, o_ref):
        o_ref[...] = (g_ref[...] + r_ref[...]).astype(BF16)

    return pl.pallas_call(
        body,
        grid_spec=pltpu.PrefetchScalarGridSpec(
            num_scalar_prefetch=1, grid=(4,),
            in_specs=[pl.BlockSpec((None, h, c), lambda s, core_ref: (s, core_ref[0], 0)),
                      pl.BlockSpec((None, h, c), lambda s, core_ref: (s, 0, 0))],
            out_specs=pl.BlockSpec((None, h, c), lambda s, core_ref: (s, 0, 0))),
        out_shape=_sds((4, h, c), BF16), compiler_params=_params(("arbitrary",)), name=name,
    )(core, grad, recv)


def _carry_chip_exchange(parts, swapped):
    n = len(parts)

    def copies(ins, outs, send_sems, recv_sems):
        x, y, c = _place()
        chips = [(_flip(x, k & 2), _flip(y, k & 1)) for k in (1, 2, 3)]
        cps = []
        for a in range(n):
            for j, (px, py) in enumerate(chips):
                cps.append(pltpu.make_async_remote_copy(
                    src_ref=ins[a].at[_slot(px, py, swapped[a])], dst_ref=outs[a].at[j],
                    send_sem=send_sems.at[3 * a + j], recv_sem=recv_sems.at[3 * a + j],
                    device_id=(px, py, c), device_id_type=MESH))
        return cps

    def start(*refs):
        for cp in copies(*refs):
            cp.start()

    def finish(*refs):
        for cp in copies(*refs):
            cp.wait()

    return _Carried(parts, [_sds((3,) + p.shape[1:], BF16) for p in parts], {}, 3 * n, start, finish)


def _chip_sum(part, recv, slot_core, name):
    _, h, c = part.shape

    def body(sc_ref, p_ref, r_ref, o_ref):
        acc = p_ref[...].astype(F32)
        for j in range(3):
            acc = acc + r_ref[j].astype(F32)
        o_ref[...] = acc

    return pl.pallas_call(
        body,
        grid_spec=pltpu.PrefetchScalarGridSpec(
            num_scalar_prefetch=1, grid=(1,),
            in_specs=[pl.BlockSpec((None, h, c), lambda q, sc_ref: (sc_ref[0], 0, 0)),
                      pl.BlockSpec((3, h, c), lambda q, sc_ref: (0, 0, 0))],
            out_specs=pl.BlockSpec((h, c), lambda q, sc_ref: (sc_ref[1], 0))),
        out_shape=_sds((2 * h, c), F32), compiler_params=_params(("arbitrary",)), name=name,
    )(slot_core, part, recv)


def _carry_pair_share(shards):
    n = len(shards)

    def copies(outs, send_sems, recv_sems, mine):
        x, y, c = _place()
        cps = []
        for a in range(n):
            h = outs[a].shape[0] // 2
            half = outs[a].at[pl.ds(pl.multiple_of((c if mine else 1 - c) * h, 8), h)]
            cps.append(pltpu.make_async_remote_copy(
                src_ref=half, dst_ref=half, send_sem=send_sems.at[a], recv_sem=recv_sems.at[a],
                device_id=(x, y, 1 - c), device_id_type=MESH))
        return cps

    def start(ins, outs, send_sems, recv_sems):
        for cp in copies(outs, send_sems, recv_sems, True):
            cp.start()

    def finish(ins, outs, send_sems, recv_sems):
        for cp in copies(outs, send_sems, recv_sems, False):
            cp.wait_recv()
        for cp in copies(outs, send_sems, recv_sems, True):
            cp.wait_send()

    return _Carried(shards, [_sds(s.shape, F32) for s in shards], {a: a for a in range(n)}, n, start, finish)


def _carry_gather_ici(bufs, swapped):
    n = len(bufs)

    def copies(outs, send_sems, recv_sems, sending):
        x, y, c = _place()
        cps = []
        for a in range(n):
            h = outs[a].shape[1] // 2
            mine = pl.ds(pl.multiple_of(c * h, 8), h)
            for j, k in enumerate((1, 2, 3)):
                px, py = _flip(x, k & 2), _flip(y, k & 1)
                slot = _slot(x, y, swapped[a]) if sending else _slot(px, py, swapped[a])
                piece = outs[a].at[slot, mine]
                cps.append(pltpu.make_async_remote_copy(
                    src_ref=piece, dst_ref=piece, send_sem=send_sems.at[3 * a + j], recv_sem=recv_sems.at[3 * a + j],
                    device_id=(px, py, c), device_id_type=MESH))
        return cps

    def start(ins, outs, send_sems, recv_sems):
        for cp in copies(outs, send_sems, recv_sems, True):
            cp.start()

    def finish(ins, outs, send_sems, recv_sems):
        for cp in copies(outs, send_sems, recv_sems, False):
            cp.wait_recv()
        for cp in copies(outs, send_sems, recv_sems, True):
            cp.wait_send()

    return _Carried(bufs, [_sds(b.shape, b.dtype) for b in bufs], {a: a for a in range(n)}, 3 * n, start, finish)


HBM_SPEC = pl.BlockSpec(memory_space=pltpu.HBM)
SEM_SPEC = pl.BlockSpec(memory_space=pltpu.SEMAPHORE)
SIDE_EFFECT = pltpu.SideEffectType.DATAFLOW_SIDE_EFFECTING


def _ici_pieces(buf, send_sems, recv_sems, swapped, sending):
    x, y, c = _place()
    h = buf.shape[1] // 2
    mine = pl.ds(pl.multiple_of(c * h, 8), h)
    cps = []
    for j, k in enumerate((1, 2, 3)):
        px, py = _flip(x, k & 2), _flip(y, k & 1)
        piece = buf.at[_slot(x, y, swapped) if sending else _slot(px, py, swapped), mine]
        cps.append(pltpu.make_async_remote_copy(src_ref=piece, dst_ref=piece, send_sem=send_sems.at[j],
                                                recv_sem=recv_sems.at[j], device_id=(px, py, c), device_id_type=MESH))
    return cps


def _gather_ici_start(buf, after, swapped, name):
    def body(buf_ref, after_ref, send_sems, recv_sems, thru, token):
        for cp in _ici_pieces(thru, send_sems, recv_sems, swapped, True):
            cp.start()
        token[...] = jnp.zeros_like(token)

    return pl.pallas_call(
        body, name=name,
        out_shape=(pltpu.SemaphoreType.DMA((3,)), pltpu.SemaphoreType.DMA((3,)), pltpu.HBM(buf.shape, buf.dtype),
                   jax.ShapeDtypeStruct((8, 128), F32)),
        in_specs=(HBM_SPEC, ANY), out_specs=(SEM_SPEC, SEM_SPEC, HBM_SPEC, VMEM_SPEC), input_output_aliases={0: 2},
        compiler_params=pltpu.CompilerParams(has_side_effects=SIDE_EFFECT),
    )(pltpu.with_memory_space_constraint(buf, pltpu.HBM), after)


def _gather_ici_wait(send_sems, recv_sems, thru, after, swapped, name):
    def body(thru_ref, send_sems, recv_sems, after_ref, out_ref):
        for cp in _ici_pieces(out_ref, send_sems, recv_sems, swapped, True):
            cp.wait_send()
        for cp in _ici_pieces(out_ref, send_sems, recv_sems, swapped, False):
            cp.wait_recv()

    return pl.pallas_call(
        body, name=name, out_shape=pltpu.HBM(thru.shape, thru.dtype),
        in_specs=(HBM_SPEC, SEM_SPEC, SEM_SPEC, ANY), out_specs=HBM_SPEC, input_output_aliases={0: 0},
        compiler_params=pltpu.CompilerParams(has_side_effects=SIDE_EFFECT),
    )(thru, send_sems, recv_sems, after)


def _all8_copies(buf, send_sems, recv_sems, sending):
    x, y, c = _place()
    cps = []
    for k in range(1, 8):
        px, py, pc = _flip(x, k & 4), _flip(y, k & 2), _flip(c, k & 1)
        slot = buf.at[4 * x + 2 * y + c] if sending else buf.at[4 * px + 2 * py + pc]
        cps.append(pltpu.make_async_remote_copy(src_ref=slot, dst_ref=slot, send_sem=send_sems.at[k - 1],
                                                recv_sem=recv_sems.at[k - 1], device_id=(px, py, pc), device_id_type=MESH))
    return cps


def _all8_start(buf, name):
    def body(buf_ref, send_sems, recv_sems, thru, token):
        for cp in _all8_copies(thru, send_sems, recv_sems, True):
            cp.start()
        token[...] = jnp.zeros_like(token)

    return pl.pallas_call(
        body, name=name,
        out_shape=(pltpu.SemaphoreType.DMA((7,)), pltpu.SemaphoreType.DMA((7,)), pltpu.HBM(buf.shape, buf.dtype),
                   jax.ShapeDtypeStruct((8, 128), F32)),
        in_specs=(HBM_SPEC,), out_specs=(SEM_SPEC, SEM_SPEC, HBM_SPEC, VMEM_SPEC), input_output_aliases={0: 2},
        compiler_params=pltpu.CompilerParams(has_side_effects=SIDE_EFFECT),
    )(pltpu.with_memory_space_constraint(buf, pltpu.HBM))


def _all8_wait(send_sems, recv_sems, thru, after, name):
    def body(thru_ref, send_sems, recv_sems, after_ref, out_ref):
        for cp in _all8_copies(out_ref, send_sems, recv_sems, True):
            cp.wait_send()
        for cp in _all8_copies(out_ref, send_sems, recv_sems, False):
            cp.wait_recv()

    return pl.pallas_call(
        body, name=name, out_shape=pltpu.HBM(thru.shape, thru.dtype),
        in_specs=(HBM_SPEC, SEM_SPEC, SEM_SPEC, ANY), out_specs=HBM_SPEC, input_output_aliases={0: 0},
        compiler_params=pltpu.CompilerParams(has_side_effects=SIDE_EFFECT),
    )(thru, send_sems, recv_sems, after)


def _sum8(g, name):
    def body(g_ref, o_ref):
        tot = g_ref[0]
        for d in range(1, 8):
            tot = tot + g_ref[d]
        o_ref[...] = tot

    return pl.pallas_call(body, out_shape=_sds(g.shape[1:], F32), name=name)(g)


def _carry_gather_forward(bufs, swapped):
    n = len(bufs)

    def copies(outs, send_sems, recv_sems, sending):
        x, y, c = _place()
        cps = []
        for a in range(n):
            h = outs[a].shape[1] // 2
            rows = pl.ds(pl.multiple_of((c if sending else 1 - c) * h, 8), h)
            for j, k in enumerate((1, 2, 3)):
                piece = outs[a].at[_slot(_flip(x, k & 2), _flip(y, k & 1), swapped[a]), rows]
                cps.append(pltpu.make_async_remote_copy(
                    src_ref=piece, dst_ref=piece, send_sem=send_sems.at[3 * a + j], recv_sem=recv_sems.at[3 * a + j],
                    device_id=(x, y, 1 - c), device_id_type=MESH))
        return cps

    def start(ins, outs, send_sems, recv_sems):
        for cp in copies(outs, send_sems, recv_sems, True):
            cp.start()

    def finish(ins, outs, send_sems, recv_sems):
        for cp in copies(outs, send_sems, recv_sems, False):
            cp.wait_recv()
        for cp in copies(outs, send_sems, recv_sems, True):
            cp.wait_send()

    return _Carried(bufs, [_sds(b.shape, b.dtype) for b in bufs], {a: a for a in range(n)}, 3 * n, start, finish)


def _pack(arrs, rows_multiple=8):
    parts, offs, row = [], [], 0
    for a in arrs:
        flat = a.reshape(-1)
        nrow = -(-flat.shape[0] // D)
        parts.append(jnp.pad(flat, (0, nrow * D - flat.shape[0])))
        offs.append(row)
        row += nrow
    total = -(-row // rows_multiple) * rows_multiple
    if total > row:
        parts.append(jnp.zeros(((total - row) * D,), F32))
    return jnp.concatenate(parts).reshape(total, D), offs


def _unpack(packed, offs, shapes):
    out = []
    for off, shp in zip(offs, shapes):
        size = int(np.prod(shp))
        nrow = -(-size // D)
        out.append(packed[off:off + nrow].reshape(-1)[:size].reshape(shp))
    return out


def _to_bf16_slot(w, slot, name, after=None):
    r, c = w.shape
    tm = next(cand for cand in (256, 176, 128, 64, 32, 16) if r % cand == 0)

    def body(slot_ref, w_ref, *rest):
        rest[-1][...] = w_ref[...].astype(BF16)

    in_specs = [pl.BlockSpec((tm, c), lambda i, slot_ref: (i, 0))]
    if after is not None:
        in_specs.append(pl.BlockSpec((8, 128), lambda i, slot_ref: (0, 0)))
    return pl.pallas_call(
        body,
        grid_spec=pltpu.PrefetchScalarGridSpec(
            num_scalar_prefetch=1, grid=(r // tm,), in_specs=in_specs,
            out_specs=pl.BlockSpec((None, tm, c), lambda i, slot_ref: (slot_ref[0], i, 0))),
        out_shape=_sds((4, r, c), BF16), compiler_params=_params(("arbitrary",)), name=name,
    )(slot, w, *([] if after is None else [after]))


def _unshard_cols(g):
    s, k, n = g.shape
    return jnp.transpose(g, (1, 0, 2)).reshape(k, s * n)


def _ff_swap(v):
    b = FF_BLOCK
    return jnp.concatenate([v[..., 0:b], v[..., 2 * b:3 * b], v[..., b:2 * b], v[..., 3 * b:4 * b]], axis=-1)


LATE = ("attn_o", "conv_o", "mix_o", "up", "down")
EARLY_GRADS = ("down", "up", "mix_o", "attn_o", "conv_o")


def _weight_views(bufs):
    return {"up": bufs["up"], "attn_o": _unshard_cols(bufs["attn_o"]), "conv_o": _unshard_cols(bufs["conv_o"]),
            "mix_o": bufs["mix_o"].reshape(D, D), "down": bufs["down"].reshape(D_FF, D)}


def _pair_sums(names, grads, recv, dist):
    return [_pair_sum(g, r, dist["core"], "pair_sum_" + n) for n, g, r in zip(names, grads, recv)]


def _reduce_halves(names, parts, from_chips, dist):
    return [_chip_sum(p, r, jnp.concatenate([dist["slots"][SWAPPED[n]], dist["core"]]), "chip_sum_" + n)
            for n, p, r in zip(names, parts, from_chips)]


FUSED_TILE = 256
WIDE_TILE = 512


def _gates(z):
    return [(z, 512, 5), (z, 512, 6), (z, 512, 7), (z, 512, 8)]


def _mix_out(a, cb, z, x, w_mix_o, g_post, gt, g_pre2, sc2, sh2, name):
    def lhs(av, cv, ga0, ga1, gb0, gb1):
        ga, gb = jnp.concatenate([ga0, ga1], axis=1), jnp.concatenate([gb0, gb1], axis=1)
        return _sigmoid(ga) * av + _sigmoid(gb) * cv

    def fn(ym, y, xv, gv, gtv, g2v, scv, shv):
        yn, _ = _rms_stats(ym)
        x1 = xv + gtv * (yn * gv)
        xn, _ = _rms_stats(x1)
        return ym, y, x1, (xn * g2v) * (1.0 + scv) + shv

    return _matmul_rows(w_mix_o, form="nn", tm=min(WIDE_TILE, x.shape[0]), tk=D, fn=fn, a_rows=[(a, D, 0), (cb, D, 0)] + _gates(z),
                        a_fn=lhs, rows=[(x, D, 0)], consts=[g_post, gt, g_pre2, sc2, sh2],
                        row_outs=[(F32, D), (BF16, D), (F32, D), (BF16, D)], acc_outs=[], name=name)


def _down_tail(act, w_down, x1, target, g, gt, name):
    def fn(yv, xv, tv, gv, gtv):
        yn, r = _rms_stats(yv)
        e = xv + gtv * (yn * gv) - tv
        dx2 = e * (1.0 / D)
        dyn = dx2 * gtv
        return (dx2, _rms_bwd(dyn * gv, yn, r), _colsum(e * e) * (0.5 / D), _colsum(dyn * yn),
                _colsum(dx2 * (yn * gv)))

    return _matmul_rows(w_down, form="nn", a=act, tm=min(WIDE_TILE, x1.shape[0]), tk=D_FF, fn=fn,
                        rows=[(x1, D, 0), (target, D, 0)], consts=[g, gt], row_outs=[(F32, D), (BF16, D)],
                        acc_outs=[(1, D)] * 3, name=name)


def _up_dx_tail(dup, w_up, x1, dx2, ym, g_pre2, sc2, g_post, gt, name):
    def fn(dh, xv, dov, ymv, g2v, scv, gv, gtv):
        xn, r = _rms_stats(xv)
        dyn = dh * (1.0 + scv)
        dx1 = dov + _rms_bwd(dyn * g2v, xn, r)
        yn, r2 = _rms_stats(ymv)
        dynm = dx1 * gtv
        return (dx1, _rms_bwd(dynm * gv, yn, r2), _colsum(dh), _colsum(dh * (xn * g2v)), _colsum(dyn * xn),
                _colsum(dynm * yn), _colsum(dx1 * (yn * gv)))

    return _matmul_rows(w_up, form="nt", a=dup, tm=min(FUSED_TILE, x1.shape[0]), tk=2 * D_FF, fn=fn,
                        rows=[(x1, D, 0), (dx2, D, 0), (ym, D, 0)], consts=[g_pre2, sc2, g_post, gt],
                        row_outs=[(F32, D), (BF16, D)], acc_outs=[(1, D)] * 5, name=name)


def _mix_dx_gates(dym, w_mix_o, a, cb, z, name):
    def fn(dy, av, cv, ga0, ga1, gb0, gb1):
        sa = _sigmoid(jnp.concatenate([ga0, ga1], axis=1))
        sb = _sigmoid(jnp.concatenate([gb0, gb1], axis=1))
        dcb = dy * sb
        dga = dy * av * (sa * (1.0 - sa))
        dgb = dy * cv * (sb * (1.0 - sb))
        return dy * sa, dcb, dga, dgb, _colsum(dcb), _colsum(dga), _colsum(dgb)

    return _matmul_rows(w_mix_o, form="nt", a=dym, tm=min(WIDE_TILE, a.shape[0]), tk=D, fn=fn,
                        rows=[(a, D, 0), (cb, D, 0)] + _gates(z), consts=[], row_outs=[(BF16, D)] * 4,
                        acc_outs=[(1, D)] * 3, name=name)


def _local_step(x, target, mod, w_in, late, small, dist=None):
    sh_m, sc_m, gt_m, sh_f, sc_f, gt_f = mod
    t = x.shape[0]
    tmm = min(1024, t)
    late_swapped = [SWAPPED[n] for n in LATE]

    h1 = _pre_norm(x, small["g_pre_mix"], sc_m, sh_m, "pre_norm_mix")
    if callable(w_in):
        w_in = w_in(h1)
    z = _matmul(h1, w_in, form="nn", out_dtype=F32, tm=min(FUSED_TILE, t), tn=D_IN, tk=D, bias=small["b_in"], name="mm_in")
    conv = (z, small["w_dw_conv"], small["b_dw_conv"], small["g_conv_ln"], small["b_conv_ln"], "conv_branch")
    if dist is None:
        att = _attention(z, small["gen"], "attention")
        u1, u3 = _conv_branch(*conv)
        bufs = dict(late)
    else:
        mid = [n for n in LATE if n != "down"]
        mid_swapped = [SWAPPED[n] for n in mid]
        att, landed = _attention(z, small["gen"], "attention",
                                 carried=_carry_gather_ici([late[n] for n in mid], mid_swapped))
        (u1, u3), gathered = _conv_branch(*conv, carried=_carry_gather_forward(landed, mid_swapped))
        bufs = dict(zip(mid, gathered))
        bufs["down"] = late["down"]
    w = _weight_views(bufs)
    w["in"] = w_in
    a = _matmul(att, w["attn_o"], form="nn", out_dtype=F32, tm=tmm, tn=512, tk=512, name="mm_attn_o")
    cb = _matmul(u3, w["conv_o"], form="nn", out_dtype=F32, tm=tmm, tn=512, tk=512, bias=small["b_conv_o"], name="mm_conv_o")
    ym, y, x1, h2 = _mix_out(a, cb, z, x, w["mix_o"], small["g_post_mix"], gt_m, small["g_pre_ffn"], sc_f, sh_f, "mix_out")
    mm_up = dict(form="nn", out_dtype=F32, tm=min(FUSED_TILE, t), tn=2 * D_FF, tk=D, name="mm_up")
    ffn_act = (small["w_dw_ffn"], small["b_dw_ffn"], "ffn_act")
    if dist is None:
        up = _matmul(h2, w["up"], **mm_up)
        act = _ffn_act(up, *ffn_act)
    else:
        up, landed = _matmul(h2, w["up"], carried=_carry_gather_ici([late["down"]], [False]), **mm_up)
        act, down = _ffn_act(up, *ffn_act, carried=_carry_gather_forward(landed, [False]))
        w["down"] = down[0].reshape(D_FF, D)

    dx2, dyf, loss_cols, d_g_post_ffn, d_gt_f = _down_tail(act, w["down"], x1, target, small["g_post_ffn"], gt_f, "down_tail")
    dact = _matmul(dyf, w["down"], form="nt", out_dtype=F32, tm=tmm, tn=FF_BLOCK, tk=D, name="mm_down_dx")
    g_down = _matmul(act, dyf, form="tn", out_dtype=F32, tm=FF_BLOCK, tn=512, tk=t, name="mm_down_dw")
    dup, d_w_dw_ffn, d_b_dw_ffn = _ffn_act_bwd(dact, up, small["w_dw_ffn"], small["b_dw_ffn"], "ffn_act_bwd")
    dx1, dym, d_sh_f, d_sc_f, d_g_pre_ffn, d_g_post_mix, d_gt_m = _up_dx_tail(
        dup, w["up"], x1, dx2, ym, small["g_pre_ffn"], sc_f, small["g_post_mix"], gt_m, "up_dx_tail")
    g_up = _matmul(h2, dup, form="tn", out_dtype=F32, tm=512, tn=FF_BLOCK, tk=t, out_sharded=True, name="mm_up_dw")
    da, dcb, dgate_a, dgate_b, d_b_conv_o, sga, sgb = _mix_dx_gates(dym, w["mix_o"], a, cb, z, "mix_dx_gates")
    g_mix_o = _matmul(y, dym, form="tn", out_dtype=F32, tm=D, tn=512, tk=t, name="mm_mix_o_dw")
    datt = _matmul(da, w["attn_o"], form="nt", out_dtype=F32, tm=tmm, tn=512, tk=D, name="mm_attn_o_dx")
    g_attn_o = _matmul(att, da, form="tn", out_dtype=F32, tm=512, tn=256, tk=t, out_sharded=True, name="mm_attn_o_dw")
    du3 = _matmul(dcb, w["conv_o"], form="nt", out_dtype=F32, tm=tmm, tn=512, tk=D, name="mm_conv_o_dx")
    g_conv_o = _matmul(u3, dcb, form="tn", out_dtype=F32, tm=512, tn=256, tk=t, out_sharded=True, name="mm_conv_o_dw")
    big = {"attn_o": g_attn_o, "conv_o": g_conv_o, "mix_o": g_mix_o.reshape(4, 256, D),
           "up": g_up, "down": g_down.reshape(4, D_FF // 4, D)}
    conv_bwd = (du3, u1, z, small["w_dw_conv"], small["g_conv_ln"], small["b_conv_ln"], "conv_branch_bwd")
    in_dw = dict(form="tn", out_dtype=F32, tm=512, tn=1152, tk=t, out_sharded=True, name="mm_in_dw")
    in_dx = dict(form="nt", out_dtype=F32, tm=min(WIDE_TILE, t), tn=D, tk=D_IN, name="mm_in_dx")
    if dist is None:
        dglu, d_w_dw_conv, d_b_dw_conv, d_g_conv_ln, d_b_conv_ln, sglu = _conv_branch_bwd(*conv_bwd)
        dq, dk, dv, sq, sk, sv, dgen = _attention_bwd(z, datt, small["gen"], "attention_bwd")
        dz = jnp.concatenate([dq, dk, dv, dglu, dgate_a, dgate_b], axis=1)
        big["in"] = _matmul(h1, dz, **in_dw)
        dh1 = _matmul(dz, w_in, **in_dx)
    else:
        early = [big[n] for n in EARLY_GRADS]
        (dglu, d_w_dw_conv, d_b_dw_conv, d_g_conv_ln, d_b_conv_ln, sglu), recv = _conv_branch_bwd(
            *conv_bwd, carried=_carry_pair_exchange(early))
        parts = _pair_sums(EARLY_GRADS, early, recv, dist)
        (dq, dk, dv, sq, sk, sv, dgen), from_chips = _attention_bwd(
            z, datt, small["gen"], "attention_bwd",
            carried=_carry_chip_exchange(parts, [SWAPPED[n] for n in EARLY_GRADS]))
        halves = _reduce_halves(EARLY_GRADS, parts, from_chips, dist)
        dist["small_early"](
            {"g_post_mix": d_g_post_mix, "b_in": jnp.concatenate([sq, sk, sv, sglu, sga, sgb], axis=1), "gen": dgen,
             "w_dw_conv": d_w_dw_conv, "b_dw_conv": d_b_dw_conv, "g_conv_ln": d_g_conv_ln, "b_conv_ln": d_b_conv_ln,
             "b_conv_o": d_b_conv_o, "g_pre_ffn": d_g_pre_ffn, "g_post_ffn": d_g_post_ffn,
             "w_dw_ffn": d_w_dw_ffn, "b_dw_ffn": d_b_dw_ffn},
            [d_gt_m, d_sh_f, d_sc_f, d_gt_f], loss_cols)
        dz = jnp.concatenate([dq, dk, dv, dglu, dgate_a, dgate_b], axis=1)
        g_in, shards = _matmul(h1, dz, carried=_carry_pair_share(halves), **in_dw)
        big = dict(zip(EARLY_GRADS, shards))
        exchange = _carry_pair_exchange([g_in])
        handles, token = _start_carried(exchange, "pair_exchange_in_start")
        busy = dist["adamw"]("up", big["up"], token)
        recv_in = _wait_carried(exchange, handles, busy, "pair_exchange_in_wait")
        part_in = _pair_sums(("in",), [g_in], recv_in, dist)
        dh1, from_chips_in = _matmul(dz, w_in, carried=_carry_chip_exchange(part_in, [False]), **in_dx)
        half_in = _reduce_halves(("in",), part_in, from_chips_in, dist)
        share = _carry_pair_share(half_in)
        handles, token = _start_carried(share, "pair_share_in_start")
        busy = dist["adamw"]("down", big["down"], token)
        big["in"] = _wait_carried(share, handles, busy, "pair_share_in_wait")[0]
    d_b_in = jnp.concatenate([sq, sk, sv, sglu, sga, sgb], axis=1)
    grad_x, d_sh_m, d_sc_m, d_g_pre_mix = _pre_norm_bwd(dh1, x, dx1, small["g_pre_mix"], sc_m, "pre_norm_mix_bwd")

    dmod = [d_sh_m, d_sc_m, d_gt_m, d_sh_f, d_sc_f, d_gt_f]
    sm = {"g_pre_mix": d_g_pre_mix, "g_post_mix": d_g_post_mix, "b_in": d_b_in, "gen": dgen,
          "w_dw_conv": d_w_dw_conv, "b_dw_conv": d_b_dw_conv, "g_conv_ln": d_g_conv_ln, "b_conv_ln": d_b_conv_ln,
          "b_conv_o": d_b_conv_o, "g_pre_ffn": d_g_pre_ffn, "g_post_ffn": d_g_post_ffn,
          "w_dw_ffn": d_w_dw_ffn, "b_dw_ffn": d_b_dw_ffn}
    return loss_cols, grad_x, dmod, big, sm


BIG = ("in", "attn_o", "conv_o", "mix_o", "up", "down")
SWAPPED = {"in": False, "attn_o": False, "conv_o": False, "mix_o": False, "up": True, "down": False}
SMALL_ORDER = ("b_ada", "g_pre_mix", "g_post_mix", "b_in", "rel_bias", "b_dw_conv", "g_conv_ln", "b_conv_ln",
               "b_conv_o", "g_pre_ffn", "g_post_ffn", "b_dw_ffn", "w_dw_conv", "w_dw_ffn")


def kernel(x, c, w_ada, b_ada, g_pre_mix, g_post_mix, w_in, b_in, rel_bias, w_attn_o, w_dw_conv, b_dw_conv, g_conv_ln, b_conv_ln, w_conv_o, b_conv_o, w_mix_o, g_pre_ffn, g_post_ffn, w_up, w_dw_ffn, b_dw_ffn, w_down, loss_target, m_w_ada, m_b_ada, m_g_pre_mix, m_g_post_mix, m_w_in, m_b_in, m_rel_bias, m_w_attn_o, m_w_dw_conv, m_b_dw_conv, m_g_conv_ln, m_b_conv_ln, m_w_conv_o, m_b_conv_o, m_w_mix_o, m_g_pre_ffn, m_g_post_ffn, m_w_up, m_w_dw_ffn, m_b_dw_ffn, m_w_down, v_w_ada, v_b_ada, v_g_pre_mix, v_g_post_mix, v_w_in, v_b_in, v_rel_bias, v_w_attn_o, v_w_dw_conv, v_b_dw_conv, v_g_conv_ln, v_b_conv_ln, v_w_conv_o, v_b_conv_o, v_w_mix_o, v_g_pre_ffn, v_g_post_ffn, v_w_up, v_w_dw_ffn, v_b_dw_ffn, v_w_down):
    given = dict(locals())
    ax, ay, ac = lax.axis_index("x"), lax.axis_index("y"), lax.axis_index("c")
    shard = 2 * ax + ay
    me = 4 * ax + 2 * ay + ac
    xs, target = x[0], loss_target[0]

    slots = {sw: _slot(ax, ay, sw).astype(jnp.int32).reshape(1) for sw in (False, True)}
    own = {"in": _to_bf16_slot(w_in[0], slots[False], "cast_in")}

    c_pad = jnp.pad(c, ((0, 7), (0, 0)))
    c_g, _ = _allgather8(c_pad, "gather_c")
    c_all = c_g[:, 0, :]
    b_ada_shard = lax.dynamic_slice(b_ada, (0, shard * 1536), (1, 1536))
    mod_shard, c_act = _ada_fwd(c_all, w_ada[0], b_ada_shard, "ada_fwd")
    small_in = [jnp.pad(mod_shard, ((0, 8), (0, 0))),
                jnp.pad(w_dw_conv[0], ((0, 1), (0, 0))),
                jnp.pad(w_dw_ffn[0], ((0, 13), (0, 0)))]
    mod_g, wdc_g, wdf_g = _gather_shards(small_in, [False, False, True], "gather_small")
    mod_all = jnp.transpose(mod_g[:, :8, :], (1, 0, 2)).reshape(8, 6 * D)
    in_send, in_recv, in_flight, token = _gather_ici_start(own["in"], mod_g, False, "gather_w_in_start")

    def w_in_ready(after):
        landed = _gather_ici_wait(in_send, in_recv, in_flight, after, False, "gather_w_in_wait")
        return _run_carried(_carry_gather_forward([landed], [False]), "gather_forward_in")[0]

    for n in LATE:
        own[n] = _to_bf16_slot(given["w_" + n][0], slots[SWAPPED[n]], "cast_" + n, after=token)
    mod_row = lax.dynamic_slice(mod_all, (me, 0), (1, 6 * D)) + token[0:1, 0:1]
    mod = [mod_row[:, k * D:(k + 1) * D] for k in range(6)]

    core = ac.astype(jnp.int32).reshape(1)
    out = {}

    def adamw_big(n, g, token):
        g, dl, nm, nv = _adamw(given["w_" + n][0], g, given["m_w_" + n][0], given["v_w_" + n][0], "adamw_" + n, token)
        out["grad_w_" + n], out["delta_w_" + n], out["new_m_w_" + n], out["new_v_w_" + n] = g[None], dl[None], nm[None], nv[None]
        return dl

    early = {}
    early_names = [n for n in SMALL_ORDER if n not in ("b_ada", "g_pre_mix")]

    def small_early(sm_e, dmod_e, loss_cols):
        d_rel = _select_call(sm_e["gen"], sel.astype(BF16), "bias_fold")[:, :2 * MAX_REL + 1]
        grads = dict(sm_e, rel_bias=d_rel[None], b_dw_ffn=_ff_swap(sm_e["b_dw_ffn"]), w_dw_ffn=_ff_swap(sm_e["w_dw_ffn"]))
        packed, offs = _pack([jnp.concatenate(dmod_e, axis=1)] + [grads[n] for n in early_names] + [loss_cols])
        mine = lax.dynamic_update_slice(jnp.zeros((8,) + packed.shape, F32), packed[None], (me, 0, 0))
        send, recv, flight, _ = _all8_start(mine, "gather_small_grads_start")
        early.update(handles=(send, recv, flight), offs=offs)

    dist = {"core": core, "slots": slots, "adamw": adamw_big, "small_early": small_early}

    sel = jnp.asarray(_toeplitz_map())
    rel_pad = jnp.pad(rel_bias[0], ((0, 0), (0, REL_PAD - (2 * MAX_REL + 1))))
    gen = _select_call(rel_pad, sel.T.astype(BF16), "bias_rows")
    small = {"g_pre_mix": g_pre_mix, "g_post_mix": g_post_mix, "b_in": b_in, "gen": gen,
             "w_dw_conv": _unshard_cols(wdc_g[:, :CONV_K, :]), "b_dw_conv": b_dw_conv, "g_conv_ln": g_conv_ln,
             "b_conv_ln": b_conv_ln, "b_conv_o": b_conv_o, "g_pre_ffn": g_pre_ffn, "g_post_ffn": g_post_ffn,
             "w_dw_ffn": _unshard_cols(wdf_g[:, :FFN_K, :]), "b_dw_ffn": _ff_swap(b_dw_ffn)}

    loss_cols, grad_x, dmod, reduced, sm = _local_step(xs, target, mod, w_in_ready, {n: own[n] for n in LATE}, small, dist)

    late, _ = _pack([jnp.concatenate(dmod[0:2], axis=1), sm["g_pre_mix"]])
    mine = lax.dynamic_update_slice(jnp.zeros((8,) + late.shape, F32), late[None], (me, 0, 0))
    lg_send, lg_recv, lg_flight, lg_token = _all8_start(mine, "gather_late_grads_start")
    for n in BIG:
        if "grad_w_" + n not in out:
            busy = adamw_big(n, reduced[n], lg_token)
    every_late = _all8_wait(lg_send, lg_recv, lg_flight, busy, "gather_late_grads_wait")
    every = _all8_wait(*early["handles"], busy, "gather_small_grads_wait")
    total, total_late = _sum8(every, "sum_small_grads"), _sum8(every_late, "sum_late_grads")
    offs = early["offs"]
    loss = jnp.sum(total[offs[-1]])
    dmod_all = jnp.concatenate([every_late[:, 0:2, :], every[:, 0:4, :]], axis=1).reshape(8, 6 * D)
    full_shapes = {n: given[n].shape for n in early_names}
    full_shapes["w_dw_conv"], full_shapes["w_dw_ffn"] = (1, CONV_K, 512), (1, FFN_K, 2 * D_FF)
    sums = dict(zip(early_names, _unpack(total, offs[1:-1], [full_shapes[n] for n in early_names])))
    sums["g_pre_mix"] = total_late[2:3].reshape(1, D)
    sums["b_ada"] = jnp.concatenate([total_late[0:2], total[0:4]], axis=0).reshape(1, 6 * D)
    sums["w_dw_conv"] = lax.dynamic_slice(sums["w_dw_conv"], (0, 0, shard * 128), (1, CONV_K, 128))
    sums["w_dw_ffn"] = lax.dynamic_slice(sums["w_dw_ffn"], (0, 0, shard * FF_BLOCK), (1, FFN_K, FF_BLOCK))

    upd = dict(zip(SMALL_ORDER, _adamw_many(
        [(given[n], sums[n], given["m_" + n], given["v_" + n]) for n in SMALL_ORDER], "adamw_small")))

    dmod_shard = lax.dynamic_slice(dmod_all, (0, shard * 1536), (8, 1536))
    ada = _ada_bwd_adamw(c_act.T, dmod_shard, w_ada[0], m_w_ada[0], v_w_ada[0], "ada_bwd_adamw")

    out.update({"grad_w_ada": ada[0][None], "delta_w_ada": ada[1][None], "new_m_w_ada": ada[2][None],
                "new_v_w_ada": ada[3][None]})
    for n in SMALL_ORDER:
        out["grad_" + n], out["delta_" + n], out["new_m_" + n], out["new_v_" + n] = sums[n], *upd[n]

    weights = ["w_ada", "b_ada", "g_pre_mix", "g_post_mix", "w_in", "b_in", "rel_bias", "w_attn_o", "w_dw_conv", "b_dw_conv",
               "g_conv_ln", "b_conv_ln", "w_conv_o", "b_conv_o", "w_mix_o", "g_pre_ffn", "g_post_ffn", "w_up", "w_dw_ffn",
               "b_dw_ffn", "w_down"]
    return (loss, grad_x[None], *[out["grad_" + n] for n in weights], *[out["delta_" + n] for n in weights],
            *[out["new_m_" + n] for n in weights], *[out["new_v_" + n] for n in weights])
```

```python
import functools
import math

import numpy as np
import jax
import jax.numpy as jnp
from jax import lax
from jax.experimental import pallas as pl
from jax.experimental.pallas import tpu as pltpu

F32, BF16 = jnp.float32, jnp.bfloat16
MESH = pl.DeviceIdType.MESH

D = 1024
D_IN = 4608
D_FF = 2816
N_CHIPS = 4
IN_SHARD = D_IN // N_CHIPS
ADA_SHARD = 6 * D // N_CHIPS
CONV_K = 31
FFN_K = 3
N_HEADS = 8
CHUNK = 64
LEFT_CHUNKS = 8
MAX_REL = 128
EPS = 1e-6
NEG_INF = -1e30
Q_TILE = 256
WINDOW = Q_TILE + LEFT_CHUNKS * CHUNK
STEP_ROWS = 256
REL_PAD = 384
TOEP = 1024
ROW_TILE = 256
VMEM_LIMIT = 60 * 1024 * 1024

ADAM_LR, ADAM_B1, ADAM_B2, ADAM_EPS, ADAM_WD, ADAM_STEP = 0.001, 0.9, 0.999, 1e-08, 0.01, 10


def _params(sem=None):
    return pltpu.CompilerParams(dimension_semantics=sem, vmem_limit_bytes=VMEM_LIMIT)


def _sds(shape, dtype):
    return jax.ShapeDtypeStruct(tuple(shape), dtype)


ANY = pl.BlockSpec(memory_space=pl.ANY)


class _Carried:
    def __init__(self, ins, out_shapes, aliases, n_sems, start, finish):
        self.ins, self.out_shapes, self.aliases = list(ins), list(out_shapes), dict(aliases)
        self.n_sems, self.start, self.finish = n_sems, start, finish


def _call(body, *, grid, in_specs, out_specs, out_shape, scratch_shapes, sem, name, args, carried=None):
    in_specs, out_specs, out_shape = list(in_specs), list(out_specs), list(out_shape)
    scratch_shapes = list(scratch_shapes)
    if carried is None:
        return pl.pallas_call(body, grid=grid, in_specs=in_specs, out_specs=out_specs, out_shape=out_shape,
                              scratch_shapes=scratch_shapes, compiler_params=_params(sem), name=name)(*args)
    n_in, n_out, n_scr = len(in_specs), len(out_specs), len(scratch_shapes)
    c_in, c_out = len(carried.ins), len(carried.out_shapes)

    def full(*refs):
        pos = [0]

        def take(k):
            part = refs[pos[0]:pos[0] + k]
            pos[0] += k
            return part

        ins, cins, outs, couts, scr = take(n_in), take(c_in), take(n_out), take(c_out), take(n_scr)
        send_sems, recv_sems = take(2)
        first = last = None
        for d, size in enumerate(grid):
            pid = pl.program_id(d)
            first = (pid == 0) if first is None else first & (pid == 0)
            last = (pid == size - 1) if last is None else last & (pid == size - 1)

        @pl.when(first)
        def _():
            carried.start(cins, couts, send_sems, recv_sems)

        body(*ins, *outs, *scr)

        @pl.when(last)
        def _():
            carried.finish(cins, couts, send_sems, recv_sems)

    sems = [pltpu.SemaphoreType.DMA((carried.n_sems,)), pltpu.SemaphoreType.DMA((carried.n_sems,))]
    return pl.pallas_call(
        full, grid=grid, in_specs=in_specs + [ANY] * c_in, out_specs=out_specs + [ANY] * c_out,
        out_shape=out_shape + carried.out_shapes, scratch_shapes=scratch_shapes + sems,
        input_output_aliases={n_in + k: n_out + v for k, v in carried.aliases.items()},
        compiler_params=_params(tuple("arbitrary" for _ in grid)), name=name,
    )(*args, *carried.ins)


def _run_carried(carried, name):
    c_in = len(carried.ins)

    def body(*refs):
        cins, couts = refs[:c_in], refs[c_in:c_in + len(carried.out_shapes)]
        send_sems, recv_sems = refs[-2:]
        carried.start(cins, couts, send_sems, recv_sems)
        carried.finish(cins, couts, send_sems, recv_sems)

    return pl.pallas_call(
        body, in_specs=[ANY] * c_in, out_specs=[ANY] * len(carried.out_shapes), out_shape=carried.out_shapes,
        scratch_shapes=[pltpu.SemaphoreType.DMA((carried.n_sems,)), pltpu.SemaphoreType.DMA((carried.n_sems,))],
        input_output_aliases=carried.aliases, name=name,
    )(*carried.ins)


def _matmul(a, b, *, form, out_dtype, tm, tn, tk, name, bias=None, add=None, out_sharded=False, carried=None):
    b3 = b.ndim == 3
    resident = 0
    if form == "nn":
        m, k = a.shape
        n = b.shape[0] * b.shape[2] if b3 else b.shape[1]
        dn = (((1,), (0,)), ((), ()))
        a_spec = pl.BlockSpec((tm, tk), lambda i, j, kk: (i, kk))
        if b3 and tn == n and tk == k:
            resident = b.shape[0]
            b_spec = pl.BlockSpec(b.shape, lambda i, j, kk: (0, 0, 0))
        else:
            b_spec = (pl.BlockSpec((None, tk, tn), lambda i, j, kk: (j, kk, 0)) if b3
                      else pl.BlockSpec((tk, tn), lambda i, j, kk: (kk, j)))
    elif form == "nt":
        m, k = a.shape
        n = b.shape[1] if b3 else b.shape[0]
        dn = (((1,), (1,)), ((), ()))
        a_spec = pl.BlockSpec((tm, tk), lambda i, j, kk: (i, kk))
        if b3 and tk == k:
            resident = b.shape[0]
            b_spec = pl.BlockSpec((resident, tn, b.shape[2]), lambda i, j, kk: (0, j, 0))
        else:
            b_spec = (pl.BlockSpec((None, tn, tk), lambda i, j, kk: (kk, j, 0)) if b3
                      else pl.BlockSpec((tn, tk), lambda i, j, kk: (j, kk)))
    else:
        k, m = a.shape
        n = b.shape[1]
        dn = (((0,), (0,)), ((), ()))
        a_spec = pl.BlockSpec((tk, tm), lambda i, j, kk: (kk, i))
        b_spec = pl.BlockSpec((tk, tn), lambda i, j, kk: (kk, j))
    assert m % tm == 0 and n % tn == 0 and k % tk == 0, (name, m, n, k, tm, tn, tk)
    nk = k // tk
    in_specs, args = [a_spec, b_spec], [a, b]
    if bias is not None:
        in_specs.append(pl.BlockSpec((1, tn), lambda i, j, kk: (0, j)))
        args.append(bias)
    if add is not None:
        in_specs.append(pl.BlockSpec((tm, tn), lambda i, j, kk: (i, j)))
        args.append(add)
    if out_sharded:
        out_shape = _sds((n // tn, m, tn), out_dtype)
        out_spec = pl.BlockSpec((None, tm, tn), lambda i, j, kk: (j, i, 0))
    else:
        out_shape = _sds((m, n), out_dtype)
        out_spec = pl.BlockSpec((tm, tn), lambda i, j, kk: (i, j))

    def body(*refs):
        a_ref, b_ref = refs[0], refs[1]
        pos = 2
        bias_ref = add_ref = None
        if bias is not None:
            bias_ref, pos = refs[pos], pos + 1
        if add is not None:
            add_ref, pos = refs[pos], pos + 1
        o_ref = refs[pos]
        if resident and form == "nn":
            ns = b_ref.shape[2]
            for s in range(resident):
                cols = slice(s * ns, (s + 1) * ns)
                ps = lax.dot_general(a_ref[...], b_ref[s], dn, preferred_element_type=F32)
                if bias_ref is not None:
                    ps = ps + bias_ref[:, cols]
                o_ref[:, cols] = ps.astype(o_ref.dtype)
            return
        if resident:
            ks = b_ref.shape[2]
            p = None
            for s in range(resident):
                ps = lax.dot_general(a_ref[:, s * ks:(s + 1) * ks], b_ref[s], dn, preferred_element_type=F32)
                p = ps if p is None else p + ps
        else:
            av, bv = a_ref[...], b_ref[...]
            if av.dtype != BF16:
                av = av.astype(BF16)
            if bv.dtype != BF16:
                bv = bv.astype(BF16)
            p = lax.dot_general(av, bv, dn, preferred_element_type=F32)

        def finish(acc):
            if bias_ref is not None:
                acc = acc + bias_ref[...]
            if add_ref is not None:
                acc = acc + add_ref[...]
            o_ref[...] = acc.astype(o_ref.dtype)

        if nk == 1:
            finish(p)
        else:
            acc_ref = refs[pos + 1]
            kk = pl.program_id(2)

            @pl.when(kk == 0)
            def _():
                acc_ref[...] = p

            @pl.when(kk > 0)
            def _():
                acc_ref[...] += p

            @pl.when(kk == nk - 1)
            def _():
                finish(acc_ref[...])

    res = _call(body, grid=(m // tm, n // tn, nk), in_specs=in_specs, out_specs=[out_spec], out_shape=[out_shape],
                scratch_shapes=[pltpu.VMEM((tm, tn), F32)] if nk > 1 else [],
                sem=("parallel", "parallel", "arbitrary"), name=name, args=args, carried=carried)
    return res[0] if carried is None else (res[0], res[1:])


def _rowcall(fn, rows, consts, row_outs, acc_outs, *, name, tm=ROW_TILE, col_grid=1):
    n_rows = rows[0][0].shape[0]
    assert n_rows % tm == 0
    grid = (col_grid, n_rows // tm)
    in_specs = [pl.BlockSpec((tm, w), functools.partial(lambda c, i, cb: (i, cb + c), cb=cb)) for _, w, cb in rows]
    in_specs += [pl.BlockSpec(k.shape, functools.partial(lambda c, i, nd: (0,) * nd, nd=k.ndim)) for k in consts]
    out_specs = [pl.BlockSpec((tm, w), lambda c, i: (i, c)) for _, _, _, w in row_outs]
    out_specs += [pl.BlockSpec((r, w), lambda c, i: (0, c)) for r, _, w in acc_outs]
    out_shape = [_sds((nr, nc), dt) for nr, nc, dt, _ in row_outs] + [_sds((r, nc), F32) for r, nc, _ in acc_outs]
    n_in, n_ro = len(rows) + len(consts), len(row_outs)

    def body(*refs):
        res = fn(*[r[...] for r in refs[:n_in]])
        if not isinstance(res, (tuple, list)):
            res = (res,)
        outs = refs[n_in:]
        for o_ref, val in zip(outs[:n_ro], res[:n_ro]):
            o_ref[...] = val.astype(o_ref.dtype)
        if acc_outs:
            first = pl.program_id(1) == 0

            @pl.when(first)
            def _():
                for o_ref, val in zip(outs[n_ro:], res[n_ro:]):
                    o_ref[...] = val

            @pl.when(jnp.logical_not(first))
            def _():
                for o_ref, val in zip(outs[n_ro:], res[n_ro:]):
                    o_ref[...] += val

    out = pl.pallas_call(
        body, grid=grid, in_specs=in_specs, out_specs=out_specs, out_shape=out_shape,
        compiler_params=_params(("arbitrary", "arbitrary")), name=name,
    )(*[r[0] for r in rows], *consts)
    return out


def _matmul_rows(b, *, form, tm, tk, fn, rows, consts, row_outs, acc_outs, name, a=None, a_rows=None, a_fn=None,
                 carried=None):
    b3 = b.ndim == 3
    resident = 0
    if form == "nn":
        k, n = b.shape
        b_spec = pl.BlockSpec((tk, n), lambda i, kk: (kk, 0))
        dn = (((1,), (0,)), ((), ()))
    else:
        n = b.shape[1] if b3 else b.shape[0]
        k = b.shape[0] * b.shape[2] if b3 else b.shape[1]
        if b3 and tk == k:
            resident = b.shape[0]
            b_spec = pl.BlockSpec(b.shape, lambda i, kk: (0, 0, 0))
        else:
            b_spec = (pl.BlockSpec((None, n, tk), lambda i, kk: (kk, 0, 0)) if b3
                      else pl.BlockSpec((n, tk), lambda i, kk: (0, kk)))
        dn = (((1,), (1,)), ((), ()))
    nk = k // tk
    lhs_in = [(a, tk, 0)] if a is not None else list(a_rows)
    assert a is not None or nk == 1
    m = lhs_in[0][0].shape[0]
    n_lhs = len(lhs_in)
    in_specs = [pl.BlockSpec((tm, tk), lambda i, kk: (i, kk))] if a is not None else [
        pl.BlockSpec((tm, w), functools.partial(lambda i, kk, cb: (i, cb), cb=cb)) for _, w, cb in a_rows]
    in_specs.append(b_spec)
    in_specs += [pl.BlockSpec((tm, w), functools.partial(lambda i, kk, cb: (i, cb), cb=cb)) for _, w, cb in rows]
    in_specs += [pl.BlockSpec(c.shape, functools.partial(lambda i, kk, nd: (0,) * nd, nd=c.ndim)) for c in consts]
    out_specs = [pl.BlockSpec((tm, w), lambda i, kk: (i, 0)) for _, w in row_outs]
    out_specs += [pl.BlockSpec((r, w), lambda i, kk: (0, 0)) for r, w in acc_outs]
    out_shape = [_sds((m, w), dt) for dt, w in row_outs] + [_sds((r, w), F32) for r, w in acc_outs]
    n_rows, n_consts, n_ro, n_acc = len(rows), len(consts), len(row_outs), len(acc_outs)

    def body(*refs):
        pos = n_lhs + 1
        row_refs, const_refs = refs[pos:pos + n_rows], refs[pos + n_rows:pos + n_rows + n_consts]
        pos += n_rows + n_consts
        out_refs, acc_refs = refs[pos:pos + n_ro], refs[pos + n_ro:pos + n_ro + n_acc]
        i, kk = pl.program_id(0), pl.program_id(1)
        if resident:
            b_ref, ks, p = refs[n_lhs], b.shape[2], None
            for s in range(resident):
                ps = lax.dot_general(refs[0][:, s * ks:(s + 1) * ks], b_ref[s], dn, preferred_element_type=F32)
                p = ps if p is None else p + ps
        else:
            lhs = refs[0][...] if a is not None else a_fn(*[r[...] for r in refs[:n_lhs]]).astype(BF16)
            p = lax.dot_general(lhs, refs[n_lhs][...], dn, preferred_element_type=F32)

        def finish(acc):
            extra = [r[...] for r in row_refs] + [c[...] for c in const_refs]
            res = fn(acc, lhs, *extra) if a is None else fn(acc, *extra)
            for o_ref, val in zip(out_refs, res[:n_ro]):
                o_ref[...] = val.astype(o_ref.dtype)
            if n_acc:
                @pl.when(i == 0)
                def _():
                    for o_ref, val in zip(acc_refs, res[n_ro:]):
                        o_ref[...] = val

                @pl.when(i > 0)
                def _():
                    for o_ref, val in zip(acc_refs, res[n_ro:]):
                        o_ref[...] += val

        if nk == 1:
            finish(p)
        else:
            acc_ref = refs[pos + n_ro + n_acc]

            @pl.when(kk == 0)
            def _():
                acc_ref[...] = p

            @pl.when(kk > 0)
            def _():
                acc_ref[...] += p

            @pl.when(kk == nk - 1)
            def _():
                finish(acc_ref[...])

    res = _call(body, grid=(m // tm, nk), in_specs=in_specs, out_specs=out_specs, out_shape=out_shape,
                scratch_shapes=[pltpu.VMEM((tm, n), F32)] if nk > 1 else [], sem=("arbitrary", "arbitrary"),
                name=name, args=[r[0] for r in lhs_in] + [b] + [r[0] for r in rows] + list(consts), carried=carried)
    own = n_ro + n_acc
    return res[:own] if carried is None else (res[:own], res[own:])


def _colsum(v):
    return jnp.sum(v, axis=0, keepdims=True)


def _sigmoid(v):
    return 1.0 / (1.0 + jnp.exp(-v))


_GELU_C = math.sqrt(2.0 / math.pi)


def _gelu(v):
    return 0.5 * v * (1.0 + jnp.tanh(_GELU_C * (v + 0.044715 * (v * v * v))))


def _gelu_and_grad(v):
    th = jnp.tanh(_GELU_C * (v + 0.044715 * (v * v * v)))
    g = 0.5 * v * (1.0 + th)
    dg = 0.5 * (1.0 + th) + 0.5 * v * (1.0 - th * th) * (_GELU_C * (1.0 + 3.0 * 0.044715 * (v * v)))
    return g, dg


def _rms_stats(v):
    r = lax.rsqrt(jnp.mean(v * v, axis=-1, keepdims=True) + EPS)
    return v * r, r


def _rms_bwd(dn, vn, r):
    return r * (dn - vn * jnp.mean(dn * vn, axis=-1, keepdims=True))


def _pre_norm(x, g, sc, sh, name):
    def fn(xv, gv, scv, shv):
        xn, _ = _rms_stats(xv)
        return (xn * gv) * (1.0 + scv) + shv
    return _rowcall(fn, [(x, D, 0)], [g, sc, sh], [(x.shape[0], D, BF16, D)], [], name=name,
                    tm=min(2 * ROW_TILE, x.shape[0]))[0]


def _pre_norm_bwd(dh, x, dx_other, g, sc, name):
    def fn(dhv, xv, dov, gv, scv):
        xn, r = _rms_stats(xv)
        yn = xn * gv
        dyn = dhv * (1.0 + scv)
        dx = _rms_bwd(dyn * gv, xn, r)
        return dov + dx, _colsum(dhv), _colsum(dhv * yn), _colsum(dyn * xn)
    t = x.shape[0]
    return _rowcall(fn, [(dh, D, 0), (x, D, 0), (dx_other, D, 0)], [g, sc], [(t, D, F32, D)],
                    [(1, D, D)] * 3, name=name, tm=min(2 * ROW_TILE, t))


CONV_HALO = 32


def _layer_norm_parts(u):
    mu = jnp.mean(u, axis=-1, keepdims=True)
    d = u - mu
    r = lax.rsqrt(jnp.mean(d * d, axis=-1, keepdims=True) + EPS)
    return d * r, r


LANES = 128
SUBLANE_ROWS = 8
CONV_ROWS = 64


def _lanes(c):
    return slice(c * LANES, (c + 1) * LANES)


def _conv_branch(z, w_dw, b_dw, g_ln, b_ln, name, tm=ROW_TILE, carried=None):
    t = z.shape[0]
    per = tm // CONV_HALO
    n_chunks = 512 // LANES

    def body(ga_ref, gb_ref, gah_ref, gbh_ref, w_ref, b_ref, g_ref, bl_ref, u1_ref, u3_ref, scr):
        i = pl.program_id(0)
        u0h = jnp.where(i > 0, gah_ref[...] * _sigmoid(gbh_ref[...]), 0.0)
        u0 = ga_ref[...] * _sigmoid(gb_ref[...])
        for c in range(n_chunks):
            scr[c, 0:CONV_HALO, :] = u0h[:, _lanes(c)]
            scr[c, CONV_HALO:CONV_HALO + tm, :] = u0[:, _lanes(c)]
        for c in range(n_chunks):
            for r0 in range(0, tm, CONV_ROWS):
                acc = jnp.zeros((CONV_ROWS, LANES), F32) + b_ref[:, _lanes(c)]
                for j in range(CONV_K):
                    acc = acc + w_ref[j:j + 1, _lanes(c)] * scr[c, pl.ds(r0 + CONV_HALO - (CONV_K - 1) + j, CONV_ROWS), :]
                u1_ref[r0:r0 + CONV_ROWS, _lanes(c)] = acc
        xh, _ = _layer_norm_parts(u1_ref[...])
        u2 = xh * g_ref[...] + bl_ref[...]
        u3_ref[...] = (u2 * _sigmoid(u2)).astype(BF16)

    cur = lambda cb: pl.BlockSpec((tm, 512), lambda i: (i, cb))
    halo = lambda cb: pl.BlockSpec((CONV_HALO, 512), lambda i: (jnp.maximum(i * per - 1, 0), cb))
    whole = lambda a: pl.BlockSpec(a.shape, lambda i: (0, 0))
    res = _call(
        body, grid=(t // tm,),
        in_specs=[cur(3), cur(4), halo(3), halo(4), whole(w_dw), whole(b_dw), whole(g_ln), whole(b_ln)],
        out_specs=[pl.BlockSpec((tm, 512), lambda i: (i, 0))] * 2,
        out_shape=[_sds((t, 512), F32), _sds((t, 512), BF16)],
        scratch_shapes=[pltpu.VMEM((n_chunks, CONV_HALO + tm, LANES), F32)],
        sem=("arbitrary",), name=name, args=(z, z, z, z, w_dw, b_dw, g_ln, b_ln), carried=carried)
    return res[:2] if carried is None else (res[:2], res[2:])


def _conv_branch_bwd(du3, u1, z, w_dw, g_ln, b_ln, name, tm=ROW_TILE, carried=None):
    t = z.shape[0]
    per = tm // CONV_HALO
    last = t // tm - 1
    n_chunks = 512 // LANES

    def du1_of(du3v, u1v, g, b):
        xh, r = _layer_norm_parts(u1v)
        u2 = xh * g + b
        s = _sigmoid(u2)
        du2 = du3v * (s * (1.0 + u2 * (1.0 - s)))
        dxh = du2 * g
        du1 = r * (dxh - jnp.mean(dxh, axis=-1, keepdims=True) - xh * jnp.mean(dxh * xh, axis=-1, keepdims=True))
        return du1, du2, xh

    def body(d_ref, u_ref, dn_ref, un_ref, ga_ref, gb_ref, gah_ref, gbh_ref, w_ref, g_ref, bl_ref,
             dglu_ref, dw_ref, dbdw_ref, dg_ref, dbl_ref, dbin_ref, scr, scd):
        i = pl.program_id(0)
        g, b = g_ref[...], bl_ref[...]
        du1, du2, xh = du1_of(d_ref[...], u_ref[...], g, b)
        du1n, _, _ = du1_of(dn_ref[...], un_ref[...], g, b)
        du1n = jnp.where(i < last, du1n, 0.0)
        sgb = _sigmoid(gb_ref[...])
        ga = ga_ref[...]
        u0 = ga * sgb
        u0h = jnp.where(i > 0, gah_ref[...] * _sigmoid(gbh_ref[...]), 0.0)
        for c in range(n_chunks):
            scd[c, 0:tm, :] = du1[:, _lanes(c)]
            scd[c, tm:tm + CONV_HALO, :] = du1n[:, _lanes(c)]
            scr[c, 0:CONV_HALO, :] = u0h[:, _lanes(c)]
            scr[c, CONV_HALO:CONV_HALO + tm, :] = u0[:, _lanes(c)]

        @pl.when(i == 0)
        def _():
            for ref in (dw_ref, dbdw_ref, dg_ref, dbl_ref, dbin_ref):
                ref[...] = jnp.zeros_like(ref)

        dsg = ga * (sgb * (1.0 - sgb))
        for c in range(n_chunks):
            gate = slice(512 + c * LANES, 512 + (c + 1) * LANES)
            for r0 in range(0, tm, CONV_ROWS):
                rows = slice(r0, r0 + CONV_ROWS)
                du0 = jnp.zeros((CONV_ROWS, LANES), F32)
                for j in range(CONV_K):
                    du0 = du0 + w_ref[j:j + 1, _lanes(c)] * scd[c, pl.ds(r0 + CONV_K - 1 - j, CONV_ROWS), :]
                dga = du0 * sgb[rows, _lanes(c)]
                dgb = du0 * dsg[rows, _lanes(c)]
                dglu_ref[rows, _lanes(c)] = dga.astype(BF16)
                dglu_ref[rows, gate] = dgb.astype(BF16)
                dbin_ref[:, _lanes(c)] += _colsum(dga)
                dbin_ref[:, gate] += _colsum(dgb)
            for j in range(CONV_K):
                dwj = jnp.zeros((SUBLANE_ROWS, LANES), F32)
                for r0 in range(0, tm, CONV_ROWS):
                    prod = (scd[c, pl.ds(r0, CONV_ROWS), :]
                            * scr[c, pl.ds(r0 + CONV_HALO - (CONV_K - 1) + j, CONV_ROWS), :])
                    dwj = dwj + jnp.sum(prod.reshape(CONV_ROWS // SUBLANE_ROWS, SUBLANE_ROWS, LANES), axis=0)
                dw_ref[j:j + 1, _lanes(c)] += _colsum(dwj)
        dbdw_ref[...] += _colsum(du1)
        dg_ref[...] += _colsum(du2 * xh)
        dbl_ref[...] += _colsum(du2)

    cur = lambda cb: pl.BlockSpec((tm, 512), lambda i: (i, cb))
    prev = lambda cb: pl.BlockSpec((CONV_HALO, 512), lambda i: (jnp.maximum(i * per - 1, 0), cb))
    nxt = pl.BlockSpec((CONV_HALO, 512), lambda i: (jnp.minimum((i + 1) * per, t // CONV_HALO - 1), 0))
    whole = lambda a: pl.BlockSpec(a.shape, lambda i: (0, 0))
    acc = lambda r, w: pl.BlockSpec((r, w), lambda i: (0, 0))
    res = _call(
        body, grid=(t // tm,),
        in_specs=[cur(0), cur(0), nxt, nxt, cur(3), cur(4), prev(3), prev(4), whole(w_dw), whole(g_ln), whole(b_ln)],
        out_specs=[pl.BlockSpec((tm, 1024), lambda i: (i, 0)), acc(CONV_K, 512), acc(1, 512), acc(1, 512),
                   acc(1, 512), acc(1, 1024)],
        out_shape=[_sds((t, 1024), BF16), _sds((CONV_K, 512), F32), _sds((1, 512), F32), _sds((1, 512), F32),
                   _sds((1, 512), F32), _sds((1, 1024), F32)],
        scratch_shapes=[pltpu.VMEM((n_chunks, CONV_HALO + tm, LANES), F32),
                        pltpu.VMEM((n_chunks, tm + CONV_HALO, LANES), F32)],
        sem=("arbitrary",), name=name, args=(du3, u1, du3, u1, z, z, z, z, w_dw, g_ln, b_ln), carried=carried)
    return res[:6] if carried is None else (res[:6], res[6:])


FF_BLOCK = D_FF // 2
FF_HALO = 8
FF_CHUNKS = FF_BLOCK // LANES


FF_ROWS = 64
FF_EXT_ROWS = 88


def _ffn_conv(w_ref, b_ref, scr, k, rows, r0=0):
    acc = b_ref[:, _lanes(k)] + w_ref[0:1, _lanes(k)] * scr[k, pl.ds(r0 + FF_HALO - 2, rows), :]
    acc = acc + w_ref[1:2, _lanes(k)] * scr[k, pl.ds(r0 + FF_HALO - 1, rows), :]
    return acc + w_ref[2:3, _lanes(k)] * scr[k, pl.ds(r0 + FF_HALO, rows), :]


def _ffn_act(up, w3, b3, name, tm=ROW_TILE, carried=None):
    t = up.shape[0]
    per = tm // FF_HALO
    wide = 2 * FF_BLOCK

    def body(u_ref, uh_ref, w_ref, b_ref, o_ref, scr):
        i = pl.program_id(1)
        for k in range(2 * FF_CHUNKS):
            scr[k, 0:FF_HALO, :] = jnp.where(i > 0, uh_ref[:, _lanes(k)], 0.0)
            scr[k, FF_HALO:FF_HALO + tm, :] = u_ref[:, _lanes(k)]
        for cc in range(FF_CHUNKS):
            for r0 in range(0, tm, FF_ROWS):
                val = _ffn_conv(w_ref, b_ref, scr, cc, FF_ROWS, r0)
                gate = _ffn_conv(w_ref, b_ref, scr, FF_CHUNKS + cc, FF_ROWS, r0)
                o_ref[r0:r0 + FF_ROWS, _lanes(cc)] = (_gelu(gate) * val).astype(BF16)

    res = _call(
        body, grid=(2, t // tm),
        in_specs=[pl.BlockSpec((tm, wide), lambda c, i: (i, c)),
                  pl.BlockSpec((FF_HALO, wide), lambda c, i: (jnp.maximum(i * per - 1, 0), c)),
                  pl.BlockSpec((FFN_K, wide), lambda c, i: (0, c)),
                  pl.BlockSpec((1, wide), lambda c, i: (0, c))],
        out_specs=[pl.BlockSpec((tm, FF_BLOCK), lambda c, i: (i, c))],
        out_shape=[_sds((t, D_FF), BF16)],
        scratch_shapes=[pltpu.VMEM((2 * FF_CHUNKS, FF_HALO + tm, LANES), F32)],
        sem=("arbitrary", "arbitrary"), name=name, args=(up, up, w3, b3), carried=carried)
    return res[0] if carried is None else (res[0], res[1:])


def _ffn_act_bwd(dact, up, w3, b3, name, tm=ROW_TILE):
    t = up.shape[0]
    per = tm // FF_HALO
    wide = 2 * FF_BLOCK
    last = t // tm - 1
    ext = tm + FF_HALO

    def body(u_ref, up_ref, un_ref, d_ref, dn_ref, w_ref, b_ref, o_ref, dw_ref, db_ref, scr, scd):
        i = pl.program_id(1)
        for k in range(2 * FF_CHUNKS):
            scr[k, 0:FF_HALO, :] = jnp.where(i > 0, up_ref[:, _lanes(k)], 0.0)
            scr[k, FF_HALO:FF_HALO + tm, :] = u_ref[:, _lanes(k)]
            scr[k, FF_HALO + tm:FF_HALO + ext, :] = un_ref[:, _lanes(k)]
        dn = jnp.where(i < last, dn_ref[...], 0.0)

        @pl.when(i == 0)
        def _():
            dw_ref[...] = jnp.zeros_like(dw_ref)
            db_ref[...] = jnp.zeros_like(db_ref)

        for cc in range(FF_CHUNKS):
            gc = FF_CHUNKS + cc
            for r0 in range(0, ext, FF_EXT_ROWS):
                rows = pl.ds(r0, FF_EXT_ROWS)
                val = _ffn_conv(w_ref, b_ref, scr, cc, FF_EXT_ROWS, r0)
                gel, dgel = _gelu_and_grad(_ffn_conv(w_ref, b_ref, scr, gc, FF_EXT_ROWS, r0))
                da = d_ref[r0:r0 + FF_EXT_ROWS, _lanes(cc)] if r0 + FF_EXT_ROWS <= tm else jnp.concatenate(
                    [d_ref[r0:tm, _lanes(cc)], dn[:, _lanes(cc)]], axis=0)
                scd[cc, rows, :] = da * gel
                scd[gc, rows, :] = da * val * dgel
            for k in (cc, gc):
                dwk = [jnp.zeros((SUBLANE_ROWS, LANES), F32) for _ in range(FFN_K)]
                dbk = jnp.zeros((SUBLANE_ROWS, LANES), F32)
                for r0 in range(0, tm, FF_ROWS):
                    shifted = [scd[k, pl.ds(r0 + FFN_K - 1 - j, FF_ROWS), :] for j in range(FFN_K)]
                    ucur = scr[k, pl.ds(r0 + FF_HALO, FF_ROWS), :]
                    o_ref[r0:r0 + FF_ROWS, _lanes(k)] = (
                        w_ref[0:1, _lanes(k)] * shifted[0] + w_ref[1:2, _lanes(k)] * shifted[1]
                        + w_ref[2:3, _lanes(k)] * shifted[2]).astype(BF16)
                    fold = lambda v: jnp.sum(v.reshape(FF_ROWS // SUBLANE_ROWS, SUBLANE_ROWS, LANES), axis=0)
                    for j in range(FFN_K):
                        dwk[j] = dwk[j] + fold(shifted[j] * ucur)
                    dbk = dbk + fold(shifted[FFN_K - 1])
                for j in range(FFN_K):
                    dw_ref[j:j + 1, _lanes(k)] += _colsum(dwk[j])
                db_ref[:, _lanes(k)] += _colsum(dbk)

    nblk = t // FF_HALO
    return pl.pallas_call(
        body, grid=(2, t // tm),
        in_specs=[pl.BlockSpec((tm, wide), lambda c, i: (i, c)),
                  pl.BlockSpec((FF_HALO, wide), lambda c, i: (jnp.maximum(i * per - 1, 0), c)),
                  pl.BlockSpec((FF_HALO, wide), lambda c, i: (jnp.minimum((i + 1) * per, nblk - 1), c)),
                  pl.BlockSpec((tm, FF_BLOCK), lambda c, i: (i, c)),
                  pl.BlockSpec((FF_HALO, FF_BLOCK), lambda c, i: (jnp.minimum((i + 1) * per, nblk - 1), c)),
                  pl.BlockSpec((FFN_K, wide), lambda c, i: (0, c)),
                  pl.BlockSpec((1, wide), lambda c, i: (0, c))],
        out_specs=[pl.BlockSpec((tm, wide), lambda c, i: (i, c)),
                   pl.BlockSpec((FFN_K, wide), lambda c, i: (0, c)),
                   pl.BlockSpec((1, wide), lambda c, i: (0, c))],
        out_shape=[_sds((t, 2 * D_FF), BF16), _sds((FFN_K, 2 * D_FF), F32), _sds((1, 2 * D_FF), F32)],
        scratch_shapes=[pltpu.VMEM((2 * FF_CHUNKS, FF_HALO + ext, LANES), F32),
                        pltpu.VMEM((2 * FF_CHUNKS, ext, LANES), F32)],
        compiler_params=_params(("arbitrary", "arbitrary")), name=name,
    )(up, up, up, dact, dact, w3, b3)


def _toeplitz_map():
    f = np.zeros((TOEP, REL_PAD), np.float32)
    for m in range(TOEP - 1):
        rel = (WINDOW - 1) - m
        f[m, int(np.clip(rel, -MAX_REL, MAX_REL)) + MAX_REL] = 1.0
    return f


def _split3(v):
    hi = v.astype(BF16)
    r1 = v - hi.astype(F32)
    mid = r1.astype(BF16)
    lo = (r1 - mid.astype(F32)).astype(BF16)
    return hi, mid, lo


def _exact_select(v, sel):
    out = None
    for part in _split3(v):
        p = jnp.dot(part, sel, preferred_element_type=F32)
        out = p if out is None else out + p
    return out


def _select_call(v, sel, name):
    def body(v_ref, s_ref, o_ref):
        o_ref[...] = _exact_select(v_ref[...], s_ref[...])
    return pl.pallas_call(body, out_shape=_sds((v.shape[0], sel.shape[1]), F32), name=name)(v, sel)


def _band_bias(gen_row):
    b0 = jnp.broadcast_to(gen_row, (Q_TILE, TOEP))
    bias = pltpu.roll(b0, TOEP - (Q_TILE - 1), 1, stride=1, stride_axis=0)[:, :WINDOW]
    qq = lax.broadcasted_iota(jnp.int32, (Q_TILE, WINDOW), 0) // CHUNK
    kc = lax.broadcasted_iota(jnp.int32, (Q_TILE, WINDOW), 1) // CHUNK
    return jnp.where((kc >= qq) & (kc <= qq + LEFT_CHUNKS), bias, NEG_INF)


PAD_ROWS = WINDOW - Q_TILE
NT_DIMS = (((1,), (1,)), ((), ()))
TN_DIMS = (((0,), (0,)), ((), ()))


def _head_mask(hh):
    lane = lax.broadcasted_iota(jnp.int32, (1, 128), 1)
    return (lane < 64) if hh == 0 else (lane >= 64)


SOFTMAX_ROWS = 16


def _probs_block(s_scr, bias, hh, rows, q_start):
    s = s_scr[rows, :] + bias[hh, rows, :]
    col = lax.broadcasted_iota(jnp.int32, (SOFTMAX_ROWS, WINDOW), 1)
    s = jnp.where(col >= PAD_ROWS - q_start, s, NEG_INF)
    p = jnp.exp(s - jnp.max(s, axis=-1, keepdims=True))
    return p / jnp.sum(p, axis=-1, keepdims=True)


def _attention(z, gen, name, carried=None):
    t = z.shape[0]
    n_i = t // STEP_ROWS

    def body(q_ref, k_ref, v_ref, g_ref, o_ref, kpad, vpad, bias, s_scr, p_scr):
        hp, i = pl.program_id(0), pl.program_id(1)

        @pl.when(i == 0)
        def _():
            kpad[0:PAD_ROWS, :] = jnp.zeros((PAD_ROWS, 128), BF16)
            vpad[0:PAD_ROWS, :] = jnp.zeros((PAD_ROWS, 128), BF16)
            kpad[PAD_ROWS:PAD_ROWS + t, :] = k_ref[...].astype(BF16)
            vpad[PAD_ROWS:PAD_ROWS + t, :] = v_ref[...].astype(BF16)
            for hh in range(2):
                bias[hh] = _band_bias(g_ref[pl.ds(2 * hp + hh, 1), :])

        for q0 in range(0, STEP_ROWS, Q_TILE):
            q_start = i * STEP_ROWS + q0
            win = pl.ds(pl.multiple_of(q_start, Q_TILE), WINDOW)
            out = None
            for hh in range(2):
                mask = _head_mask(hh)
                qm = jnp.where(mask, q_ref[q0:q0 + Q_TILE, :] * (CHUNK ** -0.5), 0.0).astype(BF16)
                slot = 2 * (q0 // Q_TILE) + hh
                s_scr[slot] = lax.dot_general(qm, kpad[win, :], NT_DIMS, preferred_element_type=F32)
                for r0 in range(0, Q_TILE, SOFTMAX_ROWS):
                    rows = slice(r0, r0 + SOFTMAX_ROWS)
                    p_scr[slot, rows, :] = _probs_block(s_scr.at[slot], bias, hh, rows, q_start).astype(BF16)
                o = jnp.dot(p_scr[slot], vpad[win, :], preferred_element_type=F32)
                out = jnp.where(mask, o, 0.0) if out is None else jnp.where(mask, o, out)
            o_ref[q0:q0 + Q_TILE, :] = out.astype(BF16)

    res = _call(
        body, grid=(4, n_i),
        in_specs=[pl.BlockSpec((STEP_ROWS, 128), lambda h, i: (i, h)),
                  pl.BlockSpec((t, 128), lambda h, i: (0, 4 + h)),
                  pl.BlockSpec((t, 128), lambda h, i: (0, 8 + h)),
                  pl.BlockSpec((N_HEADS, TOEP), lambda h, i: (0, 0))],
        out_specs=[pl.BlockSpec((STEP_ROWS, 128), lambda h, i: (i, h))],
        out_shape=[_sds((t, 512), BF16)],
        scratch_shapes=[pltpu.VMEM((PAD_ROWS + t, 128), BF16), pltpu.VMEM((PAD_ROWS + t, 128), BF16),
                        pltpu.VMEM((2, Q_TILE, WINDOW), F32), pltpu.VMEM((4, Q_TILE, WINDOW), F32),
                        pltpu.VMEM((4, Q_TILE, WINDOW), BF16)],
        sem=("arbitrary", "arbitrary"), name=name, args=(z, z, z, gen), carried=carried)
    return res[0] if carried is None else (res[0], res[1:])


def _attention_bwd(z, datt, gen, name, carried=None):
    t = z.shape[0]
    n_i = t // STEP_ROWS

    def body(q_ref, k_ref, v_ref, d_ref, g_ref, dq_ref, dk_ref, dv_ref, sq_ref, sk_ref, sv_ref, dg_ref,
             kpad, vpad, dkacc, dvacc, bias, dsacc, s_scr, dp_scr, p_scr, ds_scr):
        hp, i = pl.program_id(0), pl.program_id(1)

        @pl.when(i == 0)
        def _():
            kpad[0:PAD_ROWS, :] = jnp.zeros((PAD_ROWS, 128), BF16)
            vpad[0:PAD_ROWS, :] = jnp.zeros((PAD_ROWS, 128), BF16)
            kpad[PAD_ROWS:PAD_ROWS + t, :] = k_ref[...].astype(BF16)
            vpad[PAD_ROWS:PAD_ROWS + t, :] = v_ref[...].astype(BF16)
            dkacc[...] = jnp.zeros_like(dkacc)
            dvacc[...] = jnp.zeros_like(dvacc)
            dsacc[...] = jnp.zeros_like(dsacc)
            for hh in range(2):
                bias[hh] = _band_bias(g_ref[pl.ds(2 * hp + hh, 1), :])

        dq_sum = None
        for q0 in range(0, STEP_ROWS, Q_TILE):
            q_start = i * STEP_ROWS + q0
            win = pl.ds(pl.multiple_of(q_start, Q_TILE), WINDOW)
            dq = None
            for hh in range(2):
                mask = _head_mask(hh)
                qm = jnp.where(mask, q_ref[q0:q0 + Q_TILE, :] * (CHUNK ** -0.5), 0.0).astype(BF16)
                dom = jnp.where(mask, d_ref[q0:q0 + Q_TILE, :], 0.0).astype(BF16)
                slot = 2 * (q0 // Q_TILE) + hh
                s_scr[slot] = lax.dot_general(qm, kpad[win, :], NT_DIMS, preferred_element_type=F32)
                dp_scr[slot] = lax.dot_general(dom, vpad[win, :], NT_DIMS, preferred_element_type=F32)
                for r0 in range(0, Q_TILE, SOFTMAX_ROWS):
                    rows = slice(r0, r0 + SOFTMAX_ROWS)
                    p = _probs_block(s_scr.at[slot], bias, hh, rows, q_start)
                    dp = dp_scr[slot, rows, :]
                    ds = p * (dp - jnp.sum(p * dp, axis=-1, keepdims=True))
                    dsacc[hh, rows, :] += ds
                    ds_scr[slot, rows, :] = ds.astype(BF16)
                    p_scr[slot, rows, :] = p.astype(BF16)
                ds16 = ds_scr[slot]
                dqh = jnp.dot(ds16, kpad[win, :], preferred_element_type=F32) * (CHUNK ** -0.5)
                dq = jnp.where(mask, dqh, 0.0) if dq is None else jnp.where(mask, dqh, dq)
                dkacc[win, :] += lax.dot_general(ds16, qm, TN_DIMS, preferred_element_type=F32)
                dvacc[win, :] += lax.dot_general(p_scr[slot], dom, TN_DIMS, preferred_element_type=F32)
            dq_ref[q0:q0 + Q_TILE, :] = dq.astype(BF16)
            dq_sum = _colsum(dq) if dq_sum is None else dq_sum + _colsum(dq)

        @pl.when(i == 0)
        def _():
            sq_ref[...] = dq_sum

        @pl.when(i > 0)
        def _():
            sq_ref[...] += dq_sum

        @pl.when(i == n_i - 1)
        def _():
            dk = dkacc[PAD_ROWS:PAD_ROWS + t, :]
            dv = dvacc[PAD_ROWS:PAD_ROWS + t, :]
            dk_ref[...] = dk.astype(BF16)
            dv_ref[...] = dv.astype(BF16)
            sk_ref[...] = _colsum(dk)
            sv_ref[...] = _colsum(dv)
            rr = lax.broadcasted_iota(jnp.int32, (Q_TILE, Q_TILE), 0)
            cc = lax.broadcasted_iota(jnp.int32, (Q_TILE, Q_TILE), 1)
            rev = jnp.where(rr + cc == Q_TILE - 1, 1.0, 0.0).astype(BF16)
            for hh in range(2):
                acc = None
                for part in _split3(dsacc[hh]):
                    pr = jnp.dot(rev, part, preferred_element_type=F32)
                    acc = pr if acc is None else acc + pr
                wide = jnp.concatenate([acc, jnp.zeros((Q_TILE, TOEP - WINDOW), F32)], axis=1)
                dg_ref[pl.ds(2 * hp + hh, 1), :] = _colsum(pltpu.roll(wide, 0, 1, stride=1, stride_axis=0))

    col = lambda off: pl.BlockSpec((t, 128), lambda h, i: (0, off + h))
    tile = lambda: pl.BlockSpec((STEP_ROWS, 128), lambda h, i: (i, h))
    sums = lambda: pl.BlockSpec((1, 128), lambda h, i: (0, h))
    res = _call(
        body, grid=(4, n_i),
        in_specs=[tile(), col(4), col(8), tile(), pl.BlockSpec((N_HEADS, TOEP), lambda h, i: (0, 0))],
        out_specs=[tile(), col(0), col(0), sums(), sums(), sums(), pl.BlockSpec((N_HEADS, TOEP), lambda h, i: (0, 0))],
        out_shape=[_sds((t, 512), BF16)] * 3 + [_sds((1, 512), F32)] * 3 + [_sds((N_HEADS, TOEP), F32)],
        scratch_shapes=[pltpu.VMEM((PAD_ROWS + t, 128), BF16), pltpu.VMEM((PAD_ROWS + t, 128), BF16),
                        pltpu.VMEM((PAD_ROWS + t, 128), F32), pltpu.VMEM((PAD_ROWS + t, 128), F32),
                        pltpu.VMEM((2, Q_TILE, WINDOW), F32), pltpu.VMEM((2, Q_TILE, WINDOW), F32),
                        pltpu.VMEM((4, Q_TILE, WINDOW), F32), pltpu.VMEM((4, Q_TILE, WINDOW), F32),
                        pltpu.VMEM((4, Q_TILE, WINDOW), BF16), pltpu.VMEM((4, Q_TILE, WINDOW), BF16)],
        sem=("arbitrary", "arbitrary"), name=name, args=(z, z, z, datt, gen), carried=carried)
    return res[:7] if carried is None else (res[:7], res[7:])


def _adamw_math(w, g, m, v):
    m = ADAM_B1 * m + (1.0 - ADAM_B1) * g
    v = ADAM_B2 * v + (1.0 - ADAM_B2) * (g * g)
    m_hat = m / (1.0 - ADAM_B1 ** ADAM_STEP)
    v_hat = v / (1.0 - ADAM_B2 ** ADAM_STEP)
    delta = -ADAM_LR * (m_hat / (jnp.sqrt(v_hat) + ADAM_EPS) + ADAM_WD * w)
    return delta, m, v


def _adamw_many(items, name):
    n = len(items)

    def body(*refs):
        ins, outs = refs[:4 * n], refs[4 * n:]
        for k in range(n):
            w, g, m, v = (r[...] for r in ins[4 * k:4 * k + 4])
            outs[3 * k][...], outs[3 * k + 1][...], outs[3 * k + 2][...] = _adamw_math(w, g, m, v)

    flat = [a for item in items for a in item]
    res = pl.pallas_call(body, out_shape=[_sds(item[0].shape, F32) for item in items for _ in range(3)],
                         name=name)(*flat)
    return [tuple(res[3 * k:3 * k + 3]) for k in range(n)]


def _adamw(w, g, m, v, name, after):
    r, c = w.shape
    tm = next(cand for cand in (256, 176, 128, 64, 32, 16, 8) if r % cand == 0)
    return _rowcall(lambda wv, gv, mv, vv, _: (gv,) + _adamw_math(wv, gv, mv, vv),
                    [(w, c, 0), (g, c, 0), (m, c, 0), (v, c, 0)], [after], [(r, c, F32, c)] * 4, [], name=name, tm=tm)


def _ada_fwd(c_all, w_shard, b_shard, name):
    n = w_shard.shape[1]
    tn = 512

    def body(c_ref, w_ref, b_ref, o_ref, a_ref):
        cv = c_ref[...]
        act = cv * _sigmoid(cv)
        a_ref[...] = act
        o_ref[...] = jnp.dot(act.astype(BF16), w_ref[...].astype(BF16), preferred_element_type=F32) + b_ref[...]

    return pl.pallas_call(
        body, grid=(n // tn,),
        in_specs=[pl.BlockSpec((8, D), lambda j: (0, 0)), pl.BlockSpec((D, tn), lambda j: (0, j)),
                  pl.BlockSpec((1, tn), lambda j: (0, j))],
        out_specs=[pl.BlockSpec((8, tn), lambda j: (0, j)), pl.BlockSpec((8, D), lambda j: (0, 0))],
        out_shape=[_sds((8, n), F32), _sds((8, D), F32)],
        compiler_params=_params(("arbitrary",)), name=name,
    )(c_all, w_shard, b_shard)


def _ada_bwd_adamw(act_t, dmod_shard, w, m, v, name):
    r, c = w.shape
    tm = 256

    def body(a_ref, d_ref, w_ref, m_ref, v_ref, g_ref, dl_ref, nm_ref, nv_ref):
        g = jnp.dot(a_ref[...], d_ref[...], precision=lax.Precision.HIGHEST, preferred_element_type=F32)
        g_ref[...] = g
        dl_ref[...], nm_ref[...], nv_ref[...] = _adamw_math(w_ref[...], g, m_ref[...], v_ref[...])

    blk = pl.BlockSpec((tm, c), lambda i: (i, 0))
    return pl.pallas_call(
        body, grid=(r // tm,),
        in_specs=[pl.BlockSpec((tm, 8), lambda i: (i, 0)), pl.BlockSpec((8, c), lambda i: (0, 0)), blk, blk, blk],
        out_specs=[blk] * 4, out_shape=[_sds((r, c), F32)] * 4,
        compiler_params=_params(("arbitrary",)), name=name,
    )(act_t, dmod_shard, w, m, v)


def _place():
    return lax.axis_index("x"), lax.axis_index("y"), lax.axis_index("c")


def _flip(v, bit):
    return 1 - v if bit else v


VMEM_SPEC = pl.BlockSpec(memory_space=pltpu.VMEM)


def _allgather8(v, name):
    r, c = v.shape

    def body(v_ref, g_ref, tot_ref, send_sems, recv_sems, local_sem):
        x, y, cc = _place()
        sibling = (x, y, 1 - cc)
        chips = [(_flip(x, k & 2), _flip(y, k & 1)) for k in (1, 2, 3)]

        def block(px, py, pc):
            return g_ref.at[4 * px + 2 * py + pc]

        def copy(k, place, to, src=None):
            slot = block(*place)
            return pltpu.make_async_remote_copy(src_ref=slot if src is None else src, dst_ref=slot,
                                                send_sem=send_sems.at[k], recv_sem=recv_sems.at[k],
                                                device_id=to, device_id_type=MESH)

        mine = pltpu.make_async_copy(v_ref, block(x, y, cc), local_sem)
        mine.start()
        first = [copy(0, (x, y, cc), sibling, src=v_ref)]
        first += [copy(1 + j, (x, y, cc), (px, py, cc), src=v_ref) for j, (px, py) in enumerate(chips)]
        for cp in first:
            cp.start()
        passed = [copy(4 + j, (px, py, cc), sibling) for j, (px, py) in enumerate(chips)]
        for j, (px, py) in enumerate(chips):
            copy(1 + j, (px, py, cc), (x, y, cc)).wait_recv()
            passed[j].start()
        copy(0, sibling, (x, y, cc)).wait_recv()
        for j, (px, py) in enumerate(chips):
            copy(4 + j, (px, py, 1 - cc), (x, y, cc)).wait_recv()
        for cp in first + passed:
            cp.wait_send()
        mine.wait()
        tot = g_ref[0]
        for d in range(1, 8):
            tot = tot + g_ref[d]
        tot_ref[...] = tot

    return pl.pallas_call(
        body, in_specs=[VMEM_SPEC], out_specs=[VMEM_SPEC, VMEM_SPEC],
        out_shape=[_sds((8, r, c), F32), _sds((r, c), F32)],
        scratch_shapes=[pltpu.SemaphoreType.DMA((7,)), pltpu.SemaphoreType.DMA((7,)), pltpu.SemaphoreType.DMA],
        compiler_params=pltpu.CompilerParams(vmem_limit_bytes=VMEM_LIMIT), name=name,
    )(v)


def _slot(px, py, swapped):
    return 2 * py + px if swapped else 2 * px + py


def _gather_shards(arrs, swapped, name):
    n = len(arrs)

    def body(*refs):
        ins, outs = refs[:n], refs[n:2 * n]
        send1, recv1, send2, recv2, local_sems = refs[2 * n:]
        x, y, c = _place()
        sibling = (x, y, 1 - c)
        chips = [(_flip(x, k & 2), _flip(y, k & 1)) for k in (1, 2, 3)]
        local_copies, sends = [], []
        for a in range(n):
            h = outs[a].shape[1] // 2
            mine = pl.ds(pl.multiple_of(c * h, 8), h)
            own = _slot(x, y, swapped[a])
            lc = pltpu.make_async_copy(ins[a], outs[a].at[own], local_sems.at[a])
            lc.start()
            local_copies.append(lc)
            for j, (px, py) in enumerate(chips):
                cp = pltpu.make_async_remote_copy(
                    src_ref=ins[a].at[mine], dst_ref=outs[a].at[own, mine], send_sem=send1.at[3 * a + j],
                    recv_sem=recv1.at[3 * a + j], device_id=(px, py, c), device_id_type=MESH)
                cp.start()
                sends.append(cp)
        for a in range(n):
            h = outs[a].shape[1] // 2
            mine = pl.ds(pl.multiple_of(c * h, 8), h)
            for j, (px, py) in enumerate(chips):
                piece = outs[a].at[_slot(px, py, swapped[a]), mine]
                pltpu.make_async_remote_copy(
                    src_ref=piece, dst_ref=piece, send_sem=send1.at[3 * a + j], recv_sem=recv1.at[3 * a + j],
                    device_id=(px, py, c), device_id_type=MESH).wait_recv()
                fwd = pltpu.make_async_remote_copy(
                    src_ref=piece, dst_ref=piece, send_sem=send2.at[3 * a + j], recv_sem=recv2.at[3 * a + j],
                    device_id=sibling, device_id_type=MESH)
                fwd.start()
                sends.append(fwd)
        for a in range(n):
            h = outs[a].shape[1] // 2
            other = pl.ds(pl.multiple_of((1 - c) * h, 8), h)
            for j, (px, py) in enumerate(chips):
                piece = outs[a].at[_slot(px, py, swapped[a]), other]
                pltpu.make_async_remote_copy(
                    src_ref=piece, dst_ref=piece, send_sem=send2.at[3 * a + j], recv_sem=recv2.at[3 * a + j],
                    device_id=sibling, device_id_type=MESH).wait_recv()
        for cp in sends:
            cp.wait_send()
        for lc in local_copies:
            lc.wait()

    dma = lambda k: pltpu.SemaphoreType.DMA((k,))
    return pl.pallas_call(
        body, in_specs=[ANY] * n, out_specs=[ANY] * n,
        out_shape=[_sds((4,) + a.shape, a.dtype) for a in arrs],
        scratch_shapes=[dma(3 * n), dma(3 * n), dma(3 * n), dma(3 * n), dma(n)], name=name,
    )(*arrs)


def _carry_pair_exchange(grads):
    n = len(grads)

    def copies(ins, outs, send_sems, recv_sems):
        x, y, c = _place()
        cps = []
        for a in range(n):
            h = ins[a].shape[1] // 2
            theirs = pl.ds(pl.multiple_of((1 - c) * h, 8), h)
            cps.append(pltpu.make_async_remote_copy(
                src_ref=ins[a].at[:, theirs, :], dst_ref=outs[a], send_sem=send_sems.at[a], recv_sem=recv_sems.at[a],
                device_id=(x, y, 1 - c), device_id_type=MESH))
        return cps

    def start(*refs):
        for cp in copies(*refs):
            cp.start()

    def finish(*refs):
        for cp in copies(*refs):
            cp.wait()

    return _Carried(grads, [_sds((4, g.shape[1] // 2, g.shape[2]), F32) for g in grads], {}, n, start, finish)


def _pair_sum(grad, recv, core, name):
    _, r, c = grad.shape
    h = r // 2

    def body(core_ref, g_ref, r_ref, o_ref):
        o_ref[...] = (g_ref[...] + r_ref[...]).astype(BF16)

    return pl.pallas_call(
        body,
        grid_spec=pltpu.PrefetchScalarGridSpec(
            num_scalar_prefetch=1, grid=(4,),
            in_specs=[pl.BlockSpec((None, h, c), lambda s, core_ref: (s, core_ref[0], 0)),
                      pl.BlockSpec((None, h, c), lambda s, core_ref: (s, 0, 0))],
            out_specs=pl.BlockSpec((None, h, c), lambda s, core_ref: (s, 0, 0))),
        out_shape=_sds((4, h, c), BF16), compiler_params=_params(("arbitrary",)), name=name,
    )(core, grad, recv)


def _carry_chip_exchange(parts, swapped):
    n = len(parts)

    def copies(ins, outs, send_sems, recv_sems):
        x, y, c = _place()
        chips = [(_flip(x, k & 2), _flip(y, k & 1)) for k in (1, 2, 3)]
        cps = []
        for a in range(n):
            for j, (px, py) in enumerate(chips):
                cps.append(pltpu.make_async_remote_copy(
                    src_ref=ins[a].at[_slot(px, py, swapped[a])], dst_ref=outs[a].at[j],
                    send_sem=send_sems.at[3 * a + j], recv_sem=recv_sems.at[3 * a + j],
                    device_id=(px, py, c), device_id_type=MESH))
        return cps

    def start(*refs):
        for cp in copies(*refs):
            cp.start()

    def finish(*refs):
        for cp in copies(*refs):
            cp.wait()

    return _Carried(parts, [_sds((3,) + p.shape[1:], BF16) for p in parts], {}, 3 * n, start, finish)


def _chip_sum(part, recv, slot_core, name):
    _, h, c = part.shape

    def body(sc_ref, p_ref, r_ref, o_ref):
        acc = p_ref[...].astype(F32)
        for j in range(3):
            acc = acc + r_ref[j].astype(F32)
        o_ref[...] = acc

    return pl.pallas_call(
        body,
        grid_spec=pltpu.PrefetchScalarGridSpec(
            num_scalar_prefetch=1, grid=(1,),
            in_specs=[pl.BlockSpec((None, h, c), lambda q, sc_ref: (sc_ref[0], 0, 0)),
                      pl.BlockSpec((3, h, c), lambda q, sc_ref: (0, 0, 0))],
            out_specs=pl.BlockSpec((h, c), lambda q, sc_ref: (sc_ref[1], 0))),
        out_shape=_sds((2 * h, c), F32), compiler_params=_params(("arbitrary",)), name=name,
    )(slot_core, part, recv)


def _carry_pair_share(shards):
    n = len(shards)

    def copies(outs, send_sems, recv_sems, mine):
        x, y, c = _place()
        cps = []
        for a in range(n):
            h = outs[a].shape[0] // 2
            half = outs[a].at[pl.ds(pl.multiple_of((c if mine else 1 - c) * h, 8), h)]
            cps.append(pltpu.make_async_remote_copy(
                src_ref=half, dst_ref=half, send_sem=send_sems.at[a], recv_sem=recv_sems.at[a],
                device_id=(x, y, 1 - c), device_id_type=MESH))
        return cps

    def start(ins, outs, send_sems, recv_sems):
        for cp in copies(outs, send_sems, recv_sems, True):
            cp.start()

    def finish(ins, outs, send_sems, recv_sems):
        for cp in copies(outs, send_sems, recv_sems, False):
            cp.wait_recv()
        for cp in copies(outs, send_sems, recv_sems, True):
            cp.wait_send()

    return _Carried(shards, [_sds(s.shape, F32) for s in shards], {a: a for a in range(n)}, n, start, finish)


def _carry_gather_ici(bufs, swapped):
    n = len(bufs)

    def copies(outs, send_sems, recv_sems, sending):
        x, y, c = _place()
        cps = []
        for a in range(n):
            h = outs[a].shape[1] // 2
            mine = pl.ds(pl.multiple_of(c * h, 8), h)
            for j, k in enumerate((1, 2, 3)):
                px, py = _flip(x, k & 2), _flip(y, k & 1)
                slot = _slot(x, y, swapped[a]) if sending else _slot(px, py, swapped[a])
                piece = outs[a].at[slot, mine]
                cps.append(pltpu.make_async_remote_copy(
                    src_ref=piece, dst_ref=piece, send_sem=send_sems.at[3 * a + j], recv_sem=recv_sems.at[3 * a + j],
                    device_id=(px, py, c), device_id_type=MESH))
        return cps

    def start(ins, outs, send_sems, recv_sems):
        for cp in copies(outs, send_sems, recv_sems, True):
            cp.start()

    def finish(ins, outs, send_sems, recv_sems):
        for cp in copies(outs, send_sems, recv_sems, False):
            cp.wait_recv()
        for cp in copies(outs, send_sems, recv_sems, True):
            cp.wait_send()

    return _Carried(bufs, [_sds(b.shape, b.dtype) for b in bufs], {a: a for a in range(n)}, 3 * n, start, finish)


HBM_SPEC = pl.BlockSpec(memory_space=pltpu.HBM)
SEM_SPEC = pl.BlockSpec(memory_space=pltpu.SEMAPHORE)
SIDE_EFFECT = pltpu.SideEffectType.DATAFLOW_SIDE_EFFECTING


def _ici_pieces(buf, send_sems, recv_sems, swapped, sending):
    x, y, c = _place()
    h = buf.shape[1] // 2
    mine = pl.ds(pl.multiple_of(c * h, 8), h)
    cps = []
    for j, k in enumerate((1, 2, 3)):
        px, py = _flip(x, k & 2), _flip(y, k & 1)
        piece = buf.at[_slot(x, y, swapped) if sending else _slot(px, py, swapped), mine]
        cps.append(pltpu.make_async_remote_copy(src_ref=piece, dst_ref=piece, send_sem=send_sems.at[j],
                                                recv_sem=recv_sems.at[j], device_id=(px, py, c), device_id_type=MESH))
    return cps


def _gather_ici_start(buf, after, swapped, name):
    def body(buf_ref, after_ref, send_sems, recv_sems, thru, token):
        for cp in _ici_pieces(thru, send_sems, recv_sems, swapped, True):
            cp.start()
        token[...] = jnp.zeros_like(token)

    return pl.pallas_call(
        body, name=name,
        out_shape=(pltpu.SemaphoreType.DMA((3,)), pltpu.SemaphoreType.DMA((3,)), pltpu.HBM(buf.shape, buf.dtype),
                   jax.ShapeDtypeStruct((8, 128), F32)),
        in_specs=(HBM_SPEC, ANY), out_specs=(SEM_SPEC, SEM_SPEC, HBM_SPEC, VMEM_SPEC), input_output_aliases={0: 2},
        compiler_params=pltpu.CompilerParams(has_side_effects=SIDE_EFFECT),
    )(pltpu.with_memory_space_constraint(buf, pltpu.HBM), after)


def _gather_ici_wait(send_sems, recv_sems, thru, after, swapped, name):
    def body(thru_ref, send_sems, recv_sems, after_ref, out_ref):
        for cp in _ici_pieces(out_ref, send_sems, recv_sems, swapped, True):
            cp.wait_send()
        for cp in _ici_pieces(out_ref, send_sems, recv_sems, swapped, False):
            cp.wait_recv()

    return pl.pallas_call(
        body, name=name, out_shape=pltpu.HBM(thru.shape, thru.dtype),
        in_specs=(HBM_SPEC, SEM_SPEC, SEM_SPEC, ANY), out_specs=HBM_SPEC, input_output_aliases={0: 0},
        compiler_params=pltpu.CompilerParams(has_side_effects=SIDE_EFFECT),
    )(thru, send_sems, recv_sems, after)


def _all8_copies(buf, send_sems, recv_sems, sending):
    x, y, c = _place()
    cps = []
    for k in range(1, 8):
        px, py, pc = _flip(x, k & 4), _flip(y, k & 2), _flip(c, k & 1)
        slot = buf.at[4 * x + 2 * y + c] if sending else buf.at[4 * px + 2 * py + pc]
        cps.append(pltpu.make_async_remote_copy(src_ref=slot, dst_ref=slot, send_sem=send_sems.at[k - 1],
                                                recv_sem=recv_sems.at[k - 1], device_id=(px, py, pc), device_id_type=MESH))
    return cps


def _all8_start(buf, name):
    def body(buf_ref, send_sems, recv_sems, thru, token):
        for cp in _all8_copies(thru, send_sems, recv_sems, True):
            cp.start()
        token[...] = jnp.zeros_like(token)

    return pl.pallas_call(
        body, name=name,
        out_shape=(pltpu.SemaphoreType.DMA((7,)), pltpu.SemaphoreType.DMA((7,)), pltpu.HBM(buf.shape, buf.dtype),
                   jax.ShapeDtypeStruct((8, 128), F32)),
        in_specs=(HBM_SPEC,), out_specs=(SEM_SPEC, SEM_SPEC, HBM_SPEC, VMEM_SPEC), input_output_aliases={0: 2},
        compiler_params=pltpu.CompilerParams(has_side_effects=SIDE_EFFECT),
    )(pltpu.with_memory_space_constraint(buf, pltpu.HBM))


def _all8_wait(send_sems, recv_sems, thru, after, name):
    def body(thru_ref, send_sems, recv_sems, after_ref, out_ref):
        for cp in _all8_copies(out_ref, send_sems, recv_sems, True):
            cp.wait_send()
        for cp in _all8_copies(out_ref, send_sems, recv_sems, False):
            cp.wait_recv()

    return pl.pallas_call(
        body, name=name, out_shape=pltpu.HBM(thru.shape, thru.dtype),
        in_specs=(HBM_SPEC, SEM_SPEC, SEM_SPEC, ANY), out_specs=HBM_SPEC, input_output_aliases={0: 0},
        compiler_params=pltpu.CompilerParams(has_side_effects=SIDE_EFFECT),
    )(thru, send_sems, recv_sems, after)


def _sum8(g, name):
    def body(g_ref, o_ref):
        tot = g_ref[0]
        for d in range(1, 8):
            tot = tot + g_ref[d]
        o_ref[...] = tot

    return pl.pallas_call(body, out_shape=_sds(g.shape[1:], F32), name=name)(g)


def _carry_gather_forward(bufs, swapped):
    n = len(bufs)

    def copies(outs, send_sems, recv_sems, sending):
        x, y, c = _place()
        cps = []
        for a in range(n):
            h = outs[a].shape[1] // 2
            rows = pl.ds(pl.multiple_of((c if sending else 1 - c) * h, 8), h)
            for j, k in enumerate((1, 2, 3)):
                piece = outs[a].at[_slot(_flip(x, k & 2), _flip(y, k & 1), swapped[a]), rows]
                cps.append(pltpu.make_async_remote_copy(
                    src_ref=piece, dst_ref=piece, send_sem=send_sems.at[3 * a + j], recv_sem=recv_sems.at[3 * a + j],
                    device_id=(x, y, 1 - c), device_id_type=MESH))
        return cps

    def start(ins, outs, send_sems, recv_sems):
        for cp in copies(outs, send_sems, recv_sems, True):
            cp.start()

    def finish(ins, outs, send_sems, recv_sems):
        for cp in copies(outs, send_sems, recv_sems, False):
            cp.wait_recv()
        for cp in copies(outs, send_sems, recv_sems, True):
            cp.wait_send()

    return _Carried(bufs, [_sds(b.shape, b.dtype) for b in bufs], {a: a for a in range(n)}, 3 * n, start, finish)


def _pack(arrs, rows_multiple=8):
    parts, offs, row = [], [], 0
    for a in arrs:
        flat = a.reshape(-1)
        nrow = -(-flat.shape[0] // D)
        parts.append(jnp.pad(flat, (0, nrow * D - flat.shape[0])))
        offs.append(row)
        row += nrow
    total = -(-row // rows_multiple) * rows_multiple
    if total > row:
        parts.append(jnp.zeros(((total - row) * D,), F32))
    return jnp.concatenate(parts).reshape(total, D), offs


def _unpack(packed, offs, shapes):
    out = []
    for off, shp in zip(offs, shapes):
        size = int(np.prod(shp))
        nrow = -(-size // D)
        out.append(packed[off:off + nrow].reshape(-1)[:size].reshape(shp))
    return out


def _to_bf16_slot(w, slot, name, after=None):
    r, c = w.shape
    tm = next(cand for cand in (256, 176, 128, 64, 32, 16) if r % cand == 0)

    def body(slot_ref, w_ref, *rest):
        rest[-1][...] = w_ref[...].astype(BF16)

    in_specs = [pl.BlockSpec((tm, c), lambda i, slot_ref: (i, 0))]
    if after is not None:
        in_specs.append(pl.BlockSpec((8, 128), lambda i, slot_ref: (0, 0)))
    return pl.pallas_call(
        body,
        grid_spec=pltpu.PrefetchScalarGridSpec(
            num_scalar_prefetch=1, grid=(r // tm,), in_specs=in_specs,
            out_specs=pl.BlockSpec((None, tm, c), lambda i, slot_ref: (slot_ref[0], i, 0))),
        out_shape=_sds((4, r, c), BF16), compiler_params=_params(("arbitrary",)), name=name,
    )(slot, w, *([] if after is None else [after]))


def _unshard_cols(g):
    s, k, n = g.shape
    return jnp.transpose(g, (1, 0, 2)).reshape(k, s * n)


def _ff_swap(v):
    b = FF_BLOCK
    return jnp.concatenate([v[..., 0:b], v[..., 2 * b:3 * b], v[..., b:2 * b], v[..., 3 * b:4 * b]], axis=-1)


LATE = ("attn_o", "conv_o", "mix_o", "up", "down")
EARLY_GRADS = ("down", "up", "mix_o", "attn_o", "conv_o")


def _weight_views(bufs):
    return {"up": bufs["up"], "attn_o": _unshard_cols(bufs["attn_o"]), "conv_o": _unshard_cols(bufs["conv_o"]),
            "mix_o": bufs["mix_o"].reshape(D, D), "down": bufs["down"].reshape(D_FF, D)}


def _pair_sums(names, grads, recv, dist):
    return [_pair_sum(g, r, dist["core"], "pair_sum_" + n) for n, g, r in zip(names, grads, recv)]


def _reduce_halves(names, parts, from_chips, dist):
    return [_chip_sum(p, r, jnp.concatenate([dist["slots"][SWAPPED[n]], dist["core"]]), "chip_sum_" + n)
            for n, p, r in zip(names, parts, from_chips)]


FUSED_TILE = 256
WIDE_TILE = 512


def _gates(z):
    return [(z, 512, 5), (z, 512, 6), (z, 512, 7), (z, 512, 8)]


def _mix_out(a, cb, z, x, w_mix_o, g_post, gt, g_pre2, sc2, sh2, name):
    def lhs(av, cv, ga0, ga1, gb0, gb1):
        ga, gb = jnp.concatenate([ga0, ga1], axis=1), jnp.concatenate([gb0, gb1], axis=1)
        return _sigmoid(ga) * av + _sigmoid(gb) * cv

    def fn(ym, y, xv, gv, gtv, g2v, scv, shv):
        yn, _ = _rms_stats(ym)
        x1 = xv + gtv * (yn * gv)
        xn, _ = _rms_stats(x1)
        return ym, y, x1, (xn * g2v) * (1.0 + scv) + shv

    return _matmul_rows(w_mix_o, form="nn", tm=min(WIDE_TILE, x.shape[0]), tk=D, fn=fn, a_rows=[(a, D, 0), (cb, D, 0)] + _gates(z),
                        a_fn=lhs, rows=[(x, D, 0)], consts=[g_post, gt, g_pre2, sc2, sh2],
                        row_outs=[(F32, D), (BF16, D), (F32, D), (BF16, D)], acc_outs=[], name=name)


def _down_tail(act, w_down, x1, target, g, gt, name):
    def fn(yv, xv, tv, gv, gtv):
        yn, r = _rms_stats(yv)
        e = xv + gtv * (yn * gv) - tv
        dx2 = e * (1.0 / D)
        dyn = dx2 * gtv
        return (dx2, _rms_bwd(dyn * gv, yn, r), _colsum(e * e) * (0.5 / D), _colsum(dyn * yn),
                _colsum(dx2 * (yn * gv)))

    return _matmul_rows(w_down, form="nn", a=act, tm=min(WIDE_TILE, x1.shape[0]), tk=D_FF, fn=fn,
                        rows=[(x1, D, 0), (target, D, 0)], consts=[g, gt], row_outs=[(F32, D), (BF16, D)],
                        acc_outs=[(1, D)] * 3, name=name)


def _up_dx_tail(dup, w_up, x1, dx2, ym, g_pre2, sc2, g_post, gt, name):
    def fn(dh, xv, dov, ymv, g2v, scv, gv, gtv):
        xn, r = _rms_stats(xv)
        dyn = dh * (1.0 + scv)
        dx1 = dov + _rms_bwd(dyn * g2v, xn, r)
        yn, r2 = _rms_stats(ymv)
        dynm = dx1 * gtv
        return (dx1, _rms_bwd(dynm * gv, yn, r2), _colsum(dh), _colsum(dh * (xn * g2v)), _colsum(dyn * xn),
                _colsum(dynm * yn), _colsum(dx1 * (yn * gv)))

    return _matmul_rows(w_up, form="nt", a=dup, tm=min(FUSED_TILE, x1.shape[0]), tk=2 * D_FF, fn=fn,
                        rows=[(x1, D, 0), (dx2, D, 0), (ym, D, 0)], consts=[g_pre2, sc2, g_post, gt],
                        row_outs=[(F32, D), (BF16, D)], acc_outs=[(1, D)] * 5, name=name)


def _mix_dx_gates(dym, w_mix_o, a, cb, z, name):
    def fn(dy, av, cv, ga0, ga1, gb0, gb1):
        sa = _sigmoid(jnp.concatenate([ga0, ga1], axis=1))
        sb = _sigmoid(jnp.concatenate([gb0, gb1], axis=1))
        dcb = dy * sb
        dga = dy * av * (sa * (1.0 - sa))
        dgb = dy * cv * (sb * (1.0 - sb))
        return dy * sa, dcb, dga, dgb, _colsum(dcb), _colsum(dga), _colsum(dgb)

    return _matmul_rows(w_mix_o, form="nt", a=dym, tm=min(WIDE_TILE, a.shape[0]), tk=D, fn=fn,
                        rows=[(a, D, 0), (cb, D, 0)] + _gates(z), consts=[], row_outs=[(BF16, D)] * 4,
                        acc_outs=[(1, D)] * 3, name=name)


def _local_step(x, target, mod, w_in, late, small, dist=None):
    sh_m, sc_m, gt_m, sh_f, sc_f, gt_f = mod
    t = x.shape[0]
    tmm = min(1024, t)
    late_swapped = [SWAPPED[n] for n in LATE]

    h1 = _pre_norm(x, small["g_pre_mix"], sc_m, sh_m, "pre_norm_mix")
    if callable(w_in):
        w_in = w_in(h1)
    z = _matmul(h1, w_in, form="nn", out_dtype=F32, tm=min(FUSED_TILE, t), tn=D_IN, tk=D, bias=small["b_in"], name="mm_in")
    conv = (z, small["w_dw_conv"], small["b_dw_conv"], small["g_conv_ln"], small["b_conv_ln"], "conv_branch")
    if dist is None:
        att = _attention(z, small["gen"], "attention")
        u1, u3 = _conv_branch(*conv)
        bufs = dict(late)
    else:
        mid = [n for n in LATE if n != "down"]
        mid_swapped = [SWAPPED[n] for n in mid]
        att, landed = _attention(z, small["gen"], "attention",
                                 carried=_carry_gather_ici([late[n] for n in mid], mid_swapped))
        (u1, u3), gathered = _conv_branch(*conv, carried=_carry_gather_forward(landed, mid_swapped))
        bufs = dict(zip(mid, gathered))
        bufs["down"] = late["down"]
    w = _weight_views(bufs)
    w["in"] = w_in
    a = _matmul(att, w["attn_o"], form="nn", out_dtype=F32, tm=tmm, tn=512, tk=512, name="mm_attn_o")
    cb = _matmul(u3, w["conv_o"], form="nn", out_dtype=F32, tm=tmm, tn=512, tk=512, bias=small["b_conv_o"], name="mm_conv_o")
    ym, y, x1, h2 = _mix_out(a, cb, z, x, w["mix_o"], small["g_post_mix"], gt_m, small["g_pre_ffn"], sc_f, sh_f, "mix_out")
    mm_up = dict(form="nn", out_dtype=F32, tm=min(FUSED_TILE, t), tn=2 * D_FF, tk=D, name="mm_up")
    ffn_act = (small["w_dw_ffn"], small["b_dw_ffn"], "ffn_act")
    if dist is None:
        up = _matmul(h2, w["up"], **mm_up)
        act = _ffn_act(up, *ffn_act)
    else:
        up, landed = _matmul(h2, w["up"], carried=_carry_gather_ici([late["down"]], [False]), **mm_up)
        act, down = _ffn_act(up, *ffn_act, carried=_carry_gather_forward(landed, [False]))
        w["down"] = down[0].reshape(D_FF, D)

    dx2, dyf, loss_cols, d_g_post_ffn, d_gt_f = _down_tail(act, w["down"], x1, target, small["g_post_ffn"], gt_f, "down_tail")
    dact = _matmul(dyf, w["down"], form="nt", out_dtype=F32, tm=tmm, tn=FF_BLOCK, tk=D, name="mm_down_dx")
    g_down = _matmul(act, dyf, form="tn", out_dtype=F32, tm=FF_BLOCK, tn=512, tk=t, name="mm_down_dw")
    dup, d_w_dw_ffn, d_b_dw_ffn = _ffn_act_bwd(dact, up, small["w_dw_ffn"], small["b_dw_ffn"], "ffn_act_bwd")
    dx1, dym, d_sh_f, d_sc_f, d_g_pre_ffn, d_g_post_mix, d_gt_m = _up_dx_tail(
        dup, w["up"], x1, dx2, ym, small["g_pre_ffn"], sc_f, small["g_post_mix"], gt_m, "up_dx_tail")
    g_up = _matmul(h2, dup, form="tn", out_dtype=F32, tm=512, tn=FF_BLOCK, tk=t, out_sharded=True, name="mm_up_dw")
    da, dcb, dgate_a, dgate_b, d_b_conv_o, sga, sgb = _mix_dx_gates(dym, w["mix_o"], a, cb, z, "mix_dx_gates")
    g_mix_o = _matmul(y, dym, form="tn", out_dtype=F32, tm=D, tn=512, tk=t, name="mm_mix_o_dw")
    datt = _matmul(da, w["attn_o"], form="nt", out_dtype=F32, tm=tmm, tn=512, tk=D, name="mm_attn_o_dx")
    g_attn_o = _matmul(att, da, form="tn", out_dtype=F32, tm=512, tn=256, tk=t, out_sharded=True, name="mm_attn_o_dw")
    du3 = _matmul(dcb, w["conv_o"], form="nt", out_dtype=F32, tm=tmm, tn=512, tk=D, name="mm_conv_o_dx")
    g_conv_o = _matmul(u3, dcb, form="tn", out_dtype=F32, tm=512, tn=256, tk=t, out_sharded=True, name="mm_conv_o_dw")
    big = {"attn_o": g_attn_o, "conv_o": g_conv_o, "mix_o": g_mix_o.reshape(4, 256, D),
           "up": g_up, "down": g_down.reshape(4, D_FF // 4, D)}
    conv_bwd = (du3, u1, z, small["w_dw_conv"], small["g_conv_ln"], small["b_conv_ln"], "conv_branch_bwd")
    in_dw = dict(form="tn", out_dtype=F32, tm=512, tn=IN_SHARD, tk=t, out_sharded=True, name="mm_in_dw")
    in_dx = dict(form="nt", out_dtype=F32, tm=min(WIDE_TILE, t), tn=D, tk=D_IN, name="mm_in_dx")
    if dist is None:
        dglu, d_w_dw_conv, d_b_dw_conv, d_g_conv_ln, d_b_conv_ln, sglu = _conv_branch_bwd(*conv_bwd)
        dq, dk, dv, sq, sk, sv, dgen = _attention_bwd(z, datt, small["gen"], "attention_bwd")
        dz = jnp.concatenate([dq, dk, dv, dglu, dgate_a, dgate_b], axis=1)
        big["in"] = _matmul(h1, dz, **in_dw)
        dh1 = _matmul(dz, w_in, **in_dx)
    else:
        early = [big[n] for n in EARLY_GRADS]
        (dglu, d_w_dw_conv, d_b_dw_conv, d_g_conv_ln, d_b_conv_ln, sglu), recv = _conv_branch_bwd(
            *conv_bwd, carried=_carry_pair_exchange(early))
        parts = _pair_sums(EARLY_GRADS, early, recv, dist)
        (dq, dk, dv, sq, sk, sv, dgen), from_chips = _attention_bwd(
            z, datt, small["gen"], "attention_bwd",
            carried=_carry_chip_exchange(parts, [SWAPPED[n] for n in EARLY_GRADS]))
        halves = _reduce_halves(EARLY_GRADS, parts, from_chips, dist)
        dz = jnp.concatenate([dq, dk, dv, dglu, dgate_a, dgate_b], axis=1)
        g_in, shards = _matmul(h1, dz, carried=_carry_pair_share(halves), **in_dw)
        big = dict(zip(EARLY_GRADS, shards))
        recv_in = _run_carried(_carry_pair_exchange([g_in]), "pair_exchange_in")
        part_in = _pair_sums(("in",), [g_in], recv_in, dist)
        dh1, from_chips_in = _matmul(dz, w_in, carried=_carry_chip_exchange(part_in, [False]), **in_dx)
        half_in = _reduce_halves(("in",), part_in, from_chips_in, dist)
        big["in"] = _run_carried(_carry_pair_share(half_in), "pair_share_in")[0]
    d_b_in = jnp.concatenate([sq, sk, sv, sglu, sga, sgb], axis=1)
    grad_x, d_sh_m, d_sc_m, d_g_pre_mix = _pre_norm_bwd(dh1, x, dx1, small["g_pre_mix"], sc_m, "pre_norm_mix_bwd")

    dmod = [d_sh_m, d_sc_m, d_gt_m, d_sh_f, d_sc_f, d_gt_f]
    sm = {"g_pre_mix": d_g_pre_mix, "g_post_mix": d_g_post_mix, "b_in": d_b_in, "gen": dgen,
          "w_dw_conv": d_w_dw_conv, "b_dw_conv": d_b_dw_conv, "g_conv_ln": d_g_conv_ln, "b_conv_ln": d_b_conv_ln,
          "b_conv_o": d_b_conv_o, "g_pre_ffn": d_g_pre_ffn, "g_post_ffn": d_g_post_ffn,
          "w_dw_ffn": d_w_dw_ffn, "b_dw_ffn": d_b_dw_ffn}
    return loss_cols, grad_x, dmod, big, sm


BIG = ("in", "attn_o", "conv_o", "mix_o", "up", "down")
SWAPPED = {"in": False, "attn_o": False, "conv_o": False, "mix_o": False, "up": True, "down": False}
SMALL_ORDER = ("b_ada", "g_pre_mix", "g_post_mix", "b_in", "rel_bias", "b_dw_conv", "g_conv_ln", "b_conv_ln",
               "b_conv_o", "g_pre_ffn", "g_post_ffn", "b_dw_ffn", "w_dw_conv", "w_dw_ffn")


def kernel(x, c, w_ada, b_ada, g_pre_mix, g_post_mix, w_in, b_in, rel_bias, w_attn_o, w_dw_conv, b_dw_conv, g_conv_ln, b_conv_ln, w_conv_o, b_conv_o, w_mix_o, g_pre_ffn, g_post_ffn, w_up, w_dw_ffn, b_dw_ffn, w_down, loss_target, m_w_ada, m_b_ada, m_g_pre_mix, m_g_post_mix, m_w_in, m_b_in, m_rel_bias, m_w_attn_o, m_w_dw_conv, m_b_dw_conv, m_g_conv_ln, m_b_conv_ln, m_w_conv_o, m_b_conv_o, m_w_mix_o, m_g_pre_ffn, m_g_post_ffn, m_w_up, m_w_dw_ffn, m_b_dw_ffn, m_w_down, v_w_ada, v_b_ada, v_g_pre_mix, v_g_post_mix, v_w_in, v_b_in, v_rel_bias, v_w_attn_o, v_w_dw_conv, v_b_dw_conv, v_g_conv_ln, v_b_conv_ln, v_w_conv_o, v_b_conv_o, v_w_mix_o, v_g_pre_ffn, v_g_post_ffn, v_w_up, v_w_dw_ffn, v_b_dw_ffn, v_w_down):
    given = dict(locals())
    ax, ay, ac = lax.axis_index("x"), lax.axis_index("y"), lax.axis_index("c")
    shard = 2 * ax + ay
    me = 4 * ax + 2 * ay + ac
    xs, target = x[0], loss_target[0]

    slots = {sw: _slot(ax, ay, sw).astype(jnp.int32).reshape(1) for sw in (False, True)}
    own = {"in": _to_bf16_slot(w_in[0], slots[False], "cast_in")}

    c_pad = jnp.pad(c, ((0, 7), (0, 0)))
    c_g, _ = _allgather8(c_pad, "gather_c")
    c_all = c_g[:, 0, :]
    b_ada_shard = lax.dynamic_slice(b_ada, (0, shard * ADA_SHARD), (1, ADA_SHARD))
    mod_shard, c_act = _ada_fwd(c_all, w_ada[0], b_ada_shard, "ada_fwd")
    small_in = [jnp.pad(mod_shard, ((0, 8), (0, 0))),
                jnp.pad(w_dw_conv[0], ((0, 1), (0, 0))),
                jnp.pad(w_dw_ffn[0], ((0, 13), (0, 0)))]
    mod_g, wdc_g, wdf_g = _gather_shards(small_in, [False, False, True], "gather_small")
    mod_all = jnp.transpose(mod_g[:, :8, :], (1, 0, 2)).reshape(8, 6 * D)
    in_send, in_recv, in_flight, token = _gather_ici_start(own["in"], mod_g, False, "gather_w_in_start")

    def w_in_ready(after):
        landed = _gather_ici_wait(in_send, in_recv, in_flight, after, False, "gather_w_in_wait")
        return _run_carried(_carry_gather_forward([landed], [False]), "gather_forward_in")[0]

    for n in LATE:
        own[n] = _to_bf16_slot(given["w_" + n][0], slots[SWAPPED[n]], "cast_" + n, after=token)
    mod_row = lax.dynamic_slice(mod_all, (me, 0), (1, 6 * D)) + token[0:1, 0:1]
    mod = [mod_row[:, k * D:(k + 1) * D] for k in range(6)]

    core = ac.astype(jnp.int32).reshape(1)
    dist = {"core": core, "slots": slots}

    sel = jnp.asarray(_toeplitz_map())
    rel_pad = jnp.pad(rel_bias[0], ((0, 0), (0, REL_PAD - (2 * MAX_REL + 1))))
    gen = _select_call(rel_pad, sel.T.astype(BF16), "bias_rows")
    small = {"g_pre_mix": g_pre_mix, "g_post_mix": g_post_mix, "b_in": b_in, "gen": gen,
             "w_dw_conv": _unshard_cols(wdc_g[:, :CONV_K, :]), "b_dw_conv": b_dw_conv, "g_conv_ln": g_conv_ln,
             "b_conv_ln": b_conv_ln, "b_conv_o": b_conv_o, "g_pre_ffn": g_pre_ffn, "g_post_ffn": g_post_ffn,
             "w_dw_ffn": _unshard_cols(wdf_g[:, :FFN_K, :]), "b_dw_ffn": _ff_swap(b_dw_ffn)}

    loss_cols, grad_x, dmod, reduced, sm = _local_step(xs, target, mod, w_in_ready, {n: own[n] for n in LATE}, small, dist)

    d_rel = _select_call(sm["gen"], sel.astype(BF16), "bias_fold")[:, :2 * MAX_REL + 1]
    small_grads = {"g_pre_mix": sm["g_pre_mix"], "g_post_mix": sm["g_post_mix"], "b_in": sm["b_in"], "rel_bias": d_rel[None],
                   "b_dw_conv": sm["b_dw_conv"], "g_conv_ln": sm["g_conv_ln"], "b_conv_ln": sm["b_conv_ln"],
                   "b_conv_o": sm["b_conv_o"], "g_pre_ffn": sm["g_pre_ffn"], "g_post_ffn": sm["g_post_ffn"],
                   "b_dw_ffn": _ff_swap(sm["b_dw_ffn"]), "w_dw_conv": sm["w_dw_conv"], "w_dw_ffn": _ff_swap(sm["w_dw_ffn"])}
    order = [n for n in SMALL_ORDER if n != "b_ada"]
    packed, offs = _pack([jnp.concatenate(dmod, axis=1)] + [small_grads[n] for n in order] + [loss_cols])
    mine = lax.dynamic_update_slice(jnp.zeros((8,) + packed.shape, F32), packed[None], (me, 0, 0))
    sg_send, sg_recv, sg_flight, sg_token = _all8_start(mine, "gather_small_grads_start")

    out = {}
    for n in BIG:
        g, dl, nm, nv = _adamw(given["w_" + n][0], reduced[n], given["m_w_" + n][0], given["v_w_" + n][0],
                               "adamw_" + n, sg_token)
        out["grad_w_" + n], out["delta_w_" + n], out["new_m_w_" + n], out["new_v_w_" + n] = g[None], dl[None], nm[None], nv[None]
    every = _all8_wait(sg_send, sg_recv, sg_flight, out["delta_w_in"], "gather_small_grads_wait")
    total = _sum8(every, "sum_small_grads")
    loss = jnp.sum(total[offs[-1]])
    offs = offs[:-1]
    dmod_all = every[:, 0:6, :].reshape(8, 6 * D)
    full_shapes = {n: given[n].shape for n in order}
    full_shapes["w_dw_conv"], full_shapes["w_dw_ffn"] = (1, CONV_K, 512), (1, FFN_K, 2 * D_FF)
    sums = dict(zip(order, _unpack(total, offs[1:], [full_shapes[n] for n in order])))
    sums["b_ada"] = total[0:6].reshape(1, 6 * D)
    sums["w_dw_conv"] = lax.dynamic_slice(sums["w_dw_conv"], (0, 0, shard * 128), (1, CONV_K, 128))
    sums["w_dw_ffn"] = lax.dynamic_slice(sums["w_dw_ffn"], (0, 0, shard * FF_BLOCK), (1, FFN_K, FF_BLOCK))

    upd = dict(zip(SMALL_ORDER, _adamw_many(
        [(given[n], sums[n], given["m_" + n], given["v_" + n]) for n in SMALL_ORDER], "adamw_small")))

    dmod_shard = lax.dynamic_slice(dmod_all, (0, shard * ADA_SHARD), (8, ADA_SHARD))
    ada = _ada_bwd_adamw(c_act.T, dmod_shard, w_ada[0], m_w_ada[0], v_w_ada[0], "ada_bwd_adamw")

    out.update({"grad_w_ada": ada[0][None], "delta_w_ada": ada[1][None], "new_m_w_ada": ada[2][None],
                "new_v_w_ada": ada[3][None]})
    for n in SMALL_ORDER:
        out["grad_" + n], out["delta_" + n], out["new_m_" + n], out["new_v_" + n] = sums[n], *upd[n]

    weights = ["w_ada", "b_ada", "g_pre_mix", "g_post_mix", "w_in", "b_in", "rel_bias", "w_attn_o", "w_dw_conv", "b_dw_conv",
               "g_conv_ln", "b_conv_ln", "w_conv_o", "b_conv_o", "w_mix_o", "g_pre_ffn", "g_post_ffn", "w_up", "w_dw_ffn",
               "b_dw_ffn", "w_down"]
    return (loss, grad_x[None], *[out["grad_" + n] for n in weights], *[out["delta_" + n] for n in weights],
            *[out["new_m_" + n] for n in weights], *[out["new_v_" + n] for n in weights])
```

```python
import functools
import math

import numpy as np
import jax
import jax.numpy as jnp
from jax import lax
from jax.experimental import pallas as pl
from jax.experimental.pallas import tpu as pltpu

F32, BF16 = jnp.float32, jnp.bfloat16
MESH = pl.DeviceIdType.MESH

D = 1024
D_IN = 4608
D_FF = 2816
N_CHIPS = 4
IN_SHARD = D_IN // N_CHIPS
ADA_SHARD = 6 * D // N_CHIPS
CONV_K = 31
FFN_K = 3
N_HEADS = 8
CHUNK = 64
LEFT_CHUNKS = 8
MAX_REL = 128
EPS = 1e-6
NEG_INF = -1e30
Q_TILE = 256
WINDOW = Q_TILE + LEFT_CHUNKS * CHUNK
STEP_ROWS = 256
REL_PAD = 384
TOEP = 1024
ROW_TILE = 256
VMEM_LIMIT = 60 * 1024 * 1024

ADAM_LR, ADAM_B1, ADAM_B2, ADAM_EPS, ADAM_WD, ADAM_STEP = 0.001, 0.9, 0.999, 1e-08, 0.01, 10


def _params(sem=None):
    return pltpu.CompilerParams(dimension_semantics=sem, vmem_limit_bytes=VMEM_LIMIT)


def _sds(shape, dtype):
    return jax.ShapeDtypeStruct(tuple(shape), dtype)


ANY = pl.BlockSpec(memory_space=pl.ANY)


class _Carried:
    def __init__(self, ins, out_shapes, aliases, n_sems, start, finish):
        self.ins, self.out_shapes, self.aliases = list(ins), list(out_shapes), dict(aliases)
        self.n_sems, self.start, self.finish = n_sems, start, finish


def _call(body, *, grid, in_specs, out_specs, out_shape, scratch_shapes, sem, name, args, carried=None):
    in_specs, out_specs, out_shape = list(in_specs), list(out_specs), list(out_shape)
    scratch_shapes = list(scratch_shapes)
    if carried is None:
        return pl.pallas_call(body, grid=grid, in_specs=in_specs, out_specs=out_specs, out_shape=out_shape,
                              scratch_shapes=scratch_shapes, compiler_params=_params(sem), name=name)(*args)
    n_in, n_out, n_scr = len(in_specs), len(out_specs), len(scratch_shapes)
    c_in, c_out = len(carried.ins), len(carried.out_shapes)

    def full(*refs):
        pos = [0]

        def take(k):
            part = refs[pos[0]:pos[0] + k]
            pos[0] += k
            return part

        ins, cins, outs, couts, scr = take(n_in), take(c_in), take(n_out), take(c_out), take(n_scr)
        send_sems, recv_sems = take(2)
        first = last = None
        for d, size in enumerate(grid):
            pid = pl.program_id(d)
            first = (pid == 0) if first is None else first & (pid == 0)
            last = (pid == size - 1) if last is None else last & (pid == size - 1)

        @pl.when(first)
        def _():
            carried.start(cins, couts, send_sems, recv_sems)

        body(*ins, *outs, *scr)

        @pl.when(last)
        def _():
            carried.finish(cins, couts, send_sems, recv_sems)

    sems = [pltpu.SemaphoreType.DMA((carried.n_sems,)), pltpu.SemaphoreType.DMA((carried.n_sems,))]
    return pl.pallas_call(
        full, grid=grid, in_specs=in_specs + [ANY] * c_in, out_specs=out_specs + [ANY] * c_out,
        out_shape=out_shape + carried.out_shapes, scratch_shapes=scratch_shapes + sems,
        input_output_aliases={n_in + k: n_out + v for k, v in carried.aliases.items()},
        compiler_params=_params(tuple("arbitrary" for _ in grid)), name=name,
    )(*args, *carried.ins)


def _run_carried(carried, name):
    c_in = len(carried.ins)

    def body(*refs):
        cins, couts = refs[:c_in], refs[c_in:c_in + len(carried.out_shapes)]
        send_sems, recv_sems = refs[-2:]
        carried.start(cins, couts, send_sems, recv_sems)
        carried.finish(cins, couts, send_sems, recv_sems)

    return pl.pallas_call(
        body, in_specs=[ANY] * c_in, out_specs=[ANY] * len(carried.out_shapes), out_shape=carried.out_shapes,
        scratch_shapes=[pltpu.SemaphoreType.DMA((carried.n_sems,)), pltpu.SemaphoreType.DMA((carried.n_sems,))],
        input_output_aliases=carried.aliases, name=name,
    )(*carried.ins)


def _matmul(a, b, *, form, out_dtype, tm, tn, tk, name, bias=None, add=None, out_sharded=False, carried=None):
    b3 = b.ndim == 3
    resident = 0
    if form == "nn":
        m, k = a.shape
        n = b.shape[0] * b.shape[2] if b3 else b.shape[1]
        dn = (((1,), (0,)), ((), ()))
        a_spec = pl.BlockSpec((tm, tk), lambda i, j, kk: (i, kk))
        if b3 and tn == n and tk == k:
            resident = b.shape[0]
            b_spec = pl.BlockSpec(b.shape, lambda i, j, kk: (0, 0, 0))
        else:
            b_spec = (pl.BlockSpec((None, tk, tn), lambda i, j, kk: (j, kk, 0)) if b3
                      else pl.BlockSpec((tk, tn), lambda i, j, kk: (kk, j)))
    elif form == "nt":
        m, k = a.shape
        n = b.shape[1] if b3 else b.shape[0]
        dn = (((1,), (1,)), ((), ()))
        a_spec = pl.BlockSpec((tm, tk), lambda i, j, kk: (i, kk))
        if b3 and tk == k:
            resident = b.shape[0]
            b_spec = pl.BlockSpec((resident, tn, b.shape[2]), lambda i, j, kk: (0, j, 0))
        else:
            b_spec = (pl.BlockSpec((None, tn, tk), lambda i, j, kk: (kk, j, 0)) if b3
                      else pl.BlockSpec((tn, tk), lambda i, j, kk: (j, kk)))
    else:
        k, m = a.shape
        n = b.shape[1]
        dn = (((0,), (0,)), ((), ()))
        a_spec = pl.BlockSpec((tk, tm), lambda i, j, kk: (kk, i))
        b_spec = pl.BlockSpec((tk, tn), lambda i, j, kk: (kk, j))
    assert m % tm == 0 and n % tn == 0 and k % tk == 0, (name, m, n, k, tm, tn, tk)
    nk = k // tk
    in_specs, args = [a_spec, b_spec], [a, b]
    if bias is not None:
        in_specs.append(pl.BlockSpec((1, tn), lambda i, j, kk: (0, j)))
        args.append(bias)
    if add is not None:
        in_specs.append(pl.BlockSpec((tm, tn), lambda i, j, kk: (i, j)))
        args.append(add)
    if out_sharded:
        out_shape = _sds((n // tn, m, tn), out_dtype)
        out_spec = pl.BlockSpec((None, tm, tn), lambda i, j, kk: (j, i, 0))
    else:
        out_shape = _sds((m, n), out_dtype)
        out_spec = pl.BlockSpec((tm, tn), lambda i, j, kk: (i, j))

    def body(*refs):
        a_ref, b_ref = refs[0], refs[1]
        pos = 2
        bias_ref = add_ref = None
        if bias is not None:
            bias_ref, pos = refs[pos], pos + 1
        if add is not None:
            add_ref, pos = refs[pos], pos + 1
        o_ref = refs[pos]
        if resident and form == "nn":
            ns = b_ref.shape[2]
            for s in range(resident):
                cols = slice(s * ns, (s + 1) * ns)
                ps = lax.dot_general(a_ref[...], b_ref[s], dn, preferred_element_type=F32)
                if bias_ref is not None:
                    ps = ps + bias_ref[:, cols]
                o_ref[:, cols] = ps.astype(o_ref.dtype)
            return
        if resident:
            ks = b_ref.shape[2]
            p = None
            for s in range(resident):
                ps = lax.dot_general(a_ref[:, s * ks:(s + 1) * ks], b_ref[s], dn, preferred_element_type=F32)
                p = ps if p is None else p + ps
        else:
            av, bv = a_ref[...], b_ref[...]
            if av.dtype != BF16:
                av = av.astype(BF16)
            if bv.dtype != BF16:
                bv = bv.astype(BF16)
            p = lax.dot_general(av, bv, dn, preferred_element_type=F32)

        def finish(acc):
            if bias_ref is not None:
                acc = acc + bias_ref[...]
            if add_ref is not None:
                acc = acc + add_ref[...]
            o_ref[...] = acc.astype(o_ref.dtype)

        if nk == 1:
            finish(p)
        else:
            acc_ref = refs[pos + 1]
            kk = pl.program_id(2)

            @pl.when(kk == 0)
            def _():
                acc_ref[...] = p

            @pl.when(kk > 0)
            def _():
                acc_ref[...] += p

            @pl.when(kk == nk - 1)
            def _():
                finish(acc_ref[...])

    res = _call(body, grid=(m // tm, n // tn, nk), in_specs=in_specs, out_specs=[out_spec], out_shape=[out_shape],
                scratch_shapes=[pltpu.VMEM((tm, tn), F32)] if nk > 1 else [],
                sem=("parallel", "parallel", "arbitrary"), name=name, args=args, carried=carried)
    return res[0] if carried is None else (res[0], res[1:])


def _rowcall(fn, rows, consts, row_outs, acc_outs, *, name, tm=ROW_TILE, col_grid=1):
    n_rows = rows[0][0].shape[0]
    assert n_rows % tm == 0
    grid = (col_grid, n_rows // tm)
    in_specs = [pl.BlockSpec((tm, w), functools.partial(lambda c, i, cb: (i, cb + c), cb=cb)) for _, w, cb in rows]
    in_specs += [pl.BlockSpec(k.shape, functools.partial(lambda c, i, nd: (0,) * nd, nd=k.ndim)) for k in consts]
    out_specs = [pl.BlockSpec((tm, w), lambda c, i: (i, c)) for _, _, _, w in row_outs]
    out_specs += [pl.BlockSpec((r, w), lambda c, i: (0, c)) for r, _, w in acc_outs]
    out_shape = [_sds((nr, nc), dt) for nr, nc, dt, _ in row_outs] + [_sds((r, nc), F32) for r, nc, _ in acc_outs]
    n_in, n_ro = len(rows) + len(consts), len(row_outs)

    def body(*refs):
        res = fn(*[r[...] for r in refs[:n_in]])
        if not isinstance(res, (tuple, list)):
            res = (res,)
        outs = refs[n_in:]
        for o_ref, val in zip(outs[:n_ro], res[:n_ro]):
            o_ref[...] = val.astype(o_ref.dtype)
        if acc_outs:
            first = pl.program_id(1) == 0

            @pl.when(first)
            def _():
                for o_ref, val in zip(outs[n_ro:], res[n_ro:]):
                    o_ref[...] = val

            @pl.when(jnp.logical_not(first))
            def _():
                for o_ref, val in zip(outs[n_ro:], res[n_ro:]):
                    o_ref[...] += val

    out = pl.pallas_call(
        body, grid=grid, in_specs=in_specs, out_specs=out_specs, out_shape=out_shape,
        compiler_params=_params(("arbitrary", "arbitrary")), name=name,
    )(*[r[0] for r in rows], *consts)
    return out


def _matmul_rows(b, *, form, tm, tk, fn, rows, consts, row_outs, acc_outs, name, a=None, a_rows=None, a_fn=None,
                 carried=None):
    b3 = b.ndim == 3
    resident = 0
    if form == "nn":
        k, n = b.shape
        b_spec = pl.BlockSpec((tk, n), lambda i, kk: (kk, 0))
        dn = (((1,), (0,)), ((), ()))
    else:
        n = b.shape[1] if b3 else b.shape[0]
        k = b.shape[0] * b.shape[2] if b3 else b.shape[1]
        if b3 and tk == k:
            resident = b.shape[0]
            b_spec = pl.BlockSpec(b.shape, lambda i, kk: (0, 0, 0))
        else:
            b_spec = (pl.BlockSpec((None, n, tk), lambda i, kk: (kk, 0, 0)) if b3
                      else pl.BlockSpec((n, tk), lambda i, kk: (0, kk)))
        dn = (((1,), (1,)), ((), ()))
    nk = k // tk
    lhs_in = [(a, tk, 0)] if a is not None else list(a_rows)
    assert a is not None or nk == 1
    m = lhs_in[0][0].shape[0]
    n_lhs = len(lhs_in)
    in_specs = [pl.BlockSpec((tm, tk), lambda i, kk: (i, kk))] if a is not None else [
        pl.BlockSpec((tm, w), functools.partial(lambda i, kk, cb: (i, cb), cb=cb)) for _, w, cb in a_rows]
    in_specs.append(b_spec)
    in_specs += [pl.BlockSpec((tm, w), functools.partial(lambda i, kk, cb: (i, cb), cb=cb)) for _, w, cb in rows]
    in_specs += [pl.BlockSpec(c.shape, functools.partial(lambda i, kk, nd: (0,) * nd, nd=c.ndim)) for c in consts]
    out_specs = [pl.BlockSpec((tm, w), lambda i, kk: (i, 0)) for _, w in row_outs]
    out_specs += [pl.BlockSpec((r, w), lambda i, kk: (0, 0)) for r, w in acc_outs]
    out_shape = [_sds((m, w), dt) for dt, w in row_outs] + [_sds((r, w), F32) for r, w in acc_outs]
    n_rows, n_consts, n_ro, n_acc = len(rows), len(consts), len(row_outs), len(acc_outs)

    def body(*refs):
        pos = n_lhs + 1
        row_refs, const_refs = refs[pos:pos + n_rows], refs[pos + n_rows:pos + n_rows + n_consts]
        pos += n_rows + n_consts
        out_refs, acc_refs = refs[pos:pos + n_ro], refs[pos + n_ro:pos + n_ro + n_acc]
        i, kk = pl.program_id(0), pl.program_id(1)
        if resident:
            b_ref, ks, p = refs[n_lhs], b.shape[2], None
            for s in range(resident):
                ps = lax.dot_general(refs[0][:, s * ks:(s + 1) * ks], b_ref[s], dn, preferred_element_type=F32)
                p = ps if p is None else p + ps
        else:
            lhs = refs[0][...] if a is not None else a_fn(*[r[...] for r in refs[:n_lhs]]).astype(BF16)
            p = lax.dot_general(lhs, refs[n_lhs][...], dn, preferred_element_type=F32)

        def finish(acc):
            extra = [r[...] for r in row_refs] + [c[...] for c in const_refs]
            res = fn(acc, lhs, *extra) if a is None else fn(acc, *extra)
            for o_ref, val in zip(out_refs, res[:n_ro]):
                o_ref[...] = val.astype(o_ref.dtype)
            if n_acc:
                @pl.when(i == 0)
                def _():
                    for o_ref, val in zip(acc_refs, res[n_ro:]):
                        o_ref[...] = val

                @pl.when(i > 0)
                def _():
                    for o_ref, val in zip(acc_refs, res[n_ro:]):
                        o_ref[...] += val

        if nk == 1:
            finish(p)
        else:
            acc_ref = refs[pos + n_ro + n_acc]

            @pl.when(kk == 0)
            def _():
                acc_ref[...] = p

            @pl.when(kk > 0)
            def _():
                acc_ref[...] += p

            @pl.when(kk == nk - 1)
            def _():
                finish(acc_ref[...])

    res = _call(body, grid=(m // tm, nk), in_specs=in_specs, out_specs=out_specs, out_shape=out_shape,
                scratch_shapes=[pltpu.VMEM((tm, n), F32)] if nk > 1 else [], sem=("arbitrary", "arbitrary"),
                name=name, args=[r[0] for r in lhs_in] + [b] + [r[0] for r in rows] + list(consts), carried=carried)
    own = n_ro + n_acc
    return res[:own] if carried is None else (res[:own], res[own:])


def _colsum(v):
    return jnp.sum(v, axis=0, keepdims=True)


def _sigmoid(v):
    return 1.0 / (1.0 + jnp.exp(-v))


_GELU_C = math.sqrt(2.0 / math.pi)


def _gelu(v):
    return 0.5 * v * (1.0 + jnp.tanh(_GELU_C * (v + 0.044715 * (v * v * v))))


def _gelu_and_grad(v):
    th = jnp.tanh(_GELU_C * (v + 0.044715 * (v * v * v)))
    g = 0.5 * v * (1.0 + th)
    dg = 0.5 * (1.0 + th) + 0.5 * v * (1.0 - th * th) * (_GELU_C * (1.0 + 3.0 * 0.044715 * (v * v)))
    return g, dg


def _rms_stats(v):
    r = lax.rsqrt(jnp.mean(v * v, axis=-1, keepdims=True) + EPS)
    return v * r, r


def _rms_bwd(dn, vn, r):
    return r * (dn - vn * jnp.mean(dn * vn, axis=-1, keepdims=True))


def _pre_norm(x, g, sc, sh, name):
    def fn(xv, gv, scv, shv):
        xn, _ = _rms_stats(xv)
        return (xn * gv) * (1.0 + scv) + shv
    return _rowcall(fn, [(x, D, 0)], [g, sc, sh], [(x.shape[0], D, BF16, D)], [], name=name,
                    tm=min(2 * ROW_TILE, x.shape[0]))[0]


def _pre_norm_bwd(dh, x, dx_other, g, sc, name):
    def fn(dhv, xv, dov, gv, scv):
        xn, r = _rms_stats(xv)
        yn = xn * gv
        dyn = dhv * (1.0 + scv)
        dx = _rms_bwd(dyn * gv, xn, r)
        return dov + dx, _colsum(dhv), _colsum(dhv * yn), _colsum(dyn * xn)
    t = x.shape[0]
    return _rowcall(fn, [(dh, D, 0), (x, D, 0), (dx_other, D, 0)], [g, sc], [(t, D, F32, D)],
                    [(1, D, D)] * 3, name=name, tm=min(2 * ROW_TILE, t))


CONV_HALO = 32


def _layer_norm_parts(u):
    mu = jnp.mean(u, axis=-1, keepdims=True)
    d = u - mu
    r = lax.rsqrt(jnp.mean(d * d, axis=-1, keepdims=True) + EPS)
    return d * r, r


LANES = 128
SUBLANE_ROWS = 8
CONV_ROWS = 64


def _lanes(c):
    return slice(c * LANES, (c + 1) * LANES)


def _conv_branch(z, w_dw, b_dw, g_ln, b_ln, name, tm=ROW_TILE, carried=None):
    t = z.shape[0]
    per = tm // CONV_HALO
    n_chunks = 512 // LANES

    def body(ga_ref, gb_ref, gah_ref, gbh_ref, w_ref, b_ref, g_ref, bl_ref, u1_ref, u3_ref, scr):
        i = pl.program_id(0)
        u0h = jnp.where(i > 0, gah_ref[...] * _sigmoid(gbh_ref[...]), 0.0)
        u0 = ga_ref[...] * _sigmoid(gb_ref[...])
        for c in range(n_chunks):
            scr[c, 0:CONV_HALO, :] = u0h[:, _lanes(c)]
            scr[c, CONV_HALO:CONV_HALO + tm, :] = u0[:, _lanes(c)]
        for c in range(n_chunks):
            for r0 in range(0, tm, CONV_ROWS):
                acc = jnp.zeros((CONV_ROWS, LANES), F32) + b_ref[:, _lanes(c)]
                for j in range(CONV_K):
                    acc = acc + w_ref[j:j + 1, _lanes(c)] * scr[c, pl.ds(r0 + CONV_HALO - (CONV_K - 1) + j, CONV_ROWS), :]
                u1_ref[r0:r0 + CONV_ROWS, _lanes(c)] = acc
        xh, _ = _layer_norm_parts(u1_ref[...])
        u2 = xh * g_ref[...] + bl_ref[...]
        u3_ref[...] = (u2 * _sigmoid(u2)).astype(BF16)

    cur = lambda cb: pl.BlockSpec((tm, 512), lambda i: (i, cb))
    halo = lambda cb: pl.BlockSpec((CONV_HALO, 512), lambda i: (jnp.maximum(i * per - 1, 0), cb))
    whole = lambda a: pl.BlockSpec(a.shape, lambda i: (0, 0))
    res = _call(
        body, grid=(t // tm,),
        in_specs=[cur(3), cur(4), halo(3), halo(4), whole(w_dw), whole(b_dw), whole(g_ln), whole(b_ln)],
        out_specs=[pl.BlockSpec((tm, 512), lambda i: (i, 0))] * 2,
        out_shape=[_sds((t, 512), F32), _sds((t, 512), BF16)],
        scratch_shapes=[pltpu.VMEM((n_chunks, CONV_HALO + tm, LANES), F32)],
        sem=("arbitrary",), name=name, args=(z, z, z, z, w_dw, b_dw, g_ln, b_ln), carried=carried)
    return res[:2] if carried is None else (res[:2], res[2:])


def _conv_branch_bwd(du3, u1, z, w_dw, g_ln, b_ln, name, tm=ROW_TILE, carried=None):
    t = z.shape[0]
    per = tm // CONV_HALO
    last = t // tm - 1
    n_chunks = 512 // LANES

    def du1_of(du3v, u1v, g, b):
        xh, r = _layer_norm_parts(u1v)
        u2 = xh * g + b
        s = _sigmoid(u2)
        du2 = du3v * (s * (1.0 + u2 * (1.0 - s)))
        dxh = du2 * g
        du1 = r * (dxh - jnp.mean(dxh, axis=-1, keepdims=True) - xh * jnp.mean(dxh * xh, axis=-1, keepdims=True))
        return du1, du2, xh

    def body(d_ref, u_ref, dn_ref, un_ref, ga_ref, gb_ref, gah_ref, gbh_ref, w_ref, g_ref, bl_ref,
             dglu_ref, dw_ref, dbdw_ref, dg_ref, dbl_ref, dbin_ref, scr, scd):
        i = pl.program_id(0)
        g, b = g_ref[...], bl_ref[...]
        du1, du2, xh = du1_of(d_ref[...], u_ref[...], g, b)
        du1n, _, _ = du1_of(dn_ref[...], un_ref[...], g, b)
        du1n = jnp.where(i < last, du1n, 0.0)
        sgb = _sigmoid(gb_ref[...])
        ga = ga_ref[...]
        u0 = ga * sgb
        u0h = jnp.where(i > 0, gah_ref[...] * _sigmoid(gbh_ref[...]), 0.0)
        for c in range(n_chunks):
            scd[c, 0:tm, :] = du1[:, _lanes(c)]
            scd[c, tm:tm + CONV_HALO, :] = du1n[:, _lanes(c)]
            scr[c, 0:CONV_HALO, :] = u0h[:, _lanes(c)]
            scr[c, CONV_HALO:CONV_HALO + tm, :] = u0[:, _lanes(c)]

        @pl.when(i == 0)
        def _():
            for ref in (dw_ref, dbdw_ref, dg_ref, dbl_ref, dbin_ref):
                ref[...] = jnp.zeros_like(ref)

        dsg = ga * (sgb * (1.0 - sgb))
        for c in range(n_chunks):
            gate = slice(512 + c * LANES, 512 + (c + 1) * LANES)
            for r0 in range(0, tm, CONV_ROWS):
                rows = slice(r0, r0 + CONV_ROWS)
                du0 = jnp.zeros((CONV_ROWS, LANES), F32)
                for j in range(CONV_K):
                    du0 = du0 + w_ref[j:j + 1, _lanes(c)] * scd[c, pl.ds(r0 + CONV_K - 1 - j, CONV_ROWS), :]
                dga = du0 * sgb[rows, _lanes(c)]
                dgb = du0 * dsg[rows, _lanes(c)]
                dglu_ref[rows, _lanes(c)] = dga.astype(BF16)
                dglu_ref[rows, gate] = dgb.astype(BF16)
                dbin_ref[:, _lanes(c)] += _colsum(dga)
                dbin_ref[:, gate] += _colsum(dgb)
            for j in range(CONV_K):
                dwj = jnp.zeros((SUBLANE_ROWS, LANES), F32)
                for r0 in range(0, tm, CONV_ROWS):
                    prod = (scd[c, pl.ds(r0, CONV_ROWS), :]
                            * scr[c, pl.ds(r0 + CONV_HALO - (CONV_K - 1) + j, CONV_ROWS), :])
                    dwj = dwj + jnp.sum(prod.reshape(CONV_ROWS // SUBLANE_ROWS, SUBLANE_ROWS, LANES), axis=0)
                dw_ref[j:j + 1, _lanes(c)] += _colsum(dwj)
        dbdw_ref[...] += _colsum(du1)
        dg_ref[...] += _colsum(du2 * xh)
        dbl_ref[...] += _colsum(du2)

    cur = lambda cb: pl.BlockSpec((tm, 512), lambda i: (i, cb))
    prev = lambda cb: pl.BlockSpec((CONV_HALO, 512), lambda i: (jnp.maximum(i * per - 1, 0), cb))
    nxt = pl.BlockSpec((CONV_HALO, 512), lambda i: (jnp.minimum((i + 1) * per, t // CONV_HALO - 1), 0))
    whole = lambda a: pl.BlockSpec(a.shape, lambda i: (0, 0))
    acc = lambda r, w: pl.BlockSpec((r, w), lambda i: (0, 0))
    res = _call(
        body, grid=(t // tm,),
        in_specs=[cur(0), cur(0), nxt, nxt, cur(3), cur(4), prev(3), prev(4), whole(w_dw), whole(g_ln), whole(b_ln)],
        out_specs=[pl.BlockSpec((tm, 1024), lambda i: (i, 0)), acc(CONV_K, 512), acc(1, 512), acc(1, 512),
                   acc(1, 512), acc(1, 1024)],
        out_shape=[_sds((t, 1024), BF16), _sds((CONV_K, 512), F32), _sds((1, 512), F32), _sds((1, 512), F32),
                   _sds((1, 512), F32), _sds((1, 1024), F32)],
        scratch_shapes=[pltpu.VMEM((n_chunks, CONV_HALO + tm, LANES), F32),
                        pltpu.VMEM((n_chunks, tm + CONV_HALO, LANES), F32)],
        sem=("arbitrary",), name=name, args=(du3, u1, du3, u1, z, z, z, z, w_dw, g_ln, b_ln), carried=carried)
    return res[:6] if carried is None else (res[:6], res[6:])


FF_BLOCK = D_FF // 2
FF_HALO = 8
FF_CHUNKS = FF_BLOCK // LANES


FF_ROWS = 64
FF_EXT_ROWS = 88


def _ffn_conv(w_ref, b_ref, scr, k, rows, r0=0):
    acc = b_ref[:, _lanes(k)] + w_ref[0:1, _lanes(k)] * scr[k, pl.ds(r0 + FF_HALO - 2, rows), :]
    acc = acc + w_ref[1:2, _lanes(k)] * scr[k, pl.ds(r0 + FF_HALO - 1, rows), :]
    return acc + w_ref[2:3, _lanes(k)] * scr[k, pl.ds(r0 + FF_HALO, rows), :]


def _ffn_act(up, w3, b3, name, tm=ROW_TILE, carried=None):
    t = up.shape[0]
    per = tm // FF_HALO
    wide = 2 * FF_BLOCK

    def body(u_ref, uh_ref, w_ref, b_ref, o_ref, scr):
        i = pl.program_id(1)
        for k in range(2 * FF_CHUNKS):
            scr[k, 0:FF_HALO, :] = jnp.where(i > 0, uh_ref[:, _lanes(k)], 0.0)
            scr[k, FF_HALO:FF_HALO + tm, :] = u_ref[:, _lanes(k)]
        for cc in range(FF_CHUNKS):
            for r0 in range(0, tm, FF_ROWS):
                val = _ffn_conv(w_ref, b_ref, scr, cc, FF_ROWS, r0)
                gate = _ffn_conv(w_ref, b_ref, scr, FF_CHUNKS + cc, FF_ROWS, r0)
                o_ref[r0:r0 + FF_ROWS, _lanes(cc)] = (_gelu(gate) * val).astype(BF16)

    res = _call(
        body, grid=(2, t // tm),
        in_specs=[pl.BlockSpec((tm, wide), lambda c, i: (i, c)),
                  pl.BlockSpec((FF_HALO, wide), lambda c, i: (jnp.maximum(i * per - 1, 0), c)),
                  pl.BlockSpec((FFN_K, wide), lambda c, i: (0, c)),
                  pl.BlockSpec((1, wide), lambda c, i: (0, c))],
        out_specs=[pl.BlockSpec((tm, FF_BLOCK), lambda c, i: (i, c))],
        out_shape=[_sds((t, D_FF), BF16)],
        scratch_shapes=[pltpu.VMEM((2 * FF_CHUNKS, FF_HALO + tm, LANES), F32)],
        sem=("arbitrary", "arbitrary"), name=name, args=(up, up, w3, b3), carried=carried)
    return res[0] if carried is None else (res[0], res[1:])


def _ffn_act_bwd(dact, up, w3, b3, name, tm=ROW_TILE):
    t = up.shape[0]
    per = tm // FF_HALO
    wide = 2 * FF_BLOCK
    last = t // tm - 1
    ext = tm + FF_HALO

    def body(u_ref, up_ref, un_ref, d_ref, dn_ref, w_ref, b_ref, o_ref, dw_ref, db_ref, scr, scd):
        i = pl.program_id(1)
        for k in range(2 * FF_CHUNKS):
            scr[k, 0:FF_HALO, :] = jnp.where(i > 0, up_ref[:, _lanes(k)], 0.0)
            scr[k, FF_HALO:FF_HALO + tm, :] = u_ref[:, _lanes(k)]
            scr[k, FF_HALO + tm:FF_HALO + ext, :] = un_ref[:, _lanes(k)]
        dn = jnp.where(i < last, dn_ref[...], 0.0)

        @pl.when(i == 0)
        def _():
            dw_ref[...] = jnp.zeros_like(dw_ref)
            db_ref[...] = jnp.zeros_like(db_ref)

        for cc in range(FF_CHUNKS):
            gc = FF_CHUNKS + cc
            for r0 in range(0, ext, FF_EXT_ROWS):
                rows = pl.ds(r0, FF_EXT_ROWS)
                val = _ffn_conv(w_ref, b_ref, scr, cc, FF_EXT_ROWS, r0)
                gel, dgel = _gelu_and_grad(_ffn_conv(w_ref, b_ref, scr, gc, FF_EXT_ROWS, r0))
                da = d_ref[r0:r0 + FF_EXT_ROWS, _lanes(cc)] if r0 + FF_EXT_ROWS <= tm else jnp.concatenate(
                    [d_ref[r0:tm, _lanes(cc)], dn[:, _lanes(cc)]], axis=0)
                scd[cc, rows, :] = da * gel
                scd[gc, rows, :] = da * val * dgel
            for k in (cc, gc):
                dwk = [jnp.zeros((SUBLANE_ROWS, LANES), F32) for _ in range(FFN_K)]
                dbk = jnp.zeros((SUBLANE_ROWS, LANES), F32)
                for r0 in range(0, tm, FF_ROWS):
                    shifted = [scd[k, pl.ds(r0 + FFN_K - 1 - j, FF_ROWS), :] for j in range(FFN_K)]
                    ucur = scr[k, pl.ds(r0 + FF_HALO, FF_ROWS), :]
                    o_ref[r0:r0 + FF_ROWS, _lanes(k)] = (
                        w_ref[0:1, _lanes(k)] * shifted[0] + w_ref[1:2, _lanes(k)] * shifted[1]
                        + w_ref[2:3, _lanes(k)] * shifted[2]).astype(BF16)
                    fold = lambda v: jnp.sum(v.reshape(FF_ROWS // SUBLANE_ROWS, SUBLANE_ROWS, LANES), axis=0)
                    for j in range(FFN_K):
                        dwk[j] = dwk[j] + fold(shifted[j] * ucur)
                    dbk = dbk + fold(shifted[FFN_K - 1])
                for j in range(FFN_K):
                    dw_ref[j:j + 1, _lanes(k)] += _colsum(dwk[j])
                db_ref[:, _lanes(k)] += _colsum(dbk)

    nblk = t // FF_HALO
    return pl.pallas_call(
        body, grid=(2, t // tm),
        in_specs=[pl.BlockSpec((tm, wide), lambda c, i: (i, c)),
                  pl.BlockSpec((FF_HALO, wide), lambda c, i: (jnp.maximum(i * per - 1, 0), c)),
                  pl.BlockSpec((FF_HALO, wide), lambda c, i: (jnp.minimum((i + 1) * per, nblk - 1), c)),
                  pl.BlockSpec((tm, FF_BLOCK), lambda c, i: (i, c)),
                  pl.BlockSpec((FF_HALO, FF_BLOCK), lambda c, i: (jnp.minimum((i + 1) * per, nblk - 1), c)),
                  pl.BlockSpec((FFN_K, wide), lambda c, i: (0, c)),
                  pl.BlockSpec((1, wide), lambda c, i: (0, c))],
        out_specs=[pl.BlockSpec((tm, wide), lambda c, i: (i, c)),
                   pl.BlockSpec((FFN_K, wide), lambda c, i: (0, c)),
                   pl.BlockSpec((1, wide), lambda c, i: (0, c))],
        out_shape=[_sds((t, 2 * D_FF), BF16), _sds((FFN_K, 2 * D_FF), F32), _sds((1, 2 * D_FF), F32)],
        scratch_shapes=[pltpu.VMEM((2 * FF_CHUNKS, FF_HALO + ext, LANES), F32),
                        pltpu.VMEM((2 * FF_CHUNKS, ext, LANES), F32)],
        compiler_params=_params(("arbitrary", "arbitrary")), name=name,
    )(up, up, up, dact, dact, w3, b3)


def _toeplitz_map():
    f = np.zeros((TOEP, REL_PAD), np.float32)
    for m in range(TOEP - 1):
        rel = (WINDOW - 1) - m
        f[m, int(np.clip(rel, -MAX_REL, MAX_REL)) + MAX_REL] = 1.0
    return f


def _split3(v):
    hi = v.astype(BF16)
    r1 = v - hi.astype(F32)
    mid = r1.astype(BF16)
    lo = (r1 - mid.astype(F32)).astype(BF16)
    return hi, mid, lo


def _exact_select(v, sel):
    out = None
    for part in _split3(v):
        p = jnp.dot(part, sel, preferred_element_type=F32)
        out = p if out is None else out + p
    return out


def _select_call(v, sel, name):
    def body(v_ref, s_ref, o_ref):
        o_ref[...] = _exact_select(v_ref[...], s_ref[...])
    return pl.pallas_call(body, out_shape=_sds((v.shape[0], sel.shape[1]), F32), name=name)(v, sel)


def _band_bias(gen_row):
    b0 = jnp.broadcast_to(gen_row, (Q_TILE, TOEP))
    bias = pltpu.roll(b0, TOEP - (Q_TILE - 1), 1, stride=1, stride_axis=0)[:, :WINDOW]
    qq = lax.broadcasted_iota(jnp.int32, (Q_TILE, WINDOW), 0) // CHUNK
    kc = lax.broadcasted_iota(jnp.int32, (Q_TILE, WINDOW), 1) // CHUNK
    return jnp.where((kc >= qq) & (kc <= qq + LEFT_CHUNKS), bias, NEG_INF)


PAD_ROWS = WINDOW - Q_TILE
NT_DIMS = (((1,), (1,)), ((), ()))
TN_DIMS = (((0,), (0,)), ((), ()))


def _head_mask(hh):
    lane = lax.broadcasted_iota(jnp.int32, (1, 128), 1)
    return (lane < 64) if hh == 0 else (lane >= 64)


SOFTMAX_ROWS = 16


def _probs_block(s_scr, bias, hh, rows, q_start):
    s = s_scr[rows, :] + bias[hh, rows, :]
    col = lax.broadcasted_iota(jnp.int32, (SOFTMAX_ROWS, WINDOW), 1)
    s = jnp.where(col >= PAD_ROWS - q_start, s, NEG_INF)
    p = jnp.exp(s - jnp.max(s, axis=-1, keepdims=True))
    return p / jnp.sum(p, axis=-1, keepdims=True)


def _attention(z, gen, name, carried=None):
    t = z.shape[0]
    n_i = t // STEP_ROWS

    def body(q_ref, k_ref, v_ref, g_ref, o_ref, kpad, vpad, bias, s_scr, p_scr):
        hp, i = pl.program_id(0), pl.program_id(1)

        @pl.when(i == 0)
        def _():
            kpad[0:PAD_ROWS, :] = jnp.zeros((PAD_ROWS, 128), BF16)
            vpad[0:PAD_ROWS, :] = jnp.zeros((PAD_ROWS, 128), BF16)
            kpad[PAD_ROWS:PAD_ROWS + t, :] = k_ref[...].astype(BF16)
            vpad[PAD_ROWS:PAD_ROWS + t, :] = v_ref[...].astype(BF16)
            for hh in range(2):
                bias[hh] = _band_bias(g_ref[pl.ds(2 * hp + hh, 1), :])

        for q0 in range(0, STEP_ROWS, Q_TILE):
            q_start = i * STEP_ROWS + q0
            win = pl.ds(pl.multiple_of(q_start, Q_TILE), WINDOW)
            out = None
            for hh in range(2):
                mask = _head_mask(hh)
                qm = jnp.where(mask, q_ref[q0:q0 + Q_TILE, :] * (CHUNK ** -0.5), 0.0).astype(BF16)
                slot = 2 * (q0 // Q_TILE) + hh
                s_scr[slot] = lax.dot_general(qm, kpad[win, :], NT_DIMS, preferred_element_type=F32)
                for r0 in range(0, Q_TILE, SOFTMAX_ROWS):
                    rows = slice(r0, r0 + SOFTMAX_ROWS)
                    p_scr[slot, rows, :] = _probs_block(s_scr.at[slot], bias, hh, rows, q_start).astype(BF16)
                o = jnp.dot(p_scr[slot], vpad[win, :], preferred_element_type=F32)
                out = jnp.where(mask, o, 0.0) if out is None else jnp.where(mask, o, out)
            o_ref[q0:q0 + Q_TILE, :] = out.astype(BF16)

    res = _call(
        body, grid=(4, n_i),
        in_specs=[pl.BlockSpec((STEP_ROWS, 128), lambda h, i: (i, h)),
                  pl.BlockSpec((t, 128), lambda h, i: (0, 4 + h)),
                  pl.BlockSpec((t, 128), lambda h, i: (0, 8 + h)),
                  pl.BlockSpec((N_HEADS, TOEP), lambda h, i: (0, 0))],
        out_specs=[pl.BlockSpec((STEP_ROWS, 128), lambda h, i: (i, h))],
        out_shape=[_sds((t, 512), BF16)],
        scratch_shapes=[pltpu.VMEM((PAD_ROWS + t, 128), BF16), pltpu.VMEM((PAD_ROWS + t, 128), BF16),
                        pltpu.VMEM((2, Q_TILE, WINDOW), F32), pltpu.VMEM((4, Q_TILE, WINDOW), F32),
                        pltpu.VMEM((4, Q_TILE, WINDOW), BF16)],
        sem=("arbitrary", "arbitrary"), name=name, args=(z, z, z, gen), carried=carried)
    return res[0] if carried is None else (res[0], res[1:])


def _attention_bwd(z, datt, gen, name, carried=None):
    t = z.shape[0]
    n_i = t // STEP_ROWS

    def body(q_ref, k_ref, v_ref, d_ref, g_ref, dq_ref, dk_ref, dv_ref, sq_ref, sk_ref, sv_ref, dg_ref,
             kpad, vpad, dkacc, dvacc, bias, dsacc, s_scr, dp_scr, p_scr, ds_scr):
        hp, i = pl.program_id(0), pl.program_id(1)

        @pl.when(i == 0)
        def _():
            kpad[0:PAD_ROWS, :] = jnp.zeros((PAD_ROWS, 128), BF16)
            vpad[0:PAD_ROWS, :] = jnp.zeros((PAD_ROWS, 128), BF16)
            kpad[PAD_ROWS:PAD_ROWS + t, :] = k_ref[...].astype(BF16)
            vpad[PAD_ROWS:PAD_ROWS + t, :] = v_ref[...].astype(BF16)
            dkacc[...] = jnp.zeros_like(dkacc)
            dvacc[...] = jnp.zeros_like(dvacc)
            dsacc[...] = jnp.zeros_like(dsacc)
            for hh in range(2):
                bias[hh] = _band_bias(g_ref[pl.ds(2 * hp + hh, 1), :])

        dq_sum = None
        for q0 in range(0, STEP_ROWS, Q_TILE):
            q_start = i * STEP_ROWS + q0
            win = pl.ds(pl.multiple_of(q_start, Q_TILE), WINDOW)
            dq = None
            for hh in range(2):
                mask = _head_mask(hh)
                qm = jnp.where(mask, q_ref[q0:q0 + Q_TILE, :] * (CHUNK ** -0.5), 0.0).astype(BF16)
                dom = jnp.where(mask, d_ref[q0:q0 + Q_TILE, :], 0.0).astype(BF16)
                slot = 2 * (q0 // Q_TILE) + hh
                s_scr[slot] = lax.dot_general(qm, kpad[win, :], NT_DIMS, preferred_element_type=F32)
                dp_scr[slot] = lax.dot_general(dom, vpad[win, :], NT_DIMS, preferred_element_type=F32)
                for r0 in range(0, Q_TILE, SOFTMAX_ROWS):
                    rows = slice(r0, r0 + SOFTMAX_ROWS)
                    p = _probs_block(s_scr.at[slot], bias, hh, rows, q_start)
                    dp = dp_scr[slot, rows, :]
                    ds = p * (dp - jnp.sum(p * dp, axis=-1, keepdims=True))
                    dsacc[hh, rows, :] += ds
                    ds_scr[slot, rows, :] = ds.astype(BF16)
                    p_scr[slot, rows, :] = p.astype(BF16)
                ds16 = ds_scr[slot]
                dqh = jnp.dot(ds16, kpad[win, :], preferred_element_type=F32) * (CHUNK ** -0.5)
                dq = jnp.where(mask, dqh, 0.0) if dq is None else jnp.where(mask, dqh, dq)
                dkacc[win, :] += lax.dot_general(ds16, qm, TN_DIMS, preferred_element_type=F32)
                dvacc[win, :] += lax.dot_general(p_scr[slot], dom, TN_DIMS, preferred_element_type=F32)
            dq_ref[q0:q0 + Q_TILE, :] = dq.astype(BF16)
            dq_sum = _colsum(dq) if dq_sum is None else dq_sum + _colsum(dq)

        @pl.when(i == 0)
        def _():
            sq_ref[...] = dq_sum

        @pl.when(i > 0)
        def _():
            sq_ref[...] += dq_sum

        @pl.when(i == n_i - 1)
        def _():
            dk = dkacc[PAD_ROWS:PAD_ROWS + t, :]
            dv = dvacc[PAD_ROWS:PAD_ROWS + t, :]
            dk_ref[...] = dk.astype(BF16)
            dv_ref[...] = dv.astype(BF16)
            sk_ref[...] = _colsum(dk)
            sv_ref[...] = _colsum(dv)
            rr = lax.broadcasted_iota(jnp.int32, (Q_TILE, Q_TILE), 0)
            cc = lax.broadcasted_iota(jnp.int32, (Q_TILE, Q_TILE), 1)
            rev = jnp.where(rr + cc == Q_TILE - 1, 1.0, 0.0).astype(BF16)
            for hh in range(2):
                acc = None
                for part in _split3(dsacc[hh]):
                    pr = jnp.dot(rev, part, preferred_element_type=F32)
                    acc = pr if acc is None else acc + pr
                wide = jnp.concatenate([acc, jnp.zeros((Q_TILE, TOEP - WINDOW), F32)], axis=1)
                dg_ref[pl.ds(2 * hp + hh, 1), :] = _colsum(pltpu.roll(wide, 0, 1, stride=1, stride_axis=0))

    col = lambda off: pl.BlockSpec((t, 128), lambda h, i: (0, off + h))
    tile = lambda: pl.BlockSpec((STEP_ROWS, 128), lambda h, i: (i, h))
    sums = lambda: pl.BlockSpec((1, 128), lambda h, i: (0, h))
    res = _call(
        body, grid=(4, n_i),
        in_specs=[tile(), col(4), col(8), tile(), pl.BlockSpec((N_HEADS, TOEP), lambda h, i: (0, 0))],
        out_specs=[tile(), col(0), col(0), sums(), sums(), sums(), pl.BlockSpec((N_HEADS, TOEP), lambda h, i: (0, 0))],
        out_shape=[_sds((t, 512), BF16)] * 3 + [_sds((1, 512), F32)] * 3 + [_sds((N_HEADS, TOEP), F32)],
        scratch_shapes=[pltpu.VMEM((PAD_ROWS + t, 128), BF16), pltpu.VMEM((PAD_ROWS + t, 128), BF16),
                        pltpu.VMEM((PAD_ROWS + t, 128), F32), pltpu.VMEM((PAD_ROWS + t, 128), F32),
                        pltpu.VMEM((2, Q_TILE, WINDOW), F32), pltpu.VMEM((2, Q_TILE, WINDOW), F32),
                        pltpu.VMEM((4, Q_TILE, WINDOW), F32), pltpu.VMEM((4, Q_TILE, WINDOW), F32),
                        pltpu.VMEM((4, Q_TILE, WINDOW), BF16), pltpu.VMEM((4, Q_TILE, WINDOW), BF16)],
        sem=("arbitrary", "arbitrary"), name=name, args=(z, z, z, datt, gen), carried=carried)
    return res[:7] if carried is None else (res[:7], res[7:])


def _adamw_math(w, g, m, v):
    m = ADAM_B1 * m + (1.0 - ADAM_B1) * g
    v = ADAM_B2 * v + (1.0 - ADAM_B2) * (g * g)
    m_hat = m / (1.0 - ADAM_B1 ** ADAM_STEP)
    v_hat = v / (1.0 - ADAM_B2 ** ADAM_STEP)
    delta = -ADAM_LR * (m_hat / (jnp.sqrt(v_hat) + ADAM_EPS) + ADAM_WD * w)
    return delta, m, v


def _adamw_many(items, name):
    n = len(items)

    def body(*refs):
        ins, outs = refs[:4 * n], refs[4 * n:]
        for k in range(n):
            w, g, m, v = (r[...] for r in ins[4 * k:4 * k + 4])
            outs[3 * k][...], outs[3 * k + 1][...], outs[3 * k + 2][...] = _adamw_math(w, g, m, v)

    flat = [a for item in items for a in item]
    res = pl.pallas_call(body, out_shape=[_sds(item[0].shape, F32) for item in items for _ in range(3)],
                         name=name)(*flat)
    return [tuple(res[3 * k:3 * k + 3]) for k in range(n)]


def _adamw(w, g, m, v, name, after):
    r, c = w.shape
    tm = next(cand for cand in (512, 352, 256, 128, 64, 32, 16, 8) if r % cand == 0)
    return _rowcall(lambda wv, gv, mv, vv, _: (gv,) + _adamw_math(wv, gv, mv, vv),
                    [(w, c, 0), (g, c, 0), (m, c, 0), (v, c, 0)], [after], [(r, c, F32, c)] * 4, [], name=name, tm=tm)


def _ada_fwd(c_all, w_shard, b_shard, name):
    n = w_shard.shape[1]
    tn = 512

    def body(c_ref, w_ref, b_ref, o_ref, a_ref):
        cv = c_ref[...]
        act = cv * _sigmoid(cv)
        a_ref[...] = act
        o_ref[...] = jnp.dot(act.astype(BF16), w_ref[...].astype(BF16), preferred_element_type=F32) + b_ref[...]

    return pl.pallas_call(
        body, grid=(n // tn,),
        in_specs=[pl.BlockSpec((8, D), lambda j: (0, 0)), pl.BlockSpec((D, tn), lambda j: (0, j)),
                  pl.BlockSpec((1, tn), lambda j: (0, j))],
        out_specs=[pl.BlockSpec((8, tn), lambda j: (0, j)), pl.BlockSpec((8, D), lambda j: (0, 0))],
        out_shape=[_sds((8, n), F32), _sds((8, D), F32)],
        compiler_params=_params(("arbitrary",)), name=name,
    )(c_all, w_shard, b_shard)


def _ada_bwd_adamw(act_t, dmod_shard, w, m, v, name):
    r, c = w.shape
    tm = 2 * ROW_TILE

    def body(a_ref, d_ref, w_ref, m_ref, v_ref, g_ref, dl_ref, nm_ref, nv_ref):
        g = jnp.dot(a_ref[...], d_ref[...], precision=lax.Precision.HIGHEST, preferred_element_type=F32)
        g_ref[...] = g
        dl_ref[...], nm_ref[...], nv_ref[...] = _adamw_math(w_ref[...], g, m_ref[...], v_ref[...])

    blk = pl.BlockSpec((tm, c), lambda i: (i, 0))
    return pl.pallas_call(
        body, grid=(r // tm,),
        in_specs=[pl.BlockSpec((tm, 8), lambda i: (i, 0)), pl.BlockSpec((8, c), lambda i: (0, 0)), blk, blk, blk],
        out_specs=[blk] * 4, out_shape=[_sds((r, c), F32)] * 4,
        compiler_params=_params(("arbitrary",)), name=name,
    )(act_t, dmod_shard, w, m, v)


def _place():
    return lax.axis_index("x"), lax.axis_index("y"), lax.axis_index("c")


def _flip(v, bit):
    return 1 - v if bit else v


VMEM_SPEC = pl.BlockSpec(memory_space=pltpu.VMEM)


def _allgather8(v, name):
    r, c = v.shape

    def body(v_ref, g_ref, tot_ref, send_sems, recv_sems, local_sem):
        x, y, cc = _place()
        sibling = (x, y, 1 - cc)
        chips = [(_flip(x, k & 2), _flip(y, k & 1)) for k in (1, 2, 3)]

        def block(px, py, pc):
            return g_ref.at[4 * px + 2 * py + pc]

        def copy(k, place, to, src=None):
            slot = block(*place)
            return pltpu.make_async_remote_copy(src_ref=slot if src is None else src, dst_ref=slot,
                                                send_sem=send_sems.at[k], recv_sem=recv_sems.at[k],
                                                device_id=to, device_id_type=MESH)

        mine = pltpu.make_async_copy(v_ref, block(x, y, cc), local_sem)
        mine.start()
        first = [copy(0, (x, y, cc), sibling, src=v_ref)]
        first += [copy(1 + j, (x, y, cc), (px, py, cc), src=v_ref) for j, (px, py) in enumerate(chips)]
        for cp in first:
            cp.start()
        passed = [copy(4 + j, (px, py, cc), sibling) for j, (px, py) in enumerate(chips)]
        for j, (px, py) in enumerate(chips):
            copy(1 + j, (px, py, cc), (x, y, cc)).wait_recv()
            passed[j].start()
        copy(0, sibling, (x, y, cc)).wait_recv()
        for j, (px, py) in enumerate(chips):
            copy(4 + j, (px, py, 1 - cc), (x, y, cc)).wait_recv()
        for cp in first + passed:
            cp.wait_send()
        mine.wait()
        tot = g_ref[0]
        for d in range(1, 8):
            tot = tot + g_ref[d]
        tot_ref[...] = tot

    return pl.pallas_call(
        body, in_specs=[VMEM_SPEC], out_specs=[VMEM_SPEC, VMEM_SPEC],
        out_shape=[_sds((8, r, c), F32), _sds((r, c), F32)],
        scratch_shapes=[pltpu.SemaphoreType.DMA((7,)), pltpu.SemaphoreType.DMA((7,)), pltpu.SemaphoreType.DMA],
        compiler_params=pltpu.CompilerParams(vmem_limit_bytes=VMEM_LIMIT), name=name,
    )(v)


def _slot(px, py, swapped):
    return 2 * py + px if swapped else 2 * px + py


def _gather_shards(arrs, swapped, name):
    n = len(arrs)

    def body(*refs):
        ins, outs = refs[:n], refs[n:2 * n]
        send1, recv1, send2, recv2, local_sems = refs[2 * n:]
        x, y, c = _place()
        sibling = (x, y, 1 - c)
        chips = [(_flip(x, k & 2), _flip(y, k & 1)) for k in (1, 2, 3)]
        local_copies, sends = [], []
        for a in range(n):
            h = outs[a].shape[1] // 2
            mine = pl.ds(pl.multiple_of(c * h, 8), h)
            own = _slot(x, y, swapped[a])
            lc = pltpu.make_async_copy(ins[a], outs[a].at[own], local_sems.at[a])
            lc.start()
            local_copies.append(lc)
            for j, (px, py) in enumerate(chips):
                cp = pltpu.make_async_remote_copy(
                    src_ref=ins[a].at[mine], dst_ref=outs[a].at[own, mine], send_sem=send1.at[3 * a + j],
                    recv_sem=recv1.at[3 * a + j], device_id=(px, py, c), device_id_type=MESH)
                cp.start()
                sends.append(cp)
        for a in range(n):
            h = outs[a].shape[1] // 2
            mine = pl.ds(pl.multiple_of(c * h, 8), h)
            for j, (px, py) in enumerate(chips):
                piece = outs[a].at[_slot(px, py, swapped[a]), mine]
                pltpu.make_async_remote_copy(
                    src_ref=piece, dst_ref=piece, send_sem=send1.at[3 * a + j], recv_sem=recv1.at[3 * a + j],
                    device_id=(px, py, c), device_id_type=MESH).wait_recv()
                fwd = pltpu.make_async_remote_copy(
                    src_ref=piece, dst_ref=piece, send_sem=send2.at[3 * a + j], recv_sem=recv2.at[3 * a + j],
                    device_id=sibling, device_id_type=MESH)
                fwd.start()
                sends.append(fwd)
        for a in range(n):
            h = outs[a].shape[1] // 2
            other = pl.ds(pl.multiple_of((1 - c) * h, 8), h)
            for j, (px, py) in enumerate(chips):
                piece = outs[a].at[_slot(px, py, swapped[a]), other]
                pltpu.make_async_remote_copy(
                    src_ref=piece, dst_ref=piece, send_sem=send2.at[3 * a + j], recv_sem=recv2.at[3 * a + j],
                    device_id=sibling, device_id_type=MESH).wait_recv()
        for cp in sends:
            cp.wait_send()
        for lc in local_copies:
            lc.wait()

    dma = lambda k: pltpu.SemaphoreType.DMA((k,))
    return pl.pallas_call(
        body, in_specs=[ANY] * n, out_specs=[ANY] * n,
        out_shape=[_sds((4,) + a.shape, a.dtype) for a in arrs],
        scratch_shapes=[dma(3 * n), dma(3 * n), dma(3 * n), dma(3 * n), dma(n)], name=name,
    )(*arrs)


def _carry_pair_exchange(grads):
    n = len(grads)

    def copies(ins, outs, send_sems, recv_sems):
        x, y, c = _place()
        cps = []
        for a in range(n):
            h = ins[a].shape[1] // 2
            theirs = pl.ds(pl.multiple_of((1 - c) * h, 8), h)
            cps.append(pltpu.make_async_remote_copy(
                src_ref=ins[a].at[:, theirs, :], dst_ref=outs[a], send_sem=send_sems.at[a], recv_sem=recv_sems.at[a],
                device_id=(x, y, 1 - c), device_id_type=MESH))
        return cps

    def start(*refs):
        for cp in copies(*refs):
            cp.start()

    def finish(*refs):
        for cp in copies(*refs):
            cp.wait()

    return _Carried(grads, [_sds((4, g.shape[1] // 2, g.shape[2]), F32) for g in grads], {}, n, start, finish)


def _pair_sum(grad, recv, core, name):
    _, r, c = grad.shape
    h = r // 2

    def body(core_ref, g_ref, r_ref, o_ref):
        o_ref[...] = (g_ref[...] + r_ref[...]).astype(BF16)

    return pl.pallas_call(
        body,
        grid_spec=pltpu.PrefetchScalarGridSpec(
            num_scalar_prefetch=1, grid=(4,),
            in_specs=[pl.BlockSpec((None, h, c), lambda s, core_ref: (s, core_ref[0], 0)),
                      pl.BlockSpec((None, h, c), lambda s, core_ref: (s, 0, 0))],
            out_specs=pl.BlockSpec((None, h, c), lambda s, core_ref: (s, 0, 0))),
        out_shape=_sds((4, h, c), BF16), compiler_params=_params(("arbitrary",)), name=name,
    )(core, grad, recv)


def _carry_chip_exchange(parts, swapped):
    n = len(parts)

    def copies(ins, outs, send_sems, recv_sems):
        x, y, c = _place()
        chips = [(_flip(x, k & 2), _flip(y, k & 1)) for k in (1, 2, 3)]
        cps = []
        for a in range(n):
            for j, (px, py) in enumerate(chips):
                cps.append(pltpu.make_async_remote_copy(
                    src_ref=ins[a].at[_slot(px, py, swapped[a])], dst_ref=outs[a].at[j],
                    send_sem=send_sems.at[3 * a + j], recv_sem=recv_sems.at[3 * a + j],
                    device_id=(px, py, c), device_id_type=MESH))
        return cps

    def start(*refs):
        for cp in copies(*refs):
            cp.start()

    def finish(*refs):
        for cp in copies(*refs):
            cp.wait()

    return _Carried(parts, [_sds((3,) + p.shape[1:], BF16) for p in parts], {}, 3 * n, start, finish)


def _chip_sum(part, recv, slot_core, name):
    _, h, c = part.shape

    def body(sc_ref, p_ref, r_ref, o_ref):
        acc = p_ref[...].astype(F32)
        for j in range(3):
            acc = acc + r_ref[j].astype(F32)
        o_ref[...] = acc

    return pl.pallas_call(
        body,
        grid_spec=pltpu.PrefetchScalarGridSpec(
            num_scalar_prefetch=1, grid=(1,),
            in_specs=[pl.BlockSpec((None, h, c), lambda q, sc_ref: (sc_ref[0], 0, 0)),
                      pl.BlockSpec((3, h, c), lambda q, sc_ref: (0, 0, 0))],
            out_specs=pl.BlockSpec((h, c), lambda q, sc_ref: (sc_ref[1], 0))),
        out_shape=_sds((2 * h, c), F32), compiler_params=_params(("arbitrary",)), name=name,
    )(slot_core, part, recv)


def _carry_pair_share(shards):
    n = len(shards)

    def copies(outs, send_sems, recv_sems, mine):
        x, y, c = _place()
        cps = []
        for a in range(n):
            h = outs[a].shape[0] // 2
            half = outs[a].at[pl.ds(pl.multiple_of((c if mine else 1 - c) * h, 8), h)]
            cps.append(pltpu.make_async_remote_copy(
                src_ref=half, dst_ref=half, send_sem=send_sems.at[a], recv_sem=recv_sems.at[a],
                device_id=(x, y, 1 - c), device_id_type=MESH))
        return cps

    def start(ins, outs, send_sems, recv_sems):
        for cp in copies(outs, send_sems, recv_sems, True):
            cp.start()

    def finish(ins, outs, send_sems, recv_sems):
        for cp in copies(outs, send_sems, recv_sems, False):
            cp.wait_recv()
        for cp in copies(outs, send_sems, recv_sems, True):
            cp.wait_send()

    return _Carried(shards, [_sds(s.shape, F32) for s in shards], {a: a for a in range(n)}, n, start, finish)


def _carry_gather_ici(bufs, swapped):
    n = len(bufs)

    def copies(outs, send_sems, recv_sems, sending):
        x, y, c = _place()
        cps = []
        for a in range(n):
            h = outs[a].shape[1] // 2
            mine = pl.ds(pl.multiple_of(c * h, 8), h)
            for j, k in enumerate((1, 2, 3)):
                px, py = _flip(x, k & 2), _flip(y, k & 1)
                slot = _slot(x, y, swapped[a]) if sending else _slot(px, py, swapped[a])
                piece = outs[a].at[slot, mine]
                cps.append(pltpu.make_async_remote_copy(
                    src_ref=piece, dst_ref=piece, send_sem=send_sems.at[3 * a + j], recv_sem=recv_sems.at[3 * a + j],
                    device_id=(px, py, c), device_id_type=MESH))
        return cps

    def start(ins, outs, send_sems, recv_sems):
        for cp in copies(outs, send_sems, recv_sems, True):
            cp.start()

    def finish(ins, outs, send_sems, recv_sems):
        for cp in copies(outs, send_sems, recv_sems, False):
            cp.wait_recv()
        for cp in copies(outs, send_sems, recv_sems, True):
            cp.wait_send()

    return _Carried(bufs, [_sds(b.shape, b.dtype) for b in bufs], {a: a for a in range(n)}, 3 * n, start, finish)


HBM_SPEC = pl.BlockSpec(memory_space=pltpu.HBM)
SEM_SPEC = pl.BlockSpec(memory_space=pltpu.SEMAPHORE)
SIDE_EFFECT = pltpu.SideEffectType.DATAFLOW_SIDE_EFFECTING


def _ici_pieces(buf, send_sems, recv_sems, swapped, sending):
    x, y, c = _place()
    h = buf.shape[1] // 2
    mine = pl.ds(pl.multiple_of(c * h, 8), h)
    cps = []
    for j, k in enumerate((1, 2, 3)):
        px, py = _flip(x, k & 2), _flip(y, k & 1)
        piece = buf.at[_slot(x, y, swapped) if sending else _slot(px, py, swapped), mine]
        cps.append(pltpu.make_async_remote_copy(src_ref=piece, dst_ref=piece, send_sem=send_sems.at[j],
                                                recv_sem=recv_sems.at[j], device_id=(px, py, c), device_id_type=MESH))
    return cps


def _gather_ici_start(buf, after, swapped, name):
    def body(buf_ref, after_ref, send_sems, recv_sems, thru, token):
        for cp in _ici_pieces(thru, send_sems, recv_sems, swapped, True):
            cp.start()
        token[...] = jnp.zeros_like(token)

    return pl.pallas_call(
        body, name=name,
        out_shape=(pltpu.SemaphoreType.DMA((3,)), pltpu.SemaphoreType.DMA((3,)), pltpu.HBM(buf.shape, buf.dtype),
                   jax.ShapeDtypeStruct((8, 128), F32)),
        in_specs=(HBM_SPEC, ANY), out_specs=(SEM_SPEC, SEM_SPEC, HBM_SPEC, VMEM_SPEC), input_output_aliases={0: 2},
        compiler_params=pltpu.CompilerParams(has_side_effects=SIDE_EFFECT),
    )(pltpu.with_memory_space_constraint(buf, pltpu.HBM), after)


def _gather_ici_wait(send_sems, recv_sems, thru, after, swapped, name):
    def body(thru_ref, send_sems, recv_sems, after_ref, out_ref):
        for cp in _ici_pieces(out_ref, send_sems, recv_sems, swapped, True):
            cp.wait_send()
        for cp in _ici_pieces(out_ref, send_sems, recv_sems, swapped, False):
            cp.wait_recv()

    return pl.pallas_call(
        body, name=name, out_shape=pltpu.HBM(thru.shape, thru.dtype),
        in_specs=(HBM_SPEC, SEM_SPEC, SEM_SPEC, ANY), out_specs=HBM_SPEC, input_output_aliases={0: 0},
        compiler_params=pltpu.CompilerParams(has_side_effects=SIDE_EFFECT),
    )(thru, send_sems, recv_sems, after)


def _all8_copies(buf, send_sems, recv_sems, sending):
    x, y, c = _place()
    cps = []
    for k in range(1, 8):
        px, py, pc = _flip(x, k & 4), _flip(y, k & 2), _flip(c, k & 1)
        slot = buf.at[4 * x + 2 * y + c] if sending else buf.at[4 * px + 2 * py + pc]
        cps.append(pltpu.make_async_remote_copy(src_ref=slot, dst_ref=slot, send_sem=send_sems.at[k - 1],
                                                recv_sem=recv_sems.at[k - 1], device_id=(px, py, pc), device_id_type=MESH))
    return cps


def _all8_start(buf, name):
    def body(buf_ref, send_sems, recv_sems, thru, token):
        for cp in _all8_copies(thru, send_sems, recv_sems, True):
            cp.start()
        token[...] = jnp.zeros_like(token)

    return pl.pallas_call(
        body, name=name,
        out_shape=(pltpu.SemaphoreType.DMA((7,)), pltpu.SemaphoreType.DMA((7,)), pltpu.HBM(buf.shape, buf.dtype),
                   jax.ShapeDtypeStruct((8, 128), F32)),
        in_specs=(HBM_SPEC,), out_specs=(SEM_SPEC, SEM_SPEC, HBM_SPEC, VMEM_SPEC), input_output_aliases={0: 2},
        compiler_params=pltpu.CompilerParams(has_side_effects=SIDE_EFFECT),
    )(pltpu.with_memory_space_constraint(buf, pltpu.HBM))


def _all8_wait(send_sems, recv_sems, thru, after, name):
    def body(thru_ref, send_sems, recv_sems, after_ref, out_ref):
        for cp in _all8_copies(out_ref, send_sems, recv_sems, True):
            cp.wait_send()
        for cp in _all8_copies(out_ref, send_sems, recv_sems, False):
            cp.wait_recv()

    return pl.pallas_call(
        body, name=name, out_shape=pltpu.HBM(thru.shape, thru.dtype),
        in_specs=(HBM_SPEC, SEM_SPEC, SEM_SPEC, ANY), out_specs=HBM_SPEC, input_output_aliases={0: 0},
        compiler_params=pltpu.CompilerParams(has_side_effects=SIDE_EFFECT),
    )(thru, send_sems, recv_sems, after)


def _sum8(g, name):
    def body(g_ref, o_ref):
        tot = g_ref[0]
        for d in range(1, 8):
            tot = tot + g_ref[d]
        o_ref[...] = tot

    return pl.pallas_call(body, out_shape=_sds(g.shape[1:], F32), name=name)(g)


def _carry_gather_forward(bufs, swapped):
    n = len(bufs)

    def copies(outs, send_sems, recv_sems, sending):
        x, y, c = _place()
        cps = []
        for a in range(n):
            h = outs[a].shape[1] // 2
            rows = pl.ds(pl.multiple_of((c if sending else 1 - c) * h, 8), h)
            for j, k in enumerate((1, 2, 3)):
                piece = outs[a].at[_slot(_flip(x, k & 2), _flip(y, k & 1), swapped[a]), rows]
                cps.append(pltpu.make_async_remote_copy(
                    src_ref=piece, dst_ref=piece, send_sem=send_sems.at[3 * a + j], recv_sem=recv_sems.at[3 * a + j],
                    device_id=(x, y, 1 - c), device_id_type=MESH))
        return cps

    def start(ins, outs, send_sems, recv_sems):
        for cp in copies(outs, send_sems, recv_sems, True):
            cp.start()

    def finish(ins, outs, send_sems, recv_sems):
        for cp in copies(outs, send_sems, recv_sems, False):
            cp.wait_recv()
        for cp in copies(outs, send_sems, recv_sems, True):
            cp.wait_send()

    return _Carried(bufs, [_sds(b.shape, b.dtype) for b in bufs], {a: a for a in range(n)}, 3 * n, start, finish)


def _pack(arrs, rows_multiple=8):
    parts, offs, row = [], [], 0
    for a in arrs:
        flat = a.reshape(-1)
        nrow = -(-flat.shape[0] // D)
        parts.append(jnp.pad(flat, (0, nrow * D - flat.shape[0])))
        offs.append(row)
        row += nrow
    total = -(-row // rows_multiple) * rows_multiple
    if total > row:
        parts.append(jnp.zeros(((total - row) * D,), F32))
    return jnp.concatenate(parts).reshape(total, D), offs


def _unpack(packed, offs, shapes):
    out = []
    for off, shp in zip(offs, shapes):
        size = int(np.prod(shp))
        nrow = -(-size // D)
        out.append(packed[off:off + nrow].reshape(-1)[:size].reshape(shp))
    return out


def _to_bf16_slot(w, slot, name, after=None):
    r, c = w.shape
    tm = next(cand for cand in (512, 352, 256, 128, 64, 32, 16) if r % cand == 0)

    def body(slot_ref, w_ref, *rest):
        rest[-1][...] = w_ref[...].astype(BF16)

    in_specs = [pl.BlockSpec((tm, c), lambda i, slot_ref: (i, 0))]
    if after is not None:
        in_specs.append(pl.BlockSpec((8, 128), lambda i, slot_ref: (0, 0)))
    return pl.pallas_call(
        body,
        grid_spec=pltpu.PrefetchScalarGridSpec(
            num_scalar_prefetch=1, grid=(r // tm,), in_specs=in_specs,
            out_specs=pl.BlockSpec((None, tm, c), lambda i, slot_ref: (slot_ref[0], i, 0))),
        out_shape=_sds((4, r, c), BF16), compiler_params=_params(("arbitrary",)), name=name,
    )(slot, w, *([] if after is None else [after]))


def _unshard_cols(g):
    s, k, n = g.shape
    return jnp.transpose(g, (1, 0, 2)).reshape(k, s * n)


def _ff_swap(v):
    b = FF_BLOCK
    return jnp.concatenate([v[..., 0:b], v[..., 2 * b:3 * b], v[..., b:2 * b], v[..., 3 * b:4 * b]], axis=-1)


LATE = ("attn_o", "conv_o", "mix_o", "up", "down")
EARLY_GRADS = ("down", "up", "mix_o", "attn_o", "conv_o")


def _weight_views(bufs):
    return {"up": bufs["up"], "attn_o": _unshard_cols(bufs["attn_o"]), "conv_o": _unshard_cols(bufs["conv_o"]),
            "mix_o": bufs["mix_o"].reshape(D, D), "down": bufs["down"].reshape(D_FF, D)}


def _pair_sums(names, grads, recv, dist):
    return [_pair_sum(g, r, dist["core"], "pair_sum_" + n) for n, g, r in zip(names, grads, recv)]


def _reduce_halves(names, parts, from_chips, dist):
    return [_chip_sum(p, r, jnp.concatenate([dist["slots"][SWAPPED[n]], dist["core"]]), "chip_sum_" + n)
            for n, p, r in zip(names, parts, from_chips)]


FUSED_TILE = 256
WIDE_TILE = 512


def _gates(z):
    return [(z, 512, 5), (z, 512, 6), (z, 512, 7), (z, 512, 8)]


def _mix_out(a, cb, z, x, w_mix_o, g_post, gt, g_pre2, sc2, sh2, name):
    def lhs(av, cv, ga0, ga1, gb0, gb1):
        ga, gb = jnp.concatenate([ga0, ga1], axis=1), jnp.concatenate([gb0, gb1], axis=1)
        return _sigmoid(ga) * av + _sigmoid(gb) * cv

    def fn(ym, y, xv, gv, gtv, g2v, scv, shv):
        yn, _ = _rms_stats(ym)
        x1 = xv + gtv * (yn * gv)
        xn, _ = _rms_stats(x1)
        return ym, y, x1, (xn * g2v) * (1.0 + scv) + shv

    return _matmul_rows(w_mix_o, form="nn", tm=min(WIDE_TILE, x.shape[0]), tk=D, fn=fn, a_rows=[(a, D, 0), (cb, D, 0)] + _gates(z),
                        a_fn=lhs, rows=[(x, D, 0)], consts=[g_post, gt, g_pre2, sc2, sh2],
                        row_outs=[(F32, D), (BF16, D), (F32, D), (BF16, D)], acc_outs=[], name=name)


def _down_tail(act, w_down, x1, target, g, gt, name):
    def fn(yv, xv, tv, gv, gtv):
        yn, r = _rms_stats(yv)
        e = xv + gtv * (yn * gv) - tv
        dx2 = e * (1.0 / D)
        dyn = dx2 * gtv
        return (dx2, _rms_bwd(dyn * gv, yn, r), _colsum(e * e) * (0.5 / D), _colsum(dyn * yn),
                _colsum(dx2 * (yn * gv)))

    return _matmul_rows(w_down, form="nn", a=act, tm=min(WIDE_TILE, x1.shape[0]), tk=D_FF, fn=fn,
                        rows=[(x1, D, 0), (target, D, 0)], consts=[g, gt], row_outs=[(F32, D), (BF16, D)],
                        acc_outs=[(1, D)] * 3, name=name)


def _up_dx_tail(dup, w_up, x1, dx2, ym, g_pre2, sc2, g_post, gt, name):
    def fn(dh, xv, dov, ymv, g2v, scv, gv, gtv):
        xn, r = _rms_stats(xv)
        dyn = dh * (1.0 + scv)
        dx1 = dov + _rms_bwd(dyn * g2v, xn, r)
        yn, r2 = _rms_stats(ymv)
        dynm = dx1 * gtv
        return (dx1, _rms_bwd(dynm * gv, yn, r2), _colsum(dh), _colsum(dh * (xn * g2v)), _colsum(dyn * xn),
                _colsum(dynm * yn), _colsum(dx1 * (yn * gv)))

    return _matmul_rows(w_up, form="nt", a=dup, tm=min(FUSED_TILE, x1.shape[0]), tk=2 * D_FF, fn=fn,
                        rows=[(x1, D, 0), (dx2, D, 0), (ym, D, 0)], consts=[g_pre2, sc2, g_post, gt],
                        row_outs=[(F32, D), (BF16, D)], acc_outs=[(1, D)] * 5, name=name)


def _mix_dx_gates(dym, w_mix_o, a, cb, z, name):
    def fn(dy, av, cv, ga0, ga1, gb0, gb1):
        sa = _sigmoid(jnp.concatenate([ga0, ga1], axis=1))
        sb = _sigmoid(jnp.concatenate([gb0, gb1], axis=1))
        dcb = dy * sb
        dga = dy * av * (sa * (1.0 - sa))
        dgb = dy * cv * (sb * (1.0 - sb))
        return dy * sa, dcb, dga, dgb, _colsum(dcb), _colsum(dga), _colsum(dgb)

    return _matmul_rows(w_mix_o, form="nt", a=dym, tm=min(WIDE_TILE, a.shape[0]), tk=D, fn=fn,
                        rows=[(a, D, 0), (cb, D, 0)] + _gates(z), consts=[], row_outs=[(BF16, D)] * 4,
                        acc_outs=[(1, D)] * 3, name=name)


def _local_step(x, target, mod, w_in, late, small, dist=None):
    sh_m, sc_m, gt_m, sh_f, sc_f, gt_f = mod
    t = x.shape[0]
    tmm = min(1024, t)
    late_swapped = [SWAPPED[n] for n in LATE]

    h1 = _pre_norm(x, small["g_pre_mix"], sc_m, sh_m, "pre_norm_mix")
    if callable(w_in):
        w_in = w_in(h1)
    z = _matmul(h1, w_in, form="nn", out_dtype=F32, tm=min(FUSED_TILE, t), tn=D_IN, tk=D, bias=small["b_in"], name="mm_in")
    conv = (z, small["w_dw_conv"], small["b_dw_conv"], small["g_conv_ln"], small["b_conv_ln"], "conv_branch")
    if dist is None:
        att = _attention(z, small["gen"], "attention")
        u1, u3 = _conv_branch(*conv)
        bufs = dict(late)
    else:
        mid = [n for n in LATE if n != "down"]
        mid_swapped = [SWAPPED[n] for n in mid]
        att, landed = _attention(z, small["gen"], "attention",
                                 carried=_carry_gather_ici([late[n] for n in mid], mid_swapped))
        (u1, u3), gathered = _conv_branch(*conv, carried=_carry_gather_forward(landed, mid_swapped))
        bufs = dict(zip(mid, gathered))
        bufs["down"] = late["down"]
    w = _weight_views(bufs)
    w["in"] = w_in
    a = _matmul(att, w["attn_o"], form="nn", out_dtype=F32, tm=tmm, tn=512, tk=512, name="mm_attn_o")
    cb = _matmul(u3, w["conv_o"], form="nn", out_dtype=F32, tm=tmm, tn=512, tk=512, bias=small["b_conv_o"], name="mm_conv_o")
    ym, y, x1, h2 = _mix_out(a, cb, z, x, w["mix_o"], small["g_post_mix"], gt_m, small["g_pre_ffn"], sc_f, sh_f, "mix_out")
    mm_up = dict(form="nn", out_dtype=F32, tm=min(FUSED_TILE, t), tn=2 * D_FF, tk=D, name="mm_up")
    ffn_act = (small["w_dw_ffn"], small["b_dw_ffn"], "ffn_act")
    if dist is None:
        up = _matmul(h2, w["up"], **mm_up)
        act = _ffn_act(up, *ffn_act)
    else:
        up, landed = _matmul(h2, w["up"], carried=_carry_gather_ici([late["down"]], [False]), **mm_up)
        act, down = _ffn_act(up, *ffn_act, carried=_carry_gather_forward(landed, [False]))
        w["down"] = down[0].reshape(D_FF, D)

    dx2, dyf, loss_cols, d_g_post_ffn, d_gt_f = _down_tail(act, w["down"], x1, target, small["g_post_ffn"], gt_f, "down_tail")
    dact = _matmul(dyf, w["down"], form="nt", out_dtype=F32, tm=tmm, tn=FF_BLOCK, tk=D, name="mm_down_dx")
    g_down = _matmul(act, dyf, form="tn", out_dtype=F32, tm=FF_BLOCK, tn=512, tk=t, name="mm_down_dw")
    dup, d_w_dw_ffn, d_b_dw_ffn = _ffn_act_bwd(dact, up, small["w_dw_ffn"], small["b_dw_ffn"], "ffn_act_bwd")
    dx1, dym, d_sh_f, d_sc_f, d_g_pre_ffn, d_g_post_mix, d_gt_m = _up_dx_tail(
        dup, w["up"], x1, dx2, ym, small["g_pre_ffn"], sc_f, small["g_post_mix"], gt_m, "up_dx_tail")
    g_up = _matmul(h2, dup, form="tn", out_dtype=F32, tm=512, tn=FF_BLOCK, tk=t, out_sharded=True, name="mm_up_dw")
    da, dcb, dgate_a, dgate_b, d_b_conv_o, sga, sgb = _mix_dx_gates(dym, w["mix_o"], a, cb, z, "mix_dx_gates")
    g_mix_o = _matmul(y, dym, form="tn", out_dtype=F32, tm=D, tn=512, tk=t, name="mm_mix_o_dw")
    datt = _matmul(da, w["attn_o"], form="nt", out_dtype=F32, tm=tmm, tn=512, tk=D, name="mm_attn_o_dx")
    g_attn_o = _matmul(att, da, form="tn", out_dtype=F32, tm=512, tn=256, tk=t, out_sharded=True, name="mm_attn_o_dw")
    du3 = _matmul(dcb, w["conv_o"], form="nt", out_dtype=F32, tm=tmm, tn=512, tk=D, name="mm_conv_o_dx")
    g_conv_o = _matmul(u3, dcb, form="tn", out_dtype=F32, tm=512, tn=256, tk=t, out_sharded=True, name="mm_conv_o_dw")
    big = {"attn_o": g_attn_o, "conv_o": g_conv_o, "mix_o": g_mix_o.reshape(4, 256, D),
           "up": g_up, "down": g_down.reshape(4, D_FF // 4, D)}
    conv_bwd = (du3, u1, z, small["w_dw_conv"], small["g_conv_ln"], small["b_conv_ln"], "conv_branch_bwd")
    in_dw = dict(form="tn", out_dtype=F32, tm=512, tn=IN_SHARD, tk=t, out_sharded=True, name="mm_in_dw")
    in_dx = dict(form="nt", out_dtype=F32, tm=min(WIDE_TILE, t), tn=D, tk=D_IN, name="mm_in_dx")
    if dist is None:
        dglu, d_w_dw_conv, d_b_dw_conv, d_g_conv_ln, d_b_conv_ln, sglu = _conv_branch_bwd(*conv_bwd)
        dq, dk, dv, sq, sk, sv, dgen = _attention_bwd(z, datt, small["gen"], "attention_bwd")
        dz = jnp.concatenate([dq, dk, dv, dglu, dgate_a, dgate_b], axis=1)
        big["in"] = _matmul(h1, dz, **in_dw)
        dh1 = _matmul(dz, w_in, **in_dx)
    else:
        early = [big[n] for n in EARLY_GRADS]
        (dglu, d_w_dw_conv, d_b_dw_conv, d_g_conv_ln, d_b_conv_ln, sglu), recv = _conv_branch_bwd(
            *conv_bwd, carried=_carry_pair_exchange(early))
        parts = _pair_sums(EARLY_GRADS, early, recv, dist)
        (dq, dk, dv, sq, sk, sv, dgen), from_chips = _attention_bwd(
            z, datt, small["gen"], "attention_bwd",
            carried=_carry_chip_exchange(parts, [SWAPPED[n] for n in EARLY_GRADS]))
        halves = _reduce_halves(EARLY_GRADS, parts, from_chips, dist)
        dz = jnp.concatenate([dq, dk, dv, dglu, dgate_a, dgate_b], axis=1)
        g_in, shards = _matmul(h1, dz, carried=_carry_pair_share(halves), **in_dw)
        big = dict(zip(EARLY_GRADS, shards))
        recv_in = _run_carried(_carry_pair_exchange([g_in]), "pair_exchange_in")
        part_in = _pair_sums(("in",), [g_in], recv_in, dist)
        dh1, from_chips_in = _matmul(dz, w_in, carried=_carry_chip_exchange(part_in, [False]), **in_dx)
        half_in = _reduce_halves(("in",), part_in, from_chips_in, dist)
        big["in"] = _run_carried(_carry_pair_share(half_in), "pair_share_in")[0]
    d_b_in = jnp.concatenate([sq, sk, sv, sglu, sga, sgb], axis=1)
    grad_x, d_sh_m, d_sc_m, d_g_pre_mix = _pre_norm_bwd(dh1, x, dx1, small["g_pre_mix"], sc_m, "pre_norm_mix_bwd")

    dmod = [d_sh_m, d_sc_m, d_gt_m, d_sh_f, d_sc_f, d_gt_f]
    sm = {"g_pre_mix": d_g_pre_mix, "g_post_mix": d_g_post_mix, "b_in": d_b_in, "gen": dgen,
          "w_dw_conv": d_w_dw_conv, "b_dw_conv": d_b_dw_conv, "g_conv_ln": d_g_conv_ln, "b_conv_ln": d_b_conv_ln,
          "b_conv_o": d_b_conv_o, "g_pre_ffn": d_g_pre_ffn, "g_post_ffn": d_g_post_ffn,
          "w_dw_ffn": d_w_dw_ffn, "b_dw_ffn": d_b_dw_ffn}
    return loss_cols, grad_x, dmod, big, sm


BIG = ("in", "attn_o", "conv_o", "mix_o", "up", "down")
SWAPPED = {"in": False, "attn_o": False, "conv_o": False, "mix_o": False, "up": True, "down": False}
SMALL_ORDER = ("b_ada", "g_pre_mix", "g_post_mix", "b_in", "rel_bias", "b_dw_conv", "g_conv_ln", "b_conv_ln",
               "b_conv_o", "g_pre_ffn", "g_post_ffn", "b_dw_ffn", "w_dw_conv", "w_dw_ffn")


def kernel(x, c, w_ada, b_ada, g_pre_mix, g_post_mix, w_in, b_in, rel_bias, w_attn_o, w_dw_conv, b_dw_conv, g_conv_ln, b_conv_ln, w_conv_o, b_conv_o, w_mix_o, g_pre_ffn, g_post_ffn, w_up, w_dw_ffn, b_dw_ffn, w_down, loss_target, m_w_ada, m_b_ada, m_g_pre_mix, m_g_post_mix, m_w_in, m_b_in, m_rel_bias, m_w_attn_o, m_w_dw_conv, m_b_dw_conv, m_g_conv_ln, m_b_conv_ln, m_w_conv_o, m_b_conv_o, m_w_mix_o, m_g_pre_ffn, m_g_post_ffn, m_w_up, m_w_dw_ffn, m_b_dw_ffn, m_w_down, v_w_ada, v_b_ada, v_g_pre_mix, v_g_post_mix, v_w_in, v_b_in, v_rel_bias, v_w_attn_o, v_w_dw_conv, v_b_dw_conv, v_g_conv_ln, v_b_conv_ln, v_w_conv_o, v_b_conv_o, v_w_mix_o, v_g_pre_ffn, v_g_post_ffn, v_w_up, v_w_dw_ffn, v_b_dw_ffn, v_w_down):
    given = dict(locals())
    ax, ay, ac = lax.axis_index("x"), lax.axis_index("y"), lax.axis_index("c")
    shard = 2 * ax + ay
    me = 4 * ax + 2 * ay + ac
    xs, target = x[0], loss_target[0]

    slots = {sw: _slot(ax, ay, sw).astype(jnp.int32).reshape(1) for sw in (False, True)}
    own = {"in": _to_bf16_slot(w_in[0], slots[False], "cast_in")}

    c_pad = jnp.pad(c, ((0, 7), (0, 0)))
    c_g, _ = _allgather8(c_pad, "gather_c")
    c_all = c_g[:, 0, :]
    b_ada_shard = lax.dynamic_slice(b_ada, (0, shard * ADA_SHARD), (1, ADA_SHARD))
    mod_shard, c_act = _ada_fwd(c_all, w_ada[0], b_ada_shard, "ada_fwd")
    small_in = [jnp.pad(mod_shard, ((0, 8), (0, 0))),
                jnp.pad(w_dw_conv[0], ((0, 1), (0, 0))),
                jnp.pad(w_dw_ffn[0], ((0, 13), (0, 0)))]
    mod_g, wdc_g, wdf_g = _gather_shards(small_in, [False, False, True], "gather_small")
    mod_all = jnp.transpose(mod_g[:, :8, :], (1, 0, 2)).reshape(8, 6 * D)
    in_send, in_recv, in_flight, token = _gather_ici_start(own["in"], mod_g, False, "gather_w_in_start")

    def w_in_ready(after):
        landed = _gather_ici_wait(in_send, in_recv, in_flight, after, False, "gather_w_in_wait")
        return _run_carried(_carry_gather_forward([landed], [False]), "gather_forward_in")[0]

    for n in LATE:
        own[n] = _to_bf16_slot(given["w_" + n][0], slots[SWAPPED[n]], "cast_" + n, after=token)
    mod_row = lax.dynamic_slice(mod_all, (me, 0), (1, 6 * D)) + token[0:1, 0:1]
    mod = [mod_row[:, k * D:(k + 1) * D] for k in range(6)]

    core = ac.astype(jnp.int32).reshape(1)
    dist = {"core": core, "slots": slots}

    sel = jnp.asarray(_toeplitz_map())
    rel_pad = jnp.pad(rel_bias[0], ((0, 0), (0, REL_PAD - (2 * MAX_REL + 1))))
    gen = _select_call(rel_pad, sel.T.astype(BF16), "bias_rows")
    small = {"g_pre_mix": g_pre_mix, "g_post_mix": g_post_mix, "b_in": b_in, "gen": gen,
             "w_dw_conv": _unshard_cols(wdc_g[:, :CONV_K, :]), "b_dw_conv": b_dw_conv, "g_conv_ln": g_conv_ln,
             "b_conv_ln": b_conv_ln, "b_conv_o": b_conv_o, "g_pre_ffn": g_pre_ffn, "g_post_ffn": g_post_ffn,
             "w_dw_ffn": _unshard_cols(wdf_g[:, :FFN_K, :]), "b_dw_ffn": _ff_swap(b_dw_ffn)}

    loss_cols, grad_x, dmod, reduced, sm = _local_step(xs, target, mod, w_in_ready, {n: own[n] for n in LATE}, small, dist)

    d_rel = _select_call(sm["gen"], sel.astype(BF16), "bias_fold")[:, :2 * MAX_REL + 1]
    small_grads = {"g_pre_mix": sm["g_pre_mix"], "g_post_mix": sm["g_post_mix"], "b_in": sm["b_in"], "rel_bias": d_rel[None],
                   "b_dw_conv": sm["b_dw_conv"], "g_conv_ln": sm["g_conv_ln"], "b_conv_ln": sm["b_conv_ln"],
                   "b_conv_o": sm["b_conv_o"], "g_pre_ffn": sm["g_pre_ffn"], "g_post_ffn": sm["g_post_ffn"],
                   "b_dw_ffn": _ff_swap(sm["b_dw_ffn"]), "w_dw_conv": sm["w_dw_conv"], "w_dw_ffn": _ff_swap(sm["w_dw_ffn"])}
    order = [n for n in SMALL_ORDER if n != "b_ada"]
    packed, offs = _pack([jnp.concatenate(dmod, axis=1)] + [small_grads[n] for n in order] + [loss_cols])
    mine = lax.dynamic_update_slice(jnp.zeros((8,) + packed.shape, F32), packed[None], (me, 0, 0))
    sg_send, sg_recv, sg_flight, sg_token = _all8_start(mine, "gather_small_grads_start")

    out = {}
    for n in BIG:
        g, dl, nm, nv = _adamw(given["w_" + n][0], reduced[n], given["m_w_" + n][0], given["v_w_" + n][0],
                               "adamw_" + n, sg_token)
        out["grad_w_" + n], out["delta_w_" + n], out["new_m_w_" + n], out["new_v_w_" + n] = g[None], dl[None], nm[None], nv[None]
    every = _all8_wait(sg_send, sg_recv, sg_flight, out["delta_w_in"], "gather_small_grads_wait")
    total = _sum8(every, "sum_small_grads")
    loss = jnp.sum(total[offs[-1]])
    offs = offs[:-1]
    dmod_all = every[:, 0:6, :].reshape(8, 6 * D)
    full_shapes = {n: given[n].shape for n in order}
    full_shapes["w_dw_conv"], full_shapes["w_dw_ffn"] = (1, CONV_K, 512), (1, FFN_K, 2 * D_FF)
    sums = dict(zip(order, _unpack(total, offs[1:], [full_shapes[n] for n in order])))
    sums["b_ada"] = total[0:6].reshape(1, 6 * D)
    sums["w_dw_conv"] = lax.dynamic_slice(sums["w_dw_conv"], (0, 0, shard * 128), (1, CONV_K, 128))
    sums["w_dw_ffn"] = lax.dynamic_slice(sums["w_dw_ffn"], (0, 0, shard * FF_BLOCK), (1, FFN_K, FF_BLOCK))

    upd = dict(zip(SMALL_ORDER, _adamw_many(
        [(given[n], sums[n], given["m_" + n], given["v_" + n]) for n in SMALL_ORDER], "adamw_small")))

    dmod_shard = lax.dynamic_slice(dmod_all, (0, shard * ADA_SHARD), (8, ADA_SHARD))
    ada = _ada_bwd_adamw(c_act.T, dmod_shard, w_ada[0], m_w_ada[0], v_w_ada[0], "ada_bwd_adamw")

    out.update({"grad_w_ada": ada[0][None], "delta_w_ada": ada[1][None], "new_m_w_ada": ada[2][None],
                "new_v_w_ada": ada[3][None]})
    for n in SMALL_ORDER:
        out["grad_" + n], out["delta_" + n], out["new_m_" + n], out["new_v_" + n] = sums[n], *upd[n]

    weights = ["w_ada", "b_ada", "g_pre_mix", "g_post_mix", "w_in", "b_in", "rel_bias", "w_attn_o", "w_dw_conv", "b_dw_conv",
               "g_conv_ln", "b_conv_ln", "w_conv_o", "b_conv_o", "w_mix_o", "g_pre_ffn", "g_post_ffn", "w_up", "w_dw_ffn",
               "b_dw_ffn", "w_down"]
    return (loss, grad_x[None], *[out["grad_" + n] for n in weights], *[out["delta_" + n] for n in weights],
            *[out["new_m_" + n] for n in weights], *[out["new_v_" + n] for n in weights])
```

```python
import functools
import math

import numpy as np
import jax
import jax.numpy as jnp
from jax import lax
from jax.experimental import pallas as pl
from jax.experimental.pallas import tpu as pltpu

F32, BF16 = jnp.float32, jnp.bfloat16
MESH = pl.DeviceIdType.MESH

D = 1024
D_IN = 4608
D_FF = 2816
N_CHIPS = 4
IN_SHARD = D_IN // N_CHIPS
ADA_SHARD = 6 * D // N_CHIPS
CONV_K = 31
FFN_K = 3
N_HEADS = 8
CHUNK = 64
LEFT_CHUNKS = 8
MAX_REL = 128
EPS = 1e-6
NEG_INF = -1e30
Q_TILE = 256
WINDOW = Q_TILE + LEFT_CHUNKS * CHUNK
STEP_ROWS = 256
REL_PAD = 384
TOEP = 1024
ROW_TILE = 256
WORK_TILE = 512
VMEM_LIMIT = 60 * 1024 * 1024

ADAM_LR, ADAM_B1, ADAM_B2, ADAM_EPS, ADAM_WD, ADAM_STEP = 0.001, 0.9, 0.999, 1e-08, 0.01, 10


def _params(sem=None):
    return pltpu.CompilerParams(dimension_semantics=sem, vmem_limit_bytes=VMEM_LIMIT)


def _sds(shape, dtype):
    return jax.ShapeDtypeStruct(tuple(shape), dtype)


ANY = pl.BlockSpec(memory_space=pl.ANY)


class _Carried:
    def __init__(self, ins, out_shapes, aliases, n_sems, start, finish):
        self.ins, self.out_shapes, self.aliases = list(ins), list(out_shapes), dict(aliases)
        self.n_sems, self.start, self.finish = n_sems, start, finish


def _call(body, *, grid, in_specs, out_specs, out_shape, scratch_shapes, sem, name, args, carried=None):
    in_specs, out_specs, out_shape = list(in_specs), list(out_specs), list(out_shape)
    scratch_shapes = list(scratch_shapes)
    if carried is None:
        return pl.pallas_call(body, grid=grid, in_specs=in_specs, out_specs=out_specs, out_shape=out_shape,
                              scratch_shapes=scratch_shapes, compiler_params=_params(sem), name=name)(*args)
    n_in, n_out, n_scr = len(in_specs), len(out_specs), len(scratch_shapes)
    c_in, c_out = len(carried.ins), len(carried.out_shapes)

    def full(*refs):
        pos = [0]

        def take(k):
            part = refs[pos[0]:pos[0] + k]
            pos[0] += k
            return part

        ins, cins, outs, couts, scr = take(n_in), take(c_in), take(n_out), take(c_out), take(n_scr)
        send_sems, recv_sems = take(2)
        first = last = None
        for d, size in enumerate(grid):
            pid = pl.program_id(d)
            first = (pid == 0) if first is None else first & (pid == 0)
            last = (pid == size - 1) if last is None else last & (pid == size - 1)

        @pl.when(first)
        def _():
            carried.start(cins, couts, send_sems, recv_sems)

        body(*ins, *outs, *scr)

        @pl.when(last)
        def _():
            carried.finish(cins, couts, send_sems, recv_sems)

    sems = [pltpu.SemaphoreType.DMA((carried.n_sems,)), pltpu.SemaphoreType.DMA((carried.n_sems,))]
    return pl.pallas_call(
        full, grid=grid, in_specs=in_specs + [ANY] * c_in, out_specs=out_specs + [ANY] * c_out,
        out_shape=out_shape + carried.out_shapes, scratch_shapes=scratch_shapes + sems,
        input_output_aliases={n_in + k: n_out + v for k, v in carried.aliases.items()},
        compiler_params=_params(tuple("arbitrary" for _ in grid)), name=name,
    )(*args, *carried.ins)


def _run_carried(carried, name):
    c_in = len(carried.ins)

    def body(*refs):
        cins, couts = refs[:c_in], refs[c_in:c_in + len(carried.out_shapes)]
        send_sems, recv_sems = refs[-2:]
        carried.start(cins, couts, send_sems, recv_sems)
        carried.finish(cins, couts, send_sems, recv_sems)

    return pl.pallas_call(
        body, in_specs=[ANY] * c_in, out_specs=[ANY] * len(carried.out_shapes), out_shape=carried.out_shapes,
        scratch_shapes=[pltpu.SemaphoreType.DMA((carried.n_sems,)), pltpu.SemaphoreType.DMA((carried.n_sems,))],
        input_output_aliases=carried.aliases, name=name,
    )(*carried.ins)


def _matmul(a, b, *, form, out_dtype, tm, tn, tk, name, bias=None, add=None, out_sharded=False, carried=None):
    b3 = b.ndim == 3
    resident = 0
    if form == "nn":
        m, k = a.shape
        n = b.shape[0] * b.shape[2] if b3 else b.shape[1]
        dn = (((1,), (0,)), ((), ()))
        a_spec = pl.BlockSpec((tm, tk), lambda i, j, kk: (i, kk))
        if b3 and tn == n and tk == k:
            resident = b.shape[0]
            b_spec = pl.BlockSpec(b.shape, lambda i, j, kk: (0, 0, 0))
        else:
            b_spec = (pl.BlockSpec((None, tk, tn), lambda i, j, kk: (j, kk, 0)) if b3
                      else pl.BlockSpec((tk, tn), lambda i, j, kk: (kk, j)))
    elif form == "nt":
        m, k = a.shape
        n = b.shape[1] if b3 else b.shape[0]
        dn = (((1,), (1,)), ((), ()))
        a_spec = pl.BlockSpec((tm, tk), lambda i, j, kk: (i, kk))
        if b3 and tk == k:
            resident = b.shape[0]
            b_spec = pl.BlockSpec((resident, tn, b.shape[2]), lambda i, j, kk: (0, j, 0))
        else:
            b_spec = (pl.BlockSpec((None, tn, tk), lambda i, j, kk: (kk, j, 0)) if b3
                      else pl.BlockSpec((tn, tk), lambda i, j, kk: (j, kk)))
    else:
        k, m = a.shape
        n = b.shape[1]
        dn = (((0,), (0,)), ((), ()))
        a_spec = pl.BlockSpec((tk, tm), lambda i, j, kk: (kk, i))
        b_spec = pl.BlockSpec((tk, tn), lambda i, j, kk: (kk, j))
    assert m % tm == 0 and n % tn == 0 and k % tk == 0, (name, m, n, k, tm, tn, tk)
    nk = k // tk
    in_specs, args = [a_spec, b_spec], [a, b]
    if bias is not None:
        in_specs.append(pl.BlockSpec((1, tn), lambda i, j, kk: (0, j)))
        args.append(bias)
    if add is not None:
        in_specs.append(pl.BlockSpec((tm, tn), lambda i, j, kk: (i, j)))
        args.append(add)
    if out_sharded:
        out_shape = _sds((n // tn, m, tn), out_dtype)
        out_spec = pl.BlockSpec((None, tm, tn), lambda i, j, kk: (j, i, 0))
    else:
        out_shape = _sds((m, n), out_dtype)
        out_spec = pl.BlockSpec((tm, tn), lambda i, j, kk: (i, j))

    def body(*refs):
        a_ref, b_ref = refs[0], refs[1]
        pos = 2
        bias_ref = add_ref = None
        if bias is not None:
            bias_ref, pos = refs[pos], pos + 1
        if add is not None:
            add_ref, pos = refs[pos], pos + 1
        o_ref = refs[pos]
        if resident and form == "nn":
            ns = b_ref.shape[2]
            for s in range(resident):
                cols = slice(s * ns, (s + 1) * ns)
                ps = lax.dot_general(a_ref[...], b_ref[s], dn, preferred_element_type=F32)
                if bias_ref is not None:
                    ps = ps + bias_ref[:, cols]
                o_ref[:, cols] = ps.astype(o_ref.dtype)
            return
        if resident:
            ks = b_ref.shape[2]
            p = None
            for s in range(resident):
                ps = lax.dot_general(a_ref[:, s * ks:(s + 1) * ks], b_ref[s], dn, preferred_element_type=F32)
                p = ps if p is None else p + ps
        else:
            av, bv = a_ref[...], b_ref[...]
            if av.dtype != BF16:
                av = av.astype(BF16)
            if bv.dtype != BF16:
                bv = bv.astype(BF16)
            p = lax.dot_general(av, bv, dn, preferred_element_type=F32)

        def finish(acc):
            if bias_ref is not None:
                acc = acc + bias_ref[...]
            if add_ref is not None:
                acc = acc + add_ref[...]
            o_ref[...] = acc.astype(o_ref.dtype)

        if nk == 1:
            finish(p)
        else:
            acc_ref = refs[pos + 1]
            kk = pl.program_id(2)

            @pl.when(kk == 0)
            def _():
                acc_ref[...] = p

            @pl.when(kk > 0)
            def _():
                acc_ref[...] += p

            @pl.when(kk == nk - 1)
            def _():
                finish(acc_ref[...])

    res = _call(body, grid=(m // tm, n // tn, nk), in_specs=in_specs, out_specs=[out_spec], out_shape=[out_shape],
                scratch_shapes=[pltpu.VMEM((tm, tn), F32)] if nk > 1 else [],
                sem=("parallel", "parallel", "arbitrary"), name=name, args=args, carried=carried)
    return res[0] if carried is None else (res[0], res[1:])


def _rowcall(fn, rows, consts, row_outs, acc_outs, *, name, tm=ROW_TILE, col_grid=1):
    n_rows = rows[0][0].shape[0]
    assert n_rows % tm == 0
    grid = (col_grid, n_rows // tm)
    in_specs = [pl.BlockSpec((tm, w), functools.partial(lambda c, i, cb: (i, cb + c), cb=cb)) for _, w, cb in rows]
    in_specs += [pl.BlockSpec(k.shape, functools.partial(lambda c, i, nd: (0,) * nd, nd=k.ndim)) for k in consts]
    out_specs = [pl.BlockSpec((tm, w), lambda c, i: (i, c)) for _, _, _, w in row_outs]
    out_specs += [pl.BlockSpec((r, w), lambda c, i: (0, c)) for r, _, w in acc_outs]
    out_shape = [_sds((nr, nc), dt) for nr, nc, dt, _ in row_outs] + [_sds((r, nc), F32) for r, nc, _ in acc_outs]
    n_in, n_ro = len(rows) + len(consts), len(row_outs)

    def body(*refs):
        res = fn(*[r[...] for r in refs[:n_in]])
        if not isinstance(res, (tuple, list)):
            res = (res,)
        outs = refs[n_in:]
        for o_ref, val in zip(outs[:n_ro], res[:n_ro]):
            o_ref[...] = val.astype(o_ref.dtype)
        if acc_outs:
            first = pl.program_id(1) == 0

            @pl.when(first)
            def _():
                for o_ref, val in zip(outs[n_ro:], res[n_ro:]):
                    o_ref[...] = val

            @pl.when(jnp.logical_not(first))
            def _():
                for o_ref, val in zip(outs[n_ro:], res[n_ro:]):
                    o_ref[...] += val

    out = pl.pallas_call(
        body, grid=grid, in_specs=in_specs, out_specs=out_specs, out_shape=out_shape,
        compiler_params=_params(("arbitrary", "arbitrary")), name=name,
    )(*[r[0] for r in rows], *consts)
    return out


def _matmul_rows(b, *, form, tm, tk, fn, rows, consts, row_outs, acc_outs, name, a=None, a_rows=None, a_fn=None,
                 carried=None):
    b3 = b.ndim == 3
    resident = 0
    if form == "nn":
        k, n = b.shape
        b_spec = pl.BlockSpec((tk, n), lambda i, kk: (kk, 0))
        dn = (((1,), (0,)), ((), ()))
    else:
        n = b.shape[1] if b3 else b.shape[0]
        k = b.shape[0] * b.shape[2] if b3 else b.shape[1]
        if b3 and tk == k:
            resident = b.shape[0]
            b_spec = pl.BlockSpec(b.shape, lambda i, kk: (0, 0, 0))
        else:
            b_spec = (pl.BlockSpec((None, n, tk), lambda i, kk: (kk, 0, 0)) if b3
                      else pl.BlockSpec((n, tk), lambda i, kk: (0, kk)))
        dn = (((1,), (1,)), ((), ()))
    nk = k // tk
    lhs_in = [(a, tk, 0)] if a is not None else list(a_rows)
    assert a is not None or nk == 1
    m = lhs_in[0][0].shape[0]
    n_lhs = len(lhs_in)
    in_specs = [pl.BlockSpec((tm, tk), lambda i, kk: (i, kk))] if a is not None else [
        pl.BlockSpec((tm, w), functools.partial(lambda i, kk, cb: (i, cb), cb=cb)) for _, w, cb in a_rows]
    in_specs.append(b_spec)
    in_specs += [pl.BlockSpec((tm, w), functools.partial(lambda i, kk, cb: (i, cb), cb=cb)) for _, w, cb in rows]
    in_specs += [pl.BlockSpec(c.shape, functools.partial(lambda i, kk, nd: (0,) * nd, nd=c.ndim)) for c in consts]
    out_specs = [pl.BlockSpec((tm, w), lambda i, kk: (i, 0)) for _, w in row_outs]
    out_specs += [pl.BlockSpec((r, w), lambda i, kk: (0, 0)) for r, w in acc_outs]
    out_shape = [_sds((m, w), dt) for dt, w in row_outs] + [_sds((r, w), F32) for r, w in acc_outs]
    n_rows, n_consts, n_ro, n_acc = len(rows), len(consts), len(row_outs), len(acc_outs)

    def body(*refs):
        pos = n_lhs + 1
        row_refs, const_refs = refs[pos:pos + n_rows], refs[pos + n_rows:pos + n_rows + n_consts]
        pos += n_rows + n_consts
        out_refs, acc_refs = refs[pos:pos + n_ro], refs[pos + n_ro:pos + n_ro + n_acc]
        i, kk = pl.program_id(0), pl.program_id(1)
        if resident:
            b_ref, ks, p = refs[n_lhs], b.shape[2], None
            for s in range(resident):
                ps = lax.dot_general(refs[0][:, s * ks:(s + 1) * ks], b_ref[s], dn, preferred_element_type=F32)
                p = ps if p is None else p + ps
        else:
            lhs = refs[0][...] if a is not None else a_fn(*[r[...] for r in refs[:n_lhs]]).astype(BF16)
            p = lax.dot_general(lhs, refs[n_lhs][...], dn, preferred_element_type=F32)

        def finish(acc):
            extra = [r[...] for r in row_refs] + [c[...] for c in const_refs]
            res = fn(acc, lhs, *extra) if a is None else fn(acc, *extra)
            for o_ref, val in zip(out_refs, res[:n_ro]):
                o_ref[...] = val.astype(o_ref.dtype)
            if n_acc:
                @pl.when(i == 0)
                def _():
                    for o_ref, val in zip(acc_refs, res[n_ro:]):
                        o_ref[...] = val

                @pl.when(i > 0)
                def _():
                    for o_ref, val in zip(acc_refs, res[n_ro:]):
                        o_ref[...] += val

        if nk == 1:
            finish(p)
        else:
            acc_ref = refs[pos + n_ro + n_acc]

            @pl.when(kk == 0)
            def _():
                acc_ref[...] = p

            @pl.when(kk > 0)
            def _():
                acc_ref[...] += p

            @pl.when(kk == nk - 1)
            def _():
                finish(acc_ref[...])

    res = _call(body, grid=(m // tm, nk), in_specs=in_specs, out_specs=out_specs, out_shape=out_shape,
                scratch_shapes=[pltpu.VMEM((tm, n), F32)] if nk > 1 else [], sem=("arbitrary", "arbitrary"),
                name=name, args=[r[0] for r in lhs_in] + [b] + [r[0] for r in rows] + list(consts), carried=carried)
    own = n_ro + n_acc
    return res[:own] if carried is None else (res[:own], res[own:])


def _colsum(v):
    return jnp.sum(v, axis=0, keepdims=True)


def _sigmoid(v):
    return 1.0 / (1.0 + jnp.exp(-v))


_GELU_C = math.sqrt(2.0 / math.pi)


def _gelu(v):
    return 0.5 * v * (1.0 + jnp.tanh(_GELU_C * (v + 0.044715 * (v * v * v))))


def _gelu_and_grad(v):
    th = jnp.tanh(_GELU_C * (v + 0.044715 * (v * v * v)))
    g = 0.5 * v * (1.0 + th)
    dg = 0.5 * (1.0 + th) + 0.5 * v * (1.0 - th * th) * (_GELU_C * (1.0 + 3.0 * 0.044715 * (v * v)))
    return g, dg


def _rms_stats(v):
    r = lax.rsqrt(jnp.mean(v * v, axis=-1, keepdims=True) + EPS)
    return v * r, r


def _rms_bwd(dn, vn, r):
    return r * (dn - vn * jnp.mean(dn * vn, axis=-1, keepdims=True))


def _pre_norm(x, g, sc, sh, name):
    def fn(xv, gv, scv, shv):
        xn, _ = _rms_stats(xv)
        return (xn * gv) * (1.0 + scv) + shv
    return _rowcall(fn, [(x, D, 0)], [g, sc, sh], [(x.shape[0], D, BF16, D)], [], name=name,
                    tm=min(2 * ROW_TILE, x.shape[0]))[0]


def _pre_norm_bwd(dh, x, dx_other, g, sc, name):
    def fn(dhv, xv, dov, gv, scv):
        xn, r = _rms_stats(xv)
        yn = xn * gv
        dyn = dhv * (1.0 + scv)
        dx = _rms_bwd(dyn * gv, xn, r)
        return dov + dx, _colsum(dhv), _colsum(dhv * yn), _colsum(dyn * xn)
    t = x.shape[0]
    return _rowcall(fn, [(dh, D, 0), (x, D, 0), (dx_other, D, 0)], [g, sc], [(t, D, F32, D)],
                    [(1, D, D)] * 3, name=name, tm=min(2 * ROW_TILE, t))


CONV_HALO = 32


def _layer_norm_parts(u):
    mu = jnp.mean(u, axis=-1, keepdims=True)
    d = u - mu
    r = lax.rsqrt(jnp.mean(d * d, axis=-1, keepdims=True) + EPS)
    return d * r, r


LANES = 128
SUBLANE_ROWS = 8
CONV_ROWS = 64


def _lanes(c):
    return slice(c * LANES, (c + 1) * LANES)


def _conv_branch(z, w_dw, b_dw, g_ln, b_ln, name, tm=WORK_TILE, carried=None):
    t = z.shape[0]
    per = tm // CONV_HALO
    n_chunks = 512 // LANES

    def body(ga_ref, gb_ref, gah_ref, gbh_ref, w_ref, b_ref, g_ref, bl_ref, u1_ref, u3_ref, scr):
        i = pl.program_id(0)
        u0h = jnp.where(i > 0, gah_ref[...] * _sigmoid(gbh_ref[...]), 0.0)
        u0 = ga_ref[...] * _sigmoid(gb_ref[...])
        for c in range(n_chunks):
            scr[c, 0:CONV_HALO, :] = u0h[:, _lanes(c)]
            scr[c, CONV_HALO:CONV_HALO + tm, :] = u0[:, _lanes(c)]
        for c in range(n_chunks):
            for r0 in range(0, tm, CONV_ROWS):
                acc = jnp.zeros((CONV_ROWS, LANES), F32) + b_ref[:, _lanes(c)]
                for j in range(CONV_K):
                    acc = acc + w_ref[j:j + 1, _lanes(c)] * scr[c, pl.ds(r0 + CONV_HALO - (CONV_K - 1) + j, CONV_ROWS), :]
                u1_ref[r0:r0 + CONV_ROWS, _lanes(c)] = acc
        xh, _ = _layer_norm_parts(u1_ref[...])
        u2 = xh * g_ref[...] + bl_ref[...]
        u3_ref[...] = (u2 * _sigmoid(u2)).astype(BF16)

    cur = lambda cb: pl.BlockSpec((tm, 512), lambda i: (i, cb))
    halo = lambda cb: pl.BlockSpec((CONV_HALO, 512), lambda i: (jnp.maximum(i * per - 1, 0), cb))
    whole = lambda a: pl.BlockSpec(a.shape, lambda i: (0, 0))
    res = _call(
        body, grid=(t // tm,),
        in_specs=[cur(3), cur(4), halo(3), halo(4), whole(w_dw), whole(b_dw), whole(g_ln), whole(b_ln)],
        out_specs=[pl.BlockSpec((tm, 512), lambda i: (i, 0))] * 2,
        out_shape=[_sds((t, 512), F32), _sds((t, 512), BF16)],
        scratch_shapes=[pltpu.VMEM((n_chunks, CONV_HALO + tm, LANES), F32)],
        sem=("arbitrary",), name=name, args=(z, z, z, z, w_dw, b_dw, g_ln, b_ln), carried=carried)
    return res[:2] if carried is None else (res[:2], res[2:])


def _conv_branch_bwd(du3, u1, z, w_dw, g_ln, b_ln, name, tm=WORK_TILE, carried=None):
    t = z.shape[0]
    per = tm // CONV_HALO
    last = t // tm - 1
    n_chunks = 512 // LANES

    def du1_of(du3v, u1v, g, b):
        xh, r = _layer_norm_parts(u1v)
        u2 = xh * g + b
        s = _sigmoid(u2)
        du2 = du3v * (s * (1.0 + u2 * (1.0 - s)))
        dxh = du2 * g
        du1 = r * (dxh - jnp.mean(dxh, axis=-1, keepdims=True) - xh * jnp.mean(dxh * xh, axis=-1, keepdims=True))
        return du1, du2, xh

    def body(d_ref, u_ref, dn_ref, un_ref, ga_ref, gb_ref, gah_ref, gbh_ref, w_ref, g_ref, bl_ref,
             dglu_ref, dw_ref, dbdw_ref, dg_ref, dbl_ref, dbin_ref, scr, scd):
        i = pl.program_id(0)
        g, b = g_ref[...], bl_ref[...]
        du1, du2, xh = du1_of(d_ref[...], u_ref[...], g, b)
        du1n, _, _ = du1_of(dn_ref[...], un_ref[...], g, b)
        du1n = jnp.where(i < last, du1n, 0.0)
        sgb = _sigmoid(gb_ref[...])
        ga = ga_ref[...]
        u0 = ga * sgb
        u0h = jnp.where(i > 0, gah_ref[...] * _sigmoid(gbh_ref[...]), 0.0)
        for c in range(n_chunks):
            scd[c, 0:tm, :] = du1[:, _lanes(c)]
            scd[c, tm:tm + CONV_HALO, :] = du1n[:, _lanes(c)]
            scr[c, 0:CONV_HALO, :] = u0h[:, _lanes(c)]
            scr[c, CONV_HALO:CONV_HALO + tm, :] = u0[:, _lanes(c)]

        @pl.when(i == 0)
        def _():
            for ref in (dw_ref, dbdw_ref, dg_ref, dbl_ref, dbin_ref):
                ref[...] = jnp.zeros_like(ref)

        dsg = ga * (sgb * (1.0 - sgb))
        for c in range(n_chunks):
            gate = slice(512 + c * LANES, 512 + (c + 1) * LANES)
            for r0 in range(0, tm, CONV_ROWS):
                rows = slice(r0, r0 + CONV_ROWS)
                du0 = jnp.zeros((CONV_ROWS, LANES), F32)
                for j in range(CONV_K):
                    du0 = du0 + w_ref[j:j + 1, _lanes(c)] * scd[c, pl.ds(r0 + CONV_K - 1 - j, CONV_ROWS), :]
                dga = du0 * sgb[rows, _lanes(c)]
                dgb = du0 * dsg[rows, _lanes(c)]
                dglu_ref[rows, _lanes(c)] = dga.astype(BF16)
                dglu_ref[rows, gate] = dgb.astype(BF16)
                dbin_ref[:, _lanes(c)] += _colsum(dga)
                dbin_ref[:, gate] += _colsum(dgb)
            for j in range(CONV_K):
                dwj = jnp.zeros((SUBLANE_ROWS, LANES), F32)
                for r0 in range(0, tm, CONV_ROWS):
                    prod = (scd[c, pl.ds(r0, CONV_ROWS), :]
                            * scr[c, pl.ds(r0 + CONV_HALO - (CONV_K - 1) + j, CONV_ROWS), :])
                    dwj = dwj + jnp.sum(prod.reshape(CONV_ROWS // SUBLANE_ROWS, SUBLANE_ROWS, LANES), axis=0)
                dw_ref[j:j + 1, _lanes(c)] += _colsum(dwj)
        dbdw_ref[...] += _colsum(du1)
        dg_ref[...] += _colsum(du2 * xh)
        dbl_ref[...] += _colsum(du2)

    cur = lambda cb: pl.BlockSpec((tm, 512), lambda i: (i, cb))
    prev = lambda cb: pl.BlockSpec((CONV_HALO, 512), lambda i: (jnp.maximum(i * per - 1, 0), cb))
    nxt = pl.BlockSpec((CONV_HALO, 512), lambda i: (jnp.minimum((i + 1) * per, t // CONV_HALO - 1), 0))
    whole = lambda a: pl.BlockSpec(a.shape, lambda i: (0, 0))
    acc = lambda r, w: pl.BlockSpec((r, w), lambda i: (0, 0))
    res = _call(
        body, grid=(t // tm,),
        in_specs=[cur(0), cur(0), nxt, nxt, cur(3), cur(4), prev(3), prev(4), whole(w_dw), whole(g_ln), whole(b_ln)],
        out_specs=[pl.BlockSpec((tm, 1024), lambda i: (i, 0)), acc(CONV_K, 512), acc(1, 512), acc(1, 512),
                   acc(1, 512), acc(1, 1024)],
        out_shape=[_sds((t, 1024), BF16), _sds((CONV_K, 512), F32), _sds((1, 512), F32), _sds((1, 512), F32),
                   _sds((1, 512), F32), _sds((1, 1024), F32)],
        scratch_shapes=[pltpu.VMEM((n_chunks, CONV_HALO + tm, LANES), F32),
                        pltpu.VMEM((n_chunks, tm + CONV_HALO, LANES), F32)],
        sem=("arbitrary",), name=name, args=(du3, u1, du3, u1, z, z, z, z, w_dw, g_ln, b_ln), carried=carried)
    return res[:6] if carried is None else (res[:6], res[6:])


FF_BLOCK = D_FF // 2
FF_HALO = 8
FF_CHUNKS = FF_BLOCK // LANES


FF_ROWS = 64


def _ext_rows(ext):
    return next(d for d in (104, 88, 72, 56, 40, 24, 8) if ext % d == 0)


def _ffn_conv(w_ref, b_ref, scr, k, rows, r0=0):
    acc = b_ref[:, _lanes(k)] + w_ref[0:1, _lanes(k)] * scr[k, pl.ds(r0 + FF_HALO - 2, rows), :]
    acc = acc + w_ref[1:2, _lanes(k)] * scr[k, pl.ds(r0 + FF_HALO - 1, rows), :]
    return acc + w_ref[2:3, _lanes(k)] * scr[k, pl.ds(r0 + FF_HALO, rows), :]


def _ffn_act(up, w3, b3, name, tm=WORK_TILE, carried=None):
    t = up.shape[0]
    per = tm // FF_HALO
    wide = 2 * FF_BLOCK

    def body(u_ref, uh_ref, w_ref, b_ref, o_ref, scr):
        i = pl.program_id(1)
        for k in range(2 * FF_CHUNKS):
            scr[k, 0:FF_HALO, :] = jnp.where(i > 0, uh_ref[:, _lanes(k)], 0.0)
            scr[k, FF_HALO:FF_HALO + tm, :] = u_ref[:, _lanes(k)]
        for cc in range(FF_CHUNKS):
            for r0 in range(0, tm, FF_ROWS):
                val = _ffn_conv(w_ref, b_ref, scr, cc, FF_ROWS, r0)
                gate = _ffn_conv(w_ref, b_ref, scr, FF_CHUNKS + cc, FF_ROWS, r0)
                o_ref[r0:r0 + FF_ROWS, _lanes(cc)] = (_gelu(gate) * val).astype(BF16)

    res = _call(
        body, grid=(2, t // tm),
        in_specs=[pl.BlockSpec((tm, wide), lambda c, i: (i, c)),
                  pl.BlockSpec((FF_HALO, wide), lambda c, i: (jnp.maximum(i * per - 1, 0), c)),
                  pl.BlockSpec((FFN_K, wide), lambda c, i: (0, c)),
                  pl.BlockSpec((1, wide), lambda c, i: (0, c))],
        out_specs=[pl.BlockSpec((tm, FF_BLOCK), lambda c, i: (i, c))],
        out_shape=[_sds((t, D_FF), BF16)],
        scratch_shapes=[pltpu.VMEM((2 * FF_CHUNKS, FF_HALO + tm, LANES), F32)],
        sem=("arbitrary", "arbitrary"), name=name, args=(up, up, w3, b3), carried=carried)
    return res[0] if carried is None else (res[0], res[1:])


def _ffn_act_bwd(dact, up, w3, b3, name, tm=WORK_TILE):
    t = up.shape[0]
    per = tm // FF_HALO
    wide = 2 * FF_BLOCK
    last = t // tm - 1
    ext = tm + FF_HALO
    FF_EXT_ROWS = _ext_rows(ext)

    def body(u_ref, up_ref, un_ref, d_ref, dn_ref, w_ref, b_ref, o_ref, dw_ref, db_ref, scr, scd):
        i = pl.program_id(1)
        for k in range(2 * FF_CHUNKS):
            scr[k, 0:FF_HALO, :] = jnp.where(i > 0, up_ref[:, _lanes(k)], 0.0)
            scr[k, FF_HALO:FF_HALO + tm, :] = u_ref[:, _lanes(k)]
            scr[k, FF_HALO + tm:FF_HALO + ext, :] = un_ref[:, _lanes(k)]
        dn = jnp.where(i < last, dn_ref[...], 0.0)

        @pl.when(i == 0)
        def _():
            dw_ref[...] = jnp.zeros_like(dw_ref)
            db_ref[...] = jnp.zeros_like(db_ref)

        for cc in range(FF_CHUNKS):
            gc = FF_CHUNKS + cc
            for r0 in range(0, ext, FF_EXT_ROWS):
                rows = pl.ds(r0, FF_EXT_ROWS)
                val = _ffn_conv(w_ref, b_ref, scr, cc, FF_EXT_ROWS, r0)
                gel, dgel = _gelu_and_grad(_ffn_conv(w_ref, b_ref, scr, gc, FF_EXT_ROWS, r0))
                da = d_ref[r0:r0 + FF_EXT_ROWS, _lanes(cc)] if r0 + FF_EXT_ROWS <= tm else jnp.concatenate(
                    [d_ref[r0:tm, _lanes(cc)], dn[:, _lanes(cc)]], axis=0)
                scd[cc, rows, :] = da * gel
                scd[gc, rows, :] = da * val * dgel
            for k in (cc, gc):
                dwk = [jnp.zeros((SUBLANE_ROWS, LANES), F32) for _ in range(FFN_K)]
                dbk = jnp.zeros((SUBLANE_ROWS, LANES), F32)
                for r0 in range(0, tm, FF_ROWS):
                    shifted = [scd[k, pl.ds(r0 + FFN_K - 1 - j, FF_ROWS), :] for j in range(FFN_K)]
                    ucur = scr[k, pl.ds(r0 + FF_HALO, FF_ROWS), :]
                    o_ref[r0:r0 + FF_ROWS, _lanes(k)] = (
                        w_ref[0:1, _lanes(k)] * shifted[0] + w_ref[1:2, _lanes(k)] * shifted[1]
                        + w_ref[2:3, _lanes(k)] * shifted[2]).astype(BF16)
                    fold = lambda v: jnp.sum(v.reshape(FF_ROWS // SUBLANE_ROWS, SUBLANE_ROWS, LANES), axis=0)
                    for j in range(FFN_K):
                        dwk[j] = dwk[j] + fold(shifted[j] * ucur)
                    dbk = dbk + fold(shifted[FFN_K - 1])
                for j in range(FFN_K):
                    dw_ref[j:j + 1, _lanes(k)] += _colsum(dwk[j])
                db_ref[:, _lanes(k)] += _colsum(dbk)

    nblk = t // FF_HALO
    return pl.pallas_call(
        body, grid=(2, t // tm),
        in_specs=[pl.BlockSpec((tm, wide), lambda c, i: (i, c)),
                  pl.BlockSpec((FF_HALO, wide), lambda c, i: (jnp.maximum(i * per - 1, 0), c)),
                  pl.BlockSpec((FF_HALO, wide), lambda c, i: (jnp.minimum((i + 1) * per, nblk - 1), c)),
                  pl.BlockSpec((tm, FF_BLOCK), lambda c, i: (i, c)),
                  pl.BlockSpec((FF_HALO, FF_BLOCK), lambda c, i: (jnp.minimum((i + 1) * per, nblk - 1), c)),
                  pl.BlockSpec((FFN_K, wide), lambda c, i: (0, c)),
                  pl.BlockSpec((1, wide), lambda c, i: (0, c))],
        out_specs=[pl.BlockSpec((tm, wide), lambda c, i: (i, c)),
                   pl.BlockSpec((FFN_K, wide), lambda c, i: (0, c)),
                   pl.BlockSpec((1, wide), lambda c, i: (0, c))],
        out_shape=[_sds((t, 2 * D_FF), BF16), _sds((FFN_K, 2 * D_FF), F32), _sds((1, 2 * D_FF), F32)],
        scratch_shapes=[pltpu.VMEM((2 * FF_CHUNKS, FF_HALO + ext, LANES), F32),
                        pltpu.VMEM((2 * FF_CHUNKS, ext, LANES), F32)],
        compiler_params=_params(("arbitrary", "arbitrary")), name=name,
    )(up, up, up, dact, dact, w3, b3)


def _toeplitz_map():
    f = np.zeros((TOEP, REL_PAD), np.float32)
    for m in range(TOEP - 1):
        rel = (WINDOW - 1) - m
        f[m, int(np.clip(rel, -MAX_REL, MAX_REL)) + MAX_REL] = 1.0
    return f


def _split3(v):
    hi = v.astype(BF16)
    r1 = v - hi.astype(F32)
    mid = r1.astype(BF16)
    lo = (r1 - mid.astype(F32)).astype(BF16)
    return hi, mid, lo


def _exact_select(v, sel):
    out = None
    for part in _split3(v):
        p = jnp.dot(part, sel, preferred_element_type=F32)
        out = p if out is None else out + p
    return out


def _select_call(v, sel, name):
    def body(v_ref, s_ref, o_ref):
        o_ref[...] = _exact_select(v_ref[...], s_ref[...])
    return pl.pallas_call(body, out_shape=_sds((v.shape[0], sel.shape[1]), F32), name=name)(v, sel)


def _band_bias(gen_row):
    b0 = jnp.broadcast_to(gen_row, (Q_TILE, TOEP))
    bias = pltpu.roll(b0, TOEP - (Q_TILE - 1), 1, stride=1, stride_axis=0)[:, :WINDOW]
    qq = lax.broadcasted_iota(jnp.int32, (Q_TILE, WINDOW), 0) // CHUNK
    kc = lax.broadcasted_iota(jnp.int32, (Q_TILE, WINDOW), 1) // CHUNK
    return jnp.where((kc >= qq) & (kc <= qq + LEFT_CHUNKS), bias, NEG_INF)


PAD_ROWS = WINDOW - Q_TILE
NT_DIMS = (((1,), (1,)), ((), ()))
TN_DIMS = (((0,), (0,)), ((), ()))


def _head_mask(hh):
    lane = lax.broadcasted_iota(jnp.int32, (1, 128), 1)
    return (lane < 64) if hh == 0 else (lane >= 64)


SOFTMAX_ROWS = 16


def _probs_block(s_scr, bias, hh, rows, q_start):
    s = s_scr[rows, :] + bias[hh, rows, :]
    col = lax.broadcasted_iota(jnp.int32, (SOFTMAX_ROWS, WINDOW), 1)
    s = jnp.where(col >= PAD_ROWS - q_start, s, NEG_INF)
    p = jnp.exp(s - jnp.max(s, axis=-1, keepdims=True))
    return p / jnp.sum(p, axis=-1, keepdims=True)


def _attention(z, gen, name, carried=None):
    t = z.shape[0]
    n_i = t // STEP_ROWS

    def body(q_ref, k_ref, v_ref, g_ref, o_ref, kpad, vpad, bias, s_scr, p_scr):
        hp, i = pl.program_id(0), pl.program_id(1)

        @pl.when(i == 0)
        def _():
            kpad[0:PAD_ROWS, :] = jnp.zeros((PAD_ROWS, 128), BF16)
            vpad[0:PAD_ROWS, :] = jnp.zeros((PAD_ROWS, 128), BF16)
            kpad[PAD_ROWS:PAD_ROWS + t, :] = k_ref[...].astype(BF16)
            vpad[PAD_ROWS:PAD_ROWS + t, :] = v_ref[...].astype(BF16)
            for hh in range(2):
                bias[hh] = _band_bias(g_ref[pl.ds(2 * hp + hh, 1), :])

        for q0 in range(0, STEP_ROWS, Q_TILE):
            q_start = i * STEP_ROWS + q0
            win = pl.ds(pl.multiple_of(q_start, Q_TILE), WINDOW)
            out = None
            for hh in range(2):
                mask = _head_mask(hh)
                qm = jnp.where(mask, q_ref[q0:q0 + Q_TILE, :] * (CHUNK ** -0.5), 0.0).astype(BF16)
                slot = 2 * (q0 // Q_TILE) + hh
                s_scr[slot] = lax.dot_general(qm, kpad[win, :], NT_DIMS, preferred_element_type=F32)
                for r0 in range(0, Q_TILE, SOFTMAX_ROWS):
                    rows = slice(r0, r0 + SOFTMAX_ROWS)
                    p_scr[slot, rows, :] = _probs_block(s_scr.at[slot], bias, hh, rows, q_start).astype(BF16)
                o = jnp.dot(p_scr[slot], vpad[win, :], preferred_element_type=F32)
                out = jnp.where(mask, o, 0.0) if out is None else jnp.where(mask, o, out)
            o_ref[q0:q0 + Q_TILE, :] = out.astype(BF16)

    res = _call(
        body, grid=(4, n_i),
        in_specs=[pl.BlockSpec((STEP_ROWS, 128), lambda h, i: (i, h)),
                  pl.BlockSpec((t, 128), lambda h, i: (0, 4 + h)),
                  pl.BlockSpec((t, 128), lambda h, i: (0, 8 + h)),
                  pl.BlockSpec((N_HEADS, TOEP), lambda h, i: (0, 0))],
        out_specs=[pl.BlockSpec((STEP_ROWS, 128), lambda h, i: (i, h))],
        out_shape=[_sds((t, 512), BF16)],
        scratch_shapes=[pltpu.VMEM((PAD_ROWS + t, 128), BF16), pltpu.VMEM((PAD_ROWS + t, 128), BF16),
                        pltpu.VMEM((2, Q_TILE, WINDOW), F32), pltpu.VMEM((4, Q_TILE, WINDOW), F32),
                        pltpu.VMEM((4, Q_TILE, WINDOW), BF16)],
        sem=("arbitrary", "arbitrary"), name=name, args=(z, z, z, gen), carried=carried)
    return res[0] if carried is None else (res[0], res[1:])


def _attention_bwd(z, datt, gen, name, carried=None):
    t = z.shape[0]
    n_i = t // STEP_ROWS

    def body(q_ref, k_ref, v_ref, d_ref, g_ref, dq_ref, dk_ref, dv_ref, sq_ref, sk_ref, sv_ref, dg_ref,
             kpad, vpad, dkacc, dvacc, bias, dsacc, s_scr, dp_scr, p_scr, ds_scr):
        hp, i = pl.program_id(0), pl.program_id(1)

        @pl.when(i == 0)
        def _():
            kpad[0:PAD_ROWS, :] = jnp.zeros((PAD_ROWS, 128), BF16)
            vpad[0:PAD_ROWS, :] = jnp.zeros((PAD_ROWS, 128), BF16)
            kpad[PAD_ROWS:PAD_ROWS + t, :] = k_ref[...].astype(BF16)
            vpad[PAD_ROWS:PAD_ROWS + t, :] = v_ref[...].astype(BF16)
            dkacc[...] = jnp.zeros_like(dkacc)
            dvacc[...] = jnp.zeros_like(dvacc)
            dsacc[...] = jnp.zeros_like(dsacc)
            for hh in range(2):
                bias[hh] = _band_bias(g_ref[pl.ds(2 * hp + hh, 1), :])

        dq_sum = None
        for q0 in range(0, STEP_ROWS, Q_TILE):
            q_start = i * STEP_ROWS + q0
            win = pl.ds(pl.multiple_of(q_start, Q_TILE), WINDOW)
            dq = None
            for hh in range(2):
                mask = _head_mask(hh)
                qm = jnp.where(mask, q_ref[q0:q0 + Q_TILE, :] * (CHUNK ** -0.5), 0.0).astype(BF16)
                dom = jnp.where(mask, d_ref[q0:q0 + Q_TILE, :], 0.0).astype(BF16)
                slot = 2 * (q0 // Q_TILE) + hh
                s_scr[slot] = lax.dot_general(qm, kpad[win, :], NT_DIMS, preferred_element_type=F32)
                dp_scr[slot] = lax.dot_general(dom, vpad[win, :], NT_DIMS, preferred_element_type=F32)
                for r0 in range(0, Q_TILE, SOFTMAX_ROWS):
                    rows = slice(r0, r0 + SOFTMAX_ROWS)
                    p = _probs_block(s_scr.at[slot], bias, hh, rows, q_start)
                    dp = dp_scr[slot, rows, :]
                    ds = p * (dp - jnp.sum(p * dp, axis=-1, keepdims=True))
                    dsacc[hh, rows, :] += ds
                    ds_scr[slot, rows, :] = ds.astype(BF16)
                    p_scr[slot, rows, :] = p.astype(BF16)
                ds16 = ds_scr[slot]
                dqh = jnp.dot(ds16, kpad[win, :], preferred_element_type=F32) * (CHUNK ** -0.5)
                dq = jnp.where(mask, dqh, 0.0) if dq is None else jnp.where(mask, dqh, dq)
                dkacc[win, :] += lax.dot_general(ds16, qm, TN_DIMS, preferred_element_type=F32)
                dvacc[win, :] += lax.dot_general(p_scr[slot], dom, TN_DIMS, preferred_element_type=F32)
            dq_ref[q0:q0 + Q_TILE, :] = dq.astype(BF16)
            dq_sum = _colsum(dq) if dq_sum is None else dq_sum + _colsum(dq)

        @pl.when(i == 0)
        def _():
            sq_ref[...] = dq_sum

        @pl.when(i > 0)
        def _():
            sq_ref[...] += dq_sum

        @pl.when(i == n_i - 1)
        def _():
            dk = dkacc[PAD_ROWS:PAD_ROWS + t, :]
            dv = dvacc[PAD_ROWS:PAD_ROWS + t, :]
            dk_ref[...] = dk.astype(BF16)
            dv_ref[...] = dv.astype(BF16)
            sk_ref[...] = _colsum(dk)
            sv_ref[...] = _colsum(dv)
            rr = lax.broadcasted_iota(jnp.int32, (Q_TILE, Q_TILE), 0)
            cc = lax.broadcasted_iota(jnp.int32, (Q_TILE, Q_TILE), 1)
            rev = jnp.where(rr + cc == Q_TILE - 1, 1.0, 0.0).astype(BF16)
            for hh in range(2):
                acc = None
                for part in _split3(dsacc[hh]):
                    pr = jnp.dot(rev, part, preferred_element_type=F32)
                    acc = pr if acc is None else acc + pr
                wide = jnp.concatenate([acc, jnp.zeros((Q_TILE, TOEP - WINDOW), F32)], axis=1)
                dg_ref[pl.ds(2 * hp + hh, 1), :] = _colsum(pltpu.roll(wide, 0, 1, stride=1, stride_axis=0))

    col = lambda off: pl.BlockSpec((t, 128), lambda h, i: (0, off + h))
    tile = lambda: pl.BlockSpec((STEP_ROWS, 128), lambda h, i: (i, h))
    sums = lambda: pl.BlockSpec((1, 128), lambda h, i: (0, h))
    res = _call(
        body, grid=(4, n_i),
        in_specs=[tile(), col(4), col(8), tile(), pl.BlockSpec((N_HEADS, TOEP), lambda h, i: (0, 0))],
        out_specs=[tile(), col(0), col(0), sums(), sums(), sums(), pl.BlockSpec((N_HEADS, TOEP), lambda h, i: (0, 0))],
        out_shape=[_sds((t, 512), BF16)] * 3 + [_sds((1, 512), F32)] * 3 + [_sds((N_HEADS, TOEP), F32)],
        scratch_shapes=[pltpu.VMEM((PAD_ROWS + t, 128), BF16), pltpu.VMEM((PAD_ROWS + t, 128), BF16),
                        pltpu.VMEM((PAD_ROWS + t, 128), F32), pltpu.VMEM((PAD_ROWS + t, 128), F32),
                        pltpu.VMEM((2, Q_TILE, WINDOW), F32), pltpu.VMEM((2, Q_TILE, WINDOW), F32),
                        pltpu.VMEM((4, Q_TILE, WINDOW), F32), pltpu.VMEM((4, Q_TILE, WINDOW), F32),
                        pltpu.VMEM((4, Q_TILE, WINDOW), BF16), pltpu.VMEM((4, Q_TILE, WINDOW), BF16)],
        sem=("arbitrary", "arbitrary"), name=name, args=(z, z, z, datt, gen), carried=carried)
    return res[:7] if carried is None else (res[:7], res[7:])


def _adamw_math(w, g, m, v):
    m = ADAM_B1 * m + (1.0 - ADAM_B1) * g
    v = ADAM_B2 * v + (1.0 - ADAM_B2) * (g * g)
    m_hat = m / (1.0 - ADAM_B1 ** ADAM_STEP)
    v_hat = v / (1.0 - ADAM_B2 ** ADAM_STEP)
    delta = -ADAM_LR * (m_hat / (jnp.sqrt(v_hat) + ADAM_EPS) + ADAM_WD * w)
    return delta, m, v


def _adamw_many(items, name):
    n = len(items)

    def body(*refs):
        ins, outs = refs[:4 * n], refs[4 * n:]
        for k in range(n):
            w, g, m, v = (r[...] for r in ins[4 * k:4 * k + 4])
            outs[3 * k][...], outs[3 * k + 1][...], outs[3 * k + 2][...] = _adamw_math(w, g, m, v)

    flat = [a for item in items for a in item]
    res = pl.pallas_call(body, out_shape=[_sds(item[0].shape, F32) for item in items for _ in range(3)],
                         name=name)(*flat)
    return [tuple(res[3 * k:3 * k + 3]) for k in range(n)]


def _adamw(w, g, m, v, name, after):
    r, c = w.shape
    tm = next(cand for cand in (512, 352, 256, 128, 64, 32, 16, 8) if r % cand == 0)
    return _rowcall(lambda wv, gv, mv, vv, _: (gv,) + _adamw_math(wv, gv, mv, vv),
                    [(w, c, 0), (g, c, 0), (m, c, 0), (v, c, 0)], [after], [(r, c, F32, c)] * 4, [], name=name, tm=tm)


def _ada_fwd(c_all, w_shard, b_shard, name):
    n = w_shard.shape[1]
    tn = 512

    def body(c_ref, w_ref, b_ref, o_ref, a_ref):
        cv = c_ref[...]
        act = cv * _sigmoid(cv)
        a_ref[...] = act
        o_ref[...] = jnp.dot(act.astype(BF16), w_ref[...].astype(BF16), preferred_element_type=F32) + b_ref[...]

    return pl.pallas_call(
        body, grid=(n // tn,),
        in_specs=[pl.BlockSpec((8, D), lambda j: (0, 0)), pl.BlockSpec((D, tn), lambda j: (0, j)),
                  pl.BlockSpec((1, tn), lambda j: (0, j))],
        out_specs=[pl.BlockSpec((8, tn), lambda j: (0, j)), pl.BlockSpec((8, D), lambda j: (0, 0))],
        out_shape=[_sds((8, n), F32), _sds((8, D), F32)],
        compiler_params=_params(("arbitrary",)), name=name,
    )(c_all, w_shard, b_shard)


def _ada_bwd_adamw(act_t, dmod_shard, w, m, v, name):
    r, c = w.shape
    tm = 2 * ROW_TILE

    def body(a_ref, d_ref, w_ref, m_ref, v_ref, g_ref, dl_ref, nm_ref, nv_ref):
        g = jnp.dot(a_ref[...], d_ref[...], precision=lax.Precision.HIGHEST, preferred_element_type=F32)
        g_ref[...] = g
        dl_ref[...], nm_ref[...], nv_ref[...] = _adamw_math(w_ref[...], g, m_ref[...], v_ref[...])

    blk = pl.BlockSpec((tm, c), lambda i: (i, 0))
    return pl.pallas_call(
        body, grid=(r // tm,),
        in_specs=[pl.BlockSpec((tm, 8), lambda i: (i, 0)), pl.BlockSpec((8, c), lambda i: (0, 0)), blk, blk, blk],
        out_specs=[blk] * 4, out_shape=[_sds((r, c), F32)] * 4,
        compiler_params=_params(("arbitrary",)), name=name,
    )(act_t, dmod_shard, w, m, v)


def _place():
    return lax.axis_index("x"), lax.axis_index("y"), lax.axis_index("c")


def _flip(v, bit):
    return 1 - v if bit else v


VMEM_SPEC = pl.BlockSpec(memory_space=pltpu.VMEM)


def _allgather8(v, name):
    r, c = v.shape

    def body(v_ref, g_ref, tot_ref, send_sems, recv_sems, local_sem):
        x, y, cc = _place()
        sibling = (x, y, 1 - cc)
        chips = [(_flip(x, k & 2), _flip(y, k & 1)) for k in (1, 2, 3)]

        def block(px, py, pc):
            return g_ref.at[4 * px + 2 * py + pc]

        def copy(k, place, to, src=None):
            slot = block(*place)
            return pltpu.make_async_remote_copy(src_ref=slot if src is None else src, dst_ref=slot,
                                                send_sem=send_sems.at[k], recv_sem=recv_sems.at[k],
                                                device_id=to, device_id_type=MESH)

        mine = pltpu.make_async_copy(v_ref, block(x, y, cc), local_sem)
        mine.start()
        first = [copy(0, (x, y, cc), sibling, src=v_ref)]
        first += [copy(1 + j, (x, y, cc), (px, py, cc), src=v_ref) for j, (px, py) in enumerate(chips)]
        for cp in first:
            cp.start()
        passed = [copy(4 + j, (px, py, cc), sibling) for j, (px, py) in enumerate(chips)]
        for j, (px, py) in enumerate(chips):
            copy(1 + j, (px, py, cc), (x, y, cc)).wait_recv()
            passed[j].start()
        copy(0, sibling, (x, y, cc)).wait_recv()
        for j, (px, py) in enumerate(chips):
            copy(4 + j, (px, py, 1 - cc), (x, y, cc)).wait_recv()
        for cp in first + passed:
            cp.wait_send()
        mine.wait()
        tot = g_ref[0]
        for d in range(1, 8):
            tot = tot + g_ref[d]
        tot_ref[...] = tot

    return pl.pallas_call(
        body, in_specs=[VMEM_SPEC], out_specs=[VMEM_SPEC, VMEM_SPEC],
        out_shape=[_sds((8, r, c), F32), _sds((r, c), F32)],
        scratch_shapes=[pltpu.SemaphoreType.DMA((7,)), pltpu.SemaphoreType.DMA((7,)), pltpu.SemaphoreType.DMA],
        compiler_params=pltpu.CompilerParams(vmem_limit_bytes=VMEM_LIMIT), name=name,
    )(v)


def _slot(px, py, swapped):
    return 2 * py + px if swapped else 2 * px + py


def _gather_shards(arrs, swapped, name):
    n = len(arrs)

    def body(*refs):
        ins, outs = refs[:n], refs[n:2 * n]
        send1, recv1, send2, recv2, local_sems = refs[2 * n:]
        x, y, c = _place()
        sibling = (x, y, 1 - c)
        chips = [(_flip(x, k & 2), _flip(y, k & 1)) for k in (1, 2, 3)]
        local_copies, sends = [], []
        for a in range(n):
            h = outs[a].shape[1] // 2
            mine = pl.ds(pl.multiple_of(c * h, 8), h)
            own = _slot(x, y, swapped[a])
            lc = pltpu.make_async_copy(ins[a], outs[a].at[own], local_sems.at[a])
            lc.start()
            local_copies.append(lc)
            for j, (px, py) in enumerate(chips):
                cp = pltpu.make_async_remote_copy(
                    src_ref=ins[a].at[mine], dst_ref=outs[a].at[own, mine], send_sem=send1.at[3 * a + j],
                    recv_sem=recv1.at[3 * a + j], device_id=(px, py, c), device_id_type=MESH)
                cp.start()
                sends.append(cp)
        for a in range(n):
            h = outs[a].shape[1] // 2
            mine = pl.ds(pl.multiple_of(c * h, 8), h)
            for j, (px, py) in enumerate(chips):
                piece = outs[a].at[_slot(px, py, swapped[a]), mine]
                pltpu.make_async_remote_copy(
                    src_ref=piece, dst_ref=piece, send_sem=send1.at[3 * a + j], recv_sem=recv1.at[3 * a + j],
                    device_id=(px, py, c), device_id_type=MESH).wait_recv()
                fwd = pltpu.make_async_remote_copy(
                    src_ref=piece, dst_ref=piece, send_sem=send2.at[3 * a + j], recv_sem=recv2.at[3 * a + j],
                    device_id=sibling, device_id_type=MESH)
                fwd.start()
                sends.append(fwd)
        for a in range(n):
            h = outs[a].shape[1] // 2
            other = pl.ds(pl.multiple_of((1 - c) * h, 8), h)
            for j, (px, py) in enumerate(chips):
                piece = outs[a].at[_slot(px, py, swapped[a]), other]
                pltpu.make_async_remote_copy(
                    src_ref=piece, dst_ref=piece, send_sem=send2.at[3 * a + j], recv_sem=recv2.at[3 * a + j],
                    device_id=sibling, device_id_type=MESH).wait_recv()
        for cp in sends:
            cp.wait_send()
        for lc in local_copies:
            lc.wait()

    dma = lambda k: pltpu.SemaphoreType.DMA((k,))
    return pl.pallas_call(
        body, in_specs=[ANY] * n, out_specs=[ANY] * n,
        out_shape=[_sds((4,) + a.shape, a.dtype) for a in arrs],
        scratch_shapes=[dma(3 * n), dma(3 * n), dma(3 * n), dma(3 * n), dma(n)], name=name,
    )(*arrs)


def _carry_pair_exchange(grads):
    n = len(grads)

    def copies(ins, outs, send_sems, recv_sems):
        x, y, c = _place()
        cps = []
        for a in range(n):
            h = ins[a].shape[1] // 2
            theirs = pl.ds(pl.multiple_of((1 - c) * h, 8), h)
            cps.append(pltpu.make_async_remote_copy(
                src_ref=ins[a].at[:, theirs, :], dst_ref=outs[a], send_sem=send_sems.at[a], recv_sem=recv_sems.at[a],
                device_id=(x, y, 1 - c), device_id_type=MESH))
        return cps

    def start(*refs):
        for cp in copies(*refs):
            cp.start()

    def finish(*refs):
        for cp in copies(*refs):
            cp.wait()

    return _Carried(grads, [_sds((4, g.shape[1] // 2, g.shape[2]), F32) for g in grads], {}, n, start, finish)


def _pair_sum(grad, recv, core, name):
    _, r, c = grad.shape
    h = r // 2

    def body(core_ref, g_ref, r_ref, o_ref):
        o_ref[...] = (g_ref[...] + r_ref[...]).astype(BF16)

    return pl.pallas_call(
        body,
        grid_spec=pltpu.PrefetchScalarGridSpec(
            num_scalar_prefetch=1, grid=(4,),
            in_specs=[pl.BlockSpec((None, h, c), lambda s, core_ref: (s, core_ref[0], 0)),
                      pl.BlockSpec((None, h, c), lambda s, core_ref: (s, 0, 0))],
            out_specs=pl.BlockSpec((None, h, c), lambda s, core_ref: (s, 0, 0))),
        out_shape=_sds((4, h, c), BF16), compiler_params=_params(("arbitrary",)), name=name,
    )(core, grad, recv)


def _carry_chip_exchange(parts, swapped):
    n = len(parts)

    def copies(ins, outs, send_sems, recv_sems):
        x, y, c = _place()
        chips = [(_flip(x, k & 2), _flip(y, k & 1)) for k in (1, 2, 3)]
        cps = []
        for a in range(n):
            for j, (px, py) in enumerate(chips):
                cps.append(pltpu.make_async_remote_copy(
                    src_ref=ins[a].at[_slot(px, py, swapped[a])], dst_ref=outs[a].at[j],
                    send_sem=send_sems.at[3 * a + j], recv_sem=recv_sems.at[3 * a + j],
                    device_id=(px, py, c), device_id_type=MESH))
        return cps

    def start(*refs):
        for cp in copies(*refs):
            cp.start()

    def finish(*refs):
        for cp in copies(*refs):
            cp.wait()

    return _Carried(parts, [_sds((3,) + p.shape[1:], BF16) for p in parts], {}, 3 * n, start, finish)


def _chip_sum(part, recv, slot_core, name):
    _, h, c = part.shape

    def body(sc_ref, p_ref, r_ref, o_ref):
        acc = p_ref[...].astype(F32)
        for j in range(3):
            acc = acc + r_ref[j].astype(F32)
        o_ref[...] = acc

    return pl.pallas_call(
        body,
        grid_spec=pltpu.PrefetchScalarGridSpec(
            num_scalar_prefetch=1, grid=(1,),
            in_specs=[pl.BlockSpec((None, h, c), lambda q, sc_ref: (sc_ref[0], 0, 0)),
                      pl.BlockSpec((3, h, c), lambda q, sc_ref: (0, 0, 0))],
            out_specs=pl.BlockSpec((h, c), lambda q, sc_ref: (sc_ref[1], 0))),
        out_shape=_sds((2 * h, c), F32), compiler_params=_params(("arbitrary",)), name=name,
    )(slot_core, part, recv)


def _carry_pair_share(shards):
    n = len(shards)

    def copies(outs, send_sems, recv_sems, mine):
        x, y, c = _place()
        cps = []
        for a in range(n):
            h = outs[a].shape[0] // 2
            half = outs[a].at[pl.ds(pl.multiple_of((c if mine else 1 - c) * h, 8), h)]
            cps.append(pltpu.make_async_remote_copy(
                src_ref=half, dst_ref=half, send_sem=send_sems.at[a], recv_sem=recv_sems.at[a],
                device_id=(x, y, 1 - c), device_id_type=MESH))
        return cps

    def start(ins, outs, send_sems, recv_sems):
        for cp in copies(outs, send_sems, recv_sems, True):
            cp.start()

    def finish(ins, outs, send_sems, recv_sems):
        for cp in copies(outs, send_sems, recv_sems, False):
            cp.wait_recv()
        for cp in copies(outs, send_sems, recv_sems, True):
            cp.wait_send()

    return _Carried(shards, [_sds(s.shape, F32) for s in shards], {a: a for a in range(n)}, n, start, finish)


def _carry_gather_ici(bufs, swapped):
    n = len(bufs)

    def copies(outs, send_sems, recv_sems, sending):
        x, y, c = _place()
        cps = []
        for a in range(n):
            h = outs[a].shape[1] // 2
            mine = pl.ds(pl.multiple_of(c * h, 8), h)
            for j, k in enumerate((1, 2, 3)):
                px, py = _flip(x, k & 2), _flip(y, k & 1)
                slot = _slot(x, y, swapped[a]) if sending else _slot(px, py, swapped[a])
                piece = outs[a].at[slot, mine]
                cps.append(pltpu.make_async_remote_copy(
                    src_ref=piece, dst_ref=piece, send_sem=send_sems.at[3 * a + j], recv_sem=recv_sems.at[3 * a + j],
                    device_id=(px, py, c), device_id_type=MESH))
        return cps

    def start(ins, outs, send_sems, recv_sems):
        for cp in copies(outs, send_sems, recv_sems, True):
            cp.start()

    def finish(ins, outs, send_sems, recv_sems):
        for cp in copies(outs, send_sems, recv_sems, False):
            cp.wait_recv()
        for cp in copies(outs, send_sems, recv_sems, True):
            cp.wait_send()

    return _Carried(bufs, [_sds(b.shape, b.dtype) for b in bufs], {a: a for a in range(n)}, 3 * n, start, finish)


HBM_SPEC = pl.BlockSpec(memory_space=pltpu.HBM)
SEM_SPEC = pl.BlockSpec(memory_space=pltpu.SEMAPHORE)
SIDE_EFFECT = pltpu.SideEffectType.DATAFLOW_SIDE_EFFECTING


def _ici_pieces(buf, send_sems, recv_sems, swapped, sending):
    x, y, c = _place()
    h = buf.shape[1] // 2
    mine = pl.ds(pl.multiple_of(c * h, 8), h)
    cps = []
    for j, k in enumerate((1, 2, 3)):
        px, py = _flip(x, k & 2), _flip(y, k & 1)
        piece = buf.at[_slot(x, y, swapped) if sending else _slot(px, py, swapped), mine]
        cps.append(pltpu.make_async_remote_copy(src_ref=piece, dst_ref=piece, send_sem=send_sems.at[j],
                                                recv_sem=recv_sems.at[j], device_id=(px, py, c), device_id_type=MESH))
    return cps


def _gather_ici_start(buf, after, swapped, name):
    def body(buf_ref, after_ref, send_sems, recv_sems, thru, token):
        for cp in _ici_pieces(thru, send_sems, recv_sems, swapped, True):
            cp.start()
        token[...] = jnp.zeros_like(token)

    return pl.pallas_call(
        body, name=name,
        out_shape=(pltpu.SemaphoreType.DMA((3,)), pltpu.SemaphoreType.DMA((3,)), pltpu.HBM(buf.shape, buf.dtype),
                   jax.ShapeDtypeStruct((8, 128), F32)),
        in_specs=(HBM_SPEC, ANY), out_specs=(SEM_SPEC, SEM_SPEC, HBM_SPEC, VMEM_SPEC), input_output_aliases={0: 2},
        compiler_params=pltpu.CompilerParams(has_side_effects=SIDE_EFFECT),
    )(pltpu.with_memory_space_constraint(buf, pltpu.HBM), after)


def _gather_ici_wait(send_sems, recv_sems, thru, after, swapped, name):
    def body(thru_ref, send_sems, recv_sems, after_ref, out_ref):
        for cp in _ici_pieces(out_ref, send_sems, recv_sems, swapped, True):
            cp.wait_send()
        for cp in _ici_pieces(out_ref, send_sems, recv_sems, swapped, False):
            cp.wait_recv()

    return pl.pallas_call(
        body, name=name, out_shape=pltpu.HBM(thru.shape, thru.dtype),
        in_specs=(HBM_SPEC, SEM_SPEC, SEM_SPEC, ANY), out_specs=HBM_SPEC, input_output_aliases={0: 0},
        compiler_params=pltpu.CompilerParams(has_side_effects=SIDE_EFFECT),
    )(thru, send_sems, recv_sems, after)


def _all8_copies(buf, send_sems, recv_sems, sending):
    x, y, c = _place()
    cps = []
    for k in range(1, 8):
        px, py, pc = _flip(x, k & 4), _flip(y, k & 2), _flip(c, k & 1)
        slot = buf.at[4 * x + 2 * y + c] if sending else buf.at[4 * px + 2 * py + pc]
        cps.append(pltpu.make_async_remote_copy(src_ref=slot, dst_ref=slot, send_sem=send_sems.at[k - 1],
                                                recv_sem=recv_sems.at[k - 1], device_id=(px, py, pc), device_id_type=MESH))
    return cps


def _all8_start(buf, name):
    def body(buf_ref, send_sems, recv_sems, thru, token):
        for cp in _all8_copies(thru, send_sems, recv_sems, True):
            cp.start()
        token[...] = jnp.zeros_like(token)

    return pl.pallas_call(
        body, name=name,
        out_shape=(pltpu.SemaphoreType.DMA((7,)), pltpu.SemaphoreType.DMA((7,)), pltpu.HBM(buf.shape, buf.dtype),
                   jax.ShapeDtypeStruct((8, 128), F32)),
        in_specs=(HBM_SPEC,), out_specs=(SEM_SPEC, SEM_SPEC, HBM_SPEC, VMEM_SPEC), input_output_aliases={0: 2},
        compiler_params=pltpu.CompilerParams(has_side_effects=SIDE_EFFECT),
    )(pltpu.with_memory_space_constraint(buf, pltpu.HBM))


def _all8_wait(send_sems, recv_sems, thru, after, name):
    def body(thru_ref, send_sems, recv_sems, after_ref, out_ref):
        for cp in _all8_copies(out_ref, send_sems, recv_sems, True):
            cp.wait_send()
        for cp in _all8_copies(out_ref, send_sems, recv_sems, False):
            cp.wait_recv()

    return pl.pallas_call(
        body, name=name, out_shape=pltpu.HBM(thru.shape, thru.dtype),
        in_specs=(HBM_SPEC, SEM_SPEC, SEM_SPEC, ANY), out_specs=HBM_SPEC, input_output_aliases={0: 0},
        compiler_params=pltpu.CompilerParams(has_side_effects=SIDE_EFFECT),
    )(thru, send_sems, recv_sems, after)


def _sum8(g, name):
    def body(g_ref, o_ref):
        tot = g_ref[0]
        for d in range(1, 8):
            tot = tot + g_ref[d]
        o_ref[...] = tot

    return pl.pallas_call(body, out_shape=_sds(g.shape[1:], F32), name=name)(g)


def _carry_gather_forward(bufs, swapped):
    n = len(bufs)

    def copies(outs, send_sems, recv_sems, sending):
        x, y, c = _place()
        cps = []
        for a in range(n):
            h = outs[a].shape[1] // 2
            rows = pl.ds(pl.multiple_of((c if sending else 1 - c) * h, 8), h)
            for j, k in enumerate((1, 2, 3)):
                piece = outs[a].at[_slot(_flip(x, k & 2), _flip(y, k & 1), swapped[a]), rows]
                cps.append(pltpu.make_async_remote_copy(
                    src_ref=piece, dst_ref=piece, send_sem=send_sems.at[3 * a + j], recv_sem=recv_sems.at[3 * a + j],
                    device_id=(x, y, 1 - c), device_id_type=MESH))
        return cps

    def start(ins, outs, send_sems, recv_sems):
        for cp in copies(outs, send_sems, recv_sems, True):
            cp.start()

    def finish(ins, outs, send_sems, recv_sems):
        for cp in copies(outs, send_sems, recv_sems, False):
            cp.wait_recv()
        for cp in copies(outs, send_sems, recv_sems, True):
            cp.wait_send()

    return _Carried(bufs, [_sds(b.shape, b.dtype) for b in bufs], {a: a for a in range(n)}, 3 * n, start, finish)


def _pack(arrs, rows_multiple=8):
    parts, offs, row = [], [], 0
    for a in arrs:
        flat = a.reshape(-1)
        nrow = -(-flat.shape[0] // D)
        parts.append(jnp.pad(flat, (0, nrow * D - flat.shape[0])))
        offs.append(row)
        row += nrow
    total = -(-row // rows_multiple) * rows_multiple
    if total > row:
        parts.append(jnp.zeros(((total - row) * D,), F32))
    return jnp.concatenate(parts).reshape(total, D), offs


def _unpack(packed, offs, shapes):
    out = []
    for off, shp in zip(offs, shapes):
        size = int(np.prod(shp))
        nrow = -(-size // D)
        out.append(packed[off:off + nrow].reshape(-1)[:size].reshape(shp))
    return out


def _to_bf16_slot(w, slot, name, after=None):
    r, c = w.shape
    tm = next(cand for cand in (512, 352, 256, 128, 64, 32, 16) if r % cand == 0)

    def body(slot_ref, w_ref, *rest):
        rest[-1][...] = w_ref[...].astype(BF16)

    in_specs = [pl.BlockSpec((tm, c), lambda i, slot_ref: (i, 0))]
    if after is not None:
        in_specs.append(pl.BlockSpec((8, 128), lambda i, slot_ref: (0, 0)))
    return pl.pallas_call(
        body,
        grid_spec=pltpu.PrefetchScalarGridSpec(
            num_scalar_prefetch=1, grid=(r // tm,), in_specs=in_specs,
            out_specs=pl.BlockSpec((None, tm, c), lambda i, slot_ref: (slot_ref[0], i, 0))),
        out_shape=_sds((4, r, c), BF16), compiler_params=_params(("arbitrary",)), name=name,
    )(slot, w, *([] if after is None else [after]))


def _unshard_cols(g):
    s, k, n = g.shape
    return jnp.transpose(g, (1, 0, 2)).reshape(k, s * n)


def _ff_swap(v):
    b = FF_BLOCK
    return jnp.concatenate([v[..., 0:b], v[..., 2 * b:3 * b], v[..., b:2 * b], v[..., 3 * b:4 * b]], axis=-1)


LATE = ("attn_o", "conv_o", "mix_o", "up", "down")
EARLY_GRADS = ("down", "up", "mix_o", "attn_o", "conv_o")


def _weight_views(bufs):
    return {"up": bufs["up"], "attn_o": _unshard_cols(bufs["attn_o"]), "conv_o": _unshard_cols(bufs["conv_o"]),
            "mix_o": bufs["mix_o"].reshape(D, D), "down": bufs["down"].reshape(D_FF, D)}


def _pair_sums(names, grads, recv, dist):
    return [_pair_sum(g, r, dist["core"], "pair_sum_" + n) for n, g, r in zip(names, grads, recv)]


def _reduce_halves(names, parts, from_chips, dist):
    return [_chip_sum(p, r, jnp.concatenate([dist["slots"][SWAPPED[n]], dist["core"]]), "chip_sum_" + n)
            for n, p, r in zip(names, parts, from_chips)]


FUSED_TILE = 256
WIDE_TILE = 512


def _gates(z):
    return [(z, 512, 5), (z, 512, 6), (z, 512, 7), (z, 512, 8)]


def _mix_out(a, cb, z, x, w_mix_o, g_post, gt, g_pre2, sc2, sh2, name):
    def lhs(av, cv, ga0, ga1, gb0, gb1):
        ga, gb = jnp.concatenate([ga0, ga1], axis=1), jnp.concatenate([gb0, gb1], axis=1)
        return _sigmoid(ga) * av + _sigmoid(gb) * cv

    def fn(ym, y, xv, gv, gtv, g2v, scv, shv):
        yn, _ = _rms_stats(ym)
        x1 = xv + gtv * (yn * gv)
        xn, _ = _rms_stats(x1)
        return ym, y, x1, (xn * g2v) * (1.0 + scv) + shv

    return _matmul_rows(w_mix_o, form="nn", tm=min(WIDE_TILE, x.shape[0]), tk=D, fn=fn, a_rows=[(a, D, 0), (cb, D, 0)] + _gates(z),
                        a_fn=lhs, rows=[(x, D, 0)], consts=[g_post, gt, g_pre2, sc2, sh2],
                        row_outs=[(F32, D), (BF16, D), (F32, D), (BF16, D)], acc_outs=[], name=name)


def _down_tail(act, w_down, x1, target, g, gt, name):
    def fn(yv, xv, tv, gv, gtv):
        yn, r = _rms_stats(yv)
        e = xv + gtv * (yn * gv) - tv
        dx2 = e * (1.0 / D)
        dyn = dx2 * gtv
        return (dx2, _rms_bwd(dyn * gv, yn, r), _colsum(e * e) * (0.5 / D), _colsum(dyn * yn),
                _colsum(dx2 * (yn * gv)))

    return _matmul_rows(w_down, form="nn", a=act, tm=min(WIDE_TILE, x1.shape[0]), tk=D_FF, fn=fn,
                        rows=[(x1, D, 0), (target, D, 0)], consts=[g, gt], row_outs=[(F32, D), (BF16, D)],
                        acc_outs=[(1, D)] * 3, name=name)


def _up_dx_tail(dup, w_up, x1, dx2, ym, g_pre2, sc2, g_post, gt, name):
    def fn(dh, xv, dov, ymv, g2v, scv, gv, gtv):
        xn, r = _rms_stats(xv)
        dyn = dh * (1.0 + scv)
        dx1 = dov + _rms_bwd(dyn * g2v, xn, r)
        yn, r2 = _rms_stats(ymv)
        dynm = dx1 * gtv
        return (dx1, _rms_bwd(dynm * gv, yn, r2), _colsum(dh), _colsum(dh * (xn * g2v)), _colsum(dyn * xn),
                _colsum(dynm * yn), _colsum(dx1 * (yn * gv)))

    return _matmul_rows(w_up, form="nt", a=dup, tm=min(FUSED_TILE, x1.shape[0]), tk=2 * D_FF, fn=fn,
                        rows=[(x1, D, 0), (dx2, D, 0), (ym, D, 0)], consts=[g_pre2, sc2, g_post, gt],
                        row_outs=[(F32, D), (BF16, D)], acc_outs=[(1, D)] * 5, name=name)


def _mix_dx_gates(dym, w_mix_o, a, cb, z, name):
    def fn(dy, av, cv, ga0, ga1, gb0, gb1):
        sa = _sigmoid(jnp.concatenate([ga0, ga1], axis=1))
        sb = _sigmoid(jnp.concatenate([gb0, gb1], axis=1))
        dcb = dy * sb
        dga = dy * av * (sa * (1.0 - sa))
        dgb = dy * cv * (sb * (1.0 - sb))
        return dy * sa, dcb, dga, dgb, _colsum(dcb), _colsum(dga), _colsum(dgb)

    return _matmul_rows(w_mix_o, form="nt", a=dym, tm=min(WIDE_TILE, a.shape[0]), tk=D, fn=fn,
                        rows=[(a, D, 0), (cb, D, 0)] + _gates(z), consts=[], row_outs=[(BF16, D)] * 4,
                        acc_outs=[(1, D)] * 3, name=name)


def _local_step(x, target, mod, w_in, late, small, dist=None):
    sh_m, sc_m, gt_m, sh_f, sc_f, gt_f = mod
    t = x.shape[0]
    tmm = min(1024, t)
    late_swapped = [SWAPPED[n] for n in LATE]

    h1 = _pre_norm(x, small["g_pre_mix"], sc_m, sh_m, "pre_norm_mix")
    if callable(w_in):
        w_in = w_in(h1)
    z = _matmul(h1, w_in, form="nn", out_dtype=F32, tm=min(FUSED_TILE, t), tn=D_IN, tk=D, bias=small["b_in"], name="mm_in")
    conv = (z, small["w_dw_conv"], small["b_dw_conv"], small["g_conv_ln"], small["b_conv_ln"], "conv_branch")
    if dist is None:
        att = _attention(z, small["gen"], "attention")
        u1, u3 = _conv_branch(*conv)
        bufs = dict(late)
    else:
        mid = [n for n in LATE if n != "down"]
        mid_swapped = [SWAPPED[n] for n in mid]
        att, landed = _attention(z, small["gen"], "attention",
                                 carried=_carry_gather_ici([late[n] for n in mid], mid_swapped))
        (u1, u3), gathered = _conv_branch(*conv, carried=_carry_gather_forward(landed, mid_swapped))
        bufs = dict(zip(mid, gathered))
        bufs["down"] = late["down"]
    w = _weight_views(bufs)
    w["in"] = w_in
    a = _matmul(att, w["attn_o"], form="nn", out_dtype=F32, tm=tmm, tn=512, tk=512, name="mm_attn_o")
    cb = _matmul(u3, w["conv_o"], form="nn", out_dtype=F32, tm=tmm, tn=512, tk=512, bias=small["b_conv_o"], name="mm_conv_o")
    ym, y, x1, h2 = _mix_out(a, cb, z, x, w["mix_o"], small["g_post_mix"], gt_m, small["g_pre_ffn"], sc_f, sh_f, "mix_out")
    mm_up = dict(form="nn", out_dtype=F32, tm=min(FUSED_TILE, t), tn=2 * D_FF, tk=D, name="mm_up")
    ffn_act = (small["w_dw_ffn"], small["b_dw_ffn"], "ffn_act")
    if dist is None:
        up = _matmul(h2, w["up"], **mm_up)
        act = _ffn_act(up, *ffn_act)
    else:
        up, landed = _matmul(h2, w["up"], carried=_carry_gather_ici([late["down"]], [False]), **mm_up)
        act, down = _ffn_act(up, *ffn_act, carried=_carry_gather_forward(landed, [False]))
        w["down"] = down[0].reshape(D_FF, D)

    dx2, dyf, loss_cols, d_g_post_ffn, d_gt_f = _down_tail(act, w["down"], x1, target, small["g_post_ffn"], gt_f, "down_tail")
    dact = _matmul(dyf, w["down"], form="nt", out_dtype=F32, tm=tmm, tn=FF_BLOCK, tk=D, name="mm_down_dx")
    g_down = _matmul(act, dyf, form="tn", out_dtype=F32, tm=FF_BLOCK, tn=512, tk=t, name="mm_down_dw")
    dup, d_w_dw_ffn, d_b_dw_ffn = _ffn_act_bwd(dact, up, small["w_dw_ffn"], small["b_dw_ffn"], "ffn_act_bwd")
    dx1, dym, d_sh_f, d_sc_f, d_g_pre_ffn, d_g_post_mix, d_gt_m = _up_dx_tail(
        dup, w["up"], x1, dx2, ym, small["g_pre_ffn"], sc_f, small["g_post_mix"], gt_m, "up_dx_tail")
    g_up = _matmul(h2, dup, form="tn", out_dtype=F32, tm=512, tn=FF_BLOCK, tk=t, out_sharded=True, name="mm_up_dw")
    da, dcb, dgate_a, dgate_b, d_b_conv_o, sga, sgb = _mix_dx_gates(dym, w["mix_o"], a, cb, z, "mix_dx_gates")
    g_mix_o = _matmul(y, dym, form="tn", out_dtype=F32, tm=D, tn=512, tk=t, name="mm_mix_o_dw")
    datt = _matmul(da, w["attn_o"], form="nt", out_dtype=F32, tm=tmm, tn=512, tk=D, name="mm_attn_o_dx")
    g_attn_o = _matmul(att, da, form="tn", out_dtype=F32, tm=512, tn=256, tk=t, out_sharded=True, name="mm_attn_o_dw")
    du3 = _matmul(dcb, w["conv_o"], form="nt", out_dtype=F32, tm=tmm, tn=512, tk=D, name="mm_conv_o_dx")
    g_conv_o = _matmul(u3, dcb, form="tn", out_dtype=F32, tm=512, tn=256, tk=t, out_sharded=True, name="mm_conv_o_dw")
    big = {"attn_o": g_attn_o, "conv_o": g_conv_o, "mix_o": g_mix_o.reshape(4, 256, D),
           "up": g_up, "down": g_down.reshape(4, D_FF // 4, D)}
    conv_bwd = (du3, u1, z, small["w_dw_conv"], small["g_conv_ln"], small["b_conv_ln"], "conv_branch_bwd")
    in_dw = dict(form="tn", out_dtype=F32, tm=512, tn=IN_SHARD, tk=t, out_sharded=True, name="mm_in_dw")
    in_dx = dict(form="nt", out_dtype=F32, tm=min(WIDE_TILE, t), tn=D, tk=D_IN, name="mm_in_dx")
    if dist is None:
        dglu, d_w_dw_conv, d_b_dw_conv, d_g_conv_ln, d_b_conv_ln, sglu = _conv_branch_bwd(*conv_bwd)
        dq, dk, dv, sq, sk, sv, dgen = _attention_bwd(z, datt, small["gen"], "attention_bwd")
        dz = jnp.concatenate([dq, dk, dv, dglu, dgate_a, dgate_b], axis=1)
        big["in"] = _matmul(h1, dz, **in_dw)
        dh1 = _matmul(dz, w_in, **in_dx)
    else:
        early = [big[n] for n in EARLY_GRADS]
        (dglu, d_w_dw_conv, d_b_dw_conv, d_g_conv_ln, d_b_conv_ln, sglu), recv = _conv_branch_bwd(
            *conv_bwd, carried=_carry_pair_exchange(early))
        parts = _pair_sums(EARLY_GRADS, early, recv, dist)
        (dq, dk, dv, sq, sk, sv, dgen), from_chips = _attention_bwd(
            z, datt, small["gen"], "attention_bwd",
            carried=_carry_chip_exchange(parts, [SWAPPED[n] for n in EARLY_GRADS]))
        halves = _reduce_halves(EARLY_GRADS, parts, from_chips, dist)
        dz = jnp.concatenate([dq, dk, dv, dglu, dgate_a, dgate_b], axis=1)
        g_in, shards = _matmul(h1, dz, carried=_carry_pair_share(halves), **in_dw)
        big = dict(zip(EARLY_GRADS, shards))
        recv_in = _run_carried(_carry_pair_exchange([g_in]), "pair_exchange_in")
        part_in = _pair_sums(("in",), [g_in], recv_in, dist)
        dh1, from_chips_in = _matmul(dz, w_in, carried=_carry_chip_exchange(part_in, [False]), **in_dx)
        half_in = _reduce_halves(("in",), part_in, from_chips_in, dist)
        big["in"] = _run_carried(_carry_pair_share(half_in), "pair_share_in")[0]
    d_b_in = jnp.concatenate([sq, sk, sv, sglu, sga, sgb], axis=1)
    grad_x, d_sh_m, d_sc_m, d_g_pre_mix = _pre_norm_bwd(dh1, x, dx1, small["g_pre_mix"], sc_m, "pre_norm_mix_bwd")

    dmod = [d_sh_m, d_sc_m, d_gt_m, d_sh_f, d_sc_f, d_gt_f]
    sm = {"g_pre_mix": d_g_pre_mix, "g_post_mix": d_g_post_mix, "b_in": d_b_in, "gen": dgen,
          "w_dw_conv": d_w_dw_conv, "b_dw_conv": d_b_dw_conv, "g_conv_ln": d_g_conv_ln, "b_conv_ln": d_b_conv_ln,
          "b_conv_o": d_b_conv_o, "g_pre_ffn": d_g_pre_ffn, "g_post_ffn": d_g_post_ffn,
          "w_dw_ffn": d_w_dw_ffn, "b_dw_ffn": d_b_dw_ffn}
    return loss_cols, grad_x, dmod, big, sm


BIG = ("in", "attn_o", "conv_o", "mix_o", "up", "down")
SWAPPED = {"in": False, "attn_o": False, "conv_o": False, "mix_o": False, "up": True, "down": False}
SMALL_ORDER = ("b_ada", "g_pre_mix", "g_post_mix", "b_in", "rel_bias", "b_dw_conv", "g_conv_ln", "b_conv_ln",
               "b_conv_o", "g_pre_ffn", "g_post_ffn", "b_dw_ffn", "w_dw_conv", "w_dw_ffn")


def kernel(x, c, w_ada, b_ada, g_pre_mix, g_post_mix, w_in, b_in, rel_bias, w_attn_o, w_dw_conv, b_dw_conv, g_conv_ln, b_conv_ln, w_conv_o, b_conv_o, w_mix_o, g_pre_ffn, g_post_ffn, w_up, w_dw_ffn, b_dw_ffn, w_down, loss_target, m_w_ada, m_b_ada, m_g_pre_mix, m_g_post_mix, m_w_in, m_b_in, m_rel_bias, m_w_attn_o, m_w_dw_conv, m_b_dw_conv, m_g_conv_ln, m_b_conv_ln, m_w_conv_o, m_b_conv_o, m_w_mix_o, m_g_pre_ffn, m_g_post_ffn, m_w_up, m_w_dw_ffn, m_b_dw_ffn, m_w_down, v_w_ada, v_b_ada, v_g_pre_mix, v_g_post_mix, v_w_in, v_b_in, v_rel_bias, v_w_attn_o, v_w_dw_conv, v_b_dw_conv, v_g_conv_ln, v_b_conv_ln, v_w_conv_o, v_b_conv_o, v_w_mix_o, v_g_pre_ffn, v_g_post_ffn, v_w_up, v_w_dw_ffn, v_b_dw_ffn, v_w_down):
    given = dict(locals())
    ax, ay, ac = lax.axis_index("x"), lax.axis_index("y"), lax.axis_index("c")
    shard = 2 * ax + ay
    me = 4 * ax + 2 * ay + ac
    xs, target = x[0], loss_target[0]

    slots = {sw: _slot(ax, ay, sw).astype(jnp.int32).reshape(1) for sw in (False, True)}
    own = {"in": _to_bf16_slot(w_in[0], slots[False], "cast_in")}

    c_pad = jnp.pad(c, ((0, 7), (0, 0)))
    c_g, _ = _allgather8(c_pad, "gather_c")
    c_all = c_g[:, 0, :]
    b_ada_shard = lax.dynamic_slice(b_ada, (0, shard * ADA_SHARD), (1, ADA_SHARD))
    mod_shard, c_act = _ada_fwd(c_all, w_ada[0], b_ada_shard, "ada_fwd")
    small_in = [jnp.pad(mod_shard, ((0, 8), (0, 0))),
                jnp.pad(w_dw_conv[0], ((0, 1), (0, 0))),
                jnp.pad(w_dw_ffn[0], ((0, 13), (0, 0)))]
    mod_g, wdc_g, wdf_g = _gather_shards(small_in, [False, False, True], "gather_small")
    mod_all = jnp.transpose(mod_g[:, :8, :], (1, 0, 2)).reshape(8, 6 * D)
    in_send, in_recv, in_flight, token = _gather_ici_start(own["in"], mod_g, False, "gather_w_in_start")

    def w_in_ready(after):
        landed = _gather_ici_wait(in_send, in_recv, in_flight, after, False, "gather_w_in_wait")
        return _run_carried(_carry_gather_forward([landed], [False]), "gather_forward_in")[0]

    for n in LATE:
        own[n] = _to_bf16_slot(given["w_" + n][0], slots[SWAPPED[n]], "cast_" + n, after=token)
    mod_row = lax.dynamic_slice(mod_all, (me, 0), (1, 6 * D)) + token[0:1, 0:1]
    mod = [mod_row[:, k * D:(k + 1) * D] for k in range(6)]

    core = ac.astype(jnp.int32).reshape(1)
    dist = {"core": core, "slots": slots}

    sel = jnp.asarray(_toeplitz_map())
    rel_pad = jnp.pad(rel_bias[0], ((0, 0), (0, REL_PAD - (2 * MAX_REL + 1))))
    gen = _select_call(rel_pad, sel.T.astype(BF16), "bias_rows")
    small = {"g_pre_mix": g_pre_mix, "g_post_mix": g_post_mix, "b_in": b_in, "gen": gen,
             "w_dw_conv": _unshard_cols(wdc_g[:, :CONV_K, :]), "b_dw_conv": b_dw_conv, "g_conv_ln": g_conv_ln,
             "b_conv_ln": b_conv_ln, "b_conv_o": b_conv_o, "g_pre_ffn": g_pre_ffn, "g_post_ffn": g_post_ffn,
             "w_dw_ffn": _unshard_cols(wdf_g[:, :FFN_K, :]), "b_dw_ffn": _ff_swap(b_dw_ffn)}

    loss_cols, grad_x, dmod, reduced, sm = _local_step(xs, target, mod, w_in_ready, {n: own[n] for n in LATE}, small, dist)

    d_rel = _select_call(sm["gen"], sel.astype(BF16), "bias_fold")[:, :2 * MAX_REL + 1]
    small_grads = {"g_pre_mix": sm["g_pre_mix"], "g_post_mix": sm["g_post_mix"], "b_in": sm["b_in"], "rel_bias": d_rel[None],
                   "b_dw_conv": sm["b_dw_conv"], "g_conv_ln": sm["g_conv_ln"], "b_conv_ln": sm["b_conv_ln"],
                   "b_conv_o": sm["b_conv_o"], "g_pre_ffn": sm["g_pre_ffn"], "g_post_ffn": sm["g_post_ffn"],
                   "b_dw_ffn": _ff_swap(sm["b_dw_ffn"]), "w_dw_conv": sm["w_dw_conv"], "w_dw_ffn": _ff_swap(sm["w_dw_ffn"])}
    order = [n for n in SMALL_ORDER if n != "b_ada"]
    packed, offs = _pack([jnp.concatenate(dmod, axis=1)] + [small_grads[n] for n in order] + [loss_cols])
    mine = lax.dynamic_update_slice(jnp.zeros((8,) + packed.shape, F32), packed[None], (me, 0, 0))
    sg_send, sg_recv, sg_flight, sg_token = _all8_start(mine, "gather_small_grads_start")

    out = {}
    for n in BIG:
        g, dl, nm, nv = _adamw(given["w_" + n][0], reduced[n], given["m_w_" + n][0], given["v_w_" + n][0],
                               "adamw_" + n, sg_token)
        out["grad_w_" + n], out["delta_w_" + n], out["new_m_w_" + n], out["new_v_w_" + n] = g[None], dl[None], nm[None], nv[None]
    every = _all8_wait(sg_send, sg_recv, sg_flight, out["delta_w_in"], "gather_small_grads_wait")
    total = _sum8(every, "sum_small_grads")
    loss = jnp.sum(total[offs[-1]])
    offs = offs[:-1]
    dmod_all = every[:, 0:6, :].reshape(8, 6 * D)
    full_shapes = {n: given[n].shape for n in order}
    full_shapes["w_dw_conv"], full_shapes["w_dw_ffn"] = (1, CONV_K, 512), (1, FFN_K, 2 * D_FF)
    sums = dict(zip(order, _unpack(total, offs[1:], [full_shapes[n] for n in order])))
    sums["b_ada"] = total[0:6].reshape(1, 6 * D)
    sums["w_dw_conv"] = lax.dynamic_slice(sums["w_dw_conv"], (0, 0, shard * 128), (1, CONV_K, 128))
    sums["w_dw_ffn"] = lax.dynamic_slice(sums["w_dw_ffn"], (0, 0, shard * FF_BLOCK), (1, FFN_K, FF_BLOCK))

    upd = dict(zip(SMALL_ORDER, _adamw_many(
        [(given[n], sums[n], given["m_" + n], given["v_" + n]) for n in SMALL_ORDER], "adamw_small")))

    dmod_shard = lax.dynamic_slice(dmod_all, (0, shard * ADA_SHARD), (8, ADA_SHARD))
    ada = _ada_bwd_adamw(c_act.T, dmod_shard, w_ada[0], m_w_ada[0], v_w_ada[0], "ada_bwd_adamw")

    out.update({"grad_w_ada": ada[0][None], "delta_w_ada": ada[1][None], "new_m_w_ada": ada[2][None],
                "new_v_w_ada": ada[3][None]})
    for n in SMALL_ORDER:
        out["grad_" + n], out["delta_" + n], out["new_m_" + n], out["new_v_" + n] = sums[n], *upd[n]

    weights = ["w_ada", "b_ada", "g_pre_mix", "g_post_mix", "w_in", "b_in", "rel_bias", "w_attn_o", "w_dw_conv", "b_dw_conv",
               "g_conv_ln", "b_conv_ln", "w_conv_o", "b_conv_o", "w_mix_o", "g_pre_ffn", "g_post_ffn", "w_up", "w_dw_ffn",
               "b_dw_ffn", "w_down"]
    return (loss, grad_x[None], *[out["grad_" + n] for n in weights], *[out["delta_" + n] for n in weights],
            *[out["new_m_" + n] for n in weights], *[out["new_v_" + n] for n in weights])
```

```python
import functools
import math

import numpy as np
import jax
import jax.numpy as jnp
from jax import lax
from jax.experimental import pallas as pl
from jax.experimental.pallas import tpu as pltpu

F32, BF16 = jnp.float32, jnp.bfloat16
MESH = pl.DeviceIdType.MESH

D = 1024
D_IN = 4608
D_FF = 2816
N_CHIPS = 4
IN_SHARD = D_IN // N_CHIPS
ADA_SHARD = 6 * D // N_CHIPS
CONV_K = 31
FFN_K = 3
N_HEADS = 8
CHUNK = 64
LEFT_CHUNKS = 8
MAX_REL = 128
EPS = 1e-6
NEG_INF = -1e30
Q_TILE = 256
WINDOW = Q_TILE + LEFT_CHUNKS * CHUNK
STEP_ROWS = 512
REL_PAD = 384
TOEP = 1024
ROW_TILE = 256
WORK_TILE = 512
VMEM_LIMIT = 60 * 1024 * 1024

ADAM_LR, ADAM_B1, ADAM_B2, ADAM_EPS, ADAM_WD, ADAM_STEP = 0.001, 0.9, 0.999, 1e-08, 0.01, 10


def _params(sem=None):
    return pltpu.CompilerParams(dimension_semantics=sem, vmem_limit_bytes=VMEM_LIMIT)


def _sds(shape, dtype):
    return jax.ShapeDtypeStruct(tuple(shape), dtype)


ANY = pl.BlockSpec(memory_space=pl.ANY)


class _Carried:
    def __init__(self, ins, out_shapes, aliases, n_sems, start, finish):
        self.ins, self.out_shapes, self.aliases = list(ins), list(out_shapes), dict(aliases)
        self.n_sems, self.start, self.finish = n_sems, start, finish


def _call(body, *, grid, in_specs, out_specs, out_shape, scratch_shapes, sem, name, args, carried=None):
    in_specs, out_specs, out_shape = list(in_specs), list(out_specs), list(out_shape)
    scratch_shapes = list(scratch_shapes)
    if carried is None:
        return pl.pallas_call(body, grid=grid, in_specs=in_specs, out_specs=out_specs, out_shape=out_shape,
                              scratch_shapes=scratch_shapes, compiler_params=_params(sem), name=name)(*args)
    n_in, n_out, n_scr = len(in_specs), len(out_specs), len(scratch_shapes)
    c_in, c_out = len(carried.ins), len(carried.out_shapes)

    def full(*refs):
        pos = [0]

        def take(k):
            part = refs[pos[0]:pos[0] + k]
            pos[0] += k
            return part

        ins, cins, outs, couts, scr = take(n_in), take(c_in), take(n_out), take(c_out), take(n_scr)
        send_sems, recv_sems = take(2)
        first = last = None
        for d, size in enumerate(grid):
            pid = pl.program_id(d)
            first = (pid == 0) if first is None else first & (pid == 0)
            last = (pid == size - 1) if last is None else last & (pid == size - 1)

        @pl.when(first)
        def _():
            carried.start(cins, couts, send_sems, recv_sems)

        body(*ins, *outs, *scr)

        @pl.when(last)
        def _():
            carried.finish(cins, couts, send_sems, recv_sems)

    sems = [pltpu.SemaphoreType.DMA((carried.n_sems,)), pltpu.SemaphoreType.DMA((carried.n_sems,))]
    return pl.pallas_call(
        full, grid=grid, in_specs=in_specs + [ANY] * c_in, out_specs=out_specs + [ANY] * c_out,
        out_shape=out_shape + carried.out_shapes, scratch_shapes=scratch_shapes + sems,
        input_output_aliases={n_in + k: n_out + v for k, v in carried.aliases.items()},
        compiler_params=_params(tuple("arbitrary" for _ in grid)), name=name,
    )(*args, *carried.ins)


def _run_carried(carried, name):
    c_in = len(carried.ins)

    def body(*refs):
        cins, couts = refs[:c_in], refs[c_in:c_in + len(carried.out_shapes)]
        send_sems, recv_sems = refs[-2:]
        carried.start(cins, couts, send_sems, recv_sems)
        carried.finish(cins, couts, send_sems, recv_sems)

    return pl.pallas_call(
        body, in_specs=[ANY] * c_in, out_specs=[ANY] * len(carried.out_shapes), out_shape=carried.out_shapes,
        scratch_shapes=[pltpu.SemaphoreType.DMA((carried.n_sems,)), pltpu.SemaphoreType.DMA((carried.n_sems,))],
        input_output_aliases=carried.aliases, name=name,
    )(*carried.ins)


def _matmul(a, b, *, form, out_dtype, tm, tn, tk, name, bias=None, add=None, out_sharded=False, carried=None):
    b3 = b.ndim == 3
    resident = 0
    if form == "nn":
        m, k = a.shape
        n = b.shape[0] * b.shape[2] if b3 else b.shape[1]
        dn = (((1,), (0,)), ((), ()))
        a_spec = pl.BlockSpec((tm, tk), lambda i, j, kk: (i, kk))
        if b3 and tn == n and tk == k:
            resident = b.shape[0]
            b_spec = pl.BlockSpec(b.shape, lambda i, j, kk: (0, 0, 0))
        else:
            b_spec = (pl.BlockSpec((None, tk, tn), lambda i, j, kk: (j, kk, 0)) if b3
                      else pl.BlockSpec((tk, tn), lambda i, j, kk: (kk, j)))
    elif form == "nt":
        m, k = a.shape
        n = b.shape[1] if b3 else b.shape[0]
        dn = (((1,), (1,)), ((), ()))
        a_spec = pl.BlockSpec((tm, tk), lambda i, j, kk: (i, kk))
        if b3 and tk == k:
            resident = b.shape[0]
            b_spec = pl.BlockSpec((resident, tn, b.shape[2]), lambda i, j, kk: (0, j, 0))
        else:
            b_spec = (pl.BlockSpec((None, tn, tk), lambda i, j, kk: (kk, j, 0)) if b3
                      else pl.BlockSpec((tn, tk), lambda i, j, kk: (j, kk)))
    else:
        k, m = a.shape
        n = b.shape[1]
        dn = (((0,), (0,)), ((), ()))
        a_spec = pl.BlockSpec((tk, tm), lambda i, j, kk: (kk, i))
        b_spec = pl.BlockSpec((tk, tn), lambda i, j, kk: (kk, j))
    assert m % tm == 0 and n % tn == 0 and k % tk == 0, (name, m, n, k, tm, tn, tk)
    nk = k // tk
    in_specs, args = [a_spec, b_spec], [a, b]
    if bias is not None:
        in_specs.append(pl.BlockSpec((1, tn), lambda i, j, kk: (0, j)))
        args.append(bias)
    if add is not None:
        in_specs.append(pl.BlockSpec((tm, tn), lambda i, j, kk: (i, j)))
        args.append(add)
    if out_sharded:
        out_shape = _sds((n // tn, m, tn), out_dtype)
        out_spec = pl.BlockSpec((None, tm, tn), lambda i, j, kk: (j, i, 0))
    else:
        out_shape = _sds((m, n), out_dtype)
        out_spec = pl.BlockSpec((tm, tn), lambda i, j, kk: (i, j))

    def body(*refs):
        a_ref, b_ref = refs[0], refs[1]
        pos = 2
        bias_ref = add_ref = None
        if bias is not None:
            bias_ref, pos = refs[pos], pos + 1
        if add is not None:
            add_ref, pos = refs[pos], pos + 1
        o_ref = refs[pos]
        if resident and form == "nn":
            ns = b_ref.shape[2]
            for s in range(resident):
                cols = slice(s * ns, (s + 1) * ns)
                ps = lax.dot_general(a_ref[...], b_ref[s], dn, preferred_element_type=F32)
                if bias_ref is not None:
                    ps = ps + bias_ref[:, cols]
                o_ref[:, cols] = ps.astype(o_ref.dtype)
            return
        if resident:
            ks = b_ref.shape[2]
            p = None
            for s in range(resident):
                ps = lax.dot_general(a_ref[:, s * ks:(s + 1) * ks], b_ref[s], dn, preferred_element_type=F32)
                p = ps if p is None else p + ps
        else:
            av, bv = a_ref[...], b_ref[...]
            if av.dtype != BF16:
                av = av.astype(BF16)
            if bv.dtype != BF16:
                bv = bv.astype(BF16)
            p = lax.dot_general(av, bv, dn, preferred_element_type=F32)

        def finish(acc):
            if bias_ref is not None:
                acc = acc + bias_ref[...]
            if add_ref is not None:
                acc = acc + add_ref[...]
            o_ref[...] = acc.astype(o_ref.dtype)

        if nk == 1:
            finish(p)
        else:
            acc_ref = refs[pos + 1]
            kk = pl.program_id(2)

            @pl.when(kk == 0)
            def _():
                acc_ref[...] = p

            @pl.when(kk > 0)
            def _():
                acc_ref[...] += p

            @pl.when(kk == nk - 1)
            def _():
                finish(acc_ref[...])

    res = _call(body, grid=(m // tm, n // tn, nk), in_specs=in_specs, out_specs=[out_spec], out_shape=[out_shape],
                scratch_shapes=[pltpu.VMEM((tm, tn), F32)] if nk > 1 else [],
                sem=("parallel", "parallel", "arbitrary"), name=name, args=args, carried=carried)
    return res[0] if carried is None else (res[0], res[1:])


def _rowcall(fn, rows, consts, row_outs, acc_outs, *, name, tm=ROW_TILE, col_grid=1):
    n_rows = rows[0][0].shape[0]
    assert n_rows % tm == 0
    grid = (col_grid, n_rows // tm)
    in_specs = [pl.BlockSpec((tm, w), functools.partial(lambda c, i, cb: (i, cb + c), cb=cb)) for _, w, cb in rows]
    in_specs += [pl.BlockSpec(k.shape, functools.partial(lambda c, i, nd: (0,) * nd, nd=k.ndim)) for k in consts]
    out_specs = [pl.BlockSpec((tm, w), lambda c, i: (i, c)) for _, _, _, w in row_outs]
    out_specs += [pl.BlockSpec((r, w), lambda c, i: (0, c)) for r, _, w in acc_outs]
    out_shape = [_sds((nr, nc), dt) for nr, nc, dt, _ in row_outs] + [_sds((r, nc), F32) for r, nc, _ in acc_outs]
    n_in, n_ro = len(rows) + len(consts), len(row_outs)

    def body(*refs):
        res = fn(*[r[...] for r in refs[:n_in]])
        if not isinstance(res, (tuple, list)):
            res = (res,)
        outs = refs[n_in:]
        for o_ref, val in zip(outs[:n_ro], res[:n_ro]):
            o_ref[...] = val.astype(o_ref.dtype)
        if acc_outs:
            first = pl.program_id(1) == 0

            @pl.when(first)
            def _():
                for o_ref, val in zip(outs[n_ro:], res[n_ro:]):
                    o_ref[...] = val

            @pl.when(jnp.logical_not(first))
            def _():
                for o_ref, val in zip(outs[n_ro:], res[n_ro:]):
                    o_ref[...] += val

    out = pl.pallas_call(
        body, grid=grid, in_specs=in_specs, out_specs=out_specs, out_shape=out_shape,
        compiler_params=_params(("arbitrary", "arbitrary")), name=name,
    )(*[r[0] for r in rows], *consts)
    return out


def _matmul_rows(b, *, form, tm, tk, fn, rows, consts, row_outs, acc_outs, name, a=None, a_rows=None, a_fn=None,
                 carried=None):
    b3 = b.ndim == 3
    resident = 0
    if form == "nn":
        k, n = b.shape
        b_spec = pl.BlockSpec((tk, n), lambda i, kk: (kk, 0))
        dn = (((1,), (0,)), ((), ()))
    else:
        n = b.shape[1] if b3 else b.shape[0]
        k = b.shape[0] * b.shape[2] if b3 else b.shape[1]
        if b3 and tk == k:
            resident = b.shape[0]
            b_spec = pl.BlockSpec(b.shape, lambda i, kk: (0, 0, 0))
        else:
            b_spec = (pl.BlockSpec((None, n, tk), lambda i, kk: (kk, 0, 0)) if b3
                      else pl.BlockSpec((n, tk), lambda i, kk: (0, kk)))
        dn = (((1,), (1,)), ((), ()))
    nk = k // tk
    lhs_in = [(a, tk, 0)] if a is not None else list(a_rows)
    assert a is not None or nk == 1
    m = lhs_in[0][0].shape[0]
    n_lhs = len(lhs_in)
    in_specs = [pl.BlockSpec((tm, tk), lambda i, kk: (i, kk))] if a is not None else [
        pl.BlockSpec((tm, w), functools.partial(lambda i, kk, cb: (i, cb), cb=cb)) for _, w, cb in a_rows]
    in_specs.append(b_spec)
    in_specs += [pl.BlockSpec((tm, w), functools.partial(lambda i, kk, cb: (i, cb), cb=cb)) for _, w, cb in rows]
    in_specs += [pl.BlockSpec(c.shape, functools.partial(lambda i, kk, nd: (0,) * nd, nd=c.ndim)) for c in consts]
    out_specs = [pl.BlockSpec((tm, w), lambda i, kk: (i, 0)) for _, w in row_outs]
    out_specs += [pl.BlockSpec((r, w), lambda i, kk: (0, 0)) for r, w in acc_outs]
    out_shape = [_sds((m, w), dt) for dt, w in row_outs] + [_sds((r, w), F32) for r, w in acc_outs]
    n_rows, n_consts, n_ro, n_acc = len(rows), len(consts), len(row_outs), len(acc_outs)

    def body(*refs):
        pos = n_lhs + 1
        row_refs, const_refs = refs[pos:pos + n_rows], refs[pos + n_rows:pos + n_rows + n_consts]
        pos += n_rows + n_consts
        out_refs, acc_refs = refs[pos:pos + n_ro], refs[pos + n_ro:pos + n_ro + n_acc]
        i, kk = pl.program_id(0), pl.program_id(1)
        if resident:
            b_ref, ks, p = refs[n_lhs], b.shape[2], None
            for s in range(resident):
                ps = lax.dot_general(refs[0][:, s * ks:(s + 1) * ks], b_ref[s], dn, preferred_element_type=F32)
                p = ps if p is None else p + ps
        else:
            lhs = refs[0][...] if a is not None else a_fn(*[r[...] for r in refs[:n_lhs]]).astype(BF16)
            p = lax.dot_general(lhs, refs[n_lhs][...], dn, preferred_element_type=F32)

        def finish(acc):
            extra = [r[...] for r in row_refs] + [c[...] for c in const_refs]
            res = fn(acc, lhs, *extra) if a is None else fn(acc, *extra)
            for o_ref, val in zip(out_refs, res[:n_ro]):
                o_ref[...] = val.astype(o_ref.dtype)
            if n_acc:
                @pl.when(i == 0)
                def _():
                    for o_ref, val in zip(acc_refs, res[n_ro:]):
                        o_ref[...] = val

                @pl.when(i > 0)
                def _():
                    for o_ref, val in zip(acc_refs, res[n_ro:]):
                        o_ref[...] += val

        if nk == 1:
            finish(p)
        else:
            acc_ref = refs[pos + n_ro + n_acc]

            @pl.when(kk == 0)
            def _():
                acc_ref[...] = p

            @pl.when(kk > 0)
            def _():
                acc_ref[...] += p

            @pl.when(kk == nk - 1)
            def _():
                finish(acc_ref[...])

    res = _call(body, grid=(m // tm, nk), in_specs=in_specs, out_specs=out_specs, out_shape=out_shape,
                scratch_shapes=[pltpu.VMEM((tm, n), F32)] if nk > 1 else [], sem=("arbitrary", "arbitrary"),
                name=name, args=[r[0] for r in lhs_in] + [b] + [r[0] for r in rows] + list(consts), carried=carried)
    own = n_ro + n_acc
    return res[:own] if carried is None else (res[:own], res[own:])


def _colsum(v):
    return jnp.sum(v, axis=0, keepdims=True)


def _sigmoid(v):
    return 1.0 / (1.0 + jnp.exp(-v))


_GELU_C = math.sqrt(2.0 / math.pi)


def _gelu(v):
    return 0.5 * v * (1.0 + jnp.tanh(_GELU_C * (v + 0.044715 * (v * v * v))))


def _gelu_and_grad(v):
    th = jnp.tanh(_GELU_C * (v + 0.044715 * (v * v * v)))
    g = 0.5 * v * (1.0 + th)
    dg = 0.5 * (1.0 + th) + 0.5 * v * (1.0 - th * th) * (_GELU_C * (1.0 + 3.0 * 0.044715 * (v * v)))
    return g, dg


def _rms_stats(v):
    r = lax.rsqrt(jnp.mean(v * v, axis=-1, keepdims=True) + EPS)
    return v * r, r


def _rms_bwd(dn, vn, r):
    return r * (dn - vn * jnp.mean(dn * vn, axis=-1, keepdims=True))


def _pre_norm(x, g, sc, sh, name):
    def fn(xv, gv, scv, shv):
        xn, _ = _rms_stats(xv)
        return (xn * gv) * (1.0 + scv) + shv
    return _rowcall(fn, [(x, D, 0)], [g, sc, sh], [(x.shape[0], D, BF16, D)], [], name=name,
                    tm=min(2 * ROW_TILE, x.shape[0]))[0]


def _pre_norm_bwd(dh, x, dx_other, g, sc, name):
    def fn(dhv, xv, dov, gv, scv):
        xn, r = _rms_stats(xv)
        yn = xn * gv
        dyn = dhv * (1.0 + scv)
        dx = _rms_bwd(dyn * gv, xn, r)
        return dov + dx, _colsum(dhv), _colsum(dhv * yn), _colsum(dyn * xn)
    t = x.shape[0]
    return _rowcall(fn, [(dh, D, 0), (x, D, 0), (dx_other, D, 0)], [g, sc], [(t, D, F32, D)],
                    [(1, D, D)] * 3, name=name, tm=min(2 * ROW_TILE, t))


CONV_HALO = 32


def _layer_norm_parts(u):
    mu = jnp.mean(u, axis=-1, keepdims=True)
    d = u - mu
    r = lax.rsqrt(jnp.mean(d * d, axis=-1, keepdims=True) + EPS)
    return d * r, r


LANES = 128
SUBLANE_ROWS = 8
CONV_ROWS = 64


def _lanes(c):
    return slice(c * LANES, (c + 1) * LANES)


def _conv_branch(z, w_dw, b_dw, g_ln, b_ln, name, tm=WORK_TILE, carried=None):
    t = z.shape[0]
    per = tm // CONV_HALO
    n_chunks = 512 // LANES

    def body(ga_ref, gb_ref, gah_ref, gbh_ref, w_ref, b_ref, g_ref, bl_ref, u1_ref, u3_ref, scr):
        i = pl.program_id(0)
        u0h = jnp.where(i > 0, gah_ref[...] * _sigmoid(gbh_ref[...]), 0.0)
        u0 = ga_ref[...] * _sigmoid(gb_ref[...])
        for c in range(n_chunks):
            scr[c, 0:CONV_HALO, :] = u0h[:, _lanes(c)]
            scr[c, CONV_HALO:CONV_HALO + tm, :] = u0[:, _lanes(c)]
        for c in range(n_chunks):
            for r0 in range(0, tm, CONV_ROWS):
                acc = jnp.zeros((CONV_ROWS, LANES), F32) + b_ref[:, _lanes(c)]
                for j in range(CONV_K):
                    acc = acc + w_ref[j:j + 1, _lanes(c)] * scr[c, pl.ds(r0 + CONV_HALO - (CONV_K - 1) + j, CONV_ROWS), :]
                u1_ref[r0:r0 + CONV_ROWS, _lanes(c)] = acc
        xh, _ = _layer_norm_parts(u1_ref[...])
        u2 = xh * g_ref[...] + bl_ref[...]
        u3_ref[...] = (u2 * _sigmoid(u2)).astype(BF16)

    cur = lambda cb: pl.BlockSpec((tm, 512), lambda i: (i, cb))
    halo = lambda cb: pl.BlockSpec((CONV_HALO, 512), lambda i: (jnp.maximum(i * per - 1, 0), cb))
    whole = lambda a: pl.BlockSpec(a.shape, lambda i: (0, 0))
    res = _call(
        body, grid=(t // tm,),
        in_specs=[cur(3), cur(4), halo(3), halo(4), whole(w_dw), whole(b_dw), whole(g_ln), whole(b_ln)],
        out_specs=[pl.BlockSpec((tm, 512), lambda i: (i, 0))] * 2,
        out_shape=[_sds((t, 512), F32), _sds((t, 512), BF16)],
        scratch_shapes=[pltpu.VMEM((n_chunks, CONV_HALO + tm, LANES), F32)],
        sem=("arbitrary",), name=name, args=(z, z, z, z, w_dw, b_dw, g_ln, b_ln), carried=carried)
    return res[:2] if carried is None else (res[:2], res[2:])


def _conv_branch_bwd(du3, u1, z, w_dw, g_ln, b_ln, name, tm=WORK_TILE, carried=None):
    t = z.shape[0]
    per = tm // CONV_HALO
    last = t // tm - 1
    n_chunks = 512 // LANES

    def du1_of(du3v, u1v, g, b):
        xh, r = _layer_norm_parts(u1v)
        u2 = xh * g + b
        s = _sigmoid(u2)
        du2 = du3v * (s * (1.0 + u2 * (1.0 - s)))
        dxh = du2 * g
        du1 = r * (dxh - jnp.mean(dxh, axis=-1, keepdims=True) - xh * jnp.mean(dxh * xh, axis=-1, keepdims=True))
        return du1, du2, xh

    def body(d_ref, u_ref, dn_ref, un_ref, ga_ref, gb_ref, gah_ref, gbh_ref, w_ref, g_ref, bl_ref,
             dglu_ref, dw_ref, dbdw_ref, dg_ref, dbl_ref, dbin_ref, scr, scd):
        i = pl.program_id(0)
        g, b = g_ref[...], bl_ref[...]
        du1, du2, xh = du1_of(d_ref[...], u_ref[...], g, b)
        du1n, _, _ = du1_of(dn_ref[...], un_ref[...], g, b)
        du1n = jnp.where(i < last, du1n, 0.0)
        sgb = _sigmoid(gb_ref[...])
        ga = ga_ref[...]
        u0 = ga * sgb
        u0h = jnp.where(i > 0, gah_ref[...] * _sigmoid(gbh_ref[...]), 0.0)
        for c in range(n_chunks):
            scd[c, 0:tm, :] = du1[:, _lanes(c)]
            scd[c, tm:tm + CONV_HALO, :] = du1n[:, _lanes(c)]
            scr[c, 0:CONV_HALO, :] = u0h[:, _lanes(c)]
            scr[c, CONV_HALO:CONV_HALO + tm, :] = u0[:, _lanes(c)]

        @pl.when(i == 0)
        def _():
            for ref in (dw_ref, dbdw_ref, dg_ref, dbl_ref, dbin_ref):
                ref[...] = jnp.zeros_like(ref)

        dsg = ga * (sgb * (1.0 - sgb))
        for c in range(n_chunks):
            gate = slice(512 + c * LANES, 512 + (c + 1) * LANES)
            for r0 in range(0, tm, CONV_ROWS):
                rows = slice(r0, r0 + CONV_ROWS)
                du0 = jnp.zeros((CONV_ROWS, LANES), F32)
                for j in range(CONV_K):
                    du0 = du0 + w_ref[j:j + 1, _lanes(c)] * scd[c, pl.ds(r0 + CONV_K - 1 - j, CONV_ROWS), :]
                dga = du0 * sgb[rows, _lanes(c)]
                dgb = du0 * dsg[rows, _lanes(c)]
                dglu_ref[rows, _lanes(c)] = dga.astype(BF16)
                dglu_ref[rows, gate] = dgb.astype(BF16)
                dbin_ref[:, _lanes(c)] += _colsum(dga)
                dbin_ref[:, gate] += _colsum(dgb)
            for j in range(CONV_K):
                dwj = jnp.zeros((SUBLANE_ROWS, LANES), F32)
                for r0 in range(0, tm, CONV_ROWS):
                    prod = (scd[c, pl.ds(r0, CONV_ROWS), :]
                            * scr[c, pl.ds(r0 + CONV_HALO - (CONV_K - 1) + j, CONV_ROWS), :])
                    dwj = dwj + jnp.sum(prod.reshape(CONV_ROWS // SUBLANE_ROWS, SUBLANE_ROWS, LANES), axis=0)
                dw_ref[j:j + 1, _lanes(c)] += _colsum(dwj)
        dbdw_ref[...] += _colsum(du1)
        dg_ref[...] += _colsum(du2 * xh)
        dbl_ref[...] += _colsum(du2)

    cur = lambda cb: pl.BlockSpec((tm, 512), lambda i: (i, cb))
    prev = lambda cb: pl.BlockSpec((CONV_HALO, 512), lambda i: (jnp.maximum(i * per - 1, 0), cb))
    nxt = pl.BlockSpec((CONV_HALO, 512), lambda i: (jnp.minimum((i + 1) * per, t // CONV_HALO - 1), 0))
    whole = lambda a: pl.BlockSpec(a.shape, lambda i: (0, 0))
    acc = lambda r, w: pl.BlockSpec((r, w), lambda i: (0, 0))
    res = _call(
        body, grid=(t // tm,),
        in_specs=[cur(0), cur(0), nxt, nxt, cur(3), cur(4), prev(3), prev(4), whole(w_dw), whole(g_ln), whole(b_ln)],
        out_specs=[pl.BlockSpec((tm, 1024), lambda i: (i, 0)), acc(CONV_K, 512), acc(1, 512), acc(1, 512),
                   acc(1, 512), acc(1, 1024)],
        out_shape=[_sds((t, 1024), BF16), _sds((CONV_K, 512), F32), _sds((1, 512), F32), _sds((1, 512), F32),
                   _sds((1, 512), F32), _sds((1, 1024), F32)],
        scratch_shapes=[pltpu.VMEM((n_chunks, CONV_HALO + tm, LANES), F32),
                        pltpu.VMEM((n_chunks, tm + CONV_HALO, LANES), F32)],
        sem=("arbitrary",), name=name, args=(du3, u1, du3, u1, z, z, z, z, w_dw, g_ln, b_ln), carried=carried)
    return res[:6] if carried is None else (res[:6], res[6:])


FF_BLOCK = D_FF // 2
FF_HALO = 8
FF_CHUNKS = FF_BLOCK // LANES


FF_ROWS = 64


def _ext_rows(ext):
    return next(d for d in (104, 88, 72, 56, 40, 24, 8) if ext % d == 0)


def _ffn_conv(w_ref, b_ref, scr, k, rows, r0=0):
    acc = b_ref[:, _lanes(k)] + w_ref[0:1, _lanes(k)] * scr[k, pl.ds(r0 + FF_HALO - 2, rows), :]
    acc = acc + w_ref[1:2, _lanes(k)] * scr[k, pl.ds(r0 + FF_HALO - 1, rows), :]
    return acc + w_ref[2:3, _lanes(k)] * scr[k, pl.ds(r0 + FF_HALO, rows), :]


def _ffn_act(up, w3, b3, name, tm=WORK_TILE, carried=None):
    t = up.shape[0]
    per = tm // FF_HALO
    wide = 2 * FF_BLOCK

    def body(u_ref, uh_ref, w_ref, b_ref, o_ref, scr):
        i = pl.program_id(1)
        for k in range(2 * FF_CHUNKS):
            scr[k, 0:FF_HALO, :] = jnp.where(i > 0, uh_ref[:, _lanes(k)], 0.0)
            scr[k, FF_HALO:FF_HALO + tm, :] = u_ref[:, _lanes(k)]
        for cc in range(FF_CHUNKS):
            for r0 in range(0, tm, FF_ROWS):
                val = _ffn_conv(w_ref, b_ref, scr, cc, FF_ROWS, r0)
                gate = _ffn_conv(w_ref, b_ref, scr, FF_CHUNKS + cc, FF_ROWS, r0)
                o_ref[r0:r0 + FF_ROWS, _lanes(cc)] = (_gelu(gate) * val).astype(BF16)

    res = _call(
        body, grid=(2, t // tm),
        in_specs=[pl.BlockSpec((tm, wide), lambda c, i: (i, c)),
                  pl.BlockSpec((FF_HALO, wide), lambda c, i: (jnp.maximum(i * per - 1, 0), c)),
                  pl.BlockSpec((FFN_K, wide), lambda c, i: (0, c)),
                  pl.BlockSpec((1, wide), lambda c, i: (0, c))],
        out_specs=[pl.BlockSpec((tm, FF_BLOCK), lambda c, i: (i, c))],
        out_shape=[_sds((t, D_FF), BF16)],
        scratch_shapes=[pltpu.VMEM((2 * FF_CHUNKS, FF_HALO + tm, LANES), F32)],
        sem=("arbitrary", "arbitrary"), name=name, args=(up, up, w3, b3), carried=carried)
    return res[0] if carried is None else (res[0], res[1:])


def _ffn_act_bwd(dact, up, w3, b3, name, tm=WORK_TILE):
    t = up.shape[0]
    per = tm // FF_HALO
    wide = 2 * FF_BLOCK
    last = t // tm - 1
    ext = tm + FF_HALO
    FF_EXT_ROWS = _ext_rows(ext)

    def body(u_ref, up_ref, un_ref, d_ref, dn_ref, w_ref, b_ref, o_ref, dw_ref, db_ref, scr, scd):
        i = pl.program_id(1)
        for k in range(2 * FF_CHUNKS):
            scr[k, 0:FF_HALO, :] = jnp.where(i > 0, up_ref[:, _lanes(k)], 0.0)
            scr[k, FF_HALO:FF_HALO + tm, :] = u_ref[:, _lanes(k)]
            scr[k, FF_HALO + tm:FF_HALO + ext, :] = un_ref[:, _lanes(k)]
        dn = jnp.where(i < last, dn_ref[...], 0.0)

        @pl.when(i == 0)
        def _():
            dw_ref[...] = jnp.zeros_like(dw_ref)
            db_ref[...] = jnp.zeros_like(db_ref)

        for cc in range(FF_CHUNKS):
            gc = FF_CHUNKS + cc
            for r0 in range(0, ext, FF_EXT_ROWS):
                rows = pl.ds(r0, FF_EXT_ROWS)
                val = _ffn_conv(w_ref, b_ref, scr, cc, FF_EXT_ROWS, r0)
                gel, dgel = _gelu_and_grad(_ffn_conv(w_ref, b_ref, scr, gc, FF_EXT_ROWS, r0))
                da = d_ref[r0:r0 + FF_EXT_ROWS, _lanes(cc)] if r0 + FF_EXT_ROWS <= tm else jnp.concatenate(
                    [d_ref[r0:tm, _lanes(cc)], dn[:, _lanes(cc)]], axis=0)
                scd[cc, rows, :] = da * gel
                scd[gc, rows, :] = da * val * dgel
            for k in (cc, gc):
                dwk = [jnp.zeros((SUBLANE_ROWS, LANES), F32) for _ in range(FFN_K)]
                dbk = jnp.zeros((SUBLANE_ROWS, LANES), F32)
                for r0 in range(0, tm, FF_ROWS):
                    shifted = [scd[k, pl.ds(r0 + FFN_K - 1 - j, FF_ROWS), :] for j in range(FFN_K)]
                    ucur = scr[k, pl.ds(r0 + FF_HALO, FF_ROWS), :]
                    o_ref[r0:r0 + FF_ROWS, _lanes(k)] = (
                        w_ref[0:1, _lanes(k)] * shifted[0] + w_ref[1:2, _lanes(k)] * shifted[1]
                        + w_ref[2:3, _lanes(k)] * shifted[2]).astype(BF16)
                    fold = lambda v: jnp.sum(v.reshape(FF_ROWS // SUBLANE_ROWS, SUBLANE_ROWS, LANES), axis=0)
                    for j in range(FFN_K):
                        dwk[j] = dwk[j] + fold(shifted[j] * ucur)
                    dbk = dbk + fold(shifted[FFN_K - 1])
                for j in range(FFN_K):
                    dw_ref[j:j + 1, _lanes(k)] += _colsum(dwk[j])
                db_ref[:, _lanes(k)] += _colsum(dbk)

    nblk = t // FF_HALO
    return pl.pallas_call(
        body, grid=(2, t // tm),
        in_specs=[pl.BlockSpec((tm, wide), lambda c, i: (i, c)),
                  pl.BlockSpec((FF_HALO, wide), lambda c, i: (jnp.maximum(i * per - 1, 0), c)),
                  pl.BlockSpec((FF_HALO, wide), lambda c, i: (jnp.minimum((i + 1) * per, nblk - 1), c)),
                  pl.BlockSpec((tm, FF_BLOCK), lambda c, i: (i, c)),
                  pl.BlockSpec((FF_HALO, FF_BLOCK), lambda c, i: (jnp.minimum((i + 1) * per, nblk - 1), c)),
                  pl.BlockSpec((FFN_K, wide), lambda c, i: (0, c)),
                  pl.BlockSpec((1, wide), lambda c, i: (0, c))],
        out_specs=[pl.BlockSpec((tm, wide), lambda c, i: (i, c)),
                   pl.BlockSpec((FFN_K, wide), lambda c, i: (0, c)),
                   pl.BlockSpec((1, wide), lambda c, i: (0, c))],
        out_shape=[_sds((t, 2 * D_FF), BF16), _sds((FFN_K, 2 * D_FF), F32), _sds((1, 2 * D_FF), F32)],
        scratch_shapes=[pltpu.VMEM((2 * FF_CHUNKS, FF_HALO + ext, LANES), F32),
                        pltpu.VMEM((2 * FF_CHUNKS, ext, LANES), F32)],
        compiler_params=_params(("arbitrary", "arbitrary")), name=name,
    )(up, up, up, dact, dact, w3, b3)


def _toeplitz_map():
    f = np.zeros((TOEP, REL_PAD), np.float32)
    for m in range(TOEP - 1):
        rel = (WINDOW - 1) - m
        f[m, int(np.clip(rel, -MAX_REL, MAX_REL)) + MAX_REL] = 1.0
    return f


def _split3(v):
    hi = v.astype(BF16)
    r1 = v - hi.astype(F32)
    mid = r1.astype(BF16)
    lo = (r1 - mid.astype(F32)).astype(BF16)
    return hi, mid, lo


def _exact_select(v, sel):
    out = None
    for part in _split3(v):
        p = jnp.dot(part, sel, preferred_element_type=F32)
        out = p if out is None else out + p
    return out


def _select_call(v, sel, name):
    def body(v_ref, s_ref, o_ref):
        o_ref[...] = _exact_select(v_ref[...], s_ref[...])
    return pl.pallas_call(body, out_shape=_sds((v.shape[0], sel.shape[1]), F32), name=name)(v, sel)


def _band_bias(gen_row):
    b0 = jnp.broadcast_to(gen_row, (Q_TILE, TOEP))
    bias = pltpu.roll(b0, TOEP - (Q_TILE - 1), 1, stride=1, stride_axis=0)[:, :WINDOW]
    qq = lax.broadcasted_iota(jnp.int32, (Q_TILE, WINDOW), 0) // CHUNK
    kc = lax.broadcasted_iota(jnp.int32, (Q_TILE, WINDOW), 1) // CHUNK
    return jnp.where((kc >= qq) & (kc <= qq + LEFT_CHUNKS), bias, NEG_INF)


PAD_ROWS = WINDOW - Q_TILE
NT_DIMS = (((1,), (1,)), ((), ()))
TN_DIMS = (((0,), (0,)), ((), ()))


def _head_mask(hh):
    lane = lax.broadcasted_iota(jnp.int32, (1, 128), 1)
    return (lane < 64) if hh == 0 else (lane >= 64)


SOFTMAX_ROWS = 16


def _probs_block(s_scr, bias, hh, rows, q_start):
    s = s_scr[rows, :] + bias[hh, rows, :]
    col = lax.broadcasted_iota(jnp.int32, (SOFTMAX_ROWS, WINDOW), 1)
    s = jnp.where(col >= PAD_ROWS - q_start, s, NEG_INF)
    p = jnp.exp(s - jnp.max(s, axis=-1, keepdims=True))
    return p / jnp.sum(p, axis=-1, keepdims=True)


def _attention(z, gen, name, carried=None):
    t = z.shape[0]
    n_i = t // STEP_ROWS

    def body(q_ref, k_ref, v_ref, g_ref, o_ref, kpad, vpad, bias, s_scr, p_scr):
        hp, i = pl.program_id(0), pl.program_id(1)

        @pl.when(i == 0)
        def _():
            kpad[0:PAD_ROWS, :] = jnp.zeros((PAD_ROWS, 128), BF16)
            vpad[0:PAD_ROWS, :] = jnp.zeros((PAD_ROWS, 128), BF16)
            kpad[PAD_ROWS:PAD_ROWS + t, :] = k_ref[...].astype(BF16)
            vpad[PAD_ROWS:PAD_ROWS + t, :] = v_ref[...].astype(BF16)
            for hh in range(2):
                bias[hh] = _band_bias(g_ref[pl.ds(2 * hp + hh, 1), :])

        for q0 in range(0, STEP_ROWS, Q_TILE):
            q_start = i * STEP_ROWS + q0
            win = pl.ds(pl.multiple_of(q_start, Q_TILE), WINDOW)
            out = None
            for hh in range(2):
                mask = _head_mask(hh)
                qm = jnp.where(mask, q_ref[q0:q0 + Q_TILE, :] * (CHUNK ** -0.5), 0.0).astype(BF16)
                slot = 2 * (q0 // Q_TILE) + hh
                s_scr[slot] = lax.dot_general(qm, kpad[win, :], NT_DIMS, preferred_element_type=F32)
                for r0 in range(0, Q_TILE, SOFTMAX_ROWS):
                    rows = slice(r0, r0 + SOFTMAX_ROWS)
                    p_scr[slot, rows, :] = _probs_block(s_scr.at[slot], bias, hh, rows, q_start).astype(BF16)
                o = jnp.dot(p_scr[slot], vpad[win, :], preferred_element_type=F32)
                out = jnp.where(mask, o, 0.0) if out is None else jnp.where(mask, o, out)
            o_ref[q0:q0 + Q_TILE, :] = out.astype(BF16)

    res = _call(
        body, grid=(4, n_i),
        in_specs=[pl.BlockSpec((STEP_ROWS, 128), lambda h, i: (i, h)),
                  pl.BlockSpec((t, 128), lambda h, i: (0, 4 + h)),
                  pl.BlockSpec((t, 128), lambda h, i: (0, 8 + h)),
                  pl.BlockSpec((N_HEADS, TOEP), lambda h, i: (0, 0))],
        out_specs=[pl.BlockSpec((STEP_ROWS, 128), lambda h, i: (i, h))],
        out_shape=[_sds((t, 512), BF16)],
        scratch_shapes=[pltpu.VMEM((PAD_ROWS + t, 128), BF16), pltpu.VMEM((PAD_ROWS + t, 128), BF16),
                        pltpu.VMEM((2, Q_TILE, WINDOW), F32), pltpu.VMEM((4, Q_TILE, WINDOW), F32),
                        pltpu.VMEM((4, Q_TILE, WINDOW), BF16)],
        sem=("arbitrary", "arbitrary"), name=name, args=(z, z, z, gen), carried=carried)
    return res[0] if carried is None else (res[0], res[1:])


def _attention_bwd(z, datt, gen, name, carried=None):
    t = z.shape[0]
    n_i = t // STEP_ROWS

    def body(q_ref, k_ref, v_ref, d_ref, g_ref, dq_ref, dk_ref, dv_ref, sq_ref, sk_ref, sv_ref, dg_ref,
             kpad, vpad, dkacc, dvacc, bias, dsacc, s_scr, dp_scr, p_scr, ds_scr):
        hp, i = pl.program_id(0), pl.program_id(1)

        @pl.when(i == 0)
        def _():
            kpad[0:PAD_ROWS, :] = jnp.zeros((PAD_ROWS, 128), BF16)
            vpad[0:PAD_ROWS, :] = jnp.zeros((PAD_ROWS, 128), BF16)
            kpad[PAD_ROWS:PAD_ROWS + t, :] = k_ref[...].astype(BF16)
            vpad[PAD_ROWS:PAD_ROWS + t, :] = v_ref[...].astype(BF16)
            dkacc[...] = jnp.zeros_like(dkacc)
            dvacc[...] = jnp.zeros_like(dvacc)
            dsacc[...] = jnp.zeros_like(dsacc)
            for hh in range(2):
                bias[hh] = _band_bias(g_ref[pl.ds(2 * hp + hh, 1), :])

        dq_sum = None
        for q0 in range(0, STEP_ROWS, Q_TILE):
            q_start = i * STEP_ROWS + q0
            win = pl.ds(pl.multiple_of(q_start, Q_TILE), WINDOW)
            dq = None
            for hh in range(2):
                mask = _head_mask(hh)
                qm = jnp.where(mask, q_ref[q0:q0 + Q_TILE, :] * (CHUNK ** -0.5), 0.0).astype(BF16)
                dom = jnp.where(mask, d_ref[q0:q0 + Q_TILE, :], 0.0).astype(BF16)
                slot = 2 * (q0 // Q_TILE) + hh
                s_scr[slot] = lax.dot_general(qm, kpad[win, :], NT_DIMS, preferred_element_type=F32)
                dp_scr[slot] = lax.dot_general(dom, vpad[win, :], NT_DIMS, preferred_element_type=F32)
                for r0 in range(0, Q_TILE, SOFTMAX_ROWS):
                    rows = slice(r0, r0 + SOFTMAX_ROWS)
                    p = _probs_block(s_scr.at[slot], bias, hh, rows, q_start)
                    dp = dp_scr[slot, rows, :]
                    ds = p * (dp - jnp.sum(p * dp, axis=-1, keepdims=True))
                    dsacc[hh, rows, :] += ds
                    ds_scr[slot, rows, :] = ds.astype(BF16)
                    p_scr[slot, rows, :] = p.astype(BF16)
                ds16 = ds_scr[slot]
                dqh = jnp.dot(ds16, kpad[win, :], preferred_element_type=F32) * (CHUNK ** -0.5)
                dq = jnp.where(mask, dqh, 0.0) if dq is None else jnp.where(mask, dqh, dq)
                dkacc[win, :] += lax.dot_general(ds16, qm, TN_DIMS, preferred_element_type=F32)
                dvacc[win, :] += lax.dot_general(p_scr[slot], dom, TN_DIMS, preferred_element_type=F32)
            dq_ref[q0:q0 + Q_TILE, :] = dq.astype(BF16)
            dq_sum = _colsum(dq) if dq_sum is None else dq_sum + _colsum(dq)

        @pl.when(i == 0)
        def _():
            sq_ref[...] = dq_sum

        @pl.when(i > 0)
        def _():
            sq_ref[...] += dq_sum

        @pl.when(i == n_i - 1)
        def _():
            dk = dkacc[PAD_ROWS:PAD_ROWS + t, :]
            dv = dvacc[PAD_ROWS:PAD_ROWS + t, :]
            dk_ref[...] = dk.astype(BF16)
            dv_ref[...] = dv.astype(BF16)
            sk_ref[...] = _colsum(dk)
            sv_ref[...] = _colsum(dv)
            rr = lax.broadcasted_iota(jnp.int32, (Q_TILE, Q_TILE), 0)
            cc = lax.broadcasted_iota(jnp.int32, (Q_TILE, Q_TILE), 1)
            rev = jnp.where(rr + cc == Q_TILE - 1, 1.0, 0.0).astype(BF16)
            for hh in range(2):
                acc = None
                for part in _split3(dsacc[hh]):
                    pr = jnp.dot(rev, part, preferred_element_type=F32)
                    acc = pr if acc is None else acc + pr
                wide = jnp.concatenate([acc, jnp.zeros((Q_TILE, TOEP - WINDOW), F32)], axis=1)
                dg_ref[pl.ds(2 * hp + hh, 1), :] = _colsum(pltpu.roll(wide, 0, 1, stride=1, stride_axis=0))

    col = lambda off: pl.BlockSpec((t, 128), lambda h, i: (0, off + h))
    tile = lambda: pl.BlockSpec((STEP_ROWS, 128), lambda h, i: (i, h))
    sums = lambda: pl.BlockSpec((1, 128), lambda h, i: (0, h))
    res = _call(
        body, grid=(4, n_i),
        in_specs=[tile(), col(4), col(8), tile(), pl.BlockSpec((N_HEADS, TOEP), lambda h, i: (0, 0))],
        out_specs=[tile(), col(0), col(0), sums(), sums(), sums(), pl.BlockSpec((N_HEADS, TOEP), lambda h, i: (0, 0))],
        out_shape=[_sds((t, 512), BF16)] * 3 + [_sds((1, 512), F32)] * 3 + [_sds((N_HEADS, TOEP), F32)],
        scratch_shapes=[pltpu.VMEM((PAD_ROWS + t, 128), BF16), pltpu.VMEM((PAD_ROWS + t, 128), BF16),
                        pltpu.VMEM((PAD_ROWS + t, 128), F32), pltpu.VMEM((PAD_ROWS + t, 128), F32),
                        pltpu.VMEM((2, Q_TILE, WINDOW), F32), pltpu.VMEM((2, Q_TILE, WINDOW), F32),
                        pltpu.VMEM((4, Q_TILE, WINDOW), F32), pltpu.VMEM((4, Q_TILE, WINDOW), F32),
                        pltpu.VMEM((4, Q_TILE, WINDOW), BF16), pltpu.VMEM((4, Q_TILE, WINDOW), BF16)],
        sem=("arbitrary", "arbitrary"), name=name, args=(z, z, z, datt, gen), carried=carried)
    return res[:7] if carried is None else (res[:7], res[7:])


def _adamw_math(w, g, m, v):
    m = ADAM_B1 * m + (1.0 - ADAM_B1) * g
    v = ADAM_B2 * v + (1.0 - ADAM_B2) * (g * g)
    m_hat = m / (1.0 - ADAM_B1 ** ADAM_STEP)
    v_hat = v / (1.0 - ADAM_B2 ** ADAM_STEP)
    delta = -ADAM_LR * (m_hat / (jnp.sqrt(v_hat) + ADAM_EPS) + ADAM_WD * w)
    return delta, m, v


def _adamw_many(items, name):
    n = len(items)

    def body(*refs):
        ins, outs = refs[:4 * n], refs[4 * n:]
        for k in range(n):
            w, g, m, v = (r[...] for r in ins[4 * k:4 * k + 4])
            outs[3 * k][...], outs[3 * k + 1][...], outs[3 * k + 2][...] = _adamw_math(w, g, m, v)

    flat = [a for item in items for a in item]
    res = pl.pallas_call(body, out_shape=[_sds(item[0].shape, F32) for item in items for _ in range(3)],
                         name=name)(*flat)
    return [tuple(res[3 * k:3 * k + 3]) for k in range(n)]


def _adamw(w, g, m, v, name, after):
    r, c = w.shape
    tm = next(cand for cand in (512, 352, 256, 128, 64, 32, 16, 8) if r % cand == 0)
    return _rowcall(lambda wv, gv, mv, vv, _: (gv,) + _adamw_math(wv, gv, mv, vv),
                    [(w, c, 0), (g, c, 0), (m, c, 0), (v, c, 0)], [after], [(r, c, F32, c)] * 4, [], name=name, tm=tm)


def _ada_fwd(c_all, w_shard, b_shard, name):
    n = w_shard.shape[1]
    tn = 512

    def body(c_ref, w_ref, b_ref, o_ref, a_ref):
        cv = c_ref[...]
        act = cv * _sigmoid(cv)
        a_ref[...] = act
        o_ref[...] = jnp.dot(act.astype(BF16), w_ref[...].astype(BF16), preferred_element_type=F32) + b_ref[...]

    return pl.pallas_call(
        body, grid=(n // tn,),
        in_specs=[pl.BlockSpec((8, D), lambda j: (0, 0)), pl.BlockSpec((D, tn), lambda j: (0, j)),
                  pl.BlockSpec((1, tn), lambda j: (0, j))],
        out_specs=[pl.BlockSpec((8, tn), lambda j: (0, j)), pl.BlockSpec((8, D), lambda j: (0, 0))],
        out_shape=[_sds((8, n), F32), _sds((8, D), F32)],
        compiler_params=_params(("arbitrary",)), name=name,
    )(c_all, w_shard, b_shard)


def _ada_bwd_adamw(act_t, dmod_shard, w, m, v, name):
    r, c = w.shape
    tm = 2 * ROW_TILE

    def body(a_ref, d_ref, w_ref, m_ref, v_ref, g_ref, dl_ref, nm_ref, nv_ref):
        g = jnp.dot(a_ref[...], d_ref[...], precision=lax.Precision.HIGHEST, preferred_element_type=F32)
        g_ref[...] = g
        dl_ref[...], nm_ref[...], nv_ref[...] = _adamw_math(w_ref[...], g, m_ref[...], v_ref[...])

    blk = pl.BlockSpec((tm, c), lambda i: (i, 0))
    return pl.pallas_call(
        body, grid=(r // tm,),
        in_specs=[pl.BlockSpec((tm, 8), lambda i: (i, 0)), pl.BlockSpec((8, c), lambda i: (0, 0)), blk, blk, blk],
        out_specs=[blk] * 4, out_shape=[_sds((r, c), F32)] * 4,
        compiler_params=_params(("arbitrary",)), name=name,
    )(act_t, dmod_shard, w, m, v)


def _place():
    return lax.axis_index("x"), lax.axis_index("y"), lax.axis_index("c")


def _flip(v, bit):
    return 1 - v if bit else v


VMEM_SPEC = pl.BlockSpec(memory_space=pltpu.VMEM)


def _allgather8(v, name):
    r, c = v.shape

    def body(v_ref, g_ref, tot_ref, send_sems, recv_sems, local_sem):
        x, y, cc = _place()
        sibling = (x, y, 1 - cc)
        chips = [(_flip(x, k & 2), _flip(y, k & 1)) for k in (1, 2, 3)]

        def block(px, py, pc):
            return g_ref.at[4 * px + 2 * py + pc]

        def copy(k, place, to, src=None):
            slot = block(*place)
            return pltpu.make_async_remote_copy(src_ref=slot if src is None else src, dst_ref=slot,
                                                send_sem=send_sems.at[k], recv_sem=recv_sems.at[k],
                                                device_id=to, device_id_type=MESH)

        mine = pltpu.make_async_copy(v_ref, block(x, y, cc), local_sem)
        mine.start()
        first = [copy(0, (x, y, cc), sibling, src=v_ref)]
        first += [copy(1 + j, (x, y, cc), (px, py, cc), src=v_ref) for j, (px, py) in enumerate(chips)]
        for cp in first:
            cp.start()
        passed = [copy(4 + j, (px, py, cc), sibling) for j, (px, py) in enumerate(chips)]
        for j, (px, py) in enumerate(chips):
            copy(1 + j, (px, py, cc), (x, y, cc)).wait_recv()
            passed[j].start()
        copy(0, sibling, (x, y, cc)).wait_recv()
        for j, (px, py) in enumerate(chips):
            copy(4 + j, (px, py, 1 - cc), (x, y, cc)).wait_recv()
        for cp in first + passed:
            cp.wait_send()
        mine.wait()
        tot = g_ref[0]
        for d in range(1, 8):
            tot = tot + g_ref[d]
        tot_ref[...] = tot

    return pl.pallas_call(
        body, in_specs=[VMEM_SPEC], out_specs=[VMEM_SPEC, VMEM_SPEC],
        out_shape=[_sds((8, r, c), F32), _sds((r, c), F32)],
        scratch_shapes=[pltpu.SemaphoreType.DMA((7,)), pltpu.SemaphoreType.DMA((7,)), pltpu.SemaphoreType.DMA],
        compiler_params=pltpu.CompilerParams(vmem_limit_bytes=VMEM_LIMIT), name=name,
    )(v)


def _slot(px, py, swapped):
    return 2 * py + px if swapped else 2 * px + py


def _gather_shards(arrs, swapped, name):
    n = len(arrs)

    def body(*refs):
        ins, outs = refs[:n], refs[n:2 * n]
        send1, recv1, send2, recv2, local_sems = refs[2 * n:]
        x, y, c = _place()
        sibling = (x, y, 1 - c)
        chips = [(_flip(x, k & 2), _flip(y, k & 1)) for k in (1, 2, 3)]
        local_copies, sends = [], []
        for a in range(n):
            h = outs[a].shape[1] // 2
            mine = pl.ds(pl.multiple_of(c * h, 8), h)
            own = _slot(x, y, swapped[a])
            lc = pltpu.make_async_copy(ins[a], outs[a].at[own], local_sems.at[a])
            lc.start()
            local_copies.append(lc)
            for j, (px, py) in enumerate(chips):
                cp = pltpu.make_async_remote_copy(
                    src_ref=ins[a].at[mine], dst_ref=outs[a].at[own, mine], send_sem=send1.at[3 * a + j],
                    recv_sem=recv1.at[3 * a + j], device_id=(px, py, c), device_id_type=MESH)
                cp.start()
                sends.append(cp)
        for a in range(n):
            h = outs[a].shape[1] // 2
            mine = pl.ds(pl.multiple_of(c * h, 8), h)
            for j, (px, py) in enumerate(chips):
                piece = outs[a].at[_slot(px, py, swapped[a]), mine]
                pltpu.make_async_remote_copy(
                    src_ref=piece, dst_ref=piece, send_sem=send1.at[3 * a + j], recv_sem=recv1.at[3 * a + j],
                    device_id=(px, py, c), device_id_type=MESH).wait_recv()
                fwd = pltpu.make_async_remote_copy(
                    src_ref=piece, dst_ref=piece, send_sem=send2.at[3 * a + j], recv_sem=recv2.at[3 * a + j],
                    device_id=sibling, device_id_type=MESH)
                fwd.start()
                sends.append(fwd)
        for a in range(n):
            h = outs[a].shape[1] // 2
            other = pl.ds(pl.multiple_of((1 - c) * h, 8), h)
            for j, (px, py) in enumerate(chips):
                piece = outs[a].at[_slot(px, py, swapped[a]), other]
                pltpu.make_async_remote_copy(
                    src_ref=piece, dst_ref=piece, send_sem=send2.at[3 * a + j], recv_sem=recv2.at[3 * a + j],
                    device_id=sibling, device_id_type=MESH).wait_recv()
        for cp in sends:
            cp.wait_send()
        for lc in local_copies:
            lc.wait()

    dma = lambda k: pltpu.SemaphoreType.DMA((k,))
    return pl.pallas_call(
        body, in_specs=[ANY] * n, out_specs=[ANY] * n,
        out_shape=[_sds((4,) + a.shape, a.dtype) for a in arrs],
        scratch_shapes=[dma(3 * n), dma(3 * n), dma(3 * n), dma(3 * n), dma(n)], name=name,
    )(*arrs)


def _carry_pair_exchange(grads):
    n = len(grads)

    def copies(ins, outs, send_sems, recv_sems):
        x, y, c = _place()
        cps = []
        for a in range(n):
            h = ins[a].shape[1] // 2
            theirs = pl.ds(pl.multiple_of((1 - c) * h, 8), h)
            cps.append(pltpu.make_async_remote_copy(
                src_ref=ins[a].at[:, theirs, :], dst_ref=outs[a], send_sem=send_sems.at[a], recv_sem=recv_sems.at[a],
                device_id=(x, y, 1 - c), device_id_type=MESH))
        return cps

    def start(*refs):
        for cp in copies(*refs):
            cp.start()

    def finish(*refs):
        for cp in copies(*refs):
            cp.wait()

    return _Carried(grads, [_sds((4, g.shape[1] // 2, g.shape[2]), F32) for g in grads], {}, n, start, finish)


def _pair_sum(grad, recv, core, name):
    _, r, c = grad.shape
    h = r // 2

    def body(core_ref, g_ref, r_ref, o_ref):
        o_ref[...] = (g_ref[...] + r_ref[...]).astype(BF16)

    return pl.pallas_call(
        body,
        grid_spec=pltpu.PrefetchScalarGridSpec(
            num_scalar_prefetch=1, grid=(4,),
            in_specs=[pl.BlockSpec((None, h, c), lambda s, core_ref: (s, core_ref[0], 0)),
                      pl.BlockSpec((None, h, c), lambda s, core_ref: (s, 0, 0))],
            out_specs=pl.BlockSpec((None, h, c), lambda s, core_ref: (s, 0, 0))),
        out_shape=_sds((4, h, c), BF16), compiler_params=_params(("arbitrary",)), name=name,
    )(core, grad, recv)


def _carry_chip_exchange(parts, swapped):
    n = len(parts)

    def copies(ins, outs, send_sems, recv_sems):
        x, y, c = _place()
        chips = [(_flip(x, k & 2), _flip(y, k & 1)) for k in (1, 2, 3)]
        cps = []
        for a in range(n):
            for j, (px, py) in enumerate(chips):
                cps.append(pltpu.make_async_remote_copy(
                    src_ref=ins[a].at[_slot(px, py, swapped[a])], dst_ref=outs[a].at[j],
                    send_sem=send_sems.at[3 * a + j], recv_sem=recv_sems.at[3 * a + j],
                    device_id=(px, py, c), device_id_type=MESH))
        return cps

    def start(*refs):
        for cp in copies(*refs):
            cp.start()

    def finish(*refs):
        for cp in copies(*refs):
            cp.wait()

    return _Carried(parts, [_sds((3,) + p.shape[1:], BF16) for p in parts], {}, 3 * n, start, finish)


def _chip_sum(part, recv, slot_core, name):
    _, h, c = part.shape

    def body(sc_ref, p_ref, r_ref, o_ref):
        acc = p_ref[...].astype(F32)
        for j in range(3):
            acc = acc + r_ref[j].astype(F32)
        o_ref[...] = acc

    return pl.pallas_call(
        body,
        grid_spec=pltpu.PrefetchScalarGridSpec(
            num_scalar_prefetch=1, grid=(1,),
            in_specs=[pl.BlockSpec((None, h, c), lambda q, sc_ref: (sc_ref[0], 0, 0)),
                      pl.BlockSpec((3, h, c), lambda q, sc_ref: (0, 0, 0))],
            out_specs=pl.BlockSpec((h, c), lambda q, sc_ref: (sc_ref[1], 0))),
        out_shape=_sds((2 * h, c), F32), compiler_params=_params(("arbitrary",)), name=name,
    )(slot_core, part, recv)


def _carry_pair_share(shards):
    n = len(shards)

    def copies(outs, send_sems, recv_sems, mine):
        x, y, c = _place()
        cps = []
        for a in range(n):
            h = outs[a].shape[0] // 2
            half = outs[a].at[pl.ds(pl.multiple_of((c if mine else 1 - c) * h, 8), h)]
            cps.append(pltpu.make_async_remote_copy(
                src_ref=half, dst_ref=half, send_sem=send_sems.at[a], recv_sem=recv_sems.at[a],
                device_id=(x, y, 1 - c), device_id_type=MESH))
        return cps

    def start(ins, outs, send_sems, recv_sems):
        for cp in copies(outs, send_sems, recv_sems, True):
            cp.start()

    def finish(ins, outs, send_sems, recv_sems):
        for cp in copies(outs, send_sems, recv_sems, False):
            cp.wait_recv()
        for cp in copies(outs, send_sems, recv_sems, True):
            cp.wait_send()

    return _Carried(shards, [_sds(s.shape, F32) for s in shards], {a: a for a in range(n)}, n, start, finish)


def _carry_gather_ici(bufs, swapped):
    n = len(bufs)

    def copies(outs, send_sems, recv_sems, sending):
        x, y, c = _place()
        cps = []
        for a in range(n):
            h = outs[a].shape[1] // 2
            mine = pl.ds(pl.multiple_of(c * h, 8), h)
            for j, k in enumerate((1, 2, 3)):
                px, py = _flip(x, k & 2), _flip(y, k & 1)
                slot = _slot(x, y, swapped[a]) if sending else _slot(px, py, swapped[a])
                piece = outs[a].at[slot, mine]
                cps.append(pltpu.make_async_remote_copy(
                    src_ref=piece, dst_ref=piece, send_sem=send_sems.at[3 * a + j], recv_sem=recv_sems.at[3 * a + j],
                    device_id=(px, py, c), device_id_type=MESH))
        return cps

    def start(ins, outs, send_sems, recv_sems):
        for cp in copies(outs, send_sems, recv_sems, True):
            cp.start()

    def finish(ins, outs, send_sems, recv_sems):
        for cp in copies(outs, send_sems, recv_sems, False):
            cp.wait_recv()
        for cp in copies(outs, send_sems, recv_sems, True):
            cp.wait_send()

    return _Carried(bufs, [_sds(b.shape, b.dtype) for b in bufs], {a: a for a in range(n)}, 3 * n, start, finish)


HBM_SPEC = pl.BlockSpec(memory_space=pltpu.HBM)
SEM_SPEC = pl.BlockSpec(memory_space=pltpu.SEMAPHORE)
SIDE_EFFECT = pltpu.SideEffectType.DATAFLOW_SIDE_EFFECTING


def _ici_pieces(buf, send_sems, recv_sems, swapped, sending):
    x, y, c = _place()
    h = buf.shape[1] // 2
    mine = pl.ds(pl.multiple_of(c * h, 8), h)
    cps = []
    for j, k in enumerate((1, 2, 3)):
        px, py = _flip(x, k & 2), _flip(y, k & 1)
        piece = buf.at[_slot(x, y, swapped) if sending else _slot(px, py, swapped), mine]
        cps.append(pltpu.make_async_remote_copy(src_ref=piece, dst_ref=piece, send_sem=send_sems.at[j],
                                                recv_sem=recv_sems.at[j], device_id=(px, py, c), device_id_type=MESH))
    return cps


def _gather_ici_start(buf, after, swapped, name):
    def body(buf_ref, after_ref, send_sems, recv_sems, thru, token):
        for cp in _ici_pieces(thru, send_sems, recv_sems, swapped, True):
            cp.start()
        token[...] = jnp.zeros_like(token)

    return pl.pallas_call(
        body, name=name,
        out_shape=(pltpu.SemaphoreType.DMA((3,)), pltpu.SemaphoreType.DMA((3,)), pltpu.HBM(buf.shape, buf.dtype),
                   jax.ShapeDtypeStruct((8, 128), F32)),
        in_specs=(HBM_SPEC, ANY), out_specs=(SEM_SPEC, SEM_SPEC, HBM_SPEC, VMEM_SPEC), input_output_aliases={0: 2},
        compiler_params=pltpu.CompilerParams(has_side_effects=SIDE_EFFECT),
    )(pltpu.with_memory_space_constraint(buf, pltpu.HBM), after)


def _gather_ici_wait(send_sems, recv_sems, thru, after, swapped, name):
    def body(thru_ref, send_sems, recv_sems, after_ref, out_ref):
        for cp in _ici_pieces(out_ref, send_sems, recv_sems, swapped, True):
            cp.wait_send()
        for cp in _ici_pieces(out_ref, send_sems, recv_sems, swapped, False):
            cp.wait_recv()

    return pl.pallas_call(
        body, name=name, out_shape=pltpu.HBM(thru.shape, thru.dtype),
        in_specs=(HBM_SPEC, SEM_SPEC, SEM_SPEC, ANY), out_specs=HBM_SPEC, input_output_aliases={0: 0},
        compiler_params=pltpu.CompilerParams(has_side_effects=SIDE_EFFECT),
    )(thru, send_sems, recv_sems, after)


def _all8_copies(buf, send_sems, recv_sems, sending):
    x, y, c = _place()
    cps = []
    for k in range(1, 8):
        px, py, pc = _flip(x, k & 4), _flip(y, k & 2), _flip(c, k & 1)
        slot = buf.at[4 * x + 2 * y + c] if sending else buf.at[4 * px + 2 * py + pc]
        cps.append(pltpu.make_async_remote_copy(src_ref=slot, dst_ref=slot, send_sem=send_sems.at[k - 1],
                                                recv_sem=recv_sems.at[k - 1], device_id=(px, py, pc), device_id_type=MESH))
    return cps


def _all8_start(buf, name):
    def body(buf_ref, send_sems, recv_sems, thru, token):
        for cp in _all8_copies(thru, send_sems, recv_sems, True):
            cp.start()
        token[...] = jnp.zeros_like(token)

    return pl.pallas_call(
        body, name=name,
        out_shape=(pltpu.SemaphoreType.DMA((7,)), pltpu.SemaphoreType.DMA((7,)), pltpu.HBM(buf.shape, buf.dtype),
                   jax.ShapeDtypeStruct((8, 128), F32)),
        in_specs=(HBM_SPEC,), out_specs=(SEM_SPEC, SEM_SPEC, HBM_SPEC, VMEM_SPEC), input_output_aliases={0: 2},
        compiler_params=pltpu.CompilerParams(has_side_effects=SIDE_EFFECT),
    )(pltpu.with_memory_space_constraint(buf, pltpu.HBM))


def _all8_wait(send_sems, recv_sems, thru, after, name):
    def body(thru_ref, send_sems, recv_sems, after_ref, out_ref):
        for cp in _all8_copies(out_ref, send_sems, recv_sems, True):
            cp.wait_send()
        for cp in _all8_copies(out_ref, send_sems, recv_sems, False):
            cp.wait_recv()

    return pl.pallas_call(
        body, name=name, out_shape=pltpu.HBM(thru.shape, thru.dtype),
        in_specs=(HBM_SPEC, SEM_SPEC, SEM_SPEC, ANY), out_specs=HBM_SPEC, input_output_aliases={0: 0},
        compiler_params=pltpu.CompilerParams(has_side_effects=SIDE_EFFECT),
    )(thru, send_sems, recv_sems, after)


def _sum8(g, name):
    def body(g_ref, o_ref):
        tot = g_ref[0]
        for d in range(1, 8):
            tot = tot + g_ref[d]
        o_ref[...] = tot

    return pl.pallas_call(body, out_shape=_sds(g.shape[1:], F32), name=name)(g)


def _carry_gather_forward(bufs, swapped):
    n = len(bufs)

    def copies(outs, send_sems, recv_sems, sending):
        x, y, c = _place()
        cps = []
        for a in range(n):
            h = outs[a].shape[1] // 2
            rows = pl.ds(pl.multiple_of((c if sending else 1 - c) * h, 8), h)
            for j, k in enumerate((1, 2, 3)):
                piece = outs[a].at[_slot(_flip(x, k & 2), _flip(y, k & 1), swapped[a]), rows]
                cps.append(pltpu.make_async_remote_copy(
                    src_ref=piece, dst_ref=piece, send_sem=send_sems.at[3 * a + j], recv_sem=recv_sems.at[3 * a + j],
                    device_id=(x, y, 1 - c), device_id_type=MESH))
        return cps

    def start(ins, outs, send_sems, recv_sems):
        for cp in copies(outs, send_sems, recv_sems, True):
            cp.start()

    def finish(ins, outs, send_sems, recv_sems):
        for cp in copies(outs, send_sems, recv_sems, False):
            cp.wait_recv()
        for cp in copies(outs, send_sems, recv_sems, True):
            cp.wait_send()

    return _Carried(bufs, [_sds(b.shape, b.dtype) for b in bufs], {a: a for a in range(n)}, 3 * n, start, finish)


def _pack(arrs, rows_multiple=8):
    parts, offs, row = [], [], 0
    for a in arrs:
        flat = a.reshape(-1)
        nrow = -(-flat.shape[0] // D)
        parts.append(jnp.pad(flat, (0, nrow * D - flat.shape[0])))
        offs.append(row)
        row += nrow
    total = -(-row // rows_multiple) * rows_multiple
    if total > row:
        parts.append(jnp.zeros(((total - row) * D,), F32))
    return jnp.concatenate(parts).reshape(total, D), offs


def _unpack(packed, offs, shapes):
    out = []
    for off, shp in zip(offs, shapes):
        size = int(np.prod(shp))
        nrow = -(-size // D)
        out.append(packed[off:off + nrow].reshape(-1)[:size].reshape(shp))
    return out


def _to_bf16_slot(w, slot, name, after=None):
    r, c = w.shape
    tm = next(cand for cand in (512, 352, 256, 128, 64, 32, 16) if r % cand == 0)

    def body(slot_ref, w_ref, *rest):
        rest[-1][...] = w_ref[...].astype(BF16)

    in_specs = [pl.BlockSpec((tm, c), lambda i, slot_ref: (i, 0))]
    if after is not None:
        in_specs.append(pl.BlockSpec((8, 128), lambda i, slot_ref: (0, 0)))
    return pl.pallas_call(
        body,
        grid_spec=pltpu.PrefetchScalarGridSpec(
            num_scalar_prefetch=1, grid=(r // tm,), in_specs=in_specs,
            out_specs=pl.BlockSpec((None, tm, c), lambda i, slot_ref: (slot_ref[0], i, 0))),
        out_shape=_sds((4, r, c), BF16), compiler_params=_params(("arbitrary",)), name=name,
    )(slot, w, *([] if after is None else [after]))


def _unshard_cols(g):
    s, k, n = g.shape
    return jnp.transpose(g, (1, 0, 2)).reshape(k, s * n)


def _ff_swap(v):
    b = FF_BLOCK
    return jnp.concatenate([v[..., 0:b], v[..., 2 * b:3 * b], v[..., b:2 * b], v[..., 3 * b:4 * b]], axis=-1)


LATE = ("attn_o", "conv_o", "mix_o", "up", "down")
EARLY_GRADS = ("down", "up", "mix_o", "attn_o", "conv_o")


def _weight_views(bufs):
    return {"up": bufs["up"], "attn_o": _unshard_cols(bufs["attn_o"]), "conv_o": _unshard_cols(bufs["conv_o"]),
            "mix_o": bufs["mix_o"].reshape(D, D), "down": bufs["down"].reshape(D_FF, D)}


def _pair_sums(names, grads, recv, dist):
    return [_pair_sum(g, r, dist["core"], "pair_sum_" + n) for n, g, r in zip(names, grads, recv)]


def _reduce_halves(names, parts, from_chips, dist):
    return [_chip_sum(p, r, jnp.concatenate([dist["slots"][SWAPPED[n]], dist["core"]]), "chip_sum_" + n)
            for n, p, r in zip(names, parts, from_chips)]


FUSED_TILE = 256
WIDE_TILE = 512


def _gates(z):
    return [(z, 512, 5), (z, 512, 6), (z, 512, 7), (z, 512, 8)]


def _mix_out(a, cb, z, x, w_mix_o, g_post, gt, g_pre2, sc2, sh2, name):
    def lhs(av, cv, ga0, ga1, gb0, gb1):
        ga, gb = jnp.concatenate([ga0, ga1], axis=1), jnp.concatenate([gb0, gb1], axis=1)
        return _sigmoid(ga) * av + _sigmoid(gb) * cv

    def fn(ym, y, xv, gv, gtv, g2v, scv, shv):
        yn, _ = _rms_stats(ym)
        x1 = xv + gtv * (yn * gv)
        xn, _ = _rms_stats(x1)
        return ym, y, x1, (xn * g2v) * (1.0 + scv) + shv

    return _matmul_rows(w_mix_o, form="nn", tm=min(WIDE_TILE, x.shape[0]), tk=D, fn=fn, a_rows=[(a, D, 0), (cb, D, 0)] + _gates(z),
                        a_fn=lhs, rows=[(x, D, 0)], consts=[g_post, gt, g_pre2, sc2, sh2],
                        row_outs=[(F32, D), (BF16, D), (F32, D), (BF16, D)], acc_outs=[], name=name)


def _down_tail(act, w_down, x1, target, g, gt, name):
    def fn(yv, xv, tv, gv, gtv):
        yn, r = _rms_stats(yv)
        e = xv + gtv * (yn * gv) - tv
        dx2 = e * (1.0 / D)
        dyn = dx2 * gtv
        return (dx2, _rms_bwd(dyn * gv, yn, r), _colsum(e * e) * (0.5 / D), _colsum(dyn * yn),
                _colsum(dx2 * (yn * gv)))

    return _matmul_rows(w_down, form="nn", a=act, tm=min(WIDE_TILE, x1.shape[0]), tk=D_FF, fn=fn,
                        rows=[(x1, D, 0), (target, D, 0)], consts=[g, gt], row_outs=[(F32, D), (BF16, D)],
                        acc_outs=[(1, D)] * 3, name=name)


def _up_dx_tail(dup, w_up, x1, dx2, ym, g_pre2, sc2, g_post, gt, name):
    def fn(dh, xv, dov, ymv, g2v, scv, gv, gtv):
        xn, r = _rms_stats(xv)
        dyn = dh * (1.0 + scv)
        dx1 = dov + _rms_bwd(dyn * g2v, xn, r)
        yn, r2 = _rms_stats(ymv)
        dynm = dx1 * gtv
        return (dx1, _rms_bwd(dynm * gv, yn, r2), _colsum(dh), _colsum(dh * (xn * g2v)), _colsum(dyn * xn),
                _colsum(dynm * yn), _colsum(dx1 * (yn * gv)))

    return _matmul_rows(w_up, form="nt", a=dup, tm=min(FUSED_TILE, x1.shape[0]), tk=2 * D_FF, fn=fn,
                        rows=[(x1, D, 0), (dx2, D, 0), (ym, D, 0)], consts=[g_pre2, sc2, g_post, gt],
                        row_outs=[(F32, D), (BF16, D)], acc_outs=[(1, D)] * 5, name=name)


def _mix_dx_gates(dym, w_mix_o, a, cb, z, name):
    def fn(dy, av, cv, ga0, ga1, gb0, gb1):
        sa = _sigmoid(jnp.concatenate([ga0, ga1], axis=1))
        sb = _sigmoid(jnp.concatenate([gb0, gb1], axis=1))
        dcb = dy * sb
        dga = dy * av * (sa * (1.0 - sa))
        dgb = dy * cv * (sb * (1.0 - sb))
        return dy * sa, dcb, dga, dgb, _colsum(dcb), _colsum(dga), _colsum(dgb)

    return _matmul_rows(w_mix_o, form="nt", a=dym, tm=min(WIDE_TILE, a.shape[0]), tk=D, fn=fn,
                        rows=[(a, D, 0), (cb, D, 0)] + _gates(z), consts=[], row_outs=[(BF16, D)] * 4,
                        acc_outs=[(1, D)] * 3, name=name)


def _local_step(x, target, mod, w_in, late, small, dist=None):
    sh_m, sc_m, gt_m, sh_f, sc_f, gt_f = mod
    t = x.shape[0]
    tmm = min(1024, t)
    late_swapped = [SWAPPED[n] for n in LATE]

    h1 = _pre_norm(x, small["g_pre_mix"], sc_m, sh_m, "pre_norm_mix")
    if callable(w_in):
        w_in = w_in(h1)
    z = _matmul(h1, w_in, form="nn", out_dtype=F32, tm=min(FUSED_TILE, t), tn=D_IN, tk=D, bias=small["b_in"], name="mm_in")
    conv = (z, small["w_dw_conv"], small["b_dw_conv"], small["g_conv_ln"], small["b_conv_ln"], "conv_branch")
    if dist is None:
        att = _attention(z, small["gen"], "attention")
        u1, u3 = _conv_branch(*conv)
        bufs = dict(late)
    else:
        mid = [n for n in LATE if n != "down"]
        mid_swapped = [SWAPPED[n] for n in mid]
        att, landed = _attention(z, small["gen"], "attention",
                                 carried=_carry_gather_ici([late[n] for n in mid], mid_swapped))
        (u1, u3), gathered = _conv_branch(*conv, carried=_carry_gather_forward(landed, mid_swapped))
        bufs = dict(zip(mid, gathered))
        bufs["down"] = late["down"]
    w = _weight_views(bufs)
    w["in"] = w_in
    a = _matmul(att, w["attn_o"], form="nn", out_dtype=F32, tm=tmm, tn=512, tk=512, name="mm_attn_o")
    cb = _matmul(u3, w["conv_o"], form="nn", out_dtype=F32, tm=tmm, tn=512, tk=512, bias=small["b_conv_o"], name="mm_conv_o")
    ym, y, x1, h2 = _mix_out(a, cb, z, x, w["mix_o"], small["g_post_mix"], gt_m, small["g_pre_ffn"], sc_f, sh_f, "mix_out")
    mm_up = dict(form="nn", out_dtype=F32, tm=min(FUSED_TILE, t), tn=2 * D_FF, tk=D, name="mm_up")
    ffn_act = (small["w_dw_ffn"], small["b_dw_ffn"], "ffn_act")
    if dist is None:
        up = _matmul(h2, w["up"], **mm_up)
        act = _ffn_act(up, *ffn_act)
    else:
        up, landed = _matmul(h2, w["up"], carried=_carry_gather_ici([late["down"]], [False]), **mm_up)
        act, down = _ffn_act(up, *ffn_act, carried=_carry_gather_forward(landed, [False]))
        w["down"] = down[0].reshape(D_FF, D)

    dx2, dyf, loss_cols, d_g_post_ffn, d_gt_f = _down_tail(act, w["down"], x1, target, small["g_post_ffn"], gt_f, "down_tail")
    dact = _matmul(dyf, w["down"], form="nt", out_dtype=F32, tm=tmm, tn=FF_BLOCK, tk=D, name="mm_down_dx")
    g_down = _matmul(act, dyf, form="tn", out_dtype=F32, tm=FF_BLOCK, tn=512, tk=t, name="mm_down_dw")
    dup, d_w_dw_ffn, d_b_dw_ffn = _ffn_act_bwd(dact, up, small["w_dw_ffn"], small["b_dw_ffn"], "ffn_act_bwd")
    dx1, dym, d_sh_f, d_sc_f, d_g_pre_ffn, d_g_post_mix, d_gt_m = _up_dx_tail(
        dup, w["up"], x1, dx2, ym, small["g_pre_ffn"], sc_f, small["g_post_mix"], gt_m, "up_dx_tail")
    g_up = _matmul(h2, dup, form="tn", out_dtype=F32, tm=512, tn=FF_BLOCK, tk=t, out_sharded=True, name="mm_up_dw")
    da, dcb, dgate_a, dgate_b, d_b_conv_o, sga, sgb = _mix_dx_gates(dym, w["mix_o"], a, cb, z, "mix_dx_gates")
    g_mix_o = _matmul(y, dym, form="tn", out_dtype=F32, tm=D, tn=512, tk=t, name="mm_mix_o_dw")
    datt = _matmul(da, w["attn_o"], form="nt", out_dtype=F32, tm=tmm, tn=512, tk=D, name="mm_attn_o_dx")
    g_attn_o = _matmul(att, da, form="tn", out_dtype=F32, tm=512, tn=256, tk=t, out_sharded=True, name="mm_attn_o_dw")
    du3 = _matmul(dcb, w["conv_o"], form="nt", out_dtype=F32, tm=tmm, tn=512, tk=D, name="mm_conv_o_dx")
    g_conv_o = _matmul(u3, dcb, form="tn", out_dtype=F32, tm=512, tn=256, tk=t, out_sharded=True, name="mm_conv_o_dw")
    big = {"attn_o": g_attn_o, "conv_o": g_conv_o, "mix_o": g_mix_o.reshape(4, 256, D),
           "up": g_up, "down": g_down.reshape(4, D_FF // 4, D)}
    conv_bwd = (du3, u1, z, small["w_dw_conv"], small["g_conv_ln"], small["b_conv_ln"], "conv_branch_bwd")
    in_dw = dict(form="tn", out_dtype=F32, tm=512, tn=IN_SHARD, tk=t, out_sharded=True, name="mm_in_dw")
    in_dx = dict(form="nt", out_dtype=F32, tm=min(WIDE_TILE, t), tn=D, tk=D_IN, name="mm_in_dx")
    if dist is None:
        dglu, d_w_dw_conv, d_b_dw_conv, d_g_conv_ln, d_b_conv_ln, sglu = _conv_branch_bwd(*conv_bwd)
        dq, dk, dv, sq, sk, sv, dgen = _attention_bwd(z, datt, small["gen"], "attention_bwd")
        dz = jnp.concatenate([dq, dk, dv, dglu, dgate_a, dgate_b], axis=1)
        big["in"] = _matmul(h1, dz, **in_dw)
        dh1 = _matmul(dz, w_in, **in_dx)
    else:
        early = [big[n] for n in EARLY_GRADS]
        (dglu, d_w_dw_conv, d_b_dw_conv, d_g_conv_ln, d_b_conv_ln, sglu), recv = _conv_branch_bwd(
            *conv_bwd, carried=_carry_pair_exchange(early))
        parts = _pair_sums(EARLY_GRADS, early, recv, dist)
        (dq, dk, dv, sq, sk, sv, dgen), from_chips = _attention_bwd(
            z, datt, small["gen"], "attention_bwd",
            carried=_carry_chip_exchange(parts, [SWAPPED[n] for n in EARLY_GRADS]))
        halves = _reduce_halves(EARLY_GRADS, parts, from_chips, dist)
        dz = jnp.concatenate([dq, dk, dv, dglu, dgate_a, dgate_b], axis=1)
        g_in, shards = _matmul(h1, dz, carried=_carry_pair_share(halves), **in_dw)
        big = dict(zip(EARLY_GRADS, shards))
        recv_in = _run_carried(_carry_pair_exchange([g_in]), "pair_exchange_in")
        part_in = _pair_sums(("in",), [g_in], recv_in, dist)
        dh1, from_chips_in = _matmul(dz, w_in, carried=_carry_chip_exchange(part_in, [False]), **in_dx)
        half_in = _reduce_halves(("in",), part_in, from_chips_in, dist)
        big["in"] = _run_carried(_carry_pair_share(half_in), "pair_share_in")[0]
    d_b_in = jnp.concatenate([sq, sk, sv, sglu, sga, sgb], axis=1)
    grad_x, d_sh_m, d_sc_m, d_g_pre_mix = _pre_norm_bwd(dh1, x, dx1, small["g_pre_mix"], sc_m, "pre_norm_mix_bwd")

    dmod = [d_sh_m, d_sc_m, d_gt_m, d_sh_f, d_sc_f, d_gt_f]
    sm = {"g_pre_mix": d_g_pre_mix, "g_post_mix": d_g_post_mix, "b_in": d_b_in, "gen": dgen,
          "w_dw_conv": d_w_dw_conv, "b_dw_conv": d_b_dw_conv, "g_conv_ln": d_g_conv_ln, "b_conv_ln": d_b_conv_ln,
          "b_conv_o": d_b_conv_o, "g_pre_ffn": d_g_pre_ffn, "g_post_ffn": d_g_post_ffn,
          "w_dw_ffn": d_w_dw_ffn, "b_dw_ffn": d_b_dw_ffn}
    return loss_cols, grad_x, dmod, big, sm


BIG = ("in", "attn_o", "conv_o", "mix_o", "up", "down")
SWAPPED = {"in": False, "attn_o": False, "conv_o": False, "mix_o": False, "up": True, "down": False}
SMALL_ORDER = ("b_ada", "g_pre_mix", "g_post_mix", "b_in", "rel_bias", "b_dw_conv", "g_conv_ln", "b_conv_ln",
               "b_conv_o", "g_pre_ffn", "g_post_ffn", "b_dw_ffn", "w_dw_conv", "w_dw_ffn")


def kernel(x, c, w_ada, b_ada, g_pre_mix, g_post_mix, w_in, b_in, rel_bias, w_attn_o, w_dw_conv, b_dw_conv, g_conv_ln, b_conv_ln, w_conv_o, b_conv_o, w_mix_o, g_pre_ffn, g_post_ffn, w_up, w_dw_ffn, b_dw_ffn, w_down, loss_target, m_w_ada, m_b_ada, m_g_pre_mix, m_g_post_mix, m_w_in, m_b_in, m_rel_bias, m_w_attn_o, m_w_dw_conv, m_b_dw_conv, m_g_conv_ln, m_b_conv_ln, m_w_conv_o, m_b_conv_o, m_w_mix_o, m_g_pre_ffn, m_g_post_ffn, m_w_up, m_w_dw_ffn, m_b_dw_ffn, m_w_down, v_w_ada, v_b_ada, v_g_pre_mix, v_g_post_mix, v_w_in, v_b_in, v_rel_bias, v_w_attn_o, v_w_dw_conv, v_b_dw_conv, v_g_conv_ln, v_b_conv_ln, v_w_conv_o, v_b_conv_o, v_w_mix_o, v_g_pre_ffn, v_g_post_ffn, v_w_up, v_w_dw_ffn, v_b_dw_ffn, v_w_down):
    given = dict(locals())
    ax, ay, ac = lax.axis_index("x"), lax.axis_index("y"), lax.axis_index("c")
    shard = 2 * ax + ay
    me = 4 * ax + 2 * ay + ac
    xs, target = x[0], loss_target[0]

    slots = {sw: _slot(ax, ay, sw).astype(jnp.int32).reshape(1) for sw in (False, True)}
    own = {"in": _to_bf16_slot(w_in[0], slots[False], "cast_in")}

    c_pad = jnp.pad(c, ((0, 7), (0, 0)))
    c_g, _ = _allgather8(c_pad, "gather_c")
    c_all = c_g[:, 0, :]
    b_ada_shard = lax.dynamic_slice(b_ada, (0, shard * ADA_SHARD), (1, ADA_SHARD))
    mod_shard, c_act = _ada_fwd(c_all, w_ada[0], b_ada_shard, "ada_fwd")
    small_in = [jnp.pad(mod_shard, ((0, 8), (0, 0))),
                jnp.pad(w_dw_conv[0], ((0, 1), (0, 0))),
                jnp.pad(w_dw_ffn[0], ((0, 13), (0, 0)))]
    mod_g, wdc_g, wdf_g = _gather_shards(small_in, [False, False, True], "gather_small")
    mod_all = jnp.transpose(mod_g[:, :8, :], (1, 0, 2)).reshape(8, 6 * D)
    in_send, in_recv, in_flight, token = _gather_ici_start(own["in"], mod_g, False, "gather_w_in_start")

    def w_in_ready(after):
        landed = _gather_ici_wait(in_send, in_recv, in_flight, after, False, "gather_w_in_wait")
        return _run_carried(_carry_gather_forward([landed], [False]), "gather_forward_in")[0]

    for n in LATE:
        own[n] = _to_bf16_slot(given["w_" + n][0], slots[SWAPPED[n]], "cast_" + n, after=token)
    mod_row = lax.dynamic_slice(mod_all, (me, 0), (1, 6 * D)) + token[0:1, 0:1]
    mod = [mod_row[:, k * D:(k + 1) * D] for k in range(6)]

    core = ac.astype(jnp.int32).reshape(1)
    dist = {"core": core, "slots": slots}

    sel = jnp.asarray(_toeplitz_map())
    rel_pad = jnp.pad(rel_bias[0], ((0, 0), (0, REL_PAD - (2 * MAX_REL + 1))))
    gen = _select_call(rel_pad, sel.T.astype(BF16), "bias_rows")
    small = {"g_pre_mix": g_pre_mix, "g_post_mix": g_post_mix, "b_in": b_in, "gen": gen,
             "w_dw_conv": _unshard_cols(wdc_g[:, :CONV_K, :]), "b_dw_conv": b_dw_conv, "g_conv_ln": g_conv_ln,
             "b_conv_ln": b_conv_ln, "b_conv_o": b_conv_o, "g_pre_ffn": g_pre_ffn, "g_post_ffn": g_post_ffn,
             "w_dw_ffn": _unshard_cols(wdf_g[:, :FFN_K, :]), "b_dw_ffn": _ff_swap(b_dw_ffn)}

    loss_cols, grad_x, dmod, reduced, sm = _local_step(xs, target, mod, w_in_ready, {n: own[n] for n in LATE}, small, dist)

    d_rel = _select_call(sm["gen"], sel.astype(BF16), "bias_fold")[:, :2 * MAX_REL + 1]
    small_grads = {"g_pre_mix": sm["g_pre_mix"], "g_post_mix": sm["g_post_mix"], "b_in": sm["b_in"], "rel_bias": d_rel[None],
                   "b_dw_conv": sm["b_dw_conv"], "g_conv_ln": sm["g_conv_ln"], "b_conv_ln": sm["b_conv_ln"],
                   "b_conv_o": sm["b_conv_o"], "g_pre_ffn": sm["g_pre_ffn"], "g_post_ffn": sm["g_post_ffn"],
                   "b_dw_ffn": _ff_swap(sm["b_dw_ffn"]), "w_dw_conv": sm["w_dw_conv"], "w_dw_ffn": _ff_swap(sm["w_dw_ffn"])}
    order = [n for n in SMALL_ORDER if n != "b_ada"]
    packed, offs = _pack([jnp.concatenate(dmod, axis=1)] + [small_grads[n] for n in order] + [loss_cols])
    mine = lax.dynamic_update_slice(jnp.zeros((8,) + packed.shape, F32), packed[None], (me, 0, 0))
    sg_send, sg_recv, sg_flight, sg_token = _all8_start(mine, "gather_small_grads_start")

    out = {}
    for n in BIG:
        g, dl, nm, nv = _adamw(given["w_" + n][0], reduced[n], given["m_w_" + n][0], given["v_w_" + n][0],
                               "adamw_" + n, sg_token)
        out["grad_w_" + n], out["delta_w_" + n], out["new_m_w_" + n], out["new_v_w_" + n] = g[None], dl[None], nm[None], nv[None]
    every = _all8_wait(sg_send, sg_recv, sg_flight, out["delta_w_in"], "gather_small_grads_wait")
    total = _sum8(every, "sum_small_grads")
    loss = jnp.sum(total[offs[-1]])
    offs = offs[:-1]
    dmod_all = every[:, 0:6, :].reshape(8, 6 * D)
    full_shapes = {n: given[n].shape for n in order}
    full_shapes["w_dw_conv"], full_shapes["w_dw_ffn"] = (1, CONV_K, 512), (1, FFN_K, 2 * D_FF)
    sums = dict(zip(order, _unpack(total, offs[1:], [full_shapes[n] for n in order])))
    sums["b_ada"] = total[0:6].reshape(1, 6 * D)
    sums["w_dw_conv"] = lax.dynamic_slice(sums["w_dw_conv"], (0, 0, shard * 128), (1, CONV_K, 128))
    sums["w_dw_ffn"] = lax.dynamic_slice(sums["w_dw_ffn"], (0, 0, shard * FF_BLOCK), (1, FFN_K, FF_BLOCK))

    upd = dict(zip(SMALL_ORDER, _adamw_many(
        [(given[n], sums[n], given["m_" + n], given["v_" + n]) for n in SMALL_ORDER], "adamw_small")))

    dmod_shard = lax.dynamic_slice(dmod_all, (0, shard * ADA_SHARD), (8, ADA_SHARD))
    ada = _ada_bwd_adamw(c_act.T, dmod_shard, w_ada[0], m_w_ada[0], v_w_ada[0], "ada_bwd_adamw")

    out.update({"grad_w_ada": ada[0][None], "delta_w_ada": ada[1][None], "new_m_w_ada": ada[2][None],
                "new_v_w_ada": ada[3][None]})
    for n in SMALL_ORDER:
        out["grad_" + n], out["delta_" + n], out["new_m_" + n], out["new_v_" + n] = sums[n], *upd[n]

    weights = ["w_ada", "b_ada", "g_pre_mix", "g_post_mix", "w_in", "b_in", "rel_bias", "w_attn_o", "w_dw_conv", "b_dw_conv",
               "g_conv_ln", "b_conv_ln", "w_conv_o", "b_conv_o", "w_mix_o", "g_pre_ffn", "g_post_ffn", "w_up", "w_dw_ffn",
               "b_dw_ffn", "w_down"]
    return (loss, grad_x[None], *[out["grad_" + n] for n in weights], *[out["delta_" + n] for n in weights],
            *[out["new_m_" + n] for n in weights], *[out["new_v_" + n] for n in weights])
```

```python
import functools
import math

import numpy as np
import jax
import jax.numpy as jnp
from jax import lax
from jax.experimental import pallas as pl
from jax.experimental.pallas import tpu as pltpu

F32, BF16 = jnp.float32, jnp.bfloat16
MESH = pl.DeviceIdType.MESH

D = 1024
D_IN = 4608
D_FF = 2816
N_CHIPS = 4
IN_SHARD = D_IN // N_CHIPS
ADA_SHARD = 6 * D // N_CHIPS
CONV_K = 31
FFN_K = 3
N_HEADS = 8
CHUNK = 64
LEFT_CHUNKS = 8
MAX_REL = 128
EPS = 1e-6
NEG_INF = -1e30
Q_TILE = 256
WINDOW = Q_TILE + LEFT_CHUNKS * CHUNK
STEP_ROWS = 1024
SCORE_SLOTS = 2 * STEP_ROWS // Q_TILE
REL_PAD = 384
TOEP = 1024
ROW_TILE = 256
WORK_TILE = 512
VMEM_LIMIT = 60 * 1024 * 1024

ADAM_LR, ADAM_B1, ADAM_B2, ADAM_EPS, ADAM_WD, ADAM_STEP = 0.001, 0.9, 0.999, 1e-08, 0.01, 10


def _params(sem=None):
    return pltpu.CompilerParams(dimension_semantics=sem, vmem_limit_bytes=VMEM_LIMIT)


def _sds(shape, dtype):
    return jax.ShapeDtypeStruct(tuple(shape), dtype)


ANY = pl.BlockSpec(memory_space=pl.ANY)


class _Carried:
    def __init__(self, ins, out_shapes, aliases, n_sems, start, finish):
        self.ins, self.out_shapes, self.aliases = list(ins), list(out_shapes), dict(aliases)
        self.n_sems, self.start, self.finish = n_sems, start, finish


def _call(body, *, grid, in_specs, out_specs, out_shape, scratch_shapes, sem, name, args, carried=None):
    in_specs, out_specs, out_shape = list(in_specs), list(out_specs), list(out_shape)
    scratch_shapes = list(scratch_shapes)
    if carried is None:
        return pl.pallas_call(body, grid=grid, in_specs=in_specs, out_specs=out_specs, out_shape=out_shape,
                              scratch_shapes=scratch_shapes, compiler_params=_params(sem), name=name)(*args)
    n_in, n_out, n_scr = len(in_specs), len(out_specs), len(scratch_shapes)
    c_in, c_out = len(carried.ins), len(carried.out_shapes)

    def full(*refs):
        pos = [0]

        def take(k):
            part = refs[pos[0]:pos[0] + k]
            pos[0] += k
            return part

        ins, cins, outs, couts, scr = take(n_in), take(c_in), take(n_out), take(c_out), take(n_scr)
        send_sems, recv_sems = take(2)
        first = last = None
        for d, size in enumerate(grid):
            pid = pl.program_id(d)
            first = (pid == 0) if first is None else first & (pid == 0)
            last = (pid == size - 1) if last is None else last & (pid == size - 1)

        @pl.when(first)
        def _():
            carried.start(cins, couts, send_sems, recv_sems)

        body(*ins, *outs, *scr)

        @pl.when(last)
        def _():
            carried.finish(cins, couts, send_sems, recv_sems)

    sems = [pltpu.SemaphoreType.DMA((carried.n_sems,)), pltpu.SemaphoreType.DMA((carried.n_sems,))]
    return pl.pallas_call(
        full, grid=grid, in_specs=in_specs + [ANY] * c_in, out_specs=out_specs + [ANY] * c_out,
        out_shape=out_shape + carried.out_shapes, scratch_shapes=scratch_shapes + sems,
        input_output_aliases={n_in + k: n_out + v for k, v in carried.aliases.items()},
        compiler_params=_params(tuple("arbitrary" for _ in grid)), name=name,
    )(*args, *carried.ins)


def _run_carried(carried, name):
    c_in = len(carried.ins)

    def body(*refs):
        cins, couts = refs[:c_in], refs[c_in:c_in + len(carried.out_shapes)]
        send_sems, recv_sems = refs[-2:]
        carried.start(cins, couts, send_sems, recv_sems)
        carried.finish(cins, couts, send_sems, recv_sems)

    return pl.pallas_call(
        body, in_specs=[ANY] * c_in, out_specs=[ANY] * len(carried.out_shapes), out_shape=carried.out_shapes,
        scratch_shapes=[pltpu.SemaphoreType.DMA((carried.n_sems,)), pltpu.SemaphoreType.DMA((carried.n_sems,))],
        input_output_aliases=carried.aliases, name=name,
    )(*carried.ins)


def _matmul(a, b, *, form, out_dtype, tm, tn, tk, name, bias=None, add=None, out_sharded=False, carried=None):
    b3 = b.ndim == 3
    resident = 0
    if form == "nn":
        m, k = a.shape
        n = b.shape[0] * b.shape[2] if b3 else b.shape[1]
        dn = (((1,), (0,)), ((), ()))
        a_spec = pl.BlockSpec((tm, tk), lambda i, j, kk: (i, kk))
        if b3 and tn == n and tk == k:
            resident = b.shape[0]
            b_spec = pl.BlockSpec(b.shape, lambda i, j, kk: (0, 0, 0))
        else:
            b_spec = (pl.BlockSpec((None, tk, tn), lambda i, j, kk: (j, kk, 0)) if b3
                      else pl.BlockSpec((tk, tn), lambda i, j, kk: (kk, j)))
    elif form == "nt":
        m, k = a.shape
        n = b.shape[1] if b3 else b.shape[0]
        dn = (((1,), (1,)), ((), ()))
        a_spec = pl.BlockSpec((tm, tk), lambda i, j, kk: (i, kk))
        if b3 and tk == k:
            resident = b.shape[0]
            b_spec = pl.BlockSpec((resident, tn, b.shape[2]), lambda i, j, kk: (0, j, 0))
        else:
            b_spec = (pl.BlockSpec((None, tn, tk), lambda i, j, kk: (kk, j, 0)) if b3
                      else pl.BlockSpec((tn, tk), lambda i, j, kk: (j, kk)))
    else:
        k, m = a.shape
        n = b.shape[1]
        dn = (((0,), (0,)), ((), ()))
        a_spec = pl.BlockSpec((tk, tm), lambda i, j, kk: (kk, i))
        b_spec = pl.BlockSpec((tk, tn), lambda i, j, kk: (kk, j))
    assert m % tm == 0 and n % tn == 0 and k % tk == 0, (name, m, n, k, tm, tn, tk)
    nk = k // tk
    in_specs, args = [a_spec, b_spec], [a, b]
    if bias is not None:
        in_specs.append(pl.BlockSpec((1, tn), lambda i, j, kk: (0, j)))
        args.append(bias)
    if add is not None:
        in_specs.append(pl.BlockSpec((tm, tn), lambda i, j, kk: (i, j)))
        args.append(add)
    if out_sharded:
        out_shape = _sds((n // tn, m, tn), out_dtype)
        out_spec = pl.BlockSpec((None, tm, tn), lambda i, j, kk: (j, i, 0))
    else:
        out_shape = _sds((m, n), out_dtype)
        out_spec = pl.BlockSpec((tm, tn), lambda i, j, kk: (i, j))

    def body(*refs):
        a_ref, b_ref = refs[0], refs[1]
        pos = 2
        bias_ref = add_ref = None
        if bias is not None:
            bias_ref, pos = refs[pos], pos + 1
        if add is not None:
            add_ref, pos = refs[pos], pos + 1
        o_ref = refs[pos]
        if resident and form == "nn":
            ns = b_ref.shape[2]
            for s in range(resident):
                cols = slice(s * ns, (s + 1) * ns)
                ps = lax.dot_general(a_ref[...], b_ref[s], dn, preferred_element_type=F32)
                if bias_ref is not None:
                    ps = ps + bias_ref[:, cols]
                o_ref[:, cols] = ps.astype(o_ref.dtype)
            return
        if resident:
            ks = b_ref.shape[2]
            p = None
            for s in range(resident):
                ps = lax.dot_general(a_ref[:, s * ks:(s + 1) * ks], b_ref[s], dn, preferred_element_type=F32)
                p = ps if p is None else p + ps
        else:
            av, bv = a_ref[...], b_ref[...]
            if av.dtype != BF16:
                av = av.astype(BF16)
            if bv.dtype != BF16:
                bv = bv.astype(BF16)
            p = lax.dot_general(av, bv, dn, preferred_element_type=F32)

        def finish(acc):
            if bias_ref is not None:
                acc = acc + bias_ref[...]
            if add_ref is not None:
                acc = acc + add_ref[...]
            o_ref[...] = acc.astype(o_ref.dtype)

        if nk == 1:
            finish(p)
        else:
            acc_ref = refs[pos + 1]
            kk = pl.program_id(2)

            @pl.when(kk == 0)
            def _():
                acc_ref[...] = p

            @pl.when(kk > 0)
            def _():
                acc_ref[...] += p

            @pl.when(kk == nk - 1)
            def _():
                finish(acc_ref[...])

    res = _call(body, grid=(m // tm, n // tn, nk), in_specs=in_specs, out_specs=[out_spec], out_shape=[out_shape],
                scratch_shapes=[pltpu.VMEM((tm, tn), F32)] if nk > 1 else [],
                sem=("parallel", "parallel", "arbitrary"), name=name, args=args, carried=carried)
    return res[0] if carried is None else (res[0], res[1:])


def _rowcall(fn, rows, consts, row_outs, acc_outs, *, name, tm=ROW_TILE, col_grid=1):
    n_rows = rows[0][0].shape[0]
    assert n_rows % tm == 0
    grid = (col_grid, n_rows // tm)
    in_specs = [pl.BlockSpec((tm, w), functools.partial(lambda c, i, cb: (i, cb + c), cb=cb)) for _, w, cb in rows]
    in_specs += [pl.BlockSpec(k.shape, functools.partial(lambda c, i, nd: (0,) * nd, nd=k.ndim)) for k in consts]
    out_specs = [pl.BlockSpec((tm, w), lambda c, i: (i, c)) for _, _, _, w in row_outs]
    out_specs += [pl.BlockSpec((r, w), lambda c, i: (0, c)) for r, _, w in acc_outs]
    out_shape = [_sds((nr, nc), dt) for nr, nc, dt, _ in row_outs] + [_sds((r, nc), F32) for r, nc, _ in acc_outs]
    n_in, n_ro = len(rows) + len(consts), len(row_outs)

    def body(*refs):
        res = fn(*[r[...] for r in refs[:n_in]])
        if not isinstance(res, (tuple, list)):
            res = (res,)
        outs = refs[n_in:]
        for o_ref, val in zip(outs[:n_ro], res[:n_ro]):
            o_ref[...] = val.astype(o_ref.dtype)
        if acc_outs:
            first = pl.program_id(1) == 0

            @pl.when(first)
            def _():
                for o_ref, val in zip(outs[n_ro:], res[n_ro:]):
                    o_ref[...] = val

            @pl.when(jnp.logical_not(first))
            def _():
                for o_ref, val in zip(outs[n_ro:], res[n_ro:]):
                    o_ref[...] += val

    out = pl.pallas_call(
        body, grid=grid, in_specs=in_specs, out_specs=out_specs, out_shape=out_shape,
        compiler_params=_params(("arbitrary", "arbitrary")), name=name,
    )(*[r[0] for r in rows], *consts)
    return out


def _matmul_rows(b, *, form, tm, tk, fn, rows, consts, row_outs, acc_outs, name, a=None, a_rows=None, a_fn=None,
                 carried=None):
    b3 = b.ndim == 3
    resident = 0
    if form == "nn":
        k, n = b.shape
        b_spec = pl.BlockSpec((tk, n), lambda i, kk: (kk, 0))
        dn = (((1,), (0,)), ((), ()))
    else:
        n = b.shape[1] if b3 else b.shape[0]
        k = b.shape[0] * b.shape[2] if b3 else b.shape[1]
        if b3 and tk == k:
            resident = b.shape[0]
            b_spec = pl.BlockSpec(b.shape, lambda i, kk: (0, 0, 0))
        else:
            b_spec = (pl.BlockSpec((None, n, tk), lambda i, kk: (kk, 0, 0)) if b3
                      else pl.BlockSpec((n, tk), lambda i, kk: (0, kk)))
        dn = (((1,), (1,)), ((), ()))
    nk = k // tk
    lhs_in = [(a, tk, 0)] if a is not None else list(a_rows)
    assert a is not None or nk == 1
    m = lhs_in[0][0].shape[0]
    n_lhs = len(lhs_in)
    in_specs = [pl.BlockSpec((tm, tk), lambda i, kk: (i, kk))] if a is not None else [
        pl.BlockSpec((tm, w), functools.partial(lambda i, kk, cb: (i, cb), cb=cb)) for _, w, cb in a_rows]
    in_specs.append(b_spec)
    in_specs += [pl.BlockSpec((tm, w), functools.partial(lambda i, kk, cb: (i, cb), cb=cb)) for _, w, cb in rows]
    in_specs += [pl.BlockSpec(c.shape, functools.partial(lambda i, kk, nd: (0,) * nd, nd=c.ndim)) for c in consts]
    out_specs = [pl.BlockSpec((tm, w), lambda i, kk: (i, 0)) for _, w in row_outs]
    out_specs += [pl.BlockSpec((r, w), lambda i, kk: (0, 0)) for r, w in acc_outs]
    out_shape = [_sds((m, w), dt) for dt, w in row_outs] + [_sds((r, w), F32) for r, w in acc_outs]
    n_rows, n_consts, n_ro, n_acc = len(rows), len(consts), len(row_outs), len(acc_outs)

    def body(*refs):
        pos = n_lhs + 1
        row_refs, const_refs = refs[pos:pos + n_rows], refs[pos + n_rows:pos + n_rows + n_consts]
        pos += n_rows + n_consts
        out_refs, acc_refs = refs[pos:pos + n_ro], refs[pos + n_ro:pos + n_ro + n_acc]
        i, kk = pl.program_id(0), pl.program_id(1)
        if resident:
            b_ref, ks, p = refs[n_lhs], b.shape[2], None
            for s in range(resident):
                ps = lax.dot_general(refs[0][:, s * ks:(s + 1) * ks], b_ref[s], dn, preferred_element_type=F32)
                p = ps if p is None else p + ps
        else:
            lhs = refs[0][...] if a is not None else a_fn(*[r[...] for r in refs[:n_lhs]]).astype(BF16)
            p = lax.dot_general(lhs, refs[n_lhs][...], dn, preferred_element_type=F32)

        def finish(acc):
            extra = [r[...] for r in row_refs] + [c[...] for c in const_refs]
            res = fn(acc, lhs, *extra) if a is None else fn(acc, *extra)
            for o_ref, val in zip(out_refs, res[:n_ro]):
                o_ref[...] = val.astype(o_ref.dtype)
            if n_acc:
                @pl.when(i == 0)
                def _():
                    for o_ref, val in zip(acc_refs, res[n_ro:]):
                        o_ref[...] = val

                @pl.when(i > 0)
                def _():
                    for o_ref, val in zip(acc_refs, res[n_ro:]):
                        o_ref[...] += val

        if nk == 1:
            finish(p)
        else:
            acc_ref = refs[pos + n_ro + n_acc]

            @pl.when(kk == 0)
            def _():
                acc_ref[...] = p

            @pl.when(kk > 0)
            def _():
                acc_ref[...] += p

            @pl.when(kk == nk - 1)
            def _():
                finish(acc_ref[...])

    res = _call(body, grid=(m // tm, nk), in_specs=in_specs, out_specs=out_specs, out_shape=out_shape,
                scratch_shapes=[pltpu.VMEM((tm, n), F32)] if nk > 1 else [], sem=("arbitrary", "arbitrary"),
                name=name, args=[r[0] for r in lhs_in] + [b] + [r[0] for r in rows] + list(consts), carried=carried)
    own = n_ro + n_acc
    return res[:own] if carried is None else (res[:own], res[own:])


def _colsum(v):
    return jnp.sum(v, axis=0, keepdims=True)


def _sigmoid(v):
    return 1.0 / (1.0 + jnp.exp(-v))


_GELU_C = math.sqrt(2.0 / math.pi)


def _gelu(v):
    return 0.5 * v * (1.0 + jnp.tanh(_GELU_C * (v + 0.044715 * (v * v * v))))


def _gelu_and_grad(v):
    th = jnp.tanh(_GELU_C * (v + 0.044715 * (v * v * v)))
    g = 0.5 * v * (1.0 + th)
    dg = 0.5 * (1.0 + th) + 0.5 * v * (1.0 - th * th) * (_GELU_C * (1.0 + 3.0 * 0.044715 * (v * v)))
    return g, dg


def _rms_stats(v):
    r = lax.rsqrt(jnp.mean(v * v, axis=-1, keepdims=True) + EPS)
    return v * r, r


def _rms_bwd(dn, vn, r):
    return r * (dn - vn * jnp.mean(dn * vn, axis=-1, keepdims=True))


def _pre_norm(x, g, sc, sh, name):
    def fn(xv, gv, scv, shv):
        xn, _ = _rms_stats(xv)
        return (xn * gv) * (1.0 + scv) + shv
    return _rowcall(fn, [(x, D, 0)], [g, sc, sh], [(x.shape[0], D, BF16, D)], [], name=name,
                    tm=min(2 * ROW_TILE, x.shape[0]))[0]


def _pre_norm_bwd(dh, x, dx_other, g, sc, name):
    def fn(dhv, xv, dov, gv, scv):
        xn, r = _rms_stats(xv)
        yn = xn * gv
        dyn = dhv * (1.0 + scv)
        dx = _rms_bwd(dyn * gv, xn, r)
        return dov + dx, _colsum(dhv), _colsum(dhv * yn), _colsum(dyn * xn)
    t = x.shape[0]
    return _rowcall(fn, [(dh, D, 0), (x, D, 0), (dx_other, D, 0)], [g, sc], [(t, D, F32, D)],
                    [(1, D, D)] * 3, name=name, tm=min(2 * ROW_TILE, t))


CONV_HALO = 32


def _layer_norm_parts(u):
    mu = jnp.mean(u, axis=-1, keepdims=True)
    d = u - mu
    r = lax.rsqrt(jnp.mean(d * d, axis=-1, keepdims=True) + EPS)
    return d * r, r


LANES = 128
SUBLANE_ROWS = 8
CONV_ROWS = 64


def _lanes(c):
    return slice(c * LANES, (c + 1) * LANES)


def _conv_branch(z, w_dw, b_dw, g_ln, b_ln, name, tm=WORK_TILE, carried=None):
    t = z.shape[0]
    per = tm // CONV_HALO
    n_chunks = 512 // LANES

    def body(ga_ref, gb_ref, gah_ref, gbh_ref, w_ref, b_ref, g_ref, bl_ref, u1_ref, u3_ref, scr):
        i = pl.program_id(0)
        u0h = jnp.where(i > 0, gah_ref[...] * _sigmoid(gbh_ref[...]), 0.0)
        u0 = ga_ref[...] * _sigmoid(gb_ref[...])
        for c in range(n_chunks):
            scr[c, 0:CONV_HALO, :] = u0h[:, _lanes(c)]
            scr[c, CONV_HALO:CONV_HALO + tm, :] = u0[:, _lanes(c)]
        for c in range(n_chunks):
            for r0 in range(0, tm, CONV_ROWS):
                acc = jnp.zeros((CONV_ROWS, LANES), F32) + b_ref[:, _lanes(c)]
                for j in range(CONV_K):
                    acc = acc + w_ref[j:j + 1, _lanes(c)] * scr[c, pl.ds(r0 + CONV_HALO - (CONV_K - 1) + j, CONV_ROWS), :]
                u1_ref[r0:r0 + CONV_ROWS, _lanes(c)] = acc
        xh, _ = _layer_norm_parts(u1_ref[...])
        u2 = xh * g_ref[...] + bl_ref[...]
        u3_ref[...] = (u2 * _sigmoid(u2)).astype(BF16)

    cur = lambda cb: pl.BlockSpec((tm, 512), lambda i: (i, cb))
    halo = lambda cb: pl.BlockSpec((CONV_HALO, 512), lambda i: (jnp.maximum(i * per - 1, 0), cb))
    whole = lambda a: pl.BlockSpec(a.shape, lambda i: (0, 0))
    res = _call(
        body, grid=(t // tm,),
        in_specs=[cur(3), cur(4), halo(3), halo(4), whole(w_dw), whole(b_dw), whole(g_ln), whole(b_ln)],
        out_specs=[pl.BlockSpec((tm, 512), lambda i: (i, 0))] * 2,
        out_shape=[_sds((t, 512), F32), _sds((t, 512), BF16)],
        scratch_shapes=[pltpu.VMEM((n_chunks, CONV_HALO + tm, LANES), F32)],
        sem=("arbitrary",), name=name, args=(z, z, z, z, w_dw, b_dw, g_ln, b_ln), carried=carried)
    return res[:2] if carried is None else (res[:2], res[2:])


def _conv_branch_bwd(du3, u1, z, w_dw, g_ln, b_ln, name, tm=WORK_TILE, carried=None):
    t = z.shape[0]
    per = tm // CONV_HALO
    last = t // tm - 1
    n_chunks = 512 // LANES

    def du1_of(du3v, u1v, g, b):
        xh, r = _layer_norm_parts(u1v)
        u2 = xh * g + b
        s = _sigmoid(u2)
        du2 = du3v * (s * (1.0 + u2 * (1.0 - s)))
        dxh = du2 * g
        du1 = r * (dxh - jnp.mean(dxh, axis=-1, keepdims=True) - xh * jnp.mean(dxh * xh, axis=-1, keepdims=True))
        return du1, du2, xh

    def body(d_ref, u_ref, dn_ref, un_ref, ga_ref, gb_ref, gah_ref, gbh_ref, w_ref, g_ref, bl_ref,
             dglu_ref, dw_ref, dbdw_ref, dg_ref, dbl_ref, dbin_ref, scr, scd):
        i = pl.program_id(0)
        g, b = g_ref[...], bl_ref[...]
        du1, du2, xh = du1_of(d_ref[...], u_ref[...], g, b)
        du1n, _, _ = du1_of(dn_ref[...], un_ref[...], g, b)
        du1n = jnp.where(i < last, du1n, 0.0)
        sgb = _sigmoid(gb_ref[...])
        ga = ga_ref[...]
        u0 = ga * sgb
        u0h = jnp.where(i > 0, gah_ref[...] * _sigmoid(gbh_ref[...]), 0.0)
        for c in range(n_chunks):
            scd[c, 0:tm, :] = du1[:, _lanes(c)]
            scd[c, tm:tm + CONV_HALO, :] = du1n[:, _lanes(c)]
            scr[c, 0:CONV_HALO, :] = u0h[:, _lanes(c)]
            scr[c, CONV_HALO:CONV_HALO + tm, :] = u0[:, _lanes(c)]

        @pl.when(i == 0)
        def _():
            for ref in (dw_ref, dbdw_ref, dg_ref, dbl_ref, dbin_ref):
                ref[...] = jnp.zeros_like(ref)

        dsg = ga * (sgb * (1.0 - sgb))
        for c in range(n_chunks):
            gate = slice(512 + c * LANES, 512 + (c + 1) * LANES)
            for r0 in range(0, tm, CONV_ROWS):
                rows = slice(r0, r0 + CONV_ROWS)
                du0 = jnp.zeros((CONV_ROWS, LANES), F32)
                for j in range(CONV_K):
                    du0 = du0 + w_ref[j:j + 1, _lanes(c)] * scd[c, pl.ds(r0 + CONV_K - 1 - j, CONV_ROWS), :]
                dga = du0 * sgb[rows, _lanes(c)]
                dgb = du0 * dsg[rows, _lanes(c)]
                dglu_ref[rows, _lanes(c)] = dga.astype(BF16)
                dglu_ref[rows, gate] = dgb.astype(BF16)
                dbin_ref[:, _lanes(c)] += _colsum(dga)
                dbin_ref[:, gate] += _colsum(dgb)
            for j in range(CONV_K):
                dwj = jnp.zeros((SUBLANE_ROWS, LANES), F32)
                for r0 in range(0, tm, CONV_ROWS):
                    prod = (scd[c, pl.ds(r0, CONV_ROWS), :]
                            * scr[c, pl.ds(r0 + CONV_HALO - (CONV_K - 1) + j, CONV_ROWS), :])
                    dwj = dwj + jnp.sum(prod.reshape(CONV_ROWS // SUBLANE_ROWS, SUBLANE_ROWS, LANES), axis=0)
                dw_ref[j:j + 1, _lanes(c)] += _colsum(dwj)
        dbdw_ref[...] += _colsum(du1)
        dg_ref[...] += _colsum(du2 * xh)
        dbl_ref[...] += _colsum(du2)

    cur = lambda cb: pl.BlockSpec((tm, 512), lambda i: (i, cb))
    prev = lambda cb: pl.BlockSpec((CONV_HALO, 512), lambda i: (jnp.maximum(i * per - 1, 0), cb))
    nxt = pl.BlockSpec((CONV_HALO, 512), lambda i: (jnp.minimum((i + 1) * per, t // CONV_HALO - 1), 0))
    whole = lambda a: pl.BlockSpec(a.shape, lambda i: (0, 0))
    acc = lambda r, w: pl.BlockSpec((r, w), lambda i: (0, 0))
    res = _call(
        body, grid=(t // tm,),
        in_specs=[cur(0), cur(0), nxt, nxt, cur(3), cur(4), prev(3), prev(4), whole(w_dw), whole(g_ln), whole(b_ln)],
        out_specs=[pl.BlockSpec((tm, 1024), lambda i: (i, 0)), acc(CONV_K, 512), acc(1, 512), acc(1, 512),
                   acc(1, 512), acc(1, 1024)],
        out_shape=[_sds((t, 1024), BF16), _sds((CONV_K, 512), F32), _sds((1, 512), F32), _sds((1, 512), F32),
                   _sds((1, 512), F32), _sds((1, 1024), F32)],
        scratch_shapes=[pltpu.VMEM((n_chunks, CONV_HALO + tm, LANES), F32),
                        pltpu.VMEM((n_chunks, tm + CONV_HALO, LANES), F32)],
        sem=("arbitrary",), name=name, args=(du3, u1, du3, u1, z, z, z, z, w_dw, g_ln, b_ln), carried=carried)
    return res[:6] if carried is None else (res[:6], res[6:])


FF_BLOCK = D_FF // 2
FF_HALO = 8
FF_CHUNKS = FF_BLOCK // LANES


FF_ROWS = 64


def _ext_rows(ext):
    return next(d for d in (104, 88, 72, 56, 40, 24, 8) if ext % d == 0)


def _ffn_conv(w_ref, b_ref, scr, k, rows, r0=0):
    acc = b_ref[:, _lanes(k)] + w_ref[0:1, _lanes(k)] * scr[k, pl.ds(r0 + FF_HALO - 2, rows), :]
    acc = acc + w_ref[1:2, _lanes(k)] * scr[k, pl.ds(r0 + FF_HALO - 1, rows), :]
    return acc + w_ref[2:3, _lanes(k)] * scr[k, pl.ds(r0 + FF_HALO, rows), :]


def _ffn_act(up, w3, b3, name, tm=WORK_TILE, carried=None):
    t = up.shape[0]
    per = tm // FF_HALO
    wide = 2 * FF_BLOCK

    def body(u_ref, uh_ref, w_ref, b_ref, o_ref, scr):
        i = pl.program_id(1)
        for k in range(2 * FF_CHUNKS):
            scr[k, 0:FF_HALO, :] = jnp.where(i > 0, uh_ref[:, _lanes(k)], 0.0)
            scr[k, FF_HALO:FF_HALO + tm, :] = u_ref[:, _lanes(k)]
        for cc in range(FF_CHUNKS):
            for r0 in range(0, tm, FF_ROWS):
                val = _ffn_conv(w_ref, b_ref, scr, cc, FF_ROWS, r0)
                gate = _ffn_conv(w_ref, b_ref, scr, FF_CHUNKS + cc, FF_ROWS, r0)
                o_ref[r0:r0 + FF_ROWS, _lanes(cc)] = (_gelu(gate) * val).astype(BF16)

    res = _call(
        body, grid=(2, t // tm),
        in_specs=[pl.BlockSpec((tm, wide), lambda c, i: (i, c)),
                  pl.BlockSpec((FF_HALO, wide), lambda c, i: (jnp.maximum(i * per - 1, 0), c)),
                  pl.BlockSpec((FFN_K, wide), lambda c, i: (0, c)),
                  pl.BlockSpec((1, wide), lambda c, i: (0, c))],
        out_specs=[pl.BlockSpec((tm, FF_BLOCK), lambda c, i: (i, c))],
        out_shape=[_sds((t, D_FF), BF16)],
        scratch_shapes=[pltpu.VMEM((2 * FF_CHUNKS, FF_HALO + tm, LANES), F32)],
        sem=("arbitrary", "arbitrary"), name=name, args=(up, up, w3, b3), carried=carried)
    return res[0] if carried is None else (res[0], res[1:])


def _ffn_act_bwd(dact, up, w3, b3, name, tm=WORK_TILE):
    t = up.shape[0]
    per = tm // FF_HALO
    wide = 2 * FF_BLOCK
    last = t // tm - 1
    ext = tm + FF_HALO
    FF_EXT_ROWS = _ext_rows(ext)

    def body(u_ref, up_ref, un_ref, d_ref, dn_ref, w_ref, b_ref, o_ref, dw_ref, db_ref, scr, scd):
        i = pl.program_id(1)
        for k in range(2 * FF_CHUNKS):
            scr[k, 0:FF_HALO, :] = jnp.where(i > 0, up_ref[:, _lanes(k)], 0.0)
            scr[k, FF_HALO:FF_HALO + tm, :] = u_ref[:, _lanes(k)]
            scr[k, FF_HALO + tm:FF_HALO + ext, :] = un_ref[:, _lanes(k)]
        dn = jnp.where(i < last, dn_ref[...], 0.0)

        @pl.when(i == 0)
        def _():
            dw_ref[...] = jnp.zeros_like(dw_ref)
            db_ref[...] = jnp.zeros_like(db_ref)

        for cc in range(FF_CHUNKS):
            gc = FF_CHUNKS + cc
            for r0 in range(0, ext, FF_EXT_ROWS):
                rows = pl.ds(r0, FF_EXT_ROWS)
                val = _ffn_conv(w_ref, b_ref, scr, cc, FF_EXT_ROWS, r0)
                gel, dgel = _gelu_and_grad(_ffn_conv(w_ref, b_ref, scr, gc, FF_EXT_ROWS, r0))
                da = d_ref[r0:r0 + FF_EXT_ROWS, _lanes(cc)] if r0 + FF_EXT_ROWS <= tm else jnp.concatenate(
                    [d_ref[r0:tm, _lanes(cc)], dn[:, _lanes(cc)]], axis=0)
                scd[cc, rows, :] = da * gel
                scd[gc, rows, :] = da * val * dgel
            for k in (cc, gc):
                dwk = [jnp.zeros((SUBLANE_ROWS, LANES), F32) for _ in range(FFN_K)]
                dbk = jnp.zeros((SUBLANE_ROWS, LANES), F32)
                for r0 in range(0, tm, FF_ROWS):
                    shifted = [scd[k, pl.ds(r0 + FFN_K - 1 - j, FF_ROWS), :] for j in range(FFN_K)]
                    ucur = scr[k, pl.ds(r0 + FF_HALO, FF_ROWS), :]
                    o_ref[r0:r0 + FF_ROWS, _lanes(k)] = (
                        w_ref[0:1, _lanes(k)] * shifted[0] + w_ref[1:2, _lanes(k)] * shifted[1]
                        + w_ref[2:3, _lanes(k)] * shifted[2]).astype(BF16)
                    fold = lambda v: jnp.sum(v.reshape(FF_ROWS // SUBLANE_ROWS, SUBLANE_ROWS, LANES), axis=0)
                    for j in range(FFN_K):
                        dwk[j] = dwk[j] + fold(shifted[j] * ucur)
                    dbk = dbk + fold(shifted[FFN_K - 1])
                for j in range(FFN_K):
                    dw_ref[j:j + 1, _lanes(k)] += _colsum(dwk[j])
                db_ref[:, _lanes(k)] += _colsum(dbk)

    nblk = t // FF_HALO
    return pl.pallas_call(
        body, grid=(2, t // tm),
        in_specs=[pl.BlockSpec((tm, wide), lambda c, i: (i, c)),
                  pl.BlockSpec((FF_HALO, wide), lambda c, i: (jnp.maximum(i * per - 1, 0), c)),
                  pl.BlockSpec((FF_HALO, wide), lambda c, i: (jnp.minimum((i + 1) * per, nblk - 1), c)),
                  pl.BlockSpec((tm, FF_BLOCK), lambda c, i: (i, c)),
                  pl.BlockSpec((FF_HALO, FF_BLOCK), lambda c, i: (jnp.minimum((i + 1) * per, nblk - 1), c)),
                  pl.BlockSpec((FFN_K, wide), lambda c, i: (0, c)),
                  pl.BlockSpec((1, wide), lambda c, i: (0, c))],
        out_specs=[pl.BlockSpec((tm, wide), lambda c, i: (i, c)),
                   pl.BlockSpec((FFN_K, wide), lambda c, i: (0, c)),
                   pl.BlockSpec((1, wide), lambda c, i: (0, c))],
        out_shape=[_sds((t, 2 * D_FF), BF16), _sds((FFN_K, 2 * D_FF), F32), _sds((1, 2 * D_FF), F32)],
        scratch_shapes=[pltpu.VMEM((2 * FF_CHUNKS, FF_HALO + ext, LANES), F32),
                        pltpu.VMEM((2 * FF_CHUNKS, ext, LANES), F32)],
        compiler_params=_params(("arbitrary", "arbitrary")), name=name,
    )(up, up, up, dact, dact, w3, b3)


def _toeplitz_map():
    f = np.zeros((TOEP, REL_PAD), np.float32)
    for m in range(TOEP - 1):
        rel = (WINDOW - 1) - m
        f[m, int(np.clip(rel, -MAX_REL, MAX_REL)) + MAX_REL] = 1.0
    return f


def _split3(v):
    hi = v.astype(BF16)
    r1 = v - hi.astype(F32)
    mid = r1.astype(BF16)
    lo = (r1 - mid.astype(F32)).astype(BF16)
    return hi, mid, lo


def _exact_select(v, sel):
    out = None
    for part in _split3(v):
        p = jnp.dot(part, sel, preferred_element_type=F32)
        out = p if out is None else out + p
    return out


def _select_call(v, sel, name):
    def body(v_ref, s_ref, o_ref):
        o_ref[...] = _exact_select(v_ref[...], s_ref[...])
    return pl.pallas_call(body, out_shape=_sds((v.shape[0], sel.shape[1]), F32), name=name)(v, sel)


def _band_bias(gen_row):
    b0 = jnp.broadcast_to(gen_row, (Q_TILE, TOEP))
    bias = pltpu.roll(b0, TOEP - (Q_TILE - 1), 1, stride=1, stride_axis=0)[:, :WINDOW]
    qq = lax.broadcasted_iota(jnp.int32, (Q_TILE, WINDOW), 0) // CHUNK
    kc = lax.broadcasted_iota(jnp.int32, (Q_TILE, WINDOW), 1) // CHUNK
    return jnp.where((kc >= qq) & (kc <= qq + LEFT_CHUNKS), bias, NEG_INF)


PAD_ROWS = WINDOW - Q_TILE
NT_DIMS = (((1,), (1,)), ((), ()))
TN_DIMS = (((0,), (0,)), ((), ()))


def _head_mask(hh):
    lane = lax.broadcasted_iota(jnp.int32, (1, 128), 1)
    return (lane < 64) if hh == 0 else (lane >= 64)


SOFTMAX_ROWS = 16


def _probs_block(s_scr, bias, hh, rows, q_start):
    s = s_scr[rows, :] + bias[hh, rows, :]
    col = lax.broadcasted_iota(jnp.int32, (SOFTMAX_ROWS, WINDOW), 1)
    s = jnp.where(col >= PAD_ROWS - q_start, s, NEG_INF)
    p = jnp.exp(s - jnp.max(s, axis=-1, keepdims=True))
    return p / jnp.sum(p, axis=-1, keepdims=True)


def _attention(z, gen, name, carried=None):
    t = z.shape[0]
    n_i = t // STEP_ROWS

    def body(q_ref, k_ref, v_ref, g_ref, o_ref, kpad, vpad, bias, s_scr, p_scr):
        hp, i = pl.program_id(0), pl.program_id(1)

        @pl.when(i == 0)
        def _():
            kpad[0:PAD_ROWS, :] = jnp.zeros((PAD_ROWS, 128), BF16)
            vpad[0:PAD_ROWS, :] = jnp.zeros((PAD_ROWS, 128), BF16)
            kpad[PAD_ROWS:PAD_ROWS + t, :] = k_ref[...].astype(BF16)
            vpad[PAD_ROWS:PAD_ROWS + t, :] = v_ref[...].astype(BF16)
            for hh in range(2):
                bias[hh] = _band_bias(g_ref[pl.ds(2 * hp + hh, 1), :])

        for q0 in range(0, STEP_ROWS, Q_TILE):
            q_start = i * STEP_ROWS + q0
            win = pl.ds(pl.multiple_of(q_start, Q_TILE), WINDOW)
            out = None
            for hh in range(2):
                mask = _head_mask(hh)
                qm = jnp.where(mask, q_ref[q0:q0 + Q_TILE, :] * (CHUNK ** -0.5), 0.0).astype(BF16)
                slot = 2 * (q0 // Q_TILE) + hh
                s_scr[slot] = lax.dot_general(qm, kpad[win, :], NT_DIMS, preferred_element_type=F32)
                for r0 in range(0, Q_TILE, SOFTMAX_ROWS):
                    rows = slice(r0, r0 + SOFTMAX_ROWS)
                    p_scr[slot, rows, :] = _probs_block(s_scr.at[slot], bias, hh, rows, q_start).astype(BF16)
                o = jnp.dot(p_scr[slot], vpad[win, :], preferred_element_type=F32)
                out = jnp.where(mask, o, 0.0) if out is None else jnp.where(mask, o, out)
            o_ref[q0:q0 + Q_TILE, :] = out.astype(BF16)

    res = _call(
        body, grid=(4, n_i),
        in_specs=[pl.BlockSpec((STEP_ROWS, 128), lambda h, i: (i, h)),
                  pl.BlockSpec((t, 128), lambda h, i: (0, 4 + h)),
                  pl.BlockSpec((t, 128), lambda h, i: (0, 8 + h)),
                  pl.BlockSpec((N_HEADS, TOEP), lambda h, i: (0, 0))],
        out_specs=[pl.BlockSpec((STEP_ROWS, 128), lambda h, i: (i, h))],
        out_shape=[_sds((t, 512), BF16)],
        scratch_shapes=[pltpu.VMEM((PAD_ROWS + t, 128), BF16), pltpu.VMEM((PAD_ROWS + t, 128), BF16),
                        pltpu.VMEM((2, Q_TILE, WINDOW), F32), pltpu.VMEM((SCORE_SLOTS, Q_TILE, WINDOW), F32),
                        pltpu.VMEM((SCORE_SLOTS, Q_TILE, WINDOW), BF16)],
        sem=("arbitrary", "arbitrary"), name=name, args=(z, z, z, gen), carried=carried)
    return res[0] if carried is None else (res[0], res[1:])


def _attention_bwd(z, datt, gen, name, carried=None):
    t = z.shape[0]
    n_i = t // STEP_ROWS

    def body(q_ref, k_ref, v_ref, d_ref, g_ref, dq_ref, dk_ref, dv_ref, sq_ref, sk_ref, sv_ref, dg_ref,
             kpad, vpad, dkacc, dvacc, bias, dsacc, s_scr, dp_scr, p_scr, ds_scr):
        hp, i = pl.program_id(0), pl.program_id(1)

        @pl.when(i == 0)
        def _():
            kpad[0:PAD_ROWS, :] = jnp.zeros((PAD_ROWS, 128), BF16)
            vpad[0:PAD_ROWS, :] = jnp.zeros((PAD_ROWS, 128), BF16)
            kpad[PAD_ROWS:PAD_ROWS + t, :] = k_ref[...].astype(BF16)
            vpad[PAD_ROWS:PAD_ROWS + t, :] = v_ref[...].astype(BF16)
            dkacc[...] = jnp.zeros_like(dkacc)
            dvacc[...] = jnp.zeros_like(dvacc)
            dsacc[...] = jnp.zeros_like(dsacc)
            for hh in range(2):
                bias[hh] = _band_bias(g_ref[pl.ds(2 * hp + hh, 1), :])

        dq_sum = None
        for q0 in range(0, STEP_ROWS, Q_TILE):
            q_start = i * STEP_ROWS + q0
            win = pl.ds(pl.multiple_of(q_start, Q_TILE), WINDOW)
            dq = None
            for hh in range(2):
                mask = _head_mask(hh)
                qm = jnp.where(mask, q_ref[q0:q0 + Q_TILE, :] * (CHUNK ** -0.5), 0.0).astype(BF16)
                dom = jnp.where(mask, d_ref[q0:q0 + Q_TILE, :], 0.0).astype(BF16)
                slot = 2 * (q0 // Q_TILE) + hh
                s_scr[slot] = lax.dot_general(qm, kpad[win, :], NT_DIMS, preferred_element_type=F32)
                dp_scr[slot] = lax.dot_general(dom, vpad[win, :], NT_DIMS, preferred_element_type=F32)
                for r0 in range(0, Q_TILE, SOFTMAX_ROWS):
                    rows = slice(r0, r0 + SOFTMAX_ROWS)
                    p = _probs_block(s_scr.at[slot], bias, hh, rows, q_start)
                    dp = dp_scr[slot, rows, :]
                    ds = p * (dp - jnp.sum(p * dp, axis=-1, keepdims=True))
                    dsacc[hh, rows, :] += ds
                    ds_scr[slot, rows, :] = ds.astype(BF16)
                    p_scr[slot, rows, :] = p.astype(BF16)
                ds16 = ds_scr[slot]
                dqh = jnp.dot(ds16, kpad[win, :], preferred_element_type=F32) * (CHUNK ** -0.5)
                dq = jnp.where(mask, dqh, 0.0) if dq is None else jnp.where(mask, dqh, dq)
                dkacc[win, :] += lax.dot_general(ds16, qm, TN_DIMS, preferred_element_type=F32)
                dvacc[win, :] += lax.dot_general(p_scr[slot], dom, TN_DIMS, preferred_element_type=F32)
            dq_ref[q0:q0 + Q_TILE, :] = dq.astype(BF16)
            dq_sum = _colsum(dq) if dq_sum is None else dq_sum + _colsum(dq)

        @pl.when(i == 0)
        def _():
            sq_ref[...] = dq_sum

        @pl.when(i > 0)
        def _():
            sq_ref[...] += dq_sum

        @pl.when(i == n_i - 1)
        def _():
            dk = dkacc[PAD_ROWS:PAD_ROWS + t, :]
            dv = dvacc[PAD_ROWS:PAD_ROWS + t, :]
            dk_ref[...] = dk.astype(BF16)
            dv_ref[...] = dv.astype(BF16)
            sk_ref[...] = _colsum(dk)
            sv_ref[...] = _colsum(dv)
            rr = lax.broadcasted_iota(jnp.int32, (Q_TILE, Q_TILE), 0)
            cc = lax.broadcasted_iota(jnp.int32, (Q_TILE, Q_TILE), 1)
            rev = jnp.where(rr + cc == Q_TILE - 1, 1.0, 0.0).astype(BF16)
            for hh in range(2):
                acc = None
                for part in _split3(dsacc[hh]):
                    pr = jnp.dot(rev, part, preferred_element_type=F32)
                    acc = pr if acc is None else acc + pr
                wide = jnp.concatenate([acc, jnp.zeros((Q_TILE, TOEP - WINDOW), F32)], axis=1)
                dg_ref[pl.ds(2 * hp + hh, 1), :] = _colsum(pltpu.roll(wide, 0, 1, stride=1, stride_axis=0))

    col = lambda off: pl.BlockSpec((t, 128), lambda h, i: (0, off + h))
    tile = lambda: pl.BlockSpec((STEP_ROWS, 128), lambda h, i: (i, h))
    sums = lambda: pl.BlockSpec((1, 128), lambda h, i: (0, h))
    res = _call(
        body, grid=(4, n_i),
        in_specs=[tile(), col(4), col(8), tile(), pl.BlockSpec((N_HEADS, TOEP), lambda h, i: (0, 0))],
        out_specs=[tile(), col(0), col(0), sums(), sums(), sums(), pl.BlockSpec((N_HEADS, TOEP), lambda h, i: (0, 0))],
        out_shape=[_sds((t, 512), BF16)] * 3 + [_sds((1, 512), F32)] * 3 + [_sds((N_HEADS, TOEP), F32)],
        scratch_shapes=[pltpu.VMEM((PAD_ROWS + t, 128), BF16), pltpu.VMEM((PAD_ROWS + t, 128), BF16),
                        pltpu.VMEM((PAD_ROWS + t, 128), F32), pltpu.VMEM((PAD_ROWS + t, 128), F32),
                        pltpu.VMEM((2, Q_TILE, WINDOW), F32), pltpu.VMEM((2, Q_TILE, WINDOW), F32),
                        pltpu.VMEM((SCORE_SLOTS, Q_TILE, WINDOW), F32), pltpu.VMEM((SCORE_SLOTS, Q_TILE, WINDOW), F32),
                        pltpu.VMEM((SCORE_SLOTS, Q_TILE, WINDOW), BF16), pltpu.VMEM((SCORE_SLOTS, Q_TILE, WINDOW), BF16)],
        sem=("arbitrary", "arbitrary"), name=name, args=(z, z, z, datt, gen), carried=carried)
    return res[:7] if carried is None else (res[:7], res[7:])


def _adamw_math(w, g, m, v):
    m = ADAM_B1 * m + (1.0 - ADAM_B1) * g
    v = ADAM_B2 * v + (1.0 - ADAM_B2) * (g * g)
    m_hat = m / (1.0 - ADAM_B1 ** ADAM_STEP)
    v_hat = v / (1.0 - ADAM_B2 ** ADAM_STEP)
    delta = -ADAM_LR * (m_hat / (jnp.sqrt(v_hat) + ADAM_EPS) + ADAM_WD * w)
    return delta, m, v


def _adamw_many(items, name):
    n = len(items)

    def body(*refs):
        ins, outs = refs[:4 * n], refs[4 * n:]
        for k in range(n):
            w, g, m, v = (r[...] for r in ins[4 * k:4 * k + 4])
            outs[3 * k][...], outs[3 * k + 1][...], outs[3 * k + 2][...] = _adamw_math(w, g, m, v)

    flat = [a for item in items for a in item]
    res = pl.pallas_call(body, out_shape=[_sds(item[0].shape, F32) for item in items for _ in range(3)],
                         name=name)(*flat)
    return [tuple(res[3 * k:3 * k + 3]) for k in range(n)]


def _adamw(w, g, m, v, name, after):
    r, c = w.shape
    tm = next(cand for cand in (512, 352, 256, 128, 64, 32, 16, 8) if r % cand == 0)
    return _rowcall(lambda wv, gv, mv, vv, _: (gv,) + _adamw_math(wv, gv, mv, vv),
                    [(w, c, 0), (g, c, 0), (m, c, 0), (v, c, 0)], [after], [(r, c, F32, c)] * 4, [], name=name, tm=tm)


def _ada_fwd(c_all, w_shard, b_shard, name):
    n = w_shard.shape[1]
    tn = 512

    def body(c_ref, w_ref, b_ref, o_ref, a_ref):
        cv = c_ref[...]
        act = cv * _sigmoid(cv)
        a_ref[...] = act
        o_ref[...] = jnp.dot(act.astype(BF16), w_ref[...].astype(BF16), preferred_element_type=F32) + b_ref[...]

    return pl.pallas_call(
        body, grid=(n // tn,),
        in_specs=[pl.BlockSpec((8, D), lambda j: (0, 0)), pl.BlockSpec((D, tn), lambda j: (0, j)),
                  pl.BlockSpec((1, tn), lambda j: (0, j))],
        out_specs=[pl.BlockSpec((8, tn), lambda j: (0, j)), pl.BlockSpec((8, D), lambda j: (0, 0))],
        out_shape=[_sds((8, n), F32), _sds((8, D), F32)],
        compiler_params=_params(("arbitrary",)), name=name,
    )(c_all, w_shard, b_shard)


def _ada_bwd_adamw(act_t, dmod_shard, w, m, v, name):
    r, c = w.shape
    tm = 2 * ROW_TILE

    def body(a_ref, d_ref, w_ref, m_ref, v_ref, g_ref, dl_ref, nm_ref, nv_ref):
        g = jnp.dot(a_ref[...], d_ref[...], precision=lax.Precision.HIGHEST, preferred_element_type=F32)
        g_ref[...] = g
        dl_ref[...], nm_ref[...], nv_ref[...] = _adamw_math(w_ref[...], g, m_ref[...], v_ref[...])

    blk = pl.BlockSpec((tm, c), lambda i: (i, 0))
    return pl.pallas_call(
        body, grid=(r // tm,),
        in_specs=[pl.BlockSpec((tm, 8), lambda i: (i, 0)), pl.BlockSpec((8, c), lambda i: (0, 0)), blk, blk, blk],
        out_specs=[blk] * 4, out_shape=[_sds((r, c), F32)] * 4,
        compiler_params=_params(("arbitrary",)), name=name,
    )(act_t, dmod_shard, w, m, v)


def _place():
    return lax.axis_index("x"), lax.axis_index("y"), lax.axis_index("c")


def _flip(v, bit):
    return 1 - v if bit else v


VMEM_SPEC = pl.BlockSpec(memory_space=pltpu.VMEM)


def _allgather8(v, name):
    r, c = v.shape

    def body(v_ref, g_ref, tot_ref, send_sems, recv_sems, local_sem):
        x, y, cc = _place()
        sibling = (x, y, 1 - cc)
        chips = [(_flip(x, k & 2), _flip(y, k & 1)) for k in (1, 2, 3)]

        def block(px, py, pc):
            return g_ref.at[4 * px + 2 * py + pc]

        def copy(k, place, to, src=None):
            slot = block(*place)
            return pltpu.make_async_remote_copy(src_ref=slot if src is None else src, dst_ref=slot,
                                                send_sem=send_sems.at[k], recv_sem=recv_sems.at[k],
                                                device_id=to, device_id_type=MESH)

        mine = pltpu.make_async_copy(v_ref, block(x, y, cc), local_sem)
        mine.start()
        first = [copy(0, (x, y, cc), sibling, src=v_ref)]
        first += [copy(1 + j, (x, y, cc), (px, py, cc), src=v_ref) for j, (px, py) in enumerate(chips)]
        for cp in first:
            cp.start()
        passed = [copy(4 + j, (px, py, cc), sibling) for j, (px, py) in enumerate(chips)]
        for j, (px, py) in enumerate(chips):
            copy(1 + j, (px, py, cc), (x, y, cc)).wait_recv()
            passed[j].start()
        copy(0, sibling, (x, y, cc)).wait_recv()
        for j, (px, py) in enumerate(chips):
            copy(4 + j, (px, py, 1 - cc), (x, y, cc)).wait_recv()
        for cp in first + passed:
            cp.wait_send()
        mine.wait()
        tot = g_ref[0]
        for d in range(1, 8):
            tot = tot + g_ref[d]
        tot_ref[...] = tot

    return pl.pallas_call(
        body, in_specs=[VMEM_SPEC], out_specs=[VMEM_SPEC, VMEM_SPEC],
        out_shape=[_sds((8, r, c), F32), _sds((r, c), F32)],
        scratch_shapes=[pltpu.SemaphoreType.DMA((7,)), pltpu.SemaphoreType.DMA((7,)), pltpu.SemaphoreType.DMA],
        compiler_params=pltpu.CompilerParams(vmem_limit_bytes=VMEM_LIMIT), name=name,
    )(v)


def _slot(px, py, swapped):
    return 2 * py + px if swapped else 2 * px + py


def _gather_shards(arrs, swapped, name):
    n = len(arrs)

    def body(*refs):
        ins, outs = refs[:n], refs[n:2 * n]
        send1, recv1, send2, recv2, local_sems = refs[2 * n:]
        x, y, c = _place()
        sibling = (x, y, 1 - c)
        chips = [(_flip(x, k & 2), _flip(y, k & 1)) for k in (1, 2, 3)]
        local_copies, sends = [], []
        for a in range(n):
            h = outs[a].shape[1] // 2
            mine = pl.ds(pl.multiple_of(c * h, 8), h)
            own = _slot(x, y, swapped[a])
            lc = pltpu.make_async_copy(ins[a], outs[a].at[own], local_sems.at[a])
            lc.start()
            local_copies.append(lc)
            for j, (px, py) in enumerate(chips):
                cp = pltpu.make_async_remote_copy(
                    src_ref=ins[a].at[mine], dst_ref=outs[a].at[own, mine], send_sem=send1.at[3 * a + j],
                    recv_sem=recv1.at[3 * a + j], device_id=(px, py, c), device_id_type=MESH)
                cp.start()
                sends.append(cp)
        for a in range(n):
            h = outs[a].shape[1] // 2
            mine = pl.ds(pl.multiple_of(c * h, 8), h)
            for j, (px, py) in enumerate(chips):
                piece = outs[a].at[_slot(px, py, swapped[a]), mine]
                pltpu.make_async_remote_copy(
                    src_ref=piece, dst_ref=piece, send_sem=send1.at[3 * a + j], recv_sem=recv1.at[3 * a + j],
                    device_id=(px, py, c), device_id_type=MESH).wait_recv()
                fwd = pltpu.make_async_remote_copy(
                    src_ref=piece, dst_ref=piece, send_sem=send2.at[3 * a + j], recv_sem=recv2.at[3 * a + j],
                    device_id=sibling, device_id_type=MESH)
                fwd.start()
                sends.append(fwd)
        for a in range(n):
            h = outs[a].shape[1] // 2
            other = pl.ds(pl.multiple_of((1 - c) * h, 8), h)
            for j, (px, py) in enumerate(chips):
                piece = outs[a].at[_slot(px, py, swapped[a]), other]
                pltpu.make_async_remote_copy(
                    src_ref=piece, dst_ref=piece, send_sem=send2.at[3 * a + j], recv_sem=recv2.at[3 * a + j],
                    device_id=sibling, device_id_type=MESH).wait_recv()
        for cp in sends:
            cp.wait_send()
        for lc in local_copies:
            lc.wait()

    dma = lambda k: pltpu.SemaphoreType.DMA((k,))
    return pl.pallas_call(
        body, in_specs=[ANY] * n, out_specs=[ANY] * n,
        out_shape=[_sds((4,) + a.shape, a.dtype) for a in arrs],
        scratch_shapes=[dma(3 * n), dma(3 * n), dma(3 * n), dma(3 * n), dma(n)], name=name,
    )(*arrs)


def _carry_pair_exchange(grads):
    n = len(grads)

    def copies(ins, outs, send_sems, recv_sems):
        x, y, c = _place()
        cps = []
        for a in range(n):
            h = ins[a].shape[1] // 2
            theirs = pl.ds(pl.multiple_of((1 - c) * h, 8), h)
            cps.append(pltpu.make_async_remote_copy(
                src_ref=ins[a].at[:, theirs, :], dst_ref=outs[a], send_sem=send_sems.at[a], recv_sem=recv_sems.at[a],
                device_id=(x, y, 1 - c), device_id_type=MESH))
        return cps

    def start(*refs):
        for cp in copies(*refs):
            cp.start()

    def finish(*refs):
        for cp in copies(*refs):
            cp.wait()

    return _Carried(grads, [_sds((4, g.shape[1] // 2, g.shape[2]), F32) for g in grads], {}, n, start, finish)


def _pair_sum(grad, recv, core, name):
    _, r, c = grad.shape
    h = r // 2

    def body(core_ref, g_ref, r_ref, o_ref):
        o_ref[...] = (g_ref[...] + r_ref[...]).astype(BF16)

    return pl.pallas_call(
        body,
        grid_spec=pltpu.PrefetchScalarGridSpec(
            num_scalar_prefetch=1, grid=(4,),
            in_specs=[pl.BlockSpec((None, h, c), lambda s, core_ref: (s, core_ref[0], 0)),
                      pl.BlockSpec((None, h, c), lambda s, core_ref: (s, 0, 0))],
            out_specs=pl.BlockSpec((None, h, c), lambda s, core_ref: (s, 0, 0))),
        out_shape=_sds((4, h, c), BF16), compiler_params=_params(("arbitrary",)), name=name,
    )(core, grad, recv)


def _carry_chip_exchange(parts, swapped):
    n = len(parts)

    def copies(ins, outs, send_sems, recv_sems):
        x, y, c = _place()
        chips = [(_flip(x, k & 2), _flip(y, k & 1)) for k in (1, 2, 3)]
        cps = []
        for a in range(n):
            for j, (px, py) in enumerate(chips):
                cps.append(pltpu.make_async_remote_copy(
                    src_ref=ins[a].at[_slot(px, py, swapped[a])], dst_ref=outs[a].at[j],
                    send_sem=send_sems.at[3 * a + j], recv_sem=recv_sems.at[3 * a + j],
                    device_id=(px, py, c), device_id_type=MESH))
        return cps

    def start(*refs):
        for cp in copies(*refs):
            cp.start()

    def finish(*refs):
        for cp in copies(*refs):
            cp.wait()

    return _Carried(parts, [_sds((3,) + p.shape[1:], BF16) for p in parts], {}, 3 * n, start, finish)


def _chip_sum(part, recv, slot_core, name):
    _, h, c = part.shape

    def body(sc_ref, p_ref, r_ref, o_ref):
        acc = p_ref[...].astype(F32)
        for j in range(3):
            acc = acc + r_ref[j].astype(F32)
        o_ref[...] = acc

    return pl.pallas_call(
        body,
        grid_spec=pltpu.PrefetchScalarGridSpec(
            num_scalar_prefetch=1, grid=(1,),
            in_specs=[pl.BlockSpec((None, h, c), lambda q, sc_ref: (sc_ref[0], 0, 0)),
                      pl.BlockSpec((3, h, c), lambda q, sc_ref: (0, 0, 0))],
            out_specs=pl.BlockSpec((h, c), lambda q, sc_ref: (sc_ref[1], 0))),
        out_shape=_sds((2 * h, c), F32), compiler_params=_params(("arbitrary",)), name=name,
    )(slot_core, part, recv)


def _carry_pair_share(shards):
    n = len(shards)

    def copies(outs, send_sems, recv_sems, mine):
        x, y, c = _place()
        cps = []
        for a in range(n):
            h = outs[a].shape[0] // 2
            half = outs[a].at[pl.ds(pl.multiple_of((c if mine else 1 - c) * h, 8), h)]
            cps.append(pltpu.make_async_remote_copy(
                src_ref=half, dst_ref=half, send_sem=send_sems.at[a], recv_sem=recv_sems.at[a],
                device_id=(x, y, 1 - c), device_id_type=MESH))
        return cps

    def start(ins, outs, send_sems, recv_sems):
        for cp in copies(outs, send_sems, recv_sems, True):
            cp.start()

    def finish(ins, outs, send_sems, recv_sems):
        for cp in copies(outs, send_sems, recv_sems, False):
            cp.wait_recv()
        for cp in copies(outs, send_sems, recv_sems, True):
            cp.wait_send()

    return _Carried(shards, [_sds(s.shape, F32) for s in shards], {a: a for a in range(n)}, n, start, finish)


def _carry_gather_ici(bufs, swapped):
    n = len(bufs)

    def copies(outs, send_sems, recv_sems, sending):
        x, y, c = _place()
        cps = []
        for a in range(n):
            h = outs[a].shape[1] // 2
            mine = pl.ds(pl.multiple_of(c * h, 8), h)
            for j, k in enumerate((1, 2, 3)):
                px, py = _flip(x, k & 2), _flip(y, k & 1)
                slot = _slot(x, y, swapped[a]) if sending else _slot(px, py, swapped[a])
                piece = outs[a].at[slot, mine]
                cps.append(pltpu.make_async_remote_copy(
                    src_ref=piece, dst_ref=piece, send_sem=send_sems.at[3 * a + j], recv_sem=recv_sems.at[3 * a + j],
                    device_id=(px, py, c), device_id_type=MESH))
        return cps

    def start(ins, outs, send_sems, recv_sems):
        for cp in copies(outs, send_sems, recv_sems, True):
            cp.start()

    def finish(ins, outs, send_sems, recv_sems):
        for cp in copies(outs, send_sems, recv_sems, False):
            cp.wait_recv()
        for cp in copies(outs, send_sems, recv_sems, True):
            cp.wait_send()

    return _Carried(bufs, [_sds(b.shape, b.dtype) for b in bufs], {a: a for a in range(n)}, 3 * n, start, finish)


HBM_SPEC = pl.BlockSpec(memory_space=pltpu.HBM)
SEM_SPEC = pl.BlockSpec(memory_space=pltpu.SEMAPHORE)
SIDE_EFFECT = pltpu.SideEffectType.DATAFLOW_SIDE_EFFECTING


def _ici_pieces(buf, send_sems, recv_sems, swapped, sending):
    x, y, c = _place()
    h = buf.shape[1] // 2
    mine = pl.ds(pl.multiple_of(c * h, 8), h)
    cps = []
    for j, k in enumerate((1, 2, 3)):
        px, py = _flip(x, k & 2), _flip(y, k & 1)
        piece = buf.at[_slot(x, y, swapped) if sending else _slot(px, py, swapped), mine]
        cps.append(pltpu.make_async_remote_copy(src_ref=piece, dst_ref=piece, send_sem=send_sems.at[j],
                                                recv_sem=recv_sems.at[j], device_id=(px, py, c), device_id_type=MESH))
    return cps


def _gather_ici_start(buf, after, swapped, name):
    def body(buf_ref, after_ref, send_sems, recv_sems, thru, token):
        for cp in _ici_pieces(thru, send_sems, recv_sems, swapped, True):
            cp.start()
        token[...] = jnp.zeros_like(token)

    return pl.pallas_call(
        body, name=name,
        out_shape=(pltpu.SemaphoreType.DMA((3,)), pltpu.SemaphoreType.DMA((3,)), pltpu.HBM(buf.shape, buf.dtype),
                   jax.ShapeDtypeStruct((8, 128), F32)),
        in_specs=(HBM_SPEC, ANY), out_specs=(SEM_SPEC, SEM_SPEC, HBM_SPEC, VMEM_SPEC), input_output_aliases={0: 2},
        compiler_params=pltpu.CompilerParams(has_side_effects=SIDE_EFFECT),
    )(pltpu.with_memory_space_constraint(buf, pltpu.HBM), after)


def _gather_ici_wait(send_sems, recv_sems, thru, after, swapped, name):
    def body(thru_ref, send_sems, recv_sems, after_ref, out_ref):
        for cp in _ici_pieces(out_ref, send_sems, recv_sems, swapped, True):
            cp.wait_send()
        for cp in _ici_pieces(out_ref, send_sems, recv_sems, swapped, False):
            cp.wait_recv()

    return pl.pallas_call(
        body, name=name, out_shape=pltpu.HBM(thru.shape, thru.dtype),
        in_specs=(HBM_SPEC, SEM_SPEC, SEM_SPEC, ANY), out_specs=HBM_SPEC, input_output_aliases={0: 0},
        compiler_params=pltpu.CompilerParams(has_side_effects=SIDE_EFFECT),
    )(thru, send_sems, recv_sems, after)


def _all8_copies(buf, send_sems, recv_sems, sending):
    x, y, c = _place()
    cps = []
    for k in range(1, 8):
        px, py, pc = _flip(x, k & 4), _flip(y, k & 2), _flip(c, k & 1)
        slot = buf.at[4 * x + 2 * y + c] if sending else buf.at[4 * px + 2 * py + pc]
        cps.append(pltpu.make_async_remote_copy(src_ref=slot, dst_ref=slot, send_sem=send_sems.at[k - 1],
                                                recv_sem=recv_sems.at[k - 1], device_id=(px, py, pc), device_id_type=MESH))
    return cps


def _all8_start(buf, name):
    def body(buf_ref, send_sems, recv_sems, thru, token):
        for cp in _all8_copies(thru, send_sems, recv_sems, True):
            cp.start()
        token[...] = jnp.zeros_like(token)

    return pl.pallas_call(
        body, name=name,
        out_shape=(pltpu.SemaphoreType.DMA((7,)), pltpu.SemaphoreType.DMA((7,)), pltpu.HBM(buf.shape, buf.dtype),
                   jax.ShapeDtypeStruct((8, 128), F32)),
        in_specs=(HBM_SPEC,), out_specs=(SEM_SPEC, SEM_SPEC, HBM_SPEC, VMEM_SPEC), input_output_aliases={0: 2},
        compiler_params=pltpu.CompilerParams(has_side_effects=SIDE_EFFECT),
    )(pltpu.with_memory_space_constraint(buf, pltpu.HBM))


def _all8_wait(send_sems, recv_sems, thru, after, name):
    def body(thru_ref, send_sems, recv_sems, after_ref, out_ref):
        for cp in _all8_copies(out_ref, send_sems, recv_sems, True):
            cp.wait_send()
        for cp in _all8_copies(out_ref, send_sems, recv_sems, False):
            cp.wait_recv()

    return pl.pallas_call(
        body, name=name, out_shape=pltpu.HBM(thru.shape, thru.dtype),
        in_specs=(HBM_SPEC, SEM_SPEC, SEM_SPEC, ANY), out_specs=HBM_SPEC, input_output_aliases={0: 0},
        compiler_params=pltpu.CompilerParams(has_side_effects=SIDE_EFFECT),
    )(thru, send_sems, recv_sems, after)


def _sum8(g, name):
    def body(g_ref, o_ref):
        tot = g_ref[0]
        for d in range(1, 8):
            tot = tot + g_ref[d]
        o_ref[...] = tot

    return pl.pallas_call(body, out_shape=_sds(g.shape[1:], F32), name=name)(g)


def _carry_gather_forward(bufs, swapped):
    n = len(bufs)

    def copies(outs, send_sems, recv_sems, sending):
        x, y, c = _place()
        cps = []
        for a in range(n):
            h = outs[a].shape[1] // 2
            rows = pl.ds(pl.multiple_of((c if sending else 1 - c) * h, 8), h)
            for j, k in enumerate((1, 2, 3)):
                piece = outs[a].at[_slot(_flip(x, k & 2), _flip(y, k & 1), swapped[a]), rows]
                cps.append(pltpu.make_async_remote_copy(
                    src_ref=piece, dst_ref=piece, send_sem=send_sems.at[3 * a + j], recv_sem=recv_sems.at[3 * a + j],
                    device_id=(x, y, 1 - c), device_id_type=MESH))
        return cps

    def start(ins, outs, send_sems, recv_sems):
        for cp in copies(outs, send_sems, recv_sems, True):
            cp.start()

    def finish(ins, outs, send_sems, recv_sems):
        for cp in copies(outs, send_sems, recv_sems, False):
            cp.wait_recv()
        for cp in copies(outs, send_sems, recv_sems, True):
            cp.wait_send()

    return _Carried(bufs, [_sds(b.shape, b.dtype) for b in bufs], {a: a for a in range(n)}, 3 * n, start, finish)


def _pack(arrs, rows_multiple=8):
    parts, offs, row = [], [], 0
    for a in arrs:
        flat = a.reshape(-1)
        nrow = -(-flat.shape[0] // D)
        parts.append(jnp.pad(flat, (0, nrow * D - flat.shape[0])))
        offs.append(row)
        row += nrow
    total = -(-row // rows_multiple) * rows_multiple
    if total > row:
        parts.append(jnp.zeros(((total - row) * D,), F32))
    return jnp.concatenate(parts).reshape(total, D), offs


def _unpack(packed, offs, shapes):
    out = []
    for off, shp in zip(offs, shapes):
        size = int(np.prod(shp))
        nrow = -(-size // D)
        out.append(packed[off:off + nrow].reshape(-1)[:size].reshape(shp))
    return out


def _to_bf16_slot(w, slot, name, after=None):
    r, c = w.shape
    tm = next(cand for cand in (512, 352, 256, 128, 64, 32, 16) if r % cand == 0)

    def body(slot_ref, w_ref, *rest):
        rest[-1][...] = w_ref[...].astype(BF16)

    in_specs = [pl.BlockSpec((tm, c), lambda i, slot_ref: (i, 0))]
    if after is not None:
        in_specs.append(pl.BlockSpec((8, 128), lambda i, slot_ref: (0, 0)))
    return pl.pallas_call(
        body,
        grid_spec=pltpu.PrefetchScalarGridSpec(
            num_scalar_prefetch=1, grid=(r // tm,), in_specs=in_specs,
            out_specs=pl.BlockSpec((None, tm, c), lambda i, slot_ref: (slot_ref[0], i, 0))),
        out_shape=_sds((4, r, c), BF16), compiler_params=_params(("arbitrary",)), name=name,
    )(slot, w, *([] if after is None else [after]))


def _unshard_cols(g):
    s, k, n = g.shape
    return jnp.transpose(g, (1, 0, 2)).reshape(k, s * n)


def _ff_swap(v):
    b = FF_BLOCK
    return jnp.concatenate([v[..., 0:b], v[..., 2 * b:3 * b], v[..., b:2 * b], v[..., 3 * b:4 * b]], axis=-1)


LATE = ("attn_o", "conv_o", "mix_o", "up", "down")
EARLY_GRADS = ("down", "up", "mix_o", "attn_o", "conv_o")


def _weight_views(bufs):
    return {"up": bufs["up"], "attn_o": _unshard_cols(bufs["attn_o"]), "conv_o": _unshard_cols(bufs["conv_o"]),
            "mix_o": bufs["mix_o"].reshape(D, D), "down": bufs["down"].reshape(D_FF, D)}


def _pair_sums(names, grads, recv, dist):
    return [_pair_sum(g, r, dist["core"], "pair_sum_" + n) for n, g, r in zip(names, grads, recv)]


def _reduce_halves(names, parts, from_chips, dist):
    return [_chip_sum(p, r, jnp.concatenate([dist["slots"][SWAPPED[n]], dist["core"]]), "chip_sum_" + n)
            for n, p, r in zip(names, parts, from_chips)]


FUSED_TILE = 256
WIDE_TILE = 512


def _gates(z):
    return [(z, 512, 5), (z, 512, 6), (z, 512, 7), (z, 512, 8)]


def _mix_out(a, cb, z, x, w_mix_o, g_post, gt, g_pre2, sc2, sh2, name):
    def lhs(av, cv, ga0, ga1, gb0, gb1):
        ga, gb = jnp.concatenate([ga0, ga1], axis=1), jnp.concatenate([gb0, gb1], axis=1)
        return _sigmoid(ga) * av + _sigmoid(gb) * cv

    def fn(ym, y, xv, gv, gtv, g2v, scv, shv):
        yn, _ = _rms_stats(ym)
        x1 = xv + gtv * (yn * gv)
        xn, _ = _rms_stats(x1)
        return ym, y, x1, (xn * g2v) * (1.0 + scv) + shv

    return _matmul_rows(w_mix_o, form="nn", tm=min(WIDE_TILE, x.shape[0]), tk=D, fn=fn, a_rows=[(a, D, 0), (cb, D, 0)] + _gates(z),
                        a_fn=lhs, rows=[(x, D, 0)], consts=[g_post, gt, g_pre2, sc2, sh2],
                        row_outs=[(F32, D), (BF16, D), (F32, D), (BF16, D)], acc_outs=[], name=name)


def _down_tail(act, w_down, x1, target, g, gt, name):
    def fn(yv, xv, tv, gv, gtv):
        yn, r = _rms_stats(yv)
        e = xv + gtv * (yn * gv) - tv
        dx2 = e * (1.0 / D)
        dyn = dx2 * gtv
        return (dx2, _rms_bwd(dyn * gv, yn, r), _colsum(e * e) * (0.5 / D), _colsum(dyn * yn),
                _colsum(dx2 * (yn * gv)))

    return _matmul_rows(w_down, form="nn", a=act, tm=min(WIDE_TILE, x1.shape[0]), tk=D_FF, fn=fn,
                        rows=[(x1, D, 0), (target, D, 0)], consts=[g, gt], row_outs=[(F32, D), (BF16, D)],
                        acc_outs=[(1, D)] * 3, name=name)


def _up_dx_tail(dup, w_up, x1, dx2, ym, g_pre2, sc2, g_post, gt, name):
    def fn(dh, xv, dov, ymv, g2v, scv, gv, gtv):
        xn, r = _rms_stats(xv)
        dyn = dh * (1.0 + scv)
        dx1 = dov + _rms_bwd(dyn * g2v, xn, r)
        yn, r2 = _rms_stats(ymv)
        dynm = dx1 * gtv
        return (dx1, _rms_bwd(dynm * gv, yn, r2), _colsum(dh), _colsum(dh * (xn * g2v)), _colsum(dyn * xn),
                _colsum(dynm * yn), _colsum(dx1 * (yn * gv)))

    return _matmul_rows(w_up, form="nt", a=dup, tm=min(FUSED_TILE, x1.shape[0]), tk=2 * D_FF, fn=fn,
                        rows=[(x1, D, 0), (dx2, D, 0), (ym, D, 0)], consts=[g_pre2, sc2, g_post, gt],
                        row_outs=[(F32, D), (BF16, D)], acc_outs=[(1, D)] * 5, name=name)


def _mix_dx_gates(dym, w_mix_o, a, cb, z, name):
    def fn(dy, av, cv, ga0, ga1, gb0, gb1):
        sa = _sigmoid(jnp.concatenate([ga0, ga1], axis=1))
        sb = _sigmoid(jnp.concatenate([gb0, gb1], axis=1))
        dcb = dy * sb
        dga = dy * av * (sa * (1.0 - sa))
        dgb = dy * cv * (sb * (1.0 - sb))
        return dy * sa, dcb, dga, dgb, _colsum(dcb), _colsum(dga), _colsum(dgb)

    return _matmul_rows(w_mix_o, form="nt", a=dym, tm=min(WIDE_TILE, a.shape[0]), tk=D, fn=fn,
                        rows=[(a, D, 0), (cb, D, 0)] + _gates(z), consts=[], row_outs=[(BF16, D)] * 4,
                        acc_outs=[(1, D)] * 3, name=name)


def _local_step(x, target, mod, w_in, late, small, dist=None):
    sh_m, sc_m, gt_m, sh_f, sc_f, gt_f = mod
    t = x.shape[0]
    tmm = min(1024, t)
    late_swapped = [SWAPPED[n] for n in LATE]

    h1 = _pre_norm(x, small["g_pre_mix"], sc_m, sh_m, "pre_norm_mix")
    if callable(w_in):
        w_in = w_in(h1)
    z = _matmul(h1, w_in, form="nn", out_dtype=F32, tm=min(FUSED_TILE, t), tn=D_IN, tk=D, bias=small["b_in"], name="mm_in")
    conv = (z, small["w_dw_conv"], small["b_dw_conv"], small["g_conv_ln"], small["b_conv_ln"], "conv_branch")
    if dist is None:
        att = _attention(z, small["gen"], "attention")
        u1, u3 = _conv_branch(*conv)
        bufs = dict(late)
    else:
        mid = [n for n in LATE if n != "down"]
        mid_swapped = [SWAPPED[n] for n in mid]
        att, landed = _attention(z, small["gen"], "attention",
                                 carried=_carry_gather_ici([late[n] for n in mid], mid_swapped))
        (u1, u3), gathered = _conv_branch(*conv, carried=_carry_gather_forward(landed, mid_swapped))
        bufs = dict(zip(mid, gathered))
        bufs["down"] = late["down"]
    w = _weight_views(bufs)
    w["in"] = w_in
    a = _matmul(att, w["attn_o"], form="nn", out_dtype=F32, tm=tmm, tn=512, tk=512, name="mm_attn_o")
    cb = _matmul(u3, w["conv_o"], form="nn", out_dtype=F32, tm=tmm, tn=512, tk=512, bias=small["b_conv_o"], name="mm_conv_o")
    ym, y, x1, h2 = _mix_out(a, cb, z, x, w["mix_o"], small["g_post_mix"], gt_m, small["g_pre_ffn"], sc_f, sh_f, "mix_out")
    mm_up = dict(form="nn", out_dtype=F32, tm=min(FUSED_TILE, t), tn=2 * D_FF, tk=D, name="mm_up")
    ffn_act = (small["w_dw_ffn"], small["b_dw_ffn"], "ffn_act")
    if dist is None:
        up = _matmul(h2, w["up"], **mm_up)
        act = _ffn_act(up, *ffn_act)
    else:
        up, landed = _matmul(h2, w["up"], carried=_carry_gather_ici([late["down"]], [False]), **mm_up)
        act, down = _ffn_act(up, *ffn_act, carried=_carry_gather_forward(landed, [False]))
        w["down"] = down[0].reshape(D_FF, D)

    dx2, dyf, loss_cols, d_g_post_ffn, d_gt_f = _down_tail(act, w["down"], x1, target, small["g_post_ffn"], gt_f, "down_tail")
    dact = _matmul(dyf, w["down"], form="nt", out_dtype=F32, tm=tmm, tn=FF_BLOCK, tk=D, name="mm_down_dx")
    g_down = _matmul(act, dyf, form="tn", out_dtype=F32, tm=FF_BLOCK, tn=512, tk=t, name="mm_down_dw")
    dup, d_w_dw_ffn, d_b_dw_ffn = _ffn_act_bwd(dact, up, small["w_dw_ffn"], small["b_dw_ffn"], "ffn_act_bwd")
    dx1, dym, d_sh_f, d_sc_f, d_g_pre_ffn, d_g_post_mix, d_gt_m = _up_dx_tail(
        dup, w["up"], x1, dx2, ym, small["g_pre_ffn"], sc_f, small["g_post_mix"], gt_m, "up_dx_tail")
    g_up = _matmul(h2, dup, form="tn", out_dtype=F32, tm=512, tn=FF_BLOCK, tk=t, out_sharded=True, name="mm_up_dw")
    da, dcb, dgate_a, dgate_b, d_b_conv_o, sga, sgb = _mix_dx_gates(dym, w["mix_o"], a, cb, z, "mix_dx_gates")
    g_mix_o = _matmul(y, dym, form="tn", out_dtype=F32, tm=D, tn=512, tk=t, name="mm_mix_o_dw")
    datt = _matmul(da, w["attn_o"], form="nt", out_dtype=F32, tm=tmm, tn=512, tk=D, name="mm_attn_o_dx")
    g_attn_o = _matmul(att, da, form="tn", out_dtype=F32, tm=512, tn=256, tk=t, out_sharded=True, name="mm_attn_o_dw")
    du3 = _matmul(dcb, w["conv_o"], form="nt", out_dtype=F32, tm=tmm, tn=512, tk=D, name="mm_conv_o_dx")
    g_conv_o = _matmul(u3, dcb, form="tn", out_dtype=F32, tm=512, tn=256, tk=t, out_sharded=True, name="mm_conv_o_dw")
    big = {"attn_o": g_attn_o, "conv_o": g_conv_o, "mix_o": g_mix_o.reshape(4, 256, D),
           "up": g_up, "down": g_down.reshape(4, D_FF // 4, D)}
    conv_bwd = (du3, u1, z, small["w_dw_conv"], small["g_conv_ln"], small["b_conv_ln"], "conv_branch_bwd")
    in_dw = dict(form="tn", out_dtype=F32, tm=512, tn=IN_SHARD, tk=t, out_sharded=True, name="mm_in_dw")
    in_dx = dict(form="nt", out_dtype=F32, tm=min(WIDE_TILE, t), tn=D, tk=D_IN, name="mm_in_dx")
    if dist is None:
        dglu, d_w_dw_conv, d_b_dw_conv, d_g_conv_ln, d_b_conv_ln, sglu = _conv_branch_bwd(*conv_bwd)
        dq, dk, dv, sq, sk, sv, dgen = _attention_bwd(z, datt, small["gen"], "attention_bwd")
        dz = jnp.concatenate([dq, dk, dv, dglu, dgate_a, dgate_b], axis=1)
        big["in"] = _matmul(h1, dz, **in_dw)
        dh1 = _matmul(dz, w_in, **in_dx)
    else:
        early = [big[n] for n in EARLY_GRADS]
        (dglu, d_w_dw_conv, d_b_dw_conv, d_g_conv_ln, d_b_conv_ln, sglu), recv = _conv_branch_bwd(
            *conv_bwd, carried=_carry_pair_exchange(early))
        parts = _pair_sums(EARLY_GRADS, early, recv, dist)
        (dq, dk, dv, sq, sk, sv, dgen), from_chips = _attention_bwd(
            z, datt, small["gen"], "attention_bwd",
            carried=_carry_chip_exchange(parts, [SWAPPED[n] for n in EARLY_GRADS]))
        halves = _reduce_halves(EARLY_GRADS, parts, from_chips, dist)
        dz = jnp.concatenate([dq, dk, dv, dglu, dgate_a, dgate_b], axis=1)
        g_in, shards = _matmul(h1, dz, carried=_carry_pair_share(halves), **in_dw)
        big = dict(zip(EARLY_GRADS, shards))
        recv_in = _run_carried(_carry_pair_exchange([g_in]), "pair_exchange_in")
        part_in = _pair_sums(("in",), [g_in], recv_in, dist)
        dh1, from_chips_in = _matmul(dz, w_in, carried=_carry_chip_exchange(part_in, [False]), **in_dx)
        half_in = _reduce_halves(("in",), part_in, from_chips_in, dist)
        big["in"] = _run_carried(_carry_pair_share(half_in), "pair_share_in")[0]
    d_b_in = jnp.concatenate([sq, sk, sv, sglu, sga, sgb], axis=1)
    grad_x, d_sh_m, d_sc_m, d_g_pre_mix = _pre_norm_bwd(dh1, x, dx1, small["g_pre_mix"], sc_m, "pre_norm_mix_bwd")

    dmod = [d_sh_m, d_sc_m, d_gt_m, d_sh_f, d_sc_f, d_gt_f]
    sm = {"g_pre_mix": d_g_pre_mix, "g_post_mix": d_g_post_mix, "b_in": d_b_in, "gen": dgen,
          "w_dw_conv": d_w_dw_conv, "b_dw_conv": d_b_dw_conv, "g_conv_ln": d_g_conv_ln, "b_conv_ln": d_b_conv_ln,
          "b_conv_o": d_b_conv_o, "g_pre_ffn": d_g_pre_ffn, "g_post_ffn": d_g_post_ffn,
          "w_dw_ffn": d_w_dw_ffn, "b_dw_ffn": d_b_dw_ffn}
    return loss_cols, grad_x, dmod, big, sm


BIG = ("in", "attn_o", "conv_o", "mix_o", "up", "down")
SWAPPED = {"in": False, "attn_o": False, "conv_o": False, "mix_o": False, "up": True, "down": False}
SMALL_ORDER = ("b_ada", "g_pre_mix", "g_post_mix", "b_in", "rel_bias", "b_dw_conv", "g_conv_ln", "b_conv_ln",
               "b_conv_o", "g_pre_ffn", "g_post_ffn", "b_dw_ffn", "w_dw_conv", "w_dw_ffn")


def kernel(x, c, w_ada, b_ada, g_pre_mix, g_post_mix, w_in, b_in, rel_bias, w_attn_o, w_dw_conv, b_dw_conv, g_conv_ln, b_conv_ln, w_conv_o, b_conv_o, w_mix_o, g_pre_ffn, g_post_ffn, w_up, w_dw_ffn, b_dw_ffn, w_down, loss_target, m_w_ada, m_b_ada, m_g_pre_mix, m_g_post_mix, m_w_in, m_b_in, m_rel_bias, m_w_attn_o, m_w_dw_conv, m_b_dw_conv, m_g_conv_ln, m_b_conv_ln, m_w_conv_o, m_b_conv_o, m_w_mix_o, m_g_pre_ffn, m_g_post_ffn, m_w_up, m_w_dw_ffn, m_b_dw_ffn, m_w_down, v_w_ada, v_b_ada, v_g_pre_mix, v_g_post_mix, v_w_in, v_b_in, v_rel_bias, v_w_attn_o, v_w_dw_conv, v_b_dw_conv, v_g_conv_ln, v_b_conv_ln, v_w_conv_o, v_b_conv_o, v_w_mix_o, v_g_pre_ffn, v_g_post_ffn, v_w_up, v_w_dw_ffn, v_b_dw_ffn, v_w_down):
    given = dict(locals())
    ax, ay, ac = lax.axis_index("x"), lax.axis_index("y"), lax.axis_index("c")
    shard = 2 * ax + ay
    me = 4 * ax + 2 * ay + ac
    xs, target = x[0], loss_target[0]

    slots = {sw: _slot(ax, ay, sw).astype(jnp.int32).reshape(1) for sw in (False, True)}
    own = {"in": _to_bf16_slot(w_in[0], slots[False], "cast_in")}

    c_pad = jnp.pad(c, ((0, 7), (0, 0)))
    c_g, _ = _allgather8(c_pad, "gather_c")
    c_all = c_g[:, 0, :]
    b_ada_shard = lax.dynamic_slice(b_ada, (0, shard * ADA_SHARD), (1, ADA_SHARD))
    mod_shard, c_act = _ada_fwd(c_all, w_ada[0], b_ada_shard, "ada_fwd")
    small_in = [jnp.pad(mod_shard, ((0, 8), (0, 0))),
                jnp.pad(w_dw_conv[0], ((0, 1), (0, 0))),
                jnp.pad(w_dw_ffn[0], ((0, 13), (0, 0)))]
    mod_g, wdc_g, wdf_g = _gather_shards(small_in, [False, False, True], "gather_small")
    mod_all = jnp.transpose(mod_g[:, :8, :], (1, 0, 2)).reshape(8, 6 * D)
    in_send, in_recv, in_flight, token = _gather_ici_start(own["in"], mod_g, False, "gather_w_in_start")

    def w_in_ready(after):
        landed = _gather_ici_wait(in_send, in_recv, in_flight, after, False, "gather_w_in_wait")
        return _run_carried(_carry_gather_forward([landed], [False]), "gather_forward_in")[0]

    for n in LATE:
        own[n] = _to_bf16_slot(given["w_" + n][0], slots[SWAPPED[n]], "cast_" + n, after=token)
    mod_row = lax.dynamic_slice(mod_all, (me, 0), (1, 6 * D)) + token[0:1, 0:1]
    mod = [mod_row[:, k * D:(k + 1) * D] for k in range(6)]

    core = ac.astype(jnp.int32).reshape(1)
    dist = {"core": core, "slots": slots}

    sel = jnp.asarray(_toeplitz_map())
    rel_pad = jnp.pad(rel_bias[0], ((0, 0), (0, REL_PAD - (2 * MAX_REL + 1))))
    gen = _select_call(rel_pad, sel.T.astype(BF16), "bias_rows")
    small = {"g_pre_mix": g_pre_mix, "g_post_mix": g_post_mix, "b_in": b_in, "gen": gen,
             "w_dw_conv": _unshard_cols(wdc_g[:, :CONV_K, :]), "b_dw_conv": b_dw_conv, "g_conv_ln": g_conv_ln,
             "b_conv_ln": b_conv_ln, "b_conv_o": b_conv_o, "g_pre_ffn": g_pre_ffn, "g_post_ffn": g_post_ffn,
             "w_dw_ffn": _unshard_cols(wdf_g[:, :FFN_K, :]), "b_dw_ffn": _ff_swap(b_dw_ffn)}

    loss_cols, grad_x, dmod, reduced, sm = _local_step(xs, target, mod, w_in_ready, {n: own[n] for n in LATE}, small, dist)

    d_rel = _select_call(sm["gen"], sel.astype(BF16), "bias_fold")[:, :2 * MAX_REL + 1]
    small_grads = {"g_pre_mix": sm["g_pre_mix"], "g_post_mix": sm["g_post_mix"], "b_in": sm["b_in"], "rel_bias": d_rel[None],
                   "b_dw_conv": sm["b_dw_conv"], "g_conv_ln": sm["g_conv_ln"], "b_conv_ln": sm["b_conv_ln"],
                   "b_conv_o": sm["b_conv_o"], "g_pre_ffn": sm["g_pre_ffn"], "g_post_ffn": sm["g_post_ffn"],
                   "b_dw_ffn": _ff_swap(sm["b_dw_ffn"]), "w_dw_conv": sm["w_dw_conv"], "w_dw_ffn": _ff_swap(sm["w_dw_ffn"])}
    order = [n for n in SMALL_ORDER if n != "b_ada"]
    packed, offs = _pack([jnp.concatenate(dmod, axis=1)] + [small_grads[n] for n in order] + [loss_cols])
    mine = lax.dynamic_update_slice(jnp.zeros((8,) + packed.shape, F32), packed[None], (me, 0, 0))
    sg_send, sg_recv, sg_flight, sg_token = _all8_start(mine, "gather_small_grads_start")

    out = {}
    for n in BIG:
        g, dl, nm, nv = _adamw(given["w_" + n][0], reduced[n], given["m_w_" + n][0], given["v_w_" + n][0],
                               "adamw_" + n, sg_token)
        out["grad_w_" + n], out["delta_w_" + n], out["new_m_w_" + n], out["new_v_w_" + n] = g[None], dl[None], nm[None], nv[None]
    every = _all8_wait(sg_send, sg_recv, sg_flight, out["delta_w_in"], "gather_small_grads_wait")
    total = _sum8(every, "sum_small_grads")
    loss = jnp.sum(total[offs[-1]])
    offs = offs[:-1]
    dmod_all = every[:, 0:6, :].reshape(8, 6 * D)
    full_shapes = {n: given[n].shape for n in order}
    full_shapes["w_dw_conv"], full_shapes["w_dw_ffn"] = (1, CONV_K, 512), (1, FFN_K, 2 * D_FF)
    sums = dict(zip(order, _unpack(total, offs[1:], [full_shapes[n] for n in order])))
    sums["b_ada"] = total[0:6].reshape(1, 6 * D)
    sums["w_dw_conv"] = lax.dynamic_slice(sums["w_dw_conv"], (0, 0, shard * 128), (1, CONV_K, 128))
    sums["w_dw_ffn"] = lax.dynamic_slice(sums["w_dw_ffn"], (0, 0, shard * FF_BLOCK), (1, FFN_K, FF_BLOCK))

    upd = dict(zip(SMALL_ORDER, _adamw_many(
        [(given[n], sums[n], given["m_" + n], given["v_" + n]) for n in SMALL_ORDER], "adamw_small")))

    dmod_shard = lax.dynamic_slice(dmod_all, (0, shard * ADA_SHARD), (8, ADA_SHARD))
    ada = _ada_bwd_adamw(c_act.T, dmod_shard, w_ada[0], m_w_ada[0], v_w_ada[0], "ada_bwd_adamw")

    out.update({"grad_w_ada": ada[0][None], "delta_w_ada": ada[1][None], "new_m_w_ada": ada[2][None],
                "new_v_w_ada": ada[3][None]})
    for n in SMALL_ORDER:
        out["grad_" + n], out["delta_" + n], out["new_m_" + n], out["new_v_" + n] = sums[n], *upd[n]

    weights = ["w_ada", "b_ada", "g_pre_mix", "g_post_mix", "w_in", "b_in", "rel_bias", "w_attn_o", "w_dw_conv", "b_dw_conv",
               "g_conv_ln", "b_conv_ln", "w_conv_o", "b_conv_o", "w_mix_o", "g_pre_ffn", "g_post_ffn", "w_up", "w_dw_ffn",
               "b_dw_ffn", "w_down"]
    return (loss, grad_x[None], *[out["grad_" + n] for n in weights], *[out["delta_" + n] for n in weights],
            *[out["new_m_" + n] for n in weights], *[out["new_v_" + n] for n in weights])
```

```python
import functools
import math

import numpy as np
import jax
import jax.numpy as jnp
from jax import lax
from jax.experimental import pallas as pl
from jax.experimental.pallas import tpu as pltpu

F32, BF16 = jnp.float32, jnp.bfloat16
MESH = pl.DeviceIdType.MESH

D = 1024
D_IN = 4608
D_FF = 2816
N_CHIPS = 4
IN_SHARD = D_IN // N_CHIPS
ADA_SHARD = 6 * D // N_CHIPS
CONV_K = 31
FFN_K = 3
N_HEADS = 8
CHUNK = 64
LEFT_CHUNKS = 8
MAX_REL = 128
EPS = 1e-6
NEG_INF = -1e30
Q_TILE = 256
WINDOW = Q_TILE + LEFT_CHUNKS * CHUNK
STEP_ROWS = 1024
SCORE_SLOTS = 2 * STEP_ROWS // Q_TILE
REL_PAD = 384
TOEP = 1024
ROW_TILE = 256
WORK_TILE = 512
VMEM_LIMIT = 60 * 1024 * 1024

ADAM_LR, ADAM_B1, ADAM_B2, ADAM_EPS, ADAM_WD, ADAM_STEP = 0.001, 0.9, 0.999, 1e-08, 0.01, 10


def _params(sem=None):
    return pltpu.CompilerParams(dimension_semantics=sem, vmem_limit_bytes=VMEM_LIMIT)


def _sds(shape, dtype):
    return jax.ShapeDtypeStruct(tuple(shape), dtype)


ANY = pl.BlockSpec(memory_space=pl.ANY)


class _Carried:
    def __init__(self, ins, out_shapes, aliases, n_sems, start, finish):
        self.ins, self.out_shapes, self.aliases = list(ins), list(out_shapes), dict(aliases)
        self.n_sems, self.start, self.finish = n_sems, start, finish


def _call(body, *, grid, in_specs, out_specs, out_shape, scratch_shapes, sem, name, args, carried=None):
    in_specs, out_specs, out_shape = list(in_specs), list(out_specs), list(out_shape)
    scratch_shapes = list(scratch_shapes)
    if carried is None:
        return pl.pallas_call(body, grid=grid, in_specs=in_specs, out_specs=out_specs, out_shape=out_shape,
                              scratch_shapes=scratch_shapes, compiler_params=_params(sem), name=name)(*args)
    n_in, n_out, n_scr = len(in_specs), len(out_specs), len(scratch_shapes)
    c_in, c_out = len(carried.ins), len(carried.out_shapes)

    def full(*refs):
        pos = [0]

        def take(k):
            part = refs[pos[0]:pos[0] + k]
            pos[0] += k
            return part

        ins, cins, outs, couts, scr = take(n_in), take(c_in), take(n_out), take(c_out), take(n_scr)
        send_sems, recv_sems = take(2)
        first = last = None
        for d, size in enumerate(grid):
            pid = pl.program_id(d)
            first = (pid == 0) if first is None else first & (pid == 0)
            last = (pid == size - 1) if last is None else last & (pid == size - 1)

        @pl.when(first)
        def _():
            carried.start(cins, couts, send_sems, recv_sems)

        body(*ins, *outs, *scr)

        @pl.when(last)
        def _():
            carried.finish(cins, couts, send_sems, recv_sems)

    sems = [pltpu.SemaphoreType.DMA((carried.n_sems,)), pltpu.SemaphoreType.DMA((carried.n_sems,))]
    return pl.pallas_call(
        full, grid=grid, in_specs=in_specs + [ANY] * c_in, out_specs=out_specs + [ANY] * c_out,
        out_shape=out_shape + carried.out_shapes, scratch_shapes=scratch_shapes + sems,
        input_output_aliases={n_in + k: n_out + v for k, v in carried.aliases.items()},
        compiler_params=_params(tuple("arbitrary" for _ in grid)), name=name,
    )(*args, *carried.ins)


def _run_carried(carried, name):
    c_in = len(carried.ins)

    def body(*refs):
        cins, couts = refs[:c_in], refs[c_in:c_in + len(carried.out_shapes)]
        send_sems, recv_sems = refs[-2:]
        carried.start(cins, couts, send_sems, recv_sems)
        carried.finish(cins, couts, send_sems, recv_sems)

    return pl.pallas_call(
        body, in_specs=[ANY] * c_in, out_specs=[ANY] * len(carried.out_shapes), out_shape=carried.out_shapes,
        scratch_shapes=[pltpu.SemaphoreType.DMA((carried.n_sems,)), pltpu.SemaphoreType.DMA((carried.n_sems,))],
        input_output_aliases=carried.aliases, name=name,
    )(*carried.ins)


def _matmul(a, b, *, form, out_dtype, tm, tn, tk, name, bias=None, add=None, out_sharded=False, carried=None):
    b3 = b.ndim == 3
    resident = 0
    if form == "nn":
        m, k = a.shape
        n = b.shape[0] * b.shape[2] if b3 else b.shape[1]
        dn = (((1,), (0,)), ((), ()))
        a_spec = pl.BlockSpec((tm, tk), lambda i, j, kk: (i, kk))
        if b3 and tn == n and tk == k:
            resident = b.shape[0]
            b_spec = pl.BlockSpec(b.shape, lambda i, j, kk: (0, 0, 0))
        else:
            b_spec = (pl.BlockSpec((None, tk, tn), lambda i, j, kk: (j, kk, 0)) if b3
                      else pl.BlockSpec((tk, tn), lambda i, j, kk: (kk, j)))
    elif form == "nt":
        m, k = a.shape
        n = b.shape[1] if b3 else b.shape[0]
        dn = (((1,), (1,)), ((), ()))
        a_spec = pl.BlockSpec((tm, tk), lambda i, j, kk: (i, kk))
        if b3 and tk == k:
            resident = b.shape[0]
            b_spec = pl.BlockSpec((resident, tn, b.shape[2]), lambda i, j, kk: (0, j, 0))
        else:
            b_spec = (pl.BlockSpec((None, tn, tk), lambda i, j, kk: (kk, j, 0)) if b3
                      else pl.BlockSpec((tn, tk), lambda i, j, kk: (j, kk)))
    else:
        k, m = a.shape
        n = b.shape[1]
        dn = (((0,), (0,)), ((), ()))
        a_spec = pl.BlockSpec((tk, tm), lambda i, j, kk: (kk, i))
        b_spec = pl.BlockSpec((tk, tn), lambda i, j, kk: (kk, j))
    assert m % tm == 0 and n % tn == 0 and k % tk == 0, (name, m, n, k, tm, tn, tk)
    nk = k // tk
    in_specs, args = [a_spec, b_spec], [a, b]
    if bias is not None:
        in_specs.append(pl.BlockSpec((1, tn), lambda i, j, kk: (0, j)))
        args.append(bias)
    if add is not None:
        in_specs.append(pl.BlockSpec((tm, tn), lambda i, j, kk: (i, j)))
        args.append(add)
    if out_sharded:
        out_shape = _sds((n // tn, m, tn), out_dtype)
        out_spec = pl.BlockSpec((None, tm, tn), lambda i, j, kk: (j, i, 0))
    else:
        out_shape = _sds((m, n), out_dtype)
        out_spec = pl.BlockSpec((tm, tn), lambda i, j, kk: (i, j))

    def body(*refs):
        a_ref, b_ref = refs[0], refs[1]
        pos = 2
        bias_ref = add_ref = None
        if bias is not None:
            bias_ref, pos = refs[pos], pos + 1
        if add is not None:
            add_ref, pos = refs[pos], pos + 1
        o_ref = refs[pos]
        if resident and form == "nn":
            ns = b_ref.shape[2]
            for s in range(resident):
                cols = slice(s * ns, (s + 1) * ns)
                ps = lax.dot_general(a_ref[...], b_ref[s], dn, preferred_element_type=F32)
                if bias_ref is not None:
                    ps = ps + bias_ref[:, cols]
                o_ref[:, cols] = ps.astype(o_ref.dtype)
            return
        if resident:
            ks = b_ref.shape[2]
            p = None
            for s in range(resident):
                ps = lax.dot_general(a_ref[:, s * ks:(s + 1) * ks], b_ref[s], dn, preferred_element_type=F32)
                p = ps if p is None else p + ps
        else:
            av, bv = a_ref[...], b_ref[...]
            if av.dtype != BF16:
                av = av.astype(BF16)
            if bv.dtype != BF16:
                bv = bv.astype(BF16)
            p = lax.dot_general(av, bv, dn, preferred_element_type=F32)

        def finish(acc):
            if bias_ref is not None:
                acc = acc + bias_ref[...]
            if add_ref is not None:
                acc = acc + add_ref[...]
            o_ref[...] = acc.astype(o_ref.dtype)

        if nk == 1:
            finish(p)
        else:
            acc_ref = refs[pos + 1]
            kk = pl.program_id(2)

            @pl.when(kk == 0)
            def _():
                acc_ref[...] = p

            @pl.when(kk > 0)
            def _():
                acc_ref[...] += p

            @pl.when(kk == nk - 1)
            def _():
                finish(acc_ref[...])

    res = _call(body, grid=(m // tm, n // tn, nk), in_specs=in_specs, out_specs=[out_spec], out_shape=[out_shape],
                scratch_shapes=[pltpu.VMEM((tm, tn), F32)] if nk > 1 else [],
                sem=("parallel", "parallel", "arbitrary"), name=name, args=args, carried=carried)
    return res[0] if carried is None else (res[0], res[1:])


def _rowcall(fn, rows, consts, row_outs, acc_outs, *, name, tm=ROW_TILE, col_grid=1):
    n_rows = rows[0][0].shape[0]
    assert n_rows % tm == 0
    grid = (col_grid, n_rows // tm)
    in_specs = [pl.BlockSpec((tm, w), functools.partial(lambda c, i, cb: (i, cb + c), cb=cb)) for _, w, cb in rows]
    in_specs += [pl.BlockSpec(k.shape, functools.partial(lambda c, i, nd: (0,) * nd, nd=k.ndim)) for k in consts]
    out_specs = [pl.BlockSpec((tm, w), lambda c, i: (i, c)) for _, _, _, w in row_outs]
    out_specs += [pl.BlockSpec((r, w), lambda c, i: (0, c)) for r, _, w in acc_outs]
    out_shape = [_sds((nr, nc), dt) for nr, nc, dt, _ in row_outs] + [_sds((r, nc), F32) for r, nc, _ in acc_outs]
    n_in, n_ro = len(rows) + len(consts), len(row_outs)

    def body(*refs):
        res = fn(*[r[...] for r in refs[:n_in]])
        if not isinstance(res, (tuple, list)):
            res = (res,)
        outs = refs[n_in:]
        for o_ref, val in zip(outs[:n_ro], res[:n_ro]):
            o_ref[...] = val.astype(o_ref.dtype)
        if acc_outs:
            first = pl.program_id(1) == 0

            @pl.when(first)
            def _():
                for o_ref, val in zip(outs[n_ro:], res[n_ro:]):
                    o_ref[...] = val

            @pl.when(jnp.logical_not(first))
            def _():
                for o_ref, val in zip(outs[n_ro:], res[n_ro:]):
                    o_ref[...] += val

    out = pl.pallas_call(
        body, grid=grid, in_specs=in_specs, out_specs=out_specs, out_shape=out_shape,
        compiler_params=_params(("arbitrary", "arbitrary")), name=name,
    )(*[r[0] for r in rows], *consts)
    return out


def _matmul_rows(b, *, form, tm, tk, fn, rows, consts, row_outs, acc_outs, name, a=None, a_rows=None, a_fn=None,
                 carried=None):
    b3 = b.ndim == 3
    resident = 0
    if form == "nn":
        k, n = b.shape
        b_spec = pl.BlockSpec((tk, n), lambda i, kk: (kk, 0))
        dn = (((1,), (0,)), ((), ()))
    else:
        n = b.shape[1] if b3 else b.shape[0]
        k = b.shape[0] * b.shape[2] if b3 else b.shape[1]
        if b3 and tk == k:
            resident = b.shape[0]
            b_spec = pl.BlockSpec(b.shape, lambda i, kk: (0, 0, 0))
        else:
            b_spec = (pl.BlockSpec((None, n, tk), lambda i, kk: (kk, 0, 0)) if b3
                      else pl.BlockSpec((n, tk), lambda i, kk: (0, kk)))
        dn = (((1,), (1,)), ((), ()))
    nk = k // tk
    lhs_in = [(a, tk, 0)] if a is not None else list(a_rows)
    assert a is not None or nk == 1
    m = lhs_in[0][0].shape[0]
    n_lhs = len(lhs_in)
    in_specs = [pl.BlockSpec((tm, tk), lambda i, kk: (i, kk))] if a is not None else [
        pl.BlockSpec((tm, w), functools.partial(lambda i, kk, cb: (i, cb), cb=cb)) for _, w, cb in a_rows]
    in_specs.append(b_spec)
    in_specs += [pl.BlockSpec((tm, w), functools.partial(lambda i, kk, cb: (i, cb), cb=cb)) for _, w, cb in rows]
    in_specs += [pl.BlockSpec(c.shape, functools.partial(lambda i, kk, nd: (0,) * nd, nd=c.ndim)) for c in consts]
    out_specs = [pl.BlockSpec((tm, w), lambda i, kk: (i, 0)) for _, w in row_outs]
    out_specs += [pl.BlockSpec((r, w), lambda i, kk: (0, 0)) for r, w in acc_outs]
    out_shape = [_sds((m, w), dt) for dt, w in row_outs] + [_sds((r, w), F32) for r, w in acc_outs]
    n_rows, n_consts, n_ro, n_acc = len(rows), len(consts), len(row_outs), len(acc_outs)

    def body(*refs):
        pos = n_lhs + 1
        row_refs, const_refs = refs[pos:pos + n_rows], refs[pos + n_rows:pos + n_rows + n_consts]
        pos += n_rows + n_consts
        out_refs, acc_refs = refs[pos:pos + n_ro], refs[pos + n_ro:pos + n_ro + n_acc]
        i, kk = pl.program_id(0), pl.program_id(1)
        if resident:
            b_ref, ks, p = refs[n_lhs], b.shape[2], None
            for s in range(resident):
                ps = lax.dot_general(refs[0][:, s * ks:(s + 1) * ks], b_ref[s], dn, preferred_element_type=F32)
                p = ps if p is None else p + ps
        else:
            lhs = refs[0][...] if a is not None else a_fn(*[r[...] for r in refs[:n_lhs]]).astype(BF16)
            p = lax.dot_general(lhs, refs[n_lhs][...], dn, preferred_element_type=F32)

        def finish(acc):
            extra = [r[...] for r in row_refs] + [c[...] for c in const_refs]
            res = fn(acc, lhs, *extra) if a is None else fn(acc, *extra)
            for o_ref, val in zip(out_refs, res[:n_ro]):
                o_ref[...] = val.astype(o_ref.dtype)
            if n_acc:
                @pl.when(i == 0)
                def _():
                    for o_ref, val in zip(acc_refs, res[n_ro:]):
                        o_ref[...] = val

                @pl.when(i > 0)
                def _():
                    for o_ref, val in zip(acc_refs, res[n_ro:]):
                        o_ref[...] += val

        if nk == 1:
            finish(p)
        else:
            acc_ref = refs[pos + n_ro + n_acc]

            @pl.when(kk == 0)
            def _():
                acc_ref[...] = p

            @pl.when(kk > 0)
            def _():
                acc_ref[...] += p

            @pl.when(kk == nk - 1)
            def _():
                finish(acc_ref[...])

    res = _call(body, grid=(m // tm, nk), in_specs=in_specs, out_specs=out_specs, out_shape=out_shape,
                scratch_shapes=[pltpu.VMEM((tm, n), F32)] if nk > 1 else [], sem=("arbitrary", "arbitrary"),
                name=name, args=[r[0] for r in lhs_in] + [b] + [r[0] for r in rows] + list(consts), carried=carried)
    own = n_ro + n_acc
    return res[:own] if carried is None else (res[:own], res[own:])


def _colsum(v):
    return jnp.sum(v, axis=0, keepdims=True)


def _sigmoid(v):
    return 1.0 / (1.0 + jnp.exp(-v))


_GELU_C = math.sqrt(2.0 / math.pi)


def _gelu(v):
    return 0.5 * v * (1.0 + jnp.tanh(_GELU_C * (v + 0.044715 * (v * v * v))))


def _gelu_and_grad(v):
    th = jnp.tanh(_GELU_C * (v + 0.044715 * (v * v * v)))
    g = 0.5 * v * (1.0 + th)
    dg = 0.5 * (1.0 + th) + 0.5 * v * (1.0 - th * th) * (_GELU_C * (1.0 + 3.0 * 0.044715 * (v * v)))
    return g, dg


def _rms_stats(v):
    r = lax.rsqrt(jnp.mean(v * v, axis=-1, keepdims=True) + EPS)
    return v * r, r


def _rms_bwd(dn, vn, r):
    return r * (dn - vn * jnp.mean(dn * vn, axis=-1, keepdims=True))


def _pre_norm(x, g, sc, sh, name):
    def fn(xv, gv, scv, shv):
        xn, _ = _rms_stats(xv)
        return (xn * gv) * (1.0 + scv) + shv
    return _rowcall(fn, [(x, D, 0)], [g, sc, sh], [(x.shape[0], D, BF16, D)], [], name=name,
                    tm=min(2 * ROW_TILE, x.shape[0]))[0]


def _pre_norm_bwd(dh, x, dx_other, g, sc, name):
    def fn(dhv, xv, dov, gv, scv):
        xn, r = _rms_stats(xv)
        yn = xn * gv
        dyn = dhv * (1.0 + scv)
        dx = _rms_bwd(dyn * gv, xn, r)
        return dov + dx, _colsum(dhv), _colsum(dhv * yn), _colsum(dyn * xn)
    t = x.shape[0]
    return _rowcall(fn, [(dh, D, 0), (x, D, 0), (dx_other, D, 0)], [g, sc], [(t, D, F32, D)],
                    [(1, D, D)] * 3, name=name, tm=min(2 * ROW_TILE, t))


CONV_HALO = 32


def _layer_norm_parts(u):
    mu = jnp.mean(u, axis=-1, keepdims=True)
    d = u - mu
    r = lax.rsqrt(jnp.mean(d * d, axis=-1, keepdims=True) + EPS)
    return d * r, r


LANES = 128
SUBLANE_ROWS = 8
CONV_ROWS = 64


def _lanes(c):
    return slice(c * LANES, (c + 1) * LANES)


def _conv_branch(z, w_dw, b_dw, g_ln, b_ln, name, tm=WORK_TILE, carried=None):
    t = z.shape[0]
    per = tm // CONV_HALO
    n_chunks = 512 // LANES

    def body(ga_ref, gb_ref, gah_ref, gbh_ref, w_ref, b_ref, g_ref, bl_ref, u1_ref, u3_ref, scr):
        i = pl.program_id(0)
        u0h = jnp.where(i > 0, gah_ref[...] * _sigmoid(gbh_ref[...]), 0.0)
        u0 = ga_ref[...] * _sigmoid(gb_ref[...])
        for c in range(n_chunks):
            scr[c, 0:CONV_HALO, :] = u0h[:, _lanes(c)]
            scr[c, CONV_HALO:CONV_HALO + tm, :] = u0[:, _lanes(c)]
        for c in range(n_chunks):
            for r0 in range(0, tm, CONV_ROWS):
                acc = jnp.zeros((CONV_ROWS, LANES), F32) + b_ref[:, _lanes(c)]
                for j in range(CONV_K):
                    acc = acc + w_ref[j:j + 1, _lanes(c)] * scr[c, pl.ds(r0 + CONV_HALO - (CONV_K - 1) + j, CONV_ROWS), :]
                u1_ref[r0:r0 + CONV_ROWS, _lanes(c)] = acc
        xh, _ = _layer_norm_parts(u1_ref[...])
        u2 = xh * g_ref[...] + bl_ref[...]
        u3_ref[...] = (u2 * _sigmoid(u2)).astype(BF16)

    cur = lambda cb: pl.BlockSpec((tm, 512), lambda i: (i, cb))
    halo = lambda cb: pl.BlockSpec((CONV_HALO, 512), lambda i: (jnp.maximum(i * per - 1, 0), cb))
    whole = lambda a: pl.BlockSpec(a.shape, lambda i: (0, 0))
    res = _call(
        body, grid=(t // tm,),
        in_specs=[cur(3), cur(4), halo(3), halo(4), whole(w_dw), whole(b_dw), whole(g_ln), whole(b_ln)],
        out_specs=[pl.BlockSpec((tm, 512), lambda i: (i, 0))] * 2,
        out_shape=[_sds((t, 512), F32), _sds((t, 512), BF16)],
        scratch_shapes=[pltpu.VMEM((n_chunks, CONV_HALO + tm, LANES), F32)],
        sem=("arbitrary",), name=name, args=(z, z, z, z, w_dw, b_dw, g_ln, b_ln), carried=carried)
    return res[:2] if carried is None else (res[:2], res[2:])


def _conv_branch_bwd(du3, u1, z, w_dw, g_ln, b_ln, name, tm=WORK_TILE, carried=None):
    t = z.shape[0]
    per = tm // CONV_HALO
    last = t // tm - 1
    n_chunks = 512 // LANES

    def du1_of(du3v, u1v, g, b):
        xh, r = _layer_norm_parts(u1v)
        u2 = xh * g + b
        s = _sigmoid(u2)
        du2 = du3v * (s * (1.0 + u2 * (1.0 - s)))
        dxh = du2 * g
        du1 = r * (dxh - jnp.mean(dxh, axis=-1, keepdims=True) - xh * jnp.mean(dxh * xh, axis=-1, keepdims=True))
        return du1, du2, xh

    def body(d_ref, u_ref, dn_ref, un_ref, ga_ref, gb_ref, gah_ref, gbh_ref, w_ref, g_ref, bl_ref,
             dglu_ref, dw_ref, dbdw_ref, dg_ref, dbl_ref, dbin_ref, scr, scd):
        i = pl.program_id(0)
        g, b = g_ref[...], bl_ref[...]
        du1, du2, xh = du1_of(d_ref[...], u_ref[...], g, b)
        du1n, _, _ = du1_of(dn_ref[...], un_ref[...], g, b)
        du1n = jnp.where(i < last, du1n, 0.0)
        sgb = _sigmoid(gb_ref[...])
        ga = ga_ref[...]
        u0 = ga * sgb
        u0h = jnp.where(i > 0, gah_ref[...] * _sigmoid(gbh_ref[...]), 0.0)
        for c in range(n_chunks):
            scd[c, 0:tm, :] = du1[:, _lanes(c)]
            scd[c, tm:tm + CONV_HALO, :] = du1n[:, _lanes(c)]
            scr[c, 0:CONV_HALO, :] = u0h[:, _lanes(c)]
            scr[c, CONV_HALO:CONV_HALO + tm, :] = u0[:, _lanes(c)]

        @pl.when(i == 0)
        def _():
            for ref in (dw_ref, dbdw_ref, dg_ref, dbl_ref, dbin_ref):
                ref[...] = jnp.zeros_like(ref)

        dsg = ga * (sgb * (1.0 - sgb))
        for c in range(n_chunks):
            gate = slice(512 + c * LANES, 512 + (c + 1) * LANES)
            for r0 in range(0, tm, CONV_ROWS):
                rows = slice(r0, r0 + CONV_ROWS)
                du0 = jnp.zeros((CONV_ROWS, LANES), F32)
                for j in range(CONV_K):
                    du0 = du0 + w_ref[j:j + 1, _lanes(c)] * scd[c, pl.ds(r0 + CONV_K - 1 - j, CONV_ROWS), :]
                dga = du0 * sgb[rows, _lanes(c)]
                dgb = du0 * dsg[rows, _lanes(c)]
                dglu_ref[rows, _lanes(c)] = dga.astype(BF16)
                dglu_ref[rows, gate] = dgb.astype(BF16)
                dbin_ref[:, _lanes(c)] += _colsum(dga)
                dbin_ref[:, gate] += _colsum(dgb)
            for j in range(CONV_K):
                dwj = jnp.zeros((SUBLANE_ROWS, LANES), F32)
                for r0 in range(0, tm, CONV_ROWS):
                    prod = (scd[c, pl.ds(r0, CONV_ROWS), :]
                            * scr[c, pl.ds(r0 + CONV_HALO - (CONV_K - 1) + j, CONV_ROWS), :])
                    dwj = dwj + jnp.sum(prod.reshape(CONV_ROWS // SUBLANE_ROWS, SUBLANE_ROWS, LANES), axis=0)
                dw_ref[j:j + 1, _lanes(c)] += _colsum(dwj)
        dbdw_ref[...] += _colsum(du1)
        dg_ref[...] += _colsum(du2 * xh)
        dbl_ref[...] += _colsum(du2)

    cur = lambda cb: pl.BlockSpec((tm, 512), lambda i: (i, cb))
    prev = lambda cb: pl.BlockSpec((CONV_HALO, 512), lambda i: (jnp.maximum(i * per - 1, 0), cb))
    nxt = pl.BlockSpec((CONV_HALO, 512), lambda i: (jnp.minimum((i + 1) * per, t // CONV_HALO - 1), 0))
    whole = lambda a: pl.BlockSpec(a.shape, lambda i: (0, 0))
    acc = lambda r, w: pl.BlockSpec((r, w), lambda i: (0, 0))
    res = _call(
        body, grid=(t // tm,),
        in_specs=[cur(0), cur(0), nxt, nxt, cur(3), cur(4), prev(3), prev(4), whole(w_dw), whole(g_ln), whole(b_ln)],
        out_specs=[pl.BlockSpec((tm, 1024), lambda i: (i, 0)), acc(CONV_K, 512), acc(1, 512), acc(1, 512),
                   acc(1, 512), acc(1, 1024)],
        out_shape=[_sds((t, 1024), BF16), _sds((CONV_K, 512), F32), _sds((1, 512), F32), _sds((1, 512), F32),
                   _sds((1, 512), F32), _sds((1, 1024), F32)],
        scratch_shapes=[pltpu.VMEM((n_chunks, CONV_HALO + tm, LANES), F32),
                        pltpu.VMEM((n_chunks, tm + CONV_HALO, LANES), F32)],
        sem=("arbitrary",), name=name, args=(du3, u1, du3, u1, z, z, z, z, w_dw, g_ln, b_ln), carried=carried)
    return res[:6] if carried is None else (res[:6], res[6:])


FF_BLOCK = D_FF // 2
FF_HALO = 8
FF_CHUNKS = FF_BLOCK // LANES


FF_ROWS = 64


def _ext_rows(ext):
    return next(d for d in (104, 88, 72, 56, 40, 24, 8) if ext % d == 0)


def _ffn_conv(w_ref, b_ref, scr, k, rows, r0=0):
    acc = b_ref[:, _lanes(k)] + w_ref[0:1, _lanes(k)] * scr[k, pl.ds(r0 + FF_HALO - 2, rows), :]
    acc = acc + w_ref[1:2, _lanes(k)] * scr[k, pl.ds(r0 + FF_HALO - 1, rows), :]
    return acc + w_ref[2:3, _lanes(k)] * scr[k, pl.ds(r0 + FF_HALO, rows), :]


def _ffn_act(up, w3, b3, name, tm=WORK_TILE, carried=None):
    t = up.shape[0]
    per = tm // FF_HALO
    wide = 2 * FF_BLOCK

    def body(u_ref, uh_ref, w_ref, b_ref, o_ref, scr):
        i = pl.program_id(1)
        for k in range(2 * FF_CHUNKS):
            scr[k, 0:FF_HALO, :] = jnp.where(i > 0, uh_ref[:, _lanes(k)], 0.0)
            scr[k, FF_HALO:FF_HALO + tm, :] = u_ref[:, _lanes(k)]
        for cc in range(FF_CHUNKS):
            for r0 in range(0, tm, FF_ROWS):
                val = _ffn_conv(w_ref, b_ref, scr, cc, FF_ROWS, r0)
                gate = _ffn_conv(w_ref, b_ref, scr, FF_CHUNKS + cc, FF_ROWS, r0)
                o_ref[r0:r0 + FF_ROWS, _lanes(cc)] = (_gelu(gate) * val).astype(BF16)

    res = _call(
        body, grid=(2, t // tm),
        in_specs=[pl.BlockSpec((tm, wide), lambda c, i: (i, c)),
                  pl.BlockSpec((FF_HALO, wide), lambda c, i: (jnp.maximum(i * per - 1, 0), c)),
                  pl.BlockSpec((FFN_K, wide), lambda c, i: (0, c)),
                  pl.BlockSpec((1, wide), lambda c, i: (0, c))],
        out_specs=[pl.BlockSpec((tm, FF_BLOCK), lambda c, i: (i, c))],
        out_shape=[_sds((t, D_FF), BF16)],
        scratch_shapes=[pltpu.VMEM((2 * FF_CHUNKS, FF_HALO + tm, LANES), F32)],
        sem=("arbitrary", "arbitrary"), name=name, args=(up, up, w3, b3), carried=carried)
    return res[0] if carried is None else (res[0], res[1:])


def _ffn_act_bwd(dact, up, w3, b3, name, tm=WORK_TILE):
    t = up.shape[0]
    per = tm // FF_HALO
    wide = 2 * FF_BLOCK
    last = t // tm - 1
    ext = tm + FF_HALO
    FF_EXT_ROWS = _ext_rows(ext)

    def body(u_ref, up_ref, un_ref, d_ref, dn_ref, w_ref, b_ref, o_ref, dw_ref, db_ref, scr, scd):
        i = pl.program_id(1)
        for k in range(2 * FF_CHUNKS):
            scr[k, 0:FF_HALO, :] = jnp.where(i > 0, up_ref[:, _lanes(k)], 0.0)
            scr[k, FF_HALO:FF_HALO + tm, :] = u_ref[:, _lanes(k)]
            scr[k, FF_HALO + tm:FF_HALO + ext, :] = un_ref[:, _lanes(k)]
        dn = jnp.where(i < last, dn_ref[...], 0.0)

        @pl.when(i == 0)
        def _():
            dw_ref[...] = jnp.zeros_like(dw_ref)
            db_ref[...] = jnp.zeros_like(db_ref)

        for cc in range(FF_CHUNKS):
            gc = FF_CHUNKS + cc
            for r0 in range(0, ext, FF_EXT_ROWS):
                rows = pl.ds(r0, FF_EXT_ROWS)
                val = _ffn_conv(w_ref, b_ref, scr, cc, FF_EXT_ROWS, r0)
                gel, dgel = _gelu_and_grad(_ffn_conv(w_ref, b_ref, scr, gc, FF_EXT_ROWS, r0))
                da = d_ref[r0:r0 + FF_EXT_ROWS, _lanes(cc)] if r0 + FF_EXT_ROWS <= tm else jnp.concatenate(
                    [d_ref[r0:tm, _lanes(cc)], dn[:, _lanes(cc)]], axis=0)
                scd[cc, rows, :] = da * gel
                scd[gc, rows, :] = da * val * dgel
            for k in (cc, gc):
                dwk = [jnp.zeros((SUBLANE_ROWS, LANES), F32) for _ in range(FFN_K)]
                dbk = jnp.zeros((SUBLANE_ROWS, LANES), F32)
                for r0 in range(0, tm, FF_ROWS):
                    shifted = [scd[k, pl.ds(r0 + FFN_K - 1 - j, FF_ROWS), :] for j in range(FFN_K)]
                    ucur = scr[k, pl.ds(r0 + FF_HALO, FF_ROWS), :]
                    o_ref[r0:r0 + FF_ROWS, _lanes(k)] = (
                        w_ref[0:1, _lanes(k)] * shifted[0] + w_ref[1:2, _lanes(k)] * shifted[1]
                        + w_ref[2:3, _lanes(k)] * shifted[2]).astype(BF16)
                    fold = lambda v: jnp.sum(v.reshape(FF_ROWS // SUBLANE_ROWS, SUBLANE_ROWS, LANES), axis=0)
                    for j in range(FFN_K):
                        dwk[j] = dwk[j] + fold(shifted[j] * ucur)
                    dbk = dbk + fold(shifted[FFN_K - 1])
                for j in range(FFN_K):
                    dw_ref[j:j + 1, _lanes(k)] += _colsum(dwk[j])
                db_ref[:, _lanes(k)] += _colsum(dbk)

    nblk = t // FF_HALO
    return pl.pallas_call(
        body, grid=(2, t // tm),
        in_specs=[pl.BlockSpec((tm, wide), lambda c, i: (i, c)),
                  pl.BlockSpec((FF_HALO, wide), lambda c, i: (jnp.maximum(i * per - 1, 0), c)),
                  pl.BlockSpec((FF_HALO, wide), lambda c, i: (jnp.minimum((i + 1) * per, nblk - 1), c)),
                  pl.BlockSpec((tm, FF_BLOCK), lambda c, i: (i, c)),
                  pl.BlockSpec((FF_HALO, FF_BLOCK), lambda c, i: (jnp.minimum((i + 1) * per, nblk - 1), c)),
                  pl.BlockSpec((FFN_K, wide), lambda c, i: (0, c)),
                  pl.BlockSpec((1, wide), lambda c, i: (0, c))],
        out_specs=[pl.BlockSpec((tm, wide), lambda c, i: (i, c)),
                   pl.BlockSpec((FFN_K, wide), lambda c, i: (0, c)),
                   pl.BlockSpec((1, wide), lambda c, i: (0, c))],
        out_shape=[_sds((t, 2 * D_FF), BF16), _sds((FFN_K, 2 * D_FF), F32), _sds((1, 2 * D_FF), F32)],
        scratch_shapes=[pltpu.VMEM((2 * FF_CHUNKS, FF_HALO + ext, LANES), F32),
                        pltpu.VMEM((2 * FF_CHUNKS, ext, LANES), F32)],
        compiler_params=_params(("arbitrary", "arbitrary")), name=name,
    )(up, up, up, dact, dact, w3, b3)


def _toeplitz_map():
    f = np.zeros((TOEP, REL_PAD), np.float32)
    for m in range(TOEP - 1):
        rel = (WINDOW - 1) - m
        f[m, int(np.clip(rel, -MAX_REL, MAX_REL)) + MAX_REL] = 1.0
    return f


def _split3(v):
    hi = v.astype(BF16)
    r1 = v - hi.astype(F32)
    mid = r1.astype(BF16)
    lo = (r1 - mid.astype(F32)).astype(BF16)
    return hi, mid, lo


def _exact_select(v, sel):
    out = None
    for part in _split3(v):
        p = jnp.dot(part, sel, preferred_element_type=F32)
        out = p if out is None else out + p
    return out


def _select_call(v, sel, name):
    def body(v_ref, s_ref, o_ref):
        o_ref[...] = _exact_select(v_ref[...], s_ref[...])
    return pl.pallas_call(body, out_shape=_sds((v.shape[0], sel.shape[1]), F32), name=name)(v, sel)


def _band_bias(gen_row):
    b0 = jnp.broadcast_to(gen_row, (Q_TILE, TOEP))
    bias = pltpu.roll(b0, TOEP - (Q_TILE - 1), 1, stride=1, stride_axis=0)[:, :WINDOW]
    qq = lax.broadcasted_iota(jnp.int32, (Q_TILE, WINDOW), 0) // CHUNK
    kc = lax.broadcasted_iota(jnp.int32, (Q_TILE, WINDOW), 1) // CHUNK
    return jnp.where((kc >= qq) & (kc <= qq + LEFT_CHUNKS), bias, NEG_INF)


PAD_ROWS = WINDOW - Q_TILE
NT_DIMS = (((1,), (1,)), ((), ()))
TN_DIMS = (((0,), (0,)), ((), ()))


def _head_mask(hh):
    lane = lax.broadcasted_iota(jnp.int32, (1, 128), 1)
    return (lane < 64) if hh == 0 else (lane >= 64)


SOFTMAX_ROWS = 16


def _probs_block(s_scr, bias, hh, rows, q_start):
    s = s_scr[rows, :] + bias[hh, rows, :]
    col = lax.broadcasted_iota(jnp.int32, (SOFTMAX_ROWS, WINDOW), 1)
    s = jnp.where(col >= PAD_ROWS - q_start, s, NEG_INF)
    p = jnp.exp(s - jnp.max(s, axis=-1, keepdims=True))
    return p / jnp.sum(p, axis=-1, keepdims=True)


def _attention(z, gen, name, carried=None):
    t = z.shape[0]
    n_i = t // STEP_ROWS

    def body(q_ref, k_ref, v_ref, g_ref, o_ref, kpad, vpad, bias, s_scr, p_scr):
        hp, i = pl.program_id(0), pl.program_id(1)

        @pl.when(i == 0)
        def _():
            kpad[0:PAD_ROWS, :] = jnp.zeros((PAD_ROWS, 128), BF16)
            vpad[0:PAD_ROWS, :] = jnp.zeros((PAD_ROWS, 128), BF16)
            kpad[PAD_ROWS:PAD_ROWS + t, :] = k_ref[...].astype(BF16)
            vpad[PAD_ROWS:PAD_ROWS + t, :] = v_ref[...].astype(BF16)
            for hh in range(2):
                bias[hh] = _band_bias(g_ref[pl.ds(2 * hp + hh, 1), :])

        for q0 in range(0, STEP_ROWS, Q_TILE):
            q_start = i * STEP_ROWS + q0
            win = pl.ds(pl.multiple_of(q_start, Q_TILE), WINDOW)
            out = None
            for hh in range(2):
                mask = _head_mask(hh)
                qm = jnp.where(mask, q_ref[q0:q0 + Q_TILE, :] * (CHUNK ** -0.5), 0.0).astype(BF16)
                slot = 2 * (q0 // Q_TILE) + hh
                s_scr[slot] = lax.dot_general(qm, kpad[win, :], NT_DIMS, preferred_element_type=F32)
                for r0 in range(0, Q_TILE, SOFTMAX_ROWS):
                    rows = slice(r0, r0 + SOFTMAX_ROWS)
                    p_scr[slot, rows, :] = _probs_block(s_scr.at[slot], bias, hh, rows, q_start).astype(BF16)
                o = jnp.dot(p_scr[slot], vpad[win, :], preferred_element_type=F32)
                out = jnp.where(mask, o, 0.0) if out is None else jnp.where(mask, o, out)
            o_ref[q0:q0 + Q_TILE, :] = out.astype(BF16)

    res = _call(
        body, grid=(4, n_i),
        in_specs=[pl.BlockSpec((STEP_ROWS, 128), lambda h, i: (i, h)),
                  pl.BlockSpec((t, 128), lambda h, i: (0, 4 + h)),
                  pl.BlockSpec((t, 128), lambda h, i: (0, 8 + h)),
                  pl.BlockSpec((N_HEADS, TOEP), lambda h, i: (0, 0))],
        out_specs=[pl.BlockSpec((STEP_ROWS, 128), lambda h, i: (i, h))],
        out_shape=[_sds((t, 512), BF16)],
        scratch_shapes=[pltpu.VMEM((PAD_ROWS + t, 128), BF16), pltpu.VMEM((PAD_ROWS + t, 128), BF16),
                        pltpu.VMEM((2, Q_TILE, WINDOW), F32), pltpu.VMEM((SCORE_SLOTS, Q_TILE, WINDOW), F32),
                        pltpu.VMEM((SCORE_SLOTS, Q_TILE, WINDOW), BF16)],
        sem=("arbitrary", "arbitrary"), name=name, args=(z, z, z, gen), carried=carried)
    return res[0] if carried is None else (res[0], res[1:])


def _attention_bwd(z, datt, gen, name, carried=None):
    t = z.shape[0]
    n_i = t // STEP_ROWS

    def body(q_ref, k_ref, v_ref, d_ref, g_ref, dq_ref, dk_ref, dv_ref, sq_ref, sk_ref, sv_ref, dg_ref,
             kpad, vpad, dkacc, dvacc, bias, dsacc, s_scr, dp_scr, p_scr, ds_scr):
        hp, i = pl.program_id(0), pl.program_id(1)

        @pl.when(i == 0)
        def _():
            kpad[0:PAD_ROWS, :] = jnp.zeros((PAD_ROWS, 128), BF16)
            vpad[0:PAD_ROWS, :] = jnp.zeros((PAD_ROWS, 128), BF16)
            kpad[PAD_ROWS:PAD_ROWS + t, :] = k_ref[...].astype(BF16)
            vpad[PAD_ROWS:PAD_ROWS + t, :] = v_ref[...].astype(BF16)
            dkacc[...] = jnp.zeros_like(dkacc)
            dvacc[...] = jnp.zeros_like(dvacc)
            dsacc[...] = jnp.zeros_like(dsacc)
            for hh in range(2):
                bias[hh] = _band_bias(g_ref[pl.ds(2 * hp + hh, 1), :])

        dq_sum = None
        for q0 in range(0, STEP_ROWS, Q_TILE):
            q_start = i * STEP_ROWS + q0
            win = pl.ds(pl.multiple_of(q_start, Q_TILE), WINDOW)
            dq = None
            for hh in range(2):
                mask = _head_mask(hh)
                qm = jnp.where(mask, q_ref[q0:q0 + Q_TILE, :] * (CHUNK ** -0.5), 0.0).astype(BF16)
                dom = jnp.where(mask, d_ref[q0:q0 + Q_TILE, :], 0.0).astype(BF16)
                slot = 2 * (q0 // Q_TILE) + hh
                s_scr[slot] = lax.dot_general(qm, kpad[win, :], NT_DIMS, preferred_element_type=F32)
                dp_scr[slot] = lax.dot_general(dom, vpad[win, :], NT_DIMS, preferred_element_type=F32)
                for r0 in range(0, Q_TILE, SOFTMAX_ROWS):
                    rows = slice(r0, r0 + SOFTMAX_ROWS)
                    p = _probs_block(s_scr.at[slot], bias, hh, rows, q_start)
                    dp = dp_scr[slot, rows, :]
                    ds = p * (dp - jnp.sum(p * dp, axis=-1, keepdims=True))
                    dsacc[hh, rows, :] += ds
                    ds_scr[slot, rows, :] = ds.astype(BF16)
                    p_scr[slot, rows, :] = p.astype(BF16)
                ds16 = ds_scr[slot]
                dqh = jnp.dot(ds16, kpad[win, :], preferred_element_type=F32) * (CHUNK ** -0.5)
                dq = jnp.where(mask, dqh, 0.0) if dq is None else jnp.where(mask, dqh, dq)
                dkacc[win, :] += lax.dot_general(ds16, qm, TN_DIMS, preferred_element_type=F32)
                dvacc[win, :] += lax.dot_general(p_scr[slot], dom, TN_DIMS, preferred_element_type=F32)
            dq_ref[q0:q0 + Q_TILE, :] = dq.astype(BF16)
            dq_sum = _colsum(dq) if dq_sum is None else dq_sum + _colsum(dq)

        @pl.when(i == 0)
        def _():
            sq_ref[...] = dq_sum

        @pl.when(i > 0)
        def _():
            sq_ref[...] += dq_sum

        @pl.when(i == n_i - 1)
        def _():
            dk = dkacc[PAD_ROWS:PAD_ROWS + t, :]
            dv = dvacc[PAD_ROWS:PAD_ROWS + t, :]
            dk_ref[...] = dk.astype(BF16)
            dv_ref[...] = dv.astype(BF16)
            sk_ref[...] = _colsum(dk)
            sv_ref[...] = _colsum(dv)
            rr = lax.broadcasted_iota(jnp.int32, (Q_TILE, Q_TILE), 0)
            cc = lax.broadcasted_iota(jnp.int32, (Q_TILE, Q_TILE), 1)
            rev = jnp.where(rr + cc == Q_TILE - 1, 1.0, 0.0).astype(BF16)
            for hh in range(2):
                acc = None
                for part in _split3(dsacc[hh]):
                    pr = jnp.dot(rev, part, preferred_element_type=F32)
                    acc = pr if acc is None else acc + pr
                wide = jnp.concatenate([acc, jnp.zeros((Q_TILE, TOEP - WINDOW), F32)], axis=1)
                dg_ref[pl.ds(2 * hp + hh, 1), :] = _colsum(pltpu.roll(wide, 0, 1, stride=1, stride_axis=0))

    col = lambda off: pl.BlockSpec((t, 128), lambda h, i: (0, off + h))
    tile = lambda: pl.BlockSpec((STEP_ROWS, 128), lambda h, i: (i, h))
    sums = lambda: pl.BlockSpec((1, 128), lambda h, i: (0, h))
    res = _call(
        body, grid=(4, n_i),
        in_specs=[tile(), col(4), col(8), tile(), pl.BlockSpec((N_HEADS, TOEP), lambda h, i: (0, 0))],
        out_specs=[tile(), col(0), col(0), sums(), sums(), sums(), pl.BlockSpec((N_HEADS, TOEP), lambda h, i: (0, 0))],
        out_shape=[_sds((t, 512), BF16)] * 3 + [_sds((1, 512), F32)] * 3 + [_sds((N_HEADS, TOEP), F32)],
        scratch_shapes=[pltpu.VMEM((PAD_ROWS + t, 128), BF16), pltpu.VMEM((PAD_ROWS + t, 128), BF16),
                        pltpu.VMEM((PAD_ROWS + t, 128), F32), pltpu.VMEM((PAD_ROWS + t, 128), F32),
                        pltpu.VMEM((2, Q_TILE, WINDOW), F32), pltpu.VMEM((2, Q_TILE, WINDOW), F32),
                        pltpu.VMEM((SCORE_SLOTS, Q_TILE, WINDOW), F32), pltpu.VMEM((SCORE_SLOTS, Q_TILE, WINDOW), F32),
                        pltpu.VMEM((SCORE_SLOTS, Q_TILE, WINDOW), BF16), pltpu.VMEM((SCORE_SLOTS, Q_TILE, WINDOW), BF16)],
        sem=("arbitrary", "arbitrary"), name=name, args=(z, z, z, datt, gen), carried=carried)
    return res[:7] if carried is None else (res[:7], res[7:])


def _adamw_math(w, g, m, v):
    m = ADAM_B1 * m + (1.0 - ADAM_B1) * g
    v = ADAM_B2 * v + (1.0 - ADAM_B2) * (g * g)
    m_hat = m / (1.0 - ADAM_B1 ** ADAM_STEP)
    v_hat = v / (1.0 - ADAM_B2 ** ADAM_STEP)
    delta = -ADAM_LR * (m_hat / (jnp.sqrt(v_hat) + ADAM_EPS) + ADAM_WD * w)
    return delta, m, v


def _adamw_many(items, name):
    n = len(items)

    def body(*refs):
        ins, outs = refs[:4 * n], refs[4 * n:]
        for k in range(n):
            w, g, m, v = (r[...] for r in ins[4 * k:4 * k + 4])
            outs[3 * k][...], outs[3 * k + 1][...], outs[3 * k + 2][...] = _adamw_math(w, g, m, v)

    flat = [a for item in items for a in item]
    res = pl.pallas_call(body, out_shape=[_sds(item[0].shape, F32) for item in items for _ in range(3)],
                         name=name)(*flat)
    return [tuple(res[3 * k:3 * k + 3]) for k in range(n)]


def _adamw(w, g, m, v, name, after):
    r, c = w.shape
    tm = next(cand for cand in (512, 352, 256, 128, 64, 32, 16, 8) if r % cand == 0)
    return _rowcall(lambda wv, gv, mv, vv, _: (gv,) + _adamw_math(wv, gv, mv, vv),
                    [(w, c, 0), (g, c, 0), (m, c, 0), (v, c, 0)], [after], [(r, c, F32, c)] * 4, [], name=name, tm=tm)


def _ada_fwd(c_all, w_shard, b_shard, name):
    n = w_shard.shape[1]
    tn = 512

    def body(c_ref, w_ref, b_ref, o_ref, a_ref):
        cv = c_ref[...]
        act = cv * _sigmoid(cv)
        a_ref[...] = act
        o_ref[...] = jnp.dot(act.astype(BF16), w_ref[...].astype(BF16), preferred_element_type=F32) + b_ref[...]

    return pl.pallas_call(
        body, grid=(n // tn,),
        in_specs=[pl.BlockSpec((8, D), lambda j: (0, 0)), pl.BlockSpec((D, tn), lambda j: (0, j)),
                  pl.BlockSpec((1, tn), lambda j: (0, j))],
        out_specs=[pl.BlockSpec((8, tn), lambda j: (0, j)), pl.BlockSpec((8, D), lambda j: (0, 0))],
        out_shape=[_sds((8, n), F32), _sds((8, D), F32)],
        compiler_params=_params(("arbitrary",)), name=name,
    )(c_all, w_shard, b_shard)


def _ada_bwd_adamw(act_t, dmod_shard, w, m, v, name):
    r, c = w.shape
    tm = 2 * ROW_TILE

    def body(a_ref, d_ref, w_ref, m_ref, v_ref, g_ref, dl_ref, nm_ref, nv_ref):
        g = jnp.dot(a_ref[...], d_ref[...], precision=lax.Precision.HIGHEST, preferred_element_type=F32)
        g_ref[...] = g
        dl_ref[...], nm_ref[...], nv_ref[...] = _adamw_math(w_ref[...], g, m_ref[...], v_ref[...])

    blk = pl.BlockSpec((tm, c), lambda i: (i, 0))
    return pl.pallas_call(
        body, grid=(r // tm,),
        in_specs=[pl.BlockSpec((tm, 8), lambda i: (i, 0)), pl.BlockSpec((8, c), lambda i: (0, 0)), blk, blk, blk],
        out_specs=[blk] * 4, out_shape=[_sds((r, c), F32)] * 4,
        compiler_params=_params(("arbitrary",)), name=name,
    )(act_t, dmod_shard, w, m, v)


def _place():
    return lax.axis_index("x"), lax.axis_index("y"), lax.axis_index("c")


def _flip(v, bit):
    return 1 - v if bit else v


VMEM_SPEC = pl.BlockSpec(memory_space=pltpu.VMEM)


def _allgather8(v, name):
    r, c = v.shape

    def body(v_ref, g_ref, tot_ref, send_sems, recv_sems, local_sem):
        x, y, cc = _place()
        sibling = (x, y, 1 - cc)
        chips = [(_flip(x, k & 2), _flip(y, k & 1)) for k in (1, 2, 3)]

        def block(px, py, pc):
            return g_ref.at[4 * px + 2 * py + pc]

        def copy(k, place, to, src=None):
            slot = block(*place)
            return pltpu.make_async_remote_copy(src_ref=slot if src is None else src, dst_ref=slot,
                                                send_sem=send_sems.at[k], recv_sem=recv_sems.at[k],
                                                device_id=to, device_id_type=MESH)

        mine = pltpu.make_async_copy(v_ref, block(x, y, cc), local_sem)
        mine.start()
        first = [copy(0, (x, y, cc), sibling, src=v_ref)]
        first += [copy(1 + j, (x, y, cc), (px, py, cc), src=v_ref) for j, (px, py) in enumerate(chips)]
        for cp in first:
            cp.start()
        passed = [copy(4 + j, (px, py, cc), sibling) for j, (px, py) in enumerate(chips)]
        for j, (px, py) in enumerate(chips):
            copy(1 + j, (px, py, cc), (x, y, cc)).wait_recv()
            passed[j].start()
        copy(0, sibling, (x, y, cc)).wait_recv()
        for j, (px, py) in enumerate(chips):
            copy(4 + j, (px, py, 1 - cc), (x, y, cc)).wait_recv()
        for cp in first + passed:
            cp.wait_send()
        mine.wait()
        tot = g_ref[0]
        for d in range(1, 8):
            tot = tot + g_ref[d]
        tot_ref[...] = tot

    return pl.pallas_call(
        body, in_specs=[VMEM_SPEC], out_specs=[VMEM_SPEC, VMEM_SPEC],
        out_shape=[_sds((8, r, c), F32), _sds((r, c), F32)],
        scratch_shapes=[pltpu.SemaphoreType.DMA((7,)), pltpu.SemaphoreType.DMA((7,)), pltpu.SemaphoreType.DMA],
        compiler_params=pltpu.CompilerParams(vmem_limit_bytes=VMEM_LIMIT), name=name,
    )(v)


def _slot(px, py, swapped):
    return 2 * py + px if swapped else 2 * px + py


def _gather_shards(arrs, swapped, name):
    n = len(arrs)

    def body(*refs):
        ins, outs = refs[:n], refs[n:2 * n]
        send1, recv1, send2, recv2, local_sems = refs[2 * n:]
        x, y, c = _place()
        sibling = (x, y, 1 - c)
        chips = [(_flip(x, k & 2), _flip(y, k & 1)) for k in (1, 2, 3)]
        local_copies, sends = [], []
        for a in range(n):
            h = outs[a].shape[1] // 2
            mine = pl.ds(pl.multiple_of(c * h, 8), h)
            own = _slot(x, y, swapped[a])
            lc = pltpu.make_async_copy(ins[a], outs[a].at[own], local_sems.at[a])
            lc.start()
            local_copies.append(lc)
            for j, (px, py) in enumerate(chips):
                cp = pltpu.make_async_remote_copy(
                    src_ref=ins[a].at[mine], dst_ref=outs[a].at[own, mine], send_sem=send1.at[3 * a + j],
                    recv_sem=recv1.at[3 * a + j], device_id=(px, py, c), device_id_type=MESH)
                cp.start()
                sends.append(cp)
        for a in range(n):
            h = outs[a].shape[1] // 2
            mine = pl.ds(pl.multiple_of(c * h, 8), h)
            for j, (px, py) in enumerate(chips):
                piece = outs[a].at[_slot(px, py, swapped[a]), mine]
                pltpu.make_async_remote_copy(
                    src_ref=piece, dst_ref=piece, send_sem=send1.at[3 * a + j], recv_sem=recv1.at[3 * a + j],
                    device_id=(px, py, c), device_id_type=MESH).wait_recv()
                fwd = pltpu.make_async_remote_copy(
                    src_ref=piece, dst_ref=piece, send_sem=send2.at[3 * a + j], recv_sem=recv2.at[3 * a + j],
                    device_id=sibling, device_id_type=MESH)
                fwd.start()
                sends.append(fwd)
        for a in range(n):
            h = outs[a].shape[1] // 2
            other = pl.ds(pl.multiple_of((1 - c) * h, 8), h)
            for j, (px, py) in enumerate(chips):
                piece = outs[a].at[_slot(px, py, swapped[a]), other]
                pltpu.make_async_remote_copy(
                    src_ref=piece, dst_ref=piece, send_sem=send2.at[3 * a + j], recv_sem=recv2.at[3 * a + j],
                    device_id=sibling, device_id_type=MESH).wait_recv()
        for cp in sends:
            cp.wait_send()
        for lc in local_copies:
            lc.wait()

    dma = lambda k: pltpu.SemaphoreType.DMA((k,))
    return pl.pallas_call(
        body, in_specs=[ANY] * n, out_specs=[ANY] * n,
        out_shape=[_sds((4,) + a.shape, a.dtype) for a in arrs],
        scratch_shapes=[dma(3 * n), dma(3 * n), dma(3 * n), dma(3 * n), dma(n)], name=name,
    )(*arrs)


def _carry_pair_exchange(grads):
    n = len(grads)

    def copies(ins, outs, send_sems, recv_sems):
        x, y, c = _place()
        cps = []
        for a in range(n):
            h = ins[a].shape[1] // 2
            theirs = pl.ds(pl.multiple_of((1 - c) * h, 8), h)
            cps.append(pltpu.make_async_remote_copy(
                src_ref=ins[a].at[:, theirs, :], dst_ref=outs[a], send_sem=send_sems.at[a], recv_sem=recv_sems.at[a],
                device_id=(x, y, 1 - c), device_id_type=MESH))
        return cps

    def start(*refs):
        for cp in copies(*refs):
            cp.start()

    def finish(*refs):
        for cp in copies(*refs):
            cp.wait()

    return _Carried(grads, [_sds((4, g.shape[1] // 2, g.shape[2]), F32) for g in grads], {}, n, start, finish)


def _pair_sum(grad, recv, core, name):
    _, r, c = grad.shape
    h = r // 2

    def body(core_ref, g_ref, r_ref, o_ref):
        o_ref[...] = (g_ref[...] + r_ref[...]).astype(BF16)

    return pl.pallas_call(
        body,
        grid_spec=pltpu.PrefetchScalarGridSpec(
            num_scalar_prefetch=1, grid=(4,),
            in_specs=[pl.BlockSpec((None, h, c), lambda s, core_ref: (s, core_ref[0], 0)),
                      pl.BlockSpec((None, h, c), lambda s, core_ref: (s, 0, 0))],
            out_specs=pl.BlockSpec((None, h, c), lambda s, core_ref: (s, 0, 0))),
        out_shape=_sds((4, h, c), BF16), compiler_params=_params(("arbitrary",)), name=name,
    )(core, grad, recv)


def _carry_chip_exchange(parts, swapped):
    n = len(parts)

    def copies(ins, outs, send_sems, recv_sems):
        x, y, c = _place()
        chips = [(_flip(x, k & 2), _flip(y, k & 1)) for k in (1, 2, 3)]
        cps = []
        for a in range(n):
            for j, (px, py) in enumerate(chips):
                cps.append(pltpu.make_async_remote_copy(
                    src_ref=ins[a].at[_slot(px, py, swapped[a])], dst_ref=outs[a].at[j],
                    send_sem=send_sems.at[3 * a + j], recv_sem=recv_sems.at[3 * a + j],
                    device_id=(px, py, c), device_id_type=MESH))
        return cps

    def start(*refs):
        for cp in copies(*refs):
            cp.start()

    def finish(*refs):
        for cp in copies(*refs):
            cp.wait()

    return _Carried(parts, [_sds((3,) + p.shape[1:], BF16) for p in parts], {}, 3 * n, start, finish)


def _chip_sum(part, recv, slot_core, name):
    _, h, c = part.shape

    def body(sc_ref, p_ref, r_ref, o_ref):
        acc = p_ref[...].astype(F32)
        for j in range(3):
            acc = acc + r_ref[j].astype(F32)
        o_ref[...] = acc

    return pl.pallas_call(
        body,
        grid_spec=pltpu.PrefetchScalarGridSpec(
            num_scalar_prefetch=1, grid=(1,),
            in_specs=[pl.BlockSpec((None, h, c), lambda q, sc_ref: (sc_ref[0], 0, 0)),
                      pl.BlockSpec((3, h, c), lambda q, sc_ref: (0, 0, 0))],
            out_specs=pl.BlockSpec((h, c), lambda q, sc_ref: (sc_ref[1], 0))),
        out_shape=_sds((2 * h, c), F32), compiler_params=_params(("arbitrary",)), name=name,
    )(slot_core, part, recv)


def _carry_pair_share(shards):
    n = len(shards)

    def copies(outs, send_sems, recv_sems, mine):
        x, y, c = _place()
        cps = []
        for a in range(n):
            h = outs[a].shape[0] // 2
            half = outs[a].at[pl.ds(pl.multiple_of((c if mine else 1 - c) * h, 8), h)]
            cps.append(pltpu.make_async_remote_copy(
                src_ref=half, dst_ref=half, send_sem=send_sems.at[a], recv_sem=recv_sems.at[a],
                device_id=(x, y, 1 - c), device_id_type=MESH))
        return cps

    def start(ins, outs, send_sems, recv_sems):
        for cp in copies(outs, send_sems, recv_sems, True):
            cp.start()

    def finish(ins, outs, send_sems, recv_sems):
        for cp in copies(outs, send_sems, recv_sems, False):
            cp.wait_recv()
        for cp in copies(outs, send_sems, recv_sems, True):
            cp.wait_send()

    return _Carried(shards, [_sds(s.shape, F32) for s in shards], {a: a for a in range(n)}, n, start, finish)


def _carry_gather_ici(bufs, swapped):
    n = len(bufs)

    def copies(outs, send_sems, recv_sems, sending):
        x, y, c = _place()
        cps = []
        for a in range(n):
            h = outs[a].shape[1] // 2
            mine = pl.ds(pl.multiple_of(c * h, 8), h)
            for j, k in enumerate((1, 2, 3)):
                px, py = _flip(x, k & 2), _flip(y, k & 1)
                slot = _slot(x, y, swapped[a]) if sending else _slot(px, py, swapped[a])
                piece = outs[a].at[slot, mine]
                cps.append(pltpu.make_async_remote_copy(
                    src_ref=piece, dst_ref=piece, send_sem=send_sems.at[3 * a + j], recv_sem=recv_sems.at[3 * a + j],
                    device_id=(px, py, c), device_id_type=MESH))
        return cps

    def start(ins, outs, send_sems, recv_sems):
        for cp in copies(outs, send_sems, recv_sems, True):
            cp.start()

    def finish(ins, outs, send_sems, recv_sems):
        for cp in copies(outs, send_sems, recv_sems, False):
            cp.wait_recv()
        for cp in copies(outs, send_sems, recv_sems, True):
            cp.wait_send()

    return _Carried(bufs, [_sds(b.shape, b.dtype) for b in bufs], {a: a for a in range(n)}, 3 * n, start, finish)


HBM_SPEC = pl.BlockSpec(memory_space=pltpu.HBM)
SEM_SPEC = pl.BlockSpec(memory_space=pltpu.SEMAPHORE)
SIDE_EFFECT = pltpu.SideEffectType.DATAFLOW_SIDE_EFFECTING


def _ici_pieces(buf, send_sems, recv_sems, swapped, sending):
    x, y, c = _place()
    h = buf.shape[1] // 2
    mine = pl.ds(pl.multiple_of(c * h, 8), h)
    cps = []
    for j, k in enumerate((1, 2, 3)):
        px, py = _flip(x, k & 2), _flip(y, k & 1)
        piece = buf.at[_slot(x, y, swapped) if sending else _slot(px, py, swapped), mine]
        cps.append(pltpu.make_async_remote_copy(src_ref=piece, dst_ref=piece, send_sem=send_sems.at[j],
                                                recv_sem=recv_sems.at[j], device_id=(px, py, c), device_id_type=MESH))
    return cps


def _gather_ici_start(buf, after, swapped, name):
    def body(buf_ref, after_ref, send_sems, recv_sems, thru, token):
        for cp in _ici_pieces(thru, send_sems, recv_sems, swapped, True):
            cp.start()
        token[...] = jnp.zeros_like(token)

    return pl.pallas_call(
        body, name=name,
        out_shape=(pltpu.SemaphoreType.DMA((3,)), pltpu.SemaphoreType.DMA((3,)), pltpu.HBM(buf.shape, buf.dtype),
                   jax.ShapeDtypeStruct((8, 128), F32)),
        in_specs=(HBM_SPEC, ANY), out_specs=(SEM_SPEC, SEM_SPEC, HBM_SPEC, VMEM_SPEC), input_output_aliases={0: 2},
        compiler_params=pltpu.CompilerParams(has_side_effects=SIDE_EFFECT),
    )(pltpu.with_memory_space_constraint(buf, pltpu.HBM), after)


def _gather_ici_wait(send_sems, recv_sems, thru, after, swapped, name):
    def body(thru_ref, send_sems, recv_sems, after_ref, out_ref):
        for cp in _ici_pieces(out_ref, send_sems, recv_sems, swapped, True):
            cp.wait_send()
        for cp in _ici_pieces(out_ref, send_sems, recv_sems, swapped, False):
            cp.wait_recv()

    return pl.pallas_call(
        body, name=name, out_shape=pltpu.HBM(thru.shape, thru.dtype),
        in_specs=(HBM_SPEC, SEM_SPEC, SEM_SPEC, ANY), out_specs=HBM_SPEC, input_output_aliases={0: 0},
        compiler_params=pltpu.CompilerParams(has_side_effects=SIDE_EFFECT),
    )(thru, send_sems, recv_sems, after)


def _all8_copies(buf, send_sems, recv_sems, sending):
    x, y, c = _place()
    cps = []
    for k in range(1, 8):
        px, py, pc = _flip(x, k & 4), _flip(y, k & 2), _flip(c, k & 1)
        slot = buf.at[4 * x + 2 * y + c] if sending else buf.at[4 * px + 2 * py + pc]
        cps.append(pltpu.make_async_remote_copy(src_ref=slot, dst_ref=slot, send_sem=send_sems.at[k - 1],
                                                recv_sem=recv_sems.at[k - 1], device_id=(px, py, pc), device_id_type=MESH))
    return cps


def _all8_start(buf, name):
    def body(buf_ref, send_sems, recv_sems, thru, token):
        for cp in _all8_copies(thru, send_sems, recv_sems, True):
            cp.start()
        token[...] = jnp.zeros_like(token)

    return pl.pallas_call(
        body, name=name,
        out_shape=(pltpu.SemaphoreType.DMA((7,)), pltpu.SemaphoreType.DMA((7,)), pltpu.HBM(buf.shape, buf.dtype),
                   jax.ShapeDtypeStruct((8, 128), F32)),
        in_specs=(HBM_SPEC,), out_specs=(SEM_SPEC, SEM_SPEC, HBM_SPEC, VMEM_SPEC), input_output_aliases={0: 2},
        compiler_params=pltpu.CompilerParams(has_side_effects=SIDE_EFFECT),
    )(pltpu.with_memory_space_constraint(buf, pltpu.HBM))


def _all8_wait(send_sems, recv_sems, thru, after, name):
    def body(thru_ref, send_sems, recv_sems, after_ref, out_ref):
        for cp in _all8_copies(out_ref, send_sems, recv_sems, True):
            cp.wait_send()
        for cp in _all8_copies(out_ref, send_sems, recv_sems, False):
            cp.wait_recv()

    return pl.pallas_call(
        body, name=name, out_shape=pltpu.HBM(thru.shape, thru.dtype),
        in_specs=(HBM_SPEC, SEM_SPEC, SEM_SPEC, ANY), out_specs=HBM_SPEC, input_output_aliases={0: 0},
        compiler_params=pltpu.CompilerParams(has_side_effects=SIDE_EFFECT),
    )(thru, send_sems, recv_sems, after)


def _sum8(g, name):
    def body(g_ref, o_ref):
        tot = g_ref[0]
        for d in range(1, 8):
            tot = tot + g_ref[d]
        o_ref[...] = tot

    return pl.pallas_call(body, out_shape=_sds(g.shape[1:], F32), name=name)(g)


def _carry_gather_forward(bufs, swapped):
    n = len(bufs)

    def copies(outs, send_sems, recv_sems, sending):
        x, y, c = _place()
        cps = []
        for a in range(n):
            h = outs[a].shape[1] // 2
            rows = pl.ds(pl.multiple_of((c if sending else 1 - c) * h, 8), h)
            for j, k in enumerate((1, 2, 3)):
                piece = outs[a].at[_slot(_flip(x, k & 2), _flip(y, k & 1), swapped[a]), rows]
                cps.append(pltpu.make_async_remote_copy(
                    src_ref=piece, dst_ref=piece, send_sem=send_sems.at[3 * a + j], recv_sem=recv_sems.at[3 * a + j],
                    device_id=(x, y, 1 - c), device_id_type=MESH))
        return cps

    def start(ins, outs, send_sems, recv_sems):
        for cp in copies(outs, send_sems, recv_sems, True):
            cp.start()

    def finish(ins, outs, send_sems, recv_sems):
        for cp in copies(outs, send_sems, recv_sems, False):
            cp.wait_recv()
        for cp in copies(outs, send_sems, recv_sems, True):
            cp.wait_send()

    return _Carried(bufs, [_sds(b.shape, b.dtype) for b in bufs], {a: a for a in range(n)}, 3 * n, start, finish)


def _pack(arrs, rows_multiple=8):
    parts, offs, row = [], [], 0
    for a in arrs:
        flat = a.reshape(-1)
        nrow = -(-flat.shape[0] // D)
        parts.append(jnp.pad(flat, (0, nrow * D - flat.shape[0])))
        offs.append(row)
        row += nrow
    total = -(-row // rows_multiple) * rows_multiple
    if total > row:
        parts.append(jnp.zeros(((total - row) * D,), F32))
    return jnp.concatenate(parts).reshape(total, D), offs


def _unpack(packed, offs, shapes):
    out = []
    for off, shp in zip(offs, shapes):
        size = int(np.prod(shp))
        nrow = -(-size // D)
        out.append(packed[off:off + nrow].reshape(-1)[:size].reshape(shp))
    return out


def _to_bf16_slot(w, slot, name, after=None):
    r, c = w.shape
    tm = next(cand for cand in (512, 352, 256, 128, 64, 32, 16) if r % cand == 0)

    def body(slot_ref, w_ref, *rest):
        rest[-1][...] = w_ref[...].astype(BF16)

    in_specs = [pl.BlockSpec((tm, c), lambda i, slot_ref: (i, 0))]
    if after is not None:
        in_specs.append(pl.BlockSpec((8, 128), lambda i, slot_ref: (0, 0)))
    return pl.pallas_call(
        body,
        grid_spec=pltpu.PrefetchScalarGridSpec(
            num_scalar_prefetch=1, grid=(r // tm,), in_specs=in_specs,
            out_specs=pl.BlockSpec((None, tm, c), lambda i, slot_ref: (slot_ref[0], i, 0))),
        out_shape=_sds((4, r, c), BF16), compiler_params=_params(("arbitrary",)), name=name,
    )(slot, w, *([] if after is None else [after]))


def _unshard_cols(g):
    s, k, n = g.shape
    return jnp.transpose(g, (1, 0, 2)).reshape(k, s * n)


def _ff_swap(v):
    b = FF_BLOCK
    return jnp.concatenate([v[..., 0:b], v[..., 2 * b:3 * b], v[..., b:2 * b], v[..., 3 * b:4 * b]], axis=-1)


LATE = ("attn_o", "conv_o", "mix_o", "up", "down")
EARLY_GRADS = ("down", "up", "mix_o", "attn_o", "conv_o")


def _weight_views(bufs):
    return {"up": bufs["up"], "attn_o": _unshard_cols(bufs["attn_o"]), "conv_o": _unshard_cols(bufs["conv_o"]),
            "mix_o": bufs["mix_o"].reshape(D, D), "down": bufs["down"].reshape(D_FF, D)}


def _pair_sums(names, grads, recv, dist):
    return [_pair_sum(g, r, dist["core"], "pair_sum_" + n) for n, g, r in zip(names, grads, recv)]


def _reduce_halves(names, parts, from_chips, dist):
    return [_chip_sum(p, r, jnp.concatenate([dist["slots"][SWAPPED[n]], dist["core"]]), "chip_sum_" + n)
            for n, p, r in zip(names, parts, from_chips)]


FUSED_TILE = 256
WIDE_TILE = 512


def _gates(z):
    return [(z, 512, 5), (z, 512, 6), (z, 512, 7), (z, 512, 8)]


def _mix_out(a, cb, z, x, w_mix_o, g_post, gt, g_pre2, sc2, sh2, name):
    def lhs(av, cv, ga0, ga1, gb0, gb1):
        ga, gb = jnp.concatenate([ga0, ga1], axis=1), jnp.concatenate([gb0, gb1], axis=1)
        return _sigmoid(ga) * av + _sigmoid(gb) * cv

    def fn(ym, y, xv, gv, gtv, g2v, scv, shv):
        yn, _ = _rms_stats(ym)
        x1 = xv + gtv * (yn * gv)
        xn, _ = _rms_stats(x1)
        return ym, y, x1, (xn * g2v) * (1.0 + scv) + shv

    return _matmul_rows(w_mix_o, form="nn", tm=min(WIDE_TILE, x.shape[0]), tk=D, fn=fn, a_rows=[(a, D, 0), (cb, D, 0)] + _gates(z),
                        a_fn=lhs, rows=[(x, D, 0)], consts=[g_post, gt, g_pre2, sc2, sh2],
                        row_outs=[(F32, D), (BF16, D), (F32, D), (BF16, D)], acc_outs=[], name=name)


def _down_tail(act, w_down, x1, target, g, gt, name):
    def fn(yv, xv, tv, gv, gtv):
        yn, r = _rms_stats(yv)
        e = xv + gtv * (yn * gv) - tv
        dx2 = e * (1.0 / D)
        dyn = dx2 * gtv
        return (dx2, _rms_bwd(dyn * gv, yn, r), _colsum(e * e) * (0.5 / D), _colsum(dyn * yn),
                _colsum(dx2 * (yn * gv)))

    return _matmul_rows(w_down, form="nn", a=act, tm=min(WIDE_TILE, x1.shape[0]), tk=D_FF, fn=fn,
                        rows=[(x1, D, 0), (target, D, 0)], consts=[g, gt], row_outs=[(F32, D), (BF16, D)],
                        acc_outs=[(1, D)] * 3, name=name)


def _up_dx_tail(dup, w_up, x1, dx2, ym, g_pre2, sc2, g_post, gt, name):
    def fn(dh, xv, dov, ymv, g2v, scv, gv, gtv):
        xn, r = _rms_stats(xv)
        dyn = dh * (1.0 + scv)
        dx1 = dov + _rms_bwd(dyn * g2v, xn, r)
        yn, r2 = _rms_stats(ymv)
        dynm = dx1 * gtv
        return (dx1, _rms_bwd(dynm * gv, yn, r2), _colsum(dh), _colsum(dh * (xn * g2v)), _colsum(dyn * xn),
                _colsum(dynm * yn), _colsum(dx1 * (yn * gv)))

    return _matmul_rows(w_up, form="nt", a=dup, tm=min(FUSED_TILE, x1.shape[0]), tk=2 * D_FF, fn=fn,
                        rows=[(x1, D, 0), (dx2, D, 0), (ym, D, 0)], consts=[g_pre2, sc2, g_post, gt],
                        row_outs=[(F32, D), (BF16, D)], acc_outs=[(1, D)] * 5, name=name)


def _mix_dx_gates(dym, w_mix_o, a, cb, z, name):
    def fn(dy, av, cv, ga0, ga1, gb0, gb1):
        sa = _sigmoid(jnp.concatenate([ga0, ga1], axis=1))
        sb = _sigmoid(jnp.concatenate([gb0, gb1], axis=1))
        dcb = dy * sb
        dga = dy * av * (sa * (1.0 - sa))
        dgb = dy * cv * (sb * (1.0 - sb))
        return dy * sa, dcb, dga, dgb, _colsum(dcb), _colsum(dga), _colsum(dgb)

    return _matmul_rows(w_mix_o, form="nt", a=dym, tm=min(WIDE_TILE, a.shape[0]), tk=D, fn=fn,
                        rows=[(a, D, 0), (cb, D, 0)] + _gates(z), consts=[], row_outs=[(BF16, D)] * 4,
                        acc_outs=[(1, D)] * 3, name=name)


def _local_step(x, target, mod, w_in, late, small, dist=None):
    sh_m, sc_m, gt_m, sh_f, sc_f, gt_f = mod
    t = x.shape[0]
    tmm = min(1024, t)
    late_swapped = [SWAPPED[n] for n in LATE]

    h1 = _pre_norm(x, small["g_pre_mix"], sc_m, sh_m, "pre_norm_mix")
    if callable(w_in):
        w_in = w_in(h1)
    z = _matmul(h1, w_in, form="nn", out_dtype=F32, tm=min(FUSED_TILE, t), tn=D_IN, tk=D, bias=small["b_in"], name="mm_in")
    conv = (z, small["w_dw_conv"], small["b_dw_conv"], small["g_conv_ln"], small["b_conv_ln"], "conv_branch")
    if dist is None:
        att = _attention(z, small["gen"], "attention")
        u1, u3 = _conv_branch(*conv)
        bufs = dict(late)
    else:
        mid = [n for n in LATE if n != "down"]
        mid_swapped = [SWAPPED[n] for n in mid]
        att, landed = _attention(z, small["gen"], "attention",
                                 carried=_carry_gather_ici([late[n] for n in mid], mid_swapped))
        (u1, u3), gathered = _conv_branch(*conv, carried=_carry_gather_forward(landed, mid_swapped))
        bufs = dict(zip(mid, gathered))
        bufs["down"] = late["down"]
    w = _weight_views(bufs)
    w["in"] = w_in
    a = _matmul(att, w["attn_o"], form="nn", out_dtype=F32, tm=tmm, tn=512, tk=512, name="mm_attn_o")
    cb = _matmul(u3, w["conv_o"], form="nn", out_dtype=F32, tm=tmm, tn=512, tk=512, bias=small["b_conv_o"], name="mm_conv_o")
    ym, y, x1, h2 = _mix_out(a, cb, z, x, w["mix_o"], small["g_post_mix"], gt_m, small["g_pre_ffn"], sc_f, sh_f, "mix_out")
    mm_up = dict(form="nn", out_dtype=F32, tm=min(FUSED_TILE, t), tn=2 * D_FF, tk=D, name="mm_up")
    ffn_act = (small["w_dw_ffn"], small["b_dw_ffn"], "ffn_act")
    if dist is None:
        up = _matmul(h2, w["up"], **mm_up)
        act = _ffn_act(up, *ffn_act)
    else:
        up, landed = _matmul(h2, w["up"], carried=_carry_gather_ici([late["down"]], [False]), **mm_up)
        act, down = _ffn_act(up, *ffn_act, carried=_carry_gather_forward(landed, [False]))
        w["down"] = down[0].reshape(D_FF, D)

    dx2, dyf, loss_cols, d_g_post_ffn, d_gt_f = _down_tail(act, w["down"], x1, target, small["g_post_ffn"], gt_f, "down_tail")
    dact = _matmul(dyf, w["down"], form="nt", out_dtype=F32, tm=tmm, tn=D_FF, tk=D, name="mm_down_dx")
    g_down = _matmul(act, dyf, form="tn", out_dtype=F32, tm=FF_BLOCK, tn=512, tk=t, name="mm_down_dw")
    dup, d_w_dw_ffn, d_b_dw_ffn = _ffn_act_bwd(dact, up, small["w_dw_ffn"], small["b_dw_ffn"], "ffn_act_bwd")
    dx1, dym, d_sh_f, d_sc_f, d_g_pre_ffn, d_g_post_mix, d_gt_m = _up_dx_tail(
        dup, w["up"], x1, dx2, ym, small["g_pre_ffn"], sc_f, small["g_post_mix"], gt_m, "up_dx_tail")
    g_up = _matmul(h2, dup, form="tn", out_dtype=F32, tm=512, tn=FF_BLOCK, tk=t, out_sharded=True, name="mm_up_dw")
    da, dcb, dgate_a, dgate_b, d_b_conv_o, sga, sgb = _mix_dx_gates(dym, w["mix_o"], a, cb, z, "mix_dx_gates")
    g_mix_o = _matmul(y, dym, form="tn", out_dtype=F32, tm=D, tn=512, tk=t, name="mm_mix_o_dw")
    datt = _matmul(da, w["attn_o"], form="nt", out_dtype=F32, tm=tmm, tn=512, tk=D, name="mm_attn_o_dx")
    g_attn_o = _matmul(att, da, form="tn", out_dtype=F32, tm=512, tn=256, tk=t, out_sharded=True, name="mm_attn_o_dw")
    du3 = _matmul(dcb, w["conv_o"], form="nt", out_dtype=F32, tm=tmm, tn=512, tk=D, name="mm_conv_o_dx")
    g_conv_o = _matmul(u3, dcb, form="tn", out_dtype=F32, tm=512, tn=256, tk=t, out_sharded=True, name="mm_conv_o_dw")
    big = {"attn_o": g_attn_o, "conv_o": g_conv_o, "mix_o": g_mix_o.reshape(4, 256, D),
           "up": g_up, "down": g_down.reshape(4, D_FF // 4, D)}
    conv_bwd = (du3, u1, z, small["w_dw_conv"], small["g_conv_ln"], small["b_conv_ln"], "conv_branch_bwd")
    in_dw = dict(form="tn", out_dtype=F32, tm=512, tn=IN_SHARD, tk=t, out_sharded=True, name="mm_in_dw")
    in_dx = dict(form="nt", out_dtype=F32, tm=min(WIDE_TILE, t), tn=D, tk=D_IN, name="mm_in_dx")
    if dist is None:
        dglu, d_w_dw_conv, d_b_dw_conv, d_g_conv_ln, d_b_conv_ln, sglu = _conv_branch_bwd(*conv_bwd)
        dq, dk, dv, sq, sk, sv, dgen = _attention_bwd(z, datt, small["gen"], "attention_bwd")
        dz = jnp.concatenate([dq, dk, dv, dglu, dgate_a, dgate_b], axis=1)
        big["in"] = _matmul(h1, dz, **in_dw)
        dh1 = _matmul(dz, w_in, **in_dx)
    else:
        early = [big[n] for n in EARLY_GRADS]
        (dglu, d_w_dw_conv, d_b_dw_conv, d_g_conv_ln, d_b_conv_ln, sglu), recv = _conv_branch_bwd(
            *conv_bwd, carried=_carry_pair_exchange(early))
        parts = _pair_sums(EARLY_GRADS, early, recv, dist)
        (dq, dk, dv, sq, sk, sv, dgen), from_chips = _attention_bwd(
            z, datt, small["gen"], "attention_bwd",
            carried=_carry_chip_exchange(parts, [SWAPPED[n] for n in EARLY_GRADS]))
        halves = _reduce_halves(EARLY_GRADS, parts, from_chips, dist)
        dz = jnp.concatenate([dq, dk, dv, dglu, dgate_a, dgate_b], axis=1)
        g_in, shards = _matmul(h1, dz, carried=_carry_pair_share(halves), **in_dw)
        big = dict(zip(EARLY_GRADS, shards))
        recv_in = _run_carried(_carry_pair_exchange([g_in]), "pair_exchange_in")
        part_in = _pair_sums(("in",), [g_in], recv_in, dist)
        dh1, from_chips_in = _matmul(dz, w_in, carried=_carry_chip_exchange(part_in, [False]), **in_dx)
        half_in = _reduce_halves(("in",), part_in, from_chips_in, dist)
        big["in"] = _run_carried(_carry_pair_share(half_in), "pair_share_in")[0]
    d_b_in = jnp.concatenate([sq, sk, sv, sglu, sga, sgb], axis=1)
    grad_x, d_sh_m, d_sc_m, d_g_pre_mix = _pre_norm_bwd(dh1, x, dx1, small["g_pre_mix"], sc_m, "pre_norm_mix_bwd")

    dmod = [d_sh_m, d_sc_m, d_gt_m, d_sh_f, d_sc_f, d_gt_f]
    sm = {"g_pre_mix": d_g_pre_mix, "g_post_mix": d_g_post_mix, "b_in": d_b_in, "gen": dgen,
          "w_dw_conv": d_w_dw_conv, "b_dw_conv": d_b_dw_conv, "g_conv_ln": d_g_conv_ln, "b_conv_ln": d_b_conv_ln,
          "b_conv_o": d_b_conv_o, "g_pre_ffn": d_g_pre_ffn, "g_post_ffn": d_g_post_ffn,
          "w_dw_ffn": d_w_dw_ffn, "b_dw_ffn": d_b_dw_ffn}
    return loss_cols, grad_x, dmod, big, sm


BIG = ("in", "attn_o", "conv_o", "mix_o", "up", "down")
SWAPPED = {"in": False, "attn_o": False, "conv_o": False, "mix_o": False, "up": True, "down": False}
SMALL_ORDER = ("b_ada", "g_pre_mix", "g_post_mix", "b_in", "rel_bias", "b_dw_conv", "g_conv_ln", "b_conv_ln",
               "b_conv_o", "g_pre_ffn", "g_post_ffn", "b_dw_ffn", "w_dw_conv", "w_dw_ffn")


def kernel(x, c, w_ada, b_ada, g_pre_mix, g_post_mix, w_in, b_in, rel_bias, w_attn_o, w_dw_conv, b_dw_conv, g_conv_ln, b_conv_ln, w_conv_o, b_conv_o, w_mix_o, g_pre_ffn, g_post_ffn, w_up, w_dw_ffn, b_dw_ffn, w_down, loss_target, m_w_ada, m_b_ada, m_g_pre_mix, m_g_post_mix, m_w_in, m_b_in, m_rel_bias, m_w_attn_o, m_w_dw_conv, m_b_dw_conv, m_g_conv_ln, m_b_conv_ln, m_w_conv_o, m_b_conv_o, m_w_mix_o, m_g_pre_ffn, m_g_post_ffn, m_w_up, m_w_dw_ffn, m_b_dw_ffn, m_w_down, v_w_ada, v_b_ada, v_g_pre_mix, v_g_post_mix, v_w_in, v_b_in, v_rel_bias, v_w_attn_o, v_w_dw_conv, v_b_dw_conv, v_g_conv_ln, v_b_conv_ln, v_w_conv_o, v_b_conv_o, v_w_mix_o, v_g_pre_ffn, v_g_post_ffn, v_w_up, v_w_dw_ffn, v_b_dw_ffn, v_w_down):
    given = dict(locals())
    ax, ay, ac = lax.axis_index("x"), lax.axis_index("y"), lax.axis_index("c")
    shard = 2 * ax + ay
    me = 4 * ax + 2 * ay + ac
    xs, target = x[0], loss_target[0]

    slots = {sw: _slot(ax, ay, sw).astype(jnp.int32).reshape(1) for sw in (False, True)}
    own = {"in": _to_bf16_slot(w_in[0], slots[False], "cast_in")}

    c_pad = jnp.pad(c, ((0, 7), (0, 0)))
    c_g, _ = _allgather8(c_pad, "gather_c")
    c_all = c_g[:, 0, :]
    b_ada_shard = lax.dynamic_slice(b_ada, (0, shard * ADA_SHARD), (1, ADA_SHARD))
    mod_shard, c_act = _ada_fwd(c_all, w_ada[0], b_ada_shard, "ada_fwd")
    small_in = [jnp.pad(mod_shard, ((0, 8), (0, 0))),
                jnp.pad(w_dw_conv[0], ((0, 1), (0, 0))),
                jnp.pad(w_dw_ffn[0], ((0, 13), (0, 0)))]
    mod_g, wdc_g, wdf_g = _gather_shards(small_in, [False, False, True], "gather_small")
    mod_all = jnp.transpose(mod_g[:, :8, :], (1, 0, 2)).reshape(8, 6 * D)
    in_send, in_recv, in_flight, token = _gather_ici_start(own["in"], mod_g, False, "gather_w_in_start")

    def w_in_ready(after):
        landed = _gather_ici_wait(in_send, in_recv, in_flight, after, False, "gather_w_in_wait")
        return _run_carried(_carry_gather_forward([landed], [False]), "gather_forward_in")[0]

    for n in LATE:
        own[n] = _to_bf16_slot(given["w_" + n][0], slots[SWAPPED[n]], "cast_" + n, after=token)
    mod_row = lax.dynamic_slice(mod_all, (me, 0), (1, 6 * D)) + token[0:1, 0:1]
    mod = [mod_row[:, k * D:(k + 1) * D] for k in range(6)]

    core = ac.astype(jnp.int32).reshape(1)
    dist = {"core": core, "slots": slots}

    sel = jnp.asarray(_toeplitz_map())
    rel_pad = jnp.pad(rel_bias[0], ((0, 0), (0, REL_PAD - (2 * MAX_REL + 1))))
    gen = _select_call(rel_pad, sel.T.astype(BF16), "bias_rows")
    small = {"g_pre_mix": g_pre_mix, "g_post_mix": g_post_mix, "b_in": b_in, "gen": gen,
             "w_dw_conv": _unshard_cols(wdc_g[:, :CONV_K, :]), "b_dw_conv": b_dw_conv, "g_conv_ln": g_conv_ln,
             "b_conv_ln": b_conv_ln, "b_conv_o": b_conv_o, "g_pre_ffn": g_pre_ffn, "g_post_ffn": g_post_ffn,
             "w_dw_ffn": _unshard_cols(wdf_g[:, :FFN_K, :]), "b_dw_ffn": _ff_swap(b_dw_ffn)}

    loss_cols, grad_x, dmod, reduced, sm = _local_step(xs, target, mod, w_in_ready, {n: own[n] for n in LATE}, small, dist)

    d_rel = _select_call(sm["gen"], sel.astype(BF16), "bias_fold")[:, :2 * MAX_REL + 1]
    small_grads = {"g_pre_mix": sm["g_pre_mix"], "g_post_mix": sm["g_post_mix"], "b_in": sm["b_in"], "rel_bias": d_rel[None],
                   "b_dw_conv": sm["b_dw_conv"], "g_conv_ln": sm["g_conv_ln"], "b_conv_ln": sm["b_conv_ln"],
                   "b_conv_o": sm["b_conv_o"], "g_pre_ffn": sm["g_pre_ffn"], "g_post_ffn": sm["g_post_ffn"],
                   "b_dw_ffn": _ff_swap(sm["b_dw_ffn"]), "w_dw_conv": sm["w_dw_conv"], "w_dw_ffn": _ff_swap(sm["w_dw_ffn"])}
    order = [n for n in SMALL_ORDER if n != "b_ada"]
    packed, offs = _pack([jnp.concatenate(dmod, axis=1)] + [small_grads[n] for n in order] + [loss_cols])
    mine = lax.dynamic_update_slice(jnp.zeros((8,) + packed.shape, F32), packed[None], (me, 0, 0))
    sg_send, sg_recv, sg_flight, sg_token = _all8_start(mine, "gather_small_grads_start")

    out = {}
    for n in BIG:
        g, dl, nm, nv = _adamw(given["w_" + n][0], reduced[n], given["m_w_" + n][0], given["v_w_" + n][0],
                               "adamw_" + n, sg_token)
        out["grad_w_" + n], out["delta_w_" + n], out["new_m_w_" + n], out["new_v_w_" + n] = g[None], dl[None], nm[None], nv[None]
    every = _all8_wait(sg_send, sg_recv, sg_flight, out["delta_w_in"], "gather_small_grads_wait")
    total = _sum8(every, "sum_small_grads")
    loss = jnp.sum(total[offs[-1]])
    offs = offs[:-1]
    dmod_all = every[:, 0:6, :].reshape(8, 6 * D)
    full_shapes = {n: given[n].shape for n in order}
    full_shapes["w_dw_conv"], full_shapes["w_dw_ffn"] = (1, CONV_K, 512), (1, FFN_K, 2 * D_FF)
    sums = dict(zip(order, _unpack(total, offs[1:], [full_shapes[n] for n in order])))
    sums["b_ada"] = total[0:6].reshape(1, 6 * D)
    sums["w_dw_conv"] = lax.dynamic_slice(sums["w_dw_conv"], (0, 0, shard * 128), (1, CONV_K, 128))
    sums["w_dw_ffn"] = lax.dynamic_slice(sums["w_dw_ffn"], (0, 0, shard * FF_BLOCK), (1, FFN_K, FF_BLOCK))

    upd = dict(zip(SMALL_ORDER, _adamw_many(
        [(given[n], sums[n], given["m_" + n], given["v_" + n]) for n in SMALL_ORDER], "adamw_small")))

    dmod_shard = lax.dynamic_slice(dmod_all, (0, shard * ADA_SHARD), (8, ADA_SHARD))
    ada = _ada_bwd_adamw(c_act.T, dmod_shard, w_ada[0], m_w_ada[0], v_w_ada[0], "ada_bwd_adamw")

    out.update({"grad_w_ada": ada[0][None], "delta_w_ada": ada[1][None], "new_m_w_ada": ada[2][None],
                "new_v_w_ada": ada[3][None]})
    for n in SMALL_ORDER:
        out["grad_" + n], out["delta_" + n], out["new_m_" + n], out["new_v_" + n] = sums[n], *upd[n]

    weights = ["w_ada", "b_ada", "g_pre_mix", "g_post_mix", "w_in", "b_in", "rel_bias", "w_attn_o", "w_dw_conv", "b_dw_conv",
               "g_conv_ln", "b_conv_ln", "w_conv_o", "b_conv_o", "w_mix_o", "g_pre_ffn", "g_post_ffn", "w_up", "w_dw_ffn",
               "b_dw_ffn", "w_down"]
    return (loss, grad_x[None], *[out["grad_" + n] for n in weights], *[out["delta_" + n] for n in weights],
            *[out["new_m_" + n] for n in weights], *[out["new_v_" + n] for n in weights])
```

```python
import functools
import math

import numpy as np
import jax
import jax.numpy as jnp
from jax import lax
from jax.experimental import pallas as pl
from jax.experimental.pallas import tpu as pltpu

F32, BF16 = jnp.float32, jnp.bfloat16
MESH = pl.DeviceIdType.MESH

D = 1024
D_IN = 4608
D_FF = 2816
N_CHIPS = 4
IN_SHARD = D_IN // N_CHIPS
ADA_SHARD = 6 * D // N_CHIPS
CONV_K = 31
FFN_K = 3
N_HEADS = 8
CHUNK = 64
LEFT_CHUNKS = 8
MAX_REL = 128
EPS = 1e-6
NEG_INF = -1e30
Q_TILE = 256
WINDOW = Q_TILE + LEFT_CHUNKS * CHUNK
STEP_ROWS = 1024
SCORE_SLOTS = 2 * STEP_ROWS // Q_TILE
REL_PAD = 384
TOEP = 1024
ROW_TILE = 256
WORK_TILE = 512
VMEM_LIMIT = 60 * 1024 * 1024

ADAM_LR, ADAM_B1, ADAM_B2, ADAM_EPS, ADAM_WD, ADAM_STEP = 0.001, 0.9, 0.999, 1e-08, 0.01, 10


def _params(sem=None):
    return pltpu.CompilerParams(dimension_semantics=sem, vmem_limit_bytes=VMEM_LIMIT)


def _sds(shape, dtype):
    return jax.ShapeDtypeStruct(tuple(shape), dtype)


ANY = pl.BlockSpec(memory_space=pl.ANY)


class _Carried:
    def __init__(self, ins, out_shapes, aliases, n_sems, start, finish):
        self.ins, self.out_shapes, self.aliases = list(ins), list(out_shapes), dict(aliases)
        self.n_sems, self.start, self.finish = n_sems, start, finish


def _call(body, *, grid, in_specs, out_specs, out_shape, scratch_shapes, sem, name, args, carried=None):
    in_specs, out_specs, out_shape = list(in_specs), list(out_specs), list(out_shape)
    scratch_shapes = list(scratch_shapes)
    if carried is None:
        return pl.pallas_call(body, grid=grid, in_specs=in_specs, out_specs=out_specs, out_shape=out_shape,
                              scratch_shapes=scratch_shapes, compiler_params=_params(sem), name=name)(*args)
    n_in, n_out, n_scr = len(in_specs), len(out_specs), len(scratch_shapes)
    c_in, c_out = len(carried.ins), len(carried.out_shapes)

    def full(*refs):
        pos = [0]

        def take(k):
            part = refs[pos[0]:pos[0] + k]
            pos[0] += k
            return part

        ins, cins, outs, couts, scr = take(n_in), take(c_in), take(n_out), take(c_out), take(n_scr)
        send_sems, recv_sems = take(2)
        first = last = None
        for d, size in enumerate(grid):
            pid = pl.program_id(d)
            first = (pid == 0) if first is None else first & (pid == 0)
            last = (pid == size - 1) if last is None else last & (pid == size - 1)

        @pl.when(first)
        def _():
            carried.start(cins, couts, send_sems, recv_sems)

        body(*ins, *outs, *scr)

        @pl.when(last)
        def _():
            carried.finish(cins, couts, send_sems, recv_sems)

    sems = [pltpu.SemaphoreType.DMA((carried.n_sems,)), pltpu.SemaphoreType.DMA((carried.n_sems,))]
    return pl.pallas_call(
        full, grid=grid, in_specs=in_specs + [ANY] * c_in, out_specs=out_specs + [ANY] * c_out,
        out_shape=out_shape + carried.out_shapes, scratch_shapes=scratch_shapes + sems,
        input_output_aliases={n_in + k: n_out + v for k, v in carried.aliases.items()},
        compiler_params=_params(tuple("arbitrary" for _ in grid)), name=name,
    )(*args, *carried.ins)


def _run_carried(carried, name):
    c_in = len(carried.ins)

    def body(*refs):
        cins, couts = refs[:c_in], refs[c_in:c_in + len(carried.out_shapes)]
        send_sems, recv_sems = refs[-2:]
        carried.start(cins, couts, send_sems, recv_sems)
        carried.finish(cins, couts, send_sems, recv_sems)

    return pl.pallas_call(
        body, in_specs=[ANY] * c_in, out_specs=[ANY] * len(carried.out_shapes), out_shape=carried.out_shapes,
        scratch_shapes=[pltpu.SemaphoreType.DMA((carried.n_sems,)), pltpu.SemaphoreType.DMA((carried.n_sems,))],
        input_output_aliases=carried.aliases, name=name,
    )(*carried.ins)


def _matmul(a, b, *, form, out_dtype, tm, tn, tk, name, bias=None, add=None, out_sharded=False, carried=None):
    b3 = b.ndim == 3
    resident = 0
    if form == "nn":
        m, k = a.shape
        n = b.shape[0] * b.shape[2] if b3 else b.shape[1]
        dn = (((1,), (0,)), ((), ()))
        a_spec = pl.BlockSpec((tm, tk), lambda i, j, kk: (i, kk))
        if b3 and tn == n and tk == k:
            resident = b.shape[0]
            b_spec = pl.BlockSpec(b.shape, lambda i, j, kk: (0, 0, 0))
        else:
            b_spec = (pl.BlockSpec((None, tk, tn), lambda i, j, kk: (j, kk, 0)) if b3
                      else pl.BlockSpec((tk, tn), lambda i, j, kk: (kk, j)))
    elif form == "nt":
        m, k = a.shape
        n = b.shape[1] if b3 else b.shape[0]
        dn = (((1,), (1,)), ((), ()))
        a_spec = pl.BlockSpec((tm, tk), lambda i, j, kk: (i, kk))
        if b3 and tk == k:
            resident = b.shape[0]
            b_spec = pl.BlockSpec((resident, tn, b.shape[2]), lambda i, j, kk: (0, j, 0))
        else:
            b_spec = (pl.BlockSpec((None, tn, tk), lambda i, j, kk: (kk, j, 0)) if b3
                      else pl.BlockSpec((tn, tk), lambda i, j, kk: (j, kk)))
    else:
        k, m = a.shape
        n = b.shape[1]
        dn = (((0,), (0,)), ((), ()))
        a_spec = pl.BlockSpec((tk, tm), lambda i, j, kk: (kk, i))
        b_spec = pl.BlockSpec((tk, tn), lambda i, j, kk: (kk, j))
    assert m % tm == 0 and n % tn == 0 and k % tk == 0, (name, m, n, k, tm, tn, tk)
    nk = k // tk
    in_specs, args = [a_spec, b_spec], [a, b]
    if bias is not None:
        in_specs.append(pl.BlockSpec((1, tn), lambda i, j, kk: (0, j)))
        args.append(bias)
    if add is not None:
        in_specs.append(pl.BlockSpec((tm, tn), lambda i, j, kk: (i, j)))
        args.append(add)
    if out_sharded:
        out_shape = _sds((n // tn, m, tn), out_dtype)
        out_spec = pl.BlockSpec((None, tm, tn), lambda i, j, kk: (j, i, 0))
    else:
        out_shape = _sds((m, n), out_dtype)
        out_spec = pl.BlockSpec((tm, tn), lambda i, j, kk: (i, j))

    def body(*refs):
        a_ref, b_ref = refs[0], refs[1]
        pos = 2
        bias_ref = add_ref = None
        if bias is not None:
            bias_ref, pos = refs[pos], pos + 1
        if add is not None:
            add_ref, pos = refs[pos], pos + 1
        o_ref = refs[pos]
        if resident and form == "nn":
            ns = b_ref.shape[2]
            for s in range(resident):
                cols = slice(s * ns, (s + 1) * ns)
                ps = lax.dot_general(a_ref[...], b_ref[s], dn, preferred_element_type=F32)
                if bias_ref is not None:
                    ps = ps + bias_ref[:, cols]
                o_ref[:, cols] = ps.astype(o_ref.dtype)
            return
        if resident:
            ks = b_ref.shape[2]
            p = None
            for s in range(resident):
                ps = lax.dot_general(a_ref[:, s * ks:(s + 1) * ks], b_ref[s], dn, preferred_element_type=F32)
                p = ps if p is None else p + ps
        else:
            av, bv = a_ref[...], b_ref[...]
            if av.dtype != BF16:
                av = av.astype(BF16)
            if bv.dtype != BF16:
                bv = bv.astype(BF16)
            p = lax.dot_general(av, bv, dn, preferred_element_type=F32)

        def finish(acc):
            if bias_ref is not None:
                acc = acc + bias_ref[...]
            if add_ref is not None:
                acc = acc + add_ref[...]
            o_ref[...] = acc.astype(o_ref.dtype)

        if nk == 1:
            finish(p)
        else:
            acc_ref = refs[pos + 1]
            kk = pl.program_id(2)

            @pl.when(kk == 0)
            def _():
                acc_ref[...] = p

            @pl.when(kk > 0)
            def _():
                acc_ref[...] += p

            @pl.when(kk == nk - 1)
            def _():
                finish(acc_ref[...])

    res = _call(body, grid=(m // tm, n // tn, nk), in_specs=in_specs, out_specs=[out_spec], out_shape=[out_shape],
                scratch_shapes=[pltpu.VMEM((tm, tn), F32)] if nk > 1 else [],
                sem=("parallel", "parallel", "arbitrary"), name=name, args=args, carried=carried)
    return res[0] if carried is None else (res[0], res[1:])


def _rowcall(fn, rows, consts, row_outs, acc_outs, *, name, tm=ROW_TILE, col_grid=1):
    n_rows = rows[0][0].shape[0]
    assert n_rows % tm == 0
    grid = (col_grid, n_rows // tm)
    in_specs = [pl.BlockSpec((tm, w), functools.partial(lambda c, i, cb: (i, cb + c), cb=cb)) for _, w, cb in rows]
    in_specs += [pl.BlockSpec(k.shape, functools.partial(lambda c, i, nd: (0,) * nd, nd=k.ndim)) for k in consts]
    out_specs = [pl.BlockSpec((tm, w), lambda c, i: (i, c)) for _, _, _, w in row_outs]
    out_specs += [pl.BlockSpec((r, w), lambda c, i: (0, c)) for r, _, w in acc_outs]
    out_shape = [_sds((nr, nc), dt) for nr, nc, dt, _ in row_outs] + [_sds((r, nc), F32) for r, nc, _ in acc_outs]
    n_in, n_ro = len(rows) + len(consts), len(row_outs)

    def body(*refs):
        res = fn(*[r[...] for r in refs[:n_in]])
        if not isinstance(res, (tuple, list)):
            res = (res,)
        outs = refs[n_in:]
        for o_ref, val in zip(outs[:n_ro], res[:n_ro]):
            o_ref[...] = val.astype(o_ref.dtype)
        if acc_outs:
            first = pl.program_id(1) == 0

            @pl.when(first)
            def _():
                for o_ref, val in zip(outs[n_ro:], res[n_ro:]):
                    o_ref[...] = val

            @pl.when(jnp.logical_not(first))
            def _():
                for o_ref, val in zip(outs[n_ro:], res[n_ro:]):
                    o_ref[...] += val

    out = pl.pallas_call(
        body, grid=grid, in_specs=in_specs, out_specs=out_specs, out_shape=out_shape,
        compiler_params=_params(("arbitrary", "arbitrary")), name=name,
    )(*[r[0] for r in rows], *consts)
    return out


def _matmul_rows(b, *, form, tm, tk, fn, rows, consts, row_outs, acc_outs, name, a=None, a_rows=None, a_fn=None,
                 carried=None):
    b3 = b.ndim == 3
    resident = 0
    if form == "nn":
        k, n = b.shape
        b_spec = pl.BlockSpec((tk, n), lambda i, kk: (kk, 0))
        dn = (((1,), (0,)), ((), ()))
    else:
        n = b.shape[1] if b3 else b.shape[0]
        k = b.shape[0] * b.shape[2] if b3 else b.shape[1]
        if b3 and tk == k:
            resident = b.shape[0]
            b_spec = pl.BlockSpec(b.shape, lambda i, kk: (0, 0, 0), pipeline_mode=pl.Buffered(1))
        else:
            b_spec = (pl.BlockSpec((None, n, tk), lambda i, kk: (kk, 0, 0)) if b3
                      else pl.BlockSpec((n, tk), lambda i, kk: (0, kk)))
        dn = (((1,), (1,)), ((), ()))
    nk = k // tk
    lhs_in = [(a, tk, 0)] if a is not None else list(a_rows)
    assert a is not None or nk == 1
    m = lhs_in[0][0].shape[0]
    n_lhs = len(lhs_in)
    in_specs = [pl.BlockSpec((tm, tk), lambda i, kk: (i, kk))] if a is not None else [
        pl.BlockSpec((tm, w), functools.partial(lambda i, kk, cb: (i, cb), cb=cb)) for _, w, cb in a_rows]
    in_specs.append(b_spec)
    in_specs += [pl.BlockSpec((tm, w), functools.partial(lambda i, kk, cb: (i, cb), cb=cb)) for _, w, cb in rows]
    in_specs += [pl.BlockSpec(c.shape, functools.partial(lambda i, kk, nd: (0,) * nd, nd=c.ndim)) for c in consts]
    out_specs = [pl.BlockSpec((tm, w), lambda i, kk: (i, 0)) for _, w in row_outs]
    out_specs += [pl.BlockSpec((r, w), lambda i, kk: (0, 0)) for r, w in acc_outs]
    out_shape = [_sds((m, w), dt) for dt, w in row_outs] + [_sds((r, w), F32) for r, w in acc_outs]
    n_rows, n_consts, n_ro, n_acc = len(rows), len(consts), len(row_outs), len(acc_outs)

    def body(*refs):
        pos = n_lhs + 1
        row_refs, const_refs = refs[pos:pos + n_rows], refs[pos + n_rows:pos + n_rows + n_consts]
        pos += n_rows + n_consts
        out_refs, acc_refs = refs[pos:pos + n_ro], refs[pos + n_ro:pos + n_ro + n_acc]
        i, kk = pl.program_id(0), pl.program_id(1)
        if resident:
            b_ref, ks, p = refs[n_lhs], b.shape[2], None
            for s in range(resident):
                ps = lax.dot_general(refs[0][:, s * ks:(s + 1) * ks], b_ref[s], dn, preferred_element_type=F32)
                p = ps if p is None else p + ps
        else:
            lhs = refs[0][...] if a is not None else a_fn(*[r[...] for r in refs[:n_lhs]]).astype(BF16)
            p = lax.dot_general(lhs, refs[n_lhs][...], dn, preferred_element_type=F32)

        def finish(acc):
            extra = [r[...] for r in row_refs] + [c[...] for c in const_refs]
            res = fn(acc, lhs, *extra) if a is None else fn(acc, *extra)
            for o_ref, val in zip(out_refs, res[:n_ro]):
                o_ref[...] = val.astype(o_ref.dtype)
            if n_acc:
                @pl.when(i == 0)
                def _():
                    for o_ref, val in zip(acc_refs, res[n_ro:]):
                        o_ref[...] = val

                @pl.when(i > 0)
                def _():
                    for o_ref, val in zip(acc_refs, res[n_ro:]):
                        o_ref[...] += val

        if nk == 1:
            finish(p)
        else:
            acc_ref = refs[pos + n_ro + n_acc]

            @pl.when(kk == 0)
            def _():
                acc_ref[...] = p

            @pl.when(kk > 0)
            def _():
                acc_ref[...] += p

            @pl.when(kk == nk - 1)
            def _():
                finish(acc_ref[...])

    res = _call(body, grid=(m // tm, nk), in_specs=in_specs, out_specs=out_specs, out_shape=out_shape,
                scratch_shapes=[pltpu.VMEM((tm, n), F32)] if nk > 1 else [], sem=("arbitrary", "arbitrary"),
                name=name, args=[r[0] for r in lhs_in] + [b] + [r[0] for r in rows] + list(consts), carried=carried)
    own = n_ro + n_acc
    return res[:own] if carried is None else (res[:own], res[own:])


def _colsum(v):
    return jnp.sum(v, axis=0, keepdims=True)


def _sigmoid(v):
    return 1.0 / (1.0 + jnp.exp(-v))


_GELU_C = math.sqrt(2.0 / math.pi)


def _gelu(v):
    return 0.5 * v * (1.0 + jnp.tanh(_GELU_C * (v + 0.044715 * (v * v * v))))


def _gelu_and_grad(v):
    th = jnp.tanh(_GELU_C * (v + 0.044715 * (v * v * v)))
    g = 0.5 * v * (1.0 + th)
    dg = 0.5 * (1.0 + th) + 0.5 * v * (1.0 - th * th) * (_GELU_C * (1.0 + 3.0 * 0.044715 * (v * v)))
    return g, dg


def _rms_stats(v):
    r = lax.rsqrt(jnp.mean(v * v, axis=-1, keepdims=True) + EPS)
    return v * r, r


def _rms_bwd(dn, vn, r):
    return r * (dn - vn * jnp.mean(dn * vn, axis=-1, keepdims=True))


def _pre_norm(x, g, sc, sh, name):
    def fn(xv, gv, scv, shv):
        xn, _ = _rms_stats(xv)
        return (xn * gv) * (1.0 + scv) + shv
    return _rowcall(fn, [(x, D, 0)], [g, sc, sh], [(x.shape[0], D, BF16, D)], [], name=name,
                    tm=min(2 * ROW_TILE, x.shape[0]))[0]


def _pre_norm_bwd(dh, x, dx_other, g, sc, name):
    def fn(dhv, xv, dov, gv, scv):
        xn, r = _rms_stats(xv)
        yn = xn * gv
        dyn = dhv * (1.0 + scv)
        dx = _rms_bwd(dyn * gv, xn, r)
        return dov + dx, _colsum(dhv), _colsum(dhv * yn), _colsum(dyn * xn)
    t = x.shape[0]
    return _rowcall(fn, [(dh, D, 0), (x, D, 0), (dx_other, D, 0)], [g, sc], [(t, D, F32, D)],
                    [(1, D, D)] * 3, name=name, tm=min(2 * ROW_TILE, t))


CONV_HALO = 32


def _layer_norm_parts(u):
    mu = jnp.mean(u, axis=-1, keepdims=True)
    d = u - mu
    r = lax.rsqrt(jnp.mean(d * d, axis=-1, keepdims=True) + EPS)
    return d * r, r


LANES = 128
SUBLANE_ROWS = 8
CONV_ROWS = 64


def _lanes(c):
    return slice(c * LANES, (c + 1) * LANES)


def _conv_branch(z, w_dw, b_dw, g_ln, b_ln, name, tm=WORK_TILE, carried=None):
    t = z.shape[0]
    per = tm // CONV_HALO
    n_chunks = 512 // LANES

    def body(ga_ref, gb_ref, gah_ref, gbh_ref, w_ref, b_ref, g_ref, bl_ref, u1_ref, u3_ref, scr):
        i = pl.program_id(0)
        u0h = jnp.where(i > 0, gah_ref[...] * _sigmoid(gbh_ref[...]), 0.0)
        u0 = ga_ref[...] * _sigmoid(gb_ref[...])
        for c in range(n_chunks):
            scr[c, 0:CONV_HALO, :] = u0h[:, _lanes(c)]
            scr[c, CONV_HALO:CONV_HALO + tm, :] = u0[:, _lanes(c)]
        for c in range(n_chunks):
            for r0 in range(0, tm, CONV_ROWS):
                acc = jnp.zeros((CONV_ROWS, LANES), F32) + b_ref[:, _lanes(c)]
                for j in range(CONV_K):
                    acc = acc + w_ref[j:j + 1, _lanes(c)] * scr[c, pl.ds(r0 + CONV_HALO - (CONV_K - 1) + j, CONV_ROWS), :]
                u1_ref[r0:r0 + CONV_ROWS, _lanes(c)] = acc
        xh, _ = _layer_norm_parts(u1_ref[...])
        u2 = xh * g_ref[...] + bl_ref[...]
        u3_ref[...] = (u2 * _sigmoid(u2)).astype(BF16)

    cur = lambda cb: pl.BlockSpec((tm, 512), lambda i: (i, cb))
    halo = lambda cb: pl.BlockSpec((CONV_HALO, 512), lambda i: (jnp.maximum(i * per - 1, 0), cb))
    whole = lambda a: pl.BlockSpec(a.shape, lambda i: (0, 0))
    res = _call(
        body, grid=(t // tm,),
        in_specs=[cur(3), cur(4), halo(3), halo(4), whole(w_dw), whole(b_dw), whole(g_ln), whole(b_ln)],
        out_specs=[pl.BlockSpec((tm, 512), lambda i: (i, 0))] * 2,
        out_shape=[_sds((t, 512), F32), _sds((t, 512), BF16)],
        scratch_shapes=[pltpu.VMEM((n_chunks, CONV_HALO + tm, LANES), F32)],
        sem=("arbitrary",), name=name, args=(z, z, z, z, w_dw, b_dw, g_ln, b_ln), carried=carried)
    return res[:2] if carried is None else (res[:2], res[2:])


def _conv_branch_bwd(du3, u1, z, w_dw, g_ln, b_ln, name, tm=WORK_TILE, carried=None):
    t = z.shape[0]
    per = tm // CONV_HALO
    last = t // tm - 1
    n_chunks = 512 // LANES

    def du1_of(du3v, u1v, g, b):
        xh, r = _layer_norm_parts(u1v)
        u2 = xh * g + b
        s = _sigmoid(u2)
        du2 = du3v * (s * (1.0 + u2 * (1.0 - s)))
        dxh = du2 * g
        du1 = r * (dxh - jnp.mean(dxh, axis=-1, keepdims=True) - xh * jnp.mean(dxh * xh, axis=-1, keepdims=True))
        return du1, du2, xh

    def body(d_ref, u_ref, dn_ref, un_ref, ga_ref, gb_ref, gah_ref, gbh_ref, w_ref, g_ref, bl_ref,
             dglu_ref, dw_ref, dbdw_ref, dg_ref, dbl_ref, dbin_ref, scr, scd):
        i = pl.program_id(0)
        g, b = g_ref[...], bl_ref[...]
        du1, du2, xh = du1_of(d_ref[...], u_ref[...], g, b)
        du1n, _, _ = du1_of(dn_ref[...], un_ref[...], g, b)
        du1n = jnp.where(i < last, du1n, 0.0)
        sgb = _sigmoid(gb_ref[...])
        ga = ga_ref[...]
        u0 = ga * sgb
        u0h = jnp.where(i > 0, gah_ref[...] * _sigmoid(gbh_ref[...]), 0.0)
        for c in range(n_chunks):
            scd[c, 0:tm, :] = du1[:, _lanes(c)]
            scd[c, tm:tm + CONV_HALO, :] = du1n[:, _lanes(c)]
            scr[c, 0:CONV_HALO, :] = u0h[:, _lanes(c)]
            scr[c, CONV_HALO:CONV_HALO + tm, :] = u0[:, _lanes(c)]

        @pl.when(i == 0)
        def _():
            for ref in (dw_ref, dbdw_ref, dg_ref, dbl_ref, dbin_ref):
                ref[...] = jnp.zeros_like(ref)

        dsg = ga * (sgb * (1.0 - sgb))
        for c in range(n_chunks):
            gate = slice(512 + c * LANES, 512 + (c + 1) * LANES)
            for r0 in range(0, tm, CONV_ROWS):
                rows = slice(r0, r0 + CONV_ROWS)
                du0 = jnp.zeros((CONV_ROWS, LANES), F32)
                for j in range(CONV_K):
                    du0 = du0 + w_ref[j:j + 1, _lanes(c)] * scd[c, pl.ds(r0 + CONV_K - 1 - j, CONV_ROWS), :]
                dga = du0 * sgb[rows, _lanes(c)]
                dgb = du0 * dsg[rows, _lanes(c)]
                dglu_ref[rows, _lanes(c)] = dga.astype(BF16)
                dglu_ref[rows, gate] = dgb.astype(BF16)
                dbin_ref[:, _lanes(c)] += _colsum(dga)
                dbin_ref[:, gate] += _colsum(dgb)
            for j in range(CONV_K):
                dwj = jnp.zeros((SUBLANE_ROWS, LANES), F32)
                for r0 in range(0, tm, CONV_ROWS):
                    prod = (scd[c, pl.ds(r0, CONV_ROWS), :]
                            * scr[c, pl.ds(r0 + CONV_HALO - (CONV_K - 1) + j, CONV_ROWS), :])
                    dwj = dwj + jnp.sum(prod.reshape(CONV_ROWS // SUBLANE_ROWS, SUBLANE_ROWS, LANES), axis=0)
                dw_ref[j:j + 1, _lanes(c)] += _colsum(dwj)
        dbdw_ref[...] += _colsum(du1)
        dg_ref[...] += _colsum(du2 * xh)
        dbl_ref[...] += _colsum(du2)

    cur = lambda cb: pl.BlockSpec((tm, 512), lambda i: (i, cb))
    prev = lambda cb: pl.BlockSpec((CONV_HALO, 512), lambda i: (jnp.maximum(i * per - 1, 0), cb))
    nxt = pl.BlockSpec((CONV_HALO, 512), lambda i: (jnp.minimum((i + 1) * per, t // CONV_HALO - 1), 0))
    whole = lambda a: pl.BlockSpec(a.shape, lambda i: (0, 0))
    acc = lambda r, w: pl.BlockSpec((r, w), lambda i: (0, 0))
    res = _call(
        body, grid=(t // tm,),
        in_specs=[cur(0), cur(0), nxt, nxt, cur(3), cur(4), prev(3), prev(4), whole(w_dw), whole(g_ln), whole(b_ln)],
        out_specs=[pl.BlockSpec((tm, 1024), lambda i: (i, 0)), acc(CONV_K, 512), acc(1, 512), acc(1, 512),
                   acc(1, 512), acc(1, 1024)],
        out_shape=[_sds((t, 1024), BF16), _sds((CONV_K, 512), F32), _sds((1, 512), F32), _sds((1, 512), F32),
                   _sds((1, 512), F32), _sds((1, 1024), F32)],
        scratch_shapes=[pltpu.VMEM((n_chunks, CONV_HALO + tm, LANES), F32),
                        pltpu.VMEM((n_chunks, tm + CONV_HALO, LANES), F32)],
        sem=("arbitrary",), name=name, args=(du3, u1, du3, u1, z, z, z, z, w_dw, g_ln, b_ln), carried=carried)
    return res[:6] if carried is None else (res[:6], res[6:])


FF_BLOCK = D_FF // 2
FF_HALO = 8
FF_CHUNKS = FF_BLOCK // LANES


FF_ROWS = 64


def _ext_rows(ext):
    return next(d for d in (104, 88, 72, 56, 40, 24, 8) if ext % d == 0)


def _ffn_conv(w_ref, b_ref, scr, k, rows, r0=0):
    acc = b_ref[:, _lanes(k)] + w_ref[0:1, _lanes(k)] * scr[k, pl.ds(r0 + FF_HALO - 2, rows), :]
    acc = acc + w_ref[1:2, _lanes(k)] * scr[k, pl.ds(r0 + FF_HALO - 1, rows), :]
    return acc + w_ref[2:3, _lanes(k)] * scr[k, pl.ds(r0 + FF_HALO, rows), :]


def _ffn_act(up, w3, b3, name, tm=WORK_TILE, carried=None):
    t = up.shape[0]
    per = tm // FF_HALO
    wide = 2 * FF_BLOCK

    def body(u_ref, uh_ref, w_ref, b_ref, o_ref, scr):
        i = pl.program_id(1)
        for k in range(2 * FF_CHUNKS):
            scr[k, 0:FF_HALO, :] = jnp.where(i > 0, uh_ref[:, _lanes(k)], 0.0)
            scr[k, FF_HALO:FF_HALO + tm, :] = u_ref[:, _lanes(k)]
        for cc in range(FF_CHUNKS):
            for r0 in range(0, tm, FF_ROWS):
                val = _ffn_conv(w_ref, b_ref, scr, cc, FF_ROWS, r0)
                gate = _ffn_conv(w_ref, b_ref, scr, FF_CHUNKS + cc, FF_ROWS, r0)
                o_ref[r0:r0 + FF_ROWS, _lanes(cc)] = (_gelu(gate) * val).astype(BF16)

    res = _call(
        body, grid=(2, t // tm),
        in_specs=[pl.BlockSpec((tm, wide), lambda c, i: (i, c)),
                  pl.BlockSpec((FF_HALO, wide), lambda c, i: (jnp.maximum(i * per - 1, 0), c)),
                  pl.BlockSpec((FFN_K, wide), lambda c, i: (0, c)),
                  pl.BlockSpec((1, wide), lambda c, i: (0, c))],
        out_specs=[pl.BlockSpec((tm, FF_BLOCK), lambda c, i: (i, c))],
        out_shape=[_sds((t, D_FF), BF16)],
        scratch_shapes=[pltpu.VMEM((2 * FF_CHUNKS, FF_HALO + tm, LANES), F32)],
        sem=("arbitrary", "arbitrary"), name=name, args=(up, up, w3, b3), carried=carried)
    return res[0] if carried is None else (res[0], res[1:])


def _ffn_act_bwd(dact, up, w3, b3, name, tm=WORK_TILE):
    t = up.shape[0]
    per = tm // FF_HALO
    wide = 2 * FF_BLOCK
    last = t // tm - 1
    ext = tm + FF_HALO
    FF_EXT_ROWS = _ext_rows(ext)

    def body(u_ref, up_ref, un_ref, d_ref, dn_ref, w_ref, b_ref, o_ref, dw_ref, db_ref, scr, scd):
        i = pl.program_id(1)
        for k in range(2 * FF_CHUNKS):
            scr[k, 0:FF_HALO, :] = jnp.where(i > 0, up_ref[:, _lanes(k)], 0.0)
            scr[k, FF_HALO:FF_HALO + tm, :] = u_ref[:, _lanes(k)]
            scr[k, FF_HALO + tm:FF_HALO + ext, :] = un_ref[:, _lanes(k)]
        dn = jnp.where(i < last, dn_ref[...], 0.0)

        @pl.when(i == 0)
        def _():
            dw_ref[...] = jnp.zeros_like(dw_ref)
            db_ref[...] = jnp.zeros_like(db_ref)

        for cc in range(FF_CHUNKS):
            gc = FF_CHUNKS + cc
            for r0 in range(0, ext, FF_EXT_ROWS):
                rows = pl.ds(r0, FF_EXT_ROWS)
                val = _ffn_conv(w_ref, b_ref, scr, cc, FF_EXT_ROWS, r0)
                gel, dgel = _gelu_and_grad(_ffn_conv(w_ref, b_ref, scr, gc, FF_EXT_ROWS, r0))
                da = d_ref[r0:r0 + FF_EXT_ROWS, _lanes(cc)] if r0 + FF_EXT_ROWS <= tm else jnp.concatenate(
                    [d_ref[r0:tm, _lanes(cc)], dn[:, _lanes(cc)]], axis=0)
                scd[cc, rows, :] = da * gel
                scd[gc, rows, :] = da * val * dgel
            for k in (cc, gc):
                dwk = [jnp.zeros((SUBLANE_ROWS, LANES), F32) for _ in range(FFN_K)]
                dbk = jnp.zeros((SUBLANE_ROWS, LANES), F32)
                for r0 in range(0, tm, FF_ROWS):
                    shifted = [scd[k, pl.ds(r0 + FFN_K - 1 - j, FF_ROWS), :] for j in range(FFN_K)]
                    ucur = scr[k, pl.ds(r0 + FF_HALO, FF_ROWS), :]
                    o_ref[r0:r0 + FF_ROWS, _lanes(k)] = (
                        w_ref[0:1, _lanes(k)] * shifted[0] + w_ref[1:2, _lanes(k)] * shifted[1]
                        + w_ref[2:3, _lanes(k)] * shifted[2]).astype(BF16)
                    fold = lambda v: jnp.sum(v.reshape(FF_ROWS // SUBLANE_ROWS, SUBLANE_ROWS, LANES), axis=0)
                    for j in range(FFN_K):
                        dwk[j] = dwk[j] + fold(shifted[j] * ucur)
                    dbk = dbk + fold(shifted[FFN_K - 1])
                for j in range(FFN_K):
                    dw_ref[j:j + 1, _lanes(k)] += _colsum(dwk[j])
                db_ref[:, _lanes(k)] += _colsum(dbk)

    nblk = t // FF_HALO
    return pl.pallas_call(
        body, grid=(2, t // tm),
        in_specs=[pl.BlockSpec((tm, wide), lambda c, i: (i, c)),
                  pl.BlockSpec((FF_HALO, wide), lambda c, i: (jnp.maximum(i * per - 1, 0), c)),
                  pl.BlockSpec((FF_HALO, wide), lambda c, i: (jnp.minimum((i + 1) * per, nblk - 1), c)),
                  pl.BlockSpec((tm, FF_BLOCK), lambda c, i: (i, c)),
                  pl.BlockSpec((FF_HALO, FF_BLOCK), lambda c, i: (jnp.minimum((i + 1) * per, nblk - 1), c)),
                  pl.BlockSpec((FFN_K, wide), lambda c, i: (0, c)),
                  pl.BlockSpec((1, wide), lambda c, i: (0, c))],
        out_specs=[pl.BlockSpec((tm, wide), lambda c, i: (i, c)),
                   pl.BlockSpec((FFN_K, wide), lambda c, i: (0, c)),
                   pl.BlockSpec((1, wide), lambda c, i: (0, c))],
        out_shape=[_sds((t, 2 * D_FF), BF16), _sds((FFN_K, 2 * D_FF), F32), _sds((1, 2 * D_FF), F32)],
        scratch_shapes=[pltpu.VMEM((2 * FF_CHUNKS, FF_HALO + ext, LANES), F32),
                        pltpu.VMEM((2 * FF_CHUNKS, ext, LANES), F32)],
        compiler_params=_params(("arbitrary", "arbitrary")), name=name,
    )(up, up, up, dact, dact, w3, b3)


def _toeplitz_map():
    f = np.zeros((TOEP, REL_PAD), np.float32)
    for m in range(TOEP - 1):
        rel = (WINDOW - 1) - m
        f[m, int(np.clip(rel, -MAX_REL, MAX_REL)) + MAX_REL] = 1.0
    return f


def _split3(v):
    hi = v.astype(BF16)
    r1 = v - hi.astype(F32)
    mid = r1.astype(BF16)
    lo = (r1 - mid.astype(F32)).astype(BF16)
    return hi, mid, lo


def _exact_select(v, sel):
    out = None
    for part in _split3(v):
        p = jnp.dot(part, sel, preferred_element_type=F32)
        out = p if out is None else out + p
    return out


def _select_call(v, sel, name):
    def body(v_ref, s_ref, o_ref):
        o_ref[...] = _exact_select(v_ref[...], s_ref[...])
    return pl.pallas_call(body, out_shape=_sds((v.shape[0], sel.shape[1]), F32), name=name)(v, sel)


def _band_bias(gen_row):
    b0 = jnp.broadcast_to(gen_row, (Q_TILE, TOEP))
    bias = pltpu.roll(b0, TOEP - (Q_TILE - 1), 1, stride=1, stride_axis=0)[:, :WINDOW]
    qq = lax.broadcasted_iota(jnp.int32, (Q_TILE, WINDOW), 0) // CHUNK
    kc = lax.broadcasted_iota(jnp.int32, (Q_TILE, WINDOW), 1) // CHUNK
    return jnp.where((kc >= qq) & (kc <= qq + LEFT_CHUNKS), bias, NEG_INF)


PAD_ROWS = WINDOW - Q_TILE
NT_DIMS = (((1,), (1,)), ((), ()))
TN_DIMS = (((0,), (0,)), ((), ()))


def _head_mask(hh):
    lane = lax.broadcasted_iota(jnp.int32, (1, 128), 1)
    return (lane < 64) if hh == 0 else (lane >= 64)


SOFTMAX_ROWS = 16


def _probs_block(s_scr, bias, hh, rows, q_start):
    s = s_scr[rows, :] + bias[hh, rows, :]
    col = lax.broadcasted_iota(jnp.int32, (SOFTMAX_ROWS, WINDOW), 1)
    s = jnp.where(col >= PAD_ROWS - q_start, s, NEG_INF)
    p = jnp.exp(s - jnp.max(s, axis=-1, keepdims=True))
    return p / jnp.sum(p, axis=-1, keepdims=True)


def _attention(z, gen, name, carried=None):
    t = z.shape[0]
    n_i = t // STEP_ROWS

    def body(q_ref, k_ref, v_ref, g_ref, o_ref, kpad, vpad, bias, s_scr, p_scr):
        hp, i = pl.program_id(0), pl.program_id(1)

        @pl.when(i == 0)
        def _():
            kpad[0:PAD_ROWS, :] = jnp.zeros((PAD_ROWS, 128), BF16)
            vpad[0:PAD_ROWS, :] = jnp.zeros((PAD_ROWS, 128), BF16)
            kpad[PAD_ROWS:PAD_ROWS + t, :] = k_ref[...].astype(BF16)
            vpad[PAD_ROWS:PAD_ROWS + t, :] = v_ref[...].astype(BF16)
            for hh in range(2):
                bias[hh] = _band_bias(g_ref[pl.ds(2 * hp + hh, 1), :])

        for q0 in range(0, STEP_ROWS, Q_TILE):
            q_start = i * STEP_ROWS + q0
            win = pl.ds(pl.multiple_of(q_start, Q_TILE), WINDOW)
            out = None
            for hh in range(2):
                mask = _head_mask(hh)
                qm = jnp.where(mask, q_ref[q0:q0 + Q_TILE, :] * (CHUNK ** -0.5), 0.0).astype(BF16)
                slot = 2 * (q0 // Q_TILE) + hh
                s_scr[slot] = lax.dot_general(qm, kpad[win, :], NT_DIMS, preferred_element_type=F32)
                for r0 in range(0, Q_TILE, SOFTMAX_ROWS):
                    rows = slice(r0, r0 + SOFTMAX_ROWS)
                    p_scr[slot, rows, :] = _probs_block(s_scr.at[slot], bias, hh, rows, q_start).astype(BF16)
                o = jnp.dot(p_scr[slot], vpad[win, :], preferred_element_type=F32)
                out = jnp.where(mask, o, 0.0) if out is None else jnp.where(mask, o, out)
            o_ref[q0:q0 + Q_TILE, :] = out.astype(BF16)

    res = _call(
        body, grid=(4, n_i),
        in_specs=[pl.BlockSpec((STEP_ROWS, 128), lambda h, i: (i, h)),
                  pl.BlockSpec((t, 128), lambda h, i: (0, 4 + h)),
                  pl.BlockSpec((t, 128), lambda h, i: (0, 8 + h)),
                  pl.BlockSpec((N_HEADS, TOEP), lambda h, i: (0, 0))],
        out_specs=[pl.BlockSpec((STEP_ROWS, 128), lambda h, i: (i, h))],
        out_shape=[_sds((t, 512), BF16)],
        scratch_shapes=[pltpu.VMEM((PAD_ROWS + t, 128), BF16), pltpu.VMEM((PAD_ROWS + t, 128), BF16),
                        pltpu.VMEM((2, Q_TILE, WINDOW), F32), pltpu.VMEM((SCORE_SLOTS, Q_TILE, WINDOW), F32),
                        pltpu.VMEM((SCORE_SLOTS, Q_TILE, WINDOW), BF16)],
        sem=("arbitrary", "arbitrary"), name=name, args=(z, z, z, gen), carried=carried)
    return res[0] if carried is None else (res[0], res[1:])


def _attention_bwd(z, datt, gen, name, carried=None):
    t = z.shape[0]
    n_i = t // STEP_ROWS

    def body(q_ref, k_ref, v_ref, d_ref, g_ref, dq_ref, dk_ref, dv_ref, sq_ref, sk_ref, sv_ref, dg_ref,
             kpad, vpad, dkacc, dvacc, bias, dsacc, s_scr, dp_scr, p_scr, ds_scr):
        hp, i = pl.program_id(0), pl.program_id(1)

        @pl.when(i == 0)
        def _():
            kpad[0:PAD_ROWS, :] = jnp.zeros((PAD_ROWS, 128), BF16)
            vpad[0:PAD_ROWS, :] = jnp.zeros((PAD_ROWS, 128), BF16)
            kpad[PAD_ROWS:PAD_ROWS + t, :] = k_ref[...].astype(BF16)
            vpad[PAD_ROWS:PAD_ROWS + t, :] = v_ref[...].astype(BF16)
            dkacc[...] = jnp.zeros_like(dkacc)
            dvacc[...] = jnp.zeros_like(dvacc)
            dsacc[...] = jnp.zeros_like(dsacc)
            for hh in range(2):
                bias[hh] = _band_bias(g_ref[pl.ds(2 * hp + hh, 1), :])

        dq_sum = None
        for q0 in range(0, STEP_ROWS, Q_TILE):
            q_start = i * STEP_ROWS + q0
            win = pl.ds(pl.multiple_of(q_start, Q_TILE), WINDOW)
            dq = None
            for hh in range(2):
                mask = _head_mask(hh)
                qm = jnp.where(mask, q_ref[q0:q0 + Q_TILE, :] * (CHUNK ** -0.5), 0.0).astype(BF16)
                dom = jnp.where(mask, d_ref[q0:q0 + Q_TILE, :], 0.0).astype(BF16)
                slot = 2 * (q0 // Q_TILE) + hh
                s_scr[slot] = lax.dot_general(qm, kpad[win, :], NT_DIMS, preferred_element_type=F32)
                dp_scr[slot] = lax.dot_general(dom, vpad[win, :], NT_DIMS, preferred_element_type=F32)
                for r0 in range(0, Q_TILE, SOFTMAX_ROWS):
                    rows = slice(r0, r0 + SOFTMAX_ROWS)
                    p = _probs_block(s_scr.at[slot], bias, hh, rows, q_start)
                    dp = dp_scr[slot, rows, :]
                    ds = p * (dp - jnp.sum(p * dp, axis=-1, keepdims=True))
                    dsacc[hh, rows, :] += ds
                    ds_scr[slot, rows, :] = ds.astype(BF16)
                    p_scr[slot, rows, :] = p.astype(BF16)
                ds16 = ds_scr[slot]
                dqh = jnp.dot(ds16, kpad[win, :], preferred_element_type=F32) * (CHUNK ** -0.5)
                dq = jnp.where(mask, dqh, 0.0) if dq is None else jnp.where(mask, dqh, dq)
                dkacc[win, :] += lax.dot_general(ds16, qm, TN_DIMS, preferred_element_type=F32)
                dvacc[win, :] += lax.dot_general(p_scr[slot], dom, TN_DIMS, preferred_element_type=F32)
            dq_ref[q0:q0 + Q_TILE, :] = dq.astype(BF16)
            dq_sum = _colsum(dq) if dq_sum is None else dq_sum + _colsum(dq)

        @pl.when(i == 0)
        def _():
            sq_ref[...] = dq_sum

        @pl.when(i > 0)
        def _():
            sq_ref[...] += dq_sum

        @pl.when(i == n_i - 1)
        def _():
            dk = dkacc[PAD_ROWS:PAD_ROWS + t, :]
            dv = dvacc[PAD_ROWS:PAD_ROWS + t, :]
            dk_ref[...] = dk.astype(BF16)
            dv_ref[...] = dv.astype(BF16)
            sk_ref[...] = _colsum(dk)
            sv_ref[...] = _colsum(dv)
            rr = lax.broadcasted_iota(jnp.int32, (Q_TILE, Q_TILE), 0)
            cc = lax.broadcasted_iota(jnp.int32, (Q_TILE, Q_TILE), 1)
            rev = jnp.where(rr + cc == Q_TILE - 1, 1.0, 0.0).astype(BF16)
            for hh in range(2):
                acc = None
                for part in _split3(dsacc[hh]):
                    pr = jnp.dot(rev, part, preferred_element_type=F32)
                    acc = pr if acc is None else acc + pr
                wide = jnp.concatenate([acc, jnp.zeros((Q_TILE, TOEP - WINDOW), F32)], axis=1)
                dg_ref[pl.ds(2 * hp + hh, 1), :] = _colsum(pltpu.roll(wide, 0, 1, stride=1, stride_axis=0))

    col = lambda off: pl.BlockSpec((t, 128), lambda h, i: (0, off + h))
    tile = lambda: pl.BlockSpec((STEP_ROWS, 128), lambda h, i: (i, h))
    sums = lambda: pl.BlockSpec((1, 128), lambda h, i: (0, h))
    res = _call(
        body, grid=(4, n_i),
        in_specs=[tile(), col(4), col(8), tile(), pl.BlockSpec((N_HEADS, TOEP), lambda h, i: (0, 0))],
        out_specs=[tile(), col(0), col(0), sums(), sums(), sums(), pl.BlockSpec((N_HEADS, TOEP), lambda h, i: (0, 0))],
        out_shape=[_sds((t, 512), BF16)] * 3 + [_sds((1, 512), F32)] * 3 + [_sds((N_HEADS, TOEP), F32)],
        scratch_shapes=[pltpu.VMEM((PAD_ROWS + t, 128), BF16), pltpu.VMEM((PAD_ROWS + t, 128), BF16),
                        pltpu.VMEM((PAD_ROWS + t, 128), F32), pltpu.VMEM((PAD_ROWS + t, 128), F32),
                        pltpu.VMEM((2, Q_TILE, WINDOW), F32), pltpu.VMEM((2, Q_TILE, WINDOW), F32),
                        pltpu.VMEM((SCORE_SLOTS, Q_TILE, WINDOW), F32), pltpu.VMEM((SCORE_SLOTS, Q_TILE, WINDOW), F32),
                        pltpu.VMEM((SCORE_SLOTS, Q_TILE, WINDOW), BF16), pltpu.VMEM((SCORE_SLOTS, Q_TILE, WINDOW), BF16)],
        sem=("arbitrary", "arbitrary"), name=name, args=(z, z, z, datt, gen), carried=carried)
    return res[:7] if carried is None else (res[:7], res[7:])


def _adamw_math(w, g, m, v):
    m = ADAM_B1 * m + (1.0 - ADAM_B1) * g
    v = ADAM_B2 * v + (1.0 - ADAM_B2) * (g * g)
    m_hat = m / (1.0 - ADAM_B1 ** ADAM_STEP)
    v_hat = v / (1.0 - ADAM_B2 ** ADAM_STEP)
    delta = -ADAM_LR * (m_hat / (jnp.sqrt(v_hat) + ADAM_EPS) + ADAM_WD * w)
    return delta, m, v


def _adamw_many(items, name):
    n = len(items)

    def body(*refs):
        ins, outs = refs[:4 * n], refs[4 * n:]
        for k in range(n):
            w, g, m, v = (r[...] for r in ins[4 * k:4 * k + 4])
            outs[3 * k][...], outs[3 * k + 1][...], outs[3 * k + 2][...] = _adamw_math(w, g, m, v)

    flat = [a for item in items for a in item]
    res = pl.pallas_call(body, out_shape=[_sds(item[0].shape, F32) for item in items for _ in range(3)],
                         name=name)(*flat)
    return [tuple(res[3 * k:3 * k + 3]) for k in range(n)]


def _adamw(w, g, m, v, name, after):
    r, c = w.shape
    tm = next(cand for cand in (512, 352, 256, 128, 64, 32, 16, 8) if r % cand == 0)
    return _rowcall(lambda wv, gv, mv, vv, _: (gv,) + _adamw_math(wv, gv, mv, vv),
                    [(w, c, 0), (g, c, 0), (m, c, 0), (v, c, 0)], [after], [(r, c, F32, c)] * 4, [], name=name, tm=tm)


def _ada_fwd(c_all, w_shard, b_shard, name):
    n = w_shard.shape[1]
    tn = 512

    def body(c_ref, w_ref, b_ref, o_ref, a_ref):
        cv = c_ref[...]
        act = cv * _sigmoid(cv)
        a_ref[...] = act
        o_ref[...] = jnp.dot(act.astype(BF16), w_ref[...].astype(BF16), preferred_element_type=F32) + b_ref[...]

    return pl.pallas_call(
        body, grid=(n // tn,),
        in_specs=[pl.BlockSpec((8, D), lambda j: (0, 0)), pl.BlockSpec((D, tn), lambda j: (0, j)),
                  pl.BlockSpec((1, tn), lambda j: (0, j))],
        out_specs=[pl.BlockSpec((8, tn), lambda j: (0, j)), pl.BlockSpec((8, D), lambda j: (0, 0))],
        out_shape=[_sds((8, n), F32), _sds((8, D), F32)],
        compiler_params=_params(("arbitrary",)), name=name,
    )(c_all, w_shard, b_shard)


def _ada_bwd_adamw(act_t, dmod_shard, w, m, v, name):
    r, c = w.shape
    tm = 2 * ROW_TILE

    def body(a_ref, d_ref, w_ref, m_ref, v_ref, g_ref, dl_ref, nm_ref, nv_ref):
        g = jnp.dot(a_ref[...], d_ref[...], precision=lax.Precision.HIGHEST, preferred_element_type=F32)
        g_ref[...] = g
        dl_ref[...], nm_ref[...], nv_ref[...] = _adamw_math(w_ref[...], g, m_ref[...], v_ref[...])

    blk = pl.BlockSpec((tm, c), lambda i: (i, 0))
    return pl.pallas_call(
        body, grid=(r // tm,),
        in_specs=[pl.BlockSpec((tm, 8), lambda i: (i, 0)), pl.BlockSpec((8, c), lambda i: (0, 0)), blk, blk, blk],
        out_specs=[blk] * 4, out_shape=[_sds((r, c), F32)] * 4,
        compiler_params=_params(("arbitrary",)), name=name,
    )(act_t, dmod_shard, w, m, v)


def _place():
    return lax.axis_index("x"), lax.axis_index("y"), lax.axis_index("c")


def _flip(v, bit):
    return 1 - v if bit else v


VMEM_SPEC = pl.BlockSpec(memory_space=pltpu.VMEM)


def _allgather8(v, name):
    r, c = v.shape

    def body(v_ref, g_ref, tot_ref, send_sems, recv_sems, local_sem):
        x, y, cc = _place()
        sibling = (x, y, 1 - cc)
        chips = [(_flip(x, k & 2), _flip(y, k & 1)) for k in (1, 2, 3)]

        def block(px, py, pc):
            return g_ref.at[4 * px + 2 * py + pc]

        def copy(k, place, to, src=None):
            slot = block(*place)
            return pltpu.make_async_remote_copy(src_ref=slot if src is None else src, dst_ref=slot,
                                                send_sem=send_sems.at[k], recv_sem=recv_sems.at[k],
                                                device_id=to, device_id_type=MESH)

        mine = pltpu.make_async_copy(v_ref, block(x, y, cc), local_sem)
        mine.start()
        first = [copy(0, (x, y, cc), sibling, src=v_ref)]
        first += [copy(1 + j, (x, y, cc), (px, py, cc), src=v_ref) for j, (px, py) in enumerate(chips)]
        for cp in first:
            cp.start()
        passed = [copy(4 + j, (px, py, cc), sibling) for j, (px, py) in enumerate(chips)]
        for j, (px, py) in enumerate(chips):
            copy(1 + j, (px, py, cc), (x, y, cc)).wait_recv()
            passed[j].start()
        copy(0, sibling, (x, y, cc)).wait_recv()
        for j, (px, py) in enumerate(chips):
            copy(4 + j, (px, py, 1 - cc), (x, y, cc)).wait_recv()
        for cp in first + passed:
            cp.wait_send()
        mine.wait()
        tot = g_ref[0]
        for d in range(1, 8):
            tot = tot + g_ref[d]
        tot_ref[...] = tot

    return pl.pallas_call(
        body, in_specs=[VMEM_SPEC], out_specs=[VMEM_SPEC, VMEM_SPEC],
        out_shape=[_sds((8, r, c), F32), _sds((r, c), F32)],
        scratch_shapes=[pltpu.SemaphoreType.DMA((7,)), pltpu.SemaphoreType.DMA((7,)), pltpu.SemaphoreType.DMA],
        compiler_params=pltpu.CompilerParams(vmem_limit_bytes=VMEM_LIMIT), name=name,
    )(v)


def _slot(px, py, swapped):
    return 2 * py + px if swapped else 2 * px + py


def _gather_shards(arrs, swapped, name):
    n = len(arrs)

    def body(*refs):
        ins, outs = refs[:n], refs[n:2 * n]
        send1, recv1, send2, recv2, local_sems = refs[2 * n:]
        x, y, c = _place()
        sibling = (x, y, 1 - c)
        chips = [(_flip(x, k & 2), _flip(y, k & 1)) for k in (1, 2, 3)]
        local_copies, sends = [], []
        for a in range(n):
            h = outs[a].shape[1] // 2
            mine = pl.ds(pl.multiple_of(c * h, 8), h)
            own = _slot(x, y, swapped[a])
            lc = pltpu.make_async_copy(ins[a], outs[a].at[own], local_sems.at[a])
            lc.start()
            local_copies.append(lc)
            for j, (px, py) in enumerate(chips):
                cp = pltpu.make_async_remote_copy(
                    src_ref=ins[a].at[mine], dst_ref=outs[a].at[own, mine], send_sem=send1.at[3 * a + j],
                    recv_sem=recv1.at[3 * a + j], device_id=(px, py, c), device_id_type=MESH)
                cp.start()
                sends.append(cp)
        for a in range(n):
            h = outs[a].shape[1] // 2
            mine = pl.ds(pl.multiple_of(c * h, 8), h)
            for j, (px, py) in enumerate(chips):
                piece = outs[a].at[_slot(px, py, swapped[a]), mine]
                pltpu.make_async_remote_copy(
                    src_ref=piece, dst_ref=piece, send_sem=send1.at[3 * a + j], recv_sem=recv1.at[3 * a + j],
                    device_id=(px, py, c), device_id_type=MESH).wait_recv()
                fwd = pltpu.make_async_remote_copy(
                    src_ref=piece, dst_ref=piece, send_sem=send2.at[3 * a + j], recv_sem=recv2.at[3 * a + j],
                    device_id=sibling, device_id_type=MESH)
                fwd.start()
                sends.append(fwd)
        for a in range(n):
            h = outs[a].shape[1] // 2
            other = pl.ds(pl.multiple_of((1 - c) * h, 8), h)
            for j, (px, py) in enumerate(chips):
                piece = outs[a].at[_slot(px, py, swapped[a]), other]
                pltpu.make_async_remote_copy(
                    src_ref=piece, dst_ref=piece, send_sem=send2.at[3 * a + j], recv_sem=recv2.at[3 * a + j],
                    device_id=sibling, device_id_type=MESH).wait_recv()
        for cp in sends:
            cp.wait_send()
        for lc in local_copies:
            lc.wait()

    dma = lambda k: pltpu.SemaphoreType.DMA((k,))
    return pl.pallas_call(
        body, in_specs=[ANY] * n, out_specs=[ANY] * n,
        out_shape=[_sds((4,) + a.shape, a.dtype) for a in arrs],
        scratch_shapes=[dma(3 * n), dma(3 * n), dma(3 * n), dma(3 * n), dma(n)], name=name,
    )(*arrs)


def _carry_pair_exchange(grads):
    n = len(grads)

    def copies(ins, outs, send_sems, recv_sems):
        x, y, c = _place()
        cps = []
        for a in range(n):
            h = ins[a].shape[1] // 2
            theirs = pl.ds(pl.multiple_of((1 - c) * h, 8), h)
            cps.append(pltpu.make_async_remote_copy(
                src_ref=ins[a].at[:, theirs, :], dst_ref=outs[a], send_sem=send_sems.at[a], recv_sem=recv_sems.at[a],
                device_id=(x, y, 1 - c), device_id_type=MESH))
        return cps

    def start(*refs):
        for cp in copies(*refs):
            cp.start()

    def finish(*refs):
        for cp in copies(*refs):
            cp.wait()

    return _Carried(grads, [_sds((4, g.shape[1] // 2, g.shape[2]), F32) for g in grads], {}, n, start, finish)


def _pair_sum(grad, recv, core, name):
    _, r, c = grad.shape
    h = r // 2

    def body(core_ref, g_ref, r_ref, o_ref):
        o_ref[...] = (g_ref[...] + r_ref[...]).astype(BF16)

    return pl.pallas_call(
        body,
        grid_spec=pltpu.PrefetchScalarGridSpec(
            num_scalar_prefetch=1, grid=(4,),
            in_specs=[pl.BlockSpec((None, h, c), lambda s, core_ref: (s, core_ref[0], 0)),
                      pl.BlockSpec((None, h, c), lambda s, core_ref: (s, 0, 0))],
            out_specs=pl.BlockSpec((None, h, c), lambda s, core_ref: (s, 0, 0))),
        out_shape=_sds((4, h, c), BF16), compiler_params=_params(("arbitrary",)), name=name,
    )(core, grad, recv)


def _carry_chip_exchange(parts, swapped):
    n = len(parts)

    def copies(ins, outs, send_sems, recv_sems):
        x, y, c = _place()
        chips = [(_flip(x, k & 2), _flip(y, k & 1)) for k in (1, 2, 3)]
        cps = []
        for a in range(n):
            for j, (px, py) in enumerate(chips):
                cps.append(pltpu.make_async_remote_copy(
                    src_ref=ins[a].at[_slot(px, py, swapped[a])], dst_ref=outs[a].at[j],
                    send_sem=send_sems.at[3 * a + j], recv_sem=recv_sems.at[3 * a + j],
                    device_id=(px, py, c), device_id_type=MESH))
        return cps

    def start(*refs):
        for cp in copies(*refs):
            cp.start()

    def finish(*refs):
        for cp in copies(*refs):
            cp.wait()

    return _Carried(parts, [_sds((3,) + p.shape[1:], BF16) for p in parts], {}, 3 * n, start, finish)


def _chip_sum(part, recv, slot_core, name):
    _, h, c = part.shape

    def body(sc_ref, p_ref, r_ref, o_ref):
        acc = p_ref[...].astype(F32)
        for j in range(3):
            acc = acc + r_ref[j].astype(F32)
        o_ref[...] = acc

    return pl.pallas_call(
        body,
        grid_spec=pltpu.PrefetchScalarGridSpec(
            num_scalar_prefetch=1, grid=(1,),
            in_specs=[pl.BlockSpec((None, h, c), lambda q, sc_ref: (sc_ref[0], 0, 0)),
                      pl.BlockSpec((3, h, c), lambda q, sc_ref: (0, 0, 0))],
            out_specs=pl.BlockSpec((h, c), lambda q, sc_ref: (sc_ref[1], 0))),
        out_shape=_sds((2 * h, c), F32), compiler_params=_params(("arbitrary",)), name=name,
    )(slot_core, part, recv)


def _carry_pair_share(shards):
    n = len(shards)

    def copies(outs, send_sems, recv_sems, mine):
        x, y, c = _place()
        cps = []
        for a in range(n):
            h = outs[a].shape[0] // 2
            half = outs[a].at[pl.ds(pl.multiple_of((c if mine else 1 - c) * h, 8), h)]
            cps.append(pltpu.make_async_remote_copy(
                src_ref=half, dst_ref=half, send_sem=send_sems.at[a], recv_sem=recv_sems.at[a],
                device_id=(x, y, 1 - c), device_id_type=MESH))
        return cps

    def start(ins, outs, send_sems, recv_sems):
        for cp in copies(outs, send_sems, recv_sems, True):
            cp.start()

    def finish(ins, outs, send_sems, recv_sems):
        for cp in copies(outs, send_sems, recv_sems, False):
            cp.wait_recv()
        for cp in copies(outs, send_sems, recv_sems, True):
            cp.wait_send()

    return _Carried(shards, [_sds(s.shape, F32) for s in shards], {a: a for a in range(n)}, n, start, finish)


def _carry_gather_ici(bufs, swapped):
    n = len(bufs)

    def copies(outs, send_sems, recv_sems, sending):
        x, y, c = _place()
        cps = []
        for a in range(n):
            h = outs[a].shape[1] // 2
            mine = pl.ds(pl.multiple_of(c * h, 8), h)
            for j, k in enumerate((1, 2, 3)):
                px, py = _flip(x, k & 2), _flip(y, k & 1)
                slot = _slot(x, y, swapped[a]) if sending else _slot(px, py, swapped[a])
                piece = outs[a].at[slot, mine]
                cps.append(pltpu.make_async_remote_copy(
                    src_ref=piece, dst_ref=piece, send_sem=send_sems.at[3 * a + j], recv_sem=recv_sems.at[3 * a + j],
                    device_id=(px, py, c), device_id_type=MESH))
        return cps

    def start(ins, outs, send_sems, recv_sems):
        for cp in copies(outs, send_sems, recv_sems, True):
            cp.start()

    def finish(ins, outs, send_sems, recv_sems):
        for cp in copies(outs, send_sems, recv_sems, False):
            cp.wait_recv()
        for cp in copies(outs, send_sems, recv_sems, True):
            cp.wait_send()

    return _Carried(bufs, [_sds(b.shape, b.dtype) for b in bufs], {a: a for a in range(n)}, 3 * n, start, finish)


HBM_SPEC = pl.BlockSpec(memory_space=pltpu.HBM)
SEM_SPEC = pl.BlockSpec(memory_space=pltpu.SEMAPHORE)
SIDE_EFFECT = pltpu.SideEffectType.DATAFLOW_SIDE_EFFECTING


def _ici_pieces(buf, send_sems, recv_sems, swapped, sending):
    x, y, c = _place()
    h = buf.shape[1] // 2
    mine = pl.ds(pl.multiple_of(c * h, 8), h)
    cps = []
    for j, k in enumerate((1, 2, 3)):
        px, py = _flip(x, k & 2), _flip(y, k & 1)
        piece = buf.at[_slot(x, y, swapped) if sending else _slot(px, py, swapped), mine]
        cps.append(pltpu.make_async_remote_copy(src_ref=piece, dst_ref=piece, send_sem=send_sems.at[j],
                                                recv_sem=recv_sems.at[j], device_id=(px, py, c), device_id_type=MESH))
    return cps


def _gather_ici_start(buf, after, swapped, name):
    def body(buf_ref, after_ref, send_sems, recv_sems, thru, token):
        for cp in _ici_pieces(thru, send_sems, recv_sems, swapped, True):
            cp.start()
        token[...] = jnp.zeros_like(token)

    return pl.pallas_call(
        body, name=name,
        out_shape=(pltpu.SemaphoreType.DMA((3,)), pltpu.SemaphoreType.DMA((3,)), pltpu.HBM(buf.shape, buf.dtype),
                   jax.ShapeDtypeStruct((8, 128), F32)),
        in_specs=(HBM_SPEC, ANY), out_specs=(SEM_SPEC, SEM_SPEC, HBM_SPEC, VMEM_SPEC), input_output_aliases={0: 2},
        compiler_params=pltpu.CompilerParams(has_side_effects=SIDE_EFFECT),
    )(pltpu.with_memory_space_constraint(buf, pltpu.HBM), after)


def _gather_ici_wait(send_sems, recv_sems, thru, after, swapped, name):
    def body(thru_ref, send_sems, recv_sems, after_ref, out_ref):
        for cp in _ici_pieces(out_ref, send_sems, recv_sems, swapped, True):
            cp.wait_send()
        for cp in _ici_pieces(out_ref, send_sems, recv_sems, swapped, False):
            cp.wait_recv()

    return pl.pallas_call(
        body, name=name, out_shape=pltpu.HBM(thru.shape, thru.dtype),
        in_specs=(HBM_SPEC, SEM_SPEC, SEM_SPEC, ANY), out_specs=HBM_SPEC, input_output_aliases={0: 0},
        compiler_params=pltpu.CompilerParams(has_side_effects=SIDE_EFFECT),
    )(thru, send_sems, recv_sems, after)


def _all8_copies(buf, send_sems, recv_sems, sending):
    x, y, c = _place()
    cps = []
    for k in range(1, 8):
        px, py, pc = _flip(x, k & 4), _flip(y, k & 2), _flip(c, k & 1)
        slot = buf.at[4 * x + 2 * y + c] if sending else buf.at[4 * px + 2 * py + pc]
        cps.append(pltpu.make_async_remote_copy(src_ref=slot, dst_ref=slot, send_sem=send_sems.at[k - 1],
                                                recv_sem=recv_sems.at[k - 1], device_id=(px, py, pc), device_id_type=MESH))
    return cps


def _all8_start(buf, name):
    def body(buf_ref, send_sems, recv_sems, thru, token):
        for cp in _all8_copies(thru, send_sems, recv_sems, True):
            cp.start()
        token[...] = jnp.zeros_like(token)

    return pl.pallas_call(
        body, name=name,
        out_shape=(pltpu.SemaphoreType.DMA((7,)), pltpu.SemaphoreType.DMA((7,)), pltpu.HBM(buf.shape, buf.dtype),
                   jax.ShapeDtypeStruct((8, 128), F32)),
        in_specs=(HBM_SPEC,), out_specs=(SEM_SPEC, SEM_SPEC, HBM_SPEC, VMEM_SPEC), input_output_aliases={0: 2},
        compiler_params=pltpu.CompilerParams(has_side_effects=SIDE_EFFECT),
    )(pltpu.with_memory_space_constraint(buf, pltpu.HBM))


def _all8_wait(send_sems, recv_sems, thru, after, name):
    def body(thru_ref, send_sems, recv_sems, after_ref, out_ref):
        for cp in _all8_copies(out_ref, send_sems, recv_sems, True):
            cp.wait_send()
        for cp in _all8_copies(out_ref, send_sems, recv_sems, False):
            cp.wait_recv()

    return pl.pallas_call(
        body, name=name, out_shape=pltpu.HBM(thru.shape, thru.dtype),
        in_specs=(HBM_SPEC, SEM_SPEC, SEM_SPEC, ANY), out_specs=HBM_SPEC, input_output_aliases={0: 0},
        compiler_params=pltpu.CompilerParams(has_side_effects=SIDE_EFFECT),
    )(thru, send_sems, recv_sems, after)


def _sum8(g, name):
    def body(g_ref, o_ref):
        tot = g_ref[0]
        for d in range(1, 8):
            tot = tot + g_ref[d]
        o_ref[...] = tot

    return pl.pallas_call(body, out_shape=_sds(g.shape[1:], F32), name=name)(g)


def _carry_gather_forward(bufs, swapped):
    n = len(bufs)

    def copies(outs, send_sems, recv_sems, sending):
        x, y, c = _place()
        cps = []
        for a in range(n):
            h = outs[a].shape[1] // 2
            rows = pl.ds(pl.multiple_of((c if sending else 1 - c) * h, 8), h)
            for j, k in enumerate((1, 2, 3)):
                piece = outs[a].at[_slot(_flip(x, k & 2), _flip(y, k & 1), swapped[a]), rows]
                cps.append(pltpu.make_async_remote_copy(
                    src_ref=piece, dst_ref=piece, send_sem=send_sems.at[3 * a + j], recv_sem=recv_sems.at[3 * a + j],
                    device_id=(x, y, 1 - c), device_id_type=MESH))
        return cps

    def start(ins, outs, send_sems, recv_sems):
        for cp in copies(outs, send_sems, recv_sems, True):
            cp.start()

    def finish(ins, outs, send_sems, recv_sems):
        for cp in copies(outs, send_sems, recv_sems, False):
            cp.wait_recv()
        for cp in copies(outs, send_sems, recv_sems, True):
            cp.wait_send()

    return _Carried(bufs, [_sds(b.shape, b.dtype) for b in bufs], {a: a for a in range(n)}, 3 * n, start, finish)


def _pack(arrs, rows_multiple=8):
    parts, offs, row = [], [], 0
    for a in arrs:
        flat = a.reshape(-1)
        nrow = -(-flat.shape[0] // D)
        parts.append(jnp.pad(flat, (0, nrow * D - flat.shape[0])))
        offs.append(row)
        row += nrow
    total = -(-row // rows_multiple) * rows_multiple
    if total > row:
        parts.append(jnp.zeros(((total - row) * D,), F32))
    return jnp.concatenate(parts).reshape(total, D), offs


def _unpack(packed, offs, shapes):
    out = []
    for off, shp in zip(offs, shapes):
        size = int(np.prod(shp))
        nrow = -(-size // D)
        out.append(packed[off:off + nrow].reshape(-1)[:size].reshape(shp))
    return out


def _to_bf16_slot(w, slot, name, after=None):
    r, c = w.shape
    tm = next(cand for cand in (512, 352, 256, 128, 64, 32, 16) if r % cand == 0)

    def body(slot_ref, w_ref, *rest):
        rest[-1][...] = w_ref[...].astype(BF16)

    in_specs = [pl.BlockSpec((tm, c), lambda i, slot_ref: (i, 0))]
    if after is not None:
        in_specs.append(pl.BlockSpec((8, 128), lambda i, slot_ref: (0, 0)))
    return pl.pallas_call(
        body,
        grid_spec=pltpu.PrefetchScalarGridSpec(
            num_scalar_prefetch=1, grid=(r // tm,), in_specs=in_specs,
            out_specs=pl.BlockSpec((None, tm, c), lambda i, slot_ref: (slot_ref[0], i, 0))),
        out_shape=_sds((4, r, c), BF16), compiler_params=_params(("arbitrary",)), name=name,
    )(slot, w, *([] if after is None else [after]))


def _unshard_cols(g):
    s, k, n = g.shape
    return jnp.transpose(g, (1, 0, 2)).reshape(k, s * n)


def _ff_swap(v):
    b = FF_BLOCK
    return jnp.concatenate([v[..., 0:b], v[..., 2 * b:3 * b], v[..., b:2 * b], v[..., 3 * b:4 * b]], axis=-1)


LATE = ("attn_o", "conv_o", "mix_o", "up", "down")
EARLY_GRADS = ("down", "up", "mix_o", "attn_o", "conv_o")


def _weight_views(bufs):
    return {"up": bufs["up"], "attn_o": _unshard_cols(bufs["attn_o"]), "conv_o": _unshard_cols(bufs["conv_o"]),
            "mix_o": bufs["mix_o"].reshape(D, D), "down": bufs["down"].reshape(D_FF, D)}


def _pair_sums(names, grads, recv, dist):
    return [_pair_sum(g, r, dist["core"], "pair_sum_" + n) for n, g, r in zip(names, grads, recv)]


def _reduce_halves(names, parts, from_chips, dist):
    return [_chip_sum(p, r, jnp.concatenate([dist["slots"][SWAPPED[n]], dist["core"]]), "chip_sum_" + n)
            for n, p, r in zip(names, parts, from_chips)]


FUSED_TILE = 256
WIDE_TILE = 512


def _gates(z):
    return [(z, 512, 5), (z, 512, 6), (z, 512, 7), (z, 512, 8)]


def _mix_out(a, cb, z, x, w_mix_o, g_post, gt, g_pre2, sc2, sh2, name):
    def lhs(av, cv, ga0, ga1, gb0, gb1):
        ga, gb = jnp.concatenate([ga0, ga1], axis=1), jnp.concatenate([gb0, gb1], axis=1)
        return _sigmoid(ga) * av + _sigmoid(gb) * cv

    def fn(ym, y, xv, gv, gtv, g2v, scv, shv):
        yn, _ = _rms_stats(ym)
        x1 = xv + gtv * (yn * gv)
        xn, _ = _rms_stats(x1)
        return ym, y, x1, (xn * g2v) * (1.0 + scv) + shv

    return _matmul_rows(w_mix_o, form="nn", tm=min(WIDE_TILE, x.shape[0]), tk=D, fn=fn, a_rows=[(a, D, 0), (cb, D, 0)] + _gates(z),
                        a_fn=lhs, rows=[(x, D, 0)], consts=[g_post, gt, g_pre2, sc2, sh2],
                        row_outs=[(F32, D), (BF16, D), (F32, D), (BF16, D)], acc_outs=[], name=name)


def _down_tail(act, w_down, x1, target, g, gt, name):
    def fn(yv, xv, tv, gv, gtv):
        yn, r = _rms_stats(yv)
        e = xv + gtv * (yn * gv) - tv
        dx2 = e * (1.0 / D)
        dyn = dx2 * gtv
        return (dx2, _rms_bwd(dyn * gv, yn, r), _colsum(e * e) * (0.5 / D), _colsum(dyn * yn),
                _colsum(dx2 * (yn * gv)))

    return _matmul_rows(w_down, form="nn", a=act, tm=min(WIDE_TILE, x1.shape[0]), tk=D_FF, fn=fn,
                        rows=[(x1, D, 0), (target, D, 0)], consts=[g, gt], row_outs=[(F32, D), (BF16, D)],
                        acc_outs=[(1, D)] * 3, name=name)


def _up_dx_tail(dup, w_up, x1, dx2, ym, g_pre2, sc2, g_post, gt, name):
    def fn(dh, xv, dov, ymv, g2v, scv, gv, gtv):
        xn, r = _rms_stats(xv)
        dyn = dh * (1.0 + scv)
        dx1 = dov + _rms_bwd(dyn * g2v, xn, r)
        yn, r2 = _rms_stats(ymv)
        dynm = dx1 * gtv
        return (dx1, _rms_bwd(dynm * gv, yn, r2), _colsum(dh), _colsum(dh * (xn * g2v)), _colsum(dyn * xn),
                _colsum(dynm * yn), _colsum(dx1 * (yn * gv)))

    return _matmul_rows(w_up, form="nt", a=dup, tm=min(WIDE_TILE, x1.shape[0]), tk=2 * D_FF, fn=fn,
                        rows=[(x1, D, 0), (dx2, D, 0), (ym, D, 0)], consts=[g_pre2, sc2, g_post, gt],
                        row_outs=[(F32, D), (BF16, D)], acc_outs=[(1, D)] * 5, name=name)


def _mix_dx_gates(dym, w_mix_o, a, cb, z, name):
    def fn(dy, av, cv, ga0, ga1, gb0, gb1):
        sa = _sigmoid(jnp.concatenate([ga0, ga1], axis=1))
        sb = _sigmoid(jnp.concatenate([gb0, gb1], axis=1))
        dcb = dy * sb
        dga = dy * av * (sa * (1.0 - sa))
        dgb = dy * cv * (sb * (1.0 - sb))
        return dy * sa, dcb, dga, dgb, _colsum(dcb), _colsum(dga), _colsum(dgb)

    return _matmul_rows(w_mix_o, form="nt", a=dym, tm=min(WIDE_TILE, a.shape[0]), tk=D, fn=fn,
                        rows=[(a, D, 0), (cb, D, 0)] + _gates(z), consts=[], row_outs=[(BF16, D)] * 4,
                        acc_outs=[(1, D)] * 3, name=name)


def _local_step(x, target, mod, w_in, late, small, dist=None):
    sh_m, sc_m, gt_m, sh_f, sc_f, gt_f = mod
    t = x.shape[0]
    tmm = min(1024, t)
    late_swapped = [SWAPPED[n] for n in LATE]

    h1 = _pre_norm(x, small["g_pre_mix"], sc_m, sh_m, "pre_norm_mix")
    if callable(w_in):
        w_in = w_in(h1)
    z = _matmul(h1, w_in, form="nn", out_dtype=F32, tm=min(FUSED_TILE, t), tn=D_IN, tk=D, bias=small["b_in"], name="mm_in")
    conv = (z, small["w_dw_conv"], small["b_dw_conv"], small["g_conv_ln"], small["b_conv_ln"], "conv_branch")
    if dist is None:
        att = _attention(z, small["gen"], "attention")
        u1, u3 = _conv_branch(*conv)
        bufs = dict(late)
    else:
        mid = [n for n in LATE if n != "down"]
        mid_swapped = [SWAPPED[n] for n in mid]
        att, landed = _attention(z, small["gen"], "attention",
                                 carried=_carry_gather_ici([late[n] for n in mid], mid_swapped))
        (u1, u3), gathered = _conv_branch(*conv, carried=_carry_gather_forward(landed, mid_swapped))
        bufs = dict(zip(mid, gathered))
        bufs["down"] = late["down"]
    w = _weight_views(bufs)
    w["in"] = w_in
    a = _matmul(att, w["attn_o"], form="nn", out_dtype=F32, tm=tmm, tn=512, tk=512, name="mm_attn_o")
    cb = _matmul(u3, w["conv_o"], form="nn", out_dtype=F32, tm=tmm, tn=512, tk=512, bias=small["b_conv_o"], name="mm_conv_o")
    ym, y, x1, h2 = _mix_out(a, cb, z, x, w["mix_o"], small["g_post_mix"], gt_m, small["g_pre_ffn"], sc_f, sh_f, "mix_out")
    mm_up = dict(form="nn", out_dtype=F32, tm=min(FUSED_TILE, t), tn=2 * D_FF, tk=D, name="mm_up")
    ffn_act = (small["w_dw_ffn"], small["b_dw_ffn"], "ffn_act")
    if dist is None:
        up = _matmul(h2, w["up"], **mm_up)
        act = _ffn_act(up, *ffn_act)
    else:
        up, landed = _matmul(h2, w["up"], carried=_carry_gather_ici([late["down"]], [False]), **mm_up)
        act, down = _ffn_act(up, *ffn_act, carried=_carry_gather_forward(landed, [False]))
        w["down"] = down[0].reshape(D_FF, D)

    dx2, dyf, loss_cols, d_g_post_ffn, d_gt_f = _down_tail(act, w["down"], x1, target, small["g_post_ffn"], gt_f, "down_tail")
    dact = _matmul(dyf, w["down"], form="nt", out_dtype=F32, tm=tmm, tn=D_FF, tk=D, name="mm_down_dx")
    g_down = _matmul(act, dyf, form="tn", out_dtype=F32, tm=FF_BLOCK, tn=512, tk=t, name="mm_down_dw")
    dup, d_w_dw_ffn, d_b_dw_ffn = _ffn_act_bwd(dact, up, small["w_dw_ffn"], small["b_dw_ffn"], "ffn_act_bwd")
    dx1, dym, d_sh_f, d_sc_f, d_g_pre_ffn, d_g_post_mix, d_gt_m = _up_dx_tail(
        dup, w["up"], x1, dx2, ym, small["g_pre_ffn"], sc_f, small["g_post_mix"], gt_m, "up_dx_tail")
    g_up = _matmul(h2, dup, form="tn", out_dtype=F32, tm=512, tn=FF_BLOCK, tk=t, out_sharded=True, name="mm_up_dw")
    da, dcb, dgate_a, dgate_b, d_b_conv_o, sga, sgb = _mix_dx_gates(dym, w["mix_o"], a, cb, z, "mix_dx_gates")
    g_mix_o = _matmul(y, dym, form="tn", out_dtype=F32, tm=D, tn=512, tk=t, name="mm_mix_o_dw")
    datt = _matmul(da, w["attn_o"], form="nt", out_dtype=F32, tm=tmm, tn=512, tk=D, name="mm_attn_o_dx")
    g_attn_o = _matmul(att, da, form="tn", out_dtype=F32, tm=512, tn=256, tk=t, out_sharded=True, name="mm_attn_o_dw")
    du3 = _matmul(dcb, w["conv_o"], form="nt", out_dtype=F32, tm=tmm, tn=512, tk=D, name="mm_conv_o_dx")
    g_conv_o = _matmul(u3, dcb, form="tn", out_dtype=F32, tm=512, tn=256, tk=t, out_sharded=True, name="mm_conv_o_dw")
    big = {"attn_o": g_attn_o, "conv_o": g_conv_o, "mix_o": g_mix_o.reshape(4, 256, D),
           "up": g_up, "down": g_down.reshape(4, D_FF // 4, D)}
    conv_bwd = (du3, u1, z, small["w_dw_conv"], small["g_conv_ln"], small["b_conv_ln"], "conv_branch_bwd")
    in_dw = dict(form="tn", out_dtype=F32, tm=512, tn=IN_SHARD, tk=t, out_sharded=True, name="mm_in_dw")
    in_dx = dict(form="nt", out_dtype=F32, tm=min(WIDE_TILE, t), tn=D, tk=D_IN, name="mm_in_dx")
    if dist is None:
        dglu, d_w_dw_conv, d_b_dw_conv, d_g_conv_ln, d_b_conv_ln, sglu = _conv_branch_bwd(*conv_bwd)
        dq, dk, dv, sq, sk, sv, dgen = _attention_bwd(z, datt, small["gen"], "attention_bwd")
        dz = jnp.concatenate([dq, dk, dv, dglu, dgate_a, dgate_b], axis=1)
        big["in"] = _matmul(h1, dz, **in_dw)
        dh1 = _matmul(dz, w_in, **in_dx)
    else:
        early = [big[n] for n in EARLY_GRADS]
        (dglu, d_w_dw_conv, d_b_dw_conv, d_g_conv_ln, d_b_conv_ln, sglu), recv = _conv_branch_bwd(
            *conv_bwd, carried=_carry_pair_exchange(early))
        parts = _pair_sums(EARLY_GRADS, early, recv, dist)
        (dq, dk, dv, sq, sk, sv, dgen), from_chips = _attention_bwd(
            z, datt, small["gen"], "attention_bwd",
            carried=_carry_chip_exchange(parts, [SWAPPED[n] for n in EARLY_GRADS]))
        halves = _reduce_halves(EARLY_GRADS, parts, from_chips, dist)
        dz = jnp.concatenate([dq, dk, dv, dglu, dgate_a, dgate_b], axis=1)
        g_in, shards = _matmul(h1, dz, carried=_carry_pair_share(halves), **in_dw)
        big = dict(zip(EARLY_GRADS, shards))
        recv_in = _run_carried(_carry_pair_exchange([g_in]), "pair_exchange_in")
        part_in = _pair_sums(("in",), [g_in], recv_in, dist)
        dh1, from_chips_in = _matmul(dz, w_in, carried=_carry_chip_exchange(part_in, [False]), **in_dx)
        half_in = _reduce_halves(("in",), part_in, from_chips_in, dist)
        big["in"] = _run_carried(_carry_pair_share(half_in), "pair_share_in")[0]
    d_b_in = jnp.concatenate([sq, sk, sv, sglu, sga, sgb], axis=1)
    grad_x, d_sh_m, d_sc_m, d_g_pre_mix = _pre_norm_bwd(dh1, x, dx1, small["g_pre_mix"], sc_m, "pre_norm_mix_bwd")

    dmod = [d_sh_m, d_sc_m, d_gt_m, d_sh_f, d_sc_f, d_gt_f]
    sm = {"g_pre_mix": d_g_pre_mix, "g_post_mix": d_g_post_mix, "b_in": d_b_in, "gen": dgen,
          "w_dw_conv": d_w_dw_conv, "b_dw_conv": d_b_dw_conv, "g_conv_ln": d_g_conv_ln, "b_conv_ln": d_b_conv_ln,
          "b_conv_o": d_b_conv_o, "g_pre_ffn": d_g_pre_ffn, "g_post_ffn": d_g_post_ffn,
          "w_dw_ffn": d_w_dw_ffn, "b_dw_ffn": d_b_dw_ffn}
    return loss_cols, grad_x, dmod, big, sm


BIG = ("in", "attn_o", "conv_o", "mix_o", "up", "down")
SWAPPED = {"in": False, "attn_o": False, "conv_o": False, "mix_o": False, "up": True, "down": False}
SMALL_ORDER = ("b_ada", "g_pre_mix", "g_post_mix", "b_in", "rel_bias", "b_dw_conv", "g_conv_ln", "b_conv_ln",
               "b_conv_o", "g_pre_ffn", "g_post_ffn", "b_dw_ffn", "w_dw_conv", "w_dw_ffn")


def kernel(x, c, w_ada, b_ada, g_pre_mix, g_post_mix, w_in, b_in, rel_bias, w_attn_o, w_dw_conv, b_dw_conv, g_conv_ln, b_conv_ln, w_conv_o, b_conv_o, w_mix_o, g_pre_ffn, g_post_ffn, w_up, w_dw_ffn, b_dw_ffn, w_down, loss_target, m_w_ada, m_b_ada, m_g_pre_mix, m_g_post_mix, m_w_in, m_b_in, m_rel_bias, m_w_attn_o, m_w_dw_conv, m_b_dw_conv, m_g_conv_ln, m_b_conv_ln, m_w_conv_o, m_b_conv_o, m_w_mix_o, m_g_pre_ffn, m_g_post_ffn, m_w_up, m_w_dw_ffn, m_b_dw_ffn, m_w_down, v_w_ada, v_b_ada, v_g_pre_mix, v_g_post_mix, v_w_in, v_b_in, v_rel_bias, v_w_attn_o, v_w_dw_conv, v_b_dw_conv, v_g_conv_ln, v_b_conv_ln, v_w_conv_o, v_b_conv_o, v_w_mix_o, v_g_pre_ffn, v_g_post_ffn, v_w_up, v_w_dw_ffn, v_b_dw_ffn, v_w_down):
    given = dict(locals())
    ax, ay, ac = lax.axis_index("x"), lax.axis_index("y"), lax.axis_index("c")
    shard = 2 * ax + ay
    me = 4 * ax + 2 * ay + ac
    xs, target = x[0], loss_target[0]

    slots = {sw: _slot(ax, ay, sw).astype(jnp.int32).reshape(1) for sw in (False, True)}
    own = {"in": _to_bf16_slot(w_in[0], slots[False], "cast_in")}

    c_pad = jnp.pad(c, ((0, 7), (0, 0)))
    c_g, _ = _allgather8(c_pad, "gather_c")
    c_all = c_g[:, 0, :]
    b_ada_shard = lax.dynamic_slice(b_ada, (0, shard * ADA_SHARD), (1, ADA_SHARD))
    mod_shard, c_act = _ada_fwd(c_all, w_ada[0], b_ada_shard, "ada_fwd")
    small_in = [jnp.pad(mod_shard, ((0, 8), (0, 0))),
                jnp.pad(w_dw_conv[0], ((0, 1), (0, 0))),
                jnp.pad(w_dw_ffn[0], ((0, 13), (0, 0)))]
    mod_g, wdc_g, wdf_g = _gather_shards(small_in, [False, False, True], "gather_small")
    mod_all = jnp.transpose(mod_g[:, :8, :], (1, 0, 2)).reshape(8, 6 * D)
    in_send, in_recv, in_flight, token = _gather_ici_start(own["in"], mod_g, False, "gather_w_in_start")

    def w_in_ready(after):
        landed = _gather_ici_wait(in_send, in_recv, in_flight, after, False, "gather_w_in_wait")
        return _run_carried(_carry_gather_forward([landed], [False]), "gather_forward_in")[0]

    for n in LATE:
        own[n] = _to_bf16_slot(given["w_" + n][0], slots[SWAPPED[n]], "cast_" + n, after=token)
    mod_row = lax.dynamic_slice(mod_all, (me, 0), (1, 6 * D)) + token[0:1, 0:1]
    mod = [mod_row[:, k * D:(k + 1) * D] for k in range(6)]

    core = ac.astype(jnp.int32).reshape(1)
    dist = {"core": core, "slots": slots}

    sel = jnp.asarray(_toeplitz_map())
    rel_pad = jnp.pad(rel_bias[0], ((0, 0), (0, REL_PAD - (2 * MAX_REL + 1))))
    gen = _select_call(rel_pad, sel.T.astype(BF16), "bias_rows")
    small = {"g_pre_mix": g_pre_mix, "g_post_mix": g_post_mix, "b_in": b_in, "gen": gen,
             "w_dw_conv": _unshard_cols(wdc_g[:, :CONV_K, :]), "b_dw_conv": b_dw_conv, "g_conv_ln": g_conv_ln,
             "b_conv_ln": b_conv_ln, "b_conv_o": b_conv_o, "g_pre_ffn": g_pre_ffn, "g_post_ffn": g_post_ffn,
             "w_dw_ffn": _unshard_cols(wdf_g[:, :FFN_K, :]), "b_dw_ffn": _ff_swap(b_dw_ffn)}

    loss_cols, grad_x, dmod, reduced, sm = _local_step(xs, target, mod, w_in_ready, {n: own[n] for n in LATE}, small, dist)

    d_rel = _select_call(sm["gen"], sel.astype(BF16), "bias_fold")[:, :2 * MAX_REL + 1]
    small_grads = {"g_pre_mix": sm["g_pre_mix"], "g_post_mix": sm["g_post_mix"], "b_in": sm["b_in"], "rel_bias": d_rel[None],
                   "b_dw_conv": sm["b_dw_conv"], "g_conv_ln": sm["g_conv_ln"], "b_conv_ln": sm["b_conv_ln"],
                   "b_conv_o": sm["b_conv_o"], "g_pre_ffn": sm["g_pre_ffn"], "g_post_ffn": sm["g_post_ffn"],
                   "b_dw_ffn": _ff_swap(sm["b_dw_ffn"]), "w_dw_conv": sm["w_dw_conv"], "w_dw_ffn": _ff_swap(sm["w_dw_ffn"])}
    order = [n for n in SMALL_ORDER if n != "b_ada"]
    packed, offs = _pack([jnp.concatenate(dmod, axis=1)] + [small_grads[n] for n in order] + [loss_cols])
    mine = lax.dynamic_update_slice(jnp.zeros((8,) + packed.shape, F32), packed[None], (me, 0, 0))
    sg_send, sg_recv, sg_flight, sg_token = _all8_start(mine, "gather_small_grads_start")

    out = {}
    for n in BIG:
        g, dl, nm, nv = _adamw(given["w_" + n][0], reduced[n], given["m_w_" + n][0], given["v_w_" + n][0],
                               "adamw_" + n, sg_token)
        out["grad_w_" + n], out["delta_w_" + n], out["new_m_w_" + n], out["new_v_w_" + n] = g[None], dl[None], nm[None], nv[None]
    every = _all8_wait(sg_send, sg_recv, sg_flight, out["delta_w_in"], "gather_small_grads_wait")
    total = _sum8(every, "sum_small_grads")
    loss = jnp.sum(total[offs[-1]])
    offs = offs[:-1]
    dmod_all = every[:, 0:6, :].reshape(8, 6 * D)
    full_shapes = {n: given[n].shape for n in order}
    full_shapes["w_dw_conv"], full_shapes["w_dw_ffn"] = (1, CONV_K, 512), (1, FFN_K, 2 * D_FF)
    sums = dict(zip(order, _unpack(total, offs[1:], [full_shapes[n] for n in order])))
    sums["b_ada"] = total[0:6].reshape(1, 6 * D)
    sums["w_dw_conv"] = lax.dynamic_slice(sums["w_dw_conv"], (0, 0, shard * 128), (1, CONV_K, 128))
    sums["w_dw_ffn"] = lax.dynamic_slice(sums["w_dw_ffn"], (0, 0, shard * FF_BLOCK), (1, FFN_K, FF_BLOCK))

    upd = dict(zip(SMALL_ORDER, _adamw_many(
        [(given[n], sums[n], given["m_" + n], given["v_" + n]) for n in SMALL_ORDER], "adamw_small")))

    dmod_shard = lax.dynamic_slice(dmod_all, (0, shard * ADA_SHARD), (8, ADA_SHARD))
    ada = _ada_bwd_adamw(c_act.T, dmod_shard, w_ada[0], m_w_ada[0], v_w_ada[0], "ada_bwd_adamw")

    out.update({"grad_w_ada": ada[0][None], "delta_w_ada": ada[1][None], "new_m_w_ada": ada[2][None],
                "new_v_w_ada": ada[3][None]})
    for n in SMALL_ORDER:
        out["grad_" + n], out["delta_" + n], out["new_m_" + n], out["new_v_" + n] = sums[n], *upd[n]

    weights = ["w_ada", "b_ada", "g_pre_mix", "g_post_mix", "w_in", "b_in", "rel_bias", "w_attn_o", "w_dw_conv", "b_dw_conv",
               "g_conv_ln", "b_conv_ln", "w_conv_o", "b_conv_o", "w_mix_o", "g_pre_ffn", "g_post_ffn", "w_up", "w_dw_ffn",
               "b_dw_ffn", "w_down"]
    return (loss, grad_x[None], *[out["grad_" + n] for n in weights], *[out["delta_" + n] for n in weights],
            *[out["new_m_" + n] for n in weights], *[out["new_v_" + n] for n in weights])
```
